```python
import jax, jax.numpy as jnp
from jax import lax
import numpy as np

D_MODEL = 2048
BATCH = 8
SEQ = 2048
DEPTH = 1

D_MIX = D_MODEL
HG_HEADS = 8
HG_KEY = 128
HG_VAL = 128
HG_KW = HG_HEADS * HG_KEY
HG_WIDTH = HG_HEADS * HG_VAL
ATT_HEADS = 8
ATT_HEAD_DIM = 128
ATT_WIDTH = ATT_HEADS * ATT_HEAD_DIM
DILATED_PATTERNS = ((128, 1), (512, 4), (2048, 16))
N_BACK = 128
ATT_BLOCK = N_BACK
CHUNK = 64
D_FF = 5632
CONV_WIDTH = 3
EPS = 1e-6
IN_SPLITS = (HG_KW, 2 * HG_KW, 2 * HG_KW + HG_WIDTH, 2 * HG_KW + 2 * HG_WIDTH,
             2 * HG_KW + 2 * HG_WIDTH + ATT_WIDTH, 2 * HG_KW + 2 * HG_WIDTH + 2 * ATT_WIDTH)
IN_COLS = 2 * HG_KW + 2 * HG_WIDTH + 3 * ATT_WIDTH

kernel_name = "hymba_hgrn2_dilated_alibi_convffn_adaln"


def _rms(x, w):
    xf = x.astype(jnp.float32)
    y = xf * lax.rsqrt(jnp.mean(xf * xf, axis=-1, keepdims=True) + EPS)
    return (y * w.astype(jnp.float32)).astype(x.dtype)


def _hgrn2(hq, hf, hi, hg, lb, norm_w):
    b, s, _ = hq.shape
    f32 = jnp.float32
    q = jax.nn.silu(hq.astype(f32)).reshape(b, s, HG_HEADS, HG_KEY)
    f = lb.astype(f32) + (1.0 - lb.astype(f32)) * jax.nn.sigmoid(hf.astype(f32))
    f = f.reshape(b, s, HG_HEADS, HG_KEY)
    g = jnp.log(f)
    k = 1.0 - f
    v = hi.astype(f32).reshape(b, s, HG_HEADS, HG_VAL)
    nc = s // CHUNK

    def to_chunks(t):
        return t.reshape(b, nc, CHUNK, HG_HEADS, t.shape[-1]).transpose(1, 0, 3, 2, 4)

    causal = jnp.tril(jnp.ones((CHUNK, CHUNK), dtype=bool))

    def step(state, inp):
        qc, kc, vc, gc = inp
        G = jnp.cumsum(gc, axis=2)
        o_inter = jnp.einsum('bhtk,bhkv->bhtv', qc * jnp.exp(G), state)
        diff = G[:, :, :, None, :] - G[:, :, None, :, :]
        decay = jnp.exp(jnp.where(causal[:, :, None], diff, -jnp.inf))
        A = jnp.einsum('bhtk,bhtsk,bhsk->bhts', qc, decay, kc)
        o_intra = jnp.einsum('bhts,bhsv->bhtv', A, vc)
        G_last = G[:, :, -1:, :]
        state = (jnp.exp(G_last[:, :, 0, :])[..., None] * state
                 + jnp.einsum('bhsk,bhsv->bhkv', kc * jnp.exp(G_last - G), vc))
        return state, o_inter + o_intra

    state0 = jnp.zeros((b, HG_HEADS, HG_KEY, HG_VAL), f32)
    _, o = lax.scan(step, state0, (to_chunks(q), to_chunks(k), to_chunks(v), to_chunks(g)))
    o = o.transpose(1, 0, 3, 2, 4).reshape(b, s, HG_HEADS, HG_VAL)
    o = _rms(o, norm_w).reshape(b, s, HG_WIDTH) * jax.nn.silu(hg.astype(f32))
    return o.astype(hg.dtype)


def _dilated_branch(q, k, v, slopes, dil):
    b, s, h, dh = q.shape
    L = s // dil
    nblk = -(-L // ATT_BLOCK)
    Lp = nblk * ATT_BLOCK

    def split(t):
        t = t.reshape(b, L, dil, h, dh).transpose(0, 2, 3, 1, 4)
        t = jnp.pad(t, ((0, 0), (0, 0), (0, 0), (0, Lp - L), (0, 0)))
        return t.reshape(b, dil, h, nblk, ATT_BLOCK, dh)

    def with_prev(t):
        prev = jnp.pad(t, ((0, 0), (0, 0), (0, 0), (1, 0), (0, 0), (0, 0)))[:, :, :, :-1]
        return jnp.concatenate([prev, t], axis=4)

    qb = split(q)
    kc = with_prev(split(k))
    vc = with_prev(split(v))
    scores = jnp.einsum('brhnqd,brhnkd->brhnqk', qb, kc).astype(jnp.float32) * (dh ** -0.5)

    qi = jnp.arange(ATT_BLOCK)[:, None]
    kj = jnp.arange(2 * ATT_BLOCK)[None, :]
    steps = qi + ATT_BLOCK - kj
    blk = jnp.arange(nblk)[:, None, None]
    valid = (steps >= 0) & (steps <= N_BACK) & ((blk > 0) | (kj >= ATT_BLOCK))
    alibi = -slopes[:, None, None, None] * (steps * dil).astype(jnp.float32)[None, None]
    scores = jnp.where(valid, scores + alibi, -jnp.inf)
    m = jnp.max(scores, axis=-1, keepdims=True)
    p = jnp.exp(scores - m)
    den = jnp.sum(p, axis=-1, keepdims=True)
    o = jnp.einsum('brhnqk,brhnkd->brhnqd', p, vc.astype(jnp.float32)) / den
    lse = (m + jnp.log(den))[..., 0]

    def merge(t):
        t = t.reshape(b, dil, h, Lp, *t.shape[5:])[:, :, :, :L]
        t = jnp.moveaxis(t, 3, 1)
        return t.reshape(b, s, h, *t.shape[4:])

    return merge(o), merge(lse)


def _dilated_mixture(aq, ak, av, q_norm_w, k_norm_w):
    b, s, _ = aq.shape
    q = _rms(aq.reshape(b, s, ATT_HEADS, ATT_HEAD_DIM), q_norm_w)
    k = _rms(ak.reshape(b, s, ATT_HEADS, ATT_HEAD_DIM), k_norm_w)
    v = av.reshape(b, s, ATT_HEADS, ATT_HEAD_DIM)
    slopes = jnp.exp2(-8.0 * jnp.arange(1, ATT_HEADS + 1, dtype=jnp.float32) / ATT_HEADS)
    outs, lses = [], []
    for _, dil in DILATED_PATTERNS:
        o_p, l_p = _dilated_branch(q, k, v, slopes, dil)
        outs.append(o_p)
        lses.append(l_p)
    weights = jax.nn.softmax(jnp.stack(lses), axis=0)
    o = jnp.einsum('pbsh,pbshd->bshd', weights, jnp.stack(outs))
    return o.reshape(b, s, ATT_WIDTH).astype(aq.dtype)


def _causal_dwconv(a, w, bias):
    s = a.shape[1]
    ap = jnp.pad(a, ((0, 0), (CONV_WIDTH - 1, 0), (0, 0)))
    y = bias
    for j in range(CONV_WIDTH):
        y = y + ap[:, j:j + s] * w[j]
    return y


def _layer(x, mod, norm1_w, w_in, lb, hg_norm_w, q_norm_w, k_norm_w, w_out,
           norm2_w, w_up, conv_w, conv_b, w_down):
    shift1, scale1, gate1, shift2, scale2, gate2 = jnp.split(mod, 6, axis=-1)
    h = _rms(x, norm1_w) * (1.0 + scale1[:, None]) + shift1[:, None]
    proj = h @ w_in
    hq, hf, hi, hg, aq, ak, av = jnp.split(proj, IN_SPLITS, axis=-1)
    a_out = _hgrn2(hq, hf, hi, hg, lb, hg_norm_w)
    b_out = _dilated_mixture(aq, ak, av, q_norm_w, k_norm_w)
    mix = jnp.concatenate([a_out, b_out], axis=-1) @ w_out
    x = x + gate1[:, None] * mix

    h2 = _rms(x, norm2_w) * (1.0 + scale2[:, None]) + shift2[:, None]
    u = h2 @ w_up
    a, g = jnp.split(u, 2, axis=-1)
    y = jax.nn.silu(_causal_dwconv(a, conv_w, conv_b)) * g
    return x + gate2[:, None] * (y @ w_down)


def _fwd_setup_inputs(seed: int = 0) -> dict:
    key = jax.random.key(seed)
    ks = jax.random.split(key, 16)
    f32 = jnp.float32
    nrm = lambda k, shp, sc: jax.random.normal(k, shp, f32) * sc
    D = D_MODEL
    return {
        "x": nrm(ks[0], (BATCH, SEQ, D), 1.0),
        "c": nrm(ks[1], (BATCH, D), 1.0),
        "w_ada": nrm(ks[2], (DEPTH, D, 6 * D), D ** -0.5),
        "b_ada": nrm(ks[3], (DEPTH, 6 * D), 0.02),
        "norm1_w": 1.0 + nrm(ks[4], (DEPTH, D), 0.02),
        "w_in": nrm(ks[5], (DEPTH, D, IN_COLS), D ** -0.5),
        "lb_logits": nrm(ks[6], (DEPTH + 1, HG_KW), 1.0),
        "hg_norm_w": 1.0 + nrm(ks[7], (DEPTH, HG_VAL), 0.02),
        "q_norm_w": 1.0 + nrm(ks[8], (DEPTH, ATT_HEAD_DIM), 0.02),
        "k_norm_w": 1.0 + nrm(ks[9], (DEPTH, ATT_HEAD_DIM), 0.02),
        "w_out": nrm(ks[10], (DEPTH, D_MIX, D), D_MIX ** -0.5),
        "norm2_w": 1.0 + nrm(ks[11], (DEPTH, D), 0.02),
        "w_up": nrm(ks[12], (DEPTH, D, 2 * D_FF), D ** -0.5),
        "conv_w": nrm(ks[13], (DEPTH, CONV_WIDTH, D_FF), CONV_WIDTH ** -0.5),
        "conv_b": nrm(ks[14], (DEPTH, D_FF), 0.02),
        "w_down": nrm(ks[15], (DEPTH, D_FF, D), D_FF ** -0.5),
    }


def _fwd_reference(x, c, w_ada, b_ada, norm1_w, w_in, lb_logits, hg_norm_w, q_norm_w, k_norm_w,
              w_out, norm2_w, w_up, conv_w, conv_b, w_down):
    lb_all = jnp.cumsum(jax.nn.softmax(lb_logits.astype(jnp.float32), axis=0), axis=0)
    c_act = jax.nn.silu(c)
    for l in range(DEPTH):
        mod = c_act @ w_ada[l] + b_ada[l]
        x = _layer(x, mod, norm1_w[l], w_in[l], lb_all[l], hg_norm_w[l], q_norm_w[l],
                   k_norm_w[l], w_out[l], norm2_w[l], w_up[l], conv_w[l], conv_b[l], w_down[l])
    return x


import jax as _jax
import jax.numpy as _jnp

TWIN_FORMAT = 'train_step'
FWD_PARAMS = ['x', 'c', 'w_ada', 'b_ada', 'norm1_w', 'w_in', 'lb_logits', 'hg_norm_w', 'q_norm_w', 'k_norm_w', 'w_out', 'norm2_w', 'w_up', 'conv_w', 'conv_b', 'w_down']
TWIN_WEIGHTS = ['w_ada', 'b_ada', 'norm1_w', 'w_in', 'lb_logits', 'hg_norm_w', 'q_norm_w', 'k_norm_w', 'w_out', 'norm2_w', 'w_up', 'conv_w', 'conv_b', 'w_down']
TWIN_DIFF_INPUT = 'x'
TWIN_INPUTS = ['x', 'c', 'w_ada', 'b_ada', 'norm1_w', 'w_in', 'lb_logits', 'hg_norm_w', 'q_norm_w', 'k_norm_w', 'w_out', 'norm2_w', 'w_up', 'conv_w', 'conv_b', 'w_down', 'loss_target', 'm_w_ada', 'm_b_ada', 'm_norm1_w', 'm_w_in', 'm_lb_logits', 'm_hg_norm_w', 'm_q_norm_w', 'm_k_norm_w', 'm_w_out', 'm_norm2_w', 'm_w_up', 'm_conv_w', 'm_conv_b', 'm_w_down', 'v_w_ada', 'v_b_ada', 'v_norm1_w', 'v_w_in', 'v_lb_logits', 'v_hg_norm_w', 'v_q_norm_w', 'v_k_norm_w', 'v_w_out', 'v_norm2_w', 'v_w_up', 'v_conv_w', 'v_conv_b', 'v_w_down']
TWIN_OUTPUTS = ['loss', 'grad_x', 'grad_w_ada', 'grad_b_ada', 'grad_norm1_w', 'grad_w_in', 'grad_lb_logits', 'grad_hg_norm_w', 'grad_q_norm_w', 'grad_k_norm_w', 'grad_w_out', 'grad_norm2_w', 'grad_w_up', 'grad_conv_w', 'grad_conv_b', 'grad_w_down', 'delta_w_ada', 'delta_b_ada', 'delta_norm1_w', 'delta_w_in', 'delta_lb_logits', 'delta_hg_norm_w', 'delta_q_norm_w', 'delta_k_norm_w', 'delta_w_out', 'delta_norm2_w', 'delta_w_up', 'delta_conv_w', 'delta_conv_b', 'delta_w_down', 'new_m_w_ada', 'new_m_b_ada', 'new_m_norm1_w', 'new_m_w_in', 'new_m_lb_logits', 'new_m_hg_norm_w', 'new_m_q_norm_w', 'new_m_k_norm_w', 'new_m_w_out', 'new_m_norm2_w', 'new_m_w_up', 'new_m_conv_w', 'new_m_conv_b', 'new_m_w_down', 'new_v_w_ada', 'new_v_b_ada', 'new_v_norm1_w', 'new_v_w_in', 'new_v_lb_logits', 'new_v_hg_norm_w', 'new_v_q_norm_w', 'new_v_k_norm_w', 'new_v_w_out', 'new_v_norm2_w', 'new_v_w_up', 'new_v_conv_w', 'new_v_conv_b', 'new_v_w_down']
TWIN_LEAF_KINDS = {'loss': 'loss', 'grad_x': 'grad_x', 'grad_w_ada': 'grad_w', 'grad_b_ada': 'grad_w', 'grad_norm1_w': 'grad_w', 'grad_w_in': 'grad_w', 'grad_lb_logits': 'grad_w', 'grad_hg_norm_w': 'grad_w', 'grad_q_norm_w': 'grad_w', 'grad_k_norm_w': 'grad_w', 'grad_w_out': 'grad_w', 'grad_norm2_w': 'grad_w', 'grad_w_up': 'grad_w', 'grad_conv_w': 'grad_w', 'grad_conv_b': 'grad_w', 'grad_w_down': 'grad_w', 'delta_w_ada': 'delta_w', 'delta_b_ada': 'delta_w', 'delta_norm1_w': 'delta_w', 'delta_w_in': 'delta_w', 'delta_lb_logits': 'delta_w', 'delta_hg_norm_w': 'delta_w', 'delta_q_norm_w': 'delta_w', 'delta_k_norm_w': 'delta_w', 'delta_w_out': 'delta_w', 'delta_norm2_w': 'delta_w', 'delta_w_up': 'delta_w', 'delta_conv_w': 'delta_w', 'delta_conv_b': 'delta_w', 'delta_w_down': 'delta_w', 'new_m_w_ada': 'new_m', 'new_m_b_ada': 'new_m', 'new_m_norm1_w': 'new_m', 'new_m_w_in': 'new_m', 'new_m_lb_logits': 'new_m', 'new_m_hg_norm_w': 'new_m', 'new_m_q_norm_w': 'new_m', 'new_m_k_norm_w': 'new_m', 'new_m_w_out': 'new_m', 'new_m_norm2_w': 'new_m', 'new_m_w_up': 'new_m', 'new_m_conv_w': 'new_m', 'new_m_conv_b': 'new_m', 'new_m_w_down': 'new_m', 'new_v_w_ada': 'new_v', 'new_v_b_ada': 'new_v', 'new_v_norm1_w': 'new_v', 'new_v_w_in': 'new_v', 'new_v_lb_logits': 'new_v', 'new_v_hg_norm_w': 'new_v', 'new_v_q_norm_w': 'new_v', 'new_v_k_norm_w': 'new_v', 'new_v_w_out': 'new_v', 'new_v_norm2_w': 'new_v', 'new_v_w_up': 'new_v', 'new_v_conv_w': 'new_v', 'new_v_conv_b': 'new_v', 'new_v_w_down': 'new_v'}


def _forward(args):
    return _fwd_reference(*[args[k] for k in FWD_PARAMS])


def _output_shape():
    out = _jax.eval_shape(lambda: _forward(_fwd_setup_inputs(0)))
    return out.shape, out.dtype

N_MICROBATCH = 1
ADAM_LR = 0.001
ADAM_B1 = 0.9
ADAM_B2 = 0.999
ADAM_EPS = 1e-08
ADAM_WD = 0.01
ADAM_STEP = 10
PER_EXAMPLE_BATCH_AXIS = {'x': 0, 'c': 0, 'loss_target': 0}
SHARED_INPUTS = []
_WEIGHT_DTYPES = {'w_ada': _jnp.float32, 'b_ada': _jnp.float32, 'norm1_w': _jnp.float32, 'w_in': _jnp.float32, 'lb_logits': _jnp.float32, 'hg_norm_w': _jnp.float32, 'q_norm_w': _jnp.float32, 'k_norm_w': _jnp.float32, 'w_out': _jnp.float32, 'norm2_w': _jnp.float32, 'w_up': _jnp.float32, 'conv_w': _jnp.float32, 'conv_b': _jnp.float32, 'w_down': _jnp.float32}
MOMENT_SCALE = {'w_ada': 1.555256e+00, 'b_ada': 3.259878e+00, 'norm1_w': 1.275015e+00, 'w_in': 7.361820e-01, 'lb_logits': 9.144759e-03, 'hg_norm_w': 1.692863e+01, 'q_norm_w': 6.432001e-01, 'k_norm_w': 6.443966e-01, 'w_out': 1.107693e+00, 'norm2_w': 7.404985e+00, 'w_up': 5.936083e-01, 'conv_w': 1.226745e+00, 'conv_b': 9.863663e-01, 'w_down': 4.099380e-01}


def _to_microbatches(a, axis):
    t = _jnp.moveaxis(a, axis, 0)
    t = t.reshape((N_MICROBATCH, t.shape[0] // N_MICROBATCH) + t.shape[1:])
    return _jnp.moveaxis(t, 1, axis + 1)


def setup_inputs(seed: int = 0) -> dict:
    inp = _fwd_setup_inputs(seed)
    key = _jax.random.fold_in(_jax.random.key(seed), 7919)
    shape, _ = _output_shape()
    out = dict(inp)
    out["loss_target"] = _jax.random.normal(_jax.random.fold_in(key, 0), shape, _jnp.float32)
    for i, name in enumerate(TWIN_WEIGHTS):
        w = inp[name].astype(_jnp.float32)
        if MOMENT_SCALE is None:
            s = _jnp.sqrt(_jnp.mean(_jnp.square(w)) + 1e-30)
        else:
            s = MOMENT_SCALE[name]
        km, kv = _jax.random.split(_jax.random.fold_in(key, i + 1))
        out[name] = w
        out["m_" + name] = s * _jax.random.normal(km, w.shape, _jnp.float32)
        out["v_" + name] = (s * s) * _jax.random.uniform(kv, w.shape, _jnp.float32, 0.5, 1.5)
    if N_MICROBATCH > 1:
        for name, axis in PER_EXAMPLE_BATCH_AXIS.items():
            out[name] = _to_microbatches(out[name], axis)
    return {'x': out['x'], 'c': out['c'], 'w_ada': out['w_ada'], 'b_ada': out['b_ada'], 'norm1_w': out['norm1_w'], 'w_in': out['w_in'], 'lb_logits': out['lb_logits'], 'hg_norm_w': out['hg_norm_w'], 'q_norm_w': out['q_norm_w'], 'k_norm_w': out['k_norm_w'], 'w_out': out['w_out'], 'norm2_w': out['norm2_w'], 'w_up': out['w_up'], 'conv_w': out['conv_w'], 'conv_b': out['conv_b'], 'w_down': out['w_down'], 'loss_target': out['loss_target'], 'm_w_ada': out['m_w_ada'], 'm_b_ada': out['m_b_ada'], 'm_norm1_w': out['m_norm1_w'], 'm_w_in': out['m_w_in'], 'm_lb_logits': out['m_lb_logits'], 'm_hg_norm_w': out['m_hg_norm_w'], 'm_q_norm_w': out['m_q_norm_w'], 'm_k_norm_w': out['m_k_norm_w'], 'm_w_out': out['m_w_out'], 'm_norm2_w': out['m_norm2_w'], 'm_w_up': out['m_w_up'], 'm_conv_w': out['m_conv_w'], 'm_conv_b': out['m_conv_b'], 'm_w_down': out['m_w_down'], 'v_w_ada': out['v_w_ada'], 'v_b_ada': out['v_b_ada'], 'v_norm1_w': out['v_norm1_w'], 'v_w_in': out['v_w_in'], 'v_lb_logits': out['v_lb_logits'], 'v_hg_norm_w': out['v_hg_norm_w'], 'v_q_norm_w': out['v_q_norm_w'], 'v_k_norm_w': out['v_k_norm_w'], 'v_w_out': out['v_w_out'], 'v_norm2_w': out['v_norm2_w'], 'v_w_up': out['v_w_up'], 'v_conv_w': out['v_conv_w'], 'v_conv_b': out['v_conv_b'], 'v_w_down': out['v_w_down']}


def _loss(weights, diff, rest, loss_target):
    with _jax.named_scope("forward"):
        args = {**rest, TWIN_DIFF_INPUT: diff, **{k: w.astype(_WEIGHT_DTYPES[k]) for k, w in weights.items()}}
        y = _forward(args)
    with _jax.named_scope("loss_head"):
        err = _jnp.square(y.astype(_jnp.float32) - loss_target)
        return 0.5 * _jnp.sum(_jnp.mean(err, axis=-1)) if err.ndim else 0.5 * err


def _adamw(w, g, m, v):
    m = ADAM_B1 * m + (1.0 - ADAM_B1) * g
    v = ADAM_B2 * v + (1.0 - ADAM_B2) * _jnp.square(g)
    m_hat = m / (1.0 - ADAM_B1 ** ADAM_STEP)
    v_hat = v / (1.0 - ADAM_B2 ** ADAM_STEP)
    delta = -ADAM_LR * (m_hat / (_jnp.sqrt(v_hat) + ADAM_EPS) + ADAM_WD * w)
    return delta, m, v


def reference(x, c, w_ada, b_ada, norm1_w, w_in, lb_logits, hg_norm_w, q_norm_w, k_norm_w, w_out, norm2_w, w_up, conv_w, conv_b, w_down, loss_target, m_w_ada, m_b_ada, m_norm1_w, m_w_in, m_lb_logits, m_hg_norm_w, m_q_norm_w, m_k_norm_w, m_w_out, m_norm2_w, m_w_up, m_conv_w, m_conv_b, m_w_down, v_w_ada, v_b_ada, v_norm1_w, v_w_in, v_lb_logits, v_hg_norm_w, v_q_norm_w, v_k_norm_w, v_w_out, v_norm2_w, v_w_up, v_conv_w, v_conv_b, v_w_down):
    given = dict(x=x, c=c, w_ada=w_ada, b_ada=b_ada, norm1_w=norm1_w, w_in=w_in, lb_logits=lb_logits, hg_norm_w=hg_norm_w, q_norm_w=q_norm_w, k_norm_w=k_norm_w, w_out=w_out, norm2_w=norm2_w, w_up=w_up, conv_w=conv_w, conv_b=conv_b, w_down=w_down, loss_target=loss_target, m_w_ada=m_w_ada, m_b_ada=m_b_ada, m_norm1_w=m_norm1_w, m_w_in=m_w_in, m_lb_logits=m_lb_logits, m_hg_norm_w=m_hg_norm_w, m_q_norm_w=m_q_norm_w, m_k_norm_w=m_k_norm_w, m_w_out=m_w_out, m_norm2_w=m_norm2_w, m_w_up=m_w_up, m_conv_w=m_conv_w, m_conv_b=m_conv_b, m_w_down=m_w_down, v_w_ada=v_w_ada, v_b_ada=v_b_ada, v_norm1_w=v_norm1_w, v_w_in=v_w_in, v_lb_logits=v_lb_logits, v_hg_norm_w=v_hg_norm_w, v_q_norm_w=v_q_norm_w, v_k_norm_w=v_k_norm_w, v_w_out=v_w_out, v_norm2_w=v_norm2_w, v_w_up=v_w_up, v_conv_w=v_conv_w, v_conv_b=v_conv_b, v_w_down=v_w_down)
    weights = {n: given[n] for n in TWIN_WEIGHTS}
    shared = {n: given[n] for n in SHARED_INPUTS}
    per_example = {n: given[n] for n in ['x', 'c']}
    grad_fn = _jax.value_and_grad(_loss, argnums=(0, 1))

    def one_microbatch(ex, loss_target):
        ex = dict(ex)
        diff = ex.pop(TWIN_DIFF_INPUT)
        return grad_fn(weights, diff, {**shared, **ex}, loss_target)

    if N_MICROBATCH == 1:
        loss, (grad_w, grad_x) = one_microbatch(per_example, given["loss_target"])
    else:
        def body(carry, xs):
            loss_sum, grad_sum = carry
            l_k, (gw_k, gx_k) = one_microbatch(xs[0], xs[1])
            with _jax.named_scope("update"):
                return (loss_sum + l_k, _jax.tree.map(_jnp.add, grad_sum, gw_k)), gx_k

        init = (_jnp.zeros((), _jnp.float32), _jax.tree.map(_jnp.zeros_like, weights))
        (loss, grad_w), grad_x = _jax.lax.scan(body, init, (per_example, given["loss_target"]))
    with _jax.named_scope("update"):
        delta_w, new_m, new_v = {}, {}, {}
        for n in TWIN_WEIGHTS:
            delta_w[n], new_m[n], new_v[n] = _adamw(weights[n], grad_w[n], given["m_" + n], given["v_" + n])
    return (loss, grad_x, *[grad_w[n] for n in TWIN_WEIGHTS], *[delta_w[n] for n in TWIN_WEIGHTS],
            *[new_m[n] for n in TWIN_WEIGHTS], *[new_v[n] for n in TWIN_WEIGHTS])
```

```python
import functools

import jax
import jax.numpy as jnp
from jax import lax
from jax.experimental import pallas as pl
from jax.experimental.pallas import tpu as pltpu

F32 = jnp.float32
BF16 = jnp.bfloat16

N_DEV = 8
SEQ = 2048
D_MODEL = 2048
HEADS = 8
HEAD_DIM = 128
IN_COLS = 7168
IN_BLK = IN_COLS // N_DEV
D_FF = 5632
UP_BLK = 2 * D_FF // N_DEV
FF_BLK = D_FF // N_DEV
ADA_BLK = 6 * D_MODEL // N_DEV
OUT_BLK = D_MODEL // N_DEV
EPS = 1e-6
CHUNK = 16
ROW_TILE = 256
V7X_VMEM_LIMIT = 56 * 1024 * 1024

ADAM_LR = 0.001
ADAM_B1 = 0.9
ADAM_B2 = 0.999
ADAM_EPS = 1e-08
ADAM_WD = 0.01
ADAM_STEP = 10

NN = (((1,), (0,)), ((), ()))
NT = (((1,), (1,)), ((), ()))
TN = (((0,), (0,)), ((), ()))
MESH = pl.DeviceIdType.MESH
AXES = ("x", "y", "c")


def _params(sem=None, vmem=V7X_VMEM_LIMIT):
    return pltpu.CompilerParams(dimension_semantics=sem, vmem_limit_bytes=vmem)


def _sigmoid(x):
    return 1.0 / (1.0 + jnp.exp(-x))


def _dsilu(x, s):
    return s * (1.0 + x * (1.0 - s))


def _lane_sum(x, ones_bf16):
    hi = x.astype(BF16)
    lo = (x - hi.astype(F32)).astype(BF16)
    return (jnp.dot(hi, ones_bf16, preferred_element_type=F32)
            + jnp.dot(lo, ones_bf16, preferred_element_type=F32))


def _mesh_pos():
    return lax.axis_index("x"), lax.axis_index("y"), lax.axis_index("c")


def _allgather_vmem(x_blk, name):
    m_per, n = x_blk.shape

    def body(x_ref, out_ref, send_sems, recv_sems, local_sem):
        x, y, c = _mesh_pos()
        me, sibling = (x, y, c), (x, y, 1 - c)
        chips = [(1 - x, y), (x, 1 - y), (1 - x, 1 - y)]

        def rows(px, py, pc):
            return out_ref.at[pl.ds((4 * px + 2 * py + pc) * m_per, m_per), :]

        def copy(k, block, to, src=None):
            return pltpu.make_async_remote_copy(
                src_ref=rows(*block) if src is None else src, dst_ref=rows(*block),
                send_sem=send_sems.at[k], recv_sem=recv_sems.at[k], device_id=to, device_id_type=MESH)

        mine = pltpu.make_async_copy(x_ref, rows(*me), local_sem)
        mine.start()
        first = [copy(0, me, sibling, src=x_ref)]
        first += [copy(1 + j, me, (*chip, c), src=x_ref) for j, chip in enumerate(chips)]
        for cp in first:
            cp.start()
        passed = [copy(4 + j, (*chip, c), sibling) for j, chip in enumerate(chips)]
        for j, chip in enumerate(chips):
            copy(1 + j, (*chip, c), me).wait_recv()
            passed[j].start()
        copy(0, sibling, me).wait_recv()
        for j, chip in enumerate(chips):
            copy(4 + j, (*chip, 1 - c), me).wait_recv()
        for cp in first + passed:
            cp.wait_send()
        mine.wait()

    return pl.pallas_call(
        body, name=name,
        out_shape=jax.ShapeDtypeStruct((N_DEV * m_per, n), x_blk.dtype),
        in_specs=[pl.BlockSpec(memory_space=pltpu.VMEM)],
        out_specs=pl.BlockSpec(memory_space=pltpu.VMEM),
        scratch_shapes=[pltpu.SemaphoreType.DMA((7,)), pltpu.SemaphoreType.DMA((7,)), pltpu.SemaphoreType.DMA],
    )(x_blk)


def _allgather_weights(blocks):
    n_arr = len(blocks)

    def body(*refs):
        ins, outs = refs[:n_arr], refs[n_arr:2 * n_arr]
        send_sems, recv_sems, local_sems = refs[2 * n_arr:]
        x, y, c = _mesh_pos()
        me, sibling = (x, y, c), (x, y, 1 - c)
        chips = [(1 - x, y), (x, 1 - y), (1 - x, 1 - y)]

        def slot(a, px, py, pc):
            return outs[a].at[4 * px + 2 * py + pc]

        def copy(a, k, block, to, src=None):
            return pltpu.make_async_remote_copy(
                src_ref=slot(a, *block) if src is None else src, dst_ref=slot(a, *block),
                send_sem=send_sems.at[a, k], recv_sem=recv_sems.at[a, k], device_id=to, device_id_type=MESH)

        mine, first, passed = [], [], []
        for a in range(n_arr):
            cp = pltpu.make_async_copy(ins[a], slot(a, *me), local_sems.at[a])
            cp.start()
            mine.append(cp)
            first.append(copy(a, 0, me, sibling, src=ins[a]))
            first += [copy(a, 1 + j, me, (*chip, c), src=ins[a]) for j, chip in enumerate(chips)]
        for cp in first:
            cp.start()
        for j, chip in enumerate(chips):
            for a in range(n_arr):
                copy(a, 1 + j, (*chip, c), me).wait_recv()
                cp = copy(a, 4 + j, (*chip, c), sibling)
                cp.start()
                passed.append(cp)
        for a in range(n_arr):
            copy(a, 0, sibling, me).wait_recv()
            for j, chip in enumerate(chips):
                copy(a, 4 + j, (*chip, 1 - c), me).wait_recv()
        for cp in first + passed:
            cp.wait_send()
        for cp in mine:
            cp.wait()

    hbm = pl.BlockSpec(memory_space=pltpu.HBM)
    return pl.pallas_call(
        body, name="allgather_weights",
        out_shape=[jax.ShapeDtypeStruct((N_DEV,) + b.shape, b.dtype) for b in blocks],
        in_specs=[hbm] * n_arr, out_specs=[hbm] * n_arr,
        scratch_shapes=[pltpu.SemaphoreType.DMA((n_arr, 7)), pltpu.SemaphoreType.DMA((n_arr, 7)),
                        pltpu.SemaphoreType.DMA((n_arr,))],
    )(*blocks)


def _exchange_sibling(parts):
    n_arr = len(parts)

    def body(*refs):
        ins, outs = refs[:n_arr], refs[n_arr:2 * n_arr]
        send_sems, recv_sems = refs[2 * n_arr:]
        x, y, c = _mesh_pos()
        copies = [pltpu.make_async_remote_copy(
            src_ref=ins[a], dst_ref=outs[a], send_sem=send_sems.at[a], recv_sem=recv_sems.at[a],
            device_id=(x, y, 1 - c), device_id_type=MESH) for a in range(n_arr)]
        for cp in copies:
            cp.start()
        for cp in copies:
            cp.wait_recv()
        for cp in copies:
            cp.wait_send()

    hbm = pl.BlockSpec(memory_space=pltpu.HBM)
    return pl.pallas_call(
        body, name="grad_exchange_sibling",
        out_shape=[jax.ShapeDtypeStruct(p.shape, p.dtype) for p in parts],
        in_specs=[hbm] * n_arr, out_specs=[hbm] * n_arr,
        scratch_shapes=[pltpu.SemaphoreType.DMA((n_arr,)), pltpu.SemaphoreType.DMA((n_arr,))],
    )(*parts)


def _exchange_chips(chip_sums):
    n_arr = len(chip_sums)

    def body(*refs):
        ins, outs = refs[:n_arr], refs[n_arr:2 * n_arr]
        send_sems, recv_sems = refs[2 * n_arr:]
        x, y, c = _mesh_pos()
        chips = [(1 - x, y), (x, 1 - y), (1 - x, 1 - y)]
        copies = []
        for a in range(n_arr):
            for j, (px, py) in enumerate(chips):
                copies.append(pltpu.make_async_remote_copy(
                    src_ref=ins[a].at[2 * px + py], dst_ref=outs[a].at[j],
                    send_sem=send_sems.at[a, j], recv_sem=recv_sems.at[a, j],
                    device_id=(px, py, c), device_id_type=MESH))
        for cp in copies:
            cp.start()
        for cp in copies:
            cp.wait_recv()
        for cp in copies:
            cp.wait_send()

    hbm = pl.BlockSpec(memory_space=pltpu.HBM)
    return pl.pallas_call(
        body, name="grad_exchange_chips",
        out_shape=[jax.ShapeDtypeStruct((3,) + p.shape[1:], p.dtype) for p in chip_sums],
        in_specs=[hbm] * n_arr, out_specs=[hbm] * n_arr,
        scratch_shapes=[pltpu.SemaphoreType.DMA((n_arr, 3)), pltpu.SemaphoreType.DMA((n_arr, 3))],
    )(*chip_sums)


def _matmul(name, a, b, dims, grid, a_spec, b_spec, o_spec, out_shape, acc_axis=None):
    def body(a_ref, b_ref, o_ref):
        r = lax.dot_general(a_ref[...], b_ref[...], dims, preferred_element_type=F32)
        if acc_axis is None:
            o_ref[...] = r.astype(o_ref.dtype)
        else:
            k = pl.program_id(acc_axis)

            @pl.when(k == 0)
            def _():
                o_ref[...] = r

            @pl.when(k > 0)
            def _():
                o_ref[...] += r

    sem = tuple("arbitrary" if i == acc_axis else "parallel" for i in range(len(grid)))
    return pl.pallas_call(body, name=name, grid=grid, in_specs=[a_spec, b_spec], out_specs=o_spec,
                          out_shape=out_shape, compiler_params=_params(sem))(a, b)


def _mm_blocked_rhs(name, a, w_g, tm=512):
    m, k = a.shape
    nb = w_g.shape[2]
    return _matmul(name, a, w_g, NN, (N_DEV, m // tm),
                   pl.BlockSpec((tm, k), lambda j, i: (i, 0)),
                   pl.BlockSpec((None, k, nb), lambda j, i: (j, 0, 0)),
                   pl.BlockSpec((tm, nb), lambda j, i: (i, j)),
                   jax.ShapeDtypeStruct((m, N_DEV * nb), F32))


def _mm_blocked_rhs_t(name, a, w_g, tm=512):
    m = a.shape[0]
    n, nb = w_g.shape[1], w_g.shape[2]
    return _matmul(name, a, w_g, NT, (m // tm, N_DEV),
                   pl.BlockSpec((tm, nb), lambda i, j: (i, j)),
                   pl.BlockSpec((None, n, nb), lambda i, j: (j, 0, 0)),
                   pl.BlockSpec((tm, n), lambda i, j: (i, 0)),
                   jax.ShapeDtypeStruct((m, n), F32), acc_axis=1)


def _mm_wgrad_blocked(name, act, dcols, tk=512):
    t, k = act.shape
    nb = dcols.shape[1] // N_DEV
    return _matmul(name, act, dcols, TN, (N_DEV, k // tk),
                   pl.BlockSpec((t, tk), lambda j, i: (0, i)),
                   pl.BlockSpec((t, nb), lambda j, i: (0, j)),
                   pl.BlockSpec((None, tk, nb), lambda j, i: (j, i, 0)),
                   jax.ShapeDtypeStruct((N_DEV, k, nb), BF16))


def _mm_plain(name, a, b, dims, tm, tn, out_dtype):
    if dims == NN:
        (m, k), n = a.shape, b.shape[1]
        a_spec = pl.BlockSpec((tm, k), lambda i, j: (i, 0))
        b_spec = pl.BlockSpec((k, tn), lambda i, j: (0, j))
    elif dims == NT:
        (m, k), n = a.shape, b.shape[0]
        a_spec = pl.BlockSpec((tm, k), lambda i, j: (i, 0))
        b_spec = pl.BlockSpec((tn, k), lambda i, j: (j, 0))
    else:
        (k, m), n = a.shape, b.shape[1]
        a_spec = pl.BlockSpec((k, tm), lambda i, j: (0, i))
        b_spec = pl.BlockSpec((k, tn), lambda i, j: (0, j))
    return _matmul(name, a, b, dims, (m // tm, n // tn), a_spec, b_spec,
                   pl.BlockSpec((tm, tn), lambda i, j: (i, j)), jax.ShapeDtypeStruct((m, n), out_dtype))


def _ada_fwd(c_all, w_ada_blk, b_blk):
    def body(c_ref, w_ref, b_ref, o_ref):
        cv = c_ref[...]
        o_ref[...] = jnp.dot(cv * _sigmoid(cv), w_ref[...], preferred_element_type=F32) + b_ref[...]

    tn = 512
    return pl.pallas_call(
        body, name="ada_fwd", grid=(ADA_BLK // tn,),
        in_specs=[pl.BlockSpec((N_DEV, D_MODEL), lambda j: (0, 0)),
                  pl.BlockSpec((D_MODEL, tn), lambda j: (0, j)),
                  pl.BlockSpec((1, tn), lambda j: (0, j))],
        out_specs=pl.BlockSpec((N_DEV, tn), lambda j: (0, j)),
        out_shape=jax.ShapeDtypeStruct((N_DEV, ADA_BLK), F32),
        compiler_params=_params(("parallel",)))(c_all, w_ada_blk, b_blk)


def _ada_wgrad(c_all, gmod_cols):
    def body(c_ref, g_ref, o_ref):
        cv = c_ref[...]
        o_ref[...] = lax.dot_general(cv * _sigmoid(cv), g_ref[...], TN, preferred_element_type=F32)

    tk = 512
    return pl.pallas_call(
        body, name="ada_wgrad", grid=(D_MODEL // tk,),
        in_specs=[pl.BlockSpec((N_DEV, tk), lambda i: (0, i)),
                  pl.BlockSpec((N_DEV, ADA_BLK), lambda i: (0, 0))],
        out_specs=pl.BlockSpec((tk, ADA_BLK), lambda i: (i, 0)),
        out_shape=jax.ShapeDtypeStruct((D_MODEL, ADA_BLK), F32),
        compiler_params=_params(("parallel",)))(c_all, gmod_cols)


def _row_spec(cols=D_MODEL):
    return pl.BlockSpec((ROW_TILE, cols), lambda i: (i, 0))


def _vec_spec(cols=D_MODEL):
    return pl.BlockSpec((1, cols), lambda i: (0, 0))


def _norm_fwd(name, x, w, scale, shift, resid=None, gate=None):
    has_res = resid is not None

    def body(*refs):
        if has_res:
            x_ref, r_ref, g_ref, w_ref, sc_ref, sh_ref, xr_ref, h_ref, rs_ref = refs
            xr = x_ref[...] + g_ref[...] * r_ref[...]
            xr_ref[...] = xr
        else:
            x_ref, w_ref, sc_ref, sh_ref, h_ref, rs_ref = refs
            xr = x_ref[...]
        rs = lax.rsqrt(jnp.mean(xr * xr, axis=-1, keepdims=True) + EPS)
        h = (xr * rs) * w_ref[...] * (1.0 + sc_ref[...]) + sh_ref[...]
        h_ref[...] = h.astype(BF16)
        rs_ref[...] = rs

    s = x.shape[0]
    ins = [x] + ([resid, gate] if has_res else []) + [w, scale, shift]
    in_specs = [_row_spec()] + ([_row_spec(), _vec_spec()] if has_res else []) + [_vec_spec()] * 3
    outs = ([jax.ShapeDtypeStruct((s, D_MODEL), F32)] if has_res else []) + [
        jax.ShapeDtypeStruct((s, D_MODEL), BF16), jax.ShapeDtypeStruct((s, 1), F32)]
    out_specs = ([_row_spec()] if has_res else []) + [_row_spec(), pl.BlockSpec((ROW_TILE, 1), lambda i: (i, 0))]
    return pl.pallas_call(body, name=name, grid=(s // ROW_TILE,), in_specs=in_specs, out_specs=out_specs,
                          out_shape=outs, compiler_params=_params(("parallel",)))(*ins)


def _norm_bwd(name, dh, x, rstd, w, scale, dres, mix=None, gate=None):
    has_mix = mix is not None

    def body(*refs):
        if has_mix:
            (dh_ref, x_ref, rs_ref, w_ref, sc_ref, dr_ref, mix_ref, g_ref,
             dx_ref, dmix_ref, dsh_ref, dsc_ref, dw_ref, dg_ref) = refs
        else:
            dh_ref, x_ref, rs_ref, w_ref, sc_ref, dr_ref, dx_ref, dsh_ref, dsc_ref, dw_ref = refs
        i = pl.program_id(0)
        dhv = dh_ref[...]
        rs = rs_ref[...]
        xn = x_ref[...] * rs
        wv = w_ref[...]
        one_sc = 1.0 + sc_ref[...]
        dxn = dhv * wv * one_sc
        dx = dr_ref[...] + rs * (dxn - xn * jnp.mean(dxn * xn, axis=-1, keepdims=True))
        dx_ref[...] = dx
        sums = [(dsh_ref, dhv), (dsc_ref, dhv * xn * wv), (dw_ref, dhv * one_sc * xn)]
        if has_mix:
            dmix_ref[...] = (dx * g_ref[...]).astype(BF16)
            sums.append((dg_ref, dx * mix_ref[...]))

        @pl.when(i == 0)
        def _():
            for ref, _v in sums:
                ref[...] = jnp.zeros_like(ref)

        for ref, v in sums:
            ref[...] += jnp.sum(v, axis=0, keepdims=True)

    s = x.shape[0]
    ins = [dh, x, rstd, w, scale, dres] + ([mix, gate] if has_mix else [])
    in_specs = ([_row_spec(), _row_spec(), pl.BlockSpec((ROW_TILE, 1), lambda i: (i, 0)), _vec_spec(), _vec_spec(),
                 _row_spec()] + ([_row_spec(), _vec_spec()] if has_mix else []))
    vec = jax.ShapeDtypeStruct((1, D_MODEL), F32)
    outs = ([jax.ShapeDtypeStruct((s, D_MODEL), F32)] + ([jax.ShapeDtypeStruct((s, D_MODEL), BF16)] if has_mix else [])
            + [vec] * (4 if has_mix else 3))
    out_specs = [_row_spec()] + ([_row_spec()] if has_mix else []) + [_vec_spec()] * (4 if has_mix else 3)
    return pl.pallas_call(body, name=name, grid=(s // ROW_TILE,), in_specs=in_specs, out_specs=out_specs,
                          out_shape=outs, compiler_params=_params(("arbitrary",)))(*ins)


def _loss_head(x1, ffn, gate2, target):
    def body(x_ref, f_ref, g_ref, t_ref, loss_ref, dout_ref, dffn_ref, dg_ref):
        i = pl.program_id(0)
        fv = f_ref[...]
        gv = g_ref[...]
        err = x_ref[...] + gv * fv - t_ref[...]
        dout = err * (1.0 / D_MODEL)
        dout_ref[...] = dout
        dffn_ref[...] = (dout * gv).astype(BF16)

        @pl.when(i == 0)
        def _():
            loss_ref[...] = jnp.zeros_like(loss_ref)
            dg_ref[...] = jnp.zeros_like(dg_ref)

        row = jnp.sum(err * err, axis=-1, keepdims=True) * (1.0 / D_MODEL)
        loss_ref[...] += jnp.broadcast_to(0.5 * jnp.sum(row, axis=0, keepdims=True), (1, 128))
        dg_ref[...] += jnp.sum(dout * fv, axis=0, keepdims=True)

    s = x1.shape[0]
    return pl.pallas_call(
        body, name="loss_head", grid=(s // ROW_TILE,),
        in_specs=[_row_spec(), _row_spec(), _vec_spec(), _row_spec()],
        out_specs=[pl.BlockSpec((1, 128), lambda i: (0, 0)), _row_spec(), _row_spec(), _vec_spec()],
        out_shape=[jax.ShapeDtypeStruct((1, 128), F32), jax.ShapeDtypeStruct((s, D_MODEL), F32),
                   jax.ShapeDtypeStruct((s, D_MODEL), BF16), jax.ShapeDtypeStruct((1, D_MODEL), F32)],
        compiler_params=_params(("arbitrary",)))(x1, ffn, gate2, target)


CONV_TILE = 512
N_CONV_TILES = D_FF // CONV_TILE


def _shift_rows(a, k, row):
    n = a.shape[0]
    if k > 0:
        return jnp.where(row >= k, pltpu.roll(a, k, 0), 0.0)
    return jnp.where(row < n + k, pltpu.roll(a, n + k, 0), 0.0)


def _conv_gate_fwd(u, conv_w, conv_b):
    s = u.shape[0]

    def body(a_ref, g_ref, w_ref, b_ref, y_ref):
        a = a_ref[...]
        w = w_ref[...]
        row = lax.broadcasted_iota(jnp.int32, a.shape, 0)
        ac = b_ref[...] + _shift_rows(a, 2, row) * w[0:1] + _shift_rows(a, 1, row) * w[1:2] + a * w[2:3]
        y_ref[...] = (ac * _sigmoid(ac) * g_ref[...]).astype(BF16)

    col = lambda off: pl.BlockSpec((s, CONV_TILE), lambda i: (0, i + off))
    return pl.pallas_call(
        body, name="conv_gate_fwd", grid=(N_CONV_TILES,),
        in_specs=[col(0), col(N_CONV_TILES), pl.BlockSpec((3, CONV_TILE), lambda i: (0, i)),
                  pl.BlockSpec((1, CONV_TILE), lambda i: (0, i))],
        out_specs=col(0), out_shape=jax.ShapeDtypeStruct((s, D_FF), BF16),
        compiler_params=_params(("parallel",)))(u, u, conv_w, conv_b)


def _conv_gate_bwd(u, dy, conv_w, conv_b):
    s = u.shape[0]

    def body(a_ref, g_ref, dy_ref, w_ref, b_ref, da_ref, dg_ref, gw_ref, gb_ref):
        a = a_ref[...]
        w = w_ref[...]
        row = lax.broadcasted_iota(jnp.int32, a.shape, 0)
        a1 = _shift_rows(a, 1, row)
        a2 = _shift_rows(a, 2, row)
        ac = b_ref[...] + a2 * w[0:1] + a1 * w[1:2] + a * w[2:3]
        sg = _sigmoid(ac)
        dyv = dy_ref[...]
        dg_ref[...] = (dyv * (ac * sg)).astype(BF16)
        dac = dyv * g_ref[...] * _dsilu(ac, sg)
        gb_ref[...] = jnp.sum(dac, axis=0, keepdims=True)
        gw_ref[0:1, :] = jnp.sum(dac * a2, axis=0, keepdims=True)
        gw_ref[1:2, :] = jnp.sum(dac * a1, axis=0, keepdims=True)
        gw_ref[2:3, :] = jnp.sum(dac * a, axis=0, keepdims=True)
        da = dac * w[2:3] + _shift_rows(dac, -1, row) * w[1:2] + _shift_rows(dac, -2, row) * w[0:1]
        da_ref[...] = da.astype(BF16)

    col = lambda off: pl.BlockSpec((s, CONV_TILE), lambda i: (0, i + off))
    return pl.pallas_call(
        body, name="conv_gate_bwd", grid=(N_CONV_TILES,),
        in_specs=[col(0), col(N_CONV_TILES), col(0), pl.BlockSpec((3, CONV_TILE), lambda i: (0, i)),
                  pl.BlockSpec((1, CONV_TILE), lambda i: (0, i))],
        out_specs=[col(0), col(0), pl.BlockSpec((3, CONV_TILE), lambda i: (0, i)),
                   pl.BlockSpec((1, CONV_TILE), lambda i: (0, i))],
        out_shape=[jax.ShapeDtypeStruct((s, D_FF), BF16), jax.ShapeDtypeStruct((s, D_FF), BF16),
                   jax.ShapeDtypeStruct((3, D_FF), F32), jax.ShapeDtypeStruct((1, D_FF), F32)],
        compiler_params=_params(("parallel",)))(u, u, dy, conv_w, conv_b)


HG_TILE = 128


def _head_col(off):
    return pl.BlockSpec((SEQ, HEAD_DIM), lambda h: (0, h + off))


def _hgrn_gates(hq, hf, lb, pos):
    q = hq * _sigmoid(hq)
    sig = _sigmoid(hf)
    f = lb + (1.0 - lb) * sig
    gl = jnp.log(f)
    for sh in (1, 2, 4, 8):
        gl = gl + jnp.where(pos >= sh, pltpu.roll(gl, sh, 0), 0.0)
    return q, sig, f, 1.0 - f, gl


def _lower_bound(lbl):
    return 1.0 / (1.0 + jnp.exp(lbl[1:2, :] - lbl[0:1, :]))


def _hgrn_fwd(proj, lb_logits, norm_w):
    n_tiles = SEQ // HG_TILE
    n_chunks = SEQ // CHUNK

    def body(hq_ref, hf_ref, hi_ref, hg_ref, lbl_ref, nw_ref, aout_ref, opre_ref, q_s, k_s, gl_s):
        lb = _lower_bound(lbl_ref[...])
        ones = jnp.ones((HEAD_DIM, HEAD_DIM), BF16)
        pos = lax.broadcasted_iota(jnp.int32, (HG_TILE, HEAD_DIM), 0) % CHUNK

        def tile(i, carry):
            rows = pl.ds(pl.multiple_of(i * HG_TILE, HG_TILE), HG_TILE)
            v = hi_ref[rows, :]
            q, _sig, _f, kk, gl = _hgrn_gates(hq_ref[rows, :], hf_ref[rows, :], lb, pos)
            o = _lane_sum(q * kk, ones) * v
            for d in range(1, CHUNK):
                e = jnp.where(pos >= d, jnp.exp(gl - pltpu.roll(gl, d, 0)), 0.0)
                o = o + _lane_sum(q * pltpu.roll(kk, d, 0) * e, ones) * pltpu.roll(v, d, 0)
            q_s[rows, :] = q
            k_s[rows, :] = kk
            gl_s[rows, :] = gl
            opre_ref[rows, :] = o
            return carry

        lax.fori_loop(0, n_tiles, tile, 0)

        def chunk(c, st):
            rows = pl.ds(pl.multiple_of(c * CHUNK, CHUNK), CHUNK)
            gl = gl_s[rows, :]
            qt = q_s[rows, :] * jnp.exp(gl)
            opre_ref[rows, :] += lax.dot_general(qt.astype(BF16), st.astype(BF16), NT, preferred_element_type=F32)
            gll = gl[CHUNK - 1:CHUNK, :]
            kt = k_s[rows, :] * jnp.exp(gll - gl)
            return st * jnp.exp(gll) + lax.dot_general(hi_ref[rows, :].astype(BF16), kt.astype(BF16), TN,
                                                       preferred_element_type=F32)

        lax.fori_loop(0, n_chunks, chunk, jnp.zeros((HEAD_DIM, HEAD_DIM), F32))

        def finish(i, carry):
            rows = pl.ds(pl.multiple_of(i * HG_TILE, HG_TILE), HG_TILE)
            o = opre_ref[rows, :]
            hg = hg_ref[rows, :]
            rs = lax.rsqrt(jnp.mean(o * o, axis=-1, keepdims=True) + EPS)
            aout_ref[rows, :] = ((o * rs) * nw_ref[...] * (hg * _sigmoid(hg))).astype(BF16)
            return carry

        lax.fori_loop(0, n_tiles, finish, 0)

    return pl.pallas_call(
        body, name="hgrn_fwd", grid=(HEADS,),
        in_specs=[_head_col(0), _head_col(HEADS), _head_col(2 * HEADS), _head_col(3 * HEADS),
                  pl.BlockSpec((2, HEAD_DIM), lambda h: (0, h)), pl.BlockSpec((1, HEAD_DIM), lambda h: (0, 0))],
        out_specs=[_head_col(0), _head_col(0)],
        out_shape=[jax.ShapeDtypeStruct((SEQ, HEADS * HEAD_DIM), BF16), jax.ShapeDtypeStruct((SEQ, HEADS * HEAD_DIM), F32)],
        scratch_shapes=[pltpu.VMEM((SEQ, HEAD_DIM), F32)] * 3,
        compiler_params=_params(("parallel",)))(proj, proj, proj, proj, lb_logits, norm_w)


def _hgrn_bwd(proj, lb_logits, norm_w, o_pre, d_aout):
    n_tiles = SEQ // HG_TILE
    n_chunks = SEQ // CHUNK

    def body(hq_ref, hf_ref, hi_ref, hg_ref, lbl_ref, nw_ref, opre_ref, da_ref,
             dhq_ref, dhf_ref, dhi_ref, dhg_ref, dlog_ref, gnw_ref,
             q_s, k_s, gl_s, do_s, dq_s, dk_s, dv_s, st_s):
        h = pl.program_id(0)
        lb = _lower_bound(lbl_ref[...])
        nw = nw_ref[...]
        ones = jnp.ones((HEAD_DIM, HEAD_DIM), BF16)
        pos = lax.broadcasted_iota(jnp.int32, (HG_TILE, HEAD_DIM), 0) % CHUNK

        @pl.when(h == 0)
        def _():
            gnw_ref[...] = jnp.zeros_like(gnw_ref)

        def tile(i, carry):
            rows = pl.ds(pl.multiple_of(i * HG_TILE, HG_TILE), HG_TILE)
            v = hi_ref[rows, :]
            q, _sig, _f, kk, gl = _hgrn_gates(hq_ref[rows, :], hf_ref[rows, :], lb, pos)
            o = opre_ref[rows, :]
            hg = hg_ref[rows, :]
            da = da_ref[rows, :]
            rs = lax.rsqrt(jnp.mean(o * o, axis=-1, keepdims=True) + EPS)
            oh = o * rs
            sg = _sigmoid(hg)
            dnorm = da * (hg * sg)
            dhg_ref[rows, :] = (da * (oh * nw) * _dsilu(hg, sg)).astype(BF16)
            gnw_ref[...] += jnp.sum(dnorm * oh, axis=0, keepdims=True)
            doh = dnorm * nw
            do = rs * (doh - oh * jnp.mean(doh * oh, axis=-1, keepdims=True))

            d_a = _lane_sum(do * v, ones)
            dq = d_a * kk
            dk = d_a * q
            dv = _lane_sum(q * kk, ones) * do
            for d in range(1, CHUNK):
                ks = pltpu.roll(kk, d, 0)
                e = jnp.where(pos >= d, jnp.exp(gl - pltpu.roll(gl, d, 0)), 0.0)
                a_d = _lane_sum(q * ks * e, ones)
                d_a = _lane_sum(do * pltpu.roll(v, d, 0), ones) * e
                dq = dq + d_a * ks
                dk = dk + pltpu.roll(d_a * q, HG_TILE - d, 0)
                dv = dv + pltpu.roll(a_d * do, HG_TILE - d, 0)
            q_s[rows, :] = q
            k_s[rows, :] = kk
            gl_s[rows, :] = gl
            do_s[rows, :] = do
            dq_s[rows, :] = dq
            dk_s[rows, :] = dk
            dv_s[rows, :] = dv
            return carry

        lax.fori_loop(0, n_tiles, tile, 0)

        def fwd_chunk(c, st):
            rows = pl.ds(pl.multiple_of(c * CHUNK, CHUNK), CHUNK)
            gl = gl_s[rows, :]
            st_s[c] = st
            dq_s[rows, :] += jnp.dot(do_s[rows, :].astype(BF16), st.astype(BF16),
                                     preferred_element_type=F32) * jnp.exp(gl)
            gll = gl[CHUNK - 1:CHUNK, :]
            kt = k_s[rows, :] * jnp.exp(gll - gl)
            return st * jnp.exp(gll) + lax.dot_general(hi_ref[rows, :].astype(BF16), kt.astype(BF16), TN,
                                                       preferred_element_type=F32)

        lax.fori_loop(0, n_chunks, fwd_chunk, jnp.zeros((HEAD_DIM, HEAD_DIM), F32))

        pos_c = lax.broadcasted_iota(jnp.int32, (CHUNK, HEAD_DIM), 0)

        def bwd_chunk(i, carry):
            rt, dlb = carry
            c = n_chunks - 1 - i
            rows = pl.ds(pl.multiple_of(c * CHUNK, CHUNK), CHUNK)
            gl = gl_s[rows, :]
            q = q_s[rows, :]
            kk = k_s[rows, :]
            do = do_s[rows, :]
            gll = gl[CHUNK - 1:CHUNK, :]
            egl = jnp.exp(gll)
            ekt = jnp.exp(gll - gl)
            rt_b = rt.astype(BF16)
            dk_in = dk_s[rows, :]
            dk_far = jnp.dot(hi_ref[rows, :].astype(BF16), rt_b, preferred_element_type=F32) * ekt
            dk = dk_in + dk_far
            dv = dv_s[rows, :] + lax.dot_general((kk * ekt).astype(BF16), rt_b, NT, preferred_element_type=F32)
            dq = dq_s[rows, :]
            rc = q * dq - kk * dk_in
            pc = kk * dk_far
            pre = pc
            for sh in (1, 2, 4, 8):
                rc = rc + jnp.where(pos_c < CHUNK - sh, pltpu.roll(rc, CHUNK - sh, 0), 0.0)
                pre = pre + jnp.where(pos_c >= sh, pltpu.roll(pre, sh, 0), 0.0)
            across = jnp.sum(st_s[c] * rt, axis=0, keepdims=True) * egl
            dgl = rc + (pre - pc) + across
            hf = hf_ref[rows, :]
            sig = _sigmoid(hf)
            f = lb + (1.0 - lb) * sig
            df = dgl / f - dk
            dhf_ref[rows, :] = (df * (1.0 - lb) * sig * (1.0 - sig)).astype(BF16)
            hq = hq_ref[rows, :]
            dhq_ref[rows, :] = (dq * _dsilu(hq, _sigmoid(hq))).astype(BF16)
            dhi_ref[rows, :] = dv.astype(BF16)
            rt_new = rt * egl + lax.dot_general(do.astype(BF16), (q * jnp.exp(gl)).astype(BF16), TN,
                                                preferred_element_type=F32)
            return (rt_new, dlb + jnp.sum(df * (1.0 - sig), axis=0, keepdims=True))

        _, dlb = lax.fori_loop(0, n_chunks, bwd_chunk,
                               (jnp.zeros((HEAD_DIM, HEAD_DIM), F32), jnp.zeros((1, HEAD_DIM), F32)))
        dl0 = lb * (1.0 - lb) * dlb
        dlog_ref[0:1, :] = dl0
        dlog_ref[1:2, :] = -dl0

    wide = HEADS * HEAD_DIM
    return pl.pallas_call(
        body, name="hgrn_bwd", grid=(HEADS,),
        in_specs=[_head_col(0), _head_col(HEADS), _head_col(2 * HEADS), _head_col(3 * HEADS),
                  pl.BlockSpec((2, HEAD_DIM), lambda h: (0, h)), pl.BlockSpec((1, HEAD_DIM), lambda h: (0, 0)),
                  _head_col(0), _head_col(0)],
        out_specs=[_head_col(0)] * 4 + [pl.BlockSpec((2, HEAD_DIM), lambda h: (0, h)),
                                        pl.BlockSpec((1, HEAD_DIM), lambda h: (0, 0))],
        out_shape=[jax.ShapeDtypeStruct((SEQ, wide), BF16)] * 4 + [jax.ShapeDtypeStruct((2, wide), F32),
                                                                    jax.ShapeDtypeStruct((1, HEAD_DIM), F32)],
        scratch_shapes=[pltpu.VMEM((SEQ, HEAD_DIM), F32)] * 7 + [pltpu.VMEM((n_chunks, HEAD_DIM, HEAD_DIM), F32)],
        compiler_params=_params(("arbitrary",)))(proj, proj, proj, proj, lb_logits, norm_w, o_pre, d_aout)


Q_TILE = 256
ATT_SCALE = HEAD_DIM ** -0.5
ATT_OFF = 4 * HEADS


def _qk_prep(proj, q_w, k_w):
    def body(aq_ref, ak_ref, av_ref, qw_ref, kw_ref, qn_ref, kn_ref, v_ref):
        aq = aq_ref[...]
        ak = ak_ref[...]
        qn_ref[...] = (aq * lax.rsqrt(jnp.mean(aq * aq, axis=-1, keepdims=True) + EPS) * qw_ref[...]).astype(BF16)
        kn_ref[...] = (ak * lax.rsqrt(jnp.mean(ak * ak, axis=-1, keepdims=True) + EPS) * kw_ref[...]).astype(BF16)
        v_ref[...] = av_ref[...].astype(BF16)

    wide = HEADS * HEAD_DIM
    vec = pl.BlockSpec((1, HEAD_DIM), lambda h: (0, 0))
    return pl.pallas_call(
        body, name="qk_prep", grid=(HEADS,),
        in_specs=[_head_col(ATT_OFF), _head_col(ATT_OFF + HEADS), _head_col(ATT_OFF + 2 * HEADS), vec, vec],
        out_specs=[_head_col(0)] * 3, out_shape=[jax.ShapeDtypeStruct((SEQ, wide), BF16)] * 3,
        compiler_params=_params(("parallel",)))(proj, proj, proj, q_w, k_w)


def _alibi_slopes():
    slopes = jnp.exp2(-8.0 * jnp.arange(1, HEADS + 1, dtype=F32) / HEADS)
    return jnp.broadcast_to(slopes[:, None, None], (HEADS, 1, HEAD_DIM))


SLOPE_SPEC = pl.BlockSpec((None, 1, HEAD_DIM), lambda h, i: (h, 0, 0))


def _att_scores(q, k, slope, i):
    s = lax.dot_general(q, k, NT, preferred_element_type=F32) * ATT_SCALE
    row = lax.broadcasted_iota(jnp.int32, s.shape, 0) + i * Q_TILE
    col = lax.broadcasted_iota(jnp.int32, s.shape, 1)
    dist = row - col
    mult = ((dist <= 128).astype(F32) + (((dist & 3) == 0) & (dist <= 512)).astype(F32)
            + ((dist & 15) == 0).astype(F32))
    mult = jnp.where(dist >= 0, mult, 0.0)
    return s - slope * dist.astype(F32), mult


def _attn_fwd(qn, kn, vb):
    def body(q_ref, k_ref, v_ref, sl_ref, o_ref, lse_ref):
        i = pl.program_id(1)
        sb, mult = _att_scores(q_ref[...], k_ref[...], sl_ref[0:1, 0:1], i)
        sm = jnp.where(mult > 0.0, sb, -1e30)
        m = jnp.max(sm, axis=-1, keepdims=True)
        p = jnp.exp(sm - m) * mult
        l = jnp.sum(p, axis=-1, keepdims=True)
        o_ref[...] = jnp.dot(p.astype(BF16), v_ref[...], preferred_element_type=F32) / l
        lse_ref[...] = m + jnp.log(l)

    wide = HEADS * HEAD_DIM
    qt = pl.BlockSpec((Q_TILE, HEAD_DIM), lambda h, i: (i, h))
    full = pl.BlockSpec((SEQ, HEAD_DIM), lambda h, i: (0, h))
    return pl.pallas_call(
        body, name="attn_fwd", grid=(HEADS, SEQ // Q_TILE),
        in_specs=[qt, full, full, SLOPE_SPEC],
        out_specs=[qt, pl.BlockSpec((None, Q_TILE, 1), lambda h, i: (h, i, 0))],
        out_shape=[jax.ShapeDtypeStruct((SEQ, wide), F32), jax.ShapeDtypeStruct((HEADS, SEQ, 1), F32)],
        compiler_params=_params(("parallel", "parallel")))(qn, kn, vb, _alibi_slopes())


def _attn_bwd(qn, kn, vb, o, lse, d_mix):
    def body(q_ref, k_ref, v_ref, o_ref, lse_ref, do_ref, sl_ref, dq_ref, dk_ref, dv_ref):
        i = pl.program_id(1)
        q = q_ref[...]
        k = k_ref[...]
        do = do_ref[...]
        sb, mult = _att_scores(q, k, sl_ref[0:1, 0:1], i)
        p = jnp.where(mult > 0.0, jnp.exp(sb - lse_ref[...]), 0.0) * mult
        dp = lax.dot_general(do.astype(BF16), v_ref[...], NT, preferred_element_type=F32)
        delta = jnp.sum(do * o_ref[...], axis=-1, keepdims=True)
        ds = (p * (dp - delta)).astype(BF16)
        dq_ref[...] = jnp.dot(ds, k, preferred_element_type=F32) * ATT_SCALE

        @pl.when(i == 0)
        def _():
            dk_ref[...] = jnp.zeros_like(dk_ref)
            dv_ref[...] = jnp.zeros_like(dv_ref)

        dk_ref[...] += lax.dot_general(ds, q, TN, preferred_element_type=F32) * ATT_SCALE
        dv_ref[...] += lax.dot_general(p.astype(BF16), do.astype(BF16), TN, preferred_element_type=F32)

    wide = HEADS * HEAD_DIM
    qt = pl.BlockSpec((Q_TILE, HEAD_DIM), lambda h, i: (i, h))
    full = pl.BlockSpec((SEQ, HEAD_DIM), lambda h, i: (0, h))
    return pl.pallas_call(
        body, name="attn_bwd", grid=(HEADS, SEQ // Q_TILE),
        in_specs=[qt, full, full, qt, pl.BlockSpec((None, Q_TILE, 1), lambda h, i: (h, i, 0)),
                  pl.BlockSpec((Q_TILE, HEAD_DIM), lambda h, i: (i, h + HEADS)), SLOPE_SPEC],
        out_specs=[qt, full, full], out_shape=[jax.ShapeDtypeStruct((SEQ, wide), F32)] * 3,
        compiler_params=_params(("parallel", "arbitrary")))(qn, kn, vb, o, lse, d_mix, _alibi_slopes())


def _qk_bwd(proj, q_w, k_w, dqn, dkn, dv):
    def body(aq_ref, ak_ref, qw_ref, kw_ref, dqn_ref, dkn_ref, dv_ref, daq_ref, dak_ref, dav_ref, gq_ref, gk_ref):
        h = pl.program_id(0)

        @pl.when(h == 0)
        def _():
            gq_ref[...] = jnp.zeros_like(gq_ref)
            gk_ref[...] = jnp.zeros_like(gk_ref)

        def one(a_ref, w_ref, d_ref, da_ref, g_ref):
            a = a_ref[...]
            d = d_ref[...]
            rs = lax.rsqrt(jnp.mean(a * a, axis=-1, keepdims=True) + EPS)
            ah = a * rs
            g_ref[...] += jnp.sum(d * ah, axis=0, keepdims=True)
            dah = d * w_ref[...]
            da_ref[...] = (rs * (dah - ah * jnp.mean(dah * ah, axis=-1, keepdims=True))).astype(BF16)

        one(aq_ref, qw_ref, dqn_ref, daq_ref, gq_ref)
        one(ak_ref, kw_ref, dkn_ref, dak_ref, gk_ref)
        dav_ref[...] = dv_ref[...].astype(BF16)

    wide = HEADS * HEAD_DIM
    vec = pl.BlockSpec((1, HEAD_DIM), lambda h: (0, 0))
    return pl.pallas_call(
        body, name="qk_bwd", grid=(HEADS,),
        in_specs=[_head_col(ATT_OFF), _head_col(ATT_OFF + HEADS), vec, vec, _head_col(0), _head_col(0), _head_col(0)],
        out_specs=[_head_col(0)] * 3 + [vec, vec],
        out_shape=[jax.ShapeDtypeStruct((SEQ, wide), BF16)] * 3 + [jax.ShapeDtypeStruct((1, HEAD_DIM), F32)] * 2,
        compiler_params=_params(("arbitrary",)))(proj, proj, q_w, k_w, dqn, dkn, dv)


def _pair_sum(name, mine, theirs):
    _, r, c = mine.shape
    tr = r // 2 if r % 16 == 0 else r

    def body(a_ref, b_ref, o_ref):
        o_ref[...] = (a_ref[...].astype(F32) + b_ref[...].astype(F32)).astype(BF16)

    spec = pl.BlockSpec((None, tr, c), lambda q, i: (q, i, 0))
    return pl.pallas_call(body, name=name, grid=(4, r // tr), in_specs=[spec, spec], out_specs=spec,
                          out_shape=jax.ShapeDtypeStruct(mine.shape, BF16),
                          compiler_params=_params(("parallel", "parallel")))(mine, theirs)


def _adamw(name, w, m, v, addends, tr=None):
    r, c = w.shape
    tr = r if tr is None else tr
    n_add = len(addends)
    c1 = 1.0 - ADAM_B1 ** ADAM_STEP
    c2 = 1.0 - ADAM_B2 ** ADAM_STEP

    def body(*refs):
        w_ref, m_ref, v_ref = refs[:3]
        add_refs = refs[3:3 + n_add]
        g_ref, d_ref, nm_ref, nv_ref = refs[3 + n_add:]
        g = add_refs[0][...].astype(F32)
        for a_ref in add_refs[1:]:
            g = g + a_ref[...].astype(F32)
        nm = ADAM_B1 * m_ref[...] + (1.0 - ADAM_B1) * g
        nv = ADAM_B2 * v_ref[...] + (1.0 - ADAM_B2) * (g * g)
        g_ref[...] = g
        nm_ref[...] = nm
        nv_ref[...] = nv
        d_ref[...] = -ADAM_LR * ((nm / c1) / (jnp.sqrt(nv / c2) + ADAM_EPS) + ADAM_WD * w_ref[...])

    spec = pl.BlockSpec((tr, c), lambda i: (i, 0))
    out = jax.ShapeDtypeStruct((r, c), F32)
    return pl.pallas_call(body, name=name, grid=(r // tr,), in_specs=[spec] * (3 + n_add), out_specs=[spec] * 4,
                          out_shape=[out] * 4, compiler_params=_params(("parallel",)))(w, m, v, *addends)


def _sum_devices(gathered):
    _, r, c = gathered.shape

    def body(g_ref, o_ref):
        acc = g_ref[0]
        for d in range(1, N_DEV):
            acc = acc + g_ref[d]
        o_ref[...] = acc

    return pl.pallas_call(body, name="sum_devices", out_shape=jax.ShapeDtypeStruct((r, c), F32))(gathered)


def _pack_rows(vectors, rows):
    flat = jnp.concatenate([v.reshape(-1) for v in vectors])
    return jnp.pad(flat, (0, rows * 128 - flat.shape[0])).reshape(rows, 128)


def _unpack(flat, shapes):
    out, off = [], 0
    for shp in shapes:
        n = 1
        for d in shp:
            n *= d
        out.append(flat[off:off + n].reshape(shp))
        off += n
    return out


def _device_step(xs, tgt, mod, norm1_w, norm2_w, lb_logits, hg_norm_w, q_norm_w, k_norm_w, conv_w_full, conv_b,
                 win_g, wout_full, wup_g, wdown_full):
    shift1, scale1, gate1, shift2, scale2, gate2 = (mod[k] for k in range(6))

    h, rstd1 = _norm_fwd("norm1_fwd", xs, norm1_w, scale1, shift1)
    proj = _mm_blocked_rhs("mm_in", h, win_g)
    a_out, o_pre = _hgrn_fwd(proj, lb_logits, hg_norm_w)
    qn, kn, vb = _qk_prep(proj, q_norm_w, k_norm_w)
    att_o, lse = _attn_fwd(qn, kn, vb)
    mixin = jnp.concatenate([a_out, att_o.astype(BF16)], axis=1)
    mix = _mm_plain("mm_out", mixin, wout_full, NN, 512, 1024, F32)
    x1, h2, rstd2 = _norm_fwd("norm2_fwd", xs, norm2_w, scale2, shift2, resid=mix, gate=gate1)
    u = _mm_blocked_rhs("mm_up", h2, wup_g)
    y = _conv_gate_fwd(u, conv_w_full, conv_b)
    ffn = _mm_plain("mm_down", y, wdown_full, NN, 512, 512, F32)
    loss_v, dout, dffn, dgate2 = _loss_head(x1, ffn, gate2, tgt)

    dy = _mm_plain("mm_down_dx", dffn, wdown_full, NT, 512, UP_BLK, F32)
    gw_down = _mm_plain("mm_down_dw", y, dffn, TN, UP_BLK, 1024, BF16)
    da, dg, gconv_w, gconv_b = _conv_gate_bwd(u, dy, conv_w_full, conv_b)
    du = jnp.concatenate([da, dg], axis=1)
    dh2 = _mm_blocked_rhs_t("mm_up_dx", du, wup_g)
    gw_up = _mm_wgrad_blocked("mm_up_dw", h2, du)
    dx1, dmix, dshift2, dscale2, gnorm2, dgate1 = _norm_bwd(
        "norm2_bwd", dh2, x1, rstd2, norm2_w, scale2, dout, mix=mix, gate=gate1)
    dmixin = _mm_plain("mm_out_dx", dmix, wout_full, NT, 512, 1024, F32)
    gw_out = _mm_plain("mm_out_dw", mixin, dmix, TN, 512, 1024, BF16)
    dhq, dhf, dhi, dhg, glog, ghg = _hgrn_bwd(proj, lb_logits, hg_norm_w, o_pre, dmixin)
    dqn, dkn, dvv = _attn_bwd(qn, kn, vb, att_o, lse, dmixin)
    daq, dak, dav, gqw, gkw = _qk_bwd(proj, q_norm_w, k_norm_w, dqn, dkn, dvv)
    dproj = jnp.concatenate([dhq, dhf, dhi, dhg, daq, dak, dav], axis=1)
    dh = _mm_blocked_rhs_t("mm_in_dx", dproj, win_g)
    gw_in = _mm_wgrad_blocked("mm_in_dw", h, dproj)
    grad_x, dshift1, dscale1, gnorm1 = _norm_bwd("norm1_bwd", dh, xs, rstd1, norm1_w, scale1, dx1)
    gmod = jnp.concatenate([dshift1, dscale1, dgate1, dshift2, dscale2, dgate2], axis=1)
    return (loss_v, grad_x, gmod, gnorm1, gnorm2, glog, ghg, gqw, gkw, gconv_b, gconv_w,
            gw_in, gw_out, gw_up, gw_down)


def kernel(x, c, w_ada, b_ada, norm1_w, w_in, lb_logits, hg_norm_w, q_norm_w, k_norm_w, w_out, norm2_w, w_up, conv_w, conv_b, w_down, loss_target, m_w_ada, m_b_ada, m_norm1_w, m_w_in, m_lb_logits, m_hg_norm_w, m_q_norm_w, m_k_norm_w, m_w_out, m_norm2_w, m_w_up, m_conv_w, m_conv_b, m_w_down, v_w_ada, v_b_ada, v_norm1_w, v_w_in, v_lb_logits, v_hg_norm_w, v_q_norm_w, v_k_norm_w, v_w_out, v_norm2_w, v_w_up, v_conv_w, v_conv_b, v_w_down):
    ix, iy, ic = lax.axis_index("x"), lax.axis_index("y"), lax.axis_index("c")
    me = 4 * ix + 2 * iy + ic
    my_chip = 2 * ix + iy

    xs = x[0]
    tgt = loss_target[0]

    win_g, wout_g, wup_g, wdown_g = _allgather_weights(
        [w_in[0].astype(BF16), w_out[0].astype(BF16), w_up[0].astype(BF16), w_down[0].astype(BF16)])
    wout_full = wout_g.reshape(D_MODEL, D_MODEL)
    wdown_full = wdown_g.reshape(D_FF, D_MODEL)

    c_all = _allgather_vmem(c.reshape(8, D_MODEL // 8), "allgather_c").reshape(N_DEV, D_MODEL)
    b_blk = lax.dynamic_slice_in_dim(b_ada, me * ADA_BLK, ADA_BLK, axis=1)
    mod_cols = _ada_fwd(c_all, w_ada[0], b_blk)
    mod_all = _allgather_vmem(mod_cols, "allgather_mod").reshape(N_DEV, N_DEV, ADA_BLK)
    mod = lax.dynamic_index_in_dim(mod_all, me, axis=1, keepdims=False).reshape(6, 1, D_MODEL)

    conv_w_all = _allgather_vmem(_pack_rows([conv_w[0]], 24), "allgather_conv_w").reshape(N_DEV, 24 * 128)
    conv_w_full = conv_w_all[:, :3 * FF_BLK].reshape(N_DEV, 3, FF_BLK).transpose(1, 0, 2).reshape(3, D_FF)

    (loss_v, grad_x, gmod, gnorm1, gnorm2, glog, ghg, gqw, gkw, gconv_b, gconv_w,
     gw_in, gw_out, gw_up, gw_down) = _device_step(
        xs, tgt, mod, norm1_w, norm2_w, lb_logits, hg_norm_w, q_norm_w, k_norm_w, conv_w_full, conv_b,
        win_g, wout_full, wup_g, wdown_full)
    loss = lax.psum(loss_v[0, 0], AXES)

    small_shapes = [(1, 6 * D_MODEL), (1, D_MODEL), (1, D_MODEL), (2, HEADS * HEAD_DIM), (1, HEAD_DIM),
                    (1, HEAD_DIM), (1, HEAD_DIM), (1, D_FF), (3, D_FF)]
    small = [gmod, gnorm1, gnorm2, glog, ghg, gqw, gkw, gconv_b, gconv_w]
    n_small = sum(a.size for a in small)
    rows = -(-n_small // 1024) * 8
    gathered = _allgather_vmem(_pack_rows(small, rows), "allgather_small").reshape(N_DEV, rows, 128)
    summed = _sum_devices(gathered).reshape(-1)
    (g_b_ada, g_norm1, g_norm2, g_lb, g_hg, g_q, g_k, g_conv_b, g_conv_w_full) = _unpack(summed, small_shapes)
    g_conv_w = lax.dynamic_slice_in_dim(g_conv_w_full, me * FF_BLK, FF_BLK, axis=1)

    gmod_all = gathered[:, :6 * D_MODEL // 128, :].reshape(N_DEV, 6 * D_MODEL)
    gmod_cols = lax.dynamic_slice_in_dim(gmod_all, me * ADA_BLK, ADA_BLK, axis=1)
    g_w_ada_raw = _ada_wgrad(c_all, gmod_cols)

    partials = [gw_in, gw_out.reshape(N_DEV, OUT_BLK, D_MODEL), gw_up, gw_down.reshape(N_DEV, FF_BLK, D_MODEL)]
    halves = [p.reshape(4, 2, p.shape[1], p.shape[2]) for p in partials]
    mine = [lax.dynamic_index_in_dim(hv, ic, axis=1, keepdims=False) for hv in halves]
    theirs = [lax.dynamic_index_in_dim(hv, 1 - ic, axis=1, keepdims=False) for hv in halves]
    from_sibling = _exchange_sibling(theirs)
    chip_sums = [_pair_sum(f"grad_pair_sum_{k}", a, b) for k, (a, b) in enumerate(zip(mine, from_sibling))]
    from_chips = _exchange_chips(chip_sums)
    own = [lax.dynamic_index_in_dim(cs, my_chip, axis=0, keepdims=False) for cs in chip_sums]

    def big_update(name, w, m, v, own_a, recv, tr):
        return _adamw(name, w[0], m[0], v[0], [own_a, recv[0], recv[1], recv[2]], tr=tr)

    r_in = big_update("adamw_w_in", w_in, m_w_in, v_w_in, own[0], from_chips[0], 256)
    r_out = big_update("adamw_w_out", w_out, m_w_out, v_w_out, own[1], from_chips[1], 128)
    r_up = big_update("adamw_w_up", w_up, m_w_up, v_w_up, own[2], from_chips[2], 256)
    r_down = big_update("adamw_w_down", w_down, m_w_down, v_w_down, own[3], from_chips[3], 176)
    r_ada = _adamw("adamw_w_ada", w_ada[0], m_w_ada[0], v_w_ada[0], [g_w_ada_raw], tr=256)
    r_convw = _adamw("adamw_conv_w", conv_w[0], m_conv_w[0], v_conv_w[0], [g_conv_w])

    rep_shapes = [(1, 6 * D_MODEL), (1, D_MODEL), (1, D_MODEL), (2, HEADS * HEAD_DIM), (1, HEAD_DIM),
                  (1, HEAD_DIM), (1, HEAD_DIM), (1, D_FF)]
    rep_rows = -(-sum(a * b for a, b in rep_shapes) // 1024) * 8
    pack = lambda arrs: _pack_rows(arrs, rep_rows)
    rep = _adamw("adamw_small",
                 pack([b_ada, norm1_w, norm2_w, lb_logits, hg_norm_w, q_norm_w, k_norm_w, conv_b]),
                 pack([m_b_ada, m_norm1_w, m_norm2_w, m_lb_logits, m_hg_norm_w, m_q_norm_w, m_k_norm_w, m_conv_b]),
                 pack([v_b_ada, v_norm1_w, v_norm2_w, v_lb_logits, v_hg_norm_w, v_q_norm_w, v_k_norm_w, v_conv_b]),
                 [pack([g_b_ada, g_norm1, g_norm2, g_lb, g_hg, g_q, g_k, g_conv_b])])
    rep = [_unpack(r.reshape(-1), rep_shapes) for r in rep]

    def big(r):
        return [a[None] for a in r]

    order = {"w_ada": big(r_ada), "b_ada": [r[0] for r in rep], "norm1_w": [r[1] for r in rep],
             "w_in": big(r_in), "lb_logits": [r[3] for r in rep], "hg_norm_w": [r[4] for r in rep],
             "q_norm_w": [r[5] for r in rep], "k_norm_w": [r[6] for r in rep], "w_out": big(r_out),
             "norm2_w": [r[2] for r in rep], "w_up": big(r_up), "conv_w": big(r_convw),
             "conv_b": [r[7] for r in rep], "w_down": big(r_down)}
    names = ["w_ada", "b_ada", "norm1_w", "w_in", "lb_logits", "hg_norm_w", "q_norm_w", "k_norm_w", "w_out",
             "norm2_w", "w_up", "conv_w", "conv_b", "w_down"]
    outs = [loss, grad_x[None]]
    for kind in range(4):
        outs += [order[n][kind] for n in names]
    return tuple(outs)
```

```python
import functools

import jax
import jax.numpy as jnp
from jax import lax
from jax.experimental import pallas as pl
from jax.experimental.pallas import tpu as pltpu

F32 = jnp.float32
BF16 = jnp.bfloat16

N_DEV = 8
SEQ = 2048
D_MODEL = 2048
HEADS = 8
HEAD_DIM = 128
IN_COLS = 7168
IN_BLK = IN_COLS // N_DEV
D_FF = 5632
UP_BLK = 2 * D_FF // N_DEV
FF_BLK = D_FF // N_DEV
ADA_BLK = 6 * D_MODEL // N_DEV
OUT_BLK = D_MODEL // N_DEV
EPS = 1e-6
CHUNK = 16
ROW_TILE = 256
V7X_VMEM_LIMIT = 56 * 1024 * 1024

ADAM_LR = 0.001
ADAM_B1 = 0.9
ADAM_B2 = 0.999
ADAM_EPS = 1e-08
ADAM_WD = 0.01
ADAM_STEP = 10

NN = (((1,), (0,)), ((), ()))
NT = (((1,), (1,)), ((), ()))
TN = (((0,), (0,)), ((), ()))
MESH = pl.DeviceIdType.MESH
AXES = ("x", "y", "c")


def _params(sem=None, vmem=V7X_VMEM_LIMIT):
    return pltpu.CompilerParams(dimension_semantics=sem, vmem_limit_bytes=vmem)


def _sigmoid(x):
    return 1.0 / (1.0 + jnp.exp(-x))


def _dsilu(x, s):
    return s * (1.0 + x * (1.0 - s))


def _lane_sum(x, ones_bf16):
    hi = x.astype(BF16)
    lo = (x - hi.astype(F32)).astype(BF16)
    return (jnp.dot(hi, ones_bf16, preferred_element_type=F32)
            + jnp.dot(lo, ones_bf16, preferred_element_type=F32))


def _mesh_pos():
    return lax.axis_index("x"), lax.axis_index("y"), lax.axis_index("c")


def _allgather_vmem(x_blk, name):
    m_per, n = x_blk.shape

    def body(x_ref, out_ref, send_sems, recv_sems, local_sem):
        x, y, c = _mesh_pos()
        me, sibling = (x, y, c), (x, y, 1 - c)
        chips = [(1 - x, y), (x, 1 - y), (1 - x, 1 - y)]

        def rows(px, py, pc):
            return out_ref.at[pl.ds((4 * px + 2 * py + pc) * m_per, m_per), :]

        def copy(k, block, to, src=None):
            return pltpu.make_async_remote_copy(
                src_ref=rows(*block) if src is None else src, dst_ref=rows(*block),
                send_sem=send_sems.at[k], recv_sem=recv_sems.at[k], device_id=to, device_id_type=MESH)

        mine = pltpu.make_async_copy(x_ref, rows(*me), local_sem)
        mine.start()
        first = [copy(0, me, sibling, src=x_ref)]
        first += [copy(1 + j, me, (*chip, c), src=x_ref) for j, chip in enumerate(chips)]
        for cp in first:
            cp.start()
        passed = [copy(4 + j, (*chip, c), sibling) for j, chip in enumerate(chips)]
        for j, chip in enumerate(chips):
            copy(1 + j, (*chip, c), me).wait_recv()
            passed[j].start()
        copy(0, sibling, me).wait_recv()
        for j, chip in enumerate(chips):
            copy(4 + j, (*chip, 1 - c), me).wait_recv()
        for cp in first + passed:
            cp.wait_send()
        mine.wait()

    return pl.pallas_call(
        body, name=name,
        out_shape=jax.ShapeDtypeStruct((N_DEV * m_per, n), x_blk.dtype),
        in_specs=[pl.BlockSpec(memory_space=pltpu.VMEM)],
        out_specs=pl.BlockSpec(memory_space=pltpu.VMEM),
        scratch_shapes=[pltpu.SemaphoreType.DMA((7,)), pltpu.SemaphoreType.DMA((7,)), pltpu.SemaphoreType.DMA],
    )(x_blk)


def _allgather_weights(blocks):
    n_arr = len(blocks)

    def body(*refs):
        ins, outs = refs[:n_arr], refs[n_arr:2 * n_arr]
        send_sems, recv_sems, local_sems = refs[2 * n_arr:]
        x, y, c = _mesh_pos()
        me, sibling = (x, y, c), (x, y, 1 - c)
        chips = [(1 - x, y), (x, 1 - y), (1 - x, 1 - y)]

        def slot(a, px, py, pc):
            return outs[a].at[4 * px + 2 * py + pc]

        def copy(a, k, block, to, src=None):
            return pltpu.make_async_remote_copy(
                src_ref=slot(a, *block) if src is None else src, dst_ref=slot(a, *block),
                send_sem=send_sems.at[a, k], recv_sem=recv_sems.at[a, k], device_id=to, device_id_type=MESH)

        mine, first, passed = [], [], []
        for a in range(n_arr):
            cp = pltpu.make_async_copy(ins[a], slot(a, *me), local_sems.at[a])
            cp.start()
            mine.append(cp)
            first.append(copy(a, 0, me, sibling, src=ins[a]))
            first += [copy(a, 1 + j, me, (*chip, c), src=ins[a]) for j, chip in enumerate(chips)]
        for cp in first:
            cp.start()
        for j, chip in enumerate(chips):
            for a in range(n_arr):
                copy(a, 1 + j, (*chip, c), me).wait_recv()
                cp = copy(a, 4 + j, (*chip, c), sibling)
                cp.start()
                passed.append(cp)
        for a in range(n_arr):
            copy(a, 0, sibling, me).wait_recv()
            for j, chip in enumerate(chips):
                copy(a, 4 + j, (*chip, 1 - c), me).wait_recv()
        for cp in first + passed:
            cp.wait_send()
        for cp in mine:
            cp.wait()

    hbm = pl.BlockSpec(memory_space=pltpu.HBM)
    return pl.pallas_call(
        body, name="allgather_weights",
        out_shape=[jax.ShapeDtypeStruct((N_DEV,) + b.shape, b.dtype) for b in blocks],
        in_specs=[hbm] * n_arr, out_specs=[hbm] * n_arr,
        scratch_shapes=[pltpu.SemaphoreType.DMA((n_arr, 7)), pltpu.SemaphoreType.DMA((n_arr, 7)),
                        pltpu.SemaphoreType.DMA((n_arr,))],
    )(*blocks)


def _exchange_sibling(parts):
    n_arr = len(parts)

    def body(*refs):
        ins, outs = refs[:n_arr], refs[n_arr:2 * n_arr]
        send_sems, recv_sems = refs[2 * n_arr:]
        x, y, c = _mesh_pos()
        copies = [pltpu.make_async_remote_copy(
            src_ref=ins[a], dst_ref=outs[a], send_sem=send_sems.at[a], recv_sem=recv_sems.at[a],
            device_id=(x, y, 1 - c), device_id_type=MESH) for a in range(n_arr)]
        for cp in copies:
            cp.start()
        for cp in copies:
            cp.wait_recv()
        for cp in copies:
            cp.wait_send()

    hbm = pl.BlockSpec(memory_space=pltpu.HBM)
    return pl.pallas_call(
        body, name="grad_exchange_sibling",
        out_shape=[jax.ShapeDtypeStruct(p.shape, p.dtype) for p in parts],
        in_specs=[hbm] * n_arr, out_specs=[hbm] * n_arr,
        scratch_shapes=[pltpu.SemaphoreType.DMA((n_arr,)), pltpu.SemaphoreType.DMA((n_arr,))],
    )(*parts)


def _exchange_chips(chip_sums):
    n_arr = len(chip_sums)

    def body(*refs):
        ins, outs = refs[:n_arr], refs[n_arr:2 * n_arr]
        send_sems, recv_sems = refs[2 * n_arr:]
        x, y, c = _mesh_pos()
        chips = [(1 - x, y), (x, 1 - y), (1 - x, 1 - y)]
        copies = []
        for a in range(n_arr):
            for j, (px, py) in enumerate(chips):
                copies.append(pltpu.make_async_remote_copy(
                    src_ref=ins[a].at[2 * px + py], dst_ref=outs[a].at[j],
                    send_sem=send_sems.at[a, j], recv_sem=recv_sems.at[a, j],
                    device_id=(px, py, c), device_id_type=MESH))
        for cp in copies:
            cp.start()
        for cp in copies:
            cp.wait_recv()
        for cp in copies:
            cp.wait_send()

    hbm = pl.BlockSpec(memory_space=pltpu.HBM)
    return pl.pallas_call(
        body, name="grad_exchange_chips",
        out_shape=[jax.ShapeDtypeStruct((3,) + p.shape[1:], p.dtype) for p in chip_sums],
        in_specs=[hbm] * n_arr, out_specs=[hbm] * n_arr,
        scratch_shapes=[pltpu.SemaphoreType.DMA((n_arr, 3)), pltpu.SemaphoreType.DMA((n_arr, 3))],
    )(*chip_sums)


def _matmul(name, a, b, dims, grid, a_spec, b_spec, o_spec, out_shape, acc_axis=None):
    def body(a_ref, b_ref, o_ref):
        r = lax.dot_general(a_ref[...], b_ref[...], dims, preferred_element_type=F32)
        if acc_axis is None:
            o_ref[...] = r.astype(o_ref.dtype)
        else:
            k = pl.program_id(acc_axis)

            @pl.when(k == 0)
            def _():
                o_ref[...] = r

            @pl.when(k > 0)
            def _():
                o_ref[...] += r

    sem = tuple("arbitrary" if i == acc_axis else "parallel" for i in range(len(grid)))
    return pl.pallas_call(body, name=name, grid=grid, in_specs=[a_spec, b_spec], out_specs=o_spec,
                          out_shape=out_shape, compiler_params=_params(sem))(a, b)


def _mm_blocked_rhs(name, a, w_g, tm=512):
    m, k = a.shape
    nb = w_g.shape[2]
    return _matmul(name, a, w_g, NN, (N_DEV, m // tm),
                   pl.BlockSpec((tm, k), lambda j, i: (i, 0)),
                   pl.BlockSpec((None, k, nb), lambda j, i: (j, 0, 0)),
                   pl.BlockSpec((tm, nb), lambda j, i: (i, j)),
                   jax.ShapeDtypeStruct((m, N_DEV * nb), F32))


def _mm_blocked_rhs_t(name, a, w_g, tm=512):
    m = a.shape[0]
    n, nb = w_g.shape[1], w_g.shape[2]
    return _matmul(name, a, w_g, NT, (m // tm, N_DEV),
                   pl.BlockSpec((tm, nb), lambda i, j: (i, j)),
                   pl.BlockSpec((None, n, nb), lambda i, j: (j, 0, 0)),
                   pl.BlockSpec((tm, n), lambda i, j: (i, 0)),
                   jax.ShapeDtypeStruct((m, n), F32), acc_axis=1)


def _mm_wgrad_blocked(name, act, dcols, tk=512):
    t, k = act.shape
    nb = dcols.shape[1] // N_DEV
    return _matmul(name, act, dcols, TN, (N_DEV, k // tk),
                   pl.BlockSpec((t, tk), lambda j, i: (0, i)),
                   pl.BlockSpec((t, nb), lambda j, i: (0, j)),
                   pl.BlockSpec((None, tk, nb), lambda j, i: (j, i, 0)),
                   jax.ShapeDtypeStruct((N_DEV, k, nb), BF16))


def _mm_plain(name, a, b, dims, tm, tn, out_dtype):
    if dims == NN:
        (m, k), n = a.shape, b.shape[1]
        a_spec = pl.BlockSpec((tm, k), lambda i, j: (i, 0))
        b_spec = pl.BlockSpec((k, tn), lambda i, j: (0, j))
    elif dims == NT:
        (m, k), n = a.shape, b.shape[0]
        a_spec = pl.BlockSpec((tm, k), lambda i, j: (i, 0))
        b_spec = pl.BlockSpec((tn, k), lambda i, j: (j, 0))
    else:
        (k, m), n = a.shape, b.shape[1]
        a_spec = pl.BlockSpec((k, tm), lambda i, j: (0, i))
        b_spec = pl.BlockSpec((k, tn), lambda i, j: (0, j))
    return _matmul(name, a, b, dims, (m // tm, n // tn), a_spec, b_spec,
                   pl.BlockSpec((tm, tn), lambda i, j: (i, j)), jax.ShapeDtypeStruct((m, n), out_dtype))


def _ada_fwd(c_all, w_ada_blk, b_blk):
    def body(c_ref, w_ref, b_ref, o_ref):
        cv = c_ref[...]
        o_ref[...] = jnp.dot(cv * _sigmoid(cv), w_ref[...], preferred_element_type=F32) + b_ref[...]

    tn = 512
    return pl.pallas_call(
        body, name="ada_fwd", grid=(ADA_BLK // tn,),
        in_specs=[pl.BlockSpec((N_DEV, D_MODEL), lambda j: (0, 0)),
                  pl.BlockSpec((D_MODEL, tn), lambda j: (0, j)),
                  pl.BlockSpec((1, tn), lambda j: (0, j))],
        out_specs=pl.BlockSpec((N_DEV, tn), lambda j: (0, j)),
        out_shape=jax.ShapeDtypeStruct((N_DEV, ADA_BLK), F32),
        compiler_params=_params(("parallel",)))(c_all, w_ada_blk, b_blk)


def _ada_wgrad(c_all, gmod_cols):
    def body(c_ref, g_ref, o_ref):
        cv = c_ref[...]
        o_ref[...] = lax.dot_general(cv * _sigmoid(cv), g_ref[...], TN, preferred_element_type=F32)

    tk = 512
    return pl.pallas_call(
        body, name="ada_wgrad", grid=(D_MODEL // tk,),
        in_specs=[pl.BlockSpec((N_DEV, tk), lambda i: (0, i)),
                  pl.BlockSpec((N_DEV, ADA_BLK), lambda i: (0, 0))],
        out_specs=pl.BlockSpec((tk, ADA_BLK), lambda i: (i, 0)),
        out_shape=jax.ShapeDtypeStruct((D_MODEL, ADA_BLK), F32),
        compiler_params=_params(("parallel",)))(c_all, gmod_cols)


def _row_spec(cols=D_MODEL):
    return pl.BlockSpec((ROW_TILE, cols), lambda i: (i, 0))


def _vec_spec(cols=D_MODEL):
    return pl.BlockSpec((1, cols), lambda i: (0, 0))


def _norm_fwd(name, x, w, scale, shift, resid=None, gate=None):
    has_res = resid is not None

    def body(*refs):
        if has_res:
            x_ref, r_ref, g_ref, w_ref, sc_ref, sh_ref, xr_ref, h_ref, rs_ref = refs
            xr = x_ref[...] + g_ref[...] * r_ref[...]
            xr_ref[...] = xr
        else:
            x_ref, w_ref, sc_ref, sh_ref, h_ref, rs_ref = refs
            xr = x_ref[...]
        rs = lax.rsqrt(jnp.mean(xr * xr, axis=-1, keepdims=True) + EPS)
        h = (xr * rs) * w_ref[...] * (1.0 + sc_ref[...]) + sh_ref[...]
        h_ref[...] = h.astype(BF16)
        rs_ref[...] = rs

    s = x.shape[0]
    ins = [x] + ([resid, gate] if has_res else []) + [w, scale, shift]
    in_specs = [_row_spec()] + ([_row_spec(), _vec_spec()] if has_res else []) + [_vec_spec()] * 3
    outs = ([jax.ShapeDtypeStruct((s, D_MODEL), F32)] if has_res else []) + [
        jax.ShapeDtypeStruct((s, D_MODEL), BF16), jax.ShapeDtypeStruct((s, 1), F32)]
    out_specs = ([_row_spec()] if has_res else []) + [_row_spec(), pl.BlockSpec((ROW_TILE, 1), lambda i: (i, 0))]
    return pl.pallas_call(body, name=name, grid=(s // ROW_TILE,), in_specs=in_specs, out_specs=out_specs,
                          out_shape=outs, compiler_params=_params(("parallel",)))(*ins)


def _norm_bwd(name, dh, x, rstd, w, scale, dres, mix=None, gate=None):
    has_mix = mix is not None

    def body(*refs):
        if has_mix:
            (dh_ref, x_ref, rs_ref, w_ref, sc_ref, dr_ref, mix_ref, g_ref,
             dx_ref, dmix_ref, dsh_ref, dsc_ref, dw_ref, dg_ref) = refs
        else:
            dh_ref, x_ref, rs_ref, w_ref, sc_ref, dr_ref, dx_ref, dsh_ref, dsc_ref, dw_ref = refs
        i = pl.program_id(0)
        dhv = dh_ref[...]
        rs = rs_ref[...]
        xn = x_ref[...] * rs
        wv = w_ref[...]
        one_sc = 1.0 + sc_ref[...]
        dxn = dhv * wv * one_sc
        dx = dr_ref[...] + rs * (dxn - xn * jnp.mean(dxn * xn, axis=-1, keepdims=True))
        dx_ref[...] = dx
        sums = [(dsh_ref, dhv), (dsc_ref, dhv * xn * wv), (dw_ref, dhv * one_sc * xn)]
        if has_mix:
            dmix_ref[...] = (dx * g_ref[...]).astype(BF16)
            sums.append((dg_ref, dx * mix_ref[...]))

        @pl.when(i == 0)
        def _():
            for ref, _v in sums:
                ref[...] = jnp.zeros_like(ref)

        for ref, v in sums:
            ref[...] += jnp.sum(v, axis=0, keepdims=True)

    s = x.shape[0]
    ins = [dh, x, rstd, w, scale, dres] + ([mix, gate] if has_mix else [])
    in_specs = ([_row_spec(), _row_spec(), pl.BlockSpec((ROW_TILE, 1), lambda i: (i, 0)), _vec_spec(), _vec_spec(),
                 _row_spec()] + ([_row_spec(), _vec_spec()] if has_mix else []))
    vec = jax.ShapeDtypeStruct((1, D_MODEL), F32)
    outs = ([jax.ShapeDtypeStruct((s, D_MODEL), F32)] + ([jax.ShapeDtypeStruct((s, D_MODEL), BF16)] if has_mix else [])
            + [vec] * (4 if has_mix else 3))
    out_specs = [_row_spec()] + ([_row_spec()] if has_mix else []) + [_vec_spec()] * (4 if has_mix else 3)
    return pl.pallas_call(body, name=name, grid=(s // ROW_TILE,), in_specs=in_specs, out_specs=out_specs,
                          out_shape=outs, compiler_params=_params(("arbitrary",)))(*ins)


def _loss_head(x1, ffn, gate2, target):
    def body(x_ref, f_ref, g_ref, t_ref, loss_ref, dout_ref, dffn_ref, dg_ref):
        i = pl.program_id(0)
        fv = f_ref[...]
        gv = g_ref[...]
        err = x_ref[...] + gv * fv - t_ref[...]
        dout = err * (1.0 / D_MODEL)
        dout_ref[...] = dout
        dffn_ref[...] = (dout * gv).astype(BF16)

        @pl.when(i == 0)
        def _():
            loss_ref[...] = jnp.zeros_like(loss_ref)
            dg_ref[...] = jnp.zeros_like(dg_ref)

        row = jnp.sum(err * err, axis=-1, keepdims=True) * (1.0 / D_MODEL)
        loss_ref[...] += jnp.broadcast_to(0.5 * jnp.sum(row, axis=0, keepdims=True), (1, 128))
        dg_ref[...] += jnp.sum(dout * fv, axis=0, keepdims=True)

    s = x1.shape[0]
    return pl.pallas_call(
        body, name="loss_head", grid=(s // ROW_TILE,),
        in_specs=[_row_spec(), _row_spec(), _vec_spec(), _row_spec()],
        out_specs=[pl.BlockSpec((1, 128), lambda i: (0, 0)), _row_spec(), _row_spec(), _vec_spec()],
        out_shape=[jax.ShapeDtypeStruct((1, 128), F32), jax.ShapeDtypeStruct((s, D_MODEL), F32),
                   jax.ShapeDtypeStruct((s, D_MODEL), BF16), jax.ShapeDtypeStruct((1, D_MODEL), F32)],
        compiler_params=_params(("arbitrary",)))(x1, ffn, gate2, target)


CONV_TILE = 512
N_CONV_TILES = D_FF // CONV_TILE


def _shift_rows(a, k, row):
    n = a.shape[0]
    if k > 0:
        return jnp.where(row >= k, pltpu.roll(a, k, 0), 0.0)
    return jnp.where(row < n + k, pltpu.roll(a, n + k, 0), 0.0)


def _conv_gate_fwd(u, conv_w, conv_b):
    s = u.shape[0]

    def body(a_ref, g_ref, w_ref, b_ref, y_ref):
        a = a_ref[...]
        w = w_ref[...]
        row = lax.broadcasted_iota(jnp.int32, a.shape, 0)
        ac = b_ref[...] + _shift_rows(a, 2, row) * w[0:1] + _shift_rows(a, 1, row) * w[1:2] + a * w[2:3]
        y_ref[...] = (ac * _sigmoid(ac) * g_ref[...]).astype(BF16)

    col = lambda off: pl.BlockSpec((s, CONV_TILE), lambda i: (0, i + off))
    return pl.pallas_call(
        body, name="conv_gate_fwd", grid=(N_CONV_TILES,),
        in_specs=[col(0), col(N_CONV_TILES), pl.BlockSpec((3, CONV_TILE), lambda i: (0, i)),
                  pl.BlockSpec((1, CONV_TILE), lambda i: (0, i))],
        out_specs=col(0), out_shape=jax.ShapeDtypeStruct((s, D_FF), BF16),
        compiler_params=_params(("parallel",)))(u, u, conv_w, conv_b)


def _conv_gate_bwd(u, dy, conv_w, conv_b):
    s = u.shape[0]

    def body(a_ref, g_ref, dy_ref, w_ref, b_ref, da_ref, dg_ref, gw_ref, gb_ref):
        a = a_ref[...]
        w = w_ref[...]
        row = lax.broadcasted_iota(jnp.int32, a.shape, 0)
        a1 = _shift_rows(a, 1, row)
        a2 = _shift_rows(a, 2, row)
        ac = b_ref[...] + a2 * w[0:1] + a1 * w[1:2] + a * w[2:3]
        sg = _sigmoid(ac)
        dyv = dy_ref[...]
        dg_ref[...] = (dyv * (ac * sg)).astype(BF16)
        dac = dyv * g_ref[...] * _dsilu(ac, sg)
        gb_ref[...] = jnp.sum(dac, axis=0, keepdims=True)
        gw_ref[0:1, :] = jnp.sum(dac * a2, axis=0, keepdims=True)
        gw_ref[1:2, :] = jnp.sum(dac * a1, axis=0, keepdims=True)
        gw_ref[2:3, :] = jnp.sum(dac * a, axis=0, keepdims=True)
        da = dac * w[2:3] + _shift_rows(dac, -1, row) * w[1:2] + _shift_rows(dac, -2, row) * w[0:1]
        da_ref[...] = da.astype(BF16)

    col = lambda off: pl.BlockSpec((s, CONV_TILE), lambda i: (0, i + off))
    return pl.pallas_call(
        body, name="conv_gate_bwd", grid=(N_CONV_TILES,),
        in_specs=[col(0), col(N_CONV_TILES), col(0), pl.BlockSpec((3, CONV_TILE), lambda i: (0, i)),
                  pl.BlockSpec((1, CONV_TILE), lambda i: (0, i))],
        out_specs=[col(0), col(0), pl.BlockSpec((3, CONV_TILE), lambda i: (0, i)),
                   pl.BlockSpec((1, CONV_TILE), lambda i: (0, i))],
        out_shape=[jax.ShapeDtypeStruct((s, D_FF), BF16), jax.ShapeDtypeStruct((s, D_FF), BF16),
                   jax.ShapeDtypeStruct((3, D_FF), F32), jax.ShapeDtypeStruct((1, D_FF), F32)],
        compiler_params=_params(("parallel",)))(u, u, dy, conv_w, conv_b)


HG_TILE = 128
CHUNK_UNROLL = 8


def _unrolled_loop(n, body, init):
    def group(i, carry):
        for u in range(CHUNK_UNROLL):
            carry = body(i * CHUNK_UNROLL + u, carry)
        return carry

    return lax.fori_loop(0, n // CHUNK_UNROLL, group, init)


def _head_col(off):
    return pl.BlockSpec((SEQ, HEAD_DIM), lambda h: (0, h + off))


def _hgrn_gates(hq, hf, lb, pos):
    q = hq * _sigmoid(hq)
    sig = _sigmoid(hf)
    f = lb + (1.0 - lb) * sig
    gl = jnp.log(f)
    for sh in (1, 2, 4, 8):
        gl = gl + jnp.where(pos >= sh, pltpu.roll(gl, sh, 0), 0.0)
    return q, sig, f, 1.0 - f, gl


def _lower_bound(lbl):
    return 1.0 / (1.0 + jnp.exp(lbl[1:2, :] - lbl[0:1, :]))


def _hgrn_fwd(proj, lb_logits, norm_w):
    n_tiles = SEQ // HG_TILE
    n_chunks = SEQ // CHUNK

    def body(hq_ref, hf_ref, hi_ref, hg_ref, lbl_ref, nw_ref, aout_ref, opre_ref, q_s, k_s, gl_s):
        lb = _lower_bound(lbl_ref[...])
        ones = jnp.ones((HEAD_DIM, HEAD_DIM), BF16)
        pos = lax.broadcasted_iota(jnp.int32, (HG_TILE, HEAD_DIM), 0) % CHUNK

        def tile(i, carry):
            rows = pl.ds(pl.multiple_of(i * HG_TILE, HG_TILE), HG_TILE)
            v = hi_ref[rows, :]
            q, _sig, _f, kk, gl = _hgrn_gates(hq_ref[rows, :], hf_ref[rows, :], lb, pos)
            o = _lane_sum(q * kk, ones) * v
            for d in range(1, CHUNK):
                e = jnp.where(pos >= d, jnp.exp(gl - pltpu.roll(gl, d, 0)), 0.0)
                o = o + _lane_sum(q * pltpu.roll(kk, d, 0) * e, ones) * pltpu.roll(v, d, 0)
            q_s[rows, :] = q
            k_s[rows, :] = kk
            gl_s[rows, :] = gl
            opre_ref[rows, :] = o
            return carry

        lax.fori_loop(0, n_tiles, tile, 0)

        def chunk(c, st):
            rows = pl.ds(pl.multiple_of(c * CHUNK, CHUNK), CHUNK)
            gl = gl_s[rows, :]
            qt = q_s[rows, :] * jnp.exp(gl)
            opre_ref[rows, :] += lax.dot_general(qt.astype(BF16), st.astype(BF16), NT, preferred_element_type=F32)
            gll = gl[CHUNK - 1:CHUNK, :]
            kt = k_s[rows, :] * jnp.exp(gll - gl)
            return st * jnp.exp(gll) + lax.dot_general(hi_ref[rows, :].astype(BF16), kt.astype(BF16), TN,
                                                       preferred_element_type=F32)

        _unrolled_loop(n_chunks, chunk, jnp.zeros((HEAD_DIM, HEAD_DIM), F32))

        def finish(i, carry):
            rows = pl.ds(pl.multiple_of(i * HG_TILE, HG_TILE), HG_TILE)
            o = opre_ref[rows, :]
            hg = hg_ref[rows, :]
            rs = lax.rsqrt(jnp.mean(o * o, axis=-1, keepdims=True) + EPS)
            aout_ref[rows, :] = ((o * rs) * nw_ref[...] * (hg * _sigmoid(hg))).astype(BF16)
            return carry

        lax.fori_loop(0, n_tiles, finish, 0)

    return pl.pallas_call(
        body, name="hgrn_fwd", grid=(HEADS,),
        in_specs=[_head_col(0), _head_col(HEADS), _head_col(2 * HEADS), _head_col(3 * HEADS),
                  pl.BlockSpec((2, HEAD_DIM), lambda h: (0, h)), pl.BlockSpec((1, HEAD_DIM), lambda h: (0, 0))],
        out_specs=[_head_col(0), _head_col(0)],
        out_shape=[jax.ShapeDtypeStruct((SEQ, HEADS * HEAD_DIM), BF16), jax.ShapeDtypeStruct((SEQ, HEADS * HEAD_DIM), F32)],
        scratch_shapes=[pltpu.VMEM((SEQ, HEAD_DIM), F32)] * 3,
        compiler_params=_params(("parallel",)))(proj, proj, proj, proj, lb_logits, norm_w)


def _hgrn_bwd(proj, lb_logits, norm_w, o_pre, d_aout):
    n_tiles = SEQ // HG_TILE
    n_chunks = SEQ // CHUNK

    def body(hq_ref, hf_ref, hi_ref, hg_ref, lbl_ref, nw_ref, opre_ref, da_ref,
             dhq_ref, dhf_ref, dhi_ref, dhg_ref, dlog_ref, gnw_ref,
             q_s, k_s, gl_s, do_s, dq_s, dk_s, dv_s, st_s):
        h = pl.program_id(0)
        lb = _lower_bound(lbl_ref[...])
        nw = nw_ref[...]
        ones = jnp.ones((HEAD_DIM, HEAD_DIM), BF16)
        pos = lax.broadcasted_iota(jnp.int32, (HG_TILE, HEAD_DIM), 0) % CHUNK

        @pl.when(h == 0)
        def _():
            gnw_ref[...] = jnp.zeros_like(gnw_ref)

        def tile(i, carry):
            rows = pl.ds(pl.multiple_of(i * HG_TILE, HG_TILE), HG_TILE)
            v = hi_ref[rows, :]
            q, _sig, _f, kk, gl = _hgrn_gates(hq_ref[rows, :], hf_ref[rows, :], lb, pos)
            o = opre_ref[rows, :]
            hg = hg_ref[rows, :]
            da = da_ref[rows, :]
            rs = lax.rsqrt(jnp.mean(o * o, axis=-1, keepdims=True) + EPS)
            oh = o * rs
            sg = _sigmoid(hg)
            dnorm = da * (hg * sg)
            dhg_ref[rows, :] = (da * (oh * nw) * _dsilu(hg, sg)).astype(BF16)
            gnw_ref[...] += jnp.sum(dnorm * oh, axis=0, keepdims=True)
            doh = dnorm * nw
            do = rs * (doh - oh * jnp.mean(doh * oh, axis=-1, keepdims=True))

            d_a = _lane_sum(do * v, ones)
            dq = d_a * kk
            dk = d_a * q
            dv = _lane_sum(q * kk, ones) * do
            for d in range(1, CHUNK):
                ks = pltpu.roll(kk, d, 0)
                e = jnp.where(pos >= d, jnp.exp(gl - pltpu.roll(gl, d, 0)), 0.0)
                a_d = _lane_sum(q * ks * e, ones)
                d_a = _lane_sum(do * pltpu.roll(v, d, 0), ones) * e
                dq = dq + d_a * ks
                dk = dk + pltpu.roll(d_a * q, HG_TILE - d, 0)
                dv = dv + pltpu.roll(a_d * do, HG_TILE - d, 0)
            q_s[rows, :] = q
            k_s[rows, :] = kk
            gl_s[rows, :] = gl
            do_s[rows, :] = do
            dq_s[rows, :] = dq
            dk_s[rows, :] = dk
            dv_s[rows, :] = dv
            return carry

        lax.fori_loop(0, n_tiles, tile, 0)

        def fwd_chunk(c, st):
            rows = pl.ds(pl.multiple_of(c * CHUNK, CHUNK), CHUNK)
            gl = gl_s[rows, :]
            st_s[c] = st
            dq_s[rows, :] += jnp.dot(do_s[rows, :].astype(BF16), st.astype(BF16),
                                     preferred_element_type=F32) * jnp.exp(gl)
            gll = gl[CHUNK - 1:CHUNK, :]
            kt = k_s[rows, :] * jnp.exp(gll - gl)
            return st * jnp.exp(gll) + lax.dot_general(hi_ref[rows, :].astype(BF16), kt.astype(BF16), TN,
                                                       preferred_element_type=F32)

        _unrolled_loop(n_chunks, fwd_chunk, jnp.zeros((HEAD_DIM, HEAD_DIM), F32))

        pos_c = lax.broadcasted_iota(jnp.int32, (CHUNK, HEAD_DIM), 0)

        def bwd_chunk(i, carry):
            rt, dlb = carry
            c = n_chunks - 1 - i
            rows = pl.ds(pl.multiple_of(c * CHUNK, CHUNK), CHUNK)
            gl = gl_s[rows, :]
            q = q_s[rows, :]
            kk = k_s[rows, :]
            do = do_s[rows, :]
            gll = gl[CHUNK - 1:CHUNK, :]
            egl = jnp.exp(gll)
            ekt = jnp.exp(gll - gl)
            rt_b = rt.astype(BF16)
            dk_in = dk_s[rows, :]
            dk_far = jnp.dot(hi_ref[rows, :].astype(BF16), rt_b, preferred_element_type=F32) * ekt
            dk = dk_in + dk_far
            dv = dv_s[rows, :] + lax.dot_general((kk * ekt).astype(BF16), rt_b, NT, preferred_element_type=F32)
            dq = dq_s[rows, :]
            rc = q * dq - kk * dk_in
            pc = kk * dk_far
            pre = pc
            for sh in (1, 2, 4, 8):
                rc = rc + jnp.where(pos_c < CHUNK - sh, pltpu.roll(rc, CHUNK - sh, 0), 0.0)
                pre = pre + jnp.where(pos_c >= sh, pltpu.roll(pre, sh, 0), 0.0)
            across = jnp.sum(st_s[c] * rt, axis=0, keepdims=True) * egl
            dgl = rc + (pre - pc) + across
            hf = hf_ref[rows, :]
            sig = _sigmoid(hf)
            f = lb + (1.0 - lb) * sig
            df = dgl / f - dk
            dhf_ref[rows, :] = (df * (1.0 - lb) * sig * (1.0 - sig)).astype(BF16)
            hq = hq_ref[rows, :]
            dhq_ref[rows, :] = (dq * _dsilu(hq, _sigmoid(hq))).astype(BF16)
            dhi_ref[rows, :] = dv.astype(BF16)
            rt_new = rt * egl + lax.dot_general(do.astype(BF16), (q * jnp.exp(gl)).astype(BF16), TN,
                                                preferred_element_type=F32)
            return (rt_new, dlb + jnp.sum(df * (1.0 - sig), axis=0, keepdims=True))

        _, dlb = _unrolled_loop(n_chunks, bwd_chunk,
                                (jnp.zeros((HEAD_DIM, HEAD_DIM), F32), jnp.zeros((1, HEAD_DIM), F32)))
        dl0 = lb * (1.0 - lb) * dlb
        dlog_ref[0:1, :] = dl0
        dlog_ref[1:2, :] = -dl0

    wide = HEADS * HEAD_DIM
    return pl.pallas_call(
        body, name="hgrn_bwd", grid=(HEADS,),
        in_specs=[_head_col(0), _head_col(HEADS), _head_col(2 * HEADS), _head_col(3 * HEADS),
                  pl.BlockSpec((2, HEAD_DIM), lambda h: (0, h)), pl.BlockSpec((1, HEAD_DIM), lambda h: (0, 0)),
                  _head_col(0), _head_col(0)],
        out_specs=[_head_col(0)] * 4 + [pl.BlockSpec((2, HEAD_DIM), lambda h: (0, h)),
                                        pl.BlockSpec((1, HEAD_DIM), lambda h: (0, 0))],
        out_shape=[jax.ShapeDtypeStruct((SEQ, wide), BF16)] * 4 + [jax.ShapeDtypeStruct((2, wide), F32),
                                                                    jax.ShapeDtypeStruct((1, HEAD_DIM), F32)],
        scratch_shapes=[pltpu.VMEM((SEQ, HEAD_DIM), F32)] * 7 + [pltpu.VMEM((n_chunks, HEAD_DIM, HEAD_DIM), F32)],
        compiler_params=_params(("arbitrary",)))(proj, proj, proj, proj, lb_logits, norm_w, o_pre, d_aout)


Q_TILE = 256
ATT_SCALE = HEAD_DIM ** -0.5
ATT_OFF = 4 * HEADS


def _qk_prep(proj, q_w, k_w):
    def body(aq_ref, ak_ref, av_ref, qw_ref, kw_ref, qn_ref, kn_ref, v_ref):
        aq = aq_ref[...]
        ak = ak_ref[...]
        qn_ref[...] = (aq * lax.rsqrt(jnp.mean(aq * aq, axis=-1, keepdims=True) + EPS) * qw_ref[...]).astype(BF16)
        kn_ref[...] = (ak * lax.rsqrt(jnp.mean(ak * ak, axis=-1, keepdims=True) + EPS) * kw_ref[...]).astype(BF16)
        v_ref[...] = av_ref[...].astype(BF16)

    wide = HEADS * HEAD_DIM
    vec = pl.BlockSpec((1, HEAD_DIM), lambda h: (0, 0))
    return pl.pallas_call(
        body, name="qk_prep", grid=(HEADS,),
        in_specs=[_head_col(ATT_OFF), _head_col(ATT_OFF + HEADS), _head_col(ATT_OFF + 2 * HEADS), vec, vec],
        out_specs=[_head_col(0)] * 3, out_shape=[jax.ShapeDtypeStruct((SEQ, wide), BF16)] * 3,
        compiler_params=_params(("parallel",)))(proj, proj, proj, q_w, k_w)


def _alibi_slopes():
    slopes = jnp.exp2(-8.0 * jnp.arange(1, HEADS + 1, dtype=F32) / HEADS)
    return jnp.broadcast_to(slopes[:, None, None], (HEADS, 1, HEAD_DIM))


SLOPE_SPEC = pl.BlockSpec((None, 1, HEAD_DIM), lambda h, i: (h, 0, 0))


def _att_scores(q, k, slope, i):
    s = lax.dot_general(q, k, NT, preferred_element_type=F32) * ATT_SCALE
    row = lax.broadcasted_iota(jnp.int32, s.shape, 0) + i * Q_TILE
    col = lax.broadcasted_iota(jnp.int32, s.shape, 1)
    dist = row - col
    mult = ((dist <= 128).astype(F32) + (((dist & 3) == 0) & (dist <= 512)).astype(F32)
            + ((dist & 15) == 0).astype(F32))
    mult = jnp.where(dist >= 0, mult, 0.0)
    return s - slope * dist.astype(F32), mult


def _attn_fwd(qn, kn, vb):
    def body(q_ref, k_ref, v_ref, sl_ref, o_ref, lse_ref):
        i = pl.program_id(1)
        sb, mult = _att_scores(q_ref[...], k_ref[...], sl_ref[0:1, 0:1], i)
        sm = jnp.where(mult > 0.0, sb, -1e30)
        m = jnp.max(sm, axis=-1, keepdims=True)
        p = jnp.exp(sm - m) * mult
        l = jnp.sum(p, axis=-1, keepdims=True)
        o_ref[...] = jnp.dot(p.astype(BF16), v_ref[...], preferred_element_type=F32) / l
        lse_ref[...] = m + jnp.log(l)

    wide = HEADS * HEAD_DIM
    qt = pl.BlockSpec((Q_TILE, HEAD_DIM), lambda h, i: (i, h))
    full = pl.BlockSpec((SEQ, HEAD_DIM), lambda h, i: (0, h))
    return pl.pallas_call(
        body, name="attn_fwd", grid=(HEADS, SEQ // Q_TILE),
        in_specs=[qt, full, full, SLOPE_SPEC],
        out_specs=[qt, pl.BlockSpec((None, Q_TILE, 1), lambda h, i: (h, i, 0))],
        out_shape=[jax.ShapeDtypeStruct((SEQ, wide), F32), jax.ShapeDtypeStruct((HEADS, SEQ, 1), F32)],
        compiler_params=_params(("parallel", "parallel")))(qn, kn, vb, _alibi_slopes())


def _attn_bwd(qn, kn, vb, o, lse, d_mix):
    def body(q_ref, k_ref, v_ref, o_ref, lse_ref, do_ref, sl_ref, dq_ref, dk_ref, dv_ref):
        i = pl.program_id(1)
        q = q_ref[...]
        k = k_ref[...]
        do = do_ref[...]
        sb, mult = _att_scores(q, k, sl_ref[0:1, 0:1], i)
        p = jnp.where(mult > 0.0, jnp.exp(sb - lse_ref[...]), 0.0) * mult
        dp = lax.dot_general(do.astype(BF16), v_ref[...], NT, preferred_element_type=F32)
        delta = jnp.sum(do * o_ref[...], axis=-1, keepdims=True)
        ds = (p * (dp - delta)).astype(BF16)
        dq_ref[...] = jnp.dot(ds, k, preferred_element_type=F32) * ATT_SCALE

        @pl.when(i == 0)
        def _():
            dk_ref[...] = jnp.zeros_like(dk_ref)
            dv_ref[...] = jnp.zeros_like(dv_ref)

        dk_ref[...] += lax.dot_general(ds, q, TN, preferred_element_type=F32) * ATT_SCALE
        dv_ref[...] += lax.dot_general(p.astype(BF16), do.astype(BF16), TN, preferred_element_type=F32)

    wide = HEADS * HEAD_DIM
    qt = pl.BlockSpec((Q_TILE, HEAD_DIM), lambda h, i: (i, h))
    full = pl.BlockSpec((SEQ, HEAD_DIM), lambda h, i: (0, h))
    return pl.pallas_call(
        body, name="attn_bwd", grid=(HEADS, SEQ // Q_TILE),
        in_specs=[qt, full, full, qt, pl.BlockSpec((None, Q_TILE, 1), lambda h, i: (h, i, 0)),
                  pl.BlockSpec((Q_TILE, HEAD_DIM), lambda h, i: (i, h + HEADS)), SLOPE_SPEC],
        out_specs=[qt, full, full], out_shape=[jax.ShapeDtypeStruct((SEQ, wide), F32)] * 3,
        compiler_params=_params(("parallel", "arbitrary")))(qn, kn, vb, o, lse, d_mix, _alibi_slopes())


def _qk_bwd(proj, q_w, k_w, dqn, dkn, dv):
    def body(aq_ref, ak_ref, qw_ref, kw_ref, dqn_ref, dkn_ref, dv_ref, daq_ref, dak_ref, dav_ref, gq_ref, gk_ref):
        h = pl.program_id(0)

        @pl.when(h == 0)
        def _():
            gq_ref[...] = jnp.zeros_like(gq_ref)
            gk_ref[...] = jnp.zeros_like(gk_ref)

        def one(a_ref, w_ref, d_ref, da_ref, g_ref):
            a = a_ref[...]
            d = d_ref[...]
            rs = lax.rsqrt(jnp.mean(a * a, axis=-1, keepdims=True) + EPS)
            ah = a * rs
            g_ref[...] += jnp.sum(d * ah, axis=0, keepdims=True)
            dah = d * w_ref[...]
            da_ref[...] = (rs * (dah - ah * jnp.mean(dah * ah, axis=-1, keepdims=True))).astype(BF16)

        one(aq_ref, qw_ref, dqn_ref, daq_ref, gq_ref)
        one(ak_ref, kw_ref, dkn_ref, dak_ref, gk_ref)
        dav_ref[...] = dv_ref[...].astype(BF16)

    wide = HEADS * HEAD_DIM
    vec = pl.BlockSpec((1, HEAD_DIM), lambda h: (0, 0))
    return pl.pallas_call(
        body, name="qk_bwd", grid=(HEADS,),
        in_specs=[_head_col(ATT_OFF), _head_col(ATT_OFF + HEADS), vec, vec, _head_col(0), _head_col(0), _head_col(0)],
        out_specs=[_head_col(0)] * 3 + [vec, vec],
        out_shape=[jax.ShapeDtypeStruct((SEQ, wide), BF16)] * 3 + [jax.ShapeDtypeStruct((1, HEAD_DIM), F32)] * 2,
        compiler_params=_params(("arbitrary",)))(proj, proj, q_w, k_w, dqn, dkn, dv)


def _pair_sum(name, mine, theirs):
    _, r, c = mine.shape
    tr = r // 2 if r % 16 == 0 else r

    def body(a_ref, b_ref, o_ref):
        o_ref[...] = (a_ref[...].astype(F32) + b_ref[...].astype(F32)).astype(BF16)

    spec = pl.BlockSpec((None, tr, c), lambda q, i: (q, i, 0))
    return pl.pallas_call(body, name=name, grid=(4, r // tr), in_specs=[spec, spec], out_specs=spec,
                          out_shape=jax.ShapeDtypeStruct(mine.shape, BF16),
                          compiler_params=_params(("parallel", "parallel")))(mine, theirs)


def _adamw(name, w, m, v, addends, tr=None):
    r, c = w.shape
    tr = r if tr is None else tr
    n_add = len(addends)
    c1 = 1.0 - ADAM_B1 ** ADAM_STEP
    c2 = 1.0 - ADAM_B2 ** ADAM_STEP

    def body(*refs):
        w_ref, m_ref, v_ref = refs[:3]
        add_refs = refs[3:3 + n_add]
        g_ref, d_ref, nm_ref, nv_ref = refs[3 + n_add:]
        g = add_refs[0][...].astype(F32)
        for a_ref in add_refs[1:]:
            g = g + a_ref[...].astype(F32)
        nm = ADAM_B1 * m_ref[...] + (1.0 - ADAM_B1) * g
        nv = ADAM_B2 * v_ref[...] + (1.0 - ADAM_B2) * (g * g)
        g_ref[...] = g
        nm_ref[...] = nm
        nv_ref[...] = nv
        d_ref[...] = -ADAM_LR * ((nm / c1) / (jnp.sqrt(nv / c2) + ADAM_EPS) + ADAM_WD * w_ref[...])

    spec = pl.BlockSpec((tr, c), lambda i: (i, 0))
    out = jax.ShapeDtypeStruct((r, c), F32)
    return pl.pallas_call(body, name=name, grid=(r // tr,), in_specs=[spec] * (3 + n_add), out_specs=[spec] * 4,
                          out_shape=[out] * 4, compiler_params=_params(("parallel",)))(w, m, v, *addends)


def _sum_devices(gathered):
    _, r, c = gathered.shape

    def body(g_ref, o_ref):
        acc = g_ref[0]
        for d in range(1, N_DEV):
            acc = acc + g_ref[d]
        o_ref[...] = acc

    return pl.pallas_call(body, name="sum_devices", out_shape=jax.ShapeDtypeStruct((r, c), F32))(gathered)


def _pack_rows(vectors, rows):
    flat = jnp.concatenate([v.reshape(-1) for v in vectors])
    return jnp.pad(flat, (0, rows * 128 - flat.shape[0])).reshape(rows, 128)


def _unpack(flat, shapes):
    out, off = [], 0
    for shp in shapes:
        n = 1
        for d in shp:
            n *= d
        out.append(flat[off:off + n].reshape(shp))
        off += n
    return out


def _device_step(xs, tgt, mod, norm1_w, norm2_w, lb_logits, hg_norm_w, q_norm_w, k_norm_w, conv_w_full, conv_b,
                 win_g, wout_full, wup_g, wdown_full):
    shift1, scale1, gate1, shift2, scale2, gate2 = (mod[k] for k in range(6))

    h, rstd1 = _norm_fwd("norm1_fwd", xs, norm1_w, scale1, shift1)
    proj = _mm_blocked_rhs("mm_in", h, win_g)
    a_out, o_pre = _hgrn_fwd(proj, lb_logits, hg_norm_w)
    qn, kn, vb = _qk_prep(proj, q_norm_w, k_norm_w)
    att_o, lse = _attn_fwd(qn, kn, vb)
    mixin = jnp.concatenate([a_out, att_o.astype(BF16)], axis=1)
    mix = _mm_plain("mm_out", mixin, wout_full, NN, 512, 1024, F32)
    x1, h2, rstd2 = _norm_fwd("norm2_fwd", xs, norm2_w, scale2, shift2, resid=mix, gate=gate1)
    u = _mm_blocked_rhs("mm_up", h2, wup_g)
    y = _conv_gate_fwd(u, conv_w_full, conv_b)
    ffn = _mm_plain("mm_down", y, wdown_full, NN, 512, 512, F32)
    loss_v, dout, dffn, dgate2 = _loss_head(x1, ffn, gate2, tgt)

    dy = _mm_plain("mm_down_dx", dffn, wdown_full, NT, 512, UP_BLK, F32)
    gw_down = _mm_plain("mm_down_dw", y, dffn, TN, UP_BLK, 1024, BF16)
    da, dg, gconv_w, gconv_b = _conv_gate_bwd(u, dy, conv_w_full, conv_b)
    du = jnp.concatenate([da, dg], axis=1)
    dh2 = _mm_blocked_rhs_t("mm_up_dx", du, wup_g)
    gw_up = _mm_wgrad_blocked("mm_up_dw", h2, du)
    dx1, dmix, dshift2, dscale2, gnorm2, dgate1 = _norm_bwd(
        "norm2_bwd", dh2, x1, rstd2, norm2_w, scale2, dout, mix=mix, gate=gate1)
    dmixin = _mm_plain("mm_out_dx", dmix, wout_full, NT, 512, 1024, F32)
    gw_out = _mm_plain("mm_out_dw", mixin, dmix, TN, 512, 1024, BF16)
    dhq, dhf, dhi, dhg, glog, ghg = _hgrn_bwd(proj, lb_logits, hg_norm_w, o_pre, dmixin)
    dqn, dkn, dvv = _attn_bwd(qn, kn, vb, att_o, lse, dmixin)
    daq, dak, dav, gqw, gkw = _qk_bwd(proj, q_norm_w, k_norm_w, dqn, dkn, dvv)
    dproj = jnp.concatenate([dhq, dhf, dhi, dhg, daq, dak, dav], axis=1)
    dh = _mm_blocked_rhs_t("mm_in_dx", dproj, win_g)
    gw_in = _mm_wgrad_blocked("mm_in_dw", h, dproj)
    grad_x, dshift1, dscale1, gnorm1 = _norm_bwd("norm1_bwd", dh, xs, rstd1, norm1_w, scale1, dx1)
    gmod = jnp.concatenate([dshift1, dscale1, dgate1, dshift2, dscale2, dgate2], axis=1)
    return (loss_v, grad_x, gmod, gnorm1, gnorm2, glog, ghg, gqw, gkw, gconv_b, gconv_w,
            gw_in, gw_out, gw_up, gw_down)


def kernel(x, c, w_ada, b_ada, norm1_w, w_in, lb_logits, hg_norm_w, q_norm_w, k_norm_w, w_out, norm2_w, w_up, conv_w, conv_b, w_down, loss_target, m_w_ada, m_b_ada, m_norm1_w, m_w_in, m_lb_logits, m_hg_norm_w, m_q_norm_w, m_k_norm_w, m_w_out, m_norm2_w, m_w_up, m_conv_w, m_conv_b, m_w_down, v_w_ada, v_b_ada, v_norm1_w, v_w_in, v_lb_logits, v_hg_norm_w, v_q_norm_w, v_k_norm_w, v_w_out, v_norm2_w, v_w_up, v_conv_w, v_conv_b, v_w_down):
    ix, iy, ic = lax.axis_index("x"), lax.axis_index("y"), lax.axis_index("c")
    me = 4 * ix + 2 * iy + ic
    my_chip = 2 * ix + iy

    xs = x[0]
    tgt = loss_target[0]

    win_g, wout_g, wup_g, wdown_g = _allgather_weights(
        [w_in[0].astype(BF16), w_out[0].astype(BF16), w_up[0].astype(BF16), w_down[0].astype(BF16)])
    wout_full = wout_g.reshape(D_MODEL, D_MODEL)
    wdown_full = wdown_g.reshape(D_FF, D_MODEL)

    c_all = _allgather_vmem(c.reshape(8, D_MODEL // 8), "allgather_c").reshape(N_DEV, D_MODEL)
    b_blk = lax.dynamic_slice_in_dim(b_ada, me * ADA_BLK, ADA_BLK, axis=1)
    mod_cols = _ada_fwd(c_all, w_ada[0], b_blk)
    mod_all = _allgather_vmem(mod_cols, "allgather_mod").reshape(N_DEV, N_DEV, ADA_BLK)
    mod = lax.dynamic_index_in_dim(mod_all, me, axis=1, keepdims=False).reshape(6, 1, D_MODEL)

    conv_w_all = _allgather_vmem(_pack_rows([conv_w[0]], 24), "allgather_conv_w").reshape(N_DEV, 24 * 128)
    conv_w_full = conv_w_all[:, :3 * FF_BLK].reshape(N_DEV, 3, FF_BLK).transpose(1, 0, 2).reshape(3, D_FF)

    (loss_v, grad_x, gmod, gnorm1, gnorm2, glog, ghg, gqw, gkw, gconv_b, gconv_w,
     gw_in, gw_out, gw_up, gw_down) = _device_step(
        xs, tgt, mod, norm1_w, norm2_w, lb_logits, hg_norm_w, q_norm_w, k_norm_w, conv_w_full, conv_b,
        win_g, wout_full, wup_g, wdown_full)
    loss = lax.psum(loss_v[0, 0], AXES)

    small_shapes = [(1, 6 * D_MODEL), (1, D_MODEL), (1, D_MODEL), (2, HEADS * HEAD_DIM), (1, HEAD_DIM),
                    (1, HEAD_DIM), (1, HEAD_DIM), (1, D_FF), (3, D_FF)]
    small = [gmod, gnorm1, gnorm2, glog, ghg, gqw, gkw, gconv_b, gconv_w]
    n_small = sum(a.size for a in small)
    rows = -(-n_small // 1024) * 8
    gathered = _allgather_vmem(_pack_rows(small, rows), "allgather_small").reshape(N_DEV, rows, 128)
    summed = _sum_devices(gathered).reshape(-1)
    (g_b_ada, g_norm1, g_norm2, g_lb, g_hg, g_q, g_k, g_conv_b, g_conv_w_full) = _unpack(summed, small_shapes)
    g_conv_w = lax.dynamic_slice_in_dim(g_conv_w_full, me * FF_BLK, FF_BLK, axis=1)

    gmod_all = gathered[:, :6 * D_MODEL // 128, :].reshape(N_DEV, 6 * D_MODEL)
    gmod_cols = lax.dynamic_slice_in_dim(gmod_all, me * ADA_BLK, ADA_BLK, axis=1)
    g_w_ada_raw = _ada_wgrad(c_all, gmod_cols)

    partials = [gw_in, gw_out.reshape(N_DEV, OUT_BLK, D_MODEL), gw_up, gw_down.reshape(N_DEV, FF_BLK, D_MODEL)]
    halves = [p.reshape(4, 2, p.shape[1], p.shape[2]) for p in partials]
    mine = [lax.dynamic_index_in_dim(hv, ic, axis=1, keepdims=False) for hv in halves]
    theirs = [lax.dynamic_index_in_dim(hv, 1 - ic, axis=1, keepdims=False) for hv in halves]
    from_sibling = _exchange_sibling(theirs)
    chip_sums = [_pair_sum(f"grad_pair_sum_{k}", a, b) for k, (a, b) in enumerate(zip(mine, from_sibling))]
    from_chips = _exchange_chips(chip_sums)
    own = [lax.dynamic_index_in_dim(cs, my_chip, axis=0, keepdims=False) for cs in chip_sums]

    def big_update(name, w, m, v, own_a, recv, tr):
        return _adamw(name, w[0], m[0], v[0], [own_a, recv[0], recv[1], recv[2]], tr=tr)

    r_in = big_update("adamw_w_in", w_in, m_w_in, v_w_in, own[0], from_chips[0], 256)
    r_out = big_update("adamw_w_out", w_out, m_w_out, v_w_out, own[1], from_chips[1], 128)
    r_up = big_update("adamw_w_up", w_up, m_w_up, v_w_up, own[2], from_chips[2], 256)
    r_down = big_update("adamw_w_down", w_down, m_w_down, v_w_down, own[3], from_chips[3], 176)
    r_ada = _adamw("adamw_w_ada", w_ada[0], m_w_ada[0], v_w_ada[0], [g_w_ada_raw], tr=256)
    r_convw = _adamw("adamw_conv_w", conv_w[0], m_conv_w[0], v_conv_w[0], [g_conv_w])

    rep_shapes = [(1, 6 * D_MODEL), (1, D_MODEL), (1, D_MODEL), (2, HEADS * HEAD_DIM), (1, HEAD_DIM),
                  (1, HEAD_DIM), (1, HEAD_DIM), (1, D_FF)]
    rep_rows = -(-sum(a * b for a, b in rep_shapes) // 1024) * 8
    pack = lambda arrs: _pack_rows(arrs, rep_rows)
    rep = _adamw("adamw_small",
                 pack([b_ada, norm1_w, norm2_w, lb_logits, hg_norm_w, q_norm_w, k_norm_w, conv_b]),
                 pack([m_b_ada, m_norm1_w, m_norm2_w, m_lb_logits, m_hg_norm_w, m_q_norm_w, m_k_norm_w, m_conv_b]),
                 pack([v_b_ada, v_norm1_w, v_norm2_w, v_lb_logits, v_hg_norm_w, v_q_norm_w, v_k_norm_w, v_conv_b]),
                 [pack([g_b_ada, g_norm1, g_norm2, g_lb, g_hg, g_q, g_k, g_conv_b])])
    rep = [_unpack(r.reshape(-1), rep_shapes) for r in rep]

    def big(r):
        return [a[None] for a in r]

    order = {"w_ada": big(r_ada), "b_ada": [r[0] for r in rep], "norm1_w": [r[1] for r in rep],
             "w_in": big(r_in), "lb_logits": [r[3] for r in rep], "hg_norm_w": [r[4] for r in rep],
             "q_norm_w": [r[5] for r in rep], "k_norm_w": [r[6] for r in rep], "w_out": big(r_out),
             "norm2_w": [r[2] for r in rep], "w_up": big(r_up), "conv_w": big(r_convw),
             "conv_b": [r[7] for r in rep], "w_down": big(r_down)}
    names = ["w_ada", "b_ada", "norm1_w", "w_in", "lb_logits", "hg_norm_w", "q_norm_w", "k_norm_w", "w_out",
             "norm2_w", "w_up", "conv_w", "conv_b", "w_down"]
    outs = [loss, grad_x[None]]
    for kind in range(4):
        outs += [order[n][kind] for n in names]
    return tuple(outs)
```

```python
import functools

import jax
import jax.numpy as jnp
from jax import lax
from jax.experimental import pallas as pl
from jax.experimental.pallas import tpu as pltpu

F32 = jnp.float32
BF16 = jnp.bfloat16

N_DEV = 8
SEQ = 2048
D_MODEL = 2048
HEADS = 8
HEAD_DIM = 128
IN_COLS = 7168
IN_BLK = IN_COLS // N_DEV
D_FF = 5632
UP_BLK = 2 * D_FF // N_DEV
FF_BLK = D_FF // N_DEV
ADA_BLK = 6 * D_MODEL // N_DEV
OUT_BLK = D_MODEL // N_DEV
EPS = 1e-6
CHUNK = 16
ROW_TILE = 256
V7X_VMEM_LIMIT = 56 * 1024 * 1024

ADAM_LR = 0.001
ADAM_B1 = 0.9
ADAM_B2 = 0.999
ADAM_EPS = 1e-08
ADAM_WD = 0.01
ADAM_STEP = 10

NN = (((1,), (0,)), ((), ()))
NT = (((1,), (1,)), ((), ()))
TN = (((0,), (0,)), ((), ()))
MESH = pl.DeviceIdType.MESH
AXES = ("x", "y", "c")


def _params(sem=None, vmem=V7X_VMEM_LIMIT):
    return pltpu.CompilerParams(dimension_semantics=sem, vmem_limit_bytes=vmem)


def _sigmoid(x):
    return 1.0 / (1.0 + jnp.exp(-x))


def _dsilu(x, s):
    return s * (1.0 + x * (1.0 - s))


def _lane_sum(x, ones_bf16):
    hi = x.astype(BF16)
    lo = (x - hi.astype(F32)).astype(BF16)
    return (jnp.dot(hi, ones_bf16, preferred_element_type=F32)
            + jnp.dot(lo, ones_bf16, preferred_element_type=F32))


def _mesh_pos():
    return lax.axis_index("x"), lax.axis_index("y"), lax.axis_index("c")


def _allgather_vmem(x_blk, name):
    m_per, n = x_blk.shape

    def body(x_ref, out_ref, send_sems, recv_sems, local_sem):
        x, y, c = _mesh_pos()
        me, sibling = (x, y, c), (x, y, 1 - c)
        chips = [(1 - x, y), (x, 1 - y), (1 - x, 1 - y)]

        def rows(px, py, pc):
            return out_ref.at[pl.ds((4 * px + 2 * py + pc) * m_per, m_per), :]

        def copy(k, block, to, src=None):
            return pltpu.make_async_remote_copy(
                src_ref=rows(*block) if src is None else src, dst_ref=rows(*block),
                send_sem=send_sems.at[k], recv_sem=recv_sems.at[k], device_id=to, device_id_type=MESH)

        mine = pltpu.make_async_copy(x_ref, rows(*me), local_sem)
        mine.start()
        first = [copy(0, me, sibling, src=x_ref)]
        first += [copy(1 + j, me, (*chip, c), src=x_ref) for j, chip in enumerate(chips)]
        for cp in first:
            cp.start()
        passed = [copy(4 + j, (*chip, c), sibling) for j, chip in enumerate(chips)]
        for j, chip in enumerate(chips):
            copy(1 + j, (*chip, c), me).wait_recv()
            passed[j].start()
        copy(0, sibling, me).wait_recv()
        for j, chip in enumerate(chips):
            copy(4 + j, (*chip, 1 - c), me).wait_recv()
        for cp in first + passed:
            cp.wait_send()
        mine.wait()

    return pl.pallas_call(
        body, name=name,
        out_shape=jax.ShapeDtypeStruct((N_DEV * m_per, n), x_blk.dtype),
        in_specs=[pl.BlockSpec(memory_space=pltpu.VMEM)],
        out_specs=pl.BlockSpec(memory_space=pltpu.VMEM),
        scratch_shapes=[pltpu.SemaphoreType.DMA((7,)), pltpu.SemaphoreType.DMA((7,)), pltpu.SemaphoreType.DMA],
    )(x_blk)


def _allgather_weights(blocks):
    n_arr = len(blocks)

    def body(*refs):
        ins, outs = refs[:n_arr], refs[n_arr:2 * n_arr]
        send_sems, recv_sems, local_sems = refs[2 * n_arr:]
        x, y, c = _mesh_pos()
        me, sibling = (x, y, c), (x, y, 1 - c)
        chips = [(1 - x, y), (x, 1 - y), (1 - x, 1 - y)]

        def slot(a, px, py, pc):
            return outs[a].at[4 * px + 2 * py + pc]

        def copy(a, k, block, to, src=None):
            return pltpu.make_async_remote_copy(
                src_ref=slot(a, *block) if src is None else src, dst_ref=slot(a, *block),
                send_sem=send_sems.at[a, k], recv_sem=recv_sems.at[a, k], device_id=to, device_id_type=MESH)

        mine, first, passed = [], [], []
        for a in range(n_arr):
            cp = pltpu.make_async_copy(ins[a], slot(a, *me), local_sems.at[a])
            cp.start()
            mine.append(cp)
            first.append(copy(a, 0, me, sibling, src=ins[a]))
            first += [copy(a, 1 + j, me, (*chip, c), src=ins[a]) for j, chip in enumerate(chips)]
        for cp in first:
            cp.start()
        for j, chip in enumerate(chips):
            for a in range(n_arr):
                copy(a, 1 + j, (*chip, c), me).wait_recv()
                cp = copy(a, 4 + j, (*chip, c), sibling)
                cp.start()
                passed.append(cp)
        for a in range(n_arr):
            copy(a, 0, sibling, me).wait_recv()
            for j, chip in enumerate(chips):
                copy(a, 4 + j, (*chip, 1 - c), me).wait_recv()
        for cp in first + passed:
            cp.wait_send()
        for cp in mine:
            cp.wait()

    hbm = pl.BlockSpec(memory_space=pltpu.HBM)
    return pl.pallas_call(
        body, name="allgather_weights",
        out_shape=[jax.ShapeDtypeStruct((N_DEV,) + b.shape, b.dtype) for b in blocks],
        in_specs=[hbm] * n_arr, out_specs=[hbm] * n_arr,
        scratch_shapes=[pltpu.SemaphoreType.DMA((n_arr, 7)), pltpu.SemaphoreType.DMA((n_arr, 7)),
                        pltpu.SemaphoreType.DMA((n_arr,))],
    )(*blocks)


HBM_SPEC = pl.BlockSpec(memory_space=pltpu.HBM)


class _FusedCopies:
    def __init__(self, kind, arrays):
        n = len(arrays)
        self.n = n
        self.kind = kind
        self.in_specs = [HBM_SPEC] * n
        self.out_specs = [HBM_SPEC] * n
        if kind == "gather":
            self.out_shape = [jax.ShapeDtypeStruct((N_DEV,) + a.shape, a.dtype) for a in arrays]
            self.scratch_shapes = [pltpu.SemaphoreType.DMA((n, 4)), pltpu.SemaphoreType.DMA((n, 4)),
                                   pltpu.SemaphoreType.DMA((n,))]
        else:
            self.out_shape = [jax.ShapeDtypeStruct((3,) + a.shape[1:], a.dtype) for a in arrays]
            self.scratch_shapes = [pltpu.SemaphoreType.DMA((n, 3)), pltpu.SemaphoreType.DMA((n, 3))]
        self.n_scratch = len(self.scratch_shapes)

    def copies(self, ins, outs, sems):
        x, y, c = _mesh_pos()
        chips = [(1 - x, y), (x, 1 - y), (1 - x, 1 - y)]
        starts, waits = [], []
        if self.kind == "gather":
            send_sems, recv_sems, local_sems = sems
            me = (x, y, c)
            peers = [(x, y, 1 - c)] + [(px, py, c) for px, py in chips]

            def slot(a, pos):
                return outs[a].at[4 * pos[0] + 2 * pos[1] + pos[2]]

            def remote(a, k, lands_from):
                return pltpu.make_async_remote_copy(
                    src_ref=ins[a], dst_ref=slot(a, lands_from), send_sem=send_sems.at[a, k],
                    recv_sem=recv_sems.at[a, k], device_id=peers[k], device_id_type=MESH)

            for a in range(self.n):
                local = pltpu.make_async_copy(ins[a], slot(a, me), local_sems.at[a])
                starts.append(local)
                waits.append(local)
                for k in range(4):
                    starts.append(remote(a, k, me))
                    waits.append(remote(a, k, peers[k]))
        else:
            send_sems, recv_sems = sems
            for a in range(self.n):
                for j, (px, py) in enumerate(chips):
                    cp = pltpu.make_async_remote_copy(
                        src_ref=ins[a].at[2 * px + py], dst_ref=outs[a].at[j], send_sem=send_sems.at[a, j],
                        recv_sem=recv_sems.at[a, j], device_id=(px, py, c), device_id_type=MESH)
                    starts.append(cp)
                    waits.append(cp)
        return starts, waits


def _host_body(body, n_in, n_out, fused, first_last):
    if fused is None:
        return body
    n = fused.n

    def wrapped(*refs):
        core_in, f_in = refs[:n_in], refs[n_in:n_in + n]
        core_out = refs[n_in + n:n_in + n + n_out]
        f_out = refs[n_in + n + n_out:n_in + 2 * n + n_out]
        rest = refs[n_in + 2 * n + n_out:]
        core_scratch, f_sems = rest[:len(rest) - fused.n_scratch], rest[len(rest) - fused.n_scratch:]
        starts, waits = fused.copies(f_in, f_out, f_sems)
        first, last = first_last()

        @pl.when(first)
        def _():
            for cp in starts:
                cp.start()

        body(*core_in, *core_out, *core_scratch)

        @pl.when(last)
        def _():
            for cp in waits:
                cp.wait()

    return wrapped


def _host_call(body, n_in, n_out, fused, first_last, *, name, grid, in_specs, out_specs, out_shape, scratch_shapes,
               sem, operands):
    if fused is not None:
        in_specs = list(in_specs) + fused.in_specs
        out_specs = list(out_specs) + fused.out_specs
        out_shape = list(out_shape) + fused.out_shape
        scratch_shapes = list(scratch_shapes) + fused.scratch_shapes
        sem = tuple("arbitrary" for _ in sem)
    res = pl.pallas_call(_host_body(body, n_in, n_out, fused, first_last), name=name, grid=grid, in_specs=in_specs,
                         out_specs=out_specs, out_shape=out_shape, scratch_shapes=scratch_shapes,
                         compiler_params=_params(sem))(*operands)
    return list(res[:n_out]), list(res[n_out:])


def _forward_to_sibling(name, gathered):
    n_arr = len(gathered)

    def body(*refs):
        ins, outs = refs[:n_arr], refs[n_arr:2 * n_arr]
        send_sems, recv_sems = refs[2 * n_arr:]
        x, y, c = _mesh_pos()
        chips = [(1 - x, y), (x, 1 - y), (1 - x, 1 - y)]

        def copy(a, j, pc):
            px, py = chips[j]
            s = 4 * px + 2 * py + pc
            return pltpu.make_async_remote_copy(
                src_ref=ins[a].at[s], dst_ref=outs[a].at[s], send_sem=send_sems.at[a, j], recv_sem=recv_sems.at[a, j],
                device_id=(x, y, 1 - c), device_id_type=MESH)

        for a in range(n_arr):
            for j in range(3):
                copy(a, j, c).start()
        for a in range(n_arr):
            for j in range(3):
                copy(a, j, 1 - c).wait_recv()
                copy(a, j, c).wait_send()

    return pl.pallas_call(
        body, name=name,
        out_shape=[jax.ShapeDtypeStruct(g.shape, g.dtype) for g in gathered],
        in_specs=[HBM_SPEC] * n_arr, out_specs=[HBM_SPEC] * n_arr,
        input_output_aliases={a: a for a in range(n_arr)},
        scratch_shapes=[pltpu.SemaphoreType.DMA((n_arr, 3)), pltpu.SemaphoreType.DMA((n_arr, 3))],
    )(*gathered)


def _exchange_sibling(name, parts):
    n_arr = len(parts)

    def body(*refs):
        ins, outs = refs[:n_arr], refs[n_arr:2 * n_arr]
        send_sems, recv_sems = refs[2 * n_arr:]
        x, y, c = _mesh_pos()
        copies = [pltpu.make_async_remote_copy(
            src_ref=ins[a], dst_ref=outs[a], send_sem=send_sems.at[a], recv_sem=recv_sems.at[a],
            device_id=(x, y, 1 - c), device_id_type=MESH) for a in range(n_arr)]
        for cp in copies:
            cp.start()
        for cp in copies:
            cp.wait_recv()
        for cp in copies:
            cp.wait_send()

    hbm = pl.BlockSpec(memory_space=pltpu.HBM)
    return pl.pallas_call(
        body, name=name,
        out_shape=[jax.ShapeDtypeStruct(p.shape, p.dtype) for p in parts],
        in_specs=[hbm] * n_arr, out_specs=[hbm] * n_arr,
        scratch_shapes=[pltpu.SemaphoreType.DMA((n_arr,)), pltpu.SemaphoreType.DMA((n_arr,))],
    )(*parts)


def _exchange_chips(name, chip_sums):
    n_arr = len(chip_sums)

    def body(*refs):
        ins, outs = refs[:n_arr], refs[n_arr:2 * n_arr]
        send_sems, recv_sems = refs[2 * n_arr:]
        x, y, c = _mesh_pos()
        chips = [(1 - x, y), (x, 1 - y), (1 - x, 1 - y)]
        copies = []
        for a in range(n_arr):
            for j, (px, py) in enumerate(chips):
                copies.append(pltpu.make_async_remote_copy(
                    src_ref=ins[a].at[2 * px + py], dst_ref=outs[a].at[j],
                    send_sem=send_sems.at[a, j], recv_sem=recv_sems.at[a, j],
                    device_id=(px, py, c), device_id_type=MESH))
        for cp in copies:
            cp.start()
        for cp in copies:
            cp.wait_recv()
        for cp in copies:
            cp.wait_send()

    hbm = pl.BlockSpec(memory_space=pltpu.HBM)
    return pl.pallas_call(
        body, name=name,
        out_shape=[jax.ShapeDtypeStruct((3,) + p.shape[1:], p.dtype) for p in chip_sums],
        in_specs=[hbm] * n_arr, out_specs=[hbm] * n_arr,
        scratch_shapes=[pltpu.SemaphoreType.DMA((n_arr, 3)), pltpu.SemaphoreType.DMA((n_arr, 3))],
    )(*chip_sums)


def _matmul(name, a, b, dims, grid, a_spec, b_spec, o_spec, out_shape, acc_axis=None):
    def body(a_ref, b_ref, o_ref):
        r = lax.dot_general(a_ref[...], b_ref[...], dims, preferred_element_type=F32)
        if acc_axis is None:
            o_ref[...] = r.astype(o_ref.dtype)
        else:
            k = pl.program_id(acc_axis)

            @pl.when(k == 0)
            def _():
                o_ref[...] = r

            @pl.when(k > 0)
            def _():
                o_ref[...] += r

    sem = tuple("arbitrary" if i == acc_axis else "parallel" for i in range(len(grid)))
    return pl.pallas_call(body, name=name, grid=grid, in_specs=[a_spec, b_spec], out_specs=o_spec,
                          out_shape=out_shape, compiler_params=_params(sem))(a, b)


def _mm_blocked_rhs(name, a, w_g, tm=512):
    m, k = a.shape
    nb = w_g.shape[2]
    return _matmul(name, a, w_g, NN, (N_DEV, m // tm),
                   pl.BlockSpec((tm, k), lambda j, i: (i, 0)),
                   pl.BlockSpec((None, k, nb), lambda j, i: (j, 0, 0)),
                   pl.BlockSpec((tm, nb), lambda j, i: (i, j)),
                   jax.ShapeDtypeStruct((m, N_DEV * nb), F32))


def _mm_blocked_rhs_t(name, a, w_g, tm=512):
    m = a.shape[0]
    n, nb = w_g.shape[1], w_g.shape[2]
    return _matmul(name, a, w_g, NT, (m // tm, N_DEV),
                   pl.BlockSpec((tm, nb), lambda i, j: (i, j)),
                   pl.BlockSpec((None, n, nb), lambda i, j: (j, 0, 0)),
                   pl.BlockSpec((tm, n), lambda i, j: (i, 0)),
                   jax.ShapeDtypeStruct((m, n), F32), acc_axis=1)


def _mm_wgrad_blocked(name, act, dcols, tk=512):
    t, k = act.shape
    nb = dcols.shape[1] // N_DEV
    return _matmul(name, act, dcols, TN, (N_DEV, k // tk),
                   pl.BlockSpec((t, tk), lambda j, i: (0, i)),
                   pl.BlockSpec((t, nb), lambda j, i: (0, j)),
                   pl.BlockSpec((None, tk, nb), lambda j, i: (j, i, 0)),
                   jax.ShapeDtypeStruct((N_DEV, k, nb), BF16))


def _mm_plain(name, a, b, dims, tm, tn, out_dtype):
    if dims == NN:
        (m, k), n = a.shape, b.shape[1]
        a_spec = pl.BlockSpec((tm, k), lambda i, j: (i, 0))
        b_spec = pl.BlockSpec((k, tn), lambda i, j: (0, j))
    elif dims == NT:
        (m, k), n = a.shape, b.shape[0]
        a_spec = pl.BlockSpec((tm, k), lambda i, j: (i, 0))
        b_spec = pl.BlockSpec((tn, k), lambda i, j: (j, 0))
    else:
        (k, m), n = a.shape, b.shape[1]
        a_spec = pl.BlockSpec((k, tm), lambda i, j: (0, i))
        b_spec = pl.BlockSpec((k, tn), lambda i, j: (0, j))
    return _matmul(name, a, b, dims, (m // tm, n // tn), a_spec, b_spec,
                   pl.BlockSpec((tm, tn), lambda i, j: (i, j)), jax.ShapeDtypeStruct((m, n), out_dtype))


def _ada_fwd(c_all, w_ada_blk, b_blk):
    def body(c_ref, w_ref, b_ref, o_ref):
        cv = c_ref[...]
        o_ref[...] = jnp.dot(cv * _sigmoid(cv), w_ref[...], preferred_element_type=F32) + b_ref[...]

    tn = 512
    return pl.pallas_call(
        body, name="ada_fwd", grid=(ADA_BLK // tn,),
        in_specs=[pl.BlockSpec((N_DEV, D_MODEL), lambda j: (0, 0)),
                  pl.BlockSpec((D_MODEL, tn), lambda j: (0, j)),
                  pl.BlockSpec((1, tn), lambda j: (0, j))],
        out_specs=pl.BlockSpec((N_DEV, tn), lambda j: (0, j)),
        out_shape=jax.ShapeDtypeStruct((N_DEV, ADA_BLK), F32),
        compiler_params=_params(("parallel",)))(c_all, w_ada_blk, b_blk)


def _ada_wgrad(c_all, gmod_cols):
    def body(c_ref, g_ref, o_ref):
        cv = c_ref[...]
        o_ref[...] = lax.dot_general(cv * _sigmoid(cv), g_ref[...], TN, preferred_element_type=F32)

    tk = 512
    return pl.pallas_call(
        body, name="ada_wgrad", grid=(D_MODEL // tk,),
        in_specs=[pl.BlockSpec((N_DEV, tk), lambda i: (0, i)),
                  pl.BlockSpec((N_DEV, ADA_BLK), lambda i: (0, 0))],
        out_specs=pl.BlockSpec((tk, ADA_BLK), lambda i: (i, 0)),
        out_shape=jax.ShapeDtypeStruct((D_MODEL, ADA_BLK), F32),
        compiler_params=_params(("parallel",)))(c_all, gmod_cols)


def _row_spec(cols=D_MODEL):
    return pl.BlockSpec((ROW_TILE, cols), lambda i: (i, 0))


def _vec_spec(cols=D_MODEL):
    return pl.BlockSpec((1, cols), lambda i: (0, 0))


def _norm_fwd(name, x, w, scale, shift, resid=None, gate=None):
    has_res = resid is not None

    def body(*refs):
        if has_res:
            x_ref, r_ref, g_ref, w_ref, sc_ref, sh_ref, xr_ref, h_ref, rs_ref = refs
            xr = x_ref[...] + g_ref[...] * r_ref[...]
            xr_ref[...] = xr
        else:
            x_ref, w_ref, sc_ref, sh_ref, h_ref, rs_ref = refs
            xr = x_ref[...]
        rs = lax.rsqrt(jnp.mean(xr * xr, axis=-1, keepdims=True) + EPS)
        h = (xr * rs) * w_ref[...] * (1.0 + sc_ref[...]) + sh_ref[...]
        h_ref[...] = h.astype(BF16)
        rs_ref[...] = rs

    s = x.shape[0]
    ins = [x] + ([resid, gate] if has_res else []) + [w, scale, shift]
    in_specs = [_row_spec()] + ([_row_spec(), _vec_spec()] if has_res else []) + [_vec_spec()] * 3
    outs = ([jax.ShapeDtypeStruct((s, D_MODEL), F32)] if has_res else []) + [
        jax.ShapeDtypeStruct((s, D_MODEL), BF16), jax.ShapeDtypeStruct((s, 1), F32)]
    out_specs = ([_row_spec()] if has_res else []) + [_row_spec(), pl.BlockSpec((ROW_TILE, 1), lambda i: (i, 0))]
    return pl.pallas_call(body, name=name, grid=(s // ROW_TILE,), in_specs=in_specs, out_specs=out_specs,
                          out_shape=outs, compiler_params=_params(("parallel",)))(*ins)


def _norm_bwd(name, dh, x, rstd, w, scale, dres, mix=None, gate=None):
    has_mix = mix is not None

    def body(*refs):
        if has_mix:
            (dh_ref, x_ref, rs_ref, w_ref, sc_ref, dr_ref, mix_ref, g_ref,
             dx_ref, dmix_ref, dsh_ref, dsc_ref, dw_ref, dg_ref) = refs
        else:
            dh_ref, x_ref, rs_ref, w_ref, sc_ref, dr_ref, dx_ref, dsh_ref, dsc_ref, dw_ref = refs
        i = pl.program_id(0)
        dhv = dh_ref[...]
        rs = rs_ref[...]
        xn = x_ref[...] * rs
        wv = w_ref[...]
        one_sc = 1.0 + sc_ref[...]
        dxn = dhv * wv * one_sc
        dx = dr_ref[...] + rs * (dxn - xn * jnp.mean(dxn * xn, axis=-1, keepdims=True))
        dx_ref[...] = dx
        sums = [(dsh_ref, dhv), (dsc_ref, dhv * xn * wv), (dw_ref, dhv * one_sc * xn)]
        if has_mix:
            dmix_ref[...] = (dx * g_ref[...]).astype(BF16)
            sums.append((dg_ref, dx * mix_ref[...]))

        @pl.when(i == 0)
        def _():
            for ref, _v in sums:
                ref[...] = jnp.zeros_like(ref)

        for ref, v in sums:
            ref[...] += jnp.sum(v, axis=0, keepdims=True)

    s = x.shape[0]
    ins = [dh, x, rstd, w, scale, dres] + ([mix, gate] if has_mix else [])
    in_specs = ([_row_spec(), _row_spec(), pl.BlockSpec((ROW_TILE, 1), lambda i: (i, 0)), _vec_spec(), _vec_spec(),
                 _row_spec()] + ([_row_spec(), _vec_spec()] if has_mix else []))
    vec = jax.ShapeDtypeStruct((1, D_MODEL), F32)
    outs = ([jax.ShapeDtypeStruct((s, D_MODEL), F32)] + ([jax.ShapeDtypeStruct((s, D_MODEL), BF16)] if has_mix else [])
            + [vec] * (4 if has_mix else 3))
    out_specs = [_row_spec()] + ([_row_spec()] if has_mix else []) + [_vec_spec()] * (4 if has_mix else 3)
    return pl.pallas_call(body, name=name, grid=(s // ROW_TILE,), in_specs=in_specs, out_specs=out_specs,
                          out_shape=outs, compiler_params=_params(("arbitrary",)))(*ins)


def _loss_head(x1, ffn, gate2, target):
    def body(x_ref, f_ref, g_ref, t_ref, loss_ref, dout_ref, dffn_ref, dg_ref):
        i = pl.program_id(0)
        fv = f_ref[...]
        gv = g_ref[...]
        err = x_ref[...] + gv * fv - t_ref[...]
        dout = err * (1.0 / D_MODEL)
        dout_ref[...] = dout
        dffn_ref[...] = (dout * gv).astype(BF16)

        @pl.when(i == 0)
        def _():
            loss_ref[...] = jnp.zeros_like(loss_ref)
            dg_ref[...] = jnp.zeros_like(dg_ref)

        row = jnp.sum(err * err, axis=-1, keepdims=True) * (1.0 / D_MODEL)
        loss_ref[...] += jnp.broadcast_to(0.5 * jnp.sum(row, axis=0, keepdims=True), (1, 128))
        dg_ref[...] += jnp.sum(dout * fv, axis=0, keepdims=True)

    s = x1.shape[0]
    return pl.pallas_call(
        body, name="loss_head", grid=(s // ROW_TILE,),
        in_specs=[_row_spec(), _row_spec(), _vec_spec(), _row_spec()],
        out_specs=[pl.BlockSpec((1, 128), lambda i: (0, 0)), _row_spec(), _row_spec(), _vec_spec()],
        out_shape=[jax.ShapeDtypeStruct((1, 128), F32), jax.ShapeDtypeStruct((s, D_MODEL), F32),
                   jax.ShapeDtypeStruct((s, D_MODEL), BF16), jax.ShapeDtypeStruct((1, D_MODEL), F32)],
        compiler_params=_params(("arbitrary",)))(x1, ffn, gate2, target)


CONV_TILE = 512
N_CONV_TILES = D_FF // CONV_TILE


def _shift_rows(a, k, row):
    n = a.shape[0]
    if k > 0:
        return jnp.where(row >= k, pltpu.roll(a, k, 0), 0.0)
    return jnp.where(row < n + k, pltpu.roll(a, n + k, 0), 0.0)


def _conv_gate_fwd(u, conv_w, conv_b):
    s = u.shape[0]

    def body(a_ref, g_ref, w_ref, b_ref, y_ref):
        a = a_ref[...]
        w = w_ref[...]
        row = lax.broadcasted_iota(jnp.int32, a.shape, 0)
        ac = b_ref[...] + _shift_rows(a, 2, row) * w[0:1] + _shift_rows(a, 1, row) * w[1:2] + a * w[2:3]
        y_ref[...] = (ac * _sigmoid(ac) * g_ref[...]).astype(BF16)

    col = lambda off: pl.BlockSpec((s, CONV_TILE), lambda i: (0, i + off))
    return pl.pallas_call(
        body, name="conv_gate_fwd", grid=(N_CONV_TILES,),
        in_specs=[col(0), col(N_CONV_TILES), pl.BlockSpec((3, CONV_TILE), lambda i: (0, i)),
                  pl.BlockSpec((1, CONV_TILE), lambda i: (0, i))],
        out_specs=col(0), out_shape=jax.ShapeDtypeStruct((s, D_FF), BF16),
        compiler_params=_params(("parallel",)))(u, u, conv_w, conv_b)


def _conv_gate_bwd(u, dy, conv_w, conv_b):
    s = u.shape[0]

    def body(a_ref, g_ref, dy_ref, w_ref, b_ref, da_ref, dg_ref, gw_ref, gb_ref):
        a = a_ref[...]
        w = w_ref[...]
        row = lax.broadcasted_iota(jnp.int32, a.shape, 0)
        a1 = _shift_rows(a, 1, row)
        a2 = _shift_rows(a, 2, row)
        ac = b_ref[...] + a2 * w[0:1] + a1 * w[1:2] + a * w[2:3]
        sg = _sigmoid(ac)
        dyv = dy_ref[...]
        dg_ref[...] = (dyv * (ac * sg)).astype(BF16)
        dac = dyv * g_ref[...] * _dsilu(ac, sg)
        gb_ref[...] = jnp.sum(dac, axis=0, keepdims=True)
        gw_ref[0:1, :] = jnp.sum(dac * a2, axis=0, keepdims=True)
        gw_ref[1:2, :] = jnp.sum(dac * a1, axis=0, keepdims=True)
        gw_ref[2:3, :] = jnp.sum(dac * a, axis=0, keepdims=True)
        da = dac * w[2:3] + _shift_rows(dac, -1, row) * w[1:2] + _shift_rows(dac, -2, row) * w[0:1]
        da_ref[...] = da.astype(BF16)

    col = lambda off: pl.BlockSpec((s, CONV_TILE), lambda i: (0, i + off))
    return pl.pallas_call(
        body, name="conv_gate_bwd", grid=(N_CONV_TILES,),
        in_specs=[col(0), col(N_CONV_TILES), col(0), pl.BlockSpec((3, CONV_TILE), lambda i: (0, i)),
                  pl.BlockSpec((1, CONV_TILE), lambda i: (0, i))],
        out_specs=[col(0), col(0), pl.BlockSpec((3, CONV_TILE), lambda i: (0, i)),
                   pl.BlockSpec((1, CONV_TILE), lambda i: (0, i))],
        out_shape=[jax.ShapeDtypeStruct((s, D_FF), BF16), jax.ShapeDtypeStruct((s, D_FF), BF16),
                   jax.ShapeDtypeStruct((3, D_FF), F32), jax.ShapeDtypeStruct((1, D_FF), F32)],
        compiler_params=_params(("parallel",)))(u, u, dy, conv_w, conv_b)


HG_TILE = 128
CHUNK_UNROLL = 8


def _unrolled_loop(n, body, init):
    def group(i, carry):
        for u in range(CHUNK_UNROLL):
            carry = body(i * CHUNK_UNROLL + u, carry)
        return carry

    return lax.fori_loop(0, n // CHUNK_UNROLL, group, init)


def _head_col(off):
    return pl.BlockSpec((SEQ, HEAD_DIM), lambda h: (0, h + off))


def _hgrn_gates(hq, hf, lb, pos):
    q = hq * _sigmoid(hq)
    sig = _sigmoid(hf)
    f = lb + (1.0 - lb) * sig
    gl = jnp.log(f)
    for sh in (1, 2, 4, 8):
        gl = gl + jnp.where(pos >= sh, pltpu.roll(gl, sh, 0), 0.0)
    return q, sig, f, 1.0 - f, gl


def _lower_bound(lbl):
    return 1.0 / (1.0 + jnp.exp(lbl[1:2, :] - lbl[0:1, :]))


def _head_first_last():
    h = pl.program_id(0)
    return h == 0, h == HEADS - 1


def _hgrn_fwd(proj, lb_logits, norm_w, fused=None, fused_arrays=()):
    n_tiles = SEQ // HG_TILE
    n_chunks = SEQ // CHUNK
    fused_arrays = list(fused_arrays)

    def body(hq_ref, hf_ref, hi_ref, hg_ref, lbl_ref, nw_ref, aout_ref, opre_ref, q_s, k_s, gl_s):
        lb = _lower_bound(lbl_ref[...])
        ones = jnp.ones((HEAD_DIM, HEAD_DIM), BF16)
        pos = lax.broadcasted_iota(jnp.int32, (HG_TILE, HEAD_DIM), 0) % CHUNK

        def tile(i, carry):
            rows = pl.ds(pl.multiple_of(i * HG_TILE, HG_TILE), HG_TILE)
            v = hi_ref[rows, :]
            q, _sig, _f, kk, gl = _hgrn_gates(hq_ref[rows, :], hf_ref[rows, :], lb, pos)
            o = _lane_sum(q * kk, ones) * v
            for d in range(1, CHUNK):
                e = jnp.where(pos >= d, jnp.exp(gl - pltpu.roll(gl, d, 0)), 0.0)
                o = o + _lane_sum(q * pltpu.roll(kk, d, 0) * e, ones) * pltpu.roll(v, d, 0)
            q_s[rows, :] = q
            k_s[rows, :] = kk
            gl_s[rows, :] = gl
            opre_ref[rows, :] = o
            return carry

        lax.fori_loop(0, n_tiles, tile, 0)

        def chunk(c, st):
            rows = pl.ds(pl.multiple_of(c * CHUNK, CHUNK), CHUNK)
            gl = gl_s[rows, :]
            qt = q_s[rows, :] * jnp.exp(gl)
            opre_ref[rows, :] += lax.dot_general(qt.astype(BF16), st.astype(BF16), NT, preferred_element_type=F32)
            gll = gl[CHUNK - 1:CHUNK, :]
            kt = k_s[rows, :] * jnp.exp(gll - gl)
            return st * jnp.exp(gll) + lax.dot_general(hi_ref[rows, :].astype(BF16), kt.astype(BF16), TN,
                                                       preferred_element_type=F32)

        _unrolled_loop(n_chunks, chunk, jnp.zeros((HEAD_DIM, HEAD_DIM), F32))

        def finish(i, carry):
            rows = pl.ds(pl.multiple_of(i * HG_TILE, HG_TILE), HG_TILE)
            o = opre_ref[rows, :]
            hg = hg_ref[rows, :]
            rs = lax.rsqrt(jnp.mean(o * o, axis=-1, keepdims=True) + EPS)
            aout_ref[rows, :] = ((o * rs) * nw_ref[...] * (hg * _sigmoid(hg))).astype(BF16)
            return carry

        lax.fori_loop(0, n_tiles, finish, 0)

    return _host_call(
        body, 6, 2, fused, _head_first_last, name="hgrn_fwd", grid=(HEADS,),
        in_specs=[_head_col(0), _head_col(HEADS), _head_col(2 * HEADS), _head_col(3 * HEADS),
                  pl.BlockSpec((2, HEAD_DIM), lambda h: (0, h)), pl.BlockSpec((1, HEAD_DIM), lambda h: (0, 0))],
        out_specs=[_head_col(0), _head_col(0)],
        out_shape=[jax.ShapeDtypeStruct((SEQ, HEADS * HEAD_DIM), BF16), jax.ShapeDtypeStruct((SEQ, HEADS * HEAD_DIM), F32)],
        scratch_shapes=[pltpu.VMEM((SEQ, HEAD_DIM), F32)] * 3, sem=("parallel",),
        operands=[proj, proj, proj, proj, lb_logits, norm_w] + fused_arrays)


def _hgrn_bwd(proj, lb_logits, norm_w, o_pre, d_aout, fused=None, fused_arrays=()):
    n_tiles = SEQ // HG_TILE
    n_chunks = SEQ // CHUNK

    def body(hq_ref, hf_ref, hi_ref, hg_ref, lbl_ref, nw_ref, opre_ref, da_ref,
             dhq_ref, dhf_ref, dhi_ref, dhg_ref, dlog_ref, gnw_ref,
             q_s, k_s, gl_s, do_s, dq_s, dk_s, dv_s, st_s):
        h = pl.program_id(0)
        lb = _lower_bound(lbl_ref[...])
        nw = nw_ref[...]
        ones = jnp.ones((HEAD_DIM, HEAD_DIM), BF16)
        pos = lax.broadcasted_iota(jnp.int32, (HG_TILE, HEAD_DIM), 0) % CHUNK

        @pl.when(h == 0)
        def _():
            gnw_ref[...] = jnp.zeros_like(gnw_ref)

        def tile(i, carry):
            rows = pl.ds(pl.multiple_of(i * HG_TILE, HG_TILE), HG_TILE)
            v = hi_ref[rows, :]
            q, _sig, _f, kk, gl = _hgrn_gates(hq_ref[rows, :], hf_ref[rows, :], lb, pos)
            o = opre_ref[rows, :]
            hg = hg_ref[rows, :]
            da = da_ref[rows, :]
            rs = lax.rsqrt(jnp.mean(o * o, axis=-1, keepdims=True) + EPS)
            oh = o * rs
            sg = _sigmoid(hg)
            dnorm = da * (hg * sg)
            dhg_ref[rows, :] = (da * (oh * nw) * _dsilu(hg, sg)).astype(BF16)
            gnw_ref[...] += jnp.sum(dnorm * oh, axis=0, keepdims=True)
            doh = dnorm * nw
            do = rs * (doh - oh * jnp.mean(doh * oh, axis=-1, keepdims=True))

            d_a = _lane_sum(do * v, ones)
            dq = d_a * kk
            dk = d_a * q
            dv = _lane_sum(q * kk, ones) * do
            for d in range(1, CHUNK):
                ks = pltpu.roll(kk, d, 0)
                e = jnp.where(pos >= d, jnp.exp(gl - pltpu.roll(gl, d, 0)), 0.0)
                a_d = _lane_sum(q * ks * e, ones)
                d_a = _lane_sum(do * pltpu.roll(v, d, 0), ones) * e
                dq = dq + d_a * ks
                dk = dk + pltpu.roll(d_a * q, HG_TILE - d, 0)
                dv = dv + pltpu.roll(a_d * do, HG_TILE - d, 0)
            q_s[rows, :] = q
            k_s[rows, :] = kk
            gl_s[rows, :] = gl
            do_s[rows, :] = do
            dq_s[rows, :] = dq
            dk_s[rows, :] = dk
            dv_s[rows, :] = dv
            return carry

        lax.fori_loop(0, n_tiles, tile, 0)

        def fwd_chunk(c, st):
            rows = pl.ds(pl.multiple_of(c * CHUNK, CHUNK), CHUNK)
            gl = gl_s[rows, :]
            st_s[c] = st
            dq_s[rows, :] += jnp.dot(do_s[rows, :].astype(BF16), st.astype(BF16),
                                     preferred_element_type=F32) * jnp.exp(gl)
            gll = gl[CHUNK - 1:CHUNK, :]
            kt = k_s[rows, :] * jnp.exp(gll - gl)
            return st * jnp.exp(gll) + lax.dot_general(hi_ref[rows, :].astype(BF16), kt.astype(BF16), TN,
                                                       preferred_element_type=F32)

        _unrolled_loop(n_chunks, fwd_chunk, jnp.zeros((HEAD_DIM, HEAD_DIM), F32))

        pos_c = lax.broadcasted_iota(jnp.int32, (CHUNK, HEAD_DIM), 0)

        def bwd_chunk(i, carry):
            rt, dlb = carry
            c = n_chunks - 1 - i
            rows = pl.ds(pl.multiple_of(c * CHUNK, CHUNK), CHUNK)
            gl = gl_s[rows, :]
            q = q_s[rows, :]
            kk = k_s[rows, :]
            do = do_s[rows, :]
            gll = gl[CHUNK - 1:CHUNK, :]
            egl = jnp.exp(gll)
            ekt = jnp.exp(gll - gl)
            rt_b = rt.astype(BF16)
            dk_in = dk_s[rows, :]
            dk_far = jnp.dot(hi_ref[rows, :].astype(BF16), rt_b, preferred_element_type=F32) * ekt
            dk = dk_in + dk_far
            dv = dv_s[rows, :] + lax.dot_general((kk * ekt).astype(BF16), rt_b, NT, preferred_element_type=F32)
            dq = dq_s[rows, :]
            rc = q * dq - kk * dk_in
            pc = kk * dk_far
            pre = pc
            for sh in (1, 2, 4, 8):
                rc = rc + jnp.where(pos_c < CHUNK - sh, pltpu.roll(rc, CHUNK - sh, 0), 0.0)
                pre = pre + jnp.where(pos_c >= sh, pltpu.roll(pre, sh, 0), 0.0)
            across = jnp.sum(st_s[c] * rt, axis=0, keepdims=True) * egl
            dgl = rc + (pre - pc) + across
            hf = hf_ref[rows, :]
            sig = _sigmoid(hf)
            f = lb + (1.0 - lb) * sig
            df = dgl / f - dk
            dhf_ref[rows, :] = (df * (1.0 - lb) * sig * (1.0 - sig)).astype(BF16)
            hq = hq_ref[rows, :]
            dhq_ref[rows, :] = (dq * _dsilu(hq, _sigmoid(hq))).astype(BF16)
            dhi_ref[rows, :] = dv.astype(BF16)
            rt_new = rt * egl + lax.dot_general(do.astype(BF16), (q * jnp.exp(gl)).astype(BF16), TN,
                                                preferred_element_type=F32)
            return (rt_new, dlb + jnp.sum(df * (1.0 - sig), axis=0, keepdims=True))

        _, dlb = _unrolled_loop(n_chunks, bwd_chunk,
                                (jnp.zeros((HEAD_DIM, HEAD_DIM), F32), jnp.zeros((1, HEAD_DIM), F32)))
        dl0 = lb * (1.0 - lb) * dlb
        dlog_ref[0:1, :] = dl0
        dlog_ref[1:2, :] = -dl0

    wide = HEADS * HEAD_DIM
    return _host_call(
        body, 8, 6, fused, _head_first_last, name="hgrn_bwd", grid=(HEADS,),
        in_specs=[_head_col(0), _head_col(HEADS), _head_col(2 * HEADS), _head_col(3 * HEADS),
                  pl.BlockSpec((2, HEAD_DIM), lambda h: (0, h)), pl.BlockSpec((1, HEAD_DIM), lambda h: (0, 0)),
                  _head_col(0), _head_col(0)],
        out_specs=[_head_col(0)] * 4 + [pl.BlockSpec((2, HEAD_DIM), lambda h: (0, h)),
                                        pl.BlockSpec((1, HEAD_DIM), lambda h: (0, 0))],
        out_shape=[jax.ShapeDtypeStruct((SEQ, wide), BF16)] * 4 + [jax.ShapeDtypeStruct((2, wide), F32),
                                                                    jax.ShapeDtypeStruct((1, HEAD_DIM), F32)],
        scratch_shapes=[pltpu.VMEM((SEQ, HEAD_DIM), F32)] * 7 + [pltpu.VMEM((n_chunks, HEAD_DIM, HEAD_DIM), F32)],
        sem=("arbitrary",),
        operands=[proj, proj, proj, proj, lb_logits, norm_w, o_pre, d_aout] + list(fused_arrays))


Q_TILE = 256
ATT_SCALE = HEAD_DIM ** -0.5
ATT_OFF = 4 * HEADS


def _qk_prep(proj, q_w, k_w):
    def body(aq_ref, ak_ref, av_ref, qw_ref, kw_ref, qn_ref, kn_ref, v_ref):
        aq = aq_ref[...]
        ak = ak_ref[...]
        qn_ref[...] = (aq * lax.rsqrt(jnp.mean(aq * aq, axis=-1, keepdims=True) + EPS) * qw_ref[...]).astype(BF16)
        kn_ref[...] = (ak * lax.rsqrt(jnp.mean(ak * ak, axis=-1, keepdims=True) + EPS) * kw_ref[...]).astype(BF16)
        v_ref[...] = av_ref[...].astype(BF16)

    wide = HEADS * HEAD_DIM
    vec = pl.BlockSpec((1, HEAD_DIM), lambda h: (0, 0))
    return pl.pallas_call(
        body, name="qk_prep", grid=(HEADS,),
        in_specs=[_head_col(ATT_OFF), _head_col(ATT_OFF + HEADS), _head_col(ATT_OFF + 2 * HEADS), vec, vec],
        out_specs=[_head_col(0)] * 3, out_shape=[jax.ShapeDtypeStruct((SEQ, wide), BF16)] * 3,
        compiler_params=_params(("parallel",)))(proj, proj, proj, q_w, k_w)


def _alibi_slopes():
    slopes = jnp.exp2(-8.0 * jnp.arange(1, HEADS + 1, dtype=F32) / HEADS)
    return jnp.broadcast_to(slopes[:, None, None], (HEADS, 1, HEAD_DIM))


SLOPE_SPEC = pl.BlockSpec((None, 1, HEAD_DIM), lambda h, i: (h, 0, 0))


def _att_scores(q, k, slope, i):
    s = lax.dot_general(q, k, NT, preferred_element_type=F32) * ATT_SCALE
    row = lax.broadcasted_iota(jnp.int32, s.shape, 0) + i * Q_TILE
    col = lax.broadcasted_iota(jnp.int32, s.shape, 1)
    dist = row - col
    mult = ((dist <= 128).astype(F32) + (((dist & 3) == 0) & (dist <= 512)).astype(F32)
            + ((dist & 15) == 0).astype(F32))
    mult = jnp.where(dist >= 0, mult, 0.0)
    return s - slope * dist.astype(F32), mult


def _att_first_last():
    h, i = pl.program_id(0), pl.program_id(1)
    return (h == 0) & (i == 0), (h == HEADS - 1) & (i == SEQ // Q_TILE - 1)


def _attn_fwd(qn, kn, vb, fused=None, fused_arrays=()):
    def body(q_ref, k_ref, v_ref, sl_ref, o_ref, lse_ref):
        i = pl.program_id(1)
        sb, mult = _att_scores(q_ref[...], k_ref[...], sl_ref[0:1, 0:1], i)
        sm = jnp.where(mult > 0.0, sb, -1e30)
        m = jnp.max(sm, axis=-1, keepdims=True)
        p = jnp.exp(sm - m) * mult
        l = jnp.sum(p, axis=-1, keepdims=True)
        o_ref[...] = jnp.dot(p.astype(BF16), v_ref[...], preferred_element_type=F32) / l
        lse_ref[...] = m + jnp.log(l)

    wide = HEADS * HEAD_DIM
    qt = pl.BlockSpec((Q_TILE, HEAD_DIM), lambda h, i: (i, h))
    full = pl.BlockSpec((SEQ, HEAD_DIM), lambda h, i: (0, h))
    return _host_call(
        body, 4, 2, fused, _att_first_last, name="attn_fwd", grid=(HEADS, SEQ // Q_TILE),
        in_specs=[qt, full, full, SLOPE_SPEC],
        out_specs=[qt, pl.BlockSpec((None, Q_TILE, 1), lambda h, i: (h, i, 0))],
        out_shape=[jax.ShapeDtypeStruct((SEQ, wide), F32), jax.ShapeDtypeStruct((HEADS, SEQ, 1), F32)],
        scratch_shapes=[], sem=("parallel", "parallel"),
        operands=[qn, kn, vb, _alibi_slopes()] + list(fused_arrays))


def _attn_bwd(qn, kn, vb, o, lse, d_mix, fused=None, fused_arrays=()):
    def body(q_ref, k_ref, v_ref, o_ref, lse_ref, do_ref, sl_ref, dq_ref, dk_ref, dv_ref):
        i = pl.program_id(1)
        q = q_ref[...]
        k = k_ref[...]
        do = do_ref[...]
        sb, mult = _att_scores(q, k, sl_ref[0:1, 0:1], i)
        p = jnp.where(mult > 0.0, jnp.exp(sb - lse_ref[...]), 0.0) * mult
        dp = lax.dot_general(do.astype(BF16), v_ref[...], NT, preferred_element_type=F32)
        delta = jnp.sum(do * o_ref[...], axis=-1, keepdims=True)
        ds = (p * (dp - delta)).astype(BF16)
        dq_ref[...] = jnp.dot(ds, k, preferred_element_type=F32) * ATT_SCALE

        @pl.when(i == 0)
        def _():
            dk_ref[...] = jnp.zeros_like(dk_ref)
            dv_ref[...] = jnp.zeros_like(dv_ref)

        dk_ref[...] += lax.dot_general(ds, q, TN, preferred_element_type=F32) * ATT_SCALE
        dv_ref[...] += lax.dot_general(p.astype(BF16), do.astype(BF16), TN, preferred_element_type=F32)

    wide = HEADS * HEAD_DIM
    qt = pl.BlockSpec((Q_TILE, HEAD_DIM), lambda h, i: (i, h))
    full = pl.BlockSpec((SEQ, HEAD_DIM), lambda h, i: (0, h))
    return _host_call(
        body, 7, 3, fused, _att_first_last, name="attn_bwd", grid=(HEADS, SEQ // Q_TILE),
        in_specs=[qt, full, full, qt, pl.BlockSpec((None, Q_TILE, 1), lambda h, i: (h, i, 0)),
                  pl.BlockSpec((Q_TILE, HEAD_DIM), lambda h, i: (i, h + HEADS)), SLOPE_SPEC],
        out_specs=[qt, full, full], out_shape=[jax.ShapeDtypeStruct((SEQ, wide), F32)] * 3,
        scratch_shapes=[], sem=("parallel", "arbitrary"),
        operands=[qn, kn, vb, o, lse, d_mix, _alibi_slopes()] + list(fused_arrays))


def _qk_bwd(proj, q_w, k_w, dqn, dkn, dv):
    def body(aq_ref, ak_ref, qw_ref, kw_ref, dqn_ref, dkn_ref, dv_ref, daq_ref, dak_ref, dav_ref, gq_ref, gk_ref):
        h = pl.program_id(0)

        @pl.when(h == 0)
        def _():
            gq_ref[...] = jnp.zeros_like(gq_ref)
            gk_ref[...] = jnp.zeros_like(gk_ref)

        def one(a_ref, w_ref, d_ref, da_ref, g_ref):
            a = a_ref[...]
            d = d_ref[...]
            rs = lax.rsqrt(jnp.mean(a * a, axis=-1, keepdims=True) + EPS)
            ah = a * rs
            g_ref[...] += jnp.sum(d * ah, axis=0, keepdims=True)
            dah = d * w_ref[...]
            da_ref[...] = (rs * (dah - ah * jnp.mean(dah * ah, axis=-1, keepdims=True))).astype(BF16)

        one(aq_ref, qw_ref, dqn_ref, daq_ref, gq_ref)
        one(ak_ref, kw_ref, dkn_ref, dak_ref, gk_ref)
        dav_ref[...] = dv_ref[...].astype(BF16)

    wide = HEADS * HEAD_DIM
    vec = pl.BlockSpec((1, HEAD_DIM), lambda h: (0, 0))
    return pl.pallas_call(
        body, name="qk_bwd", grid=(HEADS,),
        in_specs=[_head_col(ATT_OFF), _head_col(ATT_OFF + HEADS), vec, vec, _head_col(0), _head_col(0), _head_col(0)],
        out_specs=[_head_col(0)] * 3 + [vec, vec],
        out_shape=[jax.ShapeDtypeStruct((SEQ, wide), BF16)] * 3 + [jax.ShapeDtypeStruct((1, HEAD_DIM), F32)] * 2,
        compiler_params=_params(("arbitrary",)))(proj, proj, q_w, k_w, dqn, dkn, dv)


def _pair_sum(name, mine, theirs):
    _, r, c = mine.shape
    tr = r // 2 if r % 16 == 0 else r

    def body(a_ref, b_ref, o_ref):
        o_ref[...] = (a_ref[...].astype(F32) + b_ref[...].astype(F32)).astype(BF16)

    spec = pl.BlockSpec((None, tr, c), lambda q, i: (q, i, 0))
    return pl.pallas_call(body, name=name, grid=(4, r // tr), in_specs=[spec, spec], out_specs=spec,
                          out_shape=jax.ShapeDtypeStruct(mine.shape, BF16),
                          compiler_params=_params(("parallel", "parallel")))(mine, theirs)


def _adamw(name, w, m, v, addends, tr=None):
    r, c = w.shape
    tr = r if tr is None else tr
    n_add = len(addends)
    c1 = 1.0 - ADAM_B1 ** ADAM_STEP
    c2 = 1.0 - ADAM_B2 ** ADAM_STEP

    def body(*refs):
        w_ref, m_ref, v_ref = refs[:3]
        add_refs = refs[3:3 + n_add]
        g_ref, d_ref, nm_ref, nv_ref = refs[3 + n_add:]
        g = add_refs[0][...].astype(F32)
        for a_ref in add_refs[1:]:
            g = g + a_ref[...].astype(F32)
        nm = ADAM_B1 * m_ref[...] + (1.0 - ADAM_B1) * g
        nv = ADAM_B2 * v_ref[...] + (1.0 - ADAM_B2) * (g * g)
        g_ref[...] = g
        nm_ref[...] = nm
        nv_ref[...] = nv
        d_ref[...] = -ADAM_LR * ((nm / c1) / (jnp.sqrt(nv / c2) + ADAM_EPS) + ADAM_WD * w_ref[...])

    spec = pl.BlockSpec((tr, c), lambda i: (i, 0))
    out = jax.ShapeDtypeStruct((r, c), F32)
    return pl.pallas_call(body, name=name, grid=(r // tr,), in_specs=[spec] * (3 + n_add), out_specs=[spec] * 4,
                          out_shape=[out] * 4, compiler_params=_params(("parallel",)))(w, m, v, *addends)


def _sum_devices(gathered):
    _, r, c = gathered.shape

    def body(g_ref, o_ref):
        acc = g_ref[0]
        for d in range(1, N_DEV):
            acc = acc + g_ref[d]
        o_ref[...] = acc

    return pl.pallas_call(body, name="sum_devices", out_shape=jax.ShapeDtypeStruct((r, c), F32))(gathered)


def _pack_rows(vectors, rows):
    flat = jnp.concatenate([v.reshape(-1) for v in vectors])
    return jnp.pad(flat, (0, rows * 128 - flat.shape[0])).reshape(rows, 128)


def _unpack(flat, shapes):
    out, off = [], 0
    for shp in shapes:
        n = 1
        for d in shp:
            n *= d
        out.append(flat[off:off + n].reshape(shp))
        off += n
    return out


def _split_by_core(partials, core):
    halves = [p.reshape(4, 2, p.shape[1], p.shape[2]) for p in partials]
    return ([lax.dynamic_index_in_dim(hv, core, axis=1, keepdims=False) for hv in halves],
            [lax.dynamic_index_in_dim(hv, 1 - core, axis=1, keepdims=False) for hv in halves])


def _device_step(xs, tgt, mod, norm1_w, norm2_w, lb_logits, hg_norm_w, q_norm_w, k_norm_w, conv_w_full, conv_b,
                 win_g, w_out_x, w_up_x, w_down_x, core=None):
    fused = core is not None
    shift1, scale1, gate1, shift2, scale2, gate2 = (mod[k] for k in range(6))

    h, rstd1 = _norm_fwd("norm1_fwd", xs, norm1_w, scale1, shift1)
    proj = _mm_blocked_rhs("mm_in", h, win_g)
    if fused:
        (a_out, o_pre), (wup_g,) = _hgrn_fwd(proj, lb_logits, hg_norm_w, _FusedCopies("gather", [w_up_x]), [w_up_x])
        wup_g, = _forward_to_sibling("allgather_up_stage2", [wup_g])
    else:
        (a_out, o_pre), _ = _hgrn_fwd(proj, lb_logits, hg_norm_w)
        wup_g = w_up_x
    qn, kn, vb = _qk_prep(proj, q_norm_w, k_norm_w)
    if fused:
        (att_o, lse), (wout_g, wdown_g) = _attn_fwd(qn, kn, vb, _FusedCopies("gather", [w_out_x, w_down_x]),
                                                    [w_out_x, w_down_x])
        wout_g, wdown_g = _forward_to_sibling("allgather_out_down_stage2", [wout_g, wdown_g])
        wout_full, wdown_full = wout_g.reshape(D_MODEL, D_MODEL), wdown_g.reshape(D_FF, D_MODEL)
    else:
        (att_o, lse), _ = _attn_fwd(qn, kn, vb)
        wout_full, wdown_full = w_out_x, w_down_x
    mixin = jnp.concatenate([a_out, att_o.astype(BF16)], axis=1)
    mix = _mm_plain("mm_out", mixin, wout_full, NN, 512, 1024, F32)
    x1, h2, rstd2 = _norm_fwd("norm2_fwd", xs, norm2_w, scale2, shift2, resid=mix, gate=gate1)
    u = _mm_blocked_rhs("mm_up", h2, wup_g)
    y = _conv_gate_fwd(u, conv_w_full, conv_b)
    ffn = _mm_plain("mm_down", y, wdown_full, NN, 512, 512, F32)
    loss_v, dout, dffn, dgate2 = _loss_head(x1, ffn, gate2, tgt)

    dy = _mm_plain("mm_down_dx", dffn, wdown_full, NT, 512, UP_BLK, F32)
    gw_down = _mm_plain("mm_down_dw", y, dffn, TN, UP_BLK, 1024, BF16)
    da, dg, gconv_w, gconv_b = _conv_gate_bwd(u, dy, conv_w_full, conv_b)
    du = jnp.concatenate([da, dg], axis=1)
    dh2 = _mm_blocked_rhs_t("mm_up_dx", du, wup_g)
    gw_up = _mm_wgrad_blocked("mm_up_dw", h2, du)
    dx1, dmix, dshift2, dscale2, gnorm2, dgate1 = _norm_bwd(
        "norm2_bwd", dh2, x1, rstd2, norm2_w, scale2, dout, mix=mix, gate=gate1)
    dmixin = _mm_plain("mm_out_dx", dmix, wout_full, NT, 512, 1024, F32)
    gw_out = _mm_plain("mm_out_dw", mixin, dmix, TN, 512, 1024, BF16)
    if fused:
        mine, theirs = _split_by_core(
            [gw_up, gw_out.reshape(N_DEV, OUT_BLK, D_MODEL), gw_down.reshape(N_DEV, FF_BLK, D_MODEL)], core)
        from_sibling = _exchange_sibling("grad_exchange_sibling_a", theirs)
        cs_up, cs_out, cs_down = [_pair_sum(f"grad_pair_sum_{k}", a, b)
                                  for k, (a, b) in enumerate(zip(mine, from_sibling))]
        (dhq, dhf, dhi, dhg, glog, ghg), (fc_up, fc_out) = _hgrn_bwd(
            proj, lb_logits, hg_norm_w, o_pre, dmixin, _FusedCopies("chips", [cs_up, cs_out]), [cs_up, cs_out])
        (dqn, dkn, dvv), (fc_down,) = _attn_bwd(qn, kn, vb, att_o, lse, dmixin,
                                                _FusedCopies("chips", [cs_down]), [cs_down])
    else:
        (dhq, dhf, dhi, dhg, glog, ghg), _ = _hgrn_bwd(proj, lb_logits, hg_norm_w, o_pre, dmixin)
        (dqn, dkn, dvv), _ = _attn_bwd(qn, kn, vb, att_o, lse, dmixin)
    daq, dak, dav, gqw, gkw = _qk_bwd(proj, q_norm_w, k_norm_w, dqn, dkn, dvv)
    dproj = jnp.concatenate([dhq, dhf, dhi, dhg, daq, dak, dav], axis=1)
    dh = _mm_blocked_rhs_t("mm_in_dx", dproj, win_g)
    gw_in = _mm_wgrad_blocked("mm_in_dw", h, dproj)
    grad_x, dshift1, dscale1, gnorm1 = _norm_bwd("norm1_bwd", dh, xs, rstd1, norm1_w, scale1, dx1)
    gmod = jnp.concatenate([dshift1, dscale1, dgate1, dshift2, dscale2, dgate2], axis=1)
    if fused:
        mine, theirs = _split_by_core([gw_in], core)
        from_sibling = _exchange_sibling("grad_exchange_sibling_b", theirs)
        cs_in = _pair_sum("grad_pair_sum_in", mine[0], from_sibling[0])
        fc_in, = _exchange_chips("grad_exchange_chips_in", [cs_in])
        large = [(cs_in, fc_in), (cs_out, fc_out), (cs_up, fc_up), (cs_down, fc_down)]
    else:
        large = [gw_in, gw_out, gw_up, gw_down]
    return (loss_v, grad_x, gmod, gnorm1, gnorm2, glog, ghg, gqw, gkw, gconv_b, gconv_w, *large)


def kernel(x, c, w_ada, b_ada, norm1_w, w_in, lb_logits, hg_norm_w, q_norm_w, k_norm_w, w_out, norm2_w, w_up, conv_w, conv_b, w_down, loss_target, m_w_ada, m_b_ada, m_norm1_w, m_w_in, m_lb_logits, m_hg_norm_w, m_q_norm_w, m_k_norm_w, m_w_out, m_norm2_w, m_w_up, m_conv_w, m_conv_b, m_w_down, v_w_ada, v_b_ada, v_norm1_w, v_w_in, v_lb_logits, v_hg_norm_w, v_q_norm_w, v_k_norm_w, v_w_out, v_norm2_w, v_w_up, v_conv_w, v_conv_b, v_w_down):
    ix, iy, ic = lax.axis_index("x"), lax.axis_index("y"), lax.axis_index("c")
    me = 4 * ix + 2 * iy + ic
    my_chip = 2 * ix + iy

    xs = x[0]
    tgt = loss_target[0]

    win_g, = _allgather_weights([w_in[0].astype(BF16)])

    c_all = _allgather_vmem(c.reshape(8, D_MODEL // 8), "allgather_c").reshape(N_DEV, D_MODEL)
    b_blk = lax.dynamic_slice_in_dim(b_ada, me * ADA_BLK, ADA_BLK, axis=1)
    mod_cols = _ada_fwd(c_all, w_ada[0], b_blk)
    mod_all = _allgather_vmem(mod_cols, "allgather_mod").reshape(N_DEV, N_DEV, ADA_BLK)
    mod = lax.dynamic_index_in_dim(mod_all, me, axis=1, keepdims=False).reshape(6, 1, D_MODEL)

    conv_w_all = _allgather_vmem(_pack_rows([conv_w[0]], 24), "allgather_conv_w").reshape(N_DEV, 24 * 128)
    conv_w_full = conv_w_all[:, :3 * FF_BLK].reshape(N_DEV, 3, FF_BLK).transpose(1, 0, 2).reshape(3, D_FF)

    (loss_v, grad_x, gmod, gnorm1, gnorm2, glog, ghg, gqw, gkw, gconv_b, gconv_w,
     rs_in, rs_out, rs_up, rs_down) = _device_step(
        xs, tgt, mod, norm1_w, norm2_w, lb_logits, hg_norm_w, q_norm_w, k_norm_w, conv_w_full, conv_b,
        win_g, w_out[0].astype(BF16), w_up[0].astype(BF16), w_down[0].astype(BF16), core=ic)
    loss = lax.psum(loss_v[0, 0], AXES)

    small_shapes = [(1, 6 * D_MODEL), (1, D_MODEL), (1, D_MODEL), (2, HEADS * HEAD_DIM), (1, HEAD_DIM),
                    (1, HEAD_DIM), (1, HEAD_DIM), (1, D_FF), (3, D_FF)]
    small = [gmod, gnorm1, gnorm2, glog, ghg, gqw, gkw, gconv_b, gconv_w]
    n_small = sum(a.size for a in small)
    rows = -(-n_small // 1024) * 8
    gathered = _allgather_vmem(_pack_rows(small, rows), "allgather_small").reshape(N_DEV, rows, 128)
    summed = _sum_devices(gathered).reshape(-1)
    (g_b_ada, g_norm1, g_norm2, g_lb, g_hg, g_q, g_k, g_conv_b, g_conv_w_full) = _unpack(summed, small_shapes)
    g_conv_w = lax.dynamic_slice_in_dim(g_conv_w_full, me * FF_BLK, FF_BLK, axis=1)

    gmod_all = gathered[:, :6 * D_MODEL // 128, :].reshape(N_DEV, 6 * D_MODEL)
    gmod_cols = lax.dynamic_slice_in_dim(gmod_all, me * ADA_BLK, ADA_BLK, axis=1)
    g_w_ada_raw = _ada_wgrad(c_all, gmod_cols)

    def big_update(name, w, m, v, rs, tr):
        chip_sums, recv = rs
        own = lax.dynamic_index_in_dim(chip_sums, my_chip, axis=0, keepdims=False)
        return _adamw(name, w[0], m[0], v[0], [own, recv[0], recv[1], recv[2]], tr=tr)

    r_in = big_update("adamw_w_in", w_in, m_w_in, v_w_in, rs_in, 256)
    r_out = big_update("adamw_w_out", w_out, m_w_out, v_w_out, rs_out, 128)
    r_up = big_update("adamw_w_up", w_up, m_w_up, v_w_up, rs_up, 256)
    r_down = big_update("adamw_w_down", w_down, m_w_down, v_w_down, rs_down, 176)
    r_ada = _adamw("adamw_w_ada", w_ada[0], m_w_ada[0], v_w_ada[0], [g_w_ada_raw], tr=256)
    r_convw = _adamw("adamw_conv_w", conv_w[0], m_conv_w[0], v_conv_w[0], [g_conv_w])

    rep_shapes = [(1, 6 * D_MODEL), (1, D_MODEL), (1, D_MODEL), (2, HEADS * HEAD_DIM), (1, HEAD_DIM),
                  (1, HEAD_DIM), (1, HEAD_DIM), (1, D_FF)]
    rep_rows = -(-sum(a * b for a, b in rep_shapes) // 1024) * 8
    pack = lambda arrs: _pack_rows(arrs, rep_rows)
    rep = _adamw("adamw_small",
                 pack([b_ada, norm1_w, norm2_w, lb_logits, hg_norm_w, q_norm_w, k_norm_w, conv_b]),
                 pack([m_b_ada, m_norm1_w, m_norm2_w, m_lb_logits, m_hg_norm_w, m_q_norm_w, m_k_norm_w, m_conv_b]),
                 pack([v_b_ada, v_norm1_w, v_norm2_w, v_lb_logits, v_hg_norm_w, v_q_norm_w, v_k_norm_w, v_conv_b]),
                 [pack([g_b_ada, g_norm1, g_norm2, g_lb, g_hg, g_q, g_k, g_conv_b])])
    rep = [_unpack(r.reshape(-1), rep_shapes) for r in rep]

    def big(r):
        return [a[None] for a in r]

    order = {"w_ada": big(r_ada), "b_ada": [r[0] for r in rep], "norm1_w": [r[1] for r in rep],
             "w_in": big(r_in), "lb_logits": [r[3] for r in rep], "hg_norm_w": [r[4] for r in rep],
             "q_norm_w": [r[5] for r in rep], "k_norm_w": [r[6] for r in rep], "w_out": big(r_out),
             "norm2_w": [r[2] for r in rep], "w_up": big(r_up), "conv_w": big(r_convw),
             "conv_b": [r[7] for r in rep], "w_down": big(r_down)}
    names = ["w_ada", "b_ada", "norm1_w", "w_in", "lb_logits", "hg_norm_w", "q_norm_w", "k_norm_w", "w_out",
             "norm2_w", "w_up", "conv_w", "conv_b", "w_down"]
    outs = [loss, grad_x[None]]
    for kind in range(4):
        outs += [order[n][kind] for n in names]
    return tuple(outs)
```

```python
import functools

import jax
import jax.numpy as jnp
from jax import lax
from jax.experimental import pallas as pl
from jax.experimental.pallas import tpu as pltpu

F32 = jnp.float32
BF16 = jnp.bfloat16

N_DEV = 8
SEQ = 2048
D_MODEL = 2048
HEADS = 8
HEAD_DIM = 128
IN_COLS = 7168
IN_BLK = IN_COLS // N_DEV
D_FF = 5632
UP_BLK = 2 * D_FF // N_DEV
FF_BLK = D_FF // N_DEV
ADA_BLK = 6 * D_MODEL // N_DEV
OUT_BLK = D_MODEL // N_DEV
EPS = 1e-6
CHUNK = 16
ROW_TILE = 256
V7X_VMEM_LIMIT = 56 * 1024 * 1024

ADAM_LR = 0.001
ADAM_B1 = 0.9
ADAM_B2 = 0.999
ADAM_EPS = 1e-08
ADAM_WD = 0.01
ADAM_STEP = 10

NN = (((1,), (0,)), ((), ()))
NT = (((1,), (1,)), ((), ()))
TN = (((0,), (0,)), ((), ()))
MESH = pl.DeviceIdType.MESH
AXES = ("x", "y", "c")


def _params(sem=None, vmem=V7X_VMEM_LIMIT):
    return pltpu.CompilerParams(dimension_semantics=sem, vmem_limit_bytes=vmem)


def _sigmoid(x):
    return 1.0 / (1.0 + jnp.exp(-x))


def _dsilu(x, s):
    return s * (1.0 + x * (1.0 - s))


def _lane_sum(x, ones_bf16):
    hi = x.astype(BF16)
    lo = (x - hi.astype(F32)).astype(BF16)
    return (jnp.dot(hi, ones_bf16, preferred_element_type=F32)
            + jnp.dot(lo, ones_bf16, preferred_element_type=F32))


def _mesh_pos():
    return lax.axis_index("x"), lax.axis_index("y"), lax.axis_index("c")


def _allgather_vmem(x_blk, name):
    m_per, n = x_blk.shape

    def body(x_ref, out_ref, send_sems, recv_sems, local_sem):
        x, y, c = _mesh_pos()
        me, sibling = (x, y, c), (x, y, 1 - c)
        chips = [(1 - x, y), (x, 1 - y), (1 - x, 1 - y)]

        def rows(px, py, pc):
            return out_ref.at[pl.ds((4 * px + 2 * py + pc) * m_per, m_per), :]

        def copy(k, block, to, src=None):
            return pltpu.make_async_remote_copy(
                src_ref=rows(*block) if src is None else src, dst_ref=rows(*block),
                send_sem=send_sems.at[k], recv_sem=recv_sems.at[k], device_id=to, device_id_type=MESH)

        mine = pltpu.make_async_copy(x_ref, rows(*me), local_sem)
        mine.start()
        first = [copy(0, me, sibling, src=x_ref)]
        first += [copy(1 + j, me, (*chip, c), src=x_ref) for j, chip in enumerate(chips)]
        for cp in first:
            cp.start()
        passed = [copy(4 + j, (*chip, c), sibling) for j, chip in enumerate(chips)]
        for j, chip in enumerate(chips):
            copy(1 + j, (*chip, c), me).wait_recv()
            passed[j].start()
        copy(0, sibling, me).wait_recv()
        for j, chip in enumerate(chips):
            copy(4 + j, (*chip, 1 - c), me).wait_recv()
        for cp in first + passed:
            cp.wait_send()
        mine.wait()

    return pl.pallas_call(
        body, name=name,
        out_shape=jax.ShapeDtypeStruct((N_DEV * m_per, n), x_blk.dtype),
        in_specs=[pl.BlockSpec(memory_space=pltpu.VMEM)],
        out_specs=pl.BlockSpec(memory_space=pltpu.VMEM),
        scratch_shapes=[pltpu.SemaphoreType.DMA((7,)), pltpu.SemaphoreType.DMA((7,)), pltpu.SemaphoreType.DMA],
    )(x_blk)


def _allgather_weights(blocks):
    n_arr = len(blocks)

    def body(*refs):
        ins, outs = refs[:n_arr], refs[n_arr:2 * n_arr]
        send_sems, recv_sems, local_sems = refs[2 * n_arr:]
        x, y, c = _mesh_pos()
        me, sibling = (x, y, c), (x, y, 1 - c)
        chips = [(1 - x, y), (x, 1 - y), (1 - x, 1 - y)]

        def slot(a, px, py, pc):
            return outs[a].at[4 * px + 2 * py + pc]

        def copy(a, k, block, to, src=None):
            return pltpu.make_async_remote_copy(
                src_ref=slot(a, *block) if src is None else src, dst_ref=slot(a, *block),
                send_sem=send_sems.at[a, k], recv_sem=recv_sems.at[a, k], device_id=to, device_id_type=MESH)

        mine, first, passed = [], [], []
        for a in range(n_arr):
            cp = pltpu.make_async_copy(ins[a], slot(a, *me), local_sems.at[a])
            cp.start()
            mine.append(cp)
            first.append(copy(a, 0, me, sibling, src=ins[a]))
            first += [copy(a, 1 + j, me, (*chip, c), src=ins[a]) for j, chip in enumerate(chips)]
        for cp in first:
            cp.start()
        for j, chip in enumerate(chips):
            for a in range(n_arr):
                copy(a, 1 + j, (*chip, c), me).wait_recv()
                cp = copy(a, 4 + j, (*chip, c), sibling)
                cp.start()
                passed.append(cp)
        for a in range(n_arr):
            copy(a, 0, sibling, me).wait_recv()
            for j, chip in enumerate(chips):
                copy(a, 4 + j, (*chip, 1 - c), me).wait_recv()
        for cp in first + passed:
            cp.wait_send()
        for cp in mine:
            cp.wait()

    hbm = pl.BlockSpec(memory_space=pltpu.HBM)
    return pl.pallas_call(
        body, name="allgather_weights",
        out_shape=[jax.ShapeDtypeStruct((N_DEV,) + b.shape, b.dtype) for b in blocks],
        in_specs=[hbm] * n_arr, out_specs=[hbm] * n_arr,
        scratch_shapes=[pltpu.SemaphoreType.DMA((n_arr, 7)), pltpu.SemaphoreType.DMA((n_arr, 7)),
                        pltpu.SemaphoreType.DMA((n_arr,))],
    )(*blocks)


HBM_SPEC = pl.BlockSpec(memory_space=pltpu.HBM)


class _FusedCopies:
    def __init__(self, kind, arrays):
        n = len(arrays)
        self.n = n
        self.kind = kind
        self.in_specs = [HBM_SPEC] * n
        self.out_specs = [HBM_SPEC] * n
        if kind == "gather":
            self.out_shape = [jax.ShapeDtypeStruct((N_DEV,) + a.shape, a.dtype) for a in arrays]
            self.scratch_shapes = [pltpu.SemaphoreType.DMA((n, 4)), pltpu.SemaphoreType.DMA((n, 4)),
                                   pltpu.SemaphoreType.DMA((n,))]
        else:
            self.out_shape = [jax.ShapeDtypeStruct((3,) + a.shape[1:], a.dtype) for a in arrays]
            self.scratch_shapes = [pltpu.SemaphoreType.DMA((n, 3)), pltpu.SemaphoreType.DMA((n, 3))]
        self.n_scratch = len(self.scratch_shapes)

    def copies(self, ins, outs, sems):
        x, y, c = _mesh_pos()
        chips = [(1 - x, y), (x, 1 - y), (1 - x, 1 - y)]
        starts, waits = [], []
        if self.kind == "gather":
            send_sems, recv_sems, local_sems = sems
            me = (x, y, c)
            peers = [(x, y, 1 - c)] + [(px, py, c) for px, py in chips]

            def slot(a, pos):
                return outs[a].at[4 * pos[0] + 2 * pos[1] + pos[2]]

            def remote(a, k, lands_from):
                return pltpu.make_async_remote_copy(
                    src_ref=ins[a], dst_ref=slot(a, lands_from), send_sem=send_sems.at[a, k],
                    recv_sem=recv_sems.at[a, k], device_id=peers[k], device_id_type=MESH)

            for a in range(self.n):
                local = pltpu.make_async_copy(ins[a], slot(a, me), local_sems.at[a])
                starts.append(local)
                waits.append(local)
                for k in range(4):
                    starts.append(remote(a, k, me))
                    waits.append(remote(a, k, peers[k]))
        else:
            send_sems, recv_sems = sems
            for a in range(self.n):
                for j, (px, py) in enumerate(chips):
                    cp = pltpu.make_async_remote_copy(
                        src_ref=ins[a].at[2 * px + py], dst_ref=outs[a].at[j], send_sem=send_sems.at[a, j],
                        recv_sem=recv_sems.at[a, j], device_id=(px, py, c), device_id_type=MESH)
                    starts.append(cp)
                    waits.append(cp)
        return starts, waits


def _host_body(body, n_in, n_out, fused, first_last):
    if fused is None:
        return body
    n = fused.n

    def wrapped(*refs):
        core_in, f_in = refs[:n_in], refs[n_in:n_in + n]
        core_out = refs[n_in + n:n_in + n + n_out]
        f_out = refs[n_in + n + n_out:n_in + 2 * n + n_out]
        rest = refs[n_in + 2 * n + n_out:]
        core_scratch, f_sems = rest[:len(rest) - fused.n_scratch], rest[len(rest) - fused.n_scratch:]
        starts, waits = fused.copies(f_in, f_out, f_sems)
        first, last = first_last()

        @pl.when(first)
        def _():
            for cp in starts:
                cp.start()

        body(*core_in, *core_out, *core_scratch)

        @pl.when(last)
        def _():
            for cp in waits:
                cp.wait()

    return wrapped


def _host_call(body, n_in, n_out, fused, first_last, *, name, grid, in_specs, out_specs, out_shape, scratch_shapes,
               sem, operands):
    if fused is not None:
        in_specs = list(in_specs) + fused.in_specs
        out_specs = list(out_specs) + fused.out_specs
        out_shape = list(out_shape) + fused.out_shape
        scratch_shapes = list(scratch_shapes) + fused.scratch_shapes
        sem = tuple("arbitrary" for _ in sem)
    res = pl.pallas_call(_host_body(body, n_in, n_out, fused, first_last), name=name, grid=grid, in_specs=in_specs,
                         out_specs=out_specs, out_shape=out_shape, scratch_shapes=scratch_shapes,
                         compiler_params=_params(sem))(*operands)
    return list(res[:n_out]), list(res[n_out:])


def _forward_to_sibling(name, gathered):
    n_arr = len(gathered)

    def body(*refs):
        ins, outs = refs[:n_arr], refs[n_arr:2 * n_arr]
        send_sems, recv_sems = refs[2 * n_arr:]
        x, y, c = _mesh_pos()
        chips = [(1 - x, y), (x, 1 - y), (1 - x, 1 - y)]

        def copy(a, j, pc):
            px, py = chips[j]
            s = 4 * px + 2 * py + pc
            return pltpu.make_async_remote_copy(
                src_ref=ins[a].at[s], dst_ref=outs[a].at[s], send_sem=send_sems.at[a, j], recv_sem=recv_sems.at[a, j],
                device_id=(x, y, 1 - c), device_id_type=MESH)

        for a in range(n_arr):
            for j in range(3):
                copy(a, j, c).start()
        for a in range(n_arr):
            for j in range(3):
                copy(a, j, 1 - c).wait_recv()
                copy(a, j, c).wait_send()

    return pl.pallas_call(
        body, name=name,
        out_shape=[jax.ShapeDtypeStruct(g.shape, g.dtype) for g in gathered],
        in_specs=[HBM_SPEC] * n_arr, out_specs=[HBM_SPEC] * n_arr,
        input_output_aliases={a: a for a in range(n_arr)},
        scratch_shapes=[pltpu.SemaphoreType.DMA((n_arr, 3)), pltpu.SemaphoreType.DMA((n_arr, 3))],
    )(*gathered)


def _exchange_sibling(name, parts):
    n_arr = len(parts)

    def body(*refs):
        ins, outs = refs[:n_arr], refs[n_arr:2 * n_arr]
        send_sems, recv_sems = refs[2 * n_arr:]
        x, y, c = _mesh_pos()
        copies = [pltpu.make_async_remote_copy(
            src_ref=ins[a], dst_ref=outs[a], send_sem=send_sems.at[a], recv_sem=recv_sems.at[a],
            device_id=(x, y, 1 - c), device_id_type=MESH) for a in range(n_arr)]
        for cp in copies:
            cp.start()
        for cp in copies:
            cp.wait_recv()
        for cp in copies:
            cp.wait_send()

    hbm = pl.BlockSpec(memory_space=pltpu.HBM)
    return pl.pallas_call(
        body, name=name,
        out_shape=[jax.ShapeDtypeStruct(p.shape, p.dtype) for p in parts],
        in_specs=[hbm] * n_arr, out_specs=[hbm] * n_arr,
        scratch_shapes=[pltpu.SemaphoreType.DMA((n_arr,)), pltpu.SemaphoreType.DMA((n_arr,))],
    )(*parts)


def _exchange_chips(name, chip_sums):
    n_arr = len(chip_sums)

    def body(*refs):
        ins, outs = refs[:n_arr], refs[n_arr:2 * n_arr]
        send_sems, recv_sems = refs[2 * n_arr:]
        x, y, c = _mesh_pos()
        chips = [(1 - x, y), (x, 1 - y), (1 - x, 1 - y)]
        copies = []
        for a in range(n_arr):
            for j, (px, py) in enumerate(chips):
                copies.append(pltpu.make_async_remote_copy(
                    src_ref=ins[a].at[2 * px + py], dst_ref=outs[a].at[j],
                    send_sem=send_sems.at[a, j], recv_sem=recv_sems.at[a, j],
                    device_id=(px, py, c), device_id_type=MESH))
        for cp in copies:
            cp.start()
        for cp in copies:
            cp.wait_recv()
        for cp in copies:
            cp.wait_send()

    hbm = pl.BlockSpec(memory_space=pltpu.HBM)
    return pl.pallas_call(
        body, name=name,
        out_shape=[jax.ShapeDtypeStruct((3,) + p.shape[1:], p.dtype) for p in chip_sums],
        in_specs=[hbm] * n_arr, out_specs=[hbm] * n_arr,
        scratch_shapes=[pltpu.SemaphoreType.DMA((n_arr, 3)), pltpu.SemaphoreType.DMA((n_arr, 3))],
    )(*chip_sums)


def _matmul(name, a, b, dims, grid, a_spec, b_spec, o_spec, out_shape, acc_axis=None, fused=None, fused_arrays=()):
    def body(a_ref, b_ref, o_ref):
        r = lax.dot_general(a_ref[...], b_ref[...], dims, preferred_element_type=F32)
        if acc_axis is None:
            o_ref[...] = r.astype(o_ref.dtype)
        else:
            k = pl.program_id(acc_axis)

            @pl.when(k == 0)
            def _():
                o_ref[...] = r

            @pl.when(k > 0)
            def _():
                o_ref[...] += r

    sem = tuple("arbitrary" if i == acc_axis else "parallel" for i in range(len(grid)))
    if fused is None:
        return pl.pallas_call(body, name=name, grid=grid, in_specs=[a_spec, b_spec], out_specs=o_spec,
                              out_shape=out_shape, compiler_params=_params(sem))(a, b)

    def first_last():
        first = last = None
        for ax, n in enumerate(grid):
            f, l = pl.program_id(ax) == 0, pl.program_id(ax) == n - 1
            first, last = (f, l) if first is None else (first & f, last & l)
        return first, last

    (out,), extra = _host_call(body, 2, 1, fused, first_last, name=name, grid=grid, in_specs=[a_spec, b_spec],
                               out_specs=[o_spec], out_shape=[out_shape], scratch_shapes=[], sem=sem,
                               operands=[a, b] + list(fused_arrays))
    return out, extra


def _mm_blocked_rhs(name, a, w_g, tm=512, fused=None, fused_arrays=()):
    m, k = a.shape
    nb = w_g.shape[2]
    return _matmul(name, a, w_g, NN, (N_DEV, m // tm),
                   pl.BlockSpec((tm, k), lambda j, i: (i, 0)),
                   pl.BlockSpec((None, k, nb), lambda j, i: (j, 0, 0)),
                   pl.BlockSpec((tm, nb), lambda j, i: (i, j)),
                   jax.ShapeDtypeStruct((m, N_DEV * nb), F32), fused=fused, fused_arrays=fused_arrays)


def _mm_blocked_rhs_t(name, a, w_g, tm=512):
    m = a.shape[0]
    n, nb = w_g.shape[1], w_g.shape[2]
    return _matmul(name, a, w_g, NT, (m // tm, N_DEV),
                   pl.BlockSpec((tm, nb), lambda i, j: (i, j)),
                   pl.BlockSpec((None, n, nb), lambda i, j: (j, 0, 0)),
                   pl.BlockSpec((tm, n), lambda i, j: (i, 0)),
                   jax.ShapeDtypeStruct((m, n), F32), acc_axis=1)


def _mm_wgrad_blocked(name, act, dcols, tk=512):
    t, k = act.shape
    nb = dcols.shape[1] // N_DEV
    return _matmul(name, act, dcols, TN, (N_DEV, k // tk),
                   pl.BlockSpec((t, tk), lambda j, i: (0, i)),
                   pl.BlockSpec((t, nb), lambda j, i: (0, j)),
                   pl.BlockSpec((None, tk, nb), lambda j, i: (j, i, 0)),
                   jax.ShapeDtypeStruct((N_DEV, k, nb), BF16))


def _mm_plain(name, a, b, dims, tm, tn, out_dtype):
    if dims == NN:
        (m, k), n = a.shape, b.shape[1]
        a_spec = pl.BlockSpec((tm, k), lambda i, j: (i, 0))
        b_spec = pl.BlockSpec((k, tn), lambda i, j: (0, j))
    elif dims == NT:
        (m, k), n = a.shape, b.shape[0]
        a_spec = pl.BlockSpec((tm, k), lambda i, j: (i, 0))
        b_spec = pl.BlockSpec((tn, k), lambda i, j: (j, 0))
    else:
        (k, m), n = a.shape, b.shape[1]
        a_spec = pl.BlockSpec((k, tm), lambda i, j: (0, i))
        b_spec = pl.BlockSpec((k, tn), lambda i, j: (0, j))
    return _matmul(name, a, b, dims, (m // tm, n // tn), a_spec, b_spec,
                   pl.BlockSpec((tm, tn), lambda i, j: (i, j)), jax.ShapeDtypeStruct((m, n), out_dtype))


def _ada_fwd(c_all, w_ada_blk, b_blk):
    def body(c_ref, w_ref, b_ref, o_ref):
        cv = c_ref[...]
        o_ref[...] = jnp.dot(cv * _sigmoid(cv), w_ref[...], preferred_element_type=F32) + b_ref[...]

    tn = 512
    return pl.pallas_call(
        body, name="ada_fwd", grid=(ADA_BLK // tn,),
        in_specs=[pl.BlockSpec((N_DEV, D_MODEL), lambda j: (0, 0)),
                  pl.BlockSpec((D_MODEL, tn), lambda j: (0, j)),
                  pl.BlockSpec((1, tn), lambda j: (0, j))],
        out_specs=pl.BlockSpec((N_DEV, tn), lambda j: (0, j)),
        out_shape=jax.ShapeDtypeStruct((N_DEV, ADA_BLK), F32),
        compiler_params=_params(("parallel",)))(c_all, w_ada_blk, b_blk)


def _ada_wgrad(c_all, gmod_cols):
    def body(c_ref, g_ref, o_ref):
        cv = c_ref[...]
        o_ref[...] = lax.dot_general(cv * _sigmoid(cv), g_ref[...], TN, preferred_element_type=F32)

    tk = 512
    return pl.pallas_call(
        body, name="ada_wgrad", grid=(D_MODEL // tk,),
        in_specs=[pl.BlockSpec((N_DEV, tk), lambda i: (0, i)),
                  pl.BlockSpec((N_DEV, ADA_BLK), lambda i: (0, 0))],
        out_specs=pl.BlockSpec((tk, ADA_BLK), lambda i: (i, 0)),
        out_shape=jax.ShapeDtypeStruct((D_MODEL, ADA_BLK), F32),
        compiler_params=_params(("parallel",)))(c_all, gmod_cols)


def _row_spec(cols=D_MODEL):
    return pl.BlockSpec((ROW_TILE, cols), lambda i: (i, 0))


def _vec_spec(cols=D_MODEL):
    return pl.BlockSpec((1, cols), lambda i: (0, 0))


def _norm_fwd(name, x, w, scale, shift, resid=None, gate=None):
    has_res = resid is not None

    def body(*refs):
        if has_res:
            x_ref, r_ref, g_ref, w_ref, sc_ref, sh_ref, xr_ref, h_ref, rs_ref = refs
            xr = x_ref[...] + g_ref[...] * r_ref[...]
            xr_ref[...] = xr
        else:
            x_ref, w_ref, sc_ref, sh_ref, h_ref, rs_ref = refs
            xr = x_ref[...]
        rs = lax.rsqrt(jnp.mean(xr * xr, axis=-1, keepdims=True) + EPS)
        h = (xr * rs) * w_ref[...] * (1.0 + sc_ref[...]) + sh_ref[...]
        h_ref[...] = h.astype(BF16)
        rs_ref[...] = rs

    s = x.shape[0]
    ins = [x] + ([resid, gate] if has_res else []) + [w, scale, shift]
    in_specs = [_row_spec()] + ([_row_spec(), _vec_spec()] if has_res else []) + [_vec_spec()] * 3
    outs = ([jax.ShapeDtypeStruct((s, D_MODEL), F32)] if has_res else []) + [
        jax.ShapeDtypeStruct((s, D_MODEL), BF16), jax.ShapeDtypeStruct((s, 1), F32)]
    out_specs = ([_row_spec()] if has_res else []) + [_row_spec(), pl.BlockSpec((ROW_TILE, 1), lambda i: (i, 0))]
    return pl.pallas_call(body, name=name, grid=(s // ROW_TILE,), in_specs=in_specs, out_specs=out_specs,
                          out_shape=outs, compiler_params=_params(("parallel",)))(*ins)


def _norm_bwd(name, dh, x, rstd, w, scale, dres, mix=None, gate=None):
    has_mix = mix is not None

    def body(*refs):
        if has_mix:
            (dh_ref, x_ref, rs_ref, w_ref, sc_ref, dr_ref, mix_ref, g_ref,
             dx_ref, dmix_ref, dsh_ref, dsc_ref, dw_ref, dg_ref) = refs
        else:
            dh_ref, x_ref, rs_ref, w_ref, sc_ref, dr_ref, dx_ref, dsh_ref, dsc_ref, dw_ref = refs
        i = pl.program_id(0)
        dhv = dh_ref[...]
        rs = rs_ref[...]
        xn = x_ref[...] * rs
        wv = w_ref[...]
        one_sc = 1.0 + sc_ref[...]
        dxn = dhv * wv * one_sc
        dx = dr_ref[...] + rs * (dxn - xn * jnp.mean(dxn * xn, axis=-1, keepdims=True))
        dx_ref[...] = dx
        sums = [(dsh_ref, dhv), (dsc_ref, dhv * xn * wv), (dw_ref, dhv * one_sc * xn)]
        if has_mix:
            dmix_ref[...] = (dx * g_ref[...]).astype(BF16)
            sums.append((dg_ref, dx * mix_ref[...]))

        @pl.when(i == 0)
        def _():
            for ref, _v in sums:
                ref[...] = jnp.zeros_like(ref)

        for ref, v in sums:
            ref[...] += jnp.sum(v, axis=0, keepdims=True)

    s = x.shape[0]
    ins = [dh, x, rstd, w, scale, dres] + ([mix, gate] if has_mix else [])
    in_specs = ([_row_spec(), _row_spec(), pl.BlockSpec((ROW_TILE, 1), lambda i: (i, 0)), _vec_spec(), _vec_spec(),
                 _row_spec()] + ([_row_spec(), _vec_spec()] if has_mix else []))
    vec = jax.ShapeDtypeStruct((1, D_MODEL), F32)
    outs = ([jax.ShapeDtypeStruct((s, D_MODEL), F32)] + ([jax.ShapeDtypeStruct((s, D_MODEL), BF16)] if has_mix else [])
            + [vec] * (4 if has_mix else 3))
    out_specs = [_row_spec()] + ([_row_spec()] if has_mix else []) + [_vec_spec()] * (4 if has_mix else 3)
    return pl.pallas_call(body, name=name, grid=(s // ROW_TILE,), in_specs=in_specs, out_specs=out_specs,
                          out_shape=outs, compiler_params=_params(("arbitrary",)))(*ins)


def _loss_head(x1, ffn, gate2, target):
    def body(x_ref, f_ref, g_ref, t_ref, loss_ref, dout_ref, dffn_ref, dg_ref):
        i = pl.program_id(0)
        fv = f_ref[...]
        gv = g_ref[...]
        err = x_ref[...] + gv * fv - t_ref[...]
        dout = err * (1.0 / D_MODEL)
        dout_ref[...] = dout
        dffn_ref[...] = (dout * gv).astype(BF16)

        @pl.when(i == 0)
        def _():
            loss_ref[...] = jnp.zeros_like(loss_ref)
            dg_ref[...] = jnp.zeros_like(dg_ref)

        row = jnp.sum(err * err, axis=-1, keepdims=True) * (1.0 / D_MODEL)
        loss_ref[...] += jnp.broadcast_to(0.5 * jnp.sum(row, axis=0, keepdims=True), (1, 128))
        dg_ref[...] += jnp.sum(dout * fv, axis=0, keepdims=True)

    s = x1.shape[0]
    return pl.pallas_call(
        body, name="loss_head", grid=(s // ROW_TILE,),
        in_specs=[_row_spec(), _row_spec(), _vec_spec(), _row_spec()],
        out_specs=[pl.BlockSpec((1, 128), lambda i: (0, 0)), _row_spec(), _row_spec(), _vec_spec()],
        out_shape=[jax.ShapeDtypeStruct((1, 128), F32), jax.ShapeDtypeStruct((s, D_MODEL), F32),
                   jax.ShapeDtypeStruct((s, D_MODEL), BF16), jax.ShapeDtypeStruct((1, D_MODEL), F32)],
        compiler_params=_params(("arbitrary",)))(x1, ffn, gate2, target)


CONV_TILE = 512
N_CONV_TILES = D_FF // CONV_TILE


def _shift_rows(a, k, row):
    n = a.shape[0]
    if k > 0:
        return jnp.where(row >= k, pltpu.roll(a, k, 0), 0.0)
    return jnp.where(row < n + k, pltpu.roll(a, n + k, 0), 0.0)


def _conv_gate_fwd(u, conv_w, conv_b):
    s = u.shape[0]

    def body(a_ref, g_ref, w_ref, b_ref, y_ref):
        a = a_ref[...]
        w = w_ref[...]
        row = lax.broadcasted_iota(jnp.int32, a.shape, 0)
        ac = b_ref[...] + _shift_rows(a, 2, row) * w[0:1] + _shift_rows(a, 1, row) * w[1:2] + a * w[2:3]
        y_ref[...] = (ac * _sigmoid(ac) * g_ref[...]).astype(BF16)

    col = lambda off: pl.BlockSpec((s, CONV_TILE), lambda i: (0, i + off))
    return pl.pallas_call(
        body, name="conv_gate_fwd", grid=(N_CONV_TILES,),
        in_specs=[col(0), col(N_CONV_TILES), pl.BlockSpec((3, CONV_TILE), lambda i: (0, i)),
                  pl.BlockSpec((1, CONV_TILE), lambda i: (0, i))],
        out_specs=col(0), out_shape=jax.ShapeDtypeStruct((s, D_FF), BF16),
        compiler_params=_params(("parallel",)))(u, u, conv_w, conv_b)


def _conv_gate_bwd(u, dy, conv_w, conv_b):
    s = u.shape[0]

    def body(a_ref, g_ref, dy_ref, w_ref, b_ref, da_ref, dg_ref, gw_ref, gb_ref):
        a = a_ref[...]
        w = w_ref[...]
        row = lax.broadcasted_iota(jnp.int32, a.shape, 0)
        a1 = _shift_rows(a, 1, row)
        a2 = _shift_rows(a, 2, row)
        ac = b_ref[...] + a2 * w[0:1] + a1 * w[1:2] + a * w[2:3]
        sg = _sigmoid(ac)
        dyv = dy_ref[...]
        dg_ref[...] = (dyv * (ac * sg)).astype(BF16)
        dac = dyv * g_ref[...] * _dsilu(ac, sg)
        gb_ref[...] = jnp.sum(dac, axis=0, keepdims=True)
        gw_ref[0:1, :] = jnp.sum(dac * a2, axis=0, keepdims=True)
        gw_ref[1:2, :] = jnp.sum(dac * a1, axis=0, keepdims=True)
        gw_ref[2:3, :] = jnp.sum(dac * a, axis=0, keepdims=True)
        da = dac * w[2:3] + _shift_rows(dac, -1, row) * w[1:2] + _shift_rows(dac, -2, row) * w[0:1]
        da_ref[...] = da.astype(BF16)

    col = lambda off: pl.BlockSpec((s, CONV_TILE), lambda i: (0, i + off))
    return pl.pallas_call(
        body, name="conv_gate_bwd", grid=(N_CONV_TILES,),
        in_specs=[col(0), col(N_CONV_TILES), col(0), pl.BlockSpec((3, CONV_TILE), lambda i: (0, i)),
                  pl.BlockSpec((1, CONV_TILE), lambda i: (0, i))],
        out_specs=[col(0), col(0), pl.BlockSpec((3, CONV_TILE), lambda i: (0, i)),
                   pl.BlockSpec((1, CONV_TILE), lambda i: (0, i))],
        out_shape=[jax.ShapeDtypeStruct((s, D_FF), BF16), jax.ShapeDtypeStruct((s, D_FF), BF16),
                   jax.ShapeDtypeStruct((3, D_FF), F32), jax.ShapeDtypeStruct((1, D_FF), F32)],
        compiler_params=_params(("parallel",)))(u, u, dy, conv_w, conv_b)


HG_TILE = 128
CHUNK_UNROLL = 8


def _unrolled_loop(n, body, init):
    def group(i, carry):
        for u in range(CHUNK_UNROLL):
            carry = body(i * CHUNK_UNROLL + u, carry)
        return carry

    return lax.fori_loop(0, n // CHUNK_UNROLL, group, init)


def _head_col(off):
    return pl.BlockSpec((SEQ, HEAD_DIM), lambda h: (0, h + off))


def _hgrn_gates(hq, hf, lb, pos):
    q = hq * _sigmoid(hq)
    sig = _sigmoid(hf)
    f = lb + (1.0 - lb) * sig
    gl = jnp.log(f)
    for sh in (1, 2, 4, 8):
        gl = gl + jnp.where(pos >= sh, pltpu.roll(gl, sh, 0), 0.0)
    return q, sig, f, 1.0 - f, gl


def _lower_bound(lbl):
    return 1.0 / (1.0 + jnp.exp(lbl[1:2, :] - lbl[0:1, :]))


def _head_first_last():
    h = pl.program_id(0)
    return h == 0, h == HEADS - 1


def _hgrn_fwd(proj, lb_logits, norm_w, fused=None, fused_arrays=()):
    n_tiles = SEQ // HG_TILE
    n_chunks = SEQ // CHUNK
    fused_arrays = list(fused_arrays)

    def body(hq_ref, hf_ref, hi_ref, hg_ref, lbl_ref, nw_ref, aout_ref, opre_ref, q_s, k_s, gl_s):
        lb = _lower_bound(lbl_ref[...])
        ones = jnp.ones((HEAD_DIM, HEAD_DIM), BF16)
        pos = lax.broadcasted_iota(jnp.int32, (HG_TILE, HEAD_DIM), 0) % CHUNK

        def tile(i, carry):
            rows = pl.ds(pl.multiple_of(i * HG_TILE, HG_TILE), HG_TILE)
            v = hi_ref[rows, :]
            q, _sig, _f, kk, gl = _hgrn_gates(hq_ref[rows, :], hf_ref[rows, :], lb, pos)
            o = _lane_sum(q * kk, ones) * v
            for d in range(1, CHUNK):
                e = jnp.where(pos >= d, jnp.exp(gl - pltpu.roll(gl, d, 0)), 0.0)
                o = o + _lane_sum(q * pltpu.roll(kk, d, 0) * e, ones) * pltpu.roll(v, d, 0)
            q_s[rows, :] = q
            k_s[rows, :] = kk
            gl_s[rows, :] = gl
            opre_ref[rows, :] = o
            return carry

        lax.fori_loop(0, n_tiles, tile, 0)

        def chunk(c, st):
            rows = pl.ds(pl.multiple_of(c * CHUNK, CHUNK), CHUNK)
            gl = gl_s[rows, :]
            qt = q_s[rows, :] * jnp.exp(gl)
            opre_ref[rows, :] += lax.dot_general(qt.astype(BF16), st.astype(BF16), NT, preferred_element_type=F32)
            gll = gl[CHUNK - 1:CHUNK, :]
            kt = k_s[rows, :] * jnp.exp(gll - gl)
            return st * jnp.exp(gll) + lax.dot_general(hi_ref[rows, :].astype(BF16), kt.astype(BF16), TN,
                                                       preferred_element_type=F32)

        _unrolled_loop(n_chunks, chunk, jnp.zeros((HEAD_DIM, HEAD_DIM), F32))

        def finish(i, carry):
            rows = pl.ds(pl.multiple_of(i * HG_TILE, HG_TILE), HG_TILE)
            o = opre_ref[rows, :]
            hg = hg_ref[rows, :]
            rs = lax.rsqrt(jnp.mean(o * o, axis=-1, keepdims=True) + EPS)
            aout_ref[rows, :] = ((o * rs) * nw_ref[...] * (hg * _sigmoid(hg))).astype(BF16)
            return carry

        lax.fori_loop(0, n_tiles, finish, 0)

    return _host_call(
        body, 6, 2, fused, _head_first_last, name="hgrn_fwd", grid=(HEADS,),
        in_specs=[_head_col(0), _head_col(HEADS), _head_col(2 * HEADS), _head_col(3 * HEADS),
                  pl.BlockSpec((2, HEAD_DIM), lambda h: (0, h)), pl.BlockSpec((1, HEAD_DIM), lambda h: (0, 0))],
        out_specs=[_head_col(0), _head_col(0)],
        out_shape=[jax.ShapeDtypeStruct((SEQ, HEADS * HEAD_DIM), BF16), jax.ShapeDtypeStruct((SEQ, HEADS * HEAD_DIM), F32)],
        scratch_shapes=[pltpu.VMEM((SEQ, HEAD_DIM), F32)] * 3, sem=("parallel",),
        operands=[proj, proj, proj, proj, lb_logits, norm_w] + fused_arrays)


def _hgrn_bwd(proj, lb_logits, norm_w, o_pre, d_aout, fused=None, fused_arrays=()):
    n_tiles = SEQ // HG_TILE
    n_chunks = SEQ // CHUNK

    def body(hq_ref, hf_ref, hi_ref, hg_ref, lbl_ref, nw_ref, opre_ref, da_ref,
             dhq_ref, dhf_ref, dhi_ref, dhg_ref, dlog_ref, gnw_ref,
             q_s, k_s, gl_s, do_s, dq_s, dk_s, dv_s, st_s):
        h = pl.program_id(0)
        lb = _lower_bound(lbl_ref[...])
        nw = nw_ref[...]
        ones = jnp.ones((HEAD_DIM, HEAD_DIM), BF16)
        pos = lax.broadcasted_iota(jnp.int32, (HG_TILE, HEAD_DIM), 0) % CHUNK

        @pl.when(h == 0)
        def _():
            gnw_ref[...] = jnp.zeros_like(gnw_ref)

        def tile(i, carry):
            rows = pl.ds(pl.multiple_of(i * HG_TILE, HG_TILE), HG_TILE)
            v = hi_ref[rows, :]
            q, _sig, _f, kk, gl = _hgrn_gates(hq_ref[rows, :], hf_ref[rows, :], lb, pos)
            o = opre_ref[rows, :]
            hg = hg_ref[rows, :]
            da = da_ref[rows, :]
            rs = lax.rsqrt(jnp.mean(o * o, axis=-1, keepdims=True) + EPS)
            oh = o * rs
            sg = _sigmoid(hg)
            dnorm = da * (hg * sg)
            dhg_ref[rows, :] = (da * (oh * nw) * _dsilu(hg, sg)).astype(BF16)
            gnw_ref[...] += jnp.sum(dnorm * oh, axis=0, keepdims=True)
            doh = dnorm * nw
            do = rs * (doh - oh * jnp.mean(doh * oh, axis=-1, keepdims=True))

            d_a = _lane_sum(do * v, ones)
            dq = d_a * kk
            dk = d_a * q
            dv = _lane_sum(q * kk, ones) * do
            for d in range(1, CHUNK):
                ks = pltpu.roll(kk, d, 0)
                e = jnp.where(pos >= d, jnp.exp(gl - pltpu.roll(gl, d, 0)), 0.0)
                a_d = _lane_sum(q * ks * e, ones)
                d_a = _lane_sum(do * pltpu.roll(v, d, 0), ones) * e
                dq = dq + d_a * ks
                dk = dk + pltpu.roll(d_a * q, HG_TILE - d, 0)
                dv = dv + pltpu.roll(a_d * do, HG_TILE - d, 0)
            q_s[rows, :] = q
            k_s[rows, :] = kk
            gl_s[rows, :] = gl
            do_s[rows, :] = do
            dq_s[rows, :] = dq
            dk_s[rows, :] = dk
            dv_s[rows, :] = dv
            return carry

        lax.fori_loop(0, n_tiles, tile, 0)

        def fwd_chunk(c, st):
            rows = pl.ds(pl.multiple_of(c * CHUNK, CHUNK), CHUNK)
            gl = gl_s[rows, :]
            st_s[c] = st
            dq_s[rows, :] += jnp.dot(do_s[rows, :].astype(BF16), st.astype(BF16),
                                     preferred_element_type=F32) * jnp.exp(gl)
            gll = gl[CHUNK - 1:CHUNK, :]
            kt = k_s[rows, :] * jnp.exp(gll - gl)
            return st * jnp.exp(gll) + lax.dot_general(hi_ref[rows, :].astype(BF16), kt.astype(BF16), TN,
                                                       preferred_element_type=F32)

        _unrolled_loop(n_chunks, fwd_chunk, jnp.zeros((HEAD_DIM, HEAD_DIM), F32))

        pos_c = lax.broadcasted_iota(jnp.int32, (CHUNK, HEAD_DIM), 0)

        def bwd_chunk(i, carry):
            rt, dlb = carry
            c = n_chunks - 1 - i
            rows = pl.ds(pl.multiple_of(c * CHUNK, CHUNK), CHUNK)
            gl = gl_s[rows, :]
            q = q_s[rows, :]
            kk = k_s[rows, :]
            do = do_s[rows, :]
            gll = gl[CHUNK - 1:CHUNK, :]
            egl = jnp.exp(gll)
            ekt = jnp.exp(gll - gl)
            rt_b = rt.astype(BF16)
            dk_in = dk_s[rows, :]
            dk_far = jnp.dot(hi_ref[rows, :].astype(BF16), rt_b, preferred_element_type=F32) * ekt
            dk = dk_in + dk_far
            dv = dv_s[rows, :] + lax.dot_general((kk * ekt).astype(BF16), rt_b, NT, preferred_element_type=F32)
            dq = dq_s[rows, :]
            rc = q * dq - kk * dk_in
            pc = kk * dk_far
            pre = pc
            for sh in (1, 2, 4, 8):
                rc = rc + jnp.where(pos_c < CHUNK - sh, pltpu.roll(rc, CHUNK - sh, 0), 0.0)
                pre = pre + jnp.where(pos_c >= sh, pltpu.roll(pre, sh, 0), 0.0)
            across = jnp.sum(st_s[c] * rt, axis=0, keepdims=True) * egl
            dgl = rc + (pre - pc) + across
            hf = hf_ref[rows, :]
            sig = _sigmoid(hf)
            f = lb + (1.0 - lb) * sig
            df = dgl / f - dk
            dhf_ref[rows, :] = (df * (1.0 - lb) * sig * (1.0 - sig)).astype(BF16)
            hq = hq_ref[rows, :]
            dhq_ref[rows, :] = (dq * _dsilu(hq, _sigmoid(hq))).astype(BF16)
            dhi_ref[rows, :] = dv.astype(BF16)
            rt_new = rt * egl + lax.dot_general(do.astype(BF16), (q * jnp.exp(gl)).astype(BF16), TN,
                                                preferred_element_type=F32)
            return (rt_new, dlb + jnp.sum(df * (1.0 - sig), axis=0, keepdims=True))

        _, dlb = _unrolled_loop(n_chunks, bwd_chunk,
                                (jnp.zeros((HEAD_DIM, HEAD_DIM), F32), jnp.zeros((1, HEAD_DIM), F32)))
        dl0 = lb * (1.0 - lb) * dlb
        dlog_ref[0:1, :] = dl0
        dlog_ref[1:2, :] = -dl0

    wide = HEADS * HEAD_DIM
    return _host_call(
        body, 8, 6, fused, _head_first_last, name="hgrn_bwd", grid=(HEADS,),
        in_specs=[_head_col(0), _head_col(HEADS), _head_col(2 * HEADS), _head_col(3 * HEADS),
                  pl.BlockSpec((2, HEAD_DIM), lambda h: (0, h)), pl.BlockSpec((1, HEAD_DIM), lambda h: (0, 0)),
                  _head_col(0), _head_col(0)],
        out_specs=[_head_col(0)] * 4 + [pl.BlockSpec((2, HEAD_DIM), lambda h: (0, h)),
                                        pl.BlockSpec((1, HEAD_DIM), lambda h: (0, 0))],
        out_shape=[jax.ShapeDtypeStruct((SEQ, wide), BF16)] * 4 + [jax.ShapeDtypeStruct((2, wide), F32),
                                                                    jax.ShapeDtypeStruct((1, HEAD_DIM), F32)],
        scratch_shapes=[pltpu.VMEM((SEQ, HEAD_DIM), F32)] * 7 + [pltpu.VMEM((n_chunks, HEAD_DIM, HEAD_DIM), F32)],
        sem=("arbitrary",),
        operands=[proj, proj, proj, proj, lb_logits, norm_w, o_pre, d_aout] + list(fused_arrays))


Q_TILE = 256
ATT_SCALE = HEAD_DIM ** -0.5
ATT_OFF = 4 * HEADS


def _qk_prep(proj, q_w, k_w):
    def body(aq_ref, ak_ref, av_ref, qw_ref, kw_ref, qn_ref, kn_ref, v_ref):
        aq = aq_ref[...]
        ak = ak_ref[...]
        qn_ref[...] = (aq * lax.rsqrt(jnp.mean(aq * aq, axis=-1, keepdims=True) + EPS) * qw_ref[...]).astype(BF16)
        kn_ref[...] = (ak * lax.rsqrt(jnp.mean(ak * ak, axis=-1, keepdims=True) + EPS) * kw_ref[...]).astype(BF16)
        v_ref[...] = av_ref[...].astype(BF16)

    wide = HEADS * HEAD_DIM
    vec = pl.BlockSpec((1, HEAD_DIM), lambda h: (0, 0))
    return pl.pallas_call(
        body, name="qk_prep", grid=(HEADS,),
        in_specs=[_head_col(ATT_OFF), _head_col(ATT_OFF + HEADS), _head_col(ATT_OFF + 2 * HEADS), vec, vec],
        out_specs=[_head_col(0)] * 3, out_shape=[jax.ShapeDtypeStruct((SEQ, wide), BF16)] * 3,
        compiler_params=_params(("parallel",)))(proj, proj, proj, q_w, k_w)


def _alibi_slopes():
    slopes = jnp.exp2(-8.0 * jnp.arange(1, HEADS + 1, dtype=F32) / HEADS)
    return jnp.broadcast_to(slopes[:, None, None], (HEADS, 1, HEAD_DIM))


SLOPE_SPEC = pl.BlockSpec((None, 1, HEAD_DIM), lambda h, i: (h, 0, 0))


N_Q_TILES = SEQ // Q_TILE
NOT_ATTENDED = 1e35


def _att_tables():
    o = jnp.arange(N_Q_TILES, dtype=jnp.int32)[:, None, None]
    r = jnp.arange(Q_TILE, dtype=jnp.int32)[None, :, None]
    c = jnp.arange(Q_TILE, dtype=jnp.int32)[None, None, :]
    dist = o * Q_TILE + r - c
    mult = ((dist <= 128).astype(F32) + (((dist % 4) == 0) & (dist <= 512)).astype(F32)
            + ((dist % 16) == 0).astype(F32))
    valid = (dist >= 0) & (mult > 0)
    return (jnp.where(valid, dist.astype(F32), NOT_ATTENDED),
            jnp.where(valid, jnp.log(jnp.maximum(mult, 1.0)), 0.0))


TABLE_SPEC = pl.BlockSpec((N_Q_TILES, Q_TILE, Q_TILE), lambda h, i: (0, 0, 0))


def _att_block(q, k_ref, j, off, slope, dist_ref, lmul_ref):
    rows = pl.ds(pl.multiple_of(j * Q_TILE, Q_TILE), Q_TILE)
    s = lax.dot_general(q, k_ref[rows, :], NT, preferred_element_type=F32) * ATT_SCALE
    return s - slope * dist_ref[off] + lmul_ref[off], rows


def _att_first_last():
    h, i = pl.program_id(0), pl.program_id(1)
    return (h == 0) & (i == 0), (h == HEADS - 1) & (i == N_Q_TILES - 1)


def _attn_fwd(qn, kn, vb, fused=None, fused_arrays=()):
    def body(q_ref, k_ref, v_ref, sl_ref, dist_ref, lmul_ref, o_ref, lse_ref):
        i = pl.program_id(1)
        q = q_ref[...]
        slope = sl_ref[0:1, 0:1]

        def step(j, carry):
            m, l, acc = carry
            sb, rows = _att_block(q, k_ref, j, i - j, slope, dist_ref, lmul_ref)
            m_new = jnp.maximum(m, jnp.max(sb, axis=-1, keepdims=True))
            alpha = jnp.exp(m - m_new)
            p = jnp.exp(sb - m_new)
            l = alpha * l + jnp.sum(p, axis=-1, keepdims=True)
            acc = alpha * acc + jnp.dot(p.astype(BF16), v_ref[rows, :], preferred_element_type=F32)
            return m_new, l, acc

        m, l, acc = lax.fori_loop(0, i + 1, step, (jnp.full((Q_TILE, 1), -1e30, F32), jnp.zeros((Q_TILE, 1), F32),
                                                   jnp.zeros((Q_TILE, HEAD_DIM), F32)))
        o_ref[...] = acc / l
        lse_ref[...] = m + jnp.log(l)

    wide = HEADS * HEAD_DIM
    qt = pl.BlockSpec((Q_TILE, HEAD_DIM), lambda h, i: (i, h))
    full = pl.BlockSpec((SEQ, HEAD_DIM), lambda h, i: (0, h))
    return _host_call(
        body, 6, 2, fused, _att_first_last, name="attn_fwd", grid=(HEADS, N_Q_TILES),
        in_specs=[qt, full, full, SLOPE_SPEC, TABLE_SPEC, TABLE_SPEC],
        out_specs=[qt, pl.BlockSpec((None, Q_TILE, 1), lambda h, i: (h, i, 0))],
        out_shape=[jax.ShapeDtypeStruct((SEQ, wide), F32), jax.ShapeDtypeStruct((HEADS, SEQ, 1), F32)],
        scratch_shapes=[], sem=("parallel", "parallel"),
        operands=[qn, kn, vb, _alibi_slopes(), *_att_tables()] + list(fused_arrays))


def _attn_bwd(qn, kn, vb, o, lse, d_mix, fused=None, fused_arrays=()):
    def body(q_ref, k_ref, v_ref, o_ref, lse_ref, do_ref, sl_ref, dist_ref, lmul_ref, dq_ref, dk_ref, dv_ref):
        i = pl.program_id(1)
        q = q_ref[...]
        do = do_ref[...]
        do_b = do.astype(BF16)
        slope = sl_ref[0:1, 0:1]
        lse = lse_ref[...]
        delta = jnp.sum(do * o_ref[...], axis=-1, keepdims=True)

        @pl.when(i == 0)
        def _():
            dk_ref[...] = jnp.zeros_like(dk_ref)
            dv_ref[...] = jnp.zeros_like(dv_ref)

        def step(j, dq):
            sb, rows = _att_block(q, k_ref, j, i - j, slope, dist_ref, lmul_ref)
            p = jnp.exp(sb - lse)
            dp = lax.dot_general(do_b, v_ref[rows, :], NT, preferred_element_type=F32)
            ds = (p * (dp - delta)).astype(BF16)
            dk_ref[rows, :] += lax.dot_general(ds, q, TN, preferred_element_type=F32) * ATT_SCALE
            dv_ref[rows, :] += lax.dot_general(p.astype(BF16), do_b, TN, preferred_element_type=F32)
            return dq + jnp.dot(ds, k_ref[rows, :], preferred_element_type=F32)

        dq = lax.fori_loop(0, i + 1, step, jnp.zeros((Q_TILE, HEAD_DIM), F32))
        dq_ref[...] = dq * ATT_SCALE

    wide = HEADS * HEAD_DIM
    qt = pl.BlockSpec((Q_TILE, HEAD_DIM), lambda h, i: (i, h))
    full = pl.BlockSpec((SEQ, HEAD_DIM), lambda h, i: (0, h))
    return _host_call(
        body, 9, 3, fused, _att_first_last, name="attn_bwd", grid=(HEADS, N_Q_TILES),
        in_specs=[qt, full, full, qt, pl.BlockSpec((None, Q_TILE, 1), lambda h, i: (h, i, 0)),
                  pl.BlockSpec((Q_TILE, HEAD_DIM), lambda h, i: (i, h + HEADS)), SLOPE_SPEC, TABLE_SPEC, TABLE_SPEC],
        out_specs=[qt, full, full], out_shape=[jax.ShapeDtypeStruct((SEQ, wide), F32)] * 3,
        scratch_shapes=[], sem=("parallel", "arbitrary"),
        operands=[qn, kn, vb, o, lse, d_mix, _alibi_slopes(), *_att_tables()] + list(fused_arrays))


def _qk_bwd(proj, q_w, k_w, dqn, dkn, dv):
    def body(aq_ref, ak_ref, qw_ref, kw_ref, dqn_ref, dkn_ref, dv_ref, daq_ref, dak_ref, dav_ref, gq_ref, gk_ref):
        h = pl.program_id(0)

        @pl.when(h == 0)
        def _():
            gq_ref[...] = jnp.zeros_like(gq_ref)
            gk_ref[...] = jnp.zeros_like(gk_ref)

        def one(a_ref, w_ref, d_ref, da_ref, g_ref):
            a = a_ref[...]
            d = d_ref[...]
            rs = lax.rsqrt(jnp.mean(a * a, axis=-1, keepdims=True) + EPS)
            ah = a * rs
            g_ref[...] += jnp.sum(d * ah, axis=0, keepdims=True)
            dah = d * w_ref[...]
            da_ref[...] = (rs * (dah - ah * jnp.mean(dah * ah, axis=-1, keepdims=True))).astype(BF16)

        one(aq_ref, qw_ref, dqn_ref, daq_ref, gq_ref)
        one(ak_ref, kw_ref, dkn_ref, dak_ref, gk_ref)
        dav_ref[...] = dv_ref[...].astype(BF16)

    wide = HEADS * HEAD_DIM
    vec = pl.BlockSpec((1, HEAD_DIM), lambda h: (0, 0))
    return pl.pallas_call(
        body, name="qk_bwd", grid=(HEADS,),
        in_specs=[_head_col(ATT_OFF), _head_col(ATT_OFF + HEADS), vec, vec, _head_col(0), _head_col(0), _head_col(0)],
        out_specs=[_head_col(0)] * 3 + [vec, vec],
        out_shape=[jax.ShapeDtypeStruct((SEQ, wide), BF16)] * 3 + [jax.ShapeDtypeStruct((1, HEAD_DIM), F32)] * 2,
        compiler_params=_params(("arbitrary",)))(proj, proj, q_w, k_w, dqn, dkn, dv)


def _pair_sum(name, mine, theirs):
    _, r, c = mine.shape
    tr = r // 2 if r % 16 == 0 else r

    def body(a_ref, b_ref, o_ref):
        o_ref[...] = (a_ref[...].astype(F32) + b_ref[...].astype(F32)).astype(BF16)

    spec = pl.BlockSpec((None, tr, c), lambda q, i: (q, i, 0))
    return pl.pallas_call(body, name=name, grid=(4, r // tr), in_specs=[spec, spec], out_specs=spec,
                          out_shape=jax.ShapeDtypeStruct(mine.shape, BF16),
                          compiler_params=_params(("parallel", "parallel")))(mine, theirs)


def _adamw(name, w, m, v, addends, tr=None):
    r, c = w.shape
    tr = r if tr is None else tr
    n_add = len(addends)
    c1 = 1.0 - ADAM_B1 ** ADAM_STEP
    c2 = 1.0 - ADAM_B2 ** ADAM_STEP

    def body(*refs):
        w_ref, m_ref, v_ref = refs[:3]
        add_refs = refs[3:3 + n_add]
        g_ref, d_ref, nm_ref, nv_ref = refs[3 + n_add:]
        g = add_refs[0][...].astype(F32)
        for a_ref in add_refs[1:]:
            g = g + a_ref[...].astype(F32)
        nm = ADAM_B1 * m_ref[...] + (1.0 - ADAM_B1) * g
        nv = ADAM_B2 * v_ref[...] + (1.0 - ADAM_B2) * (g * g)
        g_ref[...] = g
        nm_ref[...] = nm
        nv_ref[...] = nv
        d_ref[...] = -ADAM_LR * ((nm / c1) / (jnp.sqrt(nv / c2) + ADAM_EPS) + ADAM_WD * w_ref[...])

    spec = pl.BlockSpec((tr, c), lambda i: (i, 0))
    out = jax.ShapeDtypeStruct((r, c), F32)
    return pl.pallas_call(body, name=name, grid=(r // tr,), in_specs=[spec] * (3 + n_add), out_specs=[spec] * 4,
                          out_shape=[out] * 4, compiler_params=_params(("parallel",)))(w, m, v, *addends)


def _sum_devices(gathered):
    _, r, c = gathered.shape

    def body(g_ref, o_ref):
        acc = g_ref[0]
        for d in range(1, N_DEV):
            acc = acc + g_ref[d]
        o_ref[...] = acc

    return pl.pallas_call(body, name="sum_devices", out_shape=jax.ShapeDtypeStruct((r, c), F32))(gathered)


def _pack_rows(vectors, rows):
    flat = jnp.concatenate([v.reshape(-1) for v in vectors])
    return jnp.pad(flat, (0, rows * 128 - flat.shape[0])).reshape(rows, 128)


def _unpack(flat, shapes):
    out, off = [], 0
    for shp in shapes:
        n = 1
        for d in shp:
            n *= d
        out.append(flat[off:off + n].reshape(shp))
        off += n
    return out


def _split_by_core(partials, core):
    halves = [p.reshape(4, 2, p.shape[1], p.shape[2]) for p in partials]
    return ([lax.dynamic_index_in_dim(hv, core, axis=1, keepdims=False) for hv in halves],
            [lax.dynamic_index_in_dim(hv, 1 - core, axis=1, keepdims=False) for hv in halves])


def _device_step(xs, tgt, mod, norm1_w, norm2_w, lb_logits, hg_norm_w, q_norm_w, k_norm_w, conv_w_full, conv_b,
                 win_g, w_out_x, w_up_x, w_down_x, core=None):
    fused = core is not None
    shift1, scale1, gate1, shift2, scale2, gate2 = (mod[k] for k in range(6))

    h, rstd1 = _norm_fwd("norm1_fwd", xs, norm1_w, scale1, shift1)
    proj = _mm_blocked_rhs("mm_in", h, win_g)
    qn, kn, vb = _qk_prep(proj, q_norm_w, k_norm_w)
    if fused:
        (a_out, o_pre), (wup_g,) = _hgrn_fwd(proj, lb_logits, hg_norm_w, _FusedCopies("gather", [w_up_x]), [w_up_x])
        (att_o, lse), (wout_g,) = _attn_fwd(qn, kn, vb, _FusedCopies("gather", [w_out_x]), [w_out_x])
        wup_g, wout_g = _forward_to_sibling("allgather_stage2_up_out", [wup_g, wout_g])
        wout_full = wout_g.reshape(D_MODEL, D_MODEL)
    else:
        (a_out, o_pre), _ = _hgrn_fwd(proj, lb_logits, hg_norm_w)
        (att_o, lse), _ = _attn_fwd(qn, kn, vb)
        wup_g, wout_full, wdown_full = w_up_x, w_out_x, w_down_x
    mixin = jnp.concatenate([a_out, att_o.astype(BF16)], axis=1)
    mix = _mm_plain("mm_out", mixin, wout_full, NN, 512, 1024, F32)
    x1, h2, rstd2 = _norm_fwd("norm2_fwd", xs, norm2_w, scale2, shift2, resid=mix, gate=gate1)
    if fused:
        u, (wdown_g,) = _mm_blocked_rhs("mm_up", h2, wup_g, fused=_FusedCopies("gather", [w_down_x]),
                                        fused_arrays=[w_down_x])
        wdown_g, = _forward_to_sibling("allgather_stage2_down", [wdown_g])
        wdown_full = wdown_g.reshape(D_FF, D_MODEL)
    else:
        u = _mm_blocked_rhs("mm_up", h2, wup_g)
    y = _conv_gate_fwd(u, conv_w_full, conv_b)
    ffn = _mm_plain("mm_down", y, wdown_full, NN, 512, 512, F32)
    loss_v, dout, dffn, dgate2 = _loss_head(x1, ffn, gate2, tgt)

    dy = _mm_plain("mm_down_dx", dffn, wdown_full, NT, 512, UP_BLK, F32)
    gw_down = _mm_plain("mm_down_dw", y, dffn, TN, UP_BLK, 1024, BF16)
    da, dg, gconv_w, gconv_b = _conv_gate_bwd(u, dy, conv_w_full, conv_b)
    du = jnp.concatenate([da, dg], axis=1)
    dh2 = _mm_blocked_rhs_t("mm_up_dx", du, wup_g)
    gw_up = _mm_wgrad_blocked("mm_up_dw", h2, du)
    dx1, dmix, dshift2, dscale2, gnorm2, dgate1 = _norm_bwd(
        "norm2_bwd", dh2, x1, rstd2, norm2_w, scale2, dout, mix=mix, gate=gate1)
    dmixin = _mm_plain("mm_out_dx", dmix, wout_full, NT, 512, 1024, F32)
    gw_out = _mm_plain("mm_out_dw", mixin, dmix, TN, 512, 1024, BF16)
    if fused:
        mine, theirs = _split_by_core(
            [gw_up, gw_out.reshape(N_DEV, OUT_BLK, D_MODEL), gw_down.reshape(N_DEV, FF_BLK, D_MODEL)], core)
        from_sibling = _exchange_sibling("grad_exchange_sibling_a", theirs)
        cs_up, cs_out, cs_down = [_pair_sum(f"grad_pair_sum_{k}", a, b)
                                  for k, (a, b) in enumerate(zip(mine, from_sibling))]
        (dhq, dhf, dhi, dhg, glog, ghg), (fc_up, fc_out) = _hgrn_bwd(
            proj, lb_logits, hg_norm_w, o_pre, dmixin, _FusedCopies("chips", [cs_up, cs_out]), [cs_up, cs_out])
        (dqn, dkn, dvv), (fc_down,) = _attn_bwd(qn, kn, vb, att_o, lse, dmixin,
                                                _FusedCopies("chips", [cs_down]), [cs_down])
    else:
        (dhq, dhf, dhi, dhg, glog, ghg), _ = _hgrn_bwd(proj, lb_logits, hg_norm_w, o_pre, dmixin)
        (dqn, dkn, dvv), _ = _attn_bwd(qn, kn, vb, att_o, lse, dmixin)
    daq, dak, dav, gqw, gkw = _qk_bwd(proj, q_norm_w, k_norm_w, dqn, dkn, dvv)
    dproj = jnp.concatenate([dhq, dhf, dhi, dhg, daq, dak, dav], axis=1)
    dh = _mm_blocked_rhs_t("mm_in_dx", dproj, win_g)
    gw_in = _mm_wgrad_blocked("mm_in_dw", h, dproj)
    grad_x, dshift1, dscale1, gnorm1 = _norm_bwd("norm1_bwd", dh, xs, rstd1, norm1_w, scale1, dx1)
    gmod = jnp.concatenate([dshift1, dscale1, dgate1, dshift2, dscale2, dgate2], axis=1)
    if fused:
        mine, theirs = _split_by_core([gw_in], core)
        from_sibling = _exchange_sibling("grad_exchange_sibling_b", theirs)
        cs_in = _pair_sum("grad_pair_sum_in", mine[0], from_sibling[0])
        fc_in, = _exchange_chips("grad_exchange_chips_in", [cs_in])
        large = [(cs_in, fc_in), (cs_out, fc_out), (cs_up, fc_up), (cs_down, fc_down)]
    else:
        large = [gw_in, gw_out, gw_up, gw_down]
    return (loss_v, grad_x, gmod, gnorm1, gnorm2, glog, ghg, gqw, gkw, gconv_b, gconv_w, *large)


def kernel(x, c, w_ada, b_ada, norm1_w, w_in, lb_logits, hg_norm_w, q_norm_w, k_norm_w, w_out, norm2_w, w_up, conv_w, conv_b, w_down, loss_target, m_w_ada, m_b_ada, m_norm1_w, m_w_in, m_lb_logits, m_hg_norm_w, m_q_norm_w, m_k_norm_w, m_w_out, m_norm2_w, m_w_up, m_conv_w, m_conv_b, m_w_down, v_w_ada, v_b_ada, v_norm1_w, v_w_in, v_lb_logits, v_hg_norm_w, v_q_norm_w, v_k_norm_w, v_w_out, v_norm2_w, v_w_up, v_conv_w, v_conv_b, v_w_down):
    ix, iy, ic = lax.axis_index("x"), lax.axis_index("y"), lax.axis_index("c")
    me = 4 * ix + 2 * iy + ic
    my_chip = 2 * ix + iy

    xs = x[0]
    tgt = loss_target[0]

    win_g, = _allgather_weights([w_in[0].astype(BF16)])

    c_all = _allgather_vmem(c.reshape(8, D_MODEL // 8), "allgather_c").reshape(N_DEV, D_MODEL)
    b_blk = lax.dynamic_slice_in_dim(b_ada, me * ADA_BLK, ADA_BLK, axis=1)
    mod_cols = _ada_fwd(c_all, w_ada[0], b_blk)
    mod_all = _allgather_vmem(mod_cols, "allgather_mod").reshape(N_DEV, N_DEV, ADA_BLK)
    mod = lax.dynamic_index_in_dim(mod_all, me, axis=1, keepdims=False).reshape(6, 1, D_MODEL)

    conv_w_all = _allgather_vmem(_pack_rows([conv_w[0]], 24), "allgather_conv_w").reshape(N_DEV, 24 * 128)
    conv_w_full = conv_w_all[:, :3 * FF_BLK].reshape(N_DEV, 3, FF_BLK).transpose(1, 0, 2).reshape(3, D_FF)

    (loss_v, grad_x, gmod, gnorm1, gnorm2, glog, ghg, gqw, gkw, gconv_b, gconv_w,
     rs_in, rs_out, rs_up, rs_down) = _device_step(
        xs, tgt, mod, norm1_w, norm2_w, lb_logits, hg_norm_w, q_norm_w, k_norm_w, conv_w_full, conv_b,
        win_g, w_out[0].astype(BF16), w_up[0].astype(BF16), w_down[0].astype(BF16), core=ic)
    loss = lax.psum(loss_v[0, 0], AXES)

    small_shapes = [(1, 6 * D_MODEL), (1, D_MODEL), (1, D_MODEL), (2, HEADS * HEAD_DIM), (1, HEAD_DIM),
                    (1, HEAD_DIM), (1, HEAD_DIM), (1, D_FF), (3, D_FF)]
    small = [gmod, gnorm1, gnorm2, glog, ghg, gqw, gkw, gconv_b, gconv_w]
    n_small = sum(a.size for a in small)
    rows = -(-n_small // 1024) * 8
    gathered = _allgather_vmem(_pack_rows(small, rows), "allgather_small").reshape(N_DEV, rows, 128)
    summed = _sum_devices(gathered).reshape(-1)
    (g_b_ada, g_norm1, g_norm2, g_lb, g_hg, g_q, g_k, g_conv_b, g_conv_w_full) = _unpack(summed, small_shapes)
    g_conv_w = lax.dynamic_slice_in_dim(g_conv_w_full, me * FF_BLK, FF_BLK, axis=1)

    gmod_all = gathered[:, :6 * D_MODEL // 128, :].reshape(N_DEV, 6 * D_MODEL)
    gmod_cols = lax.dynamic_slice_in_dim(gmod_all, me * ADA_BLK, ADA_BLK, axis=1)
    g_w_ada_raw = _ada_wgrad(c_all, gmod_cols)

    def big_update(name, w, m, v, rs, tr):
        chip_sums, recv = rs
        own = lax.dynamic_index_in_dim(chip_sums, my_chip, axis=0, keepdims=False)
        return _adamw(name, w[0], m[0], v[0], [own, recv[0], recv[1], recv[2]], tr=tr)

    r_in = big_update("adamw_w_in", w_in, m_w_in, v_w_in, rs_in, 256)
    r_out = big_update("adamw_w_out", w_out, m_w_out, v_w_out, rs_out, 128)
    r_up = big_update("adamw_w_up", w_up, m_w_up, v_w_up, rs_up, 256)
    r_down = big_update("adamw_w_down", w_down, m_w_down, v_w_down, rs_down, 176)
    r_ada = _adamw("adamw_w_ada", w_ada[0], m_w_ada[0], v_w_ada[0], [g_w_ada_raw], tr=256)
    r_convw = _adamw("adamw_conv_w", conv_w[0], m_conv_w[0], v_conv_w[0], [g_conv_w])

    rep_shapes = [(1, 6 * D_MODEL), (1, D_MODEL), (1, D_MODEL), (2, HEADS * HEAD_DIM), (1, HEAD_DIM),
                  (1, HEAD_DIM), (1, HEAD_DIM), (1, D_FF)]
    rep_rows = -(-sum(a * b for a, b in rep_shapes) // 1024) * 8
    pack = lambda arrs: _pack_rows(arrs, rep_rows)
    rep = _adamw("adamw_small",
                 pack([b_ada, norm1_w, norm2_w, lb_logits, hg_norm_w, q_norm_w, k_norm_w, conv_b]),
                 pack([m_b_ada, m_norm1_w, m_norm2_w, m_lb_logits, m_hg_norm_w, m_q_norm_w, m_k_norm_w, m_conv_b]),
                 pack([v_b_ada, v_norm1_w, v_norm2_w, v_lb_logits, v_hg_norm_w, v_q_norm_w, v_k_norm_w, v_conv_b]),
                 [pack([g_b_ada, g_norm1, g_norm2, g_lb, g_hg, g_q, g_k, g_conv_b])])
    rep = [_unpack(r.reshape(-1), rep_shapes) for r in rep]

    def big(r):
        return [a[None] for a in r]

    order = {"w_ada": big(r_ada), "b_ada": [r[0] for r in rep], "norm1_w": [r[1] for r in rep],
             "w_in": big(r_in), "lb_logits": [r[3] for r in rep], "hg_norm_w": [r[4] for r in rep],
             "q_norm_w": [r[5] for r in rep], "k_norm_w": [r[6] for r in rep], "w_out": big(r_out),
             "norm2_w": [r[2] for r in rep], "w_up": big(r_up), "conv_w": big(r_convw),
             "conv_b": [r[7] for r in rep], "w_down": big(r_down)}
    names = ["w_ada", "b_ada", "norm1_w", "w_in", "lb_logits", "hg_norm_w", "q_norm_w", "k_norm_w", "w_out",
             "norm2_w", "w_up", "conv_w", "conv_b", "w_down"]
    outs = [loss, grad_x[None]]
    for kind in range(4):
        outs += [order[n][kind] for n in names]
    return tuple(outs)
```

```python
import functools

import jax
import jax.numpy as jnp
from jax import lax
from jax.experimental import pallas as pl
from jax.experimental.pallas import tpu as pltpu

F32 = jnp.float32
BF16 = jnp.bfloat16

N_DEV = 8
SEQ = 2048
D_MODEL = 2048
HEADS = 8
HEAD_DIM = 128
IN_COLS = 7168
IN_BLK = IN_COLS // N_DEV
D_FF = 5632
UP_BLK = 2 * D_FF // N_DEV
FF_BLK = D_FF // N_DEV
ADA_BLK = 6 * D_MODEL // N_DEV
OUT_BLK = D_MODEL // N_DEV
EPS = 1e-6
CHUNK = 16
ROW_TILE = 256
V7X_VMEM_LIMIT = 56 * 1024 * 1024

ADAM_LR = 0.001
ADAM_B1 = 0.9
ADAM_B2 = 0.999
ADAM_EPS = 1e-08
ADAM_WD = 0.01
ADAM_STEP = 10

NN = (((1,), (0,)), ((), ()))
NT = (((1,), (1,)), ((), ()))
TN = (((0,), (0,)), ((), ()))
MESH = pl.DeviceIdType.MESH
AXES = ("x", "y", "c")


def _params(sem=None, vmem=V7X_VMEM_LIMIT):
    return pltpu.CompilerParams(dimension_semantics=sem, vmem_limit_bytes=vmem)


def _sigmoid(x):
    return 1.0 / (1.0 + jnp.exp(-x))


def _dsilu(x, s):
    return s * (1.0 + x * (1.0 - s))


def _lane_sum(x, ones_bf16):
    hi = x.astype(BF16)
    lo = (x - hi.astype(F32)).astype(BF16)
    return (jnp.dot(hi, ones_bf16, preferred_element_type=F32)
            + jnp.dot(lo, ones_bf16, preferred_element_type=F32))


def _mesh_pos():
    return lax.axis_index("x"), lax.axis_index("y"), lax.axis_index("c")


def _allgather_vmem(x_blk, name):
    m_per, n = x_blk.shape

    def body(x_ref, out_ref, send_sems, recv_sems, local_sem):
        x, y, c = _mesh_pos()
        me, sibling = (x, y, c), (x, y, 1 - c)
        chips = [(1 - x, y), (x, 1 - y), (1 - x, 1 - y)]

        def rows(px, py, pc):
            return out_ref.at[pl.ds((4 * px + 2 * py + pc) * m_per, m_per), :]

        def copy(k, block, to, src=None):
            return pltpu.make_async_remote_copy(
                src_ref=rows(*block) if src is None else src, dst_ref=rows(*block),
                send_sem=send_sems.at[k], recv_sem=recv_sems.at[k], device_id=to, device_id_type=MESH)

        mine = pltpu.make_async_copy(x_ref, rows(*me), local_sem)
        mine.start()
        first = [copy(0, me, sibling, src=x_ref)]
        first += [copy(1 + j, me, (*chip, c), src=x_ref) for j, chip in enumerate(chips)]
        for cp in first:
            cp.start()
        passed = [copy(4 + j, (*chip, c), sibling) for j, chip in enumerate(chips)]
        for j, chip in enumerate(chips):
            copy(1 + j, (*chip, c), me).wait_recv()
            passed[j].start()
        copy(0, sibling, me).wait_recv()
        for j, chip in enumerate(chips):
            copy(4 + j, (*chip, 1 - c), me).wait_recv()
        for cp in first + passed:
            cp.wait_send()
        mine.wait()

    return pl.pallas_call(
        body, name=name,
        out_shape=jax.ShapeDtypeStruct((N_DEV * m_per, n), x_blk.dtype),
        in_specs=[pl.BlockSpec(memory_space=pltpu.VMEM)],
        out_specs=pl.BlockSpec(memory_space=pltpu.VMEM),
        scratch_shapes=[pltpu.SemaphoreType.DMA((7,)), pltpu.SemaphoreType.DMA((7,)), pltpu.SemaphoreType.DMA],
    )(x_blk)


def _allgather_weights(blocks):
    n_arr = len(blocks)

    def body(*refs):
        ins, outs = refs[:n_arr], refs[n_arr:2 * n_arr]
        send_sems, recv_sems, local_sems = refs[2 * n_arr:]
        x, y, c = _mesh_pos()
        me, sibling = (x, y, c), (x, y, 1 - c)
        chips = [(1 - x, y), (x, 1 - y), (1 - x, 1 - y)]

        def slot(a, px, py, pc):
            return outs[a].at[4 * px + 2 * py + pc]

        def copy(a, k, block, to, src=None):
            return pltpu.make_async_remote_copy(
                src_ref=slot(a, *block) if src is None else src, dst_ref=slot(a, *block),
                send_sem=send_sems.at[a, k], recv_sem=recv_sems.at[a, k], device_id=to, device_id_type=MESH)

        mine, first, passed = [], [], []
        for a in range(n_arr):
            cp = pltpu.make_async_copy(ins[a], slot(a, *me), local_sems.at[a])
            cp.start()
            mine.append(cp)
            first.append(copy(a, 0, me, sibling, src=ins[a]))
            first += [copy(a, 1 + j, me, (*chip, c), src=ins[a]) for j, chip in enumerate(chips)]
        for cp in first:
            cp.start()
        for j, chip in enumerate(chips):
            for a in range(n_arr):
                copy(a, 1 + j, (*chip, c), me).wait_recv()
                cp = copy(a, 4 + j, (*chip, c), sibling)
                cp.start()
                passed.append(cp)
        for a in range(n_arr):
            copy(a, 0, sibling, me).wait_recv()
            for j, chip in enumerate(chips):
                copy(a, 4 + j, (*chip, 1 - c), me).wait_recv()
        for cp in first + passed:
            cp.wait_send()
        for cp in mine:
            cp.wait()

    hbm = pl.BlockSpec(memory_space=pltpu.HBM)
    return pl.pallas_call(
        body, name="allgather_weights",
        out_shape=[jax.ShapeDtypeStruct((N_DEV,) + b.shape, b.dtype) for b in blocks],
        in_specs=[hbm] * n_arr, out_specs=[hbm] * n_arr,
        scratch_shapes=[pltpu.SemaphoreType.DMA((n_arr, 7)), pltpu.SemaphoreType.DMA((n_arr, 7)),
                        pltpu.SemaphoreType.DMA((n_arr,))],
    )(*blocks)


HBM_SPEC = pl.BlockSpec(memory_space=pltpu.HBM)


class _FusedCopies:
    def __init__(self, kind, arrays):
        n = len(arrays)
        self.n = n
        self.kind = kind
        self.in_specs = [HBM_SPEC] * n
        self.out_specs = [HBM_SPEC] * n
        if kind == "gather":
            self.out_shape = [jax.ShapeDtypeStruct((N_DEV,) + a.shape, a.dtype) for a in arrays]
            self.scratch_shapes = [pltpu.SemaphoreType.DMA((n, 4)), pltpu.SemaphoreType.DMA((n, 4)),
                                   pltpu.SemaphoreType.DMA((n,))]
        else:
            self.out_shape = [jax.ShapeDtypeStruct((3,) + a.shape[1:], a.dtype) for a in arrays]
            self.scratch_shapes = [pltpu.SemaphoreType.DMA((n, 3)), pltpu.SemaphoreType.DMA((n, 3))]
        self.n_scratch = len(self.scratch_shapes)

    def copies(self, ins, outs, sems):
        x, y, c = _mesh_pos()
        chips = [(1 - x, y), (x, 1 - y), (1 - x, 1 - y)]
        starts, waits = [], []
        if self.kind == "gather":
            send_sems, recv_sems, local_sems = sems
            me = (x, y, c)
            peers = [(x, y, 1 - c)] + [(px, py, c) for px, py in chips]

            def slot(a, pos):
                return outs[a].at[4 * pos[0] + 2 * pos[1] + pos[2]]

            def remote(a, k, lands_from):
                return pltpu.make_async_remote_copy(
                    src_ref=ins[a], dst_ref=slot(a, lands_from), send_sem=send_sems.at[a, k],
                    recv_sem=recv_sems.at[a, k], device_id=peers[k], device_id_type=MESH)

            for a in range(self.n):
                local = pltpu.make_async_copy(ins[a], slot(a, me), local_sems.at[a])
                starts.append(local)
                waits.append(local)
                for k in range(4):
                    starts.append(remote(a, k, me))
                    waits.append(remote(a, k, peers[k]))
        else:
            send_sems, recv_sems = sems
            for a in range(self.n):
                for j, (px, py) in enumerate(chips):
                    cp = pltpu.make_async_remote_copy(
                        src_ref=ins[a].at[2 * px + py], dst_ref=outs[a].at[j], send_sem=send_sems.at[a, j],
                        recv_sem=recv_sems.at[a, j], device_id=(px, py, c), device_id_type=MESH)
                    starts.append(cp)
                    waits.append(cp)
        return starts, waits


def _host_body(body, n_in, n_out, fused, first_last):
    if fused is None:
        return body
    n = fused.n

    def wrapped(*refs):
        core_in, f_in = refs[:n_in], refs[n_in:n_in + n]
        core_out = refs[n_in + n:n_in + n + n_out]
        f_out = refs[n_in + n + n_out:n_in + 2 * n + n_out]
        rest = refs[n_in + 2 * n + n_out:]
        core_scratch, f_sems = rest[:len(rest) - fused.n_scratch], rest[len(rest) - fused.n_scratch:]
        starts, waits = fused.copies(f_in, f_out, f_sems)
        first, last = first_last()

        @pl.when(first)
        def _():
            for cp in starts:
                cp.start()

        body(*core_in, *core_out, *core_scratch)

        @pl.when(last)
        def _():
            for cp in waits:
                cp.wait()

    return wrapped


def _host_call(body, n_in, n_out, fused, first_last, *, name, grid, in_specs, out_specs, out_shape, scratch_shapes,
               sem, operands):
    if fused is not None:
        in_specs = list(in_specs) + fused.in_specs
        out_specs = list(out_specs) + fused.out_specs
        out_shape = list(out_shape) + fused.out_shape
        scratch_shapes = list(scratch_shapes) + fused.scratch_shapes
        sem = tuple("arbitrary" for _ in sem)
    res = pl.pallas_call(_host_body(body, n_in, n_out, fused, first_last), name=name, grid=grid, in_specs=in_specs,
                         out_specs=out_specs, out_shape=out_shape, scratch_shapes=scratch_shapes,
                         compiler_params=_params(sem))(*operands)
    return list(res[:n_out]), list(res[n_out:])


def _forward_to_sibling(name, gathered):
    n_arr = len(gathered)

    def body(*refs):
        ins, outs = refs[:n_arr], refs[n_arr:2 * n_arr]
        send_sems, recv_sems = refs[2 * n_arr:]
        x, y, c = _mesh_pos()
        chips = [(1 - x, y), (x, 1 - y), (1 - x, 1 - y)]

        def copy(a, j, pc):
            px, py = chips[j]
            s = 4 * px + 2 * py + pc
            return pltpu.make_async_remote_copy(
                src_ref=ins[a].at[s], dst_ref=outs[a].at[s], send_sem=send_sems.at[a, j], recv_sem=recv_sems.at[a, j],
                device_id=(x, y, 1 - c), device_id_type=MESH)

        for a in range(n_arr):
            for j in range(3):
                copy(a, j, c).start()
        for a in range(n_arr):
            for j in range(3):
                copy(a, j, 1 - c).wait_recv()
                copy(a, j, c).wait_send()

    return pl.pallas_call(
        body, name=name,
        out_shape=[jax.ShapeDtypeStruct(g.shape, g.dtype) for g in gathered],
        in_specs=[HBM_SPEC] * n_arr, out_specs=[HBM_SPEC] * n_arr,
        input_output_aliases={a: a for a in range(n_arr)},
        scratch_shapes=[pltpu.SemaphoreType.DMA((n_arr, 3)), pltpu.SemaphoreType.DMA((n_arr, 3))],
    )(*gathered)


def _exchange_sibling(name, partials):
    n_arr = len(partials)

    def body(*refs):
        ins, outs = refs[:n_arr], refs[n_arr:2 * n_arr]
        send_sems, recv_sems = refs[2 * n_arr:]
        x, y, c = _mesh_pos()
        copies = [pltpu.make_async_remote_copy(
            src_ref=ins[a].at[2 * q + 1 - c], dst_ref=outs[a].at[q], send_sem=send_sems.at[a, q],
            recv_sem=recv_sems.at[a, q], device_id=(x, y, 1 - c), device_id_type=MESH)
            for a in range(n_arr) for q in range(4)]
        for cp in copies:
            cp.start()
        for cp in copies:
            cp.wait_recv()
        for cp in copies:
            cp.wait_send()

    return pl.pallas_call(
        body, name=name,
        out_shape=[jax.ShapeDtypeStruct((4,) + p.shape[1:], p.dtype) for p in partials],
        in_specs=[HBM_SPEC] * n_arr, out_specs=[HBM_SPEC] * n_arr,
        scratch_shapes=[pltpu.SemaphoreType.DMA((n_arr, 4)), pltpu.SemaphoreType.DMA((n_arr, 4))],
    )(*partials)


def _exchange_chips(name, chip_sums):
    n_arr = len(chip_sums)

    def body(*refs):
        ins, outs = refs[:n_arr], refs[n_arr:2 * n_arr]
        send_sems, recv_sems = refs[2 * n_arr:]
        x, y, c = _mesh_pos()
        chips = [(1 - x, y), (x, 1 - y), (1 - x, 1 - y)]
        copies = []
        for a in range(n_arr):
            for j, (px, py) in enumerate(chips):
                copies.append(pltpu.make_async_remote_copy(
                    src_ref=ins[a].at[2 * px + py], dst_ref=outs[a].at[j],
                    send_sem=send_sems.at[a, j], recv_sem=recv_sems.at[a, j],
                    device_id=(px, py, c), device_id_type=MESH))
        for cp in copies:
            cp.start()
        for cp in copies:
            cp.wait_recv()
        for cp in copies:
            cp.wait_send()

    hbm = pl.BlockSpec(memory_space=pltpu.HBM)
    return pl.pallas_call(
        body, name=name,
        out_shape=[jax.ShapeDtypeStruct((3,) + p.shape[1:], p.dtype) for p in chip_sums],
        in_specs=[hbm] * n_arr, out_specs=[hbm] * n_arr,
        scratch_shapes=[pltpu.SemaphoreType.DMA((n_arr, 3)), pltpu.SemaphoreType.DMA((n_arr, 3))],
    )(*chip_sums)


def _matmul(name, a, b, dims, grid, a_spec, b_spec, o_spec, out_shape, acc_axis=None, fused=None, fused_arrays=()):
    def body(a_ref, b_ref, o_ref):
        r = lax.dot_general(a_ref[...], b_ref[...], dims, preferred_element_type=F32)
        if acc_axis is None:
            o_ref[...] = r.astype(o_ref.dtype)
        else:
            k = pl.program_id(acc_axis)

            @pl.when(k == 0)
            def _():
                o_ref[...] = r

            @pl.when(k > 0)
            def _():
                o_ref[...] += r

    sem = tuple("arbitrary" if i == acc_axis else "parallel" for i in range(len(grid)))
    if fused is None:
        return pl.pallas_call(body, name=name, grid=grid, in_specs=[a_spec, b_spec], out_specs=o_spec,
                              out_shape=out_shape, compiler_params=_params(sem))(a, b)

    def first_last():
        first = last = None
        for ax, n in enumerate(grid):
            f, l = pl.program_id(ax) == 0, pl.program_id(ax) == n - 1
            first, last = (f, l) if first is None else (first & f, last & l)
        return first, last

    (out,), extra = _host_call(body, 2, 1, fused, first_last, name=name, grid=grid, in_specs=[a_spec, b_spec],
                               out_specs=[o_spec], out_shape=[out_shape], scratch_shapes=[], sem=sem,
                               operands=[a, b] + list(fused_arrays))
    return out, extra


def _mm_blocked_rhs(name, a, w_g, tm=512, fused=None, fused_arrays=()):
    m, k = a.shape
    nb = w_g.shape[2]
    return _matmul(name, a, w_g, NN, (N_DEV, m // tm),
                   pl.BlockSpec((tm, k), lambda j, i: (i, 0)),
                   pl.BlockSpec((None, k, nb), lambda j, i: (j, 0, 0)),
                   pl.BlockSpec((tm, nb), lambda j, i: (i, j)),
                   jax.ShapeDtypeStruct((m, N_DEV * nb), F32), fused=fused, fused_arrays=fused_arrays)


def _mm_blocked_rhs_t(name, a, w_g, tm=512, fused=None, fused_arrays=()):
    m = a.shape[0]
    n, nb = w_g.shape[1], w_g.shape[2]
    return _matmul(name, a, w_g, NT, (m // tm, N_DEV),
                   pl.BlockSpec((tm, nb), lambda i, j: (i, j)),
                   pl.BlockSpec((None, n, nb), lambda i, j: (j, 0, 0)),
                   pl.BlockSpec((tm, n), lambda i, j: (i, 0)),
                   jax.ShapeDtypeStruct((m, n), F32), acc_axis=1, fused=fused, fused_arrays=fused_arrays)


def _mm_wgrad_blocked(name, act, dcols, tk=512):
    t, k = act.shape
    nb = dcols.shape[1] // N_DEV
    return _matmul(name, act, dcols, TN, (N_DEV, k // tk),
                   pl.BlockSpec((t, tk), lambda j, i: (0, i)),
                   pl.BlockSpec((t, nb), lambda j, i: (0, j)),
                   pl.BlockSpec((None, tk, nb), lambda j, i: (j, i, 0)),
                   jax.ShapeDtypeStruct((N_DEV, k, nb), BF16))


def _mm_plain(name, a, b, dims, tm, tn, out_dtype):
    if dims == NN:
        (m, k), n = a.shape, b.shape[1]
        a_spec = pl.BlockSpec((tm, k), lambda i, j: (i, 0))
        b_spec = pl.BlockSpec((k, tn), lambda i, j: (0, j))
    elif dims == NT:
        (m, k), n = a.shape, b.shape[0]
        a_spec = pl.BlockSpec((tm, k), lambda i, j: (i, 0))
        b_spec = pl.BlockSpec((tn, k), lambda i, j: (j, 0))
    else:
        (k, m), n = a.shape, b.shape[1]
        a_spec = pl.BlockSpec((k, tm), lambda i, j: (0, i))
        b_spec = pl.BlockSpec((k, tn), lambda i, j: (0, j))
    return _matmul(name, a, b, dims, (m // tm, n // tn), a_spec, b_spec,
                   pl.BlockSpec((tm, tn), lambda i, j: (i, j)), jax.ShapeDtypeStruct((m, n), out_dtype))


def _ada_fwd(c_all, w_ada_blk, b_blk):
    def body(c_ref, w_ref, b_ref, o_ref):
        cv = c_ref[...]
        o_ref[...] = jnp.dot(cv * _sigmoid(cv), w_ref[...], preferred_element_type=F32) + b_ref[...]

    tn = 512
    return pl.pallas_call(
        body, name="ada_fwd", grid=(ADA_BLK // tn,),
        in_specs=[pl.BlockSpec((N_DEV, D_MODEL), lambda j: (0, 0)),
                  pl.BlockSpec((D_MODEL, tn), lambda j: (0, j)),
                  pl.BlockSpec((1, tn), lambda j: (0, j))],
        out_specs=pl.BlockSpec((N_DEV, tn), lambda j: (0, j)),
        out_shape=jax.ShapeDtypeStruct((N_DEV, ADA_BLK), F32),
        compiler_params=_params(("parallel",)))(c_all, w_ada_blk, b_blk)


def _ada_wgrad(c_all, gmod_cols):
    def body(c_ref, g_ref, o_ref):
        cv = c_ref[...]
        o_ref[...] = lax.dot_general(cv * _sigmoid(cv), g_ref[...], TN, preferred_element_type=F32)

    tk = 512
    return pl.pallas_call(
        body, name="ada_wgrad", grid=(D_MODEL // tk,),
        in_specs=[pl.BlockSpec((N_DEV, tk), lambda i: (0, i)),
                  pl.BlockSpec((N_DEV, ADA_BLK), lambda i: (0, 0))],
        out_specs=pl.BlockSpec((tk, ADA_BLK), lambda i: (i, 0)),
        out_shape=jax.ShapeDtypeStruct((D_MODEL, ADA_BLK), F32),
        compiler_params=_params(("parallel",)))(c_all, gmod_cols)


def _row_spec(cols=D_MODEL):
    return pl.BlockSpec((ROW_TILE, cols), lambda i: (i, 0))


def _vec_spec(cols=D_MODEL):
    return pl.BlockSpec((1, cols), lambda i: (0, 0))


def _norm_fwd(name, x, w, scale, shift, resid=None, gate=None):
    has_res = resid is not None

    def body(*refs):
        if has_res:
            x_ref, r_ref, g_ref, w_ref, sc_ref, sh_ref, xr_ref, h_ref, rs_ref = refs
            xr = x_ref[...] + g_ref[...] * r_ref[...]
            xr_ref[...] = xr
        else:
            x_ref, w_ref, sc_ref, sh_ref, h_ref, rs_ref = refs
            xr = x_ref[...]
        rs = lax.rsqrt(jnp.mean(xr * xr, axis=-1, keepdims=True) + EPS)
        h = (xr * rs) * w_ref[...] * (1.0 + sc_ref[...]) + sh_ref[...]
        h_ref[...] = h.astype(BF16)
        rs_ref[...] = rs

    s = x.shape[0]
    ins = [x] + ([resid, gate] if has_res else []) + [w, scale, shift]
    in_specs = [_row_spec()] + ([_row_spec(), _vec_spec()] if has_res else []) + [_vec_spec()] * 3
    outs = ([jax.ShapeDtypeStruct((s, D_MODEL), F32)] if has_res else []) + [
        jax.ShapeDtypeStruct((s, D_MODEL), BF16), jax.ShapeDtypeStruct((s, 1), F32)]
    out_specs = ([_row_spec()] if has_res else []) + [_row_spec(), pl.BlockSpec((ROW_TILE, 1), lambda i: (i, 0))]
    return pl.pallas_call(body, name=name, grid=(s // ROW_TILE,), in_specs=in_specs, out_specs=out_specs,
                          out_shape=outs, compiler_params=_params(("parallel",)))(*ins)


def _norm_bwd(name, dh, x, rstd, w, scale, dres, mix=None, gate=None):
    has_mix = mix is not None

    def body(*refs):
        if has_mix:
            (dh_ref, x_ref, rs_ref, w_ref, sc_ref, dr_ref, mix_ref, g_ref,
             dx_ref, dmix_ref, dsh_ref, dsc_ref, dw_ref, dg_ref) = refs
        else:
            dh_ref, x_ref, rs_ref, w_ref, sc_ref, dr_ref, dx_ref, dsh_ref, dsc_ref, dw_ref = refs
        i = pl.program_id(0)
        dhv = dh_ref[...]
        rs = rs_ref[...]
        xn = x_ref[...] * rs
        wv = w_ref[...]
        one_sc = 1.0 + sc_ref[...]
        dxn = dhv * wv * one_sc
        dx = dr_ref[...] + rs * (dxn - xn * jnp.mean(dxn * xn, axis=-1, keepdims=True))
        dx_ref[...] = dx
        sums = [(dsh_ref, dhv), (dsc_ref, dhv * xn * wv), (dw_ref, dhv * one_sc * xn)]
        if has_mix:
            dmix_ref[...] = (dx * g_ref[...]).astype(BF16)
            sums.append((dg_ref, dx * mix_ref[...]))

        @pl.when(i == 0)
        def _():
            for ref, _v in sums:
                ref[...] = jnp.zeros_like(ref)

        for ref, v in sums:
            ref[...] += jnp.sum(v, axis=0, keepdims=True)

    s = x.shape[0]
    ins = [dh, x, rstd, w, scale, dres] + ([mix, gate] if has_mix else [])
    in_specs = ([_row_spec(), _row_spec(), pl.BlockSpec((ROW_TILE, 1), lambda i: (i, 0)), _vec_spec(), _vec_spec(),
                 _row_spec()] + ([_row_spec(), _vec_spec()] if has_mix else []))
    vec = jax.ShapeDtypeStruct((1, D_MODEL), F32)
    outs = ([jax.ShapeDtypeStruct((s, D_MODEL), F32)] + ([jax.ShapeDtypeStruct((s, D_MODEL), BF16)] if has_mix else [])
            + [vec] * (4 if has_mix else 3))
    out_specs = [_row_spec()] + ([_row_spec()] if has_mix else []) + [_vec_spec()] * (4 if has_mix else 3)
    return pl.pallas_call(body, name=name, grid=(s // ROW_TILE,), in_specs=in_specs, out_specs=out_specs,
                          out_shape=outs, compiler_params=_params(("arbitrary",)))(*ins)


def _loss_head(x1, ffn, gate2, target):
    def body(x_ref, f_ref, g_ref, t_ref, loss_ref, dout_ref, dffn_ref, dg_ref):
        i = pl.program_id(0)
        fv = f_ref[...]
        gv = g_ref[...]
        err = x_ref[...] + gv * fv - t_ref[...]
        dout = err * (1.0 / D_MODEL)
        dout_ref[...] = dout
        dffn_ref[...] = (dout * gv).astype(BF16)

        @pl.when(i == 0)
        def _():
            loss_ref[...] = jnp.zeros_like(loss_ref)
            dg_ref[...] = jnp.zeros_like(dg_ref)

        row = jnp.sum(err * err, axis=-1, keepdims=True) * (1.0 / D_MODEL)
        loss_ref[...] += jnp.broadcast_to(0.5 * jnp.sum(row, axis=0, keepdims=True), (1, 128))
        dg_ref[...] += jnp.sum(dout * fv, axis=0, keepdims=True)

    s = x1.shape[0]
    return pl.pallas_call(
        body, name="loss_head", grid=(s // ROW_TILE,),
        in_specs=[_row_spec(), _row_spec(), _vec_spec(), _row_spec()],
        out_specs=[pl.BlockSpec((1, 128), lambda i: (0, 0)), _row_spec(), _row_spec(), _vec_spec()],
        out_shape=[jax.ShapeDtypeStruct((1, 128), F32), jax.ShapeDtypeStruct((s, D_MODEL), F32),
                   jax.ShapeDtypeStruct((s, D_MODEL), BF16), jax.ShapeDtypeStruct((1, D_MODEL), F32)],
        compiler_params=_params(("arbitrary",)))(x1, ffn, gate2, target)


CONV_TILE = 512
N_CONV_TILES = D_FF // CONV_TILE


def _shift_rows(a, k, row):
    n = a.shape[0]
    if k > 0:
        return jnp.where(row >= k, pltpu.roll(a, k, 0), 0.0)
    return jnp.where(row < n + k, pltpu.roll(a, n + k, 0), 0.0)


def _conv_gate_fwd(u, conv_w, conv_b):
    s = u.shape[0]

    def body(a_ref, g_ref, w_ref, b_ref, y_ref):
        a = a_ref[...]
        w = w_ref[...]
        row = lax.broadcasted_iota(jnp.int32, a.shape, 0)
        ac = b_ref[...] + _shift_rows(a, 2, row) * w[0:1] + _shift_rows(a, 1, row) * w[1:2] + a * w[2:3]
        y_ref[...] = (ac * _sigmoid(ac) * g_ref[...]).astype(BF16)

    col = lambda off: pl.BlockSpec((s, CONV_TILE), lambda i: (0, i + off))
    return pl.pallas_call(
        body, name="conv_gate_fwd", grid=(N_CONV_TILES,),
        in_specs=[col(0), col(N_CONV_TILES), pl.BlockSpec((3, CONV_TILE), lambda i: (0, i)),
                  pl.BlockSpec((1, CONV_TILE), lambda i: (0, i))],
        out_specs=col(0), out_shape=jax.ShapeDtypeStruct((s, D_FF), BF16),
        compiler_params=_params(("parallel",)))(u, u, conv_w, conv_b)


def _conv_gate_bwd(u, dy, conv_w, conv_b):
    s = u.shape[0]

    def body(a_ref, g_ref, dy_ref, w_ref, b_ref, da_ref, dg_ref, gw_ref, gb_ref):
        a = a_ref[...]
        w = w_ref[...]
        row = lax.broadcasted_iota(jnp.int32, a.shape, 0)
        a1 = _shift_rows(a, 1, row)
        a2 = _shift_rows(a, 2, row)
        ac = b_ref[...] + a2 * w[0:1] + a1 * w[1:2] + a * w[2:3]
        sg = _sigmoid(ac)
        dyv = dy_ref[...]
        dg_ref[...] = (dyv * (ac * sg)).astype(BF16)
        dac = dyv * g_ref[...] * _dsilu(ac, sg)
        gb_ref[...] = jnp.sum(dac, axis=0, keepdims=True)
        gw_ref[0:1, :] = jnp.sum(dac * a2, axis=0, keepdims=True)
        gw_ref[1:2, :] = jnp.sum(dac * a1, axis=0, keepdims=True)
        gw_ref[2:3, :] = jnp.sum(dac * a, axis=0, keepdims=True)
        da = dac * w[2:3] + _shift_rows(dac, -1, row) * w[1:2] + _shift_rows(dac, -2, row) * w[0:1]
        da_ref[...] = da.astype(BF16)

    col = lambda off: pl.BlockSpec((s, CONV_TILE), lambda i: (0, i + off))
    return pl.pallas_call(
        body, name="conv_gate_bwd", grid=(N_CONV_TILES,),
        in_specs=[col(0), col(N_CONV_TILES), col(0), pl.BlockSpec((3, CONV_TILE), lambda i: (0, i)),
                  pl.BlockSpec((1, CONV_TILE), lambda i: (0, i))],
        out_specs=[col(0), col(0), pl.BlockSpec((3, CONV_TILE), lambda i: (0, i)),
                   pl.BlockSpec((1, CONV_TILE), lambda i: (0, i))],
        out_shape=[jax.ShapeDtypeStruct((s, D_FF), BF16), jax.ShapeDtypeStruct((s, D_FF), BF16),
                   jax.ShapeDtypeStruct((3, D_FF), F32), jax.ShapeDtypeStruct((1, D_FF), F32)],
        compiler_params=_params(("parallel",)))(u, u, dy, conv_w, conv_b)


HG_TILE = 128
CHUNK_UNROLL = 8


def _unrolled_loop(n, body, init):
    def group(i, carry):
        for u in range(CHUNK_UNROLL):
            carry = body(i * CHUNK_UNROLL + u, carry)
        return carry

    return lax.fori_loop(0, n // CHUNK_UNROLL, group, init)


def _head_col(off):
    return pl.BlockSpec((SEQ, HEAD_DIM), lambda h: (0, h + off))


def _hgrn_gates(hq, hf, lb, pos):
    q = hq * _sigmoid(hq)
    sig = _sigmoid(hf)
    f = lb + (1.0 - lb) * sig
    gl = jnp.log(f)
    for sh in (1, 2, 4, 8):
        gl = gl + jnp.where(pos >= sh, pltpu.roll(gl, sh, 0), 0.0)
    return q, sig, f, 1.0 - f, gl


def _lower_bound(lbl):
    return 1.0 / (1.0 + jnp.exp(lbl[1:2, :] - lbl[0:1, :]))


def _head_first_last():
    h = pl.program_id(0)
    return h == 0, h == HEADS - 1


def _hgrn_fwd(proj, lb_logits, norm_w, fused=None, fused_arrays=()):
    n_tiles = SEQ // HG_TILE
    n_chunks = SEQ // CHUNK
    fused_arrays = list(fused_arrays)

    def body(hq_ref, hf_ref, hi_ref, hg_ref, lbl_ref, nw_ref, aout_ref, opre_ref, q_s, k_s, gl_s):
        lb = _lower_bound(lbl_ref[...])
        ones = jnp.ones((HEAD_DIM, HEAD_DIM), BF16)
        pos = lax.broadcasted_iota(jnp.int32, (HG_TILE, HEAD_DIM), 0) % CHUNK

        def tile(i, carry):
            rows = pl.ds(pl.multiple_of(i * HG_TILE, HG_TILE), HG_TILE)
            v = hi_ref[rows, :]
            q, _sig, _f, kk, gl = _hgrn_gates(hq_ref[rows, :], hf_ref[rows, :], lb, pos)
            o = _lane_sum(q * kk, ones) * v
            for d in range(1, CHUNK):
                e = jnp.where(pos >= d, jnp.exp(gl - pltpu.roll(gl, d, 0)), 0.0)
                o = o + _lane_sum(q * pltpu.roll(kk, d, 0) * e, ones) * pltpu.roll(v, d, 0)
            q_s[rows, :] = q
            k_s[rows, :] = kk
            gl_s[rows, :] = gl
            opre_ref[rows, :] = o
            return carry

        lax.fori_loop(0, n_tiles, tile, 0)

        def chunk(c, st):
            rows = pl.ds(pl.multiple_of(c * CHUNK, CHUNK), CHUNK)
            gl = gl_s[rows, :]
            qt = q_s[rows, :] * jnp.exp(gl)
            opre_ref[rows, :] += lax.dot_general(qt.astype(BF16), st.astype(BF16), NT, preferred_element_type=F32)
            gll = gl[CHUNK - 1:CHUNK, :]
            kt = k_s[rows, :] * jnp.exp(gll - gl)
            return st * jnp.exp(gll) + lax.dot_general(hi_ref[rows, :].astype(BF16), kt.astype(BF16), TN,
                                                       preferred_element_type=F32)

        _unrolled_loop(n_chunks, chunk, jnp.zeros((HEAD_DIM, HEAD_DIM), F32))

        def finish(i, carry):
            rows = pl.ds(pl.multiple_of(i * HG_TILE, HG_TILE), HG_TILE)
            o = opre_ref[rows, :]
            hg = hg_ref[rows, :]
            rs = lax.rsqrt(jnp.mean(o * o, axis=-1, keepdims=True) + EPS)
            aout_ref[rows, :] = ((o * rs) * nw_ref[...] * (hg * _sigmoid(hg))).astype(BF16)
            return carry

        lax.fori_loop(0, n_tiles, finish, 0)

    return _host_call(
        body, 6, 2, fused, _head_first_last, name="hgrn_fwd", grid=(HEADS,),
        in_specs=[_head_col(0), _head_col(HEADS), _head_col(2 * HEADS), _head_col(3 * HEADS),
                  pl.BlockSpec((2, HEAD_DIM), lambda h: (0, h)), pl.BlockSpec((1, HEAD_DIM), lambda h: (0, 0))],
        out_specs=[_head_col(0), _head_col(0)],
        out_shape=[jax.ShapeDtypeStruct((SEQ, HEADS * HEAD_DIM), BF16), jax.ShapeDtypeStruct((SEQ, HEADS * HEAD_DIM), F32)],
        scratch_shapes=[pltpu.VMEM((SEQ, HEAD_DIM), F32)] * 3, sem=("parallel",),
        operands=[proj, proj, proj, proj, lb_logits, norm_w] + fused_arrays)


def _hgrn_bwd(proj, lb_logits, norm_w, o_pre, d_aout, fused=None, fused_arrays=()):
    n_tiles = SEQ // HG_TILE
    n_chunks = SEQ // CHUNK

    def body(hq_ref, hf_ref, hi_ref, hg_ref, lbl_ref, nw_ref, opre_ref, da_ref,
             dhq_ref, dhf_ref, dhi_ref, dhg_ref, dlog_ref, gnw_ref,
             q_s, k_s, gl_s, do_s, dq_s, dk_s, dv_s, st_s):
        h = pl.program_id(0)
        lb = _lower_bound(lbl_ref[...])
        nw = nw_ref[...]
        ones = jnp.ones((HEAD_DIM, HEAD_DIM), BF16)
        pos = lax.broadcasted_iota(jnp.int32, (HG_TILE, HEAD_DIM), 0) % CHUNK

        @pl.when(h == 0)
        def _():
            gnw_ref[...] = jnp.zeros_like(gnw_ref)

        def tile(i, carry):
            rows = pl.ds(pl.multiple_of(i * HG_TILE, HG_TILE), HG_TILE)
            v = hi_ref[rows, :]
            q, _sig, _f, kk, gl = _hgrn_gates(hq_ref[rows, :], hf_ref[rows, :], lb, pos)
            o = opre_ref[rows, :]
            hg = hg_ref[rows, :]
            da = da_ref[rows, :]
            rs = lax.rsqrt(jnp.mean(o * o, axis=-1, keepdims=True) + EPS)
            oh = o * rs
            sg = _sigmoid(hg)
            dnorm = da * (hg * sg)
            dhg_ref[rows, :] = (da * (oh * nw) * _dsilu(hg, sg)).astype(BF16)
            gnw_ref[...] += jnp.sum(dnorm * oh, axis=0, keepdims=True)
            doh = dnorm * nw
            do = rs * (doh - oh * jnp.mean(doh * oh, axis=-1, keepdims=True))

            d_a = _lane_sum(do * v, ones)
            dq = d_a * kk
            dk = d_a * q
            dv = _lane_sum(q * kk, ones) * do
            for d in range(1, CHUNK):
                ks = pltpu.roll(kk, d, 0)
                e = jnp.where(pos >= d, jnp.exp(gl - pltpu.roll(gl, d, 0)), 0.0)
                a_d = _lane_sum(q * ks * e, ones)
                d_a = _lane_sum(do * pltpu.roll(v, d, 0), ones) * e
                dq = dq + d_a * ks
                dk = dk + pltpu.roll(d_a * q, HG_TILE - d, 0)
                dv = dv + pltpu.roll(a_d * do, HG_TILE - d, 0)
            q_s[rows, :] = q
            k_s[rows, :] = kk
            gl_s[rows, :] = gl
            do_s[rows, :] = do
            dq_s[rows, :] = dq
            dk_s[rows, :] = dk
            dv_s[rows, :] = dv
            return carry

        lax.fori_loop(0, n_tiles, tile, 0)

        def fwd_chunk(c, st):
            rows = pl.ds(pl.multiple_of(c * CHUNK, CHUNK), CHUNK)
            gl = gl_s[rows, :]
            st_s[c] = st
            dq_s[rows, :] += jnp.dot(do_s[rows, :].astype(BF16), st.astype(BF16),
                                     preferred_element_type=F32) * jnp.exp(gl)
            gll = gl[CHUNK - 1:CHUNK, :]
            kt = k_s[rows, :] * jnp.exp(gll - gl)
            return st * jnp.exp(gll) + lax.dot_general(hi_ref[rows, :].astype(BF16), kt.astype(BF16), TN,
                                                       preferred_element_type=F32)

        _unrolled_loop(n_chunks, fwd_chunk, jnp.zeros((HEAD_DIM, HEAD_DIM), F32))

        pos_c = lax.broadcasted_iota(jnp.int32, (CHUNK, HEAD_DIM), 0)

        def bwd_chunk(i, carry):
            rt, dlb = carry
            c = n_chunks - 1 - i
            rows = pl.ds(pl.multiple_of(c * CHUNK, CHUNK), CHUNK)
            gl = gl_s[rows, :]
            q = q_s[rows, :]
            kk = k_s[rows, :]
            do = do_s[rows, :]
            gll = gl[CHUNK - 1:CHUNK, :]
            egl = jnp.exp(gll)
            ekt = jnp.exp(gll - gl)
            rt_b = rt.astype(BF16)
            dk_in = dk_s[rows, :]
            dk_far = jnp.dot(hi_ref[rows, :].astype(BF16), rt_b, preferred_element_type=F32) * ekt
            dk = dk_in + dk_far
            dv = dv_s[rows, :] + lax.dot_general((kk * ekt).astype(BF16), rt_b, NT, preferred_element_type=F32)
            dq = dq_s[rows, :]
            rc = q * dq - kk * dk_in
            pc = kk * dk_far
            pre = pc
            for sh in (1, 2, 4, 8):
                rc = rc + jnp.where(pos_c < CHUNK - sh, pltpu.roll(rc, CHUNK - sh, 0), 0.0)
                pre = pre + jnp.where(pos_c >= sh, pltpu.roll(pre, sh, 0), 0.0)
            across = jnp.sum(st_s[c] * rt, axis=0, keepdims=True) * egl
            dgl = rc + (pre - pc) + across
            hf = hf_ref[rows, :]
            sig = _sigmoid(hf)
            f = lb + (1.0 - lb) * sig
            df = dgl / f - dk
            dhf_ref[rows, :] = (df * (1.0 - lb) * sig * (1.0 - sig)).astype(BF16)
            hq = hq_ref[rows, :]
            dhq_ref[rows, :] = (dq * _dsilu(hq, _sigmoid(hq))).astype(BF16)
            dhi_ref[rows, :] = dv.astype(BF16)
            rt_new = rt * egl + lax.dot_general(do.astype(BF16), (q * jnp.exp(gl)).astype(BF16), TN,
                                                preferred_element_type=F32)
            return (rt_new, dlb + jnp.sum(df * (1.0 - sig), axis=0, keepdims=True))

        _, dlb = _unrolled_loop(n_chunks, bwd_chunk,
                                (jnp.zeros((HEAD_DIM, HEAD_DIM), F32), jnp.zeros((1, HEAD_DIM), F32)))
        dl0 = lb * (1.0 - lb) * dlb
        dlog_ref[0:1, :] = dl0
        dlog_ref[1:2, :] = -dl0

    wide = HEADS * HEAD_DIM
    return _host_call(
        body, 8, 6, fused, _head_first_last, name="hgrn_bwd", grid=(HEADS,),
        in_specs=[_head_col(0), _head_col(HEADS), _head_col(2 * HEADS), _head_col(3 * HEADS),
                  pl.BlockSpec((2, HEAD_DIM), lambda h: (0, h)), pl.BlockSpec((1, HEAD_DIM), lambda h: (0, 0)),
                  _head_col(0), _head_col(0)],
        out_specs=[_head_col(0)] * 4 + [pl.BlockSpec((2, HEAD_DIM), lambda h: (0, h)),
                                        pl.BlockSpec((1, HEAD_DIM), lambda h: (0, 0))],
        out_shape=[jax.ShapeDtypeStruct((SEQ, wide), BF16)] * 4 + [jax.ShapeDtypeStruct((2, wide), F32),
                                                                    jax.ShapeDtypeStruct((1, HEAD_DIM), F32)],
        scratch_shapes=[pltpu.VMEM((SEQ, HEAD_DIM), F32)] * 7 + [pltpu.VMEM((n_chunks, HEAD_DIM, HEAD_DIM), F32)],
        sem=("arbitrary",),
        operands=[proj, proj, proj, proj, lb_logits, norm_w, o_pre, d_aout] + list(fused_arrays))


Q_TILE = 256
ATT_SCALE = HEAD_DIM ** -0.5
ATT_OFF = 4 * HEADS


def _qk_prep(proj, q_w, k_w):
    def body(aq_ref, ak_ref, av_ref, qw_ref, kw_ref, qn_ref, kn_ref, v_ref):
        aq = aq_ref[...]
        ak = ak_ref[...]
        qn_ref[...] = (aq * lax.rsqrt(jnp.mean(aq * aq, axis=-1, keepdims=True) + EPS) * qw_ref[...]).astype(BF16)
        kn_ref[...] = (ak * lax.rsqrt(jnp.mean(ak * ak, axis=-1, keepdims=True) + EPS) * kw_ref[...]).astype(BF16)
        v_ref[...] = av_ref[...].astype(BF16)

    wide = HEADS * HEAD_DIM
    vec = pl.BlockSpec((1, HEAD_DIM), lambda h: (0, 0))
    return pl.pallas_call(
        body, name="qk_prep", grid=(HEADS,),
        in_specs=[_head_col(ATT_OFF), _head_col(ATT_OFF + HEADS), _head_col(ATT_OFF + 2 * HEADS), vec, vec],
        out_specs=[_head_col(0)] * 3, out_shape=[jax.ShapeDtypeStruct((SEQ, wide), BF16)] * 3,
        compiler_params=_params(("parallel",)))(proj, proj, proj, q_w, k_w)


def _alibi_slopes():
    slopes = jnp.exp2(-8.0 * jnp.arange(1, HEADS + 1, dtype=F32) / HEADS)
    return jnp.broadcast_to(slopes[:, None, None], (HEADS, 1, HEAD_DIM))


SLOPE_SPEC = pl.BlockSpec((None, 1, HEAD_DIM), lambda h, i: (h, 0, 0))


N_Q_TILES = SEQ // Q_TILE
K_BLOCK = 512
NOT_ATTENDED = 1e35


def _att_tables():
    o = jnp.arange(N_Q_TILES, dtype=jnp.int32)[:, None, None]
    r = jnp.arange(Q_TILE, dtype=jnp.int32)[None, :, None]
    c = jnp.arange(K_BLOCK, dtype=jnp.int32)[None, None, :]
    dist = o * Q_TILE + r - c
    mult = ((dist <= 128).astype(F32) + (((dist % 4) == 0) & (dist <= 512)).astype(F32)
            + ((dist % 16) == 0).astype(F32))
    valid = (dist >= 0) & (mult > 0)
    return (jnp.where(valid, dist.astype(F32), NOT_ATTENDED),
            jnp.where(valid, jnp.log(jnp.maximum(mult, 1.0)), 0.0))


TABLE_SPEC = pl.BlockSpec((N_Q_TILES, Q_TILE, K_BLOCK), lambda h, i: (0, 0, 0))


def _att_block(q, k_ref, j, i, slope, dist_ref, lmul_ref):
    rows = pl.ds(pl.multiple_of(j * K_BLOCK, K_BLOCK), K_BLOCK)
    off = i - j * (K_BLOCK // Q_TILE)
    s = lax.dot_general(q, k_ref[rows, :], NT, preferred_element_type=F32) * ATT_SCALE
    return s - slope * dist_ref[off] + lmul_ref[off], rows


def _n_key_blocks(i):
    return (i + K_BLOCK // Q_TILE) // (K_BLOCK // Q_TILE)


def _att_first_last():
    h, i = pl.program_id(0), pl.program_id(1)
    return (h == 0) & (i == 0), (h == HEADS - 1) & (i == N_Q_TILES - 1)


def _attn_fwd(qn, kn, vb, fused=None, fused_arrays=()):
    def body(q_ref, k_ref, v_ref, sl_ref, dist_ref, lmul_ref, o_ref, lse_ref):
        i = pl.program_id(1)
        q = q_ref[...]
        slope = sl_ref[0:1, 0:1]

        def step(j, carry):
            m, l, acc = carry
            sb, rows = _att_block(q, k_ref, j, i, slope, dist_ref, lmul_ref)
            m_new = jnp.maximum(m, jnp.max(sb, axis=-1, keepdims=True))
            alpha = jnp.exp(m - m_new)
            p = jnp.exp(sb - m_new)
            l = alpha * l + jnp.sum(p, axis=-1, keepdims=True)
            acc = alpha * acc + jnp.dot(p.astype(BF16), v_ref[rows, :], preferred_element_type=F32)
            return m_new, l, acc

        m, l, acc = lax.fori_loop(0, _n_key_blocks(i), step,
                                  (jnp.full((Q_TILE, 1), -1e30, F32), jnp.zeros((Q_TILE, 1), F32),
                                   jnp.zeros((Q_TILE, HEAD_DIM), F32)))
        o_ref[...] = acc / l
        lse_ref[...] = m + jnp.log(l)

    wide = HEADS * HEAD_DIM
    qt = pl.BlockSpec((Q_TILE, HEAD_DIM), lambda h, i: (i, h))
    full = pl.BlockSpec((SEQ, HEAD_DIM), lambda h, i: (0, h))
    return _host_call(
        body, 6, 2, fused, _att_first_last, name="attn_fwd", grid=(HEADS, N_Q_TILES),
        in_specs=[qt, full, full, SLOPE_SPEC, TABLE_SPEC, TABLE_SPEC],
        out_specs=[qt, pl.BlockSpec((None, Q_TILE, 1), lambda h, i: (h, i, 0))],
        out_shape=[jax.ShapeDtypeStruct((SEQ, wide), F32), jax.ShapeDtypeStruct((HEADS, SEQ, 1), F32)],
        scratch_shapes=[], sem=("parallel", "parallel"),
        operands=[qn, kn, vb, _alibi_slopes(), *_att_tables()] + list(fused_arrays))


def _attn_bwd(qn, kn, vb, o, lse, d_mix, fused=None, fused_arrays=()):
    def body(q_ref, k_ref, v_ref, o_ref, lse_ref, do_ref, sl_ref, dist_ref, lmul_ref, dq_ref, dk_ref, dv_ref):
        i = pl.program_id(1)
        q = q_ref[...]
        do = do_ref[...]
        do_b = do.astype(BF16)
        slope = sl_ref[0:1, 0:1]
        lse = lse_ref[...]
        delta = jnp.sum(do * o_ref[...], axis=-1, keepdims=True)

        @pl.when(i == 0)
        def _():
            dk_ref[...] = jnp.zeros_like(dk_ref)
            dv_ref[...] = jnp.zeros_like(dv_ref)

        def step(j, dq):
            sb, rows = _att_block(q, k_ref, j, i, slope, dist_ref, lmul_ref)
            p = jnp.exp(sb - lse)
            dp = lax.dot_general(do_b, v_ref[rows, :], NT, preferred_element_type=F32)
            ds = (p * (dp - delta)).astype(BF16)
            dk_ref[rows, :] += lax.dot_general(ds, q, TN, preferred_element_type=F32) * ATT_SCALE
            dv_ref[rows, :] += lax.dot_general(p.astype(BF16), do_b, TN, preferred_element_type=F32)
            return dq + jnp.dot(ds, k_ref[rows, :], preferred_element_type=F32)

        dq = lax.fori_loop(0, _n_key_blocks(i), step, jnp.zeros((Q_TILE, HEAD_DIM), F32))
        dq_ref[...] = dq * ATT_SCALE

    wide = HEADS * HEAD_DIM
    qt = pl.BlockSpec((Q_TILE, HEAD_DIM), lambda h, i: (i, h))
    full = pl.BlockSpec((SEQ, HEAD_DIM), lambda h, i: (0, h))
    return _host_call(
        body, 9, 3, fused, _att_first_last, name="attn_bwd", grid=(HEADS, N_Q_TILES),
        in_specs=[qt, full, full, qt, pl.BlockSpec((None, Q_TILE, 1), lambda h, i: (h, i, 0)),
                  pl.BlockSpec((Q_TILE, HEAD_DIM), lambda h, i: (i, h + HEADS)), SLOPE_SPEC, TABLE_SPEC, TABLE_SPEC],
        out_specs=[qt, full, full], out_shape=[jax.ShapeDtypeStruct((SEQ, wide), F32)] * 3,
        scratch_shapes=[], sem=("parallel", "arbitrary"),
        operands=[qn, kn, vb, o, lse, d_mix, _alibi_slopes(), *_att_tables()] + list(fused_arrays))


def _qk_bwd(proj, q_w, k_w, dqn, dkn, dv):
    def body(aq_ref, ak_ref, qw_ref, kw_ref, dqn_ref, dkn_ref, dv_ref, daq_ref, dak_ref, dav_ref, gq_ref, gk_ref):
        h = pl.program_id(0)

        @pl.when(h == 0)
        def _():
            gq_ref[...] = jnp.zeros_like(gq_ref)
            gk_ref[...] = jnp.zeros_like(gk_ref)

        def one(a_ref, w_ref, d_ref, da_ref, g_ref):
            a = a_ref[...]
            d = d_ref[...]
            rs = lax.rsqrt(jnp.mean(a * a, axis=-1, keepdims=True) + EPS)
            ah = a * rs
            g_ref[...] += jnp.sum(d * ah, axis=0, keepdims=True)
            dah = d * w_ref[...]
            da_ref[...] = (rs * (dah - ah * jnp.mean(dah * ah, axis=-1, keepdims=True))).astype(BF16)

        one(aq_ref, qw_ref, dqn_ref, daq_ref, gq_ref)
        one(ak_ref, kw_ref, dkn_ref, dak_ref, gk_ref)
        dav_ref[...] = dv_ref[...].astype(BF16)

    wide = HEADS * HEAD_DIM
    vec = pl.BlockSpec((1, HEAD_DIM), lambda h: (0, 0))
    return pl.pallas_call(
        body, name="qk_bwd", grid=(HEADS,),
        in_specs=[_head_col(ATT_OFF), _head_col(ATT_OFF + HEADS), vec, vec, _head_col(0), _head_col(0), _head_col(0)],
        out_specs=[_head_col(0)] * 3 + [vec, vec],
        out_shape=[jax.ShapeDtypeStruct((SEQ, wide), BF16)] * 3 + [jax.ShapeDtypeStruct((1, HEAD_DIM), F32)] * 2,
        compiler_params=_params(("arbitrary",)))(proj, proj, q_w, k_w, dqn, dkn, dv)


def _pair_sum(name, partial, theirs, core):
    _, r, c = theirs.shape
    tr = r // 2 if r % 16 == 0 else r

    def body(core_ref, a_ref, b_ref, o_ref):
        o_ref[...] = (a_ref[...].astype(F32) + b_ref[...].astype(F32)).astype(BF16)

    spec = pl.BlockSpec((None, tr, c), lambda q, i, core_ref: (q, i, 0))
    grid_spec = pltpu.PrefetchScalarGridSpec(
        num_scalar_prefetch=1, grid=(4, r // tr),
        in_specs=[pl.BlockSpec((None, tr, c), lambda q, i, core_ref: (2 * q + core_ref[0], i, 0)), spec],
        out_specs=spec)
    return pl.pallas_call(body, name=name, grid_spec=grid_spec, out_shape=jax.ShapeDtypeStruct(theirs.shape, BF16),
                          compiler_params=_params(("parallel", "parallel")))(core, partial, theirs)


def _adamw_step(w, m, v, g):
    nm = ADAM_B1 * m + (1.0 - ADAM_B1) * g
    nv = ADAM_B2 * v + (1.0 - ADAM_B2) * (g * g)
    m_hat = nm / (1.0 - ADAM_B1 ** ADAM_STEP)
    v_hat = nv / (1.0 - ADAM_B2 ** ADAM_STEP)
    return -ADAM_LR * (m_hat / (jnp.sqrt(v_hat) + ADAM_EPS) + ADAM_WD * w), nm, nv


def _adamw(name, w, m, v, addends, tr=None):
    r, c = w.shape
    tr = r if tr is None else tr
    n_add = len(addends)

    def body(*refs):
        w_ref, m_ref, v_ref = refs[:3]
        add_refs = refs[3:3 + n_add]
        g_ref, d_ref, nm_ref, nv_ref = refs[3 + n_add:]
        g = add_refs[0][...].astype(F32)
        for a_ref in add_refs[1:]:
            g = g + a_ref[...].astype(F32)
        g_ref[...] = g
        d_ref[...], nm_ref[...], nv_ref[...] = _adamw_step(w_ref[...], m_ref[...], v_ref[...], g)

    spec = pl.BlockSpec((tr, c), lambda i: (i, 0))
    out = jax.ShapeDtypeStruct((r, c), F32)
    return pl.pallas_call(body, name=name, grid=(r // tr,), in_specs=[spec] * (3 + n_add), out_specs=[spec] * 4,
                          out_shape=[out] * 4, compiler_params=_params(("parallel",)))(w, m, v, *addends)


def _adamw_reduced(name, w, m, v, chip_sums, received, chip, tr):
    r, c = w.shape

    def body(chip_ref, w_ref, m_ref, v_ref, own_ref, r0_ref, r1_ref, r2_ref, g_ref, d_ref, nm_ref, nv_ref):
        g = ((own_ref[...].astype(F32) + r0_ref[...].astype(F32)) + r1_ref[...].astype(F32)) + r2_ref[...].astype(F32)
        g_ref[...] = g
        d_ref[...], nm_ref[...], nv_ref[...] = _adamw_step(w_ref[...], m_ref[...], v_ref[...], g)

    spec = pl.BlockSpec((tr, c), lambda i, chip_ref: (i, 0))

    def slot(k):
        return pl.BlockSpec((None, tr, c), lambda i, chip_ref: (k, i, 0))

    grid_spec = pltpu.PrefetchScalarGridSpec(
        num_scalar_prefetch=1, grid=(r // tr,),
        in_specs=[spec, spec, spec, pl.BlockSpec((None, tr, c), lambda i, chip_ref: (chip_ref[0], i, 0)),
                  slot(0), slot(1), slot(2)],
        out_specs=[spec] * 4)
    out = jax.ShapeDtypeStruct((r, c), F32)
    return pl.pallas_call(body, name=name, grid_spec=grid_spec, out_shape=[out] * 4,
                          compiler_params=_params(("parallel",)))(chip, w, m, v, chip_sums, received, received, received)


def _sum_devices(gathered):
    _, r, c = gathered.shape

    def body(g_ref, o_ref):
        acc = g_ref[0]
        for d in range(1, N_DEV):
            acc = acc + g_ref[d]
        o_ref[...] = acc

    return pl.pallas_call(body, name="sum_devices", out_shape=jax.ShapeDtypeStruct((r, c), F32))(gathered)


def _pack_rows(vectors, rows):
    flat = jnp.concatenate([v.reshape(-1) for v in vectors])
    return jnp.pad(flat, (0, rows * 128 - flat.shape[0])).reshape(rows, 128)


def _unpack(flat, shapes):
    out, off = [], 0
    for shp in shapes:
        n = 1
        for d in shp:
            n *= d
        out.append(flat[off:off + n].reshape(shp))
        off += n
    return out


def _device_step(xs, tgt, mod, norm1_w, norm2_w, lb_logits, hg_norm_w, q_norm_w, k_norm_w, conv_w_full, conv_b,
                 win_g, w_out_x, w_up_x, w_down_x, core=None):
    fused = core is not None
    shift1, scale1, gate1, shift2, scale2, gate2 = (mod[k] for k in range(6))

    h, rstd1 = _norm_fwd("norm1_fwd", xs, norm1_w, scale1, shift1)
    if fused:
        proj, (wout_g,) = _mm_blocked_rhs("mm_in", h, win_g, fused=_FusedCopies("gather", [w_out_x]),
                                          fused_arrays=[w_out_x])
        (a_out, o_pre), (wup_g,) = _hgrn_fwd(proj, lb_logits, hg_norm_w, _FusedCopies("gather", [w_up_x]), [w_up_x])
        wup_g, wout_g = _forward_to_sibling("allgather_stage2_up_out", [wup_g, wout_g])
        wout_full = wout_g.reshape(D_MODEL, D_MODEL)
    else:
        proj = _mm_blocked_rhs("mm_in", h, win_g)
        (a_out, o_pre), _ = _hgrn_fwd(proj, lb_logits, hg_norm_w)
        wup_g, wout_full, wdown_full = w_up_x, w_out_x, w_down_x
    qn, kn, vb = _qk_prep(proj, q_norm_w, k_norm_w)
    (att_o, lse), _ = _attn_fwd(qn, kn, vb)
    mixin = jnp.concatenate([a_out, att_o.astype(BF16)], axis=1)
    mix = _mm_plain("mm_out", mixin, wout_full, NN, 512, 1024, F32)
    x1, h2, rstd2 = _norm_fwd("norm2_fwd", xs, norm2_w, scale2, shift2, resid=mix, gate=gate1)
    if fused:
        u, (wdown_g,) = _mm_blocked_rhs("mm_up", h2, wup_g, fused=_FusedCopies("gather", [w_down_x]),
                                        fused_arrays=[w_down_x])
        wdown_g, = _forward_to_sibling("allgather_stage2_down", [wdown_g])
        wdown_full = wdown_g.reshape(D_FF, D_MODEL)
    else:
        u = _mm_blocked_rhs("mm_up", h2, wup_g)
    y = _conv_gate_fwd(u, conv_w_full, conv_b)
    ffn = _mm_plain("mm_down", y, wdown_full, NN, 512, 512, F32)
    loss_v, dout, dffn, dgate2 = _loss_head(x1, ffn, gate2, tgt)

    dy = _mm_plain("mm_down_dx", dffn, wdown_full, NT, 512, UP_BLK, F32)
    gw_down = _mm_plain("mm_down_dw", y, dffn, TN, UP_BLK, 1024, BF16)
    da, dg, gconv_w, gconv_b = _conv_gate_bwd(u, dy, conv_w_full, conv_b)
    du = jnp.concatenate([da, dg], axis=1)
    dh2 = _mm_blocked_rhs_t("mm_up_dx", du, wup_g)
    gw_up = _mm_wgrad_blocked("mm_up_dw", h2, du)
    dx1, dmix, dshift2, dscale2, gnorm2, dgate1 = _norm_bwd(
        "norm2_bwd", dh2, x1, rstd2, norm2_w, scale2, dout, mix=mix, gate=gate1)
    dmixin = _mm_plain("mm_out_dx", dmix, wout_full, NT, 512, 1024, F32)
    gw_out = _mm_plain("mm_out_dw", mixin, dmix, TN, 512, 1024, BF16)
    if fused:
        partials = [gw_up, gw_out.reshape(N_DEV, OUT_BLK, D_MODEL), gw_down.reshape(N_DEV, FF_BLK, D_MODEL)]
        from_sibling = _exchange_sibling("grad_exchange_sibling_a", partials)
        cs_up, cs_out, cs_down = [_pair_sum(f"grad_pair_sum_{k}", a, b, core)
                                  for k, (a, b) in enumerate(zip(partials, from_sibling))]
        (dhq, dhf, dhi, dhg, glog, ghg), (fc_up, fc_out) = _hgrn_bwd(
            proj, lb_logits, hg_norm_w, o_pre, dmixin, _FusedCopies("chips", [cs_up, cs_out]), [cs_up, cs_out])
        (dqn, dkn, dvv), (fc_down,) = _attn_bwd(qn, kn, vb, att_o, lse, dmixin,
                                                _FusedCopies("chips", [cs_down]), [cs_down])
    else:
        (dhq, dhf, dhi, dhg, glog, ghg), _ = _hgrn_bwd(proj, lb_logits, hg_norm_w, o_pre, dmixin)
        (dqn, dkn, dvv), _ = _attn_bwd(qn, kn, vb, att_o, lse, dmixin)
    daq, dak, dav, gqw, gkw = _qk_bwd(proj, q_norm_w, k_norm_w, dqn, dkn, dvv)
    dproj = jnp.concatenate([dhq, dhf, dhi, dhg, daq, dak, dav], axis=1)
    gw_in = _mm_wgrad_blocked("mm_in_dw", h, dproj)
    if fused:
        from_sibling, = _exchange_sibling("grad_exchange_sibling_b", [gw_in])
        cs_in = _pair_sum("grad_pair_sum_in", gw_in, from_sibling, core)
        dh, (fc_in,) = _mm_blocked_rhs_t("mm_in_dx", dproj, win_g, fused=_FusedCopies("chips", [cs_in]),
                                         fused_arrays=[cs_in])
        large = [(cs_in, fc_in), (cs_out, fc_out), (cs_up, fc_up), (cs_down, fc_down)]
    else:
        dh = _mm_blocked_rhs_t("mm_in_dx", dproj, win_g)
        large = [gw_in, gw_out, gw_up, gw_down]
    grad_x, dshift1, dscale1, gnorm1 = _norm_bwd("norm1_bwd", dh, xs, rstd1, norm1_w, scale1, dx1)
    gmod = jnp.concatenate([dshift1, dscale1, dgate1, dshift2, dscale2, dgate2], axis=1)
    return (loss_v, grad_x, gmod, gnorm1, gnorm2, glog, ghg, gqw, gkw, gconv_b, gconv_w, *large)


def kernel(x, c, w_ada, b_ada, norm1_w, w_in, lb_logits, hg_norm_w, q_norm_w, k_norm_w, w_out, norm2_w, w_up, conv_w, conv_b, w_down, loss_target, m_w_ada, m_b_ada, m_norm1_w, m_w_in, m_lb_logits, m_hg_norm_w, m_q_norm_w, m_k_norm_w, m_w_out, m_norm2_w, m_w_up, m_conv_w, m_conv_b, m_w_down, v_w_ada, v_b_ada, v_norm1_w, v_w_in, v_lb_logits, v_hg_norm_w, v_q_norm_w, v_k_norm_w, v_w_out, v_norm2_w, v_w_up, v_conv_w, v_conv_b, v_w_down):
    ix, iy, ic = lax.axis_index("x"), lax.axis_index("y"), lax.axis_index("c")
    me = 4 * ix + 2 * iy + ic
    my_chip = 2 * ix + iy

    xs = x[0]
    tgt = loss_target[0]

    win_g, = _allgather_weights([w_in[0].astype(BF16)])

    c_all = _allgather_vmem(c.reshape(8, D_MODEL // 8), "allgather_c").reshape(N_DEV, D_MODEL)
    b_blk = lax.dynamic_slice_in_dim(b_ada, me * ADA_BLK, ADA_BLK, axis=1)
    mod_cols = _ada_fwd(c_all, w_ada[0], b_blk)
    mod_all = _allgather_vmem(mod_cols, "allgather_mod").reshape(N_DEV, N_DEV, ADA_BLK)
    mod = lax.dynamic_index_in_dim(mod_all, me, axis=1, keepdims=False).reshape(6, 1, D_MODEL)

    conv_w_all = _allgather_vmem(_pack_rows([conv_w[0]], 24), "allgather_conv_w").reshape(N_DEV, 24 * 128)
    conv_w_full = conv_w_all[:, :3 * FF_BLK].reshape(N_DEV, 3, FF_BLK).transpose(1, 0, 2).reshape(3, D_FF)

    (loss_v, grad_x, gmod, gnorm1, gnorm2, glog, ghg, gqw, gkw, gconv_b, gconv_w,
     rs_in, rs_out, rs_up, rs_down) = _device_step(
        xs, tgt, mod, norm1_w, norm2_w, lb_logits, hg_norm_w, q_norm_w, k_norm_w, conv_w_full, conv_b,
        win_g, w_out[0].astype(BF16), w_up[0].astype(BF16), w_down[0].astype(BF16),
        core=jnp.reshape(ic, (1,)).astype(jnp.int32))
    loss = lax.psum(loss_v[0, 0], AXES)

    small_shapes = [(1, 6 * D_MODEL), (1, D_MODEL), (1, D_MODEL), (2, HEADS * HEAD_DIM), (1, HEAD_DIM),
                    (1, HEAD_DIM), (1, HEAD_DIM), (1, D_FF), (3, D_FF)]
    small = [gmod, gnorm1, gnorm2, glog, ghg, gqw, gkw, gconv_b, gconv_w]
    n_small = sum(a.size for a in small)
    rows = -(-n_small // 1024) * 8
    gathered = _allgather_vmem(_pack_rows(small, rows), "allgather_small").reshape(N_DEV, rows, 128)
    summed = _sum_devices(gathered).reshape(-1)
    (g_b_ada, g_norm1, g_norm2, g_lb, g_hg, g_q, g_k, g_conv_b, g_conv_w_full) = _unpack(summed, small_shapes)
    g_conv_w = lax.dynamic_slice_in_dim(g_conv_w_full, me * FF_BLK, FF_BLK, axis=1)

    gmod_all = gathered[:, :6 * D_MODEL // 128, :].reshape(N_DEV, 6 * D_MODEL)
    gmod_cols = lax.dynamic_slice_in_dim(gmod_all, me * ADA_BLK, ADA_BLK, axis=1)
    g_w_ada_raw = _ada_wgrad(c_all, gmod_cols)

    chip = jnp.reshape(my_chip, (1,)).astype(jnp.int32)

    def big_update(name, w, m, v, rs, tr):
        chip_sums, received = rs
        return _adamw_reduced(name, w[0], m[0], v[0], chip_sums, received, chip, tr)

    r_in = big_update("adamw_w_in", w_in, m_w_in, v_w_in, rs_in, 256)
    r_out = big_update("adamw_w_out", w_out, m_w_out, v_w_out, rs_out, 128)
    r_up = big_update("adamw_w_up", w_up, m_w_up, v_w_up, rs_up, 256)
    r_down = big_update("adamw_w_down", w_down, m_w_down, v_w_down, rs_down, 176)
    r_ada = _adamw("adamw_w_ada", w_ada[0], m_w_ada[0], v_w_ada[0], [g_w_ada_raw], tr=256)
    r_convw = _adamw("adamw_conv_w", conv_w[0], m_conv_w[0], v_conv_w[0], [g_conv_w])

    rep_shapes = [(1, 6 * D_MODEL), (1, D_MODEL), (1, D_MODEL), (2, HEADS * HEAD_DIM), (1, HEAD_DIM),
                  (1, HEAD_DIM), (1, HEAD_DIM), (1, D_FF)]
    rep_rows = -(-sum(a * b for a, b in rep_shapes) // 1024) * 8
    pack = lambda arrs: _pack_rows(arrs, rep_rows)
    rep = _adamw("adamw_small",
                 pack([b_ada, norm1_w, norm2_w, lb_logits, hg_norm_w, q_norm_w, k_norm_w, conv_b]),
                 pack([m_b_ada, m_norm1_w, m_norm2_w, m_lb_logits, m_hg_norm_w, m_q_norm_w, m_k_norm_w, m_conv_b]),
                 pack([v_b_ada, v_norm1_w, v_norm2_w, v_lb_logits, v_hg_norm_w, v_q_norm_w, v_k_norm_w, v_conv_b]),
                 [pack([g_b_ada, g_norm1, g_norm2, g_lb, g_hg, g_q, g_k, g_conv_b])])
    rep = [_unpack(r.reshape(-1), rep_shapes) for r in rep]

    def big(r):
        return [a[None] for a in r]

    order = {"w_ada": big(r_ada), "b_ada": [r[0] for r in rep], "norm1_w": [r[1] for r in rep],
             "w_in": big(r_in), "lb_logits": [r[3] for r in rep], "hg_norm_w": [r[4] for r in rep],
             "q_norm_w": [r[5] for r in rep], "k_norm_w": [r[6] for r in rep], "w_out": big(r_out),
             "norm2_w": [r[2] for r in rep], "w_up": big(r_up), "conv_w": big(r_convw),
             "conv_b": [r[7] for r in rep], "w_down": big(r_down)}
    names = ["w_ada", "b_ada", "norm1_w", "w_in", "lb_logits", "hg_norm_w", "q_norm_w", "k_norm_w", "w_out",
             "norm2_w", "w_up", "conv_w", "conv_b", "w_down"]
    outs = [loss, grad_x[None]]
    for kind in range(4):
        outs += [order[n][kind] for n in names]
    return tuple(outs)
```

```python
import functools

import jax
import jax.numpy as jnp
from jax import lax
from jax.experimental import pallas as pl
from jax.experimental.pallas import tpu as pltpu

F32 = jnp.float32
BF16 = jnp.bfloat16

N_DEV = 8
SEQ = 2048
D_MODEL = 2048
HEADS = 8
HEAD_DIM = 128
IN_COLS = 7168
IN_BLK = IN_COLS // N_DEV
D_FF = 5632
UP_BLK = 2 * D_FF // N_DEV
FF_BLK = D_FF // N_DEV
ADA_BLK = 6 * D_MODEL // N_DEV
OUT_BLK = D_MODEL // N_DEV
EPS = 1e-6
CHUNK = 16
ROW_TILE = 256
V7X_VMEM_LIMIT = 56 * 1024 * 1024

ADAM_LR = 0.001
ADAM_B1 = 0.9
ADAM_B2 = 0.999
ADAM_EPS = 1e-08
ADAM_WD = 0.01
ADAM_STEP = 10

NN = (((1,), (0,)), ((), ()))
NT = (((1,), (1,)), ((), ()))
TN = (((0,), (0,)), ((), ()))
MESH = pl.DeviceIdType.MESH
AXES = ("x", "y", "c")


def _params(sem=None, vmem=V7X_VMEM_LIMIT):
    return pltpu.CompilerParams(dimension_semantics=sem, vmem_limit_bytes=vmem)


def _sigmoid(x):
    return 1.0 / (1.0 + jnp.exp(-x))


def _dsilu(x, s):
    return s * (1.0 + x * (1.0 - s))


def _lane_sum(x, ones_bf16):
    hi = x.astype(BF16)
    lo = (x - hi.astype(F32)).astype(BF16)
    return (jnp.dot(hi, ones_bf16, preferred_element_type=F32)
            + jnp.dot(lo, ones_bf16, preferred_element_type=F32))


def _mesh_pos():
    return lax.axis_index("x"), lax.axis_index("y"), lax.axis_index("c")


def _allgather_vmem(x_blk, name):
    m_per, n = x_blk.shape

    def body(x_ref, out_ref, send_sems, recv_sems, local_sem):
        x, y, c = _mesh_pos()
        me, sibling = (x, y, c), (x, y, 1 - c)
        chips = [(1 - x, y), (x, 1 - y), (1 - x, 1 - y)]

        def rows(px, py, pc):
            return out_ref.at[pl.ds((4 * px + 2 * py + pc) * m_per, m_per), :]

        def copy(k, block, to, src=None):
            return pltpu.make_async_remote_copy(
                src_ref=rows(*block) if src is None else src, dst_ref=rows(*block),
                send_sem=send_sems.at[k], recv_sem=recv_sems.at[k], device_id=to, device_id_type=MESH)

        mine = pltpu.make_async_copy(x_ref, rows(*me), local_sem)
        mine.start()
        first = [copy(0, me, sibling, src=x_ref)]
        first += [copy(1 + j, me, (*chip, c), src=x_ref) for j, chip in enumerate(chips)]
        for cp in first:
            cp.start()
        passed = [copy(4 + j, (*chip, c), sibling) for j, chip in enumerate(chips)]
        for j, chip in enumerate(chips):
            copy(1 + j, (*chip, c), me).wait_recv()
            passed[j].start()
        copy(0, sibling, me).wait_recv()
        for j, chip in enumerate(chips):
            copy(4 + j, (*chip, 1 - c), me).wait_recv()
        for cp in first + passed:
            cp.wait_send()
        mine.wait()

    return pl.pallas_call(
        body, name=name,
        out_shape=jax.ShapeDtypeStruct((N_DEV * m_per, n), x_blk.dtype),
        in_specs=[pl.BlockSpec(memory_space=pltpu.VMEM)],
        out_specs=pl.BlockSpec(memory_space=pltpu.VMEM),
        scratch_shapes=[pltpu.SemaphoreType.DMA((7,)), pltpu.SemaphoreType.DMA((7,)), pltpu.SemaphoreType.DMA],
    )(x_blk)


def _allgather_weights(blocks):
    n_arr = len(blocks)

    def body(*refs):
        ins, outs = refs[:n_arr], refs[n_arr:2 * n_arr]
        send_sems, recv_sems, local_sems = refs[2 * n_arr:]
        x, y, c = _mesh_pos()
        me, sibling = (x, y, c), (x, y, 1 - c)
        chips = [(1 - x, y), (x, 1 - y), (1 - x, 1 - y)]

        def slot(a, px, py, pc):
            return outs[a].at[4 * px + 2 * py + pc]

        def copy(a, k, block, to, src=None):
            return pltpu.make_async_remote_copy(
                src_ref=slot(a, *block) if src is None else src, dst_ref=slot(a, *block),
                send_sem=send_sems.at[a, k], recv_sem=recv_sems.at[a, k], device_id=to, device_id_type=MESH)

        mine, first, passed = [], [], []
        for a in range(n_arr):
            cp = pltpu.make_async_copy(ins[a], slot(a, *me), local_sems.at[a])
            cp.start()
            mine.append(cp)
            first.append(copy(a, 0, me, sibling, src=ins[a]))
            first += [copy(a, 1 + j, me, (*chip, c), src=ins[a]) for j, chip in enumerate(chips)]
        for cp in first:
            cp.start()
        for j, chip in enumerate(chips):
            for a in range(n_arr):
                copy(a, 1 + j, (*chip, c), me).wait_recv()
                cp = copy(a, 4 + j, (*chip, c), sibling)
                cp.start()
                passed.append(cp)
        for a in range(n_arr):
            copy(a, 0, sibling, me).wait_recv()
            for j, chip in enumerate(chips):
                copy(a, 4 + j, (*chip, 1 - c), me).wait_recv()
        for cp in first + passed:
            cp.wait_send()
        for cp in mine:
            cp.wait()

    hbm = pl.BlockSpec(memory_space=pltpu.HBM)
    return pl.pallas_call(
        body, name="allgather_weights",
        out_shape=[jax.ShapeDtypeStruct((N_DEV,) + b.shape, b.dtype) for b in blocks],
        in_specs=[hbm] * n_arr, out_specs=[hbm] * n_arr,
        scratch_shapes=[pltpu.SemaphoreType.DMA((n_arr, 7)), pltpu.SemaphoreType.DMA((n_arr, 7)),
                        pltpu.SemaphoreType.DMA((n_arr,))],
    )(*blocks)


HBM_SPEC = pl.BlockSpec(memory_space=pltpu.HBM)


class _FusedCopies:
    def __init__(self, kind, arrays, peers=(0, 1, 2, 3)):
        self.kind = kind
        self.peers = peers
        n = len(arrays) // 2 if kind == "gather_into" else len(arrays)
        self.n = n
        self.n_in = len(arrays)
        self.aliases = {}
        if kind == "gather":
            self.out_shape = [jax.ShapeDtypeStruct((N_DEV,) + a.shape, a.dtype) for a in arrays]
            self.scratch_shapes = [pltpu.SemaphoreType.DMA((n, 4)), pltpu.SemaphoreType.DMA((n, 4)),
                                   pltpu.SemaphoreType.DMA((n,))]
        elif kind == "gather_into":
            self.out_shape = [jax.ShapeDtypeStruct(a.shape, a.dtype) for a in arrays[n:]]
            self.scratch_shapes = [pltpu.SemaphoreType.DMA((n, 4)), pltpu.SemaphoreType.DMA((n, 4))]
            self.aliases = {n + a: a for a in range(n)}
        elif kind == "forward":
            self.out_shape = [jax.ShapeDtypeStruct(a.shape, a.dtype) for a in arrays]
            self.scratch_shapes = [pltpu.SemaphoreType.DMA((n, 3)), pltpu.SemaphoreType.DMA((n, 3))]
            self.aliases = {a: a for a in range(n)}
        elif kind == "sibling":
            self.out_shape = [jax.ShapeDtypeStruct((4,) + a.shape[1:], a.dtype) for a in arrays]
            self.scratch_shapes = [pltpu.SemaphoreType.DMA((n, 4)), pltpu.SemaphoreType.DMA((n, 4))]
        else:
            self.out_shape = [jax.ShapeDtypeStruct((3,) + a.shape[1:], a.dtype) for a in arrays]
            self.scratch_shapes = [pltpu.SemaphoreType.DMA((n, 3)), pltpu.SemaphoreType.DMA((n, 3))]
        self.in_specs = [HBM_SPEC] * self.n_in
        self.out_specs = [HBM_SPEC] * n
        self.n_scratch = len(self.scratch_shapes)

    def copies(self, ins, outs, sems):
        x, y, c = _mesh_pos()
        chips = [(1 - x, y), (x, 1 - y), (1 - x, 1 - y)]
        sibling = (x, y, 1 - c)
        starts, waits = [], []
        if self.kind in ("gather", "gather_into"):
            send_sems, recv_sems = sems[0], sems[1]
            me = (x, y, c)
            peers = [sibling] + [(px, py, c) for px, py in chips]

            def slot(a, pos):
                return outs[a].at[4 * pos[0] + 2 * pos[1] + pos[2]]

            def remote(a, k, lands_from):
                return pltpu.make_async_remote_copy(
                    src_ref=ins[a], dst_ref=slot(a, lands_from), send_sem=send_sems.at[a, k],
                    recv_sem=recv_sems.at[a, k], device_id=peers[k], device_id_type=MESH)

            for a in range(self.n):
                if self.kind == "gather":
                    local = pltpu.make_async_copy(ins[a], slot(a, me), sems[2].at[a])
                    starts.append(local)
                    waits.append(local)
                for k in self.peers:
                    starts.append(remote(a, k, me))
                    waits.append(remote(a, k, peers[k]))
        elif self.kind == "forward":
            send_sems, recv_sems = sems

            def passed_on(a, j, pc_src, pc_dst):
                px, py = chips[j]
                return pltpu.make_async_remote_copy(
                    src_ref=ins[a].at[4 * px + 2 * py + pc_src], dst_ref=outs[a].at[4 * px + 2 * py + pc_dst],
                    send_sem=send_sems.at[a, j], recv_sem=recv_sems.at[a, j], device_id=sibling, device_id_type=MESH)

            for a in range(self.n):
                for j in range(3):
                    starts.append(passed_on(a, j, c, c))
                    waits.append(passed_on(a, j, c, 1 - c))
        elif self.kind == "sibling":
            send_sems, recv_sems = sems
            for a in range(self.n):
                for q in range(4):
                    cp = pltpu.make_async_remote_copy(
                        src_ref=ins[a].at[2 * q + 1 - c], dst_ref=outs[a].at[q], send_sem=send_sems.at[a, q],
                        recv_sem=recv_sems.at[a, q], device_id=sibling, device_id_type=MESH)
                    starts.append(cp)
                    waits.append(cp)
        else:
            send_sems, recv_sems = sems
            for a in range(self.n):
                for j, (px, py) in enumerate(chips):
                    cp = pltpu.make_async_remote_copy(
                        src_ref=ins[a].at[2 * px + py], dst_ref=outs[a].at[j], send_sem=send_sems.at[a, j],
                        recv_sem=recv_sems.at[a, j], device_id=(px, py, c), device_id_type=MESH)
                    starts.append(cp)
                    waits.append(cp)
        return starts, waits


def _host_body(body, n_in, n_out, fused, first_last):
    if fused is None:
        return body
    n_fin, n_fout = fused.n_in, fused.n

    def wrapped(*refs):
        core_in, f_in = refs[:n_in], refs[n_in:n_in + n_fin]
        core_out = refs[n_in + n_fin:n_in + n_fin + n_out]
        f_out = refs[n_in + n_fin + n_out:n_in + n_fin + n_out + n_fout]
        rest = refs[n_in + n_fin + n_out + n_fout:]
        core_scratch, f_sems = rest[:len(rest) - fused.n_scratch], rest[len(rest) - fused.n_scratch:]
        starts, waits = fused.copies(f_in, f_out, f_sems)
        first, last = first_last()

        @pl.when(first)
        def _():
            for cp in starts:
                cp.start()

        body(*core_in, *core_out, *core_scratch)

        @pl.when(last)
        def _():
            for cp in waits:
                cp.wait()

    return wrapped


def _host_call(body, n_in, n_out, fused, first_last, *, name, grid, in_specs, out_specs, out_shape, scratch_shapes,
               sem, operands):
    aliases = {}
    if fused is not None:
        in_specs = list(in_specs) + fused.in_specs
        out_specs = list(out_specs) + fused.out_specs
        out_shape = list(out_shape) + fused.out_shape
        scratch_shapes = list(scratch_shapes) + fused.scratch_shapes
        sem = tuple("arbitrary" for _ in sem)
        aliases = {n_in + fi: n_out + fo for fi, fo in fused.aliases.items()}
    res = pl.pallas_call(_host_body(body, n_in, n_out, fused, first_last), name=name, grid=grid, in_specs=in_specs,
                         out_specs=out_specs, out_shape=out_shape, scratch_shapes=scratch_shapes,
                         input_output_aliases=aliases, compiler_params=_params(sem))(*operands)
    return list(res[:n_out]), list(res[n_out:])


def _forward_to_sibling(name, gathered):
    n_arr = len(gathered)

    def body(*refs):
        ins, outs = refs[:n_arr], refs[n_arr:2 * n_arr]
        send_sems, recv_sems = refs[2 * n_arr:]
        x, y, c = _mesh_pos()
        chips = [(1 - x, y), (x, 1 - y), (1 - x, 1 - y)]

        def copy(a, j, pc):
            px, py = chips[j]
            s = 4 * px + 2 * py + pc
            return pltpu.make_async_remote_copy(
                src_ref=ins[a].at[s], dst_ref=outs[a].at[s], send_sem=send_sems.at[a, j], recv_sem=recv_sems.at[a, j],
                device_id=(x, y, 1 - c), device_id_type=MESH)

        for a in range(n_arr):
            for j in range(3):
                copy(a, j, c).start()
        for a in range(n_arr):
            for j in range(3):
                copy(a, j, 1 - c).wait_recv()
                copy(a, j, c).wait_send()

    return pl.pallas_call(
        body, name=name,
        out_shape=[jax.ShapeDtypeStruct(g.shape, g.dtype) for g in gathered],
        in_specs=[HBM_SPEC] * n_arr, out_specs=[HBM_SPEC] * n_arr,
        input_output_aliases={a: a for a in range(n_arr)},
        scratch_shapes=[pltpu.SemaphoreType.DMA((n_arr, 3)), pltpu.SemaphoreType.DMA((n_arr, 3))],
    )(*gathered)


def _exchange_sibling(name, partials):
    n_arr = len(partials)

    def body(*refs):
        ins, outs = refs[:n_arr], refs[n_arr:2 * n_arr]
        send_sems, recv_sems = refs[2 * n_arr:]
        x, y, c = _mesh_pos()
        copies = [pltpu.make_async_remote_copy(
            src_ref=ins[a].at[2 * q + 1 - c], dst_ref=outs[a].at[q], send_sem=send_sems.at[a, q],
            recv_sem=recv_sems.at[a, q], device_id=(x, y, 1 - c), device_id_type=MESH)
            for a in range(n_arr) for q in range(4)]
        for cp in copies:
            cp.start()
        for cp in copies:
            cp.wait_recv()
        for cp in copies:
            cp.wait_send()

    return pl.pallas_call(
        body, name=name,
        out_shape=[jax.ShapeDtypeStruct((4,) + p.shape[1:], p.dtype) for p in partials],
        in_specs=[HBM_SPEC] * n_arr, out_specs=[HBM_SPEC] * n_arr,
        scratch_shapes=[pltpu.SemaphoreType.DMA((n_arr, 4)), pltpu.SemaphoreType.DMA((n_arr, 4))],
    )(*partials)


def _exchange_chips(name, chip_sums):
    n_arr = len(chip_sums)

    def body(*refs):
        ins, outs = refs[:n_arr], refs[n_arr:2 * n_arr]
        send_sems, recv_sems = refs[2 * n_arr:]
        x, y, c = _mesh_pos()
        chips = [(1 - x, y), (x, 1 - y), (1 - x, 1 - y)]
        copies = []
        for a in range(n_arr):
            for j, (px, py) in enumerate(chips):
                copies.append(pltpu.make_async_remote_copy(
                    src_ref=ins[a].at[2 * px + py], dst_ref=outs[a].at[j],
                    send_sem=send_sems.at[a, j], recv_sem=recv_sems.at[a, j],
                    device_id=(px, py, c), device_id_type=MESH))
        for cp in copies:
            cp.start()
        for cp in copies:
            cp.wait_recv()
        for cp in copies:
            cp.wait_send()

    hbm = pl.BlockSpec(memory_space=pltpu.HBM)
    return pl.pallas_call(
        body, name=name,
        out_shape=[jax.ShapeDtypeStruct((3,) + p.shape[1:], p.dtype) for p in chip_sums],
        in_specs=[hbm] * n_arr, out_specs=[hbm] * n_arr,
        scratch_shapes=[pltpu.SemaphoreType.DMA((n_arr, 3)), pltpu.SemaphoreType.DMA((n_arr, 3))],
    )(*chip_sums)


def _matmul(name, a, b, dims, grid, a_spec, b_spec, o_spec, out_shape, acc_axis=None, fused=None, fused_arrays=()):
    def body(a_ref, b_ref, o_ref):
        r = lax.dot_general(a_ref[...], b_ref[...], dims, preferred_element_type=F32)
        if acc_axis is None:
            o_ref[...] = r.astype(o_ref.dtype)
        else:
            k = pl.program_id(acc_axis)

            @pl.when(k == 0)
            def _():
                o_ref[...] = r

            @pl.when(k > 0)
            def _():
                o_ref[...] += r

    sem = tuple("arbitrary" if i == acc_axis else "parallel" for i in range(len(grid)))
    if fused is None:
        return pl.pallas_call(body, name=name, grid=grid, in_specs=[a_spec, b_spec], out_specs=o_spec,
                              out_shape=out_shape, compiler_params=_params(sem))(a, b)

    def first_last():
        first = last = None
        for ax, n in enumerate(grid):
            f, l = pl.program_id(ax) == 0, pl.program_id(ax) == n - 1
            first, last = (f, l) if first is None else (first & f, last & l)
        return first, last

    (out,), extra = _host_call(body, 2, 1, fused, first_last, name=name, grid=grid, in_specs=[a_spec, b_spec],
                               out_specs=[o_spec], out_shape=[out_shape], scratch_shapes=[], sem=sem,
                               operands=[a, b] + list(fused_arrays))
    return out, extra


def _mm_blocked_rhs(name, a, w_g, tm=512, fused=None, fused_arrays=()):
    m, k = a.shape
    nb = w_g.shape[2]
    return _matmul(name, a, w_g, NN, (N_DEV, m // tm),
                   pl.BlockSpec((tm, k), lambda j, i: (i, 0)),
                   pl.BlockSpec((None, k, nb), lambda j, i: (j, 0, 0)),
                   pl.BlockSpec((tm, nb), lambda j, i: (i, j)),
                   jax.ShapeDtypeStruct((m, N_DEV * nb), F32), fused=fused, fused_arrays=fused_arrays)


def _mm_blocked_rhs_t(name, a, w_g, tm=512, fused=None, fused_arrays=()):
    m = a.shape[0]
    n, nb = w_g.shape[1], w_g.shape[2]
    return _matmul(name, a, w_g, NT, (m // tm, N_DEV),
                   pl.BlockSpec((tm, nb), lambda i, j: (i, j)),
                   pl.BlockSpec((None, n, nb), lambda i, j: (j, 0, 0)),
                   pl.BlockSpec((tm, n), lambda i, j: (i, 0)),
                   jax.ShapeDtypeStruct((m, n), F32), acc_axis=1, fused=fused, fused_arrays=fused_arrays)


def _mm_wgrad_blocked(name, act, dcols, tk=512):
    t, k = act.shape
    nb = dcols.shape[1] // N_DEV
    return _matmul(name, act, dcols, TN, (N_DEV, k // tk),
                   pl.BlockSpec((t, tk), lambda j, i: (0, i)),
                   pl.BlockSpec((t, nb), lambda j, i: (0, j)),
                   pl.BlockSpec((None, tk, nb), lambda j, i: (j, i, 0)),
                   jax.ShapeDtypeStruct((N_DEV, k, nb), BF16))


def _mm_plain(name, a, b, dims, tm, tn, out_dtype, fused=None, fused_arrays=()):
    if dims == NN:
        (m, k), n = a.shape, b.shape[1]
        a_spec = pl.BlockSpec((tm, k), lambda i, j: (i, 0))
        b_spec = pl.BlockSpec((k, tn), lambda i, j: (0, j))
    elif dims == NT:
        (m, k), n = a.shape, b.shape[0]
        a_spec = pl.BlockSpec((tm, k), lambda i, j: (i, 0))
        b_spec = pl.BlockSpec((tn, k), lambda i, j: (j, 0))
    else:
        (k, m), n = a.shape, b.shape[1]
        a_spec = pl.BlockSpec((k, tm), lambda i, j: (0, i))
        b_spec = pl.BlockSpec((k, tn), lambda i, j: (0, j))
    return _matmul(name, a, b, dims, (m // tm, n // tn), a_spec, b_spec,
                   pl.BlockSpec((tm, tn), lambda i, j: (i, j)), jax.ShapeDtypeStruct((m, n), out_dtype),
                   fused=fused, fused_arrays=fused_arrays)


def _ada_fwd(c_all, w_ada_blk, b_blk):
    def body(c_ref, w_ref, b_ref, o_ref):
        cv = c_ref[...]
        o_ref[...] = jnp.dot(cv * _sigmoid(cv), w_ref[...], preferred_element_type=F32) + b_ref[...]

    tn = 512
    return pl.pallas_call(
        body, name="ada_fwd", grid=(ADA_BLK // tn,),
        in_specs=[pl.BlockSpec((N_DEV, D_MODEL), lambda j: (0, 0)),
                  pl.BlockSpec((D_MODEL, tn), lambda j: (0, j)),
                  pl.BlockSpec((1, tn), lambda j: (0, j))],
        out_specs=pl.BlockSpec((N_DEV, tn), lambda j: (0, j)),
        out_shape=jax.ShapeDtypeStruct((N_DEV, ADA_BLK), F32),
        compiler_params=_params(("parallel",)))(c_all, w_ada_blk, b_blk)


def _ada_wgrad(c_all, gmod_cols):
    def body(c_ref, g_ref, o_ref):
        cv = c_ref[...]
        o_ref[...] = lax.dot_general(cv * _sigmoid(cv), g_ref[...], TN, preferred_element_type=F32)

    tk = 512
    return pl.pallas_call(
        body, name="ada_wgrad", grid=(D_MODEL // tk,),
        in_specs=[pl.BlockSpec((N_DEV, tk), lambda i: (0, i)),
                  pl.BlockSpec((N_DEV, ADA_BLK), lambda i: (0, 0))],
        out_specs=pl.BlockSpec((tk, ADA_BLK), lambda i: (i, 0)),
        out_shape=jax.ShapeDtypeStruct((D_MODEL, ADA_BLK), F32),
        compiler_params=_params(("parallel",)))(c_all, gmod_cols)


def _row_spec(cols=D_MODEL):
    return pl.BlockSpec((ROW_TILE, cols), lambda i: (i, 0))


def _vec_spec(cols=D_MODEL):
    return pl.BlockSpec((1, cols), lambda i: (0, 0))


def _norm_fwd(name, x, w, scale, shift, resid=None, gate=None):
    has_res = resid is not None

    def body(*refs):
        if has_res:
            x_ref, r_ref, g_ref, w_ref, sc_ref, sh_ref, xr_ref, h_ref, rs_ref = refs
            xr = x_ref[...] + g_ref[...] * r_ref[...]
            xr_ref[...] = xr
        else:
            x_ref, w_ref, sc_ref, sh_ref, h_ref, rs_ref = refs
            xr = x_ref[...]
        rs = lax.rsqrt(jnp.mean(xr * xr, axis=-1, keepdims=True) + EPS)
        h = (xr * rs) * w_ref[...] * (1.0 + sc_ref[...]) + sh_ref[...]
        h_ref[...] = h.astype(BF16)
        rs_ref[...] = rs

    s = x.shape[0]
    ins = [x] + ([resid, gate] if has_res else []) + [w, scale, shift]
    in_specs = [_row_spec()] + ([_row_spec(), _vec_spec()] if has_res else []) + [_vec_spec()] * 3
    outs = ([jax.ShapeDtypeStruct((s, D_MODEL), F32)] if has_res else []) + [
        jax.ShapeDtypeStruct((s, D_MODEL), BF16), jax.ShapeDtypeStruct((s, 1), F32)]
    out_specs = ([_row_spec()] if has_res else []) + [_row_spec(), pl.BlockSpec((ROW_TILE, 1), lambda i: (i, 0))]
    return pl.pallas_call(body, name=name, grid=(s // ROW_TILE,), in_specs=in_specs, out_specs=out_specs,
                          out_shape=outs, compiler_params=_params(("parallel",)))(*ins)


def _norm_bwd(name, dh, x, rstd, w, scale, dres, mix=None, gate=None):
    has_mix = mix is not None

    def body(*refs):
        if has_mix:
            (dh_ref, x_ref, rs_ref, w_ref, sc_ref, dr_ref, mix_ref, g_ref,
             dx_ref, dmix_ref, dsh_ref, dsc_ref, dw_ref, dg_ref) = refs
        else:
            dh_ref, x_ref, rs_ref, w_ref, sc_ref, dr_ref, dx_ref, dsh_ref, dsc_ref, dw_ref = refs
        i = pl.program_id(0)
        dhv = dh_ref[...]
        rs = rs_ref[...]
        xn = x_ref[...] * rs
        wv = w_ref[...]
        one_sc = 1.0 + sc_ref[...]
        dxn = dhv * wv * one_sc
        dx = dr_ref[...] + rs * (dxn - xn * jnp.mean(dxn * xn, axis=-1, keepdims=True))
        dx_ref[...] = dx
        sums = [(dsh_ref, dhv), (dsc_ref, dhv * xn * wv), (dw_ref, dhv * one_sc * xn)]
        if has_mix:
            dmix_ref[...] = (dx * g_ref[...]).astype(BF16)
            sums.append((dg_ref, dx * mix_ref[...]))

        @pl.when(i == 0)
        def _():
            for ref, _v in sums:
                ref[...] = jnp.zeros_like(ref)

        for ref, v in sums:
            ref[...] += jnp.sum(v, axis=0, keepdims=True)

    s = x.shape[0]
    ins = [dh, x, rstd, w, scale, dres] + ([mix, gate] if has_mix else [])
    in_specs = ([_row_spec(), _row_spec(), pl.BlockSpec((ROW_TILE, 1), lambda i: (i, 0)), _vec_spec(), _vec_spec(),
                 _row_spec()] + ([_row_spec(), _vec_spec()] if has_mix else []))
    vec = jax.ShapeDtypeStruct((1, D_MODEL), F32)
    outs = ([jax.ShapeDtypeStruct((s, D_MODEL), F32)] + ([jax.ShapeDtypeStruct((s, D_MODEL), BF16)] if has_mix else [])
            + [vec] * (4 if has_mix else 3))
    out_specs = [_row_spec()] + ([_row_spec()] if has_mix else []) + [_vec_spec()] * (4 if has_mix else 3)
    return pl.pallas_call(body, name=name, grid=(s // ROW_TILE,), in_specs=in_specs, out_specs=out_specs,
                          out_shape=outs, compiler_params=_params(("arbitrary",)))(*ins)


def _loss_head(x1, ffn, gate2, target):
    def body(x_ref, f_ref, g_ref, t_ref, loss_ref, dout_ref, dffn_ref, dg_ref):
        i = pl.program_id(0)
        fv = f_ref[...]
        gv = g_ref[...]
        err = x_ref[...] + gv * fv - t_ref[...]
        dout = err * (1.0 / D_MODEL)
        dout_ref[...] = dout
        dffn_ref[...] = (dout * gv).astype(BF16)

        @pl.when(i == 0)
        def _():
            loss_ref[...] = jnp.zeros_like(loss_ref)
            dg_ref[...] = jnp.zeros_like(dg_ref)

        row = jnp.sum(err * err, axis=-1, keepdims=True) * (1.0 / D_MODEL)
        loss_ref[...] += jnp.broadcast_to(0.5 * jnp.sum(row, axis=0, keepdims=True), (1, 128))
        dg_ref[...] += jnp.sum(dout * fv, axis=0, keepdims=True)

    s = x1.shape[0]
    return pl.pallas_call(
        body, name="loss_head", grid=(s // ROW_TILE,),
        in_specs=[_row_spec(), _row_spec(), _vec_spec(), _row_spec()],
        out_specs=[pl.BlockSpec((1, 128), lambda i: (0, 0)), _row_spec(), _row_spec(), _vec_spec()],
        out_shape=[jax.ShapeDtypeStruct((1, 128), F32), jax.ShapeDtypeStruct((s, D_MODEL), F32),
                   jax.ShapeDtypeStruct((s, D_MODEL), BF16), jax.ShapeDtypeStruct((1, D_MODEL), F32)],
        compiler_params=_params(("arbitrary",)))(x1, ffn, gate2, target)


CONV_TILE = 512
N_CONV_TILES = D_FF // CONV_TILE


def _shift_rows(a, k, row):
    n = a.shape[0]
    if k > 0:
        return jnp.where(row >= k, pltpu.roll(a, k, 0), 0.0)
    return jnp.where(row < n + k, pltpu.roll(a, n + k, 0), 0.0)


def _conv_gate_fwd(u, conv_w, conv_b, fused=None, fused_arrays=()):
    s = u.shape[0]

    def body(a_ref, g_ref, w_ref, b_ref, y_ref):
        a = a_ref[...]
        w = w_ref[...]
        row = lax.broadcasted_iota(jnp.int32, a.shape, 0)
        ac = b_ref[...] + _shift_rows(a, 2, row) * w[0:1] + _shift_rows(a, 1, row) * w[1:2] + a * w[2:3]
        y_ref[...] = (ac * _sigmoid(ac) * g_ref[...]).astype(BF16)

    def first_last():
        i = pl.program_id(0)
        return i == 0, i == N_CONV_TILES - 1

    col = lambda off: pl.BlockSpec((s, CONV_TILE), lambda i: (0, i + off))
    (y,), extra = _host_call(
        body, 4, 1, fused, first_last, name="conv_gate_fwd", grid=(N_CONV_TILES,),
        in_specs=[col(0), col(N_CONV_TILES), pl.BlockSpec((3, CONV_TILE), lambda i: (0, i)),
                  pl.BlockSpec((1, CONV_TILE), lambda i: (0, i))],
        out_specs=[col(0)], out_shape=[jax.ShapeDtypeStruct((s, D_FF), BF16)], scratch_shapes=[], sem=("parallel",),
        operands=[u, u, conv_w, conv_b] + list(fused_arrays))
    return y if fused is None else (y, extra)


def _conv_gate_bwd(u, dy, conv_w, conv_b):
    s = u.shape[0]

    def body(a_ref, g_ref, dy_ref, w_ref, b_ref, da_ref, dg_ref, gw_ref, gb_ref):
        a = a_ref[...]
        w = w_ref[...]
        row = lax.broadcasted_iota(jnp.int32, a.shape, 0)
        a1 = _shift_rows(a, 1, row)
        a2 = _shift_rows(a, 2, row)
        ac = b_ref[...] + a2 * w[0:1] + a1 * w[1:2] + a * w[2:3]
        sg = _sigmoid(ac)
        dyv = dy_ref[...]
        dg_ref[...] = (dyv * (ac * sg)).astype(BF16)
        dac = dyv * g_ref[...] * _dsilu(ac, sg)
        gb_ref[...] = jnp.sum(dac, axis=0, keepdims=True)
        gw_ref[0:1, :] = jnp.sum(dac * a2, axis=0, keepdims=True)
        gw_ref[1:2, :] = jnp.sum(dac * a1, axis=0, keepdims=True)
        gw_ref[2:3, :] = jnp.sum(dac * a, axis=0, keepdims=True)
        da = dac * w[2:3] + _shift_rows(dac, -1, row) * w[1:2] + _shift_rows(dac, -2, row) * w[0:1]
        da_ref[...] = da.astype(BF16)

    col = lambda off: pl.BlockSpec((s, CONV_TILE), lambda i: (0, i + off))
    return pl.pallas_call(
        body, name="conv_gate_bwd", grid=(N_CONV_TILES,),
        in_specs=[col(0), col(N_CONV_TILES), col(0), pl.BlockSpec((3, CONV_TILE), lambda i: (0, i)),
                  pl.BlockSpec((1, CONV_TILE), lambda i: (0, i))],
        out_specs=[col(0), col(0), pl.BlockSpec((3, CONV_TILE), lambda i: (0, i)),
                   pl.BlockSpec((1, CONV_TILE), lambda i: (0, i))],
        out_shape=[jax.ShapeDtypeStruct((s, D_FF), BF16), jax.ShapeDtypeStruct((s, D_FF), BF16),
                   jax.ShapeDtypeStruct((3, D_FF), F32), jax.ShapeDtypeStruct((1, D_FF), F32)],
        compiler_params=_params(("parallel",)))(u, u, dy, conv_w, conv_b)


HG_TILE = 128
CHUNK_UNROLL = 8


def _unrolled_loop(n, body, init):
    def group(i, carry):
        for u in range(CHUNK_UNROLL):
            carry = body(i * CHUNK_UNROLL + u, carry)
        return carry

    return lax.fori_loop(0, n // CHUNK_UNROLL, group, init)


def _head_col(off):
    return pl.BlockSpec((SEQ, HEAD_DIM), lambda h: (0, h + off))


def _hgrn_gates(hq, hf, lb, pos):
    q = hq * _sigmoid(hq)
    sig = _sigmoid(hf)
    f = lb + (1.0 - lb) * sig
    gl = jnp.log(f)
    for sh in (1, 2, 4, 8):
        gl = gl + jnp.where(pos >= sh, pltpu.roll(gl, sh, 0), 0.0)
    return q, sig, f, 1.0 - f, gl


def _lower_bound(lbl):
    return 1.0 / (1.0 + jnp.exp(lbl[1:2, :] - lbl[0:1, :]))


def _head_first_last():
    h = pl.program_id(0)
    return h == 0, h == HEADS - 1


def _hgrn_fwd(proj, lb_logits, norm_w, fused=None, fused_arrays=()):
    n_tiles = SEQ // HG_TILE
    n_chunks = SEQ // CHUNK
    fused_arrays = list(fused_arrays)

    def body(hq_ref, hf_ref, hi_ref, hg_ref, lbl_ref, nw_ref, aout_ref, opre_ref, q_s, k_s, gl_s):
        lb = _lower_bound(lbl_ref[...])
        ones = jnp.ones((HEAD_DIM, HEAD_DIM), BF16)
        pos = lax.broadcasted_iota(jnp.int32, (HG_TILE, HEAD_DIM), 0) % CHUNK

        def tile(i, carry):
            rows = pl.ds(pl.multiple_of(i * HG_TILE, HG_TILE), HG_TILE)
            v = hi_ref[rows, :]
            q, _sig, _f, kk, gl = _hgrn_gates(hq_ref[rows, :], hf_ref[rows, :], lb, pos)
            o = _lane_sum(q * kk, ones) * v
            for d in range(1, CHUNK):
                e = jnp.where(pos >= d, jnp.exp(gl - pltpu.roll(gl, d, 0)), 0.0)
                o = o + _lane_sum(q * pltpu.roll(kk, d, 0) * e, ones) * pltpu.roll(v, d, 0)
            q_s[rows, :] = q
            k_s[rows, :] = kk
            gl_s[rows, :] = gl
            opre_ref[rows, :] = o
            return carry

        lax.fori_loop(0, n_tiles, tile, 0)

        def chunk(c, st):
            rows = pl.ds(pl.multiple_of(c * CHUNK, CHUNK), CHUNK)
            gl = gl_s[rows, :]
            qt = q_s[rows, :] * jnp.exp(gl)
            opre_ref[rows, :] += lax.dot_general(qt.astype(BF16), st.astype(BF16), NT, preferred_element_type=F32)
            gll = gl[CHUNK - 1:CHUNK, :]
            kt = k_s[rows, :] * jnp.exp(gll - gl)
            return st * jnp.exp(gll) + lax.dot_general(hi_ref[rows, :].astype(BF16), kt.astype(BF16), TN,
                                                       preferred_element_type=F32)

        _unrolled_loop(n_chunks, chunk, jnp.zeros((HEAD_DIM, HEAD_DIM), F32))

        def finish(i, carry):
            rows = pl.ds(pl.multiple_of(i * HG_TILE, HG_TILE), HG_TILE)
            o = opre_ref[rows, :]
            hg = hg_ref[rows, :]
            rs = lax.rsqrt(jnp.mean(o * o, axis=-1, keepdims=True) + EPS)
            aout_ref[rows, :] = ((o * rs) * nw_ref[...] * (hg * _sigmoid(hg))).astype(BF16)
            return carry

        lax.fori_loop(0, n_tiles, finish, 0)

    return _host_call(
        body, 6, 2, fused, _head_first_last, name="hgrn_fwd", grid=(HEADS,),
        in_specs=[_head_col(0), _head_col(HEADS), _head_col(2 * HEADS), _head_col(3 * HEADS),
                  pl.BlockSpec((2, HEAD_DIM), lambda h: (0, h)), pl.BlockSpec((1, HEAD_DIM), lambda h: (0, 0))],
        out_specs=[_head_col(0), _head_col(0)],
        out_shape=[jax.ShapeDtypeStruct((SEQ, HEADS * HEAD_DIM), BF16), jax.ShapeDtypeStruct((SEQ, HEADS * HEAD_DIM), F32)],
        scratch_shapes=[pltpu.VMEM((SEQ, HEAD_DIM), F32)] * 3, sem=("parallel",),
        operands=[proj, proj, proj, proj, lb_logits, norm_w] + fused_arrays)


def _hgrn_bwd(proj, lb_logits, norm_w, o_pre, d_aout, fused=None, fused_arrays=()):
    n_tiles = SEQ // HG_TILE
    n_chunks = SEQ // CHUNK

    def body(hq_ref, hf_ref, hi_ref, hg_ref, lbl_ref, nw_ref, opre_ref, da_ref,
             dhq_ref, dhf_ref, dhi_ref, dhg_ref, dlog_ref, gnw_ref,
             q_s, k_s, gl_s, do_s, dq_s, dk_s, dv_s, st_s):
        h = pl.program_id(0)
        lb = _lower_bound(lbl_ref[...])
        nw = nw_ref[...]
        ones = jnp.ones((HEAD_DIM, HEAD_DIM), BF16)
        pos = lax.broadcasted_iota(jnp.int32, (HG_TILE, HEAD_DIM), 0) % CHUNK

        @pl.when(h == 0)
        def _():
            gnw_ref[...] = jnp.zeros_like(gnw_ref)

        def tile(i, carry):
            rows = pl.ds(pl.multiple_of(i * HG_TILE, HG_TILE), HG_TILE)
            v = hi_ref[rows, :]
            q, _sig, _f, kk, gl = _hgrn_gates(hq_ref[rows, :], hf_ref[rows, :], lb, pos)
            o = opre_ref[rows, :]
            hg = hg_ref[rows, :]
            da = da_ref[rows, :]
            rs = lax.rsqrt(jnp.mean(o * o, axis=-1, keepdims=True) + EPS)
            oh = o * rs
            sg = _sigmoid(hg)
            dnorm = da * (hg * sg)
            dhg_ref[rows, :] = (da * (oh * nw) * _dsilu(hg, sg)).astype(BF16)
            gnw_ref[...] += jnp.sum(dnorm * oh, axis=0, keepdims=True)
            doh = dnorm * nw
            do = rs * (doh - oh * jnp.mean(doh * oh, axis=-1, keepdims=True))

            d_a = _lane_sum(do * v, ones)
            dq = d_a * kk
            dk = d_a * q
            dv = _lane_sum(q * kk, ones) * do
            for d in range(1, CHUNK):
                ks = pltpu.roll(kk, d, 0)
                e = jnp.where(pos >= d, jnp.exp(gl - pltpu.roll(gl, d, 0)), 0.0)
                a_d = _lane_sum(q * ks * e, ones)
                d_a = _lane_sum(do * pltpu.roll(v, d, 0), ones) * e
                dq = dq + d_a * ks
                dk = dk + pltpu.roll(d_a * q, HG_TILE - d, 0)
                dv = dv + pltpu.roll(a_d * do, HG_TILE - d, 0)
            q_s[rows, :] = q
            k_s[rows, :] = kk
            gl_s[rows, :] = gl
            do_s[rows, :] = do
            dq_s[rows, :] = dq
            dk_s[rows, :] = dk
            dv_s[rows, :] = dv
            return carry

        lax.fori_loop(0, n_tiles, tile, 0)

        def fwd_chunk(c, st):
            rows = pl.ds(pl.multiple_of(c * CHUNK, CHUNK), CHUNK)
            gl = gl_s[rows, :]
            st_s[c] = st
            dq_s[rows, :] += jnp.dot(do_s[rows, :].astype(BF16), st.astype(BF16),
                                     preferred_element_type=F32) * jnp.exp(gl)
            gll = gl[CHUNK - 1:CHUNK, :]
            kt = k_s[rows, :] * jnp.exp(gll - gl)
            return st * jnp.exp(gll) + lax.dot_general(hi_ref[rows, :].astype(BF16), kt.astype(BF16), TN,
                                                       preferred_element_type=F32)

        _unrolled_loop(n_chunks, fwd_chunk, jnp.zeros((HEAD_DIM, HEAD_DIM), F32))

        pos_c = lax.broadcasted_iota(jnp.int32, (CHUNK, HEAD_DIM), 0)

        def bwd_chunk(i, carry):
            rt, dlb = carry
            c = n_chunks - 1 - i
            rows = pl.ds(pl.multiple_of(c * CHUNK, CHUNK), CHUNK)
            gl = gl_s[rows, :]
            q = q_s[rows, :]
            kk = k_s[rows, :]
            do = do_s[rows, :]
            gll = gl[CHUNK - 1:CHUNK, :]
            egl = jnp.exp(gll)
            ekt = jnp.exp(gll - gl)
            rt_b = rt.astype(BF16)
            dk_in = dk_s[rows, :]
            dk_far = jnp.dot(hi_ref[rows, :].astype(BF16), rt_b, preferred_element_type=F32) * ekt
            dk = dk_in + dk_far
            dv = dv_s[rows, :] + lax.dot_general((kk * ekt).astype(BF16), rt_b, NT, preferred_element_type=F32)
            dq = dq_s[rows, :]
            rc = q * dq - kk * dk_in
            pc = kk * dk_far
            pre = pc
            for sh in (1, 2, 4, 8):
                rc = rc + jnp.where(pos_c < CHUNK - sh, pltpu.roll(rc, CHUNK - sh, 0), 0.0)
                pre = pre + jnp.where(pos_c >= sh, pltpu.roll(pre, sh, 0), 0.0)
            across = jnp.sum(st_s[c] * rt, axis=0, keepdims=True) * egl
            dgl = rc + (pre - pc) + across
            hf = hf_ref[rows, :]
            sig = _sigmoid(hf)
            f = lb + (1.0 - lb) * sig
            df = dgl / f - dk
            dhf_ref[rows, :] = (df * (1.0 - lb) * sig * (1.0 - sig)).astype(BF16)
            hq = hq_ref[rows, :]
            dhq_ref[rows, :] = (dq * _dsilu(hq, _sigmoid(hq))).astype(BF16)
            dhi_ref[rows, :] = dv.astype(BF16)
            rt_new = rt * egl + lax.dot_general(do.astype(BF16), (q * jnp.exp(gl)).astype(BF16), TN,
                                                preferred_element_type=F32)
            return (rt_new, dlb + jnp.sum(df * (1.0 - sig), axis=0, keepdims=True))

        _, dlb = _unrolled_loop(n_chunks, bwd_chunk,
                                (jnp.zeros((HEAD_DIM, HEAD_DIM), F32), jnp.zeros((1, HEAD_DIM), F32)))
        dl0 = lb * (1.0 - lb) * dlb
        dlog_ref[0:1, :] = dl0
        dlog_ref[1:2, :] = -dl0

    wide = HEADS * HEAD_DIM
    return _host_call(
        body, 8, 6, fused, _head_first_last, name="hgrn_bwd", grid=(HEADS,),
        in_specs=[_head_col(0), _head_col(HEADS), _head_col(2 * HEADS), _head_col(3 * HEADS),
                  pl.BlockSpec((2, HEAD_DIM), lambda h: (0, h)), pl.BlockSpec((1, HEAD_DIM), lambda h: (0, 0)),
                  _head_col(0), _head_col(0)],
        out_specs=[_head_col(0)] * 4 + [pl.BlockSpec((2, HEAD_DIM), lambda h: (0, h)),
                                        pl.BlockSpec((1, HEAD_DIM), lambda h: (0, 0))],
        out_shape=[jax.ShapeDtypeStruct((SEQ, wide), BF16)] * 4 + [jax.ShapeDtypeStruct((2, wide), F32),
                                                                    jax.ShapeDtypeStruct((1, HEAD_DIM), F32)],
        scratch_shapes=[pltpu.VMEM((SEQ, HEAD_DIM), F32)] * 7 + [pltpu.VMEM((n_chunks, HEAD_DIM, HEAD_DIM), F32)],
        sem=("arbitrary",),
        operands=[proj, proj, proj, proj, lb_logits, norm_w, o_pre, d_aout] + list(fused_arrays))


Q_TILE = 256
ATT_SCALE = HEAD_DIM ** -0.5
ATT_OFF = 4 * HEADS


def _qk_prep(proj, q_w, k_w):
    def body(aq_ref, ak_ref, av_ref, qw_ref, kw_ref, qn_ref, kn_ref, v_ref):
        aq = aq_ref[...]
        ak = ak_ref[...]
        qn_ref[...] = (aq * lax.rsqrt(jnp.mean(aq * aq, axis=-1, keepdims=True) + EPS) * qw_ref[...]).astype(BF16)
        kn_ref[...] = (ak * lax.rsqrt(jnp.mean(ak * ak, axis=-1, keepdims=True) + EPS) * kw_ref[...]).astype(BF16)
        v_ref[...] = av_ref[...].astype(BF16)

    wide = HEADS * HEAD_DIM
    vec = pl.BlockSpec((1, HEAD_DIM), lambda h: (0, 0))
    return pl.pallas_call(
        body, name="qk_prep", grid=(HEADS,),
        in_specs=[_head_col(ATT_OFF), _head_col(ATT_OFF + HEADS), _head_col(ATT_OFF + 2 * HEADS), vec, vec],
        out_specs=[_head_col(0)] * 3, out_shape=[jax.ShapeDtypeStruct((SEQ, wide), BF16)] * 3,
        compiler_params=_params(("parallel",)))(proj, proj, proj, q_w, k_w)


def _alibi_slopes():
    slopes = jnp.exp2(-8.0 * jnp.arange(1, HEADS + 1, dtype=F32) / HEADS)
    return jnp.broadcast_to(slopes[:, None, None], (HEADS, 1, HEAD_DIM))


SLOPE_SPEC = pl.BlockSpec((None, 1, HEAD_DIM), lambda h, i: (h, 0, 0))


N_Q_TILES = SEQ // Q_TILE
K_BLOCK = 512
NOT_ATTENDED = 1e35


def _att_tables():
    o = jnp.arange(N_Q_TILES, dtype=jnp.int32)[:, None, None]
    r = jnp.arange(Q_TILE, dtype=jnp.int32)[None, :, None]
    c = jnp.arange(K_BLOCK, dtype=jnp.int32)[None, None, :]
    dist = o * Q_TILE + r - c
    mult = ((dist <= 128).astype(F32) + (((dist % 4) == 0) & (dist <= 512)).astype(F32)
            + ((dist % 16) == 0).astype(F32))
    valid = (dist >= 0) & (mult > 0)
    return (jnp.where(valid, dist.astype(F32), NOT_ATTENDED),
            jnp.where(valid, jnp.log(jnp.maximum(mult, 1.0)), 0.0))


TABLE_SPEC = pl.BlockSpec((N_Q_TILES, Q_TILE, K_BLOCK), lambda h, i: (0, 0, 0))


def _att_block(q, k_ref, j, i, slope, dist_ref, lmul_ref):
    rows = pl.ds(pl.multiple_of(j * K_BLOCK, K_BLOCK), K_BLOCK)
    off = i - j * (K_BLOCK // Q_TILE)
    s = lax.dot_general(q, k_ref[rows, :], NT, preferred_element_type=F32) * ATT_SCALE
    return s - slope * dist_ref[off] + lmul_ref[off], rows


def _n_key_blocks(i):
    return (i + K_BLOCK // Q_TILE) // (K_BLOCK // Q_TILE)


def _att_first_last():
    h, i = pl.program_id(0), pl.program_id(1)
    return (h == 0) & (i == 0), (h == HEADS - 1) & (i == N_Q_TILES - 1)


def _attn_fwd(qn, kn, vb, fused=None, fused_arrays=()):
    def body(q_ref, k_ref, v_ref, sl_ref, dist_ref, lmul_ref, o_ref, lse_ref):
        i = pl.program_id(1)
        q = q_ref[...]
        slope = sl_ref[0:1, 0:1]

        def step(j, carry):
            m, l, acc = carry
            sb, rows = _att_block(q, k_ref, j, i, slope, dist_ref, lmul_ref)
            m_new = jnp.maximum(m, jnp.max(sb, axis=-1, keepdims=True))
            alpha = jnp.exp(m - m_new)
            p = jnp.exp(sb - m_new)
            l = alpha * l + jnp.sum(p, axis=-1, keepdims=True)
            acc = alpha * acc + jnp.dot(p.astype(BF16), v_ref[rows, :], preferred_element_type=F32)
            return m_new, l, acc

        m, l, acc = lax.fori_loop(0, _n_key_blocks(i), step,
                                  (jnp.full((Q_TILE, 1), -1e30, F32), jnp.zeros((Q_TILE, 1), F32),
                                   jnp.zeros((Q_TILE, HEAD_DIM), F32)))
        o_ref[...] = acc / l
        lse_ref[...] = m + jnp.log(l)

    wide = HEADS * HEAD_DIM
    qt = pl.BlockSpec((Q_TILE, HEAD_DIM), lambda h, i: (i, h))
    full = pl.BlockSpec((SEQ, HEAD_DIM), lambda h, i: (0, h))
    return _host_call(
        body, 6, 2, fused, _att_first_last, name="attn_fwd", grid=(HEADS, N_Q_TILES),
        in_specs=[qt, full, full, SLOPE_SPEC, TABLE_SPEC, TABLE_SPEC],
        out_specs=[qt, pl.BlockSpec((None, Q_TILE, 1), lambda h, i: (h, i, 0))],
        out_shape=[jax.ShapeDtypeStruct((SEQ, wide), F32), jax.ShapeDtypeStruct((HEADS, SEQ, 1), F32)],
        scratch_shapes=[], sem=("parallel", "parallel"),
        operands=[qn, kn, vb, _alibi_slopes(), *_att_tables()] + list(fused_arrays))


def _attn_bwd(qn, kn, vb, o, lse, d_mix, fused=None, fused_arrays=()):
    def body(q_ref, k_ref, v_ref, o_ref, lse_ref, do_ref, sl_ref, dist_ref, lmul_ref, dq_ref, dk_ref, dv_ref):
        i = pl.program_id(1)
        q = q_ref[...]
        do = do_ref[...]
        do_b = do.astype(BF16)
        slope = sl_ref[0:1, 0:1]
        lse = lse_ref[...]
        delta = jnp.sum(do * o_ref[...], axis=-1, keepdims=True)

        @pl.when(i == 0)
        def _():
            dk_ref[...] = jnp.zeros_like(dk_ref)
            dv_ref[...] = jnp.zeros_like(dv_ref)

        def step(j, dq):
            sb, rows = _att_block(q, k_ref, j, i, slope, dist_ref, lmul_ref)
            p = jnp.exp(sb - lse)
            dp = lax.dot_general(do_b, v_ref[rows, :], NT, preferred_element_type=F32)
            ds = (p * (dp - delta)).astype(BF16)
            dk_ref[rows, :] += lax.dot_general(ds, q, TN, preferred_element_type=F32) * ATT_SCALE
            dv_ref[rows, :] += lax.dot_general(p.astype(BF16), do_b, TN, preferred_element_type=F32)
            return dq + jnp.dot(ds, k_ref[rows, :], preferred_element_type=F32)

        dq = lax.fori_loop(0, _n_key_blocks(i), step, jnp.zeros((Q_TILE, HEAD_DIM), F32))
        dq_ref[...] = dq * ATT_SCALE

    wide = HEADS * HEAD_DIM
    qt = pl.BlockSpec((Q_TILE, HEAD_DIM), lambda h, i: (i, h))
    full = pl.BlockSpec((SEQ, HEAD_DIM), lambda h, i: (0, h))
    return _host_call(
        body, 9, 3, fused, _att_first_last, name="attn_bwd", grid=(HEADS, N_Q_TILES),
        in_specs=[qt, full, full, qt, pl.BlockSpec((None, Q_TILE, 1), lambda h, i: (h, i, 0)),
                  pl.BlockSpec((Q_TILE, HEAD_DIM), lambda h, i: (i, h + HEADS)), SLOPE_SPEC, TABLE_SPEC, TABLE_SPEC],
        out_specs=[qt, full, full], out_shape=[jax.ShapeDtypeStruct((SEQ, wide), F32)] * 3,
        scratch_shapes=[], sem=("parallel", "arbitrary"),
        operands=[qn, kn, vb, o, lse, d_mix, _alibi_slopes(), *_att_tables()] + list(fused_arrays))


def _qk_bwd(proj, q_w, k_w, dqn, dkn, dv):
    def body(aq_ref, ak_ref, qw_ref, kw_ref, dqn_ref, dkn_ref, dv_ref, daq_ref, dak_ref, dav_ref, gq_ref, gk_ref):
        h = pl.program_id(0)

        @pl.when(h == 0)
        def _():
            gq_ref[...] = jnp.zeros_like(gq_ref)
            gk_ref[...] = jnp.zeros_like(gk_ref)

        def one(a_ref, w_ref, d_ref, da_ref, g_ref):
            a = a_ref[...]
            d = d_ref[...]
            rs = lax.rsqrt(jnp.mean(a * a, axis=-1, keepdims=True) + EPS)
            ah = a * rs
            g_ref[...] += jnp.sum(d * ah, axis=0, keepdims=True)
            dah = d * w_ref[...]
            da_ref[...] = (rs * (dah - ah * jnp.mean(dah * ah, axis=-1, keepdims=True))).astype(BF16)

        one(aq_ref, qw_ref, dqn_ref, daq_ref, gq_ref)
        one(ak_ref, kw_ref, dkn_ref, dak_ref, gk_ref)
        dav_ref[...] = dv_ref[...].astype(BF16)

    wide = HEADS * HEAD_DIM
    vec = pl.BlockSpec((1, HEAD_DIM), lambda h: (0, 0))
    return pl.pallas_call(
        body, name="qk_bwd", grid=(HEADS,),
        in_specs=[_head_col(ATT_OFF), _head_col(ATT_OFF + HEADS), vec, vec, _head_col(0), _head_col(0), _head_col(0)],
        out_specs=[_head_col(0)] * 3 + [vec, vec],
        out_shape=[jax.ShapeDtypeStruct((SEQ, wide), BF16)] * 3 + [jax.ShapeDtypeStruct((1, HEAD_DIM), F32)] * 2,
        compiler_params=_params(("arbitrary",)))(proj, proj, q_w, k_w, dqn, dkn, dv)


def _pair_sum(name, partial, theirs, core):
    _, r, c = theirs.shape
    tr = r // 2 if r % 16 == 0 else r

    def body(core_ref, a_ref, b_ref, o_ref):
        o_ref[...] = (a_ref[...].astype(F32) + b_ref[...].astype(F32)).astype(BF16)

    spec = pl.BlockSpec((None, tr, c), lambda q, i, core_ref: (q, i, 0))
    grid_spec = pltpu.PrefetchScalarGridSpec(
        num_scalar_prefetch=1, grid=(4, r // tr),
        in_specs=[pl.BlockSpec((None, tr, c), lambda q, i, core_ref: (2 * q + core_ref[0], i, 0)), spec],
        out_specs=spec)
    return pl.pallas_call(body, name=name, grid_spec=grid_spec, out_shape=jax.ShapeDtypeStruct(theirs.shape, BF16),
                          compiler_params=_params(("parallel", "parallel")))(core, partial, theirs)


def _adamw_step(w, m, v, g):
    nm = ADAM_B1 * m + (1.0 - ADAM_B1) * g
    nv = ADAM_B2 * v + (1.0 - ADAM_B2) * (g * g)
    m_hat = nm / (1.0 - ADAM_B1 ** ADAM_STEP)
    v_hat = nv / (1.0 - ADAM_B2 ** ADAM_STEP)
    return -ADAM_LR * (m_hat / (jnp.sqrt(v_hat) + ADAM_EPS) + ADAM_WD * w), nm, nv


def _adamw(name, w, m, v, addends, tr=None):
    r, c = w.shape
    tr = r if tr is None else tr
    n_add = len(addends)

    def body(*refs):
        w_ref, m_ref, v_ref = refs[:3]
        add_refs = refs[3:3 + n_add]
        g_ref, d_ref, nm_ref, nv_ref = refs[3 + n_add:]
        g = add_refs[0][...].astype(F32)
        for a_ref in add_refs[1:]:
            g = g + a_ref[...].astype(F32)
        g_ref[...] = g
        d_ref[...], nm_ref[...], nv_ref[...] = _adamw_step(w_ref[...], m_ref[...], v_ref[...], g)

    spec = pl.BlockSpec((tr, c), lambda i: (i, 0))
    out = jax.ShapeDtypeStruct((r, c), F32)
    return pl.pallas_call(body, name=name, grid=(r // tr,), in_specs=[spec] * (3 + n_add), out_specs=[spec] * 4,
                          out_shape=[out] * 4, compiler_params=_params(("parallel",)))(w, m, v, *addends)


def _adamw_reduced(name, w, m, v, chip_sums, received, chip, tr):
    r, c = w.shape

    def body(chip_ref, w_ref, m_ref, v_ref, own_ref, r0_ref, r1_ref, r2_ref, g_ref, d_ref, nm_ref, nv_ref):
        g = ((own_ref[...].astype(F32) + r0_ref[...].astype(F32)) + r1_ref[...].astype(F32)) + r2_ref[...].astype(F32)
        g_ref[...] = g
        d_ref[...], nm_ref[...], nv_ref[...] = _adamw_step(w_ref[...], m_ref[...], v_ref[...], g)

    spec = pl.BlockSpec((tr, c), lambda i, chip_ref: (i, 0))

    def slot(k):
        return pl.BlockSpec((None, tr, c), lambda i, chip_ref: (k, i, 0))

    grid_spec = pltpu.PrefetchScalarGridSpec(
        num_scalar_prefetch=1, grid=(r // tr,),
        in_specs=[spec, spec, spec, pl.BlockSpec((None, tr, c), lambda i, chip_ref: (chip_ref[0], i, 0)),
                  slot(0), slot(1), slot(2)],
        out_specs=[spec] * 4)
    out = jax.ShapeDtypeStruct((r, c), F32)
    return pl.pallas_call(body, name=name, grid_spec=grid_spec, out_shape=[out] * 4,
                          compiler_params=_params(("parallel",)))(chip, w, m, v, chip_sums, received, received, received)


def _sum_devices(gathered):
    _, r, c = gathered.shape

    def body(g_ref, o_ref):
        acc = g_ref[0]
        for d in range(1, N_DEV):
            acc = acc + g_ref[d]
        o_ref[...] = acc

    return pl.pallas_call(body, name="sum_devices", out_shape=jax.ShapeDtypeStruct((r, c), F32))(gathered)


def _pack_rows(vectors, rows):
    flat = jnp.concatenate([v.reshape(-1) for v in vectors])
    return jnp.pad(flat, (0, rows * 128 - flat.shape[0])).reshape(rows, 128)


def _unpack(flat, shapes):
    out, off = [], 0
    for shp in shapes:
        n = 1
        for d in shp:
            n *= d
        out.append(flat[off:off + n].reshape(shp))
        off += n
    return out


def _device_step(xs, tgt, mod, norm1_w, norm2_w, lb_logits, hg_norm_w, q_norm_w, k_norm_w, conv_w_full, conv_b,
                 win_g, w_out_x, w_up_x, w_down_x, core=None):
    fused = core is not None
    shift1, scale1, gate1, shift2, scale2, gate2 = (mod[k] for k in range(6))

    h, rstd1 = _norm_fwd("norm1_fwd", xs, norm1_w, scale1, shift1)
    if fused:
        proj, (wout_g,) = _mm_blocked_rhs("mm_in", h, win_g, fused=_FusedCopies("gather", [w_out_x]),
                                          fused_arrays=[w_out_x])
        (a_out, o_pre), (wup_g,) = _hgrn_fwd(proj, lb_logits, hg_norm_w,
                                             _FusedCopies("gather", [w_up_x], peers=(0, 1, 2)), [w_up_x])
        wout_g, = _forward_to_sibling("allgather_stage2_out", [wout_g])
        wout_full = wout_g.reshape(D_MODEL, D_MODEL)
        qn, kn, vb = _qk_prep(proj, q_norm_w, k_norm_w)
        (att_o, lse), (wup_g,) = _attn_fwd(qn, kn, vb, _FusedCopies("gather_into", [w_up_x, wup_g], peers=(3,)),
                                           [w_up_x, wup_g])
    else:
        proj = _mm_blocked_rhs("mm_in", h, win_g)
        (a_out, o_pre), _ = _hgrn_fwd(proj, lb_logits, hg_norm_w)
        wup_g, wout_full, wdown_full = w_up_x, w_out_x, w_down_x
        qn, kn, vb = _qk_prep(proj, q_norm_w, k_norm_w)
        (att_o, lse), _ = _attn_fwd(qn, kn, vb)
    mixin = jnp.concatenate([a_out, att_o.astype(BF16)], axis=1)
    if fused:
        mix, (wup_g,) = _mm_plain("mm_out", mixin, wout_full, NN, 512, 1024, F32,
                                  fused=_FusedCopies("forward", [wup_g]), fused_arrays=[wup_g])
    else:
        mix = _mm_plain("mm_out", mixin, wout_full, NN, 512, 1024, F32)
    x1, h2, rstd2 = _norm_fwd("norm2_fwd", xs, norm2_w, scale2, shift2, resid=mix, gate=gate1)
    if fused:
        u, (wdown_g,) = _mm_blocked_rhs("mm_up", h2, wup_g, fused=_FusedCopies("gather", [w_down_x]),
                                        fused_arrays=[w_down_x])
        y, (wdown_g,) = _conv_gate_fwd(u, conv_w_full, conv_b, _FusedCopies("forward", [wdown_g]), [wdown_g])
        wdown_full = wdown_g.reshape(D_FF, D_MODEL)
    else:
        u = _mm_blocked_rhs("mm_up", h2, wup_g)
        y = _conv_gate_fwd(u, conv_w_full, conv_b)
    ffn = _mm_plain("mm_down", y, wdown_full, NN, 512, 512, F32)
    loss_v, dout, dffn, dgate2 = _loss_head(x1, ffn, gate2, tgt)

    dy = _mm_plain("mm_down_dx", dffn, wdown_full, NT, 512, UP_BLK, F32)
    gw_down = _mm_plain("mm_down_dw", y, dffn, TN, UP_BLK, 1024, BF16)
    da, dg, gconv_w, gconv_b = _conv_gate_bwd(u, dy, conv_w_full, conv_b)
    du = jnp.concatenate([da, dg], axis=1)
    dh2 = _mm_blocked_rhs_t("mm_up_dx", du, wup_g)
    gw_up = _mm_wgrad_blocked("mm_up_dw", h2, du)
    dx1, dmix, dshift2, dscale2, gnorm2, dgate1 = _norm_bwd(
        "norm2_bwd", dh2, x1, rstd2, norm2_w, scale2, dout, mix=mix, gate=gate1)
    gw_out = _mm_plain("mm_out_dw", mixin, dmix, TN, 512, 1024, BF16)
    if fused:
        partials = [gw_up, gw_out.reshape(N_DEV, OUT_BLK, D_MODEL), gw_down.reshape(N_DEV, FF_BLK, D_MODEL)]
        dmixin, from_sibling = _mm_plain("mm_out_dx", dmix, wout_full, NT, 512, 1024, F32,
                                         fused=_FusedCopies("sibling", partials), fused_arrays=partials)
        cs_up, cs_out, cs_down = [_pair_sum(f"grad_pair_sum_{k}", a, b, core)
                                  for k, (a, b) in enumerate(zip(partials, from_sibling))]
        (dhq, dhf, dhi, dhg, glog, ghg), (fc_up, fc_out) = _hgrn_bwd(
            proj, lb_logits, hg_norm_w, o_pre, dmixin, _FusedCopies("chips", [cs_up, cs_out]), [cs_up, cs_out])
        (dqn, dkn, dvv), (fc_down,) = _attn_bwd(qn, kn, vb, att_o, lse, dmixin,
                                                _FusedCopies("chips", [cs_down]), [cs_down])
    else:
        dmixin = _mm_plain("mm_out_dx", dmix, wout_full, NT, 512, 1024, F32)
        (dhq, dhf, dhi, dhg, glog, ghg), _ = _hgrn_bwd(proj, lb_logits, hg_norm_w, o_pre, dmixin)
        (dqn, dkn, dvv), _ = _attn_bwd(qn, kn, vb, att_o, lse, dmixin)
    daq, dak, dav, gqw, gkw = _qk_bwd(proj, q_norm_w, k_norm_w, dqn, dkn, dvv)
    dproj = jnp.concatenate([dhq, dhf, dhi, dhg, daq, dak, dav], axis=1)
    gw_in = _mm_wgrad_blocked("mm_in_dw", h, dproj)
    if fused:
        from_sibling, = _exchange_sibling("grad_exchange_sibling_b", [gw_in])
        cs_in = _pair_sum("grad_pair_sum_in", gw_in, from_sibling, core)
        dh, (fc_in,) = _mm_blocked_rhs_t("mm_in_dx", dproj, win_g, fused=_FusedCopies("chips", [cs_in]),
                                         fused_arrays=[cs_in])
        large = [(cs_in, fc_in), (cs_out, fc_out), (cs_up, fc_up), (cs_down, fc_down)]
    else:
        dh = _mm_blocked_rhs_t("mm_in_dx", dproj, win_g)
        large = [gw_in, gw_out, gw_up, gw_down]
    grad_x, dshift1, dscale1, gnorm1 = _norm_bwd("norm1_bwd", dh, xs, rstd1, norm1_w, scale1, dx1)
    gmod = jnp.concatenate([dshift1, dscale1, dgate1, dshift2, dscale2, dgate2], axis=1)
    return (loss_v, grad_x, gmod, gnorm1, gnorm2, glog, ghg, gqw, gkw, gconv_b, gconv_w, *large)


def kernel(x, c, w_ada, b_ada, norm1_w, w_in, lb_logits, hg_norm_w, q_norm_w, k_norm_w, w_out, norm2_w, w_up, conv_w, conv_b, w_down, loss_target, m_w_ada, m_b_ada, m_norm1_w, m_w_in, m_lb_logits, m_hg_norm_w, m_q_norm_w, m_k_norm_w, m_w_out, m_norm2_w, m_w_up, m_conv_w, m_conv_b, m_w_down, v_w_ada, v_b_ada, v_norm1_w, v_w_in, v_lb_logits, v_hg_norm_w, v_q_norm_w, v_k_norm_w, v_w_out, v_norm2_w, v_w_up, v_conv_w, v_conv_b, v_w_down):
    ix, iy, ic = lax.axis_index("x"), lax.axis_index("y"), lax.axis_index("c")
    me = 4 * ix + 2 * iy + ic
    my_chip = 2 * ix + iy

    xs = x[0]
    tgt = loss_target[0]

    win_g, = _allgather_weights([w_in[0].astype(BF16)])

    c_all = _allgather_vmem(c.reshape(8, D_MODEL // 8), "allgather_c").reshape(N_DEV, D_MODEL)
    b_blk = lax.dynamic_slice_in_dim(b_ada, me * ADA_BLK, ADA_BLK, axis=1)
    mod_cols = _ada_fwd(c_all, w_ada[0], b_blk)
    mod_all = _allgather_vmem(mod_cols, "allgather_mod").reshape(N_DEV, N_DEV, ADA_BLK)
    mod = lax.dynamic_index_in_dim(mod_all, me, axis=1, keepdims=False).reshape(6, 1, D_MODEL)

    conv_w_all = _allgather_vmem(_pack_rows([conv_w[0]], 24), "allgather_conv_w").reshape(N_DEV, 24 * 128)
    conv_w_full = conv_w_all[:, :3 * FF_BLK].reshape(N_DEV, 3, FF_BLK).transpose(1, 0, 2).reshape(3, D_FF)

    (loss_v, grad_x, gmod, gnorm1, gnorm2, glog, ghg, gqw, gkw, gconv_b, gconv_w,
     rs_in, rs_out, rs_up, rs_down) = _device_step(
        xs, tgt, mod, norm1_w, norm2_w, lb_logits, hg_norm_w, q_norm_w, k_norm_w, conv_w_full, conv_b,
        win_g, w_out[0].astype(BF16), w_up[0].astype(BF16), w_down[0].astype(BF16),
        core=jnp.reshape(ic, (1,)).astype(jnp.int32))
    loss = lax.psum(loss_v[0, 0], AXES)

    small_shapes = [(1, 6 * D_MODEL), (1, D_MODEL), (1, D_MODEL), (2, HEADS * HEAD_DIM), (1, HEAD_DIM),
                    (1, HEAD_DIM), (1, HEAD_DIM), (1, D_FF), (3, D_FF)]
    small = [gmod, gnorm1, gnorm2, glog, ghg, gqw, gkw, gconv_b, gconv_w]
    n_small = sum(a.size for a in small)
    rows = -(-n_small // 1024) * 8
    gathered = _allgather_vmem(_pack_rows(small, rows), "allgather_small").reshape(N_DEV, rows, 128)
    summed = _sum_devices(gathered).reshape(-1)
    (g_b_ada, g_norm1, g_norm2, g_lb, g_hg, g_q, g_k, g_conv_b, g_conv_w_full) = _unpack(summed, small_shapes)
    g_conv_w = lax.dynamic_slice_in_dim(g_conv_w_full, me * FF_BLK, FF_BLK, axis=1)

    gmod_all = gathered[:, :6 * D_MODEL // 128, :].reshape(N_DEV, 6 * D_MODEL)
    gmod_cols = lax.dynamic_slice_in_dim(gmod_all, me * ADA_BLK, ADA_BLK, axis=1)
    g_w_ada_raw = _ada_wgrad(c_all, gmod_cols)

    chip = jnp.reshape(my_chip, (1,)).astype(jnp.int32)

    def big_update(name, w, m, v, rs, tr):
        chip_sums, received = rs
        return _adamw_reduced(name, w[0], m[0], v[0], chip_sums, received, chip, tr)

    r_in = big_update("adamw_w_in", w_in, m_w_in, v_w_in, rs_in, 256)
    r_out = big_update("adamw_w_out", w_out, m_w_out, v_w_out, rs_out, 128)
    r_up = big_update("adamw_w_up", w_up, m_w_up, v_w_up, rs_up, 256)
    r_down = big_update("adamw_w_down", w_down, m_w_down, v_w_down, rs_down, 176)
    r_ada = _adamw("adamw_w_ada", w_ada[0], m_w_ada[0], v_w_ada[0], [g_w_ada_raw], tr=256)
    r_convw = _adamw("adamw_conv_w", conv_w[0], m_conv_w[0], v_conv_w[0], [g_conv_w])

    rep_shapes = [(1, 6 * D_MODEL), (1, D_MODEL), (1, D_MODEL), (2, HEADS * HEAD_DIM), (1, HEAD_DIM),
                  (1, HEAD_DIM), (1, HEAD_DIM), (1, D_FF)]
    rep_rows = -(-sum(a * b for a, b in rep_shapes) // 1024) * 8
    pack = lambda arrs: _pack_rows(arrs, rep_rows)
    rep = _adamw("adamw_small",
                 pack([b_ada, norm1_w, norm2_w, lb_logits, hg_norm_w, q_norm_w, k_norm_w, conv_b]),
                 pack([m_b_ada, m_norm1_w, m_norm2_w, m_lb_logits, m_hg_norm_w, m_q_norm_w, m_k_norm_w, m_conv_b]),
                 pack([v_b_ada, v_norm1_w, v_norm2_w, v_lb_logits, v_hg_norm_w, v_q_norm_w, v_k_norm_w, v_conv_b]),
                 [pack([g_b_ada, g_norm1, g_norm2, g_lb, g_hg, g_q, g_k, g_conv_b])])
    rep = [_unpack(r.reshape(-1), rep_shapes) for r in rep]

    def big(r):
        return [a[None] for a in r]

    order = {"w_ada": big(r_ada), "b_ada": [r[0] for r in rep], "norm1_w": [r[1] for r in rep],
             "w_in": big(r_in), "lb_logits": [r[3] for r in rep], "hg_norm_w": [r[4] for r in rep],
             "q_norm_w": [r[5] for r in rep], "k_norm_w": [r[6] for r in rep], "w_out": big(r_out),
             "norm2_w": [r[2] for r in rep], "w_up": big(r_up), "conv_w": big(r_convw),
             "conv_b": [r[7] for r in rep], "w_down": big(r_down)}
    names = ["w_ada", "b_ada", "norm1_w", "w_in", "lb_logits", "hg_norm_w", "q_norm_w", "k_norm_w", "w_out",
             "norm2_w", "w_up", "conv_w", "conv_b", "w_down"]
    outs = [loss, grad_x[None]]
    for kind in range(4):
        outs += [order[n][kind] for n in names]
    return tuple(outs)
```

```python
import functools

import jax
import jax.numpy as jnp
from jax import lax
from jax.experimental import pallas as pl
from jax.experimental.pallas import tpu as pltpu

F32 = jnp.float32
BF16 = jnp.bfloat16

N_DEV = 8
SEQ = 2048
D_MODEL = 2048
HEADS = 8
HEAD_DIM = 128
IN_COLS = 7168
IN_BLK = IN_COLS // N_DEV
D_FF = 5632
UP_BLK = 2 * D_FF // N_DEV
FF_BLK = D_FF // N_DEV
ADA_BLK = 6 * D_MODEL // N_DEV
OUT_BLK = D_MODEL // N_DEV
EPS = 1e-6
CHUNK = 16
ROW_TILE = 256
V7X_VMEM_LIMIT = 56 * 1024 * 1024

ADAM_LR = 0.001
ADAM_B1 = 0.9
ADAM_B2 = 0.999
ADAM_EPS = 1e-08
ADAM_WD = 0.01
ADAM_STEP = 10

NN = (((1,), (0,)), ((), ()))
NT = (((1,), (1,)), ((), ()))
TN = (((0,), (0,)), ((), ()))
MESH = pl.DeviceIdType.MESH
AXES = ("x", "y", "c")


def _params(sem=None, vmem=V7X_VMEM_LIMIT):
    return pltpu.CompilerParams(dimension_semantics=sem, vmem_limit_bytes=vmem)


def _sigmoid(x):
    return 1.0 / (1.0 + jnp.exp(-x))


def _dsilu(x, s):
    return s * (1.0 + x * (1.0 - s))


def _lane_sum(x, ones_bf16):
    hi = x.astype(BF16)
    lo = (x - hi.astype(F32)).astype(BF16)
    return (jnp.dot(hi, ones_bf16, preferred_element_type=F32)
            + jnp.dot(lo, ones_bf16, preferred_element_type=F32))


def _mesh_pos():
    return lax.axis_index("x"), lax.axis_index("y"), lax.axis_index("c")


def _allgather_vmem(x_blk, name):
    m_per, n = x_blk.shape

    def body(x_ref, out_ref, send_sems, recv_sems, local_sem):
        x, y, c = _mesh_pos()
        me, sibling = (x, y, c), (x, y, 1 - c)
        chips = [(1 - x, y), (x, 1 - y), (1 - x, 1 - y)]

        def rows(px, py, pc):
            return out_ref.at[pl.ds((4 * px + 2 * py + pc) * m_per, m_per), :]

        def copy(k, block, to, src=None):
            return pltpu.make_async_remote_copy(
                src_ref=rows(*block) if src is None else src, dst_ref=rows(*block),
                send_sem=send_sems.at[k], recv_sem=recv_sems.at[k], device_id=to, device_id_type=MESH)

        mine = pltpu.make_async_copy(x_ref, rows(*me), local_sem)
        mine.start()
        first = [copy(0, me, sibling, src=x_ref)]
        first += [copy(1 + j, me, (*chip, c), src=x_ref) for j, chip in enumerate(chips)]
        for cp in first:
            cp.start()
        passed = [copy(4 + j, (*chip, c), sibling) for j, chip in enumerate(chips)]
        for j, chip in enumerate(chips):
            copy(1 + j, (*chip, c), me).wait_recv()
            passed[j].start()
        copy(0, sibling, me).wait_recv()
        for j, chip in enumerate(chips):
            copy(4 + j, (*chip, 1 - c), me).wait_recv()
        for cp in first + passed:
            cp.wait_send()
        mine.wait()

    return pl.pallas_call(
        body, name=name,
        out_shape=jax.ShapeDtypeStruct((N_DEV * m_per, n), x_blk.dtype),
        in_specs=[pl.BlockSpec(memory_space=pltpu.VMEM)],
        out_specs=pl.BlockSpec(memory_space=pltpu.VMEM),
        scratch_shapes=[pltpu.SemaphoreType.DMA((7,)), pltpu.SemaphoreType.DMA((7,)), pltpu.SemaphoreType.DMA],
    )(x_blk)


def _allgather_weights(blocks):
    n_arr = len(blocks)

    def body(*refs):
        ins, outs = refs[:n_arr], refs[n_arr:2 * n_arr]
        send_sems, recv_sems, local_sems = refs[2 * n_arr:]
        x, y, c = _mesh_pos()
        me, sibling = (x, y, c), (x, y, 1 - c)
        chips = [(1 - x, y), (x, 1 - y), (1 - x, 1 - y)]

        def slot(a, px, py, pc):
            return outs[a].at[4 * px + 2 * py + pc]

        def copy(a, k, block, to, src=None):
            return pltpu.make_async_remote_copy(
                src_ref=slot(a, *block) if src is None else src, dst_ref=slot(a, *block),
                send_sem=send_sems.at[a, k], recv_sem=recv_sems.at[a, k], device_id=to, device_id_type=MESH)

        mine, first, passed = [], [], []
        for a in range(n_arr):
            cp = pltpu.make_async_copy(ins[a], slot(a, *me), local_sems.at[a])
            cp.start()
            mine.append(cp)
            first.append(copy(a, 0, me, sibling, src=ins[a]))
            first += [copy(a, 1 + j, me, (*chip, c), src=ins[a]) for j, chip in enumerate(chips)]
        for cp in first:
            cp.start()
        for j, chip in enumerate(chips):
            for a in range(n_arr):
                copy(a, 1 + j, (*chip, c), me).wait_recv()
                cp = copy(a, 4 + j, (*chip, c), sibling)
                cp.start()
                passed.append(cp)
        for a in range(n_arr):
            copy(a, 0, sibling, me).wait_recv()
            for j, chip in enumerate(chips):
                copy(a, 4 + j, (*chip, 1 - c), me).wait_recv()
        for cp in first + passed:
            cp.wait_send()
        for cp in mine:
            cp.wait()

    hbm = pl.BlockSpec(memory_space=pltpu.HBM)
    return pl.pallas_call(
        body, name="allgather_weights",
        out_shape=[jax.ShapeDtypeStruct((N_DEV,) + b.shape, b.dtype) for b in blocks],
        in_specs=[hbm] * n_arr, out_specs=[hbm] * n_arr,
        scratch_shapes=[pltpu.SemaphoreType.DMA((n_arr, 7)), pltpu.SemaphoreType.DMA((n_arr, 7)),
                        pltpu.SemaphoreType.DMA((n_arr,))],
    )(*blocks)


HBM_SPEC = pl.BlockSpec(memory_space=pltpu.HBM)


class _FusedCopies:
    def __init__(self, kind, arrays, peers=(0, 1, 2, 3)):
        self.kind = kind
        self.peers = peers
        n = len(arrays) // 2 if kind == "gather_into" else len(arrays)
        self.n = n
        self.n_in = len(arrays)
        self.aliases = {}
        if kind == "gather":
            self.out_shape = [jax.ShapeDtypeStruct((N_DEV,) + a.shape, a.dtype) for a in arrays]
            self.scratch_shapes = [pltpu.SemaphoreType.DMA((n, 4)), pltpu.SemaphoreType.DMA((n, 4)),
                                   pltpu.SemaphoreType.DMA((n,))]
        elif kind == "gather_into":
            self.out_shape = [jax.ShapeDtypeStruct(a.shape, a.dtype) for a in arrays[n:]]
            self.scratch_shapes = [pltpu.SemaphoreType.DMA((n, 4)), pltpu.SemaphoreType.DMA((n, 4))]
            self.aliases = {n + a: a for a in range(n)}
        elif kind == "forward":
            self.out_shape = [jax.ShapeDtypeStruct(a.shape, a.dtype) for a in arrays]
            self.scratch_shapes = [pltpu.SemaphoreType.DMA((n, 3)), pltpu.SemaphoreType.DMA((n, 3))]
            self.aliases = {a: a for a in range(n)}
        elif kind == "sibling":
            self.out_shape = [jax.ShapeDtypeStruct((4,) + a.shape[1:], a.dtype) for a in arrays]
            self.scratch_shapes = [pltpu.SemaphoreType.DMA((n, 4)), pltpu.SemaphoreType.DMA((n, 4))]
        else:
            self.out_shape = [jax.ShapeDtypeStruct((3,) + a.shape[1:], a.dtype) for a in arrays]
            self.scratch_shapes = [pltpu.SemaphoreType.DMA((n, 3)), pltpu.SemaphoreType.DMA((n, 3))]
        self.in_specs = [HBM_SPEC] * self.n_in
        self.out_specs = [HBM_SPEC] * n
        self.n_scratch = len(self.scratch_shapes)

    def copies(self, ins, outs, sems):
        x, y, c = _mesh_pos()
        chips = [(1 - x, y), (x, 1 - y), (1 - x, 1 - y)]
        sibling = (x, y, 1 - c)
        starts, waits = [], []
        if self.kind in ("gather", "gather_into"):
            send_sems, recv_sems = sems[0], sems[1]
            me = (x, y, c)
            peers = [sibling] + [(px, py, c) for px, py in chips]

            def slot(a, pos):
                return outs[a].at[4 * pos[0] + 2 * pos[1] + pos[2]]

            def remote(a, k, lands_from):
                return pltpu.make_async_remote_copy(
                    src_ref=ins[a], dst_ref=slot(a, lands_from), send_sem=send_sems.at[a, k],
                    recv_sem=recv_sems.at[a, k], device_id=peers[k], device_id_type=MESH)

            for a in range(self.n):
                if self.kind == "gather":
                    local = pltpu.make_async_copy(ins[a], slot(a, me), sems[2].at[a])
                    starts.append(local)
                    waits.append(local)
                for k in self.peers:
                    starts.append(remote(a, k, me))
                    waits.append(remote(a, k, peers[k]))
        elif self.kind == "forward":
            send_sems, recv_sems = sems

            def passed_on(a, j, pc_src, pc_dst):
                px, py = chips[j]
                return pltpu.make_async_remote_copy(
                    src_ref=ins[a].at[4 * px + 2 * py + pc_src], dst_ref=outs[a].at[4 * px + 2 * py + pc_dst],
                    send_sem=send_sems.at[a, j], recv_sem=recv_sems.at[a, j], device_id=sibling, device_id_type=MESH)

            for a in range(self.n):
                for j in range(3):
                    starts.append(passed_on(a, j, c, c))
                    waits.append(passed_on(a, j, c, 1 - c))
        elif self.kind == "sibling":
            send_sems, recv_sems = sems
            for a in range(self.n):
                for q in range(4):
                    cp = pltpu.make_async_remote_copy(
                        src_ref=ins[a].at[2 * q + 1 - c], dst_ref=outs[a].at[q], send_sem=send_sems.at[a, q],
                        recv_sem=recv_sems.at[a, q], device_id=sibling, device_id_type=MESH)
                    starts.append(cp)
                    waits.append(cp)
        else:
            send_sems, recv_sems = sems
            for a in range(self.n):
                for j, (px, py) in enumerate(chips):
                    cp = pltpu.make_async_remote_copy(
                        src_ref=ins[a].at[2 * px + py], dst_ref=outs[a].at[j], send_sem=send_sems.at[a, j],
                        recv_sem=recv_sems.at[a, j], device_id=(px, py, c), device_id_type=MESH)
                    starts.append(cp)
                    waits.append(cp)
        return starts, waits


def _host_body(body, n_in, n_out, fused, first_last):
    if fused is None:
        return body
    n_fin, n_fout = fused.n_in, fused.n

    def wrapped(*refs):
        core_in, f_in = refs[:n_in], refs[n_in:n_in + n_fin]
        core_out = refs[n_in + n_fin:n_in + n_fin + n_out]
        f_out = refs[n_in + n_fin + n_out:n_in + n_fin + n_out + n_fout]
        rest = refs[n_in + n_fin + n_out + n_fout:]
        core_scratch, f_sems = rest[:len(rest) - fused.n_scratch], rest[len(rest) - fused.n_scratch:]
        starts, waits = fused.copies(f_in, f_out, f_sems)
        first, last = first_last()

        @pl.when(first)
        def _():
            for cp in starts:
                cp.start()

        body(*core_in, *core_out, *core_scratch)

        @pl.when(last)
        def _():
            for cp in waits:
                cp.wait()

    return wrapped


def _host_call(body, n_in, n_out, fused, first_last, *, name, grid, in_specs, out_specs, out_shape, scratch_shapes,
               sem, operands):
    aliases = {}
    if fused is not None:
        in_specs = list(in_specs) + fused.in_specs
        out_specs = list(out_specs) + fused.out_specs
        out_shape = list(out_shape) + fused.out_shape
        scratch_shapes = list(scratch_shapes) + fused.scratch_shapes
        sem = tuple("arbitrary" for _ in sem)
        aliases = {n_in + fi: n_out + fo for fi, fo in fused.aliases.items()}
    res = pl.pallas_call(_host_body(body, n_in, n_out, fused, first_last), name=name, grid=grid, in_specs=in_specs,
                         out_specs=out_specs, out_shape=out_shape, scratch_shapes=scratch_shapes,
                         input_output_aliases=aliases, compiler_params=_params(sem))(*operands)
    return list(res[:n_out]), list(res[n_out:])


def _forward_to_sibling(name, gathered):
    n_arr = len(gathered)

    def body(*refs):
        ins, outs = refs[:n_arr], refs[n_arr:2 * n_arr]
        send_sems, recv_sems = refs[2 * n_arr:]
        x, y, c = _mesh_pos()
        chips = [(1 - x, y), (x, 1 - y), (1 - x, 1 - y)]

        def copy(a, j, pc):
            px, py = chips[j]
            s = 4 * px + 2 * py + pc
            return pltpu.make_async_remote_copy(
                src_ref=ins[a].at[s], dst_ref=outs[a].at[s], send_sem=send_sems.at[a, j], recv_sem=recv_sems.at[a, j],
                device_id=(x, y, 1 - c), device_id_type=MESH)

        for a in range(n_arr):
            for j in range(3):
                copy(a, j, c).start()
        for a in range(n_arr):
            for j in range(3):
                copy(a, j, 1 - c).wait_recv()
                copy(a, j, c).wait_send()

    return pl.pallas_call(
        body, name=name,
        out_shape=[jax.ShapeDtypeStruct(g.shape, g.dtype) for g in gathered],
        in_specs=[HBM_SPEC] * n_arr, out_specs=[HBM_SPEC] * n_arr,
        input_output_aliases={a: a for a in range(n_arr)},
        scratch_shapes=[pltpu.SemaphoreType.DMA((n_arr, 3)), pltpu.SemaphoreType.DMA((n_arr, 3))],
    )(*gathered)


def _exchange_sibling(name, partials):
    n_arr = len(partials)

    def body(*refs):
        ins, outs = refs[:n_arr], refs[n_arr:2 * n_arr]
        send_sems, recv_sems = refs[2 * n_arr:]
        x, y, c = _mesh_pos()
        copies = [pltpu.make_async_remote_copy(
            src_ref=ins[a].at[2 * q + 1 - c], dst_ref=outs[a].at[q], send_sem=send_sems.at[a, q],
            recv_sem=recv_sems.at[a, q], device_id=(x, y, 1 - c), device_id_type=MESH)
            for a in range(n_arr) for q in range(4)]
        for cp in copies:
            cp.start()
        for cp in copies:
            cp.wait_recv()
        for cp in copies:
            cp.wait_send()

    return pl.pallas_call(
        body, name=name,
        out_shape=[jax.ShapeDtypeStruct((4,) + p.shape[1:], p.dtype) for p in partials],
        in_specs=[HBM_SPEC] * n_arr, out_specs=[HBM_SPEC] * n_arr,
        scratch_shapes=[pltpu.SemaphoreType.DMA((n_arr, 4)), pltpu.SemaphoreType.DMA((n_arr, 4))],
    )(*partials)


def _exchange_chips(name, chip_sums):
    n_arr = len(chip_sums)

    def body(*refs):
        ins, outs = refs[:n_arr], refs[n_arr:2 * n_arr]
        send_sems, recv_sems = refs[2 * n_arr:]
        x, y, c = _mesh_pos()
        chips = [(1 - x, y), (x, 1 - y), (1 - x, 1 - y)]
        copies = []
        for a in range(n_arr):
            for j, (px, py) in enumerate(chips):
                copies.append(pltpu.make_async_remote_copy(
                    src_ref=ins[a].at[2 * px + py], dst_ref=outs[a].at[j],
                    send_sem=send_sems.at[a, j], recv_sem=recv_sems.at[a, j],
                    device_id=(px, py, c), device_id_type=MESH))
        for cp in copies:
            cp.start()
        for cp in copies:
            cp.wait_recv()
        for cp in copies:
            cp.wait_send()

    hbm = pl.BlockSpec(memory_space=pltpu.HBM)
    return pl.pallas_call(
        body, name=name,
        out_shape=[jax.ShapeDtypeStruct((3,) + p.shape[1:], p.dtype) for p in chip_sums],
        in_specs=[hbm] * n_arr, out_specs=[hbm] * n_arr,
        scratch_shapes=[pltpu.SemaphoreType.DMA((n_arr, 3)), pltpu.SemaphoreType.DMA((n_arr, 3))],
    )(*chip_sums)


def _matmul(name, a, b, dims, grid, a_spec, b_spec, o_spec, out_shape, acc_axis=None, fused=None, fused_arrays=()):
    def body(a_ref, b_ref, o_ref):
        r = lax.dot_general(a_ref[...], b_ref[...], dims, preferred_element_type=F32)
        if acc_axis is None:
            o_ref[...] = r.astype(o_ref.dtype)
        else:
            k = pl.program_id(acc_axis)

            @pl.when(k == 0)
            def _():
                o_ref[...] = r

            @pl.when(k > 0)
            def _():
                o_ref[...] += r

    sem = tuple("arbitrary" if i == acc_axis else "parallel" for i in range(len(grid)))
    if fused is None:
        return pl.pallas_call(body, name=name, grid=grid, in_specs=[a_spec, b_spec], out_specs=o_spec,
                              out_shape=out_shape, compiler_params=_params(sem))(a, b)

    def first_last():
        first = last = None
        for ax, n in enumerate(grid):
            f, l = pl.program_id(ax) == 0, pl.program_id(ax) == n - 1
            first, last = (f, l) if first is None else (first & f, last & l)
        return first, last

    (out,), extra = _host_call(body, 2, 1, fused, first_last, name=name, grid=grid, in_specs=[a_spec, b_spec],
                               out_specs=[o_spec], out_shape=[out_shape], scratch_shapes=[], sem=sem,
                               operands=[a, b] + list(fused_arrays))
    return out, extra


def _mm_blocked_rhs(name, a, w_g, tm=512, fused=None, fused_arrays=()):
    m, k = a.shape
    nb = w_g.shape[2]
    return _matmul(name, a, w_g, NN, (N_DEV, m // tm),
                   pl.BlockSpec((tm, k), lambda j, i: (i, 0)),
                   pl.BlockSpec((None, k, nb), lambda j, i: (j, 0, 0)),
                   pl.BlockSpec((tm, nb), lambda j, i: (i, j)),
                   jax.ShapeDtypeStruct((m, N_DEV * nb), F32), fused=fused, fused_arrays=fused_arrays)


def _mm_blocked_rhs_t(name, a, w_g, tm=512, fused=None, fused_arrays=()):
    m = a.shape[0]
    n, nb = w_g.shape[1], w_g.shape[2]
    return _matmul(name, a, w_g, NT, (m // tm, N_DEV),
                   pl.BlockSpec((tm, nb), lambda i, j: (i, j)),
                   pl.BlockSpec((None, n, nb), lambda i, j: (j, 0, 0)),
                   pl.BlockSpec((tm, n), lambda i, j: (i, 0)),
                   jax.ShapeDtypeStruct((m, n), F32), acc_axis=1, fused=fused, fused_arrays=fused_arrays)


def _mm_wgrad_blocked(name, act, dcols, tk=512):
    t, k = act.shape
    nb = dcols.shape[1] // N_DEV
    return _matmul(name, act, dcols, TN, (N_DEV, k // tk),
                   pl.BlockSpec((t, tk), lambda j, i: (0, i)),
                   pl.BlockSpec((t, nb), lambda j, i: (0, j)),
                   pl.BlockSpec((None, tk, nb), lambda j, i: (j, i, 0)),
                   jax.ShapeDtypeStruct((N_DEV, k, nb), BF16))


def _mm_plain(name, a, b, dims, tm, tn, out_dtype, fused=None, fused_arrays=()):
    if dims == NN:
        (m, k), n = a.shape, b.shape[1]
        a_spec = pl.BlockSpec((tm, k), lambda i, j: (i, 0))
        b_spec = pl.BlockSpec((k, tn), lambda i, j: (0, j))
    elif dims == NT:
        (m, k), n = a.shape, b.shape[0]
        a_spec = pl.BlockSpec((tm, k), lambda i, j: (i, 0))
        b_spec = pl.BlockSpec((tn, k), lambda i, j: (j, 0))
    else:
        (k, m), n = a.shape, b.shape[1]
        a_spec = pl.BlockSpec((k, tm), lambda i, j: (0, i))
        b_spec = pl.BlockSpec((k, tn), lambda i, j: (0, j))
    return _matmul(name, a, b, dims, (m // tm, n // tn), a_spec, b_spec,
                   pl.BlockSpec((tm, tn), lambda i, j: (i, j)), jax.ShapeDtypeStruct((m, n), out_dtype),
                   fused=fused, fused_arrays=fused_arrays)


def _ada_fwd(c_all, w_ada_blk, b_blk):
    def body(c_ref, w_ref, b_ref, o_ref):
        cv = c_ref[...]
        o_ref[...] = jnp.dot(cv * _sigmoid(cv), w_ref[...], preferred_element_type=F32) + b_ref[...]

    tn = 512
    return pl.pallas_call(
        body, name="ada_fwd", grid=(ADA_BLK // tn,),
        in_specs=[pl.BlockSpec((N_DEV, D_MODEL), lambda j: (0, 0)),
                  pl.BlockSpec((D_MODEL, tn), lambda j: (0, j)),
                  pl.BlockSpec((1, tn), lambda j: (0, j))],
        out_specs=pl.BlockSpec((N_DEV, tn), lambda j: (0, j)),
        out_shape=jax.ShapeDtypeStruct((N_DEV, ADA_BLK), F32),
        compiler_params=_params(("parallel",)))(c_all, w_ada_blk, b_blk)


def _ada_wgrad(c_all, gmod_cols):
    def body(c_ref, g_ref, o_ref):
        cv = c_ref[...]
        o_ref[...] = lax.dot_general(cv * _sigmoid(cv), g_ref[...], TN, preferred_element_type=F32)

    tk = 512
    return pl.pallas_call(
        body, name="ada_wgrad", grid=(D_MODEL // tk,),
        in_specs=[pl.BlockSpec((N_DEV, tk), lambda i: (0, i)),
                  pl.BlockSpec((N_DEV, ADA_BLK), lambda i: (0, 0))],
        out_specs=pl.BlockSpec((tk, ADA_BLK), lambda i: (i, 0)),
        out_shape=jax.ShapeDtypeStruct((D_MODEL, ADA_BLK), F32),
        compiler_params=_params(("parallel",)))(c_all, gmod_cols)


def _row_spec(cols=D_MODEL):
    return pl.BlockSpec((ROW_TILE, cols), lambda i: (i, 0))


def _vec_spec(cols=D_MODEL):
    return pl.BlockSpec((1, cols), lambda i: (0, 0))


def _norm_fwd(name, x, w, scale, shift, resid=None, gate=None):
    has_res = resid is not None

    def body(*refs):
        if has_res:
            x_ref, r_ref, g_ref, w_ref, sc_ref, sh_ref, xr_ref, h_ref, rs_ref = refs
            xr = x_ref[...] + g_ref[...] * r_ref[...]
            xr_ref[...] = xr
        else:
            x_ref, w_ref, sc_ref, sh_ref, h_ref, rs_ref = refs
            xr = x_ref[...]
        rs = lax.rsqrt(jnp.mean(xr * xr, axis=-1, keepdims=True) + EPS)
        h = (xr * rs) * w_ref[...] * (1.0 + sc_ref[...]) + sh_ref[...]
        h_ref[...] = h.astype(BF16)
        rs_ref[...] = rs

    s = x.shape[0]
    ins = [x] + ([resid, gate] if has_res else []) + [w, scale, shift]
    in_specs = [_row_spec()] + ([_row_spec(), _vec_spec()] if has_res else []) + [_vec_spec()] * 3
    outs = ([jax.ShapeDtypeStruct((s, D_MODEL), F32)] if has_res else []) + [
        jax.ShapeDtypeStruct((s, D_MODEL), BF16), jax.ShapeDtypeStruct((s, 1), F32)]
    out_specs = ([_row_spec()] if has_res else []) + [_row_spec(), pl.BlockSpec((ROW_TILE, 1), lambda i: (i, 0))]
    return pl.pallas_call(body, name=name, grid=(s // ROW_TILE,), in_specs=in_specs, out_specs=out_specs,
                          out_shape=outs, compiler_params=_params(("parallel",)))(*ins)


def _norm_bwd(name, dh, x, rstd, w, scale, dres, mix=None, gate=None):
    has_mix = mix is not None

    def body(*refs):
        if has_mix:
            (dh_ref, x_ref, rs_ref, w_ref, sc_ref, dr_ref, mix_ref, g_ref,
             dx_ref, dmix_ref, dsh_ref, dsc_ref, dw_ref, dg_ref) = refs
        else:
            dh_ref, x_ref, rs_ref, w_ref, sc_ref, dr_ref, dx_ref, dsh_ref, dsc_ref, dw_ref = refs
        i = pl.program_id(0)
        dhv = dh_ref[...]
        rs = rs_ref[...]
        xn = x_ref[...] * rs
        wv = w_ref[...]
        one_sc = 1.0 + sc_ref[...]
        dxn = dhv * wv * one_sc
        dx = dr_ref[...] + rs * (dxn - xn * jnp.mean(dxn * xn, axis=-1, keepdims=True))
        dx_ref[...] = dx
        sums = [(dsh_ref, dhv), (dsc_ref, dhv * xn * wv), (dw_ref, dhv * one_sc * xn)]
        if has_mix:
            dmix_ref[...] = (dx * g_ref[...]).astype(BF16)
            sums.append((dg_ref, dx * mix_ref[...]))

        @pl.when(i == 0)
        def _():
            for ref, _v in sums:
                ref[...] = jnp.zeros_like(ref)

        for ref, v in sums:
            ref[...] += jnp.sum(v, axis=0, keepdims=True)

    s = x.shape[0]
    ins = [dh, x, rstd, w, scale, dres] + ([mix, gate] if has_mix else [])
    in_specs = ([_row_spec(), _row_spec(), pl.BlockSpec((ROW_TILE, 1), lambda i: (i, 0)), _vec_spec(), _vec_spec(),
                 _row_spec()] + ([_row_spec(), _vec_spec()] if has_mix else []))
    vec = jax.ShapeDtypeStruct((1, D_MODEL), F32)
    outs = ([jax.ShapeDtypeStruct((s, D_MODEL), F32)] + ([jax.ShapeDtypeStruct((s, D_MODEL), BF16)] if has_mix else [])
            + [vec] * (4 if has_mix else 3))
    out_specs = [_row_spec()] + ([_row_spec()] if has_mix else []) + [_vec_spec()] * (4 if has_mix else 3)
    return pl.pallas_call(body, name=name, grid=(s // ROW_TILE,), in_specs=in_specs, out_specs=out_specs,
                          out_shape=outs, compiler_params=_params(("arbitrary",)))(*ins)


def _loss_head(x1, ffn, gate2, target):
    def body(x_ref, f_ref, g_ref, t_ref, loss_ref, dout_ref, dffn_ref, dg_ref):
        i = pl.program_id(0)
        fv = f_ref[...]
        gv = g_ref[...]
        err = x_ref[...] + gv * fv - t_ref[...]
        dout = err * (1.0 / D_MODEL)
        dout_ref[...] = dout
        dffn_ref[...] = (dout * gv).astype(BF16)

        @pl.when(i == 0)
        def _():
            loss_ref[...] = jnp.zeros_like(loss_ref)
            dg_ref[...] = jnp.zeros_like(dg_ref)

        row = jnp.sum(err * err, axis=-1, keepdims=True) * (1.0 / D_MODEL)
        loss_ref[...] += jnp.broadcast_to(0.5 * jnp.sum(row, axis=0, keepdims=True), (1, 128))
        dg_ref[...] += jnp.sum(dout * fv, axis=0, keepdims=True)

    s = x1.shape[0]
    return pl.pallas_call(
        body, name="loss_head", grid=(s // ROW_TILE,),
        in_specs=[_row_spec(), _row_spec(), _vec_spec(), _row_spec()],
        out_specs=[pl.BlockSpec((1, 128), lambda i: (0, 0)), _row_spec(), _row_spec(), _vec_spec()],
        out_shape=[jax.ShapeDtypeStruct((1, 128), F32), jax.ShapeDtypeStruct((s, D_MODEL), F32),
                   jax.ShapeDtypeStruct((s, D_MODEL), BF16), jax.ShapeDtypeStruct((1, D_MODEL), F32)],
        compiler_params=_params(("arbitrary",)))(x1, ffn, gate2, target)


CONV_TILE = 512
N_CONV_TILES = D_FF // CONV_TILE


def _shift_rows(a, k, row):
    n = a.shape[0]
    if k > 0:
        return jnp.where(row >= k, pltpu.roll(a, k, 0), 0.0)
    return jnp.where(row < n + k, pltpu.roll(a, n + k, 0), 0.0)


def _conv_gate_fwd(u, conv_w, conv_b, fused=None, fused_arrays=()):
    s = u.shape[0]

    def body(a_ref, g_ref, w_ref, b_ref, y_ref):
        a = a_ref[...]
        w = w_ref[...]
        row = lax.broadcasted_iota(jnp.int32, a.shape, 0)
        ac = b_ref[...] + _shift_rows(a, 2, row) * w[0:1] + _shift_rows(a, 1, row) * w[1:2] + a * w[2:3]
        y_ref[...] = (ac * _sigmoid(ac) * g_ref[...]).astype(BF16)

    def first_last():
        i = pl.program_id(0)
        return i == 0, i == N_CONV_TILES - 1

    col = lambda off: pl.BlockSpec((s, CONV_TILE), lambda i: (0, i + off))
    (y,), extra = _host_call(
        body, 4, 1, fused, first_last, name="conv_gate_fwd", grid=(N_CONV_TILES,),
        in_specs=[col(0), col(N_CONV_TILES), pl.BlockSpec((3, CONV_TILE), lambda i: (0, i)),
                  pl.BlockSpec((1, CONV_TILE), lambda i: (0, i))],
        out_specs=[col(0)], out_shape=[jax.ShapeDtypeStruct((s, D_FF), BF16)], scratch_shapes=[], sem=("parallel",),
        operands=[u, u, conv_w, conv_b] + list(fused_arrays))
    return y if fused is None else (y, extra)


def _conv_gate_bwd(u, dy, conv_w, conv_b):
    s = u.shape[0]

    def body(a_ref, g_ref, dy_ref, w_ref, b_ref, da_ref, dg_ref, gw_ref, gb_ref):
        a = a_ref[...]
        w = w_ref[...]
        row = lax.broadcasted_iota(jnp.int32, a.shape, 0)
        a1 = _shift_rows(a, 1, row)
        a2 = _shift_rows(a, 2, row)
        ac = b_ref[...] + a2 * w[0:1] + a1 * w[1:2] + a * w[2:3]
        sg = _sigmoid(ac)
        dyv = dy_ref[...]
        dg_ref[...] = (dyv * (ac * sg)).astype(BF16)
        dac = dyv * g_ref[...] * _dsilu(ac, sg)
        gb_ref[...] = jnp.sum(dac, axis=0, keepdims=True)
        gw_ref[0:1, :] = jnp.sum(dac * a2, axis=0, keepdims=True)
        gw_ref[1:2, :] = jnp.sum(dac * a1, axis=0, keepdims=True)
        gw_ref[2:3, :] = jnp.sum(dac * a, axis=0, keepdims=True)
        da = dac * w[2:3] + _shift_rows(dac, -1, row) * w[1:2] + _shift_rows(dac, -2, row) * w[0:1]
        da_ref[...] = da.astype(BF16)

    col = lambda off: pl.BlockSpec((s, CONV_TILE), lambda i: (0, i + off))
    return pl.pallas_call(
        body, name="conv_gate_bwd", grid=(N_CONV_TILES,),
        in_specs=[col(0), col(N_CONV_TILES), col(0), pl.BlockSpec((3, CONV_TILE), lambda i: (0, i)),
                  pl.BlockSpec((1, CONV_TILE), lambda i: (0, i))],
        out_specs=[col(0), col(0), pl.BlockSpec((3, CONV_TILE), lambda i: (0, i)),
                   pl.BlockSpec((1, CONV_TILE), lambda i: (0, i))],
        out_shape=[jax.ShapeDtypeStruct((s, D_FF), BF16), jax.ShapeDtypeStruct((s, D_FF), BF16),
                   jax.ShapeDtypeStruct((3, D_FF), F32), jax.ShapeDtypeStruct((1, D_FF), F32)],
        compiler_params=_params(("parallel",)))(u, u, dy, conv_w, conv_b)


HG_TILE = 128
CHUNK_UNROLL = 8


def _unrolled_loop(n, body, init):
    def group(i, carry):
        for u in range(CHUNK_UNROLL):
            carry = body(i * CHUNK_UNROLL + u, carry)
        return carry

    return lax.fori_loop(0, n // CHUNK_UNROLL, group, init)


def _head_col(off):
    return pl.BlockSpec((SEQ, HEAD_DIM), lambda h: (0, h + off))


def _hgrn_gates(hq, hf, lb, pos):
    q = hq * _sigmoid(hq)
    sig = _sigmoid(hf)
    f = lb + (1.0 - lb) * sig
    gl = jnp.log(f)
    for sh in (1, 2, 4, 8):
        gl = gl + jnp.where(pos >= sh, pltpu.roll(gl, sh, 0), 0.0)
    return q, sig, f, 1.0 - f, gl


def _lower_bound(lbl):
    return 1.0 / (1.0 + jnp.exp(lbl[1:2, :] - lbl[0:1, :]))


def _head_first_last():
    h = pl.program_id(0)
    return h == 0, h == HEADS - 1


def _hgrn_fwd_serial(proj, lb_logits, norm_w, fused=None, fused_arrays=()):
    n_tiles = SEQ // HG_TILE
    n_chunks = SEQ // CHUNK
    fused_arrays = list(fused_arrays)

    def body(hq_ref, hf_ref, hi_ref, hg_ref, lbl_ref, nw_ref, aout_ref, opre_ref, q_s, k_s, gl_s):
        lb = _lower_bound(lbl_ref[...])
        ones = jnp.ones((HEAD_DIM, HEAD_DIM), BF16)
        pos = lax.broadcasted_iota(jnp.int32, (HG_TILE, HEAD_DIM), 0) % CHUNK

        def tile(i, carry):
            rows = pl.ds(pl.multiple_of(i * HG_TILE, HG_TILE), HG_TILE)
            v = hi_ref[rows, :]
            q, _sig, _f, kk, gl = _hgrn_gates(hq_ref[rows, :], hf_ref[rows, :], lb, pos)
            o = _lane_sum(q * kk, ones) * v
            for d in range(1, CHUNK):
                e = jnp.where(pos >= d, jnp.exp(gl - pltpu.roll(gl, d, 0)), 0.0)
                o = o + _lane_sum(q * pltpu.roll(kk, d, 0) * e, ones) * pltpu.roll(v, d, 0)
            q_s[rows, :] = q
            k_s[rows, :] = kk
            gl_s[rows, :] = gl
            opre_ref[rows, :] = o
            return carry

        lax.fori_loop(0, n_tiles, tile, 0)

        def chunk(c, st):
            rows = pl.ds(pl.multiple_of(c * CHUNK, CHUNK), CHUNK)
            gl = gl_s[rows, :]
            qt = q_s[rows, :] * jnp.exp(gl)
            opre_ref[rows, :] += lax.dot_general(qt.astype(BF16), st.astype(BF16), NT, preferred_element_type=F32)
            gll = gl[CHUNK - 1:CHUNK, :]
            kt = k_s[rows, :] * jnp.exp(gll - gl)
            return st * jnp.exp(gll) + lax.dot_general(hi_ref[rows, :].astype(BF16), kt.astype(BF16), TN,
                                                       preferred_element_type=F32)

        _unrolled_loop(n_chunks, chunk, jnp.zeros((HEAD_DIM, HEAD_DIM), F32))

        def finish(i, carry):
            rows = pl.ds(pl.multiple_of(i * HG_TILE, HG_TILE), HG_TILE)
            o = opre_ref[rows, :]
            hg = hg_ref[rows, :]
            rs = lax.rsqrt(jnp.mean(o * o, axis=-1, keepdims=True) + EPS)
            aout_ref[rows, :] = ((o * rs) * nw_ref[...] * (hg * _sigmoid(hg))).astype(BF16)
            return carry

        lax.fori_loop(0, n_tiles, finish, 0)

    return _host_call(
        body, 6, 2, fused, _head_first_last, name="hgrn_fwd", grid=(HEADS,),
        in_specs=[_head_col(0), _head_col(HEADS), _head_col(2 * HEADS), _head_col(3 * HEADS),
                  pl.BlockSpec((2, HEAD_DIM), lambda h: (0, h)), pl.BlockSpec((1, HEAD_DIM), lambda h: (0, 0))],
        out_specs=[_head_col(0), _head_col(0)],
        out_shape=[jax.ShapeDtypeStruct((SEQ, HEADS * HEAD_DIM), BF16), jax.ShapeDtypeStruct((SEQ, HEADS * HEAD_DIM), F32)],
        scratch_shapes=[pltpu.VMEM((SEQ, HEAD_DIM), F32)] * 3, sem=("parallel",),
        operands=[proj, proj, proj, proj, lb_logits, norm_w] + fused_arrays)


def _hgrn_bwd_serial(proj, lb_logits, norm_w, o_pre, d_aout, fused=None, fused_arrays=()):
    n_tiles = SEQ // HG_TILE
    n_chunks = SEQ // CHUNK

    def body(hq_ref, hf_ref, hi_ref, hg_ref, lbl_ref, nw_ref, opre_ref, da_ref,
             dhq_ref, dhf_ref, dhi_ref, dhg_ref, dlog_ref, gnw_ref,
             q_s, k_s, gl_s, do_s, dq_s, dk_s, dv_s, st_s):
        h = pl.program_id(0)
        lb = _lower_bound(lbl_ref[...])
        nw = nw_ref[...]
        ones = jnp.ones((HEAD_DIM, HEAD_DIM), BF16)
        pos = lax.broadcasted_iota(jnp.int32, (HG_TILE, HEAD_DIM), 0) % CHUNK

        @pl.when(h == 0)
        def _():
            gnw_ref[...] = jnp.zeros_like(gnw_ref)

        def tile(i, carry):
            rows = pl.ds(pl.multiple_of(i * HG_TILE, HG_TILE), HG_TILE)
            v = hi_ref[rows, :]
            q, _sig, _f, kk, gl = _hgrn_gates(hq_ref[rows, :], hf_ref[rows, :], lb, pos)
            o = opre_ref[rows, :]
            hg = hg_ref[rows, :]
            da = da_ref[rows, :]
            rs = lax.rsqrt(jnp.mean(o * o, axis=-1, keepdims=True) + EPS)
            oh = o * rs
            sg = _sigmoid(hg)
            dnorm = da * (hg * sg)
            dhg_ref[rows, :] = (da * (oh * nw) * _dsilu(hg, sg)).astype(BF16)
            gnw_ref[...] += jnp.sum(dnorm * oh, axis=0, keepdims=True)
            doh = dnorm * nw
            do = rs * (doh - oh * jnp.mean(doh * oh, axis=-1, keepdims=True))

            d_a = _lane_sum(do * v, ones)
            dq = d_a * kk
            dk = d_a * q
            dv = _lane_sum(q * kk, ones) * do
            for d in range(1, CHUNK):
                ks = pltpu.roll(kk, d, 0)
                e = jnp.where(pos >= d, jnp.exp(gl - pltpu.roll(gl, d, 0)), 0.0)
                a_d = _lane_sum(q * ks * e, ones)
                d_a = _lane_sum(do * pltpu.roll(v, d, 0), ones) * e
                dq = dq + d_a * ks
                dk = dk + pltpu.roll(d_a * q, HG_TILE - d, 0)
                dv = dv + pltpu.roll(a_d * do, HG_TILE - d, 0)
            q_s[rows, :] = q
            k_s[rows, :] = kk
            gl_s[rows, :] = gl
            do_s[rows, :] = do
            dq_s[rows, :] = dq
            dk_s[rows, :] = dk
            dv_s[rows, :] = dv
            return carry

        lax.fori_loop(0, n_tiles, tile, 0)

        def fwd_chunk(c, st):
            rows = pl.ds(pl.multiple_of(c * CHUNK, CHUNK), CHUNK)
            gl = gl_s[rows, :]
            st_s[c] = st
            dq_s[rows, :] += jnp.dot(do_s[rows, :].astype(BF16), st.astype(BF16),
                                     preferred_element_type=F32) * jnp.exp(gl)
            gll = gl[CHUNK - 1:CHUNK, :]
            kt = k_s[rows, :] * jnp.exp(gll - gl)
            return st * jnp.exp(gll) + lax.dot_general(hi_ref[rows, :].astype(BF16), kt.astype(BF16), TN,
                                                       preferred_element_type=F32)

        _unrolled_loop(n_chunks, fwd_chunk, jnp.zeros((HEAD_DIM, HEAD_DIM), F32))

        pos_c = lax.broadcasted_iota(jnp.int32, (CHUNK, HEAD_DIM), 0)

        def bwd_chunk(i, carry):
            rt, dlb = carry
            c = n_chunks - 1 - i
            rows = pl.ds(pl.multiple_of(c * CHUNK, CHUNK), CHUNK)
            gl = gl_s[rows, :]
            q = q_s[rows, :]
            kk = k_s[rows, :]
            do = do_s[rows, :]
            gll = gl[CHUNK - 1:CHUNK, :]
            egl = jnp.exp(gll)
            ekt = jnp.exp(gll - gl)
            rt_b = rt.astype(BF16)
            dk_in = dk_s[rows, :]
            dk_far = jnp.dot(hi_ref[rows, :].astype(BF16), rt_b, preferred_element_type=F32) * ekt
            dk = dk_in + dk_far
            dv = dv_s[rows, :] + lax.dot_general((kk * ekt).astype(BF16), rt_b, NT, preferred_element_type=F32)
            dq = dq_s[rows, :]
            rc = q * dq - kk * dk_in
            pc = kk * dk_far
            pre = pc
            for sh in (1, 2, 4, 8):
                rc = rc + jnp.where(pos_c < CHUNK - sh, pltpu.roll(rc, CHUNK - sh, 0), 0.0)
                pre = pre + jnp.where(pos_c >= sh, pltpu.roll(pre, sh, 0), 0.0)
            across = jnp.sum(st_s[c] * rt, axis=0, keepdims=True) * egl
            dgl = rc + (pre - pc) + across
            hf = hf_ref[rows, :]
            sig = _sigmoid(hf)
            f = lb + (1.0 - lb) * sig
            df = dgl / f - dk
            dhf_ref[rows, :] = (df * (1.0 - lb) * sig * (1.0 - sig)).astype(BF16)
            hq = hq_ref[rows, :]
            dhq_ref[rows, :] = (dq * _dsilu(hq, _sigmoid(hq))).astype(BF16)
            dhi_ref[rows, :] = dv.astype(BF16)
            rt_new = rt * egl + lax.dot_general(do.astype(BF16), (q * jnp.exp(gl)).astype(BF16), TN,
                                                preferred_element_type=F32)
            return (rt_new, dlb + jnp.sum(df * (1.0 - sig), axis=0, keepdims=True))

        _, dlb = _unrolled_loop(n_chunks, bwd_chunk,
                                (jnp.zeros((HEAD_DIM, HEAD_DIM), F32), jnp.zeros((1, HEAD_DIM), F32)))
        dl0 = lb * (1.0 - lb) * dlb
        dlog_ref[0:1, :] = dl0
        dlog_ref[1:2, :] = -dl0

    wide = HEADS * HEAD_DIM
    return _host_call(
        body, 8, 6, fused, _head_first_last, name="hgrn_bwd", grid=(HEADS,),
        in_specs=[_head_col(0), _head_col(HEADS), _head_col(2 * HEADS), _head_col(3 * HEADS),
                  pl.BlockSpec((2, HEAD_DIM), lambda h: (0, h)), pl.BlockSpec((1, HEAD_DIM), lambda h: (0, 0)),
                  _head_col(0), _head_col(0)],
        out_specs=[_head_col(0)] * 4 + [pl.BlockSpec((2, HEAD_DIM), lambda h: (0, h)),
                                        pl.BlockSpec((1, HEAD_DIM), lambda h: (0, 0))],
        out_shape=[jax.ShapeDtypeStruct((SEQ, wide), BF16)] * 4 + [jax.ShapeDtypeStruct((2, wide), F32),
                                                                    jax.ShapeDtypeStruct((1, HEAD_DIM), F32)],
        scratch_shapes=[pltpu.VMEM((SEQ, HEAD_DIM), F32)] * 7 + [pltpu.VMEM((n_chunks, HEAD_DIM, HEAD_DIM), F32)],
        sem=("arbitrary",),
        operands=[proj, proj, proj, proj, lb_logits, norm_w, o_pre, d_aout] + list(fused_arrays))


CHUNKS_PER_TILE = HG_TILE // CHUNK


def _chunk_end(x, pos):
    y = jnp.where(pos == CHUNK - 1, x, 0.0)
    for sh in (1, 2, 4, 8):
        y = y + jnp.where(pos < CHUNK - sh, pltpu.roll(y, x.shape[0] - sh, 0), 0.0)
    return y


def _suffix_in_chunk(x, pos):
    for sh in (1, 2, 4, 8):
        x = x + jnp.where(pos < CHUNK - sh, pltpu.roll(x, x.shape[0] - sh, 0), 0.0)
    return x


def _prefix_in_chunk(x, pos):
    for sh in (1, 2, 4, 8):
        x = x + jnp.where(pos >= sh, pltpu.roll(x, sh, 0), 0.0)
    return x


def _chunk_rows(cc):
    return slice(cc * CHUNK, (cc + 1) * CHUNK)


def _outer_products(lhs_b, rhs_b, dst, i):
    for cc in range(CHUNKS_PER_TILE):
        dst[i * CHUNKS_PER_TILE + cc] = lax.dot_general(lhs_b[_chunk_rows(cc)], rhs_b[_chunk_rows(cc)], TN,
                                                        preferred_element_type=F32)


def _state_scan(n_chunks, gl_s, u_s, keep, reverse):
    def step(k, st):
        c = n_chunks - 1 - k if reverse else k
        keep[c] = st.astype(BF16)
        gl = gl_s[pl.ds(pl.multiple_of(c * CHUNK, CHUNK), CHUNK), :]
        return st * jnp.exp(gl[CHUNK - 1:CHUNK, :]) + u_s[c]

    _unrolled_loop(n_chunks, step, jnp.zeros((HEAD_DIM, HEAD_DIM), F32))


def _hgrn_fwd(proj, lb_logits, norm_w, fused=None, fused_arrays=()):
    n_tiles = SEQ // HG_TILE
    n_chunks = SEQ // CHUNK
    fused_arrays = list(fused_arrays)

    def body(hq_ref, hf_ref, hi_ref, hg_ref, lbl_ref, nw_ref, aout_ref, opre_ref, qt_s, gl_s, u_s, st_s):
        lb = _lower_bound(lbl_ref[...])
        ones = jnp.ones((HEAD_DIM, HEAD_DIM), BF16)
        pos = lax.broadcasted_iota(jnp.int32, (HG_TILE, HEAD_DIM), 0) % CHUNK

        def tile(i, carry):
            rows = pl.ds(pl.multiple_of(i * HG_TILE, HG_TILE), HG_TILE)
            v = hi_ref[rows, :]
            q, _sig, _f, kk, gl = _hgrn_gates(hq_ref[rows, :], hf_ref[rows, :], lb, pos)
            o = _lane_sum(q * kk, ones) * v
            for d in range(1, CHUNK):
                e = jnp.where(pos >= d, jnp.exp(gl - pltpu.roll(gl, d, 0)), 0.0)
                o = o + _lane_sum(q * pltpu.roll(kk, d, 0) * e, ones) * pltpu.roll(v, d, 0)
            opre_ref[rows, :] = o
            qt_s[rows, :] = q * jnp.exp(gl)
            gl_s[rows, :] = gl
            kt = kk * jnp.exp(_chunk_end(gl, pos) - gl)
            _outer_products(v.astype(BF16), kt.astype(BF16), u_s, i)
            return carry

        lax.fori_loop(0, n_tiles, tile, 0)
        _state_scan(n_chunks, gl_s, u_s, st_s, reverse=False)

        def finish(i, carry):
            rows = pl.ds(pl.multiple_of(i * HG_TILE, HG_TILE), HG_TILE)
            qt_b = qt_s[rows, :].astype(BF16)
            past = [lax.dot_general(qt_b[_chunk_rows(cc)], st_s[i * CHUNKS_PER_TILE + cc], NT,
                                    preferred_element_type=F32) for cc in range(CHUNKS_PER_TILE)]
            o = opre_ref[rows, :] + jnp.concatenate(past, axis=0)
            opre_ref[rows, :] = o
            hg = hg_ref[rows, :]
            rs = lax.rsqrt(jnp.mean(o * o, axis=-1, keepdims=True) + EPS)
            aout_ref[rows, :] = ((o * rs) * nw_ref[...] * (hg * _sigmoid(hg))).astype(BF16)
            return carry

        lax.fori_loop(0, n_tiles, finish, 0)

    return _host_call(
        body, 6, 2, fused, _head_first_last, name="hgrn_fwd", grid=(HEADS,),
        in_specs=[_head_col(0), _head_col(HEADS), _head_col(2 * HEADS), _head_col(3 * HEADS),
                  pl.BlockSpec((2, HEAD_DIM), lambda h: (0, h)), pl.BlockSpec((1, HEAD_DIM), lambda h: (0, 0))],
        out_specs=[_head_col(0), _head_col(0)],
        out_shape=[jax.ShapeDtypeStruct((SEQ, HEADS * HEAD_DIM), BF16), jax.ShapeDtypeStruct((SEQ, HEADS * HEAD_DIM), F32)],
        scratch_shapes=[pltpu.VMEM((SEQ, HEAD_DIM), F32)] * 2 + [pltpu.VMEM((n_chunks, HEAD_DIM, HEAD_DIM), F32),
                                                                 pltpu.VMEM((n_chunks, HEAD_DIM, HEAD_DIM), BF16)],
        sem=("parallel",), operands=[proj, proj, proj, proj, lb_logits, norm_w] + fused_arrays)


def _hgrn_bwd(proj, lb_logits, norm_w, o_pre, d_aout, fused=None, fused_arrays=()):
    n_tiles = SEQ // HG_TILE
    n_chunks = SEQ // CHUNK

    def body(hq_ref, hf_ref, hi_ref, hg_ref, lbl_ref, nw_ref, opre_ref, da_ref,
             dhq_ref, dhf_ref, dhi_ref, dhg_ref, dlog_ref, gnw_ref,
             q_s, k_s, gl_s, do_s, dq_s, dk_s, dv_s, u_s, st_s, rt_s):
        h = pl.program_id(0)
        lb = _lower_bound(lbl_ref[...])
        nw = nw_ref[...]
        ones = jnp.ones((HEAD_DIM, HEAD_DIM), BF16)
        pos = lax.broadcasted_iota(jnp.int32, (HG_TILE, HEAD_DIM), 0) % CHUNK

        @pl.when(h == 0)
        def _():
            gnw_ref[...] = jnp.zeros_like(gnw_ref)

        def tile(i, carry):
            rows = pl.ds(pl.multiple_of(i * HG_TILE, HG_TILE), HG_TILE)
            v = hi_ref[rows, :]
            q, _sig, _f, kk, gl = _hgrn_gates(hq_ref[rows, :], hf_ref[rows, :], lb, pos)
            o = opre_ref[rows, :]
            hg = hg_ref[rows, :]
            da = da_ref[rows, :]
            rs = lax.rsqrt(jnp.mean(o * o, axis=-1, keepdims=True) + EPS)
            oh = o * rs
            sg = _sigmoid(hg)
            dnorm = da * (hg * sg)
            dhg_ref[rows, :] = (da * (oh * nw) * _dsilu(hg, sg)).astype(BF16)
            gnw_ref[...] += jnp.sum(dnorm * oh, axis=0, keepdims=True)
            doh = dnorm * nw
            do = rs * (doh - oh * jnp.mean(doh * oh, axis=-1, keepdims=True))

            d_a = _lane_sum(do * v, ones)
            dq = d_a * kk
            dk = d_a * q
            dv = _lane_sum(q * kk, ones) * do
            for d in range(1, CHUNK):
                ks = pltpu.roll(kk, d, 0)
                e = jnp.where(pos >= d, jnp.exp(gl - pltpu.roll(gl, d, 0)), 0.0)
                a_d = _lane_sum(q * ks * e, ones)
                d_a = _lane_sum(do * pltpu.roll(v, d, 0), ones) * e
                dq = dq + d_a * ks
                dk = dk + pltpu.roll(d_a * q, HG_TILE - d, 0)
                dv = dv + pltpu.roll(a_d * do, HG_TILE - d, 0)
            q_s[rows, :] = q
            k_s[rows, :] = kk
            gl_s[rows, :] = gl
            do_s[rows, :] = do
            dq_s[rows, :] = dq
            dk_s[rows, :] = dk
            dv_s[rows, :] = dv
            kt = kk * jnp.exp(_chunk_end(gl, pos) - gl)
            _outer_products(v.astype(BF16), kt.astype(BF16), u_s, i)
            return carry

        lax.fori_loop(0, n_tiles, tile, 0)
        _state_scan(n_chunks, gl_s, u_s, st_s, reverse=False)

        def reverse_increments(i, carry):
            rows = pl.ds(pl.multiple_of(i * HG_TILE, HG_TILE), HG_TILE)
            qt = q_s[rows, :] * jnp.exp(gl_s[rows, :])
            _outer_products(do_s[rows, :].astype(BF16), qt.astype(BF16), u_s, i)
            return carry

        lax.fori_loop(0, n_tiles, reverse_increments, 0)
        _state_scan(n_chunks, gl_s, u_s, rt_s, reverse=True)

        def finish(i, dlb):
            rows = pl.ds(pl.multiple_of(i * HG_TILE, HG_TILE), HG_TILE)
            q = q_s[rows, :]
            kk = k_s[rows, :]
            gl = gl_s[rows, :]
            gll = _chunk_end(gl, pos)
            ekt = jnp.exp(gll - gl)
            do_b = do_s[rows, :].astype(BF16)
            v_b = hi_ref[rows, :].astype(BF16)
            kt_b = (kk * ekt).astype(BF16)
            dq_far, dk_far, dv_far, across = [], [], [], []
            for cc in range(CHUNKS_PER_TILE):
                st = st_s[i * CHUNKS_PER_TILE + cc]
                rt = rt_s[i * CHUNKS_PER_TILE + cc]
                sl = _chunk_rows(cc)
                dq_far.append(jnp.dot(do_b[sl], st, preferred_element_type=F32))
                dk_far.append(jnp.dot(v_b[sl], rt, preferred_element_type=F32))
                dv_far.append(lax.dot_general(kt_b[sl], rt, NT, preferred_element_type=F32))
                both = jnp.sum(st.astype(F32) * rt.astype(F32), axis=0, keepdims=True)
                across.append(jnp.broadcast_to(both, (CHUNK, HEAD_DIM)))
            dq = dq_s[rows, :] + jnp.concatenate(dq_far, axis=0) * jnp.exp(gl)
            dk_in = dk_s[rows, :]
            dk_out = jnp.concatenate(dk_far, axis=0) * ekt
            dk = dk_in + dk_out
            dv = dv_s[rows, :] + jnp.concatenate(dv_far, axis=0)
            pc = kk * dk_out
            dgl = (_suffix_in_chunk(q * dq - kk * dk_in, pos) + (_prefix_in_chunk(pc, pos) - pc)
                   + jnp.concatenate(across, axis=0) * jnp.exp(gll))
            hf = hf_ref[rows, :]
            sig = _sigmoid(hf)
            f = lb + (1.0 - lb) * sig
            df = dgl / f - dk
            dhf_ref[rows, :] = (df * (1.0 - lb) * sig * (1.0 - sig)).astype(BF16)
            hq = hq_ref[rows, :]
            dhq_ref[rows, :] = (dq * _dsilu(hq, _sigmoid(hq))).astype(BF16)
            dhi_ref[rows, :] = dv.astype(BF16)
            return dlb + jnp.sum(df * (1.0 - sig), axis=0, keepdims=True)

        dlb = lax.fori_loop(0, n_tiles, finish, jnp.zeros((1, HEAD_DIM), F32))
        dl0 = lb * (1.0 - lb) * dlb
        dlog_ref[0:1, :] = dl0
        dlog_ref[1:2, :] = -dl0

    wide = HEADS * HEAD_DIM
    return _host_call(
        body, 8, 6, fused, _head_first_last, name="hgrn_bwd", grid=(HEADS,),
        in_specs=[_head_col(0), _head_col(HEADS), _head_col(2 * HEADS), _head_col(3 * HEADS),
                  pl.BlockSpec((2, HEAD_DIM), lambda h: (0, h)), pl.BlockSpec((1, HEAD_DIM), lambda h: (0, 0)),
                  _head_col(0), _head_col(0)],
        out_specs=[_head_col(0)] * 4 + [pl.BlockSpec((2, HEAD_DIM), lambda h: (0, h)),
                                        pl.BlockSpec((1, HEAD_DIM), lambda h: (0, 0))],
        out_shape=[jax.ShapeDtypeStruct((SEQ, wide), BF16)] * 4 + [jax.ShapeDtypeStruct((2, wide), F32),
                                                                    jax.ShapeDtypeStruct((1, HEAD_DIM), F32)],
        scratch_shapes=[pltpu.VMEM((SEQ, HEAD_DIM), F32)] * 7 + [pltpu.VMEM((n_chunks, HEAD_DIM, HEAD_DIM), F32),
                                                                 pltpu.VMEM((n_chunks, HEAD_DIM, HEAD_DIM), BF16),
                                                                 pltpu.VMEM((n_chunks, HEAD_DIM, HEAD_DIM), BF16)],
        sem=("arbitrary",),
        operands=[proj, proj, proj, proj, lb_logits, norm_w, o_pre, d_aout] + list(fused_arrays))


Q_TILE = 256
ATT_SCALE = HEAD_DIM ** -0.5
ATT_OFF = 4 * HEADS


def _qk_prep(proj, q_w, k_w):
    def body(aq_ref, ak_ref, av_ref, qw_ref, kw_ref, qn_ref, kn_ref, v_ref):
        aq = aq_ref[...]
        ak = ak_ref[...]
        qn_ref[...] = (aq * lax.rsqrt(jnp.mean(aq * aq, axis=-1, keepdims=True) + EPS) * qw_ref[...]).astype(BF16)
        kn_ref[...] = (ak * lax.rsqrt(jnp.mean(ak * ak, axis=-1, keepdims=True) + EPS) * kw_ref[...]).astype(BF16)
        v_ref[...] = av_ref[...].astype(BF16)

    wide = HEADS * HEAD_DIM
    vec = pl.BlockSpec((1, HEAD_DIM), lambda h: (0, 0))
    return pl.pallas_call(
        body, name="qk_prep", grid=(HEADS,),
        in_specs=[_head_col(ATT_OFF), _head_col(ATT_OFF + HEADS), _head_col(ATT_OFF + 2 * HEADS), vec, vec],
        out_specs=[_head_col(0)] * 3, out_shape=[jax.ShapeDtypeStruct((SEQ, wide), BF16)] * 3,
        compiler_params=_params(("parallel",)))(proj, proj, proj, q_w, k_w)


def _alibi_slopes():
    slopes = jnp.exp2(-8.0 * jnp.arange(1, HEADS + 1, dtype=F32) / HEADS)
    return jnp.broadcast_to(slopes[:, None, None], (HEADS, 1, HEAD_DIM))


SLOPE_SPEC = pl.BlockSpec((None, 1, HEAD_DIM), lambda h, i: (h, 0, 0))


N_Q_TILES = SEQ // Q_TILE
K_BLOCK = 512
NOT_ATTENDED = 1e35


def _att_tables():
    o = jnp.arange(N_Q_TILES, dtype=jnp.int32)[:, None, None]
    r = jnp.arange(Q_TILE, dtype=jnp.int32)[None, :, None]
    c = jnp.arange(K_BLOCK, dtype=jnp.int32)[None, None, :]
    dist = o * Q_TILE + r - c
    mult = ((dist <= 128).astype(F32) + (((dist % 4) == 0) & (dist <= 512)).astype(F32)
            + ((dist % 16) == 0).astype(F32))
    valid = (dist >= 0) & (mult > 0)
    return (jnp.where(valid, dist.astype(F32), NOT_ATTENDED),
            jnp.where(valid, jnp.log(jnp.maximum(mult, 1.0)), 0.0))


TABLE_SPEC = pl.BlockSpec((N_Q_TILES, Q_TILE, K_BLOCK), lambda h, i: (0, 0, 0))


def _att_block(q, k_ref, j, i, slope, dist_ref, lmul_ref):
    rows = pl.ds(pl.multiple_of(j * K_BLOCK, K_BLOCK), K_BLOCK)
    off = i - j * (K_BLOCK // Q_TILE)
    s = lax.dot_general(q, k_ref[rows, :], NT, preferred_element_type=F32) * ATT_SCALE
    return s - slope * dist_ref[off] + lmul_ref[off], rows


def _n_key_blocks(i):
    return (i + K_BLOCK // Q_TILE) // (K_BLOCK // Q_TILE)


def _att_first_last():
    h, i = pl.program_id(0), pl.program_id(1)
    return (h == 0) & (i == 0), (h == HEADS - 1) & (i == N_Q_TILES - 1)


def _attn_fwd(qn, kn, vb, fused=None, fused_arrays=()):
    def body(q_ref, k_ref, v_ref, sl_ref, dist_ref, lmul_ref, o_ref, lse_ref):
        i = pl.program_id(1)
        q = q_ref[...]
        slope = sl_ref[0:1, 0:1]

        def step(j, carry):
            m, l, acc = carry
            sb, rows = _att_block(q, k_ref, j, i, slope, dist_ref, lmul_ref)
            m_new = jnp.maximum(m, jnp.max(sb, axis=-1, keepdims=True))
            alpha = jnp.exp(m - m_new)
            p = jnp.exp(sb - m_new)
            l = alpha * l + jnp.sum(p, axis=-1, keepdims=True)
            acc = alpha * acc + jnp.dot(p.astype(BF16), v_ref[rows, :], preferred_element_type=F32)
            return m_new, l, acc

        m, l, acc = lax.fori_loop(0, _n_key_blocks(i), step,
                                  (jnp.full((Q_TILE, 1), -1e30, F32), jnp.zeros((Q_TILE, 1), F32),
                                   jnp.zeros((Q_TILE, HEAD_DIM), F32)))
        o_ref[...] = acc / l
        lse_ref[...] = m + jnp.log(l)

    wide = HEADS * HEAD_DIM
    qt = pl.BlockSpec((Q_TILE, HEAD_DIM), lambda h, i: (i, h))
    full = pl.BlockSpec((SEQ, HEAD_DIM), lambda h, i: (0, h))
    return _host_call(
        body, 6, 2, fused, _att_first_last, name="attn_fwd", grid=(HEADS, N_Q_TILES),
        in_specs=[qt, full, full, SLOPE_SPEC, TABLE_SPEC, TABLE_SPEC],
        out_specs=[qt, pl.BlockSpec((None, Q_TILE, 1), lambda h, i: (h, i, 0))],
        out_shape=[jax.ShapeDtypeStruct((SEQ, wide), F32), jax.ShapeDtypeStruct((HEADS, SEQ, 1), F32)],
        scratch_shapes=[], sem=("parallel", "parallel"),
        operands=[qn, kn, vb, _alibi_slopes(), *_att_tables()] + list(fused_arrays))


def _attn_bwd(qn, kn, vb, o, lse, d_mix, fused=None, fused_arrays=()):
    def body(q_ref, k_ref, v_ref, o_ref, lse_ref, do_ref, sl_ref, dist_ref, lmul_ref, dq_ref, dk_ref, dv_ref):
        i = pl.program_id(1)
        q = q_ref[...]
        do = do_ref[...]
        do_b = do.astype(BF16)
        slope = sl_ref[0:1, 0:1]
        lse = lse_ref[...]
        delta = jnp.sum(do * o_ref[...], axis=-1, keepdims=True)

        @pl.when(i == 0)
        def _():
            dk_ref[...] = jnp.zeros_like(dk_ref)
            dv_ref[...] = jnp.zeros_like(dv_ref)

        def step(j, dq):
            sb, rows = _att_block(q, k_ref, j, i, slope, dist_ref, lmul_ref)
            p = jnp.exp(sb - lse)
            dp = lax.dot_general(do_b, v_ref[rows, :], NT, preferred_element_type=F32)
            ds = (p * (dp - delta)).astype(BF16)
            dk_ref[rows, :] += lax.dot_general(ds, q, TN, preferred_element_type=F32) * ATT_SCALE
            dv_ref[rows, :] += lax.dot_general(p.astype(BF16), do_b, TN, preferred_element_type=F32)
            return dq + jnp.dot(ds, k_ref[rows, :], preferred_element_type=F32)

        dq = lax.fori_loop(0, _n_key_blocks(i), step, jnp.zeros((Q_TILE, HEAD_DIM), F32))
        dq_ref[...] = dq * ATT_SCALE

    wide = HEADS * HEAD_DIM
    qt = pl.BlockSpec((Q_TILE, HEAD_DIM), lambda h, i: (i, h))
    full = pl.BlockSpec((SEQ, HEAD_DIM), lambda h, i: (0, h))
    return _host_call(
        body, 9, 3, fused, _att_first_last, name="attn_bwd", grid=(HEADS, N_Q_TILES),
        in_specs=[qt, full, full, qt, pl.BlockSpec((None, Q_TILE, 1), lambda h, i: (h, i, 0)),
                  pl.BlockSpec((Q_TILE, HEAD_DIM), lambda h, i: (i, h + HEADS)), SLOPE_SPEC, TABLE_SPEC, TABLE_SPEC],
        out_specs=[qt, full, full], out_shape=[jax.ShapeDtypeStruct((SEQ, wide), F32)] * 3,
        scratch_shapes=[], sem=("parallel", "arbitrary"),
        operands=[qn, kn, vb, o, lse, d_mix, _alibi_slopes(), *_att_tables()] + list(fused_arrays))


def _qk_bwd(proj, q_w, k_w, dqn, dkn, dv):
    def body(aq_ref, ak_ref, qw_ref, kw_ref, dqn_ref, dkn_ref, dv_ref, daq_ref, dak_ref, dav_ref, gq_ref, gk_ref):
        h = pl.program_id(0)

        @pl.when(h == 0)
        def _():
            gq_ref[...] = jnp.zeros_like(gq_ref)
            gk_ref[...] = jnp.zeros_like(gk_ref)

        def one(a_ref, w_ref, d_ref, da_ref, g_ref):
            a = a_ref[...]
            d = d_ref[...]
            rs = lax.rsqrt(jnp.mean(a * a, axis=-1, keepdims=True) + EPS)
            ah = a * rs
            g_ref[...] += jnp.sum(d * ah, axis=0, keepdims=True)
            dah = d * w_ref[...]
            da_ref[...] = (rs * (dah - ah * jnp.mean(dah * ah, axis=-1, keepdims=True))).astype(BF16)

        one(aq_ref, qw_ref, dqn_ref, daq_ref, gq_ref)
        one(ak_ref, kw_ref, dkn_ref, dak_ref, gk_ref)
        dav_ref[...] = dv_ref[...].astype(BF16)

    wide = HEADS * HEAD_DIM
    vec = pl.BlockSpec((1, HEAD_DIM), lambda h: (0, 0))
    return pl.pallas_call(
        body, name="qk_bwd", grid=(HEADS,),
        in_specs=[_head_col(ATT_OFF), _head_col(ATT_OFF + HEADS), vec, vec, _head_col(0), _head_col(0), _head_col(0)],
        out_specs=[_head_col(0)] * 3 + [vec, vec],
        out_shape=[jax.ShapeDtypeStruct((SEQ, wide), BF16)] * 3 + [jax.ShapeDtypeStruct((1, HEAD_DIM), F32)] * 2,
        compiler_params=_params(("arbitrary",)))(proj, proj, q_w, k_w, dqn, dkn, dv)


def _pair_sum(name, partial, theirs, core):
    _, r, c = theirs.shape
    tr = r // 2 if r % 16 == 0 else r

    def body(core_ref, a_ref, b_ref, o_ref):
        o_ref[...] = (a_ref[...].astype(F32) + b_ref[...].astype(F32)).astype(BF16)

    spec = pl.BlockSpec((None, tr, c), lambda q, i, core_ref: (q, i, 0))
    grid_spec = pltpu.PrefetchScalarGridSpec(
        num_scalar_prefetch=1, grid=(4, r // tr),
        in_specs=[pl.BlockSpec((None, tr, c), lambda q, i, core_ref: (2 * q + core_ref[0], i, 0)), spec],
        out_specs=spec)
    return pl.pallas_call(body, name=name, grid_spec=grid_spec, out_shape=jax.ShapeDtypeStruct(theirs.shape, BF16),
                          compiler_params=_params(("parallel", "parallel")))(core, partial, theirs)


def _adamw_step(w, m, v, g):
    nm = ADAM_B1 * m + (1.0 - ADAM_B1) * g
    nv = ADAM_B2 * v + (1.0 - ADAM_B2) * (g * g)
    m_hat = nm / (1.0 - ADAM_B1 ** ADAM_STEP)
    v_hat = nv / (1.0 - ADAM_B2 ** ADAM_STEP)
    return -ADAM_LR * (m_hat / (jnp.sqrt(v_hat) + ADAM_EPS) + ADAM_WD * w), nm, nv


def _adamw(name, w, m, v, addends, tr=None):
    r, c = w.shape
    tr = r if tr is None else tr
    n_add = len(addends)

    def body(*refs):
        w_ref, m_ref, v_ref = refs[:3]
        add_refs = refs[3:3 + n_add]
        g_ref, d_ref, nm_ref, nv_ref = refs[3 + n_add:]
        g = add_refs[0][...].astype(F32)
        for a_ref in add_refs[1:]:
            g = g + a_ref[...].astype(F32)
        g_ref[...] = g
        d_ref[...], nm_ref[...], nv_ref[...] = _adamw_step(w_ref[...], m_ref[...], v_ref[...], g)

    spec = pl.BlockSpec((tr, c), lambda i: (i, 0))
    out = jax.ShapeDtypeStruct((r, c), F32)
    return pl.pallas_call(body, name=name, grid=(r // tr,), in_specs=[spec] * (3 + n_add), out_specs=[spec] * 4,
                          out_shape=[out] * 4, compiler_params=_params(("parallel",)))(w, m, v, *addends)


def _adamw_reduced(name, w, m, v, chip_sums, received, chip, tr):
    r, c = w.shape

    def body(chip_ref, w_ref, m_ref, v_ref, own_ref, r0_ref, r1_ref, r2_ref, g_ref, d_ref, nm_ref, nv_ref):
        g = ((own_ref[...].astype(F32) + r0_ref[...].astype(F32)) + r1_ref[...].astype(F32)) + r2_ref[...].astype(F32)
        g_ref[...] = g
        d_ref[...], nm_ref[...], nv_ref[...] = _adamw_step(w_ref[...], m_ref[...], v_ref[...], g)

    spec = pl.BlockSpec((tr, c), lambda i, chip_ref: (i, 0))

    def slot(k):
        return pl.BlockSpec((None, tr, c), lambda i, chip_ref: (k, i, 0))

    grid_spec = pltpu.PrefetchScalarGridSpec(
        num_scalar_prefetch=1, grid=(r // tr,),
        in_specs=[spec, spec, spec, pl.BlockSpec((None, tr, c), lambda i, chip_ref: (chip_ref[0], i, 0)),
                  slot(0), slot(1), slot(2)],
        out_specs=[spec] * 4)
    out = jax.ShapeDtypeStruct((r, c), F32)
    return pl.pallas_call(body, name=name, grid_spec=grid_spec, out_shape=[out] * 4,
                          compiler_params=_params(("parallel",)))(chip, w, m, v, chip_sums, received, received, received)


def _sum_devices(gathered):
    _, r, c = gathered.shape

    def body(g_ref, o_ref):
        acc = g_ref[0]
        for d in range(1, N_DEV):
            acc = acc + g_ref[d]
        o_ref[...] = acc

    return pl.pallas_call(body, name="sum_devices", out_shape=jax.ShapeDtypeStruct((r, c), F32))(gathered)


def _pack_rows(vectors, rows):
    flat = jnp.concatenate([v.reshape(-1) for v in vectors])
    return jnp.pad(flat, (0, rows * 128 - flat.shape[0])).reshape(rows, 128)


def _unpack(flat, shapes):
    out, off = [], 0
    for shp in shapes:
        n = 1
        for d in shp:
            n *= d
        out.append(flat[off:off + n].reshape(shp))
        off += n
    return out


def _device_step(xs, tgt, mod, norm1_w, norm2_w, lb_logits, hg_norm_w, q_norm_w, k_norm_w, conv_w_full, conv_b,
                 win_g, w_out_x, w_up_x, w_down_x, core=None):
    fused = core is not None
    shift1, scale1, gate1, shift2, scale2, gate2 = (mod[k] for k in range(6))

    h, rstd1 = _norm_fwd("norm1_fwd", xs, norm1_w, scale1, shift1)
    if fused:
        proj, (wout_g,) = _mm_blocked_rhs("mm_in", h, win_g, fused=_FusedCopies("gather", [w_out_x]),
                                          fused_arrays=[w_out_x])
        (a_out, o_pre), (wup_g,) = _hgrn_fwd(proj, lb_logits, hg_norm_w,
                                             _FusedCopies("gather", [w_up_x], peers=(0, 1, 2)), [w_up_x])
        wout_g, = _forward_to_sibling("allgather_stage2_out", [wout_g])
        wout_full = wout_g.reshape(D_MODEL, D_MODEL)
        qn, kn, vb = _qk_prep(proj, q_norm_w, k_norm_w)
        (att_o, lse), (wup_g,) = _attn_fwd(qn, kn, vb, _FusedCopies("gather_into", [w_up_x, wup_g], peers=(3,)),
                                           [w_up_x, wup_g])
    else:
        proj = _mm_blocked_rhs("mm_in", h, win_g)
        (a_out, o_pre), _ = _hgrn_fwd(proj, lb_logits, hg_norm_w)
        wup_g, wout_full, wdown_full = w_up_x, w_out_x, w_down_x
        qn, kn, vb = _qk_prep(proj, q_norm_w, k_norm_w)
        (att_o, lse), _ = _attn_fwd(qn, kn, vb)
    mixin = jnp.concatenate([a_out, att_o.astype(BF16)], axis=1)
    if fused:
        mix, (wup_g,) = _mm_plain("mm_out", mixin, wout_full, NN, 512, 1024, F32,
                                  fused=_FusedCopies("forward", [wup_g]), fused_arrays=[wup_g])
    else:
        mix = _mm_plain("mm_out", mixin, wout_full, NN, 512, 1024, F32)
    x1, h2, rstd2 = _norm_fwd("norm2_fwd", xs, norm2_w, scale2, shift2, resid=mix, gate=gate1)
    if fused:
        u, (wdown_g,) = _mm_blocked_rhs("mm_up", h2, wup_g, fused=_FusedCopies("gather", [w_down_x]),
                                        fused_arrays=[w_down_x])
        y, (wdown_g,) = _conv_gate_fwd(u, conv_w_full, conv_b, _FusedCopies("forward", [wdown_g]), [wdown_g])
        wdown_full = wdown_g.reshape(D_FF, D_MODEL)
    else:
        u = _mm_blocked_rhs("mm_up", h2, wup_g)
        y = _conv_gate_fwd(u, conv_w_full, conv_b)
    ffn = _mm_plain("mm_down", y, wdown_full, NN, 512, 512, F32)
    loss_v, dout, dffn, dgate2 = _loss_head(x1, ffn, gate2, tgt)

    dy = _mm_plain("mm_down_dx", dffn, wdown_full, NT, 512, UP_BLK, F32)
    gw_down = _mm_plain("mm_down_dw", y, dffn, TN, UP_BLK, 1024, BF16)
    da, dg, gconv_w, gconv_b = _conv_gate_bwd(u, dy, conv_w_full, conv_b)
    du = jnp.concatenate([da, dg], axis=1)
    dh2 = _mm_blocked_rhs_t("mm_up_dx", du, wup_g)
    gw_up = _mm_wgrad_blocked("mm_up_dw", h2, du)
    dx1, dmix, dshift2, dscale2, gnorm2, dgate1 = _norm_bwd(
        "norm2_bwd", dh2, x1, rstd2, norm2_w, scale2, dout, mix=mix, gate=gate1)
    gw_out = _mm_plain("mm_out_dw", mixin, dmix, TN, 512, 1024, BF16)
    if fused:
        partials = [gw_up, gw_out.reshape(N_DEV, OUT_BLK, D_MODEL), gw_down.reshape(N_DEV, FF_BLK, D_MODEL)]
        dmixin, from_sibling = _mm_plain("mm_out_dx", dmix, wout_full, NT, 512, 1024, F32,
                                         fused=_FusedCopies("sibling", partials), fused_arrays=partials)
        cs_up, cs_out, cs_down = [_pair_sum(f"grad_pair_sum_{k}", a, b, core)
                                  for k, (a, b) in enumerate(zip(partials, from_sibling))]
        (dhq, dhf, dhi, dhg, glog, ghg), (fc_up, fc_out) = _hgrn_bwd(
            proj, lb_logits, hg_norm_w, o_pre, dmixin, _FusedCopies("chips", [cs_up, cs_out]), [cs_up, cs_out])
        (dqn, dkn, dvv), (fc_down,) = _attn_bwd(qn, kn, vb, att_o, lse, dmixin,
                                                _FusedCopies("chips", [cs_down]), [cs_down])
    else:
        dmixin = _mm_plain("mm_out_dx", dmix, wout_full, NT, 512, 1024, F32)
        (dhq, dhf, dhi, dhg, glog, ghg), _ = _hgrn_bwd(proj, lb_logits, hg_norm_w, o_pre, dmixin)
        (dqn, dkn, dvv), _ = _attn_bwd(qn, kn, vb, att_o, lse, dmixin)
    daq, dak, dav, gqw, gkw = _qk_bwd(proj, q_norm_w, k_norm_w, dqn, dkn, dvv)
    dproj = jnp.concatenate([dhq, dhf, dhi, dhg, daq, dak, dav], axis=1)
    gw_in = _mm_wgrad_blocked("mm_in_dw", h, dproj)
    if fused:
        from_sibling, = _exchange_sibling("grad_exchange_sibling_b", [gw_in])
        cs_in = _pair_sum("grad_pair_sum_in", gw_in, from_sibling, core)
        dh, (fc_in,) = _mm_blocked_rhs_t("mm_in_dx", dproj, win_g, fused=_FusedCopies("chips", [cs_in]),
                                         fused_arrays=[cs_in])
        large = [(cs_in, fc_in), (cs_out, fc_out), (cs_up, fc_up), (cs_down, fc_down)]
    else:
        dh = _mm_blocked_rhs_t("mm_in_dx", dproj, win_g)
        large = [gw_in, gw_out, gw_up, gw_down]
    grad_x, dshift1, dscale1, gnorm1 = _norm_bwd("norm1_bwd", dh, xs, rstd1, norm1_w, scale1, dx1)
    gmod = jnp.concatenate([dshift1, dscale1, dgate1, dshift2, dscale2, dgate2], axis=1)
    return (loss_v, grad_x, gmod, gnorm1, gnorm2, glog, ghg, gqw, gkw, gconv_b, gconv_w, *large)


def kernel(x, c, w_ada, b_ada, norm1_w, w_in, lb_logits, hg_norm_w, q_norm_w, k_norm_w, w_out, norm2_w, w_up, conv_w, conv_b, w_down, loss_target, m_w_ada, m_b_ada, m_norm1_w, m_w_in, m_lb_logits, m_hg_norm_w, m_q_norm_w, m_k_norm_w, m_w_out, m_norm2_w, m_w_up, m_conv_w, m_conv_b, m_w_down, v_w_ada, v_b_ada, v_norm1_w, v_w_in, v_lb_logits, v_hg_norm_w, v_q_norm_w, v_k_norm_w, v_w_out, v_norm2_w, v_w_up, v_conv_w, v_conv_b, v_w_down):
    ix, iy, ic = lax.axis_index("x"), lax.axis_index("y"), lax.axis_index("c")
    me = 4 * ix + 2 * iy + ic
    my_chip = 2 * ix + iy

    xs = x[0]
    tgt = loss_target[0]

    win_g, = _allgather_weights([w_in[0].astype(BF16)])

    c_all = _allgather_vmem(c.reshape(8, D_MODEL // 8), "allgather_c").reshape(N_DEV, D_MODEL)
    b_blk = lax.dynamic_slice_in_dim(b_ada, me * ADA_BLK, ADA_BLK, axis=1)
    mod_cols = _ada_fwd(c_all, w_ada[0], b_blk)
    mod_all = _allgather_vmem(mod_cols, "allgather_mod").reshape(N_DEV, N_DEV, ADA_BLK)
    mod = lax.dynamic_index_in_dim(mod_all, me, axis=1, keepdims=False).reshape(6, 1, D_MODEL)

    conv_w_all = _allgather_vmem(_pack_rows([conv_w[0]], 24), "allgather_conv_w").reshape(N_DEV, 24 * 128)
    conv_w_full = conv_w_all[:, :3 * FF_BLK].reshape(N_DEV, 3, FF_BLK).transpose(1, 0, 2).reshape(3, D_FF)

    (loss_v, grad_x, gmod, gnorm1, gnorm2, glog, ghg, gqw, gkw, gconv_b, gconv_w,
     rs_in, rs_out, rs_up, rs_down) = _device_step(
        xs, tgt, mod, norm1_w, norm2_w, lb_logits, hg_norm_w, q_norm_w, k_norm_w, conv_w_full, conv_b,
        win_g, w_out[0].astype(BF16), w_up[0].astype(BF16), w_down[0].astype(BF16),
        core=jnp.reshape(ic, (1,)).astype(jnp.int32))
    loss = lax.psum(loss_v[0, 0], AXES)

    small_shapes = [(1, 6 * D_MODEL), (1, D_MODEL), (1, D_MODEL), (2, HEADS * HEAD_DIM), (1, HEAD_DIM),
                    (1, HEAD_DIM), (1, HEAD_DIM), (1, D_FF), (3, D_FF)]
    small = [gmod, gnorm1, gnorm2, glog, ghg, gqw, gkw, gconv_b, gconv_w]
    n_small = sum(a.size for a in small)
    rows = -(-n_small // 1024) * 8
    gathered = _allgather_vmem(_pack_rows(small, rows), "allgather_small").reshape(N_DEV, rows, 128)
    summed = _sum_devices(gathered).reshape(-1)
    (g_b_ada, g_norm1, g_norm2, g_lb, g_hg, g_q, g_k, g_conv_b, g_conv_w_full) = _unpack(summed, small_shapes)
    g_conv_w = lax.dynamic_slice_in_dim(g_conv_w_full, me * FF_BLK, FF_BLK, axis=1)

    gmod_all = gathered[:, :6 * D_MODEL // 128, :].reshape(N_DEV, 6 * D_MODEL)
    gmod_cols = lax.dynamic_slice_in_dim(gmod_all, me * ADA_BLK, ADA_BLK, axis=1)
    g_w_ada_raw = _ada_wgrad(c_all, gmod_cols)

    chip = jnp.reshape(my_chip, (1,)).astype(jnp.int32)

    def big_update(name, w, m, v, rs, tr):
        chip_sums, received = rs
        return _adamw_reduced(name, w[0], m[0], v[0], chip_sums, received, chip, tr)

    r_in = big_update("adamw_w_in", w_in, m_w_in, v_w_in, rs_in, 256)
    r_out = big_update("adamw_w_out", w_out, m_w_out, v_w_out, rs_out, 128)
    r_up = big_update("adamw_w_up", w_up, m_w_up, v_w_up, rs_up, 256)
    r_down = big_update("adamw_w_down", w_down, m_w_down, v_w_down, rs_down, 176)
    r_ada = _adamw("adamw_w_ada", w_ada[0], m_w_ada[0], v_w_ada[0], [g_w_ada_raw], tr=256)
    r_convw = _adamw("adamw_conv_w", conv_w[0], m_conv_w[0], v_conv_w[0], [g_conv_w])

    rep_shapes = [(1, 6 * D_MODEL), (1, D_MODEL), (1, D_MODEL), (2, HEADS * HEAD_DIM), (1, HEAD_DIM),
                  (1, HEAD_DIM), (1, HEAD_DIM), (1, D_FF)]
    rep_rows = -(-sum(a * b for a, b in rep_shapes) // 1024) * 8
    pack = lambda arrs: _pack_rows(arrs, rep_rows)
    rep = _adamw("adamw_small",
                 pack([b_ada, norm1_w, norm2_w, lb_logits, hg_norm_w, q_norm_w, k_norm_w, conv_b]),
                 pack([m_b_ada, m_norm1_w, m_norm2_w, m_lb_logits, m_hg_norm_w, m_q_norm_w, m_k_norm_w, m_conv_b]),
                 pack([v_b_ada, v_norm1_w, v_norm2_w, v_lb_logits, v_hg_norm_w, v_q_norm_w, v_k_norm_w, v_conv_b]),
                 [pack([g_b_ada, g_norm1, g_norm2, g_lb, g_hg, g_q, g_k, g_conv_b])])
    rep = [_unpack(r.reshape(-1), rep_shapes) for r in rep]

    def big(r):
        return [a[None] for a in r]

    order = {"w_ada": big(r_ada), "b_ada": [r[0] for r in rep], "norm1_w": [r[1] for r in rep],
             "w_in": big(r_in), "lb_logits": [r[3] for r in rep], "hg_norm_w": [r[4] for r in rep],
             "q_norm_w": [r[5] for r in rep], "k_norm_w": [r[6] for r in rep], "w_out": big(r_out),
             "norm2_w": [r[2] for r in rep], "w_up": big(r_up), "conv_w": big(r_convw),
             "conv_b": [r[7] for r in rep], "w_down": big(r_down)}
    names = ["w_ada", "b_ada", "norm1_w", "w_in", "lb_logits", "hg_norm_w", "q_norm_w", "k_norm_w", "w_out",
             "norm2_w", "w_up", "conv_w", "conv_b", "w_down"]
    outs = [loss, grad_x[None]]
    for kind in range(4):
        outs += [order[n][kind] for n in names]
    return tuple(outs)
```

```python
import functools

import jax
import jax.numpy as jnp
from jax import lax
from jax.experimental import pallas as pl
from jax.experimental.pallas import tpu as pltpu

F32 = jnp.float32
BF16 = jnp.bfloat16

N_DEV = 8
SEQ = 2048
D_MODEL = 2048
HEADS = 8
HEAD_DIM = 128
IN_COLS = 7168
IN_BLK = IN_COLS // N_DEV
D_FF = 5632
UP_BLK = 2 * D_FF // N_DEV
FF_BLK = D_FF // N_DEV
ADA_BLK = 6 * D_MODEL // N_DEV
OUT_BLK = D_MODEL // N_DEV
EPS = 1e-6
CHUNK = 16
ROW_TILE = 256
V7X_VMEM_LIMIT = 56 * 1024 * 1024

ADAM_LR = 0.001
ADAM_B1 = 0.9
ADAM_B2 = 0.999
ADAM_EPS = 1e-08
ADAM_WD = 0.01
ADAM_STEP = 10

NN = (((1,), (0,)), ((), ()))
NT = (((1,), (1,)), ((), ()))
TN = (((0,), (0,)), ((), ()))
MESH = pl.DeviceIdType.MESH
AXES = ("x", "y", "c")


def _params(sem=None, vmem=V7X_VMEM_LIMIT):
    return pltpu.CompilerParams(dimension_semantics=sem, vmem_limit_bytes=vmem)


def _sigmoid(x):
    return 1.0 / (1.0 + jnp.exp(-x))


def _dsilu(x, s):
    return s * (1.0 + x * (1.0 - s))


def _lane_sum(x, ones_bf16):
    hi = x.astype(BF16)
    lo = (x - hi.astype(F32)).astype(BF16)
    return (jnp.dot(hi, ones_bf16, preferred_element_type=F32)
            + jnp.dot(lo, ones_bf16, preferred_element_type=F32))


def _mesh_pos():
    return lax.axis_index("x"), lax.axis_index("y"), lax.axis_index("c")


def _allgather_vmem(x_blk, name):
    m_per, n = x_blk.shape

    def body(x_ref, out_ref, send_sems, recv_sems, local_sem):
        x, y, c = _mesh_pos()
        me, sibling = (x, y, c), (x, y, 1 - c)
        chips = [(1 - x, y), (x, 1 - y), (1 - x, 1 - y)]

        def rows(px, py, pc):
            return out_ref.at[pl.ds((4 * px + 2 * py + pc) * m_per, m_per), :]

        def copy(k, block, to, src=None):
            return pltpu.make_async_remote_copy(
                src_ref=rows(*block) if src is None else src, dst_ref=rows(*block),
                send_sem=send_sems.at[k], recv_sem=recv_sems.at[k], device_id=to, device_id_type=MESH)

        mine = pltpu.make_async_copy(x_ref, rows(*me), local_sem)
        mine.start()
        first = [copy(0, me, sibling, src=x_ref)]
        first += [copy(1 + j, me, (*chip, c), src=x_ref) for j, chip in enumerate(chips)]
        for cp in first:
            cp.start()
        passed = [copy(4 + j, (*chip, c), sibling) for j, chip in enumerate(chips)]
        for j, chip in enumerate(chips):
            copy(1 + j, (*chip, c), me).wait_recv()
            passed[j].start()
        copy(0, sibling, me).wait_recv()
        for j, chip in enumerate(chips):
            copy(4 + j, (*chip, 1 - c), me).wait_recv()
        for cp in first + passed:
            cp.wait_send()
        mine.wait()

    return pl.pallas_call(
        body, name=name,
        out_shape=jax.ShapeDtypeStruct((N_DEV * m_per, n), x_blk.dtype),
        in_specs=[pl.BlockSpec(memory_space=pltpu.VMEM)],
        out_specs=pl.BlockSpec(memory_space=pltpu.VMEM),
        scratch_shapes=[pltpu.SemaphoreType.DMA((7,)), pltpu.SemaphoreType.DMA((7,)), pltpu.SemaphoreType.DMA],
    )(x_blk)


def _flip(v, bit):
    return v + bit - 2 * v * bit


def _relay_chips(x, y, c):
    return (_flip(x, 1 - c), _flip(y, c)), (_flip(x, c), _flip(y, 1 - c))


GATHER_PARTS = 2


def _allgather_weights(blocks):
    n_arr = len(blocks)
    parts = GATHER_PARTS

    def body(*refs):
        ins, outs = refs[:n_arr], refs[n_arr:2 * n_arr]
        send_sems, recv_sems, local_sems = refs[2 * n_arr:]
        x, y, c = _mesh_pos()
        me, sibling = (x, y, c), (x, y, 1 - c)
        near = [(1 - x, y), (x, 1 - y)]
        chips = near + [(1 - x, 1 - y)]
        relay_from, relay_to = _relay_chips(x, y, c)

        def rows(a, p):
            hr = ins[a].shape[0] // parts
            return pl.ds(p * hr, hr)

        def slot(a, pos, p):
            return outs[a].at[4 * pos[0] + 2 * pos[1] + pos[2], rows(a, p)]

        def copy(a, k, p, src, lands, to):
            return pltpu.make_async_remote_copy(
                src_ref=src, dst_ref=slot(a, lands, p), send_sem=send_sems.at[a, k, p], recv_sem=recv_sems.at[a, k, p],
                device_id=to, device_id_type=MESH)

        sent = []
        local = [pltpu.make_async_copy(ins[a], outs[a].at[4 * x + 2 * y + c], local_sems.at[a]) for a in range(n_arr)]
        for cp in local:
            cp.start()
        for p in range(parts):
            for a in range(n_arr):
                own = ins[a].at[rows(a, p)]
                sent.append(copy(a, 0, p, own, me, sibling))
                sent += [copy(a, 1 + j, p, own, me, (*chip, c)) for j, chip in enumerate(near)]
        for cp in sent:
            cp.start()

        def start(cp):
            cp.start()
            sent.append(cp)

        for p in range(parts):
            for a in range(n_arr):
                for j, chip in enumerate(near):
                    copy(a, 1 + j, p, ins[a].at[rows(a, p)], (*chip, c), me).wait_recv()
                    start(copy(a, 4 + j, p, slot(a, (*chip, c), p), (*chip, c), sibling))
                start(copy(a, 3, p, slot(a, (*relay_from, c), p), (*relay_from, c), (*relay_to, c)))
        for p in range(parts):
            for a in range(n_arr):
                copy(a, 3, p, ins[a].at[rows(a, p)], (*chips[2], c), me).wait_recv()
                start(copy(a, 6, p, slot(a, (*chips[2], c), p), (*chips[2], c), sibling))
        for p in range(parts):
            for a in range(n_arr):
                copy(a, 0, p, ins[a].at[rows(a, p)], sibling, me).wait_recv()
                for j, chip in enumerate(chips):
                    copy(a, 4 + j, p, ins[a].at[rows(a, p)], (*chip, 1 - c), me).wait_recv()
        for cp in sent:
            cp.wait_send()
        for cp in local:
            cp.wait()

    return pl.pallas_call(
        body, name="allgather_weights",
        out_shape=[jax.ShapeDtypeStruct((N_DEV,) + b.shape, b.dtype) for b in blocks],
        in_specs=[pl.BlockSpec(memory_space=pltpu.HBM)] * n_arr, out_specs=[pl.BlockSpec(memory_space=pltpu.HBM)] * n_arr,
        scratch_shapes=[pltpu.SemaphoreType.DMA((n_arr, 7, parts)), pltpu.SemaphoreType.DMA((n_arr, 7, parts)),
                        pltpu.SemaphoreType.DMA((n_arr,))],
    )(*blocks)


HBM_SPEC = pl.BlockSpec(memory_space=pltpu.HBM)


class _FusedCopies:
    def __init__(self, kind, arrays, peers=(0, 1, 2, 3)):
        self.kind = kind
        self.peers = peers
        n = len(arrays)
        self.n = n
        self.n_in = n
        self.aliases = {}
        if kind == "gather":
            self.out_shape = [jax.ShapeDtypeStruct((N_DEV,) + a.shape, a.dtype) for a in arrays]
            self.scratch_shapes = [pltpu.SemaphoreType.DMA((n, 4)), pltpu.SemaphoreType.DMA((n, 4)),
                                   pltpu.SemaphoreType.DMA((n,))]
        elif kind == "relay":
            self.out_shape = [jax.ShapeDtypeStruct(a.shape, a.dtype) for a in arrays]
            self.scratch_shapes = [pltpu.SemaphoreType.DMA((n,)), pltpu.SemaphoreType.DMA((n,))]
            self.aliases = {a: a for a in range(n)}
        elif kind == "forward":
            self.out_shape = [jax.ShapeDtypeStruct(a.shape, a.dtype) for a in arrays]
            self.scratch_shapes = [pltpu.SemaphoreType.DMA((n, 3)), pltpu.SemaphoreType.DMA((n, 3))]
            self.aliases = {a: a for a in range(n)}
        elif kind == "sibling":
            self.out_shape = [jax.ShapeDtypeStruct((4,) + a.shape[1:], a.dtype) for a in arrays]
            self.scratch_shapes = [pltpu.SemaphoreType.DMA((n, 4)), pltpu.SemaphoreType.DMA((n, 4))]
        else:
            self.out_shape = [jax.ShapeDtypeStruct((3,) + a.shape[1:], a.dtype) for a in arrays]
            self.scratch_shapes = [pltpu.SemaphoreType.DMA((n, 3)), pltpu.SemaphoreType.DMA((n, 3))]
        self.in_specs = [HBM_SPEC] * self.n_in
        self.out_specs = [HBM_SPEC] * n
        self.n_scratch = len(self.scratch_shapes)

    def copies(self, ins, outs, sems):
        x, y, c = _mesh_pos()
        chips = [(1 - x, y), (x, 1 - y), (1 - x, 1 - y)]
        sibling = (x, y, 1 - c)
        starts, waits = [], []
        if self.kind == "gather":
            send_sems, recv_sems, local_sems = sems
            me = (x, y, c)
            peers = [sibling] + [(px, py, c) for px, py in chips]

            def slot(a, pos):
                return outs[a].at[4 * pos[0] + 2 * pos[1] + pos[2]]

            def remote(a, k, lands_from):
                return pltpu.make_async_remote_copy(
                    src_ref=ins[a], dst_ref=slot(a, lands_from), send_sem=send_sems.at[a, k],
                    recv_sem=recv_sems.at[a, k], device_id=peers[k], device_id_type=MESH)

            for a in range(self.n):
                local = pltpu.make_async_copy(ins[a], slot(a, me), local_sems.at[a])
                starts.append(local)
                waits.append(local)
                for k in self.peers:
                    starts.append(remote(a, k, me))
                    waits.append(remote(a, k, peers[k]))
        elif self.kind == "relay":
            send_sems, recv_sems = sems
            relay_from, relay_to = _relay_chips(x, y, c)

            def relayed(a, lands):
                return pltpu.make_async_remote_copy(
                    src_ref=ins[a].at[4 * relay_from[0] + 2 * relay_from[1] + c],
                    dst_ref=outs[a].at[4 * lands[0] + 2 * lands[1] + c], send_sem=send_sems.at[a],
                    recv_sem=recv_sems.at[a], device_id=(*relay_to, c), device_id_type=MESH)

            for a in range(self.n):
                starts.append(relayed(a, relay_from))
                waits.append(relayed(a, chips[2]))
        elif self.kind == "forward":
            send_sems, recv_sems = sems

            def passed_on(a, j, pc_src, pc_dst):
                px, py = chips[j]
                return pltpu.make_async_remote_copy(
                    src_ref=ins[a].at[4 * px + 2 * py + pc_src], dst_ref=outs[a].at[4 * px + 2 * py + pc_dst],
                    send_sem=send_sems.at[a, j], recv_sem=recv_sems.at[a, j], device_id=sibling, device_id_type=MESH)

            for a in range(self.n):
                for j in range(3):
                    starts.append(passed_on(a, j, c, c))
                    waits.append(passed_on(a, j, c, 1 - c))
        elif self.kind == "sibling":
            send_sems, recv_sems = sems
            for a in range(self.n):
                for q in range(4):
                    cp = pltpu.make_async_remote_copy(
                        src_ref=ins[a].at[2 * q + 1 - c], dst_ref=outs[a].at[q], send_sem=send_sems.at[a, q],
                        recv_sem=recv_sems.at[a, q], device_id=sibling, device_id_type=MESH)
                    starts.append(cp)
                    waits.append(cp)
        else:
            send_sems, recv_sems = sems
            for a in range(self.n):
                for j, (px, py) in enumerate(chips):
                    cp = pltpu.make_async_remote_copy(
                        src_ref=ins[a].at[2 * px + py], dst_ref=outs[a].at[j], send_sem=send_sems.at[a, j],
                        recv_sem=recv_sems.at[a, j], device_id=(px, py, c), device_id_type=MESH)
                    starts.append(cp)
                    waits.append(cp)
        return starts, waits


def _host_body(body, n_in, n_out, fused, first_last):
    if fused is None:
        return body
    n_fin, n_fout = fused.n_in, fused.n

    def wrapped(*refs):
        core_in, f_in = refs[:n_in], refs[n_in:n_in + n_fin]
        core_out = refs[n_in + n_fin:n_in + n_fin + n_out]
        f_out = refs[n_in + n_fin + n_out:n_in + n_fin + n_out + n_fout]
        rest = refs[n_in + n_fin + n_out + n_fout:]
        core_scratch, f_sems = rest[:len(rest) - fused.n_scratch], rest[len(rest) - fused.n_scratch:]
        starts, waits = fused.copies(f_in, f_out, f_sems)
        first, last = first_last()

        @pl.when(first)
        def _():
            for cp in starts:
                cp.start()

        body(*core_in, *core_out, *core_scratch)

        @pl.when(last)
        def _():
            for cp in waits:
                cp.wait()

    return wrapped


def _host_call(body, n_in, n_out, fused, first_last, *, name, grid, in_specs, out_specs, out_shape, scratch_shapes,
               sem, operands):
    aliases = {}
    if fused is not None:
        in_specs = list(in_specs) + fused.in_specs
        out_specs = list(out_specs) + fused.out_specs
        out_shape = list(out_shape) + fused.out_shape
        scratch_shapes = list(scratch_shapes) + fused.scratch_shapes
        sem = tuple("arbitrary" for _ in sem)
        aliases = {n_in + fi: n_out + fo for fi, fo in fused.aliases.items()}
    res = pl.pallas_call(_host_body(body, n_in, n_out, fused, first_last), name=name, grid=grid, in_specs=in_specs,
                         out_specs=out_specs, out_shape=out_shape, scratch_shapes=scratch_shapes,
                         input_output_aliases=aliases, compiler_params=_params(sem))(*operands)
    return list(res[:n_out]), list(res[n_out:])


def _forward_to_sibling(name, gathered):
    n_arr = len(gathered)

    def body(*refs):
        ins, outs = refs[:n_arr], refs[n_arr:2 * n_arr]
        send_sems, recv_sems = refs[2 * n_arr:]
        x, y, c = _mesh_pos()
        chips = [(1 - x, y), (x, 1 - y), (1 - x, 1 - y)]

        def copy(a, j, pc):
            px, py = chips[j]
            s = 4 * px + 2 * py + pc
            return pltpu.make_async_remote_copy(
                src_ref=ins[a].at[s], dst_ref=outs[a].at[s], send_sem=send_sems.at[a, j], recv_sem=recv_sems.at[a, j],
                device_id=(x, y, 1 - c), device_id_type=MESH)

        for a in range(n_arr):
            for j in range(3):
                copy(a, j, c).start()
        for a in range(n_arr):
            for j in range(3):
                copy(a, j, 1 - c).wait_recv()
                copy(a, j, c).wait_send()

    return pl.pallas_call(
        body, name=name,
        out_shape=[jax.ShapeDtypeStruct(g.shape, g.dtype) for g in gathered],
        in_specs=[HBM_SPEC] * n_arr, out_specs=[HBM_SPEC] * n_arr,
        input_output_aliases={a: a for a in range(n_arr)},
        scratch_shapes=[pltpu.SemaphoreType.DMA((n_arr, 3)), pltpu.SemaphoreType.DMA((n_arr, 3))],
    )(*gathered)


def _exchange_sibling(name, partials):
    n_arr = len(partials)

    def body(*refs):
        ins, outs = refs[:n_arr], refs[n_arr:2 * n_arr]
        send_sems, recv_sems = refs[2 * n_arr:]
        x, y, c = _mesh_pos()
        copies = [pltpu.make_async_remote_copy(
            src_ref=ins[a].at[2 * q + 1 - c], dst_ref=outs[a].at[q], send_sem=send_sems.at[a, q],
            recv_sem=recv_sems.at[a, q], device_id=(x, y, 1 - c), device_id_type=MESH)
            for a in range(n_arr) for q in range(4)]
        for cp in copies:
            cp.start()
        for cp in copies:
            cp.wait_recv()
        for cp in copies:
            cp.wait_send()

    return pl.pallas_call(
        body, name=name,
        out_shape=[jax.ShapeDtypeStruct((4,) + p.shape[1:], p.dtype) for p in partials],
        in_specs=[HBM_SPEC] * n_arr, out_specs=[HBM_SPEC] * n_arr,
        scratch_shapes=[pltpu.SemaphoreType.DMA((n_arr, 4)), pltpu.SemaphoreType.DMA((n_arr, 4))],
    )(*partials)


def _exchange_chips(name, chip_sums):
    n_arr = len(chip_sums)

    def body(*refs):
        ins, outs = refs[:n_arr], refs[n_arr:2 * n_arr]
        send_sems, recv_sems = refs[2 * n_arr:]
        x, y, c = _mesh_pos()
        chips = [(1 - x, y), (x, 1 - y), (1 - x, 1 - y)]
        copies = []
        for a in range(n_arr):
            for j, (px, py) in enumerate(chips):
                copies.append(pltpu.make_async_remote_copy(
                    src_ref=ins[a].at[2 * px + py], dst_ref=outs[a].at[j],
                    send_sem=send_sems.at[a, j], recv_sem=recv_sems.at[a, j],
                    device_id=(px, py, c), device_id_type=MESH))
        for cp in copies:
            cp.start()
        for cp in copies:
            cp.wait_recv()
        for cp in copies:
            cp.wait_send()

    hbm = pl.BlockSpec(memory_space=pltpu.HBM)
    return pl.pallas_call(
        body, name=name,
        out_shape=[jax.ShapeDtypeStruct((3,) + p.shape[1:], p.dtype) for p in chip_sums],
        in_specs=[hbm] * n_arr, out_specs=[hbm] * n_arr,
        scratch_shapes=[pltpu.SemaphoreType.DMA((n_arr, 3)), pltpu.SemaphoreType.DMA((n_arr, 3))],
    )(*chip_sums)


def _matmul(name, a, b, dims, grid, a_spec, b_spec, o_spec, out_shape, acc_axis=None, fused=None, fused_arrays=()):
    def body(a_ref, b_ref, o_ref):
        r = lax.dot_general(a_ref[...], b_ref[...], dims, preferred_element_type=F32)
        if acc_axis is None:
            o_ref[...] = r.astype(o_ref.dtype)
        else:
            k = pl.program_id(acc_axis)

            @pl.when(k == 0)
            def _():
                o_ref[...] = r

            @pl.when(k > 0)
            def _():
                o_ref[...] += r

    sem = tuple("arbitrary" if i == acc_axis else "parallel" for i in range(len(grid)))
    if fused is None:
        return pl.pallas_call(body, name=name, grid=grid, in_specs=[a_spec, b_spec], out_specs=o_spec,
                              out_shape=out_shape, compiler_params=_params(sem))(a, b)

    def first_last():
        first = last = None
        for ax, n in enumerate(grid):
            f, l = pl.program_id(ax) == 0, pl.program_id(ax) == n - 1
            first, last = (f, l) if first is None else (first & f, last & l)
        return first, last

    (out,), extra = _host_call(body, 2, 1, fused, first_last, name=name, grid=grid, in_specs=[a_spec, b_spec],
                               out_specs=[o_spec], out_shape=[out_shape], scratch_shapes=[], sem=sem,
                               operands=[a, b] + list(fused_arrays))
    return out, extra


def _mm_blocked_rhs(name, a, w_g, tm=512, fused=None, fused_arrays=()):
    m, k = a.shape
    nb = w_g.shape[2]
    return _matmul(name, a, w_g, NN, (N_DEV, m // tm),
                   pl.BlockSpec((tm, k), lambda j, i: (i, 0)),
                   pl.BlockSpec((None, k, nb), lambda j, i: (j, 0, 0)),
                   pl.BlockSpec((tm, nb), lambda j, i: (i, j)),
                   jax.ShapeDtypeStruct((m, N_DEV * nb), F32), fused=fused, fused_arrays=fused_arrays)


def _mm_blocked_rhs_t(name, a, w_g, tm=512, fused=None, fused_arrays=()):
    m = a.shape[0]
    n, nb = w_g.shape[1], w_g.shape[2]
    return _matmul(name, a, w_g, NT, (m // tm, N_DEV),
                   pl.BlockSpec((tm, nb), lambda i, j: (i, j)),
                   pl.BlockSpec((None, n, nb), lambda i, j: (j, 0, 0)),
                   pl.BlockSpec((tm, n), lambda i, j: (i, 0)),
                   jax.ShapeDtypeStruct((m, n), F32), acc_axis=1, fused=fused, fused_arrays=fused_arrays)


def _mm_wgrad_blocked(name, act, dcols, tk=512):
    t, k = act.shape
    nb = dcols.shape[1] // N_DEV
    return _matmul(name, act, dcols, TN, (N_DEV, k // tk),
                   pl.BlockSpec((t, tk), lambda j, i: (0, i)),
                   pl.BlockSpec((t, nb), lambda j, i: (0, j)),
                   pl.BlockSpec((None, tk, nb), lambda j, i: (j, i, 0)),
                   jax.ShapeDtypeStruct((N_DEV, k, nb), BF16))


def _mm_plain(name, a, b, dims, tm, tn, out_dtype, fused=None, fused_arrays=()):
    if dims == NN:
        (m, k), n = a.shape, b.shape[1]
        a_spec = pl.BlockSpec((tm, k), lambda i, j: (i, 0))
        b_spec = pl.BlockSpec((k, tn), lambda i, j: (0, j))
    elif dims == NT:
        (m, k), n = a.shape, b.shape[0]
        a_spec = pl.BlockSpec((tm, k), lambda i, j: (i, 0))
        b_spec = pl.BlockSpec((tn, k), lambda i, j: (j, 0))
    else:
        (k, m), n = a.shape, b.shape[1]
        a_spec = pl.BlockSpec((k, tm), lambda i, j: (0, i))
        b_spec = pl.BlockSpec((k, tn), lambda i, j: (0, j))
    return _matmul(name, a, b, dims, (m // tm, n // tn), a_spec, b_spec,
                   pl.BlockSpec((tm, tn), lambda i, j: (i, j)), jax.ShapeDtypeStruct((m, n), out_dtype),
                   fused=fused, fused_arrays=fused_arrays)


def _ada_fwd(c_all, w_ada_blk, b_blk):
    def body(c_ref, w_ref, b_ref, o_ref):
        cv = c_ref[...]
        o_ref[...] = jnp.dot(cv * _sigmoid(cv), w_ref[...], preferred_element_type=F32) + b_ref[...]

    tn = 512
    return pl.pallas_call(
        body, name="ada_fwd", grid=(ADA_BLK // tn,),
        in_specs=[pl.BlockSpec((N_DEV, D_MODEL), lambda j: (0, 0)),
                  pl.BlockSpec((D_MODEL, tn), lambda j: (0, j)),
                  pl.BlockSpec((1, tn), lambda j: (0, j))],
        out_specs=pl.BlockSpec((N_DEV, tn), lambda j: (0, j)),
        out_shape=jax.ShapeDtypeStruct((N_DEV, ADA_BLK), F32),
        compiler_params=_params(("parallel",)))(c_all, w_ada_blk, b_blk)


def _ada_wgrad(c_all, gmod_cols):
    def body(c_ref, g_ref, o_ref):
        cv = c_ref[...]
        o_ref[...] = lax.dot_general(cv * _sigmoid(cv), g_ref[...], TN, preferred_element_type=F32)

    tk = 512
    return pl.pallas_call(
        body, name="ada_wgrad", grid=(D_MODEL // tk,),
        in_specs=[pl.BlockSpec((N_DEV, tk), lambda i: (0, i)),
                  pl.BlockSpec((N_DEV, ADA_BLK), lambda i: (0, 0))],
        out_specs=pl.BlockSpec((tk, ADA_BLK), lambda i: (i, 0)),
        out_shape=jax.ShapeDtypeStruct((D_MODEL, ADA_BLK), F32),
        compiler_params=_params(("parallel",)))(c_all, gmod_cols)


def _row_spec(cols=D_MODEL):
    return pl.BlockSpec((ROW_TILE, cols), lambda i: (i, 0))


def _vec_spec(cols=D_MODEL):
    return pl.BlockSpec((1, cols), lambda i: (0, 0))


def _norm_fwd(name, x, w, scale, shift, resid=None, gate=None):
    has_res = resid is not None

    def body(*refs):
        if has_res:
            x_ref, r_ref, g_ref, w_ref, sc_ref, sh_ref, xr_ref, h_ref, rs_ref = refs
            xr = x_ref[...] + g_ref[...] * r_ref[...]
            xr_ref[...] = xr
        else:
            x_ref, w_ref, sc_ref, sh_ref, h_ref, rs_ref = refs
            xr = x_ref[...]
        rs = lax.rsqrt(jnp.mean(xr * xr, axis=-1, keepdims=True) + EPS)
        h = (xr * rs) * w_ref[...] * (1.0 + sc_ref[...]) + sh_ref[...]
        h_ref[...] = h.astype(BF16)
        rs_ref[...] = rs

    s = x.shape[0]
    ins = [x] + ([resid, gate] if has_res else []) + [w, scale, shift]
    in_specs = [_row_spec()] + ([_row_spec(), _vec_spec()] if has_res else []) + [_vec_spec()] * 3
    outs = ([jax.ShapeDtypeStruct((s, D_MODEL), F32)] if has_res else []) + [
        jax.ShapeDtypeStruct((s, D_MODEL), BF16), jax.ShapeDtypeStruct((s, 1), F32)]
    out_specs = ([_row_spec()] if has_res else []) + [_row_spec(), pl.BlockSpec((ROW_TILE, 1), lambda i: (i, 0))]
    return pl.pallas_call(body, name=name, grid=(s // ROW_TILE,), in_specs=in_specs, out_specs=out_specs,
                          out_shape=outs, compiler_params=_params(("parallel",)))(*ins)


def _norm_bwd(name, dh, x, rstd, w, scale, dres, mix=None, gate=None, fused=None, fused_arrays=()):
    has_mix = mix is not None

    def body(*refs):
        if has_mix:
            (dh_ref, x_ref, rs_ref, w_ref, sc_ref, dr_ref, mix_ref, g_ref,
             dx_ref, dmix_ref, dsh_ref, dsc_ref, dw_ref, dg_ref) = refs
        else:
            dh_ref, x_ref, rs_ref, w_ref, sc_ref, dr_ref, dx_ref, dsh_ref, dsc_ref, dw_ref = refs
        i = pl.program_id(0)
        dhv = dh_ref[...]
        rs = rs_ref[...]
        xn = x_ref[...] * rs
        wv = w_ref[...]
        one_sc = 1.0 + sc_ref[...]
        dxn = dhv * wv * one_sc
        dx = dr_ref[...] + rs * (dxn - xn * jnp.mean(dxn * xn, axis=-1, keepdims=True))
        dx_ref[...] = dx
        sums = [(dsh_ref, dhv), (dsc_ref, dhv * xn * wv), (dw_ref, dhv * one_sc * xn)]
        if has_mix:
            dmix_ref[...] = (dx * g_ref[...]).astype(BF16)
            sums.append((dg_ref, dx * mix_ref[...]))

        @pl.when(i == 0)
        def _():
            for ref, _v in sums:
                ref[...] = jnp.zeros_like(ref)

        for ref, v in sums:
            ref[...] += jnp.sum(v, axis=0, keepdims=True)

    s = x.shape[0]
    ins = [dh, x, rstd, w, scale, dres] + ([mix, gate] if has_mix else [])
    in_specs = ([_row_spec(), _row_spec(), pl.BlockSpec((ROW_TILE, 1), lambda i: (i, 0)), _vec_spec(), _vec_spec(),
                 _row_spec()] + ([_row_spec(), _vec_spec()] if has_mix else []))
    vec = jax.ShapeDtypeStruct((1, D_MODEL), F32)
    outs = ([jax.ShapeDtypeStruct((s, D_MODEL), F32)] + ([jax.ShapeDtypeStruct((s, D_MODEL), BF16)] if has_mix else [])
            + [vec] * (4 if has_mix else 3))
    out_specs = [_row_spec()] + ([_row_spec()] if has_mix else []) + [_vec_spec()] * (4 if has_mix else 3)

    def first_last():
        i = pl.program_id(0)
        return i == 0, i == s // ROW_TILE - 1

    res, extra = _host_call(body, len(ins), len(outs), fused, first_last, name=name, grid=(s // ROW_TILE,),
                            in_specs=in_specs, out_specs=out_specs, out_shape=outs, scratch_shapes=[],
                            sem=("arbitrary",), operands=ins + list(fused_arrays))
    return res if fused is None else (res, extra)


def _loss_head(x1, ffn, gate2, target):
    def body(x_ref, f_ref, g_ref, t_ref, loss_ref, dout_ref, dffn_ref, dg_ref):
        i = pl.program_id(0)
        fv = f_ref[...]
        gv = g_ref[...]
        err = x_ref[...] + gv * fv - t_ref[...]
        dout = err * (1.0 / D_MODEL)
        dout_ref[...] = dout
        dffn_ref[...] = (dout * gv).astype(BF16)

        @pl.when(i == 0)
        def _():
            loss_ref[...] = jnp.zeros_like(loss_ref)
            dg_ref[...] = jnp.zeros_like(dg_ref)

        row = jnp.sum(err * err, axis=-1, keepdims=True) * (1.0 / D_MODEL)
        loss_ref[...] += jnp.broadcast_to(0.5 * jnp.sum(row, axis=0, keepdims=True), (1, 128))
        dg_ref[...] += jnp.sum(dout * fv, axis=0, keepdims=True)

    s = x1.shape[0]
    return pl.pallas_call(
        body, name="loss_head", grid=(s // ROW_TILE,),
        in_specs=[_row_spec(), _row_spec(), _vec_spec(), _row_spec()],
        out_specs=[pl.BlockSpec((1, 128), lambda i: (0, 0)), _row_spec(), _row_spec(), _vec_spec()],
        out_shape=[jax.ShapeDtypeStruct((1, 128), F32), jax.ShapeDtypeStruct((s, D_MODEL), F32),
                   jax.ShapeDtypeStruct((s, D_MODEL), BF16), jax.ShapeDtypeStruct((1, D_MODEL), F32)],
        compiler_params=_params(("arbitrary",)))(x1, ffn, gate2, target)


CONV_TILE = 512
N_CONV_TILES = D_FF // CONV_TILE


def _shift_rows(a, k, row):
    n = a.shape[0]
    if k > 0:
        return jnp.where(row >= k, pltpu.roll(a, k, 0), 0.0)
    return jnp.where(row < n + k, pltpu.roll(a, n + k, 0), 0.0)


def _conv_gate_fwd(u, conv_w, conv_b, fused=None, fused_arrays=()):
    s = u.shape[0]

    def body(a_ref, g_ref, w_ref, b_ref, y_ref):
        a = a_ref[...]
        w = w_ref[...]
        row = lax.broadcasted_iota(jnp.int32, a.shape, 0)
        ac = b_ref[...] + _shift_rows(a, 2, row) * w[0:1] + _shift_rows(a, 1, row) * w[1:2] + a * w[2:3]
        y_ref[...] = (ac * _sigmoid(ac) * g_ref[...]).astype(BF16)

    def first_last():
        i = pl.program_id(0)
        return i == 0, i == N_CONV_TILES - 1

    col = lambda off: pl.BlockSpec((s, CONV_TILE), lambda i: (0, i + off))
    (y,), extra = _host_call(
        body, 4, 1, fused, first_last, name="conv_gate_fwd", grid=(N_CONV_TILES,),
        in_specs=[col(0), col(N_CONV_TILES), pl.BlockSpec((3, CONV_TILE), lambda i: (0, i)),
                  pl.BlockSpec((1, CONV_TILE), lambda i: (0, i))],
        out_specs=[col(0)], out_shape=[jax.ShapeDtypeStruct((s, D_FF), BF16)], scratch_shapes=[], sem=("parallel",),
        operands=[u, u, conv_w, conv_b] + list(fused_arrays))
    return y if fused is None else (y, extra)


def _conv_gate_bwd(u, dy, conv_w, conv_b):
    s = u.shape[0]

    def body(a_ref, g_ref, dy_ref, w_ref, b_ref, da_ref, dg_ref, gw_ref, gb_ref):
        a = a_ref[...]
        w = w_ref[...]
        row = lax.broadcasted_iota(jnp.int32, a.shape, 0)
        a1 = _shift_rows(a, 1, row)
        a2 = _shift_rows(a, 2, row)
        ac = b_ref[...] + a2 * w[0:1] + a1 * w[1:2] + a * w[2:3]
        sg = _sigmoid(ac)
        dyv = dy_ref[...]
        dg_ref[...] = (dyv * (ac * sg)).astype(BF16)
        dac = dyv * g_ref[...] * _dsilu(ac, sg)
        gb_ref[...] = jnp.sum(dac, axis=0, keepdims=True)
        gw_ref[0:1, :] = jnp.sum(dac * a2, axis=0, keepdims=True)
        gw_ref[1:2, :] = jnp.sum(dac * a1, axis=0, keepdims=True)
        gw_ref[2:3, :] = jnp.sum(dac * a, axis=0, keepdims=True)
        da = dac * w[2:3] + _shift_rows(dac, -1, row) * w[1:2] + _shift_rows(dac, -2, row) * w[0:1]
        da_ref[...] = da.astype(BF16)

    col = lambda off: pl.BlockSpec((s, CONV_TILE), lambda i: (0, i + off))
    return pl.pallas_call(
        body, name="conv_gate_bwd", grid=(N_CONV_TILES,),
        in_specs=[col(0), col(N_CONV_TILES), col(0), pl.BlockSpec((3, CONV_TILE), lambda i: (0, i)),
                  pl.BlockSpec((1, CONV_TILE), lambda i: (0, i))],
        out_specs=[col(0), col(0), pl.BlockSpec((3, CONV_TILE), lambda i: (0, i)),
                   pl.BlockSpec((1, CONV_TILE), lambda i: (0, i))],
        out_shape=[jax.ShapeDtypeStruct((s, D_FF), BF16), jax.ShapeDtypeStruct((s, D_FF), BF16),
                   jax.ShapeDtypeStruct((3, D_FF), F32), jax.ShapeDtypeStruct((1, D_FF), F32)],
        compiler_params=_params(("parallel",)))(u, u, dy, conv_w, conv_b)


HG_TILE = 128
CHUNK_UNROLL = 8


def _unrolled_loop(n, body, init):
    def group(i, carry):
        for u in range(CHUNK_UNROLL):
            carry = body(i * CHUNK_UNROLL + u, carry)
        return carry

    return lax.fori_loop(0, n // CHUNK_UNROLL, group, init)


def _head_col(off):
    return pl.BlockSpec((SEQ, HEAD_DIM), lambda h: (0, h + off))


def _hgrn_gates(hq, hf, lb, pos):
    q = hq * _sigmoid(hq)
    sig = _sigmoid(hf)
    f = lb + (1.0 - lb) * sig
    gl = jnp.log(f)
    for sh in (1, 2, 4, 8):
        gl = gl + jnp.where(pos >= sh, pltpu.roll(gl, sh, 0), 0.0)
    return q, sig, f, 1.0 - f, gl


def _lower_bound(lbl):
    return 1.0 / (1.0 + jnp.exp(lbl[1:2, :] - lbl[0:1, :]))


def _head_first_last():
    h = pl.program_id(0)
    return h == 0, h == HEADS - 1


def _hgrn_fwd_serial(proj, lb_logits, norm_w, fused=None, fused_arrays=()):
    n_tiles = SEQ // HG_TILE
    n_chunks = SEQ // CHUNK
    fused_arrays = list(fused_arrays)

    def body(hq_ref, hf_ref, hi_ref, hg_ref, lbl_ref, nw_ref, aout_ref, opre_ref, q_s, k_s, gl_s):
        lb = _lower_bound(lbl_ref[...])
        ones = jnp.ones((HEAD_DIM, HEAD_DIM), BF16)
        pos = lax.broadcasted_iota(jnp.int32, (HG_TILE, HEAD_DIM), 0) % CHUNK

        def tile(i, carry):
            rows = pl.ds(pl.multiple_of(i * HG_TILE, HG_TILE), HG_TILE)
            v = hi_ref[rows, :]
            q, _sig, _f, kk, gl = _hgrn_gates(hq_ref[rows, :], hf_ref[rows, :], lb, pos)
            o = _lane_sum(q * kk, ones) * v
            for d in range(1, CHUNK):
                e = jnp.where(pos >= d, jnp.exp(gl - pltpu.roll(gl, d, 0)), 0.0)
                o = o + _lane_sum(q * pltpu.roll(kk, d, 0) * e, ones) * pltpu.roll(v, d, 0)
            q_s[rows, :] = q
            k_s[rows, :] = kk
            gl_s[rows, :] = gl
            opre_ref[rows, :] = o
            return carry

        lax.fori_loop(0, n_tiles, tile, 0)

        def chunk(c, st):
            rows = pl.ds(pl.multiple_of(c * CHUNK, CHUNK), CHUNK)
            gl = gl_s[rows, :]
            qt = q_s[rows, :] * jnp.exp(gl)
            opre_ref[rows, :] += lax.dot_general(qt.astype(BF16), st.astype(BF16), NT, preferred_element_type=F32)
            gll = gl[CHUNK - 1:CHUNK, :]
            kt = k_s[rows, :] * jnp.exp(gll - gl)
            return st * jnp.exp(gll) + lax.dot_general(hi_ref[rows, :].astype(BF16), kt.astype(BF16), TN,
                                                       preferred_element_type=F32)

        _unrolled_loop(n_chunks, chunk, jnp.zeros((HEAD_DIM, HEAD_DIM), F32))

        def finish(i, carry):
            rows = pl.ds(pl.multiple_of(i * HG_TILE, HG_TILE), HG_TILE)
            o = opre_ref[rows, :]
            hg = hg_ref[rows, :]
            rs = lax.rsqrt(jnp.mean(o * o, axis=-1, keepdims=True) + EPS)
            aout_ref[rows, :] = ((o * rs) * nw_ref[...] * (hg * _sigmoid(hg))).astype(BF16)
            return carry

        lax.fori_loop(0, n_tiles, finish, 0)

    return _host_call(
        body, 6, 2, fused, _head_first_last, name="hgrn_fwd", grid=(HEADS,),
        in_specs=[_head_col(0), _head_col(HEADS), _head_col(2 * HEADS), _head_col(3 * HEADS),
                  pl.BlockSpec((2, HEAD_DIM), lambda h: (0, h)), pl.BlockSpec((1, HEAD_DIM), lambda h: (0, 0))],
        out_specs=[_head_col(0), _head_col(0)],
        out_shape=[jax.ShapeDtypeStruct((SEQ, HEADS * HEAD_DIM), BF16), jax.ShapeDtypeStruct((SEQ, HEADS * HEAD_DIM), F32)],
        scratch_shapes=[pltpu.VMEM((SEQ, HEAD_DIM), F32)] * 3, sem=("parallel",),
        operands=[proj, proj, proj, proj, lb_logits, norm_w] + fused_arrays)


def _hgrn_bwd_serial(proj, lb_logits, norm_w, o_pre, d_aout, fused=None, fused_arrays=()):
    n_tiles = SEQ // HG_TILE
    n_chunks = SEQ // CHUNK

    def body(hq_ref, hf_ref, hi_ref, hg_ref, lbl_ref, nw_ref, opre_ref, da_ref,
             dhq_ref, dhf_ref, dhi_ref, dhg_ref, dlog_ref, gnw_ref,
             q_s, k_s, gl_s, do_s, dq_s, dk_s, dv_s, st_s):
        h = pl.program_id(0)
        lb = _lower_bound(lbl_ref[...])
        nw = nw_ref[...]
        ones = jnp.ones((HEAD_DIM, HEAD_DIM), BF16)
        pos = lax.broadcasted_iota(jnp.int32, (HG_TILE, HEAD_DIM), 0) % CHUNK

        @pl.when(h == 0)
        def _():
            gnw_ref[...] = jnp.zeros_like(gnw_ref)

        def tile(i, carry):
            rows = pl.ds(pl.multiple_of(i * HG_TILE, HG_TILE), HG_TILE)
            v = hi_ref[rows, :]
            q, _sig, _f, kk, gl = _hgrn_gates(hq_ref[rows, :], hf_ref[rows, :], lb, pos)
            o = opre_ref[rows, :]
            hg = hg_ref[rows, :]
            da = da_ref[rows, :]
            rs = lax.rsqrt(jnp.mean(o * o, axis=-1, keepdims=True) + EPS)
            oh = o * rs
            sg = _sigmoid(hg)
            dnorm = da * (hg * sg)
            dhg_ref[rows, :] = (da * (oh * nw) * _dsilu(hg, sg)).astype(BF16)
            gnw_ref[...] += jnp.sum(dnorm * oh, axis=0, keepdims=True)
            doh = dnorm * nw
            do = rs * (doh - oh * jnp.mean(doh * oh, axis=-1, keepdims=True))

            d_a = _lane_sum(do * v, ones)
            dq = d_a * kk
            dk = d_a * q
            dv = _lane_sum(q * kk, ones) * do
            for d in range(1, CHUNK):
                ks = pltpu.roll(kk, d, 0)
                e = jnp.where(pos >= d, jnp.exp(gl - pltpu.roll(gl, d, 0)), 0.0)
                a_d = _lane_sum(q * ks * e, ones)
                d_a = _lane_sum(do * pltpu.roll(v, d, 0), ones) * e
                dq = dq + d_a * ks
                dk = dk + pltpu.roll(d_a * q, HG_TILE - d, 0)
                dv = dv + pltpu.roll(a_d * do, HG_TILE - d, 0)
            q_s[rows, :] = q
            k_s[rows, :] = kk
            gl_s[rows, :] = gl
            do_s[rows, :] = do
            dq_s[rows, :] = dq
            dk_s[rows, :] = dk
            dv_s[rows, :] = dv
            return carry

        lax.fori_loop(0, n_tiles, tile, 0)

        def fwd_chunk(c, st):
            rows = pl.ds(pl.multiple_of(c * CHUNK, CHUNK), CHUNK)
            gl = gl_s[rows, :]
            st_s[c] = st
            dq_s[rows, :] += jnp.dot(do_s[rows, :].astype(BF16), st.astype(BF16),
                                     preferred_element_type=F32) * jnp.exp(gl)
            gll = gl[CHUNK - 1:CHUNK, :]
            kt = k_s[rows, :] * jnp.exp(gll - gl)
            return st * jnp.exp(gll) + lax.dot_general(hi_ref[rows, :].astype(BF16), kt.astype(BF16), TN,
                                                       preferred_element_type=F32)

        _unrolled_loop(n_chunks, fwd_chunk, jnp.zeros((HEAD_DIM, HEAD_DIM), F32))

        pos_c = lax.broadcasted_iota(jnp.int32, (CHUNK, HEAD_DIM), 0)

        def bwd_chunk(i, carry):
            rt, dlb = carry
            c = n_chunks - 1 - i
            rows = pl.ds(pl.multiple_of(c * CHUNK, CHUNK), CHUNK)
            gl = gl_s[rows, :]
            q = q_s[rows, :]
            kk = k_s[rows, :]
            do = do_s[rows, :]
            gll = gl[CHUNK - 1:CHUNK, :]
            egl = jnp.exp(gll)
            ekt = jnp.exp(gll - gl)
            rt_b = rt.astype(BF16)
            dk_in = dk_s[rows, :]
            dk_far = jnp.dot(hi_ref[rows, :].astype(BF16), rt_b, preferred_element_type=F32) * ekt
            dk = dk_in + dk_far
            dv = dv_s[rows, :] + lax.dot_general((kk * ekt).astype(BF16), rt_b, NT, preferred_element_type=F32)
            dq = dq_s[rows, :]
            rc = q * dq - kk * dk_in
            pc = kk * dk_far
            pre = pc
            for sh in (1, 2, 4, 8):
                rc = rc + jnp.where(pos_c < CHUNK - sh, pltpu.roll(rc, CHUNK - sh, 0), 0.0)
                pre = pre + jnp.where(pos_c >= sh, pltpu.roll(pre, sh, 0), 0.0)
            across = jnp.sum(st_s[c] * rt, axis=0, keepdims=True) * egl
            dgl = rc + (pre - pc) + across
            hf = hf_ref[rows, :]
            sig = _sigmoid(hf)
            f = lb + (1.0 - lb) * sig
            df = dgl / f - dk
            dhf_ref[rows, :] = (df * (1.0 - lb) * sig * (1.0 - sig)).astype(BF16)
            hq = hq_ref[rows, :]
            dhq_ref[rows, :] = (dq * _dsilu(hq, _sigmoid(hq))).astype(BF16)
            dhi_ref[rows, :] = dv.astype(BF16)
            rt_new = rt * egl + lax.dot_general(do.astype(BF16), (q * jnp.exp(gl)).astype(BF16), TN,
                                                preferred_element_type=F32)
            return (rt_new, dlb + jnp.sum(df * (1.0 - sig), axis=0, keepdims=True))

        _, dlb = _unrolled_loop(n_chunks, bwd_chunk,
                                (jnp.zeros((HEAD_DIM, HEAD_DIM), F32), jnp.zeros((1, HEAD_DIM), F32)))
        dl0 = lb * (1.0 - lb) * dlb
        dlog_ref[0:1, :] = dl0
        dlog_ref[1:2, :] = -dl0

    wide = HEADS * HEAD_DIM
    return _host_call(
        body, 8, 6, fused, _head_first_last, name="hgrn_bwd", grid=(HEADS,),
        in_specs=[_head_col(0), _head_col(HEADS), _head_col(2 * HEADS), _head_col(3 * HEADS),
                  pl.BlockSpec((2, HEAD_DIM), lambda h: (0, h)), pl.BlockSpec((1, HEAD_DIM), lambda h: (0, 0)),
                  _head_col(0), _head_col(0)],
        out_specs=[_head_col(0)] * 4 + [pl.BlockSpec((2, HEAD_DIM), lambda h: (0, h)),
                                        pl.BlockSpec((1, HEAD_DIM), lambda h: (0, 0))],
        out_shape=[jax.ShapeDtypeStruct((SEQ, wide), BF16)] * 4 + [jax.ShapeDtypeStruct((2, wide), F32),
                                                                    jax.ShapeDtypeStruct((1, HEAD_DIM), F32)],
        scratch_shapes=[pltpu.VMEM((SEQ, HEAD_DIM), F32)] * 7 + [pltpu.VMEM((n_chunks, HEAD_DIM, HEAD_DIM), F32)],
        sem=("arbitrary",),
        operands=[proj, proj, proj, proj, lb_logits, norm_w, o_pre, d_aout] + list(fused_arrays))


CHUNKS_PER_TILE = HG_TILE // CHUNK


def _chunk_end(x, pos):
    y = jnp.where(pos == CHUNK - 1, x, 0.0)
    for sh in (1, 2, 4, 8):
        y = y + jnp.where(pos < CHUNK - sh, pltpu.roll(y, x.shape[0] - sh, 0), 0.0)
    return y


def _suffix_in_chunk(x, pos):
    for sh in (1, 2, 4, 8):
        x = x + jnp.where(pos < CHUNK - sh, pltpu.roll(x, x.shape[0] - sh, 0), 0.0)
    return x


def _prefix_in_chunk(x, pos):
    for sh in (1, 2, 4, 8):
        x = x + jnp.where(pos >= sh, pltpu.roll(x, sh, 0), 0.0)
    return x


def _chunk_rows(cc):
    return slice(cc * CHUNK, (cc + 1) * CHUNK)


def _outer_products(lhs_b, rhs_b, dst, i):
    for cc in range(CHUNKS_PER_TILE):
        dst[i * CHUNKS_PER_TILE + cc] = lax.dot_general(lhs_b[_chunk_rows(cc)], rhs_b[_chunk_rows(cc)], TN,
                                                        preferred_element_type=F32)


def _state_scan(n_chunks, gl_s, u_s, keep, reverse):
    def step(k, st):
        c = n_chunks - 1 - k if reverse else k
        keep[c] = st.astype(BF16)
        gl = gl_s[pl.ds(pl.multiple_of(c * CHUNK, CHUNK), CHUNK), :]
        return st * jnp.exp(gl[CHUNK - 1:CHUNK, :]) + u_s[c]

    _unrolled_loop(n_chunks, step, jnp.zeros((HEAD_DIM, HEAD_DIM), F32))


def _hgrn_fwd(proj, lb_logits, norm_w, fused=None, fused_arrays=()):
    n_tiles = SEQ // HG_TILE
    n_chunks = SEQ // CHUNK
    fused_arrays = list(fused_arrays)

    def body(hq_ref, hf_ref, hi_ref, hg_ref, lbl_ref, nw_ref, aout_ref, opre_ref, qt_s, gl_s, u_s, st_s):
        lb = _lower_bound(lbl_ref[...])
        ones = jnp.ones((HEAD_DIM, HEAD_DIM), BF16)
        pos = lax.broadcasted_iota(jnp.int32, (HG_TILE, HEAD_DIM), 0) % CHUNK

        def tile(i, carry):
            rows = pl.ds(pl.multiple_of(i * HG_TILE, HG_TILE), HG_TILE)
            v = hi_ref[rows, :]
            q, _sig, _f, kk, gl = _hgrn_gates(hq_ref[rows, :], hf_ref[rows, :], lb, pos)
            o = _lane_sum(q * kk, ones) * v
            for d in range(1, CHUNK):
                e = jnp.where(pos >= d, jnp.exp(gl - pltpu.roll(gl, d, 0)), 0.0)
                o = o + _lane_sum(q * pltpu.roll(kk, d, 0) * e, ones) * pltpu.roll(v, d, 0)
            opre_ref[rows, :] = o
            qt_s[rows, :] = q * jnp.exp(gl)
            gl_s[rows, :] = gl
            kt = kk * jnp.exp(_chunk_end(gl, pos) - gl)
            _outer_products(v.astype(BF16), kt.astype(BF16), u_s, i)
            return carry

        lax.fori_loop(0, n_tiles, tile, 0)
        _state_scan(n_chunks, gl_s, u_s, st_s, reverse=False)

        def finish(i, carry):
            rows = pl.ds(pl.multiple_of(i * HG_TILE, HG_TILE), HG_TILE)
            qt_b = qt_s[rows, :].astype(BF16)
            past = [lax.dot_general(qt_b[_chunk_rows(cc)], st_s[i * CHUNKS_PER_TILE + cc], NT,
                                    preferred_element_type=F32) for cc in range(CHUNKS_PER_TILE)]
            o = opre_ref[rows, :] + jnp.concatenate(past, axis=0)
            opre_ref[rows, :] = o
            hg = hg_ref[rows, :]
            rs = lax.rsqrt(jnp.mean(o * o, axis=-1, keepdims=True) + EPS)
            aout_ref[rows, :] = ((o * rs) * nw_ref[...] * (hg * _sigmoid(hg))).astype(BF16)
            return carry

        lax.fori_loop(0, n_tiles, finish, 0)

    return _host_call(
        body, 6, 2, fused, _head_first_last, name="hgrn_fwd", grid=(HEADS,),
        in_specs=[_head_col(0), _head_col(HEADS), _head_col(2 * HEADS), _head_col(3 * HEADS),
                  pl.BlockSpec((2, HEAD_DIM), lambda h: (0, h)), pl.BlockSpec((1, HEAD_DIM), lambda h: (0, 0))],
        out_specs=[_head_col(0), _head_col(0)],
        out_shape=[jax.ShapeDtypeStruct((SEQ, HEADS * HEAD_DIM), BF16), jax.ShapeDtypeStruct((SEQ, HEADS * HEAD_DIM), F32)],
        scratch_shapes=[pltpu.VMEM((SEQ, HEAD_DIM), F32)] * 2 + [pltpu.VMEM((n_chunks, HEAD_DIM, HEAD_DIM), F32),
                                                                 pltpu.VMEM((n_chunks, HEAD_DIM, HEAD_DIM), BF16)],
        sem=("parallel",), operands=[proj, proj, proj, proj, lb_logits, norm_w] + fused_arrays)


def _hgrn_bwd(proj, lb_logits, norm_w, o_pre, d_aout, fused=None, fused_arrays=()):
    n_tiles = SEQ // HG_TILE
    n_chunks = SEQ // CHUNK

    def body(hq_ref, hf_ref, hi_ref, hg_ref, lbl_ref, nw_ref, opre_ref, da_ref,
             dhq_ref, dhf_ref, dhi_ref, dhg_ref, dlog_ref, gnw_ref,
             q_s, k_s, gl_s, do_s, dq_s, dk_s, dv_s, u_s, st_s, rt_s):
        h = pl.program_id(0)
        lb = _lower_bound(lbl_ref[...])
        nw = nw_ref[...]
        ones = jnp.ones((HEAD_DIM, HEAD_DIM), BF16)
        pos = lax.broadcasted_iota(jnp.int32, (HG_TILE, HEAD_DIM), 0) % CHUNK

        @pl.when(h == 0)
        def _():
            gnw_ref[...] = jnp.zeros_like(gnw_ref)

        def tile(i, carry):
            rows = pl.ds(pl.multiple_of(i * HG_TILE, HG_TILE), HG_TILE)
            v = hi_ref[rows, :]
            q, _sig, _f, kk, gl = _hgrn_gates(hq_ref[rows, :], hf_ref[rows, :], lb, pos)
            o = opre_ref[rows, :]
            hg = hg_ref[rows, :]
            da = da_ref[rows, :]
            rs = lax.rsqrt(jnp.mean(o * o, axis=-1, keepdims=True) + EPS)
            oh = o * rs
            sg = _sigmoid(hg)
            dnorm = da * (hg * sg)
            dhg_ref[rows, :] = (da * (oh * nw) * _dsilu(hg, sg)).astype(BF16)
            gnw_ref[...] += jnp.sum(dnorm * oh, axis=0, keepdims=True)
            doh = dnorm * nw
            do = rs * (doh - oh * jnp.mean(doh * oh, axis=-1, keepdims=True))

            d_a = _lane_sum(do * v, ones)
            dq = d_a * kk
            dk = d_a * q
            dv = _lane_sum(q * kk, ones) * do
            for d in range(1, CHUNK):
                ks = pltpu.roll(kk, d, 0)
                e = jnp.where(pos >= d, jnp.exp(gl - pltpu.roll(gl, d, 0)), 0.0)
                a_d = _lane_sum(q * ks * e, ones)
                d_a = _lane_sum(do * pltpu.roll(v, d, 0), ones) * e
                dq = dq + d_a * ks
                dk = dk + pltpu.roll(d_a * q, HG_TILE - d, 0)
                dv = dv + pltpu.roll(a_d * do, HG_TILE - d, 0)
            q_s[rows, :] = q
            k_s[rows, :] = kk
            gl_s[rows, :] = gl
            do_s[rows, :] = do
            dq_s[rows, :] = dq
            dk_s[rows, :] = dk
            dv_s[rows, :] = dv
            kt = kk * jnp.exp(_chunk_end(gl, pos) - gl)
            _outer_products(v.astype(BF16), kt.astype(BF16), u_s, i)
            return carry

        lax.fori_loop(0, n_tiles, tile, 0)
        _state_scan(n_chunks, gl_s, u_s, st_s, reverse=False)

        def reverse_increments(i, carry):
            rows = pl.ds(pl.multiple_of(i * HG_TILE, HG_TILE), HG_TILE)
            qt = q_s[rows, :] * jnp.exp(gl_s[rows, :])
            _outer_products(do_s[rows, :].astype(BF16), qt.astype(BF16), u_s, i)
            return carry

        lax.fori_loop(0, n_tiles, reverse_increments, 0)
        _state_scan(n_chunks, gl_s, u_s, rt_s, reverse=True)

        def finish(i, dlb):
            rows = pl.ds(pl.multiple_of(i * HG_TILE, HG_TILE), HG_TILE)
            q = q_s[rows, :]
            kk = k_s[rows, :]
            gl = gl_s[rows, :]
            gll = _chunk_end(gl, pos)
            ekt = jnp.exp(gll - gl)
            do_b = do_s[rows, :].astype(BF16)
            v_b = hi_ref[rows, :].astype(BF16)
            kt_b = (kk * ekt).astype(BF16)
            dq_far, dk_far, dv_far, across = [], [], [], []
            for cc in range(CHUNKS_PER_TILE):
                st = st_s[i * CHUNKS_PER_TILE + cc]
                rt = rt_s[i * CHUNKS_PER_TILE + cc]
                sl = _chunk_rows(cc)
                dq_far.append(jnp.dot(do_b[sl], st, preferred_element_type=F32))
                dk_far.append(jnp.dot(v_b[sl], rt, preferred_element_type=F32))
                dv_far.append(lax.dot_general(kt_b[sl], rt, NT, preferred_element_type=F32))
                both = jnp.sum(st.astype(F32) * rt.astype(F32), axis=0, keepdims=True)
                across.append(jnp.broadcast_to(both, (CHUNK, HEAD_DIM)))
            dq = dq_s[rows, :] + jnp.concatenate(dq_far, axis=0) * jnp.exp(gl)
            dk_in = dk_s[rows, :]
            dk_out = jnp.concatenate(dk_far, axis=0) * ekt
            dk = dk_in + dk_out
            dv = dv_s[rows, :] + jnp.concatenate(dv_far, axis=0)
            pc = kk * dk_out
            dgl = (_suffix_in_chunk(q * dq - kk * dk_in, pos) + (_prefix_in_chunk(pc, pos) - pc)
                   + jnp.concatenate(across, axis=0) * jnp.exp(gll))
            hf = hf_ref[rows, :]
            sig = _sigmoid(hf)
            f = lb + (1.0 - lb) * sig
            df = dgl / f - dk
            dhf_ref[rows, :] = (df * (1.0 - lb) * sig * (1.0 - sig)).astype(BF16)
            hq = hq_ref[rows, :]
            dhq_ref[rows, :] = (dq * _dsilu(hq, _sigmoid(hq))).astype(BF16)
            dhi_ref[rows, :] = dv.astype(BF16)
            return dlb + jnp.sum(df * (1.0 - sig), axis=0, keepdims=True)

        dlb = lax.fori_loop(0, n_tiles, finish, jnp.zeros((1, HEAD_DIM), F32))
        dl0 = lb * (1.0 - lb) * dlb
        dlog_ref[0:1, :] = dl0
        dlog_ref[1:2, :] = -dl0

    wide = HEADS * HEAD_DIM
    return _host_call(
        body, 8, 6, fused, _head_first_last, name="hgrn_bwd", grid=(HEADS,),
        in_specs=[_head_col(0), _head_col(HEADS), _head_col(2 * HEADS), _head_col(3 * HEADS),
                  pl.BlockSpec((2, HEAD_DIM), lambda h: (0, h)), pl.BlockSpec((1, HEAD_DIM), lambda h: (0, 0)),
                  _head_col(0), _head_col(0)],
        out_specs=[_head_col(0)] * 4 + [pl.BlockSpec((2, HEAD_DIM), lambda h: (0, h)),
                                        pl.BlockSpec((1, HEAD_DIM), lambda h: (0, 0))],
        out_shape=[jax.ShapeDtypeStruct((SEQ, wide), BF16)] * 4 + [jax.ShapeDtypeStruct((2, wide), F32),
                                                                    jax.ShapeDtypeStruct((1, HEAD_DIM), F32)],
        scratch_shapes=[pltpu.VMEM((SEQ, HEAD_DIM), F32)] * 7 + [pltpu.VMEM((n_chunks, HEAD_DIM, HEAD_DIM), F32),
                                                                 pltpu.VMEM((n_chunks, HEAD_DIM, HEAD_DIM), BF16),
                                                                 pltpu.VMEM((n_chunks, HEAD_DIM, HEAD_DIM), BF16)],
        sem=("arbitrary",),
        operands=[proj, proj, proj, proj, lb_logits, norm_w, o_pre, d_aout] + list(fused_arrays))


Q_TILE = 256
ATT_SCALE = HEAD_DIM ** -0.5
ATT_OFF = 4 * HEADS


def _qk_prep(proj, q_w, k_w):
    def body(aq_ref, ak_ref, av_ref, qw_ref, kw_ref, qn_ref, kn_ref, v_ref):
        aq = aq_ref[...]
        ak = ak_ref[...]
        qn_ref[...] = (aq * lax.rsqrt(jnp.mean(aq * aq, axis=-1, keepdims=True) + EPS) * qw_ref[...]).astype(BF16)
        kn_ref[...] = (ak * lax.rsqrt(jnp.mean(ak * ak, axis=-1, keepdims=True) + EPS) * kw_ref[...]).astype(BF16)
        v_ref[...] = av_ref[...].astype(BF16)

    wide = HEADS * HEAD_DIM
    vec = pl.BlockSpec((1, HEAD_DIM), lambda h: (0, 0))
    return pl.pallas_call(
        body, name="qk_prep", grid=(HEADS,),
        in_specs=[_head_col(ATT_OFF), _head_col(ATT_OFF + HEADS), _head_col(ATT_OFF + 2 * HEADS), vec, vec],
        out_specs=[_head_col(0)] * 3, out_shape=[jax.ShapeDtypeStruct((SEQ, wide), BF16)] * 3,
        compiler_params=_params(("parallel",)))(proj, proj, proj, q_w, k_w)


def _alibi_slopes():
    slopes = jnp.exp2(-8.0 * jnp.arange(1, HEADS + 1, dtype=F32) / HEADS)
    return jnp.broadcast_to(slopes[:, None, None], (HEADS, 1, HEAD_DIM))


SLOPE_SPEC = pl.BlockSpec((None, 1, HEAD_DIM), lambda h, i: (h, 0, 0))


N_Q_TILES = SEQ // Q_TILE
K_BLOCK = 512
NOT_ATTENDED = 1e35


def _att_tables():
    o = jnp.arange(N_Q_TILES, dtype=jnp.int32)[:, None, None]
    r = jnp.arange(Q_TILE, dtype=jnp.int32)[None, :, None]
    c = jnp.arange(K_BLOCK, dtype=jnp.int32)[None, None, :]
    dist = o * Q_TILE + r - c
    mult = ((dist <= 128).astype(F32) + (((dist % 4) == 0) & (dist <= 512)).astype(F32)
            + ((dist % 16) == 0).astype(F32))
    valid = (dist >= 0) & (mult > 0)
    return (jnp.where(valid, dist.astype(F32), NOT_ATTENDED),
            jnp.where(valid, jnp.log(jnp.maximum(mult, 1.0)), 0.0))


TABLE_SPEC = pl.BlockSpec((N_Q_TILES, Q_TILE, K_BLOCK), lambda h, i: (0, 0, 0))


def _att_block(q, k_ref, j, i, slope, dist_ref, lmul_ref):
    rows = pl.ds(pl.multiple_of(j * K_BLOCK, K_BLOCK), K_BLOCK)
    off = i - j * (K_BLOCK // Q_TILE)
    s = lax.dot_general(q, k_ref[rows, :], NT, preferred_element_type=F32) * ATT_SCALE
    return s - slope * dist_ref[off] + lmul_ref[off], rows


def _n_key_blocks(i):
    return (i + K_BLOCK // Q_TILE) // (K_BLOCK // Q_TILE)


def _att_first_last():
    h, i = pl.program_id(0), pl.program_id(1)
    return (h == 0) & (i == 0), (h == HEADS - 1) & (i == N_Q_TILES - 1)


def _attn_fwd(qn, kn, vb, fused=None, fused_arrays=()):
    def body(q_ref, k_ref, v_ref, sl_ref, dist_ref, lmul_ref, o_ref, lse_ref):
        i = pl.program_id(1)
        q = q_ref[...]
        slope = sl_ref[0:1, 0:1]

        def step(j, carry):
            m, l, acc = carry
            sb, rows = _att_block(q, k_ref, j, i, slope, dist_ref, lmul_ref)
            m_new = jnp.maximum(m, jnp.max(sb, axis=-1, keepdims=True))
            alpha = jnp.exp(m - m_new)
            p = jnp.exp(sb - m_new)
            l = alpha * l + jnp.sum(p, axis=-1, keepdims=True)
            acc = alpha * acc + jnp.dot(p.astype(BF16), v_ref[rows, :], preferred_element_type=F32)
            return m_new, l, acc

        m, l, acc = lax.fori_loop(0, _n_key_blocks(i), step,
                                  (jnp.full((Q_TILE, 1), -1e30, F32), jnp.zeros((Q_TILE, 1), F32),
                                   jnp.zeros((Q_TILE, HEAD_DIM), F32)))
        o_ref[...] = acc / l
        lse_ref[...] = m + jnp.log(l)

    wide = HEADS * HEAD_DIM
    qt = pl.BlockSpec((Q_TILE, HEAD_DIM), lambda h, i: (i, h))
    full = pl.BlockSpec((SEQ, HEAD_DIM), lambda h, i: (0, h))
    return _host_call(
        body, 6, 2, fused, _att_first_last, name="attn_fwd", grid=(HEADS, N_Q_TILES),
        in_specs=[qt, full, full, SLOPE_SPEC, TABLE_SPEC, TABLE_SPEC],
        out_specs=[qt, pl.BlockSpec((None, Q_TILE, 1), lambda h, i: (h, i, 0))],
        out_shape=[jax.ShapeDtypeStruct((SEQ, wide), F32), jax.ShapeDtypeStruct((HEADS, SEQ, 1), F32)],
        scratch_shapes=[], sem=("parallel", "parallel"),
        operands=[qn, kn, vb, _alibi_slopes(), *_att_tables()] + list(fused_arrays))


def _attn_bwd(qn, kn, vb, o, lse, d_mix, fused=None, fused_arrays=()):
    def body(q_ref, k_ref, v_ref, o_ref, lse_ref, do_ref, sl_ref, dist_ref, lmul_ref, dq_ref, dk_ref, dv_ref):
        i = pl.program_id(1)
        q = q_ref[...]
        do = do_ref[...]
        do_b = do.astype(BF16)
        slope = sl_ref[0:1, 0:1]
        lse = lse_ref[...]
        delta = jnp.sum(do * o_ref[...], axis=-1, keepdims=True)

        @pl.when(i == 0)
        def _():
            dk_ref[...] = jnp.zeros_like(dk_ref)
            dv_ref[...] = jnp.zeros_like(dv_ref)

        def step(j, dq):
            sb, rows = _att_block(q, k_ref, j, i, slope, dist_ref, lmul_ref)
            p = jnp.exp(sb - lse)
            dp = lax.dot_general(do_b, v_ref[rows, :], NT, preferred_element_type=F32)
            ds = (p * (dp - delta)).astype(BF16)
            dk_ref[rows, :] += lax.dot_general(ds, q, TN, preferred_element_type=F32) * ATT_SCALE
            dv_ref[rows, :] += lax.dot_general(p.astype(BF16), do_b, TN, preferred_element_type=F32)
            return dq + jnp.dot(ds, k_ref[rows, :], preferred_element_type=F32)

        dq = lax.fori_loop(0, _n_key_blocks(i), step, jnp.zeros((Q_TILE, HEAD_DIM), F32))
        dq_ref[...] = dq * ATT_SCALE

    wide = HEADS * HEAD_DIM
    qt = pl.BlockSpec((Q_TILE, HEAD_DIM), lambda h, i: (i, h))
    full = pl.BlockSpec((SEQ, HEAD_DIM), lambda h, i: (0, h))
    return _host_call(
        body, 9, 3, fused, _att_first_last, name="attn_bwd", grid=(HEADS, N_Q_TILES),
        in_specs=[qt, full, full, qt, pl.BlockSpec((None, Q_TILE, 1), lambda h, i: (h, i, 0)),
                  pl.BlockSpec((Q_TILE, HEAD_DIM), lambda h, i: (i, h + HEADS)), SLOPE_SPEC, TABLE_SPEC, TABLE_SPEC],
        out_specs=[qt, full, full], out_shape=[jax.ShapeDtypeStruct((SEQ, wide), F32)] * 3,
        scratch_shapes=[], sem=("parallel", "arbitrary"),
        operands=[qn, kn, vb, o, lse, d_mix, _alibi_slopes(), *_att_tables()] + list(fused_arrays))


def _qk_bwd(proj, q_w, k_w, dqn, dkn, dv):
    def body(aq_ref, ak_ref, qw_ref, kw_ref, dqn_ref, dkn_ref, dv_ref, daq_ref, dak_ref, dav_ref, gq_ref, gk_ref):
        h = pl.program_id(0)

        @pl.when(h == 0)
        def _():
            gq_ref[...] = jnp.zeros_like(gq_ref)
            gk_ref[...] = jnp.zeros_like(gk_ref)

        def one(a_ref, w_ref, d_ref, da_ref, g_ref):
            a = a_ref[...]
            d = d_ref[...]
            rs = lax.rsqrt(jnp.mean(a * a, axis=-1, keepdims=True) + EPS)
            ah = a * rs
            g_ref[...] += jnp.sum(d * ah, axis=0, keepdims=True)
            dah = d * w_ref[...]
            da_ref[...] = (rs * (dah - ah * jnp.mean(dah * ah, axis=-1, keepdims=True))).astype(BF16)

        one(aq_ref, qw_ref, dqn_ref, daq_ref, gq_ref)
        one(ak_ref, kw_ref, dkn_ref, dak_ref, gk_ref)
        dav_ref[...] = dv_ref[...].astype(BF16)

    wide = HEADS * HEAD_DIM
    vec = pl.BlockSpec((1, HEAD_DIM), lambda h: (0, 0))
    return pl.pallas_call(
        body, name="qk_bwd", grid=(HEADS,),
        in_specs=[_head_col(ATT_OFF), _head_col(ATT_OFF + HEADS), vec, vec, _head_col(0), _head_col(0), _head_col(0)],
        out_specs=[_head_col(0)] * 3 + [vec, vec],
        out_shape=[jax.ShapeDtypeStruct((SEQ, wide), BF16)] * 3 + [jax.ShapeDtypeStruct((1, HEAD_DIM), F32)] * 2,
        compiler_params=_params(("arbitrary",)))(proj, proj, q_w, k_w, dqn, dkn, dv)


def _pair_sum(name, partial, theirs, core):
    _, r, c = theirs.shape
    tr = r // 2 if r % 16 == 0 else r

    def body(core_ref, a_ref, b_ref, o_ref):
        o_ref[...] = (a_ref[...].astype(F32) + b_ref[...].astype(F32)).astype(BF16)

    spec = pl.BlockSpec((None, tr, c), lambda q, i, core_ref: (q, i, 0))
    grid_spec = pltpu.PrefetchScalarGridSpec(
        num_scalar_prefetch=1, grid=(4, r // tr),
        in_specs=[pl.BlockSpec((None, tr, c), lambda q, i, core_ref: (2 * q + core_ref[0], i, 0)), spec],
        out_specs=spec)
    return pl.pallas_call(body, name=name, grid_spec=grid_spec, out_shape=jax.ShapeDtypeStruct(theirs.shape, BF16),
                          compiler_params=_params(("parallel", "parallel")))(core, partial, theirs)


def _adamw_step(w, m, v, g):
    nm = ADAM_B1 * m + (1.0 - ADAM_B1) * g
    nv = ADAM_B2 * v + (1.0 - ADAM_B2) * (g * g)
    m_hat = nm / (1.0 - ADAM_B1 ** ADAM_STEP)
    v_hat = nv / (1.0 - ADAM_B2 ** ADAM_STEP)
    return -ADAM_LR * (m_hat / (jnp.sqrt(v_hat) + ADAM_EPS) + ADAM_WD * w), nm, nv


def _adamw(name, w, m, v, addends, tr=None):
    r, c = w.shape
    tr = r if tr is None else tr
    n_add = len(addends)

    def body(*refs):
        w_ref, m_ref, v_ref = refs[:3]
        add_refs = refs[3:3 + n_add]
        g_ref, d_ref, nm_ref, nv_ref = refs[3 + n_add:]
        g = add_refs[0][...].astype(F32)
        for a_ref in add_refs[1:]:
            g = g + a_ref[...].astype(F32)
        g_ref[...] = g
        d_ref[...], nm_ref[...], nv_ref[...] = _adamw_step(w_ref[...], m_ref[...], v_ref[...], g)

    spec = pl.BlockSpec((tr, c), lambda i: (i, 0))
    out = jax.ShapeDtypeStruct((r, c), F32)
    return pl.pallas_call(body, name=name, grid=(r // tr,), in_specs=[spec] * (3 + n_add), out_specs=[spec] * 4,
                          out_shape=[out] * 4, compiler_params=_params(("parallel",)))(w, m, v, *addends)


def _adamw_reduced(name, w, m, v, chip_sums, received, chip, tr):
    r, c = w.shape

    def body(chip_ref, w_ref, m_ref, v_ref, own_ref, r0_ref, r1_ref, r2_ref, g_ref, d_ref, nm_ref, nv_ref):
        g = ((own_ref[...].astype(F32) + r0_ref[...].astype(F32)) + r1_ref[...].astype(F32)) + r2_ref[...].astype(F32)
        g_ref[...] = g
        d_ref[...], nm_ref[...], nv_ref[...] = _adamw_step(w_ref[...], m_ref[...], v_ref[...], g)

    spec = pl.BlockSpec((tr, c), lambda i, chip_ref: (i, 0))

    def slot(k):
        return pl.BlockSpec((None, tr, c), lambda i, chip_ref: (k, i, 0))

    grid_spec = pltpu.PrefetchScalarGridSpec(
        num_scalar_prefetch=1, grid=(r // tr,),
        in_specs=[spec, spec, spec, pl.BlockSpec((None, tr, c), lambda i, chip_ref: (chip_ref[0], i, 0)),
                  slot(0), slot(1), slot(2)],
        out_specs=[spec] * 4)
    out = jax.ShapeDtypeStruct((r, c), F32)
    return pl.pallas_call(body, name=name, grid_spec=grid_spec, out_shape=[out] * 4,
                          compiler_params=_params(("parallel",)))(chip, w, m, v, chip_sums, received, received, received)


def _sum_devices(gathered):
    _, r, c = gathered.shape

    def body(g_ref, o_ref):
        acc = g_ref[0]
        for d in range(1, N_DEV):
            acc = acc + g_ref[d]
        o_ref[...] = acc

    return pl.pallas_call(body, name="sum_devices", out_shape=jax.ShapeDtypeStruct((r, c), F32))(gathered)


def _pack_rows(vectors, rows):
    flat = jnp.concatenate([v.reshape(-1) for v in vectors])
    return jnp.pad(flat, (0, rows * 128 - flat.shape[0])).reshape(rows, 128)


def _unpack(flat, shapes):
    out, off = [], 0
    for shp in shapes:
        n = 1
        for d in shp:
            n *= d
        out.append(flat[off:off + n].reshape(shp))
        off += n
    return out


def _device_step(xs, tgt, mod, norm1_w, norm2_w, lb_logits, hg_norm_w, q_norm_w, k_norm_w, conv_w_full, conv_b,
                 win_g, w_out_x, w_up_x, w_down_x, core=None):
    fused = core is not None
    shift1, scale1, gate1, shift2, scale2, gate2 = (mod[k] for k in range(6))

    h, rstd1 = _norm_fwd("norm1_fwd", xs, norm1_w, scale1, shift1)
    if fused:
        proj, (wout_g,) = _mm_blocked_rhs("mm_in", h, win_g, fused=_FusedCopies("gather", [w_out_x]),
                                          fused_arrays=[w_out_x])
        (a_out, o_pre), (wup_g,) = _hgrn_fwd(proj, lb_logits, hg_norm_w,
                                             _FusedCopies("gather", [w_up_x], peers=(0, 1, 2)), [w_up_x])
        wout_g, = _forward_to_sibling("allgather_stage2_out", [wout_g])
        wout_full = wout_g.reshape(D_MODEL, D_MODEL)
        qn, kn, vb = _qk_prep(proj, q_norm_w, k_norm_w)
        (att_o, lse), (wup_g,) = _attn_fwd(qn, kn, vb, _FusedCopies("relay", [wup_g]), [wup_g])
    else:
        proj = _mm_blocked_rhs("mm_in", h, win_g)
        (a_out, o_pre), _ = _hgrn_fwd(proj, lb_logits, hg_norm_w)
        wup_g, wout_full, wdown_full = w_up_x, w_out_x, w_down_x
        qn, kn, vb = _qk_prep(proj, q_norm_w, k_norm_w)
        (att_o, lse), _ = _attn_fwd(qn, kn, vb)
    mixin = jnp.concatenate([a_out, att_o.astype(BF16)], axis=1)
    if fused:
        mix, (wup_g,) = _mm_plain("mm_out", mixin, wout_full, NN, 512, 1024, F32,
                                  fused=_FusedCopies("forward", [wup_g]), fused_arrays=[wup_g])
    else:
        mix = _mm_plain("mm_out", mixin, wout_full, NN, 512, 1024, F32)
    x1, h2, rstd2 = _norm_fwd("norm2_fwd", xs, norm2_w, scale2, shift2, resid=mix, gate=gate1)
    if fused:
        u, (wdown_g,) = _mm_blocked_rhs("mm_up", h2, wup_g, fused=_FusedCopies("gather", [w_down_x]),
                                        fused_arrays=[w_down_x])
        y, (wdown_g,) = _conv_gate_fwd(u, conv_w_full, conv_b, _FusedCopies("forward", [wdown_g]), [wdown_g])
        wdown_full = wdown_g.reshape(D_FF, D_MODEL)
    else:
        u = _mm_blocked_rhs("mm_up", h2, wup_g)
        y = _conv_gate_fwd(u, conv_w_full, conv_b)
    ffn = _mm_plain("mm_down", y, wdown_full, NN, 512, 512, F32)
    loss_v, dout, dffn, dgate2 = _loss_head(x1, ffn, gate2, tgt)

    dy = _mm_plain("mm_down_dx", dffn, wdown_full, NT, 512, UP_BLK, F32)
    gw_down = _mm_plain("mm_down_dw", y, dffn, TN, UP_BLK, 1024, BF16)
    da, dg, gconv_w, gconv_b = _conv_gate_bwd(u, dy, conv_w_full, conv_b)
    du = jnp.concatenate([da, dg], axis=1)
    dh2 = _mm_blocked_rhs_t("mm_up_dx", du, wup_g)
    gw_up = _mm_wgrad_blocked("mm_up_dw", h2, du)
    if fused:
        part_up, part_down = gw_up, gw_down.reshape(N_DEV, FF_BLK, D_MODEL)
        (dx1, dmix, dshift2, dscale2, gnorm2, dgate1), (sib_up, sib_down) = _norm_bwd(
            "norm2_bwd", dh2, x1, rstd2, norm2_w, scale2, dout, mix=mix, gate=gate1,
            fused=_FusedCopies("sibling", [part_up, part_down]), fused_arrays=[part_up, part_down])
    else:
        dx1, dmix, dshift2, dscale2, gnorm2, dgate1 = _norm_bwd(
            "norm2_bwd", dh2, x1, rstd2, norm2_w, scale2, dout, mix=mix, gate=gate1)
    gw_out = _mm_plain("mm_out_dw", mixin, dmix, TN, 512, 1024, BF16)
    if fused:
        part_out = gw_out.reshape(N_DEV, OUT_BLK, D_MODEL)
        dmixin, (sib_out,) = _mm_plain("mm_out_dx", dmix, wout_full, NT, 512, 1024, F32,
                                       fused=_FusedCopies("sibling", [part_out]), fused_arrays=[part_out])
        cs_up = _pair_sum("grad_pair_sum_up", part_up, sib_up, core)
        cs_out = _pair_sum("grad_pair_sum_out", part_out, sib_out, core)
        cs_down = _pair_sum("grad_pair_sum_down", part_down, sib_down, core)
        (dhq, dhf, dhi, dhg, glog, ghg), (fc_up, fc_out) = _hgrn_bwd(
            proj, lb_logits, hg_norm_w, o_pre, dmixin, _FusedCopies("chips", [cs_up, cs_out]), [cs_up, cs_out])
        (dqn, dkn, dvv), (fc_down,) = _attn_bwd(qn, kn, vb, att_o, lse, dmixin,
                                                _FusedCopies("chips", [cs_down]), [cs_down])
    else:
        dmixin = _mm_plain("mm_out_dx", dmix, wout_full, NT, 512, 1024, F32)
        (dhq, dhf, dhi, dhg, glog, ghg), _ = _hgrn_bwd(proj, lb_logits, hg_norm_w, o_pre, dmixin)
        (dqn, dkn, dvv), _ = _attn_bwd(qn, kn, vb, att_o, lse, dmixin)
    daq, dak, dav, gqw, gkw = _qk_bwd(proj, q_norm_w, k_norm_w, dqn, dkn, dvv)
    dproj = jnp.concatenate([dhq, dhf, dhi, dhg, daq, dak, dav], axis=1)
    gw_in = _mm_wgrad_blocked("mm_in_dw", h, dproj)
    if fused:
        from_sibling, = _exchange_sibling("grad_exchange_sibling_b", [gw_in])
        cs_in = _pair_sum("grad_pair_sum_in", gw_in, from_sibling, core)
        dh, (fc_in,) = _mm_blocked_rhs_t("mm_in_dx", dproj, win_g, fused=_FusedCopies("chips", [cs_in]),
                                         fused_arrays=[cs_in])
        large = [(cs_in, fc_in), (cs_out, fc_out), (cs_up, fc_up), (cs_down, fc_down)]
    else:
        dh = _mm_blocked_rhs_t("mm_in_dx", dproj, win_g)
        large = [gw_in, gw_out, gw_up, gw_down]
    grad_x, dshift1, dscale1, gnorm1 = _norm_bwd("norm1_bwd", dh, xs, rstd1, norm1_w, scale1, dx1)
    gmod = jnp.concatenate([dshift1, dscale1, dgate1, dshift2, dscale2, dgate2], axis=1)
    return (loss_v, grad_x, gmod, gnorm1, gnorm2, glog, ghg, gqw, gkw, gconv_b, gconv_w, *large)


def kernel(x, c, w_ada, b_ada, norm1_w, w_in, lb_logits, hg_norm_w, q_norm_w, k_norm_w, w_out, norm2_w, w_up, conv_w, conv_b, w_down, loss_target, m_w_ada, m_b_ada, m_norm1_w, m_w_in, m_lb_logits, m_hg_norm_w, m_q_norm_w, m_k_norm_w, m_w_out, m_norm2_w, m_w_up, m_conv_w, m_conv_b, m_w_down, v_w_ada, v_b_ada, v_norm1_w, v_w_in, v_lb_logits, v_hg_norm_w, v_q_norm_w, v_k_norm_w, v_w_out, v_norm2_w, v_w_up, v_conv_w, v_conv_b, v_w_down):
    ix, iy, ic = lax.axis_index("x"), lax.axis_index("y"), lax.axis_index("c")
    me = 4 * ix + 2 * iy + ic
    my_chip = 2 * ix + iy

    xs = x[0]
    tgt = loss_target[0]

    win_g, = _allgather_weights([w_in[0].astype(BF16)])

    c_all = _allgather_vmem(c.reshape(8, D_MODEL // 8), "allgather_c").reshape(N_DEV, D_MODEL)
    b_blk = lax.dynamic_slice_in_dim(b_ada, me * ADA_BLK, ADA_BLK, axis=1)
    mod_cols = _ada_fwd(c_all, w_ada[0], b_blk)
    mod_all = _allgather_vmem(mod_cols, "allgather_mod").reshape(N_DEV, N_DEV, ADA_BLK)
    mod = lax.dynamic_index_in_dim(mod_all, me, axis=1, keepdims=False).reshape(6, 1, D_MODEL)

    conv_w_all = _allgather_vmem(_pack_rows([conv_w[0]], 24), "allgather_conv_w").reshape(N_DEV, 24 * 128)
    conv_w_full = conv_w_all[:, :3 * FF_BLK].reshape(N_DEV, 3, FF_BLK).transpose(1, 0, 2).reshape(3, D_FF)

    (loss_v, grad_x, gmod, gnorm1, gnorm2, glog, ghg, gqw, gkw, gconv_b, gconv_w,
     rs_in, rs_out, rs_up, rs_down) = _device_step(
        xs, tgt, mod, norm1_w, norm2_w, lb_logits, hg_norm_w, q_norm_w, k_norm_w, conv_w_full, conv_b,
        win_g, w_out[0].astype(BF16), w_up[0].astype(BF16), w_down[0].astype(BF16),
        core=jnp.reshape(ic, (1,)).astype(jnp.int32))
    loss = lax.psum(loss_v[0, 0], AXES)

    small_shapes = [(1, 6 * D_MODEL), (1, D_MODEL), (1, D_MODEL), (2, HEADS * HEAD_DIM), (1, HEAD_DIM),
                    (1, HEAD_DIM), (1, HEAD_DIM), (1, D_FF), (3, D_FF)]
    small = [gmod, gnorm1, gnorm2, glog, ghg, gqw, gkw, gconv_b, gconv_w]
    n_small = sum(a.size for a in small)
    rows = -(-n_small // 1024) * 8
    gathered = _allgather_vmem(_pack_rows(small, rows), "allgather_small").reshape(N_DEV, rows, 128)
    summed = _sum_devices(gathered).reshape(-1)
    (g_b_ada, g_norm1, g_norm2, g_lb, g_hg, g_q, g_k, g_conv_b, g_conv_w_full) = _unpack(summed, small_shapes)
    g_conv_w = lax.dynamic_slice_in_dim(g_conv_w_full, me * FF_BLK, FF_BLK, axis=1)

    gmod_all = gathered[:, :6 * D_MODEL // 128, :].reshape(N_DEV, 6 * D_MODEL)
    gmod_cols = lax.dynamic_slice_in_dim(gmod_all, me * ADA_BLK, ADA_BLK, axis=1)
    g_w_ada_raw = _ada_wgrad(c_all, gmod_cols)

    chip = jnp.reshape(my_chip, (1,)).astype(jnp.int32)

    def big_update(name, w, m, v, rs, tr):
        chip_sums, received = rs
        return _adamw_reduced(name, w[0], m[0], v[0], chip_sums, received, chip, tr)

    r_in = big_update("adamw_w_in", w_in, m_w_in, v_w_in, rs_in, 256)
    r_out = big_update("adamw_w_out", w_out, m_w_out, v_w_out, rs_out, 128)
    r_up = big_update("adamw_w_up", w_up, m_w_up, v_w_up, rs_up, 256)
    r_down = big_update("adamw_w_down", w_down, m_w_down, v_w_down, rs_down, 176)
    r_ada = _adamw("adamw_w_ada", w_ada[0], m_w_ada[0], v_w_ada[0], [g_w_ada_raw], tr=256)
    r_convw = _adamw("adamw_conv_w", conv_w[0], m_conv_w[0], v_conv_w[0], [g_conv_w])

    rep_shapes = [(1, 6 * D_MODEL), (1, D_MODEL), (1, D_MODEL), (2, HEADS * HEAD_DIM), (1, HEAD_DIM),
                  (1, HEAD_DIM), (1, HEAD_DIM), (1, D_FF)]
    rep_rows = -(-sum(a * b for a, b in rep_shapes) // 1024) * 8
    pack = lambda arrs: _pack_rows(arrs, rep_rows)
    rep = _adamw("adamw_small",
                 pack([b_ada, norm1_w, norm2_w, lb_logits, hg_norm_w, q_norm_w, k_norm_w, conv_b]),
                 pack([m_b_ada, m_norm1_w, m_norm2_w, m_lb_logits, m_hg_norm_w, m_q_norm_w, m_k_norm_w, m_conv_b]),
                 pack([v_b_ada, v_norm1_w, v_norm2_w, v_lb_logits, v_hg_norm_w, v_q_norm_w, v_k_norm_w, v_conv_b]),
                 [pack([g_b_ada, g_norm1, g_norm2, g_lb, g_hg, g_q, g_k, g_conv_b])])
    rep = [_unpack(r.reshape(-1), rep_shapes) for r in rep]

    def big(r):
        return [a[None] for a in r]

    order = {"w_ada": big(r_ada), "b_ada": [r[0] for r in rep], "norm1_w": [r[1] for r in rep],
             "w_in": big(r_in), "lb_logits": [r[3] for r in rep], "hg_norm_w": [r[4] for r in rep],
             "q_norm_w": [r[5] for r in rep], "k_norm_w": [r[6] for r in rep], "w_out": big(r_out),
             "norm2_w": [r[2] for r in rep], "w_up": big(r_up), "conv_w": big(r_convw),
             "conv_b": [r[7] for r in rep], "w_down": big(r_down)}
    names = ["w_ada", "b_ada", "norm1_w", "w_in", "lb_logits", "hg_norm_w", "q_norm_w", "k_norm_w", "w_out",
             "norm2_w", "w_up", "conv_w", "conv_b", "w_down"]
    outs = [loss, grad_x[None]]
    for kind in range(4):
        outs += [order[n][kind] for n in names]
    return tuple(outs)
```

```python
import functools

import jax
import jax.numpy as jnp
from jax import lax
from jax.experimental import pallas as pl
from jax.experimental.pallas import tpu as pltpu

F32 = jnp.float32
BF16 = jnp.bfloat16

N_DEV = 8
SEQ = 2048
D_MODEL = 2048
HEADS = 8
HEAD_DIM = 128
IN_COLS = 7168
IN_BLK = IN_COLS // N_DEV
D_FF = 5632
UP_BLK = 2 * D_FF // N_DEV
FF_BLK = D_FF // N_DEV
ADA_BLK = 6 * D_MODEL // N_DEV
OUT_BLK = D_MODEL // N_DEV
EPS = 1e-6
CHUNK = 16
ROW_TILE = 256
V7X_VMEM_LIMIT = 56 * 1024 * 1024

ADAM_LR = 0.001
ADAM_B1 = 0.9
ADAM_B2 = 0.999
ADAM_EPS = 1e-08
ADAM_WD = 0.01
ADAM_STEP = 10

NN = (((1,), (0,)), ((), ()))
NT = (((1,), (1,)), ((), ()))
TN = (((0,), (0,)), ((), ()))
MESH = pl.DeviceIdType.MESH
AXES = ("x", "y", "c")


def _params(sem=None, vmem=V7X_VMEM_LIMIT):
    return pltpu.CompilerParams(dimension_semantics=sem, vmem_limit_bytes=vmem)


def _sigmoid(x):
    return 1.0 / (1.0 + jnp.exp(-x))


def _dsilu(x, s):
    return s * (1.0 + x * (1.0 - s))


def _lane_sum(x, ones_bf16):
    return jnp.dot(x.astype(BF16), ones_bf16, preferred_element_type=F32)


def _mesh_pos():
    return lax.axis_index("x"), lax.axis_index("y"), lax.axis_index("c")


def _allgather_vmem(x_blk, name):
    m_per, n = x_blk.shape

    def body(x_ref, out_ref, send_sems, recv_sems, local_sem):
        x, y, c = _mesh_pos()
        me, sibling = (x, y, c), (x, y, 1 - c)
        chips = [(1 - x, y), (x, 1 - y), (1 - x, 1 - y)]

        def rows(px, py, pc):
            return out_ref.at[pl.ds((4 * px + 2 * py + pc) * m_per, m_per), :]

        def copy(k, block, to, src=None):
            return pltpu.make_async_remote_copy(
                src_ref=rows(*block) if src is None else src, dst_ref=rows(*block),
                send_sem=send_sems.at[k], recv_sem=recv_sems.at[k], device_id=to, device_id_type=MESH)

        mine = pltpu.make_async_copy(x_ref, rows(*me), local_sem)
        mine.start()
        first = [copy(0, me, sibling, src=x_ref)]
        first += [copy(1 + j, me, (*chip, c), src=x_ref) for j, chip in enumerate(chips)]
        for cp in first:
            cp.start()
        passed = [copy(4 + j, (*chip, c), sibling) for j, chip in enumerate(chips)]
        for j, chip in enumerate(chips):
            copy(1 + j, (*chip, c), me).wait_recv()
            passed[j].start()
        copy(0, sibling, me).wait_recv()
        for j, chip in enumerate(chips):
            copy(4 + j, (*chip, 1 - c), me).wait_recv()
        for cp in first + passed:
            cp.wait_send()
        mine.wait()

    return pl.pallas_call(
        body, name=name,
        out_shape=jax.ShapeDtypeStruct((N_DEV * m_per, n), x_blk.dtype),
        in_specs=[pl.BlockSpec(memory_space=pltpu.VMEM)],
        out_specs=pl.BlockSpec(memory_space=pltpu.VMEM),
        scratch_shapes=[pltpu.SemaphoreType.DMA((7,)), pltpu.SemaphoreType.DMA((7,)), pltpu.SemaphoreType.DMA],
    )(x_blk)


def _flip(v, bit):
    return v + bit - 2 * v * bit


def _relay_chips(x, y, c):
    return (_flip(x, 1 - c), _flip(y, c)), (_flip(x, c), _flip(y, 1 - c))


GATHER_PARTS = 4


def _allgather_weights(blocks):
    n_arr = len(blocks)
    parts = GATHER_PARTS

    def body(*refs):
        ins, outs = refs[:n_arr], refs[n_arr:2 * n_arr]
        send_sems, recv_sems, local_sems = refs[2 * n_arr:]
        x, y, c = _mesh_pos()
        me, sibling = (x, y, c), (x, y, 1 - c)
        near = [(1 - x, y), (x, 1 - y)]
        chips = near + [(1 - x, 1 - y)]
        relay_from, relay_to = _relay_chips(x, y, c)

        def rows(a, p):
            hr = ins[a].shape[0] // parts
            return pl.ds(p * hr, hr)

        def slot(a, pos, p):
            return outs[a].at[4 * pos[0] + 2 * pos[1] + pos[2], rows(a, p)]

        def copy(a, k, p, src, lands, to):
            return pltpu.make_async_remote_copy(
                src_ref=src, dst_ref=slot(a, lands, p), send_sem=send_sems.at[a, k, p], recv_sem=recv_sems.at[a, k, p],
                device_id=to, device_id_type=MESH)

        sent = []
        local = [pltpu.make_async_copy(ins[a], outs[a].at[4 * x + 2 * y + c], local_sems.at[a]) for a in range(n_arr)]
        for cp in local:
            cp.start()
        for p in range(parts):
            for a in range(n_arr):
                own = ins[a].at[rows(a, p)]
                sent.append(copy(a, 0, p, own, me, sibling))
                sent += [copy(a, 1 + j, p, own, me, (*chip, c)) for j, chip in enumerate(near)]
        for cp in sent:
            cp.start()

        def start(cp):
            cp.start()
            sent.append(cp)

        for p in range(parts):
            for a in range(n_arr):
                for j, chip in enumerate(near):
                    copy(a, 1 + j, p, ins[a].at[rows(a, p)], (*chip, c), me).wait_recv()
                    start(copy(a, 4 + j, p, slot(a, (*chip, c), p), (*chip, c), sibling))
                start(copy(a, 3, p, slot(a, (*relay_from, c), p), (*relay_from, c), (*relay_to, c)))
        for p in range(parts):
            for a in range(n_arr):
                copy(a, 3, p, ins[a].at[rows(a, p)], (*chips[2], c), me).wait_recv()
                start(copy(a, 6, p, slot(a, (*chips[2], c), p), (*chips[2], c), sibling))
        for p in range(parts):
            for a in range(n_arr):
                copy(a, 0, p, ins[a].at[rows(a, p)], sibling, me).wait_recv()
                for j, chip in enumerate(chips):
                    copy(a, 4 + j, p, ins[a].at[rows(a, p)], (*chip, 1 - c), me).wait_recv()
        for cp in sent:
            cp.wait_send()
        for cp in local:
            cp.wait()

    return pl.pallas_call(
        body, name="allgather_weights",
        out_shape=[jax.ShapeDtypeStruct((N_DEV,) + b.shape, b.dtype) for b in blocks],
        in_specs=[pl.BlockSpec(memory_space=pltpu.HBM)] * n_arr, out_specs=[pl.BlockSpec(memory_space=pltpu.HBM)] * n_arr,
        scratch_shapes=[pltpu.SemaphoreType.DMA((n_arr, 7, parts)), pltpu.SemaphoreType.DMA((n_arr, 7, parts)),
                        pltpu.SemaphoreType.DMA((n_arr,))],
    )(*blocks)


HBM_SPEC = pl.BlockSpec(memory_space=pltpu.HBM)


class _FusedCopies:
    def __init__(self, kind, arrays, peers=(0, 1, 2, 3)):
        self.kind = kind
        self.peers = peers
        n = len(arrays)
        self.n = n
        self.n_in = n
        self.aliases = {}
        if kind == "gather":
            self.out_shape = [jax.ShapeDtypeStruct((N_DEV,) + a.shape, a.dtype) for a in arrays]
            self.scratch_shapes = [pltpu.SemaphoreType.DMA((n, 4)), pltpu.SemaphoreType.DMA((n, 4)),
                                   pltpu.SemaphoreType.DMA((n,))]
        elif kind == "relay":
            self.out_shape = [jax.ShapeDtypeStruct(a.shape, a.dtype) for a in arrays]
            self.scratch_shapes = [pltpu.SemaphoreType.DMA((n,)), pltpu.SemaphoreType.DMA((n,))]
            self.aliases = {a: a for a in range(n)}
        elif kind == "forward":
            self.out_shape = [jax.ShapeDtypeStruct(a.shape, a.dtype) for a in arrays]
            self.scratch_shapes = [pltpu.SemaphoreType.DMA((n, 3)), pltpu.SemaphoreType.DMA((n, 3))]
            self.aliases = {a: a for a in range(n)}
        elif kind == "sibling":
            self.out_shape = [jax.ShapeDtypeStruct((4,) + a.shape[1:], a.dtype) for a in arrays]
            self.scratch_shapes = [pltpu.SemaphoreType.DMA((n, 4)), pltpu.SemaphoreType.DMA((n, 4))]
        else:
            self.out_shape = [jax.ShapeDtypeStruct((3,) + a.shape[1:], a.dtype) for a in arrays]
            self.scratch_shapes = [pltpu.SemaphoreType.DMA((n, 3)), pltpu.SemaphoreType.DMA((n, 3))]
        self.in_specs = [HBM_SPEC] * self.n_in
        self.out_specs = [HBM_SPEC] * n
        self.n_scratch = len(self.scratch_shapes)

    def copies(self, ins, outs, sems):
        x, y, c = _mesh_pos()
        chips = [(1 - x, y), (x, 1 - y), (1 - x, 1 - y)]
        sibling = (x, y, 1 - c)
        starts, waits = [], []
        if self.kind == "gather":
            send_sems, recv_sems, local_sems = sems
            me = (x, y, c)
            peers = [sibling] + [(px, py, c) for px, py in chips]

            def slot(a, pos):
                return outs[a].at[4 * pos[0] + 2 * pos[1] + pos[2]]

            def remote(a, k, lands_from):
                return pltpu.make_async_remote_copy(
                    src_ref=ins[a], dst_ref=slot(a, lands_from), send_sem=send_sems.at[a, k],
                    recv_sem=recv_sems.at[a, k], device_id=peers[k], device_id_type=MESH)

            for a in range(self.n):
                local = pltpu.make_async_copy(ins[a], slot(a, me), local_sems.at[a])
                starts.append(local)
                waits.append(local)
                for k in self.peers:
                    starts.append(remote(a, k, me))
                    waits.append(remote(a, k, peers[k]))
        elif self.kind == "relay":
            send_sems, recv_sems = sems
            relay_from, relay_to = _relay_chips(x, y, c)

            def relayed(a, lands):
                return pltpu.make_async_remote_copy(
                    src_ref=ins[a].at[4 * relay_from[0] + 2 * relay_from[1] + c],
                    dst_ref=outs[a].at[4 * lands[0] + 2 * lands[1] + c], send_sem=send_sems.at[a],
                    recv_sem=recv_sems.at[a], device_id=(*relay_to, c), device_id_type=MESH)

            for a in range(self.n):
                starts.append(relayed(a, relay_from))
                waits.append(relayed(a, chips[2]))
        elif self.kind == "forward":
            send_sems, recv_sems = sems

            def passed_on(a, j, pc_src, pc_dst):
                px, py = chips[j]
                return pltpu.make_async_remote_copy(
                    src_ref=ins[a].at[4 * px + 2 * py + pc_src], dst_ref=outs[a].at[4 * px + 2 * py + pc_dst],
                    send_sem=send_sems.at[a, j], recv_sem=recv_sems.at[a, j], device_id=sibling, device_id_type=MESH)

            for a in range(self.n):
                for j in range(3):
                    starts.append(passed_on(a, j, c, c))
                    waits.append(passed_on(a, j, c, 1 - c))
        elif self.kind == "sibling":
            send_sems, recv_sems = sems
            for a in range(self.n):
                for q in range(4):
                    cp = pltpu.make_async_remote_copy(
                        src_ref=ins[a].at[2 * q + 1 - c], dst_ref=outs[a].at[q], send_sem=send_sems.at[a, q],
                        recv_sem=recv_sems.at[a, q], device_id=sibling, device_id_type=MESH)
                    starts.append(cp)
                    waits.append(cp)
        else:
            send_sems, recv_sems = sems
            for a in range(self.n):
                for j, (px, py) in enumerate(chips):
                    cp = pltpu.make_async_remote_copy(
                        src_ref=ins[a].at[2 * px + py], dst_ref=outs[a].at[j], send_sem=send_sems.at[a, j],
                        recv_sem=recv_sems.at[a, j], device_id=(px, py, c), device_id_type=MESH)
                    starts.append(cp)
                    waits.append(cp)
        return starts, waits


def _host_body(body, n_in, n_out, fused, first_last):
    if fused is None:
        return body
    n_fin, n_fout = fused.n_in, fused.n

    def wrapped(*refs):
        core_in, f_in = refs[:n_in], refs[n_in:n_in + n_fin]
        core_out = refs[n_in + n_fin:n_in + n_fin + n_out]
        f_out = refs[n_in + n_fin + n_out:n_in + n_fin + n_out + n_fout]
        rest = refs[n_in + n_fin + n_out + n_fout:]
        core_scratch, f_sems = rest[:len(rest) - fused.n_scratch], rest[len(rest) - fused.n_scratch:]
        starts, waits = fused.copies(f_in, f_out, f_sems)
        first, last = first_last()

        @pl.when(first)
        def _():
            for cp in starts:
                cp.start()

        body(*core_in, *core_out, *core_scratch)

        @pl.when(last)
        def _():
            for cp in waits:
                cp.wait()

    return wrapped


def _host_call(body, n_in, n_out, fused, first_last, *, name, grid, in_specs, out_specs, out_shape, scratch_shapes,
               sem, operands):
    aliases = {}
    if fused is not None:
        in_specs = list(in_specs) + fused.in_specs
        out_specs = list(out_specs) + fused.out_specs
        out_shape = list(out_shape) + fused.out_shape
        scratch_shapes = list(scratch_shapes) + fused.scratch_shapes
        sem = tuple("arbitrary" for _ in sem)
        aliases = {n_in + fi: n_out + fo for fi, fo in fused.aliases.items()}
    res = pl.pallas_call(_host_body(body, n_in, n_out, fused, first_last), name=name, grid=grid, in_specs=in_specs,
                         out_specs=out_specs, out_shape=out_shape, scratch_shapes=scratch_shapes,
                         input_output_aliases=aliases, compiler_params=_params(sem))(*operands)
    return list(res[:n_out]), list(res[n_out:])


def _forward_to_sibling(name, gathered):
    n_arr = len(gathered)

    def body(*refs):
        ins, outs = refs[:n_arr], refs[n_arr:2 * n_arr]
        send_sems, recv_sems = refs[2 * n_arr:]
        x, y, c = _mesh_pos()
        chips = [(1 - x, y), (x, 1 - y), (1 - x, 1 - y)]

        def copy(a, j, pc):
            px, py = chips[j]
            s = 4 * px + 2 * py + pc
            return pltpu.make_async_remote_copy(
                src_ref=ins[a].at[s], dst_ref=outs[a].at[s], send_sem=send_sems.at[a, j], recv_sem=recv_sems.at[a, j],
                device_id=(x, y, 1 - c), device_id_type=MESH)

        for a in range(n_arr):
            for j in range(3):
                copy(a, j, c).start()
        for a in range(n_arr):
            for j in range(3):
                copy(a, j, 1 - c).wait_recv()
                copy(a, j, c).wait_send()

    return pl.pallas_call(
        body, name=name,
        out_shape=[jax.ShapeDtypeStruct(g.shape, g.dtype) for g in gathered],
        in_specs=[HBM_SPEC] * n_arr, out_specs=[HBM_SPEC] * n_arr,
        input_output_aliases={a: a for a in range(n_arr)},
        scratch_shapes=[pltpu.SemaphoreType.DMA((n_arr, 3)), pltpu.SemaphoreType.DMA((n_arr, 3))],
    )(*gathered)


def _exchange_sibling(name, partials):
    n_arr = len(partials)

    def body(*refs):
        ins, outs = refs[:n_arr], refs[n_arr:2 * n_arr]
        send_sems, recv_sems = refs[2 * n_arr:]
        x, y, c = _mesh_pos()
        copies = [pltpu.make_async_remote_copy(
            src_ref=ins[a].at[2 * q + 1 - c], dst_ref=outs[a].at[q], send_sem=send_sems.at[a, q],
            recv_sem=recv_sems.at[a, q], device_id=(x, y, 1 - c), device_id_type=MESH)
            for a in range(n_arr) for q in range(4)]
        for cp in copies:
            cp.start()
        for cp in copies:
            cp.wait_recv()
        for cp in copies:
            cp.wait_send()

    return pl.pallas_call(
        body, name=name,
        out_shape=[jax.ShapeDtypeStruct((4,) + p.shape[1:], p.dtype) for p in partials],
        in_specs=[HBM_SPEC] * n_arr, out_specs=[HBM_SPEC] * n_arr,
        scratch_shapes=[pltpu.SemaphoreType.DMA((n_arr, 4)), pltpu.SemaphoreType.DMA((n_arr, 4))],
    )(*partials)


def _exchange_chips(name, chip_sums):
    n_arr = len(chip_sums)

    def body(*refs):
        ins, outs = refs[:n_arr], refs[n_arr:2 * n_arr]
        send_sems, recv_sems = refs[2 * n_arr:]
        x, y, c = _mesh_pos()
        chips = [(1 - x, y), (x, 1 - y), (1 - x, 1 - y)]
        copies = []
        for a in range(n_arr):
            for j, (px, py) in enumerate(chips):
                copies.append(pltpu.make_async_remote_copy(
                    src_ref=ins[a].at[2 * px + py], dst_ref=outs[a].at[j],
                    send_sem=send_sems.at[a, j], recv_sem=recv_sems.at[a, j],
                    device_id=(px, py, c), device_id_type=MESH))
        for cp in copies:
            cp.start()
        for cp in copies:
            cp.wait_recv()
        for cp in copies:
            cp.wait_send()

    hbm = pl.BlockSpec(memory_space=pltpu.HBM)
    return pl.pallas_call(
        body, name=name,
        out_shape=[jax.ShapeDtypeStruct((3,) + p.shape[1:], p.dtype) for p in chip_sums],
        in_specs=[hbm] * n_arr, out_specs=[hbm] * n_arr,
        scratch_shapes=[pltpu.SemaphoreType.DMA((n_arr, 3)), pltpu.SemaphoreType.DMA((n_arr, 3))],
    )(*chip_sums)


def _matmul(name, a, b, dims, grid, a_spec, b_spec, o_spec, out_shape, acc_axis=None, fused=None, fused_arrays=()):
    def body(a_ref, b_ref, o_ref):
        r = lax.dot_general(a_ref[...], b_ref[...], dims, preferred_element_type=F32)
        if acc_axis is None:
            o_ref[...] = r.astype(o_ref.dtype)
        else:
            k = pl.program_id(acc_axis)

            @pl.when(k == 0)
            def _():
                o_ref[...] = r

            @pl.when(k > 0)
            def _():
                o_ref[...] += r

    sem = tuple("arbitrary" if i == acc_axis else "parallel" for i in range(len(grid)))
    if fused is None:
        return pl.pallas_call(body, name=name, grid=grid, in_specs=[a_spec, b_spec], out_specs=o_spec,
                              out_shape=out_shape, compiler_params=_params(sem))(a, b)

    def first_last():
        first = last = None
        for ax, n in enumerate(grid):
            f, l = pl.program_id(ax) == 0, pl.program_id(ax) == n - 1
            first, last = (f, l) if first is None else (first & f, last & l)
        return first, last

    (out,), extra = _host_call(body, 2, 1, fused, first_last, name=name, grid=grid, in_specs=[a_spec, b_spec],
                               out_specs=[o_spec], out_shape=[out_shape], scratch_shapes=[], sem=sem,
                               operands=[a, b] + list(fused_arrays))
    return out, extra


def _mm_blocked_rhs(name, a, w_g, tm=512, fused=None, fused_arrays=()):
    m, k = a.shape
    nb = w_g.shape[2]
    return _matmul(name, a, w_g, NN, (N_DEV, m // tm),
                   pl.BlockSpec((tm, k), lambda j, i: (i, 0)),
                   pl.BlockSpec((None, k, nb), lambda j, i: (j, 0, 0)),
                   pl.BlockSpec((tm, nb), lambda j, i: (i, j)),
                   jax.ShapeDtypeStruct((m, N_DEV * nb), F32), fused=fused, fused_arrays=fused_arrays)


def _mm_blocked_rhs_t(name, a, w_g, tm=512, fused=None, fused_arrays=()):
    m = a.shape[0]
    n, nb = w_g.shape[1], w_g.shape[2]
    return _matmul(name, a, w_g, NT, (m // tm, N_DEV),
                   pl.BlockSpec((tm, nb), lambda i, j: (i, j)),
                   pl.BlockSpec((None, n, nb), lambda i, j: (j, 0, 0)),
                   pl.BlockSpec((tm, n), lambda i, j: (i, 0)),
                   jax.ShapeDtypeStruct((m, n), F32), acc_axis=1, fused=fused, fused_arrays=fused_arrays)


def _mm_wgrad_blocked(name, act, dcols, tk=512):
    t, k = act.shape
    nb = dcols.shape[1] // N_DEV
    return _matmul(name, act, dcols, TN, (N_DEV, k // tk),
                   pl.BlockSpec((t, tk), lambda j, i: (0, i)),
                   pl.BlockSpec((t, nb), lambda j, i: (0, j)),
                   pl.BlockSpec((None, tk, nb), lambda j, i: (j, i, 0)),
                   jax.ShapeDtypeStruct((N_DEV, k, nb), BF16))


def _mm_plain(name, a, b, dims, tm, tn, out_dtype, fused=None, fused_arrays=()):
    if dims == NN:
        (m, k), n = a.shape, b.shape[1]
        a_spec = pl.BlockSpec((tm, k), lambda i, j: (i, 0))
        b_spec = pl.BlockSpec((k, tn), lambda i, j: (0, j))
    elif dims == NT:
        (m, k), n = a.shape, b.shape[0]
        a_spec = pl.BlockSpec((tm, k), lambda i, j: (i, 0))
        b_spec = pl.BlockSpec((tn, k), lambda i, j: (j, 0))
    else:
        (k, m), n = a.shape, b.shape[1]
        a_spec = pl.BlockSpec((k, tm), lambda i, j: (0, i))
        b_spec = pl.BlockSpec((k, tn), lambda i, j: (0, j))
    return _matmul(name, a, b, dims, (m // tm, n // tn), a_spec, b_spec,
                   pl.BlockSpec((tm, tn), lambda i, j: (i, j)), jax.ShapeDtypeStruct((m, n), out_dtype),
                   fused=fused, fused_arrays=fused_arrays)


def _ada_fwd(c_all, w_ada_blk, b_blk):
    def body(c_ref, w_ref, b_ref, o_ref):
        cv = c_ref[...]
        o_ref[...] = jnp.dot(cv * _sigmoid(cv), w_ref[...], preferred_element_type=F32) + b_ref[...]

    tn = 512
    return pl.pallas_call(
        body, name="ada_fwd", grid=(ADA_BLK // tn,),
        in_specs=[pl.BlockSpec((N_DEV, D_MODEL), lambda j: (0, 0)),
                  pl.BlockSpec((D_MODEL, tn), lambda j: (0, j)),
                  pl.BlockSpec((1, tn), lambda j: (0, j))],
        out_specs=pl.BlockSpec((N_DEV, tn), lambda j: (0, j)),
        out_shape=jax.ShapeDtypeStruct((N_DEV, ADA_BLK), F32),
        compiler_params=_params(("parallel",)))(c_all, w_ada_blk, b_blk)


def _ada_wgrad(c_all, gmod_cols):
    def body(c_ref, g_ref, o_ref):
        cv = c_ref[...]
        o_ref[...] = lax.dot_general(cv * _sigmoid(cv), g_ref[...], TN, preferred_element_type=F32)

    tk = 512
    return pl.pallas_call(
        body, name="ada_wgrad", grid=(D_MODEL // tk,),
        in_specs=[pl.BlockSpec((N_DEV, tk), lambda i: (0, i)),
                  pl.BlockSpec((N_DEV, ADA_BLK), lambda i: (0, 0))],
        out_specs=pl.BlockSpec((tk, ADA_BLK), lambda i: (i, 0)),
        out_shape=jax.ShapeDtypeStruct((D_MODEL, ADA_BLK), F32),
        compiler_params=_params(("parallel",)))(c_all, gmod_cols)


def _row_spec(cols=D_MODEL):
    return pl.BlockSpec((ROW_TILE, cols), lambda i: (i, 0))


def _vec_spec(cols=D_MODEL):
    return pl.BlockSpec((1, cols), lambda i: (0, 0))


def _norm_fwd(name, x, w, scale, shift, resid=None, gate=None):
    has_res = resid is not None

    def body(*refs):
        if has_res:
            x_ref, r_ref, g_ref, w_ref, sc_ref, sh_ref, xr_ref, h_ref, rs_ref = refs
            xr = x_ref[...] + g_ref[...] * r_ref[...]
            xr_ref[...] = xr
        else:
            x_ref, w_ref, sc_ref, sh_ref, h_ref, rs_ref = refs
            xr = x_ref[...]
        rs = lax.rsqrt(jnp.mean(xr * xr, axis=-1, keepdims=True) + EPS)
        h = (xr * rs) * w_ref[...] * (1.0 + sc_ref[...]) + sh_ref[...]
        h_ref[...] = h.astype(BF16)
        rs_ref[...] = rs

    s = x.shape[0]
    ins = [x] + ([resid, gate] if has_res else []) + [w, scale, shift]
    in_specs = [_row_spec()] + ([_row_spec(), _vec_spec()] if has_res else []) + [_vec_spec()] * 3
    outs = ([jax.ShapeDtypeStruct((s, D_MODEL), F32)] if has_res else []) + [
        jax.ShapeDtypeStruct((s, D_MODEL), BF16), jax.ShapeDtypeStruct((s, 1), F32)]
    out_specs = ([_row_spec()] if has_res else []) + [_row_spec(), pl.BlockSpec((ROW_TILE, 1), lambda i: (i, 0))]
    return pl.pallas_call(body, name=name, grid=(s // ROW_TILE,), in_specs=in_specs, out_specs=out_specs,
                          out_shape=outs, compiler_params=_params(("parallel",)))(*ins)


def _norm_bwd(name, dh, x, rstd, w, scale, dres, mix=None, gate=None, fused=None, fused_arrays=()):
    has_mix = mix is not None

    def body(*refs):
        if has_mix:
            (dh_ref, x_ref, rs_ref, w_ref, sc_ref, dr_ref, mix_ref, g_ref,
             dx_ref, dmix_ref, dsh_ref, dsc_ref, dw_ref, dg_ref) = refs
        else:
            dh_ref, x_ref, rs_ref, w_ref, sc_ref, dr_ref, dx_ref, dsh_ref, dsc_ref, dw_ref = refs
        i = pl.program_id(0)
        dhv = dh_ref[...]
        rs = rs_ref[...]
        xn = x_ref[...] * rs
        wv = w_ref[...]
        one_sc = 1.0 + sc_ref[...]
        dxn = dhv * wv * one_sc
        dx = dr_ref[...] + rs * (dxn - xn * jnp.mean(dxn * xn, axis=-1, keepdims=True))
        dx_ref[...] = dx
        sums = [(dsh_ref, dhv), (dsc_ref, dhv * xn * wv), (dw_ref, dhv * one_sc * xn)]
        if has_mix:
            dmix_ref[...] = (dx * g_ref[...]).astype(BF16)
            sums.append((dg_ref, dx * mix_ref[...]))

        @pl.when(i == 0)
        def _():
            for ref, _v in sums:
                ref[...] = jnp.zeros_like(ref)

        for ref, v in sums:
            ref[...] += jnp.sum(v, axis=0, keepdims=True)

    s = x.shape[0]
    ins = [dh, x, rstd, w, scale, dres] + ([mix, gate] if has_mix else [])
    in_specs = ([_row_spec(), _row_spec(), pl.BlockSpec((ROW_TILE, 1), lambda i: (i, 0)), _vec_spec(), _vec_spec(),
                 _row_spec()] + ([_row_spec(), _vec_spec()] if has_mix else []))
    vec = jax.ShapeDtypeStruct((1, D_MODEL), F32)
    outs = ([jax.ShapeDtypeStruct((s, D_MODEL), F32)] + ([jax.ShapeDtypeStruct((s, D_MODEL), BF16)] if has_mix else [])
            + [vec] * (4 if has_mix else 3))
    out_specs = [_row_spec()] + ([_row_spec()] if has_mix else []) + [_vec_spec()] * (4 if has_mix else 3)

    def first_last():
        i = pl.program_id(0)
        return i == 0, i == s // ROW_TILE - 1

    res, extra = _host_call(body, len(ins), len(outs), fused, first_last, name=name, grid=(s // ROW_TILE,),
                            in_specs=in_specs, out_specs=out_specs, out_shape=outs, scratch_shapes=[],
                            sem=("arbitrary",), operands=ins + list(fused_arrays))
    return res if fused is None else (res, extra)


def _loss_head(x1, ffn, gate2, target):
    def body(x_ref, f_ref, g_ref, t_ref, loss_ref, dout_ref, dffn_ref, dg_ref):
        i = pl.program_id(0)
        fv = f_ref[...]
        gv = g_ref[...]
        err = x_ref[...] + gv * fv - t_ref[...]
        dout = err * (1.0 / D_MODEL)
        dout_ref[...] = dout
        dffn_ref[...] = (dout * gv).astype(BF16)

        @pl.when(i == 0)
        def _():
            loss_ref[...] = jnp.zeros_like(loss_ref)
            dg_ref[...] = jnp.zeros_like(dg_ref)

        row = jnp.sum(err * err, axis=-1, keepdims=True) * (1.0 / D_MODEL)
        loss_ref[...] += jnp.broadcast_to(0.5 * jnp.sum(row, axis=0, keepdims=True), (1, 128))
        dg_ref[...] += jnp.sum(dout * fv, axis=0, keepdims=True)

    s = x1.shape[0]
    return pl.pallas_call(
        body, name="loss_head", grid=(s // ROW_TILE,),
        in_specs=[_row_spec(), _row_spec(), _vec_spec(), _row_spec()],
        out_specs=[pl.BlockSpec((1, 128), lambda i: (0, 0)), _row_spec(), _row_spec(), _vec_spec()],
        out_shape=[jax.ShapeDtypeStruct((1, 128), F32), jax.ShapeDtypeStruct((s, D_MODEL), F32),
                   jax.ShapeDtypeStruct((s, D_MODEL), BF16), jax.ShapeDtypeStruct((1, D_MODEL), F32)],
        compiler_params=_params(("arbitrary",)))(x1, ffn, gate2, target)


CONV_TILE = 512
N_CONV_TILES = D_FF // CONV_TILE


def _shift_rows(a, k, row):
    n = a.shape[0]
    if k > 0:
        return jnp.where(row >= k, pltpu.roll(a, k, 0), 0.0)
    return jnp.where(row < n + k, pltpu.roll(a, n + k, 0), 0.0)


def _conv_gate_fwd(u, conv_w, conv_b, fused=None, fused_arrays=()):
    s = u.shape[0]

    def body(a_ref, g_ref, w_ref, b_ref, y_ref):
        a = a_ref[...]
        w = w_ref[...]
        row = lax.broadcasted_iota(jnp.int32, a.shape, 0)
        ac = b_ref[...] + _shift_rows(a, 2, row) * w[0:1] + _shift_rows(a, 1, row) * w[1:2] + a * w[2:3]
        y_ref[...] = (ac * _sigmoid(ac) * g_ref[...]).astype(BF16)

    def first_last():
        i = pl.program_id(0)
        return i == 0, i == N_CONV_TILES - 1

    col = lambda off: pl.BlockSpec((s, CONV_TILE), lambda i: (0, i + off))
    (y,), extra = _host_call(
        body, 4, 1, fused, first_last, name="conv_gate_fwd", grid=(N_CONV_TILES,),
        in_specs=[col(0), col(N_CONV_TILES), pl.BlockSpec((3, CONV_TILE), lambda i: (0, i)),
                  pl.BlockSpec((1, CONV_TILE), lambda i: (0, i))],
        out_specs=[col(0)], out_shape=[jax.ShapeDtypeStruct((s, D_FF), BF16)], scratch_shapes=[], sem=("parallel",),
        operands=[u, u, conv_w, conv_b] + list(fused_arrays))
    return y if fused is None else (y, extra)


def _conv_gate_bwd(u, dy, conv_w, conv_b):
    s = u.shape[0]

    def body(a_ref, g_ref, dy_ref, w_ref, b_ref, da_ref, dg_ref, gw_ref, gb_ref):
        a = a_ref[...]
        w = w_ref[...]
        row = lax.broadcasted_iota(jnp.int32, a.shape, 0)
        a1 = _shift_rows(a, 1, row)
        a2 = _shift_rows(a, 2, row)
        ac = b_ref[...] + a2 * w[0:1] + a1 * w[1:2] + a * w[2:3]
        sg = _sigmoid(ac)
        dyv = dy_ref[...]
        dg_ref[...] = (dyv * (ac * sg)).astype(BF16)
        dac = dyv * g_ref[...] * _dsilu(ac, sg)
        gb_ref[...] = jnp.sum(dac, axis=0, keepdims=True)
        gw_ref[0:1, :] = jnp.sum(dac * a2, axis=0, keepdims=True)
        gw_ref[1:2, :] = jnp.sum(dac * a1, axis=0, keepdims=True)
        gw_ref[2:3, :] = jnp.sum(dac * a, axis=0, keepdims=True)
        da = dac * w[2:3] + _shift_rows(dac, -1, row) * w[1:2] + _shift_rows(dac, -2, row) * w[0:1]
        da_ref[...] = da.astype(BF16)

    col = lambda off: pl.BlockSpec((s, CONV_TILE), lambda i: (0, i + off))
    return pl.pallas_call(
        body, name="conv_gate_bwd", grid=(N_CONV_TILES,),
        in_specs=[col(0), col(N_CONV_TILES), col(0), pl.BlockSpec((3, CONV_TILE), lambda i: (0, i)),
                  pl.BlockSpec((1, CONV_TILE), lambda i: (0, i))],
        out_specs=[col(0), col(0), pl.BlockSpec((3, CONV_TILE), lambda i: (0, i)),
                   pl.BlockSpec((1, CONV_TILE), lambda i: (0, i))],
        out_shape=[jax.ShapeDtypeStruct((s, D_FF), BF16), jax.ShapeDtypeStruct((s, D_FF), BF16),
                   jax.ShapeDtypeStruct((3, D_FF), F32), jax.ShapeDtypeStruct((1, D_FF), F32)],
        compiler_params=_params(("parallel",)))(u, u, dy, conv_w, conv_b)


HG_TILE = 256
CHUNK_UNROLL = 8


def _unrolled_loop(n, body, init):
    def group(i, carry):
        for u in range(CHUNK_UNROLL):
            carry = body(i * CHUNK_UNROLL + u, carry)
        return carry

    return lax.fori_loop(0, n // CHUNK_UNROLL, group, init)


def _head_col(off):
    return pl.BlockSpec((SEQ, HEAD_DIM), lambda h: (0, h + off))


def _hgrn_gates(hq, hf, lb, pos):
    q = hq * _sigmoid(hq)
    sig = _sigmoid(hf)
    f = lb + (1.0 - lb) * sig
    gl = jnp.log(f)
    for sh in (1, 2, 4, 8):
        gl = gl + jnp.where(pos >= sh, pltpu.roll(gl, sh, 0), 0.0)
    return q, sig, f, 1.0 - f, gl


def _lower_bound(lbl):
    return 1.0 / (1.0 + jnp.exp(lbl[1:2, :] - lbl[0:1, :]))


def _head_first_last():
    h = pl.program_id(0)
    return h == 0, h == HEADS - 1


def _hgrn_fwd_serial(proj, lb_logits, norm_w, fused=None, fused_arrays=()):
    n_tiles = SEQ // HG_TILE
    n_chunks = SEQ // CHUNK
    fused_arrays = list(fused_arrays)

    def body(hq_ref, hf_ref, hi_ref, hg_ref, lbl_ref, nw_ref, aout_ref, opre_ref, q_s, k_s, gl_s):
        lb = _lower_bound(lbl_ref[...])
        ones = jnp.ones((HEAD_DIM, HEAD_DIM), BF16)
        pos = lax.broadcasted_iota(jnp.int32, (HG_TILE, HEAD_DIM), 0) % CHUNK

        def tile(i, carry):
            rows = pl.ds(pl.multiple_of(i * HG_TILE, HG_TILE), HG_TILE)
            v = hi_ref[rows, :]
            q, _sig, _f, kk, gl = _hgrn_gates(hq_ref[rows, :], hf_ref[rows, :], lb, pos)
            o = _lane_sum(q * kk, ones) * v
            for d in range(1, CHUNK):
                e = jnp.where(pos >= d, jnp.exp(gl - pltpu.roll(gl, d, 0)), 0.0)
                o = o + _lane_sum(q * pltpu.roll(kk, d, 0) * e, ones) * pltpu.roll(v, d, 0)
            q_s[rows, :] = q
            k_s[rows, :] = kk
            gl_s[rows, :] = gl
            opre_ref[rows, :] = o
            return carry

        lax.fori_loop(0, n_tiles, tile, 0)

        def chunk(c, st):
            rows = pl.ds(pl.multiple_of(c * CHUNK, CHUNK), CHUNK)
            gl = gl_s[rows, :]
            qt = q_s[rows, :] * jnp.exp(gl)
            opre_ref[rows, :] += lax.dot_general(qt.astype(BF16), st.astype(BF16), NT, preferred_element_type=F32)
            gll = gl[CHUNK - 1:CHUNK, :]
            kt = k_s[rows, :] * jnp.exp(gll - gl)
            return st * jnp.exp(gll) + lax.dot_general(hi_ref[rows, :].astype(BF16), kt.astype(BF16), TN,
                                                       preferred_element_type=F32)

        _unrolled_loop(n_chunks, chunk, jnp.zeros((HEAD_DIM, HEAD_DIM), F32))

        def finish(i, carry):
            rows = pl.ds(pl.multiple_of(i * HG_TILE, HG_TILE), HG_TILE)
            o = opre_ref[rows, :]
            hg = hg_ref[rows, :]
            rs = lax.rsqrt(jnp.mean(o * o, axis=-1, keepdims=True) + EPS)
            aout_ref[rows, :] = ((o * rs) * nw_ref[...] * (hg * _sigmoid(hg))).astype(BF16)
            return carry

        lax.fori_loop(0, n_tiles, finish, 0)

    return _host_call(
        body, 6, 2, fused, _head_first_last, name="hgrn_fwd", grid=(HEADS,),
        in_specs=[_head_col(0), _head_col(HEADS), _head_col(2 * HEADS), _head_col(3 * HEADS),
                  pl.BlockSpec((2, HEAD_DIM), lambda h: (0, h)), pl.BlockSpec((1, HEAD_DIM), lambda h: (0, 0))],
        out_specs=[_head_col(0), _head_col(0)],
        out_shape=[jax.ShapeDtypeStruct((SEQ, HEADS * HEAD_DIM), BF16), jax.ShapeDtypeStruct((SEQ, HEADS * HEAD_DIM), F32)],
        scratch_shapes=[pltpu.VMEM((SEQ, HEAD_DIM), F32)] * 3, sem=("parallel",),
        operands=[proj, proj, proj, proj, lb_logits, norm_w] + fused_arrays)


def _hgrn_bwd_serial(proj, lb_logits, norm_w, o_pre, d_aout, fused=None, fused_arrays=()):
    n_tiles = SEQ // HG_TILE
    n_chunks = SEQ // CHUNK

    def body(hq_ref, hf_ref, hi_ref, hg_ref, lbl_ref, nw_ref, opre_ref, da_ref,
             dhq_ref, dhf_ref, dhi_ref, dhg_ref, dlog_ref, gnw_ref,
             q_s, k_s, gl_s, do_s, dq_s, dk_s, dv_s, st_s):
        h = pl.program_id(0)
        lb = _lower_bound(lbl_ref[...])
        nw = nw_ref[...]
        ones = jnp.ones((HEAD_DIM, HEAD_DIM), BF16)
        pos = lax.broadcasted_iota(jnp.int32, (HG_TILE, HEAD_DIM), 0) % CHUNK

        @pl.when(h == 0)
        def _():
            gnw_ref[...] = jnp.zeros_like(gnw_ref)

        def tile(i, carry):
            rows = pl.ds(pl.multiple_of(i * HG_TILE, HG_TILE), HG_TILE)
            v = hi_ref[rows, :]
            q, _sig, f, kk, gl = _hgrn_gates(hq_ref[rows, :], hf_ref[rows, :], lb, pos)
            o = opre_ref[rows, :]
            hg = hg_ref[rows, :]
            da = da_ref[rows, :]
            rs = lax.rsqrt(jnp.mean(o * o, axis=-1, keepdims=True) + EPS)
            oh = o * rs
            sg = _sigmoid(hg)
            dnorm = da * (hg * sg)
            dhg_ref[rows, :] = (da * (oh * nw) * _dsilu(hg, sg)).astype(BF16)
            gnw_ref[...] += jnp.sum(dnorm * oh, axis=0, keepdims=True)
            doh = dnorm * nw
            do = rs * (doh - oh * jnp.mean(doh * oh, axis=-1, keepdims=True))

            d_a = _lane_sum(do * v, ones)
            dq = d_a * kk
            dk = d_a * q
            dv = _lane_sum(q * kk, ones) * do
            for d in range(1, CHUNK):
                ks = pltpu.roll(kk, d, 0)
                e = jnp.where(pos >= d, jnp.exp(gl - pltpu.roll(gl, d, 0)), 0.0)
                a_d = _lane_sum(q * ks * e, ones)
                d_a = _lane_sum(do * pltpu.roll(v, d, 0), ones) * e
                dq = dq + d_a * ks
                dk = dk + pltpu.roll(d_a * q, HG_TILE - d, 0)
                dv = dv + pltpu.roll(a_d * do, HG_TILE - d, 0)
            q_s[rows, :] = q
            k_s[rows, :] = kk
            gl_s[rows, :] = gl
            do_s[rows, :] = do
            dq_s[rows, :] = dq
            dk_s[rows, :] = dk
            dv_s[rows, :] = dv
            return carry

        lax.fori_loop(0, n_tiles, tile, 0)

        def fwd_chunk(c, st):
            rows = pl.ds(pl.multiple_of(c * CHUNK, CHUNK), CHUNK)
            gl = gl_s[rows, :]
            st_s[c] = st
            dq_s[rows, :] += jnp.dot(do_s[rows, :].astype(BF16), st.astype(BF16),
                                     preferred_element_type=F32) * jnp.exp(gl)
            gll = gl[CHUNK - 1:CHUNK, :]
            kt = k_s[rows, :] * jnp.exp(gll - gl)
            return st * jnp.exp(gll) + lax.dot_general(hi_ref[rows, :].astype(BF16), kt.astype(BF16), TN,
                                                       preferred_element_type=F32)

        _unrolled_loop(n_chunks, fwd_chunk, jnp.zeros((HEAD_DIM, HEAD_DIM), F32))

        pos_c = lax.broadcasted_iota(jnp.int32, (CHUNK, HEAD_DIM), 0)

        def bwd_chunk(i, carry):
            rt, dlb = carry
            c = n_chunks - 1 - i
            rows = pl.ds(pl.multiple_of(c * CHUNK, CHUNK), CHUNK)
            gl = gl_s[rows, :]
            q = q_s[rows, :]
            kk = k_s[rows, :]
            do = do_s[rows, :]
            gll = gl[CHUNK - 1:CHUNK, :]
            egl = jnp.exp(gll)
            ekt = jnp.exp(gll - gl)
            rt_b = rt.astype(BF16)
            dk_in = dk_s[rows, :]
            dk_far = jnp.dot(hi_ref[rows, :].astype(BF16), rt_b, preferred_element_type=F32) * ekt
            dk = dk_in + dk_far
            dv = dv_s[rows, :] + lax.dot_general((kk * ekt).astype(BF16), rt_b, NT, preferred_element_type=F32)
            dq = dq_s[rows, :]
            rc = q * dq - kk * dk_in
            pc = kk * dk_far
            pre = pc
            for sh in (1, 2, 4, 8):
                rc = rc + jnp.where(pos_c < CHUNK - sh, pltpu.roll(rc, CHUNK - sh, 0), 0.0)
                pre = pre + jnp.where(pos_c >= sh, pltpu.roll(pre, sh, 0), 0.0)
            across = jnp.sum(st_s[c] * rt, axis=0, keepdims=True) * egl
            dgl = rc + (pre - pc) + across
            hf = hf_ref[rows, :]
            sig = _sigmoid(hf)
            f = lb + (1.0 - lb) * sig
            df = dgl / f - dk
            dhf_ref[rows, :] = (df * (1.0 - lb) * sig * (1.0 - sig)).astype(BF16)
            hq = hq_ref[rows, :]
            dhq_ref[rows, :] = (dq * _dsilu(hq, _sigmoid(hq))).astype(BF16)
            dhi_ref[rows, :] = dv.astype(BF16)
            rt_new = rt * egl + lax.dot_general(do.astype(BF16), (q * jnp.exp(gl)).astype(BF16), TN,
                                                preferred_element_type=F32)
            return (rt_new, dlb + jnp.sum(df * (1.0 - sig), axis=0, keepdims=True))

        _, dlb = _unrolled_loop(n_chunks, bwd_chunk,
                                (jnp.zeros((HEAD_DIM, HEAD_DIM), F32), jnp.zeros((1, HEAD_DIM), F32)))
        dl0 = lb * (1.0 - lb) * dlb
        dlog_ref[0:1, :] = dl0
        dlog_ref[1:2, :] = -dl0

    wide = HEADS * HEAD_DIM
    return _host_call(
        body, 8, 6, fused, _head_first_last, name="hgrn_bwd", grid=(HEADS,),
        in_specs=[_head_col(0), _head_col(HEADS), _head_col(2 * HEADS), _head_col(3 * HEADS),
                  pl.BlockSpec((2, HEAD_DIM), lambda h: (0, h)), pl.BlockSpec((1, HEAD_DIM), lambda h: (0, 0)),
                  _head_col(0), _head_col(0)],
        out_specs=[_head_col(0)] * 4 + [pl.BlockSpec((2, HEAD_DIM), lambda h: (0, h)),
                                        pl.BlockSpec((1, HEAD_DIM), lambda h: (0, 0))],
        out_shape=[jax.ShapeDtypeStruct((SEQ, wide), BF16)] * 4 + [jax.ShapeDtypeStruct((2, wide), F32),
                                                                    jax.ShapeDtypeStruct((1, HEAD_DIM), F32)],
        scratch_shapes=[pltpu.VMEM((SEQ, HEAD_DIM), F32)] * 7 + [pltpu.VMEM((n_chunks, HEAD_DIM, HEAD_DIM), F32)],
        sem=("arbitrary",),
        operands=[proj, proj, proj, proj, lb_logits, norm_w, o_pre, d_aout] + list(fused_arrays))


CHUNKS_PER_TILE = HG_TILE // CHUNK


def _chunk_end(x, pos):
    y = jnp.where(pos == CHUNK - 1, x, 0.0)
    for sh in (1, 2, 4, 8):
        y = y + jnp.where(pos < CHUNK - sh, pltpu.roll(y, x.shape[0] - sh, 0), 0.0)
    return y


def _suffix_in_chunk(x, pos):
    for sh in (1, 2, 4, 8):
        x = x + jnp.where(pos < CHUNK - sh, pltpu.roll(x, x.shape[0] - sh, 0), 0.0)
    return x


def _prefix_in_chunk(x, pos):
    for sh in (1, 2, 4, 8):
        x = x + jnp.where(pos >= sh, pltpu.roll(x, sh, 0), 0.0)
    return x


def _pair_decays(f, pos):
    shifted = jnp.where(pos >= 1, f, 0.0)
    e = shifted
    yield 1, e
    for d in range(2, CHUNK):
        shifted = pltpu.roll(shifted, 1, 0)
        e = e * shifted
        yield d, e


def _chunk_rows(cc):
    return slice(cc * CHUNK, (cc + 1) * CHUNK)


def _outer_products(lhs_b, rhs_b, dst, i):
    for cc in range(CHUNKS_PER_TILE):
        dst[i * CHUNKS_PER_TILE + cc] = lax.dot_general(lhs_b[_chunk_rows(cc)], rhs_b[_chunk_rows(cc)], TN,
                                                        preferred_element_type=F32)


def _state_scan(n_chunks, gl_s, u_s, keep, reverse):
    def step(k, st):
        c = n_chunks - 1 - k if reverse else k
        keep[c] = st.astype(BF16)
        gl = gl_s[pl.ds(pl.multiple_of(c * CHUNK, CHUNK), CHUNK), :]
        return st * jnp.exp(gl[CHUNK - 1:CHUNK, :]) + u_s[c]

    _unrolled_loop(n_chunks, step, jnp.zeros((HEAD_DIM, HEAD_DIM), F32))


def _hgrn_fwd(proj, lb_logits, norm_w, fused=None, fused_arrays=()):
    n_tiles = SEQ // HG_TILE
    n_chunks = SEQ // CHUNK
    fused_arrays = list(fused_arrays)

    def body(hq_ref, hf_ref, hi_ref, hg_ref, lbl_ref, nw_ref, aout_ref, opre_ref, qt_s, gl_s, u_s, st_s):
        lb = _lower_bound(lbl_ref[...])
        ones = jnp.ones((HEAD_DIM, HEAD_DIM), BF16)
        pos = lax.broadcasted_iota(jnp.int32, (HG_TILE, HEAD_DIM), 0) % CHUNK

        def tile(i, carry):
            rows = pl.ds(pl.multiple_of(i * HG_TILE, HG_TILE), HG_TILE)
            v = hi_ref[rows, :]
            q, _sig, f, kk, gl = _hgrn_gates(hq_ref[rows, :], hf_ref[rows, :], lb, pos)
            o = _lane_sum(q * kk, ones) * v
            for d, e in _pair_decays(f, pos):
                o = o + _lane_sum(q * pltpu.roll(kk, d, 0) * e, ones) * pltpu.roll(v, d, 0)
            opre_ref[rows, :] = o
            qt_s[rows, :] = q * jnp.exp(gl)
            gl_s[rows, :] = gl
            kt = kk * jnp.exp(_chunk_end(gl, pos) - gl)
            _outer_products(v.astype(BF16), kt.astype(BF16), u_s, i)
            return carry

        lax.fori_loop(0, n_tiles, tile, 0)
        _state_scan(n_chunks, gl_s, u_s, st_s, reverse=False)

        def finish(i, carry):
            rows = pl.ds(pl.multiple_of(i * HG_TILE, HG_TILE), HG_TILE)
            qt_b = qt_s[rows, :].astype(BF16)
            past = [lax.dot_general(qt_b[_chunk_rows(cc)], st_s[i * CHUNKS_PER_TILE + cc], NT,
                                    preferred_element_type=F32) for cc in range(CHUNKS_PER_TILE)]
            o = opre_ref[rows, :] + jnp.concatenate(past, axis=0)
            opre_ref[rows, :] = o
            hg = hg_ref[rows, :]
            rs = lax.rsqrt(jnp.mean(o * o, axis=-1, keepdims=True) + EPS)
            aout_ref[rows, :] = ((o * rs) * nw_ref[...] * (hg * _sigmoid(hg))).astype(BF16)
            return carry

        lax.fori_loop(0, n_tiles, finish, 0)

    return _host_call(
        body, 6, 2, fused, _head_first_last, name="hgrn_fwd", grid=(HEADS,),
        in_specs=[_head_col(0), _head_col(HEADS), _head_col(2 * HEADS), _head_col(3 * HEADS),
                  pl.BlockSpec((2, HEAD_DIM), lambda h: (0, h)), pl.BlockSpec((1, HEAD_DIM), lambda h: (0, 0))],
        out_specs=[_head_col(0), _head_col(0)],
        out_shape=[jax.ShapeDtypeStruct((SEQ, HEADS * HEAD_DIM), BF16), jax.ShapeDtypeStruct((SEQ, HEADS * HEAD_DIM), F32)],
        scratch_shapes=[pltpu.VMEM((SEQ, HEAD_DIM), F32)] * 2 + [pltpu.VMEM((n_chunks, HEAD_DIM, HEAD_DIM), F32),
                                                                 pltpu.VMEM((n_chunks, HEAD_DIM, HEAD_DIM), BF16)],
        sem=("parallel",), operands=[proj, proj, proj, proj, lb_logits, norm_w] + fused_arrays)


def _hgrn_bwd(proj, lb_logits, norm_w, o_pre, d_aout, fused=None, fused_arrays=()):
    n_tiles = SEQ // HG_TILE
    n_chunks = SEQ // CHUNK

    def body(hq_ref, hf_ref, hi_ref, hg_ref, lbl_ref, nw_ref, opre_ref, da_ref,
             dhq_ref, dhf_ref, dhi_ref, dhg_ref, dlog_ref, gnw_ref,
             q_s, k_s, gl_s, do_s, dq_s, dk_s, dv_s, u_s, st_s, rt_s):
        h = pl.program_id(0)
        lb = _lower_bound(lbl_ref[...])
        nw = nw_ref[...]
        ones = jnp.ones((HEAD_DIM, HEAD_DIM), BF16)
        pos = lax.broadcasted_iota(jnp.int32, (HG_TILE, HEAD_DIM), 0) % CHUNK

        @pl.when(h == 0)
        def _():
            gnw_ref[...] = jnp.zeros_like(gnw_ref)

        def tile(i, carry):
            rows = pl.ds(pl.multiple_of(i * HG_TILE, HG_TILE), HG_TILE)
            v = hi_ref[rows, :]
            q, _sig, f, kk, gl = _hgrn_gates(hq_ref[rows, :], hf_ref[rows, :], lb, pos)
            o = opre_ref[rows, :]
            hg = hg_ref[rows, :]
            da = da_ref[rows, :]
            rs = lax.rsqrt(jnp.mean(o * o, axis=-1, keepdims=True) + EPS)
            oh = o * rs
            sg = _sigmoid(hg)
            dnorm = da * (hg * sg)
            dhg_ref[rows, :] = (da * (oh * nw) * _dsilu(hg, sg)).astype(BF16)
            gnw_ref[...] += jnp.sum(dnorm * oh, axis=0, keepdims=True)
            doh = dnorm * nw
            do = rs * (doh - oh * jnp.mean(doh * oh, axis=-1, keepdims=True))

            d_a = _lane_sum(do * v, ones)
            dq = d_a * kk
            dk = d_a * q
            dv = _lane_sum(q * kk, ones) * do
            for d, e in _pair_decays(f, pos):
                ks = pltpu.roll(kk, d, 0)
                a_d = _lane_sum(q * ks * e, ones)
                d_a = _lane_sum(do * pltpu.roll(v, d, 0), ones) * e
                dq = dq + d_a * ks
                dk = dk + pltpu.roll(d_a * q, HG_TILE - d, 0)
                dv = dv + pltpu.roll(a_d * do, HG_TILE - d, 0)
            q_s[rows, :] = q
            k_s[rows, :] = kk
            gl_s[rows, :] = gl
            do_s[rows, :] = do
            dq_s[rows, :] = dq
            dk_s[rows, :] = dk
            dv_s[rows, :] = dv
            kt = kk * jnp.exp(_chunk_end(gl, pos) - gl)
            _outer_products(v.astype(BF16), kt.astype(BF16), u_s, i)
            return carry

        lax.fori_loop(0, n_tiles, tile, 0)
        _state_scan(n_chunks, gl_s, u_s, st_s, reverse=False)

        def reverse_increments(i, carry):
            rows = pl.ds(pl.multiple_of(i * HG_TILE, HG_TILE), HG_TILE)
            qt = q_s[rows, :] * jnp.exp(gl_s[rows, :])
            _outer_products(do_s[rows, :].astype(BF16), qt.astype(BF16), u_s, i)
            return carry

        lax.fori_loop(0, n_tiles, reverse_increments, 0)
        _state_scan(n_chunks, gl_s, u_s, rt_s, reverse=True)

        def finish(i, dlb):
            rows = pl.ds(pl.multiple_of(i * HG_TILE, HG_TILE), HG_TILE)
            q = q_s[rows, :]
            kk = k_s[rows, :]
            gl = gl_s[rows, :]
            gll = _chunk_end(gl, pos)
            ekt = jnp.exp(gll - gl)
            do_b = do_s[rows, :].astype(BF16)
            v_b = hi_ref[rows, :].astype(BF16)
            kt_b = (kk * ekt).astype(BF16)
            dq_far, dk_far, dv_far, across = [], [], [], []
            for cc in range(CHUNKS_PER_TILE):
                st = st_s[i * CHUNKS_PER_TILE + cc]
                rt = rt_s[i * CHUNKS_PER_TILE + cc]
                sl = _chunk_rows(cc)
                dq_far.append(jnp.dot(do_b[sl], st, preferred_element_type=F32))
                dk_far.append(jnp.dot(v_b[sl], rt, preferred_element_type=F32))
                dv_far.append(lax.dot_general(kt_b[sl], rt, NT, preferred_element_type=F32))
                both = jnp.sum(st.astype(F32) * rt.astype(F32), axis=0, keepdims=True)
                across.append(jnp.broadcast_to(both, (CHUNK, HEAD_DIM)))
            dq = dq_s[rows, :] + jnp.concatenate(dq_far, axis=0) * jnp.exp(gl)
            dk_in = dk_s[rows, :]
            dk_out = jnp.concatenate(dk_far, axis=0) * ekt
            dk = dk_in + dk_out
            dv = dv_s[rows, :] + jnp.concatenate(dv_far, axis=0)
            pc = kk * dk_out
            dgl = (_suffix_in_chunk(q * dq - kk * dk_in, pos) + (_prefix_in_chunk(pc, pos) - pc)
                   + jnp.concatenate(across, axis=0) * jnp.exp(gll))
            hf = hf_ref[rows, :]
            sig = _sigmoid(hf)
            f = lb + (1.0 - lb) * sig
            df = dgl / f - dk
            dhf_ref[rows, :] = (df * (1.0 - lb) * sig * (1.0 - sig)).astype(BF16)
            hq = hq_ref[rows, :]
            dhq_ref[rows, :] = (dq * _dsilu(hq, _sigmoid(hq))).astype(BF16)
            dhi_ref[rows, :] = dv.astype(BF16)
            return dlb + jnp.sum(df * (1.0 - sig), axis=0, keepdims=True)

        dlb = lax.fori_loop(0, n_tiles, finish, jnp.zeros((1, HEAD_DIM), F32))
        dl0 = lb * (1.0 - lb) * dlb
        dlog_ref[0:1, :] = dl0
        dlog_ref[1:2, :] = -dl0

    wide = HEADS * HEAD_DIM
    return _host_call(
        body, 8, 6, fused, _head_first_last, name="hgrn_bwd", grid=(HEADS,),
        in_specs=[_head_col(0), _head_col(HEADS), _head_col(2 * HEADS), _head_col(3 * HEADS),
                  pl.BlockSpec((2, HEAD_DIM), lambda h: (0, h)), pl.BlockSpec((1, HEAD_DIM), lambda h: (0, 0)),
                  _head_col(0), _head_col(0)],
        out_specs=[_head_col(0)] * 4 + [pl.BlockSpec((2, HEAD_DIM), lambda h: (0, h)),
                                        pl.BlockSpec((1, HEAD_DIM), lambda h: (0, 0))],
        out_shape=[jax.ShapeDtypeStruct((SEQ, wide), BF16)] * 4 + [jax.ShapeDtypeStruct((2, wide), F32),
                                                                    jax.ShapeDtypeStruct((1, HEAD_DIM), F32)],
        scratch_shapes=[pltpu.VMEM((SEQ, HEAD_DIM), F32)] * 7 + [pltpu.VMEM((n_chunks, HEAD_DIM, HEAD_DIM), F32),
                                                                 pltpu.VMEM((n_chunks, HEAD_DIM, HEAD_DIM), BF16),
                                                                 pltpu.VMEM((n_chunks, HEAD_DIM, HEAD_DIM), BF16)],
        sem=("arbitrary",),
        operands=[proj, proj, proj, proj, lb_logits, norm_w, o_pre, d_aout] + list(fused_arrays))


Q_TILE = 256
ATT_SCALE = HEAD_DIM ** -0.5
ATT_OFF = 4 * HEADS


def _qk_prep(proj, q_w, k_w):
    def body(aq_ref, ak_ref, av_ref, qw_ref, kw_ref, qn_ref, kn_ref, v_ref):
        aq = aq_ref[...]
        ak = ak_ref[...]
        qn_ref[...] = (aq * lax.rsqrt(jnp.mean(aq * aq, axis=-1, keepdims=True) + EPS) * qw_ref[...]).astype(BF16)
        kn_ref[...] = (ak * lax.rsqrt(jnp.mean(ak * ak, axis=-1, keepdims=True) + EPS) * kw_ref[...]).astype(BF16)
        v_ref[...] = av_ref[...].astype(BF16)

    wide = HEADS * HEAD_DIM
    vec = pl.BlockSpec((1, HEAD_DIM), lambda h: (0, 0))
    return pl.pallas_call(
        body, name="qk_prep", grid=(HEADS,),
        in_specs=[_head_col(ATT_OFF), _head_col(ATT_OFF + HEADS), _head_col(ATT_OFF + 2 * HEADS), vec, vec],
        out_specs=[_head_col(0)] * 3, out_shape=[jax.ShapeDtypeStruct((SEQ, wide), BF16)] * 3,
        compiler_params=_params(("parallel",)))(proj, proj, proj, q_w, k_w)


def _alibi_slopes():
    slopes = jnp.exp2(-8.0 * jnp.arange(1, HEADS + 1, dtype=F32) / HEADS)
    return jnp.broadcast_to(slopes[:, None, None], (HEADS, 1, HEAD_DIM))


SLOPE_SPEC = pl.BlockSpec((None, 1, HEAD_DIM), lambda h, i: (h, 0, 0))


N_Q_TILES = SEQ // Q_TILE
K_BLOCK = 512
NOT_ATTENDED = 1e35


def _att_tables():
    o = jnp.arange(N_Q_TILES, dtype=jnp.int32)[:, None, None]
    r = jnp.arange(Q_TILE, dtype=jnp.int32)[None, :, None]
    c = jnp.arange(K_BLOCK, dtype=jnp.int32)[None, None, :]
    dist = o * Q_TILE + r - c
    mult = ((dist <= 128).astype(F32) + (((dist % 4) == 0) & (dist <= 512)).astype(F32)
            + ((dist % 16) == 0).astype(F32))
    valid = (dist >= 0) & (mult > 0)
    return (jnp.where(valid, dist.astype(F32), NOT_ATTENDED),
            jnp.where(valid, jnp.log(jnp.maximum(mult, 1.0)), 0.0))


TABLE_SPEC = pl.BlockSpec((N_Q_TILES, Q_TILE, K_BLOCK), lambda h, i: (0, 0, 0))


def _att_block(q, k_ref, j, i, slope, dist_ref, lmul_ref):
    rows = pl.ds(pl.multiple_of(j * K_BLOCK, K_BLOCK), K_BLOCK)
    off = i - j * (K_BLOCK // Q_TILE)
    s = lax.dot_general(q, k_ref[rows, :], NT, preferred_element_type=F32) * ATT_SCALE
    return s - slope * dist_ref[off] + lmul_ref[off], rows


def _n_key_blocks(i):
    return (i + K_BLOCK // Q_TILE) // (K_BLOCK // Q_TILE)


def _att_first_last():
    h, i = pl.program_id(0), pl.program_id(1)
    return (h == 0) & (i == 0), (h == HEADS - 1) & (i == N_Q_TILES - 1)


def _attn_fwd(qn, kn, vb, fused=None, fused_arrays=()):
    def body(q_ref, k_ref, v_ref, sl_ref, dist_ref, lmul_ref, o_ref, lse_ref):
        i = pl.program_id(1)
        q = q_ref[...]
        slope = sl_ref[0:1, 0:1]

        def step(j, carry):
            m, l, acc = carry
            sb, rows = _att_block(q, k_ref, j, i, slope, dist_ref, lmul_ref)
            m_new = jnp.maximum(m, jnp.max(sb, axis=-1, keepdims=True))
            alpha = jnp.exp(m - m_new)
            p = jnp.exp(sb - m_new)
            l = alpha * l + jnp.sum(p, axis=-1, keepdims=True)
            acc = alpha * acc + jnp.dot(p.astype(BF16), v_ref[rows, :], preferred_element_type=F32)
            return m_new, l, acc

        m, l, acc = lax.fori_loop(0, _n_key_blocks(i), step,
                                  (jnp.full((Q_TILE, 1), -1e30, F32), jnp.zeros((Q_TILE, 1), F32),
                                   jnp.zeros((Q_TILE, HEAD_DIM), F32)))
        o_ref[...] = acc / l
        lse_ref[...] = m + jnp.log(l)

    wide = HEADS * HEAD_DIM
    qt = pl.BlockSpec((Q_TILE, HEAD_DIM), lambda h, i: (i, h))
    full = pl.BlockSpec((SEQ, HEAD_DIM), lambda h, i: (0, h))
    return _host_call(
        body, 6, 2, fused, _att_first_last, name="attn_fwd", grid=(HEADS, N_Q_TILES),
        in_specs=[qt, full, full, SLOPE_SPEC, TABLE_SPEC, TABLE_SPEC],
        out_specs=[qt, pl.BlockSpec((None, Q_TILE, 1), lambda h, i: (h, i, 0))],
        out_shape=[jax.ShapeDtypeStruct((SEQ, wide), F32), jax.ShapeDtypeStruct((HEADS, SEQ, 1), F32)],
        scratch_shapes=[], sem=("parallel", "parallel"),
        operands=[qn, kn, vb, _alibi_slopes(), *_att_tables()] + list(fused_arrays))


def _attn_bwd(qn, kn, vb, o, lse, d_mix, fused=None, fused_arrays=()):
    def body(q_ref, k_ref, v_ref, o_ref, lse_ref, do_ref, sl_ref, dist_ref, lmul_ref, dq_ref, dk_ref, dv_ref):
        i = pl.program_id(1)
        q = q_ref[...]
        do = do_ref[...]
        do_b = do.astype(BF16)
        slope = sl_ref[0:1, 0:1]
        lse = lse_ref[...]
        delta = jnp.sum(do * o_ref[...], axis=-1, keepdims=True)

        @pl.when(i == 0)
        def _():
            dk_ref[...] = jnp.zeros_like(dk_ref)
            dv_ref[...] = jnp.zeros_like(dv_ref)

        def step(j, dq):
            sb, rows = _att_block(q, k_ref, j, i, slope, dist_ref, lmul_ref)
            p = jnp.exp(sb - lse)
            dp = lax.dot_general(do_b, v_ref[rows, :], NT, preferred_element_type=F32)
            ds = (p * (dp - delta)).astype(BF16)
            dk_ref[rows, :] += lax.dot_general(ds, q, TN, preferred_element_type=F32) * ATT_SCALE
            dv_ref[rows, :] += lax.dot_general(p.astype(BF16), do_b, TN, preferred_element_type=F32)
            return dq + jnp.dot(ds, k_ref[rows, :], preferred_element_type=F32)

        dq = lax.fori_loop(0, _n_key_blocks(i), step, jnp.zeros((Q_TILE, HEAD_DIM), F32))
        dq_ref[...] = dq * ATT_SCALE

    wide = HEADS * HEAD_DIM
    qt = pl.BlockSpec((Q_TILE, HEAD_DIM), lambda h, i: (i, h))
    full = pl.BlockSpec((SEQ, HEAD_DIM), lambda h, i: (0, h))
    return _host_call(
        body, 9, 3, fused, _att_first_last, name="attn_bwd", grid=(HEADS, N_Q_TILES),
        in_specs=[qt, full, full, qt, pl.BlockSpec((None, Q_TILE, 1), lambda h, i: (h, i, 0)),
                  pl.BlockSpec((Q_TILE, HEAD_DIM), lambda h, i: (i, h + HEADS)), SLOPE_SPEC, TABLE_SPEC, TABLE_SPEC],
        out_specs=[qt, full, full], out_shape=[jax.ShapeDtypeStruct((SEQ, wide), F32)] * 3,
        scratch_shapes=[], sem=("parallel", "arbitrary"),
        operands=[qn, kn, vb, o, lse, d_mix, _alibi_slopes(), *_att_tables()] + list(fused_arrays))


def _qk_bwd(proj, q_w, k_w, dqn, dkn, dv):
    def body(aq_ref, ak_ref, qw_ref, kw_ref, dqn_ref, dkn_ref, dv_ref, daq_ref, dak_ref, dav_ref, gq_ref, gk_ref):
        h = pl.program_id(0)

        @pl.when(h == 0)
        def _():
            gq_ref[...] = jnp.zeros_like(gq_ref)
            gk_ref[...] = jnp.zeros_like(gk_ref)

        def one(a_ref, w_ref, d_ref, da_ref, g_ref):
            a = a_ref[...]
            d = d_ref[...]
            rs = lax.rsqrt(jnp.mean(a * a, axis=-1, keepdims=True) + EPS)
            ah = a * rs
            g_ref[...] += jnp.sum(d * ah, axis=0, keepdims=True)
            dah = d * w_ref[...]
            da_ref[...] = (rs * (dah - ah * jnp.mean(dah * ah, axis=-1, keepdims=True))).astype(BF16)

        one(aq_ref, qw_ref, dqn_ref, daq_ref, gq_ref)
        one(ak_ref, kw_ref, dkn_ref, dak_ref, gk_ref)
        dav_ref[...] = dv_ref[...].astype(BF16)

    wide = HEADS * HEAD_DIM
    vec = pl.BlockSpec((1, HEAD_DIM), lambda h: (0, 0))
    return pl.pallas_call(
        body, name="qk_bwd", grid=(HEADS,),
        in_specs=[_head_col(ATT_OFF), _head_col(ATT_OFF + HEADS), vec, vec, _head_col(0), _head_col(0), _head_col(0)],
        out_specs=[_head_col(0)] * 3 + [vec, vec],
        out_shape=[jax.ShapeDtypeStruct((SEQ, wide), BF16)] * 3 + [jax.ShapeDtypeStruct((1, HEAD_DIM), F32)] * 2,
        compiler_params=_params(("arbitrary",)))(proj, proj, q_w, k_w, dqn, dkn, dv)


def _pair_sum(name, partial, theirs, core):
    _, r, c = theirs.shape
    tr = r // 2 if r % 16 == 0 else r

    def body(core_ref, a_ref, b_ref, o_ref):
        o_ref[...] = (a_ref[...].astype(F32) + b_ref[...].astype(F32)).astype(BF16)

    spec = pl.BlockSpec((None, tr, c), lambda q, i, core_ref: (q, i, 0))
    grid_spec = pltpu.PrefetchScalarGridSpec(
        num_scalar_prefetch=1, grid=(4, r // tr),
        in_specs=[pl.BlockSpec((None, tr, c), lambda q, i, core_ref: (2 * q + core_ref[0], i, 0)), spec],
        out_specs=spec)
    return pl.pallas_call(body, name=name, grid_spec=grid_spec, out_shape=jax.ShapeDtypeStruct(theirs.shape, BF16),
                          compiler_params=_params(("parallel", "parallel")))(core, partial, theirs)


def _adamw_step(w, m, v, g):
    nm = ADAM_B1 * m + (1.0 - ADAM_B1) * g
    nv = ADAM_B2 * v + (1.0 - ADAM_B2) * (g * g)
    m_hat = nm / (1.0 - ADAM_B1 ** ADAM_STEP)
    v_hat = nv / (1.0 - ADAM_B2 ** ADAM_STEP)
    return -ADAM_LR * (m_hat / (jnp.sqrt(v_hat) + ADAM_EPS) + ADAM_WD * w), nm, nv


def _adamw(name, w, m, v, addends, tr=None):
    r, c = w.shape
    tr = r if tr is None else tr
    n_add = len(addends)

    def body(*refs):
        w_ref, m_ref, v_ref = refs[:3]
        add_refs = refs[3:3 + n_add]
        g_ref, d_ref, nm_ref, nv_ref = refs[3 + n_add:]
        g = add_refs[0][...].astype(F32)
        for a_ref in add_refs[1:]:
            g = g + a_ref[...].astype(F32)
        g_ref[...] = g
        d_ref[...], nm_ref[...], nv_ref[...] = _adamw_step(w_ref[...], m_ref[...], v_ref[...], g)

    spec = pl.BlockSpec((tr, c), lambda i: (i, 0))
    out = jax.ShapeDtypeStruct((r, c), F32)
    return pl.pallas_call(body, name=name, grid=(r // tr,), in_specs=[spec] * (3 + n_add), out_specs=[spec] * 4,
                          out_shape=[out] * 4, compiler_params=_params(("parallel",)))(w, m, v, *addends)


def _adamw_reduced(name, w, m, v, chip_sums, received, chip, tr):
    r, c = w.shape

    def body(chip_ref, w_ref, m_ref, v_ref, own_ref, r0_ref, r1_ref, r2_ref, g_ref, d_ref, nm_ref, nv_ref):
        g = ((own_ref[...].astype(F32) + r0_ref[...].astype(F32)) + r1_ref[...].astype(F32)) + r2_ref[...].astype(F32)
        g_ref[...] = g
        d_ref[...], nm_ref[...], nv_ref[...] = _adamw_step(w_ref[...], m_ref[...], v_ref[...], g)

    spec = pl.BlockSpec((tr, c), lambda i, chip_ref: (i, 0))

    def slot(k):
        return pl.BlockSpec((None, tr, c), lambda i, chip_ref: (k, i, 0))

    grid_spec = pltpu.PrefetchScalarGridSpec(
        num_scalar_prefetch=1, grid=(r // tr,),
        in_specs=[spec, spec, spec, pl.BlockSpec((None, tr, c), lambda i, chip_ref: (chip_ref[0], i, 0)),
                  slot(0), slot(1), slot(2)],
        out_specs=[spec] * 4)
    out = jax.ShapeDtypeStruct((r, c), F32)
    return pl.pallas_call(body, name=name, grid_spec=grid_spec, out_shape=[out] * 4,
                          compiler_params=_params(("parallel",)))(chip, w, m, v, chip_sums, received, received, received)


def _sum_devices(gathered):
    _, r, c = gathered.shape

    def body(g_ref, o_ref):
        acc = g_ref[0]
        for d in range(1, N_DEV):
            acc = acc + g_ref[d]
        o_ref[...] = acc

    return pl.pallas_call(body, name="sum_devices", out_shape=jax.ShapeDtypeStruct((r, c), F32))(gathered)


def _pack_rows(vectors, rows):
    flat = jnp.concatenate([v.reshape(-1) for v in vectors])
    return jnp.pad(flat, (0, rows * 128 - flat.shape[0])).reshape(rows, 128)


def _unpack(flat, shapes):
    out, off = [], 0
    for shp in shapes:
        n = 1
        for d in shp:
            n *= d
        out.append(flat[off:off + n].reshape(shp))
        off += n
    return out


def _device_step(xs, tgt, mod, norm1_w, norm2_w, lb_logits, hg_norm_w, q_norm_w, k_norm_w, conv_w_full, conv_b,
                 win_g, w_out_x, w_up_x, w_down_x, core=None):
    fused = core is not None
    shift1, scale1, gate1, shift2, scale2, gate2 = (mod[k] for k in range(6))

    h, rstd1 = _norm_fwd("norm1_fwd", xs, norm1_w, scale1, shift1)
    if fused:
        proj, (wout_g,) = _mm_blocked_rhs("mm_in", h, win_g, fused=_FusedCopies("gather", [w_out_x]),
                                          fused_arrays=[w_out_x])
        (a_out, o_pre), (wup_g,) = _hgrn_fwd(proj, lb_logits, hg_norm_w,
                                             _FusedCopies("gather", [w_up_x], peers=(0, 1, 2)), [w_up_x])
        wout_g, = _forward_to_sibling("allgather_stage2_out", [wout_g])
        wout_full = wout_g.reshape(D_MODEL, D_MODEL)
        qn, kn, vb = _qk_prep(proj, q_norm_w, k_norm_w)
        (att_o, lse), (wup_g,) = _attn_fwd(qn, kn, vb, _FusedCopies("relay", [wup_g]), [wup_g])
    else:
        proj = _mm_blocked_rhs("mm_in", h, win_g)
        (a_out, o_pre), _ = _hgrn_fwd(proj, lb_logits, hg_norm_w)
        wup_g, wout_full, wdown_full = w_up_x, w_out_x, w_down_x
        qn, kn, vb = _qk_prep(proj, q_norm_w, k_norm_w)
        (att_o, lse), _ = _attn_fwd(qn, kn, vb)
    mixin = jnp.concatenate([a_out, att_o.astype(BF16)], axis=1)
    if fused:
        mix, (wup_g,) = _mm_plain("mm_out", mixin, wout_full, NN, 512, 1024, F32,
                                  fused=_FusedCopies("forward", [wup_g]), fused_arrays=[wup_g])
    else:
        mix = _mm_plain("mm_out", mixin, wout_full, NN, 512, 1024, F32)
    x1, h2, rstd2 = _norm_fwd("norm2_fwd", xs, norm2_w, scale2, shift2, resid=mix, gate=gate1)
    if fused:
        u, (wdown_g,) = _mm_blocked_rhs("mm_up", h2, wup_g, fused=_FusedCopies("gather", [w_down_x]),
                                        fused_arrays=[w_down_x])
        y, (wdown_g,) = _conv_gate_fwd(u, conv_w_full, conv_b, _FusedCopies("forward", [wdown_g]), [wdown_g])
        wdown_full = wdown_g.reshape(D_FF, D_MODEL)
    else:
        u = _mm_blocked_rhs("mm_up", h2, wup_g)
        y = _conv_gate_fwd(u, conv_w_full, conv_b)
    ffn = _mm_plain("mm_down", y, wdown_full, NN, 512, 512, F32)
    loss_v, dout, dffn, dgate2 = _loss_head(x1, ffn, gate2, tgt)

    dy = _mm_plain("mm_down_dx", dffn, wdown_full, NT, 512, UP_BLK, F32)
    gw_down = _mm_plain("mm_down_dw", y, dffn, TN, UP_BLK, 1024, BF16)
    da, dg, gconv_w, gconv_b = _conv_gate_bwd(u, dy, conv_w_full, conv_b)
    du = jnp.concatenate([da, dg], axis=1)
    dh2 = _mm_blocked_rhs_t("mm_up_dx", du, wup_g)
    gw_up = _mm_wgrad_blocked("mm_up_dw", h2, du)
    if fused:
        part_up, part_down = gw_up, gw_down.reshape(N_DEV, FF_BLK, D_MODEL)
        (dx1, dmix, dshift2, dscale2, gnorm2, dgate1), (sib_up, sib_down) = _norm_bwd(
            "norm2_bwd", dh2, x1, rstd2, norm2_w, scale2, dout, mix=mix, gate=gate1,
            fused=_FusedCopies("sibling", [part_up, part_down]), fused_arrays=[part_up, part_down])
    else:
        dx1, dmix, dshift2, dscale2, gnorm2, dgate1 = _norm_bwd(
            "norm2_bwd", dh2, x1, rstd2, norm2_w, scale2, dout, mix=mix, gate=gate1)
    gw_out = _mm_plain("mm_out_dw", mixin, dmix, TN, 512, 1024, BF16)
    if fused:
        part_out = gw_out.reshape(N_DEV, OUT_BLK, D_MODEL)
        dmixin, (sib_out,) = _mm_plain("mm_out_dx", dmix, wout_full, NT, 512, 1024, F32,
                                       fused=_FusedCopies("sibling", [part_out]), fused_arrays=[part_out])
        cs_up = _pair_sum("grad_pair_sum_up", part_up, sib_up, core)
        cs_out = _pair_sum("grad_pair_sum_out", part_out, sib_out, core)
        cs_down = _pair_sum("grad_pair_sum_down", part_down, sib_down, core)
        (dhq, dhf, dhi, dhg, glog, ghg), (fc_up,) = _hgrn_bwd(
            proj, lb_logits, hg_norm_w, o_pre, dmixin, _FusedCopies("chips", [cs_up]), [cs_up])
        (dqn, dkn, dvv), (fc_down, fc_out) = _attn_bwd(qn, kn, vb, att_o, lse, dmixin,
                                                       _FusedCopies("chips", [cs_down, cs_out]), [cs_down, cs_out])
    else:
        dmixin = _mm_plain("mm_out_dx", dmix, wout_full, NT, 512, 1024, F32)
        (dhq, dhf, dhi, dhg, glog, ghg), _ = _hgrn_bwd(proj, lb_logits, hg_norm_w, o_pre, dmixin)
        (dqn, dkn, dvv), _ = _attn_bwd(qn, kn, vb, att_o, lse, dmixin)
    daq, dak, dav, gqw, gkw = _qk_bwd(proj, q_norm_w, k_norm_w, dqn, dkn, dvv)
    dproj = jnp.concatenate([dhq, dhf, dhi, dhg, daq, dak, dav], axis=1)
    gw_in = _mm_wgrad_blocked("mm_in_dw", h, dproj)
    if fused:
        from_sibling, = _exchange_sibling("grad_exchange_sibling_b", [gw_in])
        cs_in = _pair_sum("grad_pair_sum_in", gw_in, from_sibling, core)
        dh, (fc_in,) = _mm_blocked_rhs_t("mm_in_dx", dproj, win_g, fused=_FusedCopies("chips", [cs_in]),
                                         fused_arrays=[cs_in])
        large = [(cs_in, fc_in), (cs_out, fc_out), (cs_up, fc_up), (cs_down, fc_down)]
    else:
        dh = _mm_blocked_rhs_t("mm_in_dx", dproj, win_g)
        large = [gw_in, gw_out, gw_up, gw_down]
    grad_x, dshift1, dscale1, gnorm1 = _norm_bwd("norm1_bwd", dh, xs, rstd1, norm1_w, scale1, dx1)
    gmod = jnp.concatenate([dshift1, dscale1, dgate1, dshift2, dscale2, dgate2], axis=1)
    return (loss_v, grad_x, gmod, gnorm1, gnorm2, glog, ghg, gqw, gkw, gconv_b, gconv_w, *large)


def kernel(x, c, w_ada, b_ada, norm1_w, w_in, lb_logits, hg_norm_w, q_norm_w, k_norm_w, w_out, norm2_w, w_up, conv_w, conv_b, w_down, loss_target, m_w_ada, m_b_ada, m_norm1_w, m_w_in, m_lb_logits, m_hg_norm_w, m_q_norm_w, m_k_norm_w, m_w_out, m_norm2_w, m_w_up, m_conv_w, m_conv_b, m_w_down, v_w_ada, v_b_ada, v_norm1_w, v_w_in, v_lb_logits, v_hg_norm_w, v_q_norm_w, v_k_norm_w, v_w_out, v_norm2_w, v_w_up, v_conv_w, v_conv_b, v_w_down):
    ix, iy, ic = lax.axis_index("x"), lax.axis_index("y"), lax.axis_index("c")
    me = 4 * ix + 2 * iy + ic
    my_chip = 2 * ix + iy

    xs = x[0]
    tgt = loss_target[0]

    win_g, = _allgather_weights([w_in[0].astype(BF16)])

    c_all = _allgather_vmem(c.reshape(8, D_MODEL // 8), "allgather_c").reshape(N_DEV, D_MODEL)
    b_blk = lax.dynamic_slice_in_dim(b_ada, me * ADA_BLK, ADA_BLK, axis=1)
    mod_cols = _ada_fwd(c_all, w_ada[0], b_blk)
    mod_all = _allgather_vmem(mod_cols, "allgather_mod").reshape(N_DEV, N_DEV, ADA_BLK)
    mod = lax.dynamic_index_in_dim(mod_all, me, axis=1, keepdims=False).reshape(6, 1, D_MODEL)

    conv_w_all = _allgather_vmem(_pack_rows([conv_w[0]], 24), "allgather_conv_w").reshape(N_DEV, 24 * 128)
    conv_w_full = conv_w_all[:, :3 * FF_BLK].reshape(N_DEV, 3, FF_BLK).transpose(1, 0, 2).reshape(3, D_FF)

    (loss_v, grad_x, gmod, gnorm1, gnorm2, glog, ghg, gqw, gkw, gconv_b, gconv_w,
     rs_in, rs_out, rs_up, rs_down) = _device_step(
        xs, tgt, mod, norm1_w, norm2_w, lb_logits, hg_norm_w, q_norm_w, k_norm_w, conv_w_full, conv_b,
        win_g, w_out[0].astype(BF16), w_up[0].astype(BF16), w_down[0].astype(BF16),
        core=jnp.reshape(ic, (1,)).astype(jnp.int32))
    loss = lax.psum(loss_v[0, 0], AXES)

    small_shapes = [(1, 6 * D_MODEL), (1, D_MODEL), (1, D_MODEL), (2, HEADS * HEAD_DIM), (1, HEAD_DIM),
                    (1, HEAD_DIM), (1, HEAD_DIM), (1, D_FF), (3, D_FF)]
    small = [gmod, gnorm1, gnorm2, glog, ghg, gqw, gkw, gconv_b, gconv_w]
    n_small = sum(a.size for a in small)
    rows = -(-n_small // 1024) * 8
    gathered = _allgather_vmem(_pack_rows(small, rows), "allgather_small").reshape(N_DEV, rows, 128)
    summed = _sum_devices(gathered).reshape(-1)
    (g_b_ada, g_norm1, g_norm2, g_lb, g_hg, g_q, g_k, g_conv_b, g_conv_w_full) = _unpack(summed, small_shapes)
    g_conv_w = lax.dynamic_slice_in_dim(g_conv_w_full, me * FF_BLK, FF_BLK, axis=1)

    gmod_all = gathered[:, :6 * D_MODEL // 128, :].reshape(N_DEV, 6 * D_MODEL)
    gmod_cols = lax.dynamic_slice_in_dim(gmod_all, me * ADA_BLK, ADA_BLK, axis=1)
    g_w_ada_raw = _ada_wgrad(c_all, gmod_cols)

    chip = jnp.reshape(my_chip, (1,)).astype(jnp.int32)

    def big_update(name, w, m, v, rs, tr):
        chip_sums, received = rs
        return _adamw_reduced(name, w[0], m[0], v[0], chip_sums, received, chip, tr)

    r_in = big_update("adamw_w_in", w_in, m_w_in, v_w_in, rs_in, 256)
    r_out = big_update("adamw_w_out", w_out, m_w_out, v_w_out, rs_out, 128)
    r_up = big_update("adamw_w_up", w_up, m_w_up, v_w_up, rs_up, 256)
    r_down = big_update("adamw_w_down", w_down, m_w_down, v_w_down, rs_down, 176)
    r_ada = _adamw("adamw_w_ada", w_ada[0], m_w_ada[0], v_w_ada[0], [g_w_ada_raw], tr=256)
    r_convw = _adamw("adamw_conv_w", conv_w[0], m_conv_w[0], v_conv_w[0], [g_conv_w])

    rep_shapes = [(1, 6 * D_MODEL), (1, D_MODEL), (1, D_MODEL), (2, HEADS * HEAD_DIM), (1, HEAD_DIM),
                  (1, HEAD_DIM), (1, HEAD_DIM), (1, D_FF)]
    rep_rows = -(-sum(a * b for a, b in rep_shapes) // 1024) * 8
    pack = lambda arrs: _pack_rows(arrs, rep_rows)
    rep = _adamw("adamw_small",
                 pack([b_ada, norm1_w, norm2_w, lb_logits, hg_norm_w, q_norm_w, k_norm_w, conv_b]),
                 pack([m_b_ada, m_norm1_w, m_norm2_w, m_lb_logits, m_hg_norm_w, m_q_norm_w, m_k_norm_w, m_conv_b]),
                 pack([v_b_ada, v_norm1_w, v_norm2_w, v_lb_logits, v_hg_norm_w, v_q_norm_w, v_k_norm_w, v_conv_b]),
                 [pack([g_b_ada, g_norm1, g_norm2, g_lb, g_hg, g_q, g_k, g_conv_b])])
    rep = [_unpack(r.reshape(-1), rep_shapes) for r in rep]

    def big(r):
        return [a[None] for a in r]

    order = {"w_ada": big(r_ada), "b_ada": [r[0] for r in rep], "norm1_w": [r[1] for r in rep],
             "w_in": big(r_in), "lb_logits": [r[3] for r in rep], "hg_norm_w": [r[4] for r in rep],
             "q_norm_w": [r[5] for r in rep], "k_norm_w": [r[6] for r in rep], "w_out": big(r_out),
             "norm2_w": [r[2] for r in rep], "w_up": big(r_up), "conv_w": big(r_convw),
             "conv_b": [r[7] for r in rep], "w_down": big(r_down)}
    names = ["w_ada", "b_ada", "norm1_w", "w_in", "lb_logits", "hg_norm_w", "q_norm_w", "k_norm_w", "w_out",
             "norm2_w", "w_up", "conv_w", "conv_b", "w_down"]
    outs = [loss, grad_x[None]]
    for kind in range(4):
        outs += [order[n][kind] for n in names]
    return tuple(outs)
```

```python
import functools

import jax
import jax.numpy as jnp
from jax import lax
from jax.experimental import pallas as pl
from jax.experimental.pallas import tpu as pltpu

F32 = jnp.float32
BF16 = jnp.bfloat16

N_DEV = 8
SEQ = 2048
D_MODEL = 2048
HEADS = 8
HEAD_DIM = 128
IN_COLS = 7168
IN_BLK = IN_COLS // N_DEV
D_FF = 5632
UP_BLK = 2 * D_FF // N_DEV
FF_BLK = D_FF // N_DEV
ADA_BLK = 6 * D_MODEL // N_DEV
OUT_BLK = D_MODEL // N_DEV
EPS = 1e-6
CHUNK = 16
ROW_TILE = 256
V7X_VMEM_LIMIT = 56 * 1024 * 1024

ADAM_LR = 0.001
ADAM_B1 = 0.9
ADAM_B2 = 0.999
ADAM_EPS = 1e-08
ADAM_WD = 0.01
ADAM_STEP = 10

NN = (((1,), (0,)), ((), ()))
NT = (((1,), (1,)), ((), ()))
TN = (((0,), (0,)), ((), ()))
MESH = pl.DeviceIdType.MESH
AXES = ("x", "y", "c")


def _params(sem=None, vmem=V7X_VMEM_LIMIT):
    return pltpu.CompilerParams(dimension_semantics=sem, vmem_limit_bytes=vmem)


def _sigmoid(x):
    return 1.0 / (1.0 + jnp.exp(-x))


def _dsilu(x, s):
    return s * (1.0 + x * (1.0 - s))


def _lane_sum(x, ones_bf16):
    return jnp.dot(x.astype(BF16), ones_bf16, preferred_element_type=F32)


def _mesh_pos():
    return lax.axis_index("x"), lax.axis_index("y"), lax.axis_index("c")


def _allgather_vmem(x_blk, name):
    m_per, n = x_blk.shape

    def body(x_ref, out_ref, send_sems, recv_sems, local_sem):
        x, y, c = _mesh_pos()
        me, sibling = (x, y, c), (x, y, 1 - c)
        chips = [(1 - x, y), (x, 1 - y), (1 - x, 1 - y)]

        def rows(px, py, pc):
            return out_ref.at[pl.ds((4 * px + 2 * py + pc) * m_per, m_per), :]

        def copy(k, block, to, src=None):
            return pltpu.make_async_remote_copy(
                src_ref=rows(*block) if src is None else src, dst_ref=rows(*block),
                send_sem=send_sems.at[k], recv_sem=recv_sems.at[k], device_id=to, device_id_type=MESH)

        mine = pltpu.make_async_copy(x_ref, rows(*me), local_sem)
        mine.start()
        first = [copy(0, me, sibling, src=x_ref)]
        first += [copy(1 + j, me, (*chip, c), src=x_ref) for j, chip in enumerate(chips)]
        for cp in first:
            cp.start()
        passed = [copy(4 + j, (*chip, c), sibling) for j, chip in enumerate(chips)]
        for j, chip in enumerate(chips):
            copy(1 + j, (*chip, c), me).wait_recv()
            passed[j].start()
        copy(0, sibling, me).wait_recv()
        for j, chip in enumerate(chips):
            copy(4 + j, (*chip, 1 - c), me).wait_recv()
        for cp in first + passed:
            cp.wait_send()
        mine.wait()

    return pl.pallas_call(
        body, name=name,
        out_shape=jax.ShapeDtypeStruct((N_DEV * m_per, n), x_blk.dtype),
        in_specs=[pl.BlockSpec(memory_space=pltpu.VMEM)],
        out_specs=pl.BlockSpec(memory_space=pltpu.VMEM),
        scratch_shapes=[pltpu.SemaphoreType.DMA((7,)), pltpu.SemaphoreType.DMA((7,)), pltpu.SemaphoreType.DMA],
    )(x_blk)


def _flip(v, bit):
    return v + bit - 2 * v * bit


def _relay_chips(x, y, c):
    return (_flip(x, 1 - c), _flip(y, c)), (_flip(x, c), _flip(y, 1 - c))


GATHER_PARTS = 4


def _allgather_weights(blocks):
    n_arr = len(blocks)
    parts = GATHER_PARTS

    def body(*refs):
        ins, outs = refs[:n_arr], refs[n_arr:2 * n_arr]
        send_sems, recv_sems, local_sems = refs[2 * n_arr:]
        x, y, c = _mesh_pos()
        me, sibling = (x, y, c), (x, y, 1 - c)
        near = [(1 - x, y), (x, 1 - y)]
        chips = near + [(1 - x, 1 - y)]
        relay_from, relay_to = _relay_chips(x, y, c)

        def rows(a, p):
            hr = ins[a].shape[0] // parts
            return pl.ds(p * hr, hr)

        def slot(a, pos, p):
            return outs[a].at[4 * pos[0] + 2 * pos[1] + pos[2], rows(a, p)]

        def copy(a, k, p, src, lands, to):
            return pltpu.make_async_remote_copy(
                src_ref=src, dst_ref=slot(a, lands, p), send_sem=send_sems.at[a, k, p], recv_sem=recv_sems.at[a, k, p],
                device_id=to, device_id_type=MESH)

        sent = []
        local = [pltpu.make_async_copy(ins[a], outs[a].at[4 * x + 2 * y + c], local_sems.at[a]) for a in range(n_arr)]
        for cp in local:
            cp.start()
        for p in range(parts):
            for a in range(n_arr):
                own = ins[a].at[rows(a, p)]
                sent.append(copy(a, 0, p, own, me, sibling))
                sent += [copy(a, 1 + j, p, own, me, (*chip, c)) for j, chip in enumerate(near)]
        for cp in sent:
            cp.start()

        def start(cp):
            cp.start()
            sent.append(cp)

        for p in range(parts):
            for a in range(n_arr):
                for j, chip in enumerate(near):
                    copy(a, 1 + j, p, ins[a].at[rows(a, p)], (*chip, c), me).wait_recv()
                    start(copy(a, 4 + j, p, slot(a, (*chip, c), p), (*chip, c), sibling))
                start(copy(a, 3, p, slot(a, (*relay_from, c), p), (*relay_from, c), (*relay_to, c)))
        for p in range(parts):
            for a in range(n_arr):
                copy(a, 3, p, ins[a].at[rows(a, p)], (*chips[2], c), me).wait_recv()
                start(copy(a, 6, p, slot(a, (*chips[2], c), p), (*chips[2], c), sibling))
        for p in range(parts):
            for a in range(n_arr):
                copy(a, 0, p, ins[a].at[rows(a, p)], sibling, me).wait_recv()
                for j, chip in enumerate(chips):
                    copy(a, 4 + j, p, ins[a].at[rows(a, p)], (*chip, 1 - c), me).wait_recv()
        for cp in sent:
            cp.wait_send()
        for cp in local:
            cp.wait()

    return pl.pallas_call(
        body, name="allgather_weights",
        out_shape=[jax.ShapeDtypeStruct((N_DEV,) + b.shape, b.dtype) for b in blocks],
        in_specs=[pl.BlockSpec(memory_space=pltpu.HBM)] * n_arr, out_specs=[pl.BlockSpec(memory_space=pltpu.HBM)] * n_arr,
        scratch_shapes=[pltpu.SemaphoreType.DMA((n_arr, 7, parts)), pltpu.SemaphoreType.DMA((n_arr, 7, parts)),
                        pltpu.SemaphoreType.DMA((n_arr,))],
    )(*blocks)


HBM_SPEC = pl.BlockSpec(memory_space=pltpu.HBM)


class _FusedCopies:
    def __init__(self, kind, arrays, peers=(0, 1, 2, 3)):
        self.kind = kind
        self.peers = peers
        n = len(arrays)
        self.n = n
        self.n_in = n
        self.aliases = {}
        if kind == "gather":
            self.out_shape = [jax.ShapeDtypeStruct((N_DEV,) + a.shape, a.dtype) for a in arrays]
            self.scratch_shapes = [pltpu.SemaphoreType.DMA((n, 4, GATHER_PARTS)),
                                   pltpu.SemaphoreType.DMA((n, 4, GATHER_PARTS)), pltpu.SemaphoreType.DMA((n,))]
        elif kind == "relay":
            self.out_shape = [jax.ShapeDtypeStruct(a.shape, a.dtype) for a in arrays]
            self.scratch_shapes = [pltpu.SemaphoreType.DMA((n,)), pltpu.SemaphoreType.DMA((n,))]
            self.aliases = {a: a for a in range(n)}
        elif kind == "forward":
            self.out_shape = [jax.ShapeDtypeStruct(a.shape, a.dtype) for a in arrays]
            self.scratch_shapes = [pltpu.SemaphoreType.DMA((n, 3)), pltpu.SemaphoreType.DMA((n, 3))]
            self.aliases = {a: a for a in range(n)}
        elif kind == "sibling":
            self.out_shape = [jax.ShapeDtypeStruct((4,) + a.shape[1:], a.dtype) for a in arrays]
            self.scratch_shapes = [pltpu.SemaphoreType.DMA((n, 4)), pltpu.SemaphoreType.DMA((n, 4))]
        else:
            self.out_shape = [jax.ShapeDtypeStruct((3,) + a.shape[1:], a.dtype) for a in arrays]
            self.scratch_shapes = [pltpu.SemaphoreType.DMA((n, 3)), pltpu.SemaphoreType.DMA((n, 3))]
        self.in_specs = [HBM_SPEC] * self.n_in
        self.out_specs = [HBM_SPEC] * n
        self.n_scratch = len(self.scratch_shapes)

    def copies(self, ins, outs, sems):
        x, y, c = _mesh_pos()
        chips = [(1 - x, y), (x, 1 - y), (1 - x, 1 - y)]
        sibling = (x, y, 1 - c)
        starts, waits = [], []
        if self.kind == "gather":
            send_sems, recv_sems, local_sems = sems
            me = (x, y, c)
            peers = [sibling] + [(px, py, c) for px, py in chips]

            def slot(a, pos):
                return outs[a].at[4 * pos[0] + 2 * pos[1] + pos[2]]

            def remote(a, k, p, lands_from):
                hr = ins[a].shape[0] // GATHER_PARTS
                rows = pl.ds(p * hr, hr)
                return pltpu.make_async_remote_copy(
                    src_ref=ins[a].at[rows], dst_ref=slot(a, lands_from).at[rows], send_sem=send_sems.at[a, k, p],
                    recv_sem=recv_sems.at[a, k, p], device_id=peers[k], device_id_type=MESH)

            for a in range(self.n):
                local = pltpu.make_async_copy(ins[a], slot(a, me), local_sems.at[a])
                starts.append(local)
                waits.append(local)
            for p in range(GATHER_PARTS):
                for a in range(self.n):
                    for k in self.peers:
                        starts.append(remote(a, k, p, me))
                        waits.append(remote(a, k, p, peers[k]))
        elif self.kind == "relay":
            send_sems, recv_sems = sems
            relay_from, relay_to = _relay_chips(x, y, c)

            def relayed(a, lands):
                return pltpu.make_async_remote_copy(
                    src_ref=ins[a].at[4 * relay_from[0] + 2 * relay_from[1] + c],
                    dst_ref=outs[a].at[4 * lands[0] + 2 * lands[1] + c], send_sem=send_sems.at[a],
                    recv_sem=recv_sems.at[a], device_id=(*relay_to, c), device_id_type=MESH)

            for a in range(self.n):
                starts.append(relayed(a, relay_from))
                waits.append(relayed(a, chips[2]))
        elif self.kind == "forward":
            send_sems, recv_sems = sems

            def passed_on(a, j, pc_src, pc_dst):
                px, py = chips[j]
                return pltpu.make_async_remote_copy(
                    src_ref=ins[a].at[4 * px + 2 * py + pc_src], dst_ref=outs[a].at[4 * px + 2 * py + pc_dst],
                    send_sem=send_sems.at[a, j], recv_sem=recv_sems.at[a, j], device_id=sibling, device_id_type=MESH)

            for a in range(self.n):
                for j in range(3):
                    starts.append(passed_on(a, j, c, c))
                    waits.append(passed_on(a, j, c, 1 - c))
        elif self.kind == "sibling":
            send_sems, recv_sems = sems
            for a in range(self.n):
                for q in range(4):
                    cp = pltpu.make_async_remote_copy(
                        src_ref=ins[a].at[2 * q + 1 - c], dst_ref=outs[a].at[q], send_sem=send_sems.at[a, q],
                        recv_sem=recv_sems.at[a, q], device_id=sibling, device_id_type=MESH)
                    starts.append(cp)
                    waits.append(cp)
        else:
            send_sems, recv_sems = sems
            for a in range(self.n):
                for j, (px, py) in enumerate(chips):
                    cp = pltpu.make_async_remote_copy(
                        src_ref=ins[a].at[2 * px + py], dst_ref=outs[a].at[j], send_sem=send_sems.at[a, j],
                        recv_sem=recv_sems.at[a, j], device_id=(px, py, c), device_id_type=MESH)
                    starts.append(cp)
                    waits.append(cp)
        return starts, waits


def _host_body(body, n_in, n_out, fused, first_last):
    if fused is None:
        return body
    n_fin, n_fout = fused.n_in, fused.n

    def wrapped(*refs):
        core_in, f_in = refs[:n_in], refs[n_in:n_in + n_fin]
        core_out = refs[n_in + n_fin:n_in + n_fin + n_out]
        f_out = refs[n_in + n_fin + n_out:n_in + n_fin + n_out + n_fout]
        rest = refs[n_in + n_fin + n_out + n_fout:]
        core_scratch, f_sems = rest[:len(rest) - fused.n_scratch], rest[len(rest) - fused.n_scratch:]
        starts, waits = fused.copies(f_in, f_out, f_sems)
        first, last = first_last()

        @pl.when(first)
        def _():
            for cp in starts:
                cp.start()

        body(*core_in, *core_out, *core_scratch)

        @pl.when(last)
        def _():
            for cp in waits:
                cp.wait()

    return wrapped


def _host_call(body, n_in, n_out, fused, first_last, *, name, grid, in_specs, out_specs, out_shape, scratch_shapes,
               sem, operands):
    aliases = {}
    if fused is not None:
        in_specs = list(in_specs) + fused.in_specs
        out_specs = list(out_specs) + fused.out_specs
        out_shape = list(out_shape) + fused.out_shape
        scratch_shapes = list(scratch_shapes) + fused.scratch_shapes
        sem = tuple("arbitrary" for _ in sem)
        aliases = {n_in + fi: n_out + fo for fi, fo in fused.aliases.items()}
    res = pl.pallas_call(_host_body(body, n_in, n_out, fused, first_last), name=name, grid=grid, in_specs=in_specs,
                         out_specs=out_specs, out_shape=out_shape, scratch_shapes=scratch_shapes,
                         input_output_aliases=aliases, compiler_params=_params(sem))(*operands)
    return list(res[:n_out]), list(res[n_out:])


def _forward_to_sibling(name, gathered):
    n_arr = len(gathered)

    def body(*refs):
        ins, outs = refs[:n_arr], refs[n_arr:2 * n_arr]
        send_sems, recv_sems = refs[2 * n_arr:]
        x, y, c = _mesh_pos()
        chips = [(1 - x, y), (x, 1 - y), (1 - x, 1 - y)]

        def copy(a, j, pc):
            px, py = chips[j]
            s = 4 * px + 2 * py + pc
            return pltpu.make_async_remote_copy(
                src_ref=ins[a].at[s], dst_ref=outs[a].at[s], send_sem=send_sems.at[a, j], recv_sem=recv_sems.at[a, j],
                device_id=(x, y, 1 - c), device_id_type=MESH)

        for a in range(n_arr):
            for j in range(3):
                copy(a, j, c).start()
        for a in range(n_arr):
            for j in range(3):
                copy(a, j, 1 - c).wait_recv()
                copy(a, j, c).wait_send()

    return pl.pallas_call(
        body, name=name,
        out_shape=[jax.ShapeDtypeStruct(g.shape, g.dtype) for g in gathered],
        in_specs=[HBM_SPEC] * n_arr, out_specs=[HBM_SPEC] * n_arr,
        input_output_aliases={a: a for a in range(n_arr)},
        scratch_shapes=[pltpu.SemaphoreType.DMA((n_arr, 3)), pltpu.SemaphoreType.DMA((n_arr, 3))],
    )(*gathered)


def _exchange_sibling(name, partials):
    n_arr = len(partials)

    def body(*refs):
        ins, outs = refs[:n_arr], refs[n_arr:2 * n_arr]
        send_sems, recv_sems = refs[2 * n_arr:]
        x, y, c = _mesh_pos()
        copies = [pltpu.make_async_remote_copy(
            src_ref=ins[a].at[2 * q + 1 - c], dst_ref=outs[a].at[q], send_sem=send_sems.at[a, q],
            recv_sem=recv_sems.at[a, q], device_id=(x, y, 1 - c), device_id_type=MESH)
            for a in range(n_arr) for q in range(4)]
        for cp in copies:
            cp.start()
        for cp in copies:
            cp.wait_recv()
        for cp in copies:
            cp.wait_send()

    return pl.pallas_call(
        body, name=name,
        out_shape=[jax.ShapeDtypeStruct((4,) + p.shape[1:], p.dtype) for p in partials],
        in_specs=[HBM_SPEC] * n_arr, out_specs=[HBM_SPEC] * n_arr,
        scratch_shapes=[pltpu.SemaphoreType.DMA((n_arr, 4)), pltpu.SemaphoreType.DMA((n_arr, 4))],
    )(*partials)


def _matmul(name, a, b, dims, grid, a_spec, b_spec, o_spec, out_shape, acc_axis=None, fused=None, fused_arrays=()):
    def body(a_ref, b_ref, o_ref):
        r = lax.dot_general(a_ref[...], b_ref[...], dims, preferred_element_type=F32)
        if acc_axis is None:
            o_ref[...] = r.astype(o_ref.dtype)
        else:
            k = pl.program_id(acc_axis)

            @pl.when(k == 0)
            def _():
                o_ref[...] = r

            @pl.when(k > 0)
            def _():
                o_ref[...] += r

    sem = tuple("arbitrary" if i == acc_axis else "parallel" for i in range(len(grid)))
    if fused is None:
        return pl.pallas_call(body, name=name, grid=grid, in_specs=[a_spec, b_spec], out_specs=o_spec,
                              out_shape=out_shape, compiler_params=_params(sem))(a, b)

    def first_last():
        first = last = None
        for ax, n in enumerate(grid):
            f, l = pl.program_id(ax) == 0, pl.program_id(ax) == n - 1
            first, last = (f, l) if first is None else (first & f, last & l)
        return first, last

    (out,), extra = _host_call(body, 2, 1, fused, first_last, name=name, grid=grid, in_specs=[a_spec, b_spec],
                               out_specs=[o_spec], out_shape=[out_shape], scratch_shapes=[], sem=sem,
                               operands=[a, b] + list(fused_arrays))
    return out, extra


def _mm_blocked_rhs(name, a, w_g, tm=512, fused=None, fused_arrays=()):
    m, k = a.shape
    nb = w_g.shape[2]
    return _matmul(name, a, w_g, NN, (N_DEV, m // tm),
                   pl.BlockSpec((tm, k), lambda j, i: (i, 0)),
                   pl.BlockSpec((None, k, nb), lambda j, i: (j, 0, 0)),
                   pl.BlockSpec((tm, nb), lambda j, i: (i, j)),
                   jax.ShapeDtypeStruct((m, N_DEV * nb), F32), fused=fused, fused_arrays=fused_arrays)


def _mm_blocked_rhs_t(name, a, w_g, tm=512, fused=None, fused_arrays=()):
    m = a.shape[0]
    n, nb = w_g.shape[1], w_g.shape[2]
    return _matmul(name, a, w_g, NT, (m // tm, N_DEV),
                   pl.BlockSpec((tm, nb), lambda i, j: (i, j)),
                   pl.BlockSpec((None, n, nb), lambda i, j: (j, 0, 0)),
                   pl.BlockSpec((tm, n), lambda i, j: (i, 0)),
                   jax.ShapeDtypeStruct((m, n), F32), acc_axis=1, fused=fused, fused_arrays=fused_arrays)


def _mm_wgrad_blocked(name, act, dcols, tk=512):
    t, k = act.shape
    nb = dcols.shape[1] // N_DEV
    return _matmul(name, act, dcols, TN, (N_DEV, k // tk),
                   pl.BlockSpec((t, tk), lambda j, i: (0, i)),
                   pl.BlockSpec((t, nb), lambda j, i: (0, j)),
                   pl.BlockSpec((None, tk, nb), lambda j, i: (j, i, 0)),
                   jax.ShapeDtypeStruct((N_DEV, k, nb), BF16))


def _halves_specs(block, index):
    half = N_DEV // 2
    return (pl.BlockSpec(block, lambda i, j: index(i, jnp.minimum(j, half - 1))),
            pl.BlockSpec(block, lambda i, j: index(i, jnp.maximum(j - half, 0))))


def _mm_halves_rhs_t(name, a_lo, a_hi, w_g, tm=512):
    m = a_lo.shape[0]
    n, nb = w_g.shape[1], w_g.shape[2]

    def body(lo_ref, hi_ref, b_ref, o_ref):
        j = pl.program_id(1)

        def accumulate(a_ref):
            r = lax.dot_general(a_ref[...], b_ref[...], NT, preferred_element_type=F32)

            @pl.when(j == 0)
            def _():
                o_ref[...] = r

            @pl.when(j > 0)
            def _():
                o_ref[...] += r

        pl.when(j < N_DEV // 2)(lambda: accumulate(lo_ref))
        pl.when(j >= N_DEV // 2)(lambda: accumulate(hi_ref))

    lo_spec, hi_spec = _halves_specs((tm, nb), lambda i, j: (i, j))
    return pl.pallas_call(
        body, name=name, grid=(m // tm, N_DEV),
        in_specs=[lo_spec, hi_spec, pl.BlockSpec((None, n, nb), lambda i, j: (j, 0, 0))],
        out_specs=pl.BlockSpec((tm, n), lambda i, j: (i, 0)), out_shape=jax.ShapeDtypeStruct((m, n), F32),
        compiler_params=_params(("parallel", "arbitrary")))(a_lo, a_hi, w_g)


def _mm_halves_wgrad(name, act, d_lo, d_hi, tk=512):
    t, k = act.shape
    nb = d_lo.shape[1] // (N_DEV // 2)

    def body(a_ref, lo_ref, hi_ref, o_ref):
        j = pl.program_id(0)

        def product(d_ref):
            o_ref[...] = lax.dot_general(a_ref[...], d_ref[...], TN, preferred_element_type=F32).astype(o_ref.dtype)

        pl.when(j < N_DEV // 2)(lambda: product(lo_ref))
        pl.when(j >= N_DEV // 2)(lambda: product(hi_ref))

    half = N_DEV // 2
    return pl.pallas_call(
        body, name=name, grid=(N_DEV, k // tk),
        in_specs=[pl.BlockSpec((t, tk), lambda j, i: (0, i)),
                  pl.BlockSpec((t, nb), lambda j, i: (0, jnp.minimum(j, half - 1))),
                  pl.BlockSpec((t, nb), lambda j, i: (0, jnp.maximum(j - half, 0)))],
        out_specs=pl.BlockSpec((None, tk, nb), lambda j, i: (j, i, 0)),
        out_shape=jax.ShapeDtypeStruct((N_DEV, k, nb), BF16),
        compiler_params=_params(("parallel", "parallel")))(act, d_lo, d_hi)


def _mm_plain(name, a, b, dims, tm, tn, out_dtype, fused=None, fused_arrays=()):
    if dims == NN:
        (m, k), n = a.shape, b.shape[1]
        a_spec = pl.BlockSpec((tm, k), lambda i, j: (i, 0))
        b_spec = pl.BlockSpec((k, tn), lambda i, j: (0, j))
    elif dims == NT:
        (m, k), n = a.shape, b.shape[0]
        a_spec = pl.BlockSpec((tm, k), lambda i, j: (i, 0))
        b_spec = pl.BlockSpec((tn, k), lambda i, j: (j, 0))
    else:
        (k, m), n = a.shape, b.shape[1]
        a_spec = pl.BlockSpec((k, tm), lambda i, j: (0, i))
        b_spec = pl.BlockSpec((k, tn), lambda i, j: (0, j))
    return _matmul(name, a, b, dims, (m // tm, n // tn), a_spec, b_spec,
                   pl.BlockSpec((tm, tn), lambda i, j: (i, j)), jax.ShapeDtypeStruct((m, n), out_dtype),
                   fused=fused, fused_arrays=fused_arrays)


def _ada_fwd(c_all, w_ada_blk, b_blk):
    def body(c_ref, w_ref, b_ref, o_ref):
        cv = c_ref[...]
        o_ref[...] = jnp.dot(cv * _sigmoid(cv), w_ref[...], preferred_element_type=F32) + b_ref[...]

    tn = 512
    return pl.pallas_call(
        body, name="ada_fwd", grid=(ADA_BLK // tn,),
        in_specs=[pl.BlockSpec((N_DEV, D_MODEL), lambda j: (0, 0)),
                  pl.BlockSpec((D_MODEL, tn), lambda j: (0, j)),
                  pl.BlockSpec((1, tn), lambda j: (0, j))],
        out_specs=pl.BlockSpec((N_DEV, tn), lambda j: (0, j)),
        out_shape=jax.ShapeDtypeStruct((N_DEV, ADA_BLK), F32),
        compiler_params=_params(("parallel",)))(c_all, w_ada_blk, b_blk)


def _ada_wgrad(c_all, gmod_cols):
    def body(c_ref, g_ref, o_ref):
        cv = c_ref[...]
        o_ref[...] = lax.dot_general(cv * _sigmoid(cv), g_ref[...], TN, preferred_element_type=F32)

    tk = 512
    return pl.pallas_call(
        body, name="ada_wgrad", grid=(D_MODEL // tk,),
        in_specs=[pl.BlockSpec((N_DEV, tk), lambda i: (0, i)),
                  pl.BlockSpec((N_DEV, ADA_BLK), lambda i: (0, 0))],
        out_specs=pl.BlockSpec((tk, ADA_BLK), lambda i: (i, 0)),
        out_shape=jax.ShapeDtypeStruct((D_MODEL, ADA_BLK), F32),
        compiler_params=_params(("parallel",)))(c_all, gmod_cols)


def _row_spec(cols=D_MODEL):
    return pl.BlockSpec((ROW_TILE, cols), lambda i: (i, 0))


def _vec_spec(cols=D_MODEL):
    return pl.BlockSpec((1, cols), lambda i: (0, 0))


def _norm_fwd(name, x, w, scale, shift, resid=None, gate=None):
    has_res = resid is not None

    def body(*refs):
        if has_res:
            x_ref, r_ref, g_ref, w_ref, sc_ref, sh_ref, xr_ref, h_ref, rs_ref = refs
            xr = x_ref[...] + g_ref[...] * r_ref[...]
            xr_ref[...] = xr
        else:
            x_ref, w_ref, sc_ref, sh_ref, h_ref, rs_ref = refs
            xr = x_ref[...]
        rs = lax.rsqrt(jnp.mean(xr * xr, axis=-1, keepdims=True) + EPS)
        h = (xr * rs) * w_ref[...] * (1.0 + sc_ref[...]) + sh_ref[...]
        h_ref[...] = h.astype(BF16)
        rs_ref[...] = rs

    s = x.shape[0]
    ins = [x] + ([resid, gate] if has_res else []) + [w, scale, shift]
    in_specs = [_row_spec()] + ([_row_spec(), _vec_spec()] if has_res else []) + [_vec_spec()] * 3
    outs = ([jax.ShapeDtypeStruct((s, D_MODEL), F32)] if has_res else []) + [
        jax.ShapeDtypeStruct((s, D_MODEL), BF16), jax.ShapeDtypeStruct((s, 1), F32)]
    out_specs = ([_row_spec()] if has_res else []) + [_row_spec(), pl.BlockSpec((ROW_TILE, 1), lambda i: (i, 0))]
    return pl.pallas_call(body, name=name, grid=(s // ROW_TILE,), in_specs=in_specs, out_specs=out_specs,
                          out_shape=outs, compiler_params=_params(("parallel",)))(*ins)


def _norm_bwd(name, dh, x, rstd, w, scale, dres, mix=None, gate=None, fused=None, fused_arrays=()):
    has_mix = mix is not None

    def body(*refs):
        if has_mix:
            (dh_ref, x_ref, rs_ref, w_ref, sc_ref, dr_ref, mix_ref, g_ref,
             dx_ref, dmix_ref, dsh_ref, dsc_ref, dw_ref, dg_ref) = refs
        else:
            dh_ref, x_ref, rs_ref, w_ref, sc_ref, dr_ref, dx_ref, dsh_ref, dsc_ref, dw_ref = refs
        i = pl.program_id(0)
        dhv = dh_ref[...]
        rs = rs_ref[...]
        xn = x_ref[...] * rs
        wv = w_ref[...]
        one_sc = 1.0 + sc_ref[...]
        dxn = dhv * wv * one_sc
        dx = dr_ref[...] + rs * (dxn - xn * jnp.mean(dxn * xn, axis=-1, keepdims=True))
        dx_ref[...] = dx
        sums = [(dsh_ref, dhv), (dsc_ref, dhv * xn * wv), (dw_ref, dhv * one_sc * xn)]
        if has_mix:
            dmix_ref[...] = (dx * g_ref[...]).astype(BF16)
            sums.append((dg_ref, dx * mix_ref[...]))

        @pl.when(i == 0)
        def _():
            for ref, _v in sums:
                ref[...] = jnp.zeros_like(ref)

        for ref, v in sums:
            ref[...] += jnp.sum(v, axis=0, keepdims=True)

    s = x.shape[0]
    ins = [dh, x, rstd, w, scale, dres] + ([mix, gate] if has_mix else [])
    in_specs = ([_row_spec(), _row_spec(), pl.BlockSpec((ROW_TILE, 1), lambda i: (i, 0)), _vec_spec(), _vec_spec(),
                 _row_spec()] + ([_row_spec(), _vec_spec()] if has_mix else []))
    vec = jax.ShapeDtypeStruct((1, D_MODEL), F32)
    outs = ([jax.ShapeDtypeStruct((s, D_MODEL), F32)] + ([jax.ShapeDtypeStruct((s, D_MODEL), BF16)] if has_mix else [])
            + [vec] * (4 if has_mix else 3))
    out_specs = [_row_spec()] + ([_row_spec()] if has_mix else []) + [_vec_spec()] * (4 if has_mix else 3)

    def first_last():
        i = pl.program_id(0)
        return i == 0, i == s // ROW_TILE - 1

    res, extra = _host_call(body, len(ins), len(outs), fused, first_last, name=name, grid=(s // ROW_TILE,),
                            in_specs=in_specs, out_specs=out_specs, out_shape=outs, scratch_shapes=[],
                            sem=("arbitrary",), operands=ins + list(fused_arrays))
    return res if fused is None else (res, extra)


def _loss_head(x1, ffn, gate2, target):
    def body(x_ref, f_ref, g_ref, t_ref, loss_ref, dout_ref, dffn_ref, dg_ref):
        i = pl.program_id(0)
        fv = f_ref[...]
        gv = g_ref[...]
        err = x_ref[...] + gv * fv - t_ref[...]
        dout = err * (1.0 / D_MODEL)
        dout_ref[...] = dout
        dffn_ref[...] = (dout * gv).astype(BF16)

        @pl.when(i == 0)
        def _():
            loss_ref[...] = jnp.zeros_like(loss_ref)
            dg_ref[...] = jnp.zeros_like(dg_ref)

        row = jnp.sum(err * err, axis=-1, keepdims=True) * (1.0 / D_MODEL)
        loss_ref[...] += jnp.broadcast_to(0.5 * jnp.sum(row, axis=0, keepdims=True), (1, 128))
        dg_ref[...] += jnp.sum(dout * fv, axis=0, keepdims=True)

    s = x1.shape[0]
    return pl.pallas_call(
        body, name="loss_head", grid=(s // ROW_TILE,),
        in_specs=[_row_spec(), _row_spec(), _vec_spec(), _row_spec()],
        out_specs=[pl.BlockSpec((1, 128), lambda i: (0, 0)), _row_spec(), _row_spec(), _vec_spec()],
        out_shape=[jax.ShapeDtypeStruct((1, 128), F32), jax.ShapeDtypeStruct((s, D_MODEL), F32),
                   jax.ShapeDtypeStruct((s, D_MODEL), BF16), jax.ShapeDtypeStruct((1, D_MODEL), F32)],
        compiler_params=_params(("arbitrary",)))(x1, ffn, gate2, target)


CONV_TILE = 512
N_CONV_TILES = D_FF // CONV_TILE


def _shift_rows(a, k, row):
    n = a.shape[0]
    if k > 0:
        return jnp.where(row >= k, pltpu.roll(a, k, 0), 0.0)
    return jnp.where(row < n + k, pltpu.roll(a, n + k, 0), 0.0)


def _conv_gate_fwd(u, conv_w, conv_b, fused=None, fused_arrays=()):
    s = u.shape[0]

    def body(a_ref, g_ref, w_ref, b_ref, y_ref):
        a = a_ref[...]
        w = w_ref[...]
        row = lax.broadcasted_iota(jnp.int32, a.shape, 0)
        ac = b_ref[...] + _shift_rows(a, 2, row) * w[0:1] + _shift_rows(a, 1, row) * w[1:2] + a * w[2:3]
        y_ref[...] = (ac * _sigmoid(ac) * g_ref[...]).astype(BF16)

    def first_last():
        i = pl.program_id(0)
        return i == 0, i == N_CONV_TILES - 1

    col = lambda off: pl.BlockSpec((s, CONV_TILE), lambda i: (0, i + off))
    (y,), extra = _host_call(
        body, 4, 1, fused, first_last, name="conv_gate_fwd", grid=(N_CONV_TILES,),
        in_specs=[col(0), col(N_CONV_TILES), pl.BlockSpec((3, CONV_TILE), lambda i: (0, i)),
                  pl.BlockSpec((1, CONV_TILE), lambda i: (0, i))],
        out_specs=[col(0)], out_shape=[jax.ShapeDtypeStruct((s, D_FF), BF16)], scratch_shapes=[], sem=("parallel",),
        operands=[u, u, conv_w, conv_b] + list(fused_arrays))
    return y if fused is None else (y, extra)


def _conv_gate_bwd(u, dy, conv_w, conv_b):
    s = u.shape[0]

    def body(a_ref, g_ref, dy_ref, w_ref, b_ref, da_ref, dg_ref, gw_ref, gb_ref):
        a = a_ref[...]
        w = w_ref[...]
        row = lax.broadcasted_iota(jnp.int32, a.shape, 0)
        a1 = _shift_rows(a, 1, row)
        a2 = _shift_rows(a, 2, row)
        ac = b_ref[...] + a2 * w[0:1] + a1 * w[1:2] + a * w[2:3]
        sg = _sigmoid(ac)
        dyv = dy_ref[...].astype(F32)
        dg_ref[...] = (dyv * (ac * sg)).astype(BF16)
        dac = dyv * g_ref[...] * _dsilu(ac, sg)
        gb_ref[...] = jnp.sum(dac, axis=0, keepdims=True)
        gw_ref[0:1, :] = jnp.sum(dac * a2, axis=0, keepdims=True)
        gw_ref[1:2, :] = jnp.sum(dac * a1, axis=0, keepdims=True)
        gw_ref[2:3, :] = jnp.sum(dac * a, axis=0, keepdims=True)
        da = dac * w[2:3] + _shift_rows(dac, -1, row) * w[1:2] + _shift_rows(dac, -2, row) * w[0:1]
        da_ref[...] = da.astype(BF16)

    col = lambda off: pl.BlockSpec((s, CONV_TILE), lambda i: (0, i + off))
    return pl.pallas_call(
        body, name="conv_gate_bwd", grid=(N_CONV_TILES,),
        in_specs=[col(0), col(N_CONV_TILES), col(0), pl.BlockSpec((3, CONV_TILE), lambda i: (0, i)),
                  pl.BlockSpec((1, CONV_TILE), lambda i: (0, i))],
        out_specs=[col(0), col(0), pl.BlockSpec((3, CONV_TILE), lambda i: (0, i)),
                   pl.BlockSpec((1, CONV_TILE), lambda i: (0, i))],
        out_shape=[jax.ShapeDtypeStruct((s, D_FF), BF16), jax.ShapeDtypeStruct((s, D_FF), BF16),
                   jax.ShapeDtypeStruct((3, D_FF), F32), jax.ShapeDtypeStruct((1, D_FF), F32)],
        compiler_params=_params(("parallel",)))(u, u, dy, conv_w, conv_b)


HG_TILE = 256
CHUNK_UNROLL = 8


def _unrolled_loop(n, body, init):
    def group(i, carry):
        for u in range(CHUNK_UNROLL):
            carry = body(i * CHUNK_UNROLL + u, carry)
        return carry

    return lax.fori_loop(0, n // CHUNK_UNROLL, group, init)


def _head_col(off):
    return pl.BlockSpec((SEQ, HEAD_DIM), lambda h: (0, h + off))


def _hgrn_gates(hq, hf, lb, pos):
    q = hq * _sigmoid(hq)
    sig = _sigmoid(hf)
    f = lb + (1.0 - lb) * sig
    gl = jnp.log(f)
    for sh in (1, 2, 4, 8):
        gl = gl + jnp.where(pos >= sh, pltpu.roll(gl, sh, 0), 0.0)
    return q, sig, f, 1.0 - f, gl


def _lower_bound(lbl):
    return 1.0 / (1.0 + jnp.exp(lbl[1:2, :] - lbl[0:1, :]))


def _head_first_last():
    h = pl.program_id(0)
    return h == 0, h == HEADS - 1


CHUNKS_PER_TILE = HG_TILE // CHUNK


def _chunk_end(x, pos):
    y = jnp.where(pos == CHUNK - 1, x, 0.0)
    for sh in (1, 2, 4, 8):
        y = y + jnp.where(pos < CHUNK - sh, pltpu.roll(y, x.shape[0] - sh, 0), 0.0)
    return y


def _suffix_in_chunk(x, pos):
    for sh in (1, 2, 4, 8):
        x = x + jnp.where(pos < CHUNK - sh, pltpu.roll(x, x.shape[0] - sh, 0), 0.0)
    return x


def _prefix_in_chunk(x, pos):
    for sh in (1, 2, 4, 8):
        x = x + jnp.where(pos >= sh, pltpu.roll(x, sh, 0), 0.0)
    return x


def _pair_decays(f, pos):
    shifted = jnp.where(pos >= 1, f, 0.0)
    e = shifted
    yield 1, e
    for d in range(2, CHUNK):
        shifted = pltpu.roll(shifted, 1, 0)
        e = e * shifted
        yield d, e


def _chunk_rows(cc):
    return slice(cc * CHUNK, (cc + 1) * CHUNK)


def _outer_products(lhs_b, rhs_b, dst, i):
    for cc in range(CHUNKS_PER_TILE):
        dst[i * CHUNKS_PER_TILE + cc] = lax.dot_general(lhs_b[_chunk_rows(cc)], rhs_b[_chunk_rows(cc)], TN,
                                                        preferred_element_type=F32)


def _state_scan(n_chunks, gl_s, u_s, keep, reverse):
    def step(k, st):
        c = n_chunks - 1 - k if reverse else k
        keep[c] = st.astype(BF16)
        gl = gl_s[pl.ds(pl.multiple_of(c * CHUNK, CHUNK), CHUNK), :]
        return st * jnp.exp(gl[CHUNK - 1:CHUNK, :]) + u_s[c]

    _unrolled_loop(n_chunks, step, jnp.zeros((HEAD_DIM, HEAD_DIM), F32))


def _hgrn_fwd(proj, lb_logits, norm_w, fused=None, fused_arrays=()):
    n_tiles = SEQ // HG_TILE
    n_chunks = SEQ // CHUNK
    fused_arrays = list(fused_arrays)

    def body(hq_ref, hf_ref, hi_ref, hg_ref, lbl_ref, nw_ref, aout_ref, opre_ref, qt_s, gl_s, u_s, st_s):
        lb = _lower_bound(lbl_ref[...])
        ones = jnp.ones((HEAD_DIM, HEAD_DIM), BF16)
        pos = lax.broadcasted_iota(jnp.int32, (HG_TILE, HEAD_DIM), 0) % CHUNK

        def tile(i, carry):
            rows = pl.ds(pl.multiple_of(i * HG_TILE, HG_TILE), HG_TILE)
            v = hi_ref[rows, :]
            q, _sig, f, kk, gl = _hgrn_gates(hq_ref[rows, :], hf_ref[rows, :], lb, pos)
            o = _lane_sum(q * kk, ones) * v
            for d, e in _pair_decays(f, pos):
                o = o + _lane_sum(q * pltpu.roll(kk, d, 0) * e, ones) * pltpu.roll(v, d, 0)
            opre_ref[rows, :] = o
            qt_s[rows, :] = q * jnp.exp(gl)
            gl_s[rows, :] = gl
            kt = kk * jnp.exp(_chunk_end(gl, pos) - gl)
            _outer_products(v.astype(BF16), kt.astype(BF16), u_s, i)
            return carry

        lax.fori_loop(0, n_tiles, tile, 0)
        _state_scan(n_chunks, gl_s, u_s, st_s, reverse=False)

        def finish(i, carry):
            rows = pl.ds(pl.multiple_of(i * HG_TILE, HG_TILE), HG_TILE)
            qt_b = qt_s[rows, :].astype(BF16)
            past = [lax.dot_general(qt_b[_chunk_rows(cc)], st_s[i * CHUNKS_PER_TILE + cc], NT,
                                    preferred_element_type=F32) for cc in range(CHUNKS_PER_TILE)]
            o = opre_ref[rows, :] + jnp.concatenate(past, axis=0)
            opre_ref[rows, :] = o
            hg = hg_ref[rows, :]
            rs = lax.rsqrt(jnp.mean(o * o, axis=-1, keepdims=True) + EPS)
            aout_ref[rows, :] = ((o * rs) * nw_ref[...] * (hg * _sigmoid(hg))).astype(BF16)
            return carry

        lax.fori_loop(0, n_tiles, finish, 0)

    return _host_call(
        body, 6, 2, fused, _head_first_last, name="hgrn_fwd", grid=(HEADS,),
        in_specs=[_head_col(0), _head_col(HEADS), _head_col(2 * HEADS), _head_col(3 * HEADS),
                  pl.BlockSpec((2, HEAD_DIM), lambda h: (0, h)), pl.BlockSpec((1, HEAD_DIM), lambda h: (0, 0))],
        out_specs=[_head_col(0), _head_col(0)],
        out_shape=[jax.ShapeDtypeStruct((SEQ, HEADS * HEAD_DIM), BF16), jax.ShapeDtypeStruct((SEQ, HEADS * HEAD_DIM), F32)],
        scratch_shapes=[pltpu.VMEM((SEQ, HEAD_DIM), F32)] * 2 + [pltpu.VMEM((n_chunks, HEAD_DIM, HEAD_DIM), F32),
                                                                 pltpu.VMEM((n_chunks, HEAD_DIM, HEAD_DIM), BF16)],
        sem=("parallel",), operands=[proj, proj, proj, proj, lb_logits, norm_w] + fused_arrays)


def _hgrn_bwd(proj, lb_logits, norm_w, o_pre, d_aout, fused=None, fused_arrays=()):
    n_tiles = SEQ // HG_TILE
    n_chunks = SEQ // CHUNK

    def body(hq_ref, hf_ref, hi_ref, hg_ref, lbl_ref, nw_ref, opre_ref, da_ref,
             dhq_ref, dhf_ref, dhi_ref, dhg_ref, dlog_ref, gnw_ref,
             q_s, k_s, gl_s, do_s, dq_s, dk_s, dv_s, u_s, st_s, rt_s):
        h = pl.program_id(0)
        lb = _lower_bound(lbl_ref[...])
        nw = nw_ref[...]
        ones = jnp.ones((HEAD_DIM, HEAD_DIM), BF16)
        pos = lax.broadcasted_iota(jnp.int32, (HG_TILE, HEAD_DIM), 0) % CHUNK

        @pl.when(h == 0)
        def _():
            gnw_ref[...] = jnp.zeros_like(gnw_ref)

        def tile(i, carry):
            rows = pl.ds(pl.multiple_of(i * HG_TILE, HG_TILE), HG_TILE)
            v = hi_ref[rows, :]
            q, _sig, f, kk, gl = _hgrn_gates(hq_ref[rows, :], hf_ref[rows, :], lb, pos)
            o = opre_ref[rows, :]
            hg = hg_ref[rows, :]
            da = da_ref[rows, :]
            rs = lax.rsqrt(jnp.mean(o * o, axis=-1, keepdims=True) + EPS)
            oh = o * rs
            sg = _sigmoid(hg)
            dnorm = da * (hg * sg)
            dhg_ref[rows, :] = (da * (oh * nw) * _dsilu(hg, sg)).astype(BF16)
            gnw_ref[...] += jnp.sum(dnorm * oh, axis=0, keepdims=True)
            doh = dnorm * nw
            do = rs * (doh - oh * jnp.mean(doh * oh, axis=-1, keepdims=True))

            d_a = _lane_sum(do * v, ones)
            dq = d_a * kk
            dk = d_a * q
            dv = _lane_sum(q * kk, ones) * do
            for d, e in _pair_decays(f, pos):
                ks = pltpu.roll(kk, d, 0)
                a_d = _lane_sum(q * ks * e, ones)
                d_a = _lane_sum(do * pltpu.roll(v, d, 0), ones) * e
                dq = dq + d_a * ks
                dk = dk + pltpu.roll(d_a * q, HG_TILE - d, 0)
                dv = dv + pltpu.roll(a_d * do, HG_TILE - d, 0)
            q_s[rows, :] = q
            k_s[rows, :] = kk
            gl_s[rows, :] = gl
            do_s[rows, :] = do
            dq_s[rows, :] = dq
            dk_s[rows, :] = dk
            dv_s[rows, :] = dv
            kt = kk * jnp.exp(_chunk_end(gl, pos) - gl)
            _outer_products(v.astype(BF16), kt.astype(BF16), u_s, i)
            return carry

        lax.fori_loop(0, n_tiles, tile, 0)
        _state_scan(n_chunks, gl_s, u_s, st_s, reverse=False)

        def reverse_increments(i, carry):
            rows = pl.ds(pl.multiple_of(i * HG_TILE, HG_TILE), HG_TILE)
            qt = q_s[rows, :] * jnp.exp(gl_s[rows, :])
            _outer_products(do_s[rows, :].astype(BF16), qt.astype(BF16), u_s, i)
            return carry

        lax.fori_loop(0, n_tiles, reverse_increments, 0)
        _state_scan(n_chunks, gl_s, u_s, rt_s, reverse=True)

        def finish(i, dlb):
            rows = pl.ds(pl.multiple_of(i * HG_TILE, HG_TILE), HG_TILE)
            q = q_s[rows, :]
            kk = k_s[rows, :]
            gl = gl_s[rows, :]
            gll = _chunk_end(gl, pos)
            ekt = jnp.exp(gll - gl)
            do_b = do_s[rows, :].astype(BF16)
            v_b = hi_ref[rows, :].astype(BF16)
            kt_b = (kk * ekt).astype(BF16)
            dq_far, dk_far, dv_far, across = [], [], [], []
            for cc in range(CHUNKS_PER_TILE):
                st = st_s[i * CHUNKS_PER_TILE + cc]
                rt = rt_s[i * CHUNKS_PER_TILE + cc]
                sl = _chunk_rows(cc)
                dq_far.append(jnp.dot(do_b[sl], st, preferred_element_type=F32))
                dk_far.append(jnp.dot(v_b[sl], rt, preferred_element_type=F32))
                dv_far.append(lax.dot_general(kt_b[sl], rt, NT, preferred_element_type=F32))
                both = jnp.sum(st.astype(F32) * rt.astype(F32), axis=0, keepdims=True)
                across.append(jnp.broadcast_to(both, (CHUNK, HEAD_DIM)))
            dq = dq_s[rows, :] + jnp.concatenate(dq_far, axis=0) * jnp.exp(gl)
            dk_in = dk_s[rows, :]
            dk_out = jnp.concatenate(dk_far, axis=0) * ekt
            dk = dk_in + dk_out
            dv = dv_s[rows, :] + jnp.concatenate(dv_far, axis=0)
            pc = kk * dk_out
            dgl = (_suffix_in_chunk(q * dq - kk * dk_in, pos) + (_prefix_in_chunk(pc, pos) - pc)
                   + jnp.concatenate(across, axis=0) * jnp.exp(gll))
            hf = hf_ref[rows, :]
            sig = _sigmoid(hf)
            f = lb + (1.0 - lb) * sig
            df = dgl / f - dk
            dhf_ref[rows, :] = (df * (1.0 - lb) * sig * (1.0 - sig)).astype(BF16)
            hq = hq_ref[rows, :]
            dhq_ref[rows, :] = (dq * _dsilu(hq, _sigmoid(hq))).astype(BF16)
            dhi_ref[rows, :] = dv.astype(BF16)
            return dlb + jnp.sum(df * (1.0 - sig), axis=0, keepdims=True)

        dlb = lax.fori_loop(0, n_tiles, finish, jnp.zeros((1, HEAD_DIM), F32))
        dl0 = lb * (1.0 - lb) * dlb
        dlog_ref[0:1, :] = dl0
        dlog_ref[1:2, :] = -dl0

    wide = HEADS * HEAD_DIM
    return _host_call(
        body, 8, 6, fused, _head_first_last, name="hgrn_bwd", grid=(HEADS,),
        in_specs=[_head_col(0), _head_col(HEADS), _head_col(2 * HEADS), _head_col(3 * HEADS),
                  pl.BlockSpec((2, HEAD_DIM), lambda h: (0, h)), pl.BlockSpec((1, HEAD_DIM), lambda h: (0, 0)),
                  _head_col(0), _head_col(0)],
        out_specs=[_head_col(0)] * 4 + [pl.BlockSpec((2, HEAD_DIM), lambda h: (0, h)),
                                        pl.BlockSpec((1, HEAD_DIM), lambda h: (0, 0))],
        out_shape=[jax.ShapeDtypeStruct((SEQ, wide), BF16)] * 4 + [jax.ShapeDtypeStruct((2, wide), F32),
                                                                    jax.ShapeDtypeStruct((1, HEAD_DIM), F32)],
        scratch_shapes=[pltpu.VMEM((SEQ, HEAD_DIM), F32)] * 7 + [pltpu.VMEM((n_chunks, HEAD_DIM, HEAD_DIM), F32),
                                                                 pltpu.VMEM((n_chunks, HEAD_DIM, HEAD_DIM), BF16),
                                                                 pltpu.VMEM((n_chunks, HEAD_DIM, HEAD_DIM), BF16)],
        sem=("arbitrary",),
        operands=[proj, proj, proj, proj, lb_logits, norm_w, o_pre, d_aout] + list(fused_arrays))


Q_TILE = 256
ATT_SCALE = HEAD_DIM ** -0.5
ATT_OFF = 4 * HEADS


def _qk_prep(proj, q_w, k_w, fused=None, fused_arrays=()):
    def body(aq_ref, ak_ref, av_ref, qw_ref, kw_ref, qn_ref, kn_ref, v_ref):
        aq = aq_ref[...]
        ak = ak_ref[...]
        qn_ref[...] = (aq * lax.rsqrt(jnp.mean(aq * aq, axis=-1, keepdims=True) + EPS) * qw_ref[...]).astype(BF16)
        kn_ref[...] = (ak * lax.rsqrt(jnp.mean(ak * ak, axis=-1, keepdims=True) + EPS) * kw_ref[...]).astype(BF16)
        v_ref[...] = av_ref[...].astype(BF16)

    wide = HEADS * HEAD_DIM
    vec = pl.BlockSpec((1, HEAD_DIM), lambda h: (0, 0))
    return _host_call(
        body, 5, 3, fused, _head_first_last, name="qk_prep", grid=(HEADS,),
        in_specs=[_head_col(ATT_OFF), _head_col(ATT_OFF + HEADS), _head_col(ATT_OFF + 2 * HEADS), vec, vec],
        out_specs=[_head_col(0)] * 3, out_shape=[jax.ShapeDtypeStruct((SEQ, wide), BF16)] * 3,
        scratch_shapes=[], sem=("parallel",), operands=[proj, proj, proj, q_w, k_w] + list(fused_arrays))


def _alibi_slopes():
    slopes = jnp.exp2(-8.0 * jnp.arange(1, HEADS + 1, dtype=F32) / HEADS)
    return jnp.broadcast_to(slopes[:, None, None], (HEADS, 1, HEAD_DIM))


SLOPE_SPEC = pl.BlockSpec((None, 1, HEAD_DIM), lambda h, i: (h, 0, 0))


N_Q_TILES = SEQ // Q_TILE
K_BLOCK = 512
NOT_ATTENDED = 1e35


def _att_tables():
    o = jnp.arange(N_Q_TILES, dtype=jnp.int32)[:, None, None]
    r = jnp.arange(Q_TILE, dtype=jnp.int32)[None, :, None]
    c = jnp.arange(K_BLOCK, dtype=jnp.int32)[None, None, :]
    dist = o * Q_TILE + r - c
    mult = ((dist <= 128).astype(F32) + (((dist % 4) == 0) & (dist <= 512)).astype(F32)
            + ((dist % 16) == 0).astype(F32))
    valid = (dist >= 0) & (mult > 0)
    return (jnp.where(valid, dist.astype(F32), NOT_ATTENDED),
            jnp.where(valid, jnp.log(jnp.maximum(mult, 1.0)), 0.0))


TABLE_SPEC = pl.BlockSpec((N_Q_TILES, Q_TILE, K_BLOCK), lambda h, i: (0, 0, 0))


def _att_block(q, k_ref, j, i, slope, dist_ref, lmul_ref):
    rows = pl.ds(pl.multiple_of(j * K_BLOCK, K_BLOCK), K_BLOCK)
    off = i - j * (K_BLOCK // Q_TILE)
    s = lax.dot_general(q, k_ref[rows, :], NT, preferred_element_type=F32) * ATT_SCALE
    return s - slope * dist_ref[off] + lmul_ref[off], rows


def _n_key_blocks(i):
    return (i + K_BLOCK // Q_TILE) // (K_BLOCK // Q_TILE)


def _att_first_last():
    h, i = pl.program_id(0), pl.program_id(1)
    return (h == 0) & (i == 0), (h == HEADS - 1) & (i == N_Q_TILES - 1)


def _attn_fwd(qn, kn, vb, fused=None, fused_arrays=()):
    def body(q_ref, k_ref, v_ref, sl_ref, dist_ref, lmul_ref, o_ref, lse_ref):
        i = pl.program_id(1)
        q = q_ref[...]
        slope = sl_ref[0:1, 0:1]

        def step(j, carry):
            m, l, acc = carry
            sb, rows = _att_block(q, k_ref, j, i, slope, dist_ref, lmul_ref)
            m_new = jnp.maximum(m, jnp.max(sb, axis=-1, keepdims=True))
            alpha = jnp.exp(m - m_new)
            p = jnp.exp(sb - m_new)
            l = alpha * l + jnp.sum(p, axis=-1, keepdims=True)
            acc = alpha * acc + jnp.dot(p.astype(BF16), v_ref[rows, :], preferred_element_type=F32)
            return m_new, l, acc

        m, l, acc = lax.fori_loop(0, _n_key_blocks(i), step,
                                  (jnp.full((Q_TILE, 1), -1e30, F32), jnp.zeros((Q_TILE, 1), F32),
                                   jnp.zeros((Q_TILE, HEAD_DIM), F32)))
        o_ref[...] = acc / l
        lse_ref[...] = m + jnp.log(l)

    wide = HEADS * HEAD_DIM
    qt = pl.BlockSpec((Q_TILE, HEAD_DIM), lambda h, i: (i, h))
    full = pl.BlockSpec((SEQ, HEAD_DIM), lambda h, i: (0, h))
    return _host_call(
        body, 6, 2, fused, _att_first_last, name="attn_fwd", grid=(HEADS, N_Q_TILES),
        in_specs=[qt, full, full, SLOPE_SPEC, TABLE_SPEC, TABLE_SPEC],
        out_specs=[qt, pl.BlockSpec((None, Q_TILE, 1), lambda h, i: (h, i, 0))],
        out_shape=[jax.ShapeDtypeStruct((SEQ, wide), F32), jax.ShapeDtypeStruct((HEADS, SEQ, 1), F32)],
        scratch_shapes=[], sem=("parallel", "parallel"),
        operands=[qn, kn, vb, _alibi_slopes(), *_att_tables()] + list(fused_arrays))


def _attn_bwd(qn, kn, vb, o, lse, d_mix, fused=None, fused_arrays=()):
    def body(q_ref, k_ref, v_ref, o_ref, lse_ref, do_ref, sl_ref, dist_ref, lmul_ref, dq_ref, dk_ref, dv_ref):
        i = pl.program_id(1)
        q = q_ref[...]
        do = do_ref[...]
        do_b = do.astype(BF16)
        slope = sl_ref[0:1, 0:1]
        lse = lse_ref[...]
        delta = jnp.sum(do * o_ref[...], axis=-1, keepdims=True)

        @pl.when(i == 0)
        def _():
            dk_ref[...] = jnp.zeros_like(dk_ref)
            dv_ref[...] = jnp.zeros_like(dv_ref)

        def step(j, dq):
            sb, rows = _att_block(q, k_ref, j, i, slope, dist_ref, lmul_ref)
            p = jnp.exp(sb - lse)
            dp = lax.dot_general(do_b, v_ref[rows, :], NT, preferred_element_type=F32)
            ds = (p * (dp - delta)).astype(BF16)
            dk_ref[rows, :] += lax.dot_general(ds, q, TN, preferred_element_type=F32) * ATT_SCALE
            dv_ref[rows, :] += lax.dot_general(p.astype(BF16), do_b, TN, preferred_element_type=F32)
            return dq + jnp.dot(ds, k_ref[rows, :], preferred_element_type=F32)

        dq = lax.fori_loop(0, _n_key_blocks(i), step, jnp.zeros((Q_TILE, HEAD_DIM), F32))
        dq_ref[...] = dq * ATT_SCALE

    wide = HEADS * HEAD_DIM
    qt = pl.BlockSpec((Q_TILE, HEAD_DIM), lambda h, i: (i, h))
    full = pl.BlockSpec((SEQ, HEAD_DIM), lambda h, i: (0, h))
    return _host_call(
        body, 9, 3, fused, _att_first_last, name="attn_bwd", grid=(HEADS, N_Q_TILES),
        in_specs=[qt, full, full, qt, pl.BlockSpec((None, Q_TILE, 1), lambda h, i: (h, i, 0)),
                  pl.BlockSpec((Q_TILE, HEAD_DIM), lambda h, i: (i, h + HEADS)), SLOPE_SPEC, TABLE_SPEC, TABLE_SPEC],
        out_specs=[qt, full, full], out_shape=[jax.ShapeDtypeStruct((SEQ, wide), F32)] * 3,
        scratch_shapes=[], sem=("parallel", "arbitrary"),
        operands=[qn, kn, vb, o, lse, d_mix, _alibi_slopes(), *_att_tables()] + list(fused_arrays))


def _qk_bwd(proj, q_w, k_w, dqn, dkn, dv):
    def body(aq_ref, ak_ref, qw_ref, kw_ref, dqn_ref, dkn_ref, dv_ref, daq_ref, dak_ref, dav_ref, gq_ref, gk_ref):
        h = pl.program_id(0)

        @pl.when(h == 0)
        def _():
            gq_ref[...] = jnp.zeros_like(gq_ref)
            gk_ref[...] = jnp.zeros_like(gk_ref)

        def one(a_ref, w_ref, d_ref, da_ref, g_ref):
            a = a_ref[...]
            d = d_ref[...]
            rs = lax.rsqrt(jnp.mean(a * a, axis=-1, keepdims=True) + EPS)
            ah = a * rs
            g_ref[...] += jnp.sum(d * ah, axis=0, keepdims=True)
            dah = d * w_ref[...]
            da_ref[...] = (rs * (dah - ah * jnp.mean(dah * ah, axis=-1, keepdims=True))).astype(BF16)

        one(aq_ref, qw_ref, dqn_ref, daq_ref, gq_ref)
        one(ak_ref, kw_ref, dkn_ref, dak_ref, gk_ref)
        dav_ref[...] = dv_ref[...].astype(BF16)

    wide = HEADS * HEAD_DIM
    vec = pl.BlockSpec((1, HEAD_DIM), lambda h: (0, 0))
    return pl.pallas_call(
        body, name="qk_bwd", grid=(HEADS,),
        in_specs=[_head_col(ATT_OFF), _head_col(ATT_OFF + HEADS), vec, vec, _head_col(0), _head_col(0), _head_col(0)],
        out_specs=[_head_col(0)] * 3 + [vec, vec],
        out_shape=[jax.ShapeDtypeStruct((SEQ, wide), BF16)] * 3 + [jax.ShapeDtypeStruct((1, HEAD_DIM), F32)] * 2,
        compiler_params=_params(("arbitrary",)))(proj, proj, q_w, k_w, dqn, dkn, dv)


def _pair_sum(name, partial, theirs, core):
    _, r, c = theirs.shape
    tr = r // 2 if r % 16 == 0 else r

    def body(core_ref, a_ref, b_ref, o_ref):
        o_ref[...] = (a_ref[...].astype(F32) + b_ref[...].astype(F32)).astype(BF16)

    spec = pl.BlockSpec((None, tr, c), lambda q, i, core_ref: (q, i, 0))
    grid_spec = pltpu.PrefetchScalarGridSpec(
        num_scalar_prefetch=1, grid=(4, r // tr),
        in_specs=[pl.BlockSpec((None, tr, c), lambda q, i, core_ref: (2 * q + core_ref[0], i, 0)), spec],
        out_specs=spec)
    return pl.pallas_call(body, name=name, grid_spec=grid_spec, out_shape=jax.ShapeDtypeStruct(theirs.shape, BF16),
                          compiler_params=_params(("parallel", "parallel")))(core, partial, theirs)


def _adamw_step(w, m, v, g):
    nm = ADAM_B1 * m + (1.0 - ADAM_B1) * g
    nv = ADAM_B2 * v + (1.0 - ADAM_B2) * (g * g)
    m_hat = nm / (1.0 - ADAM_B1 ** ADAM_STEP)
    v_hat = nv / (1.0 - ADAM_B2 ** ADAM_STEP)
    return -ADAM_LR * (m_hat / (jnp.sqrt(v_hat) + ADAM_EPS) + ADAM_WD * w), nm, nv


def _adamw(name, w, m, v, addends, tr=None):
    r, c = w.shape
    tr = r if tr is None else tr
    n_add = len(addends)

    def body(*refs):
        w_ref, m_ref, v_ref = refs[:3]
        add_refs = refs[3:3 + n_add]
        g_ref, d_ref, nm_ref, nv_ref = refs[3 + n_add:]
        g = add_refs[0][...].astype(F32)
        for a_ref in add_refs[1:]:
            g = g + a_ref[...].astype(F32)
        g_ref[...] = g
        d_ref[...], nm_ref[...], nv_ref[...] = _adamw_step(w_ref[...], m_ref[...], v_ref[...], g)

    spec = pl.BlockSpec((tr, c), lambda i: (i, 0))
    out = jax.ShapeDtypeStruct((r, c), F32)
    return pl.pallas_call(body, name=name, grid=(r // tr,), in_specs=[spec] * (3 + n_add), out_specs=[spec] * 4,
                          out_shape=[out] * 4, compiler_params=_params(("parallel",)))(w, m, v, *addends)


def _adamw_reduced(name, w, m, v, chip_sums, received, chip, tr):
    r, c = w.shape

    def body(chip_ref, w_ref, m_ref, v_ref, own_ref, r0_ref, r1_ref, r2_ref, g_ref, d_ref, nm_ref, nv_ref):
        g = ((own_ref[...].astype(F32) + r0_ref[...].astype(F32)) + r1_ref[...].astype(F32)) + r2_ref[...].astype(F32)
        g_ref[...] = g
        d_ref[...], nm_ref[...], nv_ref[...] = _adamw_step(w_ref[...], m_ref[...], v_ref[...], g)

    spec = pl.BlockSpec((tr, c), lambda i, chip_ref: (i, 0))

    def slot(k):
        return pl.BlockSpec((None, tr, c), lambda i, chip_ref: (k, i, 0))

    grid_spec = pltpu.PrefetchScalarGridSpec(
        num_scalar_prefetch=1, grid=(r // tr,),
        in_specs=[spec, spec, spec, pl.BlockSpec((None, tr, c), lambda i, chip_ref: (chip_ref[0], i, 0)),
                  slot(0), slot(1), slot(2)],
        out_specs=[spec] * 4)
    out = jax.ShapeDtypeStruct((r, c), F32)
    return pl.pallas_call(body, name=name, grid_spec=grid_spec, out_shape=[out] * 4,
                          compiler_params=_params(("parallel",)))(chip, w, m, v, chip_sums, received, received, received)


def _sum_devices(gathered):
    _, r, c = gathered.shape

    def body(g_ref, o_ref):
        acc = g_ref[0]
        for d in range(1, N_DEV):
            acc = acc + g_ref[d]
        o_ref[...] = acc

    return pl.pallas_call(body, name="sum_devices", out_shape=jax.ShapeDtypeStruct((r, c), F32))(gathered)


def _pack_rows(vectors, rows):
    flat = jnp.concatenate([v.reshape(-1) for v in vectors])
    return jnp.pad(flat, (0, rows * 128 - flat.shape[0])).reshape(rows, 128)


def _unpack(flat, shapes):
    out, off = [], 0
    for shp in shapes:
        n = 1
        for d in shp:
            n *= d
        out.append(flat[off:off + n].reshape(shp))
        off += n
    return out


def _device_step(xs, tgt, mod, norm1_w, norm2_w, lb_logits, hg_norm_w, q_norm_w, k_norm_w, conv_w_full, conv_b,
                 win_g, w_out_x, w_up_x, w_down_x, core=None):
    fused = core is not None
    shift1, scale1, gate1, shift2, scale2, gate2 = (mod[k] for k in range(6))

    h, rstd1 = _norm_fwd("norm1_fwd", xs, norm1_w, scale1, shift1)
    if fused:
        proj, (wout_g,) = _mm_blocked_rhs("mm_in", h, win_g, fused=_FusedCopies("gather", [w_out_x]),
                                          fused_arrays=[w_out_x])
        (a_out, o_pre), (wup_g,) = _hgrn_fwd(proj, lb_logits, hg_norm_w,
                                             _FusedCopies("gather", [w_up_x], peers=(0, 1, 2)), [w_up_x])
        wout_g, = _forward_to_sibling("allgather_stage2_out", [wout_g])
        wout_full = wout_g.reshape(D_MODEL, D_MODEL)
        (qn, kn, vb), _ = _qk_prep(proj, q_norm_w, k_norm_w)
        (att_o, lse), (wup_g,) = _attn_fwd(qn, kn, vb, _FusedCopies("relay", [wup_g]), [wup_g])
    else:
        proj = _mm_blocked_rhs("mm_in", h, win_g)
        (a_out, o_pre), _ = _hgrn_fwd(proj, lb_logits, hg_norm_w)
        wup_g, wout_full, wdown_full = w_up_x, w_out_x, w_down_x
        (qn, kn, vb), _ = _qk_prep(proj, q_norm_w, k_norm_w)
        (att_o, lse), _ = _attn_fwd(qn, kn, vb)
    mixin = jnp.concatenate([a_out, att_o.astype(BF16)], axis=1)
    if fused:
        mix, (wup_g,) = _mm_plain("mm_out", mixin, wout_full, NN, 512, 1024, F32,
                                  fused=_FusedCopies("forward", [wup_g]), fused_arrays=[wup_g])
    else:
        mix = _mm_plain("mm_out", mixin, wout_full, NN, 512, 1024, F32)
    x1, h2, rstd2 = _norm_fwd("norm2_fwd", xs, norm2_w, scale2, shift2, resid=mix, gate=gate1)
    if fused:
        u, (wdown_g,) = _mm_blocked_rhs("mm_up", h2, wup_g, fused=_FusedCopies("gather", [w_down_x]),
                                        fused_arrays=[w_down_x])
        y, (wdown_g,) = _conv_gate_fwd(u, conv_w_full, conv_b, _FusedCopies("forward", [wdown_g]), [wdown_g])
        wdown_full = wdown_g.reshape(D_FF, D_MODEL)
    else:
        u = _mm_blocked_rhs("mm_up", h2, wup_g)
        y = _conv_gate_fwd(u, conv_w_full, conv_b)
    ffn = _mm_plain("mm_down", y, wdown_full, NN, 512, 512, F32)
    loss_v, dout, dffn, dgate2 = _loss_head(x1, ffn, gate2, tgt)

    dy = _mm_plain("mm_down_dx", dffn, wdown_full, NT, 512, UP_BLK, BF16)
    gw_down = _mm_plain("mm_down_dw", y, dffn, TN, UP_BLK, 1024, BF16)
    da, dg, gconv_w, gconv_b = _conv_gate_bwd(u, dy, conv_w_full, conv_b)
    dh2 = _mm_halves_rhs_t("mm_up_dx", da, dg, wup_g)
    gw_up = _mm_halves_wgrad("mm_up_dw", h2, da, dg)
    if fused:
        part_up, part_down = gw_up, gw_down.reshape(N_DEV, FF_BLK, D_MODEL)
        (dx1, dmix, dshift2, dscale2, gnorm2, dgate1), (sib_up, sib_down) = _norm_bwd(
            "norm2_bwd", dh2, x1, rstd2, norm2_w, scale2, dout, mix=mix, gate=gate1,
            fused=_FusedCopies("sibling", [part_up, part_down]), fused_arrays=[part_up, part_down])
    else:
        dx1, dmix, dshift2, dscale2, gnorm2, dgate1 = _norm_bwd(
            "norm2_bwd", dh2, x1, rstd2, norm2_w, scale2, dout, mix=mix, gate=gate1)
    gw_out = _mm_plain("mm_out_dw", mixin, dmix, TN, 512, 1024, BF16)
    if fused:
        part_out = gw_out.reshape(N_DEV, OUT_BLK, D_MODEL)
        dmixin, (sib_out,) = _mm_plain("mm_out_dx", dmix, wout_full, NT, 512, 1024, F32,
                                       fused=_FusedCopies("sibling", [part_out]), fused_arrays=[part_out])
        cs_up = _pair_sum("grad_pair_sum_up", part_up, sib_up, core)
        cs_out = _pair_sum("grad_pair_sum_out", part_out, sib_out, core)
        cs_down = _pair_sum("grad_pair_sum_down", part_down, sib_down, core)
        (dhq, dhf, dhi, dhg, glog, ghg), (fc_up,) = _hgrn_bwd(
            proj, lb_logits, hg_norm_w, o_pre, dmixin, _FusedCopies("chips", [cs_up]), [cs_up])
        (dqn, dkn, dvv), (fc_down, fc_out) = _attn_bwd(qn, kn, vb, att_o, lse, dmixin,
                                                       _FusedCopies("chips", [cs_down, cs_out]), [cs_down, cs_out])
    else:
        dmixin = _mm_plain("mm_out_dx", dmix, wout_full, NT, 512, 1024, F32)
        (dhq, dhf, dhi, dhg, glog, ghg), _ = _hgrn_bwd(proj, lb_logits, hg_norm_w, o_pre, dmixin)
        (dqn, dkn, dvv), _ = _attn_bwd(qn, kn, vb, att_o, lse, dmixin)
    daq, dak, dav, gqw, gkw = _qk_bwd(proj, q_norm_w, k_norm_w, dqn, dkn, dvv)
    dproj = jnp.concatenate([dhq, dhf, dhi, dhg, daq, dak, dav], axis=1)
    gw_in = _mm_wgrad_blocked("mm_in_dw", h, dproj)
    if fused:
        from_sibling, = _exchange_sibling("grad_exchange_sibling_b", [gw_in])
        cs_in = _pair_sum("grad_pair_sum_in", gw_in, from_sibling, core)
        dh, (fc_in,) = _mm_blocked_rhs_t("mm_in_dx", dproj, win_g, fused=_FusedCopies("chips", [cs_in]),
                                         fused_arrays=[cs_in])
        large = [(cs_in, fc_in), (cs_out, fc_out), (cs_up, fc_up), (cs_down, fc_down)]
    else:
        dh = _mm_blocked_rhs_t("mm_in_dx", dproj, win_g)
        large = [gw_in, gw_out, gw_up, gw_down]
    grad_x, dshift1, dscale1, gnorm1 = _norm_bwd("norm1_bwd", dh, xs, rstd1, norm1_w, scale1, dx1)
    gmod = jnp.concatenate([dshift1, dscale1, dgate1, dshift2, dscale2, dgate2], axis=1)
    return (loss_v, grad_x, gmod, gnorm1, gnorm2, glog, ghg, gqw, gkw, gconv_b, gconv_w, *large)


def kernel(x, c, w_ada, b_ada, norm1_w, w_in, lb_logits, hg_norm_w, q_norm_w, k_norm_w, w_out, norm2_w, w_up, conv_w, conv_b, w_down, loss_target, m_w_ada, m_b_ada, m_norm1_w, m_w_in, m_lb_logits, m_hg_norm_w, m_q_norm_w, m_k_norm_w, m_w_out, m_norm2_w, m_w_up, m_conv_w, m_conv_b, m_w_down, v_w_ada, v_b_ada, v_norm1_w, v_w_in, v_lb_logits, v_hg_norm_w, v_q_norm_w, v_k_norm_w, v_w_out, v_norm2_w, v_w_up, v_conv_w, v_conv_b, v_w_down):
    ix, iy, ic = lax.axis_index("x"), lax.axis_index("y"), lax.axis_index("c")
    me = 4 * ix + 2 * iy + ic
    my_chip = 2 * ix + iy

    xs = x[0]
    tgt = loss_target[0]

    win_g, = _allgather_weights([w_in[0].astype(BF16)])

    c_all = _allgather_vmem(c.reshape(8, D_MODEL // 8), "allgather_c").reshape(N_DEV, D_MODEL)
    b_blk = lax.dynamic_slice_in_dim(b_ada, me * ADA_BLK, ADA_BLK, axis=1)
    mod_cols = _ada_fwd(c_all, w_ada[0], b_blk)
    mod_all = _allgather_vmem(mod_cols, "allgather_mod").reshape(N_DEV, N_DEV, ADA_BLK)
    mod = lax.dynamic_index_in_dim(mod_all, me, axis=1, keepdims=False).reshape(6, 1, D_MODEL)

    conv_w_all = _allgather_vmem(_pack_rows([conv_w[0]], 24), "allgather_conv_w").reshape(N_DEV, 24 * 128)
    conv_w_full = conv_w_all[:, :3 * FF_BLK].reshape(N_DEV, 3, FF_BLK).transpose(1, 0, 2).reshape(3, D_FF)

    (loss_v, grad_x, gmod, gnorm1, gnorm2, glog, ghg, gqw, gkw, gconv_b, gconv_w,
     rs_in, rs_out, rs_up, rs_down) = _device_step(
        xs, tgt, mod, norm1_w, norm2_w, lb_logits, hg_norm_w, q_norm_w, k_norm_w, conv_w_full, conv_b,
        win_g, w_out[0].astype(BF16), w_up[0].astype(BF16), w_down[0].astype(BF16),
        core=jnp.reshape(ic, (1,)).astype(jnp.int32))
    loss = lax.psum(loss_v[0, 0], AXES)

    small_shapes = [(1, 6 * D_MODEL), (1, D_MODEL), (1, D_MODEL), (2, HEADS * HEAD_DIM), (1, HEAD_DIM),
                    (1, HEAD_DIM), (1, HEAD_DIM), (1, D_FF), (3, D_FF)]
    small = [gmod, gnorm1, gnorm2, glog, ghg, gqw, gkw, gconv_b, gconv_w]
    n_small = sum(a.size for a in small)
    rows = -(-n_small // 1024) * 8
    gathered = _allgather_vmem(_pack_rows(small, rows), "allgather_small").reshape(N_DEV, rows, 128)
    summed = _sum_devices(gathered).reshape(-1)
    (g_b_ada, g_norm1, g_norm2, g_lb, g_hg, g_q, g_k, g_conv_b, g_conv_w_full) = _unpack(summed, small_shapes)
    g_conv_w = lax.dynamic_slice_in_dim(g_conv_w_full, me * FF_BLK, FF_BLK, axis=1)

    gmod_all = gathered[:, :6 * D_MODEL // 128, :].reshape(N_DEV, 6 * D_MODEL)
    gmod_cols = lax.dynamic_slice_in_dim(gmod_all, me * ADA_BLK, ADA_BLK, axis=1)
    g_w_ada_raw = _ada_wgrad(c_all, gmod_cols)

    chip = jnp.reshape(my_chip, (1,)).astype(jnp.int32)

    def big_update(name, w, m, v, rs, tr):
        chip_sums, received = rs
        return _adamw_reduced(name, w[0], m[0], v[0], chip_sums, received, chip, tr)

    r_in = big_update("adamw_w_in", w_in, m_w_in, v_w_in, rs_in, 256)
    r_out = big_update("adamw_w_out", w_out, m_w_out, v_w_out, rs_out, 128)
    r_up = big_update("adamw_w_up", w_up, m_w_up, v_w_up, rs_up, 256)
    r_down = big_update("adamw_w_down", w_down, m_w_down, v_w_down, rs_down, 176)
    r_ada = _adamw("adamw_w_ada", w_ada[0], m_w_ada[0], v_w_ada[0], [g_w_ada_raw], tr=256)
    r_convw = _adamw("adamw_conv_w", conv_w[0], m_conv_w[0], v_conv_w[0], [g_conv_w])

    rep_shapes = [(1, 6 * D_MODEL), (1, D_MODEL), (1, D_MODEL), (2, HEADS * HEAD_DIM), (1, HEAD_DIM),
                  (1, HEAD_DIM), (1, HEAD_DIM), (1, D_FF)]
    rep_rows = -(-sum(a * b for a, b in rep_shapes) // 1024) * 8
    pack = lambda arrs: _pack_rows(arrs, rep_rows)
    rep = _adamw("adamw_small",
                 pack([b_ada, norm1_w, norm2_w, lb_logits, hg_norm_w, q_norm_w, k_norm_w, conv_b]),
                 pack([m_b_ada, m_norm1_w, m_norm2_w, m_lb_logits, m_hg_norm_w, m_q_norm_w, m_k_norm_w, m_conv_b]),
                 pack([v_b_ada, v_norm1_w, v_norm2_w, v_lb_logits, v_hg_norm_w, v_q_norm_w, v_k_norm_w, v_conv_b]),
                 [pack([g_b_ada, g_norm1, g_norm2, g_lb, g_hg, g_q, g_k, g_conv_b])])
    rep = [_unpack(r.reshape(-1), rep_shapes) for r in rep]

    def big(r):
        return [a[None] for a in r]

    order = {"w_ada": big(r_ada), "b_ada": [r[0] for r in rep], "norm1_w": [r[1] for r in rep],
             "w_in": big(r_in), "lb_logits": [r[3] for r in rep], "hg_norm_w": [r[4] for r in rep],
             "q_norm_w": [r[5] for r in rep], "k_norm_w": [r[6] for r in rep], "w_out": big(r_out),
             "norm2_w": [r[2] for r in rep], "w_up": big(r_up), "conv_w": big(r_convw),
             "conv_b": [r[7] for r in rep], "w_down": big(r_down)}
    names = ["w_ada", "b_ada", "norm1_w", "w_in", "lb_logits", "hg_norm_w", "q_norm_w", "k_norm_w", "w_out",
             "norm2_w", "w_up", "conv_w", "conv_b", "w_down"]
    outs = [loss, grad_x[None]]
    for kind in range(4):
        outs += [order[n][kind] for n in names]
    return tuple(outs)
```

```python
import functools

import jax
import jax.numpy as jnp
from jax import lax
from jax.experimental import pallas as pl
from jax.experimental.pallas import tpu as pltpu

F32 = jnp.float32
BF16 = jnp.bfloat16

N_DEV = 8
SEQ = 2048
D_MODEL = 2048
HEADS = 8
HEAD_DIM = 128
IN_COLS = 7168
IN_BLK = IN_COLS // N_DEV
D_FF = 5632
UP_BLK = 2 * D_FF // N_DEV
FF_BLK = D_FF // N_DEV
ADA_BLK = 6 * D_MODEL // N_DEV
OUT_BLK = D_MODEL // N_DEV
EPS = 1e-6
CHUNK = 16
ROW_TILE = 256
V7X_VMEM_LIMIT = 56 * 1024 * 1024

ADAM_LR = 0.001
ADAM_B1 = 0.9
ADAM_B2 = 0.999
ADAM_EPS = 1e-08
ADAM_WD = 0.01
ADAM_STEP = 10

NN = (((1,), (0,)), ((), ()))
NT = (((1,), (1,)), ((), ()))
TN = (((0,), (0,)), ((), ()))
MESH = pl.DeviceIdType.MESH
AXES = ("x", "y", "c")


def _params(sem=None, vmem=V7X_VMEM_LIMIT):
    return pltpu.CompilerParams(dimension_semantics=sem, vmem_limit_bytes=vmem)


def _sigmoid(x):
    return 1.0 / (1.0 + jnp.exp(-x))


def _dsilu(x, s):
    return s * (1.0 + x * (1.0 - s))


def _lane_sum(x, ones_bf16):
    return jnp.dot(x.astype(BF16), ones_bf16, preferred_element_type=F32)


def _mesh_pos():
    return lax.axis_index("x"), lax.axis_index("y"), lax.axis_index("c")


def _allgather_vmem(x_blk, name):
    m_per, n = x_blk.shape

    def body(x_ref, out_ref, send_sems, recv_sems, local_sem):
        x, y, c = _mesh_pos()
        me, sibling = (x, y, c), (x, y, 1 - c)
        chips = [(1 - x, y), (x, 1 - y), (1 - x, 1 - y)]

        def rows(px, py, pc):
            return out_ref.at[pl.ds((4 * px + 2 * py + pc) * m_per, m_per), :]

        def copy(k, block, to, src=None):
            return pltpu.make_async_remote_copy(
                src_ref=rows(*block) if src is None else src, dst_ref=rows(*block),
                send_sem=send_sems.at[k], recv_sem=recv_sems.at[k], device_id=to, device_id_type=MESH)

        mine = pltpu.make_async_copy(x_ref, rows(*me), local_sem)
        mine.start()
        first = [copy(0, me, sibling, src=x_ref)]
        first += [copy(1 + j, me, (*chip, c), src=x_ref) for j, chip in enumerate(chips)]
        for cp in first:
            cp.start()
        passed = [copy(4 + j, (*chip, c), sibling) for j, chip in enumerate(chips)]
        for j, chip in enumerate(chips):
            copy(1 + j, (*chip, c), me).wait_recv()
            passed[j].start()
        copy(0, sibling, me).wait_recv()
        for j, chip in enumerate(chips):
            copy(4 + j, (*chip, 1 - c), me).wait_recv()
        for cp in first + passed:
            cp.wait_send()
        mine.wait()

    return pl.pallas_call(
        body, name=name,
        out_shape=jax.ShapeDtypeStruct((N_DEV * m_per, n), x_blk.dtype),
        in_specs=[pl.BlockSpec(memory_space=pltpu.VMEM)],
        out_specs=pl.BlockSpec(memory_space=pltpu.VMEM),
        scratch_shapes=[pltpu.SemaphoreType.DMA((7,)), pltpu.SemaphoreType.DMA((7,)), pltpu.SemaphoreType.DMA],
    )(x_blk)


def _flip(v, bit):
    return v + bit - 2 * v * bit


def _relay_chips(x, y, c):
    return (_flip(x, 1 - c), _flip(y, c)), (_flip(x, c), _flip(y, 1 - c))


UP_HEAD_ROWS = 768
GATHER_PARTS = 4


def _allgather_weights(blocks):
    n_arr = len(blocks)
    parts = GATHER_PARTS

    def body(*refs):
        ins, outs = refs[:n_arr], refs[n_arr:2 * n_arr]
        send_sems, recv_sems, local_sems = refs[2 * n_arr:]
        x, y, c = _mesh_pos()
        me, sibling = (x, y, c), (x, y, 1 - c)
        near = [(1 - x, y), (x, 1 - y)]
        chips = near + [(1 - x, 1 - y)]
        relay_from, relay_to = _relay_chips(x, y, c)

        def rows(a, p):
            hr = ins[a].shape[0] // parts
            return pl.ds(p * hr, hr)

        def slot(a, pos, p):
            return outs[a].at[4 * pos[0] + 2 * pos[1] + pos[2], rows(a, p)]

        def copy(a, k, p, src, lands, to):
            return pltpu.make_async_remote_copy(
                src_ref=src, dst_ref=slot(a, lands, p), send_sem=send_sems.at[a, k, p], recv_sem=recv_sems.at[a, k, p],
                device_id=to, device_id_type=MESH)

        sent = []
        local = [pltpu.make_async_copy(ins[a], outs[a].at[4 * x + 2 * y + c], local_sems.at[a]) for a in range(n_arr)]
        for cp in local:
            cp.start()
        for p in range(parts):
            for a in range(n_arr):
                own = ins[a].at[rows(a, p)]
                sent.append(copy(a, 0, p, own, me, sibling))
                sent += [copy(a, 1 + j, p, own, me, (*chip, c)) for j, chip in enumerate(near)]
        for cp in sent:
            cp.start()

        def start(cp):
            cp.start()
            sent.append(cp)

        for p in range(parts):
            for a in range(n_arr):
                for j, chip in enumerate(near):
                    copy(a, 1 + j, p, ins[a].at[rows(a, p)], (*chip, c), me).wait_recv()
                    start(copy(a, 4 + j, p, slot(a, (*chip, c), p), (*chip, c), sibling))
                start(copy(a, 3, p, slot(a, (*relay_from, c), p), (*relay_from, c), (*relay_to, c)))
        for p in range(parts):
            for a in range(n_arr):
                copy(a, 3, p, ins[a].at[rows(a, p)], (*chips[2], c), me).wait_recv()
                start(copy(a, 6, p, slot(a, (*chips[2], c), p), (*chips[2], c), sibling))
        for p in range(parts):
            for a in range(n_arr):
                copy(a, 0, p, ins[a].at[rows(a, p)], sibling, me).wait_recv()
                for j, chip in enumerate(chips):
                    copy(a, 4 + j, p, ins[a].at[rows(a, p)], (*chip, 1 - c), me).wait_recv()
        for cp in sent:
            cp.wait_send()
        for cp in local:
            cp.wait()

    return pl.pallas_call(
        body, name="allgather_weights",
        out_shape=[jax.ShapeDtypeStruct((N_DEV,) + b.shape, b.dtype) for b in blocks],
        in_specs=[pl.BlockSpec(memory_space=pltpu.HBM)] * n_arr, out_specs=[pl.BlockSpec(memory_space=pltpu.HBM)] * n_arr,
        scratch_shapes=[pltpu.SemaphoreType.DMA((n_arr, 7, parts)), pltpu.SemaphoreType.DMA((n_arr, 7, parts)),
                        pltpu.SemaphoreType.DMA((n_arr,))],
    )(*blocks)


HBM_SPEC = pl.BlockSpec(memory_space=pltpu.HBM)


class _FusedCopies:
    def __init__(self, kind, arrays, peers=(0, 1, 2, 3), rows=None, relay_rows=None):
        self.kind = kind
        self.peers = peers
        self.rows = rows
        self.relay_rows = relay_rows
        n = len(arrays) // 2 if kind == "gather_more" else len(arrays)
        self.n = n
        self.n_in = len(arrays)
        self.aliases = {}
        if kind == "gather":
            self.out_shape = [jax.ShapeDtypeStruct((N_DEV,) + a.shape, a.dtype) for a in arrays]
            self.scratch_shapes = [pltpu.SemaphoreType.DMA((n, 4, GATHER_PARTS)),
                                   pltpu.SemaphoreType.DMA((n, 4, GATHER_PARTS)), pltpu.SemaphoreType.DMA((n,))]
        elif kind == "gather_more":
            self.out_shape = [jax.ShapeDtypeStruct(a.shape, a.dtype) for a in arrays[n:]]
            self.scratch_shapes = [pltpu.SemaphoreType.DMA((n, 5, GATHER_PARTS)),
                                   pltpu.SemaphoreType.DMA((n, 5, GATHER_PARTS)), pltpu.SemaphoreType.DMA((n,))]
            self.aliases = {n + a: a for a in range(n)}
        elif kind == "relay":
            self.out_shape = [jax.ShapeDtypeStruct(a.shape, a.dtype) for a in arrays]
            self.scratch_shapes = [pltpu.SemaphoreType.DMA((n,)), pltpu.SemaphoreType.DMA((n,))]
            self.aliases = {a: a for a in range(n)}
        elif kind == "forward":
            self.out_shape = [jax.ShapeDtypeStruct(a.shape, a.dtype) for a in arrays]
            self.scratch_shapes = [pltpu.SemaphoreType.DMA((n, 3)), pltpu.SemaphoreType.DMA((n, 3))]
            self.aliases = {a: a for a in range(n)}
        elif kind == "sibling":
            self.out_shape = [jax.ShapeDtypeStruct((4,) + a.shape[1:], a.dtype) for a in arrays]
            self.scratch_shapes = [pltpu.SemaphoreType.DMA((n, 4)), pltpu.SemaphoreType.DMA((n, 4))]
        else:
            self.out_shape = [jax.ShapeDtypeStruct((3,) + a.shape[1:], a.dtype) for a in arrays]
            self.scratch_shapes = [pltpu.SemaphoreType.DMA((n, 3)), pltpu.SemaphoreType.DMA((n, 3))]
        self.in_specs = [HBM_SPEC] * self.n_in
        self.out_specs = [HBM_SPEC] * n
        self.n_scratch = len(self.scratch_shapes)

    def copies(self, ins, outs, sems):
        x, y, c = _mesh_pos()
        chips = [(1 - x, y), (x, 1 - y), (1 - x, 1 - y)]
        sibling = (x, y, 1 - c)
        starts, waits = [], []
        relay_from, relay_to = _relay_chips(x, y, c)

        def relayed(a, buf, lands, send_sem, recv_sem, rows):
            first, count = rows or (0, buf.shape[1])
            span = pl.ds(first, count)
            return pltpu.make_async_remote_copy(
                src_ref=buf.at[4 * relay_from[0] + 2 * relay_from[1] + c, span],
                dst_ref=outs[a].at[4 * lands[0] + 2 * lands[1] + c, span], send_sem=send_sem, recv_sem=recv_sem,
                device_id=(*relay_to, c), device_id_type=MESH)

        if self.kind in ("gather", "gather_more"):
            send_sems, recv_sems, local_sems = sems
            me = (x, y, c)
            peers = [sibling] + [(px, py, c) for px, py in chips]

            def slot(a, pos):
                return outs[a].at[4 * pos[0] + 2 * pos[1] + pos[2]]

            def span(a, p=None):
                first, count = self.rows or (0, ins[a].shape[0])
                if p is None:
                    return pl.ds(first, count)
                return pl.ds(first + p * (count // GATHER_PARTS), count // GATHER_PARTS)

            def remote(a, k, p, lands_from):
                return pltpu.make_async_remote_copy(
                    src_ref=ins[a].at[span(a, p)], dst_ref=slot(a, lands_from).at[span(a, p)],
                    send_sem=send_sems.at[a, k, p], recv_sem=recv_sems.at[a, k, p], device_id=peers[k],
                    device_id_type=MESH)

            for a in range(self.n):
                local = pltpu.make_async_copy(ins[a].at[span(a)], slot(a, me).at[span(a)], local_sems.at[a])
                starts.append(local)
                waits.append(local)
            for p in range(GATHER_PARTS):
                for a in range(self.n):
                    for k in self.peers:
                        starts.append(remote(a, k, p, me))
                        waits.append(remote(a, k, p, peers[k]))
            if self.kind == "gather_more" and self.relay_rows is not None:
                for a in range(self.n):
                    buf = ins[self.n + a]
                    starts.append(relayed(a, buf, relay_from, send_sems.at[a, 4, 0], recv_sems.at[a, 4, 0],
                                          self.relay_rows))
                    waits.append(relayed(a, buf, chips[2], send_sems.at[a, 4, 0], recv_sems.at[a, 4, 0],
                                         self.relay_rows))
        elif self.kind == "relay":
            send_sems, recv_sems = sems
            for a in range(self.n):
                starts.append(relayed(a, ins[a], relay_from, send_sems.at[a], recv_sems.at[a], self.rows))
                waits.append(relayed(a, ins[a], chips[2], send_sems.at[a], recv_sems.at[a], self.rows))
        elif self.kind == "forward":
            send_sems, recv_sems = sems

            def passed_on(a, j, pc_src, pc_dst):
                px, py = chips[j]
                return pltpu.make_async_remote_copy(
                    src_ref=ins[a].at[4 * px + 2 * py + pc_src], dst_ref=outs[a].at[4 * px + 2 * py + pc_dst],
                    send_sem=send_sems.at[a, j], recv_sem=recv_sems.at[a, j], device_id=sibling, device_id_type=MESH)

            for a in range(self.n):
                for j in range(3):
                    starts.append(passed_on(a, j, c, c))
                    waits.append(passed_on(a, j, c, 1 - c))
        elif self.kind == "sibling":
            send_sems, recv_sems = sems
            for a in range(self.n):
                for q in range(4):
                    cp = pltpu.make_async_remote_copy(
                        src_ref=ins[a].at[2 * q + 1 - c], dst_ref=outs[a].at[q], send_sem=send_sems.at[a, q],
                        recv_sem=recv_sems.at[a, q], device_id=sibling, device_id_type=MESH)
                    starts.append(cp)
                    waits.append(cp)
        else:
            send_sems, recv_sems = sems
            for a in range(self.n):
                for j, (px, py) in enumerate(chips):
                    cp = pltpu.make_async_remote_copy(
                        src_ref=ins[a].at[2 * px + py], dst_ref=outs[a].at[j], send_sem=send_sems.at[a, j],
                        recv_sem=recv_sems.at[a, j], device_id=(px, py, c), device_id_type=MESH)
                    starts.append(cp)
                    waits.append(cp)
        return starts, waits


def _fused_groups(fused):
    if fused is None:
        return []
    return list(fused) if isinstance(fused, (list, tuple)) else [fused]


def _host_body(body, n_in, n_out, fused, first_last):
    groups = _fused_groups(fused)
    if not groups:
        return body
    n_fin, n_fout = sum(g.n_in for g in groups), sum(g.n for g in groups)
    n_fsem = sum(g.n_scratch for g in groups)

    def wrapped(*refs):
        core_in, f_in = refs[:n_in], refs[n_in:n_in + n_fin]
        core_out = refs[n_in + n_fin:n_in + n_fin + n_out]
        f_out = refs[n_in + n_fin + n_out:n_in + n_fin + n_out + n_fout]
        rest = refs[n_in + n_fin + n_out + n_fout:]
        core_scratch, f_sems = rest[:len(rest) - n_fsem], rest[len(rest) - n_fsem:]
        starts, waits = [], []
        for g in groups:
            s, w = g.copies(f_in[:g.n_in], f_out[:g.n], f_sems[:g.n_scratch])
            f_in, f_out, f_sems = f_in[g.n_in:], f_out[g.n:], f_sems[g.n_scratch:]
            starts += s
            waits += w
        first, last = first_last()

        @pl.when(first)
        def _():
            for cp in starts:
                cp.start()

        body(*core_in, *core_out, *core_scratch)

        @pl.when(last)
        def _():
            for cp in waits:
                cp.wait()

    return wrapped


def _host_call(body, n_in, n_out, fused, first_last, *, name, grid, in_specs, out_specs, out_shape, scratch_shapes,
               sem, operands):
    aliases = {}
    in_specs, out_specs, out_shape, scratch_shapes = list(in_specs), list(out_specs), list(out_shape), list(scratch_shapes)
    fin, fout = n_in, n_out
    for g in _fused_groups(fused):
        aliases.update({fin + fi: fout + fo for fi, fo in g.aliases.items()})
        fin, fout = fin + g.n_in, fout + g.n
        in_specs += g.in_specs
        out_specs += g.out_specs
        out_shape += g.out_shape
        scratch_shapes += g.scratch_shapes
        sem = tuple("arbitrary" for _ in sem)
    res = pl.pallas_call(_host_body(body, n_in, n_out, fused, first_last), name=name, grid=grid, in_specs=in_specs,
                         out_specs=out_specs, out_shape=out_shape, scratch_shapes=scratch_shapes,
                         input_output_aliases=aliases, compiler_params=_params(sem))(*operands)
    return list(res[:n_out]), list(res[n_out:])


def _forward_to_sibling(name, gathered):
    n_arr = len(gathered)

    def body(*refs):
        ins, outs = refs[:n_arr], refs[n_arr:2 * n_arr]
        send_sems, recv_sems = refs[2 * n_arr:]
        x, y, c = _mesh_pos()
        chips = [(1 - x, y), (x, 1 - y), (1 - x, 1 - y)]

        def copy(a, j, pc):
            px, py = chips[j]
            s = 4 * px + 2 * py + pc
            return pltpu.make_async_remote_copy(
                src_ref=ins[a].at[s], dst_ref=outs[a].at[s], send_sem=send_sems.at[a, j], recv_sem=recv_sems.at[a, j],
                device_id=(x, y, 1 - c), device_id_type=MESH)

        for a in range(n_arr):
            for j in range(3):
                copy(a, j, c).start()
        for a in range(n_arr):
            for j in range(3):
                copy(a, j, 1 - c).wait_recv()
                copy(a, j, c).wait_send()

    return pl.pallas_call(
        body, name=name,
        out_shape=[jax.ShapeDtypeStruct(g.shape, g.dtype) for g in gathered],
        in_specs=[HBM_SPEC] * n_arr, out_specs=[HBM_SPEC] * n_arr,
        input_output_aliases={a: a for a in range(n_arr)},
        scratch_shapes=[pltpu.SemaphoreType.DMA((n_arr, 3)), pltpu.SemaphoreType.DMA((n_arr, 3))],
    )(*gathered)


def _exchange_sibling(name, partials):
    n_arr = len(partials)

    def body(*refs):
        ins, outs = refs[:n_arr], refs[n_arr:2 * n_arr]
        send_sems, recv_sems = refs[2 * n_arr:]
        x, y, c = _mesh_pos()
        copies = [pltpu.make_async_remote_copy(
            src_ref=ins[a].at[2 * q + 1 - c], dst_ref=outs[a].at[q], send_sem=send_sems.at[a, q],
            recv_sem=recv_sems.at[a, q], device_id=(x, y, 1 - c), device_id_type=MESH)
            for a in range(n_arr) for q in range(4)]
        for cp in copies:
            cp.start()
        for cp in copies:
            cp.wait_recv()
        for cp in copies:
            cp.wait_send()

    return pl.pallas_call(
        body, name=name,
        out_shape=[jax.ShapeDtypeStruct((4,) + p.shape[1:], p.dtype) for p in partials],
        in_specs=[HBM_SPEC] * n_arr, out_specs=[HBM_SPEC] * n_arr,
        scratch_shapes=[pltpu.SemaphoreType.DMA((n_arr, 4)), pltpu.SemaphoreType.DMA((n_arr, 4))],
    )(*partials)


def _matmul(name, a, b, dims, grid, a_spec, b_spec, o_spec, out_shape, acc_axis=None, fused=None, fused_arrays=()):
    def body(a_ref, b_ref, o_ref):
        r = lax.dot_general(a_ref[...], b_ref[...], dims, preferred_element_type=F32)
        if acc_axis is None:
            o_ref[...] = r.astype(o_ref.dtype)
        else:
            k = pl.program_id(acc_axis)

            @pl.when(k == 0)
            def _():
                o_ref[...] = r

            @pl.when(k > 0)
            def _():
                o_ref[...] += r

    sem = tuple("arbitrary" if i == acc_axis else "parallel" for i in range(len(grid)))
    if fused is None:
        return pl.pallas_call(body, name=name, grid=grid, in_specs=[a_spec, b_spec], out_specs=o_spec,
                              out_shape=out_shape, compiler_params=_params(sem))(a, b)

    def first_last():
        first = last = None
        for ax, n in enumerate(grid):
            f, l = pl.program_id(ax) == 0, pl.program_id(ax) == n - 1
            first, last = (f, l) if first is None else (first & f, last & l)
        return first, last

    (out,), extra = _host_call(body, 2, 1, fused, first_last, name=name, grid=grid, in_specs=[a_spec, b_spec],
                               out_specs=[o_spec], out_shape=[out_shape], scratch_shapes=[], sem=sem,
                               operands=[a, b] + list(fused_arrays))
    return out, extra


def _mm_blocked_rhs(name, a, w_g, tm=512, fused=None, fused_arrays=()):
    m, k = a.shape
    nb = w_g.shape[2]
    return _matmul(name, a, w_g, NN, (N_DEV, m // tm),
                   pl.BlockSpec((tm, k), lambda j, i: (i, 0)),
                   pl.BlockSpec((None, k, nb), lambda j, i: (j, 0, 0)),
                   pl.BlockSpec((tm, nb), lambda j, i: (i, j)),
                   jax.ShapeDtypeStruct((m, N_DEV * nb), F32), fused=fused, fused_arrays=fused_arrays)


def _mm_blocked_rhs_t(name, a, w_g, tm=512, fused=None, fused_arrays=()):
    m = a.shape[0]
    n, nb = w_g.shape[1], w_g.shape[2]
    return _matmul(name, a, w_g, NT, (m // tm, N_DEV),
                   pl.BlockSpec((tm, nb), lambda i, j: (i, j)),
                   pl.BlockSpec((None, n, nb), lambda i, j: (j, 0, 0)),
                   pl.BlockSpec((tm, n), lambda i, j: (i, 0)),
                   jax.ShapeDtypeStruct((m, n), F32), acc_axis=1, fused=fused, fused_arrays=fused_arrays)


def _mm_wgrad_blocked(name, act, dcols, tk=512, fused=None, fused_arrays=()):
    t, k = act.shape
    nb = dcols.shape[1] // N_DEV
    return _matmul(name, act, dcols, TN, (N_DEV, k // tk),
                   pl.BlockSpec((t, tk), lambda j, i: (0, i)),
                   pl.BlockSpec((t, nb), lambda j, i: (0, j)),
                   pl.BlockSpec((None, tk, nb), lambda j, i: (j, i, 0)),
                   jax.ShapeDtypeStruct((N_DEV, k, nb), BF16), fused=fused, fused_arrays=fused_arrays)


def _halves_specs(block, index):
    half = N_DEV // 2
    return (pl.BlockSpec(block, lambda i, j: index(i, jnp.minimum(j, half - 1))),
            pl.BlockSpec(block, lambda i, j: index(i, jnp.maximum(j - half, 0))))


def _mm_halves_rhs_t(name, a_lo, a_hi, w_g, tm=512):
    m = a_lo.shape[0]
    n, nb = w_g.shape[1], w_g.shape[2]

    def body(lo_ref, hi_ref, b_ref, o_ref):
        j = pl.program_id(1)

        def accumulate(a_ref):
            r = lax.dot_general(a_ref[...], b_ref[...], NT, preferred_element_type=F32)

            @pl.when(j == 0)
            def _():
                o_ref[...] = r

            @pl.when(j > 0)
            def _():
                o_ref[...] += r

        pl.when(j < N_DEV // 2)(lambda: accumulate(lo_ref))
        pl.when(j >= N_DEV // 2)(lambda: accumulate(hi_ref))

    lo_spec, hi_spec = _halves_specs((tm, nb), lambda i, j: (i, j))
    return pl.pallas_call(
        body, name=name, grid=(m // tm, N_DEV),
        in_specs=[lo_spec, hi_spec, pl.BlockSpec((None, n, nb), lambda i, j: (j, 0, 0))],
        out_specs=pl.BlockSpec((tm, n), lambda i, j: (i, 0)), out_shape=jax.ShapeDtypeStruct((m, n), F32),
        compiler_params=_params(("parallel", "arbitrary")))(a_lo, a_hi, w_g)


def _mm_halves_wgrad(name, act, d_lo, d_hi, tk=512):
    t, k = act.shape
    nb = d_lo.shape[1] // (N_DEV // 2)

    def body(a_ref, lo_ref, hi_ref, o_ref):
        j = pl.program_id(0)

        def product(d_ref):
            o_ref[...] = lax.dot_general(a_ref[...], d_ref[...], TN, preferred_element_type=F32).astype(o_ref.dtype)

        pl.when(j < N_DEV // 2)(lambda: product(lo_ref))
        pl.when(j >= N_DEV // 2)(lambda: product(hi_ref))

    half = N_DEV // 2
    return pl.pallas_call(
        body, name=name, grid=(N_DEV, k // tk),
        in_specs=[pl.BlockSpec((t, tk), lambda j, i: (0, i)),
                  pl.BlockSpec((t, nb), lambda j, i: (0, jnp.minimum(j, half - 1))),
                  pl.BlockSpec((t, nb), lambda j, i: (0, jnp.maximum(j - half, 0)))],
        out_specs=pl.BlockSpec((None, tk, nb), lambda j, i: (j, i, 0)),
        out_shape=jax.ShapeDtypeStruct((N_DEV, k, nb), BF16),
        compiler_params=_params(("parallel", "parallel")))(act, d_lo, d_hi)


def _mm_plain(name, a, b, dims, tm, tn, out_dtype, fused=None, fused_arrays=()):
    if dims == NN:
        (m, k), n = a.shape, b.shape[1]
        a_spec = pl.BlockSpec((tm, k), lambda i, j: (i, 0))
        b_spec = pl.BlockSpec((k, tn), lambda i, j: (0, j))
    elif dims == NT:
        (m, k), n = a.shape, b.shape[0]
        a_spec = pl.BlockSpec((tm, k), lambda i, j: (i, 0))
        b_spec = pl.BlockSpec((tn, k), lambda i, j: (j, 0))
    else:
        (k, m), n = a.shape, b.shape[1]
        a_spec = pl.BlockSpec((k, tm), lambda i, j: (0, i))
        b_spec = pl.BlockSpec((k, tn), lambda i, j: (0, j))
    return _matmul(name, a, b, dims, (m // tm, n // tn), a_spec, b_spec,
                   pl.BlockSpec((tm, tn), lambda i, j: (i, j)), jax.ShapeDtypeStruct((m, n), out_dtype),
                   fused=fused, fused_arrays=fused_arrays)


def _ada_fwd(c_all, w_ada_blk, b_blk):
    def body(c_ref, w_ref, b_ref, o_ref):
        cv = c_ref[...]
        o_ref[...] = jnp.dot(cv * _sigmoid(cv), w_ref[...], preferred_element_type=F32) + b_ref[...]

    tn = 512
    return pl.pallas_call(
        body, name="ada_fwd", grid=(ADA_BLK // tn,),
        in_specs=[pl.BlockSpec((N_DEV, D_MODEL), lambda j: (0, 0)),
                  pl.BlockSpec((D_MODEL, tn), lambda j: (0, j)),
                  pl.BlockSpec((1, tn), lambda j: (0, j))],
        out_specs=pl.BlockSpec((N_DEV, tn), lambda j: (0, j)),
        out_shape=jax.ShapeDtypeStruct((N_DEV, ADA_BLK), F32),
        compiler_params=_params(("parallel",)))(c_all, w_ada_blk, b_blk)


def _ada_wgrad(c_all, gmod_cols):
    def body(c_ref, g_ref, o_ref):
        cv = c_ref[...]
        o_ref[...] = lax.dot_general(cv * _sigmoid(cv), g_ref[...], TN, preferred_element_type=F32)

    tk = 512
    return pl.pallas_call(
        body, name="ada_wgrad", grid=(D_MODEL // tk,),
        in_specs=[pl.BlockSpec((N_DEV, tk), lambda i: (0, i)),
                  pl.BlockSpec((N_DEV, ADA_BLK), lambda i: (0, 0))],
        out_specs=pl.BlockSpec((tk, ADA_BLK), lambda i: (i, 0)),
        out_shape=jax.ShapeDtypeStruct((D_MODEL, ADA_BLK), F32),
        compiler_params=_params(("parallel",)))(c_all, gmod_cols)


def _row_spec(cols=D_MODEL):
    return pl.BlockSpec((ROW_TILE, cols), lambda i: (i, 0))


def _vec_spec(cols=D_MODEL):
    return pl.BlockSpec((1, cols), lambda i: (0, 0))


def _norm_fwd(name, x, w, scale, shift, resid=None, gate=None):
    has_res = resid is not None

    def body(*refs):
        if has_res:
            x_ref, r_ref, g_ref, w_ref, sc_ref, sh_ref, xr_ref, h_ref, rs_ref = refs
            xr = x_ref[...] + g_ref[...] * r_ref[...]
            xr_ref[...] = xr
        else:
            x_ref, w_ref, sc_ref, sh_ref, h_ref, rs_ref = refs
            xr = x_ref[...]
        rs = lax.rsqrt(jnp.mean(xr * xr, axis=-1, keepdims=True) + EPS)
        h = (xr * rs) * w_ref[...] * (1.0 + sc_ref[...]) + sh_ref[...]
        h_ref[...] = h.astype(BF16)
        rs_ref[...] = rs

    s = x.shape[0]
    ins = [x] + ([resid, gate] if has_res else []) + [w, scale, shift]
    in_specs = [_row_spec()] + ([_row_spec(), _vec_spec()] if has_res else []) + [_vec_spec()] * 3
    outs = ([jax.ShapeDtypeStruct((s, D_MODEL), F32)] if has_res else []) + [
        jax.ShapeDtypeStruct((s, D_MODEL), BF16), jax.ShapeDtypeStruct((s, 1), F32)]
    out_specs = ([_row_spec()] if has_res else []) + [_row_spec(), pl.BlockSpec((ROW_TILE, 1), lambda i: (i, 0))]
    return pl.pallas_call(body, name=name, grid=(s // ROW_TILE,), in_specs=in_specs, out_specs=out_specs,
                          out_shape=outs, compiler_params=_params(("parallel",)))(*ins)


def _norm_bwd(name, dh, x, rstd, w, scale, dres, mix=None, gate=None, fused=None, fused_arrays=()):
    has_mix = mix is not None

    def body(*refs):
        if has_mix:
            (dh_ref, x_ref, rs_ref, w_ref, sc_ref, dr_ref, mix_ref, g_ref,
             dx_ref, dmix_ref, dsh_ref, dsc_ref, dw_ref, dg_ref) = refs
        else:
            dh_ref, x_ref, rs_ref, w_ref, sc_ref, dr_ref, dx_ref, dsh_ref, dsc_ref, dw_ref = refs
        i = pl.program_id(0)
        dhv = dh_ref[...]
        rs = rs_ref[...]
        xn = x_ref[...] * rs
        wv = w_ref[...]
        one_sc = 1.0 + sc_ref[...]
        dxn = dhv * wv * one_sc
        dx = dr_ref[...] + rs * (dxn - xn * jnp.mean(dxn * xn, axis=-1, keepdims=True))
        dx_ref[...] = dx
        sums = [(dsh_ref, dhv), (dsc_ref, dhv * xn * wv), (dw_ref, dhv * one_sc * xn)]
        if has_mix:
            dmix_ref[...] = (dx * g_ref[...]).astype(BF16)
            sums.append((dg_ref, dx * mix_ref[...]))

        @pl.when(i == 0)
        def _():
            for ref, _v in sums:
                ref[...] = jnp.zeros_like(ref)

        for ref, v in sums:
            ref[...] += jnp.sum(v, axis=0, keepdims=True)

    s = x.shape[0]
    ins = [dh, x, rstd, w, scale, dres] + ([mix, gate] if has_mix else [])
    in_specs = ([_row_spec(), _row_spec(), pl.BlockSpec((ROW_TILE, 1), lambda i: (i, 0)), _vec_spec(), _vec_spec(),
                 _row_spec()] + ([_row_spec(), _vec_spec()] if has_mix else []))
    vec = jax.ShapeDtypeStruct((1, D_MODEL), F32)
    outs = ([jax.ShapeDtypeStruct((s, D_MODEL), F32)] + ([jax.ShapeDtypeStruct((s, D_MODEL), BF16)] if has_mix else [])
            + [vec] * (4 if has_mix else 3))
    out_specs = [_row_spec()] + ([_row_spec()] if has_mix else []) + [_vec_spec()] * (4 if has_mix else 3)

    def first_last():
        i = pl.program_id(0)
        return i == 0, i == s // ROW_TILE - 1

    res, extra = _host_call(body, len(ins), len(outs), fused, first_last, name=name, grid=(s // ROW_TILE,),
                            in_specs=in_specs, out_specs=out_specs, out_shape=outs, scratch_shapes=[],
                            sem=("arbitrary",), operands=ins + list(fused_arrays))
    return res if fused is None else (res, extra)


def _loss_head(x1, ffn, gate2, target):
    def body(x_ref, f_ref, g_ref, t_ref, loss_ref, dout_ref, dffn_ref, dg_ref):
        i = pl.program_id(0)
        fv = f_ref[...]
        gv = g_ref[...]
        err = x_ref[...] + gv * fv - t_ref[...]
        dout = err * (1.0 / D_MODEL)
        dout_ref[...] = dout
        dffn_ref[...] = (dout * gv).astype(BF16)

        @pl.when(i == 0)
        def _():
            loss_ref[...] = jnp.zeros_like(loss_ref)
            dg_ref[...] = jnp.zeros_like(dg_ref)

        row = jnp.sum(err * err, axis=-1, keepdims=True) * (1.0 / D_MODEL)
        loss_ref[...] += jnp.broadcast_to(0.5 * jnp.sum(row, axis=0, keepdims=True), (1, 128))
        dg_ref[...] += jnp.sum(dout * fv, axis=0, keepdims=True)

    s = x1.shape[0]
    return pl.pallas_call(
        body, name="loss_head", grid=(s // ROW_TILE,),
        in_specs=[_row_spec(), _row_spec(), _vec_spec(), _row_spec()],
        out_specs=[pl.BlockSpec((1, 128), lambda i: (0, 0)), _row_spec(), _row_spec(), _vec_spec()],
        out_shape=[jax.ShapeDtypeStruct((1, 128), F32), jax.ShapeDtypeStruct((s, D_MODEL), F32),
                   jax.ShapeDtypeStruct((s, D_MODEL), BF16), jax.ShapeDtypeStruct((1, D_MODEL), F32)],
        compiler_params=_params(("arbitrary",)))(x1, ffn, gate2, target)


CONV_TILE = 512
N_CONV_TILES = D_FF // CONV_TILE


def _shift_rows(a, k, row):
    n = a.shape[0]
    if k > 0:
        return jnp.where(row >= k, pltpu.roll(a, k, 0), 0.0)
    return jnp.where(row < n + k, pltpu.roll(a, n + k, 0), 0.0)


def _conv_gate_fwd(u, conv_w, conv_b, fused=None, fused_arrays=()):
    s = u.shape[0]

    def body(a_ref, g_ref, w_ref, b_ref, y_ref):
        a = a_ref[...]
        w = w_ref[...]
        row = lax.broadcasted_iota(jnp.int32, a.shape, 0)
        ac = b_ref[...] + _shift_rows(a, 2, row) * w[0:1] + _shift_rows(a, 1, row) * w[1:2] + a * w[2:3]
        y_ref[...] = (ac * _sigmoid(ac) * g_ref[...]).astype(BF16)

    def first_last():
        i = pl.program_id(0)
        return i == 0, i == N_CONV_TILES - 1

    col = lambda off: pl.BlockSpec((s, CONV_TILE), lambda i: (0, i + off))
    (y,), extra = _host_call(
        body, 4, 1, fused, first_last, name="conv_gate_fwd", grid=(N_CONV_TILES,),
        in_specs=[col(0), col(N_CONV_TILES), pl.BlockSpec((3, CONV_TILE), lambda i: (0, i)),
                  pl.BlockSpec((1, CONV_TILE), lambda i: (0, i))],
        out_specs=[col(0)], out_shape=[jax.ShapeDtypeStruct((s, D_FF), BF16)], scratch_shapes=[], sem=("parallel",),
        operands=[u, u, conv_w, conv_b] + list(fused_arrays))
    return y if fused is None else (y, extra)


def _conv_gate_bwd(u, dy, conv_w, conv_b):
    s = u.shape[0]

    def body(a_ref, g_ref, dy_ref, w_ref, b_ref, da_ref, dg_ref, gw_ref, gb_ref):
        a = a_ref[...]
        w = w_ref[...]
        row = lax.broadcasted_iota(jnp.int32, a.shape, 0)
        a1 = _shift_rows(a, 1, row)
        a2 = _shift_rows(a, 2, row)
        ac = b_ref[...] + a2 * w[0:1] + a1 * w[1:2] + a * w[2:3]
        sg = _sigmoid(ac)
        dyv = dy_ref[...].astype(F32)
        dg_ref[...] = (dyv * (ac * sg)).astype(BF16)
        dac = dyv * g_ref[...] * _dsilu(ac, sg)
        gb_ref[...] = jnp.sum(dac, axis=0, keepdims=True)
        gw_ref[0:1, :] = jnp.sum(dac * a2, axis=0, keepdims=True)
        gw_ref[1:2, :] = jnp.sum(dac * a1, axis=0, keepdims=True)
        gw_ref[2:3, :] = jnp.sum(dac * a, axis=0, keepdims=True)
        da = dac * w[2:3] + _shift_rows(dac, -1, row) * w[1:2] + _shift_rows(dac, -2, row) * w[0:1]
        da_ref[...] = da.astype(BF16)

    col = lambda off: pl.BlockSpec((s, CONV_TILE), lambda i: (0, i + off))
    return pl.pallas_call(
        body, name="conv_gate_bwd", grid=(N_CONV_TILES,),
        in_specs=[col(0), col(N_CONV_TILES), col(0), pl.BlockSpec((3, CONV_TILE), lambda i: (0, i)),
                  pl.BlockSpec((1, CONV_TILE), lambda i: (0, i))],
        out_specs=[col(0), col(0), pl.BlockSpec((3, CONV_TILE), lambda i: (0, i)),
                   pl.BlockSpec((1, CONV_TILE), lambda i: (0, i))],
        out_shape=[jax.ShapeDtypeStruct((s, D_FF), BF16), jax.ShapeDtypeStruct((s, D_FF), BF16),
                   jax.ShapeDtypeStruct((3, D_FF), F32), jax.ShapeDtypeStruct((1, D_FF), F32)],
        compiler_params=_params(("parallel",)))(u, u, dy, conv_w, conv_b)


HG_TILE = 256
CHUNK_UNROLL = 8


def _unrolled_loop(n, body, init):
    def group(i, carry):
        for u in range(CHUNK_UNROLL):
            carry = body(i * CHUNK_UNROLL + u, carry)
        return carry

    return lax.fori_loop(0, n // CHUNK_UNROLL, group, init)


def _head_col(off):
    return pl.BlockSpec((SEQ, HEAD_DIM), lambda h: (0, h + off))


def _hgrn_gates(hq, hf, lb, pos):
    q = hq * _sigmoid(hq)
    sig = _sigmoid(hf)
    f = lb + (1.0 - lb) * sig
    gl = jnp.log(f)
    for sh in (1, 2, 4, 8):
        gl = gl + jnp.where(pos >= sh, pltpu.roll(gl, sh, 0), 0.0)
    return q, sig, f, 1.0 - f, gl


def _lower_bound(lbl):
    return 1.0 / (1.0 + jnp.exp(lbl[1:2, :] - lbl[0:1, :]))


def _head_first_last():
    h = pl.program_id(0)
    return h == 0, h == HEADS - 1


CHUNKS_PER_TILE = HG_TILE // CHUNK


def _chunk_end(x, pos):
    y = jnp.where(pos == CHUNK - 1, x, 0.0)
    for sh in (1, 2, 4, 8):
        y = y + jnp.where(pos < CHUNK - sh, pltpu.roll(y, x.shape[0] - sh, 0), 0.0)
    return y


def _suffix_in_chunk(x, pos):
    for sh in (1, 2, 4, 8):
        x = x + jnp.where(pos < CHUNK - sh, pltpu.roll(x, x.shape[0] - sh, 0), 0.0)
    return x


def _prefix_in_chunk(x, pos):
    for sh in (1, 2, 4, 8):
        x = x + jnp.where(pos >= sh, pltpu.roll(x, sh, 0), 0.0)
    return x


def _pair_decays(f, pos):
    shifted = jnp.where(pos >= 1, f, 0.0)
    e = shifted
    yield 1, e
    for d in range(2, CHUNK):
        shifted = pltpu.roll(shifted, 1, 0)
        e = e * shifted
        yield d, e


def _chunk_rows(cc):
    return slice(cc * CHUNK, (cc + 1) * CHUNK)


def _outer_products(lhs_b, rhs_b, dst, i):
    for cc in range(CHUNKS_PER_TILE):
        dst[i * CHUNKS_PER_TILE + cc] = lax.dot_general(lhs_b[_chunk_rows(cc)], rhs_b[_chunk_rows(cc)], TN,
                                                        preferred_element_type=F32)


def _state_scan(n_chunks, gl_s, u_s, keep, reverse):
    def step(k, st):
        c = n_chunks - 1 - k if reverse else k
        keep[c] = st.astype(BF16)
        gl = gl_s[pl.ds(pl.multiple_of(c * CHUNK, CHUNK), CHUNK), :]
        return st * jnp.exp(gl[CHUNK - 1:CHUNK, :]) + u_s[c]

    _unrolled_loop(n_chunks, step, jnp.zeros((HEAD_DIM, HEAD_DIM), F32))


def _hgrn_fwd(proj, lb_logits, norm_w, fused=None, fused_arrays=()):
    n_tiles = SEQ // HG_TILE
    n_chunks = SEQ // CHUNK
    fused_arrays = list(fused_arrays)

    def body(hq_ref, hf_ref, hi_ref, hg_ref, lbl_ref, nw_ref, aout_ref, opre_ref, qt_s, gl_s, u_s, st_s):
        lb = _lower_bound(lbl_ref[...])
        ones = jnp.ones((HEAD_DIM, HEAD_DIM), BF16)
        pos = lax.broadcasted_iota(jnp.int32, (HG_TILE, HEAD_DIM), 0) % CHUNK

        def tile(i, carry):
            rows = pl.ds(pl.multiple_of(i * HG_TILE, HG_TILE), HG_TILE)
            v = hi_ref[rows, :]
            q, _sig, f, kk, gl = _hgrn_gates(hq_ref[rows, :], hf_ref[rows, :], lb, pos)
            o = _lane_sum(q * kk, ones) * v
            for d, e in _pair_decays(f, pos):
                o = o + _lane_sum(q * pltpu.roll(kk, d, 0) * e, ones) * pltpu.roll(v, d, 0)
            opre_ref[rows, :] = o
            qt_s[rows, :] = q * jnp.exp(gl)
            gl_s[rows, :] = gl
            kt = kk * jnp.exp(_chunk_end(gl, pos) - gl)
            _outer_products(v.astype(BF16), kt.astype(BF16), u_s, i)
            return carry

        lax.fori_loop(0, n_tiles, tile, 0)
        _state_scan(n_chunks, gl_s, u_s, st_s, reverse=False)

        def finish(i, carry):
            rows = pl.ds(pl.multiple_of(i * HG_TILE, HG_TILE), HG_TILE)
            qt_b = qt_s[rows, :].astype(BF16)
            past = [lax.dot_general(qt_b[_chunk_rows(cc)], st_s[i * CHUNKS_PER_TILE + cc], NT,
                                    preferred_element_type=F32) for cc in range(CHUNKS_PER_TILE)]
            o = opre_ref[rows, :] + jnp.concatenate(past, axis=0)
            opre_ref[rows, :] = o
            hg = hg_ref[rows, :]
            rs = lax.rsqrt(jnp.mean(o * o, axis=-1, keepdims=True) + EPS)
            aout_ref[rows, :] = ((o * rs) * nw_ref[...] * (hg * _sigmoid(hg))).astype(BF16)
            return carry

        lax.fori_loop(0, n_tiles, finish, 0)

    return _host_call(
        body, 6, 2, fused, _head_first_last, name="hgrn_fwd", grid=(HEADS,),
        in_specs=[_head_col(0), _head_col(HEADS), _head_col(2 * HEADS), _head_col(3 * HEADS),
                  pl.BlockSpec((2, HEAD_DIM), lambda h: (0, h)), pl.BlockSpec((1, HEAD_DIM), lambda h: (0, 0))],
        out_specs=[_head_col(0), _head_col(0)],
        out_shape=[jax.ShapeDtypeStruct((SEQ, HEADS * HEAD_DIM), BF16), jax.ShapeDtypeStruct((SEQ, HEADS * HEAD_DIM), F32)],
        scratch_shapes=[pltpu.VMEM((SEQ, HEAD_DIM), F32)] * 2 + [pltpu.VMEM((n_chunks, HEAD_DIM, HEAD_DIM), F32),
                                                                 pltpu.VMEM((n_chunks, HEAD_DIM, HEAD_DIM), BF16)],
        sem=("parallel",), operands=[proj, proj, proj, proj, lb_logits, norm_w] + fused_arrays)


def _hgrn_bwd(proj, lb_logits, norm_w, o_pre, d_aout, fused=None, fused_arrays=()):
    n_tiles = SEQ // HG_TILE
    n_chunks = SEQ // CHUNK

    def body(hq_ref, hf_ref, hi_ref, hg_ref, lbl_ref, nw_ref, opre_ref, da_ref,
             dhq_ref, dhf_ref, dhi_ref, dhg_ref, dlog_ref, gnw_ref,
             q_s, k_s, gl_s, do_s, dq_s, dk_s, dv_s, u_s, st_s, rt_s):
        h = pl.program_id(0)
        lb = _lower_bound(lbl_ref[...])
        nw = nw_ref[...]
        ones = jnp.ones((HEAD_DIM, HEAD_DIM), BF16)
        pos = lax.broadcasted_iota(jnp.int32, (HG_TILE, HEAD_DIM), 0) % CHUNK

        @pl.when(h == 0)
        def _():
            gnw_ref[...] = jnp.zeros_like(gnw_ref)

        def tile(i, carry):
            rows = pl.ds(pl.multiple_of(i * HG_TILE, HG_TILE), HG_TILE)
            v = hi_ref[rows, :]
            q, _sig, f, kk, gl = _hgrn_gates(hq_ref[rows, :], hf_ref[rows, :], lb, pos)
            o = opre_ref[rows, :]
            hg = hg_ref[rows, :]
            da = da_ref[rows, :]
            rs = lax.rsqrt(jnp.mean(o * o, axis=-1, keepdims=True) + EPS)
            oh = o * rs
            sg = _sigmoid(hg)
            dnorm = da * (hg * sg)
            dhg_ref[rows, :] = (da * (oh * nw) * _dsilu(hg, sg)).astype(BF16)
            gnw_ref[...] += jnp.sum(dnorm * oh, axis=0, keepdims=True)
            doh = dnorm * nw
            do = rs * (doh - oh * jnp.mean(doh * oh, axis=-1, keepdims=True))

            d_a = _lane_sum(do * v, ones)
            dq = d_a * kk
            dk = d_a * q
            dv = _lane_sum(q * kk, ones) * do
            for d, e in _pair_decays(f, pos):
                ks = pltpu.roll(kk, d, 0)
                a_d = _lane_sum(q * ks * e, ones)
                d_a = _lane_sum(do * pltpu.roll(v, d, 0), ones) * e
                dq = dq + d_a * ks
                dk = dk + pltpu.roll(d_a * q, HG_TILE - d, 0)
                dv = dv + pltpu.roll(a_d * do, HG_TILE - d, 0)
            q_s[rows, :] = q
            k_s[rows, :] = kk
            gl_s[rows, :] = gl
            do_s[rows, :] = do
            dq_s[rows, :] = dq
            dk_s[rows, :] = dk
            dv_s[rows, :] = dv
            kt = kk * jnp.exp(_chunk_end(gl, pos) - gl)
            _outer_products(v.astype(BF16), kt.astype(BF16), u_s, i)
            return carry

        lax.fori_loop(0, n_tiles, tile, 0)
        _state_scan(n_chunks, gl_s, u_s, st_s, reverse=False)

        def reverse_increments(i, carry):
            rows = pl.ds(pl.multiple_of(i * HG_TILE, HG_TILE), HG_TILE)
            qt = q_s[rows, :] * jnp.exp(gl_s[rows, :])
            _outer_products(do_s[rows, :].astype(BF16), qt.astype(BF16), u_s, i)
            return carry

        lax.fori_loop(0, n_tiles, reverse_increments, 0)
        _state_scan(n_chunks, gl_s, u_s, rt_s, reverse=True)

        def finish(i, dlb):
            rows = pl.ds(pl.multiple_of(i * HG_TILE, HG_TILE), HG_TILE)
            q = q_s[rows, :]
            kk = k_s[rows, :]
            gl = gl_s[rows, :]
            gll = _chunk_end(gl, pos)
            ekt = jnp.exp(gll - gl)
            do_b = do_s[rows, :].astype(BF16)
            v_b = hi_ref[rows, :].astype(BF16)
            kt_b = (kk * ekt).astype(BF16)
            dq_far, dk_far, dv_far, across = [], [], [], []
            for cc in range(CHUNKS_PER_TILE):
                st = st_s[i * CHUNKS_PER_TILE + cc]
                rt = rt_s[i * CHUNKS_PER_TILE + cc]
                sl = _chunk_rows(cc)
                dq_far.append(jnp.dot(do_b[sl], st, preferred_element_type=F32))
                dk_far.append(jnp.dot(v_b[sl], rt, preferred_element_type=F32))
                dv_far.append(lax.dot_general(kt_b[sl], rt, NT, preferred_element_type=F32))
                both = jnp.sum(st.astype(F32) * rt.astype(F32), axis=0, keepdims=True)
                across.append(jnp.broadcast_to(both, (CHUNK, HEAD_DIM)))
            dq = dq_s[rows, :] + jnp.concatenate(dq_far, axis=0) * jnp.exp(gl)
            dk_in = dk_s[rows, :]
            dk_out = jnp.concatenate(dk_far, axis=0) * ekt
            dk = dk_in + dk_out
            dv = dv_s[rows, :] + jnp.concatenate(dv_far, axis=0)
            pc = kk * dk_out
            dgl = (_suffix_in_chunk(q * dq - kk * dk_in, pos) + (_prefix_in_chunk(pc, pos) - pc)
                   + jnp.concatenate(across, axis=0) * jnp.exp(gll))
            hf = hf_ref[rows, :]
            sig = _sigmoid(hf)
            f = lb + (1.0 - lb) * sig
            df = dgl / f - dk
            dhf_ref[rows, :] = (df * (1.0 - lb) * sig * (1.0 - sig)).astype(BF16)
            hq = hq_ref[rows, :]
            dhq_ref[rows, :] = (dq * _dsilu(hq, _sigmoid(hq))).astype(BF16)
            dhi_ref[rows, :] = dv.astype(BF16)
            return dlb + jnp.sum(df * (1.0 - sig), axis=0, keepdims=True)

        dlb = lax.fori_loop(0, n_tiles, finish, jnp.zeros((1, HEAD_DIM), F32))
        dl0 = lb * (1.0 - lb) * dlb
        dlog_ref[0:1, :] = dl0
        dlog_ref[1:2, :] = -dl0

    wide = HEADS * HEAD_DIM
    return _host_call(
        body, 8, 6, fused, _head_first_last, name="hgrn_bwd", grid=(HEADS,),
        in_specs=[_head_col(0), _head_col(HEADS), _head_col(2 * HEADS), _head_col(3 * HEADS),
                  pl.BlockSpec((2, HEAD_DIM), lambda h: (0, h)), pl.BlockSpec((1, HEAD_DIM), lambda h: (0, 0)),
                  _head_col(0), _head_col(0)],
        out_specs=[_head_col(0)] * 4 + [pl.BlockSpec((2, HEAD_DIM), lambda h: (0, h)),
                                        pl.BlockSpec((1, HEAD_DIM), lambda h: (0, 0))],
        out_shape=[jax.ShapeDtypeStruct((SEQ, wide), BF16)] * 4 + [jax.ShapeDtypeStruct((2, wide), F32),
                                                                    jax.ShapeDtypeStruct((1, HEAD_DIM), F32)],
        scratch_shapes=[pltpu.VMEM((SEQ, HEAD_DIM), F32)] * 7 + [pltpu.VMEM((n_chunks, HEAD_DIM, HEAD_DIM), F32),
                                                                 pltpu.VMEM((n_chunks, HEAD_DIM, HEAD_DIM), BF16),
                                                                 pltpu.VMEM((n_chunks, HEAD_DIM, HEAD_DIM), BF16)],
        sem=("arbitrary",),
        operands=[proj, proj, proj, proj, lb_logits, norm_w, o_pre, d_aout] + list(fused_arrays))


Q_TILE = 256
ATT_SCALE = HEAD_DIM ** -0.5
ATT_OFF = 4 * HEADS


def _qk_prep(proj, q_w, k_w, fused=None, fused_arrays=()):
    def body(aq_ref, ak_ref, av_ref, qw_ref, kw_ref, qn_ref, kn_ref, v_ref):
        aq = aq_ref[...]
        ak = ak_ref[...]
        qn_ref[...] = (aq * lax.rsqrt(jnp.mean(aq * aq, axis=-1, keepdims=True) + EPS) * qw_ref[...]).astype(BF16)
        kn_ref[...] = (ak * lax.rsqrt(jnp.mean(ak * ak, axis=-1, keepdims=True) + EPS) * kw_ref[...]).astype(BF16)
        v_ref[...] = av_ref[...].astype(BF16)

    wide = HEADS * HEAD_DIM
    vec = pl.BlockSpec((1, HEAD_DIM), lambda h: (0, 0))
    return _host_call(
        body, 5, 3, fused, _head_first_last, name="qk_prep", grid=(HEADS,),
        in_specs=[_head_col(ATT_OFF), _head_col(ATT_OFF + HEADS), _head_col(ATT_OFF + 2 * HEADS), vec, vec],
        out_specs=[_head_col(0)] * 3, out_shape=[jax.ShapeDtypeStruct((SEQ, wide), BF16)] * 3,
        scratch_shapes=[], sem=("parallel",), operands=[proj, proj, proj, q_w, k_w] + list(fused_arrays))


def _alibi_slopes():
    slopes = jnp.exp2(-8.0 * jnp.arange(1, HEADS + 1, dtype=F32) / HEADS)
    return jnp.broadcast_to(slopes[:, None, None], (HEADS, 1, HEAD_DIM))


SLOPE_SPEC = pl.BlockSpec((None, 1, HEAD_DIM), lambda h, i: (h, 0, 0))


N_Q_TILES = SEQ // Q_TILE
K_BLOCK = 512
NOT_ATTENDED = 1e35


def _att_tables():
    o = jnp.arange(N_Q_TILES, dtype=jnp.int32)[:, None, None]
    r = jnp.arange(Q_TILE, dtype=jnp.int32)[None, :, None]
    c = jnp.arange(K_BLOCK, dtype=jnp.int32)[None, None, :]
    dist = o * Q_TILE + r - c
    mult = ((dist <= 128).astype(F32) + (((dist % 4) == 0) & (dist <= 512)).astype(F32)
            + ((dist % 16) == 0).astype(F32))
    valid = (dist >= 0) & (mult > 0)
    return (jnp.where(valid, dist.astype(F32), NOT_ATTENDED),
            jnp.where(valid, jnp.log(jnp.maximum(mult, 1.0)), 0.0))


TABLE_SPEC = pl.BlockSpec((N_Q_TILES, Q_TILE, K_BLOCK), lambda h, i: (0, 0, 0))


def _att_block(q, k_ref, j, i, slope, dist_ref, lmul_ref):
    rows = pl.ds(pl.multiple_of(j * K_BLOCK, K_BLOCK), K_BLOCK)
    off = i - j * (K_BLOCK // Q_TILE)
    s = lax.dot_general(q, k_ref[rows, :], NT, preferred_element_type=F32) * ATT_SCALE
    return s - slope * dist_ref[off] + lmul_ref[off], rows


def _n_key_blocks(i):
    return (i + K_BLOCK // Q_TILE) // (K_BLOCK // Q_TILE)


def _att_first_last():
    h, i = pl.program_id(0), pl.program_id(1)
    return (h == 0) & (i == 0), (h == HEADS - 1) & (i == N_Q_TILES - 1)


def _attn_fwd(qn, kn, vb, fused=None, fused_arrays=()):
    def body(q_ref, k_ref, v_ref, sl_ref, dist_ref, lmul_ref, o_ref, lse_ref):
        i = pl.program_id(1)
        q = q_ref[...]
        slope = sl_ref[0:1, 0:1]

        def step(j, carry):
            m, l, acc = carry
            sb, rows = _att_block(q, k_ref, j, i, slope, dist_ref, lmul_ref)
            m_new = jnp.maximum(m, jnp.max(sb, axis=-1, keepdims=True))
            alpha = jnp.exp(m - m_new)
            p = jnp.exp(sb - m_new)
            l = alpha * l + jnp.sum(p, axis=-1, keepdims=True)
            acc = alpha * acc + jnp.dot(p.astype(BF16), v_ref[rows, :], preferred_element_type=F32)
            return m_new, l, acc

        m, l, acc = lax.fori_loop(0, _n_key_blocks(i), step,
                                  (jnp.full((Q_TILE, 1), -1e30, F32), jnp.zeros((Q_TILE, 1), F32),
                                   jnp.zeros((Q_TILE, HEAD_DIM), F32)))
        o_ref[...] = acc / l
        lse_ref[...] = m + jnp.log(l)

    wide = HEADS * HEAD_DIM
    qt = pl.BlockSpec((Q_TILE, HEAD_DIM), lambda h, i: (i, h))
    full = pl.BlockSpec((SEQ, HEAD_DIM), lambda h, i: (0, h))
    return _host_call(
        body, 6, 2, fused, _att_first_last, name="attn_fwd", grid=(HEADS, N_Q_TILES),
        in_specs=[qt, full, full, SLOPE_SPEC, TABLE_SPEC, TABLE_SPEC],
        out_specs=[qt, pl.BlockSpec((None, Q_TILE, 1), lambda h, i: (h, i, 0))],
        out_shape=[jax.ShapeDtypeStruct((SEQ, wide), F32), jax.ShapeDtypeStruct((HEADS, SEQ, 1), F32)],
        scratch_shapes=[], sem=("parallel", "parallel"),
        operands=[qn, kn, vb, _alibi_slopes(), *_att_tables()] + list(fused_arrays))


def _attn_bwd(qn, kn, vb, o, lse, d_mix, fused=None, fused_arrays=()):
    def body(q_ref, k_ref, v_ref, o_ref, lse_ref, do_ref, sl_ref, dist_ref, lmul_ref, dq_ref, dk_ref, dv_ref):
        i = pl.program_id(1)
        q = q_ref[...]
        do = do_ref[...]
        do_b = do.astype(BF16)
        slope = sl_ref[0:1, 0:1]
        lse = lse_ref[...]
        delta = jnp.sum(do * o_ref[...], axis=-1, keepdims=True)

        @pl.when(i == 0)
        def _():
            dk_ref[...] = jnp.zeros_like(dk_ref)
            dv_ref[...] = jnp.zeros_like(dv_ref)

        def step(j, dq):
            sb, rows = _att_block(q, k_ref, j, i, slope, dist_ref, lmul_ref)
            p = jnp.exp(sb - lse)
            dp = lax.dot_general(do_b, v_ref[rows, :], NT, preferred_element_type=F32)
            ds = (p * (dp - delta)).astype(BF16)
            dk_ref[rows, :] += lax.dot_general(ds, q, TN, preferred_element_type=F32) * ATT_SCALE
            dv_ref[rows, :] += lax.dot_general(p.astype(BF16), do_b, TN, preferred_element_type=F32)
            return dq + jnp.dot(ds, k_ref[rows, :], preferred_element_type=F32)

        dq = lax.fori_loop(0, _n_key_blocks(i), step, jnp.zeros((Q_TILE, HEAD_DIM), F32))
        dq_ref[...] = dq * ATT_SCALE

    wide = HEADS * HEAD_DIM
    qt = pl.BlockSpec((Q_TILE, HEAD_DIM), lambda h, i: (i, h))
    full = pl.BlockSpec((SEQ, HEAD_DIM), lambda h, i: (0, h))
    return _host_call(
        body, 9, 3, fused, _att_first_last, name="attn_bwd", grid=(HEADS, N_Q_TILES),
        in_specs=[qt, full, full, qt, pl.BlockSpec((None, Q_TILE, 1), lambda h, i: (h, i, 0)),
                  pl.BlockSpec((Q_TILE, HEAD_DIM), lambda h, i: (i, h + HEADS)), SLOPE_SPEC, TABLE_SPEC, TABLE_SPEC],
        out_specs=[qt, full, full], out_shape=[jax.ShapeDtypeStruct((SEQ, wide), F32)] * 3,
        scratch_shapes=[], sem=("parallel", "arbitrary"),
        operands=[qn, kn, vb, o, lse, d_mix, _alibi_slopes(), *_att_tables()] + list(fused_arrays))


def _qk_bwd(proj, q_w, k_w, dqn, dkn, dv):
    def body(aq_ref, ak_ref, qw_ref, kw_ref, dqn_ref, dkn_ref, dv_ref, daq_ref, dak_ref, dav_ref, gq_ref, gk_ref):
        h = pl.program_id(0)

        @pl.when(h == 0)
        def _():
            gq_ref[...] = jnp.zeros_like(gq_ref)
            gk_ref[...] = jnp.zeros_like(gk_ref)

        def one(a_ref, w_ref, d_ref, da_ref, g_ref):
            a = a_ref[...]
            d = d_ref[...]
            rs = lax.rsqrt(jnp.mean(a * a, axis=-1, keepdims=True) + EPS)
            ah = a * rs
            g_ref[...] += jnp.sum(d * ah, axis=0, keepdims=True)
            dah = d * w_ref[...]
            da_ref[...] = (rs * (dah - ah * jnp.mean(dah * ah, axis=-1, keepdims=True))).astype(BF16)

        one(aq_ref, qw_ref, dqn_ref, daq_ref, gq_ref)
        one(ak_ref, kw_ref, dkn_ref, dak_ref, gk_ref)
        dav_ref[...] = dv_ref[...].astype(BF16)

    wide = HEADS * HEAD_DIM
    vec = pl.BlockSpec((1, HEAD_DIM), lambda h: (0, 0))
    return pl.pallas_call(
        body, name="qk_bwd", grid=(HEADS,),
        in_specs=[_head_col(ATT_OFF), _head_col(ATT_OFF + HEADS), vec, vec, _head_col(0), _head_col(0), _head_col(0)],
        out_specs=[_head_col(0)] * 3 + [vec, vec],
        out_shape=[jax.ShapeDtypeStruct((SEQ, wide), BF16)] * 3 + [jax.ShapeDtypeStruct((1, HEAD_DIM), F32)] * 2,
        compiler_params=_params(("arbitrary",)))(proj, proj, q_w, k_w, dqn, dkn, dv)


def _pair_sum(name, partial, theirs, core):
    _, r, c = theirs.shape
    tr = r // 2 if r % 16 == 0 else r

    def body(core_ref, a_ref, b_ref, o_ref):
        o_ref[...] = (a_ref[...].astype(F32) + b_ref[...].astype(F32)).astype(BF16)

    spec = pl.BlockSpec((None, tr, c), lambda q, i, core_ref: (q, i, 0))
    grid_spec = pltpu.PrefetchScalarGridSpec(
        num_scalar_prefetch=1, grid=(4, r // tr),
        in_specs=[pl.BlockSpec((None, tr, c), lambda q, i, core_ref: (2 * q + core_ref[0], i, 0)), spec],
        out_specs=spec)
    return pl.pallas_call(body, name=name, grid_spec=grid_spec, out_shape=jax.ShapeDtypeStruct(theirs.shape, BF16),
                          compiler_params=_params(("parallel", "parallel")))(core, partial, theirs)


def _adamw_step(w, m, v, g):
    nm = ADAM_B1 * m + (1.0 - ADAM_B1) * g
    nv = ADAM_B2 * v + (1.0 - ADAM_B2) * (g * g)
    m_hat = nm / (1.0 - ADAM_B1 ** ADAM_STEP)
    v_hat = nv / (1.0 - ADAM_B2 ** ADAM_STEP)
    return -ADAM_LR * (m_hat / (jnp.sqrt(v_hat) + ADAM_EPS) + ADAM_WD * w), nm, nv


def _adamw(name, w, m, v, addends, tr=None):
    r, c = w.shape
    tr = r if tr is None else tr
    n_add = len(addends)

    def body(*refs):
        w_ref, m_ref, v_ref = refs[:3]
        add_refs = refs[3:3 + n_add]
        g_ref, d_ref, nm_ref, nv_ref = refs[3 + n_add:]
        g = add_refs[0][...].astype(F32)
        for a_ref in add_refs[1:]:
            g = g + a_ref[...].astype(F32)
        g_ref[...] = g
        d_ref[...], nm_ref[...], nv_ref[...] = _adamw_step(w_ref[...], m_ref[...], v_ref[...], g)

    spec = pl.BlockSpec((tr, c), lambda i: (i, 0))
    out = jax.ShapeDtypeStruct((r, c), F32)
    return pl.pallas_call(body, name=name, grid=(r // tr,), in_specs=[spec] * (3 + n_add), out_specs=[spec] * 4,
                          out_shape=[out] * 4, compiler_params=_params(("parallel",)))(w, m, v, *addends)


def _adamw_reduced(name, w, m, v, chip_sums, received, chip, tr):
    r, c = w.shape

    def body(chip_ref, w_ref, m_ref, v_ref, own_ref, r0_ref, r1_ref, r2_ref, g_ref, d_ref, nm_ref, nv_ref):
        g = ((own_ref[...].astype(F32) + r0_ref[...].astype(F32)) + r1_ref[...].astype(F32)) + r2_ref[...].astype(F32)
        g_ref[...] = g
        d_ref[...], nm_ref[...], nv_ref[...] = _adamw_step(w_ref[...], m_ref[...], v_ref[...], g)

    spec = pl.BlockSpec((tr, c), lambda i, chip_ref: (i, 0))

    def slot(k):
        return pl.BlockSpec((None, tr, c), lambda i, chip_ref: (k, i, 0))

    grid_spec = pltpu.PrefetchScalarGridSpec(
        num_scalar_prefetch=1, grid=(r // tr,),
        in_specs=[spec, spec, spec, pl.BlockSpec((None, tr, c), lambda i, chip_ref: (chip_ref[0], i, 0)),
                  slot(0), slot(1), slot(2)],
        out_specs=[spec] * 4)
    out = jax.ShapeDtypeStruct((r, c), F32)
    return pl.pallas_call(body, name=name, grid_spec=grid_spec, out_shape=[out] * 4,
                          compiler_params=_params(("parallel",)))(chip, w, m, v, chip_sums, received, received, received)


def _sum_devices(gathered):
    _, r, c = gathered.shape

    def body(g_ref, o_ref):
        acc = g_ref[0]
        for d in range(1, N_DEV):
            acc = acc + g_ref[d]
        o_ref[...] = acc

    return pl.pallas_call(body, name="sum_devices", out_shape=jax.ShapeDtypeStruct((r, c), F32))(gathered)


def _pack_rows(vectors, rows):
    flat = jnp.concatenate([v.reshape(-1) for v in vectors])
    return jnp.pad(flat, (0, rows * 128 - flat.shape[0])).reshape(rows, 128)


def _unpack(flat, shapes):
    out, off = [], 0
    for shp in shapes:
        n = 1
        for d in shp:
            n *= d
        out.append(flat[off:off + n].reshape(shp))
        off += n
    return out


def _device_step(xs, tgt, mod, norm1_w, norm2_w, lb_logits, hg_norm_w, q_norm_w, k_norm_w, conv_w_full, conv_b,
                 win_g, w_out_x, w_up_x, w_down_x, core=None):
    fused = core is not None
    shift1, scale1, gate1, shift2, scale2, gate2 = (mod[k] for k in range(6))

    h, rstd1 = _norm_fwd("norm1_fwd", xs, norm1_w, scale1, shift1)
    if fused:
        near = (0, 1, 2)
        head_rows, tail_rows = (0, UP_HEAD_ROWS), (UP_HEAD_ROWS, D_MODEL - UP_HEAD_ROWS)
        proj, (wout_g, wup_g) = _mm_blocked_rhs(
            "mm_in", h, win_g, fused_arrays=[w_out_x, w_up_x],
            fused=[_FusedCopies("gather", [w_out_x]), _FusedCopies("gather", [w_up_x], peers=near, rows=head_rows)])
        (a_out, o_pre), (wup_g,) = _hgrn_fwd(
            proj, lb_logits, hg_norm_w, fused_arrays=[w_up_x, wup_g],
            fused=_FusedCopies("gather_more", [w_up_x, wup_g], peers=near, rows=tail_rows, relay_rows=head_rows))
        wout_g, = _forward_to_sibling("allgather_stage2_out", [wout_g])
        wout_full = wout_g.reshape(D_MODEL, D_MODEL)
        (qn, kn, vb), _ = _qk_prep(proj, q_norm_w, k_norm_w)
        (att_o, lse), (wup_g,) = _attn_fwd(qn, kn, vb, _FusedCopies("relay", [wup_g], rows=tail_rows), [wup_g])
    else:
        proj = _mm_blocked_rhs("mm_in", h, win_g)
        (a_out, o_pre), _ = _hgrn_fwd(proj, lb_logits, hg_norm_w)
        wup_g, wout_full, wdown_full = w_up_x, w_out_x, w_down_x
        (qn, kn, vb), _ = _qk_prep(proj, q_norm_w, k_norm_w)
        (att_o, lse), _ = _attn_fwd(qn, kn, vb)
    mixin = jnp.concatenate([a_out, att_o.astype(BF16)], axis=1)
    if fused:
        mix, (wup_g,) = _mm_plain("mm_out", mixin, wout_full, NN, 512, 1024, F32,
                                  fused=_FusedCopies("forward", [wup_g]), fused_arrays=[wup_g])
    else:
        mix = _mm_plain("mm_out", mixin, wout_full, NN, 512, 1024, F32)
    x1, h2, rstd2 = _norm_fwd("norm2_fwd", xs, norm2_w, scale2, shift2, resid=mix, gate=gate1)
    if fused:
        u, (wdown_g,) = _mm_blocked_rhs("mm_up", h2, wup_g, fused=_FusedCopies("gather", [w_down_x]),
                                        fused_arrays=[w_down_x])
        y, (wdown_g,) = _conv_gate_fwd(u, conv_w_full, conv_b, _FusedCopies("forward", [wdown_g]), [wdown_g])
        wdown_full = wdown_g.reshape(D_FF, D_MODEL)
    else:
        u = _mm_blocked_rhs("mm_up", h2, wup_g)
        y = _conv_gate_fwd(u, conv_w_full, conv_b)
    ffn = _mm_plain("mm_down", y, wdown_full, NN, 512, 512, F32)
    loss_v, dout, dffn, dgate2 = _loss_head(x1, ffn, gate2, tgt)

    dy = _mm_plain("mm_down_dx", dffn, wdown_full, NT, 512, UP_BLK, BF16)
    gw_down = _mm_plain("mm_down_dw", y, dffn, TN, UP_BLK, 1024, BF16)
    da, dg, gconv_w, gconv_b = _conv_gate_bwd(u, dy, conv_w_full, conv_b)
    dh2 = _mm_halves_rhs_t("mm_up_dx", da, dg, wup_g)
    gw_up = _mm_halves_wgrad("mm_up_dw", h2, da, dg)
    if fused:
        part_up, part_down = gw_up, gw_down.reshape(N_DEV, FF_BLK, D_MODEL)
        (dx1, dmix, dshift2, dscale2, gnorm2, dgate1), (sib_up, sib_down) = _norm_bwd(
            "norm2_bwd", dh2, x1, rstd2, norm2_w, scale2, dout, mix=mix, gate=gate1,
            fused=_FusedCopies("sibling", [part_up, part_down]), fused_arrays=[part_up, part_down])
    else:
        dx1, dmix, dshift2, dscale2, gnorm2, dgate1 = _norm_bwd(
            "norm2_bwd", dh2, x1, rstd2, norm2_w, scale2, dout, mix=mix, gate=gate1)
    gw_out = _mm_plain("mm_out_dw", mixin, dmix, TN, 512, 1024, BF16)
    if fused:
        part_out = gw_out.reshape(N_DEV, OUT_BLK, D_MODEL)
        dmixin, (sib_out,) = _mm_plain("mm_out_dx", dmix, wout_full, NT, 512, 1024, F32,
                                       fused=_FusedCopies("sibling", [part_out]), fused_arrays=[part_out])
        cs_up = _pair_sum("grad_pair_sum_up", part_up, sib_up, core)
        cs_out = _pair_sum("grad_pair_sum_out", part_out, sib_out, core)
        cs_down = _pair_sum("grad_pair_sum_down", part_down, sib_down, core)
        (dhq, dhf, dhi, dhg, glog, ghg), (fc_up,) = _hgrn_bwd(
            proj, lb_logits, hg_norm_w, o_pre, dmixin, _FusedCopies("chips", [cs_up]), [cs_up])
        (dqn, dkn, dvv), (fc_down,) = _attn_bwd(qn, kn, vb, att_o, lse, dmixin,
                                                _FusedCopies("chips", [cs_down]), [cs_down])
    else:
        dmixin = _mm_plain("mm_out_dx", dmix, wout_full, NT, 512, 1024, F32)
        (dhq, dhf, dhi, dhg, glog, ghg), _ = _hgrn_bwd(proj, lb_logits, hg_norm_w, o_pre, dmixin)
        (dqn, dkn, dvv), _ = _attn_bwd(qn, kn, vb, att_o, lse, dmixin)
    daq, dak, dav, gqw, gkw = _qk_bwd(proj, q_norm_w, k_norm_w, dqn, dkn, dvv)
    dproj = jnp.concatenate([dhq, dhf, dhi, dhg, daq, dak, dav], axis=1)
    if fused:
        gw_in, (fc_out,) = _mm_wgrad_blocked("mm_in_dw", h, dproj, fused=_FusedCopies("chips", [cs_out]),
                                             fused_arrays=[cs_out])
        from_sibling, = _exchange_sibling("grad_exchange_sibling_b", [gw_in])
        cs_in = _pair_sum("grad_pair_sum_in", gw_in, from_sibling, core)
        dh, (fc_in,) = _mm_blocked_rhs_t("mm_in_dx", dproj, win_g, fused=_FusedCopies("chips", [cs_in]),
                                         fused_arrays=[cs_in])
        large = [(cs_in, fc_in), (cs_out, fc_out), (cs_up, fc_up), (cs_down, fc_down)]
    else:
        gw_in = _mm_wgrad_blocked("mm_in_dw", h, dproj)
        dh = _mm_blocked_rhs_t("mm_in_dx", dproj, win_g)
        large = [gw_in, gw_out, gw_up, gw_down]
    grad_x, dshift1, dscale1, gnorm1 = _norm_bwd("norm1_bwd", dh, xs, rstd1, norm1_w, scale1, dx1)
    gmod = jnp.concatenate([dshift1, dscale1, dgate1, dshift2, dscale2, dgate2], axis=1)
    return (loss_v, grad_x, gmod, gnorm1, gnorm2, glog, ghg, gqw, gkw, gconv_b, gconv_w, *large)


def kernel(x, c, w_ada, b_ada, norm1_w, w_in, lb_logits, hg_norm_w, q_norm_w, k_norm_w, w_out, norm2_w, w_up, conv_w, conv_b, w_down, loss_target, m_w_ada, m_b_ada, m_norm1_w, m_w_in, m_lb_logits, m_hg_norm_w, m_q_norm_w, m_k_norm_w, m_w_out, m_norm2_w, m_w_up, m_conv_w, m_conv_b, m_w_down, v_w_ada, v_b_ada, v_norm1_w, v_w_in, v_lb_logits, v_hg_norm_w, v_q_norm_w, v_k_norm_w, v_w_out, v_norm2_w, v_w_up, v_conv_w, v_conv_b, v_w_down):
    ix, iy, ic = lax.axis_index("x"), lax.axis_index("y"), lax.axis_index("c")
    me = 4 * ix + 2 * iy + ic
    my_chip = 2 * ix + iy

    xs = x[0]
    tgt = loss_target[0]

    win_g, = _allgather_weights([w_in[0].astype(BF16)])

    c_all = _allgather_vmem(c.reshape(8, D_MODEL // 8), "allgather_c").reshape(N_DEV, D_MODEL)
    b_blk = lax.dynamic_slice_in_dim(b_ada, me * ADA_BLK, ADA_BLK, axis=1)
    mod_cols = _ada_fwd(c_all, w_ada[0], b_blk)
    mod_all = _allgather_vmem(mod_cols, "allgather_mod").reshape(N_DEV, N_DEV, ADA_BLK)
    mod = lax.dynamic_index_in_dim(mod_all, me, axis=1, keepdims=False).reshape(6, 1, D_MODEL)

    conv_w_all = _allgather_vmem(_pack_rows([conv_w[0]], 24), "allgather_conv_w").reshape(N_DEV, 24 * 128)
    conv_w_full = conv_w_all[:, :3 * FF_BLK].reshape(N_DEV, 3, FF_BLK).transpose(1, 0, 2).reshape(3, D_FF)

    (loss_v, grad_x, gmod, gnorm1, gnorm2, glog, ghg, gqw, gkw, gconv_b, gconv_w,
     rs_in, rs_out, rs_up, rs_down) = _device_step(
        xs, tgt, mod, norm1_w, norm2_w, lb_logits, hg_norm_w, q_norm_w, k_norm_w, conv_w_full, conv_b,
        win_g, w_out[0].astype(BF16), w_up[0].astype(BF16), w_down[0].astype(BF16),
        core=jnp.reshape(ic, (1,)).astype(jnp.int32))
    loss = lax.psum(loss_v[0, 0], AXES)

    small_shapes = [(1, 6 * D_MODEL), (1, D_MODEL), (1, D_MODEL), (2, HEADS * HEAD_DIM), (1, HEAD_DIM),
                    (1, HEAD_DIM), (1, HEAD_DIM), (1, D_FF), (3, D_FF)]
    small = [gmod, gnorm1, gnorm2, glog, ghg, gqw, gkw, gconv_b, gconv_w]
    n_small = sum(a.size for a in small)
    rows = -(-n_small // 1024) * 8
    gathered = _allgather_vmem(_pack_rows(small, rows), "allgather_small").reshape(N_DEV, rows, 128)
    summed = _sum_devices(gathered).reshape(-1)
    (g_b_ada, g_norm1, g_norm2, g_lb, g_hg, g_q, g_k, g_conv_b, g_conv_w_full) = _unpack(summed, small_shapes)
    g_conv_w = lax.dynamic_slice_in_dim(g_conv_w_full, me * FF_BLK, FF_BLK, axis=1)

    gmod_all = gathered[:, :6 * D_MODEL // 128, :].reshape(N_DEV, 6 * D_MODEL)
    gmod_cols = lax.dynamic_slice_in_dim(gmod_all, me * ADA_BLK, ADA_BLK, axis=1)
    g_w_ada_raw = _ada_wgrad(c_all, gmod_cols)

    chip = jnp.reshape(my_chip, (1,)).astype(jnp.int32)

    def big_update(name, w, m, v, rs, tr):
        chip_sums, received = rs
        return _adamw_reduced(name, w[0], m[0], v[0], chip_sums, received, chip, tr)

    r_in = big_update("adamw_w_in", w_in, m_w_in, v_w_in, rs_in, 256)
    r_out = big_update("adamw_w_out", w_out, m_w_out, v_w_out, rs_out, 128)
    r_up = big_update("adamw_w_up", w_up, m_w_up, v_w_up, rs_up, 256)
    r_down = big_update("adamw_w_down", w_down, m_w_down, v_w_down, rs_down, 176)
    r_ada = _adamw("adamw_w_ada", w_ada[0], m_w_ada[0], v_w_ada[0], [g_w_ada_raw], tr=256)
    r_convw = _adamw("adamw_conv_w", conv_w[0], m_conv_w[0], v_conv_w[0], [g_conv_w])

    rep_shapes = [(1, 6 * D_MODEL), (1, D_MODEL), (1, D_MODEL), (2, HEADS * HEAD_DIM), (1, HEAD_DIM),
                  (1, HEAD_DIM), (1, HEAD_DIM), (1, D_FF)]
    rep_rows = -(-sum(a * b for a, b in rep_shapes) // 1024) * 8
    pack = lambda arrs: _pack_rows(arrs, rep_rows)
    rep = _adamw("adamw_small",
                 pack([b_ada, norm1_w, norm2_w, lb_logits, hg_norm_w, q_norm_w, k_norm_w, conv_b]),
                 pack([m_b_ada, m_norm1_w, m_norm2_w, m_lb_logits, m_hg_norm_w, m_q_norm_w, m_k_norm_w, m_conv_b]),
                 pack([v_b_ada, v_norm1_w, v_norm2_w, v_lb_logits, v_hg_norm_w, v_q_norm_w, v_k_norm_w, v_conv_b]),
                 [pack([g_b_ada, g_norm1, g_norm2, g_lb, g_hg, g_q, g_k, g_conv_b])])
    rep = [_unpack(r.reshape(-1), rep_shapes) for r in rep]

    def big(r):
        return [a[None] for a in r]

    order = {"w_ada": big(r_ada), "b_ada": [r[0] for r in rep], "norm1_w": [r[1] for r in rep],
             "w_in": big(r_in), "lb_logits": [r[3] for r in rep], "hg_norm_w": [r[4] for r in rep],
             "q_norm_w": [r[5] for r in rep], "k_norm_w": [r[6] for r in rep], "w_out": big(r_out),
             "norm2_w": [r[2] for r in rep], "w_up": big(r_up), "conv_w": big(r_convw),
             "conv_b": [r[7] for r in rep], "w_down": big(r_down)}
    names = ["w_ada", "b_ada", "norm1_w", "w_in", "lb_logits", "hg_norm_w", "q_norm_w", "k_norm_w", "w_out",
             "norm2_w", "w_up", "conv_w", "conv_b", "w_down"]
    outs = [loss, grad_x[None]]
    for kind in range(4):
        outs += [order[n][kind] for n in names]
    return tuple(outs)
```

```python
import functools

import jax
import jax.numpy as jnp
from jax import lax
from jax.experimental import pallas as pl
from jax.experimental.pallas import tpu as pltpu

F32 = jnp.float32
BF16 = jnp.bfloat16

N_DEV = 8
SEQ = 2048
D_MODEL = 2048
HEADS = 8
HEAD_DIM = 128
IN_COLS = 7168
IN_BLK = IN_COLS // N_DEV
D_FF = 5632
UP_BLK = 2 * D_FF // N_DEV
FF_BLK = D_FF // N_DEV
ADA_BLK = 6 * D_MODEL // N_DEV
OUT_BLK = D_MODEL // N_DEV
EPS = 1e-6
CHUNK = 16
ROW_TILE = 256
V7X_VMEM_LIMIT = 56 * 1024 * 1024

ADAM_LR = 0.001
ADAM_B1 = 0.9
ADAM_B2 = 0.999
ADAM_EPS = 1e-08
ADAM_WD = 0.01
ADAM_STEP = 10

NN = (((1,), (0,)), ((), ()))
NT = (((1,), (1,)), ((), ()))
TN = (((0,), (0,)), ((), ()))
MESH = pl.DeviceIdType.MESH
AXES = ("x", "y", "c")


def _params(sem=None, vmem=V7X_VMEM_LIMIT):
    return pltpu.CompilerParams(dimension_semantics=sem, vmem_limit_bytes=vmem)


def _sigmoid(x):
    return 1.0 / (1.0 + jnp.exp(-x))


def _dsilu(x, s):
    return s * (1.0 + x * (1.0 - s))


def _lane_sum(x, ones_bf16):
    return jnp.dot(x.astype(BF16), ones_bf16, preferred_element_type=F32)


def _mesh_pos():
    return lax.axis_index("x"), lax.axis_index("y"), lax.axis_index("c")


def _allgather_vmem(x_blk, name):
    m_per, n = x_blk.shape

    def body(x_ref, out_ref, send_sems, recv_sems, local_sem):
        x, y, c = _mesh_pos()
        me, sibling = (x, y, c), (x, y, 1 - c)
        chips = [(1 - x, y), (x, 1 - y), (1 - x, 1 - y)]

        def rows(px, py, pc):
            return out_ref.at[pl.ds((4 * px + 2 * py + pc) * m_per, m_per), :]

        def copy(k, block, to, src=None):
            return pltpu.make_async_remote_copy(
                src_ref=rows(*block) if src is None else src, dst_ref=rows(*block),
                send_sem=send_sems.at[k], recv_sem=recv_sems.at[k], device_id=to, device_id_type=MESH)

        mine = pltpu.make_async_copy(x_ref, rows(*me), local_sem)
        mine.start()
        first = [copy(0, me, sibling, src=x_ref)]
        first += [copy(1 + j, me, (*chip, c), src=x_ref) for j, chip in enumerate(chips)]
        for cp in first:
            cp.start()
        passed = [copy(4 + j, (*chip, c), sibling) for j, chip in enumerate(chips)]
        for j, chip in enumerate(chips):
            copy(1 + j, (*chip, c), me).wait_recv()
            passed[j].start()
        copy(0, sibling, me).wait_recv()
        for j, chip in enumerate(chips):
            copy(4 + j, (*chip, 1 - c), me).wait_recv()
        for cp in first + passed:
            cp.wait_send()
        mine.wait()

    return pl.pallas_call(
        body, name=name,
        out_shape=jax.ShapeDtypeStruct((N_DEV * m_per, n), x_blk.dtype),
        in_specs=[pl.BlockSpec(memory_space=pltpu.VMEM)],
        out_specs=pl.BlockSpec(memory_space=pltpu.VMEM),
        scratch_shapes=[pltpu.SemaphoreType.DMA((7,)), pltpu.SemaphoreType.DMA((7,)), pltpu.SemaphoreType.DMA],
    )(x_blk)


def _flip(v, bit):
    return v + bit - 2 * v * bit


def _relay_chips(x, y, c):
    return (_flip(x, 1 - c), _flip(y, c)), (_flip(x, c), _flip(y, 1 - c))


UP_HEAD_ROWS = 768
GATHER_PARTS = 4


def _allgather_weights(blocks):
    n_arr = len(blocks)
    parts = GATHER_PARTS

    def body(*refs):
        ins, outs = refs[:n_arr], refs[n_arr:2 * n_arr]
        send_sems, recv_sems, local_sems = refs[2 * n_arr:]
        x, y, c = _mesh_pos()
        me, sibling = (x, y, c), (x, y, 1 - c)
        near = [(1 - x, y), (x, 1 - y)]
        chips = near + [(1 - x, 1 - y)]
        relay_from, relay_to = _relay_chips(x, y, c)

        def rows(a, p):
            hr = ins[a].shape[0] // parts
            return pl.ds(p * hr, hr)

        def slot(a, pos, p):
            return outs[a].at[4 * pos[0] + 2 * pos[1] + pos[2], rows(a, p)]

        def copy(a, k, p, src, lands, to):
            return pltpu.make_async_remote_copy(
                src_ref=src, dst_ref=slot(a, lands, p), send_sem=send_sems.at[a, k, p], recv_sem=recv_sems.at[a, k, p],
                device_id=to, device_id_type=MESH)

        sent = []
        local = [pltpu.make_async_copy(ins[a], outs[a].at[4 * x + 2 * y + c], local_sems.at[a]) for a in range(n_arr)]
        for cp in local:
            cp.start()
        for p in range(parts):
            for a in range(n_arr):
                own = ins[a].at[rows(a, p)]
                sent.append(copy(a, 0, p, own, me, sibling))
                sent += [copy(a, 1 + j, p, own, me, (*chip, c)) for j, chip in enumerate(near)]
        for cp in sent:
            cp.start()

        def start(cp):
            cp.start()
            sent.append(cp)

        for p in range(parts):
            for a in range(n_arr):
                for j, chip in enumerate(near):
                    copy(a, 1 + j, p, ins[a].at[rows(a, p)], (*chip, c), me).wait_recv()
                    start(copy(a, 4 + j, p, slot(a, (*chip, c), p), (*chip, c), sibling))
                start(copy(a, 3, p, slot(a, (*relay_from, c), p), (*relay_from, c), (*relay_to, c)))
        for p in range(parts):
            for a in range(n_arr):
                copy(a, 3, p, ins[a].at[rows(a, p)], (*chips[2], c), me).wait_recv()
                start(copy(a, 6, p, slot(a, (*chips[2], c), p), (*chips[2], c), sibling))
        for p in range(parts):
            for a in range(n_arr):
                copy(a, 0, p, ins[a].at[rows(a, p)], sibling, me).wait_recv()
                for j, chip in enumerate(chips):
                    copy(a, 4 + j, p, ins[a].at[rows(a, p)], (*chip, 1 - c), me).wait_recv()
        for cp in sent:
            cp.wait_send()
        for cp in local:
            cp.wait()

    return pl.pallas_call(
        body, name="allgather_weights",
        out_shape=[jax.ShapeDtypeStruct((N_DEV,) + b.shape, b.dtype) for b in blocks],
        in_specs=[pl.BlockSpec(memory_space=pltpu.HBM)] * n_arr, out_specs=[pl.BlockSpec(memory_space=pltpu.HBM)] * n_arr,
        scratch_shapes=[pltpu.SemaphoreType.DMA((n_arr, 7, parts)), pltpu.SemaphoreType.DMA((n_arr, 7, parts)),
                        pltpu.SemaphoreType.DMA((n_arr,))],
    )(*blocks)


HBM_SPEC = pl.BlockSpec(memory_space=pltpu.HBM)


class _FusedCopies:
    def __init__(self, kind, arrays, peers=(0, 1, 2, 3), rows=None, relay_rows=None):
        self.kind = kind
        self.peers = peers
        self.rows = rows
        self.relay_rows = relay_rows
        n = len(arrays) // 2 if kind == "gather_more" else len(arrays)
        self.n = n
        self.n_in = len(arrays)
        self.aliases = {}
        if kind == "gather":
            self.out_shape = [jax.ShapeDtypeStruct((N_DEV,) + a.shape, a.dtype) for a in arrays]
            self.scratch_shapes = [pltpu.SemaphoreType.DMA((n, 4, GATHER_PARTS)),
                                   pltpu.SemaphoreType.DMA((n, 4, GATHER_PARTS)), pltpu.SemaphoreType.DMA((n,))]
        elif kind == "gather_more":
            self.out_shape = [jax.ShapeDtypeStruct(a.shape, a.dtype) for a in arrays[n:]]
            self.scratch_shapes = [pltpu.SemaphoreType.DMA((n, 5, GATHER_PARTS)),
                                   pltpu.SemaphoreType.DMA((n, 5, GATHER_PARTS)), pltpu.SemaphoreType.DMA((n,))]
            self.aliases = {n + a: a for a in range(n)}
        elif kind == "relay":
            self.out_shape = [jax.ShapeDtypeStruct(a.shape, a.dtype) for a in arrays]
            self.scratch_shapes = [pltpu.SemaphoreType.DMA((n,)), pltpu.SemaphoreType.DMA((n,))]
            self.aliases = {a: a for a in range(n)}
        elif kind == "forward":
            self.out_shape = [jax.ShapeDtypeStruct(a.shape, a.dtype) for a in arrays]
            self.scratch_shapes = [pltpu.SemaphoreType.DMA((n, 3)), pltpu.SemaphoreType.DMA((n, 3))]
            self.aliases = {a: a for a in range(n)}
        elif kind == "sibling":
            self.out_shape = [jax.ShapeDtypeStruct((4,) + a.shape[1:], a.dtype) for a in arrays]
            self.scratch_shapes = [pltpu.SemaphoreType.DMA((n, 4)), pltpu.SemaphoreType.DMA((n, 4))]
        else:
            self.out_shape = [jax.ShapeDtypeStruct((3,) + a.shape[1:], a.dtype) for a in arrays]
            self.scratch_shapes = [pltpu.SemaphoreType.DMA((n, 3)), pltpu.SemaphoreType.DMA((n, 3))]
        self.in_specs = [HBM_SPEC] * self.n_in
        self.out_specs = [HBM_SPEC] * n
        self.n_scratch = len(self.scratch_shapes)

    def copies(self, ins, outs, sems):
        x, y, c = _mesh_pos()
        chips = [(1 - x, y), (x, 1 - y), (1 - x, 1 - y)]
        sibling = (x, y, 1 - c)
        starts, waits = [], []
        relay_from, relay_to = _relay_chips(x, y, c)

        def relayed(a, buf, lands, send_sem, recv_sem, rows):
            first, count = rows or (0, buf.shape[1])
            span = pl.ds(first, count)
            return pltpu.make_async_remote_copy(
                src_ref=buf.at[4 * relay_from[0] + 2 * relay_from[1] + c, span],
                dst_ref=outs[a].at[4 * lands[0] + 2 * lands[1] + c, span], send_sem=send_sem, recv_sem=recv_sem,
                device_id=(*relay_to, c), device_id_type=MESH)

        if self.kind in ("gather", "gather_more"):
            send_sems, recv_sems, local_sems = sems
            me = (x, y, c)
            peers = [sibling] + [(px, py, c) for px, py in chips]

            def slot(a, pos):
                return outs[a].at[4 * pos[0] + 2 * pos[1] + pos[2]]

            def span(a, p=None):
                first, count = self.rows or (0, ins[a].shape[0])
                if p is None:
                    return pl.ds(first, count)
                return pl.ds(first + p * (count // GATHER_PARTS), count // GATHER_PARTS)

            def remote(a, k, p, lands_from):
                return pltpu.make_async_remote_copy(
                    src_ref=ins[a].at[span(a, p)], dst_ref=slot(a, lands_from).at[span(a, p)],
                    send_sem=send_sems.at[a, k, p], recv_sem=recv_sems.at[a, k, p], device_id=peers[k],
                    device_id_type=MESH)

            for a in range(self.n):
                local = pltpu.make_async_copy(ins[a].at[span(a)], slot(a, me).at[span(a)], local_sems.at[a])
                starts.append(local)
                waits.append(local)
            for p in range(GATHER_PARTS):
                for a in range(self.n):
                    for k in self.peers:
                        starts.append(remote(a, k, p, me))
                        waits.append(remote(a, k, p, peers[k]))
            if self.kind == "gather_more" and self.relay_rows is not None:
                for a in range(self.n):
                    buf = ins[self.n + a]
                    starts.append(relayed(a, buf, relay_from, send_sems.at[a, 4, 0], recv_sems.at[a, 4, 0],
                                          self.relay_rows))
                    waits.append(relayed(a, buf, chips[2], send_sems.at[a, 4, 0], recv_sems.at[a, 4, 0],
                                         self.relay_rows))
        elif self.kind == "relay":
            send_sems, recv_sems = sems
            for a in range(self.n):
                starts.append(relayed(a, ins[a], relay_from, send_sems.at[a], recv_sems.at[a], self.rows))
                waits.append(relayed(a, ins[a], chips[2], send_sems.at[a], recv_sems.at[a], self.rows))
        elif self.kind == "forward":
            send_sems, recv_sems = sems

            def passed_on(a, j, pc_src, pc_dst):
                px, py = chips[j]
                return pltpu.make_async_remote_copy(
                    src_ref=ins[a].at[4 * px + 2 * py + pc_src], dst_ref=outs[a].at[4 * px + 2 * py + pc_dst],
                    send_sem=send_sems.at[a, j], recv_sem=recv_sems.at[a, j], device_id=sibling, device_id_type=MESH)

            for a in range(self.n):
                for j in range(3):
                    starts.append(passed_on(a, j, c, c))
                    waits.append(passed_on(a, j, c, 1 - c))
        elif self.kind == "sibling":
            send_sems, recv_sems = sems
            for a in range(self.n):
                for q in range(4):
                    cp = pltpu.make_async_remote_copy(
                        src_ref=ins[a].at[2 * q + 1 - c], dst_ref=outs[a].at[q], send_sem=send_sems.at[a, q],
                        recv_sem=recv_sems.at[a, q], device_id=sibling, device_id_type=MESH)
                    starts.append(cp)
                    waits.append(cp)
        else:
            send_sems, recv_sems = sems
            for a in range(self.n):
                for j, (px, py) in enumerate(chips):
                    cp = pltpu.make_async_remote_copy(
                        src_ref=ins[a].at[2 * px + py], dst_ref=outs[a].at[j], send_sem=send_sems.at[a, j],
                        recv_sem=recv_sems.at[a, j], device_id=(px, py, c), device_id_type=MESH)
                    starts.append(cp)
                    waits.append(cp)
        return starts, waits


def _fused_groups(fused):
    if fused is None:
        return []
    return list(fused) if isinstance(fused, (list, tuple)) else [fused]


def _host_body(body, n_in, n_out, fused, first_last):
    groups = _fused_groups(fused)
    if not groups:
        return body
    n_fin, n_fout = sum(g.n_in for g in groups), sum(g.n for g in groups)
    n_fsem = sum(g.n_scratch for g in groups)

    def wrapped(*refs):
        core_in, f_in = refs[:n_in], refs[n_in:n_in + n_fin]
        core_out = refs[n_in + n_fin:n_in + n_fin + n_out]
        f_out = refs[n_in + n_fin + n_out:n_in + n_fin + n_out + n_fout]
        rest = refs[n_in + n_fin + n_out + n_fout:]
        core_scratch, f_sems = rest[:len(rest) - n_fsem], rest[len(rest) - n_fsem:]
        starts, waits = [], []
        for g in groups:
            s, w = g.copies(f_in[:g.n_in], f_out[:g.n], f_sems[:g.n_scratch])
            f_in, f_out, f_sems = f_in[g.n_in:], f_out[g.n:], f_sems[g.n_scratch:]
            starts += s
            waits += w
        first, last = first_last()

        @pl.when(first)
        def _():
            for cp in starts:
                cp.start()

        body(*core_in, *core_out, *core_scratch)

        @pl.when(last)
        def _():
            for cp in waits:
                cp.wait()

    return wrapped


def _host_call(body, n_in, n_out, fused, first_last, *, name, grid, in_specs, out_specs, out_shape, scratch_shapes,
               sem, operands):
    aliases = {}
    in_specs, out_specs, out_shape, scratch_shapes = list(in_specs), list(out_specs), list(out_shape), list(scratch_shapes)
    fin, fout = n_in, n_out
    for g in _fused_groups(fused):
        aliases.update({fin + fi: fout + fo for fi, fo in g.aliases.items()})
        fin, fout = fin + g.n_in, fout + g.n
        in_specs += g.in_specs
        out_specs += g.out_specs
        out_shape += g.out_shape
        scratch_shapes += g.scratch_shapes
        sem = tuple("arbitrary" for _ in sem)
    res = pl.pallas_call(_host_body(body, n_in, n_out, fused, first_last), name=name, grid=grid, in_specs=in_specs,
                         out_specs=out_specs, out_shape=out_shape, scratch_shapes=scratch_shapes,
                         input_output_aliases=aliases, compiler_params=_params(sem))(*operands)
    return list(res[:n_out]), list(res[n_out:])


def _forward_to_sibling(name, gathered):
    n_arr = len(gathered)

    def body(*refs):
        ins, outs = refs[:n_arr], refs[n_arr:2 * n_arr]
        send_sems, recv_sems = refs[2 * n_arr:]
        x, y, c = _mesh_pos()
        chips = [(1 - x, y), (x, 1 - y), (1 - x, 1 - y)]

        def copy(a, j, pc):
            px, py = chips[j]
            s = 4 * px + 2 * py + pc
            return pltpu.make_async_remote_copy(
                src_ref=ins[a].at[s], dst_ref=outs[a].at[s], send_sem=send_sems.at[a, j], recv_sem=recv_sems.at[a, j],
                device_id=(x, y, 1 - c), device_id_type=MESH)

        for a in range(n_arr):
            for j in range(3):
                copy(a, j, c).start()
        for a in range(n_arr):
            for j in range(3):
                copy(a, j, 1 - c).wait_recv()
                copy(a, j, c).wait_send()

    return pl.pallas_call(
        body, name=name,
        out_shape=[jax.ShapeDtypeStruct(g.shape, g.dtype) for g in gathered],
        in_specs=[HBM_SPEC] * n_arr, out_specs=[HBM_SPEC] * n_arr,
        input_output_aliases={a: a for a in range(n_arr)},
        scratch_shapes=[pltpu.SemaphoreType.DMA((n_arr, 3)), pltpu.SemaphoreType.DMA((n_arr, 3))],
    )(*gathered)


def _exchange_sibling(name, partials):
    n_arr = len(partials)

    def body(*refs):
        ins, outs = refs[:n_arr], refs[n_arr:2 * n_arr]
        send_sems, recv_sems = refs[2 * n_arr:]
        x, y, c = _mesh_pos()
        copies = [pltpu.make_async_remote_copy(
            src_ref=ins[a].at[2 * q + 1 - c], dst_ref=outs[a].at[q], send_sem=send_sems.at[a, q],
            recv_sem=recv_sems.at[a, q], device_id=(x, y, 1 - c), device_id_type=MESH)
            for a in range(n_arr) for q in range(4)]
        for cp in copies:
            cp.start()
        for cp in copies:
            cp.wait_recv()
        for cp in copies:
            cp.wait_send()

    return pl.pallas_call(
        body, name=name,
        out_shape=[jax.ShapeDtypeStruct((4,) + p.shape[1:], p.dtype) for p in partials],
        in_specs=[HBM_SPEC] * n_arr, out_specs=[HBM_SPEC] * n_arr,
        scratch_shapes=[pltpu.SemaphoreType.DMA((n_arr, 4)), pltpu.SemaphoreType.DMA((n_arr, 4))],
    )(*partials)


def _matmul(name, a, b, dims, grid, a_spec, b_spec, o_spec, out_shape, acc_axis=None, fused=None, fused_arrays=()):
    def body(a_ref, b_ref, o_ref):
        r = lax.dot_general(a_ref[...], b_ref[...], dims, preferred_element_type=F32)
        if acc_axis is None:
            o_ref[...] = r.astype(o_ref.dtype)
        else:
            k = pl.program_id(acc_axis)

            @pl.when(k == 0)
            def _():
                o_ref[...] = r

            @pl.when(k > 0)
            def _():
                o_ref[...] += r

    sem = tuple("arbitrary" if i == acc_axis else "parallel" for i in range(len(grid)))
    if fused is None:
        return pl.pallas_call(body, name=name, grid=grid, in_specs=[a_spec, b_spec], out_specs=o_spec,
                              out_shape=out_shape, compiler_params=_params(sem))(a, b)

    def first_last():
        first = last = None
        for ax, n in enumerate(grid):
            f, l = pl.program_id(ax) == 0, pl.program_id(ax) == n - 1
            first, last = (f, l) if first is None else (first & f, last & l)
        return first, last

    (out,), extra = _host_call(body, 2, 1, fused, first_last, name=name, grid=grid, in_specs=[a_spec, b_spec],
                               out_specs=[o_spec], out_shape=[out_shape], scratch_shapes=[], sem=sem,
                               operands=[a, b] + list(fused_arrays))
    return out, extra


def _mm_blocked_rhs(name, a, w_g, tm=512, fused=None, fused_arrays=()):
    m, k = a.shape
    nb = w_g.shape[2]
    return _matmul(name, a, w_g, NN, (N_DEV, m // tm),
                   pl.BlockSpec((tm, k), lambda j, i: (i, 0)),
                   pl.BlockSpec((None, k, nb), lambda j, i: (j, 0, 0)),
                   pl.BlockSpec((tm, nb), lambda j, i: (i, j)),
                   jax.ShapeDtypeStruct((m, N_DEV * nb), F32), fused=fused, fused_arrays=fused_arrays)


def _mm_blocked_rhs_t(name, a, w_g, tm=512, fused=None, fused_arrays=()):
    m = a.shape[0]
    n, nb = w_g.shape[1], w_g.shape[2]
    return _matmul(name, a, w_g, NT, (m // tm, N_DEV),
                   pl.BlockSpec((tm, nb), lambda i, j: (i, j)),
                   pl.BlockSpec((None, n, nb), lambda i, j: (j, 0, 0)),
                   pl.BlockSpec((tm, n), lambda i, j: (i, 0)),
                   jax.ShapeDtypeStruct((m, n), F32), acc_axis=1, fused=fused, fused_arrays=fused_arrays)


def _mm_wgrad_blocked(name, act, dcols, tk=512, fused=None, fused_arrays=()):
    t, k = act.shape
    nb = dcols.shape[1] // N_DEV
    return _matmul(name, act, dcols, TN, (N_DEV, k // tk),
                   pl.BlockSpec((t, tk), lambda j, i: (0, i)),
                   pl.BlockSpec((t, nb), lambda j, i: (0, j)),
                   pl.BlockSpec((None, tk, nb), lambda j, i: (j, i, 0)),
                   jax.ShapeDtypeStruct((N_DEV, k, nb), BF16), fused=fused, fused_arrays=fused_arrays)


def _halves_specs(block, index):
    half = N_DEV // 2
    return (pl.BlockSpec(block, lambda i, j: index(i, jnp.minimum(j, half - 1))),
            pl.BlockSpec(block, lambda i, j: index(i, jnp.maximum(j - half, 0))))


def _mm_halves_rhs_t(name, a_lo, a_hi, w_g, tm=512):
    m = a_lo.shape[0]
    n, nb = w_g.shape[1], w_g.shape[2]

    def body(lo_ref, hi_ref, b_ref, o_ref):
        j = pl.program_id(1)

        def accumulate(a_ref):
            r = lax.dot_general(a_ref[...], b_ref[...], NT, preferred_element_type=F32)

            @pl.when(j == 0)
            def _():
                o_ref[...] = r

            @pl.when(j > 0)
            def _():
                o_ref[...] += r

        pl.when(j < N_DEV // 2)(lambda: accumulate(lo_ref))
        pl.when(j >= N_DEV // 2)(lambda: accumulate(hi_ref))

    lo_spec, hi_spec = _halves_specs((tm, nb), lambda i, j: (i, j))
    return pl.pallas_call(
        body, name=name, grid=(m // tm, N_DEV),
        in_specs=[lo_spec, hi_spec, pl.BlockSpec((None, n, nb), lambda i, j: (j, 0, 0))],
        out_specs=pl.BlockSpec((tm, n), lambda i, j: (i, 0)), out_shape=jax.ShapeDtypeStruct((m, n), F32),
        compiler_params=_params(("parallel", "arbitrary")))(a_lo, a_hi, w_g)


def _mm_halves_wgrad(name, act, d_lo, d_hi, tk=512):
    t, k = act.shape
    nb = d_lo.shape[1] // (N_DEV // 2)

    def body(a_ref, lo_ref, hi_ref, o_ref):
        j = pl.program_id(0)

        def product(d_ref):
            o_ref[...] = lax.dot_general(a_ref[...], d_ref[...], TN, preferred_element_type=F32).astype(o_ref.dtype)

        pl.when(j < N_DEV // 2)(lambda: product(lo_ref))
        pl.when(j >= N_DEV // 2)(lambda: product(hi_ref))

    half = N_DEV // 2
    return pl.pallas_call(
        body, name=name, grid=(N_DEV, k // tk),
        in_specs=[pl.BlockSpec((t, tk), lambda j, i: (0, i)),
                  pl.BlockSpec((t, nb), lambda j, i: (0, jnp.minimum(j, half - 1))),
                  pl.BlockSpec((t, nb), lambda j, i: (0, jnp.maximum(j - half, 0)))],
        out_specs=pl.BlockSpec((None, tk, nb), lambda j, i: (j, i, 0)),
        out_shape=jax.ShapeDtypeStruct((N_DEV, k, nb), BF16),
        compiler_params=_params(("parallel", "parallel")))(act, d_lo, d_hi)


def _mm_plain(name, a, b, dims, tm, tn, out_dtype, fused=None, fused_arrays=()):
    if dims == NN:
        (m, k), n = a.shape, b.shape[1]
        a_spec = pl.BlockSpec((tm, k), lambda i, j: (i, 0))
        b_spec = pl.BlockSpec((k, tn), lambda i, j: (0, j))
    elif dims == NT:
        (m, k), n = a.shape, b.shape[0]
        a_spec = pl.BlockSpec((tm, k), lambda i, j: (i, 0))
        b_spec = pl.BlockSpec((tn, k), lambda i, j: (j, 0))
    else:
        (k, m), n = a.shape, b.shape[1]
        a_spec = pl.BlockSpec((k, tm), lambda i, j: (0, i))
        b_spec = pl.BlockSpec((k, tn), lambda i, j: (0, j))
    return _matmul(name, a, b, dims, (m // tm, n // tn), a_spec, b_spec,
                   pl.BlockSpec((tm, tn), lambda i, j: (i, j)), jax.ShapeDtypeStruct((m, n), out_dtype),
                   fused=fused, fused_arrays=fused_arrays)


def _ada_fwd(c_all, w_ada_blk, b_blk):
    def body(c_ref, w_ref, b_ref, o_ref):
        cv = c_ref[...]
        o_ref[...] = jnp.dot(cv * _sigmoid(cv), w_ref[...], preferred_element_type=F32) + b_ref[...]

    tn = 512
    return pl.pallas_call(
        body, name="ada_fwd", grid=(ADA_BLK // tn,),
        in_specs=[pl.BlockSpec((N_DEV, D_MODEL), lambda j: (0, 0)),
                  pl.BlockSpec((D_MODEL, tn), lambda j: (0, j)),
                  pl.BlockSpec((1, tn), lambda j: (0, j))],
        out_specs=pl.BlockSpec((N_DEV, tn), lambda j: (0, j)),
        out_shape=jax.ShapeDtypeStruct((N_DEV, ADA_BLK), F32),
        compiler_params=_params(("parallel",)))(c_all, w_ada_blk, b_blk)


def _ada_wgrad(c_all, gmod_cols):
    def body(c_ref, g_ref, o_ref):
        cv = c_ref[...]
        o_ref[...] = lax.dot_general(cv * _sigmoid(cv), g_ref[...], TN, preferred_element_type=F32)

    tk = 512
    return pl.pallas_call(
        body, name="ada_wgrad", grid=(D_MODEL // tk,),
        in_specs=[pl.BlockSpec((N_DEV, tk), lambda i: (0, i)),
                  pl.BlockSpec((N_DEV, ADA_BLK), lambda i: (0, 0))],
        out_specs=pl.BlockSpec((tk, ADA_BLK), lambda i: (i, 0)),
        out_shape=jax.ShapeDtypeStruct((D_MODEL, ADA_BLK), F32),
        compiler_params=_params(("parallel",)))(c_all, gmod_cols)


def _row_spec(cols=D_MODEL):
    return pl.BlockSpec((ROW_TILE, cols), lambda i: (i, 0))


def _vec_spec(cols=D_MODEL):
    return pl.BlockSpec((1, cols), lambda i: (0, 0))


def _norm_fwd(name, x, w, scale, shift, resid=None, gate=None):
    has_res = resid is not None

    def body(*refs):
        if has_res:
            x_ref, r_ref, g_ref, w_ref, sc_ref, sh_ref, xr_ref, h_ref, rs_ref = refs
            xr = x_ref[...] + g_ref[...] * r_ref[...]
            xr_ref[...] = xr
        else:
            x_ref, w_ref, sc_ref, sh_ref, h_ref, rs_ref = refs
            xr = x_ref[...]
        rs = lax.rsqrt(jnp.mean(xr * xr, axis=-1, keepdims=True) + EPS)
        h = (xr * rs) * w_ref[...] * (1.0 + sc_ref[...]) + sh_ref[...]
        h_ref[...] = h.astype(BF16)
        rs_ref[...] = rs

    s = x.shape[0]
    ins = [x] + ([resid, gate] if has_res else []) + [w, scale, shift]
    in_specs = [_row_spec()] + ([_row_spec(), _vec_spec()] if has_res else []) + [_vec_spec()] * 3
    outs = ([jax.ShapeDtypeStruct((s, D_MODEL), F32)] if has_res else []) + [
        jax.ShapeDtypeStruct((s, D_MODEL), BF16), jax.ShapeDtypeStruct((s, 1), F32)]
    out_specs = ([_row_spec()] if has_res else []) + [_row_spec(), pl.BlockSpec((ROW_TILE, 1), lambda i: (i, 0))]
    return pl.pallas_call(body, name=name, grid=(s // ROW_TILE,), in_specs=in_specs, out_specs=out_specs,
                          out_shape=outs, compiler_params=_params(("parallel",)))(*ins)


def _norm_bwd(name, dh, x, rstd, w, scale, dres, mix=None, gate=None, fused=None, fused_arrays=()):
    has_mix = mix is not None

    def body(*refs):
        if has_mix:
            (dh_ref, x_ref, rs_ref, w_ref, sc_ref, dr_ref, mix_ref, g_ref,
             dx_ref, dmix_ref, dsh_ref, dsc_ref, dw_ref, dg_ref) = refs
        else:
            dh_ref, x_ref, rs_ref, w_ref, sc_ref, dr_ref, dx_ref, dsh_ref, dsc_ref, dw_ref = refs
        i = pl.program_id(0)
        dhv = dh_ref[...]
        rs = rs_ref[...]
        xn = x_ref[...] * rs
        wv = w_ref[...]
        one_sc = 1.0 + sc_ref[...]
        dxn = dhv * wv * one_sc
        dx = dr_ref[...] + rs * (dxn - xn * jnp.mean(dxn * xn, axis=-1, keepdims=True))
        dx_ref[...] = dx
        sums = [(dsh_ref, dhv), (dsc_ref, dhv * xn * wv), (dw_ref, dhv * one_sc * xn)]
        if has_mix:
            dmix_ref[...] = (dx * g_ref[...]).astype(BF16)
            sums.append((dg_ref, dx * mix_ref[...]))

        @pl.when(i == 0)
        def _():
            for ref, _v in sums:
                ref[...] = jnp.zeros_like(ref)

        for ref, v in sums:
            ref[...] += jnp.sum(v, axis=0, keepdims=True)

    s = x.shape[0]
    ins = [dh, x, rstd, w, scale, dres] + ([mix, gate] if has_mix else [])
    in_specs = ([_row_spec(), _row_spec(), pl.BlockSpec((ROW_TILE, 1), lambda i: (i, 0)), _vec_spec(), _vec_spec(),
                 _row_spec()] + ([_row_spec(), _vec_spec()] if has_mix else []))
    vec = jax.ShapeDtypeStruct((1, D_MODEL), F32)
    outs = ([jax.ShapeDtypeStruct((s, D_MODEL), F32)] + ([jax.ShapeDtypeStruct((s, D_MODEL), BF16)] if has_mix else [])
            + [vec] * (4 if has_mix else 3))
    out_specs = [_row_spec()] + ([_row_spec()] if has_mix else []) + [_vec_spec()] * (4 if has_mix else 3)

    def first_last():
        i = pl.program_id(0)
        return i == 0, i == s // ROW_TILE - 1

    res, extra = _host_call(body, len(ins), len(outs), fused, first_last, name=name, grid=(s // ROW_TILE,),
                            in_specs=in_specs, out_specs=out_specs, out_shape=outs, scratch_shapes=[],
                            sem=("arbitrary",), operands=ins + list(fused_arrays))
    return res if fused is None else (res, extra)


def _loss_head(x1, ffn, gate2, target):
    def body(x_ref, f_ref, g_ref, t_ref, loss_ref, dout_ref, dffn_ref, dg_ref):
        i = pl.program_id(0)
        fv = f_ref[...]
        gv = g_ref[...]
        err = x_ref[...] + gv * fv - t_ref[...]
        dout = err * (1.0 / D_MODEL)
        dout_ref[...] = dout
        dffn_ref[...] = (dout * gv).astype(BF16)

        @pl.when(i == 0)
        def _():
            loss_ref[...] = jnp.zeros_like(loss_ref)
            dg_ref[...] = jnp.zeros_like(dg_ref)

        row = jnp.sum(err * err, axis=-1, keepdims=True) * (1.0 / D_MODEL)
        loss_ref[...] += jnp.broadcast_to(0.5 * jnp.sum(row, axis=0, keepdims=True), (1, 128))
        dg_ref[...] += jnp.sum(dout * fv, axis=0, keepdims=True)

    s = x1.shape[0]
    return pl.pallas_call(
        body, name="loss_head", grid=(s // ROW_TILE,),
        in_specs=[_row_spec(), _row_spec(), _vec_spec(), _row_spec()],
        out_specs=[pl.BlockSpec((1, 128), lambda i: (0, 0)), _row_spec(), _row_spec(), _vec_spec()],
        out_shape=[jax.ShapeDtypeStruct((1, 128), F32), jax.ShapeDtypeStruct((s, D_MODEL), F32),
                   jax.ShapeDtypeStruct((s, D_MODEL), BF16), jax.ShapeDtypeStruct((1, D_MODEL), F32)],
        compiler_params=_params(("arbitrary",)))(x1, ffn, gate2, target)


CONV_TILE = 512
N_CONV_TILES = D_FF // CONV_TILE


def _shift_rows(a, k, row):
    n = a.shape[0]
    if k > 0:
        return jnp.where(row >= k, pltpu.roll(a, k, 0), 0.0)
    return jnp.where(row < n + k, pltpu.roll(a, n + k, 0), 0.0)


def _conv_gate_fwd(u, conv_w, conv_b, fused=None, fused_arrays=()):
    s = u.shape[0]

    def body(a_ref, g_ref, w_ref, b_ref, y_ref):
        a = a_ref[...]
        w = w_ref[...]
        row = lax.broadcasted_iota(jnp.int32, a.shape, 0)
        ac = b_ref[...] + _shift_rows(a, 2, row) * w[0:1] + _shift_rows(a, 1, row) * w[1:2] + a * w[2:3]
        y_ref[...] = (ac * _sigmoid(ac) * g_ref[...]).astype(BF16)

    def first_last():
        i = pl.program_id(0)
        return i == 0, i == N_CONV_TILES - 1

    col = lambda off: pl.BlockSpec((s, CONV_TILE), lambda i: (0, i + off))
    (y,), extra = _host_call(
        body, 4, 1, fused, first_last, name="conv_gate_fwd", grid=(N_CONV_TILES,),
        in_specs=[col(0), col(N_CONV_TILES), pl.BlockSpec((3, CONV_TILE), lambda i: (0, i)),
                  pl.BlockSpec((1, CONV_TILE), lambda i: (0, i))],
        out_specs=[col(0)], out_shape=[jax.ShapeDtypeStruct((s, D_FF), BF16)], scratch_shapes=[], sem=("parallel",),
        operands=[u, u, conv_w, conv_b] + list(fused_arrays))
    return y if fused is None else (y, extra)


def _conv_gate_bwd(u, dy, conv_w, conv_b):
    s = u.shape[0]

    def body(a_ref, g_ref, dy_ref, w_ref, b_ref, da_ref, dg_ref, gw_ref, gb_ref):
        a = a_ref[...]
        w = w_ref[...]
        row = lax.broadcasted_iota(jnp.int32, a.shape, 0)
        a1 = _shift_rows(a, 1, row)
        a2 = _shift_rows(a, 2, row)
        ac = b_ref[...] + a2 * w[0:1] + a1 * w[1:2] + a * w[2:3]
        sg = _sigmoid(ac)
        dyv = dy_ref[...].astype(F32)
        dg_ref[...] = (dyv * (ac * sg)).astype(BF16)
        dac = dyv * g_ref[...] * _dsilu(ac, sg)
        gb_ref[...] = jnp.sum(dac, axis=0, keepdims=True)
        gw_ref[0:1, :] = jnp.sum(dac * a2, axis=0, keepdims=True)
        gw_ref[1:2, :] = jnp.sum(dac * a1, axis=0, keepdims=True)
        gw_ref[2:3, :] = jnp.sum(dac * a, axis=0, keepdims=True)
        da = dac * w[2:3] + _shift_rows(dac, -1, row) * w[1:2] + _shift_rows(dac, -2, row) * w[0:1]
        da_ref[...] = da.astype(BF16)

    col = lambda off: pl.BlockSpec((s, CONV_TILE), lambda i: (0, i + off))
    return pl.pallas_call(
        body, name="conv_gate_bwd", grid=(N_CONV_TILES,),
        in_specs=[col(0), col(N_CONV_TILES), col(0), pl.BlockSpec((3, CONV_TILE), lambda i: (0, i)),
                  pl.BlockSpec((1, CONV_TILE), lambda i: (0, i))],
        out_specs=[col(0), col(0), pl.BlockSpec((3, CONV_TILE), lambda i: (0, i)),
                   pl.BlockSpec((1, CONV_TILE), lambda i: (0, i))],
        out_shape=[jax.ShapeDtypeStruct((s, D_FF), BF16), jax.ShapeDtypeStruct((s, D_FF), BF16),
                   jax.ShapeDtypeStruct((3, D_FF), F32), jax.ShapeDtypeStruct((1, D_FF), F32)],
        compiler_params=_params(("parallel",)))(u, u, dy, conv_w, conv_b)


HG_TILE = 256
CHUNK_UNROLL = 8


def _unrolled_loop(n, body, init):
    def group(i, carry):
        for u in range(CHUNK_UNROLL):
            carry = body(i * CHUNK_UNROLL + u, carry)
        return carry

    return lax.fori_loop(0, n // CHUNK_UNROLL, group, init)


def _head_col(off):
    return pl.BlockSpec((SEQ, HEAD_DIM), lambda h: (0, h + off))


def _hgrn_gates(hq, hf, lb, pos):
    q = hq * _sigmoid(hq)
    sig = _sigmoid(hf)
    f = lb + (1.0 - lb) * sig
    gl = jnp.log(f)
    for sh in (1, 2, 4, 8):
        gl = gl + jnp.where(pos >= sh, pltpu.roll(gl, sh, 0), 0.0)
    return q, sig, f, 1.0 - f, gl


def _lower_bound(lbl):
    return 1.0 / (1.0 + jnp.exp(lbl[1:2, :] - lbl[0:1, :]))


def _head_first_last():
    h = pl.program_id(0)
    return h == 0, h == HEADS - 1


CHUNKS_PER_TILE = HG_TILE // CHUNK


def _chunk_end(x, pos):
    y = jnp.where(pos == CHUNK - 1, x, 0.0)
    for sh in (1, 2, 4, 8):
        y = y + jnp.where(pos < CHUNK - sh, pltpu.roll(y, x.shape[0] - sh, 0), 0.0)
    return y


def _suffix_in_chunk(x, pos):
    for sh in (1, 2, 4, 8):
        x = x + jnp.where(pos < CHUNK - sh, pltpu.roll(x, x.shape[0] - sh, 0), 0.0)
    return x


def _prefix_in_chunk(x, pos):
    for sh in (1, 2, 4, 8):
        x = x + jnp.where(pos >= sh, pltpu.roll(x, sh, 0), 0.0)
    return x


def _pair_decays(f, pos):
    shifted = jnp.where(pos >= 1, f, 0.0)
    e = shifted
    yield 1, e
    for d in range(2, CHUNK):
        shifted = pltpu.roll(shifted, 1, 0)
        e = e * shifted
        yield d, e


def _chunk_rows(cc):
    return slice(cc * CHUNK, (cc + 1) * CHUNK)


def _outer_products(lhs_b, rhs_b, dst, i):
    for cc in range(CHUNKS_PER_TILE):
        dst[i * CHUNKS_PER_TILE + cc] = lax.dot_general(lhs_b[_chunk_rows(cc)], rhs_b[_chunk_rows(cc)], TN,
                                                        preferred_element_type=F32)


def _state_scan(n_chunks, gl_s, u_s, keep, reverse):
    def step(k, st):
        c = n_chunks - 1 - k if reverse else k
        keep[c] = st.astype(BF16)
        gl = gl_s[pl.ds(pl.multiple_of(c * CHUNK, CHUNK), CHUNK), :]
        return st * jnp.exp(gl[CHUNK - 1:CHUNK, :]) + u_s[c]

    _unrolled_loop(n_chunks, step, jnp.zeros((HEAD_DIM, HEAD_DIM), F32))


def _hgrn_fwd(proj, lb_logits, norm_w, fused=None, fused_arrays=()):
    n_tiles = SEQ // HG_TILE
    n_chunks = SEQ // CHUNK
    fused_arrays = list(fused_arrays)

    def body(hq_ref, hf_ref, hi_ref, hg_ref, lbl_ref, nw_ref, aout_ref, opre_ref, qt_s, gl_s, u_s, st_s):
        lb = _lower_bound(lbl_ref[...])
        ones = jnp.ones((HEAD_DIM, HEAD_DIM), BF16)
        pos = lax.broadcasted_iota(jnp.int32, (HG_TILE, HEAD_DIM), 0) % CHUNK

        def tile(i, carry):
            rows = pl.ds(pl.multiple_of(i * HG_TILE, HG_TILE), HG_TILE)
            v = hi_ref[rows, :]
            q, _sig, f, kk, gl = _hgrn_gates(hq_ref[rows, :], hf_ref[rows, :], lb, pos)
            o = _lane_sum(q * kk, ones) * v
            for d, e in _pair_decays(f, pos):
                o = o + _lane_sum(q * pltpu.roll(kk, d, 0) * e, ones) * pltpu.roll(v, d, 0)
            opre_ref[rows, :] = o
            qt_s[rows, :] = q * jnp.exp(gl)
            gl_s[rows, :] = gl
            kt = kk * jnp.exp(_chunk_end(gl, pos) - gl)
            _outer_products(v.astype(BF16), kt.astype(BF16), u_s, i)
            return carry

        lax.fori_loop(0, n_tiles, tile, 0)
        _state_scan(n_chunks, gl_s, u_s, st_s, reverse=False)

        def finish(i, carry):
            rows = pl.ds(pl.multiple_of(i * HG_TILE, HG_TILE), HG_TILE)
            qt_b = qt_s[rows, :].astype(BF16)
            past = [lax.dot_general(qt_b[_chunk_rows(cc)], st_s[i * CHUNKS_PER_TILE + cc], NT,
                                    preferred_element_type=F32) for cc in range(CHUNKS_PER_TILE)]
            o = opre_ref[rows, :] + jnp.concatenate(past, axis=0)
            opre_ref[rows, :] = o
            hg = hg_ref[rows, :]
            rs = lax.rsqrt(jnp.mean(o * o, axis=-1, keepdims=True) + EPS)
            aout_ref[rows, :] = ((o * rs) * nw_ref[...] * (hg * _sigmoid(hg))).astype(BF16)
            return carry

        lax.fori_loop(0, n_tiles, finish, 0)

    return _host_call(
        body, 6, 2, fused, _head_first_last, name="hgrn_fwd", grid=(HEADS,),
        in_specs=[_head_col(0), _head_col(HEADS), _head_col(2 * HEADS), _head_col(3 * HEADS),
                  pl.BlockSpec((2, HEAD_DIM), lambda h: (0, h)), pl.BlockSpec((1, HEAD_DIM), lambda h: (0, 0))],
        out_specs=[_head_col(0), _head_col(0)],
        out_shape=[jax.ShapeDtypeStruct((SEQ, HEADS * HEAD_DIM), BF16), jax.ShapeDtypeStruct((SEQ, HEADS * HEAD_DIM), F32)],
        scratch_shapes=[pltpu.VMEM((SEQ, HEAD_DIM), F32)] * 2 + [pltpu.VMEM((n_chunks, HEAD_DIM, HEAD_DIM), F32),
                                                                 pltpu.VMEM((n_chunks, HEAD_DIM, HEAD_DIM), BF16)],
        sem=("parallel",), operands=[proj, proj, proj, proj, lb_logits, norm_w] + fused_arrays)


def _hgrn_bwd(proj, lb_logits, norm_w, o_pre, d_aout, fused=None, fused_arrays=()):
    n_tiles = SEQ // HG_TILE
    n_chunks = SEQ // CHUNK

    def body(hq_ref, hf_ref, hi_ref, hg_ref, lbl_ref, nw_ref, opre_ref, da_ref,
             dhq_ref, dhf_ref, dhi_ref, dhg_ref, dlog_ref, gnw_ref,
             q_s, k_s, gl_s, do_s, dq_s, dk_s, dv_s, u_s, st_s, rt_s):
        h = pl.program_id(0)
        lb = _lower_bound(lbl_ref[...])
        nw = nw_ref[...]
        ones = jnp.ones((HEAD_DIM, HEAD_DIM), BF16)
        pos = lax.broadcasted_iota(jnp.int32, (HG_TILE, HEAD_DIM), 0) % CHUNK

        @pl.when(h == 0)
        def _():
            gnw_ref[...] = jnp.zeros_like(gnw_ref)

        def tile(i, carry):
            rows = pl.ds(pl.multiple_of(i * HG_TILE, HG_TILE), HG_TILE)
            v = hi_ref[rows, :]
            q, _sig, f, kk, gl = _hgrn_gates(hq_ref[rows, :], hf_ref[rows, :], lb, pos)
            o = opre_ref[rows, :]
            hg = hg_ref[rows, :]
            da = da_ref[rows, :]
            rs = lax.rsqrt(jnp.mean(o * o, axis=-1, keepdims=True) + EPS)
            oh = o * rs
            sg = _sigmoid(hg)
            dnorm = da * (hg * sg)
            dhg_ref[rows, :] = (da * (oh * nw) * _dsilu(hg, sg)).astype(BF16)
            gnw_ref[...] += jnp.sum(dnorm * oh, axis=0, keepdims=True)
            doh = dnorm * nw
            do = rs * (doh - oh * jnp.mean(doh * oh, axis=-1, keepdims=True))

            d_a = _lane_sum(do * v, ones)
            dq = d_a * kk
            dk = d_a * q
            dv = _lane_sum(q * kk, ones) * do
            for d, e in _pair_decays(f, pos):
                ks = pltpu.roll(kk, d, 0)
                a_d = _lane_sum(q * ks * e, ones)
                d_a = _lane_sum(do * pltpu.roll(v, d, 0), ones) * e
                dq = dq + d_a * ks
                dk = dk + pltpu.roll(d_a * q, HG_TILE - d, 0)
                dv = dv + pltpu.roll(a_d * do, HG_TILE - d, 0)
            q_s[rows, :] = q
            k_s[rows, :] = kk
            gl_s[rows, :] = gl
            do_s[rows, :] = do
            dq_s[rows, :] = dq
            dk_s[rows, :] = dk
            dv_s[rows, :] = dv
            kt = kk * jnp.exp(_chunk_end(gl, pos) - gl)
            _outer_products(v.astype(BF16), kt.astype(BF16), u_s, i)
            return carry

        lax.fori_loop(0, n_tiles, tile, 0)
        _state_scan(n_chunks, gl_s, u_s, st_s, reverse=False)

        def reverse_increments(i, carry):
            rows = pl.ds(pl.multiple_of(i * HG_TILE, HG_TILE), HG_TILE)
            qt = q_s[rows, :] * jnp.exp(gl_s[rows, :])
            _outer_products(do_s[rows, :].astype(BF16), qt.astype(BF16), u_s, i)
            return carry

        lax.fori_loop(0, n_tiles, reverse_increments, 0)
        _state_scan(n_chunks, gl_s, u_s, rt_s, reverse=True)

        def finish(i, dlb):
            rows = pl.ds(pl.multiple_of(i * HG_TILE, HG_TILE), HG_TILE)
            q = q_s[rows, :]
            kk = k_s[rows, :]
            gl = gl_s[rows, :]
            gll = _chunk_end(gl, pos)
            ekt = jnp.exp(gll - gl)
            do_b = do_s[rows, :].astype(BF16)
            v_b = hi_ref[rows, :].astype(BF16)
            kt_b = (kk * ekt).astype(BF16)
            dq_far, dk_far, dv_far, across = [], [], [], []
            for cc in range(CHUNKS_PER_TILE):
                st = st_s[i * CHUNKS_PER_TILE + cc]
                rt = rt_s[i * CHUNKS_PER_TILE + cc]
                sl = _chunk_rows(cc)
                dq_far.append(jnp.dot(do_b[sl], st, preferred_element_type=F32))
                dk_far.append(jnp.dot(v_b[sl], rt, preferred_element_type=F32))
                dv_far.append(lax.dot_general(kt_b[sl], rt, NT, preferred_element_type=F32))
                both = jnp.sum(st.astype(F32) * rt.astype(F32), axis=0, keepdims=True)
                across.append(jnp.broadcast_to(both, (CHUNK, HEAD_DIM)))
            dq = dq_s[rows, :] + jnp.concatenate(dq_far, axis=0) * jnp.exp(gl)
            dk_in = dk_s[rows, :]
            dk_out = jnp.concatenate(dk_far, axis=0) * ekt
            dk = dk_in + dk_out
            dv = dv_s[rows, :] + jnp.concatenate(dv_far, axis=0)
            pc = kk * dk_out
            dgl = (_suffix_in_chunk(q * dq - kk * dk_in, pos) + (_prefix_in_chunk(pc, pos) - pc)
                   + jnp.concatenate(across, axis=0) * jnp.exp(gll))
            hf = hf_ref[rows, :]
            sig = _sigmoid(hf)
            f = lb + (1.0 - lb) * sig
            df = dgl / f - dk
            dhf_ref[rows, :] = (df * (1.0 - lb) * sig * (1.0 - sig)).astype(BF16)
            hq = hq_ref[rows, :]
            dhq_ref[rows, :] = (dq * _dsilu(hq, _sigmoid(hq))).astype(BF16)
            dhi_ref[rows, :] = dv.astype(BF16)
            return dlb + jnp.sum(df * (1.0 - sig), axis=0, keepdims=True)

        dlb = lax.fori_loop(0, n_tiles, finish, jnp.zeros((1, HEAD_DIM), F32))
        dl0 = lb * (1.0 - lb) * dlb
        dlog_ref[0:1, :] = dl0
        dlog_ref[1:2, :] = -dl0

    wide = HEADS * HEAD_DIM
    return _host_call(
        body, 8, 6, fused, _head_first_last, name="hgrn_bwd", grid=(HEADS,),
        in_specs=[_head_col(0), _head_col(HEADS), _head_col(2 * HEADS), _head_col(3 * HEADS),
                  pl.BlockSpec((2, HEAD_DIM), lambda h: (0, h)), pl.BlockSpec((1, HEAD_DIM), lambda h: (0, 0)),
                  _head_col(0), _head_col(0)],
        out_specs=[_head_col(0)] * 4 + [pl.BlockSpec((2, HEAD_DIM), lambda h: (0, h)),
                                        pl.BlockSpec((1, HEAD_DIM), lambda h: (0, 0))],
        out_shape=[jax.ShapeDtypeStruct((SEQ, wide), BF16)] * 4 + [jax.ShapeDtypeStruct((2, wide), F32),
                                                                    jax.ShapeDtypeStruct((1, HEAD_DIM), F32)],
        scratch_shapes=[pltpu.VMEM((SEQ, HEAD_DIM), F32)] * 7 + [pltpu.VMEM((n_chunks, HEAD_DIM, HEAD_DIM), F32),
                                                                 pltpu.VMEM((n_chunks, HEAD_DIM, HEAD_DIM), BF16),
                                                                 pltpu.VMEM((n_chunks, HEAD_DIM, HEAD_DIM), BF16)],
        sem=("arbitrary",),
        operands=[proj, proj, proj, proj, lb_logits, norm_w, o_pre, d_aout] + list(fused_arrays))


Q_TILE = 512
ATT_SCALE = HEAD_DIM ** -0.5
ATT_OFF = 4 * HEADS


def _qk_prep(proj, q_w, k_w, fused=None, fused_arrays=()):
    def body(aq_ref, ak_ref, av_ref, qw_ref, kw_ref, qn_ref, kn_ref, v_ref):
        aq = aq_ref[...]
        ak = ak_ref[...]
        qn_ref[...] = (aq * lax.rsqrt(jnp.mean(aq * aq, axis=-1, keepdims=True) + EPS) * qw_ref[...]).astype(BF16)
        kn_ref[...] = (ak * lax.rsqrt(jnp.mean(ak * ak, axis=-1, keepdims=True) + EPS) * kw_ref[...]).astype(BF16)
        v_ref[...] = av_ref[...].astype(BF16)

    wide = HEADS * HEAD_DIM
    vec = pl.BlockSpec((1, HEAD_DIM), lambda h: (0, 0))
    return _host_call(
        body, 5, 3, fused, _head_first_last, name="qk_prep", grid=(HEADS,),
        in_specs=[_head_col(ATT_OFF), _head_col(ATT_OFF + HEADS), _head_col(ATT_OFF + 2 * HEADS), vec, vec],
        out_specs=[_head_col(0)] * 3, out_shape=[jax.ShapeDtypeStruct((SEQ, wide), BF16)] * 3,
        scratch_shapes=[], sem=("parallel",), operands=[proj, proj, proj, q_w, k_w] + list(fused_arrays))


def _alibi_slopes():
    slopes = jnp.exp2(-8.0 * jnp.arange(1, HEADS + 1, dtype=F32) / HEADS)
    return jnp.broadcast_to(slopes[:, None, None], (HEADS, 1, HEAD_DIM))


SLOPE_SPEC = pl.BlockSpec((None, 1, HEAD_DIM), lambda h, i: (h, 0, 0))


N_Q_TILES = SEQ // Q_TILE
K_BLOCK = 512
NOT_ATTENDED = 1e35


def _att_tables():
    o = jnp.arange(N_Q_TILES, dtype=jnp.int32)[:, None, None]
    r = jnp.arange(Q_TILE, dtype=jnp.int32)[None, :, None]
    c = jnp.arange(K_BLOCK, dtype=jnp.int32)[None, None, :]
    dist = o * Q_TILE + r - c
    mult = ((dist <= 128).astype(F32) + (((dist % 4) == 0) & (dist <= 512)).astype(F32)
            + ((dist % 16) == 0).astype(F32))
    valid = (dist >= 0) & (mult > 0)
    return (jnp.where(valid, dist.astype(F32), NOT_ATTENDED),
            jnp.where(valid, jnp.log(jnp.maximum(mult, 1.0)), 0.0))


TABLE_SPEC = pl.BlockSpec((N_Q_TILES, Q_TILE, K_BLOCK), lambda h, i: (0, 0, 0))


def _att_block(q, k_ref, j, i, slope, dist_ref, lmul_ref):
    rows = pl.ds(pl.multiple_of(j * K_BLOCK, K_BLOCK), K_BLOCK)
    off = i - j * (K_BLOCK // Q_TILE)
    s = lax.dot_general(q, k_ref[rows, :], NT, preferred_element_type=F32) * ATT_SCALE
    return s - slope * dist_ref[off] + lmul_ref[off], rows


def _n_key_blocks(i):
    return (i + K_BLOCK // Q_TILE) // (K_BLOCK // Q_TILE)


def _att_first_last():
    h, i = pl.program_id(0), pl.program_id(1)
    return (h == 0) & (i == 0), (h == HEADS - 1) & (i == N_Q_TILES - 1)


def _attn_fwd(qn, kn, vb, fused=None, fused_arrays=()):
    def body(q_ref, k_ref, v_ref, sl_ref, dist_ref, lmul_ref, o_ref, lse_ref):
        i = pl.program_id(1)
        q = q_ref[...]
        slope = sl_ref[0:1, 0:1]

        def step(j, carry):
            m, l, acc = carry
            sb, rows = _att_block(q, k_ref, j, i, slope, dist_ref, lmul_ref)
            m_new = jnp.maximum(m, jnp.max(sb, axis=-1, keepdims=True))
            alpha = jnp.exp(m - m_new)
            p = jnp.exp(sb - m_new)
            l = alpha * l + jnp.sum(p, axis=-1, keepdims=True)
            acc = alpha * acc + jnp.dot(p.astype(BF16), v_ref[rows, :], preferred_element_type=F32)
            return m_new, l, acc

        m, l, acc = lax.fori_loop(0, _n_key_blocks(i), step,
                                  (jnp.full((Q_TILE, 1), -1e30, F32), jnp.zeros((Q_TILE, 1), F32),
                                   jnp.zeros((Q_TILE, HEAD_DIM), F32)))
        o_ref[...] = acc / l
        lse_ref[...] = m + jnp.log(l)

    wide = HEADS * HEAD_DIM
    qt = pl.BlockSpec((Q_TILE, HEAD_DIM), lambda h, i: (i, h))
    full = pl.BlockSpec((SEQ, HEAD_DIM), lambda h, i: (0, h))
    return _host_call(
        body, 6, 2, fused, _att_first_last, name="attn_fwd", grid=(HEADS, N_Q_TILES),
        in_specs=[qt, full, full, SLOPE_SPEC, TABLE_SPEC, TABLE_SPEC],
        out_specs=[qt, pl.BlockSpec((None, Q_TILE, 1), lambda h, i: (h, i, 0))],
        out_shape=[jax.ShapeDtypeStruct((SEQ, wide), F32), jax.ShapeDtypeStruct((HEADS, SEQ, 1), F32)],
        scratch_shapes=[], sem=("parallel", "parallel"),
        operands=[qn, kn, vb, _alibi_slopes(), *_att_tables()] + list(fused_arrays))


def _attn_bwd(qn, kn, vb, o, lse, d_mix, fused=None, fused_arrays=()):
    def body(q_ref, k_ref, v_ref, o_ref, lse_ref, do_ref, sl_ref, dist_ref, lmul_ref, dq_ref, dk_ref, dv_ref):
        i = pl.program_id(1)
        q = q_ref[...]
        do = do_ref[...]
        do_b = do.astype(BF16)
        slope = sl_ref[0:1, 0:1]
        lse = lse_ref[...]
        delta = jnp.sum(do * o_ref[...], axis=-1, keepdims=True)

        @pl.when(i == 0)
        def _():
            dk_ref[...] = jnp.zeros_like(dk_ref)
            dv_ref[...] = jnp.zeros_like(dv_ref)

        def step(j, dq):
            sb, rows = _att_block(q, k_ref, j, i, slope, dist_ref, lmul_ref)
            p = jnp.exp(sb - lse)
            dp = lax.dot_general(do_b, v_ref[rows, :], NT, preferred_element_type=F32)
            ds = (p * (dp - delta)).astype(BF16)
            dk_ref[rows, :] += lax.dot_general(ds, q, TN, preferred_element_type=F32) * ATT_SCALE
            dv_ref[rows, :] += lax.dot_general(p.astype(BF16), do_b, TN, preferred_element_type=F32)
            return dq + jnp.dot(ds, k_ref[rows, :], preferred_element_type=F32)

        dq = lax.fori_loop(0, _n_key_blocks(i), step, jnp.zeros((Q_TILE, HEAD_DIM), F32))
        dq_ref[...] = dq * ATT_SCALE

    wide = HEADS * HEAD_DIM
    qt = pl.BlockSpec((Q_TILE, HEAD_DIM), lambda h, i: (i, h))
    full = pl.BlockSpec((SEQ, HEAD_DIM), lambda h, i: (0, h))
    return _host_call(
        body, 9, 3, fused, _att_first_last, name="attn_bwd", grid=(HEADS, N_Q_TILES),
        in_specs=[qt, full, full, qt, pl.BlockSpec((None, Q_TILE, 1), lambda h, i: (h, i, 0)),
                  pl.BlockSpec((Q_TILE, HEAD_DIM), lambda h, i: (i, h + HEADS)), SLOPE_SPEC, TABLE_SPEC, TABLE_SPEC],
        out_specs=[qt, full, full], out_shape=[jax.ShapeDtypeStruct((SEQ, wide), F32)] * 3,
        scratch_shapes=[], sem=("parallel", "arbitrary"),
        operands=[qn, kn, vb, o, lse, d_mix, _alibi_slopes(), *_att_tables()] + list(fused_arrays))


def _qk_bwd(proj, q_w, k_w, dqn, dkn, dv):
    def body(aq_ref, ak_ref, qw_ref, kw_ref, dqn_ref, dkn_ref, dv_ref, daq_ref, dak_ref, dav_ref, gq_ref, gk_ref):
        h = pl.program_id(0)

        @pl.when(h == 0)
        def _():
            gq_ref[...] = jnp.zeros_like(gq_ref)
            gk_ref[...] = jnp.zeros_like(gk_ref)

        def one(a_ref, w_ref, d_ref, da_ref, g_ref):
            a = a_ref[...]
            d = d_ref[...]
            rs = lax.rsqrt(jnp.mean(a * a, axis=-1, keepdims=True) + EPS)
            ah = a * rs
            g_ref[...] += jnp.sum(d * ah, axis=0, keepdims=True)
            dah = d * w_ref[...]
            da_ref[...] = (rs * (dah - ah * jnp.mean(dah * ah, axis=-1, keepdims=True))).astype(BF16)

        one(aq_ref, qw_ref, dqn_ref, daq_ref, gq_ref)
        one(ak_ref, kw_ref, dkn_ref, dak_ref, gk_ref)
        dav_ref[...] = dv_ref[...].astype(BF16)

    wide = HEADS * HEAD_DIM
    vec = pl.BlockSpec((1, HEAD_DIM), lambda h: (0, 0))
    return pl.pallas_call(
        body, name="qk_bwd", grid=(HEADS,),
        in_specs=[_head_col(ATT_OFF), _head_col(ATT_OFF + HEADS), vec, vec, _head_col(0), _head_col(0), _head_col(0)],
        out_specs=[_head_col(0)] * 3 + [vec, vec],
        out_shape=[jax.ShapeDtypeStruct((SEQ, wide), BF16)] * 3 + [jax.ShapeDtypeStruct((1, HEAD_DIM), F32)] * 2,
        compiler_params=_params(("arbitrary",)))(proj, proj, q_w, k_w, dqn, dkn, dv)


def _pair_sum(name, partial, theirs, core):
    _, r, c = theirs.shape
    tr = r // 2 if r % 16 == 0 else r

    def body(core_ref, a_ref, b_ref, o_ref):
        o_ref[...] = (a_ref[...].astype(F32) + b_ref[...].astype(F32)).astype(BF16)

    spec = pl.BlockSpec((None, tr, c), lambda q, i, core_ref: (q, i, 0))
    grid_spec = pltpu.PrefetchScalarGridSpec(
        num_scalar_prefetch=1, grid=(4, r // tr),
        in_specs=[pl.BlockSpec((None, tr, c), lambda q, i, core_ref: (2 * q + core_ref[0], i, 0)), spec],
        out_specs=spec)
    return pl.pallas_call(body, name=name, grid_spec=grid_spec, out_shape=jax.ShapeDtypeStruct(theirs.shape, BF16),
                          compiler_params=_params(("parallel", "parallel")))(core, partial, theirs)


def _adamw_step(w, m, v, g):
    nm = ADAM_B1 * m + (1.0 - ADAM_B1) * g
    nv = ADAM_B2 * v + (1.0 - ADAM_B2) * (g * g)
    m_hat = nm / (1.0 - ADAM_B1 ** ADAM_STEP)
    v_hat = nv / (1.0 - ADAM_B2 ** ADAM_STEP)
    return -ADAM_LR * (m_hat / (jnp.sqrt(v_hat) + ADAM_EPS) + ADAM_WD * w), nm, nv


def _adamw(name, w, m, v, addends, tr=None):
    r, c = w.shape
    tr = r if tr is None else tr
    n_add = len(addends)

    def body(*refs):
        w_ref, m_ref, v_ref = refs[:3]
        add_refs = refs[3:3 + n_add]
        g_ref, d_ref, nm_ref, nv_ref = refs[3 + n_add:]
        g = add_refs[0][...].astype(F32)
        for a_ref in add_refs[1:]:
            g = g + a_ref[...].astype(F32)
        g_ref[...] = g
        d_ref[...], nm_ref[...], nv_ref[...] = _adamw_step(w_ref[...], m_ref[...], v_ref[...], g)

    spec = pl.BlockSpec((tr, c), lambda i: (i, 0))
    out = jax.ShapeDtypeStruct((r, c), F32)
    return pl.pallas_call(body, name=name, grid=(r // tr,), in_specs=[spec] * (3 + n_add), out_specs=[spec] * 4,
                          out_shape=[out] * 4, compiler_params=_params(("parallel",)))(w, m, v, *addends)


def _adamw_reduced(name, w, m, v, chip_sums, received, chip, tr):
    r, c = w.shape

    def body(chip_ref, w_ref, m_ref, v_ref, own_ref, r0_ref, r1_ref, r2_ref, g_ref, d_ref, nm_ref, nv_ref):
        g = ((own_ref[...].astype(F32) + r0_ref[...].astype(F32)) + r1_ref[...].astype(F32)) + r2_ref[...].astype(F32)
        g_ref[...] = g
        d_ref[...], nm_ref[...], nv_ref[...] = _adamw_step(w_ref[...], m_ref[...], v_ref[...], g)

    spec = pl.BlockSpec((tr, c), lambda i, chip_ref: (i, 0))

    def slot(k):
        return pl.BlockSpec((None, tr, c), lambda i, chip_ref: (k, i, 0))

    grid_spec = pltpu.PrefetchScalarGridSpec(
        num_scalar_prefetch=1, grid=(r // tr,),
        in_specs=[spec, spec, spec, pl.BlockSpec((None, tr, c), lambda i, chip_ref: (chip_ref[0], i, 0)),
                  slot(0), slot(1), slot(2)],
        out_specs=[spec] * 4)
    out = jax.ShapeDtypeStruct((r, c), F32)
    return pl.pallas_call(body, name=name, grid_spec=grid_spec, out_shape=[out] * 4,
                          compiler_params=_params(("parallel",)))(chip, w, m, v, chip_sums, received, received, received)


def _sum_devices(gathered):
    _, r, c = gathered.shape

    def body(g_ref, o_ref):
        acc = g_ref[0]
        for d in range(1, N_DEV):
            acc = acc + g_ref[d]
        o_ref[...] = acc

    return pl.pallas_call(body, name="sum_devices", out_shape=jax.ShapeDtypeStruct((r, c), F32))(gathered)


def _pack_rows(vectors, rows):
    flat = jnp.concatenate([v.reshape(-1) for v in vectors])
    return jnp.pad(flat, (0, rows * 128 - flat.shape[0])).reshape(rows, 128)


def _unpack(flat, shapes):
    out, off = [], 0
    for shp in shapes:
        n = 1
        for d in shp:
            n *= d
        out.append(flat[off:off + n].reshape(shp))
        off += n
    return out


def _device_step(xs, tgt, mod, norm1_w, norm2_w, lb_logits, hg_norm_w, q_norm_w, k_norm_w, conv_w_full, conv_b,
                 win_g, w_out_x, w_up_x, w_down_x, core=None):
    fused = core is not None
    shift1, scale1, gate1, shift2, scale2, gate2 = (mod[k] for k in range(6))

    h, rstd1 = _norm_fwd("norm1_fwd", xs, norm1_w, scale1, shift1)
    if fused:
        near = (0, 1, 2)
        head_rows, tail_rows = (0, UP_HEAD_ROWS), (UP_HEAD_ROWS, D_MODEL - UP_HEAD_ROWS)
        proj, (wout_g, wup_g) = _mm_blocked_rhs(
            "mm_in", h, win_g, fused_arrays=[w_out_x, w_up_x],
            fused=[_FusedCopies("gather", [w_out_x]), _FusedCopies("gather", [w_up_x], peers=near, rows=head_rows)])
        (a_out, o_pre), (wup_g,) = _hgrn_fwd(
            proj, lb_logits, hg_norm_w, fused_arrays=[w_up_x, wup_g],
            fused=_FusedCopies("gather_more", [w_up_x, wup_g], peers=near, rows=tail_rows, relay_rows=head_rows))
        wout_g, = _forward_to_sibling("allgather_stage2_out", [wout_g])
        wout_full = wout_g.reshape(D_MODEL, D_MODEL)
        (qn, kn, vb), _ = _qk_prep(proj, q_norm_w, k_norm_w)
        (att_o, lse), (wup_g,) = _attn_fwd(qn, kn, vb, _FusedCopies("relay", [wup_g], rows=tail_rows), [wup_g])
    else:
        proj = _mm_blocked_rhs("mm_in", h, win_g)
        (a_out, o_pre), _ = _hgrn_fwd(proj, lb_logits, hg_norm_w)
        wup_g, wout_full, wdown_full = w_up_x, w_out_x, w_down_x
        (qn, kn, vb), _ = _qk_prep(proj, q_norm_w, k_norm_w)
        (att_o, lse), _ = _attn_fwd(qn, kn, vb)
    mixin = jnp.concatenate([a_out, att_o.astype(BF16)], axis=1)
    if fused:
        mix, (wup_g,) = _mm_plain("mm_out", mixin, wout_full, NN, 512, 1024, F32,
                                  fused=_FusedCopies("forward", [wup_g]), fused_arrays=[wup_g])
    else:
        mix = _mm_plain("mm_out", mixin, wout_full, NN, 512, 1024, F32)
    x1, h2, rstd2 = _norm_fwd("norm2_fwd", xs, norm2_w, scale2, shift2, resid=mix, gate=gate1)
    if fused:
        u, (wdown_g,) = _mm_blocked_rhs("mm_up", h2, wup_g, fused=_FusedCopies("gather", [w_down_x]),
                                        fused_arrays=[w_down_x])
        y, (wdown_g,) = _conv_gate_fwd(u, conv_w_full, conv_b, _FusedCopies("forward", [wdown_g]), [wdown_g])
        wdown_full = wdown_g.reshape(D_FF, D_MODEL)
    else:
        u = _mm_blocked_rhs("mm_up", h2, wup_g)
        y = _conv_gate_fwd(u, conv_w_full, conv_b)
    ffn = _mm_plain("mm_down", y, wdown_full, NN, 512, 512, F32)
    loss_v, dout, dffn, dgate2 = _loss_head(x1, ffn, gate2, tgt)

    dy = _mm_plain("mm_down_dx", dffn, wdown_full, NT, 512, UP_BLK, BF16)
    gw_down = _mm_plain("mm_down_dw", y, dffn, TN, UP_BLK, 1024, BF16)
    da, dg, gconv_w, gconv_b = _conv_gate_bwd(u, dy, conv_w_full, conv_b)
    dh2 = _mm_halves_rhs_t("mm_up_dx", da, dg, wup_g)
    gw_up = _mm_halves_wgrad("mm_up_dw", h2, da, dg)
    if fused:
        part_up, part_down = gw_up, gw_down.reshape(N_DEV, FF_BLK, D_MODEL)
        (dx1, dmix, dshift2, dscale2, gnorm2, dgate1), (sib_up,) = _norm_bwd(
            "norm2_bwd", dh2, x1, rstd2, norm2_w, scale2, dout, mix=mix, gate=gate1,
            fused=_FusedCopies("sibling", [part_up]), fused_arrays=[part_up])
    else:
        dx1, dmix, dshift2, dscale2, gnorm2, dgate1 = _norm_bwd(
            "norm2_bwd", dh2, x1, rstd2, norm2_w, scale2, dout, mix=mix, gate=gate1)
    gw_out = _mm_plain("mm_out_dw", mixin, dmix, TN, 512, 1024, BF16)
    if fused:
        part_out = gw_out.reshape(N_DEV, OUT_BLK, D_MODEL)
        dmixin, (sib_out, sib_down) = _mm_plain(
            "mm_out_dx", dmix, wout_full, NT, 512, 1024, F32,
            fused=_FusedCopies("sibling", [part_out, part_down]), fused_arrays=[part_out, part_down])
        cs_up = _pair_sum("grad_pair_sum_up", part_up, sib_up, core)
        cs_out = _pair_sum("grad_pair_sum_out", part_out, sib_out, core)
        cs_down = _pair_sum("grad_pair_sum_down", part_down, sib_down, core)
        (dhq, dhf, dhi, dhg, glog, ghg), (fc_up,) = _hgrn_bwd(
            proj, lb_logits, hg_norm_w, o_pre, dmixin, _FusedCopies("chips", [cs_up]), [cs_up])
        (dqn, dkn, dvv), (fc_down,) = _attn_bwd(qn, kn, vb, att_o, lse, dmixin,
                                                _FusedCopies("chips", [cs_down]), [cs_down])
    else:
        dmixin = _mm_plain("mm_out_dx", dmix, wout_full, NT, 512, 1024, F32)
        (dhq, dhf, dhi, dhg, glog, ghg), _ = _hgrn_bwd(proj, lb_logits, hg_norm_w, o_pre, dmixin)
        (dqn, dkn, dvv), _ = _attn_bwd(qn, kn, vb, att_o, lse, dmixin)
    daq, dak, dav, gqw, gkw = _qk_bwd(proj, q_norm_w, k_norm_w, dqn, dkn, dvv)
    dproj = jnp.concatenate([dhq, dhf, dhi, dhg, daq, dak, dav], axis=1)
    if fused:
        gw_in, (fc_out,) = _mm_wgrad_blocked("mm_in_dw", h, dproj, fused=_FusedCopies("chips", [cs_out]),
                                             fused_arrays=[cs_out])
        from_sibling, = _exchange_sibling("grad_exchange_sibling_b", [gw_in])
        cs_in = _pair_sum("grad_pair_sum_in", gw_in, from_sibling, core)
        dh, (fc_in,) = _mm_blocked_rhs_t("mm_in_dx", dproj, win_g, fused=_FusedCopies("chips", [cs_in]),
                                         fused_arrays=[cs_in])
        large = [(cs_in, fc_in), (cs_out, fc_out), (cs_up, fc_up), (cs_down, fc_down)]
    else:
        gw_in = _mm_wgrad_blocked("mm_in_dw", h, dproj)
        dh = _mm_blocked_rhs_t("mm_in_dx", dproj, win_g)
        large = [gw_in, gw_out, gw_up, gw_down]
    grad_x, dshift1, dscale1, gnorm1 = _norm_bwd("norm1_bwd", dh, xs, rstd1, norm1_w, scale1, dx1)
    gmod = jnp.concatenate([dshift1, dscale1, dgate1, dshift2, dscale2, dgate2], axis=1)
    return (loss_v, grad_x, gmod, gnorm1, gnorm2, glog, ghg, gqw, gkw, gconv_b, gconv_w, *large)


def kernel(x, c, w_ada, b_ada, norm1_w, w_in, lb_logits, hg_norm_w, q_norm_w, k_norm_w, w_out, norm2_w, w_up, conv_w, conv_b, w_down, loss_target, m_w_ada, m_b_ada, m_norm1_w, m_w_in, m_lb_logits, m_hg_norm_w, m_q_norm_w, m_k_norm_w, m_w_out, m_norm2_w, m_w_up, m_conv_w, m_conv_b, m_w_down, v_w_ada, v_b_ada, v_norm1_w, v_w_in, v_lb_logits, v_hg_norm_w, v_q_norm_w, v_k_norm_w, v_w_out, v_norm2_w, v_w_up, v_conv_w, v_conv_b, v_w_down):
    ix, iy, ic = lax.axis_index("x"), lax.axis_index("y"), lax.axis_index("c")
    me = 4 * ix + 2 * iy + ic
    my_chip = 2 * ix + iy

    xs = x[0]
    tgt = loss_target[0]

    win_g, = _allgather_weights([w_in[0].astype(BF16)])

    c_all = _allgather_vmem(c.reshape(8, D_MODEL // 8), "allgather_c").reshape(N_DEV, D_MODEL)
    b_blk = lax.dynamic_slice_in_dim(b_ada, me * ADA_BLK, ADA_BLK, axis=1)
    mod_cols = _ada_fwd(c_all, w_ada[0], b_blk)
    mod_all = _allgather_vmem(mod_cols, "allgather_mod").reshape(N_DEV, N_DEV, ADA_BLK)
    mod = lax.dynamic_index_in_dim(mod_all, me, axis=1, keepdims=False).reshape(6, 1, D_MODEL)

    conv_w_all = _allgather_vmem(_pack_rows([conv_w[0]], 24), "allgather_conv_w").reshape(N_DEV, 24 * 128)
    conv_w_full = conv_w_all[:, :3 * FF_BLK].reshape(N_DEV, 3, FF_BLK).transpose(1, 0, 2).reshape(3, D_FF)

    (loss_v, grad_x, gmod, gnorm1, gnorm2, glog, ghg, gqw, gkw, gconv_b, gconv_w,
     rs_in, rs_out, rs_up, rs_down) = _device_step(
        xs, tgt, mod, norm1_w, norm2_w, lb_logits, hg_norm_w, q_norm_w, k_norm_w, conv_w_full, conv_b,
        win_g, w_out[0].astype(BF16), w_up[0].astype(BF16), w_down[0].astype(BF16),
        core=jnp.reshape(ic, (1,)).astype(jnp.int32))
    loss = lax.psum(loss_v[0, 0], AXES)

    small_shapes = [(1, 6 * D_MODEL), (1, D_MODEL), (1, D_MODEL), (2, HEADS * HEAD_DIM), (1, HEAD_DIM),
                    (1, HEAD_DIM), (1, HEAD_DIM), (1, D_FF), (3, D_FF)]
    small = [gmod, gnorm1, gnorm2, glog, ghg, gqw, gkw, gconv_b, gconv_w]
    n_small = sum(a.size for a in small)
    rows = -(-n_small // 1024) * 8
    gathered = _allgather_vmem(_pack_rows(small, rows), "allgather_small").reshape(N_DEV, rows, 128)
    summed = _sum_devices(gathered).reshape(-1)
    (g_b_ada, g_norm1, g_norm2, g_lb, g_hg, g_q, g_k, g_conv_b, g_conv_w_full) = _unpack(summed, small_shapes)
    g_conv_w = lax.dynamic_slice_in_dim(g_conv_w_full, me * FF_BLK, FF_BLK, axis=1)

    gmod_all = gathered[:, :6 * D_MODEL // 128, :].reshape(N_DEV, 6 * D_MODEL)
    gmod_cols = lax.dynamic_slice_in_dim(gmod_all, me * ADA_BLK, ADA_BLK, axis=1)
    g_w_ada_raw = _ada_wgrad(c_all, gmod_cols)

    chip = jnp.reshape(my_chip, (1,)).astype(jnp.int32)

    def big_update(name, w, m, v, rs, tr):
        chip_sums, received = rs
        return _adamw_reduced(name, w[0], m[0], v[0], chip_sums, received, chip, tr)

    r_in = big_update("adamw_w_in", w_in, m_w_in, v_w_in, rs_in, 256)
    r_out = big_update("adamw_w_out", w_out, m_w_out, v_w_out, rs_out, 128)
    r_up = big_update("adamw_w_up", w_up, m_w_up, v_w_up, rs_up, 256)
    r_down = big_update("adamw_w_down", w_down, m_w_down, v_w_down, rs_down, 176)
    r_ada = _adamw("adamw_w_ada", w_ada[0], m_w_ada[0], v_w_ada[0], [g_w_ada_raw], tr=256)
    r_convw = _adamw("adamw_conv_w", conv_w[0], m_conv_w[0], v_conv_w[0], [g_conv_w])

    rep_shapes = [(1, 6 * D_MODEL), (1, D_MODEL), (1, D_MODEL), (2, HEADS * HEAD_DIM), (1, HEAD_DIM),
                  (1, HEAD_DIM), (1, HEAD_DIM), (1, D_FF)]
    rep_rows = -(-sum(a * b for a, b in rep_shapes) // 1024) * 8
    pack = lambda arrs: _pack_rows(arrs, rep_rows)
    rep = _adamw("adamw_small",
                 pack([b_ada, norm1_w, norm2_w, lb_logits, hg_norm_w, q_norm_w, k_norm_w, conv_b]),
                 pack([m_b_ada, m_norm1_w, m_norm2_w, m_lb_logits, m_hg_norm_w, m_q_norm_w, m_k_norm_w, m_conv_b]),
                 pack([v_b_ada, v_norm1_w, v_norm2_w, v_lb_logits, v_hg_norm_w, v_q_norm_w, v_k_norm_w, v_conv_b]),
                 [pack([g_b_ada, g_norm1, g_norm2, g_lb, g_hg, g_q, g_k, g_conv_b])])
    rep = [_unpack(r.reshape(-1), rep_shapes) for r in rep]

    def big(r):
        return [a[None] for a in r]

    order = {"w_ada": big(r_ada), "b_ada": [r[0] for r in rep], "norm1_w": [r[1] for r in rep],
             "w_in": big(r_in), "lb_logits": [r[3] for r in rep], "hg_norm_w": [r[4] for r in rep],
             "q_norm_w": [r[5] for r in rep], "k_norm_w": [r[6] for r in rep], "w_out": big(r_out),
             "norm2_w": [r[2] for r in rep], "w_up": big(r_up), "conv_w": big(r_convw),
             "conv_b": [r[7] for r in rep], "w_down": big(r_down)}
    names = ["w_ada", "b_ada", "norm1_w", "w_in", "lb_logits", "hg_norm_w", "q_norm_w", "k_norm_w", "w_out",
             "norm2_w", "w_up", "conv_w", "conv_b", "w_down"]
    outs = [loss, grad_x[None]]
    for kind in range(4):
        outs += [order[n][kind] for n in names]
    return tuple(outs)
```

```python
import functools

import jax
import jax.numpy as jnp
from jax import lax
from jax.experimental import pallas as pl
from jax.experimental.pallas import tpu as pltpu

F32 = jnp.float32
BF16 = jnp.bfloat16

N_DEV = 8
SEQ = 2048
D_MODEL = 2048
HEADS = 8
HEAD_DIM = 128
IN_COLS = 7168
IN_BLK = IN_COLS // N_DEV
D_FF = 5632
UP_BLK = 2 * D_FF // N_DEV
FF_BLK = D_FF // N_DEV
ADA_BLK = 6 * D_MODEL // N_DEV
OUT_BLK = D_MODEL // N_DEV
EPS = 1e-6
CHUNK = 16
ROW_TILE = 256
V7X_VMEM_LIMIT = 56 * 1024 * 1024

ADAM_LR = 0.001
ADAM_B1 = 0.9
ADAM_B2 = 0.999
ADAM_EPS = 1e-08
ADAM_WD = 0.01
ADAM_STEP = 10

NN = (((1,), (0,)), ((), ()))
NT = (((1,), (1,)), ((), ()))
TN = (((0,), (0,)), ((), ()))
MESH = pl.DeviceIdType.MESH
AXES = ("x", "y", "c")


def _params(sem=None, vmem=V7X_VMEM_LIMIT):
    return pltpu.CompilerParams(dimension_semantics=sem, vmem_limit_bytes=vmem)


def _sigmoid(x):
    return 1.0 / (1.0 + jnp.exp(-x))


def _dsilu(x, s):
    return s * (1.0 + x * (1.0 - s))


def _lane_sum(x, ones_bf16):
    return jnp.dot(x.astype(BF16), ones_bf16, preferred_element_type=F32)


def _mesh_pos():
    return lax.axis_index("x"), lax.axis_index("y"), lax.axis_index("c")


def _allgather_vmem(x_blk, name):
    m_per, n = x_blk.shape

    def body(x_ref, out_ref, send_sems, recv_sems, local_sem):
        x, y, c = _mesh_pos()
        me, sibling = (x, y, c), (x, y, 1 - c)
        chips = [(1 - x, y), (x, 1 - y), (1 - x, 1 - y)]

        def rows(px, py, pc):
            return out_ref.at[pl.ds((4 * px + 2 * py + pc) * m_per, m_per), :]

        def copy(k, block, to, src=None):
            return pltpu.make_async_remote_copy(
                src_ref=rows(*block) if src is None else src, dst_ref=rows(*block),
                send_sem=send_sems.at[k], recv_sem=recv_sems.at[k], device_id=to, device_id_type=MESH)

        mine = pltpu.make_async_copy(x_ref, rows(*me), local_sem)
        mine.start()
        first = [copy(0, me, sibling, src=x_ref)]
        first += [copy(1 + j, me, (*chip, c), src=x_ref) for j, chip in enumerate(chips)]
        for cp in first:
            cp.start()
        passed = [copy(4 + j, (*chip, c), sibling) for j, chip in enumerate(chips)]
        for j, chip in enumerate(chips):
            copy(1 + j, (*chip, c), me).wait_recv()
            passed[j].start()
        copy(0, sibling, me).wait_recv()
        for j, chip in enumerate(chips):
            copy(4 + j, (*chip, 1 - c), me).wait_recv()
        for cp in first + passed:
            cp.wait_send()
        mine.wait()

    return pl.pallas_call(
        body, name=name,
        out_shape=jax.ShapeDtypeStruct((N_DEV * m_per, n), x_blk.dtype),
        in_specs=[pl.BlockSpec(memory_space=pltpu.VMEM)],
        out_specs=pl.BlockSpec(memory_space=pltpu.VMEM),
        scratch_shapes=[pltpu.SemaphoreType.DMA((7,)), pltpu.SemaphoreType.DMA((7,)), pltpu.SemaphoreType.DMA],
    )(x_blk)


def _flip(v, bit):
    return v + bit - 2 * v * bit


def _relay_chips(x, y, c):
    return (_flip(x, 1 - c), _flip(y, c)), (_flip(x, c), _flip(y, 1 - c))


W_IN_EXCHANGE_ROWS = ((0, 1280), (1280, 384), (1664, 384))
UP_HEAD_ROWS = 768
GATHER_PARTS = 4


def _allgather_weights(blocks):
    n_arr = len(blocks)
    parts = GATHER_PARTS

    def body(*refs):
        ins, outs = refs[:n_arr], refs[n_arr:2 * n_arr]
        send_sems, recv_sems, local_sems = refs[2 * n_arr:]
        x, y, c = _mesh_pos()
        me, sibling = (x, y, c), (x, y, 1 - c)
        near = [(1 - x, y), (x, 1 - y)]
        chips = near + [(1 - x, 1 - y)]
        relay_from, relay_to = _relay_chips(x, y, c)

        def rows(a, p):
            hr = ins[a].shape[0] // parts
            return pl.ds(p * hr, hr)

        def slot(a, pos, p):
            return outs[a].at[4 * pos[0] + 2 * pos[1] + pos[2], rows(a, p)]

        def copy(a, k, p, src, lands, to):
            return pltpu.make_async_remote_copy(
                src_ref=src, dst_ref=slot(a, lands, p), send_sem=send_sems.at[a, k, p], recv_sem=recv_sems.at[a, k, p],
                device_id=to, device_id_type=MESH)

        sent = []
        local = [pltpu.make_async_copy(ins[a], outs[a].at[4 * x + 2 * y + c], local_sems.at[a]) for a in range(n_arr)]
        for cp in local:
            cp.start()
        for p in range(parts):
            for a in range(n_arr):
                own = ins[a].at[rows(a, p)]
                sent.append(copy(a, 0, p, own, me, sibling))
                sent += [copy(a, 1 + j, p, own, me, (*chip, c)) for j, chip in enumerate(near)]
        for cp in sent:
            cp.start()

        def start(cp):
            cp.start()
            sent.append(cp)

        for p in range(parts):
            for a in range(n_arr):
                for j, chip in enumerate(near):
                    copy(a, 1 + j, p, ins[a].at[rows(a, p)], (*chip, c), me).wait_recv()
                    start(copy(a, 4 + j, p, slot(a, (*chip, c), p), (*chip, c), sibling))
                start(copy(a, 3, p, slot(a, (*relay_from, c), p), (*relay_from, c), (*relay_to, c)))
        for p in range(parts):
            for a in range(n_arr):
                copy(a, 3, p, ins[a].at[rows(a, p)], (*chips[2], c), me).wait_recv()
                start(copy(a, 6, p, slot(a, (*chips[2], c), p), (*chips[2], c), sibling))
        for p in range(parts):
            for a in range(n_arr):
                copy(a, 0, p, ins[a].at[rows(a, p)], sibling, me).wait_recv()
                for j, chip in enumerate(chips):
                    copy(a, 4 + j, p, ins[a].at[rows(a, p)], (*chip, 1 - c), me).wait_recv()
        for cp in sent:
            cp.wait_send()
        for cp in local:
            cp.wait()

    return pl.pallas_call(
        body, name="allgather_weights",
        out_shape=[jax.ShapeDtypeStruct((N_DEV,) + b.shape, b.dtype) for b in blocks],
        in_specs=[pl.BlockSpec(memory_space=pltpu.HBM)] * n_arr, out_specs=[pl.BlockSpec(memory_space=pltpu.HBM)] * n_arr,
        scratch_shapes=[pltpu.SemaphoreType.DMA((n_arr, 7, parts)), pltpu.SemaphoreType.DMA((n_arr, 7, parts)),
                        pltpu.SemaphoreType.DMA((n_arr,))],
    )(*blocks)


HBM_SPEC = pl.BlockSpec(memory_space=pltpu.HBM)


class _FusedCopies:
    def __init__(self, kind, arrays, peers=(0, 1, 2, 3), rows=None, relay_rows=None):
        self.kind = kind
        self.peers = peers
        self.rows = rows
        self.relay_rows = relay_rows
        n = len(arrays) // 2 if kind == "gather_more" else len(arrays)
        self.n = n
        self.n_in = len(arrays)
        self.aliases = {}
        if kind == "gather":
            self.out_shape = [jax.ShapeDtypeStruct((N_DEV,) + a.shape, a.dtype) for a in arrays]
            self.scratch_shapes = [pltpu.SemaphoreType.DMA((n, 4, GATHER_PARTS)),
                                   pltpu.SemaphoreType.DMA((n, 4, GATHER_PARTS)), pltpu.SemaphoreType.DMA((n,))]
        elif kind == "gather_more":
            self.out_shape = [jax.ShapeDtypeStruct(a.shape, a.dtype) for a in arrays[n:]]
            self.scratch_shapes = [pltpu.SemaphoreType.DMA((n, 5, GATHER_PARTS)),
                                   pltpu.SemaphoreType.DMA((n, 5, GATHER_PARTS)), pltpu.SemaphoreType.DMA((n,))]
            self.aliases = {n + a: a for a in range(n)}
        elif kind == "relay":
            self.out_shape = [jax.ShapeDtypeStruct(a.shape, a.dtype) for a in arrays]
            self.scratch_shapes = [pltpu.SemaphoreType.DMA((n,)), pltpu.SemaphoreType.DMA((n,))]
            self.aliases = {a: a for a in range(n)}
        elif kind == "forward":
            self.out_shape = [jax.ShapeDtypeStruct(a.shape, a.dtype) for a in arrays]
            self.scratch_shapes = [pltpu.SemaphoreType.DMA((n, 3)), pltpu.SemaphoreType.DMA((n, 3))]
            self.aliases = {a: a for a in range(n)}
        elif kind == "sibling":
            self.out_shape = [jax.ShapeDtypeStruct((4,) + a.shape[1:], a.dtype) for a in arrays]
            self.scratch_shapes = [pltpu.SemaphoreType.DMA((n, 4)), pltpu.SemaphoreType.DMA((n, 4))]
        elif kind == "chips_more":
            n = self.n = len(arrays) // 2
            self.out_shape = [jax.ShapeDtypeStruct(a.shape, a.dtype) for a in arrays[n:]]
            self.scratch_shapes = [pltpu.SemaphoreType.DMA((n, 3)), pltpu.SemaphoreType.DMA((n, 3))]
            self.aliases = {n + a: a for a in range(n)}
        else:
            self.out_shape = [jax.ShapeDtypeStruct((3,) + a.shape[1:], a.dtype) for a in arrays]
            self.scratch_shapes = [pltpu.SemaphoreType.DMA((n, 3)), pltpu.SemaphoreType.DMA((n, 3))]
        self.in_specs = [HBM_SPEC] * self.n_in
        self.out_specs = [HBM_SPEC] * n
        self.n_scratch = len(self.scratch_shapes)

    def copies(self, ins, outs, sems):
        x, y, c = _mesh_pos()
        chips = [(1 - x, y), (x, 1 - y), (1 - x, 1 - y)]
        sibling = (x, y, 1 - c)
        starts, waits = [], []
        relay_from, relay_to = _relay_chips(x, y, c)

        def relayed(a, buf, lands, send_sem, recv_sem, rows):
            first, count = rows or (0, buf.shape[1])
            span = pl.ds(first, count)
            return pltpu.make_async_remote_copy(
                src_ref=buf.at[4 * relay_from[0] + 2 * relay_from[1] + c, span],
                dst_ref=outs[a].at[4 * lands[0] + 2 * lands[1] + c, span], send_sem=send_sem, recv_sem=recv_sem,
                device_id=(*relay_to, c), device_id_type=MESH)

        if self.kind in ("gather", "gather_more"):
            send_sems, recv_sems, local_sems = sems
            me = (x, y, c)
            peers = [sibling] + [(px, py, c) for px, py in chips]

            def slot(a, pos):
                return outs[a].at[4 * pos[0] + 2 * pos[1] + pos[2]]

            def span(a, p=None):
                first, count = self.rows or (0, ins[a].shape[0])
                if p is None:
                    return pl.ds(first, count)
                return pl.ds(first + p * (count // GATHER_PARTS), count // GATHER_PARTS)

            def remote(a, k, p, lands_from):
                return pltpu.make_async_remote_copy(
                    src_ref=ins[a].at[span(a, p)], dst_ref=slot(a, lands_from).at[span(a, p)],
                    send_sem=send_sems.at[a, k, p], recv_sem=recv_sems.at[a, k, p], device_id=peers[k],
                    device_id_type=MESH)

            for a in range(self.n):
                local = pltpu.make_async_copy(ins[a].at[span(a)], slot(a, me).at[span(a)], local_sems.at[a])
                starts.append(local)
                waits.append(local)
            for p in range(GATHER_PARTS):
                for a in range(self.n):
                    for k in self.peers:
                        starts.append(remote(a, k, p, me))
                        waits.append(remote(a, k, p, peers[k]))
            if self.kind == "gather_more" and self.relay_rows is not None:
                for a in range(self.n):
                    buf = ins[self.n + a]
                    starts.append(relayed(a, buf, relay_from, send_sems.at[a, 4, 0], recv_sems.at[a, 4, 0],
                                          self.relay_rows))
                    waits.append(relayed(a, buf, chips[2], send_sems.at[a, 4, 0], recv_sems.at[a, 4, 0],
                                         self.relay_rows))
        elif self.kind == "relay":
            send_sems, recv_sems = sems
            for a in range(self.n):
                starts.append(relayed(a, ins[a], relay_from, send_sems.at[a], recv_sems.at[a], self.rows))
                waits.append(relayed(a, ins[a], chips[2], send_sems.at[a], recv_sems.at[a], self.rows))
        elif self.kind == "forward":
            send_sems, recv_sems = sems

            def passed_on(a, j, pc_src, pc_dst):
                px, py = chips[j]
                return pltpu.make_async_remote_copy(
                    src_ref=ins[a].at[4 * px + 2 * py + pc_src], dst_ref=outs[a].at[4 * px + 2 * py + pc_dst],
                    send_sem=send_sems.at[a, j], recv_sem=recv_sems.at[a, j], device_id=sibling, device_id_type=MESH)

            for a in range(self.n):
                for j in range(3):
                    starts.append(passed_on(a, j, c, c))
                    waits.append(passed_on(a, j, c, 1 - c))
        elif self.kind == "sibling":
            send_sems, recv_sems = sems
            for a in range(self.n):
                for q in range(4):
                    cp = pltpu.make_async_remote_copy(
                        src_ref=ins[a].at[2 * q + 1 - c], dst_ref=outs[a].at[q], send_sem=send_sems.at[a, q],
                        recv_sem=recv_sems.at[a, q], device_id=sibling, device_id_type=MESH)
                    starts.append(cp)
                    waits.append(cp)
        else:
            send_sems, recv_sems = sems
            for a in range(self.n):
                first, count = self.rows or (0, ins[a].shape[1])
                span = pl.ds(first, count)
                for j, (px, py) in enumerate(chips):
                    cp = pltpu.make_async_remote_copy(
                        src_ref=ins[a].at[2 * px + py, span], dst_ref=outs[a].at[j, span],
                        send_sem=send_sems.at[a, j], recv_sem=recv_sems.at[a, j], device_id=(px, py, c),
                        device_id_type=MESH)
                    starts.append(cp)
                    waits.append(cp)
        return starts, waits


def _fused_groups(fused):
    if fused is None:
        return []
    return list(fused) if isinstance(fused, (list, tuple)) else [fused]


def _host_body(body, n_in, n_out, fused, first_last):
    groups = _fused_groups(fused)
    if not groups:
        return body
    n_fin, n_fout = sum(g.n_in for g in groups), sum(g.n for g in groups)
    n_fsem = sum(g.n_scratch for g in groups)

    def wrapped(*refs):
        core_in, f_in = refs[:n_in], refs[n_in:n_in + n_fin]
        core_out = refs[n_in + n_fin:n_in + n_fin + n_out]
        f_out = refs[n_in + n_fin + n_out:n_in + n_fin + n_out + n_fout]
        rest = refs[n_in + n_fin + n_out + n_fout:]
        core_scratch, f_sems = rest[:len(rest) - n_fsem], rest[len(rest) - n_fsem:]
        starts, waits = [], []
        for g in groups:
            s, w = g.copies(f_in[:g.n_in], f_out[:g.n], f_sems[:g.n_scratch])
            f_in, f_out, f_sems = f_in[g.n_in:], f_out[g.n:], f_sems[g.n_scratch:]
            starts += s
            waits += w
        first, last = first_last()

        @pl.when(first)
        def _():
            for cp in starts:
                cp.start()

        body(*core_in, *core_out, *core_scratch)

        @pl.when(last)
        def _():
            for cp in waits:
                cp.wait()

    return wrapped


def _host_call(body, n_in, n_out, fused, first_last, *, name, grid, in_specs, out_specs, out_shape, scratch_shapes,
               sem, operands):
    aliases = {}
    in_specs, out_specs, out_shape, scratch_shapes = list(in_specs), list(out_specs), list(out_shape), list(scratch_shapes)
    fin, fout = n_in, n_out
    for g in _fused_groups(fused):
        aliases.update({fin + fi: fout + fo for fi, fo in g.aliases.items()})
        fin, fout = fin + g.n_in, fout + g.n
        in_specs += g.in_specs
        out_specs += g.out_specs
        out_shape += g.out_shape
        scratch_shapes += g.scratch_shapes
        sem = tuple("arbitrary" for _ in sem)
    res = pl.pallas_call(_host_body(body, n_in, n_out, fused, first_last), name=name, grid=grid, in_specs=in_specs,
                         out_specs=out_specs, out_shape=out_shape, scratch_shapes=scratch_shapes,
                         input_output_aliases=aliases, compiler_params=_params(sem))(*operands)
    return list(res[:n_out]), list(res[n_out:])


def _forward_to_sibling(name, gathered):
    n_arr = len(gathered)

    def body(*refs):
        ins, outs = refs[:n_arr], refs[n_arr:2 * n_arr]
        send_sems, recv_sems = refs[2 * n_arr:]
        x, y, c = _mesh_pos()
        chips = [(1 - x, y), (x, 1 - y), (1 - x, 1 - y)]

        def copy(a, j, pc):
            px, py = chips[j]
            s = 4 * px + 2 * py + pc
            return pltpu.make_async_remote_copy(
                src_ref=ins[a].at[s], dst_ref=outs[a].at[s], send_sem=send_sems.at[a, j], recv_sem=recv_sems.at[a, j],
                device_id=(x, y, 1 - c), device_id_type=MESH)

        for a in range(n_arr):
            for j in range(3):
                copy(a, j, c).start()
        for a in range(n_arr):
            for j in range(3):
                copy(a, j, 1 - c).wait_recv()
                copy(a, j, c).wait_send()

    return pl.pallas_call(
        body, name=name,
        out_shape=[jax.ShapeDtypeStruct(g.shape, g.dtype) for g in gathered],
        in_specs=[HBM_SPEC] * n_arr, out_specs=[HBM_SPEC] * n_arr,
        input_output_aliases={a: a for a in range(n_arr)},
        scratch_shapes=[pltpu.SemaphoreType.DMA((n_arr, 3)), pltpu.SemaphoreType.DMA((n_arr, 3))],
    )(*gathered)


def _exchange_sibling(name, partials):
    n_arr = len(partials)

    def body(*refs):
        ins, outs = refs[:n_arr], refs[n_arr:2 * n_arr]
        send_sems, recv_sems = refs[2 * n_arr:]
        x, y, c = _mesh_pos()
        copies = [pltpu.make_async_remote_copy(
            src_ref=ins[a].at[2 * q + 1 - c], dst_ref=outs[a].at[q], send_sem=send_sems.at[a, q],
            recv_sem=recv_sems.at[a, q], device_id=(x, y, 1 - c), device_id_type=MESH)
            for a in range(n_arr) for q in range(4)]
        for cp in copies:
            cp.start()
        for cp in copies:
            cp.wait_recv()
        for cp in copies:
            cp.wait_send()

    return pl.pallas_call(
        body, name=name,
        out_shape=[jax.ShapeDtypeStruct((4,) + p.shape[1:], p.dtype) for p in partials],
        in_specs=[HBM_SPEC] * n_arr, out_specs=[HBM_SPEC] * n_arr,
        scratch_shapes=[pltpu.SemaphoreType.DMA((n_arr, 4)), pltpu.SemaphoreType.DMA((n_arr, 4))],
    )(*partials)


def _matmul(name, a, b, dims, grid, a_spec, b_spec, o_spec, out_shape, acc_axis=None, fused=None, fused_arrays=()):
    def body(a_ref, b_ref, o_ref):
        r = lax.dot_general(a_ref[...], b_ref[...], dims, preferred_element_type=F32)
        if acc_axis is None:
            o_ref[...] = r.astype(o_ref.dtype)
        else:
            k = pl.program_id(acc_axis)

            @pl.when(k == 0)
            def _():
                o_ref[...] = r

            @pl.when(k > 0)
            def _():
                o_ref[...] += r

    sem = tuple("arbitrary" if i == acc_axis else "parallel" for i in range(len(grid)))
    if fused is None:
        return pl.pallas_call(body, name=name, grid=grid, in_specs=[a_spec, b_spec], out_specs=o_spec,
                              out_shape=out_shape, compiler_params=_params(sem))(a, b)

    def first_last():
        first = last = None
        for ax, n in enumerate(grid):
            f, l = pl.program_id(ax) == 0, pl.program_id(ax) == n - 1
            first, last = (f, l) if first is None else (first & f, last & l)
        return first, last

    (out,), extra = _host_call(body, 2, 1, fused, first_last, name=name, grid=grid, in_specs=[a_spec, b_spec],
                               out_specs=[o_spec], out_shape=[out_shape], scratch_shapes=[], sem=sem,
                               operands=[a, b] + list(fused_arrays))
    return out, extra


def _mm_blocked_rhs(name, a, w_g, tm=512, fused=None, fused_arrays=()):
    m, k = a.shape
    nb = w_g.shape[2]
    return _matmul(name, a, w_g, NN, (N_DEV, m // tm),
                   pl.BlockSpec((tm, k), lambda j, i: (i, 0)),
                   pl.BlockSpec((None, k, nb), lambda j, i: (j, 0, 0)),
                   pl.BlockSpec((tm, nb), lambda j, i: (i, j)),
                   jax.ShapeDtypeStruct((m, N_DEV * nb), F32), fused=fused, fused_arrays=fused_arrays)


def _mm_blocked_rhs_t(name, a, w_g, tm=512, fused=None, fused_arrays=()):
    m = a.shape[0]
    n, nb = w_g.shape[1], w_g.shape[2]
    return _matmul(name, a, w_g, NT, (m // tm, N_DEV),
                   pl.BlockSpec((tm, nb), lambda i, j: (i, j)),
                   pl.BlockSpec((None, n, nb), lambda i, j: (j, 0, 0)),
                   pl.BlockSpec((tm, n), lambda i, j: (i, 0)),
                   jax.ShapeDtypeStruct((m, n), F32), acc_axis=1, fused=fused, fused_arrays=fused_arrays)


def _mm_wgrad_blocked(name, act, dcols, tk=512, fused=None, fused_arrays=()):
    t, k = act.shape
    nb = dcols.shape[1] // N_DEV
    return _matmul(name, act, dcols, TN, (N_DEV, k // tk),
                   pl.BlockSpec((t, tk), lambda j, i: (0, i)),
                   pl.BlockSpec((t, nb), lambda j, i: (0, j)),
                   pl.BlockSpec((None, tk, nb), lambda j, i: (j, i, 0)),
                   jax.ShapeDtypeStruct((N_DEV, k, nb), BF16), fused=fused, fused_arrays=fused_arrays)


def _halves_specs(block, index):
    half = N_DEV // 2
    return (pl.BlockSpec(block, lambda i, j: index(i, jnp.minimum(j, half - 1))),
            pl.BlockSpec(block, lambda i, j: index(i, jnp.maximum(j - half, 0))))


def _mm_halves_rhs_t(name, a_lo, a_hi, w_g, tm=512):
    m = a_lo.shape[0]
    n, nb = w_g.shape[1], w_g.shape[2]

    def body(lo_ref, hi_ref, b_ref, o_ref):
        j = pl.program_id(1)

        def accumulate(a_ref):
            r = lax.dot_general(a_ref[...], b_ref[...], NT, preferred_element_type=F32)

            @pl.when(j == 0)
            def _():
                o_ref[...] = r

            @pl.when(j > 0)
            def _():
                o_ref[...] += r

        pl.when(j < N_DEV // 2)(lambda: accumulate(lo_ref))
        pl.when(j >= N_DEV // 2)(lambda: accumulate(hi_ref))

    lo_spec, hi_spec = _halves_specs((tm, nb), lambda i, j: (i, j))
    return pl.pallas_call(
        body, name=name, grid=(m // tm, N_DEV),
        in_specs=[lo_spec, hi_spec, pl.BlockSpec((None, n, nb), lambda i, j: (j, 0, 0))],
        out_specs=pl.BlockSpec((tm, n), lambda i, j: (i, 0)), out_shape=jax.ShapeDtypeStruct((m, n), F32),
        compiler_params=_params(("parallel", "arbitrary")))(a_lo, a_hi, w_g)


def _mm_halves_wgrad(name, act, d_lo, d_hi, tk=512):
    t, k = act.shape
    nb = d_lo.shape[1] // (N_DEV // 2)

    def body(a_ref, lo_ref, hi_ref, o_ref):
        j = pl.program_id(0)

        def product(d_ref):
            o_ref[...] = lax.dot_general(a_ref[...], d_ref[...], TN, preferred_element_type=F32).astype(o_ref.dtype)

        pl.when(j < N_DEV // 2)(lambda: product(lo_ref))
        pl.when(j >= N_DEV // 2)(lambda: product(hi_ref))

    half = N_DEV // 2
    return pl.pallas_call(
        body, name=name, grid=(N_DEV, k // tk),
        in_specs=[pl.BlockSpec((t, tk), lambda j, i: (0, i)),
                  pl.BlockSpec((t, nb), lambda j, i: (0, jnp.minimum(j, half - 1))),
                  pl.BlockSpec((t, nb), lambda j, i: (0, jnp.maximum(j - half, 0)))],
        out_specs=pl.BlockSpec((None, tk, nb), lambda j, i: (j, i, 0)),
        out_shape=jax.ShapeDtypeStruct((N_DEV, k, nb), BF16),
        compiler_params=_params(("parallel", "parallel")))(act, d_lo, d_hi)


def _mm_plain(name, a, b, dims, tm, tn, out_dtype, fused=None, fused_arrays=()):
    if dims == NN:
        (m, k), n = a.shape, b.shape[1]
        a_spec = pl.BlockSpec((tm, k), lambda i, j: (i, 0))
        b_spec = pl.BlockSpec((k, tn), lambda i, j: (0, j))
    elif dims == NT:
        (m, k), n = a.shape, b.shape[0]
        a_spec = pl.BlockSpec((tm, k), lambda i, j: (i, 0))
        b_spec = pl.BlockSpec((tn, k), lambda i, j: (j, 0))
    else:
        (k, m), n = a.shape, b.shape[1]
        a_spec = pl.BlockSpec((k, tm), lambda i, j: (0, i))
        b_spec = pl.BlockSpec((k, tn), lambda i, j: (0, j))
    return _matmul(name, a, b, dims, (m // tm, n // tn), a_spec, b_spec,
                   pl.BlockSpec((tm, tn), lambda i, j: (i, j)), jax.ShapeDtypeStruct((m, n), out_dtype),
                   fused=fused, fused_arrays=fused_arrays)


def _ada_fwd(c_all, w_ada_blk, b_blk):
    def body(c_ref, w_ref, b_ref, o_ref):
        cv = c_ref[...]
        o_ref[...] = jnp.dot(cv * _sigmoid(cv), w_ref[...], preferred_element_type=F32) + b_ref[...]

    tn = 512
    return pl.pallas_call(
        body, name="ada_fwd", grid=(ADA_BLK // tn,),
        in_specs=[pl.BlockSpec((N_DEV, D_MODEL), lambda j: (0, 0)),
                  pl.BlockSpec((D_MODEL, tn), lambda j: (0, j)),
                  pl.BlockSpec((1, tn), lambda j: (0, j))],
        out_specs=pl.BlockSpec((N_DEV, tn), lambda j: (0, j)),
        out_shape=jax.ShapeDtypeStruct((N_DEV, ADA_BLK), F32),
        compiler_params=_params(("parallel",)))(c_all, w_ada_blk, b_blk)


def _ada_wgrad(c_all, gmod_cols):
    def body(c_ref, g_ref, o_ref):
        cv = c_ref[...]
        o_ref[...] = lax.dot_general(cv * _sigmoid(cv), g_ref[...], TN, preferred_element_type=F32)

    tk = 512
    return pl.pallas_call(
        body, name="ada_wgrad", grid=(D_MODEL // tk,),
        in_specs=[pl.BlockSpec((N_DEV, tk), lambda i: (0, i)),
                  pl.BlockSpec((N_DEV, ADA_BLK), lambda i: (0, 0))],
        out_specs=pl.BlockSpec((tk, ADA_BLK), lambda i: (i, 0)),
        out_shape=jax.ShapeDtypeStruct((D_MODEL, ADA_BLK), F32),
        compiler_params=_params(("parallel",)))(c_all, gmod_cols)


def _row_spec(cols=D_MODEL):
    return pl.BlockSpec((ROW_TILE, cols), lambda i: (i, 0))


def _vec_spec(cols=D_MODEL):
    return pl.BlockSpec((1, cols), lambda i: (0, 0))


def _norm_fwd(name, x, w, scale, shift, resid=None, gate=None):
    has_res = resid is not None

    def body(*refs):
        if has_res:
            x_ref, r_ref, g_ref, w_ref, sc_ref, sh_ref, xr_ref, h_ref, rs_ref = refs
            xr = x_ref[...] + g_ref[...] * r_ref[...]
            xr_ref[...] = xr
        else:
            x_ref, w_ref, sc_ref, sh_ref, h_ref, rs_ref = refs
            xr = x_ref[...]
        rs = lax.rsqrt(jnp.mean(xr * xr, axis=-1, keepdims=True) + EPS)
        h = (xr * rs) * w_ref[...] * (1.0 + sc_ref[...]) + sh_ref[...]
        h_ref[...] = h.astype(BF16)
        rs_ref[...] = rs

    s = x.shape[0]
    ins = [x] + ([resid, gate] if has_res else []) + [w, scale, shift]
    in_specs = [_row_spec()] + ([_row_spec(), _vec_spec()] if has_res else []) + [_vec_spec()] * 3
    outs = ([jax.ShapeDtypeStruct((s, D_MODEL), F32)] if has_res else []) + [
        jax.ShapeDtypeStruct((s, D_MODEL), BF16), jax.ShapeDtypeStruct((s, 1), F32)]
    out_specs = ([_row_spec()] if has_res else []) + [_row_spec(), pl.BlockSpec((ROW_TILE, 1), lambda i: (i, 0))]
    return pl.pallas_call(body, name=name, grid=(s // ROW_TILE,), in_specs=in_specs, out_specs=out_specs,
                          out_shape=outs, compiler_params=_params(("parallel",)))(*ins)


def _norm_bwd(name, dh, x, rstd, w, scale, dres, mix=None, gate=None, fused=None, fused_arrays=()):
    has_mix = mix is not None

    def body(*refs):
        if has_mix:
            (dh_ref, x_ref, rs_ref, w_ref, sc_ref, dr_ref, mix_ref, g_ref,
             dx_ref, dmix_ref, dsh_ref, dsc_ref, dw_ref, dg_ref) = refs
        else:
            dh_ref, x_ref, rs_ref, w_ref, sc_ref, dr_ref, dx_ref, dsh_ref, dsc_ref, dw_ref = refs
        i = pl.program_id(0)
        dhv = dh_ref[...]
        rs = rs_ref[...]
        xn = x_ref[...] * rs
        wv = w_ref[...]
        one_sc = 1.0 + sc_ref[...]
        dxn = dhv * wv * one_sc
        dx = dr_ref[...] + rs * (dxn - xn * jnp.mean(dxn * xn, axis=-1, keepdims=True))
        dx_ref[...] = dx
        sums = [(dsh_ref, dhv), (dsc_ref, dhv * xn * wv), (dw_ref, dhv * one_sc * xn)]
        if has_mix:
            dmix_ref[...] = (dx * g_ref[...]).astype(BF16)
            sums.append((dg_ref, dx * mix_ref[...]))

        @pl.when(i == 0)
        def _():
            for ref, _v in sums:
                ref[...] = jnp.zeros_like(ref)

        for ref, v in sums:
            ref[...] += jnp.sum(v, axis=0, keepdims=True)

    s = x.shape[0]
    ins = [dh, x, rstd, w, scale, dres] + ([mix, gate] if has_mix else [])
    in_specs = ([_row_spec(), _row_spec(), pl.BlockSpec((ROW_TILE, 1), lambda i: (i, 0)), _vec_spec(), _vec_spec(),
                 _row_spec()] + ([_row_spec(), _vec_spec()] if has_mix else []))
    vec = jax.ShapeDtypeStruct((1, D_MODEL), F32)
    outs = ([jax.ShapeDtypeStruct((s, D_MODEL), F32)] + ([jax.ShapeDtypeStruct((s, D_MODEL), BF16)] if has_mix else [])
            + [vec] * (4 if has_mix else 3))
    out_specs = [_row_spec()] + ([_row_spec()] if has_mix else []) + [_vec_spec()] * (4 if has_mix else 3)

    def first_last():
        i = pl.program_id(0)
        return i == 0, i == s // ROW_TILE - 1

    res, extra = _host_call(body, len(ins), len(outs), fused, first_last, name=name, grid=(s // ROW_TILE,),
                            in_specs=in_specs, out_specs=out_specs, out_shape=outs, scratch_shapes=[],
                            sem=("arbitrary",), operands=ins + list(fused_arrays))
    return res if fused is None else (res, extra)


def _loss_head(x1, ffn, gate2, target):
    def body(x_ref, f_ref, g_ref, t_ref, loss_ref, dout_ref, dffn_ref, dg_ref):
        i = pl.program_id(0)
        fv = f_ref[...]
        gv = g_ref[...]
        err = x_ref[...] + gv * fv - t_ref[...]
        dout = err * (1.0 / D_MODEL)
        dout_ref[...] = dout
        dffn_ref[...] = (dout * gv).astype(BF16)

        @pl.when(i == 0)
        def _():
            loss_ref[...] = jnp.zeros_like(loss_ref)
            dg_ref[...] = jnp.zeros_like(dg_ref)

        row = jnp.sum(err * err, axis=-1, keepdims=True) * (1.0 / D_MODEL)
        loss_ref[...] += jnp.broadcast_to(0.5 * jnp.sum(row, axis=0, keepdims=True), (1, 128))
        dg_ref[...] += jnp.sum(dout * fv, axis=0, keepdims=True)

    s = x1.shape[0]
    return pl.pallas_call(
        body, name="loss_head", grid=(s // ROW_TILE,),
        in_specs=[_row_spec(), _row_spec(), _vec_spec(), _row_spec()],
        out_specs=[pl.BlockSpec((1, 128), lambda i: (0, 0)), _row_spec(), _row_spec(), _vec_spec()],
        out_shape=[jax.ShapeDtypeStruct((1, 128), F32), jax.ShapeDtypeStruct((s, D_MODEL), F32),
                   jax.ShapeDtypeStruct((s, D_MODEL), BF16), jax.ShapeDtypeStruct((1, D_MODEL), F32)],
        compiler_params=_params(("arbitrary",)))(x1, ffn, gate2, target)


CONV_TILE = 512
N_CONV_TILES = D_FF // CONV_TILE


def _shift_rows(a, k, row):
    n = a.shape[0]
    if k > 0:
        return jnp.where(row >= k, pltpu.roll(a, k, 0), 0.0)
    return jnp.where(row < n + k, pltpu.roll(a, n + k, 0), 0.0)


def _conv_gate_fwd(u, conv_w, conv_b, fused=None, fused_arrays=()):
    s = u.shape[0]

    def body(a_ref, g_ref, w_ref, b_ref, y_ref):
        a = a_ref[...]
        w = w_ref[...]
        row = lax.broadcasted_iota(jnp.int32, a.shape, 0)
        ac = b_ref[...] + _shift_rows(a, 2, row) * w[0:1] + _shift_rows(a, 1, row) * w[1:2] + a * w[2:3]
        y_ref[...] = (ac * _sigmoid(ac) * g_ref[...]).astype(BF16)

    def first_last():
        i = pl.program_id(0)
        return i == 0, i == N_CONV_TILES - 1

    col = lambda off: pl.BlockSpec((s, CONV_TILE), lambda i: (0, i + off))
    (y,), extra = _host_call(
        body, 4, 1, fused, first_last, name="conv_gate_fwd", grid=(N_CONV_TILES,),
        in_specs=[col(0), col(N_CONV_TILES), pl.BlockSpec((3, CONV_TILE), lambda i: (0, i)),
                  pl.BlockSpec((1, CONV_TILE), lambda i: (0, i))],
        out_specs=[col(0)], out_shape=[jax.ShapeDtypeStruct((s, D_FF), BF16)], scratch_shapes=[], sem=("parallel",),
        operands=[u, u, conv_w, conv_b] + list(fused_arrays))
    return y if fused is None else (y, extra)


def _conv_gate_bwd(u, dy, conv_w, conv_b):
    s = u.shape[0]

    def body(a_ref, g_ref, dy_ref, w_ref, b_ref, da_ref, dg_ref, gw_ref, gb_ref):
        a = a_ref[...]
        w = w_ref[...]
        row = lax.broadcasted_iota(jnp.int32, a.shape, 0)
        a1 = _shift_rows(a, 1, row)
        a2 = _shift_rows(a, 2, row)
        ac = b_ref[...] + a2 * w[0:1] + a1 * w[1:2] + a * w[2:3]
        sg = _sigmoid(ac)
        dyv = dy_ref[...].astype(F32)
        dg_ref[...] = (dyv * (ac * sg)).astype(BF16)
        dac = dyv * g_ref[...] * _dsilu(ac, sg)
        gb_ref[...] = jnp.sum(dac, axis=0, keepdims=True)
        gw_ref[0:1, :] = jnp.sum(dac * a2, axis=0, keepdims=True)
        gw_ref[1:2, :] = jnp.sum(dac * a1, axis=0, keepdims=True)
        gw_ref[2:3, :] = jnp.sum(dac * a, axis=0, keepdims=True)
        da = dac * w[2:3] + _shift_rows(dac, -1, row) * w[1:2] + _shift_rows(dac, -2, row) * w[0:1]
        da_ref[...] = da.astype(BF16)

    col = lambda off: pl.BlockSpec((s, CONV_TILE), lambda i: (0, i + off))
    return pl.pallas_call(
        body, name="conv_gate_bwd", grid=(N_CONV_TILES,),
        in_specs=[col(0), col(N_CONV_TILES), col(0), pl.BlockSpec((3, CONV_TILE), lambda i: (0, i)),
                  pl.BlockSpec((1, CONV_TILE), lambda i: (0, i))],
        out_specs=[col(0), col(0), pl.BlockSpec((3, CONV_TILE), lambda i: (0, i)),
                   pl.BlockSpec((1, CONV_TILE), lambda i: (0, i))],
        out_shape=[jax.ShapeDtypeStruct((s, D_FF), BF16), jax.ShapeDtypeStruct((s, D_FF), BF16),
                   jax.ShapeDtypeStruct((3, D_FF), F32), jax.ShapeDtypeStruct((1, D_FF), F32)],
        compiler_params=_params(("parallel",)))(u, u, dy, conv_w, conv_b)


HG_TILE = 256
CHUNK_UNROLL = 8


def _unrolled_loop(n, body, init):
    def group(i, carry):
        for u in range(CHUNK_UNROLL):
            carry = body(i * CHUNK_UNROLL + u, carry)
        return carry

    return lax.fori_loop(0, n // CHUNK_UNROLL, group, init)


def _head_col(off):
    return pl.BlockSpec((SEQ, HEAD_DIM), lambda h: (0, h + off))


def _hgrn_gates(hq, hf, lb, pos):
    q = hq * _sigmoid(hq)
    sig = _sigmoid(hf)
    f = lb + (1.0 - lb) * sig
    gl = jnp.log(f)
    for sh in (1, 2, 4, 8):
        gl = gl + jnp.where(pos >= sh, pltpu.roll(gl, sh, 0), 0.0)
    return q, sig, f, 1.0 - f, gl


def _lower_bound(lbl):
    return 1.0 / (1.0 + jnp.exp(lbl[1:2, :] - lbl[0:1, :]))


def _head_first_last():
    h = pl.program_id(0)
    return h == 0, h == HEADS - 1


CHUNKS_PER_TILE = HG_TILE // CHUNK


def _chunk_end(x, pos):
    y = jnp.where(pos == CHUNK - 1, x, 0.0)
    for sh in (1, 2, 4, 8):
        y = y + jnp.where(pos < CHUNK - sh, pltpu.roll(y, x.shape[0] - sh, 0), 0.0)
    return y


def _suffix_in_chunk(x, pos):
    for sh in (1, 2, 4, 8):
        x = x + jnp.where(pos < CHUNK - sh, pltpu.roll(x, x.shape[0] - sh, 0), 0.0)
    return x


def _prefix_in_chunk(x, pos):
    for sh in (1, 2, 4, 8):
        x = x + jnp.where(pos >= sh, pltpu.roll(x, sh, 0), 0.0)
    return x


def _pair_decays(f, pos):
    shifted = jnp.where(pos >= 1, f, 0.0)
    e = shifted
    yield 1, e
    for d in range(2, CHUNK):
        shifted = pltpu.roll(shifted, 1, 0)
        e = e * shifted
        yield d, e


def _chunk_rows(cc):
    return slice(cc * CHUNK, (cc + 1) * CHUNK)


def _outer_products(lhs_b, rhs_b, dst, i):
    for cc in range(CHUNKS_PER_TILE):
        dst[i * CHUNKS_PER_TILE + cc] = lax.dot_general(lhs_b[_chunk_rows(cc)], rhs_b[_chunk_rows(cc)], TN,
                                                        preferred_element_type=F32)


def _state_scan(n_chunks, gl_s, u_s, keep, reverse):
    def step(k, st):
        c = n_chunks - 1 - k if reverse else k
        keep[c] = st.astype(BF16)
        gl = gl_s[pl.ds(pl.multiple_of(c * CHUNK, CHUNK), CHUNK), :]
        return st * jnp.exp(gl[CHUNK - 1:CHUNK, :]) + u_s[c]

    _unrolled_loop(n_chunks, step, jnp.zeros((HEAD_DIM, HEAD_DIM), F32))


def _hgrn_fwd(proj, lb_logits, norm_w, fused=None, fused_arrays=()):
    n_tiles = SEQ // HG_TILE
    n_chunks = SEQ // CHUNK
    fused_arrays = list(fused_arrays)

    def body(hq_ref, hf_ref, hi_ref, hg_ref, lbl_ref, nw_ref, aout_ref, opre_ref, qt_s, gl_s, u_s, st_s):
        lb = _lower_bound(lbl_ref[...])
        ones = jnp.ones((HEAD_DIM, HEAD_DIM), BF16)
        pos = lax.broadcasted_iota(jnp.int32, (HG_TILE, HEAD_DIM), 0) % CHUNK

        def tile(i, carry):
            rows = pl.ds(pl.multiple_of(i * HG_TILE, HG_TILE), HG_TILE)
            v = hi_ref[rows, :]
            q, _sig, f, kk, gl = _hgrn_gates(hq_ref[rows, :], hf_ref[rows, :], lb, pos)
            o = _lane_sum(q * kk, ones) * v
            for d, e in _pair_decays(f, pos):
                o = o + _lane_sum(q * pltpu.roll(kk, d, 0) * e, ones) * pltpu.roll(v, d, 0)
            opre_ref[rows, :] = o
            qt_s[rows, :] = q * jnp.exp(gl)
            gl_s[rows, :] = gl
            kt = kk * jnp.exp(_chunk_end(gl, pos) - gl)
            _outer_products(v.astype(BF16), kt.astype(BF16), u_s, i)
            return carry

        lax.fori_loop(0, n_tiles, tile, 0)
        _state_scan(n_chunks, gl_s, u_s, st_s, reverse=False)

        def finish(i, carry):
            rows = pl.ds(pl.multiple_of(i * HG_TILE, HG_TILE), HG_TILE)
            qt_b = qt_s[rows, :].astype(BF16)
            past = [lax.dot_general(qt_b[_chunk_rows(cc)], st_s[i * CHUNKS_PER_TILE + cc], NT,
                                    preferred_element_type=F32) for cc in range(CHUNKS_PER_TILE)]
            o = opre_ref[rows, :] + jnp.concatenate(past, axis=0)
            opre_ref[rows, :] = o
            hg = hg_ref[rows, :]
            rs = lax.rsqrt(jnp.mean(o * o, axis=-1, keepdims=True) + EPS)
            aout_ref[rows, :] = ((o * rs) * nw_ref[...] * (hg * _sigmoid(hg))).astype(BF16)
            return carry

        lax.fori_loop(0, n_tiles, finish, 0)

    return _host_call(
        body, 6, 2, fused, _head_first_last, name="hgrn_fwd", grid=(HEADS,),
        in_specs=[_head_col(0), _head_col(HEADS), _head_col(2 * HEADS), _head_col(3 * HEADS),
                  pl.BlockSpec((2, HEAD_DIM), lambda h: (0, h)), pl.BlockSpec((1, HEAD_DIM), lambda h: (0, 0))],
        out_specs=[_head_col(0), _head_col(0)],
        out_shape=[jax.ShapeDtypeStruct((SEQ, HEADS * HEAD_DIM), BF16), jax.ShapeDtypeStruct((SEQ, HEADS * HEAD_DIM), F32)],
        scratch_shapes=[pltpu.VMEM((SEQ, HEAD_DIM), F32)] * 2 + [pltpu.VMEM((n_chunks, HEAD_DIM, HEAD_DIM), F32),
                                                                 pltpu.VMEM((n_chunks, HEAD_DIM, HEAD_DIM), BF16)],
        sem=("parallel",), operands=[proj, proj, proj, proj, lb_logits, norm_w] + fused_arrays)


def _hgrn_bwd(proj, lb_logits, norm_w, o_pre, d_aout, fused=None, fused_arrays=()):
    n_tiles = SEQ // HG_TILE
    n_chunks = SEQ // CHUNK

    def body(hq_ref, hf_ref, hi_ref, hg_ref, lbl_ref, nw_ref, opre_ref, da_ref,
             dhq_ref, dhf_ref, dhi_ref, dhg_ref, dlog_ref, gnw_ref,
             q_s, k_s, gl_s, do_s, dq_s, dk_s, dv_s, u_s, st_s, rt_s):
        h = pl.program_id(0)
        lb = _lower_bound(lbl_ref[...])
        nw = nw_ref[...]
        ones = jnp.ones((HEAD_DIM, HEAD_DIM), BF16)
        pos = lax.broadcasted_iota(jnp.int32, (HG_TILE, HEAD_DIM), 0) % CHUNK

        @pl.when(h == 0)
        def _():
            gnw_ref[...] = jnp.zeros_like(gnw_ref)

        def tile(i, carry):
            rows = pl.ds(pl.multiple_of(i * HG_TILE, HG_TILE), HG_TILE)
            v = hi_ref[rows, :]
            q, _sig, f, kk, gl = _hgrn_gates(hq_ref[rows, :], hf_ref[rows, :], lb, pos)
            o = opre_ref[rows, :]
            hg = hg_ref[rows, :]
            da = da_ref[rows, :]
            rs = lax.rsqrt(jnp.mean(o * o, axis=-1, keepdims=True) + EPS)
            oh = o * rs
            sg = _sigmoid(hg)
            dnorm = da * (hg * sg)
            dhg_ref[rows, :] = (da * (oh * nw) * _dsilu(hg, sg)).astype(BF16)
            gnw_ref[...] += jnp.sum(dnorm * oh, axis=0, keepdims=True)
            doh = dnorm * nw
            do = rs * (doh - oh * jnp.mean(doh * oh, axis=-1, keepdims=True))

            d_a = _lane_sum(do * v, ones)
            dq = d_a * kk
            dk = d_a * q
            dv = _lane_sum(q * kk, ones) * do
            for d, e in _pair_decays(f, pos):
                ks = pltpu.roll(kk, d, 0)
                a_d = _lane_sum(q * ks * e, ones)
                d_a = _lane_sum(do * pltpu.roll(v, d, 0), ones) * e
                dq = dq + d_a * ks
                dk = dk + pltpu.roll(d_a * q, HG_TILE - d, 0)
                dv = dv + pltpu.roll(a_d * do, HG_TILE - d, 0)
            q_s[rows, :] = q
            k_s[rows, :] = kk
            gl_s[rows, :] = gl
            do_s[rows, :] = do
            dq_s[rows, :] = dq
            dk_s[rows, :] = dk
            dv_s[rows, :] = dv
            kt = kk * jnp.exp(_chunk_end(gl, pos) - gl)
            _outer_products(v.astype(BF16), kt.astype(BF16), u_s, i)
            return carry

        lax.fori_loop(0, n_tiles, tile, 0)
        _state_scan(n_chunks, gl_s, u_s, st_s, reverse=False)

        def reverse_increments(i, carry):
            rows = pl.ds(pl.multiple_of(i * HG_TILE, HG_TILE), HG_TILE)
            qt = q_s[rows, :] * jnp.exp(gl_s[rows, :])
            _outer_products(do_s[rows, :].astype(BF16), qt.astype(BF16), u_s, i)
            return carry

        lax.fori_loop(0, n_tiles, reverse_increments, 0)
        _state_scan(n_chunks, gl_s, u_s, rt_s, reverse=True)

        def finish(i, dlb):
            rows = pl.ds(pl.multiple_of(i * HG_TILE, HG_TILE), HG_TILE)
            q = q_s[rows, :]
            kk = k_s[rows, :]
            gl = gl_s[rows, :]
            gll = _chunk_end(gl, pos)
            ekt = jnp.exp(gll - gl)
            do_b = do_s[rows, :].astype(BF16)
            v_b = hi_ref[rows, :].astype(BF16)
            kt_b = (kk * ekt).astype(BF16)
            dq_far, dk_far, dv_far, across = [], [], [], []
            for cc in range(CHUNKS_PER_TILE):
                st = st_s[i * CHUNKS_PER_TILE + cc]
                rt = rt_s[i * CHUNKS_PER_TILE + cc]
                sl = _chunk_rows(cc)
                dq_far.append(jnp.dot(do_b[sl], st, preferred_element_type=F32))
                dk_far.append(jnp.dot(v_b[sl], rt, preferred_element_type=F32))
                dv_far.append(lax.dot_general(kt_b[sl], rt, NT, preferred_element_type=F32))
                both = jnp.sum(st.astype(F32) * rt.astype(F32), axis=0, keepdims=True)
                across.append(jnp.broadcast_to(both, (CHUNK, HEAD_DIM)))
            dq = dq_s[rows, :] + jnp.concatenate(dq_far, axis=0) * jnp.exp(gl)
            dk_in = dk_s[rows, :]
            dk_out = jnp.concatenate(dk_far, axis=0) * ekt
            dk = dk_in + dk_out
            dv = dv_s[rows, :] + jnp.concatenate(dv_far, axis=0)
            pc = kk * dk_out
            dgl = (_suffix_in_chunk(q * dq - kk * dk_in, pos) + (_prefix_in_chunk(pc, pos) - pc)
                   + jnp.concatenate(across, axis=0) * jnp.exp(gll))
            hf = hf_ref[rows, :]
            sig = _sigmoid(hf)
            f = lb + (1.0 - lb) * sig
            df = dgl / f - dk
            dhf_ref[rows, :] = (df * (1.0 - lb) * sig * (1.0 - sig)).astype(BF16)
            hq = hq_ref[rows, :]
            dhq_ref[rows, :] = (dq * _dsilu(hq, _sigmoid(hq))).astype(BF16)
            dhi_ref[rows, :] = dv.astype(BF16)
            return dlb + jnp.sum(df * (1.0 - sig), axis=0, keepdims=True)

        dlb = lax.fori_loop(0, n_tiles, finish, jnp.zeros((1, HEAD_DIM), F32))
        dl0 = lb * (1.0 - lb) * dlb
        dlog_ref[0:1, :] = dl0
        dlog_ref[1:2, :] = -dl0

    wide = HEADS * HEAD_DIM
    return _host_call(
        body, 8, 6, fused, _head_first_last, name="hgrn_bwd", grid=(HEADS,),
        in_specs=[_head_col(0), _head_col(HEADS), _head_col(2 * HEADS), _head_col(3 * HEADS),
                  pl.BlockSpec((2, HEAD_DIM), lambda h: (0, h)), pl.BlockSpec((1, HEAD_DIM), lambda h: (0, 0)),
                  _head_col(0), _head_col(0)],
        out_specs=[_head_col(0)] * 4 + [pl.BlockSpec((2, HEAD_DIM), lambda h: (0, h)),
                                        pl.BlockSpec((1, HEAD_DIM), lambda h: (0, 0))],
        out_shape=[jax.ShapeDtypeStruct((SEQ, wide), BF16)] * 4 + [jax.ShapeDtypeStruct((2, wide), F32),
                                                                    jax.ShapeDtypeStruct((1, HEAD_DIM), F32)],
        scratch_shapes=[pltpu.VMEM((SEQ, HEAD_DIM), F32)] * 7 + [pltpu.VMEM((n_chunks, HEAD_DIM, HEAD_DIM), F32),
                                                                 pltpu.VMEM((n_chunks, HEAD_DIM, HEAD_DIM), BF16),
                                                                 pltpu.VMEM((n_chunks, HEAD_DIM, HEAD_DIM), BF16)],
        sem=("arbitrary",),
        operands=[proj, proj, proj, proj, lb_logits, norm_w, o_pre, d_aout] + list(fused_arrays))


Q_TILE = 512
ATT_SCALE = HEAD_DIM ** -0.5
ATT_OFF = 4 * HEADS


def _qk_prep(proj, q_w, k_w, fused=None, fused_arrays=()):
    def body(aq_ref, ak_ref, av_ref, qw_ref, kw_ref, qn_ref, kn_ref, v_ref):
        aq = aq_ref[...]
        ak = ak_ref[...]
        qn_ref[...] = (aq * lax.rsqrt(jnp.mean(aq * aq, axis=-1, keepdims=True) + EPS) * qw_ref[...]).astype(BF16)
        kn_ref[...] = (ak * lax.rsqrt(jnp.mean(ak * ak, axis=-1, keepdims=True) + EPS) * kw_ref[...]).astype(BF16)
        v_ref[...] = av_ref[...].astype(BF16)

    wide = HEADS * HEAD_DIM
    vec = pl.BlockSpec((1, HEAD_DIM), lambda h: (0, 0))
    return _host_call(
        body, 5, 3, fused, _head_first_last, name="qk_prep", grid=(HEADS,),
        in_specs=[_head_col(ATT_OFF), _head_col(ATT_OFF + HEADS), _head_col(ATT_OFF + 2 * HEADS), vec, vec],
        out_specs=[_head_col(0)] * 3, out_shape=[jax.ShapeDtypeStruct((SEQ, wide), BF16)] * 3,
        scratch_shapes=[], sem=("parallel",), operands=[proj, proj, proj, q_w, k_w] + list(fused_arrays))


def _alibi_slopes():
    slopes = jnp.exp2(-8.0 * jnp.arange(1, HEADS + 1, dtype=F32) / HEADS)
    return jnp.broadcast_to(slopes[:, None, None], (HEADS, 1, HEAD_DIM))


SLOPE_SPEC = pl.BlockSpec((None, 1, HEAD_DIM), lambda h, i: (h, 0, 0))


N_Q_TILES = SEQ // Q_TILE
K_BLOCK = 512
NOT_ATTENDED = 1e35


def _att_tables():
    o = jnp.arange(N_Q_TILES, dtype=jnp.int32)[:, None, None]
    r = jnp.arange(Q_TILE, dtype=jnp.int32)[None, :, None]
    c = jnp.arange(K_BLOCK, dtype=jnp.int32)[None, None, :]
    dist = o * Q_TILE + r - c
    mult = ((dist <= 128).astype(F32) + (((dist % 4) == 0) & (dist <= 512)).astype(F32)
            + ((dist % 16) == 0).astype(F32))
    valid = (dist >= 0) & (mult > 0)
    return (jnp.where(valid, dist.astype(F32), NOT_ATTENDED),
            jnp.where(valid, jnp.log(jnp.maximum(mult, 1.0)), 0.0))


TABLE_SPEC = pl.BlockSpec((N_Q_TILES, Q_TILE, K_BLOCK), lambda h, i: (0, 0, 0))


def _att_block(q, k_ref, j, i, slope, dist_ref, lmul_ref):
    rows = pl.ds(pl.multiple_of(j * K_BLOCK, K_BLOCK), K_BLOCK)
    off = i - j * (K_BLOCK // Q_TILE)
    s = lax.dot_general(q, k_ref[rows, :], NT, preferred_element_type=F32) * ATT_SCALE
    return s - slope * dist_ref[off] + lmul_ref[off], rows


def _n_key_blocks(i):
    return (i + K_BLOCK // Q_TILE) // (K_BLOCK // Q_TILE)


def _att_first_last():
    h, i = pl.program_id(0), pl.program_id(1)
    return (h == 0) & (i == 0), (h == HEADS - 1) & (i == N_Q_TILES - 1)


def _attn_fwd(qn, kn, vb, fused=None, fused_arrays=()):
    def body(q_ref, k_ref, v_ref, sl_ref, dist_ref, lmul_ref, o_ref, lse_ref):
        i = pl.program_id(1)
        q = q_ref[...]
        slope = sl_ref[0:1, 0:1]

        def step(j, carry):
            m, l, acc = carry
            sb, rows = _att_block(q, k_ref, j, i, slope, dist_ref, lmul_ref)
            m_new = jnp.maximum(m, jnp.max(sb, axis=-1, keepdims=True))
            alpha = jnp.exp(m - m_new)
            p = jnp.exp(sb - m_new)
            l = alpha * l + jnp.sum(p, axis=-1, keepdims=True)
            acc = alpha * acc + jnp.dot(p.astype(BF16), v_ref[rows, :], preferred_element_type=F32)
            return m_new, l, acc

        m, l, acc = lax.fori_loop(0, _n_key_blocks(i), step,
                                  (jnp.full((Q_TILE, 1), -1e30, F32), jnp.zeros((Q_TILE, 1), F32),
                                   jnp.zeros((Q_TILE, HEAD_DIM), F32)))
        o_ref[...] = acc / l
        lse_ref[...] = m + jnp.log(l)

    wide = HEADS * HEAD_DIM
    qt = pl.BlockSpec((Q_TILE, HEAD_DIM), lambda h, i: (i, h))
    full = pl.BlockSpec((SEQ, HEAD_DIM), lambda h, i: (0, h))
    return _host_call(
        body, 6, 2, fused, _att_first_last, name="attn_fwd", grid=(HEADS, N_Q_TILES),
        in_specs=[qt, full, full, SLOPE_SPEC, TABLE_SPEC, TABLE_SPEC],
        out_specs=[qt, pl.BlockSpec((None, Q_TILE, 1), lambda h, i: (h, i, 0))],
        out_shape=[jax.ShapeDtypeStruct((SEQ, wide), F32), jax.ShapeDtypeStruct((HEADS, SEQ, 1), F32)],
        scratch_shapes=[], sem=("parallel", "parallel"),
        operands=[qn, kn, vb, _alibi_slopes(), *_att_tables()] + list(fused_arrays))


def _attn_bwd(qn, kn, vb, o, lse, d_mix, fused=None, fused_arrays=()):
    def body(q_ref, k_ref, v_ref, o_ref, lse_ref, do_ref, sl_ref, dist_ref, lmul_ref, dq_ref, dk_ref, dv_ref):
        i = pl.program_id(1)
        q = q_ref[...]
        do = do_ref[...]
        do_b = do.astype(BF16)
        slope = sl_ref[0:1, 0:1]
        lse = lse_ref[...]
        delta = jnp.sum(do * o_ref[...], axis=-1, keepdims=True)

        @pl.when(i == 0)
        def _():
            dk_ref[...] = jnp.zeros_like(dk_ref)
            dv_ref[...] = jnp.zeros_like(dv_ref)

        def step(j, dq):
            sb, rows = _att_block(q, k_ref, j, i, slope, dist_ref, lmul_ref)
            p = jnp.exp(sb - lse)
            dp = lax.dot_general(do_b, v_ref[rows, :], NT, preferred_element_type=F32)
            ds = (p * (dp - delta)).astype(BF16)
            dk_ref[rows, :] += lax.dot_general(ds, q, TN, preferred_element_type=F32) * ATT_SCALE
            dv_ref[rows, :] += lax.dot_general(p.astype(BF16), do_b, TN, preferred_element_type=F32)
            return dq + jnp.dot(ds, k_ref[rows, :], preferred_element_type=F32)

        dq = lax.fori_loop(0, _n_key_blocks(i), step, jnp.zeros((Q_TILE, HEAD_DIM), F32))
        dq_ref[...] = dq * ATT_SCALE

    wide = HEADS * HEAD_DIM
    qt = pl.BlockSpec((Q_TILE, HEAD_DIM), lambda h, i: (i, h))
    full = pl.BlockSpec((SEQ, HEAD_DIM), lambda h, i: (0, h))
    return _host_call(
        body, 9, 3, fused, _att_first_last, name="attn_bwd", grid=(HEADS, N_Q_TILES),
        in_specs=[qt, full, full, qt, pl.BlockSpec((None, Q_TILE, 1), lambda h, i: (h, i, 0)),
                  pl.BlockSpec((Q_TILE, HEAD_DIM), lambda h, i: (i, h + HEADS)), SLOPE_SPEC, TABLE_SPEC, TABLE_SPEC],
        out_specs=[qt, full, full], out_shape=[jax.ShapeDtypeStruct((SEQ, wide), F32)] * 3,
        scratch_shapes=[], sem=("parallel", "arbitrary"),
        operands=[qn, kn, vb, o, lse, d_mix, _alibi_slopes(), *_att_tables()] + list(fused_arrays))


def _qk_bwd(proj, q_w, k_w, dqn, dkn, dv):
    def body(aq_ref, ak_ref, qw_ref, kw_ref, dqn_ref, dkn_ref, dv_ref, daq_ref, dak_ref, dav_ref, gq_ref, gk_ref):
        h = pl.program_id(0)

        @pl.when(h == 0)
        def _():
            gq_ref[...] = jnp.zeros_like(gq_ref)
            gk_ref[...] = jnp.zeros_like(gk_ref)

        def one(a_ref, w_ref, d_ref, da_ref, g_ref):
            a = a_ref[...]
            d = d_ref[...]
            rs = lax.rsqrt(jnp.mean(a * a, axis=-1, keepdims=True) + EPS)
            ah = a * rs
            g_ref[...] += jnp.sum(d * ah, axis=0, keepdims=True)
            dah = d * w_ref[...]
            da_ref[...] = (rs * (dah - ah * jnp.mean(dah * ah, axis=-1, keepdims=True))).astype(BF16)

        one(aq_ref, qw_ref, dqn_ref, daq_ref, gq_ref)
        one(ak_ref, kw_ref, dkn_ref, dak_ref, gk_ref)
        dav_ref[...] = dv_ref[...].astype(BF16)

    wide = HEADS * HEAD_DIM
    vec = pl.BlockSpec((1, HEAD_DIM), lambda h: (0, 0))
    return pl.pallas_call(
        body, name="qk_bwd", grid=(HEADS,),
        in_specs=[_head_col(ATT_OFF), _head_col(ATT_OFF + HEADS), vec, vec, _head_col(0), _head_col(0), _head_col(0)],
        out_specs=[_head_col(0)] * 3 + [vec, vec],
        out_shape=[jax.ShapeDtypeStruct((SEQ, wide), BF16)] * 3 + [jax.ShapeDtypeStruct((1, HEAD_DIM), F32)] * 2,
        compiler_params=_params(("arbitrary",)))(proj, proj, q_w, k_w, dqn, dkn, dv)


def _pair_sum(name, partial, theirs, core):
    _, r, c = theirs.shape
    tr = r // 2 if r % 16 == 0 else r

    def body(core_ref, a_ref, b_ref, o_ref):
        o_ref[...] = (a_ref[...].astype(F32) + b_ref[...].astype(F32)).astype(BF16)

    spec = pl.BlockSpec((None, tr, c), lambda q, i, core_ref: (q, i, 0))
    grid_spec = pltpu.PrefetchScalarGridSpec(
        num_scalar_prefetch=1, grid=(4, r // tr),
        in_specs=[pl.BlockSpec((None, tr, c), lambda q, i, core_ref: (2 * q + core_ref[0], i, 0)), spec],
        out_specs=spec)
    return pl.pallas_call(body, name=name, grid_spec=grid_spec, out_shape=jax.ShapeDtypeStruct(theirs.shape, BF16),
                          compiler_params=_params(("parallel", "parallel")))(core, partial, theirs)


def _adamw_step(w, m, v, g):
    nm = ADAM_B1 * m + (1.0 - ADAM_B1) * g
    nv = ADAM_B2 * v + (1.0 - ADAM_B2) * (g * g)
    m_hat = nm / (1.0 - ADAM_B1 ** ADAM_STEP)
    v_hat = nv / (1.0 - ADAM_B2 ** ADAM_STEP)
    return -ADAM_LR * (m_hat / (jnp.sqrt(v_hat) + ADAM_EPS) + ADAM_WD * w), nm, nv


def _adamw(name, w, m, v, addends, tr=None, fused=None, fused_arrays=()):
    r, c = w.shape
    tr = r if tr is None else tr
    n_add = len(addends)

    def body(*refs):
        w_ref, m_ref, v_ref = refs[:3]
        add_refs = refs[3:3 + n_add]
        g_ref, d_ref, nm_ref, nv_ref = refs[3 + n_add:]
        g = add_refs[0][...].astype(F32)
        for a_ref in add_refs[1:]:
            g = g + a_ref[...].astype(F32)
        g_ref[...] = g
        d_ref[...], nm_ref[...], nv_ref[...] = _adamw_step(w_ref[...], m_ref[...], v_ref[...], g)

    def first_last():
        i = pl.program_id(0)
        return i == 0, i == r // tr - 1

    spec = pl.BlockSpec((tr, c), lambda i: (i, 0))
    out = jax.ShapeDtypeStruct((r, c), F32)
    res, extra = _host_call(body, 3 + n_add, 4, fused, first_last, name=name, grid=(r // tr,),
                            in_specs=[spec] * (3 + n_add), out_specs=[spec] * 4, out_shape=[out] * 4,
                            scratch_shapes=[], sem=("parallel",), operands=[w, m, v, *addends] + list(fused_arrays))
    return res if fused is None else (res, extra)


def _adamw_reduced(name, w, m, v, chip_sums, received, chip, tr):
    r, c = w.shape

    def body(chip_ref, w_ref, m_ref, v_ref, own_ref, r0_ref, r1_ref, r2_ref, g_ref, d_ref, nm_ref, nv_ref):
        g = ((own_ref[...].astype(F32) + r0_ref[...].astype(F32)) + r1_ref[...].astype(F32)) + r2_ref[...].astype(F32)
        g_ref[...] = g
        d_ref[...], nm_ref[...], nv_ref[...] = _adamw_step(w_ref[...], m_ref[...], v_ref[...], g)

    spec = pl.BlockSpec((tr, c), lambda i, chip_ref: (i, 0))

    def slot(k):
        return pl.BlockSpec((None, tr, c), lambda i, chip_ref: (k, i, 0))

    grid_spec = pltpu.PrefetchScalarGridSpec(
        num_scalar_prefetch=1, grid=(r // tr,),
        in_specs=[spec, spec, spec, pl.BlockSpec((None, tr, c), lambda i, chip_ref: (chip_ref[0], i, 0)),
                  slot(0), slot(1), slot(2)],
        out_specs=[spec] * 4)
    out = jax.ShapeDtypeStruct((r, c), F32)
    return pl.pallas_call(body, name=name, grid_spec=grid_spec, out_shape=[out] * 4,
                          compiler_params=_params(("parallel",)))(chip, w, m, v, chip_sums, received, received, received)


def _sum_devices(gathered):
    _, r, c = gathered.shape

    def body(g_ref, o_ref):
        acc = g_ref[0]
        for d in range(1, N_DEV):
            acc = acc + g_ref[d]
        o_ref[...] = acc

    return pl.pallas_call(body, name="sum_devices", out_shape=jax.ShapeDtypeStruct((r, c), F32))(gathered)


def _pack_rows(vectors, rows):
    flat = jnp.concatenate([v.reshape(-1) for v in vectors])
    return jnp.pad(flat, (0, rows * 128 - flat.shape[0])).reshape(rows, 128)


def _unpack(flat, shapes):
    out, off = [], 0
    for shp in shapes:
        n = 1
        for d in shp:
            n *= d
        out.append(flat[off:off + n].reshape(shp))
        off += n
    return out


def _device_step(xs, tgt, mod, norm1_w, norm2_w, lb_logits, hg_norm_w, q_norm_w, k_norm_w, conv_w_full, conv_b,
                 win_g, w_out_x, w_up_x, w_down_x, core=None):
    fused = core is not None
    shift1, scale1, gate1, shift2, scale2, gate2 = (mod[k] for k in range(6))

    h, rstd1 = _norm_fwd("norm1_fwd", xs, norm1_w, scale1, shift1)
    if fused:
        near = (0, 1, 2)
        head_rows, tail_rows = (0, UP_HEAD_ROWS), (UP_HEAD_ROWS, D_MODEL - UP_HEAD_ROWS)
        proj, (wout_g, wup_g) = _mm_blocked_rhs(
            "mm_in", h, win_g, fused_arrays=[w_out_x, w_up_x],
            fused=[_FusedCopies("gather", [w_out_x]), _FusedCopies("gather", [w_up_x], peers=near, rows=head_rows)])
        (a_out, o_pre), (wup_g,) = _hgrn_fwd(
            proj, lb_logits, hg_norm_w, fused_arrays=[w_up_x, wup_g],
            fused=_FusedCopies("gather_more", [w_up_x, wup_g], peers=near, rows=tail_rows, relay_rows=head_rows))
        wout_g, = _forward_to_sibling("allgather_stage2_out", [wout_g])
        wout_full = wout_g.reshape(D_MODEL, D_MODEL)
        (qn, kn, vb), _ = _qk_prep(proj, q_norm_w, k_norm_w)
        (att_o, lse), (wup_g,) = _attn_fwd(qn, kn, vb, _FusedCopies("relay", [wup_g], rows=tail_rows), [wup_g])
    else:
        proj = _mm_blocked_rhs("mm_in", h, win_g)
        (a_out, o_pre), _ = _hgrn_fwd(proj, lb_logits, hg_norm_w)
        wup_g, wout_full, wdown_full = w_up_x, w_out_x, w_down_x
        (qn, kn, vb), _ = _qk_prep(proj, q_norm_w, k_norm_w)
        (att_o, lse), _ = _attn_fwd(qn, kn, vb)
    mixin = jnp.concatenate([a_out, att_o.astype(BF16)], axis=1)
    if fused:
        mix, (wup_g,) = _mm_plain("mm_out", mixin, wout_full, NN, 512, 1024, F32,
                                  fused=_FusedCopies("forward", [wup_g]), fused_arrays=[wup_g])
    else:
        mix = _mm_plain("mm_out", mixin, wout_full, NN, 512, 1024, F32)
    x1, h2, rstd2 = _norm_fwd("norm2_fwd", xs, norm2_w, scale2, shift2, resid=mix, gate=gate1)
    if fused:
        u, (wdown_g,) = _mm_blocked_rhs("mm_up", h2, wup_g, fused=_FusedCopies("gather", [w_down_x]),
                                        fused_arrays=[w_down_x])
        y, (wdown_g,) = _conv_gate_fwd(u, conv_w_full, conv_b, _FusedCopies("forward", [wdown_g]), [wdown_g])
        wdown_full = wdown_g.reshape(D_FF, D_MODEL)
    else:
        u = _mm_blocked_rhs("mm_up", h2, wup_g)
        y = _conv_gate_fwd(u, conv_w_full, conv_b)
    ffn = _mm_plain("mm_down", y, wdown_full, NN, 512, 512, F32)
    loss_v, dout, dffn, dgate2 = _loss_head(x1, ffn, gate2, tgt)

    dy = _mm_plain("mm_down_dx", dffn, wdown_full, NT, 512, UP_BLK, BF16)
    gw_down = _mm_plain("mm_down_dw", y, dffn, TN, UP_BLK, 1024, BF16)
    da, dg, gconv_w, gconv_b = _conv_gate_bwd(u, dy, conv_w_full, conv_b)
    dh2 = _mm_halves_rhs_t("mm_up_dx", da, dg, wup_g)
    gw_up = _mm_halves_wgrad("mm_up_dw", h2, da, dg)
    if fused:
        part_up, part_down = gw_up, gw_down.reshape(N_DEV, FF_BLK, D_MODEL)
        (dx1, dmix, dshift2, dscale2, gnorm2, dgate1), (sib_up,) = _norm_bwd(
            "norm2_bwd", dh2, x1, rstd2, norm2_w, scale2, dout, mix=mix, gate=gate1,
            fused=_FusedCopies("sibling", [part_up]), fused_arrays=[part_up])
    else:
        dx1, dmix, dshift2, dscale2, gnorm2, dgate1 = _norm_bwd(
            "norm2_bwd", dh2, x1, rstd2, norm2_w, scale2, dout, mix=mix, gate=gate1)
    gw_out = _mm_plain("mm_out_dw", mixin, dmix, TN, 512, 1024, BF16)
    if fused:
        part_out = gw_out.reshape(N_DEV, OUT_BLK, D_MODEL)
        dmixin, (sib_out, sib_down) = _mm_plain(
            "mm_out_dx", dmix, wout_full, NT, 512, 1024, F32,
            fused=_FusedCopies("sibling", [part_out, part_down]), fused_arrays=[part_out, part_down])
        cs_up = _pair_sum("grad_pair_sum_up", part_up, sib_up, core)
        cs_out = _pair_sum("grad_pair_sum_out", part_out, sib_out, core)
        cs_down = _pair_sum("grad_pair_sum_down", part_down, sib_down, core)
        (dhq, dhf, dhi, dhg, glog, ghg), (fc_up,) = _hgrn_bwd(
            proj, lb_logits, hg_norm_w, o_pre, dmixin, _FusedCopies("chips", [cs_up]), [cs_up])
        (dqn, dkn, dvv), (fc_down,) = _attn_bwd(qn, kn, vb, att_o, lse, dmixin,
                                                _FusedCopies("chips", [cs_down]), [cs_down])
    else:
        dmixin = _mm_plain("mm_out_dx", dmix, wout_full, NT, 512, 1024, F32)
        (dhq, dhf, dhi, dhg, glog, ghg), _ = _hgrn_bwd(proj, lb_logits, hg_norm_w, o_pre, dmixin)
        (dqn, dkn, dvv), _ = _attn_bwd(qn, kn, vb, att_o, lse, dmixin)
    daq, dak, dav, gqw, gkw = _qk_bwd(proj, q_norm_w, k_norm_w, dqn, dkn, dvv)
    dproj = jnp.concatenate([dhq, dhf, dhi, dhg, daq, dak, dav], axis=1)
    if fused:
        gw_in, (fc_out,) = _mm_wgrad_blocked("mm_in_dw", h, dproj, fused=_FusedCopies("chips", [cs_out]),
                                             fused_arrays=[cs_out])
        from_sibling, = _exchange_sibling("grad_exchange_sibling_b", [gw_in])
        cs_in = _pair_sum("grad_pair_sum_in", gw_in, from_sibling, core)
        dh, (fc_in,) = _mm_blocked_rhs_t("mm_in_dx", dproj, win_g, fused_arrays=[cs_in],
                                         fused=_FusedCopies("chips", [cs_in], rows=W_IN_EXCHANGE_ROWS[0]))
        (grad_x, dshift1, dscale1, gnorm1), (fc_in,) = _norm_bwd(
            "norm1_bwd", dh, xs, rstd1, norm1_w, scale1, dx1, fused_arrays=[cs_in, fc_in],
            fused=_FusedCopies("chips_more", [cs_in, fc_in], rows=W_IN_EXCHANGE_ROWS[1]))
        large = [(cs_in, fc_in), (cs_out, fc_out), (cs_up, fc_up), (cs_down, fc_down)]
    else:
        gw_in = _mm_wgrad_blocked("mm_in_dw", h, dproj)
        dh = _mm_blocked_rhs_t("mm_in_dx", dproj, win_g)
        grad_x, dshift1, dscale1, gnorm1 = _norm_bwd("norm1_bwd", dh, xs, rstd1, norm1_w, scale1, dx1)
        large = [gw_in, gw_out, gw_up, gw_down]
    gmod = jnp.concatenate([dshift1, dscale1, dgate1, dshift2, dscale2, dgate2], axis=1)
    return (loss_v, grad_x, gmod, gnorm1, gnorm2, glog, ghg, gqw, gkw, gconv_b, gconv_w, *large)


def kernel(x, c, w_ada, b_ada, norm1_w, w_in, lb_logits, hg_norm_w, q_norm_w, k_norm_w, w_out, norm2_w, w_up, conv_w, conv_b, w_down, loss_target, m_w_ada, m_b_ada, m_norm1_w, m_w_in, m_lb_logits, m_hg_norm_w, m_q_norm_w, m_k_norm_w, m_w_out, m_norm2_w, m_w_up, m_conv_w, m_conv_b, m_w_down, v_w_ada, v_b_ada, v_norm1_w, v_w_in, v_lb_logits, v_hg_norm_w, v_q_norm_w, v_k_norm_w, v_w_out, v_norm2_w, v_w_up, v_conv_w, v_conv_b, v_w_down):
    ix, iy, ic = lax.axis_index("x"), lax.axis_index("y"), lax.axis_index("c")
    me = 4 * ix + 2 * iy + ic
    my_chip = 2 * ix + iy

    xs = x[0]
    tgt = loss_target[0]

    win_g, = _allgather_weights([w_in[0].astype(BF16)])

    c_all = _allgather_vmem(c.reshape(8, D_MODEL // 8), "allgather_c").reshape(N_DEV, D_MODEL)
    b_blk = lax.dynamic_slice_in_dim(b_ada, me * ADA_BLK, ADA_BLK, axis=1)
    mod_cols = _ada_fwd(c_all, w_ada[0], b_blk)
    mod_all = _allgather_vmem(mod_cols, "allgather_mod").reshape(N_DEV, N_DEV, ADA_BLK)
    mod = lax.dynamic_index_in_dim(mod_all, me, axis=1, keepdims=False).reshape(6, 1, D_MODEL)

    conv_w_all = _allgather_vmem(_pack_rows([conv_w[0]], 24), "allgather_conv_w").reshape(N_DEV, 24 * 128)
    conv_w_full = conv_w_all[:, :3 * FF_BLK].reshape(N_DEV, 3, FF_BLK).transpose(1, 0, 2).reshape(3, D_FF)

    (loss_v, grad_x, gmod, gnorm1, gnorm2, glog, ghg, gqw, gkw, gconv_b, gconv_w,
     rs_in, rs_out, rs_up, rs_down) = _device_step(
        xs, tgt, mod, norm1_w, norm2_w, lb_logits, hg_norm_w, q_norm_w, k_norm_w, conv_w_full, conv_b,
        win_g, w_out[0].astype(BF16), w_up[0].astype(BF16), w_down[0].astype(BF16),
        core=jnp.reshape(ic, (1,)).astype(jnp.int32))
    loss = lax.psum(loss_v[0, 0], AXES)

    small_shapes = [(1, 6 * D_MODEL), (1, D_MODEL), (1, D_MODEL), (2, HEADS * HEAD_DIM), (1, HEAD_DIM),
                    (1, HEAD_DIM), (1, HEAD_DIM), (1, D_FF), (3, D_FF)]
    small = [gmod, gnorm1, gnorm2, glog, ghg, gqw, gkw, gconv_b, gconv_w]
    n_small = sum(a.size for a in small)
    rows = -(-n_small // 1024) * 8
    gathered = _allgather_vmem(_pack_rows(small, rows), "allgather_small").reshape(N_DEV, rows, 128)
    summed = _sum_devices(gathered).reshape(-1)
    (g_b_ada, g_norm1, g_norm2, g_lb, g_hg, g_q, g_k, g_conv_b, g_conv_w_full) = _unpack(summed, small_shapes)
    g_conv_w = lax.dynamic_slice_in_dim(g_conv_w_full, me * FF_BLK, FF_BLK, axis=1)

    gmod_all = gathered[:, :6 * D_MODEL // 128, :].reshape(N_DEV, 6 * D_MODEL)
    gmod_cols = lax.dynamic_slice_in_dim(gmod_all, me * ADA_BLK, ADA_BLK, axis=1)
    g_w_ada_raw = _ada_wgrad(c_all, gmod_cols)

    chip = jnp.reshape(my_chip, (1,)).astype(jnp.int32)

    def big_update(name, w, m, v, rs, tr):
        chip_sums, received = rs
        return _adamw_reduced(name, w[0], m[0], v[0], chip_sums, received, chip, tr)

    cs_in, fc_in = rs_in
    r_ada, (fc_in,) = _adamw("adamw_w_ada", w_ada[0], m_w_ada[0], v_w_ada[0], [g_w_ada_raw], tr=256,
                             fused=_FusedCopies("chips_more", [cs_in, fc_in], rows=W_IN_EXCHANGE_ROWS[2]),
                             fused_arrays=[cs_in, fc_in])
    r_in = big_update("adamw_w_in", w_in, m_w_in, v_w_in, (cs_in, fc_in), 256)
    r_out = big_update("adamw_w_out", w_out, m_w_out, v_w_out, rs_out, 128)
    r_up = big_update("adamw_w_up", w_up, m_w_up, v_w_up, rs_up, 256)
    r_down = big_update("adamw_w_down", w_down, m_w_down, v_w_down, rs_down, 176)
    r_convw = _adamw("adamw_conv_w", conv_w[0], m_conv_w[0], v_conv_w[0], [g_conv_w])

    rep_shapes = [(1, 6 * D_MODEL), (1, D_MODEL), (1, D_MODEL), (2, HEADS * HEAD_DIM), (1, HEAD_DIM),
                  (1, HEAD_DIM), (1, HEAD_DIM), (1, D_FF)]
    rep_rows = -(-sum(a * b for a, b in rep_shapes) // 1024) * 8
    pack = lambda arrs: _pack_rows(arrs, rep_rows)
    rep = _adamw("adamw_small",
                 pack([b_ada, norm1_w, norm2_w, lb_logits, hg_norm_w, q_norm_w, k_norm_w, conv_b]),
                 pack([m_b_ada, m_norm1_w, m_norm2_w, m_lb_logits, m_hg_norm_w, m_q_norm_w, m_k_norm_w, m_conv_b]),
                 pack([v_b_ada, v_norm1_w, v_norm2_w, v_lb_logits, v_hg_norm_w, v_q_norm_w, v_k_norm_w, v_conv_b]),
                 [pack([g_b_ada, g_norm1, g_norm2, g_lb, g_hg, g_q, g_k, g_conv_b])])
    rep = [_unpack(r.reshape(-1), rep_shapes) for r in rep]

    def big(r):
        return [a[None] for a in r]

    order = {"w_ada": big(r_ada), "b_ada": [r[0] for r in rep], "norm1_w": [r[1] for r in rep],
             "w_in": big(r_in), "lb_logits": [r[3] for r in rep], "hg_norm_w": [r[4] for r in rep],
             "q_norm_w": [r[5] for r in rep], "k_norm_w": [r[6] for r in rep], "w_out": big(r_out),
             "norm2_w": [r[2] for r in rep], "w_up": big(r_up), "conv_w": big(r_convw),
             "conv_b": [r[7] for r in rep], "w_down": big(r_down)}
    names = ["w_ada", "b_ada", "norm1_w", "w_in", "lb_logits", "hg_norm_w", "q_norm_w", "k_norm_w", "w_out",
             "norm2_w", "w_up", "conv_w", "conv_b", "w_down"]
    outs = [loss, grad_x[None]]
    for kind in range(4):
        outs += [order[n][kind] for n in names]
    return tuple(outs)
```

```python
import functools

import jax
import jax.numpy as jnp
from jax import lax
from jax.experimental import pallas as pl
from jax.experimental.pallas import tpu as pltpu

F32 = jnp.float32
BF16 = jnp.bfloat16

N_DEV = 8
SEQ = 2048
D_MODEL = 2048
HEADS = 8
HEAD_DIM = 128
IN_COLS = 7168
IN_BLK = IN_COLS // N_DEV
D_FF = 5632
UP_BLK = 2 * D_FF // N_DEV
FF_BLK = D_FF // N_DEV
ADA_BLK = 6 * D_MODEL // N_DEV
OUT_BLK = D_MODEL // N_DEV
EPS = 1e-6
CHUNK = 16
ROW_TILE = 256
V7X_VMEM_LIMIT = 56 * 1024 * 1024

ADAM_LR = 0.001
ADAM_B1 = 0.9
ADAM_B2 = 0.999
ADAM_EPS = 1e-08
ADAM_WD = 0.01
ADAM_STEP = 10

NN = (((1,), (0,)), ((), ()))
NT = (((1,), (1,)), ((), ()))
TN = (((0,), (0,)), ((), ()))
MESH = pl.DeviceIdType.MESH
AXES = ("x", "y", "c")


def _params(sem=None, vmem=V7X_VMEM_LIMIT):
    return pltpu.CompilerParams(dimension_semantics=sem, vmem_limit_bytes=vmem)


def _sigmoid(x):
    return 1.0 / (1.0 + jnp.exp(-x))


def _dsilu(x, s):
    return s * (1.0 + x * (1.0 - s))


def _lane_sum(x, ones_bf16):
    return jnp.dot(x.astype(BF16), ones_bf16, preferred_element_type=F32)


def _mesh_pos():
    return lax.axis_index("x"), lax.axis_index("y"), lax.axis_index("c")


def _allgather_vmem(x_blk, name):
    m_per, n = x_blk.shape

    def body(x_ref, out_ref, send_sems, recv_sems, local_sem):
        x, y, c = _mesh_pos()
        me, sibling = (x, y, c), (x, y, 1 - c)
        chips = [(1 - x, y), (x, 1 - y), (1 - x, 1 - y)]

        def rows(px, py, pc):
            return out_ref.at[pl.ds((4 * px + 2 * py + pc) * m_per, m_per), :]

        def copy(k, block, to, src=None):
            return pltpu.make_async_remote_copy(
                src_ref=rows(*block) if src is None else src, dst_ref=rows(*block),
                send_sem=send_sems.at[k], recv_sem=recv_sems.at[k], device_id=to, device_id_type=MESH)

        mine = pltpu.make_async_copy(x_ref, rows(*me), local_sem)
        mine.start()
        first = [copy(0, me, sibling, src=x_ref)]
        first += [copy(1 + j, me, (*chip, c), src=x_ref) for j, chip in enumerate(chips)]
        for cp in first:
            cp.start()
        passed = [copy(4 + j, (*chip, c), sibling) for j, chip in enumerate(chips)]
        for j, chip in enumerate(chips):
            copy(1 + j, (*chip, c), me).wait_recv()
            passed[j].start()
        copy(0, sibling, me).wait_recv()
        for j, chip in enumerate(chips):
            copy(4 + j, (*chip, 1 - c), me).wait_recv()
        for cp in first + passed:
            cp.wait_send()
        mine.wait()

    return pl.pallas_call(
        body, name=name,
        out_shape=jax.ShapeDtypeStruct((N_DEV * m_per, n), x_blk.dtype),
        in_specs=[pl.BlockSpec(memory_space=pltpu.VMEM)],
        out_specs=pl.BlockSpec(memory_space=pltpu.VMEM),
        scratch_shapes=[pltpu.SemaphoreType.DMA((7,)), pltpu.SemaphoreType.DMA((7,)), pltpu.SemaphoreType.DMA],
    )(x_blk)


def _flip(v, bit):
    return v + bit - 2 * v * bit


def _relay_chips(x, y, c):
    return (_flip(x, 1 - c), _flip(y, c)), (_flip(x, c), _flip(y, 1 - c))


W_IN_EXCHANGE_ROWS = ((0, 1408), (1408, 640))
UP_HEAD_ROWS = 768
GATHER_PARTS = 4


def _allgather_weights(blocks):
    n_arr = len(blocks)
    parts = GATHER_PARTS

    def body(*refs):
        ins, outs = refs[:n_arr], refs[n_arr:2 * n_arr]
        send_sems, recv_sems, local_sems = refs[2 * n_arr:]
        x, y, c = _mesh_pos()
        me, sibling = (x, y, c), (x, y, 1 - c)
        near = [(1 - x, y), (x, 1 - y)]
        chips = near + [(1 - x, 1 - y)]
        relay_from, relay_to = _relay_chips(x, y, c)

        def rows(a, p):
            hr = ins[a].shape[0] // parts
            return pl.ds(p * hr, hr)

        def slot(a, pos, p):
            return outs[a].at[4 * pos[0] + 2 * pos[1] + pos[2], rows(a, p)]

        def copy(a, k, p, src, lands, to):
            return pltpu.make_async_remote_copy(
                src_ref=src, dst_ref=slot(a, lands, p), send_sem=send_sems.at[a, k, p], recv_sem=recv_sems.at[a, k, p],
                device_id=to, device_id_type=MESH)

        sent = []
        local = [pltpu.make_async_copy(ins[a], outs[a].at[4 * x + 2 * y + c], local_sems.at[a]) for a in range(n_arr)]
        for cp in local:
            cp.start()
        for p in range(parts):
            for a in range(n_arr):
                own = ins[a].at[rows(a, p)]
                sent.append(copy(a, 0, p, own, me, sibling))
                sent += [copy(a, 1 + j, p, own, me, (*chip, c)) for j, chip in enumerate(near)]
        for cp in sent:
            cp.start()

        def start(cp):
            cp.start()
            sent.append(cp)

        for p in range(parts):
            for a in range(n_arr):
                for j, chip in enumerate(near):
                    copy(a, 1 + j, p, ins[a].at[rows(a, p)], (*chip, c), me).wait_recv()
                    start(copy(a, 4 + j, p, slot(a, (*chip, c), p), (*chip, c), sibling))
                start(copy(a, 3, p, slot(a, (*relay_from, c), p), (*relay_from, c), (*relay_to, c)))
        for p in range(parts):
            for a in range(n_arr):
                copy(a, 3, p, ins[a].at[rows(a, p)], (*chips[2], c), me).wait_recv()
                start(copy(a, 6, p, slot(a, (*chips[2], c), p), (*chips[2], c), sibling))
        for p in range(parts):
            for a in range(n_arr):
                copy(a, 0, p, ins[a].at[rows(a, p)], sibling, me).wait_recv()
                for j, chip in enumerate(chips):
                    copy(a, 4 + j, p, ins[a].at[rows(a, p)], (*chip, 1 - c), me).wait_recv()
        for cp in sent:
            cp.wait_send()
        for cp in local:
            cp.wait()

    return pl.pallas_call(
        body, name="allgather_weights",
        out_shape=[jax.ShapeDtypeStruct((N_DEV,) + b.shape, b.dtype) for b in blocks],
        in_specs=[pl.BlockSpec(memory_space=pltpu.HBM)] * n_arr, out_specs=[pl.BlockSpec(memory_space=pltpu.HBM)] * n_arr,
        scratch_shapes=[pltpu.SemaphoreType.DMA((n_arr, 7, parts)), pltpu.SemaphoreType.DMA((n_arr, 7, parts)),
                        pltpu.SemaphoreType.DMA((n_arr,))],
    )(*blocks)


HBM_SPEC = pl.BlockSpec(memory_space=pltpu.HBM)


class _FusedCopies:
    def __init__(self, kind, arrays, peers=(0, 1, 2, 3), rows=None, relay_rows=None):
        self.kind = kind
        self.peers = peers
        self.rows = rows
        self.relay_rows = relay_rows
        n = len(arrays) // 2 if kind == "gather_more" else len(arrays)
        self.n = n
        self.n_in = len(arrays)
        self.aliases = {}
        if kind == "gather":
            self.out_shape = [jax.ShapeDtypeStruct((N_DEV,) + a.shape, a.dtype) for a in arrays]
            self.scratch_shapes = [pltpu.SemaphoreType.DMA((n, 4, GATHER_PARTS)),
                                   pltpu.SemaphoreType.DMA((n, 4, GATHER_PARTS)), pltpu.SemaphoreType.DMA((n,))]
        elif kind == "gather_more":
            self.out_shape = [jax.ShapeDtypeStruct(a.shape, a.dtype) for a in arrays[n:]]
            self.scratch_shapes = [pltpu.SemaphoreType.DMA((n, 5, GATHER_PARTS)),
                                   pltpu.SemaphoreType.DMA((n, 5, GATHER_PARTS)), pltpu.SemaphoreType.DMA((n,))]
            self.aliases = {n + a: a for a in range(n)}
        elif kind == "relay":
            self.out_shape = [jax.ShapeDtypeStruct(a.shape, a.dtype) for a in arrays]
            self.scratch_shapes = [pltpu.SemaphoreType.DMA((n,)), pltpu.SemaphoreType.DMA((n,))]
            self.aliases = {a: a for a in range(n)}
        elif kind == "forward":
            self.out_shape = [jax.ShapeDtypeStruct(a.shape, a.dtype) for a in arrays]
            self.scratch_shapes = [pltpu.SemaphoreType.DMA((n, 3)), pltpu.SemaphoreType.DMA((n, 3))]
            self.aliases = {a: a for a in range(n)}
        elif kind == "sibling":
            self.out_shape = [jax.ShapeDtypeStruct((4,) + a.shape[1:], a.dtype) for a in arrays]
            self.scratch_shapes = [pltpu.SemaphoreType.DMA((n, 4)), pltpu.SemaphoreType.DMA((n, 4))]
        elif kind == "chips_more":
            n = self.n = len(arrays) // 2
            self.out_shape = [jax.ShapeDtypeStruct(a.shape, a.dtype) for a in arrays[n:]]
            self.scratch_shapes = [pltpu.SemaphoreType.DMA((n, 3)), pltpu.SemaphoreType.DMA((n, 3))]
            self.aliases = {n + a: a for a in range(n)}
        else:
            self.out_shape = [jax.ShapeDtypeStruct((3,) + a.shape[1:], a.dtype) for a in arrays]
            self.scratch_shapes = [pltpu.SemaphoreType.DMA((n, 3)), pltpu.SemaphoreType.DMA((n, 3))]
        self.in_specs = [HBM_SPEC] * self.n_in
        self.out_specs = [HBM_SPEC] * n
        self.n_scratch = len(self.scratch_shapes)

    def copies(self, ins, outs, sems):
        x, y, c = _mesh_pos()
        chips = [(1 - x, y), (x, 1 - y), (1 - x, 1 - y)]
        sibling = (x, y, 1 - c)
        starts, waits = [], []
        relay_from, relay_to = _relay_chips(x, y, c)

        def relayed(a, buf, lands, send_sem, recv_sem, rows):
            first, count = rows or (0, buf.shape[1])
            span = pl.ds(first, count)
            return pltpu.make_async_remote_copy(
                src_ref=buf.at[4 * relay_from[0] + 2 * relay_from[1] + c, span],
                dst_ref=outs[a].at[4 * lands[0] + 2 * lands[1] + c, span], send_sem=send_sem, recv_sem=recv_sem,
                device_id=(*relay_to, c), device_id_type=MESH)

        if self.kind in ("gather", "gather_more"):
            send_sems, recv_sems, local_sems = sems
            me = (x, y, c)
            peers = [sibling] + [(px, py, c) for px, py in chips]

            def slot(a, pos):
                return outs[a].at[4 * pos[0] + 2 * pos[1] + pos[2]]

            def span(a, p=None):
                first, count = self.rows or (0, ins[a].shape[0])
                if p is None:
                    return pl.ds(first, count)
                return pl.ds(first + p * (count // GATHER_PARTS), count // GATHER_PARTS)

            def remote(a, k, p, lands_from):
                return pltpu.make_async_remote_copy(
                    src_ref=ins[a].at[span(a, p)], dst_ref=slot(a, lands_from).at[span(a, p)],
                    send_sem=send_sems.at[a, k, p], recv_sem=recv_sems.at[a, k, p], device_id=peers[k],
                    device_id_type=MESH)

            for a in range(self.n):
                local = pltpu.make_async_copy(ins[a].at[span(a)], slot(a, me).at[span(a)], local_sems.at[a])
                starts.append(local)
                waits.append(local)
            for p in range(GATHER_PARTS):
                for a in range(self.n):
                    for k in self.peers:
                        starts.append(remote(a, k, p, me))
                        waits.append(remote(a, k, p, peers[k]))
            if self.kind == "gather_more" and self.relay_rows is not None:
                for a in range(self.n):
                    buf = ins[self.n + a]
                    starts.append(relayed(a, buf, relay_from, send_sems.at[a, 4, 0], recv_sems.at[a, 4, 0],
                                          self.relay_rows))
                    waits.append(relayed(a, buf, chips[2], send_sems.at[a, 4, 0], recv_sems.at[a, 4, 0],
                                         self.relay_rows))
        elif self.kind == "relay":
            send_sems, recv_sems = sems
            for a in range(self.n):
                starts.append(relayed(a, ins[a], relay_from, send_sems.at[a], recv_sems.at[a], self.rows))
                waits.append(relayed(a, ins[a], chips[2], send_sems.at[a], recv_sems.at[a], self.rows))
        elif self.kind == "forward":
            send_sems, recv_sems = sems

            def passed_on(a, j, pc_src, pc_dst):
                px, py = chips[j]
                first, count = self.rows or (0, ins[a].shape[1])
                span = pl.ds(first, count)
                return pltpu.make_async_remote_copy(
                    src_ref=ins[a].at[4 * px + 2 * py + pc_src, span],
                    dst_ref=outs[a].at[4 * px + 2 * py + pc_dst, span],
                    send_sem=send_sems.at[a, j], recv_sem=recv_sems.at[a, j], device_id=sibling, device_id_type=MESH)

            for a in range(self.n):
                for j in range(3):
                    starts.append(passed_on(a, j, c, c))
                    waits.append(passed_on(a, j, c, 1 - c))
        elif self.kind == "sibling":
            send_sems, recv_sems = sems
            for a in range(self.n):
                for q in range(4):
                    cp = pltpu.make_async_remote_copy(
                        src_ref=ins[a].at[2 * q + 1 - c], dst_ref=outs[a].at[q], send_sem=send_sems.at[a, q],
                        recv_sem=recv_sems.at[a, q], device_id=sibling, device_id_type=MESH)
                    starts.append(cp)
                    waits.append(cp)
        else:
            send_sems, recv_sems = sems
            for a in range(self.n):
                first, count = self.rows or (0, ins[a].shape[1])
                span = pl.ds(first, count)
                for j, (px, py) in enumerate(chips):
                    cp = pltpu.make_async_remote_copy(
                        src_ref=ins[a].at[2 * px + py, span], dst_ref=outs[a].at[j, span],
                        send_sem=send_sems.at[a, j], recv_sem=recv_sems.at[a, j], device_id=(px, py, c),
                        device_id_type=MESH)
                    starts.append(cp)
                    waits.append(cp)
        return starts, waits


def _fused_groups(fused):
    if fused is None:
        return []
    return list(fused) if isinstance(fused, (list, tuple)) else [fused]


def _host_body(body, n_in, n_out, fused, first_last):
    groups = _fused_groups(fused)
    if not groups:
        return body
    n_fin, n_fout = sum(g.n_in for g in groups), sum(g.n for g in groups)
    n_fsem = sum(g.n_scratch for g in groups)

    def wrapped(*refs):
        core_in, f_in = refs[:n_in], refs[n_in:n_in + n_fin]
        core_out = refs[n_in + n_fin:n_in + n_fin + n_out]
        f_out = refs[n_in + n_fin + n_out:n_in + n_fin + n_out + n_fout]
        rest = refs[n_in + n_fin + n_out + n_fout:]
        core_scratch, f_sems = rest[:len(rest) - n_fsem], rest[len(rest) - n_fsem:]
        starts, waits = [], []
        for g in groups:
            s, w = g.copies(f_in[:g.n_in], f_out[:g.n], f_sems[:g.n_scratch])
            f_in, f_out, f_sems = f_in[g.n_in:], f_out[g.n:], f_sems[g.n_scratch:]
            starts += s
            waits += w
        first, last = first_last()

        @pl.when(first)
        def _():
            for cp in starts:
                cp.start()

        body(*core_in, *core_out, *core_scratch)

        @pl.when(last)
        def _():
            for cp in waits:
                cp.wait()

    return wrapped


def _host_call(body, n_in, n_out, fused, first_last, *, name, grid, in_specs, out_specs, out_shape, scratch_shapes,
               sem, operands):
    aliases = {}
    in_specs, out_specs, out_shape, scratch_shapes = list(in_specs), list(out_specs), list(out_shape), list(scratch_shapes)
    fin, fout = n_in, n_out
    for g in _fused_groups(fused):
        aliases.update({fin + fi: fout + fo for fi, fo in g.aliases.items()})
        fin, fout = fin + g.n_in, fout + g.n
        in_specs += g.in_specs
        out_specs += g.out_specs
        out_shape += g.out_shape
        scratch_shapes += g.scratch_shapes
        sem = tuple("arbitrary" for _ in sem)
    res = pl.pallas_call(_host_body(body, n_in, n_out, fused, first_last), name=name, grid=grid, in_specs=in_specs,
                         out_specs=out_specs, out_shape=out_shape, scratch_shapes=scratch_shapes,
                         input_output_aliases=aliases, compiler_params=_params(sem))(*operands)
    return list(res[:n_out]), list(res[n_out:])


def _forward_to_sibling(name, gathered):
    n_arr = len(gathered)

    def body(*refs):
        ins, outs = refs[:n_arr], refs[n_arr:2 * n_arr]
        send_sems, recv_sems = refs[2 * n_arr:]
        x, y, c = _mesh_pos()
        chips = [(1 - x, y), (x, 1 - y), (1 - x, 1 - y)]

        def copy(a, j, pc):
            px, py = chips[j]
            s = 4 * px + 2 * py + pc
            return pltpu.make_async_remote_copy(
                src_ref=ins[a].at[s], dst_ref=outs[a].at[s], send_sem=send_sems.at[a, j], recv_sem=recv_sems.at[a, j],
                device_id=(x, y, 1 - c), device_id_type=MESH)

        for a in range(n_arr):
            for j in range(3):
                copy(a, j, c).start()
        for a in range(n_arr):
            for j in range(3):
                copy(a, j, 1 - c).wait_recv()
                copy(a, j, c).wait_send()

    return pl.pallas_call(
        body, name=name,
        out_shape=[jax.ShapeDtypeStruct(g.shape, g.dtype) for g in gathered],
        in_specs=[HBM_SPEC] * n_arr, out_specs=[HBM_SPEC] * n_arr,
        input_output_aliases={a: a for a in range(n_arr)},
        scratch_shapes=[pltpu.SemaphoreType.DMA((n_arr, 3)), pltpu.SemaphoreType.DMA((n_arr, 3))],
    )(*gathered)


def _exchange_sibling(name, partials):
    n_arr = len(partials)

    def body(*refs):
        ins, outs = refs[:n_arr], refs[n_arr:2 * n_arr]
        send_sems, recv_sems = refs[2 * n_arr:]
        x, y, c = _mesh_pos()
        copies = [pltpu.make_async_remote_copy(
            src_ref=ins[a].at[2 * q + 1 - c], dst_ref=outs[a].at[q], send_sem=send_sems.at[a, q],
            recv_sem=recv_sems.at[a, q], device_id=(x, y, 1 - c), device_id_type=MESH)
            for a in range(n_arr) for q in range(4)]
        for cp in copies:
            cp.start()
        for cp in copies:
            cp.wait_recv()
        for cp in copies:
            cp.wait_send()

    return pl.pallas_call(
        body, name=name,
        out_shape=[jax.ShapeDtypeStruct((4,) + p.shape[1:], p.dtype) for p in partials],
        in_specs=[HBM_SPEC] * n_arr, out_specs=[HBM_SPEC] * n_arr,
        scratch_shapes=[pltpu.SemaphoreType.DMA((n_arr, 4)), pltpu.SemaphoreType.DMA((n_arr, 4))],
    )(*partials)


def _matmul(name, a, b, dims, grid, a_spec, b_spec, o_spec, out_shape, acc_axis=None, fused=None, fused_arrays=()):
    def body(a_ref, b_ref, o_ref):
        r = lax.dot_general(a_ref[...], b_ref[...], dims, preferred_element_type=F32)
        if acc_axis is None:
            o_ref[...] = r.astype(o_ref.dtype)
        else:
            k = pl.program_id(acc_axis)

            @pl.when(k == 0)
            def _():
                o_ref[...] = r

            @pl.when(k > 0)
            def _():
                o_ref[...] += r

    sem = tuple("arbitrary" if i == acc_axis else "parallel" for i in range(len(grid)))
    if fused is None:
        return pl.pallas_call(body, name=name, grid=grid, in_specs=[a_spec, b_spec], out_specs=o_spec,
                              out_shape=out_shape, compiler_params=_params(sem))(a, b)

    def first_last():
        first = last = None
        for ax, n in enumerate(grid):
            f, l = pl.program_id(ax) == 0, pl.program_id(ax) == n - 1
            first, last = (f, l) if first is None else (first & f, last & l)
        return first, last

    (out,), extra = _host_call(body, 2, 1, fused, first_last, name=name, grid=grid, in_specs=[a_spec, b_spec],
                               out_specs=[o_spec], out_shape=[out_shape], scratch_shapes=[], sem=sem,
                               operands=[a, b] + list(fused_arrays))
    return out, extra


def _mm_blocked_rhs(name, a, w_g, tm=512, fused=None, fused_arrays=()):
    m, k = a.shape
    nb = w_g.shape[2]
    return _matmul(name, a, w_g, NN, (N_DEV, m // tm),
                   pl.BlockSpec((tm, k), lambda j, i: (i, 0)),
                   pl.BlockSpec((None, k, nb), lambda j, i: (j, 0, 0)),
                   pl.BlockSpec((tm, nb), lambda j, i: (i, j)),
                   jax.ShapeDtypeStruct((m, N_DEV * nb), F32), fused=fused, fused_arrays=fused_arrays)


def _mm_blocked_rhs_t(name, a, w_g, tm=512, fused=None, fused_arrays=()):
    m = a.shape[0]
    n, nb = w_g.shape[1], w_g.shape[2]
    return _matmul(name, a, w_g, NT, (m // tm, N_DEV),
                   pl.BlockSpec((tm, nb), lambda i, j: (i, j)),
                   pl.BlockSpec((None, n, nb), lambda i, j: (j, 0, 0)),
                   pl.BlockSpec((tm, n), lambda i, j: (i, 0)),
                   jax.ShapeDtypeStruct((m, n), F32), acc_axis=1, fused=fused, fused_arrays=fused_arrays)


def _mm_wgrad_blocked(name, act, dcols, tk=512, fused=None, fused_arrays=()):
    t, k = act.shape
    nb = dcols.shape[1] // N_DEV
    return _matmul(name, act, dcols, TN, (N_DEV, k // tk),
                   pl.BlockSpec((t, tk), lambda j, i: (0, i)),
                   pl.BlockSpec((t, nb), lambda j, i: (0, j)),
                   pl.BlockSpec((None, tk, nb), lambda j, i: (j, i, 0)),
                   jax.ShapeDtypeStruct((N_DEV, k, nb), BF16), fused=fused, fused_arrays=fused_arrays)


def _halves_specs(block, index):
    half = N_DEV // 2
    return (pl.BlockSpec(block, lambda i, j: index(i, jnp.minimum(j, half - 1))),
            pl.BlockSpec(block, lambda i, j: index(i, jnp.maximum(j - half, 0))))


def _mm_halves_rhs_t(name, a_lo, a_hi, w_g, tm=512):
    m = a_lo.shape[0]
    n, nb = w_g.shape[1], w_g.shape[2]

    def body(lo_ref, hi_ref, b_ref, o_ref):
        j = pl.program_id(1)

        def accumulate(a_ref):
            r = lax.dot_general(a_ref[...], b_ref[...], NT, preferred_element_type=F32)

            @pl.when(j == 0)
            def _():
                o_ref[...] = r

            @pl.when(j > 0)
            def _():
                o_ref[...] += r

        pl.when(j < N_DEV // 2)(lambda: accumulate(lo_ref))
        pl.when(j >= N_DEV // 2)(lambda: accumulate(hi_ref))

    lo_spec, hi_spec = _halves_specs((tm, nb), lambda i, j: (i, j))
    return pl.pallas_call(
        body, name=name, grid=(m // tm, N_DEV),
        in_specs=[lo_spec, hi_spec, pl.BlockSpec((None, n, nb), lambda i, j: (j, 0, 0))],
        out_specs=pl.BlockSpec((tm, n), lambda i, j: (i, 0)), out_shape=jax.ShapeDtypeStruct((m, n), F32),
        compiler_params=_params(("parallel", "arbitrary")))(a_lo, a_hi, w_g)


def _mm_halves_wgrad(name, act, d_lo, d_hi, tk=512):
    t, k = act.shape
    nb = d_lo.shape[1] // (N_DEV // 2)

    def body(a_ref, lo_ref, hi_ref, o_ref):
        j = pl.program_id(0)

        def product(d_ref):
            o_ref[...] = lax.dot_general(a_ref[...], d_ref[...], TN, preferred_element_type=F32).astype(o_ref.dtype)

        pl.when(j < N_DEV // 2)(lambda: product(lo_ref))
        pl.when(j >= N_DEV // 2)(lambda: product(hi_ref))

    half = N_DEV // 2
    return pl.pallas_call(
        body, name=name, grid=(N_DEV, k // tk),
        in_specs=[pl.BlockSpec((t, tk), lambda j, i: (0, i)),
                  pl.BlockSpec((t, nb), lambda j, i: (0, jnp.minimum(j, half - 1))),
                  pl.BlockSpec((t, nb), lambda j, i: (0, jnp.maximum(j - half, 0)))],
        out_specs=pl.BlockSpec((None, tk, nb), lambda j, i: (j, i, 0)),
        out_shape=jax.ShapeDtypeStruct((N_DEV, k, nb), BF16),
        compiler_params=_params(("parallel", "parallel")))(act, d_lo, d_hi)


def _mm_plain(name, a, b, dims, tm, tn, out_dtype, fused=None, fused_arrays=()):
    if dims == NN:
        (m, k), n = a.shape, b.shape[1]
        a_spec = pl.BlockSpec((tm, k), lambda i, j: (i, 0))
        b_spec = pl.BlockSpec((k, tn), lambda i, j: (0, j))
    elif dims == NT:
        (m, k), n = a.shape, b.shape[0]
        a_spec = pl.BlockSpec((tm, k), lambda i, j: (i, 0))
        b_spec = pl.BlockSpec((tn, k), lambda i, j: (j, 0))
    else:
        (k, m), n = a.shape, b.shape[1]
        a_spec = pl.BlockSpec((k, tm), lambda i, j: (0, i))
        b_spec = pl.BlockSpec((k, tn), lambda i, j: (0, j))
    return _matmul(name, a, b, dims, (m // tm, n // tn), a_spec, b_spec,
                   pl.BlockSpec((tm, tn), lambda i, j: (i, j)), jax.ShapeDtypeStruct((m, n), out_dtype),
                   fused=fused, fused_arrays=fused_arrays)


def _ada_fwd(c_all, w_ada_blk, b_blk):
    def body(c_ref, w_ref, b_ref, o_ref):
        cv = c_ref[...]
        o_ref[...] = jnp.dot(cv * _sigmoid(cv), w_ref[...], preferred_element_type=F32) + b_ref[...]

    tn = 512
    return pl.pallas_call(
        body, name="ada_fwd", grid=(ADA_BLK // tn,),
        in_specs=[pl.BlockSpec((N_DEV, D_MODEL), lambda j: (0, 0)),
                  pl.BlockSpec((D_MODEL, tn), lambda j: (0, j)),
                  pl.BlockSpec((1, tn), lambda j: (0, j))],
        out_specs=pl.BlockSpec((N_DEV, tn), lambda j: (0, j)),
        out_shape=jax.ShapeDtypeStruct((N_DEV, ADA_BLK), F32),
        compiler_params=_params(("parallel",)))(c_all, w_ada_blk, b_blk)


def _ada_wgrad(c_all, gmod_cols):
    def body(c_ref, g_ref, o_ref):
        cv = c_ref[...]
        o_ref[...] = lax.dot_general(cv * _sigmoid(cv), g_ref[...], TN, preferred_element_type=F32)

    tk = 512
    return pl.pallas_call(
        body, name="ada_wgrad", grid=(D_MODEL // tk,),
        in_specs=[pl.BlockSpec((N_DEV, tk), lambda i: (0, i)),
                  pl.BlockSpec((N_DEV, ADA_BLK), lambda i: (0, 0))],
        out_specs=pl.BlockSpec((tk, ADA_BLK), lambda i: (i, 0)),
        out_shape=jax.ShapeDtypeStruct((D_MODEL, ADA_BLK), F32),
        compiler_params=_params(("parallel",)))(c_all, gmod_cols)


def _row_spec(cols=D_MODEL):
    return pl.BlockSpec((ROW_TILE, cols), lambda i: (i, 0))


def _vec_spec(cols=D_MODEL):
    return pl.BlockSpec((1, cols), lambda i: (0, 0))


def _norm_fwd(name, x, w, scale, shift, resid=None, gate=None, fused=None, fused_arrays=()):
    has_res = resid is not None

    def body(*refs):
        if has_res:
            x_ref, r_ref, g_ref, w_ref, sc_ref, sh_ref, xr_ref, h_ref, rs_ref = refs
            xr = x_ref[...] + g_ref[...] * r_ref[...]
            xr_ref[...] = xr
        else:
            x_ref, w_ref, sc_ref, sh_ref, h_ref, rs_ref = refs
            xr = x_ref[...]
        rs = lax.rsqrt(jnp.mean(xr * xr, axis=-1, keepdims=True) + EPS)
        h = (xr * rs) * w_ref[...] * (1.0 + sc_ref[...]) + sh_ref[...]
        h_ref[...] = h.astype(BF16)
        rs_ref[...] = rs

    s = x.shape[0]
    ins = [x] + ([resid, gate] if has_res else []) + [w, scale, shift]
    in_specs = [_row_spec()] + ([_row_spec(), _vec_spec()] if has_res else []) + [_vec_spec()] * 3
    outs = ([jax.ShapeDtypeStruct((s, D_MODEL), F32)] if has_res else []) + [
        jax.ShapeDtypeStruct((s, D_MODEL), BF16), jax.ShapeDtypeStruct((s, 1), F32)]
    out_specs = ([_row_spec()] if has_res else []) + [_row_spec(), pl.BlockSpec((ROW_TILE, 1), lambda i: (i, 0))]

    def first_last():
        i = pl.program_id(0)
        return i == 0, i == s // ROW_TILE - 1

    res, extra = _host_call(body, len(ins), len(outs), fused, first_last, name=name, grid=(s // ROW_TILE,),
                            in_specs=in_specs, out_specs=out_specs, out_shape=outs, scratch_shapes=[],
                            sem=("parallel",), operands=ins + list(fused_arrays))
    return res if fused is None else (res, extra)


def _norm_bwd(name, dh, x, rstd, w, scale, dres, mix=None, gate=None, fused=None, fused_arrays=()):
    has_mix = mix is not None

    def body(*refs):
        if has_mix:
            (dh_ref, x_ref, rs_ref, w_ref, sc_ref, dr_ref, mix_ref, g_ref,
             dx_ref, dmix_ref, dsh_ref, dsc_ref, dw_ref, dg_ref) = refs
        else:
            dh_ref, x_ref, rs_ref, w_ref, sc_ref, dr_ref, dx_ref, dsh_ref, dsc_ref, dw_ref = refs
        i = pl.program_id(0)
        dhv = dh_ref[...]
        rs = rs_ref[...]
        xn = x_ref[...] * rs
        wv = w_ref[...]
        one_sc = 1.0 + sc_ref[...]
        dxn = dhv * wv * one_sc
        dx = dr_ref[...] + rs * (dxn - xn * jnp.mean(dxn * xn, axis=-1, keepdims=True))
        dx_ref[...] = dx
        sums = [(dsh_ref, dhv), (dsc_ref, dhv * xn * wv), (dw_ref, dhv * one_sc * xn)]
        if has_mix:
            dmix_ref[...] = (dx * g_ref[...]).astype(BF16)
            sums.append((dg_ref, dx * mix_ref[...]))

        @pl.when(i == 0)
        def _():
            for ref, _v in sums:
                ref[...] = jnp.zeros_like(ref)

        for ref, v in sums:
            ref[...] += jnp.sum(v, axis=0, keepdims=True)

    s = x.shape[0]
    ins = [dh, x, rstd, w, scale, dres] + ([mix, gate] if has_mix else [])
    in_specs = ([_row_spec(), _row_spec(), pl.BlockSpec((ROW_TILE, 1), lambda i: (i, 0)), _vec_spec(), _vec_spec(),
                 _row_spec()] + ([_row_spec(), _vec_spec()] if has_mix else []))
    vec = jax.ShapeDtypeStruct((1, D_MODEL), F32)
    outs = ([jax.ShapeDtypeStruct((s, D_MODEL), F32)] + ([jax.ShapeDtypeStruct((s, D_MODEL), BF16)] if has_mix else [])
            + [vec] * (4 if has_mix else 3))
    out_specs = [_row_spec()] + ([_row_spec()] if has_mix else []) + [_vec_spec()] * (4 if has_mix else 3)

    def first_last():
        i = pl.program_id(0)
        return i == 0, i == s // ROW_TILE - 1

    res, extra = _host_call(body, len(ins), len(outs), fused, first_last, name=name, grid=(s // ROW_TILE,),
                            in_specs=in_specs, out_specs=out_specs, out_shape=outs, scratch_shapes=[],
                            sem=("arbitrary",), operands=ins + list(fused_arrays))
    return res if fused is None else (res, extra)


def _loss_head(x1, ffn, gate2, target):
    def body(x_ref, f_ref, g_ref, t_ref, loss_ref, dout_ref, dffn_ref, dg_ref):
        i = pl.program_id(0)
        fv = f_ref[...]
        gv = g_ref[...]
        err = x_ref[...] + gv * fv - t_ref[...]
        dout = err * (1.0 / D_MODEL)
        dout_ref[...] = dout
        dffn_ref[...] = (dout * gv).astype(BF16)

        @pl.when(i == 0)
        def _():
            loss_ref[...] = jnp.zeros_like(loss_ref)
            dg_ref[...] = jnp.zeros_like(dg_ref)

        row = jnp.sum(err * err, axis=-1, keepdims=True) * (1.0 / D_MODEL)
        loss_ref[...] += jnp.broadcast_to(0.5 * jnp.sum(row, axis=0, keepdims=True), (1, 128))
        dg_ref[...] += jnp.sum(dout * fv, axis=0, keepdims=True)

    s = x1.shape[0]
    return pl.pallas_call(
        body, name="loss_head", grid=(s // ROW_TILE,),
        in_specs=[_row_spec(), _row_spec(), _vec_spec(), _row_spec()],
        out_specs=[pl.BlockSpec((1, 128), lambda i: (0, 0)), _row_spec(), _row_spec(), _vec_spec()],
        out_shape=[jax.ShapeDtypeStruct((1, 128), F32), jax.ShapeDtypeStruct((s, D_MODEL), F32),
                   jax.ShapeDtypeStruct((s, D_MODEL), BF16), jax.ShapeDtypeStruct((1, D_MODEL), F32)],
        compiler_params=_params(("arbitrary",)))(x1, ffn, gate2, target)


CONV_TILE = 512
N_CONV_TILES = D_FF // CONV_TILE


def _shift_rows(a, k, row):
    n = a.shape[0]
    if k > 0:
        return jnp.where(row >= k, pltpu.roll(a, k, 0), 0.0)
    return jnp.where(row < n + k, pltpu.roll(a, n + k, 0), 0.0)


def _conv_gate_fwd(u, conv_w, conv_b, fused=None, fused_arrays=()):
    s = u.shape[0]

    def body(a_ref, g_ref, w_ref, b_ref, y_ref):
        a = a_ref[...]
        w = w_ref[...]
        row = lax.broadcasted_iota(jnp.int32, a.shape, 0)
        ac = b_ref[...] + _shift_rows(a, 2, row) * w[0:1] + _shift_rows(a, 1, row) * w[1:2] + a * w[2:3]
        y_ref[...] = (ac * _sigmoid(ac) * g_ref[...]).astype(BF16)

    def first_last():
        i = pl.program_id(0)
        return i == 0, i == N_CONV_TILES - 1

    col = lambda off: pl.BlockSpec((s, CONV_TILE), lambda i: (0, i + off))
    (y,), extra = _host_call(
        body, 4, 1, fused, first_last, name="conv_gate_fwd", grid=(N_CONV_TILES,),
        in_specs=[col(0), col(N_CONV_TILES), pl.BlockSpec((3, CONV_TILE), lambda i: (0, i)),
                  pl.BlockSpec((1, CONV_TILE), lambda i: (0, i))],
        out_specs=[col(0)], out_shape=[jax.ShapeDtypeStruct((s, D_FF), BF16)], scratch_shapes=[], sem=("parallel",),
        operands=[u, u, conv_w, conv_b] + list(fused_arrays))
    return y if fused is None else (y, extra)


def _conv_gate_bwd(u, dy, conv_w, conv_b):
    s = u.shape[0]

    def body(a_ref, g_ref, dy_ref, w_ref, b_ref, da_ref, dg_ref, gw_ref, gb_ref):
        a = a_ref[...]
        w = w_ref[...]
        row = lax.broadcasted_iota(jnp.int32, a.shape, 0)
        a1 = _shift_rows(a, 1, row)
        a2 = _shift_rows(a, 2, row)
        ac = b_ref[...] + a2 * w[0:1] + a1 * w[1:2] + a * w[2:3]
        sg = _sigmoid(ac)
        dyv = dy_ref[...].astype(F32)
        dg_ref[...] = (dyv * (ac * sg)).astype(BF16)
        dac = dyv * g_ref[...] * _dsilu(ac, sg)
        gb_ref[...] = jnp.sum(dac, axis=0, keepdims=True)
        gw_ref[0:1, :] = jnp.sum(dac * a2, axis=0, keepdims=True)
        gw_ref[1:2, :] = jnp.sum(dac * a1, axis=0, keepdims=True)
        gw_ref[2:3, :] = jnp.sum(dac * a, axis=0, keepdims=True)
        da = dac * w[2:3] + _shift_rows(dac, -1, row) * w[1:2] + _shift_rows(dac, -2, row) * w[0:1]
        da_ref[...] = da.astype(BF16)

    col = lambda off: pl.BlockSpec((s, CONV_TILE), lambda i: (0, i + off))
    return pl.pallas_call(
        body, name="conv_gate_bwd", grid=(N_CONV_TILES,),
        in_specs=[col(0), col(N_CONV_TILES), col(0), pl.BlockSpec((3, CONV_TILE), lambda i: (0, i)),
                  pl.BlockSpec((1, CONV_TILE), lambda i: (0, i))],
        out_specs=[col(0), col(0), pl.BlockSpec((3, CONV_TILE), lambda i: (0, i)),
                   pl.BlockSpec((1, CONV_TILE), lambda i: (0, i))],
        out_shape=[jax.ShapeDtypeStruct((s, D_FF), BF16), jax.ShapeDtypeStruct((s, D_FF), BF16),
                   jax.ShapeDtypeStruct((3, D_FF), F32), jax.ShapeDtypeStruct((1, D_FF), F32)],
        compiler_params=_params(("parallel",)))(u, u, dy, conv_w, conv_b)


HG_TILE = 256
CHUNK_UNROLL = 8


def _unrolled_loop(n, body, init):
    def group(i, carry):
        for u in range(CHUNK_UNROLL):
            carry = body(i * CHUNK_UNROLL + u, carry)
        return carry

    return lax.fori_loop(0, n // CHUNK_UNROLL, group, init)


def _head_col(off):
    return pl.BlockSpec((SEQ, HEAD_DIM), lambda h: (0, h + off))


def _hgrn_gates(hq, hf, lb, pos):
    q = hq * _sigmoid(hq)
    sig = _sigmoid(hf)
    f = lb + (1.0 - lb) * sig
    gl = jnp.log(f)
    for sh in (1, 2, 4, 8):
        gl = gl + jnp.where(pos >= sh, pltpu.roll(gl, sh, 0), 0.0)
    return q, sig, f, 1.0 - f, gl


def _lower_bound(lbl):
    return 1.0 / (1.0 + jnp.exp(lbl[1:2, :] - lbl[0:1, :]))


def _head_first_last():
    h = pl.program_id(0)
    return h == 0, h == HEADS - 1


CHUNKS_PER_TILE = HG_TILE // CHUNK


def _chunk_end(x, pos):
    y = jnp.where(pos == CHUNK - 1, x, 0.0)
    for sh in (1, 2, 4, 8):
        y = y + jnp.where(pos < CHUNK - sh, pltpu.roll(y, x.shape[0] - sh, 0), 0.0)
    return y


def _suffix_in_chunk(x, pos):
    for sh in (1, 2, 4, 8):
        x = x + jnp.where(pos < CHUNK - sh, pltpu.roll(x, x.shape[0] - sh, 0), 0.0)
    return x


def _prefix_in_chunk(x, pos):
    for sh in (1, 2, 4, 8):
        x = x + jnp.where(pos >= sh, pltpu.roll(x, sh, 0), 0.0)
    return x


def _pair_decays(f, pos):
    shifted = jnp.where(pos >= 1, f, 0.0)
    e = shifted
    yield 1, e
    for d in range(2, CHUNK):
        shifted = pltpu.roll(shifted, 1, 0)
        e = e * shifted
        yield d, e


def _chunk_rows(cc):
    return slice(cc * CHUNK, (cc + 1) * CHUNK)


def _outer_products(lhs_b, rhs_b, dst, i):
    for cc in range(CHUNKS_PER_TILE):
        dst[i * CHUNKS_PER_TILE + cc] = lax.dot_general(lhs_b[_chunk_rows(cc)], rhs_b[_chunk_rows(cc)], TN,
                                                        preferred_element_type=F32)


def _state_scan(n_chunks, gl_s, u_s, keep, reverse):
    def step(k, st):
        c = n_chunks - 1 - k if reverse else k
        keep[c] = st.astype(BF16)
        gl = gl_s[pl.ds(pl.multiple_of(c * CHUNK, CHUNK), CHUNK), :]
        return st * jnp.exp(gl[CHUNK - 1:CHUNK, :]) + u_s[c]

    _unrolled_loop(n_chunks, step, jnp.zeros((HEAD_DIM, HEAD_DIM), F32))


def _hgrn_fwd(proj, lb_logits, norm_w, fused=None, fused_arrays=()):
    n_tiles = SEQ // HG_TILE
    n_chunks = SEQ // CHUNK
    fused_arrays = list(fused_arrays)

    def body(hq_ref, hf_ref, hi_ref, hg_ref, lbl_ref, nw_ref, aout_ref, opre_ref, qt_s, gl_s, u_s, st_s):
        lb = _lower_bound(lbl_ref[...])
        ones = jnp.ones((HEAD_DIM, HEAD_DIM), BF16)
        pos = lax.broadcasted_iota(jnp.int32, (HG_TILE, HEAD_DIM), 0) % CHUNK

        def tile(i, carry):
            rows = pl.ds(pl.multiple_of(i * HG_TILE, HG_TILE), HG_TILE)
            v = hi_ref[rows, :]
            q, _sig, f, kk, gl = _hgrn_gates(hq_ref[rows, :], hf_ref[rows, :], lb, pos)
            o = _lane_sum(q * kk, ones) * v
            for d, e in _pair_decays(f, pos):
                o = o + _lane_sum(q * pltpu.roll(kk, d, 0) * e, ones) * pltpu.roll(v, d, 0)
            opre_ref[rows, :] = o
            qt_s[rows, :] = q * jnp.exp(gl)
            gl_s[rows, :] = gl
            kt = kk * jnp.exp(_chunk_end(gl, pos) - gl)
            _outer_products(v.astype(BF16), kt.astype(BF16), u_s, i)
            return carry

        lax.fori_loop(0, n_tiles, tile, 0)
        _state_scan(n_chunks, gl_s, u_s, st_s, reverse=False)

        def finish(i, carry):
            rows = pl.ds(pl.multiple_of(i * HG_TILE, HG_TILE), HG_TILE)
            qt_b = qt_s[rows, :].astype(BF16)
            past = [lax.dot_general(qt_b[_chunk_rows(cc)], st_s[i * CHUNKS_PER_TILE + cc], NT,
                                    preferred_element_type=F32) for cc in range(CHUNKS_PER_TILE)]
            o = opre_ref[rows, :] + jnp.concatenate(past, axis=0)
            opre_ref[rows, :] = o
            hg = hg_ref[rows, :]
            rs = lax.rsqrt(jnp.mean(o * o, axis=-1, keepdims=True) + EPS)
            aout_ref[rows, :] = ((o * rs) * nw_ref[...] * (hg * _sigmoid(hg))).astype(BF16)
            return carry

        lax.fori_loop(0, n_tiles, finish, 0)

    return _host_call(
        body, 6, 2, fused, _head_first_last, name="hgrn_fwd", grid=(HEADS,),
        in_specs=[_head_col(0), _head_col(HEADS), _head_col(2 * HEADS), _head_col(3 * HEADS),
                  pl.BlockSpec((2, HEAD_DIM), lambda h: (0, h)), pl.BlockSpec((1, HEAD_DIM), lambda h: (0, 0))],
        out_specs=[_head_col(0), _head_col(0)],
        out_shape=[jax.ShapeDtypeStruct((SEQ, HEADS * HEAD_DIM), BF16), jax.ShapeDtypeStruct((SEQ, HEADS * HEAD_DIM), F32)],
        scratch_shapes=[pltpu.VMEM((SEQ, HEAD_DIM), F32)] * 2 + [pltpu.VMEM((n_chunks, HEAD_DIM, HEAD_DIM), F32),
                                                                 pltpu.VMEM((n_chunks, HEAD_DIM, HEAD_DIM), BF16)],
        sem=("parallel",), operands=[proj, proj, proj, proj, lb_logits, norm_w] + fused_arrays)


def _hgrn_bwd(proj, lb_logits, norm_w, o_pre, d_aout, fused=None, fused_arrays=()):
    n_tiles = SEQ // HG_TILE
    n_chunks = SEQ // CHUNK

    def body(hq_ref, hf_ref, hi_ref, hg_ref, lbl_ref, nw_ref, opre_ref, da_ref,
             dhq_ref, dhf_ref, dhi_ref, dhg_ref, dlog_ref, gnw_ref,
             q_s, k_s, gl_s, do_s, dq_s, dk_s, dv_s, u_s, st_s, rt_s):
        h = pl.program_id(0)
        lb = _lower_bound(lbl_ref[...])
        nw = nw_ref[...]
        ones = jnp.ones((HEAD_DIM, HEAD_DIM), BF16)
        pos = lax.broadcasted_iota(jnp.int32, (HG_TILE, HEAD_DIM), 0) % CHUNK

        @pl.when(h == 0)
        def _():
            gnw_ref[...] = jnp.zeros_like(gnw_ref)

        def tile(i, carry):
            rows = pl.ds(pl.multiple_of(i * HG_TILE, HG_TILE), HG_TILE)
            v = hi_ref[rows, :]
            q, _sig, f, kk, gl = _hgrn_gates(hq_ref[rows, :], hf_ref[rows, :], lb, pos)
            o = opre_ref[rows, :]
            hg = hg_ref[rows, :]
            da = da_ref[rows, :]
            rs = lax.rsqrt(jnp.mean(o * o, axis=-1, keepdims=True) + EPS)
            oh = o * rs
            sg = _sigmoid(hg)
            dnorm = da * (hg * sg)
            dhg_ref[rows, :] = (da * (oh * nw) * _dsilu(hg, sg)).astype(BF16)
            gnw_ref[...] += jnp.sum(dnorm * oh, axis=0, keepdims=True)
            doh = dnorm * nw
            do = rs * (doh - oh * jnp.mean(doh * oh, axis=-1, keepdims=True))

            d_a = _lane_sum(do * v, ones)
            dq = d_a * kk
            dk = d_a * q
            dv = _lane_sum(q * kk, ones) * do
            for d, e in _pair_decays(f, pos):
                ks = pltpu.roll(kk, d, 0)
                a_d = _lane_sum(q * ks * e, ones)
                d_a = _lane_sum(do * pltpu.roll(v, d, 0), ones) * e
                dq = dq + d_a * ks
                dk = dk + pltpu.roll(d_a * q, HG_TILE - d, 0)
                dv = dv + pltpu.roll(a_d * do, HG_TILE - d, 0)
            q_s[rows, :] = q
            k_s[rows, :] = kk
            gl_s[rows, :] = gl
            do_s[rows, :] = do
            dq_s[rows, :] = dq
            dk_s[rows, :] = dk
            dv_s[rows, :] = dv
            kt = kk * jnp.exp(_chunk_end(gl, pos) - gl)
            _outer_products(v.astype(BF16), kt.astype(BF16), u_s, i)
            return carry

        lax.fori_loop(0, n_tiles, tile, 0)
        _state_scan(n_chunks, gl_s, u_s, st_s, reverse=False)

        def reverse_increments(i, carry):
            rows = pl.ds(pl.multiple_of(i * HG_TILE, HG_TILE), HG_TILE)
            qt = q_s[rows, :] * jnp.exp(gl_s[rows, :])
            _outer_products(do_s[rows, :].astype(BF16), qt.astype(BF16), u_s, i)
            return carry

        lax.fori_loop(0, n_tiles, reverse_increments, 0)
        _state_scan(n_chunks, gl_s, u_s, rt_s, reverse=True)

        def finish(i, dlb):
            rows = pl.ds(pl.multiple_of(i * HG_TILE, HG_TILE), HG_TILE)
            q = q_s[rows, :]
            kk = k_s[rows, :]
            gl = gl_s[rows, :]
            gll = _chunk_end(gl, pos)
            ekt = jnp.exp(gll - gl)
            do_b = do_s[rows, :].astype(BF16)
            v_b = hi_ref[rows, :].astype(BF16)
            kt_b = (kk * ekt).astype(BF16)
            dq_far, dk_far, dv_far, across = [], [], [], []
            for cc in range(CHUNKS_PER_TILE):
                st = st_s[i * CHUNKS_PER_TILE + cc]
                rt = rt_s[i * CHUNKS_PER_TILE + cc]
                sl = _chunk_rows(cc)
                dq_far.append(jnp.dot(do_b[sl], st, preferred_element_type=F32))
                dk_far.append(jnp.dot(v_b[sl], rt, preferred_element_type=F32))
                dv_far.append(lax.dot_general(kt_b[sl], rt, NT, preferred_element_type=F32))
                both = jnp.sum(st.astype(F32) * rt.astype(F32), axis=0, keepdims=True)
                across.append(jnp.broadcast_to(both, (CHUNK, HEAD_DIM)))
            dq = dq_s[rows, :] + jnp.concatenate(dq_far, axis=0) * jnp.exp(gl)
            dk_in = dk_s[rows, :]
            dk_out = jnp.concatenate(dk_far, axis=0) * ekt
            dk = dk_in + dk_out
            dv = dv_s[rows, :] + jnp.concatenate(dv_far, axis=0)
            pc = kk * dk_out
            dgl = (_suffix_in_chunk(q * dq - kk * dk_in, pos) + (_prefix_in_chunk(pc, pos) - pc)
                   + jnp.concatenate(across, axis=0) * jnp.exp(gll))
            hf = hf_ref[rows, :]
            sig = _sigmoid(hf)
            f = lb + (1.0 - lb) * sig
            df = dgl / f - dk
            dhf_ref[rows, :] = (df * (1.0 - lb) * sig * (1.0 - sig)).astype(BF16)
            hq = hq_ref[rows, :]
            dhq_ref[rows, :] = (dq * _dsilu(hq, _sigmoid(hq))).astype(BF16)
            dhi_ref[rows, :] = dv.astype(BF16)
            return dlb + jnp.sum(df * (1.0 - sig), axis=0, keepdims=True)

        dlb = lax.fori_loop(0, n_tiles, finish, jnp.zeros((1, HEAD_DIM), F32))
        dl0 = lb * (1.0 - lb) * dlb
        dlog_ref[0:1, :] = dl0
        dlog_ref[1:2, :] = -dl0

    wide = HEADS * HEAD_DIM
    return _host_call(
        body, 8, 6, fused, _head_first_last, name="hgrn_bwd", grid=(HEADS,),
        in_specs=[_head_col(0), _head_col(HEADS), _head_col(2 * HEADS), _head_col(3 * HEADS),
                  pl.BlockSpec((2, HEAD_DIM), lambda h: (0, h)), pl.BlockSpec((1, HEAD_DIM), lambda h: (0, 0)),
                  _head_col(0), _head_col(0)],
        out_specs=[_head_col(0)] * 4 + [pl.BlockSpec((2, HEAD_DIM), lambda h: (0, h)),
                                        pl.BlockSpec((1, HEAD_DIM), lambda h: (0, 0))],
        out_shape=[jax.ShapeDtypeStruct((SEQ, wide), BF16)] * 4 + [jax.ShapeDtypeStruct((2, wide), F32),
                                                                    jax.ShapeDtypeStruct((1, HEAD_DIM), F32)],
        scratch_shapes=[pltpu.VMEM((SEQ, HEAD_DIM), F32)] * 7 + [pltpu.VMEM((n_chunks, HEAD_DIM, HEAD_DIM), F32),
                                                                 pltpu.VMEM((n_chunks, HEAD_DIM, HEAD_DIM), BF16),
                                                                 pltpu.VMEM((n_chunks, HEAD_DIM, HEAD_DIM), BF16)],
        sem=("arbitrary",),
        operands=[proj, proj, proj, proj, lb_logits, norm_w, o_pre, d_aout] + list(fused_arrays))


Q_TILE = 512
ATT_SCALE = HEAD_DIM ** -0.5
ATT_OFF = 4 * HEADS


def _qk_prep(proj, q_w, k_w, fused=None, fused_arrays=()):
    def body(aq_ref, ak_ref, av_ref, qw_ref, kw_ref, qn_ref, kn_ref, v_ref):
        aq = aq_ref[...]
        ak = ak_ref[...]
        qn_ref[...] = (aq * lax.rsqrt(jnp.mean(aq * aq, axis=-1, keepdims=True) + EPS) * qw_ref[...]).astype(BF16)
        kn_ref[...] = (ak * lax.rsqrt(jnp.mean(ak * ak, axis=-1, keepdims=True) + EPS) * kw_ref[...]).astype(BF16)
        v_ref[...] = av_ref[...].astype(BF16)

    wide = HEADS * HEAD_DIM
    vec = pl.BlockSpec((1, HEAD_DIM), lambda h: (0, 0))
    return _host_call(
        body, 5, 3, fused, _head_first_last, name="qk_prep", grid=(HEADS,),
        in_specs=[_head_col(ATT_OFF), _head_col(ATT_OFF + HEADS), _head_col(ATT_OFF + 2 * HEADS), vec, vec],
        out_specs=[_head_col(0)] * 3, out_shape=[jax.ShapeDtypeStruct((SEQ, wide), BF16)] * 3,
        scratch_shapes=[], sem=("parallel",), operands=[proj, proj, proj, q_w, k_w] + list(fused_arrays))


def _alibi_slopes():
    slopes = jnp.exp2(-8.0 * jnp.arange(1, HEADS + 1, dtype=F32) / HEADS)
    return jnp.broadcast_to(slopes[:, None, None], (HEADS, 1, HEAD_DIM))


SLOPE_SPEC = pl.BlockSpec((None, 1, HEAD_DIM), lambda h, i: (h, 0, 0))


N_Q_TILES = SEQ // Q_TILE
K_BLOCK = 512
NOT_ATTENDED = 1e35


def _att_tables():
    o = jnp.arange(N_Q_TILES, dtype=jnp.int32)[:, None, None]
    r = jnp.arange(Q_TILE, dtype=jnp.int32)[None, :, None]
    c = jnp.arange(K_BLOCK, dtype=jnp.int32)[None, None, :]
    dist = o * Q_TILE + r - c
    mult = ((dist <= 128).astype(F32) + (((dist % 4) == 0) & (dist <= 512)).astype(F32)
            + ((dist % 16) == 0).astype(F32))
    valid = (dist >= 0) & (mult > 0)
    return (jnp.where(valid, dist.astype(F32), NOT_ATTENDED),
            jnp.where(valid, jnp.log(jnp.maximum(mult, 1.0)), 0.0))


TABLE_SPEC = pl.BlockSpec((N_Q_TILES, Q_TILE, K_BLOCK), lambda h, i: (0, 0, 0))


def _att_block(q, k_ref, j, i, slope, dist_ref, lmul_ref):
    rows = pl.ds(pl.multiple_of(j * K_BLOCK, K_BLOCK), K_BLOCK)
    off = i - j * (K_BLOCK // Q_TILE)
    s = lax.dot_general(q, k_ref[rows, :], NT, preferred_element_type=F32) * ATT_SCALE
    return s - slope * dist_ref[off] + lmul_ref[off], rows


def _n_key_blocks(i):
    return (i + K_BLOCK // Q_TILE) // (K_BLOCK // Q_TILE)


def _att_first_last():
    h, i = pl.program_id(0), pl.program_id(1)
    return (h == 0) & (i == 0), (h == HEADS - 1) & (i == N_Q_TILES - 1)


def _attn_fwd(qn, kn, vb, fused=None, fused_arrays=()):
    def body(q_ref, k_ref, v_ref, sl_ref, dist_ref, lmul_ref, o_ref, lse_ref):
        i = pl.program_id(1)
        q = q_ref[...]
        slope = sl_ref[0:1, 0:1]

        def step(j, carry):
            m, l, acc = carry
            sb, rows = _att_block(q, k_ref, j, i, slope, dist_ref, lmul_ref)
            m_new = jnp.maximum(m, jnp.max(sb, axis=-1, keepdims=True))
            alpha = jnp.exp(m - m_new)
            p = jnp.exp(sb - m_new)
            l = alpha * l + jnp.sum(p, axis=-1, keepdims=True)
            acc = alpha * acc + jnp.dot(p.astype(BF16), v_ref[rows, :], preferred_element_type=F32)
            return m_new, l, acc

        m, l, acc = lax.fori_loop(0, _n_key_blocks(i), step,
                                  (jnp.full((Q_TILE, 1), -1e30, F32), jnp.zeros((Q_TILE, 1), F32),
                                   jnp.zeros((Q_TILE, HEAD_DIM), F32)))
        o_ref[...] = acc / l
        lse_ref[...] = m + jnp.log(l)

    wide = HEADS * HEAD_DIM
    qt = pl.BlockSpec((Q_TILE, HEAD_DIM), lambda h, i: (i, h))
    full = pl.BlockSpec((SEQ, HEAD_DIM), lambda h, i: (0, h))
    return _host_call(
        body, 6, 2, fused, _att_first_last, name="attn_fwd", grid=(HEADS, N_Q_TILES),
        in_specs=[qt, full, full, SLOPE_SPEC, TABLE_SPEC, TABLE_SPEC],
        out_specs=[qt, pl.BlockSpec((None, Q_TILE, 1), lambda h, i: (h, i, 0))],
        out_shape=[jax.ShapeDtypeStruct((SEQ, wide), F32), jax.ShapeDtypeStruct((HEADS, SEQ, 1), F32)],
        scratch_shapes=[], sem=("parallel", "parallel"),
        operands=[qn, kn, vb, _alibi_slopes(), *_att_tables()] + list(fused_arrays))


def _attn_bwd(qn, kn, vb, o, lse, d_mix, fused=None, fused_arrays=()):
    def body(q_ref, k_ref, v_ref, o_ref, lse_ref, do_ref, sl_ref, dist_ref, lmul_ref, dq_ref, dk_ref, dv_ref):
        i = pl.program_id(1)
        q = q_ref[...]
        do = do_ref[...]
        do_b = do.astype(BF16)
        slope = sl_ref[0:1, 0:1]
        lse = lse_ref[...]
        delta = jnp.sum(do * o_ref[...], axis=-1, keepdims=True)

        @pl.when(i == 0)
        def _():
            dk_ref[...] = jnp.zeros_like(dk_ref)
            dv_ref[...] = jnp.zeros_like(dv_ref)

        def step(j, dq):
            sb, rows = _att_block(q, k_ref, j, i, slope, dist_ref, lmul_ref)
            p = jnp.exp(sb - lse)
            dp = lax.dot_general(do_b, v_ref[rows, :], NT, preferred_element_type=F32)
            ds = (p * (dp - delta)).astype(BF16)
            dk_ref[rows, :] += lax.dot_general(ds, q, TN, preferred_element_type=F32) * ATT_SCALE
            dv_ref[rows, :] += lax.dot_general(p.astype(BF16), do_b, TN, preferred_element_type=F32)
            return dq + jnp.dot(ds, k_ref[rows, :], preferred_element_type=F32)

        dq = lax.fori_loop(0, _n_key_blocks(i), step, jnp.zeros((Q_TILE, HEAD_DIM), F32))
        dq_ref[...] = dq * ATT_SCALE

    wide = HEADS * HEAD_DIM
    qt = pl.BlockSpec((Q_TILE, HEAD_DIM), lambda h, i: (i, h))
    full = pl.BlockSpec((SEQ, HEAD_DIM), lambda h, i: (0, h))
    return _host_call(
        body, 9, 3, fused, _att_first_last, name="attn_bwd", grid=(HEADS, N_Q_TILES),
        in_specs=[qt, full, full, qt, pl.BlockSpec((None, Q_TILE, 1), lambda h, i: (h, i, 0)),
                  pl.BlockSpec((Q_TILE, HEAD_DIM), lambda h, i: (i, h + HEADS)), SLOPE_SPEC, TABLE_SPEC, TABLE_SPEC],
        out_specs=[qt, full, full], out_shape=[jax.ShapeDtypeStruct((SEQ, wide), F32)] * 3,
        scratch_shapes=[], sem=("parallel", "arbitrary"),
        operands=[qn, kn, vb, o, lse, d_mix, _alibi_slopes(), *_att_tables()] + list(fused_arrays))


def _qk_bwd(proj, q_w, k_w, dqn, dkn, dv):
    def body(aq_ref, ak_ref, qw_ref, kw_ref, dqn_ref, dkn_ref, dv_ref, daq_ref, dak_ref, dav_ref, gq_ref, gk_ref):
        h = pl.program_id(0)

        @pl.when(h == 0)
        def _():
            gq_ref[...] = jnp.zeros_like(gq_ref)
            gk_ref[...] = jnp.zeros_like(gk_ref)

        def one(a_ref, w_ref, d_ref, da_ref, g_ref):
            a = a_ref[...]
            d = d_ref[...]
            rs = lax.rsqrt(jnp.mean(a * a, axis=-1, keepdims=True) + EPS)
            ah = a * rs
            g_ref[...] += jnp.sum(d * ah, axis=0, keepdims=True)
            dah = d * w_ref[...]
            da_ref[...] = (rs * (dah - ah * jnp.mean(dah * ah, axis=-1, keepdims=True))).astype(BF16)

        one(aq_ref, qw_ref, dqn_ref, daq_ref, gq_ref)
        one(ak_ref, kw_ref, dkn_ref, dak_ref, gk_ref)
        dav_ref[...] = dv_ref[...].astype(BF16)

    wide = HEADS * HEAD_DIM
    vec = pl.BlockSpec((1, HEAD_DIM), lambda h: (0, 0))
    return pl.pallas_call(
        body, name="qk_bwd", grid=(HEADS,),
        in_specs=[_head_col(ATT_OFF), _head_col(ATT_OFF + HEADS), vec, vec, _head_col(0), _head_col(0), _head_col(0)],
        out_specs=[_head_col(0)] * 3 + [vec, vec],
        out_shape=[jax.ShapeDtypeStruct((SEQ, wide), BF16)] * 3 + [jax.ShapeDtypeStruct((1, HEAD_DIM), F32)] * 2,
        compiler_params=_params(("arbitrary",)))(proj, proj, q_w, k_w, dqn, dkn, dv)


def _pair_sum(name, partial, theirs, core):
    _, r, c = theirs.shape
    tr = r // 2 if r % 16 == 0 else r

    def body(core_ref, a_ref, b_ref, o_ref):
        o_ref[...] = (a_ref[...].astype(F32) + b_ref[...].astype(F32)).astype(BF16)

    spec = pl.BlockSpec((None, tr, c), lambda q, i, core_ref: (q, i, 0))
    grid_spec = pltpu.PrefetchScalarGridSpec(
        num_scalar_prefetch=1, grid=(4, r // tr),
        in_specs=[pl.BlockSpec((None, tr, c), lambda q, i, core_ref: (2 * q + core_ref[0], i, 0)), spec],
        out_specs=spec)
    return pl.pallas_call(body, name=name, grid_spec=grid_spec, out_shape=jax.ShapeDtypeStruct(theirs.shape, BF16),
                          compiler_params=_params(("parallel", "parallel")))(core, partial, theirs)


def _adamw_step(w, m, v, g):
    nm = ADAM_B1 * m + (1.0 - ADAM_B1) * g
    nv = ADAM_B2 * v + (1.0 - ADAM_B2) * (g * g)
    m_hat = nm / (1.0 - ADAM_B1 ** ADAM_STEP)
    v_hat = nv / (1.0 - ADAM_B2 ** ADAM_STEP)
    return -ADAM_LR * (m_hat / (jnp.sqrt(v_hat) + ADAM_EPS) + ADAM_WD * w), nm, nv


def _adamw(name, w, m, v, addends, tr=None):
    r, c = w.shape
    tr = r if tr is None else tr
    n_add = len(addends)

    def body(*refs):
        w_ref, m_ref, v_ref = refs[:3]
        add_refs = refs[3:3 + n_add]
        g_ref, d_ref, nm_ref, nv_ref = refs[3 + n_add:]
        g = add_refs[0][...].astype(F32)
        for a_ref in add_refs[1:]:
            g = g + a_ref[...].astype(F32)
        g_ref[...] = g
        d_ref[...], nm_ref[...], nv_ref[...] = _adamw_step(w_ref[...], m_ref[...], v_ref[...], g)

    spec = pl.BlockSpec((tr, c), lambda i: (i, 0))
    out = jax.ShapeDtypeStruct((r, c), F32)
    return pl.pallas_call(body, name=name, grid=(r // tr,), in_specs=[spec] * (3 + n_add), out_specs=[spec] * 4,
                          out_shape=[out] * 4, compiler_params=_params(("parallel",)))(w, m, v, *addends)


def _adamw_reduced(name, w, m, v, chip_sums, received, chip, tr):
    r, c = w.shape

    def body(chip_ref, w_ref, m_ref, v_ref, own_ref, r0_ref, r1_ref, r2_ref, g_ref, d_ref, nm_ref, nv_ref):
        g = ((own_ref[...].astype(F32) + r0_ref[...].astype(F32)) + r1_ref[...].astype(F32)) + r2_ref[...].astype(F32)
        g_ref[...] = g
        d_ref[...], nm_ref[...], nv_ref[...] = _adamw_step(w_ref[...], m_ref[...], v_ref[...], g)

    spec = pl.BlockSpec((tr, c), lambda i, chip_ref: (i, 0))

    def slot(k):
        return pl.BlockSpec((None, tr, c), lambda i, chip_ref: (k, i, 0))

    grid_spec = pltpu.PrefetchScalarGridSpec(
        num_scalar_prefetch=1, grid=(r // tr,),
        in_specs=[spec, spec, spec, pl.BlockSpec((None, tr, c), lambda i, chip_ref: (chip_ref[0], i, 0)),
                  slot(0), slot(1), slot(2)],
        out_specs=[spec] * 4)
    out = jax.ShapeDtypeStruct((r, c), F32)
    return pl.pallas_call(body, name=name, grid_spec=grid_spec, out_shape=[out] * 4,
                          compiler_params=_params(("parallel",)))(chip, w, m, v, chip_sums, received, received, received)


def _sum_devices(gathered):
    _, r, c = gathered.shape

    def body(g_ref, o_ref):
        acc = g_ref[0]
        for d in range(1, N_DEV):
            acc = acc + g_ref[d]
        o_ref[...] = acc

    return pl.pallas_call(body, name="sum_devices", out_shape=jax.ShapeDtypeStruct((r, c), F32))(gathered)


def _pack_rows(vectors, rows):
    flat = jnp.concatenate([v.reshape(-1) for v in vectors])
    return jnp.pad(flat, (0, rows * 128 - flat.shape[0])).reshape(rows, 128)


def _unpack(flat, shapes):
    out, off = [], 0
    for shp in shapes:
        n = 1
        for d in shp:
            n *= d
        out.append(flat[off:off + n].reshape(shp))
        off += n
    return out


def _device_step(xs, tgt, mod, norm1_w, norm2_w, lb_logits, hg_norm_w, q_norm_w, k_norm_w, conv_w_full, conv_b,
                 win_g, w_out_x, w_up_x, w_down_x, core=None):
    fused = core is not None
    shift1, scale1, gate1, shift2, scale2, gate2 = (mod[k] for k in range(6))

    h, rstd1 = _norm_fwd("norm1_fwd", xs, norm1_w, scale1, shift1)
    if fused:
        near = (0, 1, 2)
        head_rows, tail_rows = (0, UP_HEAD_ROWS), (UP_HEAD_ROWS, D_MODEL - UP_HEAD_ROWS)
        proj, (wout_g, wup_g) = _mm_blocked_rhs(
            "mm_in", h, win_g, fused_arrays=[w_out_x, w_up_x],
            fused=[_FusedCopies("gather", [w_out_x]), _FusedCopies("gather", [w_up_x], peers=near, rows=head_rows)])
        (a_out, o_pre), (wup_g,) = _hgrn_fwd(
            proj, lb_logits, hg_norm_w, fused_arrays=[w_up_x, wup_g],
            fused=_FusedCopies("gather_more", [w_up_x, wup_g], peers=near, rows=tail_rows, relay_rows=head_rows))
        wout_g, = _forward_to_sibling("allgather_stage2_out", [wout_g])
        wout_full = wout_g.reshape(D_MODEL, D_MODEL)
        (qn, kn, vb), _ = _qk_prep(proj, q_norm_w, k_norm_w)
        (att_o, lse), (wup_g,) = _attn_fwd(qn, kn, vb, _FusedCopies("relay", [wup_g], rows=tail_rows), [wup_g])
    else:
        proj = _mm_blocked_rhs("mm_in", h, win_g)
        (a_out, o_pre), _ = _hgrn_fwd(proj, lb_logits, hg_norm_w)
        wup_g, wout_full, wdown_full = w_up_x, w_out_x, w_down_x
        (qn, kn, vb), _ = _qk_prep(proj, q_norm_w, k_norm_w)
        (att_o, lse), _ = _attn_fwd(qn, kn, vb)
    mixin = jnp.concatenate([a_out, att_o.astype(BF16)], axis=1)
    if fused:
        half = D_MODEL // 2
        mix, (wup_g,) = _mm_plain("mm_out", mixin, wout_full, NN, 512, 1024, F32,
                                  fused=_FusedCopies("forward", [wup_g], rows=(0, half)), fused_arrays=[wup_g])
        (x1, h2, rstd2), (wup_g,) = _norm_fwd(
            "norm2_fwd", xs, norm2_w, scale2, shift2, resid=mix, gate=gate1,
            fused=_FusedCopies("forward", [wup_g], rows=(half, half)), fused_arrays=[wup_g])
    else:
        mix = _mm_plain("mm_out", mixin, wout_full, NN, 512, 1024, F32)
        x1, h2, rstd2 = _norm_fwd("norm2_fwd", xs, norm2_w, scale2, shift2, resid=mix, gate=gate1)
    if fused:
        u, (wdown_g,) = _mm_blocked_rhs("mm_up", h2, wup_g, fused=_FusedCopies("gather", [w_down_x]),
                                        fused_arrays=[w_down_x])
        y, (wdown_g,) = _conv_gate_fwd(u, conv_w_full, conv_b, _FusedCopies("forward", [wdown_g]), [wdown_g])
        wdown_full = wdown_g.reshape(D_FF, D_MODEL)
    else:
        u = _mm_blocked_rhs("mm_up", h2, wup_g)
        y = _conv_gate_fwd(u, conv_w_full, conv_b)
    ffn = _mm_plain("mm_down", y, wdown_full, NN, 512, 512, F32)
    loss_v, dout, dffn, dgate2 = _loss_head(x1, ffn, gate2, tgt)

    dy = _mm_plain("mm_down_dx", dffn, wdown_full, NT, 512, UP_BLK, BF16)
    gw_down = _mm_plain("mm_down_dw", y, dffn, TN, UP_BLK, 1024, BF16)
    da, dg, gconv_w, gconv_b = _conv_gate_bwd(u, dy, conv_w_full, conv_b)
    dh2 = _mm_halves_rhs_t("mm_up_dx", da, dg, wup_g)
    gw_up = _mm_halves_wgrad("mm_up_dw", h2, da, dg)
    if fused:
        part_up, part_down = gw_up, gw_down.reshape(N_DEV, FF_BLK, D_MODEL)
        (dx1, dmix, dshift2, dscale2, gnorm2, dgate1), (sib_up,) = _norm_bwd(
            "norm2_bwd", dh2, x1, rstd2, norm2_w, scale2, dout, mix=mix, gate=gate1,
            fused=_FusedCopies("sibling", [part_up]), fused_arrays=[part_up])
    else:
        dx1, dmix, dshift2, dscale2, gnorm2, dgate1 = _norm_bwd(
            "norm2_bwd", dh2, x1, rstd2, norm2_w, scale2, dout, mix=mix, gate=gate1)
    gw_out = _mm_plain("mm_out_dw", mixin, dmix, TN, 512, 1024, BF16)
    if fused:
        part_out = gw_out.reshape(N_DEV, OUT_BLK, D_MODEL)
        dmixin, (sib_out, sib_down) = _mm_plain(
            "mm_out_dx", dmix, wout_full, NT, 512, 1024, F32,
            fused=_FusedCopies("sibling", [part_out, part_down]), fused_arrays=[part_out, part_down])
        cs_up = _pair_sum("grad_pair_sum_up", part_up, sib_up, core)
        cs_out = _pair_sum("grad_pair_sum_out", part_out, sib_out, core)
        cs_down = _pair_sum("grad_pair_sum_down", part_down, sib_down, core)
        (dhq, dhf, dhi, dhg, glog, ghg), (fc_up,) = _hgrn_bwd(
            proj, lb_logits, hg_norm_w, o_pre, dmixin, _FusedCopies("chips", [cs_up]), [cs_up])
        (dqn, dkn, dvv), (fc_down,) = _attn_bwd(qn, kn, vb, att_o, lse, dmixin,
                                                _FusedCopies("chips", [cs_down]), [cs_down])
    else:
        dmixin = _mm_plain("mm_out_dx", dmix, wout_full, NT, 512, 1024, F32)
        (dhq, dhf, dhi, dhg, glog, ghg), _ = _hgrn_bwd(proj, lb_logits, hg_norm_w, o_pre, dmixin)
        (dqn, dkn, dvv), _ = _attn_bwd(qn, kn, vb, att_o, lse, dmixin)
    daq, dak, dav, gqw, gkw = _qk_bwd(proj, q_norm_w, k_norm_w, dqn, dkn, dvv)
    dproj = jnp.concatenate([dhq, dhf, dhi, dhg, daq, dak, dav], axis=1)
    if fused:
        gw_in, (fc_out,) = _mm_wgrad_blocked("mm_in_dw", h, dproj, fused=_FusedCopies("chips", [cs_out]),
                                             fused_arrays=[cs_out])
        from_sibling, = _exchange_sibling("grad_exchange_sibling_b", [gw_in])
        cs_in = _pair_sum("grad_pair_sum_in", gw_in, from_sibling, core)
        dh, (fc_in,) = _mm_blocked_rhs_t("mm_in_dx", dproj, win_g, fused_arrays=[cs_in],
                                         fused=_FusedCopies("chips", [cs_in], rows=W_IN_EXCHANGE_ROWS[0]))
        (grad_x, dshift1, dscale1, gnorm1), (fc_in,) = _norm_bwd(
            "norm1_bwd", dh, xs, rstd1, norm1_w, scale1, dx1, fused_arrays=[cs_in, fc_in],
            fused=_FusedCopies("chips_more", [cs_in, fc_in], rows=W_IN_EXCHANGE_ROWS[1]))
        large = [(cs_in, fc_in), (cs_out, fc_out), (cs_up, fc_up), (cs_down, fc_down)]
    else:
        gw_in = _mm_wgrad_blocked("mm_in_dw", h, dproj)
        dh = _mm_blocked_rhs_t("mm_in_dx", dproj, win_g)
        grad_x, dshift1, dscale1, gnorm1 = _norm_bwd("norm1_bwd", dh, xs, rstd1, norm1_w, scale1, dx1)
        large = [gw_in, gw_out, gw_up, gw_down]
    gmod = jnp.concatenate([dshift1, dscale1, dgate1, dshift2, dscale2, dgate2], axis=1)
    return (loss_v, grad_x, gmod, gnorm1, gnorm2, glog, ghg, gqw, gkw, gconv_b, gconv_w, *large)


def kernel(x, c, w_ada, b_ada, norm1_w, w_in, lb_logits, hg_norm_w, q_norm_w, k_norm_w, w_out, norm2_w, w_up, conv_w, conv_b, w_down, loss_target, m_w_ada, m_b_ada, m_norm1_w, m_w_in, m_lb_logits, m_hg_norm_w, m_q_norm_w, m_k_norm_w, m_w_out, m_norm2_w, m_w_up, m_conv_w, m_conv_b, m_w_down, v_w_ada, v_b_ada, v_norm1_w, v_w_in, v_lb_logits, v_hg_norm_w, v_q_norm_w, v_k_norm_w, v_w_out, v_norm2_w, v_w_up, v_conv_w, v_conv_b, v_w_down):
    ix, iy, ic = lax.axis_index("x"), lax.axis_index("y"), lax.axis_index("c")
    me = 4 * ix + 2 * iy + ic
    my_chip = 2 * ix + iy

    xs = x[0]
    tgt = loss_target[0]

    win_g, = _allgather_weights([w_in[0].astype(BF16)])

    c_all = _allgather_vmem(c.reshape(8, D_MODEL // 8), "allgather_c").reshape(N_DEV, D_MODEL)
    b_blk = lax.dynamic_slice_in_dim(b_ada, me * ADA_BLK, ADA_BLK, axis=1)
    mod_cols = _ada_fwd(c_all, w_ada[0], b_blk)
    mod_all = _allgather_vmem(mod_cols, "allgather_mod").reshape(N_DEV, N_DEV, ADA_BLK)
    mod = lax.dynamic_index_in_dim(mod_all, me, axis=1, keepdims=False).reshape(6, 1, D_MODEL)

    conv_w_all = _allgather_vmem(_pack_rows([conv_w[0]], 24), "allgather_conv_w").reshape(N_DEV, 24 * 128)
    conv_w_full = conv_w_all[:, :3 * FF_BLK].reshape(N_DEV, 3, FF_BLK).transpose(1, 0, 2).reshape(3, D_FF)

    (loss_v, grad_x, gmod, gnorm1, gnorm2, glog, ghg, gqw, gkw, gconv_b, gconv_w,
     rs_in, rs_out, rs_up, rs_down) = _device_step(
        xs, tgt, mod, norm1_w, norm2_w, lb_logits, hg_norm_w, q_norm_w, k_norm_w, conv_w_full, conv_b,
        win_g, w_out[0].astype(BF16), w_up[0].astype(BF16), w_down[0].astype(BF16),
        core=jnp.reshape(ic, (1,)).astype(jnp.int32))
    loss = lax.psum(loss_v[0, 0], AXES)

    small_shapes = [(1, 6 * D_MODEL), (1, D_MODEL), (1, D_MODEL), (2, HEADS * HEAD_DIM), (1, HEAD_DIM),
                    (1, HEAD_DIM), (1, HEAD_DIM), (1, D_FF), (3, D_FF)]
    small = [gmod, gnorm1, gnorm2, glog, ghg, gqw, gkw, gconv_b, gconv_w]
    n_small = sum(a.size for a in small)
    rows = -(-n_small // 1024) * 8
    gathered = _allgather_vmem(_pack_rows(small, rows), "allgather_small").reshape(N_DEV, rows, 128)
    summed = _sum_devices(gathered).reshape(-1)
    (g_b_ada, g_norm1, g_norm2, g_lb, g_hg, g_q, g_k, g_conv_b, g_conv_w_full) = _unpack(summed, small_shapes)
    g_conv_w = lax.dynamic_slice_in_dim(g_conv_w_full, me * FF_BLK, FF_BLK, axis=1)

    gmod_all = gathered[:, :6 * D_MODEL // 128, :].reshape(N_DEV, 6 * D_MODEL)
    gmod_cols = lax.dynamic_slice_in_dim(gmod_all, me * ADA_BLK, ADA_BLK, axis=1)
    g_w_ada_raw = _ada_wgrad(c_all, gmod_cols)

    chip = jnp.reshape(my_chip, (1,)).astype(jnp.int32)

    def big_update(name, w, m, v, rs, tr):
        chip_sums, received = rs
        return _adamw_reduced(name, w[0], m[0], v[0], chip_sums, received, chip, tr)

    r_ada = _adamw("adamw_w_ada", w_ada[0], m_w_ada[0], v_w_ada[0], [g_w_ada_raw], tr=256)
    r_in = big_update("adamw_w_in", w_in, m_w_in, v_w_in, rs_in, 256)
    r_out = big_update("adamw_w_out", w_out, m_w_out, v_w_out, rs_out, 128)
    r_up = big_update("adamw_w_up", w_up, m_w_up, v_w_up, rs_up, 256)
    r_down = big_update("adamw_w_down", w_down, m_w_down, v_w_down, rs_down, 176)
    r_convw = _adamw("adamw_conv_w", conv_w[0], m_conv_w[0], v_conv_w[0], [g_conv_w])

    rep_shapes = [(1, 6 * D_MODEL), (1, D_MODEL), (1, D_MODEL), (2, HEADS * HEAD_DIM), (1, HEAD_DIM),
                  (1, HEAD_DIM), (1, HEAD_DIM), (1, D_FF)]
    rep_rows = -(-sum(a * b for a, b in rep_shapes) // 1024) * 8
    pack = lambda arrs: _pack_rows(arrs, rep_rows)
    rep = _adamw("adamw_small",
                 pack([b_ada, norm1_w, norm2_w, lb_logits, hg_norm_w, q_norm_w, k_norm_w, conv_b]),
                 pack([m_b_ada, m_norm1_w, m_norm2_w, m_lb_logits, m_hg_norm_w, m_q_norm_w, m_k_norm_w, m_conv_b]),
                 pack([v_b_ada, v_norm1_w, v_norm2_w, v_lb_logits, v_hg_norm_w, v_q_norm_w, v_k_norm_w, v_conv_b]),
                 [pack([g_b_ada, g_norm1, g_norm2, g_lb, g_hg, g_q, g_k, g_conv_b])])
    rep = [_unpack(r.reshape(-1), rep_shapes) for r in rep]

    def big(r):
        return [a[None] for a in r]

    order = {"w_ada": big(r_ada), "b_ada": [r[0] for r in rep], "norm1_w": [r[1] for r in rep],
             "w_in": big(r_in), "lb_logits": [r[3] for r in rep], "hg_norm_w": [r[4] for r in rep],
             "q_norm_w": [r[5] for r in rep], "k_norm_w": [r[6] for r in rep], "w_out": big(r_out),
             "norm2_w": [r[2] for r in rep], "w_up": big(r_up), "conv_w": big(r_convw),
             "conv_b": [r[7] for r in rep], "w_down": big(r_down)}
    names = ["w_ada", "b_ada", "norm1_w", "w_in", "lb_logits", "hg_norm_w", "q_norm_w", "k_norm_w", "w_out",
             "norm2_w", "w_up", "conv_w", "conv_b", "w_down"]
    outs = [loss, grad_x[None]]
    for kind in range(4):
        outs += [order[n][kind] for n in names]
    return tuple(outs)
```

```python
import functools

import jax
import jax.numpy as jnp
from jax import lax
from jax.experimental import pallas as pl
from jax.experimental.pallas import tpu as pltpu

F32 = jnp.float32
BF16 = jnp.bfloat16

N_DEV = 8
SEQ = 2048
D_MODEL = 2048
HEADS = 8
HEAD_DIM = 128
IN_COLS = 7168
IN_BLK = IN_COLS // N_DEV
D_FF = 5632
UP_BLK = 2 * D_FF // N_DEV
FF_BLK = D_FF // N_DEV
ADA_BLK = 6 * D_MODEL // N_DEV
OUT_BLK = D_MODEL // N_DEV
EPS = 1e-6
CHUNK = 16
ROW_TILE = 256
V7X_VMEM_LIMIT = 56 * 1024 * 1024

ADAM_LR = 0.001
ADAM_B1 = 0.9
ADAM_B2 = 0.999
ADAM_EPS = 1e-08
ADAM_WD = 0.01
ADAM_STEP = 10

NN = (((1,), (0,)), ((), ()))
NT = (((1,), (1,)), ((), ()))
TN = (((0,), (0,)), ((), ()))
MESH = pl.DeviceIdType.MESH
AXES = ("x", "y", "c")


def _params(sem=None, vmem=V7X_VMEM_LIMIT):
    return pltpu.CompilerParams(dimension_semantics=sem, vmem_limit_bytes=vmem)


def _sigmoid(x):
    return 1.0 / (1.0 + jnp.exp(-x))


def _dsilu(x, s):
    return s * (1.0 + x * (1.0 - s))


def _lane_sum(x, ones_bf16):
    return jnp.dot(x.astype(BF16), ones_bf16, preferred_element_type=F32)


def _mesh_pos():
    return lax.axis_index("x"), lax.axis_index("y"), lax.axis_index("c")


def _allgather_vmem(x_blk, name):
    m_per, n = x_blk.shape

    def body(x_ref, out_ref, send_sems, recv_sems, local_sem):
        x, y, c = _mesh_pos()
        me, sibling = (x, y, c), (x, y, 1 - c)
        chips = [(1 - x, y), (x, 1 - y), (1 - x, 1 - y)]

        def rows(px, py, pc):
            return out_ref.at[pl.ds((4 * px + 2 * py + pc) * m_per, m_per), :]

        def copy(k, block, to, src=None):
            return pltpu.make_async_remote_copy(
                src_ref=rows(*block) if src is None else src, dst_ref=rows(*block),
                send_sem=send_sems.at[k], recv_sem=recv_sems.at[k], device_id=to, device_id_type=MESH)

        mine = pltpu.make_async_copy(x_ref, rows(*me), local_sem)
        mine.start()
        first = [copy(0, me, sibling, src=x_ref)]
        first += [copy(1 + j, me, (*chip, c), src=x_ref) for j, chip in enumerate(chips)]
        for cp in first:
            cp.start()
        passed = [copy(4 + j, (*chip, c), sibling) for j, chip in enumerate(chips)]
        for j, chip in enumerate(chips):
            copy(1 + j, (*chip, c), me).wait_recv()
            passed[j].start()
        copy(0, sibling, me).wait_recv()
        for j, chip in enumerate(chips):
            copy(4 + j, (*chip, 1 - c), me).wait_recv()
        for cp in first + passed:
            cp.wait_send()
        mine.wait()

    return pl.pallas_call(
        body, name=name,
        out_shape=jax.ShapeDtypeStruct((N_DEV * m_per, n), x_blk.dtype),
        in_specs=[pl.BlockSpec(memory_space=pltpu.VMEM)],
        out_specs=pl.BlockSpec(memory_space=pltpu.VMEM),
        scratch_shapes=[pltpu.SemaphoreType.DMA((7,)), pltpu.SemaphoreType.DMA((7,)), pltpu.SemaphoreType.DMA],
    )(x_blk)


def _flip(v, bit):
    return v + bit - 2 * v * bit


def _relay_chips(x, y, c):
    return (_flip(x, 1 - c), _flip(y, c)), (_flip(x, c), _flip(y, 1 - c))


UP_HEAD_ROWS = 768
GATHER_PARTS = 4


def _allgather_weights(blocks):
    n_arr = len(blocks)
    parts = GATHER_PARTS

    def body(*refs):
        ins, outs = refs[:n_arr], refs[n_arr:2 * n_arr]
        send_sems, recv_sems, local_sems = refs[2 * n_arr:]
        x, y, c = _mesh_pos()
        me, sibling = (x, y, c), (x, y, 1 - c)
        near = [(1 - x, y), (x, 1 - y)]
        chips = near + [(1 - x, 1 - y)]
        relay_from, relay_to = _relay_chips(x, y, c)

        def rows(a, p):
            hr = ins[a].shape[0] // parts
            return pl.ds(p * hr, hr)

        def slot(a, pos, p):
            return outs[a].at[4 * pos[0] + 2 * pos[1] + pos[2], rows(a, p)]

        def copy(a, k, p, src, lands, to):
            return pltpu.make_async_remote_copy(
                src_ref=src, dst_ref=slot(a, lands, p), send_sem=send_sems.at[a, k, p], recv_sem=recv_sems.at[a, k, p],
                device_id=to, device_id_type=MESH)

        sent = []
        local = [pltpu.make_async_copy(ins[a], outs[a].at[4 * x + 2 * y + c], local_sems.at[a]) for a in range(n_arr)]
        for cp in local:
            cp.start()
        for p in range(parts):
            for a in range(n_arr):
                own = ins[a].at[rows(a, p)]
                sent.append(copy(a, 0, p, own, me, sibling))
                sent += [copy(a, 1 + j, p, own, me, (*chip, c)) for j, chip in enumerate(near)]
        for cp in sent:
            cp.start()

        def start(cp):
            cp.start()
            sent.append(cp)

        for p in range(parts):
            for a in range(n_arr):
                for j, chip in enumerate(near):
                    copy(a, 1 + j, p, ins[a].at[rows(a, p)], (*chip, c), me).wait_recv()
                    start(copy(a, 4 + j, p, slot(a, (*chip, c), p), (*chip, c), sibling))
                start(copy(a, 3, p, slot(a, (*relay_from, c), p), (*relay_from, c), (*relay_to, c)))
        for p in range(parts):
            for a in range(n_arr):
                copy(a, 3, p, ins[a].at[rows(a, p)], (*chips[2], c), me).wait_recv()
                start(copy(a, 6, p, slot(a, (*chips[2], c), p), (*chips[2], c), sibling))
        for p in range(parts):
            for a in range(n_arr):
                copy(a, 0, p, ins[a].at[rows(a, p)], sibling, me).wait_recv()
                for j, chip in enumerate(chips):
                    copy(a, 4 + j, p, ins[a].at[rows(a, p)], (*chip, 1 - c), me).wait_recv()
        for cp in sent:
            cp.wait_send()
        for cp in local:
            cp.wait()

    return pl.pallas_call(
        body, name="allgather_weights",
        out_shape=[jax.ShapeDtypeStruct((N_DEV,) + b.shape, b.dtype) for b in blocks],
        in_specs=[pl.BlockSpec(memory_space=pltpu.HBM)] * n_arr, out_specs=[pl.BlockSpec(memory_space=pltpu.HBM)] * n_arr,
        scratch_shapes=[pltpu.SemaphoreType.DMA((n_arr, 7, parts)), pltpu.SemaphoreType.DMA((n_arr, 7, parts)),
                        pltpu.SemaphoreType.DMA((n_arr,))],
    )(*blocks)


HBM_SPEC = pl.BlockSpec(memory_space=pltpu.HBM)
SEM_SPEC = pl.BlockSpec(memory_space=pltpu.SEMAPHORE)
SPLIT_COPY_EFFECT = pltpu.SideEffectType.DATAFLOW_SIDE_EFFECTING


def _near_peers(x, y, c):
    return [(x, y, 1 - c), (1 - x, y, c), (x, 1 - y, c)]


def _gather_near_start(block):
    def body(v_ref, land_ref, send_sems, recv_sems, v_thru, land_thru, token):
        x, y, c = _mesh_pos()
        for k, peer in enumerate(_near_peers(x, y, c)):
            pltpu.make_async_remote_copy(src_ref=v_ref, dst_ref=land_ref.at[4 * x + 2 * y + c], send_sem=send_sems.at[k],
                                         recv_sem=recv_sems.at[k], device_id=peer, device_id_type=MESH).start()
        token[...] = jnp.zeros_like(token)

    land = lax.empty((N_DEV,) + block.shape, block.dtype)
    return pl.pallas_call(
        body, name="allgather_w_in_start",
        out_shape=(pltpu.SemaphoreType.DMA((3,)), pltpu.SemaphoreType.DMA((3,)), pltpu.HBM(block.shape, block.dtype),
                   pltpu.HBM(land.shape, land.dtype), jax.ShapeDtypeStruct((8, 128), F32)),
        in_specs=(HBM_SPEC, HBM_SPEC),
        out_specs=(SEM_SPEC, SEM_SPEC, HBM_SPEC, HBM_SPEC, pl.BlockSpec(memory_space=pltpu.VMEM)),
        input_output_aliases={0: 2, 1: 3}, compiler_params=pltpu.CompilerParams(has_side_effects=SPLIT_COPY_EFFECT),
    )(pltpu.with_memory_space_constraint(block, pltpu.HBM), pltpu.with_memory_space_constraint(land, pltpu.HBM))


def _gather_near_wait(send_sems, recv_sems, block, land, after):
    def body(v_ref, land_ref, send_sems, recv_sems, after_ref, v_out, land_out):
        x, y, c = _mesh_pos()
        for k, (px, py, pc) in enumerate(_near_peers(x, y, c)):
            cp = pltpu.make_async_remote_copy(src_ref=v_ref, dst_ref=land_ref.at[4 * px + 2 * py + pc],
                                              send_sem=send_sems.at[k], recv_sem=recv_sems.at[k],
                                              device_id=(px, py, pc), device_id_type=MESH)
            cp.wait_send()
            cp.wait_recv()

    return pl.pallas_call(
        body, name="allgather_w_in_wait",
        out_shape=(pltpu.HBM(block.shape, block.dtype), pltpu.HBM(land.shape, land.dtype)),
        in_specs=(HBM_SPEC, HBM_SPEC, SEM_SPEC, SEM_SPEC, pl.BlockSpec(memory_space=pl.ANY)),
        out_specs=(HBM_SPEC, HBM_SPEC), input_output_aliases={0: 0, 1: 1},
        compiler_params=pltpu.CompilerParams(has_side_effects=SPLIT_COPY_EFFECT),
    )(block, land, send_sems, recv_sems, after)


def _gather_finish(block, land):
    def body(v_ref, land_ref, out_ref, send_sems, recv_sems, local_sem):
        x, y, c = _mesh_pos()
        sibling = (x, y, 1 - c)
        chips = [(1 - x, y), (x, 1 - y), (1 - x, 1 - y)]
        relay_from, relay_to = _relay_chips(x, y, c)

        def slot(ref, chip, pc):
            return ref.at[4 * chip[0] + 2 * chip[1] + pc]

        def copy(k, src_chip, lands, lands_c, to):
            return pltpu.make_async_remote_copy(
                src_ref=slot(land_ref, src_chip, c), dst_ref=slot(out_ref, lands, lands_c), send_sem=send_sems.at[k],
                recv_sem=recv_sems.at[k], device_id=to, device_id_type=MESH)

        local = pltpu.make_async_copy(v_ref, out_ref.at[4 * x + 2 * y + c], local_sem)
        local.start()
        sent = [copy(0, relay_from, relay_from, c, (*relay_to, c))]
        sent += [copy(1 + j, chips[j], chips[j], c, sibling) for j in range(2)]
        for cp in sent:
            cp.start()
        copy(0, relay_from, chips[2], c, (*relay_to, c)).wait_recv()
        last = copy(3, chips[2], chips[2], c, sibling)
        last.start()
        for j in range(3):
            copy(1 + j, chips[j], chips[j], 1 - c, sibling).wait_recv()
        for cp in sent + [last]:
            cp.wait_send()
        local.wait()

    return pl.pallas_call(
        body, name="allgather_w_in_finish", out_shape=jax.ShapeDtypeStruct(land.shape, land.dtype),
        in_specs=[HBM_SPEC, HBM_SPEC], out_specs=HBM_SPEC, input_output_aliases={1: 0},
        scratch_shapes=[pltpu.SemaphoreType.DMA((4,)), pltpu.SemaphoreType.DMA((4,)), pltpu.SemaphoreType.DMA],
    )(block, land)


class _FusedCopies:
    def __init__(self, kind, arrays, peers=(0, 1, 2, 3), rows=None, relay_rows=None):
        self.kind = kind
        self.peers = peers
        self.rows = rows
        self.relay_rows = relay_rows
        n = len(arrays) // 2 if kind == "gather_more" else len(arrays)
        self.n = n
        self.n_in = len(arrays)
        self.aliases = {}
        if kind == "gather":
            self.out_shape = [jax.ShapeDtypeStruct((N_DEV,) + a.shape, a.dtype) for a in arrays]
            self.scratch_shapes = [pltpu.SemaphoreType.DMA((n, 4, GATHER_PARTS)),
                                   pltpu.SemaphoreType.DMA((n, 4, GATHER_PARTS)), pltpu.SemaphoreType.DMA((n,))]
        elif kind == "gather_more":
            self.out_shape = [jax.ShapeDtypeStruct(a.shape, a.dtype) for a in arrays[n:]]
            self.scratch_shapes = [pltpu.SemaphoreType.DMA((n, 5, GATHER_PARTS)),
                                   pltpu.SemaphoreType.DMA((n, 5, GATHER_PARTS)), pltpu.SemaphoreType.DMA((n,))]
            self.aliases = {n + a: a for a in range(n)}
        elif kind == "relay":
            self.out_shape = [jax.ShapeDtypeStruct(a.shape, a.dtype) for a in arrays]
            self.scratch_shapes = [pltpu.SemaphoreType.DMA((n,)), pltpu.SemaphoreType.DMA((n,))]
            self.aliases = {a: a for a in range(n)}
        elif kind == "forward":
            self.out_shape = [jax.ShapeDtypeStruct(a.shape, a.dtype) for a in arrays]
            self.scratch_shapes = [pltpu.SemaphoreType.DMA((n, 3)), pltpu.SemaphoreType.DMA((n, 3))]
            self.aliases = {a: a for a in range(n)}
        elif kind == "sibling":
            self.out_shape = [jax.ShapeDtypeStruct((4,) + a.shape[1:], a.dtype) for a in arrays]
            self.scratch_shapes = [pltpu.SemaphoreType.DMA((n, 4)), pltpu.SemaphoreType.DMA((n, 4))]
        else:
            self.out_shape = [jax.ShapeDtypeStruct((3,) + a.shape[1:], a.dtype) for a in arrays]
            self.scratch_shapes = [pltpu.SemaphoreType.DMA((n, 3)), pltpu.SemaphoreType.DMA((n, 3))]
        self.in_specs = [HBM_SPEC] * self.n_in
        self.out_specs = [HBM_SPEC] * n
        self.n_scratch = len(self.scratch_shapes)

    def copies(self, ins, outs, sems):
        x, y, c = _mesh_pos()
        chips = [(1 - x, y), (x, 1 - y), (1 - x, 1 - y)]
        sibling = (x, y, 1 - c)
        starts, waits = [], []
        relay_from, relay_to = _relay_chips(x, y, c)

        def relayed(a, buf, lands, send_sem, recv_sem, rows):
            first, count = rows or (0, buf.shape[1])
            span = pl.ds(first, count)
            return pltpu.make_async_remote_copy(
                src_ref=buf.at[4 * relay_from[0] + 2 * relay_from[1] + c, span],
                dst_ref=outs[a].at[4 * lands[0] + 2 * lands[1] + c, span], send_sem=send_sem, recv_sem=recv_sem,
                device_id=(*relay_to, c), device_id_type=MESH)

        if self.kind in ("gather", "gather_more"):
            send_sems, recv_sems, local_sems = sems
            me = (x, y, c)
            peers = [sibling] + [(px, py, c) for px, py in chips]

            def slot(a, pos):
                return outs[a].at[4 * pos[0] + 2 * pos[1] + pos[2]]

            def span(a, p=None):
                first, count = self.rows or (0, ins[a].shape[0])
                if p is None:
                    return pl.ds(first, count)
                return pl.ds(first + p * (count // GATHER_PARTS), count // GATHER_PARTS)

            def remote(a, k, p, lands_from):
                return pltpu.make_async_remote_copy(
                    src_ref=ins[a].at[span(a, p)], dst_ref=slot(a, lands_from).at[span(a, p)],
                    send_sem=send_sems.at[a, k, p], recv_sem=recv_sems.at[a, k, p], device_id=peers[k],
                    device_id_type=MESH)

            for a in range(self.n):
                local = pltpu.make_async_copy(ins[a].at[span(a)], slot(a, me).at[span(a)], local_sems.at[a])
                starts.append(local)
                waits.append(local)
            for p in range(GATHER_PARTS):
                for a in range(self.n):
                    for k in self.peers:
                        starts.append(remote(a, k, p, me))
                        waits.append(remote(a, k, p, peers[k]))
            if self.kind == "gather_more" and self.relay_rows is not None:
                for a in range(self.n):
                    buf = ins[self.n + a]
                    starts.append(relayed(a, buf, relay_from, send_sems.at[a, 4, 0], recv_sems.at[a, 4, 0],
                                          self.relay_rows))
                    waits.append(relayed(a, buf, chips[2], send_sems.at[a, 4, 0], recv_sems.at[a, 4, 0],
                                         self.relay_rows))
        elif self.kind == "relay":
            send_sems, recv_sems = sems
            for a in range(self.n):
                starts.append(relayed(a, ins[a], relay_from, send_sems.at[a], recv_sems.at[a], self.rows))
                waits.append(relayed(a, ins[a], chips[2], send_sems.at[a], recv_sems.at[a], self.rows))
        elif self.kind == "forward":
            send_sems, recv_sems = sems

            def passed_on(a, j, pc_src, pc_dst):
                px, py = chips[j]
                return pltpu.make_async_remote_copy(
                    src_ref=ins[a].at[4 * px + 2 * py + pc_src], dst_ref=outs[a].at[4 * px + 2 * py + pc_dst],
                    send_sem=send_sems.at[a, j], recv_sem=recv_sems.at[a, j], device_id=sibling, device_id_type=MESH)

            for a in range(self.n):
                for j in range(3):
                    starts.append(passed_on(a, j, c, c))
                    waits.append(passed_on(a, j, c, 1 - c))
        elif self.kind == "sibling":
            send_sems, recv_sems = sems
            for a in range(self.n):
                for q in range(4):
                    cp = pltpu.make_async_remote_copy(
                        src_ref=ins[a].at[2 * q + 1 - c], dst_ref=outs[a].at[q], send_sem=send_sems.at[a, q],
                        recv_sem=recv_sems.at[a, q], device_id=sibling, device_id_type=MESH)
                    starts.append(cp)
                    waits.append(cp)
        else:
            send_sems, recv_sems = sems
            for a in range(self.n):
                for j, (px, py) in enumerate(chips):
                    cp = pltpu.make_async_remote_copy(
                        src_ref=ins[a].at[2 * px + py], dst_ref=outs[a].at[j], send_sem=send_sems.at[a, j],
                        recv_sem=recv_sems.at[a, j], device_id=(px, py, c), device_id_type=MESH)
                    starts.append(cp)
                    waits.append(cp)
        return starts, waits


def _fused_groups(fused):
    if fused is None:
        return []
    return list(fused) if isinstance(fused, (list, tuple)) else [fused]


def _host_body(body, n_in, n_out, fused, first_last):
    groups = _fused_groups(fused)
    if not groups:
        return body
    n_fin, n_fout = sum(g.n_in for g in groups), sum(g.n for g in groups)
    n_fsem = sum(g.n_scratch for g in groups)

    def wrapped(*refs):
        core_in, f_in = refs[:n_in], refs[n_in:n_in + n_fin]
        core_out = refs[n_in + n_fin:n_in + n_fin + n_out]
        f_out = refs[n_in + n_fin + n_out:n_in + n_fin + n_out + n_fout]
        rest = refs[n_in + n_fin + n_out + n_fout:]
        core_scratch, f_sems = rest[:len(rest) - n_fsem], rest[len(rest) - n_fsem:]
        starts, waits = [], []
        for g in groups:
            s, w = g.copies(f_in[:g.n_in], f_out[:g.n], f_sems[:g.n_scratch])
            f_in, f_out, f_sems = f_in[g.n_in:], f_out[g.n:], f_sems[g.n_scratch:]
            starts += s
            waits += w
        first, last = first_last()

        @pl.when(first)
        def _():
            for cp in starts:
                cp.start()

        body(*core_in, *core_out, *core_scratch)

        @pl.when(last)
        def _():
            for cp in waits:
                cp.wait()

    return wrapped


def _host_call(body, n_in, n_out, fused, first_last, *, name, grid, in_specs, out_specs, out_shape, scratch_shapes,
               sem, operands):
    aliases = {}
    in_specs, out_specs, out_shape, scratch_shapes = list(in_specs), list(out_specs), list(out_shape), list(scratch_shapes)
    fin, fout = n_in, n_out
    for g in _fused_groups(fused):
        aliases.update({fin + fi: fout + fo for fi, fo in g.aliases.items()})
        fin, fout = fin + g.n_in, fout + g.n
        in_specs += g.in_specs
        out_specs += g.out_specs
        out_shape += g.out_shape
        scratch_shapes += g.scratch_shapes
        sem = tuple("arbitrary" for _ in sem)
    res = pl.pallas_call(_host_body(body, n_in, n_out, fused, first_last), name=name, grid=grid, in_specs=in_specs,
                         out_specs=out_specs, out_shape=out_shape, scratch_shapes=scratch_shapes,
                         input_output_aliases=aliases, compiler_params=_params(sem))(*operands)
    return list(res[:n_out]), list(res[n_out:])


def _forward_to_sibling(name, gathered):
    n_arr = len(gathered)

    def body(*refs):
        ins, outs = refs[:n_arr], refs[n_arr:2 * n_arr]
        send_sems, recv_sems = refs[2 * n_arr:]
        x, y, c = _mesh_pos()
        chips = [(1 - x, y), (x, 1 - y), (1 - x, 1 - y)]

        def copy(a, j, pc):
            px, py = chips[j]
            s = 4 * px + 2 * py + pc
            return pltpu.make_async_remote_copy(
                src_ref=ins[a].at[s], dst_ref=outs[a].at[s], send_sem=send_sems.at[a, j], recv_sem=recv_sems.at[a, j],
                device_id=(x, y, 1 - c), device_id_type=MESH)

        for a in range(n_arr):
            for j in range(3):
                copy(a, j, c).start()
        for a in range(n_arr):
            for j in range(3):
                copy(a, j, 1 - c).wait_recv()
                copy(a, j, c).wait_send()

    return pl.pallas_call(
        body, name=name,
        out_shape=[jax.ShapeDtypeStruct(g.shape, g.dtype) for g in gathered],
        in_specs=[HBM_SPEC] * n_arr, out_specs=[HBM_SPEC] * n_arr,
        input_output_aliases={a: a for a in range(n_arr)},
        scratch_shapes=[pltpu.SemaphoreType.DMA((n_arr, 3)), pltpu.SemaphoreType.DMA((n_arr, 3))],
    )(*gathered)


def _exchange_sibling(name, partials):
    n_arr = len(partials)

    def body(*refs):
        ins, outs = refs[:n_arr], refs[n_arr:2 * n_arr]
        send_sems, recv_sems = refs[2 * n_arr:]
        x, y, c = _mesh_pos()
        copies = [pltpu.make_async_remote_copy(
            src_ref=ins[a].at[2 * q + 1 - c], dst_ref=outs[a].at[q], send_sem=send_sems.at[a, q],
            recv_sem=recv_sems.at[a, q], device_id=(x, y, 1 - c), device_id_type=MESH)
            for a in range(n_arr) for q in range(4)]
        for cp in copies:
            cp.start()
        for cp in copies:
            cp.wait_recv()
        for cp in copies:
            cp.wait_send()

    return pl.pallas_call(
        body, name=name,
        out_shape=[jax.ShapeDtypeStruct((4,) + p.shape[1:], p.dtype) for p in partials],
        in_specs=[HBM_SPEC] * n_arr, out_specs=[HBM_SPEC] * n_arr,
        scratch_shapes=[pltpu.SemaphoreType.DMA((n_arr, 4)), pltpu.SemaphoreType.DMA((n_arr, 4))],
    )(*partials)


def _matmul(name, a, b, dims, grid, a_spec, b_spec, o_spec, out_shape, acc_axis=None, fused=None, fused_arrays=()):
    def body(a_ref, b_ref, o_ref):
        r = lax.dot_general(a_ref[...], b_ref[...], dims, preferred_element_type=F32)
        if acc_axis is None:
            o_ref[...] = r.astype(o_ref.dtype)
        else:
            k = pl.program_id(acc_axis)

            @pl.when(k == 0)
            def _():
                o_ref[...] = r

            @pl.when(k > 0)
            def _():
                o_ref[...] += r

    sem = tuple("arbitrary" if i == acc_axis else "parallel" for i in range(len(grid)))
    if fused is None:
        return pl.pallas_call(body, name=name, grid=grid, in_specs=[a_spec, b_spec], out_specs=o_spec,
                              out_shape=out_shape, compiler_params=_params(sem))(a, b)

    def first_last():
        first = last = None
        for ax, n in enumerate(grid):
            f, l = pl.program_id(ax) == 0, pl.program_id(ax) == n - 1
            first, last = (f, l) if first is None else (first & f, last & l)
        return first, last

    (out,), extra = _host_call(body, 2, 1, fused, first_last, name=name, grid=grid, in_specs=[a_spec, b_spec],
                               out_specs=[o_spec], out_shape=[out_shape], scratch_shapes=[], sem=sem,
                               operands=[a, b] + list(fused_arrays))
    return out, extra


def _mm_blocked_rhs(name, a, w_g, tm=512, fused=None, fused_arrays=()):
    m, k = a.shape
    nb = w_g.shape[2]
    return _matmul(name, a, w_g, NN, (N_DEV, m // tm),
                   pl.BlockSpec((tm, k), lambda j, i: (i, 0)),
                   pl.BlockSpec((None, k, nb), lambda j, i: (j, 0, 0)),
                   pl.BlockSpec((tm, nb), lambda j, i: (i, j)),
                   jax.ShapeDtypeStruct((m, N_DEV * nb), F32), fused=fused, fused_arrays=fused_arrays)


def _mm_blocked_rhs_t(name, a, w_g, tm=512, fused=None, fused_arrays=()):
    m = a.shape[0]
    n, nb = w_g.shape[1], w_g.shape[2]
    return _matmul(name, a, w_g, NT, (m // tm, N_DEV),
                   pl.BlockSpec((tm, nb), lambda i, j: (i, j)),
                   pl.BlockSpec((None, n, nb), lambda i, j: (j, 0, 0)),
                   pl.BlockSpec((tm, n), lambda i, j: (i, 0)),
                   jax.ShapeDtypeStruct((m, n), F32), acc_axis=1, fused=fused, fused_arrays=fused_arrays)


def _mm_wgrad_blocked(name, act, dcols, tk=512, fused=None, fused_arrays=()):
    t, k = act.shape
    nb = dcols.shape[1] // N_DEV
    return _matmul(name, act, dcols, TN, (N_DEV, k // tk),
                   pl.BlockSpec((t, tk), lambda j, i: (0, i)),
                   pl.BlockSpec((t, nb), lambda j, i: (0, j)),
                   pl.BlockSpec((None, tk, nb), lambda j, i: (j, i, 0)),
                   jax.ShapeDtypeStruct((N_DEV, k, nb), BF16), fused=fused, fused_arrays=fused_arrays)


def _halves_specs(block, index):
    half = N_DEV // 2
    return (pl.BlockSpec(block, lambda i, j: index(i, jnp.minimum(j, half - 1))),
            pl.BlockSpec(block, lambda i, j: index(i, jnp.maximum(j - half, 0))))


def _mm_halves_rhs_t(name, a_lo, a_hi, w_g, tm=512):
    m = a_lo.shape[0]
    n, nb = w_g.shape[1], w_g.shape[2]

    def body(lo_ref, hi_ref, b_ref, o_ref):
        j = pl.program_id(1)

        def accumulate(a_ref):
            r = lax.dot_general(a_ref[...], b_ref[...], NT, preferred_element_type=F32)

            @pl.when(j == 0)
            def _():
                o_ref[...] = r

            @pl.when(j > 0)
            def _():
                o_ref[...] += r

        pl.when(j < N_DEV // 2)(lambda: accumulate(lo_ref))
        pl.when(j >= N_DEV // 2)(lambda: accumulate(hi_ref))

    lo_spec, hi_spec = _halves_specs((tm, nb), lambda i, j: (i, j))
    return pl.pallas_call(
        body, name=name, grid=(m // tm, N_DEV),
        in_specs=[lo_spec, hi_spec, pl.BlockSpec((None, n, nb), lambda i, j: (j, 0, 0))],
        out_specs=pl.BlockSpec((tm, n), lambda i, j: (i, 0)), out_shape=jax.ShapeDtypeStruct((m, n), F32),
        compiler_params=_params(("parallel", "arbitrary")))(a_lo, a_hi, w_g)


def _mm_halves_wgrad(name, act, d_lo, d_hi, tk=512):
    t, k = act.shape
    nb = d_lo.shape[1] // (N_DEV // 2)

    def body(a_ref, lo_ref, hi_ref, o_ref):
        j = pl.program_id(0)

        def product(d_ref):
            o_ref[...] = lax.dot_general(a_ref[...], d_ref[...], TN, preferred_element_type=F32).astype(o_ref.dtype)

        pl.when(j < N_DEV // 2)(lambda: product(lo_ref))
        pl.when(j >= N_DEV // 2)(lambda: product(hi_ref))

    half = N_DEV // 2
    return pl.pallas_call(
        body, name=name, grid=(N_DEV, k // tk),
        in_specs=[pl.BlockSpec((t, tk), lambda j, i: (0, i)),
                  pl.BlockSpec((t, nb), lambda j, i: (0, jnp.minimum(j, half - 1))),
                  pl.BlockSpec((t, nb), lambda j, i: (0, jnp.maximum(j - half, 0)))],
        out_specs=pl.BlockSpec((None, tk, nb), lambda j, i: (j, i, 0)),
        out_shape=jax.ShapeDtypeStruct((N_DEV, k, nb), BF16),
        compiler_params=_params(("parallel", "parallel")))(act, d_lo, d_hi)


def _mm_plain(name, a, b, dims, tm, tn, out_dtype, fused=None, fused_arrays=()):
    if dims == NN:
        (m, k), n = a.shape, b.shape[1]
        a_spec = pl.BlockSpec((tm, k), lambda i, j: (i, 0))
        b_spec = pl.BlockSpec((k, tn), lambda i, j: (0, j))
    elif dims == NT:
        (m, k), n = a.shape, b.shape[0]
        a_spec = pl.BlockSpec((tm, k), lambda i, j: (i, 0))
        b_spec = pl.BlockSpec((tn, k), lambda i, j: (j, 0))
    else:
        (k, m), n = a.shape, b.shape[1]
        a_spec = pl.BlockSpec((k, tm), lambda i, j: (0, i))
        b_spec = pl.BlockSpec((k, tn), lambda i, j: (0, j))
    return _matmul(name, a, b, dims, (m // tm, n // tn), a_spec, b_spec,
                   pl.BlockSpec((tm, tn), lambda i, j: (i, j)), jax.ShapeDtypeStruct((m, n), out_dtype),
                   fused=fused, fused_arrays=fused_arrays)


def _ada_fwd(c_all, w_ada_blk, b_blk):
    def body(c_ref, w_ref, b_ref, o_ref):
        cv = c_ref[...]
        o_ref[...] = jnp.dot(cv * _sigmoid(cv), w_ref[...], preferred_element_type=F32) + b_ref[...]

    tn = 512
    return pl.pallas_call(
        body, name="ada_fwd", grid=(ADA_BLK // tn,),
        in_specs=[pl.BlockSpec((N_DEV, D_MODEL), lambda j: (0, 0)),
                  pl.BlockSpec((D_MODEL, tn), lambda j: (0, j)),
                  pl.BlockSpec((1, tn), lambda j: (0, j))],
        out_specs=pl.BlockSpec((N_DEV, tn), lambda j: (0, j)),
        out_shape=jax.ShapeDtypeStruct((N_DEV, ADA_BLK), F32),
        compiler_params=_params(("parallel",)))(c_all, w_ada_blk, b_blk)


def _ada_wgrad(c_all, gmod_cols):
    def body(c_ref, g_ref, o_ref):
        cv = c_ref[...]
        o_ref[...] = lax.dot_general(cv * _sigmoid(cv), g_ref[...], TN, preferred_element_type=F32)

    tk = 512
    return pl.pallas_call(
        body, name="ada_wgrad", grid=(D_MODEL // tk,),
        in_specs=[pl.BlockSpec((N_DEV, tk), lambda i: (0, i)),
                  pl.BlockSpec((N_DEV, ADA_BLK), lambda i: (0, 0))],
        out_specs=pl.BlockSpec((tk, ADA_BLK), lambda i: (i, 0)),
        out_shape=jax.ShapeDtypeStruct((D_MODEL, ADA_BLK), F32),
        compiler_params=_params(("parallel",)))(c_all, gmod_cols)


def _row_spec(cols=D_MODEL):
    return pl.BlockSpec((ROW_TILE, cols), lambda i: (i, 0))


def _vec_spec(cols=D_MODEL):
    return pl.BlockSpec((1, cols), lambda i: (0, 0))


def _norm_fwd(name, x, w, scale, shift, resid=None, gate=None):
    has_res = resid is not None

    def body(*refs):
        if has_res:
            x_ref, r_ref, g_ref, w_ref, sc_ref, sh_ref, xr_ref, h_ref, rs_ref = refs
            xr = x_ref[...] + g_ref[...] * r_ref[...]
            xr_ref[...] = xr
        else:
            x_ref, w_ref, sc_ref, sh_ref, h_ref, rs_ref = refs
            xr = x_ref[...]
        rs = lax.rsqrt(jnp.mean(xr * xr, axis=-1, keepdims=True) + EPS)
        h = (xr * rs) * w_ref[...] * (1.0 + sc_ref[...]) + sh_ref[...]
        h_ref[...] = h.astype(BF16)
        rs_ref[...] = rs

    s = x.shape[0]
    ins = [x] + ([resid, gate] if has_res else []) + [w, scale, shift]
    in_specs = [_row_spec()] + ([_row_spec(), _vec_spec()] if has_res else []) + [_vec_spec()] * 3
    outs = ([jax.ShapeDtypeStruct((s, D_MODEL), F32)] if has_res else []) + [
        jax.ShapeDtypeStruct((s, D_MODEL), BF16), jax.ShapeDtypeStruct((s, 1), F32)]
    out_specs = ([_row_spec()] if has_res else []) + [_row_spec(), pl.BlockSpec((ROW_TILE, 1), lambda i: (i, 0))]
    return pl.pallas_call(body, name=name, grid=(s // ROW_TILE,), in_specs=in_specs, out_specs=out_specs,
                          out_shape=outs, compiler_params=_params(("parallel",)))(*ins)


def _norm_bwd(name, dh, x, rstd, w, scale, dres, mix=None, gate=None, fused=None, fused_arrays=()):
    has_mix = mix is not None

    def body(*refs):
        if has_mix:
            (dh_ref, x_ref, rs_ref, w_ref, sc_ref, dr_ref, mix_ref, g_ref,
             dx_ref, dmix_ref, dsh_ref, dsc_ref, dw_ref, dg_ref) = refs
        else:
            dh_ref, x_ref, rs_ref, w_ref, sc_ref, dr_ref, dx_ref, dsh_ref, dsc_ref, dw_ref = refs
        i = pl.program_id(0)
        dhv = dh_ref[...]
        rs = rs_ref[...]
        xn = x_ref[...] * rs
        wv = w_ref[...]
        one_sc = 1.0 + sc_ref[...]
        dxn = dhv * wv * one_sc
        dx = dr_ref[...] + rs * (dxn - xn * jnp.mean(dxn * xn, axis=-1, keepdims=True))
        dx_ref[...] = dx
        sums = [(dsh_ref, dhv), (dsc_ref, dhv * xn * wv), (dw_ref, dhv * one_sc * xn)]
        if has_mix:
            dmix_ref[...] = (dx * g_ref[...]).astype(BF16)
            sums.append((dg_ref, dx * mix_ref[...]))

        @pl.when(i == 0)
        def _():
            for ref, _v in sums:
                ref[...] = jnp.zeros_like(ref)

        for ref, v in sums:
            ref[...] += jnp.sum(v, axis=0, keepdims=True)

    s = x.shape[0]
    ins = [dh, x, rstd, w, scale, dres] + ([mix, gate] if has_mix else [])
    in_specs = ([_row_spec(), _row_spec(), pl.BlockSpec((ROW_TILE, 1), lambda i: (i, 0)), _vec_spec(), _vec_spec(),
                 _row_spec()] + ([_row_spec(), _vec_spec()] if has_mix else []))
    vec = jax.ShapeDtypeStruct((1, D_MODEL), F32)
    outs = ([jax.ShapeDtypeStruct((s, D_MODEL), F32)] + ([jax.ShapeDtypeStruct((s, D_MODEL), BF16)] if has_mix else [])
            + [vec] * (4 if has_mix else 3))
    out_specs = [_row_spec()] + ([_row_spec()] if has_mix else []) + [_vec_spec()] * (4 if has_mix else 3)

    def first_last():
        i = pl.program_id(0)
        return i == 0, i == s // ROW_TILE - 1

    res, extra = _host_call(body, len(ins), len(outs), fused, first_last, name=name, grid=(s // ROW_TILE,),
                            in_specs=in_specs, out_specs=out_specs, out_shape=outs, scratch_shapes=[],
                            sem=("arbitrary",), operands=ins + list(fused_arrays))
    return res if fused is None else (res, extra)


def _loss_head(x1, ffn, gate2, target):
    def body(x_ref, f_ref, g_ref, t_ref, loss_ref, dout_ref, dffn_ref, dg_ref):
        i = pl.program_id(0)
        fv = f_ref[...]
        gv = g_ref[...]
        err = x_ref[...] + gv * fv - t_ref[...]
        dout = err * (1.0 / D_MODEL)
        dout_ref[...] = dout
        dffn_ref[...] = (dout * gv).astype(BF16)

        @pl.when(i == 0)
        def _():
            loss_ref[...] = jnp.zeros_like(loss_ref)
            dg_ref[...] = jnp.zeros_like(dg_ref)

        row = jnp.sum(err * err, axis=-1, keepdims=True) * (1.0 / D_MODEL)
        loss_ref[...] += jnp.broadcast_to(0.5 * jnp.sum(row, axis=0, keepdims=True), (1, 128))
        dg_ref[...] += jnp.sum(dout * fv, axis=0, keepdims=True)

    s = x1.shape[0]
    return pl.pallas_call(
        body, name="loss_head", grid=(s // ROW_TILE,),
        in_specs=[_row_spec(), _row_spec(), _vec_spec(), _row_spec()],
        out_specs=[pl.BlockSpec((1, 128), lambda i: (0, 0)), _row_spec(), _row_spec(), _vec_spec()],
        out_shape=[jax.ShapeDtypeStruct((1, 128), F32), jax.ShapeDtypeStruct((s, D_MODEL), F32),
                   jax.ShapeDtypeStruct((s, D_MODEL), BF16), jax.ShapeDtypeStruct((1, D_MODEL), F32)],
        compiler_params=_params(("arbitrary",)))(x1, ffn, gate2, target)


CONV_TILE = 512
N_CONV_TILES = D_FF // CONV_TILE


def _shift_rows(a, k, row):
    n = a.shape[0]
    if k > 0:
        return jnp.where(row >= k, pltpu.roll(a, k, 0), 0.0)
    return jnp.where(row < n + k, pltpu.roll(a, n + k, 0), 0.0)


def _conv_gate_fwd(u, conv_w, conv_b, fused=None, fused_arrays=()):
    s = u.shape[0]

    def body(a_ref, g_ref, w_ref, b_ref, y_ref):
        a = a_ref[...]
        w = w_ref[...]
        row = lax.broadcasted_iota(jnp.int32, a.shape, 0)
        ac = b_ref[...] + _shift_rows(a, 2, row) * w[0:1] + _shift_rows(a, 1, row) * w[1:2] + a * w[2:3]
        y_ref[...] = (ac * _sigmoid(ac) * g_ref[...]).astype(BF16)

    def first_last():
        i = pl.program_id(0)
        return i == 0, i == N_CONV_TILES - 1

    col = lambda off: pl.BlockSpec((s, CONV_TILE), lambda i: (0, i + off))
    (y,), extra = _host_call(
        body, 4, 1, fused, first_last, name="conv_gate_fwd", grid=(N_CONV_TILES,),
        in_specs=[col(0), col(N_CONV_TILES), pl.BlockSpec((3, CONV_TILE), lambda i: (0, i)),
                  pl.BlockSpec((1, CONV_TILE), lambda i: (0, i))],
        out_specs=[col(0)], out_shape=[jax.ShapeDtypeStruct((s, D_FF), BF16)], scratch_shapes=[], sem=("parallel",),
        operands=[u, u, conv_w, conv_b] + list(fused_arrays))
    return y if fused is None else (y, extra)


def _conv_gate_bwd(u, dy, conv_w, conv_b):
    s = u.shape[0]

    def body(a_ref, g_ref, dy_ref, w_ref, b_ref, da_ref, dg_ref, gw_ref, gb_ref):
        a = a_ref[...]
        w = w_ref[...]
        row = lax.broadcasted_iota(jnp.int32, a.shape, 0)
        a1 = _shift_rows(a, 1, row)
        a2 = _shift_rows(a, 2, row)
        ac = b_ref[...] + a2 * w[0:1] + a1 * w[1:2] + a * w[2:3]
        sg = _sigmoid(ac)
        dyv = dy_ref[...].astype(F32)
        dg_ref[...] = (dyv * (ac * sg)).astype(BF16)
        dac = dyv * g_ref[...] * _dsilu(ac, sg)
        gb_ref[...] = jnp.sum(dac, axis=0, keepdims=True)
        gw_ref[0:1, :] = jnp.sum(dac * a2, axis=0, keepdims=True)
        gw_ref[1:2, :] = jnp.sum(dac * a1, axis=0, keepdims=True)
        gw_ref[2:3, :] = jnp.sum(dac * a, axis=0, keepdims=True)
        da = dac * w[2:3] + _shift_rows(dac, -1, row) * w[1:2] + _shift_rows(dac, -2, row) * w[0:1]
        da_ref[...] = da.astype(BF16)

    col = lambda off: pl.BlockSpec((s, CONV_TILE), lambda i: (0, i + off))
    return pl.pallas_call(
        body, name="conv_gate_bwd", grid=(N_CONV_TILES,),
        in_specs=[col(0), col(N_CONV_TILES), col(0), pl.BlockSpec((3, CONV_TILE), lambda i: (0, i)),
                  pl.BlockSpec((1, CONV_TILE), lambda i: (0, i))],
        out_specs=[col(0), col(0), pl.BlockSpec((3, CONV_TILE), lambda i: (0, i)),
                   pl.BlockSpec((1, CONV_TILE), lambda i: (0, i))],
        out_shape=[jax.ShapeDtypeStruct((s, D_FF), BF16), jax.ShapeDtypeStruct((s, D_FF), BF16),
                   jax.ShapeDtypeStruct((3, D_FF), F32), jax.ShapeDtypeStruct((1, D_FF), F32)],
        compiler_params=_params(("parallel",)))(u, u, dy, conv_w, conv_b)


HG_TILE = 256
CHUNK_UNROLL = 8


def _unrolled_loop(n, body, init):
    def group(i, carry):
        for u in range(CHUNK_UNROLL):
            carry = body(i * CHUNK_UNROLL + u, carry)
        return carry

    return lax.fori_loop(0, n // CHUNK_UNROLL, group, init)


def _head_col(off):
    return pl.BlockSpec((SEQ, HEAD_DIM), lambda h: (0, h + off))


def _hgrn_gates(hq, hf, lb, pos):
    q = hq * _sigmoid(hq)
    sig = _sigmoid(hf)
    f = lb + (1.0 - lb) * sig
    gl = jnp.log(f)
    for sh in (1, 2, 4, 8):
        gl = gl + jnp.where(pos >= sh, pltpu.roll(gl, sh, 0), 0.0)
    return q, sig, f, 1.0 - f, gl


def _lower_bound(lbl):
    return 1.0 / (1.0 + jnp.exp(lbl[1:2, :] - lbl[0:1, :]))


def _head_first_last():
    h = pl.program_id(0)
    return h == 0, h == HEADS - 1


CHUNKS_PER_TILE = HG_TILE // CHUNK


def _chunk_end(x, pos):
    y = jnp.where(pos == CHUNK - 1, x, 0.0)
    for sh in (1, 2, 4, 8):
        y = y + jnp.where(pos < CHUNK - sh, pltpu.roll(y, x.shape[0] - sh, 0), 0.0)
    return y


def _suffix_in_chunk(x, pos):
    for sh in (1, 2, 4, 8):
        x = x + jnp.where(pos < CHUNK - sh, pltpu.roll(x, x.shape[0] - sh, 0), 0.0)
    return x


def _prefix_in_chunk(x, pos):
    for sh in (1, 2, 4, 8):
        x = x + jnp.where(pos >= sh, pltpu.roll(x, sh, 0), 0.0)
    return x


def _pair_decays(f, pos):
    shifted = jnp.where(pos >= 1, f, 0.0)
    e = shifted
    yield 1, e
    for d in range(2, CHUNK):
        shifted = pltpu.roll(shifted, 1, 0)
        e = e * shifted
        yield d, e


def _chunk_rows(cc):
    return slice(cc * CHUNK, (cc + 1) * CHUNK)


def _outer_products(lhs_b, rhs_b, dst, i):
    for cc in range(CHUNKS_PER_TILE):
        dst[i * CHUNKS_PER_TILE + cc] = lax.dot_general(lhs_b[_chunk_rows(cc)], rhs_b[_chunk_rows(cc)], TN,
                                                        preferred_element_type=F32)


def _state_scan(n_chunks, gl_s, u_s, keep, reverse):
    def step(k, st):
        c = n_chunks - 1 - k if reverse else k
        keep[c] = st.astype(BF16)
        gl = gl_s[pl.ds(pl.multiple_of(c * CHUNK, CHUNK), CHUNK), :]
        return st * jnp.exp(gl[CHUNK - 1:CHUNK, :]) + u_s[c]

    _unrolled_loop(n_chunks, step, jnp.zeros((HEAD_DIM, HEAD_DIM), F32))


def _hgrn_fwd(proj, lb_logits, norm_w, fused=None, fused_arrays=()):
    n_tiles = SEQ // HG_TILE
    n_chunks = SEQ // CHUNK
    fused_arrays = list(fused_arrays)

    def body(hq_ref, hf_ref, hi_ref, hg_ref, lbl_ref, nw_ref, aout_ref, opre_ref, qt_s, gl_s, u_s, st_s):
        lb = _lower_bound(lbl_ref[...])
        ones = jnp.ones((HEAD_DIM, HEAD_DIM), BF16)
        pos = lax.broadcasted_iota(jnp.int32, (HG_TILE, HEAD_DIM), 0) % CHUNK

        def tile(i, carry):
            rows = pl.ds(pl.multiple_of(i * HG_TILE, HG_TILE), HG_TILE)
            v = hi_ref[rows, :]
            q, _sig, f, kk, gl = _hgrn_gates(hq_ref[rows, :], hf_ref[rows, :], lb, pos)
            o = _lane_sum(q * kk, ones) * v
            for d, e in _pair_decays(f, pos):
                o = o + _lane_sum(q * pltpu.roll(kk, d, 0) * e, ones) * pltpu.roll(v, d, 0)
            opre_ref[rows, :] = o
            qt_s[rows, :] = q * jnp.exp(gl)
            gl_s[rows, :] = gl
            kt = kk * jnp.exp(_chunk_end(gl, pos) - gl)
            _outer_products(v.astype(BF16), kt.astype(BF16), u_s, i)
            return carry

        lax.fori_loop(0, n_tiles, tile, 0)
        _state_scan(n_chunks, gl_s, u_s, st_s, reverse=False)

        def finish(i, carry):
            rows = pl.ds(pl.multiple_of(i * HG_TILE, HG_TILE), HG_TILE)
            qt_b = qt_s[rows, :].astype(BF16)
            past = [lax.dot_general(qt_b[_chunk_rows(cc)], st_s[i * CHUNKS_PER_TILE + cc], NT,
                                    preferred_element_type=F32) for cc in range(CHUNKS_PER_TILE)]
            o = opre_ref[rows, :] + jnp.concatenate(past, axis=0)
            opre_ref[rows, :] = o
            hg = hg_ref[rows, :]
            rs = lax.rsqrt(jnp.mean(o * o, axis=-1, keepdims=True) + EPS)
            aout_ref[rows, :] = ((o * rs) * nw_ref[...] * (hg * _sigmoid(hg))).astype(BF16)
            return carry

        lax.fori_loop(0, n_tiles, finish, 0)

    return _host_call(
        body, 6, 2, fused, _head_first_last, name="hgrn_fwd", grid=(HEADS,),
        in_specs=[_head_col(0), _head_col(HEADS), _head_col(2 * HEADS), _head_col(3 * HEADS),
                  pl.BlockSpec((2, HEAD_DIM), lambda h: (0, h)), pl.BlockSpec((1, HEAD_DIM), lambda h: (0, 0))],
        out_specs=[_head_col(0), _head_col(0)],
        out_shape=[jax.ShapeDtypeStruct((SEQ, HEADS * HEAD_DIM), BF16), jax.ShapeDtypeStruct((SEQ, HEADS * HEAD_DIM), F32)],
        scratch_shapes=[pltpu.VMEM((SEQ, HEAD_DIM), F32)] * 2 + [pltpu.VMEM((n_chunks, HEAD_DIM, HEAD_DIM), F32),
                                                                 pltpu.VMEM((n_chunks, HEAD_DIM, HEAD_DIM), BF16)],
        sem=("parallel",), operands=[proj, proj, proj, proj, lb_logits, norm_w] + fused_arrays)


def _hgrn_bwd(proj, lb_logits, norm_w, o_pre, d_aout, fused=None, fused_arrays=()):
    n_tiles = SEQ // HG_TILE
    n_chunks = SEQ // CHUNK

    def body(hq_ref, hf_ref, hi_ref, hg_ref, lbl_ref, nw_ref, opre_ref, da_ref,
             dhq_ref, dhf_ref, dhi_ref, dhg_ref, dlog_ref, gnw_ref,
             q_s, k_s, gl_s, do_s, dq_s, dk_s, dv_s, u_s, st_s, rt_s):
        h = pl.program_id(0)
        lb = _lower_bound(lbl_ref[...])
        nw = nw_ref[...]
        ones = jnp.ones((HEAD_DIM, HEAD_DIM), BF16)
        pos = lax.broadcasted_iota(jnp.int32, (HG_TILE, HEAD_DIM), 0) % CHUNK

        @pl.when(h == 0)
        def _():
            gnw_ref[...] = jnp.zeros_like(gnw_ref)

        def tile(i, carry):
            rows = pl.ds(pl.multiple_of(i * HG_TILE, HG_TILE), HG_TILE)
            v = hi_ref[rows, :]
            q, _sig, f, kk, gl = _hgrn_gates(hq_ref[rows, :], hf_ref[rows, :], lb, pos)
            o = opre_ref[rows, :]
            hg = hg_ref[rows, :]
            da = da_ref[rows, :]
            rs = lax.rsqrt(jnp.mean(o * o, axis=-1, keepdims=True) + EPS)
            oh = o * rs
            sg = _sigmoid(hg)
            dnorm = da * (hg * sg)
            dhg_ref[rows, :] = (da * (oh * nw) * _dsilu(hg, sg)).astype(BF16)
            gnw_ref[...] += jnp.sum(dnorm * oh, axis=0, keepdims=True)
            doh = dnorm * nw
            do = rs * (doh - oh * jnp.mean(doh * oh, axis=-1, keepdims=True))

            d_a = _lane_sum(do * v, ones)
            dq = d_a * kk
            dk = d_a * q
            dv = _lane_sum(q * kk, ones) * do
            for d, e in _pair_decays(f, pos):
                ks = pltpu.roll(kk, d, 0)
                a_d = _lane_sum(q * ks * e, ones)
                d_a = _lane_sum(do * pltpu.roll(v, d, 0), ones) * e
                dq = dq + d_a * ks
                dk = dk + pltpu.roll(d_a * q, HG_TILE - d, 0)
                dv = dv + pltpu.roll(a_d * do, HG_TILE - d, 0)
            q_s[rows, :] = q
            k_s[rows, :] = kk
            gl_s[rows, :] = gl
            do_s[rows, :] = do
            dq_s[rows, :] = dq
            dk_s[rows, :] = dk
            dv_s[rows, :] = dv
            kt = kk * jnp.exp(_chunk_end(gl, pos) - gl)
            _outer_products(v.astype(BF16), kt.astype(BF16), u_s, i)
            return carry

        lax.fori_loop(0, n_tiles, tile, 0)
        _state_scan(n_chunks, gl_s, u_s, st_s, reverse=False)

        def reverse_increments(i, carry):
            rows = pl.ds(pl.multiple_of(i * HG_TILE, HG_TILE), HG_TILE)
            qt = q_s[rows, :] * jnp.exp(gl_s[rows, :])
            _outer_products(do_s[rows, :].astype(BF16), qt.astype(BF16), u_s, i)
            return carry

        lax.fori_loop(0, n_tiles, reverse_increments, 0)
        _state_scan(n_chunks, gl_s, u_s, rt_s, reverse=True)

        def finish(i, dlb):
            rows = pl.ds(pl.multiple_of(i * HG_TILE, HG_TILE), HG_TILE)
            q = q_s[rows, :]
            kk = k_s[rows, :]
            gl = gl_s[rows, :]
            gll = _chunk_end(gl, pos)
            ekt = jnp.exp(gll - gl)
            do_b = do_s[rows, :].astype(BF16)
            v_b = hi_ref[rows, :].astype(BF16)
            kt_b = (kk * ekt).astype(BF16)
            dq_far, dk_far, dv_far, across = [], [], [], []
            for cc in range(CHUNKS_PER_TILE):
                st = st_s[i * CHUNKS_PER_TILE + cc]
                rt = rt_s[i * CHUNKS_PER_TILE + cc]
                sl = _chunk_rows(cc)
                dq_far.append(jnp.dot(do_b[sl], st, preferred_element_type=F32))
                dk_far.append(jnp.dot(v_b[sl], rt, preferred_element_type=F32))
                dv_far.append(lax.dot_general(kt_b[sl], rt, NT, preferred_element_type=F32))
                both = jnp.sum(st.astype(F32) * rt.astype(F32), axis=0, keepdims=True)
                across.append(jnp.broadcast_to(both, (CHUNK, HEAD_DIM)))
            dq = dq_s[rows, :] + jnp.concatenate(dq_far, axis=0) * jnp.exp(gl)
            dk_in = dk_s[rows, :]
            dk_out = jnp.concatenate(dk_far, axis=0) * ekt
            dk = dk_in + dk_out
            dv = dv_s[rows, :] + jnp.concatenate(dv_far, axis=0)
            pc = kk * dk_out
            dgl = (_suffix_in_chunk(q * dq - kk * dk_in, pos) + (_prefix_in_chunk(pc, pos) - pc)
                   + jnp.concatenate(across, axis=0) * jnp.exp(gll))
            hf = hf_ref[rows, :]
            sig = _sigmoid(hf)
            f = lb + (1.0 - lb) * sig
            df = dgl / f - dk
            dhf_ref[rows, :] = (df * (1.0 - lb) * sig * (1.0 - sig)).astype(BF16)
            hq = hq_ref[rows, :]
            dhq_ref[rows, :] = (dq * _dsilu(hq, _sigmoid(hq))).astype(BF16)
            dhi_ref[rows, :] = dv.astype(BF16)
            return dlb + jnp.sum(df * (1.0 - sig), axis=0, keepdims=True)

        dlb = lax.fori_loop(0, n_tiles, finish, jnp.zeros((1, HEAD_DIM), F32))
        dl0 = lb * (1.0 - lb) * dlb
        dlog_ref[0:1, :] = dl0
        dlog_ref[1:2, :] = -dl0

    wide = HEADS * HEAD_DIM
    return _host_call(
        body, 8, 6, fused, _head_first_last, name="hgrn_bwd", grid=(HEADS,),
        in_specs=[_head_col(0), _head_col(HEADS), _head_col(2 * HEADS), _head_col(3 * HEADS),
                  pl.BlockSpec((2, HEAD_DIM), lambda h: (0, h)), pl.BlockSpec((1, HEAD_DIM), lambda h: (0, 0)),
                  _head_col(0), _head_col(0)],
        out_specs=[_head_col(0)] * 4 + [pl.BlockSpec((2, HEAD_DIM), lambda h: (0, h)),
                                        pl.BlockSpec((1, HEAD_DIM), lambda h: (0, 0))],
        out_shape=[jax.ShapeDtypeStruct((SEQ, wide), BF16)] * 4 + [jax.ShapeDtypeStruct((2, wide), F32),
                                                                    jax.ShapeDtypeStruct((1, HEAD_DIM), F32)],
        scratch_shapes=[pltpu.VMEM((SEQ, HEAD_DIM), F32)] * 7 + [pltpu.VMEM((n_chunks, HEAD_DIM, HEAD_DIM), F32),
                                                                 pltpu.VMEM((n_chunks, HEAD_DIM, HEAD_DIM), BF16),
                                                                 pltpu.VMEM((n_chunks, HEAD_DIM, HEAD_DIM), BF16)],
        sem=("arbitrary",),
        operands=[proj, proj, proj, proj, lb_logits, norm_w, o_pre, d_aout] + list(fused_arrays))


Q_TILE = 512
ATT_SCALE = HEAD_DIM ** -0.5
ATT_OFF = 4 * HEADS


def _qk_prep(proj, q_w, k_w, fused=None, fused_arrays=()):
    def body(aq_ref, ak_ref, av_ref, qw_ref, kw_ref, qn_ref, kn_ref, v_ref):
        aq = aq_ref[...]
        ak = ak_ref[...]
        qn_ref[...] = (aq * lax.rsqrt(jnp.mean(aq * aq, axis=-1, keepdims=True) + EPS) * qw_ref[...]).astype(BF16)
        kn_ref[...] = (ak * lax.rsqrt(jnp.mean(ak * ak, axis=-1, keepdims=True) + EPS) * kw_ref[...]).astype(BF16)
        v_ref[...] = av_ref[...].astype(BF16)

    wide = HEADS * HEAD_DIM
    vec = pl.BlockSpec((1, HEAD_DIM), lambda h: (0, 0))
    return _host_call(
        body, 5, 3, fused, _head_first_last, name="qk_prep", grid=(HEADS,),
        in_specs=[_head_col(ATT_OFF), _head_col(ATT_OFF + HEADS), _head_col(ATT_OFF + 2 * HEADS), vec, vec],
        out_specs=[_head_col(0)] * 3, out_shape=[jax.ShapeDtypeStruct((SEQ, wide), BF16)] * 3,
        scratch_shapes=[], sem=("parallel",), operands=[proj, proj, proj, q_w, k_w] + list(fused_arrays))


def _alibi_slopes():
    slopes = jnp.exp2(-8.0 * jnp.arange(1, HEADS + 1, dtype=F32) / HEADS)
    return jnp.broadcast_to(slopes[:, None, None], (HEADS, 1, HEAD_DIM))


SLOPE_SPEC = pl.BlockSpec((None, 1, HEAD_DIM), lambda h, i: (h, 0, 0))


N_Q_TILES = SEQ // Q_TILE
K_BLOCK = 512
NOT_ATTENDED = 1e35


def _att_tables():
    o = jnp.arange(N_Q_TILES, dtype=jnp.int32)[:, None, None]
    r = jnp.arange(Q_TILE, dtype=jnp.int32)[None, :, None]
    c = jnp.arange(K_BLOCK, dtype=jnp.int32)[None, None, :]
    dist = o * Q_TILE + r - c
    mult = ((dist <= 128).astype(F32) + (((dist % 4) == 0) & (dist <= 512)).astype(F32)
            + ((dist % 16) == 0).astype(F32))
    valid = (dist >= 0) & (mult > 0)
    return (jnp.where(valid, dist.astype(F32), NOT_ATTENDED),
            jnp.where(valid, jnp.log(jnp.maximum(mult, 1.0)), 0.0))


TABLE_SPEC = pl.BlockSpec((N_Q_TILES, Q_TILE, K_BLOCK), lambda h, i: (0, 0, 0))


def _att_block(q, k_ref, j, i, slope, dist_ref, lmul_ref):
    rows = pl.ds(pl.multiple_of(j * K_BLOCK, K_BLOCK), K_BLOCK)
    off = i - j * (K_BLOCK // Q_TILE)
    s = lax.dot_general(q, k_ref[rows, :], NT, preferred_element_type=F32) * ATT_SCALE
    return s - slope * dist_ref[off] + lmul_ref[off], rows


def _n_key_blocks(i):
    return (i + K_BLOCK // Q_TILE) // (K_BLOCK // Q_TILE)


def _att_first_last():
    h, i = pl.program_id(0), pl.program_id(1)
    return (h == 0) & (i == 0), (h == HEADS - 1) & (i == N_Q_TILES - 1)


def _attn_fwd(qn, kn, vb, fused=None, fused_arrays=()):
    def body(q_ref, k_ref, v_ref, sl_ref, dist_ref, lmul_ref, o_ref, lse_ref):
        i = pl.program_id(1)
        q = q_ref[...]
        slope = sl_ref[0:1, 0:1]

        def step(j, carry):
            m, l, acc = carry
            sb, rows = _att_block(q, k_ref, j, i, slope, dist_ref, lmul_ref)
            m_new = jnp.maximum(m, jnp.max(sb, axis=-1, keepdims=True))
            alpha = jnp.exp(m - m_new)
            p = jnp.exp(sb - m_new)
            l = alpha * l + jnp.sum(p, axis=-1, keepdims=True)
            acc = alpha * acc + jnp.dot(p.astype(BF16), v_ref[rows, :], preferred_element_type=F32)
            return m_new, l, acc

        m, l, acc = lax.fori_loop(0, _n_key_blocks(i), step,
                                  (jnp.full((Q_TILE, 1), -1e30, F32), jnp.zeros((Q_TILE, 1), F32),
                                   jnp.zeros((Q_TILE, HEAD_DIM), F32)))
        o_ref[...] = acc / l
        lse_ref[...] = m + jnp.log(l)

    wide = HEADS * HEAD_DIM
    qt = pl.BlockSpec((Q_TILE, HEAD_DIM), lambda h, i: (i, h))
    full = pl.BlockSpec((SEQ, HEAD_DIM), lambda h, i: (0, h))
    return _host_call(
        body, 6, 2, fused, _att_first_last, name="attn_fwd", grid=(HEADS, N_Q_TILES),
        in_specs=[qt, full, full, SLOPE_SPEC, TABLE_SPEC, TABLE_SPEC],
        out_specs=[qt, pl.BlockSpec((None, Q_TILE, 1), lambda h, i: (h, i, 0))],
        out_shape=[jax.ShapeDtypeStruct((SEQ, wide), F32), jax.ShapeDtypeStruct((HEADS, SEQ, 1), F32)],
        scratch_shapes=[], sem=("parallel", "parallel"),
        operands=[qn, kn, vb, _alibi_slopes(), *_att_tables()] + list(fused_arrays))


def _attn_bwd(qn, kn, vb, o, lse, d_mix, fused=None, fused_arrays=()):
    def body(q_ref, k_ref, v_ref, o_ref, lse_ref, do_ref, sl_ref, dist_ref, lmul_ref, dq_ref, dk_ref, dv_ref):
        i = pl.program_id(1)
        q = q_ref[...]
        do = do_ref[...]
        do_b = do.astype(BF16)
        slope = sl_ref[0:1, 0:1]
        lse = lse_ref[...]
        delta = jnp.sum(do * o_ref[...], axis=-1, keepdims=True)

        @pl.when(i == 0)
        def _():
            dk_ref[...] = jnp.zeros_like(dk_ref)
            dv_ref[...] = jnp.zeros_like(dv_ref)

        def step(j, dq):
            sb, rows = _att_block(q, k_ref, j, i, slope, dist_ref, lmul_ref)
            p = jnp.exp(sb - lse)
            dp = lax.dot_general(do_b, v_ref[rows, :], NT, preferred_element_type=F32)
            ds = (p * (dp - delta)).astype(BF16)
            dk_ref[rows, :] += lax.dot_general(ds, q, TN, preferred_element_type=F32) * ATT_SCALE
            dv_ref[rows, :] += lax.dot_general(p.astype(BF16), do_b, TN, preferred_element_type=F32)
            return dq + jnp.dot(ds, k_ref[rows, :], preferred_element_type=F32)

        dq = lax.fori_loop(0, _n_key_blocks(i), step, jnp.zeros((Q_TILE, HEAD_DIM), F32))
        dq_ref[...] = dq * ATT_SCALE

    wide = HEADS * HEAD_DIM
    qt = pl.BlockSpec((Q_TILE, HEAD_DIM), lambda h, i: (i, h))
    full = pl.BlockSpec((SEQ, HEAD_DIM), lambda h, i: (0, h))
    return _host_call(
        body, 9, 3, fused, _att_first_last, name="attn_bwd", grid=(HEADS, N_Q_TILES),
        in_specs=[qt, full, full, qt, pl.BlockSpec((None, Q_TILE, 1), lambda h, i: (h, i, 0)),
                  pl.BlockSpec((Q_TILE, HEAD_DIM), lambda h, i: (i, h + HEADS)), SLOPE_SPEC, TABLE_SPEC, TABLE_SPEC],
        out_specs=[qt, full, full], out_shape=[jax.ShapeDtypeStruct((SEQ, wide), F32)] * 3,
        scratch_shapes=[], sem=("parallel", "arbitrary"),
        operands=[qn, kn, vb, o, lse, d_mix, _alibi_slopes(), *_att_tables()] + list(fused_arrays))


def _qk_bwd(proj, q_w, k_w, dqn, dkn, dv):
    def body(aq_ref, ak_ref, qw_ref, kw_ref, dqn_ref, dkn_ref, dv_ref, daq_ref, dak_ref, dav_ref, gq_ref, gk_ref):
        h = pl.program_id(0)

        @pl.when(h == 0)
        def _():
            gq_ref[...] = jnp.zeros_like(gq_ref)
            gk_ref[...] = jnp.zeros_like(gk_ref)

        def one(a_ref, w_ref, d_ref, da_ref, g_ref):
            a = a_ref[...]
            d = d_ref[...]
            rs = lax.rsqrt(jnp.mean(a * a, axis=-1, keepdims=True) + EPS)
            ah = a * rs
            g_ref[...] += jnp.sum(d * ah, axis=0, keepdims=True)
            dah = d * w_ref[...]
            da_ref[...] = (rs * (dah - ah * jnp.mean(dah * ah, axis=-1, keepdims=True))).astype(BF16)

        one(aq_ref, qw_ref, dqn_ref, daq_ref, gq_ref)
        one(ak_ref, kw_ref, dkn_ref, dak_ref, gk_ref)
        dav_ref[...] = dv_ref[...].astype(BF16)

    wide = HEADS * HEAD_DIM
    vec = pl.BlockSpec((1, HEAD_DIM), lambda h: (0, 0))
    return pl.pallas_call(
        body, name="qk_bwd", grid=(HEADS,),
        in_specs=[_head_col(ATT_OFF), _head_col(ATT_OFF + HEADS), vec, vec, _head_col(0), _head_col(0), _head_col(0)],
        out_specs=[_head_col(0)] * 3 + [vec, vec],
        out_shape=[jax.ShapeDtypeStruct((SEQ, wide), BF16)] * 3 + [jax.ShapeDtypeStruct((1, HEAD_DIM), F32)] * 2,
        compiler_params=_params(("arbitrary",)))(proj, proj, q_w, k_w, dqn, dkn, dv)


def _pair_sum(name, partial, theirs, core):
    _, r, c = theirs.shape
    tr = r // 2 if r % 16 == 0 else r

    def body(core_ref, a_ref, b_ref, o_ref):
        o_ref[...] = (a_ref[...].astype(F32) + b_ref[...].astype(F32)).astype(BF16)

    spec = pl.BlockSpec((None, tr, c), lambda q, i, core_ref: (q, i, 0))
    grid_spec = pltpu.PrefetchScalarGridSpec(
        num_scalar_prefetch=1, grid=(4, r // tr),
        in_specs=[pl.BlockSpec((None, tr, c), lambda q, i, core_ref: (2 * q + core_ref[0], i, 0)), spec],
        out_specs=spec)
    return pl.pallas_call(body, name=name, grid_spec=grid_spec, out_shape=jax.ShapeDtypeStruct(theirs.shape, BF16),
                          compiler_params=_params(("parallel", "parallel")))(core, partial, theirs)


def _adamw_step(w, m, v, g):
    nm = ADAM_B1 * m + (1.0 - ADAM_B1) * g
    nv = ADAM_B2 * v + (1.0 - ADAM_B2) * (g * g)
    m_hat = nm / (1.0 - ADAM_B1 ** ADAM_STEP)
    v_hat = nv / (1.0 - ADAM_B2 ** ADAM_STEP)
    return -ADAM_LR * (m_hat / (jnp.sqrt(v_hat) + ADAM_EPS) + ADAM_WD * w), nm, nv


def _adamw(name, w, m, v, addends, tr=None):
    r, c = w.shape
    tr = r if tr is None else tr
    n_add = len(addends)

    def body(*refs):
        w_ref, m_ref, v_ref = refs[:3]
        add_refs = refs[3:3 + n_add]
        g_ref, d_ref, nm_ref, nv_ref = refs[3 + n_add:]
        g = add_refs[0][...].astype(F32)
        for a_ref in add_refs[1:]:
            g = g + a_ref[...].astype(F32)
        g_ref[...] = g
        d_ref[...], nm_ref[...], nv_ref[...] = _adamw_step(w_ref[...], m_ref[...], v_ref[...], g)

    spec = pl.BlockSpec((tr, c), lambda i: (i, 0))
    out = jax.ShapeDtypeStruct((r, c), F32)
    return pl.pallas_call(body, name=name, grid=(r // tr,), in_specs=[spec] * (3 + n_add), out_specs=[spec] * 4,
                          out_shape=[out] * 4, compiler_params=_params(("parallel",)))(w, m, v, *addends)


def _adamw_reduced(name, w, m, v, chip_sums, received, chip, tr):
    r, c = w.shape

    def body(chip_ref, w_ref, m_ref, v_ref, own_ref, r0_ref, r1_ref, r2_ref, g_ref, d_ref, nm_ref, nv_ref):
        g = ((own_ref[...].astype(F32) + r0_ref[...].astype(F32)) + r1_ref[...].astype(F32)) + r2_ref[...].astype(F32)
        g_ref[...] = g
        d_ref[...], nm_ref[...], nv_ref[...] = _adamw_step(w_ref[...], m_ref[...], v_ref[...], g)

    spec = pl.BlockSpec((tr, c), lambda i, chip_ref: (i, 0))

    def slot(k):
        return pl.BlockSpec((None, tr, c), lambda i, chip_ref: (k, i, 0))

    grid_spec = pltpu.PrefetchScalarGridSpec(
        num_scalar_prefetch=1, grid=(r // tr,),
        in_specs=[spec, spec, spec, pl.BlockSpec((None, tr, c), lambda i, chip_ref: (chip_ref[0], i, 0)),
                  slot(0), slot(1), slot(2)],
        out_specs=[spec] * 4)
    out = jax.ShapeDtypeStruct((r, c), F32)
    return pl.pallas_call(body, name=name, grid_spec=grid_spec, out_shape=[out] * 4,
                          compiler_params=_params(("parallel",)))(chip, w, m, v, chip_sums, received, received, received)


def _sum_devices(gathered):
    _, r, c = gathered.shape

    def body(g_ref, o_ref):
        acc = g_ref[0]
        for d in range(1, N_DEV):
            acc = acc + g_ref[d]
        o_ref[...] = acc

    return pl.pallas_call(body, name="sum_devices", out_shape=jax.ShapeDtypeStruct((r, c), F32))(gathered)


def _pack_rows(vectors, rows):
    flat = jnp.concatenate([v.reshape(-1) for v in vectors])
    return jnp.pad(flat, (0, rows * 128 - flat.shape[0])).reshape(rows, 128)


def _unpack(flat, shapes):
    out, off = [], 0
    for shp in shapes:
        n = 1
        for d in shp:
            n *= d
        out.append(flat[off:off + n].reshape(shp))
        off += n
    return out


def _device_step(xs, tgt, mod, norm1_w, norm2_w, lb_logits, hg_norm_w, q_norm_w, k_norm_w, conv_w_full, conv_b,
                 win_g, w_out_x, w_up_x, w_down_x, core=None):
    fused = core is not None
    shift1, scale1, gate1, shift2, scale2, gate2 = (mod[k] for k in range(6))

    h, rstd1 = _norm_fwd("norm1_fwd", xs, norm1_w, scale1, shift1)
    if fused:
        near = (0, 1, 2)
        head_rows, tail_rows = (0, UP_HEAD_ROWS), (UP_HEAD_ROWS, D_MODEL - UP_HEAD_ROWS)
        proj, (wout_g, wup_g) = _mm_blocked_rhs(
            "mm_in", h, win_g, fused_arrays=[w_out_x, w_up_x],
            fused=[_FusedCopies("gather", [w_out_x]), _FusedCopies("gather", [w_up_x], peers=near, rows=head_rows)])
        (a_out, o_pre), (wup_g,) = _hgrn_fwd(
            proj, lb_logits, hg_norm_w, fused_arrays=[w_up_x, wup_g],
            fused=_FusedCopies("gather_more", [w_up_x, wup_g], peers=near, rows=tail_rows, relay_rows=head_rows))
        wout_g, = _forward_to_sibling("allgather_stage2_out", [wout_g])
        wout_full = wout_g.reshape(D_MODEL, D_MODEL)
        (qn, kn, vb), _ = _qk_prep(proj, q_norm_w, k_norm_w)
        (att_o, lse), (wup_g,) = _attn_fwd(qn, kn, vb, _FusedCopies("relay", [wup_g], rows=tail_rows), [wup_g])
    else:
        proj = _mm_blocked_rhs("mm_in", h, win_g)
        (a_out, o_pre), _ = _hgrn_fwd(proj, lb_logits, hg_norm_w)
        wup_g, wout_full, wdown_full = w_up_x, w_out_x, w_down_x
        (qn, kn, vb), _ = _qk_prep(proj, q_norm_w, k_norm_w)
        (att_o, lse), _ = _attn_fwd(qn, kn, vb)
    mixin = jnp.concatenate([a_out, att_o.astype(BF16)], axis=1)
    if fused:
        mix, (wup_g,) = _mm_plain("mm_out", mixin, wout_full, NN, 512, 1024, F32,
                                  fused=_FusedCopies("forward", [wup_g]), fused_arrays=[wup_g])
    else:
        mix = _mm_plain("mm_out", mixin, wout_full, NN, 512, 1024, F32)
    x1, h2, rstd2 = _norm_fwd("norm2_fwd", xs, norm2_w, scale2, shift2, resid=mix, gate=gate1)
    if fused:
        u, (wdown_g,) = _mm_blocked_rhs("mm_up", h2, wup_g, fused=_FusedCopies("gather", [w_down_x]),
                                        fused_arrays=[w_down_x])
        y, (wdown_g,) = _conv_gate_fwd(u, conv_w_full, conv_b, _FusedCopies("forward", [wdown_g]), [wdown_g])
        wdown_full = wdown_g.reshape(D_FF, D_MODEL)
    else:
        u = _mm_blocked_rhs("mm_up", h2, wup_g)
        y = _conv_gate_fwd(u, conv_w_full, conv_b)
    ffn = _mm_plain("mm_down", y, wdown_full, NN, 512, 512, F32)
    loss_v, dout, dffn, dgate2 = _loss_head(x1, ffn, gate2, tgt)

    dy = _mm_plain("mm_down_dx", dffn, wdown_full, NT, 512, UP_BLK, BF16)
    gw_down = _mm_plain("mm_down_dw", y, dffn, TN, UP_BLK, 1024, BF16)
    da, dg, gconv_w, gconv_b = _conv_gate_bwd(u, dy, conv_w_full, conv_b)
    dh2 = _mm_halves_rhs_t("mm_up_dx", da, dg, wup_g)
    gw_up = _mm_halves_wgrad("mm_up_dw", h2, da, dg)
    if fused:
        part_up, part_down = gw_up, gw_down.reshape(N_DEV, FF_BLK, D_MODEL)
        (dx1, dmix, dshift2, dscale2, gnorm2, dgate1), (sib_up,) = _norm_bwd(
            "norm2_bwd", dh2, x1, rstd2, norm2_w, scale2, dout, mix=mix, gate=gate1,
            fused=_FusedCopies("sibling", [part_up]), fused_arrays=[part_up])
    else:
        dx1, dmix, dshift2, dscale2, gnorm2, dgate1 = _norm_bwd(
            "norm2_bwd", dh2, x1, rstd2, norm2_w, scale2, dout, mix=mix, gate=gate1)
    gw_out = _mm_plain("mm_out_dw", mixin, dmix, TN, 512, 1024, BF16)
    if fused:
        part_out = gw_out.reshape(N_DEV, OUT_BLK, D_MODEL)
        dmixin, (sib_out, sib_down) = _mm_plain(
            "mm_out_dx", dmix, wout_full, NT, 512, 1024, F32,
            fused=_FusedCopies("sibling", [part_out, part_down]), fused_arrays=[part_out, part_down])
        cs_up = _pair_sum("grad_pair_sum_up", part_up, sib_up, core)
        cs_out = _pair_sum("grad_pair_sum_out", part_out, sib_out, core)
        cs_down = _pair_sum("grad_pair_sum_down", part_down, sib_down, core)
        (dhq, dhf, dhi, dhg, glog, ghg), (fc_up,) = _hgrn_bwd(
            proj, lb_logits, hg_norm_w, o_pre, dmixin, _FusedCopies("chips", [cs_up]), [cs_up])
        (dqn, dkn, dvv), (fc_down,) = _attn_bwd(qn, kn, vb, att_o, lse, dmixin,
                                                _FusedCopies("chips", [cs_down]), [cs_down])
    else:
        dmixin = _mm_plain("mm_out_dx", dmix, wout_full, NT, 512, 1024, F32)
        (dhq, dhf, dhi, dhg, glog, ghg), _ = _hgrn_bwd(proj, lb_logits, hg_norm_w, o_pre, dmixin)
        (dqn, dkn, dvv), _ = _attn_bwd(qn, kn, vb, att_o, lse, dmixin)
    daq, dak, dav, gqw, gkw = _qk_bwd(proj, q_norm_w, k_norm_w, dqn, dkn, dvv)
    dproj = jnp.concatenate([dhq, dhf, dhi, dhg, daq, dak, dav], axis=1)
    if fused:
        gw_in, (fc_out,) = _mm_wgrad_blocked("mm_in_dw", h, dproj, fused=_FusedCopies("chips", [cs_out]),
                                             fused_arrays=[cs_out])
        from_sibling, = _exchange_sibling("grad_exchange_sibling_b", [gw_in])
        cs_in = _pair_sum("grad_pair_sum_in", gw_in, from_sibling, core)
        dh, (fc_in,) = _mm_blocked_rhs_t("mm_in_dx", dproj, win_g, fused=_FusedCopies("chips", [cs_in]),
                                         fused_arrays=[cs_in])
        large = [(cs_in, fc_in), (cs_out, fc_out), (cs_up, fc_up), (cs_down, fc_down)]
    else:
        gw_in = _mm_wgrad_blocked("mm_in_dw", h, dproj)
        dh = _mm_blocked_rhs_t("mm_in_dx", dproj, win_g)
        large = [gw_in, gw_out, gw_up, gw_down]
    grad_x, dshift1, dscale1, gnorm1 = _norm_bwd("norm1_bwd", dh, xs, rstd1, norm1_w, scale1, dx1)
    gmod = jnp.concatenate([dshift1, dscale1, dgate1, dshift2, dscale2, dgate2], axis=1)
    return (loss_v, grad_x, gmod, gnorm1, gnorm2, glog, ghg, gqw, gkw, gconv_b, gconv_w, *large)


def kernel(x, c, w_ada, b_ada, norm1_w, w_in, lb_logits, hg_norm_w, q_norm_w, k_norm_w, w_out, norm2_w, w_up, conv_w, conv_b, w_down, loss_target, m_w_ada, m_b_ada, m_norm1_w, m_w_in, m_lb_logits, m_hg_norm_w, m_q_norm_w, m_k_norm_w, m_w_out, m_norm2_w, m_w_up, m_conv_w, m_conv_b, m_w_down, v_w_ada, v_b_ada, v_norm1_w, v_w_in, v_lb_logits, v_hg_norm_w, v_q_norm_w, v_k_norm_w, v_w_out, v_norm2_w, v_w_up, v_conv_w, v_conv_b, v_w_down):
    ix, iy, ic = lax.axis_index("x"), lax.axis_index("y"), lax.axis_index("c")
    me = 4 * ix + 2 * iy + ic
    my_chip = 2 * ix + iy

    xs = x[0]
    tgt = loss_target[0]

    send_sems, recv_sems, w_in_blk, win_land, token = _gather_near_start(w_in[0].astype(BF16))

    c = c + token[0:1, 0:1]
    c_all = _allgather_vmem(c.reshape(8, D_MODEL // 8), "allgather_c").reshape(N_DEV, D_MODEL)
    b_blk = lax.dynamic_slice_in_dim(b_ada, me * ADA_BLK, ADA_BLK, axis=1)
    mod_cols = _ada_fwd(c_all, w_ada[0], b_blk)
    mod_all = _allgather_vmem(mod_cols, "allgather_mod").reshape(N_DEV, N_DEV, ADA_BLK)
    mod = lax.dynamic_index_in_dim(mod_all, me, axis=1, keepdims=False).reshape(6, 1, D_MODEL)
    w_in_blk, win_land = _gather_near_wait(send_sems, recv_sems, w_in_blk, win_land, mod)
    win_g = _gather_finish(w_in_blk, win_land)

    conv_w_all = _allgather_vmem(_pack_rows([conv_w[0]], 24), "allgather_conv_w").reshape(N_DEV, 24 * 128)
    conv_w_full = conv_w_all[:, :3 * FF_BLK].reshape(N_DEV, 3, FF_BLK).transpose(1, 0, 2).reshape(3, D_FF)

    (loss_v, grad_x, gmod, gnorm1, gnorm2, glog, ghg, gqw, gkw, gconv_b, gconv_w,
     rs_in, rs_out, rs_up, rs_down) = _device_step(
        xs, tgt, mod, norm1_w, norm2_w, lb_logits, hg_norm_w, q_norm_w, k_norm_w, conv_w_full, conv_b,
        win_g, w_out[0].astype(BF16), w_up[0].astype(BF16), w_down[0].astype(BF16),
        core=jnp.reshape(ic, (1,)).astype(jnp.int32))
    loss = lax.psum(loss_v[0, 0], AXES)

    small_shapes = [(1, 6 * D_MODEL), (1, D_MODEL), (1, D_MODEL), (2, HEADS * HEAD_DIM), (1, HEAD_DIM),
                    (1, HEAD_DIM), (1, HEAD_DIM), (1, D_FF), (3, D_FF)]
    small = [gmod, gnorm1, gnorm2, glog, ghg, gqw, gkw, gconv_b, gconv_w]
    n_small = sum(a.size for a in small)
    rows = -(-n_small // 1024) * 8
    gathered = _allgather_vmem(_pack_rows(small, rows), "allgather_small").reshape(N_DEV, rows, 128)
    summed = _sum_devices(gathered).reshape(-1)
    (g_b_ada, g_norm1, g_norm2, g_lb, g_hg, g_q, g_k, g_conv_b, g_conv_w_full) = _unpack(summed, small_shapes)
    g_conv_w = lax.dynamic_slice_in_dim(g_conv_w_full, me * FF_BLK, FF_BLK, axis=1)

    gmod_all = gathered[:, :6 * D_MODEL // 128, :].reshape(N_DEV, 6 * D_MODEL)
    gmod_cols = lax.dynamic_slice_in_dim(gmod_all, me * ADA_BLK, ADA_BLK, axis=1)
    g_w_ada_raw = _ada_wgrad(c_all, gmod_cols)

    chip = jnp.reshape(my_chip, (1,)).astype(jnp.int32)

    def big_update(name, w, m, v, rs, tr):
        chip_sums, received = rs
        return _adamw_reduced(name, w[0], m[0], v[0], chip_sums, received, chip, tr)

    r_in = big_update("adamw_w_in", w_in, m_w_in, v_w_in, rs_in, 256)
    r_out = big_update("adamw_w_out", w_out, m_w_out, v_w_out, rs_out, 128)
    r_up = big_update("adamw_w_up", w_up, m_w_up, v_w_up, rs_up, 256)
    r_down = big_update("adamw_w_down", w_down, m_w_down, v_w_down, rs_down, 176)
    r_ada = _adamw("adamw_w_ada", w_ada[0], m_w_ada[0], v_w_ada[0], [g_w_ada_raw], tr=256)
    r_convw = _adamw("adamw_conv_w", conv_w[0], m_conv_w[0], v_conv_w[0], [g_conv_w])

    rep_shapes = [(1, 6 * D_MODEL), (1, D_MODEL), (1, D_MODEL), (2, HEADS * HEAD_DIM), (1, HEAD_DIM),
                  (1, HEAD_DIM), (1, HEAD_DIM), (1, D_FF)]
    rep_rows = -(-sum(a * b for a, b in rep_shapes) // 1024) * 8
    pack = lambda arrs: _pack_rows(arrs, rep_rows)
    rep = _adamw("adamw_small",
                 pack([b_ada, norm1_w, norm2_w, lb_logits, hg_norm_w, q_norm_w, k_norm_w, conv_b]),
                 pack([m_b_ada, m_norm1_w, m_norm2_w, m_lb_logits, m_hg_norm_w, m_q_norm_w, m_k_norm_w, m_conv_b]),
                 pack([v_b_ada, v_norm1_w, v_norm2_w, v_lb_logits, v_hg_norm_w, v_q_norm_w, v_k_norm_w, v_conv_b]),
                 [pack([g_b_ada, g_norm1, g_norm2, g_lb, g_hg, g_q, g_k, g_conv_b])])
    rep = [_unpack(r.reshape(-1), rep_shapes) for r in rep]

    def big(r):
        return [a[None] for a in r]

    order = {"w_ada": big(r_ada), "b_ada": [r[0] for r in rep], "norm1_w": [r[1] for r in rep],
             "w_in": big(r_in), "lb_logits": [r[3] for r in rep], "hg_norm_w": [r[4] for r in rep],
             "q_norm_w": [r[5] for r in rep], "k_norm_w": [r[6] for r in rep], "w_out": big(r_out),
             "norm2_w": [r[2] for r in rep], "w_up": big(r_up), "conv_w": big(r_convw),
             "conv_b": [r[7] for r in rep], "w_down": big(r_down)}
    names = ["w_ada", "b_ada", "norm1_w", "w_in", "lb_logits", "hg_norm_w", "q_norm_w", "k_norm_w", "w_out",
             "norm2_w", "w_up", "conv_w", "conv_b", "w_down"]
    outs = [loss, grad_x[None]]
    for kind in range(4):
        outs += [order[n][kind] for n in names]
    return tuple(outs)
```

```python
import jax
import jax.numpy as jnp
from jax import lax
from jax.experimental import pallas as pl
from jax.experimental.pallas import tpu as pltpu

F32 = jnp.float32
BF16 = jnp.bfloat16

N_DEV = 8
SEQ = 2048
D_MODEL = 2048
HEADS = 8
HEAD_DIM = 128
IN_COLS = 7168
IN_BLK = IN_COLS // N_DEV
D_FF = 5632
UP_BLK = 2 * D_FF // N_DEV
FF_BLK = D_FF // N_DEV
ADA_BLK = 6 * D_MODEL // N_DEV
OUT_BLK = D_MODEL // N_DEV
EPS = 1e-6
CHUNK = 16
ROW_TILE = 256
MM_TILE = 1024
V7X_VMEM_LIMIT = 56 * 1024 * 1024

ADAM_LR = 0.001
ADAM_B1 = 0.9
ADAM_B2 = 0.999
ADAM_EPS = 1e-08
ADAM_WD = 0.01
ADAM_STEP = 10

NN = (((1,), (0,)), ((), ()))
NT = (((1,), (1,)), ((), ()))
TN = (((0,), (0,)), ((), ()))
MESH = pl.DeviceIdType.MESH
AXES = ("x", "y", "c")


def _params(sem=None, vmem=V7X_VMEM_LIMIT):
    return pltpu.CompilerParams(dimension_semantics=sem, vmem_limit_bytes=vmem)


def _sigmoid(x):
    return 1.0 / (1.0 + jnp.exp(-x))


def _dsilu(x, s):
    return s * (1.0 + x * (1.0 - s))


def _lane_sum(x, ones_bf16):
    return jnp.dot(x.astype(BF16), ones_bf16, preferred_element_type=F32)


def _mesh_pos():
    return lax.axis_index("x"), lax.axis_index("y"), lax.axis_index("c")


def _allgather_vmem(x_blk, name):
    m_per, n = x_blk.shape

    def body(x_ref, out_ref, send_sems, recv_sems, local_sem):
        x, y, c = _mesh_pos()
        me, sibling = (x, y, c), (x, y, 1 - c)
        chips = [(1 - x, y), (x, 1 - y), (1 - x, 1 - y)]

        def rows(px, py, pc):
            return out_ref.at[pl.ds((4 * px + 2 * py + pc) * m_per, m_per), :]

        def copy(k, block, to, src=None):
            return pltpu.make_async_remote_copy(
                src_ref=rows(*block) if src is None else src, dst_ref=rows(*block),
                send_sem=send_sems.at[k], recv_sem=recv_sems.at[k], device_id=to, device_id_type=MESH)

        mine = pltpu.make_async_copy(x_ref, rows(*me), local_sem)
        mine.start()
        first = [copy(0, me, sibling, src=x_ref)]
        first += [copy(1 + j, me, (*chip, c), src=x_ref) for j, chip in enumerate(chips)]
        for cp in first:
            cp.start()
        passed = [copy(4 + j, (*chip, c), sibling) for j, chip in enumerate(chips)]
        for j, chip in enumerate(chips):
            copy(1 + j, (*chip, c), me).wait_recv()
            passed[j].start()
        copy(0, sibling, me).wait_recv()
        for j, chip in enumerate(chips):
            copy(4 + j, (*chip, 1 - c), me).wait_recv()
        for cp in first + passed:
            cp.wait_send()
        mine.wait()

    return pl.pallas_call(
        body, name=name,
        out_shape=jax.ShapeDtypeStruct((N_DEV * m_per, n), x_blk.dtype),
        in_specs=[pl.BlockSpec(memory_space=pltpu.VMEM)],
        out_specs=pl.BlockSpec(memory_space=pltpu.VMEM),
        scratch_shapes=[pltpu.SemaphoreType.DMA((7,)), pltpu.SemaphoreType.DMA((7,)), pltpu.SemaphoreType.DMA],
    )(x_blk)


def _flip(v, bit):
    return v + bit - 2 * v * bit


def _relay_chips(x, y, c):
    return (_flip(x, 1 - c), _flip(y, c)), (_flip(x, c), _flip(y, 1 - c))


UP_HEAD_ROWS = 768
GATHER_PARTS = 4


def _allgather_weights(blocks):
    n_arr = len(blocks)
    parts = GATHER_PARTS

    def body(*refs):
        ins, outs = refs[:n_arr], refs[n_arr:2 * n_arr]
        send_sems, recv_sems, local_sems = refs[2 * n_arr:]
        x, y, c = _mesh_pos()
        me, sibling = (x, y, c), (x, y, 1 - c)
        near = [(1 - x, y), (x, 1 - y)]
        chips = near + [(1 - x, 1 - y)]
        relay_from, relay_to = _relay_chips(x, y, c)

        def rows(a, p):
            hr = ins[a].shape[0] // parts
            return pl.ds(p * hr, hr)

        def slot(a, pos, p):
            return outs[a].at[4 * pos[0] + 2 * pos[1] + pos[2], rows(a, p)]

        def copy(a, k, p, src, lands, to):
            return pltpu.make_async_remote_copy(
                src_ref=src, dst_ref=slot(a, lands, p), send_sem=send_sems.at[a, k, p], recv_sem=recv_sems.at[a, k, p],
                device_id=to, device_id_type=MESH)

        sent = []
        local = [pltpu.make_async_copy(ins[a], outs[a].at[4 * x + 2 * y + c], local_sems.at[a]) for a in range(n_arr)]
        for cp in local:
            cp.start()
        for p in range(parts):
            for a in range(n_arr):
                own = ins[a].at[rows(a, p)]
                sent.append(copy(a, 0, p, own, me, sibling))
                sent += [copy(a, 1 + j, p, own, me, (*chip, c)) for j, chip in enumerate(near)]
        for cp in sent:
            cp.start()

        def start(cp):
            cp.start()
            sent.append(cp)

        for p in range(parts):
            for a in range(n_arr):
                for j, chip in enumerate(near):
                    copy(a, 1 + j, p, ins[a].at[rows(a, p)], (*chip, c), me).wait_recv()
                    start(copy(a, 4 + j, p, slot(a, (*chip, c), p), (*chip, c), sibling))
                start(copy(a, 3, p, slot(a, (*relay_from, c), p), (*relay_from, c), (*relay_to, c)))
        for p in range(parts):
            for a in range(n_arr):
                copy(a, 3, p, ins[a].at[rows(a, p)], (*chips[2], c), me).wait_recv()
                start(copy(a, 6, p, slot(a, (*chips[2], c), p), (*chips[2], c), sibling))
        for p in range(parts):
            for a in range(n_arr):
                copy(a, 0, p, ins[a].at[rows(a, p)], sibling, me).wait_recv()
                for j, chip in enumerate(chips):
                    copy(a, 4 + j, p, ins[a].at[rows(a, p)], (*chip, 1 - c), me).wait_recv()
        for cp in sent:
            cp.wait_send()
        for cp in local:
            cp.wait()

    return pl.pallas_call(
        body, name="allgather_weights",
        out_shape=[jax.ShapeDtypeStruct((N_DEV,) + b.shape, b.dtype) for b in blocks],
        in_specs=[pl.BlockSpec(memory_space=pltpu.HBM)] * n_arr, out_specs=[pl.BlockSpec(memory_space=pltpu.HBM)] * n_arr,
        scratch_shapes=[pltpu.SemaphoreType.DMA((n_arr, 7, parts)), pltpu.SemaphoreType.DMA((n_arr, 7, parts)),
                        pltpu.SemaphoreType.DMA((n_arr,))],
    )(*blocks)


HBM_SPEC = pl.BlockSpec(memory_space=pltpu.HBM)


class _FusedCopies:
    def __init__(self, kind, arrays, peers=(0, 1, 2, 3), rows=None, relay_rows=None):
        self.kind = kind
        self.peers = peers
        self.rows = rows
        self.relay_rows = relay_rows
        n = len(arrays) // 2 if kind == "gather_more" else len(arrays)
        self.n = n
        self.n_in = len(arrays)
        self.aliases = {}
        if kind == "gather":
            self.out_shape = [jax.ShapeDtypeStruct((N_DEV,) + a.shape, a.dtype) for a in arrays]
            self.scratch_shapes = [pltpu.SemaphoreType.DMA((n, 4, GATHER_PARTS)),
                                   pltpu.SemaphoreType.DMA((n, 4, GATHER_PARTS)), pltpu.SemaphoreType.DMA((n,))]
        elif kind == "gather_more":
            self.out_shape = [jax.ShapeDtypeStruct(a.shape, a.dtype) for a in arrays[n:]]
            self.scratch_shapes = [pltpu.SemaphoreType.DMA((n, 5, GATHER_PARTS)),
                                   pltpu.SemaphoreType.DMA((n, 5, GATHER_PARTS)), pltpu.SemaphoreType.DMA((n,))]
            self.aliases = {n + a: a for a in range(n)}
        elif kind == "relay":
            self.out_shape = [jax.ShapeDtypeStruct(a.shape, a.dtype) for a in arrays]
            self.scratch_shapes = [pltpu.SemaphoreType.DMA((n,)), pltpu.SemaphoreType.DMA((n,))]
            self.aliases = {a: a for a in range(n)}
        elif kind == "forward":
            self.out_shape = [jax.ShapeDtypeStruct(a.shape, a.dtype) for a in arrays]
            self.scratch_shapes = [pltpu.SemaphoreType.DMA((n, 3)), pltpu.SemaphoreType.DMA((n, 3))]
            self.aliases = {a: a for a in range(n)}
        elif kind == "sibling":
            self.out_shape = [jax.ShapeDtypeStruct((4,) + a.shape[1:], a.dtype) for a in arrays]
            self.scratch_shapes = [pltpu.SemaphoreType.DMA((n, 4)), pltpu.SemaphoreType.DMA((n, 4))]
        else:
            self.out_shape = [jax.ShapeDtypeStruct((3,) + a.shape[1:], a.dtype) for a in arrays]
            self.scratch_shapes = [pltpu.SemaphoreType.DMA((n, 3)), pltpu.SemaphoreType.DMA((n, 3))]
        self.in_specs = [HBM_SPEC] * self.n_in
        self.out_specs = [HBM_SPEC] * n
        self.n_scratch = len(self.scratch_shapes)

    def copies(self, ins, outs, sems):
        x, y, c = _mesh_pos()
        chips = [(1 - x, y), (x, 1 - y), (1 - x, 1 - y)]
        sibling = (x, y, 1 - c)
        starts, waits = [], []
        relay_from, relay_to = _relay_chips(x, y, c)

        def relayed(a, buf, lands, send_sem, recv_sem, rows):
            first, count = rows or (0, buf.shape[1])
            span = pl.ds(first, count)
            return pltpu.make_async_remote_copy(
                src_ref=buf.at[4 * relay_from[0] + 2 * relay_from[1] + c, span],
                dst_ref=outs[a].at[4 * lands[0] + 2 * lands[1] + c, span], send_sem=send_sem, recv_sem=recv_sem,
                device_id=(*relay_to, c), device_id_type=MESH)

        if self.kind in ("gather", "gather_more"):
            send_sems, recv_sems, local_sems = sems
            me = (x, y, c)
            peers = [sibling] + [(px, py, c) for px, py in chips]

            def slot(a, pos):
                return outs[a].at[4 * pos[0] + 2 * pos[1] + pos[2]]

            def span(a, p=None):
                first, count = self.rows or (0, ins[a].shape[0])
                if p is None:
                    return pl.ds(first, count)
                return pl.ds(first + p * (count // GATHER_PARTS), count // GATHER_PARTS)

            def remote(a, k, p, lands_from):
                return pltpu.make_async_remote_copy(
                    src_ref=ins[a].at[span(a, p)], dst_ref=slot(a, lands_from).at[span(a, p)],
                    send_sem=send_sems.at[a, k, p], recv_sem=recv_sems.at[a, k, p], device_id=peers[k],
                    device_id_type=MESH)

            for a in range(self.n):
                local = pltpu.make_async_copy(ins[a].at[span(a)], slot(a, me).at[span(a)], local_sems.at[a])
                starts.append(local)
                waits.append(local)
            for p in range(GATHER_PARTS):
                for a in range(self.n):
                    for k in self.peers:
                        starts.append(remote(a, k, p, me))
                        waits.append(remote(a, k, p, peers[k]))
            if self.kind == "gather_more" and self.relay_rows is not None:
                for a in range(self.n):
                    buf = ins[self.n + a]
                    starts.append(relayed(a, buf, relay_from, send_sems.at[a, 4, 0], recv_sems.at[a, 4, 0],
                                          self.relay_rows))
                    waits.append(relayed(a, buf, chips[2], send_sems.at[a, 4, 0], recv_sems.at[a, 4, 0],
                                         self.relay_rows))
        elif self.kind == "relay":
            send_sems, recv_sems = sems
            for a in range(self.n):
                starts.append(relayed(a, ins[a], relay_from, send_sems.at[a], recv_sems.at[a], self.rows))
                waits.append(relayed(a, ins[a], chips[2], send_sems.at[a], recv_sems.at[a], self.rows))
        elif self.kind == "forward":
            send_sems, recv_sems = sems

            def passed_on(a, j, pc_src, pc_dst):
                px, py = chips[j]
                return pltpu.make_async_remote_copy(
                    src_ref=ins[a].at[4 * px + 2 * py + pc_src], dst_ref=outs[a].at[4 * px + 2 * py + pc_dst],
                    send_sem=send_sems.at[a, j], recv_sem=recv_sems.at[a, j], device_id=sibling, device_id_type=MESH)

            for a in range(self.n):
                for j in range(3):
                    starts.append(passed_on(a, j, c, c))
                    waits.append(passed_on(a, j, c, 1 - c))
        elif self.kind == "sibling":
            send_sems, recv_sems = sems
            for a in range(self.n):
                for q in range(4):
                    cp = pltpu.make_async_remote_copy(
                        src_ref=ins[a].at[2 * q + 1 - c], dst_ref=outs[a].at[q], send_sem=send_sems.at[a, q],
                        recv_sem=recv_sems.at[a, q], device_id=sibling, device_id_type=MESH)
                    starts.append(cp)
                    waits.append(cp)
        else:
            send_sems, recv_sems = sems
            for a in range(self.n):
                for j, (px, py) in enumerate(chips):
                    cp = pltpu.make_async_remote_copy(
                        src_ref=ins[a].at[2 * px + py], dst_ref=outs[a].at[j], send_sem=send_sems.at[a, j],
                        recv_sem=recv_sems.at[a, j], device_id=(px, py, c), device_id_type=MESH)
                    starts.append(cp)
                    waits.append(cp)
        return starts, waits


def _fused_groups(fused):
    if fused is None:
        return []
    return list(fused) if isinstance(fused, (list, tuple)) else [fused]


def _host_body(body, n_in, n_out, fused, first_last):
    groups = _fused_groups(fused)
    if not groups:
        return body
    n_fin, n_fout = sum(g.n_in for g in groups), sum(g.n for g in groups)
    n_fsem = sum(g.n_scratch for g in groups)

    def wrapped(*refs):
        core_in, f_in = refs[:n_in], refs[n_in:n_in + n_fin]
        core_out = refs[n_in + n_fin:n_in + n_fin + n_out]
        f_out = refs[n_in + n_fin + n_out:n_in + n_fin + n_out + n_fout]
        rest = refs[n_in + n_fin + n_out + n_fout:]
        core_scratch, f_sems = rest[:len(rest) - n_fsem], rest[len(rest) - n_fsem:]
        starts, waits = [], []
        for g in groups:
            s, w = g.copies(f_in[:g.n_in], f_out[:g.n], f_sems[:g.n_scratch])
            f_in, f_out, f_sems = f_in[g.n_in:], f_out[g.n:], f_sems[g.n_scratch:]
            starts += s
            waits += w
        first, last = first_last()

        @pl.when(first)
        def _():
            for cp in starts:
                cp.start()

        body(*core_in, *core_out, *core_scratch)

        @pl.when(last)
        def _():
            for cp in waits:
                cp.wait()

    return wrapped


def _host_call(body, n_in, n_out, fused, first_last, *, name, grid, in_specs, out_specs, out_shape, scratch_shapes,
               sem, operands):
    aliases = {}
    in_specs, out_specs, out_shape, scratch_shapes = list(in_specs), list(out_specs), list(out_shape), list(scratch_shapes)
    fin, fout = n_in, n_out
    for g in _fused_groups(fused):
        aliases.update({fin + fi: fout + fo for fi, fo in g.aliases.items()})
        fin, fout = fin + g.n_in, fout + g.n
        in_specs += g.in_specs
        out_specs += g.out_specs
        out_shape += g.out_shape
        scratch_shapes += g.scratch_shapes
        sem = tuple("arbitrary" for _ in sem)
    res = pl.pallas_call(_host_body(body, n_in, n_out, fused, first_last), name=name, grid=grid, in_specs=in_specs,
                         out_specs=out_specs, out_shape=out_shape, scratch_shapes=scratch_shapes,
                         input_output_aliases=aliases, compiler_params=_params(sem))(*operands)
    return list(res[:n_out]), list(res[n_out:])


def _forward_to_sibling(name, gathered):
    n_arr = len(gathered)

    def body(*refs):
        ins, outs = refs[:n_arr], refs[n_arr:2 * n_arr]
        send_sems, recv_sems = refs[2 * n_arr:]
        x, y, c = _mesh_pos()
        chips = [(1 - x, y), (x, 1 - y), (1 - x, 1 - y)]

        def copy(a, j, pc):
            px, py = chips[j]
            s = 4 * px + 2 * py + pc
            return pltpu.make_async_remote_copy(
                src_ref=ins[a].at[s], dst_ref=outs[a].at[s], send_sem=send_sems.at[a, j], recv_sem=recv_sems.at[a, j],
                device_id=(x, y, 1 - c), device_id_type=MESH)

        for a in range(n_arr):
            for j in range(3):
                copy(a, j, c).start()
        for a in range(n_arr):
            for j in range(3):
                copy(a, j, 1 - c).wait_recv()
                copy(a, j, c).wait_send()

    return pl.pallas_call(
        body, name=name,
        out_shape=[jax.ShapeDtypeStruct(g.shape, g.dtype) for g in gathered],
        in_specs=[HBM_SPEC] * n_arr, out_specs=[HBM_SPEC] * n_arr,
        input_output_aliases={a: a for a in range(n_arr)},
        scratch_shapes=[pltpu.SemaphoreType.DMA((n_arr, 3)), pltpu.SemaphoreType.DMA((n_arr, 3))],
    )(*gathered)


def _exchange_sibling(name, partials):
    n_arr = len(partials)

    def body(*refs):
        ins, outs = refs[:n_arr], refs[n_arr:2 * n_arr]
        send_sems, recv_sems = refs[2 * n_arr:]
        x, y, c = _mesh_pos()
        copies = [pltpu.make_async_remote_copy(
            src_ref=ins[a].at[2 * q + 1 - c], dst_ref=outs[a].at[q], send_sem=send_sems.at[a, q],
            recv_sem=recv_sems.at[a, q], device_id=(x, y, 1 - c), device_id_type=MESH)
            for a in range(n_arr) for q in range(4)]
        for cp in copies:
            cp.start()
        for cp in copies:
            cp.wait_recv()
        for cp in copies:
            cp.wait_send()

    return pl.pallas_call(
        body, name=name,
        out_shape=[jax.ShapeDtypeStruct((4,) + p.shape[1:], p.dtype) for p in partials],
        in_specs=[HBM_SPEC] * n_arr, out_specs=[HBM_SPEC] * n_arr,
        scratch_shapes=[pltpu.SemaphoreType.DMA((n_arr, 4)), pltpu.SemaphoreType.DMA((n_arr, 4))],
    )(*partials)


def _matmul(name, a, b, dims, grid, a_spec, b_spec, o_spec, out_shape, acc_axis=None, fused=None, fused_arrays=()):
    def body(a_ref, b_ref, o_ref):
        r = lax.dot_general(a_ref[...], b_ref[...], dims, preferred_element_type=F32)
        if acc_axis is None:
            o_ref[...] = r.astype(o_ref.dtype)
        else:
            k = pl.program_id(acc_axis)

            @pl.when(k == 0)
            def _():
                o_ref[...] = r

            @pl.when(k > 0)
            def _():
                o_ref[...] += r

    sem = tuple("arbitrary" if i == acc_axis else "parallel" for i in range(len(grid)))
    if fused is None:
        return pl.pallas_call(body, name=name, grid=grid, in_specs=[a_spec, b_spec], out_specs=o_spec,
                              out_shape=out_shape, compiler_params=_params(sem))(a, b)

    def first_last():
        first = last = None
        for ax, n in enumerate(grid):
            f, l = pl.program_id(ax) == 0, pl.program_id(ax) == n - 1
            first, last = (f, l) if first is None else (first & f, last & l)
        return first, last

    (out,), extra = _host_call(body, 2, 1, fused, first_last, name=name, grid=grid, in_specs=[a_spec, b_spec],
                               out_specs=[o_spec], out_shape=[out_shape], scratch_shapes=[], sem=sem,
                               operands=[a, b] + list(fused_arrays))
    return out, extra


def _mm_blocked_rhs(name, a, w_g, tm=MM_TILE, fused=None, fused_arrays=()):
    m, k = a.shape
    nb = w_g.shape[2]
    return _matmul(name, a, w_g, NN, (N_DEV, m // tm),
                   pl.BlockSpec((tm, k), lambda j, i: (i, 0)),
                   pl.BlockSpec((None, k, nb), lambda j, i: (j, 0, 0)),
                   pl.BlockSpec((tm, nb), lambda j, i: (i, j)),
                   jax.ShapeDtypeStruct((m, N_DEV * nb), F32), fused=fused, fused_arrays=fused_arrays)


def _mm_blocked_rhs_t(name, a, w_g, tm=MM_TILE, fused=None, fused_arrays=()):
    m = a.shape[0]
    n, nb = w_g.shape[1], w_g.shape[2]
    return _matmul(name, a, w_g, NT, (m // tm, N_DEV),
                   pl.BlockSpec((tm, nb), lambda i, j: (i, j)),
                   pl.BlockSpec((None, n, nb), lambda i, j: (j, 0, 0)),
                   pl.BlockSpec((tm, n), lambda i, j: (i, 0)),
                   jax.ShapeDtypeStruct((m, n), F32), acc_axis=1, fused=fused, fused_arrays=fused_arrays)


def _mm_wgrad_blocked(name, act, dcols, tk=MM_TILE, fused=None, fused_arrays=()):
    t, k = act.shape
    nb = dcols.shape[1] // N_DEV
    return _matmul(name, act, dcols, TN, (N_DEV, k // tk),
                   pl.BlockSpec((t, tk), lambda j, i: (0, i)),
                   pl.BlockSpec((t, nb), lambda j, i: (0, j)),
                   pl.BlockSpec((None, tk, nb), lambda j, i: (j, i, 0)),
                   jax.ShapeDtypeStruct((N_DEV, k, nb), BF16), fused=fused, fused_arrays=fused_arrays)


def _halves_specs(block, index):
    half = N_DEV // 2
    return (pl.BlockSpec(block, lambda i, j: index(i, jnp.minimum(j, half - 1))),
            pl.BlockSpec(block, lambda i, j: index(i, jnp.maximum(j - half, 0))))


def _mm_halves_rhs_t(name, a_lo, a_hi, w_g, tm=MM_TILE):
    m = a_lo.shape[0]
    n, nb = w_g.shape[1], w_g.shape[2]

    def body(lo_ref, hi_ref, b_ref, o_ref):
        j = pl.program_id(1)

        def accumulate(a_ref):
            r = lax.dot_general(a_ref[...], b_ref[...], NT, preferred_element_type=F32)

            @pl.when(j == 0)
            def _():
                o_ref[...] = r

            @pl.when(j > 0)
            def _():
                o_ref[...] += r

        pl.when(j < N_DEV // 2)(lambda: accumulate(lo_ref))
        pl.when(j >= N_DEV // 2)(lambda: accumulate(hi_ref))

    lo_spec, hi_spec = _halves_specs((tm, nb), lambda i, j: (i, j))
    return pl.pallas_call(
        body, name=name, grid=(m // tm, N_DEV),
        in_specs=[lo_spec, hi_spec, pl.BlockSpec((None, n, nb), lambda i, j: (j, 0, 0))],
        out_specs=pl.BlockSpec((tm, n), lambda i, j: (i, 0)), out_shape=jax.ShapeDtypeStruct((m, n), F32),
        compiler_params=_params(("parallel", "arbitrary")))(a_lo, a_hi, w_g)


def _mm_halves_wgrad(name, act, d_lo, d_hi, tk=MM_TILE):
    t, k = act.shape
    nb = d_lo.shape[1] // (N_DEV // 2)

    def body(a_ref, lo_ref, hi_ref, o_ref):
        j = pl.program_id(0)

        def product(d_ref):
            o_ref[...] = lax.dot_general(a_ref[...], d_ref[...], TN, preferred_element_type=F32).astype(o_ref.dtype)

        pl.when(j < N_DEV // 2)(lambda: product(lo_ref))
        pl.when(j >= N_DEV // 2)(lambda: product(hi_ref))

    half = N_DEV // 2
    return pl.pallas_call(
        body, name=name, grid=(N_DEV, k // tk),
        in_specs=[pl.BlockSpec((t, tk), lambda j, i: (0, i)),
                  pl.BlockSpec((t, nb), lambda j, i: (0, jnp.minimum(j, half - 1))),
                  pl.BlockSpec((t, nb), lambda j, i: (0, jnp.maximum(j - half, 0)))],
        out_specs=pl.BlockSpec((None, tk, nb), lambda j, i: (j, i, 0)),
        out_shape=jax.ShapeDtypeStruct((N_DEV, k, nb), BF16),
        compiler_params=_params(("parallel", "parallel")))(act, d_lo, d_hi)


def _mm_plain(name, a, b, dims, tm, tn, out_dtype, fused=None, fused_arrays=()):
    if dims == NN:
        (m, k), n = a.shape, b.shape[1]
        a_spec = pl.BlockSpec((tm, k), lambda i, j: (i, 0))
        b_spec = pl.BlockSpec((k, tn), lambda i, j: (0, j))
    elif dims == NT:
        (m, k), n = a.shape, b.shape[0]
        a_spec = pl.BlockSpec((tm, k), lambda i, j: (i, 0))
        b_spec = pl.BlockSpec((tn, k), lambda i, j: (j, 0))
    else:
        (k, m), n = a.shape, b.shape[1]
        a_spec = pl.BlockSpec((k, tm), lambda i, j: (0, i))
        b_spec = pl.BlockSpec((k, tn), lambda i, j: (0, j))
    return _matmul(name, a, b, dims, (m // tm, n // tn), a_spec, b_spec,
                   pl.BlockSpec((tm, tn), lambda i, j: (i, j)), jax.ShapeDtypeStruct((m, n), out_dtype),
                   fused=fused, fused_arrays=fused_arrays)


def _ada_fwd(c_all, w_ada_blk, b_blk):
    def body(c_ref, w_ref, b_ref, o_ref):
        cv = c_ref[...]
        o_ref[...] = jnp.dot(cv * _sigmoid(cv), w_ref[...], preferred_element_type=F32) + b_ref[...]

    tn = 512
    return pl.pallas_call(
        body, name="ada_fwd", grid=(ADA_BLK // tn,),
        in_specs=[pl.BlockSpec((N_DEV, D_MODEL), lambda j: (0, 0)),
                  pl.BlockSpec((D_MODEL, tn), lambda j: (0, j)),
                  pl.BlockSpec((1, tn), lambda j: (0, j))],
        out_specs=pl.BlockSpec((N_DEV, tn), lambda j: (0, j)),
        out_shape=jax.ShapeDtypeStruct((N_DEV, ADA_BLK), F32),
        compiler_params=_params(("parallel",)))(c_all, w_ada_blk, b_blk)


def _ada_wgrad(c_all, gmod_cols):
    def body(c_ref, g_ref, o_ref):
        cv = c_ref[...]
        o_ref[...] = lax.dot_general(cv * _sigmoid(cv), g_ref[...], TN, preferred_element_type=F32)

    tk = 512
    return pl.pallas_call(
        body, name="ada_wgrad", grid=(D_MODEL // tk,),
        in_specs=[pl.BlockSpec((N_DEV, tk), lambda i: (0, i)),
                  pl.BlockSpec((N_DEV, ADA_BLK), lambda i: (0, 0))],
        out_specs=pl.BlockSpec((tk, ADA_BLK), lambda i: (i, 0)),
        out_shape=jax.ShapeDtypeStruct((D_MODEL, ADA_BLK), F32),
        compiler_params=_params(("parallel",)))(c_all, gmod_cols)


def _row_spec(cols=D_MODEL):
    return pl.BlockSpec((ROW_TILE, cols), lambda i: (i, 0))


def _vec_spec(cols=D_MODEL):
    return pl.BlockSpec((1, cols), lambda i: (0, 0))


def _norm_fwd(name, x, w, scale, shift, resid=None, gate=None):
    has_res = resid is not None

    def body(*refs):
        if has_res:
            x_ref, r_ref, g_ref, w_ref, sc_ref, sh_ref, xr_ref, h_ref, rs_ref = refs
            xr = x_ref[...] + g_ref[...] * r_ref[...]
            xr_ref[...] = xr
        else:
            x_ref, w_ref, sc_ref, sh_ref, h_ref, rs_ref = refs
            xr = x_ref[...]
        rs = lax.rsqrt(jnp.mean(xr * xr, axis=-1, keepdims=True) + EPS)
        h = (xr * rs) * w_ref[...] * (1.0 + sc_ref[...]) + sh_ref[...]
        h_ref[...] = h.astype(BF16)
        rs_ref[...] = rs

    s = x.shape[0]
    ins = [x] + ([resid, gate] if has_res else []) + [w, scale, shift]
    in_specs = [_row_spec()] + ([_row_spec(), _vec_spec()] if has_res else []) + [_vec_spec()] * 3
    outs = ([jax.ShapeDtypeStruct((s, D_MODEL), F32)] if has_res else []) + [
        jax.ShapeDtypeStruct((s, D_MODEL), BF16), jax.ShapeDtypeStruct((s, 1), F32)]
    out_specs = ([_row_spec()] if has_res else []) + [_row_spec(), pl.BlockSpec((ROW_TILE, 1), lambda i: (i, 0))]
    return pl.pallas_call(body, name=name, grid=(s // ROW_TILE,), in_specs=in_specs, out_specs=out_specs,
                          out_shape=outs, compiler_params=_params(("parallel",)))(*ins)


def _norm_bwd(name, dh, x, rstd, w, scale, dres, mix=None, gate=None, fused=None, fused_arrays=()):
    has_mix = mix is not None

    def body(*refs):
        if has_mix:
            (dh_ref, x_ref, rs_ref, w_ref, sc_ref, dr_ref, mix_ref, g_ref,
             dx_ref, dmix_ref, dsh_ref, dsc_ref, dw_ref, dg_ref) = refs
        else:
            dh_ref, x_ref, rs_ref, w_ref, sc_ref, dr_ref, dx_ref, dsh_ref, dsc_ref, dw_ref = refs
        i = pl.program_id(0)
        dhv = dh_ref[...]
        rs = rs_ref[...]
        xn = x_ref[...] * rs
        wv = w_ref[...]
        one_sc = 1.0 + sc_ref[...]
        dxn = dhv * wv * one_sc
        dx = dr_ref[...] + rs * (dxn - xn * jnp.mean(dxn * xn, axis=-1, keepdims=True))
        dx_ref[...] = dx
        sums = [(dsh_ref, dhv), (dsc_ref, dhv * xn * wv), (dw_ref, dhv * one_sc * xn)]
        if has_mix:
            dmix_ref[...] = (dx * g_ref[...]).astype(BF16)
            sums.append((dg_ref, dx * mix_ref[...]))

        @pl.when(i == 0)
        def _():
            for ref, _v in sums:
                ref[...] = jnp.zeros_like(ref)

        for ref, v in sums:
            ref[...] += jnp.sum(v, axis=0, keepdims=True)

    s = x.shape[0]
    ins = [dh, x, rstd, w, scale, dres] + ([mix, gate] if has_mix else [])
    in_specs = ([_row_spec(), _row_spec(), pl.BlockSpec((ROW_TILE, 1), lambda i: (i, 0)), _vec_spec(), _vec_spec(),
                 _row_spec()] + ([_row_spec(), _vec_spec()] if has_mix else []))
    vec = jax.ShapeDtypeStruct((1, D_MODEL), F32)
    outs = ([jax.ShapeDtypeStruct((s, D_MODEL), F32)] + ([jax.ShapeDtypeStruct((s, D_MODEL), BF16)] if has_mix else [])
            + [vec] * (4 if has_mix else 3))
    out_specs = [_row_spec()] + ([_row_spec()] if has_mix else []) + [_vec_spec()] * (4 if has_mix else 3)

    def first_last():
        i = pl.program_id(0)
        return i == 0, i == s // ROW_TILE - 1

    res, extra = _host_call(body, len(ins), len(outs), fused, first_last, name=name, grid=(s // ROW_TILE,),
                            in_specs=in_specs, out_specs=out_specs, out_shape=outs, scratch_shapes=[],
                            sem=("arbitrary",), operands=ins + list(fused_arrays))
    return res if fused is None else (res, extra)


def _loss_head(x1, ffn, gate2, target):
    def body(x_ref, f_ref, g_ref, t_ref, loss_ref, dout_ref, dffn_ref, dg_ref):
        i = pl.program_id(0)
        fv = f_ref[...]
        gv = g_ref[...]
        err = x_ref[...] + gv * fv - t_ref[...]
        dout = err * (1.0 / D_MODEL)
        dout_ref[...] = dout
        dffn_ref[...] = (dout * gv).astype(BF16)

        @pl.when(i == 0)
        def _():
            loss_ref[...] = jnp.zeros_like(loss_ref)
            dg_ref[...] = jnp.zeros_like(dg_ref)

        row = jnp.sum(err * err, axis=-1, keepdims=True) * (1.0 / D_MODEL)
        loss_ref[...] += jnp.broadcast_to(0.5 * jnp.sum(row, axis=0, keepdims=True), (1, 128))
        dg_ref[...] += jnp.sum(dout * fv, axis=0, keepdims=True)

    s = x1.shape[0]
    return pl.pallas_call(
        body, name="loss_head", grid=(s // ROW_TILE,),
        in_specs=[_row_spec(), _row_spec(), _vec_spec(), _row_spec()],
        out_specs=[pl.BlockSpec((1, 128), lambda i: (0, 0)), _row_spec(), _row_spec(), _vec_spec()],
        out_shape=[jax.ShapeDtypeStruct((1, 128), F32), jax.ShapeDtypeStruct((s, D_MODEL), F32),
                   jax.ShapeDtypeStruct((s, D_MODEL), BF16), jax.ShapeDtypeStruct((1, D_MODEL), F32)],
        compiler_params=_params(("arbitrary",)))(x1, ffn, gate2, target)


CONV_TILE = 512
N_CONV_TILES = D_FF // CONV_TILE


def _shift_rows(a, k, row):
    n = a.shape[0]
    if k > 0:
        return jnp.where(row >= k, pltpu.roll(a, k, 0), 0.0)
    return jnp.where(row < n + k, pltpu.roll(a, n + k, 0), 0.0)


def _conv_gate_fwd(u, conv_w, conv_b, fused=None, fused_arrays=()):
    s = u.shape[0]

    def body(a_ref, g_ref, w_ref, b_ref, y_ref):
        a = a_ref[...]
        w = w_ref[...]
        row = lax.broadcasted_iota(jnp.int32, a.shape, 0)
        ac = b_ref[...] + _shift_rows(a, 2, row) * w[0:1] + _shift_rows(a, 1, row) * w[1:2] + a * w[2:3]
        y_ref[...] = (ac * _sigmoid(ac) * g_ref[...]).astype(BF16)

    def first_last():
        i = pl.program_id(0)
        return i == 0, i == N_CONV_TILES - 1

    col = lambda off: pl.BlockSpec((s, CONV_TILE), lambda i: (0, i + off))
    (y,), extra = _host_call(
        body, 4, 1, fused, first_last, name="conv_gate_fwd", grid=(N_CONV_TILES,),
        in_specs=[col(0), col(N_CONV_TILES), pl.BlockSpec((3, CONV_TILE), lambda i: (0, i)),
                  pl.BlockSpec((1, CONV_TILE), lambda i: (0, i))],
        out_specs=[col(0)], out_shape=[jax.ShapeDtypeStruct((s, D_FF), BF16)], scratch_shapes=[], sem=("parallel",),
        operands=[u, u, conv_w, conv_b] + list(fused_arrays))
    return y if fused is None else (y, extra)


def _conv_gate_bwd(u, dy, conv_w, conv_b):
    s = u.shape[0]

    def body(a_ref, g_ref, dy_ref, w_ref, b_ref, da_ref, dg_ref, gw_ref, gb_ref):
        a = a_ref[...]
        w = w_ref[...]
        row = lax.broadcasted_iota(jnp.int32, a.shape, 0)
        a1 = _shift_rows(a, 1, row)
        a2 = _shift_rows(a, 2, row)
        ac = b_ref[...] + a2 * w[0:1] + a1 * w[1:2] + a * w[2:3]
        sg = _sigmoid(ac)
        dyv = dy_ref[...].astype(F32)
        dg_ref[...] = (dyv * (ac * sg)).astype(BF16)
        dac = dyv * g_ref[...] * _dsilu(ac, sg)
        gb_ref[...] = jnp.sum(dac, axis=0, keepdims=True)
        gw_ref[0:1, :] = jnp.sum(dac * a2, axis=0, keepdims=True)
        gw_ref[1:2, :] = jnp.sum(dac * a1, axis=0, keepdims=True)
        gw_ref[2:3, :] = jnp.sum(dac * a, axis=0, keepdims=True)
        da = dac * w[2:3] + _shift_rows(dac, -1, row) * w[1:2] + _shift_rows(dac, -2, row) * w[0:1]
        da_ref[...] = da.astype(BF16)

    col = lambda off: pl.BlockSpec((s, CONV_TILE), lambda i: (0, i + off))
    return pl.pallas_call(
        body, name="conv_gate_bwd", grid=(N_CONV_TILES,),
        in_specs=[col(0), col(N_CONV_TILES), col(0), pl.BlockSpec((3, CONV_TILE), lambda i: (0, i)),
                  pl.BlockSpec((1, CONV_TILE), lambda i: (0, i))],
        out_specs=[col(0), col(0), pl.BlockSpec((3, CONV_TILE), lambda i: (0, i)),
                   pl.BlockSpec((1, CONV_TILE), lambda i: (0, i))],
        out_shape=[jax.ShapeDtypeStruct((s, D_FF), BF16), jax.ShapeDtypeStruct((s, D_FF), BF16),
                   jax.ShapeDtypeStruct((3, D_FF), F32), jax.ShapeDtypeStruct((1, D_FF), F32)],
        compiler_params=_params(("parallel",)))(u, u, dy, conv_w, conv_b)


HG_TILE = 256
CHUNK_UNROLL = 8


def _unrolled_loop(n, body, init):
    def group(i, carry):
        for u in range(CHUNK_UNROLL):
            carry = body(i * CHUNK_UNROLL + u, carry)
        return carry

    return lax.fori_loop(0, n // CHUNK_UNROLL, group, init)


def _head_col(off):
    return pl.BlockSpec((SEQ, HEAD_DIM), lambda h: (0, h + off))


def _hgrn_gates(hq, hf, lb, pos):
    q = hq * _sigmoid(hq)
    sig = _sigmoid(hf)
    f = lb + (1.0 - lb) * sig
    gl = jnp.log(f)
    for sh in (1, 2, 4, 8):
        gl = gl + jnp.where(pos >= sh, pltpu.roll(gl, sh, 0), 0.0)
    return q, sig, f, 1.0 - f, gl


def _lower_bound(lbl):
    return 1.0 / (1.0 + jnp.exp(lbl[1:2, :] - lbl[0:1, :]))


def _head_first_last():
    h = pl.program_id(0)
    return h == 0, h == HEADS - 1


CHUNKS_PER_TILE = HG_TILE // CHUNK


def _chunk_end(x, pos):
    y = jnp.where(pos == CHUNK - 1, x, 0.0)
    for sh in (1, 2, 4, 8):
        y = y + jnp.where(pos < CHUNK - sh, pltpu.roll(y, x.shape[0] - sh, 0), 0.0)
    return y


def _suffix_in_chunk(x, pos):
    for sh in (1, 2, 4, 8):
        x = x + jnp.where(pos < CHUNK - sh, pltpu.roll(x, x.shape[0] - sh, 0), 0.0)
    return x


def _prefix_in_chunk(x, pos):
    for sh in (1, 2, 4, 8):
        x = x + jnp.where(pos >= sh, pltpu.roll(x, sh, 0), 0.0)
    return x


def _pair_decays(f, pos):
    shifted = jnp.where(pos >= 1, f, 0.0)
    e = shifted
    yield 1, e
    for d in range(2, CHUNK):
        shifted = pltpu.roll(shifted, 1, 0)
        e = e * shifted
        yield d, e


def _chunk_rows(cc):
    return slice(cc * CHUNK, (cc + 1) * CHUNK)


def _outer_products(lhs_b, rhs_b, dst, i):
    for cc in range(CHUNKS_PER_TILE):
        dst[i * CHUNKS_PER_TILE + cc] = lax.dot_general(lhs_b[_chunk_rows(cc)], rhs_b[_chunk_rows(cc)], TN,
                                                        preferred_element_type=F32)


def _state_scan(n_chunks, gl_s, u_s, keep, reverse):
    def step(k, st):
        c = n_chunks - 1 - k if reverse else k
        keep[c] = st.astype(BF16)
        gl = gl_s[pl.ds(pl.multiple_of(c * CHUNK, CHUNK), CHUNK), :]
        return st * jnp.exp(gl[CHUNK - 1:CHUNK, :]) + u_s[c]

    _unrolled_loop(n_chunks, step, jnp.zeros((HEAD_DIM, HEAD_DIM), F32))


def _hgrn_fwd(proj, lb_logits, norm_w, fused=None, fused_arrays=()):
    n_tiles = SEQ // HG_TILE
    n_chunks = SEQ // CHUNK
    fused_arrays = list(fused_arrays)

    def body(hq_ref, hf_ref, hi_ref, hg_ref, lbl_ref, nw_ref, aout_ref, opre_ref, qt_s, gl_s, u_s, st_s):
        lb = _lower_bound(lbl_ref[...])
        ones = jnp.ones((HEAD_DIM, HEAD_DIM), BF16)
        pos = lax.broadcasted_iota(jnp.int32, (HG_TILE, HEAD_DIM), 0) % CHUNK

        def tile(i, carry):
            rows = pl.ds(pl.multiple_of(i * HG_TILE, HG_TILE), HG_TILE)
            v = hi_ref[rows, :]
            q, _sig, f, kk, gl = _hgrn_gates(hq_ref[rows, :], hf_ref[rows, :], lb, pos)
            o = _lane_sum(q * kk, ones) * v
            for d, e in _pair_decays(f, pos):
                o = o + _lane_sum(q * pltpu.roll(kk, d, 0) * e, ones) * pltpu.roll(v, d, 0)
            opre_ref[rows, :] = o
            qt_s[rows, :] = q * jnp.exp(gl)
            gl_s[rows, :] = gl
            kt = kk * jnp.exp(_chunk_end(gl, pos) - gl)
            _outer_products(v.astype(BF16), kt.astype(BF16), u_s, i)
            return carry

        lax.fori_loop(0, n_tiles, tile, 0)
        _state_scan(n_chunks, gl_s, u_s, st_s, reverse=False)

        def finish(i, carry):
            rows = pl.ds(pl.multiple_of(i * HG_TILE, HG_TILE), HG_TILE)
            qt_b = qt_s[rows, :].astype(BF16)
            past = [lax.dot_general(qt_b[_chunk_rows(cc)], st_s[i * CHUNKS_PER_TILE + cc], NT,
                                    preferred_element_type=F32) for cc in range(CHUNKS_PER_TILE)]
            o = opre_ref[rows, :] + jnp.concatenate(past, axis=0)
            opre_ref[rows, :] = o
            hg = hg_ref[rows, :]
            rs = lax.rsqrt(jnp.mean(o * o, axis=-1, keepdims=True) + EPS)
            aout_ref[rows, :] = ((o * rs) * nw_ref[...] * (hg * _sigmoid(hg))).astype(BF16)
            return carry

        lax.fori_loop(0, n_tiles, finish, 0)

    return _host_call(
        body, 6, 2, fused, _head_first_last, name="hgrn_fwd", grid=(HEADS,),
        in_specs=[_head_col(0), _head_col(HEADS), _head_col(2 * HEADS), _head_col(3 * HEADS),
                  pl.BlockSpec((2, HEAD_DIM), lambda h: (0, h)), pl.BlockSpec((1, HEAD_DIM), lambda h: (0, 0))],
        out_specs=[_head_col(0), _head_col(0)],
        out_shape=[jax.ShapeDtypeStruct((SEQ, HEADS * HEAD_DIM), BF16), jax.ShapeDtypeStruct((SEQ, HEADS * HEAD_DIM), F32)],
        scratch_shapes=[pltpu.VMEM((SEQ, HEAD_DIM), F32)] * 2 + [pltpu.VMEM((n_chunks, HEAD_DIM, HEAD_DIM), F32),
                                                                 pltpu.VMEM((n_chunks, HEAD_DIM, HEAD_DIM), BF16)],
        sem=("parallel",), operands=[proj, proj, proj, proj, lb_logits, norm_w] + fused_arrays)


def _hgrn_bwd(proj, lb_logits, norm_w, o_pre, d_aout, fused=None, fused_arrays=()):
    n_tiles = SEQ // HG_TILE
    n_chunks = SEQ // CHUNK

    def body(hq_ref, hf_ref, hi_ref, hg_ref, lbl_ref, nw_ref, opre_ref, da_ref,
             dhq_ref, dhf_ref, dhi_ref, dhg_ref, dlog_ref, gnw_ref,
             q_s, k_s, gl_s, do_s, dq_s, dk_s, dv_s, u_s, st_s, rt_s):
        h = pl.program_id(0)
        lb = _lower_bound(lbl_ref[...])
        nw = nw_ref[...]
        ones = jnp.ones((HEAD_DIM, HEAD_DIM), BF16)
        pos = lax.broadcasted_iota(jnp.int32, (HG_TILE, HEAD_DIM), 0) % CHUNK

        @pl.when(h == 0)
        def _():
            gnw_ref[...] = jnp.zeros_like(gnw_ref)

        def tile(i, carry):
            rows = pl.ds(pl.multiple_of(i * HG_TILE, HG_TILE), HG_TILE)
            v = hi_ref[rows, :]
            q, _sig, f, kk, gl = _hgrn_gates(hq_ref[rows, :], hf_ref[rows, :], lb, pos)
            o = opre_ref[rows, :]
            hg = hg_ref[rows, :]
            da = da_ref[rows, :]
            rs = lax.rsqrt(jnp.mean(o * o, axis=-1, keepdims=True) + EPS)
            oh = o * rs
            sg = _sigmoid(hg)
            dnorm = da * (hg * sg)
            dhg_ref[rows, :] = (da * (oh * nw) * _dsilu(hg, sg)).astype(BF16)
            gnw_ref[...] += jnp.sum(dnorm * oh, axis=0, keepdims=True)
            doh = dnorm * nw
            do = rs * (doh - oh * jnp.mean(doh * oh, axis=-1, keepdims=True))

            d_a = _lane_sum(do * v, ones)
            dq = d_a * kk
            dk = d_a * q
            dv = _lane_sum(q * kk, ones) * do
            for d, e in _pair_decays(f, pos):
                ks = pltpu.roll(kk, d, 0)
                a_d = _lane_sum(q * ks * e, ones)
                d_a = _lane_sum(do * pltpu.roll(v, d, 0), ones) * e
                dq = dq + d_a * ks
                dk = dk + pltpu.roll(d_a * q, HG_TILE - d, 0)
                dv = dv + pltpu.roll(a_d * do, HG_TILE - d, 0)
            q_s[rows, :] = q
            k_s[rows, :] = kk
            gl_s[rows, :] = gl
            do_s[rows, :] = do
            dq_s[rows, :] = dq
            dk_s[rows, :] = dk
            dv_s[rows, :] = dv
            kt = kk * jnp.exp(_chunk_end(gl, pos) - gl)
            _outer_products(v.astype(BF16), kt.astype(BF16), u_s, i)
            return carry

        lax.fori_loop(0, n_tiles, tile, 0)
        _state_scan(n_chunks, gl_s, u_s, st_s, reverse=False)

        def reverse_increments(i, carry):
            rows = pl.ds(pl.multiple_of(i * HG_TILE, HG_TILE), HG_TILE)
            qt = q_s[rows, :] * jnp.exp(gl_s[rows, :])
            _outer_products(do_s[rows, :].astype(BF16), qt.astype(BF16), u_s, i)
            return carry

        lax.fori_loop(0, n_tiles, reverse_increments, 0)
        _state_scan(n_chunks, gl_s, u_s, rt_s, reverse=True)

        def finish(i, dlb):
            rows = pl.ds(pl.multiple_of(i * HG_TILE, HG_TILE), HG_TILE)
            q = q_s[rows, :]
            kk = k_s[rows, :]
            gl = gl_s[rows, :]
            gll = _chunk_end(gl, pos)
            ekt = jnp.exp(gll - gl)
            do_b = do_s[rows, :].astype(BF16)
            v_b = hi_ref[rows, :].astype(BF16)
            kt_b = (kk * ekt).astype(BF16)
            dq_far, dk_far, dv_far, across = [], [], [], []
            for cc in range(CHUNKS_PER_TILE):
                st = st_s[i * CHUNKS_PER_TILE + cc]
                rt = rt_s[i * CHUNKS_PER_TILE + cc]
                sl = _chunk_rows(cc)
                dq_far.append(jnp.dot(do_b[sl], st, preferred_element_type=F32))
                dk_far.append(jnp.dot(v_b[sl], rt, preferred_element_type=F32))
                dv_far.append(lax.dot_general(kt_b[sl], rt, NT, preferred_element_type=F32))
                both = jnp.sum(st.astype(F32) * rt.astype(F32), axis=0, keepdims=True)
                across.append(jnp.broadcast_to(both, (CHUNK, HEAD_DIM)))
            dq = dq_s[rows, :] + jnp.concatenate(dq_far, axis=0) * jnp.exp(gl)
            dk_in = dk_s[rows, :]
            dk_out = jnp.concatenate(dk_far, axis=0) * ekt
            dk = dk_in + dk_out
            dv = dv_s[rows, :] + jnp.concatenate(dv_far, axis=0)
            pc = kk * dk_out
            dgl = (_suffix_in_chunk(q * dq - kk * dk_in, pos) + (_prefix_in_chunk(pc, pos) - pc)
                   + jnp.concatenate(across, axis=0) * jnp.exp(gll))
            hf = hf_ref[rows, :]
            sig = _sigmoid(hf)
            f = lb + (1.0 - lb) * sig
            df = dgl / f - dk
            dhf_ref[rows, :] = (df * (1.0 - lb) * sig * (1.0 - sig)).astype(BF16)
            hq = hq_ref[rows, :]
            dhq_ref[rows, :] = (dq * _dsilu(hq, _sigmoid(hq))).astype(BF16)
            dhi_ref[rows, :] = dv.astype(BF16)
            return dlb + jnp.sum(df * (1.0 - sig), axis=0, keepdims=True)

        dlb = lax.fori_loop(0, n_tiles, finish, jnp.zeros((1, HEAD_DIM), F32))
        dl0 = lb * (1.0 - lb) * dlb
        dlog_ref[0:1, :] = dl0
        dlog_ref[1:2, :] = -dl0

    wide = HEADS * HEAD_DIM
    return _host_call(
        body, 8, 6, fused, _head_first_last, name="hgrn_bwd", grid=(HEADS,),
        in_specs=[_head_col(0), _head_col(HEADS), _head_col(2 * HEADS), _head_col(3 * HEADS),
                  pl.BlockSpec((2, HEAD_DIM), lambda h: (0, h)), pl.BlockSpec((1, HEAD_DIM), lambda h: (0, 0)),
                  _head_col(0), _head_col(0)],
        out_specs=[_head_col(0)] * 4 + [pl.BlockSpec((2, HEAD_DIM), lambda h: (0, h)),
                                        pl.BlockSpec((1, HEAD_DIM), lambda h: (0, 0))],
        out_shape=[jax.ShapeDtypeStruct((SEQ, wide), BF16)] * 4 + [jax.ShapeDtypeStruct((2, wide), F32),
                                                                    jax.ShapeDtypeStruct((1, HEAD_DIM), F32)],
        scratch_shapes=[pltpu.VMEM((SEQ, HEAD_DIM), F32)] * 7 + [pltpu.VMEM((n_chunks, HEAD_DIM, HEAD_DIM), F32),
                                                                 pltpu.VMEM((n_chunks, HEAD_DIM, HEAD_DIM), BF16),
                                                                 pltpu.VMEM((n_chunks, HEAD_DIM, HEAD_DIM), BF16)],
        sem=("arbitrary",),
        operands=[proj, proj, proj, proj, lb_logits, norm_w, o_pre, d_aout] + list(fused_arrays))


Q_TILE = 512
ATT_SCALE = HEAD_DIM ** -0.5
ATT_OFF = 4 * HEADS


def _qk_prep(proj, q_w, k_w, fused=None, fused_arrays=()):
    def body(aq_ref, ak_ref, av_ref, qw_ref, kw_ref, qn_ref, kn_ref, v_ref):
        aq = aq_ref[...]
        ak = ak_ref[...]
        qn_ref[...] = (aq * lax.rsqrt(jnp.mean(aq * aq, axis=-1, keepdims=True) + EPS) * qw_ref[...]).astype(BF16)
        kn_ref[...] = (ak * lax.rsqrt(jnp.mean(ak * ak, axis=-1, keepdims=True) + EPS) * kw_ref[...]).astype(BF16)
        v_ref[...] = av_ref[...].astype(BF16)

    wide = HEADS * HEAD_DIM
    vec = pl.BlockSpec((1, HEAD_DIM), lambda h: (0, 0))
    return _host_call(
        body, 5, 3, fused, _head_first_last, name="qk_prep", grid=(HEADS,),
        in_specs=[_head_col(ATT_OFF), _head_col(ATT_OFF + HEADS), _head_col(ATT_OFF + 2 * HEADS), vec, vec],
        out_specs=[_head_col(0)] * 3, out_shape=[jax.ShapeDtypeStruct((SEQ, wide), BF16)] * 3,
        scratch_shapes=[], sem=("parallel",), operands=[proj, proj, proj, q_w, k_w] + list(fused_arrays))


def _alibi_slopes():
    slopes = jnp.exp2(-8.0 * jnp.arange(1, HEADS + 1, dtype=F32) / HEADS)
    return jnp.broadcast_to(slopes[:, None, None], (HEADS, 1, HEAD_DIM))


SLOPE_SPEC = pl.BlockSpec((None, 1, HEAD_DIM), lambda h, i: (h, 0, 0))


N_Q_TILES = SEQ // Q_TILE
K_BLOCK = 512
NOT_ATTENDED = 1e35


def _att_tables():
    o = jnp.arange(N_Q_TILES, dtype=jnp.int32)[:, None, None]
    r = jnp.arange(Q_TILE, dtype=jnp.int32)[None, :, None]
    c = jnp.arange(K_BLOCK, dtype=jnp.int32)[None, None, :]
    dist = o * Q_TILE + r - c
    mult = ((dist <= 128).astype(F32) + (((dist % 4) == 0) & (dist <= 512)).astype(F32)
            + ((dist % 16) == 0).astype(F32))
    valid = (dist >= 0) & (mult > 0)
    return (jnp.where(valid, dist.astype(F32), NOT_ATTENDED),
            jnp.where(valid, jnp.log(jnp.maximum(mult, 1.0)), 0.0))


TABLE_SPEC = pl.BlockSpec((N_Q_TILES, Q_TILE, K_BLOCK), lambda h, i: (0, 0, 0))


def _att_block(q, k_ref, j, i, slope, dist_ref, lmul_ref):
    rows = pl.ds(pl.multiple_of(j * K_BLOCK, K_BLOCK), K_BLOCK)
    off = i - j * (K_BLOCK // Q_TILE)
    s = lax.dot_general(q, k_ref[rows, :], NT, preferred_element_type=F32) * ATT_SCALE
    return s - slope * dist_ref[off] + lmul_ref[off], rows


def _n_key_blocks(i):
    return (i + K_BLOCK // Q_TILE) // (K_BLOCK // Q_TILE)


def _att_first_last():
    h, i = pl.program_id(0), pl.program_id(1)
    return (h == 0) & (i == 0), (h == HEADS - 1) & (i == N_Q_TILES - 1)


def _attn_fwd(qn, kn, vb, fused=None, fused_arrays=()):
    def body(q_ref, k_ref, v_ref, sl_ref, dist_ref, lmul_ref, o_ref, lse_ref):
        i = pl.program_id(1)
        q = q_ref[...]
        slope = sl_ref[0:1, 0:1]

        def step(j, carry):
            m, l, acc = carry
            sb, rows = _att_block(q, k_ref, j, i, slope, dist_ref, lmul_ref)
            m_new = jnp.maximum(m, jnp.max(sb, axis=-1, keepdims=True))
            alpha = jnp.exp(m - m_new)
            p = jnp.exp(sb - m_new)
            l = alpha * l + jnp.sum(p, axis=-1, keepdims=True)
            acc = alpha * acc + jnp.dot(p.astype(BF16), v_ref[rows, :], preferred_element_type=F32)
            return m_new, l, acc

        m, l, acc = lax.fori_loop(0, _n_key_blocks(i), step,
                                  (jnp.full((Q_TILE, 1), -1e30, F32), jnp.zeros((Q_TILE, 1), F32),
                                   jnp.zeros((Q_TILE, HEAD_DIM), F32)))
        o_ref[...] = acc / l
        lse_ref[...] = m + jnp.log(l)

    wide = HEADS * HEAD_DIM
    qt = pl.BlockSpec((Q_TILE, HEAD_DIM), lambda h, i: (i, h))
    full = pl.BlockSpec((SEQ, HEAD_DIM), lambda h, i: (0, h))
    return _host_call(
        body, 6, 2, fused, _att_first_last, name="attn_fwd", grid=(HEADS, N_Q_TILES),
        in_specs=[qt, full, full, SLOPE_SPEC, TABLE_SPEC, TABLE_SPEC],
        out_specs=[qt, pl.BlockSpec((None, Q_TILE, 1), lambda h, i: (h, i, 0))],
        out_shape=[jax.ShapeDtypeStruct((SEQ, wide), F32), jax.ShapeDtypeStruct((HEADS, SEQ, 1), F32)],
        scratch_shapes=[], sem=("parallel", "parallel"),
        operands=[qn, kn, vb, _alibi_slopes(), *_att_tables()] + list(fused_arrays))


def _attn_bwd(qn, kn, vb, o, lse, d_mix, fused=None, fused_arrays=()):
    def body(q_ref, k_ref, v_ref, o_ref, lse_ref, do_ref, sl_ref, dist_ref, lmul_ref, dq_ref, dk_ref, dv_ref):
        i = pl.program_id(1)
        q = q_ref[...]
        do = do_ref[...]
        do_b = do.astype(BF16)
        slope = sl_ref[0:1, 0:1]
        lse = lse_ref[...]
        delta = jnp.sum(do * o_ref[...], axis=-1, keepdims=True)

        @pl.when(i == 0)
        def _():
            dk_ref[...] = jnp.zeros_like(dk_ref)
            dv_ref[...] = jnp.zeros_like(dv_ref)

        def step(j, dq):
            sb, rows = _att_block(q, k_ref, j, i, slope, dist_ref, lmul_ref)
            p = jnp.exp(sb - lse)
            dp = lax.dot_general(do_b, v_ref[rows, :], NT, preferred_element_type=F32)
            ds = (p * (dp - delta)).astype(BF16)
            dk_ref[rows, :] += lax.dot_general(ds, q, TN, preferred_element_type=F32) * ATT_SCALE
            dv_ref[rows, :] += lax.dot_general(p.astype(BF16), do_b, TN, preferred_element_type=F32)
            return dq + jnp.dot(ds, k_ref[rows, :], preferred_element_type=F32)

        dq = lax.fori_loop(0, _n_key_blocks(i), step, jnp.zeros((Q_TILE, HEAD_DIM), F32))
        dq_ref[...] = dq * ATT_SCALE

    wide = HEADS * HEAD_DIM
    qt = pl.BlockSpec((Q_TILE, HEAD_DIM), lambda h, i: (i, h))
    full = pl.BlockSpec((SEQ, HEAD_DIM), lambda h, i: (0, h))
    return _host_call(
        body, 9, 3, fused, _att_first_last, name="attn_bwd", grid=(HEADS, N_Q_TILES),
        in_specs=[qt, full, full, qt, pl.BlockSpec((None, Q_TILE, 1), lambda h, i: (h, i, 0)),
                  pl.BlockSpec((Q_TILE, HEAD_DIM), lambda h, i: (i, h + HEADS)), SLOPE_SPEC, TABLE_SPEC, TABLE_SPEC],
        out_specs=[qt, full, full], out_shape=[jax.ShapeDtypeStruct((SEQ, wide), F32)] * 3,
        scratch_shapes=[], sem=("parallel", "arbitrary"),
        operands=[qn, kn, vb, o, lse, d_mix, _alibi_slopes(), *_att_tables()] + list(fused_arrays))


def _qk_bwd(proj, q_w, k_w, dqn, dkn, dv):
    def body(aq_ref, ak_ref, qw_ref, kw_ref, dqn_ref, dkn_ref, dv_ref, daq_ref, dak_ref, dav_ref, gq_ref, gk_ref):
        h = pl.program_id(0)

        @pl.when(h == 0)
        def _():
            gq_ref[...] = jnp.zeros_like(gq_ref)
            gk_ref[...] = jnp.zeros_like(gk_ref)

        def one(a_ref, w_ref, d_ref, da_ref, g_ref):
            a = a_ref[...]
            d = d_ref[...]
            rs = lax.rsqrt(jnp.mean(a * a, axis=-1, keepdims=True) + EPS)
            ah = a * rs
            g_ref[...] += jnp.sum(d * ah, axis=0, keepdims=True)
            dah = d * w_ref[...]
            da_ref[...] = (rs * (dah - ah * jnp.mean(dah * ah, axis=-1, keepdims=True))).astype(BF16)

        one(aq_ref, qw_ref, dqn_ref, daq_ref, gq_ref)
        one(ak_ref, kw_ref, dkn_ref, dak_ref, gk_ref)
        dav_ref[...] = dv_ref[...].astype(BF16)

    wide = HEADS * HEAD_DIM
    vec = pl.BlockSpec((1, HEAD_DIM), lambda h: (0, 0))
    return pl.pallas_call(
        body, name="qk_bwd", grid=(HEADS,),
        in_specs=[_head_col(ATT_OFF), _head_col(ATT_OFF + HEADS), vec, vec, _head_col(0), _head_col(0), _head_col(0)],
        out_specs=[_head_col(0)] * 3 + [vec, vec],
        out_shape=[jax.ShapeDtypeStruct((SEQ, wide), BF16)] * 3 + [jax.ShapeDtypeStruct((1, HEAD_DIM), F32)] * 2,
        compiler_params=_params(("arbitrary",)))(proj, proj, q_w, k_w, dqn, dkn, dv)


def _pair_sum(name, partial, theirs, core):
    _, r, c = theirs.shape
    tr = r // 2 if r % 16 == 0 else r

    def body(core_ref, a_ref, b_ref, o_ref):
        o_ref[...] = (a_ref[...].astype(F32) + b_ref[...].astype(F32)).astype(BF16)

    spec = pl.BlockSpec((None, tr, c), lambda q, i, core_ref: (q, i, 0))
    grid_spec = pltpu.PrefetchScalarGridSpec(
        num_scalar_prefetch=1, grid=(4, r // tr),
        in_specs=[pl.BlockSpec((None, tr, c), lambda q, i, core_ref: (2 * q + core_ref[0], i, 0)), spec],
        out_specs=spec)
    return pl.pallas_call(body, name=name, grid_spec=grid_spec, out_shape=jax.ShapeDtypeStruct(theirs.shape, BF16),
                          compiler_params=_params(("parallel", "parallel")))(core, partial, theirs)


def _adamw_step(w, m, v, g):
    nm = ADAM_B1 * m + (1.0 - ADAM_B1) * g
    nv = ADAM_B2 * v + (1.0 - ADAM_B2) * (g * g)
    m_hat = nm / (1.0 - ADAM_B1 ** ADAM_STEP)
    v_hat = nv / (1.0 - ADAM_B2 ** ADAM_STEP)
    return -ADAM_LR * (m_hat / (jnp.sqrt(v_hat) + ADAM_EPS) + ADAM_WD * w), nm, nv


def _adamw(name, w, m, v, addends, tr=None):
    r, c = w.shape
    tr = r if tr is None else tr
    n_add = len(addends)

    def body(*refs):
        w_ref, m_ref, v_ref = refs[:3]
        add_refs = refs[3:3 + n_add]
        g_ref, d_ref, nm_ref, nv_ref = refs[3 + n_add:]
        g = add_refs[0][...].astype(F32)
        for a_ref in add_refs[1:]:
            g = g + a_ref[...].astype(F32)
        g_ref[...] = g
        d_ref[...], nm_ref[...], nv_ref[...] = _adamw_step(w_ref[...], m_ref[...], v_ref[...], g)

    spec = pl.BlockSpec((tr, c), lambda i: (i, 0))
    out = jax.ShapeDtypeStruct((r, c), F32)
    return pl.pallas_call(body, name=name, grid=(r // tr,), in_specs=[spec] * (3 + n_add), out_specs=[spec] * 4,
                          out_shape=[out] * 4, compiler_params=_params(("parallel",)))(w, m, v, *addends)


def _adamw_reduced(name, w, m, v, chip_sums, received, chip, tr):
    r, c = w.shape

    def body(chip_ref, w_ref, m_ref, v_ref, own_ref, r0_ref, r1_ref, r2_ref, g_ref, d_ref, nm_ref, nv_ref):
        g = ((own_ref[...].astype(F32) + r0_ref[...].astype(F32)) + r1_ref[...].astype(F32)) + r2_ref[...].astype(F32)
        g_ref[...] = g
        d_ref[...], nm_ref[...], nv_ref[...] = _adamw_step(w_ref[...], m_ref[...], v_ref[...], g)

    spec = pl.BlockSpec((tr, c), lambda i, chip_ref: (i, 0))

    def slot(k):
        return pl.BlockSpec((None, tr, c), lambda i, chip_ref: (k, i, 0))

    grid_spec = pltpu.PrefetchScalarGridSpec(
        num_scalar_prefetch=1, grid=(r // tr,),
        in_specs=[spec, spec, spec, pl.BlockSpec((None, tr, c), lambda i, chip_ref: (chip_ref[0], i, 0)),
                  slot(0), slot(1), slot(2)],
        out_specs=[spec] * 4)
    out = jax.ShapeDtypeStruct((r, c), F32)
    return pl.pallas_call(body, name=name, grid_spec=grid_spec, out_shape=[out] * 4,
                          compiler_params=_params(("parallel",)))(chip, w, m, v, chip_sums, received, received, received)


def _sum_devices(gathered):
    _, r, c = gathered.shape

    def body(g_ref, o_ref):
        acc = g_ref[0]
        for d in range(1, N_DEV):
            acc = acc + g_ref[d]
        o_ref[...] = acc

    return pl.pallas_call(body, name="sum_devices", out_shape=jax.ShapeDtypeStruct((r, c), F32))(gathered)


def _pack_rows(vectors, rows):
    flat = jnp.concatenate([v.reshape(-1) for v in vectors])
    return jnp.pad(flat, (0, rows * 128 - flat.shape[0])).reshape(rows, 128)


def _unpack(flat, shapes):
    out, off = [], 0
    for shp in shapes:
        n = 1
        for d in shp:
            n *= d
        out.append(flat[off:off + n].reshape(shp))
        off += n
    return out


def _device_step(xs, tgt, mod, norm1_w, norm2_w, lb_logits, hg_norm_w, q_norm_w, k_norm_w, conv_w_full, conv_b,
                 win_g, w_out_x, w_up_x, w_down_x, core=None):
    fused = core is not None
    shift1, scale1, gate1, shift2, scale2, gate2 = (mod[k] for k in range(6))

    h, rstd1 = _norm_fwd("norm1_fwd", xs, norm1_w, scale1, shift1)
    if fused:
        near = (0, 1, 2)
        head_rows, tail_rows = (0, UP_HEAD_ROWS), (UP_HEAD_ROWS, D_MODEL - UP_HEAD_ROWS)
        proj, (wout_g, wup_g) = _mm_blocked_rhs(
            "mm_in", h, win_g, fused_arrays=[w_out_x, w_up_x],
            fused=[_FusedCopies("gather", [w_out_x]), _FusedCopies("gather", [w_up_x], peers=near, rows=head_rows)])
        (a_out, o_pre), (wup_g,) = _hgrn_fwd(
            proj, lb_logits, hg_norm_w, fused_arrays=[w_up_x, wup_g],
            fused=_FusedCopies("gather_more", [w_up_x, wup_g], peers=near, rows=tail_rows, relay_rows=head_rows))
        wout_g, = _forward_to_sibling("allgather_stage2_out", [wout_g])
        wout_full = wout_g.reshape(D_MODEL, D_MODEL)
        (qn, kn, vb), _ = _qk_prep(proj, q_norm_w, k_norm_w)
        (att_o, lse), (wup_g,) = _attn_fwd(qn, kn, vb, _FusedCopies("relay", [wup_g], rows=tail_rows), [wup_g])
    else:
        proj = _mm_blocked_rhs("mm_in", h, win_g)
        (a_out, o_pre), _ = _hgrn_fwd(proj, lb_logits, hg_norm_w)
        wup_g, wout_full, wdown_full = w_up_x, w_out_x, w_down_x
        (qn, kn, vb), _ = _qk_prep(proj, q_norm_w, k_norm_w)
        (att_o, lse), _ = _attn_fwd(qn, kn, vb)
    mixin = jnp.concatenate([a_out, att_o.astype(BF16)], axis=1)
    if fused:
        mix, (wup_g,) = _mm_plain("mm_out", mixin, wout_full, NN, 512, 1024, F32,
                                  fused=_FusedCopies("forward", [wup_g]), fused_arrays=[wup_g])
    else:
        mix = _mm_plain("mm_out", mixin, wout_full, NN, 512, 1024, F32)
    x1, h2, rstd2 = _norm_fwd("norm2_fwd", xs, norm2_w, scale2, shift2, resid=mix, gate=gate1)
    if fused:
        u, (wdown_g,) = _mm_blocked_rhs("mm_up", h2, wup_g, fused=_FusedCopies("gather", [w_down_x]),
                                        fused_arrays=[w_down_x])
        y, (wdown_g,) = _conv_gate_fwd(u, conv_w_full, conv_b, _FusedCopies("forward", [wdown_g]), [wdown_g])
        wdown_full = wdown_g.reshape(D_FF, D_MODEL)
    else:
        u = _mm_blocked_rhs("mm_up", h2, wup_g)
        y = _conv_gate_fwd(u, conv_w_full, conv_b)
    ffn = _mm_plain("mm_down", y, wdown_full, NN, MM_TILE, 512, F32)
    loss_v, dout, dffn, dgate2 = _loss_head(x1, ffn, gate2, tgt)

    dy = _mm_plain("mm_down_dx", dffn, wdown_full, NT, MM_TILE, UP_BLK, BF16)
    gw_down = _mm_plain("mm_down_dw", y, dffn, TN, UP_BLK, 1024, BF16)
    da, dg, gconv_w, gconv_b = _conv_gate_bwd(u, dy, conv_w_full, conv_b)
    dh2 = _mm_halves_rhs_t("mm_up_dx", da, dg, wup_g)
    gw_up = _mm_halves_wgrad("mm_up_dw", h2, da, dg)
    if fused:
        part_up, part_down = gw_up, gw_down.reshape(N_DEV, FF_BLK, D_MODEL)
        (dx1, dmix, dshift2, dscale2, gnorm2, dgate1), (sib_up,) = _norm_bwd(
            "norm2_bwd", dh2, x1, rstd2, norm2_w, scale2, dout, mix=mix, gate=gate1,
            fused=_FusedCopies("sibling", [part_up]), fused_arrays=[part_up])
    else:
        dx1, dmix, dshift2, dscale2, gnorm2, dgate1 = _norm_bwd(
            "norm2_bwd", dh2, x1, rstd2, norm2_w, scale2, dout, mix=mix, gate=gate1)
    gw_out = _mm_plain("mm_out_dw", mixin, dmix, TN, 512, 1024, BF16)
    if fused:
        part_out = gw_out.reshape(N_DEV, OUT_BLK, D_MODEL)
        dmixin, (sib_out, sib_down) = _mm_plain(
            "mm_out_dx", dmix, wout_full, NT, 512, 1024, F32,
            fused=_FusedCopies("sibling", [part_out, part_down]), fused_arrays=[part_out, part_down])
        cs_up = _pair_sum("grad_pair_sum_up", part_up, sib_up, core)
        cs_out = _pair_sum("grad_pair_sum_out", part_out, sib_out, core)
        cs_down = _pair_sum("grad_pair_sum_down", part_down, sib_down, core)
        (dhq, dhf, dhi, dhg, glog, ghg), (fc_up,) = _hgrn_bwd(
            proj, lb_logits, hg_norm_w, o_pre, dmixin, _FusedCopies("chips", [cs_up]), [cs_up])
        (dqn, dkn, dvv), (fc_down,) = _attn_bwd(qn, kn, vb, att_o, lse, dmixin,
                                                _FusedCopies("chips", [cs_down]), [cs_down])
    else:
        dmixin = _mm_plain("mm_out_dx", dmix, wout_full, NT, 512, 1024, F32)
        (dhq, dhf, dhi, dhg, glog, ghg), _ = _hgrn_bwd(proj, lb_logits, hg_norm_w, o_pre, dmixin)
        (dqn, dkn, dvv), _ = _attn_bwd(qn, kn, vb, att_o, lse, dmixin)
    daq, dak, dav, gqw, gkw = _qk_bwd(proj, q_norm_w, k_norm_w, dqn, dkn, dvv)
    dproj = jnp.concatenate([dhq, dhf, dhi, dhg, daq, dak, dav], axis=1)
    if fused:
        gw_in, (fc_out,) = _mm_wgrad_blocked("mm_in_dw", h, dproj, fused=_FusedCopies("chips", [cs_out]),
                                             fused_arrays=[cs_out])
        from_sibling, = _exchange_sibling("grad_exchange_sibling_b", [gw_in])
        cs_in = _pair_sum("grad_pair_sum_in", gw_in, from_sibling, core)
        dh, (fc_in,) = _mm_blocked_rhs_t("mm_in_dx", dproj, win_g, fused=_FusedCopies("chips", [cs_in]),
                                         fused_arrays=[cs_in])
        large = [(cs_in, fc_in), (cs_out, fc_out), (cs_up, fc_up), (cs_down, fc_down)]
    else:
        gw_in = _mm_wgrad_blocked("mm_in_dw", h, dproj)
        dh = _mm_blocked_rhs_t("mm_in_dx", dproj, win_g)
        large = [gw_in, gw_out, gw_up, gw_down]
    grad_x, dshift1, dscale1, gnorm1 = _norm_bwd("norm1_bwd", dh, xs, rstd1, norm1_w, scale1, dx1)
    gmod = jnp.concatenate([dshift1, dscale1, dgate1, dshift2, dscale2, dgate2], axis=1)
    return (loss_v, grad_x, gmod, gnorm1, gnorm2, glog, ghg, gqw, gkw, gconv_b, gconv_w, *large)


def kernel(x, c, w_ada, b_ada, norm1_w, w_in, lb_logits, hg_norm_w, q_norm_w, k_norm_w, w_out, norm2_w, w_up, conv_w, conv_b, w_down, loss_target, m_w_ada, m_b_ada, m_norm1_w, m_w_in, m_lb_logits, m_hg_norm_w, m_q_norm_w, m_k_norm_w, m_w_out, m_norm2_w, m_w_up, m_conv_w, m_conv_b, m_w_down, v_w_ada, v_b_ada, v_norm1_w, v_w_in, v_lb_logits, v_hg_norm_w, v_q_norm_w, v_k_norm_w, v_w_out, v_norm2_w, v_w_up, v_conv_w, v_conv_b, v_w_down):
    ix, iy, ic = lax.axis_index("x"), lax.axis_index("y"), lax.axis_index("c")
    me = 4 * ix + 2 * iy + ic
    my_chip = 2 * ix + iy

    xs = x[0]
    tgt = loss_target[0]

    win_g, = _allgather_weights([w_in[0].astype(BF16)])

    c_all = _allgather_vmem(c.reshape(8, D_MODEL // 8), "allgather_c").reshape(N_DEV, D_MODEL)
    b_blk = lax.dynamic_slice_in_dim(b_ada, me * ADA_BLK, ADA_BLK, axis=1)
    mod_cols = _ada_fwd(c_all, w_ada[0], b_blk)
    mod_all = _allgather_vmem(mod_cols, "allgather_mod").reshape(N_DEV, N_DEV, ADA_BLK)
    mod = lax.dynamic_index_in_dim(mod_all, me, axis=1, keepdims=False).reshape(6, 1, D_MODEL)

    conv_w_all = _allgather_vmem(_pack_rows([conv_w[0]], 24), "allgather_conv_w").reshape(N_DEV, 24 * 128)
    conv_w_full = conv_w_all[:, :3 * FF_BLK].reshape(N_DEV, 3, FF_BLK).transpose(1, 0, 2).reshape(3, D_FF)

    (loss_v, grad_x, gmod, gnorm1, gnorm2, glog, ghg, gqw, gkw, gconv_b, gconv_w,
     rs_in, rs_out, rs_up, rs_down) = _device_step(
        xs, tgt, mod, norm1_w, norm2_w, lb_logits, hg_norm_w, q_norm_w, k_norm_w, conv_w_full, conv_b,
        win_g, w_out[0].astype(BF16), w_up[0].astype(BF16), w_down[0].astype(BF16),
        core=jnp.reshape(ic, (1,)).astype(jnp.int32))
    loss = lax.psum(loss_v[0, 0], AXES)

    small_shapes = [(1, 6 * D_MODEL), (1, D_MODEL), (1, D_MODEL), (2, HEADS * HEAD_DIM), (1, HEAD_DIM),
                    (1, HEAD_DIM), (1, HEAD_DIM), (1, D_FF), (3, D_FF)]
    small = [gmod, gnorm1, gnorm2, glog, ghg, gqw, gkw, gconv_b, gconv_w]
    n_small = sum(a.size for a in small)
    rows = -(-n_small // 1024) * 8
    gathered = _allgather_vmem(_pack_rows(small, rows), "allgather_small").reshape(N_DEV, rows, 128)
    summed = _sum_devices(gathered).reshape(-1)
    (g_b_ada, g_norm1, g_norm2, g_lb, g_hg, g_q, g_k, g_conv_b, g_conv_w_full) = _unpack(summed, small_shapes)
    g_conv_w = lax.dynamic_slice_in_dim(g_conv_w_full, me * FF_BLK, FF_BLK, axis=1)

    gmod_all = gathered[:, :6 * D_MODEL // 128, :].reshape(N_DEV, 6 * D_MODEL)
    gmod_cols = lax.dynamic_slice_in_dim(gmod_all, me * ADA_BLK, ADA_BLK, axis=1)
    g_w_ada_raw = _ada_wgrad(c_all, gmod_cols)

    chip = jnp.reshape(my_chip, (1,)).astype(jnp.int32)

    def big_update(name, w, m, v, rs, tr):
        chip_sums, received = rs
        return _adamw_reduced(name, w[0], m[0], v[0], chip_sums, received, chip, tr)

    r_in = big_update("adamw_w_in", w_in, m_w_in, v_w_in, rs_in, 256)
    r_out = big_update("adamw_w_out", w_out, m_w_out, v_w_out, rs_out, 128)
    r_up = big_update("adamw_w_up", w_up, m_w_up, v_w_up, rs_up, 256)
    r_down = big_update("adamw_w_down", w_down, m_w_down, v_w_down, rs_down, 176)
    r_ada = _adamw("adamw_w_ada", w_ada[0], m_w_ada[0], v_w_ada[0], [g_w_ada_raw], tr=256)
    r_convw = _adamw("adamw_conv_w", conv_w[0], m_conv_w[0], v_conv_w[0], [g_conv_w])

    rep_shapes = [(1, 6 * D_MODEL), (1, D_MODEL), (1, D_MODEL), (2, HEADS * HEAD_DIM), (1, HEAD_DIM),
                  (1, HEAD_DIM), (1, HEAD_DIM), (1, D_FF)]
    rep_rows = -(-sum(a * b for a, b in rep_shapes) // 1024) * 8
    pack = lambda arrs: _pack_rows(arrs, rep_rows)
    rep = _adamw("adamw_small",
                 pack([b_ada, norm1_w, norm2_w, lb_logits, hg_norm_w, q_norm_w, k_norm_w, conv_b]),
                 pack([m_b_ada, m_norm1_w, m_norm2_w, m_lb_logits, m_hg_norm_w, m_q_norm_w, m_k_norm_w, m_conv_b]),
                 pack([v_b_ada, v_norm1_w, v_norm2_w, v_lb_logits, v_hg_norm_w, v_q_norm_w, v_k_norm_w, v_conv_b]),
                 [pack([g_b_ada, g_norm1, g_norm2, g_lb, g_hg, g_q, g_k, g_conv_b])])
    rep = [_unpack(r.reshape(-1), rep_shapes) for r in rep]

    def big(r):
        return [a[None] for a in r]

    order = {"w_ada": big(r_ada), "b_ada": [r[0] for r in rep], "norm1_w": [r[1] for r in rep],
             "w_in": big(r_in), "lb_logits": [r[3] for r in rep], "hg_norm_w": [r[4] for r in rep],
             "q_norm_w": [r[5] for r in rep], "k_norm_w": [r[6] for r in rep], "w_out": big(r_out),
             "norm2_w": [r[2] for r in rep], "w_up": big(r_up), "conv_w": big(r_convw),
             "conv_b": [r[7] for r in rep], "w_down": big(r_down)}
    names = ["w_ada", "b_ada", "norm1_w", "w_in", "lb_logits", "hg_norm_w", "q_norm_w", "k_norm_w", "w_out",
             "norm2_w", "w_up", "conv_w", "conv_b", "w_down"]
    outs = [loss, grad_x[None]]
    for kind in range(4):
        outs += [order[n][kind] for n in names]
    return tuple(outs)
```

```python
import jax
import jax.numpy as jnp
import numpy as np
from jax import lax
from jax.experimental import pallas as pl
from jax.experimental.pallas import tpu as pltpu

F32 = jnp.float32
BF16 = jnp.bfloat16

N_DEV = 8
SEQ = 2048
D_MODEL = 2048
HEADS = 8
HEAD_DIM = 128
IN_COLS = 7168
IN_BLK = IN_COLS // N_DEV
D_FF = 5632
UP_BLK = 2 * D_FF // N_DEV
FF_BLK = D_FF // N_DEV
ADA_BLK = 6 * D_MODEL // N_DEV
OUT_BLK = D_MODEL // N_DEV
EPS = 1e-6
CHUNK = 16
ROW_TILE = 256
MM_TILE = 1024
V7X_VMEM_LIMIT = 56 * 1024 * 1024

ADAM_LR = 0.001
ADAM_B1 = 0.9
ADAM_B2 = 0.999
ADAM_EPS = 1e-08
ADAM_WD = 0.01
ADAM_STEP = 10

NN = (((1,), (0,)), ((), ()))
NT = (((1,), (1,)), ((), ()))
TN = (((0,), (0,)), ((), ()))
MESH = pl.DeviceIdType.MESH


def _params(sem=None, vmem=V7X_VMEM_LIMIT):
    return pltpu.CompilerParams(dimension_semantics=sem, vmem_limit_bytes=vmem)


def _sigmoid(x):
    return 1.0 / (1.0 + jnp.exp(-x))


def _dsilu(x, s):
    return s * (1.0 + x * (1.0 - s))


def _lane_sum(x, ones_bf16):
    return jnp.dot(x.astype(BF16), ones_bf16, preferred_element_type=F32)


def _mesh_pos():
    return lax.axis_index("x"), lax.axis_index("y"), lax.axis_index("c")


def _allgather_vmem(x_blk, name):
    m_per, n = x_blk.shape

    def body(x_ref, out_ref, send_sems, recv_sems, local_sem):
        x, y, c = _mesh_pos()
        me, sibling = (x, y, c), (x, y, 1 - c)
        chips = [(1 - x, y), (x, 1 - y), (1 - x, 1 - y)]

        def rows(px, py, pc):
            return out_ref.at[pl.ds((4 * px + 2 * py + pc) * m_per, m_per), :]

        def copy(k, block, to, src=None):
            return pltpu.make_async_remote_copy(
                src_ref=rows(*block) if src is None else src, dst_ref=rows(*block),
                send_sem=send_sems.at[k], recv_sem=recv_sems.at[k], device_id=to, device_id_type=MESH)

        mine = pltpu.make_async_copy(x_ref, rows(*me), local_sem)
        mine.start()
        first = [copy(0, me, sibling, src=x_ref)]
        first += [copy(1 + j, me, (*chip, c), src=x_ref) for j, chip in enumerate(chips)]
        for cp in first:
            cp.start()
        passed = [copy(4 + j, (*chip, c), sibling) for j, chip in enumerate(chips)]
        for j, chip in enumerate(chips):
            copy(1 + j, (*chip, c), me).wait_recv()
            passed[j].start()
        copy(0, sibling, me).wait_recv()
        for j, chip in enumerate(chips):
            copy(4 + j, (*chip, 1 - c), me).wait_recv()
        for cp in first + passed:
            cp.wait_send()
        mine.wait()

    return pl.pallas_call(
        body, name=name,
        out_shape=jax.ShapeDtypeStruct((N_DEV * m_per, n), x_blk.dtype),
        in_specs=[pl.BlockSpec(memory_space=pltpu.VMEM)],
        out_specs=pl.BlockSpec(memory_space=pltpu.VMEM),
        scratch_shapes=[pltpu.SemaphoreType.DMA((7,)), pltpu.SemaphoreType.DMA((7,)), pltpu.SemaphoreType.DMA],
    )(x_blk)


def _flip(v, bit):
    return v + bit - 2 * v * bit


def _relay_chips(x, y, c):
    return (_flip(x, 1 - c), _flip(y, c)), (_flip(x, c), _flip(y, 1 - c))


UP_HEAD_ROWS = 768
GATHER_PARTS = 4


def _allgather_weights(blocks):
    n_arr = len(blocks)
    parts = GATHER_PARTS

    def body(*refs):
        ins, outs = refs[:n_arr], refs[n_arr:2 * n_arr]
        send_sems, recv_sems, local_sems = refs[2 * n_arr:]
        x, y, c = _mesh_pos()
        me, sibling = (x, y, c), (x, y, 1 - c)
        near = [(1 - x, y), (x, 1 - y)]
        chips = near + [(1 - x, 1 - y)]
        relay_from, relay_to = _relay_chips(x, y, c)

        def rows(a, p):
            hr = ins[a].shape[0] // parts
            return pl.ds(p * hr, hr)

        def slot(a, pos, p):
            return outs[a].at[4 * pos[0] + 2 * pos[1] + pos[2], rows(a, p)]

        def copy(a, k, p, src, lands, to):
            return pltpu.make_async_remote_copy(
                src_ref=src, dst_ref=slot(a, lands, p), send_sem=send_sems.at[a, k, p], recv_sem=recv_sems.at[a, k, p],
                device_id=to, device_id_type=MESH)

        sent = []
        local = [pltpu.make_async_copy(ins[a], outs[a].at[4 * x + 2 * y + c], local_sems.at[a]) for a in range(n_arr)]
        for cp in local:
            cp.start()
        for p in range(parts):
            for a in range(n_arr):
                own = ins[a].at[rows(a, p)]
                sent.append(copy(a, 0, p, own, me, sibling))
                sent += [copy(a, 1 + j, p, own, me, (*chip, c)) for j, chip in enumerate(near)]
        for cp in sent:
            cp.start()

        def start(cp):
            cp.start()
            sent.append(cp)

        for p in range(parts):
            for a in range(n_arr):
                for j, chip in enumerate(near):
                    copy(a, 1 + j, p, ins[a].at[rows(a, p)], (*chip, c), me).wait_recv()
                    start(copy(a, 4 + j, p, slot(a, (*chip, c), p), (*chip, c), sibling))
                start(copy(a, 3, p, slot(a, (*relay_from, c), p), (*relay_from, c), (*relay_to, c)))
        for p in range(parts):
            for a in range(n_arr):
                copy(a, 3, p, ins[a].at[rows(a, p)], (*chips[2], c), me).wait_recv()
                start(copy(a, 6, p, slot(a, (*chips[2], c), p), (*chips[2], c), sibling))
        for p in range(parts):
            for a in range(n_arr):
                copy(a, 0, p, ins[a].at[rows(a, p)], sibling, me).wait_recv()
                for j, chip in enumerate(chips):
                    copy(a, 4 + j, p, ins[a].at[rows(a, p)], (*chip, 1 - c), me).wait_recv()
        for cp in sent:
            cp.wait_send()
        for cp in local:
            cp.wait()

    return pl.pallas_call(
        body, name="allgather_weights",
        out_shape=[jax.ShapeDtypeStruct((N_DEV,) + b.shape, b.dtype) for b in blocks],
        in_specs=[pl.BlockSpec(memory_space=pltpu.HBM)] * n_arr, out_specs=[pl.BlockSpec(memory_space=pltpu.HBM)] * n_arr,
        scratch_shapes=[pltpu.SemaphoreType.DMA((n_arr, 7, parts)), pltpu.SemaphoreType.DMA((n_arr, 7, parts)),
                        pltpu.SemaphoreType.DMA((n_arr,))],
    )(*blocks)


HBM_SPEC = pl.BlockSpec(memory_space=pltpu.HBM)


class _FusedCopies:
    def __init__(self, kind, arrays, peers=(0, 1, 2, 3), rows=None, relay_rows=None):
        self.kind = kind
        self.peers = peers
        self.rows = rows
        self.relay_rows = relay_rows
        n = len(arrays) // 2 if kind == "gather_more" else len(arrays)
        self.n = n
        self.n_in = len(arrays)
        self.aliases = {}
        if kind == "gather":
            self.out_shape = [jax.ShapeDtypeStruct((N_DEV,) + a.shape, a.dtype) for a in arrays]
            self.scratch_shapes = [pltpu.SemaphoreType.DMA((n, 4, GATHER_PARTS)),
                                   pltpu.SemaphoreType.DMA((n, 4, GATHER_PARTS)), pltpu.SemaphoreType.DMA((n,))]
        elif kind == "gather_more":
            self.out_shape = [jax.ShapeDtypeStruct(a.shape, a.dtype) for a in arrays[n:]]
            self.scratch_shapes = [pltpu.SemaphoreType.DMA((n, 5, GATHER_PARTS)),
                                   pltpu.SemaphoreType.DMA((n, 5, GATHER_PARTS)), pltpu.SemaphoreType.DMA((n,))]
            self.aliases = {n + a: a for a in range(n)}
        elif kind == "relay":
            self.out_shape = [jax.ShapeDtypeStruct(a.shape, a.dtype) for a in arrays]
            self.scratch_shapes = [pltpu.SemaphoreType.DMA((n,)), pltpu.SemaphoreType.DMA((n,))]
            self.aliases = {a: a for a in range(n)}
        elif kind == "forward":
            self.out_shape = [jax.ShapeDtypeStruct(a.shape, a.dtype) for a in arrays]
            self.scratch_shapes = [pltpu.SemaphoreType.DMA((n, 3)), pltpu.SemaphoreType.DMA((n, 3))]
            self.aliases = {a: a for a in range(n)}
        elif kind == "sibling":
            self.out_shape = [jax.ShapeDtypeStruct((4,) + a.shape[1:], a.dtype) for a in arrays]
            self.scratch_shapes = [pltpu.SemaphoreType.DMA((n, 4)), pltpu.SemaphoreType.DMA((n, 4))]
        else:
            self.out_shape = [jax.ShapeDtypeStruct((3,) + a.shape[1:], a.dtype) for a in arrays]
            self.scratch_shapes = [pltpu.SemaphoreType.DMA((n, 3)), pltpu.SemaphoreType.DMA((n, 3))]
        self.in_specs = [HBM_SPEC] * self.n_in
        self.out_specs = [HBM_SPEC] * n
        self.n_scratch = len(self.scratch_shapes)

    def copies(self, ins, outs, sems):
        x, y, c = _mesh_pos()
        chips = [(1 - x, y), (x, 1 - y), (1 - x, 1 - y)]
        sibling = (x, y, 1 - c)
        starts, waits = [], []
        relay_from, relay_to = _relay_chips(x, y, c)

        def relayed(a, buf, lands, send_sem, recv_sem, rows):
            first, count = rows or (0, buf.shape[1])
            span = pl.ds(first, count)
            return pltpu.make_async_remote_copy(
                src_ref=buf.at[4 * relay_from[0] + 2 * relay_from[1] + c, span],
                dst_ref=outs[a].at[4 * lands[0] + 2 * lands[1] + c, span], send_sem=send_sem, recv_sem=recv_sem,
                device_id=(*relay_to, c), device_id_type=MESH)

        if self.kind in ("gather", "gather_more"):
            send_sems, recv_sems, local_sems = sems
            me = (x, y, c)
            peers = [sibling] + [(px, py, c) for px, py in chips]

            def slot(a, pos):
                return outs[a].at[4 * pos[0] + 2 * pos[1] + pos[2]]

            def span(a, p=None):
                first, count = self.rows or (0, ins[a].shape[0])
                if p is None:
                    return pl.ds(first, count)
                return pl.ds(first + p * (count // GATHER_PARTS), count // GATHER_PARTS)

            def remote(a, k, p, lands_from):
                return pltpu.make_async_remote_copy(
                    src_ref=ins[a].at[span(a, p)], dst_ref=slot(a, lands_from).at[span(a, p)],
                    send_sem=send_sems.at[a, k, p], recv_sem=recv_sems.at[a, k, p], device_id=peers[k],
                    device_id_type=MESH)

            for a in range(self.n):
                local = pltpu.make_async_copy(ins[a].at[span(a)], slot(a, me).at[span(a)], local_sems.at[a])
                starts.append(local)
                waits.append(local)
            for p in range(GATHER_PARTS):
                for a in range(self.n):
                    for k in self.peers:
                        starts.append(remote(a, k, p, me))
                        waits.append(remote(a, k, p, peers[k]))
            if self.kind == "gather_more" and self.relay_rows is not None:
                for a in range(self.n):
                    buf = ins[self.n + a]
                    starts.append(relayed(a, buf, relay_from, send_sems.at[a, 4, 0], recv_sems.at[a, 4, 0],
                                          self.relay_rows))
                    waits.append(relayed(a, buf, chips[2], send_sems.at[a, 4, 0], recv_sems.at[a, 4, 0],
                                         self.relay_rows))
        elif self.kind == "relay":
            send_sems, recv_sems = sems
            for a in range(self.n):
                starts.append(relayed(a, ins[a], relay_from, send_sems.at[a], recv_sems.at[a], self.rows))
                waits.append(relayed(a, ins[a], chips[2], send_sems.at[a], recv_sems.at[a], self.rows))
        elif self.kind == "forward":
            send_sems, recv_sems = sems

            def passed_on(a, j, pc_src, pc_dst):
                px, py = chips[j]
                return pltpu.make_async_remote_copy(
                    src_ref=ins[a].at[4 * px + 2 * py + pc_src], dst_ref=outs[a].at[4 * px + 2 * py + pc_dst],
                    send_sem=send_sems.at[a, j], recv_sem=recv_sems.at[a, j], device_id=sibling, device_id_type=MESH)

            for a in range(self.n):
                for j in range(3):
                    starts.append(passed_on(a, j, c, c))
                    waits.append(passed_on(a, j, c, 1 - c))
        elif self.kind == "sibling":
            send_sems, recv_sems = sems
            for a in range(self.n):
                for q in range(4):
                    cp = pltpu.make_async_remote_copy(
                        src_ref=ins[a].at[2 * q + 1 - c], dst_ref=outs[a].at[q], send_sem=send_sems.at[a, q],
                        recv_sem=recv_sems.at[a, q], device_id=sibling, device_id_type=MESH)
                    starts.append(cp)
                    waits.append(cp)
        else:
            send_sems, recv_sems = sems
            for a in range(self.n):
                for j, (px, py) in enumerate(chips):
                    cp = pltpu.make_async_remote_copy(
                        src_ref=ins[a].at[2 * px + py], dst_ref=outs[a].at[j], send_sem=send_sems.at[a, j],
                        recv_sem=recv_sems.at[a, j], device_id=(px, py, c), device_id_type=MESH)
                    starts.append(cp)
                    waits.append(cp)
        return starts, waits


def _fused_groups(fused):
    if fused is None:
        return []
    return list(fused) if isinstance(fused, (list, tuple)) else [fused]


def _host_body(body, n_in, n_out, fused, first_last):
    groups = _fused_groups(fused)
    if not groups:
        return body
    n_fin, n_fout = sum(g.n_in for g in groups), sum(g.n for g in groups)
    n_fsem = sum(g.n_scratch for g in groups)

    def wrapped(*refs):
        core_in, f_in = refs[:n_in], refs[n_in:n_in + n_fin]
        core_out = refs[n_in + n_fin:n_in + n_fin + n_out]
        f_out = refs[n_in + n_fin + n_out:n_in + n_fin + n_out + n_fout]
        rest = refs[n_in + n_fin + n_out + n_fout:]
        core_scratch, f_sems = rest[:len(rest) - n_fsem], rest[len(rest) - n_fsem:]
        starts, waits = [], []
        for g in groups:
            s, w = g.copies(f_in[:g.n_in], f_out[:g.n], f_sems[:g.n_scratch])
            f_in, f_out, f_sems = f_in[g.n_in:], f_out[g.n:], f_sems[g.n_scratch:]
            starts += s
            waits += w
        first, last = first_last()

        @pl.when(first)
        def _():
            for cp in starts:
                cp.start()

        body(*core_in, *core_out, *core_scratch)

        @pl.when(last)
        def _():
            for cp in waits:
                cp.wait()

    return wrapped


def _host_call(body, n_in, n_out, fused, first_last, *, name, grid, in_specs, out_specs, out_shape, scratch_shapes,
               sem, operands):
    aliases = {}
    in_specs, out_specs, out_shape, scratch_shapes = list(in_specs), list(out_specs), list(out_shape), list(scratch_shapes)
    fin, fout = n_in, n_out
    for g in _fused_groups(fused):
        aliases.update({fin + fi: fout + fo for fi, fo in g.aliases.items()})
        fin, fout = fin + g.n_in, fout + g.n
        in_specs += g.in_specs
        out_specs += g.out_specs
        out_shape += g.out_shape
        scratch_shapes += g.scratch_shapes
        sem = tuple("arbitrary" for _ in sem)
    res = pl.pallas_call(_host_body(body, n_in, n_out, fused, first_last), name=name, grid=grid, in_specs=in_specs,
                         out_specs=out_specs, out_shape=out_shape, scratch_shapes=scratch_shapes,
                         input_output_aliases=aliases, compiler_params=_params(sem))(*operands)
    return list(res[:n_out]), list(res[n_out:])


def _forward_to_sibling(name, gathered):
    n_arr = len(gathered)

    def body(*refs):
        ins, outs = refs[:n_arr], refs[n_arr:2 * n_arr]
        send_sems, recv_sems = refs[2 * n_arr:]
        x, y, c = _mesh_pos()
        chips = [(1 - x, y), (x, 1 - y), (1 - x, 1 - y)]

        def copy(a, j, pc):
            px, py = chips[j]
            s = 4 * px + 2 * py + pc
            return pltpu.make_async_remote_copy(
                src_ref=ins[a].at[s], dst_ref=outs[a].at[s], send_sem=send_sems.at[a, j], recv_sem=recv_sems.at[a, j],
                device_id=(x, y, 1 - c), device_id_type=MESH)

        for a in range(n_arr):
            for j in range(3):
                copy(a, j, c).start()
        for a in range(n_arr):
            for j in range(3):
                copy(a, j, 1 - c).wait_recv()
                copy(a, j, c).wait_send()

    return pl.pallas_call(
        body, name=name,
        out_shape=[jax.ShapeDtypeStruct(g.shape, g.dtype) for g in gathered],
        in_specs=[HBM_SPEC] * n_arr, out_specs=[HBM_SPEC] * n_arr,
        input_output_aliases={a: a for a in range(n_arr)},
        scratch_shapes=[pltpu.SemaphoreType.DMA((n_arr, 3)), pltpu.SemaphoreType.DMA((n_arr, 3))],
    )(*gathered)


def _exchange_sibling(name, partials):
    n_arr = len(partials)

    def body(*refs):
        ins, outs = refs[:n_arr], refs[n_arr:2 * n_arr]
        send_sems, recv_sems = refs[2 * n_arr:]
        x, y, c = _mesh_pos()
        copies = [pltpu.make_async_remote_copy(
            src_ref=ins[a].at[2 * q + 1 - c], dst_ref=outs[a].at[q], send_sem=send_sems.at[a, q],
            recv_sem=recv_sems.at[a, q], device_id=(x, y, 1 - c), device_id_type=MESH)
            for a in range(n_arr) for q in range(4)]
        for cp in copies:
            cp.start()
        for cp in copies:
            cp.wait_recv()
        for cp in copies:
            cp.wait_send()

    return pl.pallas_call(
        body, name=name,
        out_shape=[jax.ShapeDtypeStruct((4,) + p.shape[1:], p.dtype) for p in partials],
        in_specs=[HBM_SPEC] * n_arr, out_specs=[HBM_SPEC] * n_arr,
        scratch_shapes=[pltpu.SemaphoreType.DMA((n_arr, 4)), pltpu.SemaphoreType.DMA((n_arr, 4))],
    )(*partials)


def _matmul(name, a, b, dims, grid, a_spec, b_spec, o_spec, out_shape, acc_axis=None, fused=None, fused_arrays=()):
    def body(a_ref, b_ref, o_ref):
        r = lax.dot_general(a_ref[...], b_ref[...], dims, preferred_element_type=F32)
        if acc_axis is None:
            o_ref[...] = r.astype(o_ref.dtype)
        else:
            k = pl.program_id(acc_axis)

            @pl.when(k == 0)
            def _():
                o_ref[...] = r

            @pl.when(k > 0)
            def _():
                o_ref[...] += r

    sem = tuple("arbitrary" if i == acc_axis else "parallel" for i in range(len(grid)))
    if fused is None:
        return pl.pallas_call(body, name=name, grid=grid, in_specs=[a_spec, b_spec], out_specs=o_spec,
                              out_shape=out_shape, compiler_params=_params(sem))(a, b)

    def first_last():
        first = last = None
        for ax, n in enumerate(grid):
            f, l = pl.program_id(ax) == 0, pl.program_id(ax) == n - 1
            first, last = (f, l) if first is None else (first & f, last & l)
        return first, last

    (out,), extra = _host_call(body, 2, 1, fused, first_last, name=name, grid=grid, in_specs=[a_spec, b_spec],
                               out_specs=[o_spec], out_shape=[out_shape], scratch_shapes=[], sem=sem,
                               operands=[a, b] + list(fused_arrays))
    return out, extra


def _mm_blocked_rhs(name, a, w_g, tm=MM_TILE, fused=None, fused_arrays=()):
    m, k = a.shape
    nb = w_g.shape[2]
    return _matmul(name, a, w_g, NN, (N_DEV, m // tm),
                   pl.BlockSpec((tm, k), lambda j, i: (i, 0)),
                   pl.BlockSpec((None, k, nb), lambda j, i: (j, 0, 0)),
                   pl.BlockSpec((tm, nb), lambda j, i: (i, j)),
                   jax.ShapeDtypeStruct((m, N_DEV * nb), F32), fused=fused, fused_arrays=fused_arrays)


def _mm_blocked_rhs_t(name, a, w_g, tm=MM_TILE, fused=None, fused_arrays=()):
    m = a.shape[0]
    n, nb = w_g.shape[1], w_g.shape[2]
    return _matmul(name, a, w_g, NT, (m // tm, N_DEV),
                   pl.BlockSpec((tm, nb), lambda i, j: (i, j)),
                   pl.BlockSpec((None, n, nb), lambda i, j: (j, 0, 0)),
                   pl.BlockSpec((tm, n), lambda i, j: (i, 0)),
                   jax.ShapeDtypeStruct((m, n), F32), acc_axis=1, fused=fused, fused_arrays=fused_arrays)


def _mm_wgrad_blocked(name, act, dcols, tk=MM_TILE, fused=None, fused_arrays=()):
    t, k = act.shape
    nb = dcols.shape[1] // N_DEV
    return _matmul(name, act, dcols, TN, (N_DEV, k // tk),
                   pl.BlockSpec((t, tk), lambda j, i: (0, i)),
                   pl.BlockSpec((t, nb), lambda j, i: (0, j)),
                   pl.BlockSpec((None, tk, nb), lambda j, i: (j, i, 0)),
                   jax.ShapeDtypeStruct((N_DEV, k, nb), BF16), fused=fused, fused_arrays=fused_arrays)


def _halves_specs(block, index):
    half = N_DEV // 2
    return (pl.BlockSpec(block, lambda i, j: index(i, jnp.minimum(j, half - 1))),
            pl.BlockSpec(block, lambda i, j: index(i, jnp.maximum(j - half, 0))))


def _mm_halves_rhs_t(name, a_lo, a_hi, w_g, tm=MM_TILE):
    m = a_lo.shape[0]
    n, nb = w_g.shape[1], w_g.shape[2]

    def body(lo_ref, hi_ref, b_ref, o_ref):
        j = pl.program_id(1)

        def accumulate(a_ref):
            r = lax.dot_general(a_ref[...], b_ref[...], NT, preferred_element_type=F32)

            @pl.when(j == 0)
            def _():
                o_ref[...] = r

            @pl.when(j > 0)
            def _():
                o_ref[...] += r

        pl.when(j < N_DEV // 2)(lambda: accumulate(lo_ref))
        pl.when(j >= N_DEV // 2)(lambda: accumulate(hi_ref))

    lo_spec, hi_spec = _halves_specs((tm, nb), lambda i, j: (i, j))
    return pl.pallas_call(
        body, name=name, grid=(m // tm, N_DEV),
        in_specs=[lo_spec, hi_spec, pl.BlockSpec((None, n, nb), lambda i, j: (j, 0, 0))],
        out_specs=pl.BlockSpec((tm, n), lambda i, j: (i, 0)), out_shape=jax.ShapeDtypeStruct((m, n), F32),
        compiler_params=_params(("parallel", "arbitrary")))(a_lo, a_hi, w_g)


def _mm_halves_wgrad(name, act, d_lo, d_hi, tk=MM_TILE):
    t, k = act.shape
    nb = d_lo.shape[1] // (N_DEV // 2)

    def body(a_ref, lo_ref, hi_ref, o_ref):
        j = pl.program_id(0)

        def product(d_ref):
            o_ref[...] = lax.dot_general(a_ref[...], d_ref[...], TN, preferred_element_type=F32).astype(o_ref.dtype)

        pl.when(j < N_DEV // 2)(lambda: product(lo_ref))
        pl.when(j >= N_DEV // 2)(lambda: product(hi_ref))

    half = N_DEV // 2
    return pl.pallas_call(
        body, name=name, grid=(N_DEV, k // tk),
        in_specs=[pl.BlockSpec((t, tk), lambda j, i: (0, i)),
                  pl.BlockSpec((t, nb), lambda j, i: (0, jnp.minimum(j, half - 1))),
                  pl.BlockSpec((t, nb), lambda j, i: (0, jnp.maximum(j - half, 0)))],
        out_specs=pl.BlockSpec((None, tk, nb), lambda j, i: (j, i, 0)),
        out_shape=jax.ShapeDtypeStruct((N_DEV, k, nb), BF16),
        compiler_params=_params(("parallel", "parallel")))(act, d_lo, d_hi)


def _mm_plain(name, a, b, dims, tm, tn, out_dtype, fused=None, fused_arrays=()):
    if dims == NN:
        (m, k), n = a.shape, b.shape[1]
        a_spec = pl.BlockSpec((tm, k), lambda i, j: (i, 0))
        b_spec = pl.BlockSpec((k, tn), lambda i, j: (0, j))
    elif dims == NT:
        (m, k), n = a.shape, b.shape[0]
        a_spec = pl.BlockSpec((tm, k), lambda i, j: (i, 0))
        b_spec = pl.BlockSpec((tn, k), lambda i, j: (j, 0))
    else:
        (k, m), n = a.shape, b.shape[1]
        a_spec = pl.BlockSpec((k, tm), lambda i, j: (0, i))
        b_spec = pl.BlockSpec((k, tn), lambda i, j: (0, j))
    return _matmul(name, a, b, dims, (m // tm, n // tn), a_spec, b_spec,
                   pl.BlockSpec((tm, tn), lambda i, j: (i, j)), jax.ShapeDtypeStruct((m, n), out_dtype),
                   fused=fused, fused_arrays=fused_arrays)


def _ada_fwd(c_all, w_ada_blk, b_blk):
    def body(c_ref, w_ref, b_ref, o_ref):
        cv = c_ref[...]
        o_ref[...] = jnp.dot(cv * _sigmoid(cv), w_ref[...], preferred_element_type=F32) + b_ref[...]

    tn = 512
    return pl.pallas_call(
        body, name="ada_fwd", grid=(ADA_BLK // tn,),
        in_specs=[pl.BlockSpec((N_DEV, D_MODEL), lambda j: (0, 0)),
                  pl.BlockSpec((D_MODEL, tn), lambda j: (0, j)),
                  pl.BlockSpec((1, tn), lambda j: (0, j))],
        out_specs=pl.BlockSpec((N_DEV, tn), lambda j: (0, j)),
        out_shape=jax.ShapeDtypeStruct((N_DEV, ADA_BLK), F32),
        compiler_params=_params(("parallel",)))(c_all, w_ada_blk, b_blk)


def _ada_wgrad(c_all, gmod_cols):
    def body(c_ref, g_ref, o_ref):
        cv = c_ref[...]
        o_ref[...] = lax.dot_general(cv * _sigmoid(cv), g_ref[...], TN, preferred_element_type=F32)

    tk = 512
    return pl.pallas_call(
        body, name="ada_wgrad", grid=(D_MODEL // tk,),
        in_specs=[pl.BlockSpec((N_DEV, tk), lambda i: (0, i)),
                  pl.BlockSpec((N_DEV, ADA_BLK), lambda i: (0, 0))],
        out_specs=pl.BlockSpec((tk, ADA_BLK), lambda i: (i, 0)),
        out_shape=jax.ShapeDtypeStruct((D_MODEL, ADA_BLK), F32),
        compiler_params=_params(("parallel",)))(c_all, gmod_cols)


def _row_spec(cols=D_MODEL):
    return pl.BlockSpec((ROW_TILE, cols), lambda i: (i, 0))


def _vec_spec(cols=D_MODEL):
    return pl.BlockSpec((1, cols), lambda i: (0, 0))


def _norm_fwd(name, x, w, scale, shift, resid=None, gate=None):
    has_res = resid is not None

    def body(*refs):
        if has_res:
            x_ref, r_ref, g_ref, w_ref, sc_ref, sh_ref, xr_ref, h_ref, rs_ref = refs
            xr = x_ref[...] + g_ref[...] * r_ref[...]
            xr_ref[...] = xr
        else:
            x_ref, w_ref, sc_ref, sh_ref, h_ref, rs_ref = refs
            xr = x_ref[...]
        rs = lax.rsqrt(jnp.mean(xr * xr, axis=-1, keepdims=True) + EPS)
        h = (xr * rs) * w_ref[...] * (1.0 + sc_ref[...]) + sh_ref[...]
        h_ref[...] = h.astype(BF16)
        rs_ref[...] = rs

    s = x.shape[0]
    ins = [x] + ([resid, gate] if has_res else []) + [w, scale, shift]
    in_specs = [_row_spec()] + ([_row_spec(), _vec_spec()] if has_res else []) + [_vec_spec()] * 3
    outs = ([jax.ShapeDtypeStruct((s, D_MODEL), F32)] if has_res else []) + [
        jax.ShapeDtypeStruct((s, D_MODEL), BF16), jax.ShapeDtypeStruct((s, 1), F32)]
    out_specs = ([_row_spec()] if has_res else []) + [_row_spec(), pl.BlockSpec((ROW_TILE, 1), lambda i: (i, 0))]
    return pl.pallas_call(body, name=name, grid=(s // ROW_TILE,), in_specs=in_specs, out_specs=out_specs,
                          out_shape=outs, compiler_params=_params(("parallel",)))(*ins)


def _norm_bwd(name, dh, x, rstd, w, scale, dres, mix=None, gate=None, fused=None, fused_arrays=()):
    has_mix = mix is not None

    def body(*refs):
        if has_mix:
            (dh_ref, x_ref, rs_ref, w_ref, sc_ref, dr_ref, mix_ref, g_ref,
             dx_ref, dmix_ref, dsh_ref, dsc_ref, dw_ref, dg_ref) = refs
        else:
            dh_ref, x_ref, rs_ref, w_ref, sc_ref, dr_ref, dx_ref, dsh_ref, dsc_ref, dw_ref = refs
        i = pl.program_id(0)
        dhv = dh_ref[...]
        rs = rs_ref[...]
        xn = x_ref[...] * rs
        wv = w_ref[...]
        one_sc = 1.0 + sc_ref[...]
        dxn = dhv * wv * one_sc
        dx = dr_ref[...] + rs * (dxn - xn * jnp.mean(dxn * xn, axis=-1, keepdims=True))
        dx_ref[...] = dx
        sums = [(dsh_ref, dhv), (dsc_ref, dhv * xn * wv), (dw_ref, dhv * one_sc * xn)]
        if has_mix:
            dmix_ref[...] = (dx * g_ref[...]).astype(BF16)
            sums.append((dg_ref, dx * mix_ref[...]))

        @pl.when(i == 0)
        def _():
            for ref, _v in sums:
                ref[...] = jnp.zeros_like(ref)

        for ref, v in sums:
            ref[...] += jnp.sum(v, axis=0, keepdims=True)

    s = x.shape[0]
    ins = [dh, x, rstd, w, scale, dres] + ([mix, gate] if has_mix else [])
    in_specs = ([_row_spec(), _row_spec(), pl.BlockSpec((ROW_TILE, 1), lambda i: (i, 0)), _vec_spec(), _vec_spec(),
                 _row_spec()] + ([_row_spec(), _vec_spec()] if has_mix else []))
    vec = jax.ShapeDtypeStruct((1, D_MODEL), F32)
    outs = ([jax.ShapeDtypeStruct((s, D_MODEL), F32)] + ([jax.ShapeDtypeStruct((s, D_MODEL), BF16)] if has_mix else [])
            + [vec] * (4 if has_mix else 3))
    out_specs = [_row_spec()] + ([_row_spec()] if has_mix else []) + [_vec_spec()] * (4 if has_mix else 3)

    def first_last():
        i = pl.program_id(0)
        return i == 0, i == s // ROW_TILE - 1

    res, extra = _host_call(body, len(ins), len(outs), fused, first_last, name=name, grid=(s // ROW_TILE,),
                            in_specs=in_specs, out_specs=out_specs, out_shape=outs, scratch_shapes=[],
                            sem=("arbitrary",), operands=ins + list(fused_arrays))
    return res if fused is None else (res, extra)


def _loss_head(x1, ffn, gate2, target):
    def body(x_ref, f_ref, g_ref, t_ref, loss_ref, dout_ref, dffn_ref, dg_ref):
        i = pl.program_id(0)
        fv = f_ref[...]
        gv = g_ref[...]
        err = x_ref[...] + gv * fv - t_ref[...]
        dout = err * (1.0 / D_MODEL)
        dout_ref[...] = dout
        dffn_ref[...] = (dout * gv).astype(BF16)

        @pl.when(i == 0)
        def _():
            loss_ref[...] = jnp.zeros_like(loss_ref)
            dg_ref[...] = jnp.zeros_like(dg_ref)

        row = jnp.sum(err * err, axis=-1, keepdims=True) * (1.0 / D_MODEL)
        loss_ref[...] += jnp.broadcast_to(0.5 * jnp.sum(row, axis=0, keepdims=True), (1, 128))
        dg_ref[...] += jnp.sum(dout * fv, axis=0, keepdims=True)

    s = x1.shape[0]
    return pl.pallas_call(
        body, name="loss_head", grid=(s // ROW_TILE,),
        in_specs=[_row_spec(), _row_spec(), _vec_spec(), _row_spec()],
        out_specs=[pl.BlockSpec((1, 128), lambda i: (0, 0)), _row_spec(), _row_spec(), _vec_spec()],
        out_shape=[jax.ShapeDtypeStruct((1, 128), F32), jax.ShapeDtypeStruct((s, D_MODEL), F32),
                   jax.ShapeDtypeStruct((s, D_MODEL), BF16), jax.ShapeDtypeStruct((1, D_MODEL), F32)],
        compiler_params=_params(("arbitrary",)))(x1, ffn, gate2, target)


CONV_TILE = 512
N_CONV_TILES = D_FF // CONV_TILE


def _shift_rows(a, k, row):
    n = a.shape[0]
    if k > 0:
        return jnp.where(row >= k, pltpu.roll(a, k, 0), 0.0)
    return jnp.where(row < n + k, pltpu.roll(a, n + k, 0), 0.0)


def _conv_gate_fwd(u, conv_w, conv_b, fused=None, fused_arrays=()):
    s = u.shape[0]

    def body(a_ref, g_ref, w_ref, b_ref, y_ref):
        a = a_ref[...]
        w = w_ref[...]
        row = lax.broadcasted_iota(jnp.int32, a.shape, 0)
        ac = b_ref[...] + _shift_rows(a, 2, row) * w[0:1] + _shift_rows(a, 1, row) * w[1:2] + a * w[2:3]
        y_ref[...] = (ac * _sigmoid(ac) * g_ref[...]).astype(BF16)

    def first_last():
        i = pl.program_id(0)
        return i == 0, i == N_CONV_TILES - 1

    col = lambda off: pl.BlockSpec((s, CONV_TILE), lambda i: (0, i + off))
    (y,), extra = _host_call(
        body, 4, 1, fused, first_last, name="conv_gate_fwd", grid=(N_CONV_TILES,),
        in_specs=[col(0), col(N_CONV_TILES), pl.BlockSpec((3, CONV_TILE), lambda i: (0, i)),
                  pl.BlockSpec((1, CONV_TILE), lambda i: (0, i))],
        out_specs=[col(0)], out_shape=[jax.ShapeDtypeStruct((s, D_FF), BF16)], scratch_shapes=[], sem=("parallel",),
        operands=[u, u, conv_w, conv_b] + list(fused_arrays))
    return y if fused is None else (y, extra)


def _conv_gate_bwd(u, dy, conv_w, conv_b):
    s = u.shape[0]

    def body(a_ref, g_ref, dy_ref, w_ref, b_ref, da_ref, dg_ref, gw_ref, gb_ref):
        a = a_ref[...]
        w = w_ref[...]
        row = lax.broadcasted_iota(jnp.int32, a.shape, 0)
        a1 = _shift_rows(a, 1, row)
        a2 = _shift_rows(a, 2, row)
        ac = b_ref[...] + a2 * w[0:1] + a1 * w[1:2] + a * w[2:3]
        sg = _sigmoid(ac)
        dyv = dy_ref[...].astype(F32)
        dg_ref[...] = (dyv * (ac * sg)).astype(BF16)
        dac = dyv * g_ref[...] * _dsilu(ac, sg)
        gb_ref[...] = jnp.sum(dac, axis=0, keepdims=True)
        gw_ref[0:1, :] = jnp.sum(dac * a2, axis=0, keepdims=True)
        gw_ref[1:2, :] = jnp.sum(dac * a1, axis=0, keepdims=True)
        gw_ref[2:3, :] = jnp.sum(dac * a, axis=0, keepdims=True)
        da = dac * w[2:3] + _shift_rows(dac, -1, row) * w[1:2] + _shift_rows(dac, -2, row) * w[0:1]
        da_ref[...] = da.astype(BF16)

    col = lambda off: pl.BlockSpec((s, CONV_TILE), lambda i: (0, i + off))
    return pl.pallas_call(
        body, name="conv_gate_bwd", grid=(N_CONV_TILES,),
        in_specs=[col(0), col(N_CONV_TILES), col(0), pl.BlockSpec((3, CONV_TILE), lambda i: (0, i)),
                  pl.BlockSpec((1, CONV_TILE), lambda i: (0, i))],
        out_specs=[col(0), col(0), pl.BlockSpec((3, CONV_TILE), lambda i: (0, i)),
                   pl.BlockSpec((1, CONV_TILE), lambda i: (0, i))],
        out_shape=[jax.ShapeDtypeStruct((s, D_FF), BF16), jax.ShapeDtypeStruct((s, D_FF), BF16),
                   jax.ShapeDtypeStruct((3, D_FF), F32), jax.ShapeDtypeStruct((1, D_FF), F32)],
        compiler_params=_params(("parallel",)))(u, u, dy, conv_w, conv_b)


HG_TILE = 256
CHUNK_UNROLL = 8


def _unrolled_loop(n, body, init):
    def group(i, carry):
        for u in range(CHUNK_UNROLL):
            carry = body(i * CHUNK_UNROLL + u, carry)
        return carry

    return lax.fori_loop(0, n // CHUNK_UNROLL, group, init)


def _head_col(off):
    return pl.BlockSpec((SEQ, HEAD_DIM), lambda h: (0, h + off))


def _hgrn_gates(hq, hf, lb, pos):
    q = hq * _sigmoid(hq)
    sig = _sigmoid(hf)
    f = lb + (1.0 - lb) * sig
    gl = jnp.log(f)
    for sh in (1, 2, 4, 8):
        gl = gl + jnp.where(pos >= sh, pltpu.roll(gl, sh, 0), 0.0)
    return q, sig, f, 1.0 - f, gl


def _lower_bound(lbl):
    return 1.0 / (1.0 + jnp.exp(lbl[1:2, :] - lbl[0:1, :]))


def _head_first_last():
    h = pl.program_id(0)
    return h == 0, h == HEADS - 1


CHUNKS_PER_TILE = HG_TILE // CHUNK


def _chunk_end(x, pos):
    y = jnp.where(pos == CHUNK - 1, x, 0.0)
    for sh in (1, 2, 4, 8):
        y = y + jnp.where(pos < CHUNK - sh, pltpu.roll(y, x.shape[0] - sh, 0), 0.0)
    return y


def _suffix_in_chunk(x, pos):
    for sh in (1, 2, 4, 8):
        x = x + jnp.where(pos < CHUNK - sh, pltpu.roll(x, x.shape[0] - sh, 0), 0.0)
    return x


def _prefix_in_chunk(x, pos):
    for sh in (1, 2, 4, 8):
        x = x + jnp.where(pos >= sh, pltpu.roll(x, sh, 0), 0.0)
    return x


def _pair_decays(f, pos):
    shifted = jnp.where(pos >= 1, f, 0.0)
    e = shifted
    yield 1, e
    for d in range(2, CHUNK):
        shifted = pltpu.roll(shifted, 1, 0)
        e = e * shifted
        yield d, e


def _chunk_rows(cc):
    return slice(cc * CHUNK, (cc + 1) * CHUNK)


def _outer_products(lhs_b, rhs_b, dst, i):
    for cc in range(CHUNKS_PER_TILE):
        dst[i * CHUNKS_PER_TILE + cc] = lax.dot_general(lhs_b[_chunk_rows(cc)], rhs_b[_chunk_rows(cc)], TN,
                                                        preferred_element_type=F32)


def _state_scan(n_chunks, gl_s, u_s, keep, reverse):
    def step(k, st):
        c = n_chunks - 1 - k if reverse else k
        keep[c] = st.astype(BF16)
        gl = gl_s[pl.ds(pl.multiple_of(c * CHUNK, CHUNK), CHUNK), :]
        return st * jnp.exp(gl[CHUNK - 1:CHUNK, :]) + u_s[c]

    _unrolled_loop(n_chunks, step, jnp.zeros((HEAD_DIM, HEAD_DIM), F32))


def _hgrn_fwd(proj, lb_logits, norm_w, fused=None, fused_arrays=()):
    n_tiles = SEQ // HG_TILE
    n_chunks = SEQ // CHUNK
    fused_arrays = list(fused_arrays)

    def body(hq_ref, hf_ref, hi_ref, hg_ref, lbl_ref, nw_ref, aout_ref, opre_ref, qt_s, gl_s, u_s, st_s):
        lb = _lower_bound(lbl_ref[...])
        ones = jnp.ones((HEAD_DIM, HEAD_DIM), BF16)
        pos = lax.broadcasted_iota(jnp.int32, (HG_TILE, HEAD_DIM), 0) % CHUNK

        def tile(i, carry):
            rows = pl.ds(pl.multiple_of(i * HG_TILE, HG_TILE), HG_TILE)
            v = hi_ref[rows, :]
            q, _sig, f, kk, gl = _hgrn_gates(hq_ref[rows, :], hf_ref[rows, :], lb, pos)
            o = _lane_sum(q * kk, ones) * v
            for d, e in _pair_decays(f, pos):
                o = o + _lane_sum(q * pltpu.roll(kk, d, 0) * e, ones) * pltpu.roll(v, d, 0)
            opre_ref[rows, :] = o
            qt_s[rows, :] = q * jnp.exp(gl)
            gl_s[rows, :] = gl
            kt = kk * jnp.exp(_chunk_end(gl, pos) - gl)
            _outer_products(v.astype(BF16), kt.astype(BF16), u_s, i)
            return carry

        lax.fori_loop(0, n_tiles, tile, 0)
        _state_scan(n_chunks, gl_s, u_s, st_s, reverse=False)

        def finish(i, carry):
            rows = pl.ds(pl.multiple_of(i * HG_TILE, HG_TILE), HG_TILE)
            qt_b = qt_s[rows, :].astype(BF16)
            past = [lax.dot_general(qt_b[_chunk_rows(cc)], st_s[i * CHUNKS_PER_TILE + cc], NT,
                                    preferred_element_type=F32) for cc in range(CHUNKS_PER_TILE)]
            o = opre_ref[rows, :] + jnp.concatenate(past, axis=0)
            opre_ref[rows, :] = o
            hg = hg_ref[rows, :]
            rs = lax.rsqrt(jnp.mean(o * o, axis=-1, keepdims=True) + EPS)
            aout_ref[rows, :] = ((o * rs) * nw_ref[...] * (hg * _sigmoid(hg))).astype(BF16)
            return carry

        lax.fori_loop(0, n_tiles, finish, 0)

    return _host_call(
        body, 6, 2, fused, _head_first_last, name="hgrn_fwd", grid=(HEADS,),
        in_specs=[_head_col(0), _head_col(HEADS), _head_col(2 * HEADS), _head_col(3 * HEADS),
                  pl.BlockSpec((2, HEAD_DIM), lambda h: (0, h)), pl.BlockSpec((1, HEAD_DIM), lambda h: (0, 0))],
        out_specs=[_head_col(0), _head_col(0)],
        out_shape=[jax.ShapeDtypeStruct((SEQ, HEADS * HEAD_DIM), BF16), jax.ShapeDtypeStruct((SEQ, HEADS * HEAD_DIM), F32)],
        scratch_shapes=[pltpu.VMEM((SEQ, HEAD_DIM), F32)] * 2 + [pltpu.VMEM((n_chunks, HEAD_DIM, HEAD_DIM), F32),
                                                                 pltpu.VMEM((n_chunks, HEAD_DIM, HEAD_DIM), BF16)],
        sem=("parallel",), operands=[proj, proj, proj, proj, lb_logits, norm_w] + fused_arrays)


def _hgrn_bwd(proj, lb_logits, norm_w, o_pre, d_aout, fused=None, fused_arrays=()):
    n_tiles = SEQ // HG_TILE
    n_chunks = SEQ // CHUNK

    def body(hq_ref, hf_ref, hi_ref, hg_ref, lbl_ref, nw_ref, opre_ref, da_ref,
             dhq_ref, dhf_ref, dhi_ref, dhg_ref, dlog_ref, gnw_ref,
             q_s, k_s, gl_s, do_s, dq_s, dk_s, dv_s, u_s, st_s, rt_s):
        h = pl.program_id(0)
        lb = _lower_bound(lbl_ref[...])
        nw = nw_ref[...]
        ones = jnp.ones((HEAD_DIM, HEAD_DIM), BF16)
        pos = lax.broadcasted_iota(jnp.int32, (HG_TILE, HEAD_DIM), 0) % CHUNK

        @pl.when(h == 0)
        def _():
            gnw_ref[...] = jnp.zeros_like(gnw_ref)

        def tile(i, carry):
            rows = pl.ds(pl.multiple_of(i * HG_TILE, HG_TILE), HG_TILE)
            v = hi_ref[rows, :]
            q, _sig, f, kk, gl = _hgrn_gates(hq_ref[rows, :], hf_ref[rows, :], lb, pos)
            o = opre_ref[rows, :]
            hg = hg_ref[rows, :]
            da = da_ref[rows, :]
            rs = lax.rsqrt(jnp.mean(o * o, axis=-1, keepdims=True) + EPS)
            oh = o * rs
            sg = _sigmoid(hg)
            dnorm = da * (hg * sg)
            dhg_ref[rows, :] = (da * (oh * nw) * _dsilu(hg, sg)).astype(BF16)
            gnw_ref[...] += jnp.sum(dnorm * oh, axis=0, keepdims=True)
            doh = dnorm * nw
            do = rs * (doh - oh * jnp.mean(doh * oh, axis=-1, keepdims=True))

            d_a = _lane_sum(do * v, ones)
            dq = d_a * kk
            dk = d_a * q
            dv = _lane_sum(q * kk, ones) * do
            for d, e in _pair_decays(f, pos):
                ks = pltpu.roll(kk, d, 0)
                a_d = _lane_sum(q * ks * e, ones)
                d_a = _lane_sum(do * pltpu.roll(v, d, 0), ones) * e
                dq = dq + d_a * ks
                dk = dk + pltpu.roll(d_a * q, HG_TILE - d, 0)
                dv = dv + pltpu.roll(a_d * do, HG_TILE - d, 0)
            q_s[rows, :] = q
            k_s[rows, :] = kk
            gl_s[rows, :] = gl
            do_s[rows, :] = do
            dq_s[rows, :] = dq
            dk_s[rows, :] = dk
            dv_s[rows, :] = dv
            kt = kk * jnp.exp(_chunk_end(gl, pos) - gl)
            _outer_products(v.astype(BF16), kt.astype(BF16), u_s, i)
            return carry

        lax.fori_loop(0, n_tiles, tile, 0)
        _state_scan(n_chunks, gl_s, u_s, st_s, reverse=False)

        def reverse_increments(i, carry):
            rows = pl.ds(pl.multiple_of(i * HG_TILE, HG_TILE), HG_TILE)
            qt = q_s[rows, :] * jnp.exp(gl_s[rows, :])
            _outer_products(do_s[rows, :].astype(BF16), qt.astype(BF16), u_s, i)
            return carry

        lax.fori_loop(0, n_tiles, reverse_increments, 0)
        _state_scan(n_chunks, gl_s, u_s, rt_s, reverse=True)

        def finish(i, dlb):
            rows = pl.ds(pl.multiple_of(i * HG_TILE, HG_TILE), HG_TILE)
            q = q_s[rows, :]
            kk = k_s[rows, :]
            gl = gl_s[rows, :]
            gll = _chunk_end(gl, pos)
            ekt = jnp.exp(gll - gl)
            do_b = do_s[rows, :].astype(BF16)
            v_b = hi_ref[rows, :].astype(BF16)
            kt_b = (kk * ekt).astype(BF16)
            dq_far, dk_far, dv_far, across = [], [], [], []
            for cc in range(CHUNKS_PER_TILE):
                st = st_s[i * CHUNKS_PER_TILE + cc]
                rt = rt_s[i * CHUNKS_PER_TILE + cc]
                sl = _chunk_rows(cc)
                dq_far.append(jnp.dot(do_b[sl], st, preferred_element_type=F32))
                dk_far.append(jnp.dot(v_b[sl], rt, preferred_element_type=F32))
                dv_far.append(lax.dot_general(kt_b[sl], rt, NT, preferred_element_type=F32))
                both = jnp.sum(st.astype(F32) * rt.astype(F32), axis=0, keepdims=True)
                across.append(jnp.broadcast_to(both, (CHUNK, HEAD_DIM)))
            dq = dq_s[rows, :] + jnp.concatenate(dq_far, axis=0) * jnp.exp(gl)
            dk_in = dk_s[rows, :]
            dk_out = jnp.concatenate(dk_far, axis=0) * ekt
            dk = dk_in + dk_out
            dv = dv_s[rows, :] + jnp.concatenate(dv_far, axis=0)
            pc = kk * dk_out
            dgl = (_suffix_in_chunk(q * dq - kk * dk_in, pos) + (_prefix_in_chunk(pc, pos) - pc)
                   + jnp.concatenate(across, axis=0) * jnp.exp(gll))
            hf = hf_ref[rows, :]
            sig = _sigmoid(hf)
            f = lb + (1.0 - lb) * sig
            df = dgl / f - dk
            dhf_ref[rows, :] = (df * (1.0 - lb) * sig * (1.0 - sig)).astype(BF16)
            hq = hq_ref[rows, :]
            dhq_ref[rows, :] = (dq * _dsilu(hq, _sigmoid(hq))).astype(BF16)
            dhi_ref[rows, :] = dv.astype(BF16)
            return dlb + jnp.sum(df * (1.0 - sig), axis=0, keepdims=True)

        dlb = lax.fori_loop(0, n_tiles, finish, jnp.zeros((1, HEAD_DIM), F32))
        dl0 = lb * (1.0 - lb) * dlb
        dlog_ref[0:1, :] = dl0
        dlog_ref[1:2, :] = -dl0

    wide = HEADS * HEAD_DIM
    return _host_call(
        body, 8, 6, fused, _head_first_last, name="hgrn_bwd", grid=(HEADS,),
        in_specs=[_head_col(0), _head_col(HEADS), _head_col(2 * HEADS), _head_col(3 * HEADS),
                  pl.BlockSpec((2, HEAD_DIM), lambda h: (0, h)), pl.BlockSpec((1, HEAD_DIM), lambda h: (0, 0)),
                  _head_col(0), _head_col(0)],
        out_specs=[_head_col(0)] * 4 + [pl.BlockSpec((2, HEAD_DIM), lambda h: (0, h)),
                                        pl.BlockSpec((1, HEAD_DIM), lambda h: (0, 0))],
        out_shape=[jax.ShapeDtypeStruct((SEQ, wide), BF16)] * 4 + [jax.ShapeDtypeStruct((2, wide), F32),
                                                                    jax.ShapeDtypeStruct((1, HEAD_DIM), F32)],
        scratch_shapes=[pltpu.VMEM((SEQ, HEAD_DIM), F32)] * 7 + [pltpu.VMEM((n_chunks, HEAD_DIM, HEAD_DIM), F32),
                                                                 pltpu.VMEM((n_chunks, HEAD_DIM, HEAD_DIM), BF16),
                                                                 pltpu.VMEM((n_chunks, HEAD_DIM, HEAD_DIM), BF16)],
        sem=("arbitrary",),
        operands=[proj, proj, proj, proj, lb_logits, norm_w, o_pre, d_aout] + list(fused_arrays))


Q_TILE = 512
ATT_SCALE = HEAD_DIM ** -0.5
ATT_OFF = 4 * HEADS


def _qk_prep(proj, q_w, k_w, fused=None, fused_arrays=()):
    def body(aq_ref, ak_ref, av_ref, qw_ref, kw_ref, qn_ref, kn_ref, v_ref):
        aq = aq_ref[...]
        ak = ak_ref[...]
        qn_ref[...] = (aq * lax.rsqrt(jnp.mean(aq * aq, axis=-1, keepdims=True) + EPS) * qw_ref[...]).astype(BF16)
        kn_ref[...] = (ak * lax.rsqrt(jnp.mean(ak * ak, axis=-1, keepdims=True) + EPS) * kw_ref[...]).astype(BF16)
        v_ref[...] = av_ref[...].astype(BF16)

    wide = HEADS * HEAD_DIM
    vec = pl.BlockSpec((1, HEAD_DIM), lambda h: (0, 0))
    return _host_call(
        body, 5, 3, fused, _head_first_last, name="qk_prep", grid=(HEADS,),
        in_specs=[_head_col(ATT_OFF), _head_col(ATT_OFF + HEADS), _head_col(ATT_OFF + 2 * HEADS), vec, vec],
        out_specs=[_head_col(0)] * 3, out_shape=[jax.ShapeDtypeStruct((SEQ, wide), BF16)] * 3,
        scratch_shapes=[], sem=("parallel",), operands=[proj, proj, proj, q_w, k_w] + list(fused_arrays))


def _alibi_slopes():
    slopes = np.exp2(-8.0 * np.arange(1, HEADS + 1, dtype=np.float32) / HEADS).astype(np.float32)
    return np.broadcast_to(slopes[:, None, None], (HEADS, 1, HEAD_DIM))


SLOPE_SPEC = pl.BlockSpec((None, 1, HEAD_DIM), lambda h, i: (h, 0, 0))


N_Q_TILES = SEQ // Q_TILE
K_BLOCK = 512
NOT_ATTENDED = 1e35


def _att_tables():
    o = np.arange(N_Q_TILES, dtype=np.int32)[:, None, None]
    r = np.arange(Q_TILE, dtype=np.int32)[None, :, None]
    c = np.arange(K_BLOCK, dtype=np.int32)[None, None, :]
    dist = o * Q_TILE + r - c
    mult = ((dist <= 128).astype(np.float32) + (((dist % 4) == 0) & (dist <= 512)).astype(np.float32)
            + ((dist % 16) == 0).astype(np.float32))
    valid = (dist >= 0) & (mult > 0)
    return (np.where(valid, dist.astype(np.float32), np.float32(NOT_ATTENDED)).astype(np.float32),
            np.where(valid, np.log(np.maximum(mult, 1.0)), 0.0).astype(np.float32))


TABLE_SPEC = pl.BlockSpec((N_Q_TILES, Q_TILE, K_BLOCK), lambda h, i: (0, 0, 0))


def _att_block(q, k_ref, j, i, slope, dist_ref, lmul_ref):
    rows = pl.ds(pl.multiple_of(j * K_BLOCK, K_BLOCK), K_BLOCK)
    off = i - j * (K_BLOCK // Q_TILE)
    s = lax.dot_general(q, k_ref[rows, :], NT, preferred_element_type=F32) * ATT_SCALE
    return s - slope * dist_ref[off] + lmul_ref[off], rows


def _n_key_blocks(i):
    return (i + K_BLOCK // Q_TILE) // (K_BLOCK // Q_TILE)


def _att_first_last():
    h, i = pl.program_id(0), pl.program_id(1)
    return (h == 0) & (i == 0), (h == HEADS - 1) & (i == N_Q_TILES - 1)


def _attn_fwd(qn, kn, vb, fused=None, fused_arrays=()):
    def body(q_ref, k_ref, v_ref, sl_ref, dist_ref, lmul_ref, o_ref, lse_ref):
        i = pl.program_id(1)
        q = q_ref[...]
        slope = sl_ref[0:1, 0:1]

        def step(j, carry):
            m, l, acc = carry
            sb, rows = _att_block(q, k_ref, j, i, slope, dist_ref, lmul_ref)
            m_new = jnp.maximum(m, jnp.max(sb, axis=-1, keepdims=True))
            alpha = jnp.exp(m - m_new)
            p = jnp.exp(sb - m_new)
            l = alpha * l + jnp.sum(p, axis=-1, keepdims=True)
            acc = alpha * acc + jnp.dot(p.astype(BF16), v_ref[rows, :], preferred_element_type=F32)
            return m_new, l, acc

        m, l, acc = lax.fori_loop(0, _n_key_blocks(i), step,
                                  (jnp.full((Q_TILE, 1), -1e30, F32), jnp.zeros((Q_TILE, 1), F32),
                                   jnp.zeros((Q_TILE, HEAD_DIM), F32)))
        o_ref[...] = acc / l
        lse_ref[...] = m + jnp.log(l)

    wide = HEADS * HEAD_DIM
    qt = pl.BlockSpec((Q_TILE, HEAD_DIM), lambda h, i: (i, h))
    full = pl.BlockSpec((SEQ, HEAD_DIM), lambda h, i: (0, h))
    return _host_call(
        body, 6, 2, fused, _att_first_last, name="attn_fwd", grid=(HEADS, N_Q_TILES),
        in_specs=[qt, full, full, SLOPE_SPEC, TABLE_SPEC, TABLE_SPEC],
        out_specs=[qt, pl.BlockSpec((None, Q_TILE, 1), lambda h, i: (h, i, 0))],
        out_shape=[jax.ShapeDtypeStruct((SEQ, wide), F32), jax.ShapeDtypeStruct((HEADS, SEQ, 1), F32)],
        scratch_shapes=[], sem=("parallel", "parallel"),
        operands=[qn, kn, vb, _alibi_slopes(), *_att_tables()] + list(fused_arrays))


def _attn_bwd(qn, kn, vb, o, lse, d_mix, fused=None, fused_arrays=()):
    def body(q_ref, k_ref, v_ref, o_ref, lse_ref, do_ref, sl_ref, dist_ref, lmul_ref, dq_ref, dk_ref, dv_ref):
        i = pl.program_id(1)
        q = q_ref[...]
        do = do_ref[...]
        do_b = do.astype(BF16)
        slope = sl_ref[0:1, 0:1]
        lse = lse_ref[...]
        delta = jnp.sum(do * o_ref[...], axis=-1, keepdims=True)

        @pl.when(i == 0)
        def _():
            dk_ref[...] = jnp.zeros_like(dk_ref)
            dv_ref[...] = jnp.zeros_like(dv_ref)

        def step(j, dq):
            sb, rows = _att_block(q, k_ref, j, i, slope, dist_ref, lmul_ref)
            p = jnp.exp(sb - lse)
            dp = lax.dot_general(do_b, v_ref[rows, :], NT, preferred_element_type=F32)
            ds = (p * (dp - delta)).astype(BF16)
            dk_ref[rows, :] += lax.dot_general(ds, q, TN, preferred_element_type=F32) * ATT_SCALE
            dv_ref[rows, :] += lax.dot_general(p.astype(BF16), do_b, TN, preferred_element_type=F32)
            return dq + jnp.dot(ds, k_ref[rows, :], preferred_element_type=F32)

        dq = lax.fori_loop(0, _n_key_blocks(i), step, jnp.zeros((Q_TILE, HEAD_DIM), F32))
        dq_ref[...] = dq * ATT_SCALE

    wide = HEADS * HEAD_DIM
    qt = pl.BlockSpec((Q_TILE, HEAD_DIM), lambda h, i: (i, h))
    full = pl.BlockSpec((SEQ, HEAD_DIM), lambda h, i: (0, h))
    return _host_call(
        body, 9, 3, fused, _att_first_last, name="attn_bwd", grid=(HEADS, N_Q_TILES),
        in_specs=[qt, full, full, qt, pl.BlockSpec((None, Q_TILE, 1), lambda h, i: (h, i, 0)),
                  pl.BlockSpec((Q_TILE, HEAD_DIM), lambda h, i: (i, h + HEADS)), SLOPE_SPEC, TABLE_SPEC, TABLE_SPEC],
        out_specs=[qt, full, full], out_shape=[jax.ShapeDtypeStruct((SEQ, wide), F32)] * 3,
        scratch_shapes=[], sem=("parallel", "arbitrary"),
        operands=[qn, kn, vb, o, lse, d_mix, _alibi_slopes(), *_att_tables()] + list(fused_arrays))


def _qk_bwd(proj, q_w, k_w, dqn, dkn, dv):
    def body(aq_ref, ak_ref, qw_ref, kw_ref, dqn_ref, dkn_ref, dv_ref, daq_ref, dak_ref, dav_ref, gq_ref, gk_ref):
        h = pl.program_id(0)

        @pl.when(h == 0)
        def _():
            gq_ref[...] = jnp.zeros_like(gq_ref)
            gk_ref[...] = jnp.zeros_like(gk_ref)

        def one(a_ref, w_ref, d_ref, da_ref, g_ref):
            a = a_ref[...]
            d = d_ref[...]
            rs = lax.rsqrt(jnp.mean(a * a, axis=-1, keepdims=True) + EPS)
            ah = a * rs
            g_ref[...] += jnp.sum(d * ah, axis=0, keepdims=True)
            dah = d * w_ref[...]
            da_ref[...] = (rs * (dah - ah * jnp.mean(dah * ah, axis=-1, keepdims=True))).astype(BF16)

        one(aq_ref, qw_ref, dqn_ref, daq_ref, gq_ref)
        one(ak_ref, kw_ref, dkn_ref, dak_ref, gk_ref)
        dav_ref[...] = dv_ref[...].astype(BF16)

    wide = HEADS * HEAD_DIM
    vec = pl.BlockSpec((1, HEAD_DIM), lambda h: (0, 0))
    return pl.pallas_call(
        body, name="qk_bwd", grid=(HEADS,),
        in_specs=[_head_col(ATT_OFF), _head_col(ATT_OFF + HEADS), vec, vec, _head_col(0), _head_col(0), _head_col(0)],
        out_specs=[_head_col(0)] * 3 + [vec, vec],
        out_shape=[jax.ShapeDtypeStruct((SEQ, wide), BF16)] * 3 + [jax.ShapeDtypeStruct((1, HEAD_DIM), F32)] * 2,
        compiler_params=_params(("arbitrary",)))(proj, proj, q_w, k_w, dqn, dkn, dv)


def _pair_sum(name, partial, theirs, core):
    _, r, c = theirs.shape
    tr = r // 2 if r % 16 == 0 else r

    def body(core_ref, a_ref, b_ref, o_ref):
        o_ref[...] = (a_ref[...].astype(F32) + b_ref[...].astype(F32)).astype(BF16)

    spec = pl.BlockSpec((None, tr, c), lambda q, i, core_ref: (q, i, 0))
    grid_spec = pltpu.PrefetchScalarGridSpec(
        num_scalar_prefetch=1, grid=(4, r // tr),
        in_specs=[pl.BlockSpec((None, tr, c), lambda q, i, core_ref: (2 * q + core_ref[0], i, 0)), spec],
        out_specs=spec)
    return pl.pallas_call(body, name=name, grid_spec=grid_spec, out_shape=jax.ShapeDtypeStruct(theirs.shape, BF16),
                          compiler_params=_params(("parallel", "parallel")))(core, partial, theirs)


def _adamw_step(w, m, v, g):
    nm = ADAM_B1 * m + (1.0 - ADAM_B1) * g
    nv = ADAM_B2 * v + (1.0 - ADAM_B2) * (g * g)
    m_hat = nm / (1.0 - ADAM_B1 ** ADAM_STEP)
    v_hat = nv / (1.0 - ADAM_B2 ** ADAM_STEP)
    return -ADAM_LR * (m_hat / (jnp.sqrt(v_hat) + ADAM_EPS) + ADAM_WD * w), nm, nv


def _adamw(name, w, m, v, addends, tr=None):
    r, c = w.shape
    tr = r if tr is None else tr
    n_add = len(addends)

    def body(*refs):
        w_ref, m_ref, v_ref = refs[:3]
        add_refs = refs[3:3 + n_add]
        g_ref, d_ref, nm_ref, nv_ref = refs[3 + n_add:]
        g = add_refs[0][...].astype(F32)
        for a_ref in add_refs[1:]:
            g = g + a_ref[...].astype(F32)
        g_ref[...] = g
        d_ref[...], nm_ref[...], nv_ref[...] = _adamw_step(w_ref[...], m_ref[...], v_ref[...], g)

    spec = pl.BlockSpec((tr, c), lambda i: (i, 0))
    out = jax.ShapeDtypeStruct((r, c), F32)
    return pl.pallas_call(body, name=name, grid=(r // tr,), in_specs=[spec] * (3 + n_add), out_specs=[spec] * 4,
                          out_shape=[out] * 4, compiler_params=_params(("parallel",)))(w, m, v, *addends)


def _adamw_reduced(name, w, m, v, chip_sums, received, chip, tr):
    r, c = w.shape

    def body(chip_ref, w_ref, m_ref, v_ref, own_ref, r0_ref, r1_ref, r2_ref, g_ref, d_ref, nm_ref, nv_ref):
        g = ((own_ref[...].astype(F32) + r0_ref[...].astype(F32)) + r1_ref[...].astype(F32)) + r2_ref[...].astype(F32)
        g_ref[...] = g
        d_ref[...], nm_ref[...], nv_ref[...] = _adamw_step(w_ref[...], m_ref[...], v_ref[...], g)

    spec = pl.BlockSpec((tr, c), lambda i, chip_ref: (i, 0))

    def slot(k):
        return pl.BlockSpec((None, tr, c), lambda i, chip_ref: (k, i, 0))

    grid_spec = pltpu.PrefetchScalarGridSpec(
        num_scalar_prefetch=1, grid=(r // tr,),
        in_specs=[spec, spec, spec, pl.BlockSpec((None, tr, c), lambda i, chip_ref: (chip_ref[0], i, 0)),
                  slot(0), slot(1), slot(2)],
        out_specs=[spec] * 4)
    out = jax.ShapeDtypeStruct((r, c), F32)
    return pl.pallas_call(body, name=name, grid_spec=grid_spec, out_shape=[out] * 4,
                          compiler_params=_params(("parallel",)))(chip, w, m, v, chip_sums, received, received, received)


def _sum_devices(gathered):
    _, r, c = gathered.shape

    def body(g_ref, o_ref):
        acc = g_ref[0]
        for d in range(1, N_DEV):
            acc = acc + g_ref[d]
        o_ref[...] = acc

    return pl.pallas_call(body, name="sum_devices", out_shape=jax.ShapeDtypeStruct((r, c), F32))(gathered)


def _pack_rows(vectors, rows):
    flat = jnp.concatenate([v.reshape(-1) for v in vectors])
    return jnp.pad(flat, (0, rows * 128 - flat.shape[0])).reshape(rows, 128)


def _unpack(flat, shapes):
    out, off = [], 0
    for shp in shapes:
        n = 1
        for d in shp:
            n *= d
        out.append(flat[off:off + n].reshape(shp))
        off += n
    return out


def _device_step(xs, tgt, mod, norm1_w, norm2_w, lb_logits, hg_norm_w, q_norm_w, k_norm_w, conv_w_full, conv_b,
                 win_g, w_out_x, w_up_x, w_down_x, core=None):
    fused = core is not None
    shift1, scale1, gate1, shift2, scale2, gate2 = (mod[k] for k in range(6))

    h, rstd1 = _norm_fwd("norm1_fwd", xs, norm1_w, scale1, shift1)
    if fused:
        near = (0, 1, 2)
        head_rows, tail_rows = (0, UP_HEAD_ROWS), (UP_HEAD_ROWS, D_MODEL - UP_HEAD_ROWS)
        proj, (wout_g, wup_g) = _mm_blocked_rhs(
            "mm_in", h, win_g, fused_arrays=[w_out_x, w_up_x],
            fused=[_FusedCopies("gather", [w_out_x]), _FusedCopies("gather", [w_up_x], peers=near, rows=head_rows)])
        (a_out, o_pre), (wup_g,) = _hgrn_fwd(
            proj, lb_logits, hg_norm_w, fused_arrays=[w_up_x, wup_g],
            fused=_FusedCopies("gather_more", [w_up_x, wup_g], peers=near, rows=tail_rows, relay_rows=head_rows))
        wout_g, = _forward_to_sibling("allgather_stage2_out", [wout_g])
        wout_full = wout_g.reshape(D_MODEL, D_MODEL)
        (qn, kn, vb), _ = _qk_prep(proj, q_norm_w, k_norm_w)
        (att_o, lse), (wup_g,) = _attn_fwd(qn, kn, vb, _FusedCopies("relay", [wup_g], rows=tail_rows), [wup_g])
    else:
        proj = _mm_blocked_rhs("mm_in", h, win_g)
        (a_out, o_pre), _ = _hgrn_fwd(proj, lb_logits, hg_norm_w)
        wup_g, wout_full, wdown_full = w_up_x, w_out_x, w_down_x
        (qn, kn, vb), _ = _qk_prep(proj, q_norm_w, k_norm_w)
        (att_o, lse), _ = _attn_fwd(qn, kn, vb)
    mixin = jnp.concatenate([a_out, att_o.astype(BF16)], axis=1)
    if fused:
        mix, (wup_g,) = _mm_plain("mm_out", mixin, wout_full, NN, 512, 1024, F32,
                                  fused=_FusedCopies("forward", [wup_g]), fused_arrays=[wup_g])
    else:
        mix = _mm_plain("mm_out", mixin, wout_full, NN, 512, 1024, F32)
    x1, h2, rstd2 = _norm_fwd("norm2_fwd", xs, norm2_w, scale2, shift2, resid=mix, gate=gate1)
    if fused:
        u, (wdown_g,) = _mm_blocked_rhs("mm_up", h2, wup_g, fused=_FusedCopies("gather", [w_down_x]),
                                        fused_arrays=[w_down_x])
        y, (wdown_g,) = _conv_gate_fwd(u, conv_w_full, conv_b, _FusedCopies("forward", [wdown_g]), [wdown_g])
        wdown_full = wdown_g.reshape(D_FF, D_MODEL)
    else:
        u = _mm_blocked_rhs("mm_up", h2, wup_g)
        y = _conv_gate_fwd(u, conv_w_full, conv_b)
    ffn = _mm_plain("mm_down", y, wdown_full, NN, MM_TILE, 512, F32)
    loss_v, dout, dffn, dgate2 = _loss_head(x1, ffn, gate2, tgt)

    dy = _mm_plain("mm_down_dx", dffn, wdown_full, NT, MM_TILE, UP_BLK, BF16)
    gw_down = _mm_plain("mm_down_dw", y, dffn, TN, UP_BLK, 1024, BF16)
    da, dg, gconv_w, gconv_b = _conv_gate_bwd(u, dy, conv_w_full, conv_b)
    dh2 = _mm_halves_rhs_t("mm_up_dx", da, dg, wup_g)
    gw_up = _mm_halves_wgrad("mm_up_dw", h2, da, dg)
    if fused:
        part_up, part_down = gw_up, gw_down.reshape(N_DEV, FF_BLK, D_MODEL)
        (dx1, dmix, dshift2, dscale2, gnorm2, dgate1), (sib_up,) = _norm_bwd(
            "norm2_bwd", dh2, x1, rstd2, norm2_w, scale2, dout, mix=mix, gate=gate1,
            fused=_FusedCopies("sibling", [part_up]), fused_arrays=[part_up])
    else:
        dx1, dmix, dshift2, dscale2, gnorm2, dgate1 = _norm_bwd(
            "norm2_bwd", dh2, x1, rstd2, norm2_w, scale2, dout, mix=mix, gate=gate1)
    gw_out = _mm_plain("mm_out_dw", mixin, dmix, TN, 512, 1024, BF16)
    if fused:
        part_out = gw_out.reshape(N_DEV, OUT_BLK, D_MODEL)
        dmixin, (sib_out, sib_down) = _mm_plain(
            "mm_out_dx", dmix, wout_full, NT, 512, 1024, F32,
            fused=_FusedCopies("sibling", [part_out, part_down]), fused_arrays=[part_out, part_down])
        cs_up = _pair_sum("grad_pair_sum_up", part_up, sib_up, core)
        cs_out = _pair_sum("grad_pair_sum_out", part_out, sib_out, core)
        cs_down = _pair_sum("grad_pair_sum_down", part_down, sib_down, core)
        (dhq, dhf, dhi, dhg, glog, ghg), (fc_up,) = _hgrn_bwd(
            proj, lb_logits, hg_norm_w, o_pre, dmixin, _FusedCopies("chips", [cs_up]), [cs_up])
        (dqn, dkn, dvv), (fc_down,) = _attn_bwd(qn, kn, vb, att_o, lse, dmixin,
                                                _FusedCopies("chips", [cs_down]), [cs_down])
    else:
        dmixin = _mm_plain("mm_out_dx", dmix, wout_full, NT, 512, 1024, F32)
        (dhq, dhf, dhi, dhg, glog, ghg), _ = _hgrn_bwd(proj, lb_logits, hg_norm_w, o_pre, dmixin)
        (dqn, dkn, dvv), _ = _attn_bwd(qn, kn, vb, att_o, lse, dmixin)
    daq, dak, dav, gqw, gkw = _qk_bwd(proj, q_norm_w, k_norm_w, dqn, dkn, dvv)
    dproj = jnp.concatenate([dhq, dhf, dhi, dhg, daq, dak, dav], axis=1)
    if fused:
        gw_in, (fc_out,) = _mm_wgrad_blocked("mm_in_dw", h, dproj, fused=_FusedCopies("chips", [cs_out]),
                                             fused_arrays=[cs_out])
        from_sibling, = _exchange_sibling("grad_exchange_sibling_b", [gw_in])
        cs_in = _pair_sum("grad_pair_sum_in", gw_in, from_sibling, core)
        dh, (fc_in,) = _mm_blocked_rhs_t("mm_in_dx", dproj, win_g, fused=_FusedCopies("chips", [cs_in]),
                                         fused_arrays=[cs_in])
        large = [(cs_in, fc_in), (cs_out, fc_out), (cs_up, fc_up), (cs_down, fc_down)]
    else:
        gw_in = _mm_wgrad_blocked("mm_in_dw", h, dproj)
        dh = _mm_blocked_rhs_t("mm_in_dx", dproj, win_g)
        large = [gw_in, gw_out, gw_up, gw_down]
    grad_x, dshift1, dscale1, gnorm1 = _norm_bwd("norm1_bwd", dh, xs, rstd1, norm1_w, scale1, dx1)
    gmod = jnp.concatenate([dshift1, dscale1, dgate1, dshift2, dscale2, dgate2], axis=1)
    return (loss_v, grad_x, gmod, gnorm1, gnorm2, glog, ghg, gqw, gkw, gconv_b, gconv_w, *large)


def kernel(x, c, w_ada, b_ada, norm1_w, w_in, lb_logits, hg_norm_w, q_norm_w, k_norm_w, w_out, norm2_w, w_up, conv_w, conv_b, w_down, loss_target, m_w_ada, m_b_ada, m_norm1_w, m_w_in, m_lb_logits, m_hg_norm_w, m_q_norm_w, m_k_norm_w, m_w_out, m_norm2_w, m_w_up, m_conv_w, m_conv_b, m_w_down, v_w_ada, v_b_ada, v_norm1_w, v_w_in, v_lb_logits, v_hg_norm_w, v_q_norm_w, v_k_norm_w, v_w_out, v_norm2_w, v_w_up, v_conv_w, v_conv_b, v_w_down):
    ix, iy, ic = lax.axis_index("x"), lax.axis_index("y"), lax.axis_index("c")
    me = 4 * ix + 2 * iy + ic
    my_chip = 2 * ix + iy

    xs = x[0]
    tgt = loss_target[0]

    win_g, = _allgather_weights([w_in[0].astype(BF16)])

    first = _allgather_vmem(_pack_rows([c, conv_w[0]], 40), "allgather_c_conv_w").reshape(N_DEV, 40 * 128)
    c_all = first[:, :D_MODEL]
    conv_w_full = (first[:, D_MODEL:D_MODEL + 3 * FF_BLK].reshape(N_DEV, 3, FF_BLK).transpose(1, 0, 2)
                   .reshape(3, D_FF))

    b_blk = lax.dynamic_slice_in_dim(b_ada, me * ADA_BLK, ADA_BLK, axis=1)
    mod_cols = _ada_fwd(c_all, w_ada[0], b_blk)
    mod_all = _allgather_vmem(mod_cols, "allgather_mod").reshape(N_DEV, N_DEV, ADA_BLK)
    mod = lax.dynamic_index_in_dim(mod_all, me, axis=1, keepdims=False).reshape(6, 1, D_MODEL)

    (loss_v, grad_x, gmod, gnorm1, gnorm2, glog, ghg, gqw, gkw, gconv_b, gconv_w,
     rs_in, rs_out, rs_up, rs_down) = _device_step(
        xs, tgt, mod, norm1_w, norm2_w, lb_logits, hg_norm_w, q_norm_w, k_norm_w, conv_w_full, conv_b,
        win_g, w_out[0].astype(BF16), w_up[0].astype(BF16), w_down[0].astype(BF16),
        core=jnp.reshape(ic, (1,)).astype(jnp.int32))

    small_shapes = [(1, 6 * D_MODEL), (1, D_MODEL), (1, D_MODEL), (2, HEADS * HEAD_DIM), (1, HEAD_DIM),
                    (1, HEAD_DIM), (1, HEAD_DIM), (1, D_FF), (3, D_FF), (1, 1)]
    small = [gmod, gnorm1, gnorm2, glog, ghg, gqw, gkw, gconv_b, gconv_w, loss_v[:, 0:1]]
    n_small = sum(a.size for a in small)
    rows = -(-n_small // 1024) * 8
    gathered = _allgather_vmem(_pack_rows(small, rows), "allgather_small").reshape(N_DEV, rows, 128)
    summed = _sum_devices(gathered).reshape(-1)
    (g_b_ada, g_norm1, g_norm2, g_lb, g_hg, g_q, g_k, g_conv_b, g_conv_w_full, loss_sum) = _unpack(summed, small_shapes)
    loss = loss_sum[0, 0]
    g_conv_w = lax.dynamic_slice_in_dim(g_conv_w_full, me * FF_BLK, FF_BLK, axis=1)

    gmod_all = gathered[:, :6 * D_MODEL // 128, :].reshape(N_DEV, 6 * D_MODEL)
    gmod_cols = lax.dynamic_slice_in_dim(gmod_all, me * ADA_BLK, ADA_BLK, axis=1)
    g_w_ada_raw = _ada_wgrad(c_all, gmod_cols)

    chip = jnp.reshape(my_chip, (1,)).astype(jnp.int32)

    def big_update(name, w, m, v, rs, tr):
        chip_sums, received = rs
        return _adamw_reduced(name, w[0], m[0], v[0], chip_sums, received, chip, tr)

    r_in = big_update("adamw_w_in", w_in, m_w_in, v_w_in, rs_in, 256)
    r_out = big_update("adamw_w_out", w_out, m_w_out, v_w_out, rs_out, 128)
    r_up = big_update("adamw_w_up", w_up, m_w_up, v_w_up, rs_up, 256)
    r_down = big_update("adamw_w_down", w_down, m_w_down, v_w_down, rs_down, 176)
    r_ada = _adamw("adamw_w_ada", w_ada[0], m_w_ada[0], v_w_ada[0], [g_w_ada_raw], tr=256)
    r_convw = _adamw("adamw_conv_w", conv_w[0], m_conv_w[0], v_conv_w[0], [g_conv_w])

    rep_shapes = [(1, 6 * D_MODEL), (1, D_MODEL), (1, D_MODEL), (2, HEADS * HEAD_DIM), (1, HEAD_DIM),
                  (1, HEAD_DIM), (1, HEAD_DIM), (1, D_FF)]
    rep_rows = -(-sum(a * b for a, b in rep_shapes) // 1024) * 8
    pack = lambda arrs: _pack_rows(arrs, rep_rows)
    rep = _adamw("adamw_small",
                 pack([b_ada, norm1_w, norm2_w, lb_logits, hg_norm_w, q_norm_w, k_norm_w, conv_b]),
                 pack([m_b_ada, m_norm1_w, m_norm2_w, m_lb_logits, m_hg_norm_w, m_q_norm_w, m_k_norm_w, m_conv_b]),
                 pack([v_b_ada, v_norm1_w, v_norm2_w, v_lb_logits, v_hg_norm_w, v_q_norm_w, v_k_norm_w, v_conv_b]),
                 [pack([g_b_ada, g_norm1, g_norm2, g_lb, g_hg, g_q, g_k, g_conv_b])])
    rep = [_unpack(r.reshape(-1), rep_shapes) for r in rep]

    def big(r):
        return [a[None] for a in r]

    order = {"w_ada": big(r_ada), "b_ada": [r[0] for r in rep], "norm1_w": [r[1] for r in rep],
             "w_in": big(r_in), "lb_logits": [r[3] for r in rep], "hg_norm_w": [r[4] for r in rep],
             "q_norm_w": [r[5] for r in rep], "k_norm_w": [r[6] for r in rep], "w_out": big(r_out),
             "norm2_w": [r[2] for r in rep], "w_up": big(r_up), "conv_w": big(r_convw),
             "conv_b": [r[7] for r in rep], "w_down": big(r_down)}
    names = ["w_ada", "b_ada", "norm1_w", "w_in", "lb_logits", "hg_norm_w", "q_norm_w", "k_norm_w", "w_out",
             "norm2_w", "w_up", "conv_w", "conv_b", "w_down"]
    outs = [loss, grad_x[None]]
    for kind in range(4):
        outs += [order[n][kind] for n in names]
    return tuple(outs)
```

```python
import jax
import jax.numpy as jnp
import numpy as np
from jax import lax
from jax.experimental import pallas as pl
from jax.experimental.pallas import tpu as pltpu

F32 = jnp.float32
BF16 = jnp.bfloat16

N_DEV = 8
SEQ = 2048
D_MODEL = 2048
HEADS = 8
HEAD_DIM = 128
IN_COLS = 7168
IN_BLK = IN_COLS // N_DEV
D_FF = 5632
UP_BLK = 2 * D_FF // N_DEV
FF_BLK = D_FF // N_DEV
ADA_BLK = 6 * D_MODEL // N_DEV
OUT_BLK = D_MODEL // N_DEV
EPS = 1e-6
CHUNK = 16
ROW_TILE = 256
MM_TILE = 1024
V7X_VMEM_LIMIT = 56 * 1024 * 1024

ADAM_LR = 0.001
ADAM_B1 = 0.9
ADAM_B2 = 0.999
ADAM_EPS = 1e-08
ADAM_WD = 0.01
ADAM_STEP = 10

NN = (((1,), (0,)), ((), ()))
NT = (((1,), (1,)), ((), ()))
TN = (((0,), (0,)), ((), ()))
MESH = pl.DeviceIdType.MESH


def _params(sem=None, vmem=V7X_VMEM_LIMIT):
    return pltpu.CompilerParams(dimension_semantics=sem, vmem_limit_bytes=vmem)


def _sigmoid(x):
    return 1.0 / (1.0 + jnp.exp(-x))


def _dsilu(x, s):
    return s * (1.0 + x * (1.0 - s))


def _lane_sum(x, ones_bf16):
    return jnp.dot(x.astype(BF16), ones_bf16, preferred_element_type=F32)


def _mesh_pos():
    return lax.axis_index("x"), lax.axis_index("y"), lax.axis_index("c")


def _allgather_vmem(x_blk, name):
    m_per, n = x_blk.shape

    def body(x_ref, out_ref, send_sems, recv_sems, local_sem):
        x, y, c = _mesh_pos()
        me, sibling = (x, y, c), (x, y, 1 - c)
        chips = [(1 - x, y), (x, 1 - y), (1 - x, 1 - y)]

        def rows(px, py, pc):
            return out_ref.at[pl.ds((4 * px + 2 * py + pc) * m_per, m_per), :]

        def copy(k, block, to, src=None):
            return pltpu.make_async_remote_copy(
                src_ref=rows(*block) if src is None else src, dst_ref=rows(*block),
                send_sem=send_sems.at[k], recv_sem=recv_sems.at[k], device_id=to, device_id_type=MESH)

        mine = pltpu.make_async_copy(x_ref, rows(*me), local_sem)
        mine.start()
        first = [copy(0, me, sibling, src=x_ref)]
        first += [copy(1 + j, me, (*chip, c), src=x_ref) for j, chip in enumerate(chips)]
        for cp in first:
            cp.start()
        passed = [copy(4 + j, (*chip, c), sibling) for j, chip in enumerate(chips)]
        for j, chip in enumerate(chips):
            copy(1 + j, (*chip, c), me).wait_recv()
            passed[j].start()
        copy(0, sibling, me).wait_recv()
        for j, chip in enumerate(chips):
            copy(4 + j, (*chip, 1 - c), me).wait_recv()
        for cp in first + passed:
            cp.wait_send()
        mine.wait()

    return pl.pallas_call(
        body, name=name,
        out_shape=jax.ShapeDtypeStruct((N_DEV * m_per, n), x_blk.dtype),
        in_specs=[pl.BlockSpec(memory_space=pltpu.VMEM)],
        out_specs=pl.BlockSpec(memory_space=pltpu.VMEM),
        scratch_shapes=[pltpu.SemaphoreType.DMA((7,)), pltpu.SemaphoreType.DMA((7,)), pltpu.SemaphoreType.DMA],
    )(x_blk)


def _flip(v, bit):
    return v + bit - 2 * v * bit


def _relay_chips(x, y, c):
    return (_flip(x, 1 - c), _flip(y, c)), (_flip(x, c), _flip(y, 1 - c))


DOWN_HEAD_ROWS = 192
UP_HEAD_ROWS = 768
GATHER_PARTS = 4


def _allgather_weights(blocks):
    n_arr = len(blocks)
    parts = GATHER_PARTS

    def body(*refs):
        ins, outs = refs[:n_arr], refs[n_arr:2 * n_arr]
        send_sems, recv_sems, local_sems = refs[2 * n_arr:]
        x, y, c = _mesh_pos()
        me, sibling = (x, y, c), (x, y, 1 - c)
        near = [(1 - x, y), (x, 1 - y)]
        chips = near + [(1 - x, 1 - y)]
        relay_from, relay_to = _relay_chips(x, y, c)

        def rows(a, p):
            hr = ins[a].shape[0] // parts
            return pl.ds(p * hr, hr)

        def slot(a, pos, p):
            return outs[a].at[4 * pos[0] + 2 * pos[1] + pos[2], rows(a, p)]

        def copy(a, k, p, src, lands, to):
            return pltpu.make_async_remote_copy(
                src_ref=src, dst_ref=slot(a, lands, p), send_sem=send_sems.at[a, k, p], recv_sem=recv_sems.at[a, k, p],
                device_id=to, device_id_type=MESH)

        sent = []
        local = [pltpu.make_async_copy(ins[a], outs[a].at[4 * x + 2 * y + c], local_sems.at[a]) for a in range(n_arr)]
        for cp in local:
            cp.start()
        for p in range(parts):
            for a in range(n_arr):
                own = ins[a].at[rows(a, p)]
                sent.append(copy(a, 0, p, own, me, sibling))
                sent += [copy(a, 1 + j, p, own, me, (*chip, c)) for j, chip in enumerate(near)]
        for cp in sent:
            cp.start()

        def start(cp):
            cp.start()
            sent.append(cp)

        for p in range(parts):
            for a in range(n_arr):
                for j, chip in enumerate(near):
                    copy(a, 1 + j, p, ins[a].at[rows(a, p)], (*chip, c), me).wait_recv()
                    start(copy(a, 4 + j, p, slot(a, (*chip, c), p), (*chip, c), sibling))
                start(copy(a, 3, p, slot(a, (*relay_from, c), p), (*relay_from, c), (*relay_to, c)))
        for p in range(parts):
            for a in range(n_arr):
                copy(a, 3, p, ins[a].at[rows(a, p)], (*chips[2], c), me).wait_recv()
                start(copy(a, 6, p, slot(a, (*chips[2], c), p), (*chips[2], c), sibling))
        for p in range(parts):
            for a in range(n_arr):
                copy(a, 0, p, ins[a].at[rows(a, p)], sibling, me).wait_recv()
                for j, chip in enumerate(chips):
                    copy(a, 4 + j, p, ins[a].at[rows(a, p)], (*chip, 1 - c), me).wait_recv()
        for cp in sent:
            cp.wait_send()
        for cp in local:
            cp.wait()

    return pl.pallas_call(
        body, name="allgather_weights",
        out_shape=[jax.ShapeDtypeStruct((N_DEV,) + b.shape, b.dtype) for b in blocks],
        in_specs=[pl.BlockSpec(memory_space=pltpu.HBM)] * n_arr, out_specs=[pl.BlockSpec(memory_space=pltpu.HBM)] * n_arr,
        scratch_shapes=[pltpu.SemaphoreType.DMA((n_arr, 7, parts)), pltpu.SemaphoreType.DMA((n_arr, 7, parts)),
                        pltpu.SemaphoreType.DMA((n_arr,))],
    )(*blocks)


HBM_SPEC = pl.BlockSpec(memory_space=pltpu.HBM)


class _FusedCopies:
    def __init__(self, kind, arrays, peers=(0, 1, 2, 3), rows=None, relay_rows=None):
        self.kind = kind
        self.peers = peers
        self.rows = rows
        self.relay_rows = relay_rows
        n = len(arrays) // 2 if kind == "gather_more" else len(arrays)
        self.n = n
        self.n_in = len(arrays)
        self.aliases = {}
        if kind == "gather":
            self.out_shape = [jax.ShapeDtypeStruct((N_DEV,) + a.shape, a.dtype) for a in arrays]
            self.scratch_shapes = [pltpu.SemaphoreType.DMA((n, 4, GATHER_PARTS)),
                                   pltpu.SemaphoreType.DMA((n, 4, GATHER_PARTS)), pltpu.SemaphoreType.DMA((n,))]
        elif kind == "gather_more":
            self.out_shape = [jax.ShapeDtypeStruct(a.shape, a.dtype) for a in arrays[n:]]
            self.scratch_shapes = [pltpu.SemaphoreType.DMA((n, 5, GATHER_PARTS)),
                                   pltpu.SemaphoreType.DMA((n, 5, GATHER_PARTS)), pltpu.SemaphoreType.DMA((n,))]
            self.aliases = {n + a: a for a in range(n)}
        elif kind == "relay":
            self.out_shape = [jax.ShapeDtypeStruct(a.shape, a.dtype) for a in arrays]
            self.scratch_shapes = [pltpu.SemaphoreType.DMA((n,)), pltpu.SemaphoreType.DMA((n,))]
            self.aliases = {a: a for a in range(n)}
        elif kind == "forward":
            self.out_shape = [jax.ShapeDtypeStruct(a.shape, a.dtype) for a in arrays]
            self.scratch_shapes = [pltpu.SemaphoreType.DMA((n, 3)), pltpu.SemaphoreType.DMA((n, 3))]
            self.aliases = {a: a for a in range(n)}
        elif kind == "sibling":
            self.out_shape = [jax.ShapeDtypeStruct((4,) + a.shape[1:], a.dtype) for a in arrays]
            self.scratch_shapes = [pltpu.SemaphoreType.DMA((n, 4)), pltpu.SemaphoreType.DMA((n, 4))]
        else:
            self.out_shape = [jax.ShapeDtypeStruct((3,) + a.shape[1:], a.dtype) for a in arrays]
            self.scratch_shapes = [pltpu.SemaphoreType.DMA((n, 3)), pltpu.SemaphoreType.DMA((n, 3))]
        self.in_specs = [HBM_SPEC] * self.n_in
        self.out_specs = [HBM_SPEC] * n
        self.n_scratch = len(self.scratch_shapes)

    def copies(self, ins, outs, sems):
        x, y, c = _mesh_pos()
        chips = [(1 - x, y), (x, 1 - y), (1 - x, 1 - y)]
        sibling = (x, y, 1 - c)
        starts, waits = [], []
        relay_from, relay_to = _relay_chips(x, y, c)

        def relayed(a, buf, lands, send_sem, recv_sem, rows):
            first, count = rows or (0, buf.shape[1])
            span = pl.ds(first, count)
            return pltpu.make_async_remote_copy(
                src_ref=buf.at[4 * relay_from[0] + 2 * relay_from[1] + c, span],
                dst_ref=outs[a].at[4 * lands[0] + 2 * lands[1] + c, span], send_sem=send_sem, recv_sem=recv_sem,
                device_id=(*relay_to, c), device_id_type=MESH)

        if self.kind in ("gather", "gather_more"):
            send_sems, recv_sems, local_sems = sems
            me = (x, y, c)
            peers = [sibling] + [(px, py, c) for px, py in chips]

            def slot(a, pos):
                return outs[a].at[4 * pos[0] + 2 * pos[1] + pos[2]]

            def span(a, p=None):
                first, count = self.rows or (0, ins[a].shape[0])
                if p is None:
                    return pl.ds(first, count)
                return pl.ds(first + p * (count // GATHER_PARTS), count // GATHER_PARTS)

            def remote(a, k, p, lands_from):
                return pltpu.make_async_remote_copy(
                    src_ref=ins[a].at[span(a, p)], dst_ref=slot(a, lands_from).at[span(a, p)],
                    send_sem=send_sems.at[a, k, p], recv_sem=recv_sems.at[a, k, p], device_id=peers[k],
                    device_id_type=MESH)

            for a in range(self.n):
                local = pltpu.make_async_copy(ins[a].at[span(a)], slot(a, me).at[span(a)], local_sems.at[a])
                starts.append(local)
                waits.append(local)
            for p in range(GATHER_PARTS):
                for a in range(self.n):
                    for k in self.peers:
                        starts.append(remote(a, k, p, me))
                        waits.append(remote(a, k, p, peers[k]))
            if self.kind == "gather_more" and self.relay_rows is not None:
                for a in range(self.n):
                    buf = ins[self.n + a]
                    starts.append(relayed(a, buf, relay_from, send_sems.at[a, 4, 0], recv_sems.at[a, 4, 0],
                                          self.relay_rows))
                    waits.append(relayed(a, buf, chips[2], send_sems.at[a, 4, 0], recv_sems.at[a, 4, 0],
                                         self.relay_rows))
        elif self.kind == "relay":
            send_sems, recv_sems = sems
            for a in range(self.n):
                starts.append(relayed(a, ins[a], relay_from, send_sems.at[a], recv_sems.at[a], self.rows))
                waits.append(relayed(a, ins[a], chips[2], send_sems.at[a], recv_sems.at[a], self.rows))
        elif self.kind == "forward":
            send_sems, recv_sems = sems

            def passed_on(a, j, pc_src, pc_dst):
                px, py = chips[j]
                return pltpu.make_async_remote_copy(
                    src_ref=ins[a].at[4 * px + 2 * py + pc_src], dst_ref=outs[a].at[4 * px + 2 * py + pc_dst],
                    send_sem=send_sems.at[a, j], recv_sem=recv_sems.at[a, j], device_id=sibling, device_id_type=MESH)

            for a in range(self.n):
                for j in range(3):
                    starts.append(passed_on(a, j, c, c))
                    waits.append(passed_on(a, j, c, 1 - c))
        elif self.kind == "sibling":
            send_sems, recv_sems = sems
            for a in range(self.n):
                for q in range(4):
                    cp = pltpu.make_async_remote_copy(
                        src_ref=ins[a].at[2 * q + 1 - c], dst_ref=outs[a].at[q], send_sem=send_sems.at[a, q],
                        recv_sem=recv_sems.at[a, q], device_id=sibling, device_id_type=MESH)
                    starts.append(cp)
                    waits.append(cp)
        else:
            send_sems, recv_sems = sems
            for a in range(self.n):
                for j, (px, py) in enumerate(chips):
                    cp = pltpu.make_async_remote_copy(
                        src_ref=ins[a].at[2 * px + py], dst_ref=outs[a].at[j], send_sem=send_sems.at[a, j],
                        recv_sem=recv_sems.at[a, j], device_id=(px, py, c), device_id_type=MESH)
                    starts.append(cp)
                    waits.append(cp)
        return starts, waits


def _fused_groups(fused):
    if fused is None:
        return []
    return list(fused) if isinstance(fused, (list, tuple)) else [fused]


def _host_body(body, n_in, n_out, fused, first_last):
    groups = _fused_groups(fused)
    if not groups:
        return body
    n_fin, n_fout = sum(g.n_in for g in groups), sum(g.n for g in groups)
    n_fsem = sum(g.n_scratch for g in groups)

    def wrapped(*refs):
        core_in, f_in = refs[:n_in], refs[n_in:n_in + n_fin]
        core_out = refs[n_in + n_fin:n_in + n_fin + n_out]
        f_out = refs[n_in + n_fin + n_out:n_in + n_fin + n_out + n_fout]
        rest = refs[n_in + n_fin + n_out + n_fout:]
        core_scratch, f_sems = rest[:len(rest) - n_fsem], rest[len(rest) - n_fsem:]
        starts, waits = [], []
        for g in groups:
            s, w = g.copies(f_in[:g.n_in], f_out[:g.n], f_sems[:g.n_scratch])
            f_in, f_out, f_sems = f_in[g.n_in:], f_out[g.n:], f_sems[g.n_scratch:]
            starts += s
            waits += w
        first, last = first_last()

        @pl.when(first)
        def _():
            for cp in starts:
                cp.start()

        body(*core_in, *core_out, *core_scratch)

        @pl.when(last)
        def _():
            for cp in waits:
                cp.wait()

    return wrapped


def _host_call(body, n_in, n_out, fused, first_last, *, name, grid, in_specs, out_specs, out_shape, scratch_shapes,
               sem, operands):
    aliases = {}
    in_specs, out_specs, out_shape, scratch_shapes = list(in_specs), list(out_specs), list(out_shape), list(scratch_shapes)
    fin, fout = n_in, n_out
    for g in _fused_groups(fused):
        aliases.update({fin + fi: fout + fo for fi, fo in g.aliases.items()})
        fin, fout = fin + g.n_in, fout + g.n
        in_specs += g.in_specs
        out_specs += g.out_specs
        out_shape += g.out_shape
        scratch_shapes += g.scratch_shapes
        sem = tuple("arbitrary" for _ in sem)
    res = pl.pallas_call(_host_body(body, n_in, n_out, fused, first_last), name=name, grid=grid, in_specs=in_specs,
                         out_specs=out_specs, out_shape=out_shape, scratch_shapes=scratch_shapes,
                         input_output_aliases=aliases, compiler_params=_params(sem))(*operands)
    return list(res[:n_out]), list(res[n_out:])


def _forward_to_sibling(name, gathered):
    n_arr = len(gathered)

    def body(*refs):
        ins, outs = refs[:n_arr], refs[n_arr:2 * n_arr]
        send_sems, recv_sems = refs[2 * n_arr:]
        x, y, c = _mesh_pos()
        chips = [(1 - x, y), (x, 1 - y), (1 - x, 1 - y)]

        def copy(a, j, pc):
            px, py = chips[j]
            s = 4 * px + 2 * py + pc
            return pltpu.make_async_remote_copy(
                src_ref=ins[a].at[s], dst_ref=outs[a].at[s], send_sem=send_sems.at[a, j], recv_sem=recv_sems.at[a, j],
                device_id=(x, y, 1 - c), device_id_type=MESH)

        for a in range(n_arr):
            for j in range(3):
                copy(a, j, c).start()
        for a in range(n_arr):
            for j in range(3):
                copy(a, j, 1 - c).wait_recv()
                copy(a, j, c).wait_send()

    return pl.pallas_call(
        body, name=name,
        out_shape=[jax.ShapeDtypeStruct(g.shape, g.dtype) for g in gathered],
        in_specs=[HBM_SPEC] * n_arr, out_specs=[HBM_SPEC] * n_arr,
        input_output_aliases={a: a for a in range(n_arr)},
        scratch_shapes=[pltpu.SemaphoreType.DMA((n_arr, 3)), pltpu.SemaphoreType.DMA((n_arr, 3))],
    )(*gathered)


def _exchange_sibling(name, partials):
    n_arr = len(partials)

    def body(*refs):
        ins, outs = refs[:n_arr], refs[n_arr:2 * n_arr]
        send_sems, recv_sems = refs[2 * n_arr:]
        x, y, c = _mesh_pos()
        copies = [pltpu.make_async_remote_copy(
            src_ref=ins[a].at[2 * q + 1 - c], dst_ref=outs[a].at[q], send_sem=send_sems.at[a, q],
            recv_sem=recv_sems.at[a, q], device_id=(x, y, 1 - c), device_id_type=MESH)
            for a in range(n_arr) for q in range(4)]
        for cp in copies:
            cp.start()
        for cp in copies:
            cp.wait_recv()
        for cp in copies:
            cp.wait_send()

    return pl.pallas_call(
        body, name=name,
        out_shape=[jax.ShapeDtypeStruct((4,) + p.shape[1:], p.dtype) for p in partials],
        in_specs=[HBM_SPEC] * n_arr, out_specs=[HBM_SPEC] * n_arr,
        scratch_shapes=[pltpu.SemaphoreType.DMA((n_arr, 4)), pltpu.SemaphoreType.DMA((n_arr, 4))],
    )(*partials)


def _matmul(name, a, b, dims, grid, a_spec, b_spec, o_spec, out_shape, acc_axis=None, fused=None, fused_arrays=()):
    def body(a_ref, b_ref, o_ref):
        r = lax.dot_general(a_ref[...], b_ref[...], dims, preferred_element_type=F32)
        if acc_axis is None:
            o_ref[...] = r.astype(o_ref.dtype)
        else:
            k = pl.program_id(acc_axis)

            @pl.when(k == 0)
            def _():
                o_ref[...] = r

            @pl.when(k > 0)
            def _():
                o_ref[...] += r

    sem = tuple("arbitrary" if i == acc_axis else "parallel" for i in range(len(grid)))
    if fused is None:
        return pl.pallas_call(body, name=name, grid=grid, in_specs=[a_spec, b_spec], out_specs=o_spec,
                              out_shape=out_shape, compiler_params=_params(sem))(a, b)

    def first_last():
        first = last = None
        for ax, n in enumerate(grid):
            f, l = pl.program_id(ax) == 0, pl.program_id(ax) == n - 1
            first, last = (f, l) if first is None else (first & f, last & l)
        return first, last

    (out,), extra = _host_call(body, 2, 1, fused, first_last, name=name, grid=grid, in_specs=[a_spec, b_spec],
                               out_specs=[o_spec], out_shape=[out_shape], scratch_shapes=[], sem=sem,
                               operands=[a, b] + list(fused_arrays))
    return out, extra


def _mm_blocked_rhs(name, a, w_g, tm=MM_TILE, fused=None, fused_arrays=()):
    m, k = a.shape
    nb = w_g.shape[2]
    return _matmul(name, a, w_g, NN, (N_DEV, m // tm),
                   pl.BlockSpec((tm, k), lambda j, i: (i, 0)),
                   pl.BlockSpec((None, k, nb), lambda j, i: (j, 0, 0)),
                   pl.BlockSpec((tm, nb), lambda j, i: (i, j)),
                   jax.ShapeDtypeStruct((m, N_DEV * nb), F32), fused=fused, fused_arrays=fused_arrays)


def _mm_blocked_rhs_t(name, a, w_g, tm=MM_TILE, fused=None, fused_arrays=()):
    m = a.shape[0]
    n, nb = w_g.shape[1], w_g.shape[2]
    return _matmul(name, a, w_g, NT, (m // tm, N_DEV),
                   pl.BlockSpec((tm, nb), lambda i, j: (i, j)),
                   pl.BlockSpec((None, n, nb), lambda i, j: (j, 0, 0)),
                   pl.BlockSpec((tm, n), lambda i, j: (i, 0)),
                   jax.ShapeDtypeStruct((m, n), F32), acc_axis=1, fused=fused, fused_arrays=fused_arrays)


def _mm_wgrad_blocked(name, act, dcols, tk=MM_TILE, fused=None, fused_arrays=()):
    t, k = act.shape
    nb = dcols.shape[1] // N_DEV
    return _matmul(name, act, dcols, TN, (N_DEV, k // tk),
                   pl.BlockSpec((t, tk), lambda j, i: (0, i)),
                   pl.BlockSpec((t, nb), lambda j, i: (0, j)),
                   pl.BlockSpec((None, tk, nb), lambda j, i: (j, i, 0)),
                   jax.ShapeDtypeStruct((N_DEV, k, nb), BF16), fused=fused, fused_arrays=fused_arrays)


def _halves_specs(block, index):
    half = N_DEV // 2
    return (pl.BlockSpec(block, lambda i, j: index(i, jnp.minimum(j, half - 1))),
            pl.BlockSpec(block, lambda i, j: index(i, jnp.maximum(j - half, 0))))


def _mm_halves_rhs_t(name, a_lo, a_hi, w_g, tm=MM_TILE):
    m = a_lo.shape[0]
    n, nb = w_g.shape[1], w_g.shape[2]

    def body(lo_ref, hi_ref, b_ref, o_ref):
        j = pl.program_id(1)

        def accumulate(a_ref):
            r = lax.dot_general(a_ref[...], b_ref[...], NT, preferred_element_type=F32)

            @pl.when(j == 0)
            def _():
                o_ref[...] = r

            @pl.when(j > 0)
            def _():
                o_ref[...] += r

        pl.when(j < N_DEV // 2)(lambda: accumulate(lo_ref))
        pl.when(j >= N_DEV // 2)(lambda: accumulate(hi_ref))

    lo_spec, hi_spec = _halves_specs((tm, nb), lambda i, j: (i, j))
    return pl.pallas_call(
        body, name=name, grid=(m // tm, N_DEV),
        in_specs=[lo_spec, hi_spec, pl.BlockSpec((None, n, nb), lambda i, j: (j, 0, 0))],
        out_specs=pl.BlockSpec((tm, n), lambda i, j: (i, 0)), out_shape=jax.ShapeDtypeStruct((m, n), F32),
        compiler_params=_params(("parallel", "arbitrary")))(a_lo, a_hi, w_g)


def _mm_halves_wgrad(name, act, d_lo, d_hi, tk=MM_TILE):
    t, k = act.shape
    nb = d_lo.shape[1] // (N_DEV // 2)

    def body(a_ref, lo_ref, hi_ref, o_ref):
        j = pl.program_id(0)

        def product(d_ref):
            o_ref[...] = lax.dot_general(a_ref[...], d_ref[...], TN, preferred_element_type=F32).astype(o_ref.dtype)

        pl.when(j < N_DEV // 2)(lambda: product(lo_ref))
        pl.when(j >= N_DEV // 2)(lambda: product(hi_ref))

    half = N_DEV // 2
    return pl.pallas_call(
        body, name=name, grid=(N_DEV, k // tk),
        in_specs=[pl.BlockSpec((t, tk), lambda j, i: (0, i)),
                  pl.BlockSpec((t, nb), lambda j, i: (0, jnp.minimum(j, half - 1))),
                  pl.BlockSpec((t, nb), lambda j, i: (0, jnp.maximum(j - half, 0)))],
        out_specs=pl.BlockSpec((None, tk, nb), lambda j, i: (j, i, 0)),
        out_shape=jax.ShapeDtypeStruct((N_DEV, k, nb), BF16),
        compiler_params=_params(("parallel", "parallel")))(act, d_lo, d_hi)


def _mm_plain(name, a, b, dims, tm, tn, out_dtype, fused=None, fused_arrays=()):
    if dims == NN:
        (m, k), n = a.shape, b.shape[1]
        a_spec = pl.BlockSpec((tm, k), lambda i, j: (i, 0))
        b_spec = pl.BlockSpec((k, tn), lambda i, j: (0, j))
    elif dims == NT:
        (m, k), n = a.shape, b.shape[0]
        a_spec = pl.BlockSpec((tm, k), lambda i, j: (i, 0))
        b_spec = pl.BlockSpec((tn, k), lambda i, j: (j, 0))
    else:
        (k, m), n = a.shape, b.shape[1]
        a_spec = pl.BlockSpec((k, tm), lambda i, j: (0, i))
        b_spec = pl.BlockSpec((k, tn), lambda i, j: (0, j))
    return _matmul(name, a, b, dims, (m // tm, n // tn), a_spec, b_spec,
                   pl.BlockSpec((tm, tn), lambda i, j: (i, j)), jax.ShapeDtypeStruct((m, n), out_dtype),
                   fused=fused, fused_arrays=fused_arrays)


def _ada_fwd(c_all, w_ada_blk, b_blk):
    def body(c_ref, w_ref, b_ref, o_ref):
        cv = c_ref[...]
        o_ref[...] = jnp.dot(cv * _sigmoid(cv), w_ref[...], preferred_element_type=F32) + b_ref[...]

    tn = 512
    return pl.pallas_call(
        body, name="ada_fwd", grid=(ADA_BLK // tn,),
        in_specs=[pl.BlockSpec((N_DEV, D_MODEL), lambda j: (0, 0)),
                  pl.BlockSpec((D_MODEL, tn), lambda j: (0, j)),
                  pl.BlockSpec((1, tn), lambda j: (0, j))],
        out_specs=pl.BlockSpec((N_DEV, tn), lambda j: (0, j)),
        out_shape=jax.ShapeDtypeStruct((N_DEV, ADA_BLK), F32),
        compiler_params=_params(("parallel",)))(c_all, w_ada_blk, b_blk)


def _ada_wgrad(c_all, gmod_cols):
    def body(c_ref, g_ref, o_ref):
        cv = c_ref[...]
        o_ref[...] = lax.dot_general(cv * _sigmoid(cv), g_ref[...], TN, preferred_element_type=F32)

    tk = 512
    return pl.pallas_call(
        body, name="ada_wgrad", grid=(D_MODEL // tk,),
        in_specs=[pl.BlockSpec((N_DEV, tk), lambda i: (0, i)),
                  pl.BlockSpec((N_DEV, ADA_BLK), lambda i: (0, 0))],
        out_specs=pl.BlockSpec((tk, ADA_BLK), lambda i: (i, 0)),
        out_shape=jax.ShapeDtypeStruct((D_MODEL, ADA_BLK), F32),
        compiler_params=_params(("parallel",)))(c_all, gmod_cols)


def _row_spec(cols=D_MODEL):
    return pl.BlockSpec((ROW_TILE, cols), lambda i: (i, 0))


def _vec_spec(cols=D_MODEL):
    return pl.BlockSpec((1, cols), lambda i: (0, 0))


def _norm_fwd(name, x, w, scale, shift, resid=None, gate=None):
    has_res = resid is not None

    def body(*refs):
        if has_res:
            x_ref, r_ref, g_ref, w_ref, sc_ref, sh_ref, xr_ref, h_ref, rs_ref = refs
            xr = x_ref[...] + g_ref[...] * r_ref[...]
            xr_ref[...] = xr
        else:
            x_ref, w_ref, sc_ref, sh_ref, h_ref, rs_ref = refs
            xr = x_ref[...]
        rs = lax.rsqrt(jnp.mean(xr * xr, axis=-1, keepdims=True) + EPS)
        h = (xr * rs) * w_ref[...] * (1.0 + sc_ref[...]) + sh_ref[...]
        h_ref[...] = h.astype(BF16)
        rs_ref[...] = rs

    s = x.shape[0]
    ins = [x] + ([resid, gate] if has_res else []) + [w, scale, shift]
    in_specs = [_row_spec()] + ([_row_spec(), _vec_spec()] if has_res else []) + [_vec_spec()] * 3
    outs = ([jax.ShapeDtypeStruct((s, D_MODEL), F32)] if has_res else []) + [
        jax.ShapeDtypeStruct((s, D_MODEL), BF16), jax.ShapeDtypeStruct((s, 1), F32)]
    out_specs = ([_row_spec()] if has_res else []) + [_row_spec(), pl.BlockSpec((ROW_TILE, 1), lambda i: (i, 0))]
    return pl.pallas_call(body, name=name, grid=(s // ROW_TILE,), in_specs=in_specs, out_specs=out_specs,
                          out_shape=outs, compiler_params=_params(("parallel",)))(*ins)


def _norm_bwd(name, dh, x, rstd, w, scale, dres, mix=None, gate=None, fused=None, fused_arrays=()):
    has_mix = mix is not None

    def body(*refs):
        if has_mix:
            (dh_ref, x_ref, rs_ref, w_ref, sc_ref, dr_ref, mix_ref, g_ref,
             dx_ref, dmix_ref, dsh_ref, dsc_ref, dw_ref, dg_ref) = refs
        else:
            dh_ref, x_ref, rs_ref, w_ref, sc_ref, dr_ref, dx_ref, dsh_ref, dsc_ref, dw_ref = refs
        i = pl.program_id(0)
        dhv = dh_ref[...]
        rs = rs_ref[...]
        xn = x_ref[...] * rs
        wv = w_ref[...]
        one_sc = 1.0 + sc_ref[...]
        dxn = dhv * wv * one_sc
        dx = dr_ref[...] + rs * (dxn - xn * jnp.mean(dxn * xn, axis=-1, keepdims=True))
        dx_ref[...] = dx
        sums = [(dsh_ref, dhv), (dsc_ref, dhv * xn * wv), (dw_ref, dhv * one_sc * xn)]
        if has_mix:
            dmix_ref[...] = (dx * g_ref[...]).astype(BF16)
            sums.append((dg_ref, dx * mix_ref[...]))

        @pl.when(i == 0)
        def _():
            for ref, _v in sums:
                ref[...] = jnp.zeros_like(ref)

        for ref, v in sums:
            ref[...] += jnp.sum(v, axis=0, keepdims=True)

    s = x.shape[0]
    ins = [dh, x, rstd, w, scale, dres] + ([mix, gate] if has_mix else [])
    in_specs = ([_row_spec(), _row_spec(), pl.BlockSpec((ROW_TILE, 1), lambda i: (i, 0)), _vec_spec(), _vec_spec(),
                 _row_spec()] + ([_row_spec(), _vec_spec()] if has_mix else []))
    vec = jax.ShapeDtypeStruct((1, D_MODEL), F32)
    outs = ([jax.ShapeDtypeStruct((s, D_MODEL), F32)] + ([jax.ShapeDtypeStruct((s, D_MODEL), BF16)] if has_mix else [])
            + [vec] * (4 if has_mix else 3))
    out_specs = [_row_spec()] + ([_row_spec()] if has_mix else []) + [_vec_spec()] * (4 if has_mix else 3)

    def first_last():
        i = pl.program_id(0)
        return i == 0, i == s // ROW_TILE - 1

    res, extra = _host_call(body, len(ins), len(outs), fused, first_last, name=name, grid=(s // ROW_TILE,),
                            in_specs=in_specs, out_specs=out_specs, out_shape=outs, scratch_shapes=[],
                            sem=("arbitrary",), operands=ins + list(fused_arrays))
    return res if fused is None else (res, extra)


def _loss_head(x1, ffn, gate2, target):
    def body(x_ref, f_ref, g_ref, t_ref, loss_ref, dout_ref, dffn_ref, dg_ref):
        i = pl.program_id(0)
        fv = f_ref[...]
        gv = g_ref[...]
        err = x_ref[...] + gv * fv - t_ref[...]
        dout = err * (1.0 / D_MODEL)
        dout_ref[...] = dout
        dffn_ref[...] = (dout * gv).astype(BF16)

        @pl.when(i == 0)
        def _():
            loss_ref[...] = jnp.zeros_like(loss_ref)
            dg_ref[...] = jnp.zeros_like(dg_ref)

        row = jnp.sum(err * err, axis=-1, keepdims=True) * (1.0 / D_MODEL)
        loss_ref[...] += jnp.broadcast_to(0.5 * jnp.sum(row, axis=0, keepdims=True), (1, 128))
        dg_ref[...] += jnp.sum(dout * fv, axis=0, keepdims=True)

    s = x1.shape[0]
    return pl.pallas_call(
        body, name="loss_head", grid=(s // ROW_TILE,),
        in_specs=[_row_spec(), _row_spec(), _vec_spec(), _row_spec()],
        out_specs=[pl.BlockSpec((1, 128), lambda i: (0, 0)), _row_spec(), _row_spec(), _vec_spec()],
        out_shape=[jax.ShapeDtypeStruct((1, 128), F32), jax.ShapeDtypeStruct((s, D_MODEL), F32),
                   jax.ShapeDtypeStruct((s, D_MODEL), BF16), jax.ShapeDtypeStruct((1, D_MODEL), F32)],
        compiler_params=_params(("arbitrary",)))(x1, ffn, gate2, target)


CONV_TILE = 512
N_CONV_TILES = D_FF // CONV_TILE


def _shift_rows(a, k, row):
    n = a.shape[0]
    if k > 0:
        return jnp.where(row >= k, pltpu.roll(a, k, 0), 0.0)
    return jnp.where(row < n + k, pltpu.roll(a, n + k, 0), 0.0)


def _conv_gate_fwd(u, conv_w, conv_b, fused=None, fused_arrays=()):
    s = u.shape[0]

    def body(a_ref, g_ref, w_ref, b_ref, y_ref):
        a = a_ref[...]
        w = w_ref[...]
        row = lax.broadcasted_iota(jnp.int32, a.shape, 0)
        ac = b_ref[...] + _shift_rows(a, 2, row) * w[0:1] + _shift_rows(a, 1, row) * w[1:2] + a * w[2:3]
        y_ref[...] = (ac * _sigmoid(ac) * g_ref[...]).astype(BF16)

    def first_last():
        i = pl.program_id(0)
        return i == 0, i == N_CONV_TILES - 1

    col = lambda off: pl.BlockSpec((s, CONV_TILE), lambda i: (0, i + off))
    (y,), extra = _host_call(
        body, 4, 1, fused, first_last, name="conv_gate_fwd", grid=(N_CONV_TILES,),
        in_specs=[col(0), col(N_CONV_TILES), pl.BlockSpec((3, CONV_TILE), lambda i: (0, i)),
                  pl.BlockSpec((1, CONV_TILE), lambda i: (0, i))],
        out_specs=[col(0)], out_shape=[jax.ShapeDtypeStruct((s, D_FF), BF16)], scratch_shapes=[], sem=("parallel",),
        operands=[u, u, conv_w, conv_b] + list(fused_arrays))
    return y if fused is None else (y, extra)


def _conv_gate_bwd(u, dy, conv_w, conv_b):
    s = u.shape[0]

    def body(a_ref, g_ref, dy_ref, w_ref, b_ref, da_ref, dg_ref, gw_ref, gb_ref):
        a = a_ref[...]
        w = w_ref[...]
        row = lax.broadcasted_iota(jnp.int32, a.shape, 0)
        a1 = _shift_rows(a, 1, row)
        a2 = _shift_rows(a, 2, row)
        ac = b_ref[...] + a2 * w[0:1] + a1 * w[1:2] + a * w[2:3]
        sg = _sigmoid(ac)
        dyv = dy_ref[...].astype(F32)
        dg_ref[...] = (dyv * (ac * sg)).astype(BF16)
        dac = dyv * g_ref[...] * _dsilu(ac, sg)
        gb_ref[...] = jnp.sum(dac, axis=0, keepdims=True)
        gw_ref[0:1, :] = jnp.sum(dac * a2, axis=0, keepdims=True)
        gw_ref[1:2, :] = jnp.sum(dac * a1, axis=0, keepdims=True)
        gw_ref[2:3, :] = jnp.sum(dac * a, axis=0, keepdims=True)
        da = dac * w[2:3] + _shift_rows(dac, -1, row) * w[1:2] + _shift_rows(dac, -2, row) * w[0:1]
        da_ref[...] = da.astype(BF16)

    col = lambda off: pl.BlockSpec((s, CONV_TILE), lambda i: (0, i + off))
    return pl.pallas_call(
        body, name="conv_gate_bwd", grid=(N_CONV_TILES,),
        in_specs=[col(0), col(N_CONV_TILES), col(0), pl.BlockSpec((3, CONV_TILE), lambda i: (0, i)),
                  pl.BlockSpec((1, CONV_TILE), lambda i: (0, i))],
        out_specs=[col(0), col(0), pl.BlockSpec((3, CONV_TILE), lambda i: (0, i)),
                   pl.BlockSpec((1, CONV_TILE), lambda i: (0, i))],
        out_shape=[jax.ShapeDtypeStruct((s, D_FF), BF16), jax.ShapeDtypeStruct((s, D_FF), BF16),
                   jax.ShapeDtypeStruct((3, D_FF), F32), jax.ShapeDtypeStruct((1, D_FF), F32)],
        compiler_params=_params(("parallel",)))(u, u, dy, conv_w, conv_b)


HG_TILE = 256
CHUNK_UNROLL = 8


def _unrolled_loop(n, body, init):
    def group(i, carry):
        for u in range(CHUNK_UNROLL):
            carry = body(i * CHUNK_UNROLL + u, carry)
        return carry

    return lax.fori_loop(0, n // CHUNK_UNROLL, group, init)


def _head_col(off):
    return pl.BlockSpec((SEQ, HEAD_DIM), lambda h: (0, h + off))


def _hgrn_gates(hq, hf, lb, pos):
    q = hq * _sigmoid(hq)
    sig = _sigmoid(hf)
    f = lb + (1.0 - lb) * sig
    gl = jnp.log(f)
    for sh in (1, 2, 4, 8):
        gl = gl + jnp.where(pos >= sh, pltpu.roll(gl, sh, 0), 0.0)
    return q, sig, f, 1.0 - f, gl


def _lower_bound(lbl):
    return 1.0 / (1.0 + jnp.exp(lbl[1:2, :] - lbl[0:1, :]))


def _head_first_last():
    h = pl.program_id(0)
    return h == 0, h == HEADS - 1


CHUNKS_PER_TILE = HG_TILE // CHUNK


def _chunk_end(x, pos):
    y = jnp.where(pos == CHUNK - 1, x, 0.0)
    for sh in (1, 2, 4, 8):
        y = y + jnp.where(pos < CHUNK - sh, pltpu.roll(y, x.shape[0] - sh, 0), 0.0)
    return y


def _suffix_in_chunk(x, pos):
    for sh in (1, 2, 4, 8):
        x = x + jnp.where(pos < CHUNK - sh, pltpu.roll(x, x.shape[0] - sh, 0), 0.0)
    return x


def _prefix_in_chunk(x, pos):
    for sh in (1, 2, 4, 8):
        x = x + jnp.where(pos >= sh, pltpu.roll(x, sh, 0), 0.0)
    return x


def _pair_decays(f, pos):
    shifted = jnp.where(pos >= 1, f, 0.0)
    e = shifted
    yield 1, e
    for d in range(2, CHUNK):
        shifted = pltpu.roll(shifted, 1, 0)
        e = e * shifted
        yield d, e


def _chunk_rows(cc):
    return slice(cc * CHUNK, (cc + 1) * CHUNK)


def _outer_products(lhs_b, rhs_b, dst, i):
    for cc in range(CHUNKS_PER_TILE):
        dst[i * CHUNKS_PER_TILE + cc] = lax.dot_general(lhs_b[_chunk_rows(cc)], rhs_b[_chunk_rows(cc)], TN,
                                                        preferred_element_type=F32)


def _state_scan(n_chunks, gl_s, u_s, keep, reverse):
    def step(k, st):
        c = n_chunks - 1 - k if reverse else k
        keep[c] = st.astype(BF16)
        gl = gl_s[pl.ds(pl.multiple_of(c * CHUNK, CHUNK), CHUNK), :]
        return st * jnp.exp(gl[CHUNK - 1:CHUNK, :]) + u_s[c]

    _unrolled_loop(n_chunks, step, jnp.zeros((HEAD_DIM, HEAD_DIM), F32))


def _hgrn_fwd(proj, lb_logits, norm_w, fused=None, fused_arrays=()):
    n_tiles = SEQ // HG_TILE
    n_chunks = SEQ // CHUNK
    fused_arrays = list(fused_arrays)

    def body(hq_ref, hf_ref, hi_ref, hg_ref, lbl_ref, nw_ref, aout_ref, opre_ref, qt_s, gl_s, u_s, st_s):
        lb = _lower_bound(lbl_ref[...])
        ones = jnp.ones((HEAD_DIM, HEAD_DIM), BF16)
        pos = lax.broadcasted_iota(jnp.int32, (HG_TILE, HEAD_DIM), 0) % CHUNK

        def tile(i, carry):
            rows = pl.ds(pl.multiple_of(i * HG_TILE, HG_TILE), HG_TILE)
            v = hi_ref[rows, :]
            q, _sig, f, kk, gl = _hgrn_gates(hq_ref[rows, :], hf_ref[rows, :], lb, pos)
            o = _lane_sum(q * kk, ones) * v
            for d, e in _pair_decays(f, pos):
                o = o + _lane_sum(q * pltpu.roll(kk, d, 0) * e, ones) * pltpu.roll(v, d, 0)
            opre_ref[rows, :] = o
            qt_s[rows, :] = q * jnp.exp(gl)
            gl_s[rows, :] = gl
            kt = kk * jnp.exp(_chunk_end(gl, pos) - gl)
            _outer_products(v.astype(BF16), kt.astype(BF16), u_s, i)
            return carry

        lax.fori_loop(0, n_tiles, tile, 0)
        _state_scan(n_chunks, gl_s, u_s, st_s, reverse=False)

        def finish(i, carry):
            rows = pl.ds(pl.multiple_of(i * HG_TILE, HG_TILE), HG_TILE)
            qt_b = qt_s[rows, :].astype(BF16)
            past = [lax.dot_general(qt_b[_chunk_rows(cc)], st_s[i * CHUNKS_PER_TILE + cc], NT,
                                    preferred_element_type=F32) for cc in range(CHUNKS_PER_TILE)]
            o = opre_ref[rows, :] + jnp.concatenate(past, axis=0)
            opre_ref[rows, :] = o
            hg = hg_ref[rows, :]
            rs = lax.rsqrt(jnp.mean(o * o, axis=-1, keepdims=True) + EPS)
            aout_ref[rows, :] = ((o * rs) * nw_ref[...] * (hg * _sigmoid(hg))).astype(BF16)
            return carry

        lax.fori_loop(0, n_tiles, finish, 0)

    return _host_call(
        body, 6, 2, fused, _head_first_last, name="hgrn_fwd", grid=(HEADS,),
        in_specs=[_head_col(0), _head_col(HEADS), _head_col(2 * HEADS), _head_col(3 * HEADS),
                  pl.BlockSpec((2, HEAD_DIM), lambda h: (0, h)), pl.BlockSpec((1, HEAD_DIM), lambda h: (0, 0))],
        out_specs=[_head_col(0), _head_col(0)],
        out_shape=[jax.ShapeDtypeStruct((SEQ, HEADS * HEAD_DIM), BF16), jax.ShapeDtypeStruct((SEQ, HEADS * HEAD_DIM), F32)],
        scratch_shapes=[pltpu.VMEM((SEQ, HEAD_DIM), F32)] * 2 + [pltpu.VMEM((n_chunks, HEAD_DIM, HEAD_DIM), F32),
                                                                 pltpu.VMEM((n_chunks, HEAD_DIM, HEAD_DIM), BF16)],
        sem=("parallel",), operands=[proj, proj, proj, proj, lb_logits, norm_w] + fused_arrays)


def _hgrn_bwd(proj, lb_logits, norm_w, o_pre, d_aout, fused=None, fused_arrays=()):
    n_tiles = SEQ // HG_TILE
    n_chunks = SEQ // CHUNK

    def body(hq_ref, hf_ref, hi_ref, hg_ref, lbl_ref, nw_ref, opre_ref, da_ref,
             dhq_ref, dhf_ref, dhi_ref, dhg_ref, dlog_ref, gnw_ref,
             q_s, k_s, gl_s, do_s, dq_s, dk_s, dv_s, u_s, st_s, rt_s):
        h = pl.program_id(0)
        lb = _lower_bound(lbl_ref[...])
        nw = nw_ref[...]
        ones = jnp.ones((HEAD_DIM, HEAD_DIM), BF16)
        pos = lax.broadcasted_iota(jnp.int32, (HG_TILE, HEAD_DIM), 0) % CHUNK

        @pl.when(h == 0)
        def _():
            gnw_ref[...] = jnp.zeros_like(gnw_ref)

        def tile(i, carry):
            rows = pl.ds(pl.multiple_of(i * HG_TILE, HG_TILE), HG_TILE)
            v = hi_ref[rows, :]
            q, _sig, f, kk, gl = _hgrn_gates(hq_ref[rows, :], hf_ref[rows, :], lb, pos)
            o = opre_ref[rows, :]
            hg = hg_ref[rows, :]
            da = da_ref[rows, :]
            rs = lax.rsqrt(jnp.mean(o * o, axis=-1, keepdims=True) + EPS)
            oh = o * rs
            sg = _sigmoid(hg)
            dnorm = da * (hg * sg)
            dhg_ref[rows, :] = (da * (oh * nw) * _dsilu(hg, sg)).astype(BF16)
            gnw_ref[...] += jnp.sum(dnorm * oh, axis=0, keepdims=True)
            doh = dnorm * nw
            do = rs * (doh - oh * jnp.mean(doh * oh, axis=-1, keepdims=True))

            d_a = _lane_sum(do * v, ones)
            dq = d_a * kk
            dk = d_a * q
            dv = _lane_sum(q * kk, ones) * do
            for d, e in _pair_decays(f, pos):
                ks = pltpu.roll(kk, d, 0)
                a_d = _lane_sum(q * ks * e, ones)
                d_a = _lane_sum(do * pltpu.roll(v, d, 0), ones) * e
                dq = dq + d_a * ks
                dk = dk + pltpu.roll(d_a * q, HG_TILE - d, 0)
                dv = dv + pltpu.roll(a_d * do, HG_TILE - d, 0)
            q_s[rows, :] = q
            k_s[rows, :] = kk
            gl_s[rows, :] = gl
            do_s[rows, :] = do
            dq_s[rows, :] = dq
            dk_s[rows, :] = dk
            dv_s[rows, :] = dv
            kt = kk * jnp.exp(_chunk_end(gl, pos) - gl)
            _outer_products(v.astype(BF16), kt.astype(BF16), u_s, i)
            return carry

        lax.fori_loop(0, n_tiles, tile, 0)
        _state_scan(n_chunks, gl_s, u_s, st_s, reverse=False)

        def reverse_increments(i, carry):
            rows = pl.ds(pl.multiple_of(i * HG_TILE, HG_TILE), HG_TILE)
            qt = q_s[rows, :] * jnp.exp(gl_s[rows, :])
            _outer_products(do_s[rows, :].astype(BF16), qt.astype(BF16), u_s, i)
            return carry

        lax.fori_loop(0, n_tiles, reverse_increments, 0)
        _state_scan(n_chunks, gl_s, u_s, rt_s, reverse=True)

        def finish(i, dlb):
            rows = pl.ds(pl.multiple_of(i * HG_TILE, HG_TILE), HG_TILE)
            q = q_s[rows, :]
            kk = k_s[rows, :]
            gl = gl_s[rows, :]
            gll = _chunk_end(gl, pos)
            ekt = jnp.exp(gll - gl)
            do_b = do_s[rows, :].astype(BF16)
            v_b = hi_ref[rows, :].astype(BF16)
            kt_b = (kk * ekt).astype(BF16)
            dq_far, dk_far, dv_far, across = [], [], [], []
            for cc in range(CHUNKS_PER_TILE):
                st = st_s[i * CHUNKS_PER_TILE + cc]
                rt = rt_s[i * CHUNKS_PER_TILE + cc]
                sl = _chunk_rows(cc)
                dq_far.append(jnp.dot(do_b[sl], st, preferred_element_type=F32))
                dk_far.append(jnp.dot(v_b[sl], rt, preferred_element_type=F32))
                dv_far.append(lax.dot_general(kt_b[sl], rt, NT, preferred_element_type=F32))
                both = jnp.sum(st.astype(F32) * rt.astype(F32), axis=0, keepdims=True)
                across.append(jnp.broadcast_to(both, (CHUNK, HEAD_DIM)))
            dq = dq_s[rows, :] + jnp.concatenate(dq_far, axis=0) * jnp.exp(gl)
            dk_in = dk_s[rows, :]
            dk_out = jnp.concatenate(dk_far, axis=0) * ekt
            dk = dk_in + dk_out
            dv = dv_s[rows, :] + jnp.concatenate(dv_far, axis=0)
            pc = kk * dk_out
            dgl = (_suffix_in_chunk(q * dq - kk * dk_in, pos) + (_prefix_in_chunk(pc, pos) - pc)
                   + jnp.concatenate(across, axis=0) * jnp.exp(gll))
            hf = hf_ref[rows, :]
            sig = _sigmoid(hf)
            f = lb + (1.0 - lb) * sig
            df = dgl / f - dk
            dhf_ref[rows, :] = (df * (1.0 - lb) * sig * (1.0 - sig)).astype(BF16)
            hq = hq_ref[rows, :]
            dhq_ref[rows, :] = (dq * _dsilu(hq, _sigmoid(hq))).astype(BF16)
            dhi_ref[rows, :] = dv.astype(BF16)
            return dlb + jnp.sum(df * (1.0 - sig), axis=0, keepdims=True)

        dlb = lax.fori_loop(0, n_tiles, finish, jnp.zeros((1, HEAD_DIM), F32))
        dl0 = lb * (1.0 - lb) * dlb
        dlog_ref[0:1, :] = dl0
        dlog_ref[1:2, :] = -dl0

    wide = HEADS * HEAD_DIM
    return _host_call(
        body, 8, 6, fused, _head_first_last, name="hgrn_bwd", grid=(HEADS,),
        in_specs=[_head_col(0), _head_col(HEADS), _head_col(2 * HEADS), _head_col(3 * HEADS),
                  pl.BlockSpec((2, HEAD_DIM), lambda h: (0, h)), pl.BlockSpec((1, HEAD_DIM), lambda h: (0, 0)),
                  _head_col(0), _head_col(0)],
        out_specs=[_head_col(0)] * 4 + [pl.BlockSpec((2, HEAD_DIM), lambda h: (0, h)),
                                        pl.BlockSpec((1, HEAD_DIM), lambda h: (0, 0))],
        out_shape=[jax.ShapeDtypeStruct((SEQ, wide), BF16)] * 4 + [jax.ShapeDtypeStruct((2, wide), F32),
                                                                    jax.ShapeDtypeStruct((1, HEAD_DIM), F32)],
        scratch_shapes=[pltpu.VMEM((SEQ, HEAD_DIM), F32)] * 7 + [pltpu.VMEM((n_chunks, HEAD_DIM, HEAD_DIM), F32),
                                                                 pltpu.VMEM((n_chunks, HEAD_DIM, HEAD_DIM), BF16),
                                                                 pltpu.VMEM((n_chunks, HEAD_DIM, HEAD_DIM), BF16)],
        sem=("arbitrary",),
        operands=[proj, proj, proj, proj, lb_logits, norm_w, o_pre, d_aout] + list(fused_arrays))


Q_TILE = 512
ATT_SCALE = HEAD_DIM ** -0.5
ATT_OFF = 4 * HEADS


def _qk_prep(proj, q_w, k_w, fused=None, fused_arrays=()):
    def body(aq_ref, ak_ref, av_ref, qw_ref, kw_ref, qn_ref, kn_ref, v_ref):
        aq = aq_ref[...]
        ak = ak_ref[...]
        qn_ref[...] = (aq * lax.rsqrt(jnp.mean(aq * aq, axis=-1, keepdims=True) + EPS) * qw_ref[...]).astype(BF16)
        kn_ref[...] = (ak * lax.rsqrt(jnp.mean(ak * ak, axis=-1, keepdims=True) + EPS) * kw_ref[...]).astype(BF16)
        v_ref[...] = av_ref[...].astype(BF16)

    wide = HEADS * HEAD_DIM
    vec = pl.BlockSpec((1, HEAD_DIM), lambda h: (0, 0))
    return _host_call(
        body, 5, 3, fused, _head_first_last, name="qk_prep", grid=(HEADS,),
        in_specs=[_head_col(ATT_OFF), _head_col(ATT_OFF + HEADS), _head_col(ATT_OFF + 2 * HEADS), vec, vec],
        out_specs=[_head_col(0)] * 3, out_shape=[jax.ShapeDtypeStruct((SEQ, wide), BF16)] * 3,
        scratch_shapes=[], sem=("parallel",), operands=[proj, proj, proj, q_w, k_w] + list(fused_arrays))


def _alibi_slopes():
    slopes = np.exp2(-8.0 * np.arange(1, HEADS + 1, dtype=np.float32) / HEADS).astype(np.float32)
    return np.broadcast_to(slopes[:, None, None], (HEADS, 1, HEAD_DIM))


SLOPE_SPEC = pl.BlockSpec((None, 1, HEAD_DIM), lambda h, i: (h, 0, 0))


N_Q_TILES = SEQ // Q_TILE
K_BLOCK = 512
NOT_ATTENDED = 1e35


def _att_tables():
    o = np.arange(N_Q_TILES, dtype=np.int32)[:, None, None]
    r = np.arange(Q_TILE, dtype=np.int32)[None, :, None]
    c = np.arange(K_BLOCK, dtype=np.int32)[None, None, :]
    dist = o * Q_TILE + r - c
    mult = ((dist <= 128).astype(np.float32) + (((dist % 4) == 0) & (dist <= 512)).astype(np.float32)
            + ((dist % 16) == 0).astype(np.float32))
    valid = (dist >= 0) & (mult > 0)
    return (np.where(valid, dist.astype(np.float32), np.float32(NOT_ATTENDED)).astype(np.float32),
            np.where(valid, np.log(np.maximum(mult, 1.0)), 0.0).astype(np.float32))


TABLE_SPEC = pl.BlockSpec((N_Q_TILES, Q_TILE, K_BLOCK), lambda h, i: (0, 0, 0))


def _att_block(q, k_ref, j, i, slope, dist_ref, lmul_ref):
    rows = pl.ds(pl.multiple_of(j * K_BLOCK, K_BLOCK), K_BLOCK)
    off = i - j * (K_BLOCK // Q_TILE)
    s = lax.dot_general(q, k_ref[rows, :], NT, preferred_element_type=F32) * ATT_SCALE
    return s - slope * dist_ref[off] + lmul_ref[off], rows


def _n_key_blocks(i):
    return (i + K_BLOCK // Q_TILE) // (K_BLOCK // Q_TILE)


def _att_first_last():
    h, i = pl.program_id(0), pl.program_id(1)
    return (h == 0) & (i == 0), (h == HEADS - 1) & (i == N_Q_TILES - 1)


def _attn_fwd(qn, kn, vb, fused=None, fused_arrays=()):
    def body(q_ref, k_ref, v_ref, sl_ref, dist_ref, lmul_ref, o_ref, lse_ref):
        i = pl.program_id(1)
        q = q_ref[...]
        slope = sl_ref[0:1, 0:1]

        def step(j, carry):
            m, l, acc = carry
            sb, rows = _att_block(q, k_ref, j, i, slope, dist_ref, lmul_ref)
            m_new = jnp.maximum(m, jnp.max(sb, axis=-1, keepdims=True))
            alpha = jnp.exp(m - m_new)
            p = jnp.exp(sb - m_new)
            l = alpha * l + jnp.sum(p, axis=-1, keepdims=True)
            acc = alpha * acc + jnp.dot(p.astype(BF16), v_ref[rows, :], preferred_element_type=F32)
            return m_new, l, acc

        m, l, acc = lax.fori_loop(0, _n_key_blocks(i), step,
                                  (jnp.full((Q_TILE, 1), -1e30, F32), jnp.zeros((Q_TILE, 1), F32),
                                   jnp.zeros((Q_TILE, HEAD_DIM), F32)))
        o_ref[...] = acc / l
        lse_ref[...] = m + jnp.log(l)

    wide = HEADS * HEAD_DIM
    qt = pl.BlockSpec((Q_TILE, HEAD_DIM), lambda h, i: (i, h))
    full = pl.BlockSpec((SEQ, HEAD_DIM), lambda h, i: (0, h))
    return _host_call(
        body, 6, 2, fused, _att_first_last, name="attn_fwd", grid=(HEADS, N_Q_TILES),
        in_specs=[qt, full, full, SLOPE_SPEC, TABLE_SPEC, TABLE_SPEC],
        out_specs=[qt, pl.BlockSpec((None, Q_TILE, 1), lambda h, i: (h, i, 0))],
        out_shape=[jax.ShapeDtypeStruct((SEQ, wide), F32), jax.ShapeDtypeStruct((HEADS, SEQ, 1), F32)],
        scratch_shapes=[], sem=("parallel", "parallel"),
        operands=[qn, kn, vb, _alibi_slopes(), *_att_tables()] + list(fused_arrays))


def _attn_bwd(qn, kn, vb, o, lse, d_mix, fused=None, fused_arrays=()):
    def body(q_ref, k_ref, v_ref, o_ref, lse_ref, do_ref, sl_ref, dist_ref, lmul_ref, dq_ref, dk_ref, dv_ref):
        i = pl.program_id(1)
        q = q_ref[...]
        do = do_ref[...]
        do_b = do.astype(BF16)
        slope = sl_ref[0:1, 0:1]
        lse = lse_ref[...]
        delta = jnp.sum(do * o_ref[...], axis=-1, keepdims=True)

        @pl.when(i == 0)
        def _():
            dk_ref[...] = jnp.zeros_like(dk_ref)
            dv_ref[...] = jnp.zeros_like(dv_ref)

        def step(j, dq):
            sb, rows = _att_block(q, k_ref, j, i, slope, dist_ref, lmul_ref)
            p = jnp.exp(sb - lse)
            dp = lax.dot_general(do_b, v_ref[rows, :], NT, preferred_element_type=F32)
            ds = (p * (dp - delta)).astype(BF16)
            dk_ref[rows, :] += lax.dot_general(ds, q, TN, preferred_element_type=F32) * ATT_SCALE
            dv_ref[rows, :] += lax.dot_general(p.astype(BF16), do_b, TN, preferred_element_type=F32)
            return dq + jnp.dot(ds, k_ref[rows, :], preferred_element_type=F32)

        dq = lax.fori_loop(0, _n_key_blocks(i), step, jnp.zeros((Q_TILE, HEAD_DIM), F32))
        dq_ref[...] = dq * ATT_SCALE

    wide = HEADS * HEAD_DIM
    qt = pl.BlockSpec((Q_TILE, HEAD_DIM), lambda h, i: (i, h))
    full = pl.BlockSpec((SEQ, HEAD_DIM), lambda h, i: (0, h))
    return _host_call(
        body, 9, 3, fused, _att_first_last, name="attn_bwd", grid=(HEADS, N_Q_TILES),
        in_specs=[qt, full, full, qt, pl.BlockSpec((None, Q_TILE, 1), lambda h, i: (h, i, 0)),
                  pl.BlockSpec((Q_TILE, HEAD_DIM), lambda h, i: (i, h + HEADS)), SLOPE_SPEC, TABLE_SPEC, TABLE_SPEC],
        out_specs=[qt, full, full], out_shape=[jax.ShapeDtypeStruct((SEQ, wide), F32)] * 3,
        scratch_shapes=[], sem=("parallel", "arbitrary"),
        operands=[qn, kn, vb, o, lse, d_mix, _alibi_slopes(), *_att_tables()] + list(fused_arrays))


def _qk_bwd(proj, q_w, k_w, dqn, dkn, dv):
    def body(aq_ref, ak_ref, qw_ref, kw_ref, dqn_ref, dkn_ref, dv_ref, daq_ref, dak_ref, dav_ref, gq_ref, gk_ref):
        h = pl.program_id(0)

        @pl.when(h == 0)
        def _():
            gq_ref[...] = jnp.zeros_like(gq_ref)
            gk_ref[...] = jnp.zeros_like(gk_ref)

        def one(a_ref, w_ref, d_ref, da_ref, g_ref):
            a = a_ref[...]
            d = d_ref[...]
            rs = lax.rsqrt(jnp.mean(a * a, axis=-1, keepdims=True) + EPS)
            ah = a * rs
            g_ref[...] += jnp.sum(d * ah, axis=0, keepdims=True)
            dah = d * w_ref[...]
            da_ref[...] = (rs * (dah - ah * jnp.mean(dah * ah, axis=-1, keepdims=True))).astype(BF16)

        one(aq_ref, qw_ref, dqn_ref, daq_ref, gq_ref)
        one(ak_ref, kw_ref, dkn_ref, dak_ref, gk_ref)
        dav_ref[...] = dv_ref[...].astype(BF16)

    wide = HEADS * HEAD_DIM
    vec = pl.BlockSpec((1, HEAD_DIM), lambda h: (0, 0))
    return pl.pallas_call(
        body, name="qk_bwd", grid=(HEADS,),
        in_specs=[_head_col(ATT_OFF), _head_col(ATT_OFF + HEADS), vec, vec, _head_col(0), _head_col(0), _head_col(0)],
        out_specs=[_head_col(0)] * 3 + [vec, vec],
        out_shape=[jax.ShapeDtypeStruct((SEQ, wide), BF16)] * 3 + [jax.ShapeDtypeStruct((1, HEAD_DIM), F32)] * 2,
        compiler_params=_params(("arbitrary",)))(proj, proj, q_w, k_w, dqn, dkn, dv)


def _pair_sum(name, partial, theirs, core):
    _, r, c = theirs.shape
    tr = r // 2 if r % 16 == 0 else r

    def body(core_ref, a_ref, b_ref, o_ref):
        o_ref[...] = (a_ref[...].astype(F32) + b_ref[...].astype(F32)).astype(BF16)

    spec = pl.BlockSpec((None, tr, c), lambda q, i, core_ref: (q, i, 0))
    grid_spec = pltpu.PrefetchScalarGridSpec(
        num_scalar_prefetch=1, grid=(4, r // tr),
        in_specs=[pl.BlockSpec((None, tr, c), lambda q, i, core_ref: (2 * q + core_ref[0], i, 0)), spec],
        out_specs=spec)
    return pl.pallas_call(body, name=name, grid_spec=grid_spec, out_shape=jax.ShapeDtypeStruct(theirs.shape, BF16),
                          compiler_params=_params(("parallel", "parallel")))(core, partial, theirs)


def _adamw_step(w, m, v, g):
    nm = ADAM_B1 * m + (1.0 - ADAM_B1) * g
    nv = ADAM_B2 * v + (1.0 - ADAM_B2) * (g * g)
    m_hat = nm / (1.0 - ADAM_B1 ** ADAM_STEP)
    v_hat = nv / (1.0 - ADAM_B2 ** ADAM_STEP)
    return -ADAM_LR * (m_hat / (jnp.sqrt(v_hat) + ADAM_EPS) + ADAM_WD * w), nm, nv


def _adamw(name, w, m, v, addends, tr=None):
    r, c = w.shape
    tr = r if tr is None else tr
    n_add = len(addends)

    def body(*refs):
        w_ref, m_ref, v_ref = refs[:3]
        add_refs = refs[3:3 + n_add]
        g_ref, d_ref, nm_ref, nv_ref = refs[3 + n_add:]
        g = add_refs[0][...].astype(F32)
        for a_ref in add_refs[1:]:
            g = g + a_ref[...].astype(F32)
        g_ref[...] = g
        d_ref[...], nm_ref[...], nv_ref[...] = _adamw_step(w_ref[...], m_ref[...], v_ref[...], g)

    spec = pl.BlockSpec((tr, c), lambda i: (i, 0))
    out = jax.ShapeDtypeStruct((r, c), F32)
    return pl.pallas_call(body, name=name, grid=(r // tr,), in_specs=[spec] * (3 + n_add), out_specs=[spec] * 4,
                          out_shape=[out] * 4, compiler_params=_params(("parallel",)))(w, m, v, *addends)


def _adamw_reduced(name, w, m, v, chip_sums, received, chip, tr):
    r, c = w.shape

    def body(chip_ref, w_ref, m_ref, v_ref, own_ref, r0_ref, r1_ref, r2_ref, g_ref, d_ref, nm_ref, nv_ref):
        g = ((own_ref[...].astype(F32) + r0_ref[...].astype(F32)) + r1_ref[...].astype(F32)) + r2_ref[...].astype(F32)
        g_ref[...] = g
        d_ref[...], nm_ref[...], nv_ref[...] = _adamw_step(w_ref[...], m_ref[...], v_ref[...], g)

    spec = pl.BlockSpec((tr, c), lambda i, chip_ref: (i, 0))

    def slot(k):
        return pl.BlockSpec((None, tr, c), lambda i, chip_ref: (k, i, 0))

    grid_spec = pltpu.PrefetchScalarGridSpec(
        num_scalar_prefetch=1, grid=(r // tr,),
        in_specs=[spec, spec, spec, pl.BlockSpec((None, tr, c), lambda i, chip_ref: (chip_ref[0], i, 0)),
                  slot(0), slot(1), slot(2)],
        out_specs=[spec] * 4)
    out = jax.ShapeDtypeStruct((r, c), F32)
    return pl.pallas_call(body, name=name, grid_spec=grid_spec, out_shape=[out] * 4,
                          compiler_params=_params(("parallel",)))(chip, w, m, v, chip_sums, received, received, received)


def _sum_devices(gathered):
    _, r, c = gathered.shape

    def body(g_ref, o_ref):
        acc = g_ref[0]
        for d in range(1, N_DEV):
            acc = acc + g_ref[d]
        o_ref[...] = acc

    return pl.pallas_call(body, name="sum_devices", out_shape=jax.ShapeDtypeStruct((r, c), F32))(gathered)


def _pack_rows(vectors, rows):
    flat = jnp.concatenate([v.reshape(-1) for v in vectors])
    return jnp.pad(flat, (0, rows * 128 - flat.shape[0])).reshape(rows, 128)


def _unpack(flat, shapes):
    out, off = [], 0
    for shp in shapes:
        n = 1
        for d in shp:
            n *= d
        out.append(flat[off:off + n].reshape(shp))
        off += n
    return out


def _device_step(xs, tgt, mod, norm1_w, norm2_w, lb_logits, hg_norm_w, q_norm_w, k_norm_w, conv_w_full, conv_b,
                 win_g, w_out_x, w_up_x, w_down_x, core=None):
    fused = core is not None
    shift1, scale1, gate1, shift2, scale2, gate2 = (mod[k] for k in range(6))

    h, rstd1 = _norm_fwd("norm1_fwd", xs, norm1_w, scale1, shift1)
    if fused:
        near = (0, 1, 2)
        head_rows, tail_rows = (0, UP_HEAD_ROWS), (UP_HEAD_ROWS, D_MODEL - UP_HEAD_ROWS)
        proj, (wout_g, wup_g) = _mm_blocked_rhs(
            "mm_in", h, win_g, fused_arrays=[w_out_x, w_up_x],
            fused=[_FusedCopies("gather", [w_out_x]), _FusedCopies("gather", [w_up_x], peers=near, rows=head_rows)])
        (a_out, o_pre), (wup_g,) = _hgrn_fwd(
            proj, lb_logits, hg_norm_w, fused_arrays=[w_up_x, wup_g],
            fused=_FusedCopies("gather_more", [w_up_x, wup_g], peers=near, rows=tail_rows, relay_rows=head_rows))
        wout_g, = _forward_to_sibling("allgather_stage2_out", [wout_g])
        wout_full = wout_g.reshape(D_MODEL, D_MODEL)
        (qn, kn, vb), _ = _qk_prep(proj, q_norm_w, k_norm_w)
        (att_o, lse), (wup_g,) = _attn_fwd(qn, kn, vb, _FusedCopies("relay", [wup_g], rows=tail_rows), [wup_g])
    else:
        proj = _mm_blocked_rhs("mm_in", h, win_g)
        (a_out, o_pre), _ = _hgrn_fwd(proj, lb_logits, hg_norm_w)
        wup_g, wout_full, wdown_full = w_up_x, w_out_x, w_down_x
        (qn, kn, vb), _ = _qk_prep(proj, q_norm_w, k_norm_w)
        (att_o, lse), _ = _attn_fwd(qn, kn, vb)
    mixin = jnp.concatenate([a_out, att_o.astype(BF16)], axis=1)
    if fused:
        down_head = (0, DOWN_HEAD_ROWS)
        mix, (wup_g, wdown_g) = _mm_plain(
            "mm_out", mixin, wout_full, NN, 512, 1024, F32, fused_arrays=[wup_g, w_down_x],
            fused=[_FusedCopies("forward", [wup_g]), _FusedCopies("gather", [w_down_x], rows=down_head)])
    else:
        mix = _mm_plain("mm_out", mixin, wout_full, NN, 512, 1024, F32)
    x1, h2, rstd2 = _norm_fwd("norm2_fwd", xs, norm2_w, scale2, shift2, resid=mix, gate=gate1)
    if fused:
        down_tail = (DOWN_HEAD_ROWS, FF_BLK - DOWN_HEAD_ROWS)
        u, (wdown_g,) = _mm_blocked_rhs(
            "mm_up", h2, wup_g, fused_arrays=[w_down_x, wdown_g],
            fused=_FusedCopies("gather_more", [w_down_x, wdown_g], rows=down_tail))
        y, (wdown_g,) = _conv_gate_fwd(u, conv_w_full, conv_b, _FusedCopies("forward", [wdown_g]), [wdown_g])
        wdown_full = wdown_g.reshape(D_FF, D_MODEL)
    else:
        u = _mm_blocked_rhs("mm_up", h2, wup_g)
        y = _conv_gate_fwd(u, conv_w_full, conv_b)
    ffn = _mm_plain("mm_down", y, wdown_full, NN, MM_TILE, 512, F32)
    loss_v, dout, dffn, dgate2 = _loss_head(x1, ffn, gate2, tgt)

    dy = _mm_plain("mm_down_dx", dffn, wdown_full, NT, MM_TILE, UP_BLK, BF16)
    gw_down = _mm_plain("mm_down_dw", y, dffn, TN, UP_BLK, 1024, BF16)
    da, dg, gconv_w, gconv_b = _conv_gate_bwd(u, dy, conv_w_full, conv_b)
    dh2 = _mm_halves_rhs_t("mm_up_dx", da, dg, wup_g)
    gw_up = _mm_halves_wgrad("mm_up_dw", h2, da, dg)
    if fused:
        part_up, part_down = gw_up, gw_down.reshape(N_DEV, FF_BLK, D_MODEL)
        (dx1, dmix, dshift2, dscale2, gnorm2, dgate1), (sib_up,) = _norm_bwd(
            "norm2_bwd", dh2, x1, rstd2, norm2_w, scale2, dout, mix=mix, gate=gate1,
            fused=_FusedCopies("sibling", [part_up]), fused_arrays=[part_up])
    else:
        dx1, dmix, dshift2, dscale2, gnorm2, dgate1 = _norm_bwd(
            "norm2_bwd", dh2, x1, rstd2, norm2_w, scale2, dout, mix=mix, gate=gate1)
    gw_out = _mm_plain("mm_out_dw", mixin, dmix, TN, 512, 1024, BF16)
    if fused:
        part_out = gw_out.reshape(N_DEV, OUT_BLK, D_MODEL)
        dmixin, (sib_out, sib_down) = _mm_plain(
            "mm_out_dx", dmix, wout_full, NT, 512, 1024, F32,
            fused=_FusedCopies("sibling", [part_out, part_down]), fused_arrays=[part_out, part_down])
        cs_up = _pair_sum("grad_pair_sum_up", part_up, sib_up, core)
        cs_out = _pair_sum("grad_pair_sum_out", part_out, sib_out, core)
        cs_down = _pair_sum("grad_pair_sum_down", part_down, sib_down, core)
        (dhq, dhf, dhi, dhg, glog, ghg), (fc_up,) = _hgrn_bwd(
            proj, lb_logits, hg_norm_w, o_pre, dmixin, _FusedCopies("chips", [cs_up]), [cs_up])
        (dqn, dkn, dvv), (fc_down,) = _attn_bwd(qn, kn, vb, att_o, lse, dmixin,
                                                _FusedCopies("chips", [cs_down]), [cs_down])
    else:
        dmixin = _mm_plain("mm_out_dx", dmix, wout_full, NT, 512, 1024, F32)
        (dhq, dhf, dhi, dhg, glog, ghg), _ = _hgrn_bwd(proj, lb_logits, hg_norm_w, o_pre, dmixin)
        (dqn, dkn, dvv), _ = _attn_bwd(qn, kn, vb, att_o, lse, dmixin)
    daq, dak, dav, gqw, gkw = _qk_bwd(proj, q_norm_w, k_norm_w, dqn, dkn, dvv)
    dproj = jnp.concatenate([dhq, dhf, dhi, dhg, daq, dak, dav], axis=1)
    if fused:
        gw_in, (fc_out,) = _mm_wgrad_blocked("mm_in_dw", h, dproj, fused=_FusedCopies("chips", [cs_out]),
                                             fused_arrays=[cs_out])
        from_sibling, = _exchange_sibling("grad_exchange_sibling_b", [gw_in])
        cs_in = _pair_sum("grad_pair_sum_in", gw_in, from_sibling, core)
        dh, (fc_in,) = _mm_blocked_rhs_t("mm_in_dx", dproj, win_g, fused=_FusedCopies("chips", [cs_in]),
                                         fused_arrays=[cs_in])
        large = [(cs_in, fc_in), (cs_out, fc_out), (cs_up, fc_up), (cs_down, fc_down)]
    else:
        gw_in = _mm_wgrad_blocked("mm_in_dw", h, dproj)
        dh = _mm_blocked_rhs_t("mm_in_dx", dproj, win_g)
        large = [gw_in, gw_out, gw_up, gw_down]
    grad_x, dshift1, dscale1, gnorm1 = _norm_bwd("norm1_bwd", dh, xs, rstd1, norm1_w, scale1, dx1)
    gmod = jnp.concatenate([dshift1, dscale1, dgate1, dshift2, dscale2, dgate2], axis=1)
    return (loss_v, grad_x, gmod, gnorm1, gnorm2, glog, ghg, gqw, gkw, gconv_b, gconv_w, *large)


def kernel(x, c, w_ada, b_ada, norm1_w, w_in, lb_logits, hg_norm_w, q_norm_w, k_norm_w, w_out, norm2_w, w_up, conv_w, conv_b, w_down, loss_target, m_w_ada, m_b_ada, m_norm1_w, m_w_in, m_lb_logits, m_hg_norm_w, m_q_norm_w, m_k_norm_w, m_w_out, m_norm2_w, m_w_up, m_conv_w, m_conv_b, m_w_down, v_w_ada, v_b_ada, v_norm1_w, v_w_in, v_lb_logits, v_hg_norm_w, v_q_norm_w, v_k_norm_w, v_w_out, v_norm2_w, v_w_up, v_conv_w, v_conv_b, v_w_down):
    ix, iy, ic = lax.axis_index("x"), lax.axis_index("y"), lax.axis_index("c")
    me = 4 * ix + 2 * iy + ic
    my_chip = 2 * ix + iy

    xs = x[0]
    tgt = loss_target[0]

    win_g, = _allgather_weights([w_in[0].astype(BF16)])

    first = _allgather_vmem(_pack_rows([c, conv_w[0]], 40), "allgather_c_conv_w").reshape(N_DEV, 40 * 128)
    c_all = first[:, :D_MODEL]
    conv_w_full = (first[:, D_MODEL:D_MODEL + 3 * FF_BLK].reshape(N_DEV, 3, FF_BLK).transpose(1, 0, 2)
                   .reshape(3, D_FF))

    b_blk = lax.dynamic_slice_in_dim(b_ada, me * ADA_BLK, ADA_BLK, axis=1)
    mod_cols = _ada_fwd(c_all, w_ada[0], b_blk)
    mod_all = _allgather_vmem(mod_cols, "allgather_mod").reshape(N_DEV, N_DEV, ADA_BLK)
    mod = lax.dynamic_index_in_dim(mod_all, me, axis=1, keepdims=False).reshape(6, 1, D_MODEL)

    (loss_v, grad_x, gmod, gnorm1, gnorm2, glog, ghg, gqw, gkw, gconv_b, gconv_w,
     rs_in, rs_out, rs_up, rs_down) = _device_step(
        xs, tgt, mod, norm1_w, norm2_w, lb_logits, hg_norm_w, q_norm_w, k_norm_w, conv_w_full, conv_b,
        win_g, w_out[0].astype(BF16), w_up[0].astype(BF16), w_down[0].astype(BF16),
        core=jnp.reshape(ic, (1,)).astype(jnp.int32))

    small_shapes = [(1, 6 * D_MODEL), (1, D_MODEL), (1, D_MODEL), (2, HEADS * HEAD_DIM), (1, HEAD_DIM),
                    (1, HEAD_DIM), (1, HEAD_DIM), (1, D_FF), (3, D_FF), (1, 1)]
    small = [gmod, gnorm1, gnorm2, glog, ghg, gqw, gkw, gconv_b, gconv_w, loss_v[:, 0:1]]
    n_small = sum(a.size for a in small)
    rows = -(-n_small // 1024) * 8
    gathered = _allgather_vmem(_pack_rows(small, rows), "allgather_small").reshape(N_DEV, rows, 128)
    summed = _sum_devices(gathered).reshape(-1)
    (g_b_ada, g_norm1, g_norm2, g_lb, g_hg, g_q, g_k, g_conv_b, g_conv_w_full, loss_sum) = _unpack(summed, small_shapes)
    loss = loss_sum[0, 0]
    g_conv_w = lax.dynamic_slice_in_dim(g_conv_w_full, me * FF_BLK, FF_BLK, axis=1)

    gmod_all = gathered[:, :6 * D_MODEL // 128, :].reshape(N_DEV, 6 * D_MODEL)
    gmod_cols = lax.dynamic_slice_in_dim(gmod_all, me * ADA_BLK, ADA_BLK, axis=1)
    g_w_ada_raw = _ada_wgrad(c_all, gmod_cols)

    chip = jnp.reshape(my_chip, (1,)).astype(jnp.int32)

    def big_update(name, w, m, v, rs, tr):
        chip_sums, received = rs
        return _adamw_reduced(name, w[0], m[0], v[0], chip_sums, received, chip, tr)

    r_in = big_update("adamw_w_in", w_in, m_w_in, v_w_in, rs_in, 256)
    r_out = big_update("adamw_w_out", w_out, m_w_out, v_w_out, rs_out, 128)
    r_up = big_update("adamw_w_up", w_up, m_w_up, v_w_up, rs_up, 256)
    r_down = big_update("adamw_w_down", w_down, m_w_down, v_w_down, rs_down, 176)
    r_ada = _adamw("adamw_w_ada", w_ada[0], m_w_ada[0], v_w_ada[0], [g_w_ada_raw], tr=256)
    r_convw = _adamw("adamw_conv_w", conv_w[0], m_conv_w[0], v_conv_w[0], [g_conv_w])

    rep_shapes = [(1, 6 * D_MODEL), (1, D_MODEL), (1, D_MODEL), (2, HEADS * HEAD_DIM), (1, HEAD_DIM),
                  (1, HEAD_DIM), (1, HEAD_DIM), (1, D_FF)]
    rep_rows = -(-sum(a * b for a, b in rep_shapes) // 1024) * 8
    pack = lambda arrs: _pack_rows(arrs, rep_rows)
    rep = _adamw("adamw_small",
                 pack([b_ada, norm1_w, norm2_w, lb_logits, hg_norm_w, q_norm_w, k_norm_w, conv_b]),
                 pack([m_b_ada, m_norm1_w, m_norm2_w, m_lb_logits, m_hg_norm_w, m_q_norm_w, m_k_norm_w, m_conv_b]),
                 pack([v_b_ada, v_norm1_w, v_norm2_w, v_lb_logits, v_hg_norm_w, v_q_norm_w, v_k_norm_w, v_conv_b]),
                 [pack([g_b_ada, g_norm1, g_norm2, g_lb, g_hg, g_q, g_k, g_conv_b])])
    rep = [_unpack(r.reshape(-1), rep_shapes) for r in rep]

    def big(r):
        return [a[None] for a in r]

    order = {"w_ada": big(r_ada), "b_ada": [r[0] for r in rep], "norm1_w": [r[1] for r in rep],
             "w_in": big(r_in), "lb_logits": [r[3] for r in rep], "hg_norm_w": [r[4] for r in rep],
             "q_norm_w": [r[5] for r in rep], "k_norm_w": [r[6] for r in rep], "w_out": big(r_out),
             "norm2_w": [r[2] for r in rep], "w_up": big(r_up), "conv_w": big(r_convw),
             "conv_b": [r[7] for r in rep], "w_down": big(r_down)}
    names = ["w_ada", "b_ada", "norm1_w", "w_in", "lb_logits", "hg_norm_w", "q_norm_w", "k_norm_w", "w_out",
             "norm2_w", "w_up", "conv_w", "conv_b", "w_down"]
    outs = [loss, grad_x[None]]
    for kind in range(4):
        outs += [order[n][kind] for n in names]
    return tuple(outs)
```

```python
import jax
import jax.numpy as jnp
import numpy as np
from jax import lax
from jax.experimental import pallas as pl
from jax.experimental.pallas import tpu as pltpu

F32 = jnp.float32
BF16 = jnp.bfloat16

N_DEV = 8
SEQ = 2048
D_MODEL = 2048
HEADS = 8
HEAD_DIM = 128
IN_COLS = 7168
IN_BLK = IN_COLS // N_DEV
D_FF = 5632
UP_BLK = 2 * D_FF // N_DEV
FF_BLK = D_FF // N_DEV
ADA_BLK = 6 * D_MODEL // N_DEV
OUT_BLK = D_MODEL // N_DEV
EPS = 1e-6
CHUNK = 16
ROW_TILE = 256
MM_TILE = 1024
V7X_VMEM_LIMIT = 56 * 1024 * 1024

ADAM_LR = 0.001
ADAM_B1 = 0.9
ADAM_B2 = 0.999
ADAM_EPS = 1e-08
ADAM_WD = 0.01
ADAM_STEP = 10

NN = (((1,), (0,)), ((), ()))
NT = (((1,), (1,)), ((), ()))
TN = (((0,), (0,)), ((), ()))
MESH = pl.DeviceIdType.MESH


def _params(sem=None, vmem=V7X_VMEM_LIMIT):
    return pltpu.CompilerParams(dimension_semantics=sem, vmem_limit_bytes=vmem)


def _sigmoid(x):
    return 1.0 / (1.0 + jnp.exp(-x))


def _dsilu(x, s):
    return s * (1.0 + x * (1.0 - s))


def _lane_sum(x, ones_bf16):
    return jnp.dot(x.astype(BF16), ones_bf16, preferred_element_type=F32)


def _mesh_pos():
    return lax.axis_index("x"), lax.axis_index("y"), lax.axis_index("c")


def _allgather_vmem(x_blk, name):
    m_per, n = x_blk.shape

    def body(x_ref, out_ref, send_sems, recv_sems, local_sem):
        x, y, c = _mesh_pos()
        me, sibling = (x, y, c), (x, y, 1 - c)
        chips = [(1 - x, y), (x, 1 - y), (1 - x, 1 - y)]

        def rows(px, py, pc):
            return out_ref.at[pl.ds((4 * px + 2 * py + pc) * m_per, m_per), :]

        def copy(k, block, to, src=None):
            return pltpu.make_async_remote_copy(
                src_ref=rows(*block) if src is None else src, dst_ref=rows(*block),
                send_sem=send_sems.at[k], recv_sem=recv_sems.at[k], device_id=to, device_id_type=MESH)

        mine = pltpu.make_async_copy(x_ref, rows(*me), local_sem)
        mine.start()
        first = [copy(0, me, sibling, src=x_ref)]
        first += [copy(1 + j, me, (*chip, c), src=x_ref) for j, chip in enumerate(chips)]
        for cp in first:
            cp.start()
        passed = [copy(4 + j, (*chip, c), sibling) for j, chip in enumerate(chips)]
        for j, chip in enumerate(chips):
            copy(1 + j, (*chip, c), me).wait_recv()
            passed[j].start()
        copy(0, sibling, me).wait_recv()
        for j, chip in enumerate(chips):
            copy(4 + j, (*chip, 1 - c), me).wait_recv()
        for cp in first + passed:
            cp.wait_send()
        mine.wait()

    return pl.pallas_call(
        body, name=name,
        out_shape=jax.ShapeDtypeStruct((N_DEV * m_per, n), x_blk.dtype),
        in_specs=[pl.BlockSpec(memory_space=pltpu.VMEM)],
        out_specs=pl.BlockSpec(memory_space=pltpu.VMEM),
        scratch_shapes=[pltpu.SemaphoreType.DMA((7,)), pltpu.SemaphoreType.DMA((7,)), pltpu.SemaphoreType.DMA],
    )(x_blk)


def _flip(v, bit):
    return v + bit - 2 * v * bit


def _relay_chips(x, y, c):
    return (_flip(x, 1 - c), _flip(y, c)), (_flip(x, c), _flip(y, 1 - c))


DOWN_EXCHANGE_HEAD = 560
DOWN_HEAD_ROWS = 192
UP_HEAD_ROWS = 768
GATHER_PARTS = 4


def _allgather_weights(blocks):
    n_arr = len(blocks)
    parts = GATHER_PARTS

    def body(*refs):
        ins, outs = refs[:n_arr], refs[n_arr:2 * n_arr]
        send_sems, recv_sems, local_sems = refs[2 * n_arr:]
        x, y, c = _mesh_pos()
        me, sibling = (x, y, c), (x, y, 1 - c)
        near = [(1 - x, y), (x, 1 - y)]
        chips = near + [(1 - x, 1 - y)]
        relay_from, relay_to = _relay_chips(x, y, c)

        def rows(a, p):
            hr = ins[a].shape[0] // parts
            return pl.ds(p * hr, hr)

        def slot(a, pos, p):
            return outs[a].at[4 * pos[0] + 2 * pos[1] + pos[2], rows(a, p)]

        def copy(a, k, p, src, lands, to):
            return pltpu.make_async_remote_copy(
                src_ref=src, dst_ref=slot(a, lands, p), send_sem=send_sems.at[a, k, p], recv_sem=recv_sems.at[a, k, p],
                device_id=to, device_id_type=MESH)

        sent = []
        local = [pltpu.make_async_copy(ins[a], outs[a].at[4 * x + 2 * y + c], local_sems.at[a]) for a in range(n_arr)]
        for cp in local:
            cp.start()
        for p in range(parts):
            for a in range(n_arr):
                own = ins[a].at[rows(a, p)]
                sent.append(copy(a, 0, p, own, me, sibling))
                sent += [copy(a, 1 + j, p, own, me, (*chip, c)) for j, chip in enumerate(near)]
        for cp in sent:
            cp.start()

        def start(cp):
            cp.start()
            sent.append(cp)

        for p in range(parts):
            for a in range(n_arr):
                for j, chip in enumerate(near):
                    copy(a, 1 + j, p, ins[a].at[rows(a, p)], (*chip, c), me).wait_recv()
                    start(copy(a, 4 + j, p, slot(a, (*chip, c), p), (*chip, c), sibling))
                start(copy(a, 3, p, slot(a, (*relay_from, c), p), (*relay_from, c), (*relay_to, c)))
        for p in range(parts):
            for a in range(n_arr):
                copy(a, 3, p, ins[a].at[rows(a, p)], (*chips[2], c), me).wait_recv()
                start(copy(a, 6, p, slot(a, (*chips[2], c), p), (*chips[2], c), sibling))
        for p in range(parts):
            for a in range(n_arr):
                copy(a, 0, p, ins[a].at[rows(a, p)], sibling, me).wait_recv()
                for j, chip in enumerate(chips):
                    copy(a, 4 + j, p, ins[a].at[rows(a, p)], (*chip, 1 - c), me).wait_recv()
        for cp in sent:
            cp.wait_send()
        for cp in local:
            cp.wait()

    return pl.pallas_call(
        body, name="allgather_weights",
        out_shape=[jax.ShapeDtypeStruct((N_DEV,) + b.shape, b.dtype) for b in blocks],
        in_specs=[pl.BlockSpec(memory_space=pltpu.HBM)] * n_arr, out_specs=[pl.BlockSpec(memory_space=pltpu.HBM)] * n_arr,
        scratch_shapes=[pltpu.SemaphoreType.DMA((n_arr, 7, parts)), pltpu.SemaphoreType.DMA((n_arr, 7, parts)),
                        pltpu.SemaphoreType.DMA((n_arr,))],
    )(*blocks)


HBM_SPEC = pl.BlockSpec(memory_space=pltpu.HBM)


class _FusedCopies:
    def __init__(self, kind, arrays, peers=(0, 1, 2, 3), rows=None, relay_rows=None):
        self.kind = kind
        self.peers = peers
        self.rows = rows
        self.relay_rows = relay_rows
        n = len(arrays) // 2 if kind == "gather_more" else len(arrays)
        self.n = n
        self.n_in = len(arrays)
        self.aliases = {}
        if kind == "gather":
            self.out_shape = [jax.ShapeDtypeStruct((N_DEV,) + a.shape, a.dtype) for a in arrays]
            self.scratch_shapes = [pltpu.SemaphoreType.DMA((n, 4, GATHER_PARTS)),
                                   pltpu.SemaphoreType.DMA((n, 4, GATHER_PARTS)), pltpu.SemaphoreType.DMA((n,))]
        elif kind == "gather_more":
            self.out_shape = [jax.ShapeDtypeStruct(a.shape, a.dtype) for a in arrays[n:]]
            self.scratch_shapes = [pltpu.SemaphoreType.DMA((n, 5, GATHER_PARTS)),
                                   pltpu.SemaphoreType.DMA((n, 5, GATHER_PARTS)), pltpu.SemaphoreType.DMA((n,))]
            self.aliases = {n + a: a for a in range(n)}
        elif kind == "relay":
            self.out_shape = [jax.ShapeDtypeStruct(a.shape, a.dtype) for a in arrays]
            self.scratch_shapes = [pltpu.SemaphoreType.DMA((n,)), pltpu.SemaphoreType.DMA((n,))]
            self.aliases = {a: a for a in range(n)}
        elif kind == "forward":
            self.out_shape = [jax.ShapeDtypeStruct(a.shape, a.dtype) for a in arrays]
            self.scratch_shapes = [pltpu.SemaphoreType.DMA((n, 3)), pltpu.SemaphoreType.DMA((n, 3))]
            self.aliases = {a: a for a in range(n)}
        elif kind == "sibling":
            self.out_shape = [jax.ShapeDtypeStruct((4,) + a.shape[1:], a.dtype) for a in arrays]
            self.scratch_shapes = [pltpu.SemaphoreType.DMA((n, 4)), pltpu.SemaphoreType.DMA((n, 4))]
        elif kind == "chips_more":
            n = self.n = len(arrays) // 2
            self.out_shape = [jax.ShapeDtypeStruct(a.shape, a.dtype) for a in arrays[n:]]
            self.scratch_shapes = [pltpu.SemaphoreType.DMA((n, 3)), pltpu.SemaphoreType.DMA((n, 3))]
            self.aliases = {n + a: a for a in range(n)}
        else:
            self.out_shape = [jax.ShapeDtypeStruct((3,) + a.shape[1:], a.dtype) for a in arrays]
            self.scratch_shapes = [pltpu.SemaphoreType.DMA((n, 3)), pltpu.SemaphoreType.DMA((n, 3))]
        self.in_specs = [HBM_SPEC] * self.n_in
        self.out_specs = [HBM_SPEC] * n
        self.n_scratch = len(self.scratch_shapes)

    def copies(self, ins, outs, sems):
        x, y, c = _mesh_pos()
        chips = [(1 - x, y), (x, 1 - y), (1 - x, 1 - y)]
        sibling = (x, y, 1 - c)
        starts, waits = [], []
        relay_from, relay_to = _relay_chips(x, y, c)

        def relayed(a, buf, lands, send_sem, recv_sem, rows):
            first, count = rows or (0, buf.shape[1])
            span = pl.ds(first, count)
            return pltpu.make_async_remote_copy(
                src_ref=buf.at[4 * relay_from[0] + 2 * relay_from[1] + c, span],
                dst_ref=outs[a].at[4 * lands[0] + 2 * lands[1] + c, span], send_sem=send_sem, recv_sem=recv_sem,
                device_id=(*relay_to, c), device_id_type=MESH)

        if self.kind in ("gather", "gather_more"):
            send_sems, recv_sems, local_sems = sems
            me = (x, y, c)
            peers = [sibling] + [(px, py, c) for px, py in chips]

            def slot(a, pos):
                return outs[a].at[4 * pos[0] + 2 * pos[1] + pos[2]]

            def span(a, p=None):
                first, count = self.rows or (0, ins[a].shape[0])
                if p is None:
                    return pl.ds(first, count)
                return pl.ds(first + p * (count // GATHER_PARTS), count // GATHER_PARTS)

            def remote(a, k, p, lands_from):
                return pltpu.make_async_remote_copy(
                    src_ref=ins[a].at[span(a, p)], dst_ref=slot(a, lands_from).at[span(a, p)],
                    send_sem=send_sems.at[a, k, p], recv_sem=recv_sems.at[a, k, p], device_id=peers[k],
                    device_id_type=MESH)

            for a in range(self.n):
                local = pltpu.make_async_copy(ins[a].at[span(a)], slot(a, me).at[span(a)], local_sems.at[a])
                starts.append(local)
                waits.append(local)
            for p in range(GATHER_PARTS):
                for a in range(self.n):
                    for k in self.peers:
                        starts.append(remote(a, k, p, me))
                        waits.append(remote(a, k, p, peers[k]))
            if self.kind == "gather_more" and self.relay_rows is not None:
                for a in range(self.n):
                    buf = ins[self.n + a]
                    starts.append(relayed(a, buf, relay_from, send_sems.at[a, 4, 0], recv_sems.at[a, 4, 0],
                                          self.relay_rows))
                    waits.append(relayed(a, buf, chips[2], send_sems.at[a, 4, 0], recv_sems.at[a, 4, 0],
                                         self.relay_rows))
        elif self.kind == "relay":
            send_sems, recv_sems = sems
            for a in range(self.n):
                starts.append(relayed(a, ins[a], relay_from, send_sems.at[a], recv_sems.at[a], self.rows))
                waits.append(relayed(a, ins[a], chips[2], send_sems.at[a], recv_sems.at[a], self.rows))
        elif self.kind == "forward":
            send_sems, recv_sems = sems

            def passed_on(a, j, pc_src, pc_dst):
                px, py = chips[j]
                return pltpu.make_async_remote_copy(
                    src_ref=ins[a].at[4 * px + 2 * py + pc_src], dst_ref=outs[a].at[4 * px + 2 * py + pc_dst],
                    send_sem=send_sems.at[a, j], recv_sem=recv_sems.at[a, j], device_id=sibling, device_id_type=MESH)

            for a in range(self.n):
                for j in range(3):
                    starts.append(passed_on(a, j, c, c))
                    waits.append(passed_on(a, j, c, 1 - c))
        elif self.kind == "sibling":
            send_sems, recv_sems = sems
            for a in range(self.n):
                for q in range(4):
                    cp = pltpu.make_async_remote_copy(
                        src_ref=ins[a].at[2 * q + 1 - c], dst_ref=outs[a].at[q], send_sem=send_sems.at[a, q],
                        recv_sem=recv_sems.at[a, q], device_id=sibling, device_id_type=MESH)
                    starts.append(cp)
                    waits.append(cp)
        else:
            send_sems, recv_sems = sems
            for a in range(self.n):
                first, count = self.rows or (0, ins[a].shape[1])
                span = pl.ds(first, count)
                for j, (px, py) in enumerate(chips):
                    cp = pltpu.make_async_remote_copy(
                        src_ref=ins[a].at[2 * px + py, span], dst_ref=outs[a].at[j, span],
                        send_sem=send_sems.at[a, j], recv_sem=recv_sems.at[a, j], device_id=(px, py, c),
                        device_id_type=MESH)
                    starts.append(cp)
                    waits.append(cp)
        return starts, waits


def _fused_groups(fused):
    if fused is None:
        return []
    return list(fused) if isinstance(fused, (list, tuple)) else [fused]


def _host_body(body, n_in, n_out, fused, first_last):
    groups = _fused_groups(fused)
    if not groups:
        return body
    n_fin, n_fout = sum(g.n_in for g in groups), sum(g.n for g in groups)
    n_fsem = sum(g.n_scratch for g in groups)

    def wrapped(*refs):
        core_in, f_in = refs[:n_in], refs[n_in:n_in + n_fin]
        core_out = refs[n_in + n_fin:n_in + n_fin + n_out]
        f_out = refs[n_in + n_fin + n_out:n_in + n_fin + n_out + n_fout]
        rest = refs[n_in + n_fin + n_out + n_fout:]
        core_scratch, f_sems = rest[:len(rest) - n_fsem], rest[len(rest) - n_fsem:]
        starts, waits = [], []
        for g in groups:
            s, w = g.copies(f_in[:g.n_in], f_out[:g.n], f_sems[:g.n_scratch])
            f_in, f_out, f_sems = f_in[g.n_in:], f_out[g.n:], f_sems[g.n_scratch:]
            starts += s
            waits += w
        first, last = first_last()

        @pl.when(first)
        def _():
            for cp in starts:
                cp.start()

        body(*core_in, *core_out, *core_scratch)

        @pl.when(last)
        def _():
            for cp in waits:
                cp.wait()

    return wrapped


def _host_call(body, n_in, n_out, fused, first_last, *, name, grid, in_specs, out_specs, out_shape, scratch_shapes,
               sem, operands):
    aliases = {}
    in_specs, out_specs, out_shape, scratch_shapes = list(in_specs), list(out_specs), list(out_shape), list(scratch_shapes)
    fin, fout = n_in, n_out
    for g in _fused_groups(fused):
        aliases.update({fin + fi: fout + fo for fi, fo in g.aliases.items()})
        fin, fout = fin + g.n_in, fout + g.n
        in_specs += g.in_specs
        out_specs += g.out_specs
        out_shape += g.out_shape
        scratch_shapes += g.scratch_shapes
        sem = tuple("arbitrary" for _ in sem)
    res = pl.pallas_call(_host_body(body, n_in, n_out, fused, first_last), name=name, grid=grid, in_specs=in_specs,
                         out_specs=out_specs, out_shape=out_shape, scratch_shapes=scratch_shapes,
                         input_output_aliases=aliases, compiler_params=_params(sem))(*operands)
    return list(res[:n_out]), list(res[n_out:])


def _forward_to_sibling(name, gathered):
    n_arr = len(gathered)

    def body(*refs):
        ins, outs = refs[:n_arr], refs[n_arr:2 * n_arr]
        send_sems, recv_sems = refs[2 * n_arr:]
        x, y, c = _mesh_pos()
        chips = [(1 - x, y), (x, 1 - y), (1 - x, 1 - y)]

        def copy(a, j, pc):
            px, py = chips[j]
            s = 4 * px + 2 * py + pc
            return pltpu.make_async_remote_copy(
                src_ref=ins[a].at[s], dst_ref=outs[a].at[s], send_sem=send_sems.at[a, j], recv_sem=recv_sems.at[a, j],
                device_id=(x, y, 1 - c), device_id_type=MESH)

        for a in range(n_arr):
            for j in range(3):
                copy(a, j, c).start()
        for a in range(n_arr):
            for j in range(3):
                copy(a, j, 1 - c).wait_recv()
                copy(a, j, c).wait_send()

    return pl.pallas_call(
        body, name=name,
        out_shape=[jax.ShapeDtypeStruct(g.shape, g.dtype) for g in gathered],
        in_specs=[HBM_SPEC] * n_arr, out_specs=[HBM_SPEC] * n_arr,
        input_output_aliases={a: a for a in range(n_arr)},
        scratch_shapes=[pltpu.SemaphoreType.DMA((n_arr, 3)), pltpu.SemaphoreType.DMA((n_arr, 3))],
    )(*gathered)


def _exchange_sibling(name, partials):
    n_arr = len(partials)

    def body(*refs):
        ins, outs = refs[:n_arr], refs[n_arr:2 * n_arr]
        send_sems, recv_sems = refs[2 * n_arr:]
        x, y, c = _mesh_pos()
        copies = [pltpu.make_async_remote_copy(
            src_ref=ins[a].at[2 * q + 1 - c], dst_ref=outs[a].at[q], send_sem=send_sems.at[a, q],
            recv_sem=recv_sems.at[a, q], device_id=(x, y, 1 - c), device_id_type=MESH)
            for a in range(n_arr) for q in range(4)]
        for cp in copies:
            cp.start()
        for cp in copies:
            cp.wait_recv()
        for cp in copies:
            cp.wait_send()

    return pl.pallas_call(
        body, name=name,
        out_shape=[jax.ShapeDtypeStruct((4,) + p.shape[1:], p.dtype) for p in partials],
        in_specs=[HBM_SPEC] * n_arr, out_specs=[HBM_SPEC] * n_arr,
        scratch_shapes=[pltpu.SemaphoreType.DMA((n_arr, 4)), pltpu.SemaphoreType.DMA((n_arr, 4))],
    )(*partials)


def _matmul(name, a, b, dims, grid, a_spec, b_spec, o_spec, out_shape, acc_axis=None, fused=None, fused_arrays=()):
    def body(a_ref, b_ref, o_ref):
        r = lax.dot_general(a_ref[...], b_ref[...], dims, preferred_element_type=F32)
        if acc_axis is None:
            o_ref[...] = r.astype(o_ref.dtype)
        else:
            k = pl.program_id(acc_axis)

            @pl.when(k == 0)
            def _():
                o_ref[...] = r

            @pl.when(k > 0)
            def _():
                o_ref[...] += r

    sem = tuple("arbitrary" if i == acc_axis else "parallel" for i in range(len(grid)))
    if fused is None:
        return pl.pallas_call(body, name=name, grid=grid, in_specs=[a_spec, b_spec], out_specs=o_spec,
                              out_shape=out_shape, compiler_params=_params(sem))(a, b)

    def first_last():
        first = last = None
        for ax, n in enumerate(grid):
            f, l = pl.program_id(ax) == 0, pl.program_id(ax) == n - 1
            first, last = (f, l) if first is None else (first & f, last & l)
        return first, last

    (out,), extra = _host_call(body, 2, 1, fused, first_last, name=name, grid=grid, in_specs=[a_spec, b_spec],
                               out_specs=[o_spec], out_shape=[out_shape], scratch_shapes=[], sem=sem,
                               operands=[a, b] + list(fused_arrays))
    return out, extra


def _mm_blocked_rhs(name, a, w_g, tm=MM_TILE, fused=None, fused_arrays=()):
    m, k = a.shape
    nb = w_g.shape[2]
    return _matmul(name, a, w_g, NN, (N_DEV, m // tm),
                   pl.BlockSpec((tm, k), lambda j, i: (i, 0)),
                   pl.BlockSpec((None, k, nb), lambda j, i: (j, 0, 0)),
                   pl.BlockSpec((tm, nb), lambda j, i: (i, j)),
                   jax.ShapeDtypeStruct((m, N_DEV * nb), F32), fused=fused, fused_arrays=fused_arrays)


def _mm_blocked_rhs_t(name, a, w_g, tm=MM_TILE, fused=None, fused_arrays=()):
    m = a.shape[0]
    n, nb = w_g.shape[1], w_g.shape[2]
    return _matmul(name, a, w_g, NT, (m // tm, N_DEV),
                   pl.BlockSpec((tm, nb), lambda i, j: (i, j)),
                   pl.BlockSpec((None, n, nb), lambda i, j: (j, 0, 0)),
                   pl.BlockSpec((tm, n), lambda i, j: (i, 0)),
                   jax.ShapeDtypeStruct((m, n), F32), acc_axis=1, fused=fused, fused_arrays=fused_arrays)


def _mm_wgrad_blocked(name, act, dcols, tk=MM_TILE, fused=None, fused_arrays=()):
    t, k = act.shape
    nb = dcols.shape[1] // N_DEV
    return _matmul(name, act, dcols, TN, (N_DEV, k // tk),
                   pl.BlockSpec((t, tk), lambda j, i: (0, i)),
                   pl.BlockSpec((t, nb), lambda j, i: (0, j)),
                   pl.BlockSpec((None, tk, nb), lambda j, i: (j, i, 0)),
                   jax.ShapeDtypeStruct((N_DEV, k, nb), BF16), fused=fused, fused_arrays=fused_arrays)


def _halves_specs(block, index):
    half = N_DEV // 2
    return (pl.BlockSpec(block, lambda i, j: index(i, jnp.minimum(j, half - 1))),
            pl.BlockSpec(block, lambda i, j: index(i, jnp.maximum(j - half, 0))))


def _mm_halves_rhs_t(name, a_lo, a_hi, w_g, tm=MM_TILE):
    m = a_lo.shape[0]
    n, nb = w_g.shape[1], w_g.shape[2]

    def body(lo_ref, hi_ref, b_ref, o_ref):
        j = pl.program_id(1)

        def accumulate(a_ref):
            r = lax.dot_general(a_ref[...], b_ref[...], NT, preferred_element_type=F32)

            @pl.when(j == 0)
            def _():
                o_ref[...] = r

            @pl.when(j > 0)
            def _():
                o_ref[...] += r

        pl.when(j < N_DEV // 2)(lambda: accumulate(lo_ref))
        pl.when(j >= N_DEV // 2)(lambda: accumulate(hi_ref))

    lo_spec, hi_spec = _halves_specs((tm, nb), lambda i, j: (i, j))
    return pl.pallas_call(
        body, name=name, grid=(m // tm, N_DEV),
        in_specs=[lo_spec, hi_spec, pl.BlockSpec((None, n, nb), lambda i, j: (j, 0, 0))],
        out_specs=pl.BlockSpec((tm, n), lambda i, j: (i, 0)), out_shape=jax.ShapeDtypeStruct((m, n), F32),
        compiler_params=_params(("parallel", "arbitrary")))(a_lo, a_hi, w_g)


def _mm_halves_wgrad(name, act, d_lo, d_hi, tk=MM_TILE):
    t, k = act.shape
    nb = d_lo.shape[1] // (N_DEV // 2)

    def body(a_ref, lo_ref, hi_ref, o_ref):
        j = pl.program_id(0)

        def product(d_ref):
            o_ref[...] = lax.dot_general(a_ref[...], d_ref[...], TN, preferred_element_type=F32).astype(o_ref.dtype)

        pl.when(j < N_DEV // 2)(lambda: product(lo_ref))
        pl.when(j >= N_DEV // 2)(lambda: product(hi_ref))

    half = N_DEV // 2
    return pl.pallas_call(
        body, name=name, grid=(N_DEV, k // tk),
        in_specs=[pl.BlockSpec((t, tk), lambda j, i: (0, i)),
                  pl.BlockSpec((t, nb), lambda j, i: (0, jnp.minimum(j, half - 1))),
                  pl.BlockSpec((t, nb), lambda j, i: (0, jnp.maximum(j - half, 0)))],
        out_specs=pl.BlockSpec((None, tk, nb), lambda j, i: (j, i, 0)),
        out_shape=jax.ShapeDtypeStruct((N_DEV, k, nb), BF16),
        compiler_params=_params(("parallel", "parallel")))(act, d_lo, d_hi)


def _mm_plain(name, a, b, dims, tm, tn, out_dtype, fused=None, fused_arrays=()):
    if dims == NN:
        (m, k), n = a.shape, b.shape[1]
        a_spec = pl.BlockSpec((tm, k), lambda i, j: (i, 0))
        b_spec = pl.BlockSpec((k, tn), lambda i, j: (0, j))
    elif dims == NT:
        (m, k), n = a.shape, b.shape[0]
        a_spec = pl.BlockSpec((tm, k), lambda i, j: (i, 0))
        b_spec = pl.BlockSpec((tn, k), lambda i, j: (j, 0))
    else:
        (k, m), n = a.shape, b.shape[1]
        a_spec = pl.BlockSpec((k, tm), lambda i, j: (0, i))
        b_spec = pl.BlockSpec((k, tn), lambda i, j: (0, j))
    return _matmul(name, a, b, dims, (m // tm, n // tn), a_spec, b_spec,
                   pl.BlockSpec((tm, tn), lambda i, j: (i, j)), jax.ShapeDtypeStruct((m, n), out_dtype),
                   fused=fused, fused_arrays=fused_arrays)


def _ada_fwd(c_all, w_ada_blk, b_blk):
    def body(c_ref, w_ref, b_ref, o_ref):
        cv = c_ref[...]
        o_ref[...] = jnp.dot(cv * _sigmoid(cv), w_ref[...], preferred_element_type=F32) + b_ref[...]

    tn = 512
    return pl.pallas_call(
        body, name="ada_fwd", grid=(ADA_BLK // tn,),
        in_specs=[pl.BlockSpec((N_DEV, D_MODEL), lambda j: (0, 0)),
                  pl.BlockSpec((D_MODEL, tn), lambda j: (0, j)),
                  pl.BlockSpec((1, tn), lambda j: (0, j))],
        out_specs=pl.BlockSpec((N_DEV, tn), lambda j: (0, j)),
        out_shape=jax.ShapeDtypeStruct((N_DEV, ADA_BLK), F32),
        compiler_params=_params(("parallel",)))(c_all, w_ada_blk, b_blk)


def _ada_wgrad(c_all, gmod_cols):
    def body(c_ref, g_ref, o_ref):
        cv = c_ref[...]
        o_ref[...] = lax.dot_general(cv * _sigmoid(cv), g_ref[...], TN, preferred_element_type=F32)

    tk = 512
    return pl.pallas_call(
        body, name="ada_wgrad", grid=(D_MODEL // tk,),
        in_specs=[pl.BlockSpec((N_DEV, tk), lambda i: (0, i)),
                  pl.BlockSpec((N_DEV, ADA_BLK), lambda i: (0, 0))],
        out_specs=pl.BlockSpec((tk, ADA_BLK), lambda i: (i, 0)),
        out_shape=jax.ShapeDtypeStruct((D_MODEL, ADA_BLK), F32),
        compiler_params=_params(("parallel",)))(c_all, gmod_cols)


def _row_spec(cols=D_MODEL):
    return pl.BlockSpec((ROW_TILE, cols), lambda i: (i, 0))


def _vec_spec(cols=D_MODEL):
    return pl.BlockSpec((1, cols), lambda i: (0, 0))


def _norm_fwd(name, x, w, scale, shift, resid=None, gate=None):
    has_res = resid is not None

    def body(*refs):
        if has_res:
            x_ref, r_ref, g_ref, w_ref, sc_ref, sh_ref, xr_ref, h_ref, rs_ref = refs
            xr = x_ref[...] + g_ref[...] * r_ref[...]
            xr_ref[...] = xr
        else:
            x_ref, w_ref, sc_ref, sh_ref, h_ref, rs_ref = refs
            xr = x_ref[...]
        rs = lax.rsqrt(jnp.mean(xr * xr, axis=-1, keepdims=True) + EPS)
        h = (xr * rs) * w_ref[...] * (1.0 + sc_ref[...]) + sh_ref[...]
        h_ref[...] = h.astype(BF16)
        rs_ref[...] = rs

    s = x.shape[0]
    ins = [x] + ([resid, gate] if has_res else []) + [w, scale, shift]
    in_specs = [_row_spec()] + ([_row_spec(), _vec_spec()] if has_res else []) + [_vec_spec()] * 3
    outs = ([jax.ShapeDtypeStruct((s, D_MODEL), F32)] if has_res else []) + [
        jax.ShapeDtypeStruct((s, D_MODEL), BF16), jax.ShapeDtypeStruct((s, 1), F32)]
    out_specs = ([_row_spec()] if has_res else []) + [_row_spec(), pl.BlockSpec((ROW_TILE, 1), lambda i: (i, 0))]
    return pl.pallas_call(body, name=name, grid=(s // ROW_TILE,), in_specs=in_specs, out_specs=out_specs,
                          out_shape=outs, compiler_params=_params(("parallel",)))(*ins)


def _norm_bwd(name, dh, x, rstd, w, scale, dres, mix=None, gate=None, fused=None, fused_arrays=()):
    has_mix = mix is not None

    def body(*refs):
        if has_mix:
            (dh_ref, x_ref, rs_ref, w_ref, sc_ref, dr_ref, mix_ref, g_ref,
             dx_ref, dmix_ref, dsh_ref, dsc_ref, dw_ref, dg_ref) = refs
        else:
            dh_ref, x_ref, rs_ref, w_ref, sc_ref, dr_ref, dx_ref, dsh_ref, dsc_ref, dw_ref = refs
        i = pl.program_id(0)
        dhv = dh_ref[...]
        rs = rs_ref[...]
        xn = x_ref[...] * rs
        wv = w_ref[...]
        one_sc = 1.0 + sc_ref[...]
        dxn = dhv * wv * one_sc
        dx = dr_ref[...] + rs * (dxn - xn * jnp.mean(dxn * xn, axis=-1, keepdims=True))
        dx_ref[...] = dx
        sums = [(dsh_ref, dhv), (dsc_ref, dhv * xn * wv), (dw_ref, dhv * one_sc * xn)]
        if has_mix:
            dmix_ref[...] = (dx * g_ref[...]).astype(BF16)
            sums.append((dg_ref, dx * mix_ref[...]))

        @pl.when(i == 0)
        def _():
            for ref, _v in sums:
                ref[...] = jnp.zeros_like(ref)

        for ref, v in sums:
            ref[...] += jnp.sum(v, axis=0, keepdims=True)

    s = x.shape[0]
    ins = [dh, x, rstd, w, scale, dres] + ([mix, gate] if has_mix else [])
    in_specs = ([_row_spec(), _row_spec(), pl.BlockSpec((ROW_TILE, 1), lambda i: (i, 0)), _vec_spec(), _vec_spec(),
                 _row_spec()] + ([_row_spec(), _vec_spec()] if has_mix else []))
    vec = jax.ShapeDtypeStruct((1, D_MODEL), F32)
    outs = ([jax.ShapeDtypeStruct((s, D_MODEL), F32)] + ([jax.ShapeDtypeStruct((s, D_MODEL), BF16)] if has_mix else [])
            + [vec] * (4 if has_mix else 3))
    out_specs = [_row_spec()] + ([_row_spec()] if has_mix else []) + [_vec_spec()] * (4 if has_mix else 3)

    def first_last():
        i = pl.program_id(0)
        return i == 0, i == s // ROW_TILE - 1

    res, extra = _host_call(body, len(ins), len(outs), fused, first_last, name=name, grid=(s // ROW_TILE,),
                            in_specs=in_specs, out_specs=out_specs, out_shape=outs, scratch_shapes=[],
                            sem=("arbitrary",), operands=ins + list(fused_arrays))
    return res if fused is None else (res, extra)


def _loss_head(x1, ffn, gate2, target):
    def body(x_ref, f_ref, g_ref, t_ref, loss_ref, dout_ref, dffn_ref, dg_ref):
        i = pl.program_id(0)
        fv = f_ref[...]
        gv = g_ref[...]
        err = x_ref[...] + gv * fv - t_ref[...]
        dout = err * (1.0 / D_MODEL)
        dout_ref[...] = dout
        dffn_ref[...] = (dout * gv).astype(BF16)

        @pl.when(i == 0)
        def _():
            loss_ref[...] = jnp.zeros_like(loss_ref)
            dg_ref[...] = jnp.zeros_like(dg_ref)

        row = jnp.sum(err * err, axis=-1, keepdims=True) * (1.0 / D_MODEL)
        loss_ref[...] += jnp.broadcast_to(0.5 * jnp.sum(row, axis=0, keepdims=True), (1, 128))
        dg_ref[...] += jnp.sum(dout * fv, axis=0, keepdims=True)

    s = x1.shape[0]
    return pl.pallas_call(
        body, name="loss_head", grid=(s // ROW_TILE,),
        in_specs=[_row_spec(), _row_spec(), _vec_spec(), _row_spec()],
        out_specs=[pl.BlockSpec((1, 128), lambda i: (0, 0)), _row_spec(), _row_spec(), _vec_spec()],
        out_shape=[jax.ShapeDtypeStruct((1, 128), F32), jax.ShapeDtypeStruct((s, D_MODEL), F32),
                   jax.ShapeDtypeStruct((s, D_MODEL), BF16), jax.ShapeDtypeStruct((1, D_MODEL), F32)],
        compiler_params=_params(("arbitrary",)))(x1, ffn, gate2, target)


CONV_TILE = 512
N_CONV_TILES = D_FF // CONV_TILE


def _shift_rows(a, k, row):
    n = a.shape[0]
    if k > 0:
        return jnp.where(row >= k, pltpu.roll(a, k, 0), 0.0)
    return jnp.where(row < n + k, pltpu.roll(a, n + k, 0), 0.0)


def _conv_gate_fwd(u, conv_w, conv_b, fused=None, fused_arrays=()):
    s = u.shape[0]

    def body(a_ref, g_ref, w_ref, b_ref, y_ref):
        a = a_ref[...]
        w = w_ref[...]
        row = lax.broadcasted_iota(jnp.int32, a.shape, 0)
        ac = b_ref[...] + _shift_rows(a, 2, row) * w[0:1] + _shift_rows(a, 1, row) * w[1:2] + a * w[2:3]
        y_ref[...] = (ac * _sigmoid(ac) * g_ref[...]).astype(BF16)

    def first_last():
        i = pl.program_id(0)
        return i == 0, i == N_CONV_TILES - 1

    col = lambda off: pl.BlockSpec((s, CONV_TILE), lambda i: (0, i + off))
    (y,), extra = _host_call(
        body, 4, 1, fused, first_last, name="conv_gate_fwd", grid=(N_CONV_TILES,),
        in_specs=[col(0), col(N_CONV_TILES), pl.BlockSpec((3, CONV_TILE), lambda i: (0, i)),
                  pl.BlockSpec((1, CONV_TILE), lambda i: (0, i))],
        out_specs=[col(0)], out_shape=[jax.ShapeDtypeStruct((s, D_FF), BF16)], scratch_shapes=[], sem=("parallel",),
        operands=[u, u, conv_w, conv_b] + list(fused_arrays))
    return y if fused is None else (y, extra)


def _conv_gate_bwd(u, dy, conv_w, conv_b):
    s = u.shape[0]

    def body(a_ref, g_ref, dy_ref, w_ref, b_ref, da_ref, dg_ref, gw_ref, gb_ref):
        a = a_ref[...]
        w = w_ref[...]
        row = lax.broadcasted_iota(jnp.int32, a.shape, 0)
        a1 = _shift_rows(a, 1, row)
        a2 = _shift_rows(a, 2, row)
        ac = b_ref[...] + a2 * w[0:1] + a1 * w[1:2] + a * w[2:3]
        sg = _sigmoid(ac)
        dyv = dy_ref[...].astype(F32)
        dg_ref[...] = (dyv * (ac * sg)).astype(BF16)
        dac = dyv * g_ref[...] * _dsilu(ac, sg)
        gb_ref[...] = jnp.sum(dac, axis=0, keepdims=True)
        gw_ref[0:1, :] = jnp.sum(dac * a2, axis=0, keepdims=True)
        gw_ref[1:2, :] = jnp.sum(dac * a1, axis=0, keepdims=True)
        gw_ref[2:3, :] = jnp.sum(dac * a, axis=0, keepdims=True)
        da = dac * w[2:3] + _shift_rows(dac, -1, row) * w[1:2] + _shift_rows(dac, -2, row) * w[0:1]
        da_ref[...] = da.astype(BF16)

    col = lambda off: pl.BlockSpec((s, CONV_TILE), lambda i: (0, i + off))
    return pl.pallas_call(
        body, name="conv_gate_bwd", grid=(N_CONV_TILES,),
        in_specs=[col(0), col(N_CONV_TILES), col(0), pl.BlockSpec((3, CONV_TILE), lambda i: (0, i)),
                  pl.BlockSpec((1, CONV_TILE), lambda i: (0, i))],
        out_specs=[col(0), col(0), pl.BlockSpec((3, CONV_TILE), lambda i: (0, i)),
                   pl.BlockSpec((1, CONV_TILE), lambda i: (0, i))],
        out_shape=[jax.ShapeDtypeStruct((s, D_FF), BF16), jax.ShapeDtypeStruct((s, D_FF), BF16),
                   jax.ShapeDtypeStruct((3, D_FF), F32), jax.ShapeDtypeStruct((1, D_FF), F32)],
        compiler_params=_params(("parallel",)))(u, u, dy, conv_w, conv_b)


HG_TILE = 256
CHUNK_UNROLL = 8


def _unrolled_loop(n, body, init):
    def group(i, carry):
        for u in range(CHUNK_UNROLL):
            carry = body(i * CHUNK_UNROLL + u, carry)
        return carry

    return lax.fori_loop(0, n // CHUNK_UNROLL, group, init)


def _head_col(off):
    return pl.BlockSpec((SEQ, HEAD_DIM), lambda h: (0, h + off))


def _hgrn_gates(hq, hf, lb, pos):
    q = hq * _sigmoid(hq)
    sig = _sigmoid(hf)
    f = lb + (1.0 - lb) * sig
    gl = jnp.log(f)
    for sh in (1, 2, 4, 8):
        gl = gl + jnp.where(pos >= sh, pltpu.roll(gl, sh, 0), 0.0)
    return q, sig, f, 1.0 - f, gl


def _lower_bound(lbl):
    return 1.0 / (1.0 + jnp.exp(lbl[1:2, :] - lbl[0:1, :]))


def _head_first_last():
    h = pl.program_id(0)
    return h == 0, h == HEADS - 1


CHUNKS_PER_TILE = HG_TILE // CHUNK


def _chunk_end(x, pos):
    y = jnp.where(pos == CHUNK - 1, x, 0.0)
    for sh in (1, 2, 4, 8):
        y = y + jnp.where(pos < CHUNK - sh, pltpu.roll(y, x.shape[0] - sh, 0), 0.0)
    return y


def _suffix_in_chunk(x, pos):
    for sh in (1, 2, 4, 8):
        x = x + jnp.where(pos < CHUNK - sh, pltpu.roll(x, x.shape[0] - sh, 0), 0.0)
    return x


def _prefix_in_chunk(x, pos):
    for sh in (1, 2, 4, 8):
        x = x + jnp.where(pos >= sh, pltpu.roll(x, sh, 0), 0.0)
    return x


def _pair_decays(f, pos):
    shifted = jnp.where(pos >= 1, f, 0.0)
    e = shifted
    yield 1, e
    for d in range(2, CHUNK):
        shifted = pltpu.roll(shifted, 1, 0)
        e = e * shifted
        yield d, e


def _chunk_rows(cc):
    return slice(cc * CHUNK, (cc + 1) * CHUNK)


def _outer_products(lhs_b, rhs_b, dst, i):
    for cc in range(CHUNKS_PER_TILE):
        dst[i * CHUNKS_PER_TILE + cc] = lax.dot_general(lhs_b[_chunk_rows(cc)], rhs_b[_chunk_rows(cc)], TN,
                                                        preferred_element_type=F32)


def _state_scan(n_chunks, gl_s, u_s, keep, reverse):
    def step(k, st):
        c = n_chunks - 1 - k if reverse else k
        keep[c] = st.astype(BF16)
        gl = gl_s[pl.ds(pl.multiple_of(c * CHUNK, CHUNK), CHUNK), :]
        return st * jnp.exp(gl[CHUNK - 1:CHUNK, :]) + u_s[c]

    _unrolled_loop(n_chunks, step, jnp.zeros((HEAD_DIM, HEAD_DIM), F32))


def _hgrn_fwd(proj, lb_logits, norm_w, fused=None, fused_arrays=()):
    n_tiles = SEQ // HG_TILE
    n_chunks = SEQ // CHUNK
    fused_arrays = list(fused_arrays)

    def body(hq_ref, hf_ref, hi_ref, hg_ref, lbl_ref, nw_ref, aout_ref, opre_ref, qt_s, gl_s, u_s, st_s):
        lb = _lower_bound(lbl_ref[...])
        ones = jnp.ones((HEAD_DIM, HEAD_DIM), BF16)
        pos = lax.broadcasted_iota(jnp.int32, (HG_TILE, HEAD_DIM), 0) % CHUNK

        def tile(i, carry):
            rows = pl.ds(pl.multiple_of(i * HG_TILE, HG_TILE), HG_TILE)
            v = hi_ref[rows, :]
            q, _sig, f, kk, gl = _hgrn_gates(hq_ref[rows, :], hf_ref[rows, :], lb, pos)
            o = _lane_sum(q * kk, ones) * v
            for d, e in _pair_decays(f, pos):
                o = o + _lane_sum(q * pltpu.roll(kk, d, 0) * e, ones) * pltpu.roll(v, d, 0)
            opre_ref[rows, :] = o
            qt_s[rows, :] = q * jnp.exp(gl)
            gl_s[rows, :] = gl
            kt = kk * jnp.exp(_chunk_end(gl, pos) - gl)
            _outer_products(v.astype(BF16), kt.astype(BF16), u_s, i)
            return carry

        lax.fori_loop(0, n_tiles, tile, 0)
        _state_scan(n_chunks, gl_s, u_s, st_s, reverse=False)

        def finish(i, carry):
            rows = pl.ds(pl.multiple_of(i * HG_TILE, HG_TILE), HG_TILE)
            qt_b = qt_s[rows, :].astype(BF16)
            past = [lax.dot_general(qt_b[_chunk_rows(cc)], st_s[i * CHUNKS_PER_TILE + cc], NT,
                                    preferred_element_type=F32) for cc in range(CHUNKS_PER_TILE)]
            o = opre_ref[rows, :] + jnp.concatenate(past, axis=0)
            opre_ref[rows, :] = o
            hg = hg_ref[rows, :]
            rs = lax.rsqrt(jnp.mean(o * o, axis=-1, keepdims=True) + EPS)
            aout_ref[rows, :] = ((o * rs) * nw_ref[...] * (hg * _sigmoid(hg))).astype(BF16)
            return carry

        lax.fori_loop(0, n_tiles, finish, 0)

    return _host_call(
        body, 6, 2, fused, _head_first_last, name="hgrn_fwd", grid=(HEADS,),
        in_specs=[_head_col(0), _head_col(HEADS), _head_col(2 * HEADS), _head_col(3 * HEADS),
                  pl.BlockSpec((2, HEAD_DIM), lambda h: (0, h)), pl.BlockSpec((1, HEAD_DIM), lambda h: (0, 0))],
        out_specs=[_head_col(0), _head_col(0)],
        out_shape=[jax.ShapeDtypeStruct((SEQ, HEADS * HEAD_DIM), BF16), jax.ShapeDtypeStruct((SEQ, HEADS * HEAD_DIM), F32)],
        scratch_shapes=[pltpu.VMEM((SEQ, HEAD_DIM), F32)] * 2 + [pltpu.VMEM((n_chunks, HEAD_DIM, HEAD_DIM), F32),
                                                                 pltpu.VMEM((n_chunks, HEAD_DIM, HEAD_DIM), BF16)],
        sem=("parallel",), operands=[proj, proj, proj, proj, lb_logits, norm_w] + fused_arrays)


def _hgrn_bwd(proj, lb_logits, norm_w, o_pre, d_aout, fused=None, fused_arrays=()):
    n_tiles = SEQ // HG_TILE
    n_chunks = SEQ // CHUNK

    def body(hq_ref, hf_ref, hi_ref, hg_ref, lbl_ref, nw_ref, opre_ref, da_ref,
             dhq_ref, dhf_ref, dhi_ref, dhg_ref, dlog_ref, gnw_ref,
             q_s, k_s, gl_s, do_s, dq_s, dk_s, dv_s, u_s, st_s, rt_s):
        h = pl.program_id(0)
        lb = _lower_bound(lbl_ref[...])
        nw = nw_ref[...]
        ones = jnp.ones((HEAD_DIM, HEAD_DIM), BF16)
        pos = lax.broadcasted_iota(jnp.int32, (HG_TILE, HEAD_DIM), 0) % CHUNK

        @pl.when(h == 0)
        def _():
            gnw_ref[...] = jnp.zeros_like(gnw_ref)

        def tile(i, carry):
            rows = pl.ds(pl.multiple_of(i * HG_TILE, HG_TILE), HG_TILE)
            v = hi_ref[rows, :]
            q, _sig, f, kk, gl = _hgrn_gates(hq_ref[rows, :], hf_ref[rows, :], lb, pos)
            o = opre_ref[rows, :]
            hg = hg_ref[rows, :]
            da = da_ref[rows, :]
            rs = lax.rsqrt(jnp.mean(o * o, axis=-1, keepdims=True) + EPS)
            oh = o * rs
            sg = _sigmoid(hg)
            dnorm = da * (hg * sg)
            dhg_ref[rows, :] = (da * (oh * nw) * _dsilu(hg, sg)).astype(BF16)
            gnw_ref[...] += jnp.sum(dnorm * oh, axis=0, keepdims=True)
            doh = dnorm * nw
            do = rs * (doh - oh * jnp.mean(doh * oh, axis=-1, keepdims=True))

            d_a = _lane_sum(do * v, ones)
            dq = d_a * kk
            dk = d_a * q
            dv = _lane_sum(q * kk, ones) * do
            for d, e in _pair_decays(f, pos):
                ks = pltpu.roll(kk, d, 0)
                a_d = _lane_sum(q * ks * e, ones)
                d_a = _lane_sum(do * pltpu.roll(v, d, 0), ones) * e
                dq = dq + d_a * ks
                dk = dk + pltpu.roll(d_a * q, HG_TILE - d, 0)
                dv = dv + pltpu.roll(a_d * do, HG_TILE - d, 0)
            q_s[rows, :] = q
            k_s[rows, :] = kk
            gl_s[rows, :] = gl
            do_s[rows, :] = do
            dq_s[rows, :] = dq
            dk_s[rows, :] = dk
            dv_s[rows, :] = dv
            kt = kk * jnp.exp(_chunk_end(gl, pos) - gl)
            _outer_products(v.astype(BF16), kt.astype(BF16), u_s, i)
            return carry

        lax.fori_loop(0, n_tiles, tile, 0)
        _state_scan(n_chunks, gl_s, u_s, st_s, reverse=False)

        def reverse_increments(i, carry):
            rows = pl.ds(pl.multiple_of(i * HG_TILE, HG_TILE), HG_TILE)
            qt = q_s[rows, :] * jnp.exp(gl_s[rows, :])
            _outer_products(do_s[rows, :].astype(BF16), qt.astype(BF16), u_s, i)
            return carry

        lax.fori_loop(0, n_tiles, reverse_increments, 0)
        _state_scan(n_chunks, gl_s, u_s, rt_s, reverse=True)

        def finish(i, dlb):
            rows = pl.ds(pl.multiple_of(i * HG_TILE, HG_TILE), HG_TILE)
            q = q_s[rows, :]
            kk = k_s[rows, :]
            gl = gl_s[rows, :]
            gll = _chunk_end(gl, pos)
            ekt = jnp.exp(gll - gl)
            do_b = do_s[rows, :].astype(BF16)
            v_b = hi_ref[rows, :].astype(BF16)
            kt_b = (kk * ekt).astype(BF16)
            dq_far, dk_far, dv_far, across = [], [], [], []
            for cc in range(CHUNKS_PER_TILE):
                st = st_s[i * CHUNKS_PER_TILE + cc]
                rt = rt_s[i * CHUNKS_PER_TILE + cc]
                sl = _chunk_rows(cc)
                dq_far.append(jnp.dot(do_b[sl], st, preferred_element_type=F32))
                dk_far.append(jnp.dot(v_b[sl], rt, preferred_element_type=F32))
                dv_far.append(lax.dot_general(kt_b[sl], rt, NT, preferred_element_type=F32))
                both = jnp.sum(st.astype(F32) * rt.astype(F32), axis=0, keepdims=True)
                across.append(jnp.broadcast_to(both, (CHUNK, HEAD_DIM)))
            dq = dq_s[rows, :] + jnp.concatenate(dq_far, axis=0) * jnp.exp(gl)
            dk_in = dk_s[rows, :]
            dk_out = jnp.concatenate(dk_far, axis=0) * ekt
            dk = dk_in + dk_out
            dv = dv_s[rows, :] + jnp.concatenate(dv_far, axis=0)
            pc = kk * dk_out
            dgl = (_suffix_in_chunk(q * dq - kk * dk_in, pos) + (_prefix_in_chunk(pc, pos) - pc)
                   + jnp.concatenate(across, axis=0) * jnp.exp(gll))
            hf = hf_ref[rows, :]
            sig = _sigmoid(hf)
            f = lb + (1.0 - lb) * sig
            df = dgl / f - dk
            dhf_ref[rows, :] = (df * (1.0 - lb) * sig * (1.0 - sig)).astype(BF16)
            hq = hq_ref[rows, :]
            dhq_ref[rows, :] = (dq * _dsilu(hq, _sigmoid(hq))).astype(BF16)
            dhi_ref[rows, :] = dv.astype(BF16)
            return dlb + jnp.sum(df * (1.0 - sig), axis=0, keepdims=True)

        dlb = lax.fori_loop(0, n_tiles, finish, jnp.zeros((1, HEAD_DIM), F32))
        dl0 = lb * (1.0 - lb) * dlb
        dlog_ref[0:1, :] = dl0
        dlog_ref[1:2, :] = -dl0

    wide = HEADS * HEAD_DIM
    return _host_call(
        body, 8, 6, fused, _head_first_last, name="hgrn_bwd", grid=(HEADS,),
        in_specs=[_head_col(0), _head_col(HEADS), _head_col(2 * HEADS), _head_col(3 * HEADS),
                  pl.BlockSpec((2, HEAD_DIM), lambda h: (0, h)), pl.BlockSpec((1, HEAD_DIM), lambda h: (0, 0)),
                  _head_col(0), _head_col(0)],
        out_specs=[_head_col(0)] * 4 + [pl.BlockSpec((2, HEAD_DIM), lambda h: (0, h)),
                                        pl.BlockSpec((1, HEAD_DIM), lambda h: (0, 0))],
        out_shape=[jax.ShapeDtypeStruct((SEQ, wide), BF16)] * 4 + [jax.ShapeDtypeStruct((2, wide), F32),
                                                                    jax.ShapeDtypeStruct((1, HEAD_DIM), F32)],
        scratch_shapes=[pltpu.VMEM((SEQ, HEAD_DIM), F32)] * 7 + [pltpu.VMEM((n_chunks, HEAD_DIM, HEAD_DIM), F32),
                                                                 pltpu.VMEM((n_chunks, HEAD_DIM, HEAD_DIM), BF16),
                                                                 pltpu.VMEM((n_chunks, HEAD_DIM, HEAD_DIM), BF16)],
        sem=("arbitrary",),
        operands=[proj, proj, proj, proj, lb_logits, norm_w, o_pre, d_aout] + list(fused_arrays))


Q_TILE = 512
ATT_SCALE = HEAD_DIM ** -0.5
ATT_OFF = 4 * HEADS


def _qk_prep(proj, q_w, k_w, fused=None, fused_arrays=()):
    def body(aq_ref, ak_ref, av_ref, qw_ref, kw_ref, qn_ref, kn_ref, v_ref):
        aq = aq_ref[...]
        ak = ak_ref[...]
        qn_ref[...] = (aq * lax.rsqrt(jnp.mean(aq * aq, axis=-1, keepdims=True) + EPS) * qw_ref[...]).astype(BF16)
        kn_ref[...] = (ak * lax.rsqrt(jnp.mean(ak * ak, axis=-1, keepdims=True) + EPS) * kw_ref[...]).astype(BF16)
        v_ref[...] = av_ref[...].astype(BF16)

    wide = HEADS * HEAD_DIM
    vec = pl.BlockSpec((1, HEAD_DIM), lambda h: (0, 0))
    return _host_call(
        body, 5, 3, fused, _head_first_last, name="qk_prep", grid=(HEADS,),
        in_specs=[_head_col(ATT_OFF), _head_col(ATT_OFF + HEADS), _head_col(ATT_OFF + 2 * HEADS), vec, vec],
        out_specs=[_head_col(0)] * 3, out_shape=[jax.ShapeDtypeStruct((SEQ, wide), BF16)] * 3,
        scratch_shapes=[], sem=("parallel",), operands=[proj, proj, proj, q_w, k_w] + list(fused_arrays))


def _alibi_slopes():
    slopes = np.exp2(-8.0 * np.arange(1, HEADS + 1, dtype=np.float32) / HEADS).astype(np.float32)
    return np.broadcast_to(slopes[:, None, None], (HEADS, 1, HEAD_DIM))


SLOPE_SPEC = pl.BlockSpec((None, 1, HEAD_DIM), lambda h, i: (h, 0, 0))


N_Q_TILES = SEQ // Q_TILE
K_BLOCK = 512
NOT_ATTENDED = 1e35


def _att_tables():
    o = np.arange(N_Q_TILES, dtype=np.int32)[:, None, None]
    r = np.arange(Q_TILE, dtype=np.int32)[None, :, None]
    c = np.arange(K_BLOCK, dtype=np.int32)[None, None, :]
    dist = o * Q_TILE + r - c
    mult = ((dist <= 128).astype(np.float32) + (((dist % 4) == 0) & (dist <= 512)).astype(np.float32)
            + ((dist % 16) == 0).astype(np.float32))
    valid = (dist >= 0) & (mult > 0)
    return (np.where(valid, dist.astype(np.float32), np.float32(NOT_ATTENDED)).astype(np.float32),
            np.where(valid, np.log(np.maximum(mult, 1.0)), 0.0).astype(np.float32))


TABLE_SPEC = pl.BlockSpec((N_Q_TILES, Q_TILE, K_BLOCK), lambda h, i: (0, 0, 0))


def _att_block(q, k_ref, j, i, slope, dist_ref, lmul_ref):
    rows = pl.ds(pl.multiple_of(j * K_BLOCK, K_BLOCK), K_BLOCK)
    off = i - j * (K_BLOCK // Q_TILE)
    s = lax.dot_general(q, k_ref[rows, :], NT, preferred_element_type=F32) * ATT_SCALE
    return s - slope * dist_ref[off] + lmul_ref[off], rows


def _n_key_blocks(i):
    return (i + K_BLOCK // Q_TILE) // (K_BLOCK // Q_TILE)


def _att_first_last():
    h, i = pl.program_id(0), pl.program_id(1)
    return (h == 0) & (i == 0), (h == HEADS - 1) & (i == N_Q_TILES - 1)


def _attn_fwd(qn, kn, vb, fused=None, fused_arrays=()):
    def body(q_ref, k_ref, v_ref, sl_ref, dist_ref, lmul_ref, o_ref, lse_ref):
        i = pl.program_id(1)
        q = q_ref[...]
        slope = sl_ref[0:1, 0:1]

        def step(j, carry):
            m, l, acc = carry
            sb, rows = _att_block(q, k_ref, j, i, slope, dist_ref, lmul_ref)
            m_new = jnp.maximum(m, jnp.max(sb, axis=-1, keepdims=True))
            alpha = jnp.exp(m - m_new)
            p = jnp.exp(sb - m_new)
            l = alpha * l + jnp.sum(p, axis=-1, keepdims=True)
            acc = alpha * acc + jnp.dot(p.astype(BF16), v_ref[rows, :], preferred_element_type=F32)
            return m_new, l, acc

        m, l, acc = lax.fori_loop(0, _n_key_blocks(i), step,
                                  (jnp.full((Q_TILE, 1), -1e30, F32), jnp.zeros((Q_TILE, 1), F32),
                                   jnp.zeros((Q_TILE, HEAD_DIM), F32)))
        o_ref[...] = acc / l
        lse_ref[...] = m + jnp.log(l)

    wide = HEADS * HEAD_DIM
    qt = pl.BlockSpec((Q_TILE, HEAD_DIM), lambda h, i: (i, h))
    full = pl.BlockSpec((SEQ, HEAD_DIM), lambda h, i: (0, h))
    return _host_call(
        body, 6, 2, fused, _att_first_last, name="attn_fwd", grid=(HEADS, N_Q_TILES),
        in_specs=[qt, full, full, SLOPE_SPEC, TABLE_SPEC, TABLE_SPEC],
        out_specs=[qt, pl.BlockSpec((None, Q_TILE, 1), lambda h, i: (h, i, 0))],
        out_shape=[jax.ShapeDtypeStruct((SEQ, wide), F32), jax.ShapeDtypeStruct((HEADS, SEQ, 1), F32)],
        scratch_shapes=[], sem=("parallel", "parallel"),
        operands=[qn, kn, vb, _alibi_slopes(), *_att_tables()] + list(fused_arrays))


def _attn_bwd(qn, kn, vb, o, lse, d_mix, fused=None, fused_arrays=()):
    def body(q_ref, k_ref, v_ref, o_ref, lse_ref, do_ref, sl_ref, dist_ref, lmul_ref, dq_ref, dk_ref, dv_ref):
        i = pl.program_id(1)
        q = q_ref[...]
        do = do_ref[...]
        do_b = do.astype(BF16)
        slope = sl_ref[0:1, 0:1]
        lse = lse_ref[...]
        delta = jnp.sum(do * o_ref[...], axis=-1, keepdims=True)

        @pl.when(i == 0)
        def _():
            dk_ref[...] = jnp.zeros_like(dk_ref)
            dv_ref[...] = jnp.zeros_like(dv_ref)

        def step(j, dq):
            sb, rows = _att_block(q, k_ref, j, i, slope, dist_ref, lmul_ref)
            p = jnp.exp(sb - lse)
            dp = lax.dot_general(do_b, v_ref[rows, :], NT, preferred_element_type=F32)
            ds = (p * (dp - delta)).astype(BF16)
            dk_ref[rows, :] += lax.dot_general(ds, q, TN, preferred_element_type=F32) * ATT_SCALE
            dv_ref[rows, :] += lax.dot_general(p.astype(BF16), do_b, TN, preferred_element_type=F32)
            return dq + jnp.dot(ds, k_ref[rows, :], preferred_element_type=F32)

        dq = lax.fori_loop(0, _n_key_blocks(i), step, jnp.zeros((Q_TILE, HEAD_DIM), F32))
        dq_ref[...] = dq * ATT_SCALE

    wide = HEADS * HEAD_DIM
    qt = pl.BlockSpec((Q_TILE, HEAD_DIM), lambda h, i: (i, h))
    full = pl.BlockSpec((SEQ, HEAD_DIM), lambda h, i: (0, h))
    return _host_call(
        body, 9, 3, fused, _att_first_last, name="attn_bwd", grid=(HEADS, N_Q_TILES),
        in_specs=[qt, full, full, qt, pl.BlockSpec((None, Q_TILE, 1), lambda h, i: (h, i, 0)),
                  pl.BlockSpec((Q_TILE, HEAD_DIM), lambda h, i: (i, h + HEADS)), SLOPE_SPEC, TABLE_SPEC, TABLE_SPEC],
        out_specs=[qt, full, full], out_shape=[jax.ShapeDtypeStruct((SEQ, wide), F32)] * 3,
        scratch_shapes=[], sem=("parallel", "arbitrary"),
        operands=[qn, kn, vb, o, lse, d_mix, _alibi_slopes(), *_att_tables()] + list(fused_arrays))


def _qk_bwd(proj, q_w, k_w, dqn, dkn, dv):
    def body(aq_ref, ak_ref, qw_ref, kw_ref, dqn_ref, dkn_ref, dv_ref, daq_ref, dak_ref, dav_ref, gq_ref, gk_ref):
        h = pl.program_id(0)

        @pl.when(h == 0)
        def _():
            gq_ref[...] = jnp.zeros_like(gq_ref)
            gk_ref[...] = jnp.zeros_like(gk_ref)

        def one(a_ref, w_ref, d_ref, da_ref, g_ref):
            a = a_ref[...]
            d = d_ref[...]
            rs = lax.rsqrt(jnp.mean(a * a, axis=-1, keepdims=True) + EPS)
            ah = a * rs
            g_ref[...] += jnp.sum(d * ah, axis=0, keepdims=True)
            dah = d * w_ref[...]
            da_ref[...] = (rs * (dah - ah * jnp.mean(dah * ah, axis=-1, keepdims=True))).astype(BF16)

        one(aq_ref, qw_ref, dqn_ref, daq_ref, gq_ref)
        one(ak_ref, kw_ref, dkn_ref, dak_ref, gk_ref)
        dav_ref[...] = dv_ref[...].astype(BF16)

    wide = HEADS * HEAD_DIM
    vec = pl.BlockSpec((1, HEAD_DIM), lambda h: (0, 0))
    return pl.pallas_call(
        body, name="qk_bwd", grid=(HEADS,),
        in_specs=[_head_col(ATT_OFF), _head_col(ATT_OFF + HEADS), vec, vec, _head_col(0), _head_col(0), _head_col(0)],
        out_specs=[_head_col(0)] * 3 + [vec, vec],
        out_shape=[jax.ShapeDtypeStruct((SEQ, wide), BF16)] * 3 + [jax.ShapeDtypeStruct((1, HEAD_DIM), F32)] * 2,
        compiler_params=_params(("arbitrary",)))(proj, proj, q_w, k_w, dqn, dkn, dv)


def _pair_sum(name, partial, theirs, core):
    _, r, c = theirs.shape
    tr = r // 2 if r % 16 == 0 else r

    def body(core_ref, a_ref, b_ref, o_ref):
        o_ref[...] = (a_ref[...].astype(F32) + b_ref[...].astype(F32)).astype(BF16)

    spec = pl.BlockSpec((None, tr, c), lambda q, i, core_ref: (q, i, 0))
    grid_spec = pltpu.PrefetchScalarGridSpec(
        num_scalar_prefetch=1, grid=(4, r // tr),
        in_specs=[pl.BlockSpec((None, tr, c), lambda q, i, core_ref: (2 * q + core_ref[0], i, 0)), spec],
        out_specs=spec)
    return pl.pallas_call(body, name=name, grid_spec=grid_spec, out_shape=jax.ShapeDtypeStruct(theirs.shape, BF16),
                          compiler_params=_params(("parallel", "parallel")))(core, partial, theirs)


def _adamw_step(w, m, v, g):
    nm = ADAM_B1 * m + (1.0 - ADAM_B1) * g
    nv = ADAM_B2 * v + (1.0 - ADAM_B2) * (g * g)
    m_hat = nm / (1.0 - ADAM_B1 ** ADAM_STEP)
    v_hat = nv / (1.0 - ADAM_B2 ** ADAM_STEP)
    return -ADAM_LR * (m_hat / (jnp.sqrt(v_hat) + ADAM_EPS) + ADAM_WD * w), nm, nv


def _adamw(name, w, m, v, addends, tr=None):
    r, c = w.shape
    tr = r if tr is None else tr
    n_add = len(addends)

    def body(*refs):
        w_ref, m_ref, v_ref = refs[:3]
        add_refs = refs[3:3 + n_add]
        g_ref, d_ref, nm_ref, nv_ref = refs[3 + n_add:]
        g = add_refs[0][...].astype(F32)
        for a_ref in add_refs[1:]:
            g = g + a_ref[...].astype(F32)
        g_ref[...] = g
        d_ref[...], nm_ref[...], nv_ref[...] = _adamw_step(w_ref[...], m_ref[...], v_ref[...], g)

    spec = pl.BlockSpec((tr, c), lambda i: (i, 0))
    out = jax.ShapeDtypeStruct((r, c), F32)
    return pl.pallas_call(body, name=name, grid=(r // tr,), in_specs=[spec] * (3 + n_add), out_specs=[spec] * 4,
                          out_shape=[out] * 4, compiler_params=_params(("parallel",)))(w, m, v, *addends)


def _adamw_reduced(name, w, m, v, chip_sums, received, chip, tr):
    r, c = w.shape

    def body(chip_ref, w_ref, m_ref, v_ref, own_ref, r0_ref, r1_ref, r2_ref, g_ref, d_ref, nm_ref, nv_ref):
        g = ((own_ref[...].astype(F32) + r0_ref[...].astype(F32)) + r1_ref[...].astype(F32)) + r2_ref[...].astype(F32)
        g_ref[...] = g
        d_ref[...], nm_ref[...], nv_ref[...] = _adamw_step(w_ref[...], m_ref[...], v_ref[...], g)

    spec = pl.BlockSpec((tr, c), lambda i, chip_ref: (i, 0))

    def slot(k):
        return pl.BlockSpec((None, tr, c), lambda i, chip_ref: (k, i, 0))

    grid_spec = pltpu.PrefetchScalarGridSpec(
        num_scalar_prefetch=1, grid=(r // tr,),
        in_specs=[spec, spec, spec, pl.BlockSpec((None, tr, c), lambda i, chip_ref: (chip_ref[0], i, 0)),
                  slot(0), slot(1), slot(2)],
        out_specs=[spec] * 4)
    out = jax.ShapeDtypeStruct((r, c), F32)
    return pl.pallas_call(body, name=name, grid_spec=grid_spec, out_shape=[out] * 4,
                          compiler_params=_params(("parallel",)))(chip, w, m, v, chip_sums, received, received, received)


def _sum_devices(gathered):
    _, r, c = gathered.shape

    def body(g_ref, o_ref):
        acc = g_ref[0]
        for d in range(1, N_DEV):
            acc = acc + g_ref[d]
        o_ref[...] = acc

    return pl.pallas_call(body, name="sum_devices", out_shape=jax.ShapeDtypeStruct((r, c), F32))(gathered)


def _pack_rows(vectors, rows):
    flat = jnp.concatenate([v.reshape(-1) for v in vectors])
    return jnp.pad(flat, (0, rows * 128 - flat.shape[0])).reshape(rows, 128)


def _unpack(flat, shapes):
    out, off = [], 0
    for shp in shapes:
        n = 1
        for d in shp:
            n *= d
        out.append(flat[off:off + n].reshape(shp))
        off += n
    return out


def _device_step(xs, tgt, mod, norm1_w, norm2_w, lb_logits, hg_norm_w, q_norm_w, k_norm_w, conv_w_full, conv_b,
                 win_g, w_out_x, w_up_x, w_down_x, core=None):
    fused = core is not None
    shift1, scale1, gate1, shift2, scale2, gate2 = (mod[k] for k in range(6))

    h, rstd1 = _norm_fwd("norm1_fwd", xs, norm1_w, scale1, shift1)
    if fused:
        near = (0, 1, 2)
        head_rows, tail_rows = (0, UP_HEAD_ROWS), (UP_HEAD_ROWS, D_MODEL - UP_HEAD_ROWS)
        proj, (wout_g, wup_g) = _mm_blocked_rhs(
            "mm_in", h, win_g, fused_arrays=[w_out_x, w_up_x],
            fused=[_FusedCopies("gather", [w_out_x]), _FusedCopies("gather", [w_up_x], peers=near, rows=head_rows)])
        (a_out, o_pre), (wup_g,) = _hgrn_fwd(
            proj, lb_logits, hg_norm_w, fused_arrays=[w_up_x, wup_g],
            fused=_FusedCopies("gather_more", [w_up_x, wup_g], peers=near, rows=tail_rows, relay_rows=head_rows))
        wout_g, = _forward_to_sibling("allgather_stage2_out", [wout_g])
        wout_full = wout_g.reshape(D_MODEL, D_MODEL)
        (qn, kn, vb), _ = _qk_prep(proj, q_norm_w, k_norm_w)
        (att_o, lse), (wup_g,) = _attn_fwd(qn, kn, vb, _FusedCopies("relay", [wup_g], rows=tail_rows), [wup_g])
    else:
        proj = _mm_blocked_rhs("mm_in", h, win_g)
        (a_out, o_pre), _ = _hgrn_fwd(proj, lb_logits, hg_norm_w)
        wup_g, wout_full, wdown_full = w_up_x, w_out_x, w_down_x
        (qn, kn, vb), _ = _qk_prep(proj, q_norm_w, k_norm_w)
        (att_o, lse), _ = _attn_fwd(qn, kn, vb)
    mixin = jnp.concatenate([a_out, att_o.astype(BF16)], axis=1)
    if fused:
        down_head = (0, DOWN_HEAD_ROWS)
        mix, (wup_g, wdown_g) = _mm_plain(
            "mm_out", mixin, wout_full, NN, 512, 1024, F32, fused_arrays=[wup_g, w_down_x],
            fused=[_FusedCopies("forward", [wup_g]), _FusedCopies("gather", [w_down_x], rows=down_head)])
    else:
        mix = _mm_plain("mm_out", mixin, wout_full, NN, 512, 1024, F32)
    x1, h2, rstd2 = _norm_fwd("norm2_fwd", xs, norm2_w, scale2, shift2, resid=mix, gate=gate1)
    if fused:
        down_tail = (DOWN_HEAD_ROWS, FF_BLK - DOWN_HEAD_ROWS)
        u, (wdown_g,) = _mm_blocked_rhs(
            "mm_up", h2, wup_g, fused_arrays=[w_down_x, wdown_g],
            fused=_FusedCopies("gather_more", [w_down_x, wdown_g], rows=down_tail))
        y, (wdown_g,) = _conv_gate_fwd(u, conv_w_full, conv_b, _FusedCopies("forward", [wdown_g]), [wdown_g])
        wdown_full = wdown_g.reshape(D_FF, D_MODEL)
    else:
        u = _mm_blocked_rhs("mm_up", h2, wup_g)
        y = _conv_gate_fwd(u, conv_w_full, conv_b)
    ffn = _mm_plain("mm_down", y, wdown_full, NN, MM_TILE, 512, F32)
    loss_v, dout, dffn, dgate2 = _loss_head(x1, ffn, gate2, tgt)

    dy = _mm_plain("mm_down_dx", dffn, wdown_full, NT, MM_TILE, UP_BLK, BF16)
    gw_down = _mm_plain("mm_down_dw", y, dffn, TN, UP_BLK, 1024, BF16)
    da, dg, gconv_w, gconv_b = _conv_gate_bwd(u, dy, conv_w_full, conv_b)
    dh2 = _mm_halves_rhs_t("mm_up_dx", da, dg, wup_g)
    gw_up = _mm_halves_wgrad("mm_up_dw", h2, da, dg)
    if fused:
        part_up, part_down = gw_up, gw_down.reshape(N_DEV, FF_BLK, D_MODEL)
        (dx1, dmix, dshift2, dscale2, gnorm2, dgate1), (sib_up,) = _norm_bwd(
            "norm2_bwd", dh2, x1, rstd2, norm2_w, scale2, dout, mix=mix, gate=gate1,
            fused=_FusedCopies("sibling", [part_up]), fused_arrays=[part_up])
    else:
        dx1, dmix, dshift2, dscale2, gnorm2, dgate1 = _norm_bwd(
            "norm2_bwd", dh2, x1, rstd2, norm2_w, scale2, dout, mix=mix, gate=gate1)
    gw_out = _mm_plain("mm_out_dw", mixin, dmix, TN, 512, 1024, BF16)
    if fused:
        part_out = gw_out.reshape(N_DEV, OUT_BLK, D_MODEL)
        dmixin, (sib_out, sib_down) = _mm_plain(
            "mm_out_dx", dmix, wout_full, NT, 512, 1024, F32,
            fused=_FusedCopies("sibling", [part_out, part_down]), fused_arrays=[part_out, part_down])
        cs_up = _pair_sum("grad_pair_sum_up", part_up, sib_up, core)
        cs_out = _pair_sum("grad_pair_sum_out", part_out, sib_out, core)
        cs_down = _pair_sum("grad_pair_sum_down", part_down, sib_down, core)
        (dhq, dhf, dhi, dhg, glog, ghg), (fc_up,) = _hgrn_bwd(
            proj, lb_logits, hg_norm_w, o_pre, dmixin, _FusedCopies("chips", [cs_up]), [cs_up])
        (dqn, dkn, dvv), (fc_down,) = _attn_bwd(
            qn, kn, vb, att_o, lse, dmixin, _FusedCopies("chips", [cs_down], rows=(0, DOWN_EXCHANGE_HEAD)), [cs_down])
    else:
        dmixin = _mm_plain("mm_out_dx", dmix, wout_full, NT, 512, 1024, F32)
        (dhq, dhf, dhi, dhg, glog, ghg), _ = _hgrn_bwd(proj, lb_logits, hg_norm_w, o_pre, dmixin)
        (dqn, dkn, dvv), _ = _attn_bwd(qn, kn, vb, att_o, lse, dmixin)
    daq, dak, dav, gqw, gkw = _qk_bwd(proj, q_norm_w, k_norm_w, dqn, dkn, dvv)
    dproj = jnp.concatenate([dhq, dhf, dhi, dhg, daq, dak, dav], axis=1)
    if fused:
        down_tail = (DOWN_EXCHANGE_HEAD, FF_BLK - DOWN_EXCHANGE_HEAD)
        gw_in, (fc_out, fc_down) = _mm_wgrad_blocked(
            "mm_in_dw", h, dproj, fused_arrays=[cs_out, cs_down, fc_down],
            fused=[_FusedCopies("chips", [cs_out]), _FusedCopies("chips_more", [cs_down, fc_down], rows=down_tail)])
        from_sibling, = _exchange_sibling("grad_exchange_sibling_b", [gw_in])
        cs_in = _pair_sum("grad_pair_sum_in", gw_in, from_sibling, core)
        dh, (fc_in,) = _mm_blocked_rhs_t("mm_in_dx", dproj, win_g, fused=_FusedCopies("chips", [cs_in]),
                                         fused_arrays=[cs_in])
        large = [(cs_in, fc_in), (cs_out, fc_out), (cs_up, fc_up), (cs_down, fc_down)]
    else:
        gw_in = _mm_wgrad_blocked("mm_in_dw", h, dproj)
        dh = _mm_blocked_rhs_t("mm_in_dx", dproj, win_g)
        large = [gw_in, gw_out, gw_up, gw_down]
    grad_x, dshift1, dscale1, gnorm1 = _norm_bwd("norm1_bwd", dh, xs, rstd1, norm1_w, scale1, dx1)
    gmod = jnp.concatenate([dshift1, dscale1, dgate1, dshift2, dscale2, dgate2], axis=1)
    return (loss_v, grad_x, gmod, gnorm1, gnorm2, glog, ghg, gqw, gkw, gconv_b, gconv_w, *large)


def kernel(x, c, w_ada, b_ada, norm1_w, w_in, lb_logits, hg_norm_w, q_norm_w, k_norm_w, w_out, norm2_w, w_up, conv_w, conv_b, w_down, loss_target, m_w_ada, m_b_ada, m_norm1_w, m_w_in, m_lb_logits, m_hg_norm_w, m_q_norm_w, m_k_norm_w, m_w_out, m_norm2_w, m_w_up, m_conv_w, m_conv_b, m_w_down, v_w_ada, v_b_ada, v_norm1_w, v_w_in, v_lb_logits, v_hg_norm_w, v_q_norm_w, v_k_norm_w, v_w_out, v_norm2_w, v_w_up, v_conv_w, v_conv_b, v_w_down):
    ix, iy, ic = lax.axis_index("x"), lax.axis_index("y"), lax.axis_index("c")
    me = 4 * ix + 2 * iy + ic
    my_chip = 2 * ix + iy

    xs = x[0]
    tgt = loss_target[0]

    win_g, = _allgather_weights([w_in[0].astype(BF16)])

    first = _allgather_vmem(_pack_rows([c, conv_w[0]], 40), "allgather_c_conv_w").reshape(N_DEV, 40 * 128)
    c_all = first[:, :D_MODEL]
    conv_w_full = (first[:, D_MODEL:D_MODEL + 3 * FF_BLK].reshape(N_DEV, 3, FF_BLK).transpose(1, 0, 2)
                   .reshape(3, D_FF))

    b_blk = lax.dynamic_slice_in_dim(b_ada, me * ADA_BLK, ADA_BLK, axis=1)
    mod_cols = _ada_fwd(c_all, w_ada[0], b_blk)
    mod_all = _allgather_vmem(mod_cols, "allgather_mod").reshape(N_DEV, N_DEV, ADA_BLK)
    mod = lax.dynamic_index_in_dim(mod_all, me, axis=1, keepdims=False).reshape(6, 1, D_MODEL)

    (loss_v, grad_x, gmod, gnorm1, gnorm2, glog, ghg, gqw, gkw, gconv_b, gconv_w,
     rs_in, rs_out, rs_up, rs_down) = _device_step(
        xs, tgt, mod, norm1_w, norm2_w, lb_logits, hg_norm_w, q_norm_w, k_norm_w, conv_w_full, conv_b,
        win_g, w_out[0].astype(BF16), w_up[0].astype(BF16), w_down[0].astype(BF16),
        core=jnp.reshape(ic, (1,)).astype(jnp.int32))

    small_shapes = [(1, 6 * D_MODEL), (1, D_MODEL), (1, D_MODEL), (2, HEADS * HEAD_DIM), (1, HEAD_DIM),
                    (1, HEAD_DIM), (1, HEAD_DIM), (1, D_FF), (3, D_FF), (1, 1)]
    small = [gmod, gnorm1, gnorm2, glog, ghg, gqw, gkw, gconv_b, gconv_w, loss_v[:, 0:1]]
    n_small = sum(a.size for a in small)
    rows = -(-n_small // 1024) * 8
    gathered = _allgather_vmem(_pack_rows(small, rows), "allgather_small").reshape(N_DEV, rows, 128)
    summed = _sum_devices(gathered).reshape(-1)
    (g_b_ada, g_norm1, g_norm2, g_lb, g_hg, g_q, g_k, g_conv_b, g_conv_w_full, loss_sum) = _unpack(summed, small_shapes)
    loss = loss_sum[0, 0]
    g_conv_w = lax.dynamic_slice_in_dim(g_conv_w_full, me * FF_BLK, FF_BLK, axis=1)

    gmod_all = gathered[:, :6 * D_MODEL // 128, :].reshape(N_DEV, 6 * D_MODEL)
    gmod_cols = lax.dynamic_slice_in_dim(gmod_all, me * ADA_BLK, ADA_BLK, axis=1)
    g_w_ada_raw = _ada_wgrad(c_all, gmod_cols)

    chip = jnp.reshape(my_chip, (1,)).astype(jnp.int32)

    def big_update(name, w, m, v, rs, tr):
        chip_sums, received = rs
        return _adamw_reduced(name, w[0], m[0], v[0], chip_sums, received, chip, tr)

    r_in = big_update("adamw_w_in", w_in, m_w_in, v_w_in, rs_in, 256)
    r_out = big_update("adamw_w_out", w_out, m_w_out, v_w_out, rs_out, 128)
    r_up = big_update("adamw_w_up", w_up, m_w_up, v_w_up, rs_up, 256)
    r_down = big_update("adamw_w_down", w_down, m_w_down, v_w_down, rs_down, 176)
    r_ada = _adamw("adamw_w_ada", w_ada[0], m_w_ada[0], v_w_ada[0], [g_w_ada_raw], tr=256)
    r_convw = _adamw("adamw_conv_w", conv_w[0], m_conv_w[0], v_conv_w[0], [g_conv_w])

    rep_shapes = [(1, 6 * D_MODEL), (1, D_MODEL), (1, D_MODEL), (2, HEADS * HEAD_DIM), (1, HEAD_DIM),
                  (1, HEAD_DIM), (1, HEAD_DIM), (1, D_FF)]
    rep_rows = -(-sum(a * b for a, b in rep_shapes) // 1024) * 8
    pack = lambda arrs: _pack_rows(arrs, rep_rows)
    rep = _adamw("adamw_small",
                 pack([b_ada, norm1_w, norm2_w, lb_logits, hg_norm_w, q_norm_w, k_norm_w, conv_b]),
                 pack([m_b_ada, m_norm1_w, m_norm2_w, m_lb_logits, m_hg_norm_w, m_q_norm_w, m_k_norm_w, m_conv_b]),
                 pack([v_b_ada, v_norm1_w, v_norm2_w, v_lb_logits, v_hg_norm_w, v_q_norm_w, v_k_norm_w, v_conv_b]),
                 [pack([g_b_ada, g_norm1, g_norm2, g_lb, g_hg, g_q, g_k, g_conv_b])])
    rep = [_unpack(r.reshape(-1), rep_shapes) for r in rep]

    def big(r):
        return [a[None] for a in r]

    order = {"w_ada": big(r_ada), "b_ada": [r[0] for r in rep], "norm1_w": [r[1] for r in rep],
             "w_in": big(r_in), "lb_logits": [r[3] for r in rep], "hg_norm_w": [r[4] for r in rep],
             "q_norm_w": [r[5] for r in rep], "k_norm_w": [r[6] for r in rep], "w_out": big(r_out),
             "norm2_w": [r[2] for r in rep], "w_up": big(r_up), "conv_w": big(r_convw),
             "conv_b": [r[7] for r in rep], "w_down": big(r_down)}
    names = ["w_ada", "b_ada", "norm1_w", "w_in", "lb_logits", "hg_norm_w", "q_norm_w", "k_norm_w", "w_out",
             "norm2_w", "w_up", "conv_w", "conv_b", "w_down"]
    outs = [loss, grad_x[None]]
    for kind in range(4):
        outs += [order[n][kind] for n in names]
    return tuple(outs)
```

```python
import jax
import jax.numpy as jnp
import numpy as np
from jax import lax
from jax.experimental import pallas as pl
from jax.experimental.pallas import tpu as pltpu

F32 = jnp.float32
BF16 = jnp.bfloat16

N_DEV = 8
SEQ = 2048
D_MODEL = 2048
HEADS = 8
HEAD_DIM = 128
IN_COLS = 7168
IN_BLK = IN_COLS // N_DEV
D_FF = 5632
UP_BLK = 2 * D_FF // N_DEV
FF_BLK = D_FF // N_DEV
ADA_BLK = 6 * D_MODEL // N_DEV
OUT_BLK = D_MODEL // N_DEV
EPS = 1e-6
CHUNK = 16
ROW_TILE = 256
MM_TILE = 1024
V7X_VMEM_LIMIT = 56 * 1024 * 1024

ADAM_LR = 0.001
ADAM_B1 = 0.9
ADAM_B2 = 0.999
ADAM_EPS = 1e-08
ADAM_WD = 0.01
ADAM_STEP = 10

NN = (((1,), (0,)), ((), ()))
NT = (((1,), (1,)), ((), ()))
TN = (((0,), (0,)), ((), ()))
MESH = pl.DeviceIdType.MESH


def _params(sem=None, vmem=V7X_VMEM_LIMIT):
    return pltpu.CompilerParams(dimension_semantics=sem, vmem_limit_bytes=vmem)


def _sigmoid(x):
    return 1.0 / (1.0 + jnp.exp(-x))


def _dsilu(x, s):
    return s * (1.0 + x * (1.0 - s))


def _lane_sum(x, ones_bf16):
    return jnp.dot(x.astype(BF16), ones_bf16, preferred_element_type=F32)


def _mesh_pos():
    return lax.axis_index("x"), lax.axis_index("y"), lax.axis_index("c")


def _allgather_vmem(x_blk, name):
    m_per, n = x_blk.shape

    def body(x_ref, out_ref, send_sems, recv_sems, local_sem):
        x, y, c = _mesh_pos()
        me, sibling = (x, y, c), (x, y, 1 - c)
        chips = [(1 - x, y), (x, 1 - y), (1 - x, 1 - y)]

        def rows(px, py, pc):
            return out_ref.at[pl.ds((4 * px + 2 * py + pc) * m_per, m_per), :]

        def copy(k, block, to, src=None):
            return pltpu.make_async_remote_copy(
                src_ref=rows(*block) if src is None else src, dst_ref=rows(*block),
                send_sem=send_sems.at[k], recv_sem=recv_sems.at[k], device_id=to, device_id_type=MESH)

        mine = pltpu.make_async_copy(x_ref, rows(*me), local_sem)
        mine.start()
        first = [copy(0, me, sibling, src=x_ref)]
        first += [copy(1 + j, me, (*chip, c), src=x_ref) for j, chip in enumerate(chips)]
        for cp in first:
            cp.start()
        passed = [copy(4 + j, (*chip, c), sibling) for j, chip in enumerate(chips)]
        for j, chip in enumerate(chips):
            copy(1 + j, (*chip, c), me).wait_recv()
            passed[j].start()
        copy(0, sibling, me).wait_recv()
        for j, chip in enumerate(chips):
            copy(4 + j, (*chip, 1 - c), me).wait_recv()
        for cp in first + passed:
            cp.wait_send()
        mine.wait()

    return pl.pallas_call(
        body, name=name,
        out_shape=jax.ShapeDtypeStruct((N_DEV * m_per, n), x_blk.dtype),
        in_specs=[pl.BlockSpec(memory_space=pltpu.VMEM)],
        out_specs=pl.BlockSpec(memory_space=pltpu.VMEM),
        scratch_shapes=[pltpu.SemaphoreType.DMA((7,)), pltpu.SemaphoreType.DMA((7,)), pltpu.SemaphoreType.DMA],
    )(x_blk)


def _flip(v, bit):
    return v + bit - 2 * v * bit


def _relay_chips(x, y, c):
    return (_flip(x, 1 - c), _flip(y, c)), (_flip(x, c), _flip(y, 1 - c))


UP_EXCHANGE_HEAD = 1856
DOWN_EXCHANGE_HEAD = 560
DOWN_HEAD_ROWS = 192
UP_HEAD_ROWS = 768
GATHER_PARTS = 4


def _allgather_weights(blocks):
    n_arr = len(blocks)
    parts = GATHER_PARTS

    def body(*refs):
        ins, outs = refs[:n_arr], refs[n_arr:2 * n_arr]
        send_sems, recv_sems, local_sems = refs[2 * n_arr:]
        x, y, c = _mesh_pos()
        me, sibling = (x, y, c), (x, y, 1 - c)
        near = [(1 - x, y), (x, 1 - y)]
        chips = near + [(1 - x, 1 - y)]
        relay_from, relay_to = _relay_chips(x, y, c)

        def rows(a, p):
            hr = ins[a].shape[0] // parts
            return pl.ds(p * hr, hr)

        def slot(a, pos, p):
            return outs[a].at[4 * pos[0] + 2 * pos[1] + pos[2], rows(a, p)]

        def copy(a, k, p, src, lands, to):
            return pltpu.make_async_remote_copy(
                src_ref=src, dst_ref=slot(a, lands, p), send_sem=send_sems.at[a, k, p], recv_sem=recv_sems.at[a, k, p],
                device_id=to, device_id_type=MESH)

        sent = []
        local = [pltpu.make_async_copy(ins[a], outs[a].at[4 * x + 2 * y + c], local_sems.at[a]) for a in range(n_arr)]
        for cp in local:
            cp.start()
        for p in range(parts):
            for a in range(n_arr):
                own = ins[a].at[rows(a, p)]
                sent.append(copy(a, 0, p, own, me, sibling))
                sent += [copy(a, 1 + j, p, own, me, (*chip, c)) for j, chip in enumerate(near)]
        for cp in sent:
            cp.start()

        def start(cp):
            cp.start()
            sent.append(cp)

        for p in range(parts):
            for a in range(n_arr):
                for j, chip in enumerate(near):
                    copy(a, 1 + j, p, ins[a].at[rows(a, p)], (*chip, c), me).wait_recv()
                    start(copy(a, 4 + j, p, slot(a, (*chip, c), p), (*chip, c), sibling))
                start(copy(a, 3, p, slot(a, (*relay_from, c), p), (*relay_from, c), (*relay_to, c)))
        for p in range(parts):
            for a in range(n_arr):
                copy(a, 3, p, ins[a].at[rows(a, p)], (*chips[2], c), me).wait_recv()
                start(copy(a, 6, p, slot(a, (*chips[2], c), p), (*chips[2], c), sibling))
        for p in range(parts):
            for a in range(n_arr):
                copy(a, 0, p, ins[a].at[rows(a, p)], sibling, me).wait_recv()
                for j, chip in enumerate(chips):
                    copy(a, 4 + j, p, ins[a].at[rows(a, p)], (*chip, 1 - c), me).wait_recv()
        for cp in sent:
            cp.wait_send()
        for cp in local:
            cp.wait()

    return pl.pallas_call(
        body, name="allgather_weights",
        out_shape=[jax.ShapeDtypeStruct((N_DEV,) + b.shape, b.dtype) for b in blocks],
        in_specs=[pl.BlockSpec(memory_space=pltpu.HBM)] * n_arr, out_specs=[pl.BlockSpec(memory_space=pltpu.HBM)] * n_arr,
        scratch_shapes=[pltpu.SemaphoreType.DMA((n_arr, 7, parts)), pltpu.SemaphoreType.DMA((n_arr, 7, parts)),
                        pltpu.SemaphoreType.DMA((n_arr,))],
    )(*blocks)


HBM_SPEC = pl.BlockSpec(memory_space=pltpu.HBM)


class _FusedCopies:
    def __init__(self, kind, arrays, peers=(0, 1, 2, 3), rows=None, relay_rows=None):
        self.kind = kind
        self.peers = peers
        self.rows = rows
        self.relay_rows = relay_rows
        n = len(arrays) // 2 if kind == "gather_more" else len(arrays)
        self.n = n
        self.n_in = len(arrays)
        self.aliases = {}
        if kind == "gather":
            self.out_shape = [jax.ShapeDtypeStruct((N_DEV,) + a.shape, a.dtype) for a in arrays]
            self.scratch_shapes = [pltpu.SemaphoreType.DMA((n, 4, GATHER_PARTS)),
                                   pltpu.SemaphoreType.DMA((n, 4, GATHER_PARTS)), pltpu.SemaphoreType.DMA((n,))]
        elif kind == "gather_more":
            self.out_shape = [jax.ShapeDtypeStruct(a.shape, a.dtype) for a in arrays[n:]]
            self.scratch_shapes = [pltpu.SemaphoreType.DMA((n, 5, GATHER_PARTS)),
                                   pltpu.SemaphoreType.DMA((n, 5, GATHER_PARTS)), pltpu.SemaphoreType.DMA((n,))]
            self.aliases = {n + a: a for a in range(n)}
        elif kind == "relay":
            self.out_shape = [jax.ShapeDtypeStruct(a.shape, a.dtype) for a in arrays]
            self.scratch_shapes = [pltpu.SemaphoreType.DMA((n,)), pltpu.SemaphoreType.DMA((n,))]
            self.aliases = {a: a for a in range(n)}
        elif kind == "forward":
            self.out_shape = [jax.ShapeDtypeStruct(a.shape, a.dtype) for a in arrays]
            self.scratch_shapes = [pltpu.SemaphoreType.DMA((n, 3)), pltpu.SemaphoreType.DMA((n, 3))]
            self.aliases = {a: a for a in range(n)}
        elif kind == "sibling":
            self.out_shape = [jax.ShapeDtypeStruct((4,) + a.shape[1:], a.dtype) for a in arrays]
            self.scratch_shapes = [pltpu.SemaphoreType.DMA((n, 4)), pltpu.SemaphoreType.DMA((n, 4))]
        elif kind == "chips_more":
            n = self.n = len(arrays) // 2
            self.out_shape = [jax.ShapeDtypeStruct(a.shape, a.dtype) for a in arrays[n:]]
            self.scratch_shapes = [pltpu.SemaphoreType.DMA((n, 3)), pltpu.SemaphoreType.DMA((n, 3))]
            self.aliases = {n + a: a for a in range(n)}
        else:
            self.out_shape = [jax.ShapeDtypeStruct((3,) + a.shape[1:], a.dtype) for a in arrays]
            self.scratch_shapes = [pltpu.SemaphoreType.DMA((n, 3)), pltpu.SemaphoreType.DMA((n, 3))]
        self.in_specs = [HBM_SPEC] * self.n_in
        self.out_specs = [HBM_SPEC] * n
        self.n_scratch = len(self.scratch_shapes)

    def copies(self, ins, outs, sems):
        x, y, c = _mesh_pos()
        chips = [(1 - x, y), (x, 1 - y), (1 - x, 1 - y)]
        sibling = (x, y, 1 - c)
        starts, waits = [], []
        relay_from, relay_to = _relay_chips(x, y, c)

        def relayed(a, buf, lands, send_sem, recv_sem, rows):
            first, count = rows or (0, buf.shape[1])
            span = pl.ds(first, count)
            return pltpu.make_async_remote_copy(
                src_ref=buf.at[4 * relay_from[0] + 2 * relay_from[1] + c, span],
                dst_ref=outs[a].at[4 * lands[0] + 2 * lands[1] + c, span], send_sem=send_sem, recv_sem=recv_sem,
                device_id=(*relay_to, c), device_id_type=MESH)

        if self.kind in ("gather", "gather_more"):
            send_sems, recv_sems, local_sems = sems
            me = (x, y, c)
            peers = [sibling] + [(px, py, c) for px, py in chips]

            def slot(a, pos):
                return outs[a].at[4 * pos[0] + 2 * pos[1] + pos[2]]

            def span(a, p=None):
                first, count = self.rows or (0, ins[a].shape[0])
                if p is None:
                    return pl.ds(first, count)
                return pl.ds(first + p * (count // GATHER_PARTS), count // GATHER_PARTS)

            def remote(a, k, p, lands_from):
                return pltpu.make_async_remote_copy(
                    src_ref=ins[a].at[span(a, p)], dst_ref=slot(a, lands_from).at[span(a, p)],
                    send_sem=send_sems.at[a, k, p], recv_sem=recv_sems.at[a, k, p], device_id=peers[k],
                    device_id_type=MESH)

            for a in range(self.n):
                local = pltpu.make_async_copy(ins[a].at[span(a)], slot(a, me).at[span(a)], local_sems.at[a])
                starts.append(local)
                waits.append(local)
            for p in range(GATHER_PARTS):
                for a in range(self.n):
                    for k in self.peers:
                        starts.append(remote(a, k, p, me))
                        waits.append(remote(a, k, p, peers[k]))
            if self.kind == "gather_more" and self.relay_rows is not None:
                for a in range(self.n):
                    buf = ins[self.n + a]
                    starts.append(relayed(a, buf, relay_from, send_sems.at[a, 4, 0], recv_sems.at[a, 4, 0],
                                          self.relay_rows))
                    waits.append(relayed(a, buf, chips[2], send_sems.at[a, 4, 0], recv_sems.at[a, 4, 0],
                                         self.relay_rows))
        elif self.kind == "relay":
            send_sems, recv_sems = sems
            for a in range(self.n):
                starts.append(relayed(a, ins[a], relay_from, send_sems.at[a], recv_sems.at[a], self.rows))
                waits.append(relayed(a, ins[a], chips[2], send_sems.at[a], recv_sems.at[a], self.rows))
        elif self.kind == "forward":
            send_sems, recv_sems = sems

            def passed_on(a, j, pc_src, pc_dst):
                px, py = chips[j]
                return pltpu.make_async_remote_copy(
                    src_ref=ins[a].at[4 * px + 2 * py + pc_src], dst_ref=outs[a].at[4 * px + 2 * py + pc_dst],
                    send_sem=send_sems.at[a, j], recv_sem=recv_sems.at[a, j], device_id=sibling, device_id_type=MESH)

            for a in range(self.n):
                for j in range(3):
                    starts.append(passed_on(a, j, c, c))
                    waits.append(passed_on(a, j, c, 1 - c))
        elif self.kind == "sibling":
            send_sems, recv_sems = sems
            for a in range(self.n):
                for q in range(4):
                    cp = pltpu.make_async_remote_copy(
                        src_ref=ins[a].at[2 * q + 1 - c], dst_ref=outs[a].at[q], send_sem=send_sems.at[a, q],
                        recv_sem=recv_sems.at[a, q], device_id=sibling, device_id_type=MESH)
                    starts.append(cp)
                    waits.append(cp)
        else:
            send_sems, recv_sems = sems
            for a in range(self.n):
                first, count = self.rows or (0, ins[a].shape[1])
                span = pl.ds(first, count)
                for j, (px, py) in enumerate(chips):
                    cp = pltpu.make_async_remote_copy(
                        src_ref=ins[a].at[2 * px + py, span], dst_ref=outs[a].at[j, span],
                        send_sem=send_sems.at[a, j], recv_sem=recv_sems.at[a, j], device_id=(px, py, c),
                        device_id_type=MESH)
                    starts.append(cp)
                    waits.append(cp)
        return starts, waits


def _fused_groups(fused):
    if fused is None:
        return []
    return list(fused) if isinstance(fused, (list, tuple)) else [fused]


def _host_body(body, n_in, n_out, fused, first_last):
    groups = _fused_groups(fused)
    if not groups:
        return body
    n_fin, n_fout = sum(g.n_in for g in groups), sum(g.n for g in groups)
    n_fsem = sum(g.n_scratch for g in groups)

    def wrapped(*refs):
        core_in, f_in = refs[:n_in], refs[n_in:n_in + n_fin]
        core_out = refs[n_in + n_fin:n_in + n_fin + n_out]
        f_out = refs[n_in + n_fin + n_out:n_in + n_fin + n_out + n_fout]
        rest = refs[n_in + n_fin + n_out + n_fout:]
        core_scratch, f_sems = rest[:len(rest) - n_fsem], rest[len(rest) - n_fsem:]
        starts, waits = [], []
        for g in groups:
            s, w = g.copies(f_in[:g.n_in], f_out[:g.n], f_sems[:g.n_scratch])
            f_in, f_out, f_sems = f_in[g.n_in:], f_out[g.n:], f_sems[g.n_scratch:]
            starts += s
            waits += w
        first, last = first_last()

        @pl.when(first)
        def _():
            for cp in starts:
                cp.start()

        body(*core_in, *core_out, *core_scratch)

        @pl.when(last)
        def _():
            for cp in waits:
                cp.wait()

    return wrapped


def _host_call(body, n_in, n_out, fused, first_last, *, name, grid, in_specs, out_specs, out_shape, scratch_shapes,
               sem, operands):
    aliases = {}
    in_specs, out_specs, out_shape, scratch_shapes = list(in_specs), list(out_specs), list(out_shape), list(scratch_shapes)
    fin, fout = n_in, n_out
    for g in _fused_groups(fused):
        aliases.update({fin + fi: fout + fo for fi, fo in g.aliases.items()})
        fin, fout = fin + g.n_in, fout + g.n
        in_specs += g.in_specs
        out_specs += g.out_specs
        out_shape += g.out_shape
        scratch_shapes += g.scratch_shapes
        sem = tuple("arbitrary" for _ in sem)
    res = pl.pallas_call(_host_body(body, n_in, n_out, fused, first_last), name=name, grid=grid, in_specs=in_specs,
                         out_specs=out_specs, out_shape=out_shape, scratch_shapes=scratch_shapes,
                         input_output_aliases=aliases, compiler_params=_params(sem))(*operands)
    return list(res[:n_out]), list(res[n_out:])


def _forward_to_sibling(name, gathered):
    n_arr = len(gathered)

    def body(*refs):
        ins, outs = refs[:n_arr], refs[n_arr:2 * n_arr]
        send_sems, recv_sems = refs[2 * n_arr:]
        x, y, c = _mesh_pos()
        chips = [(1 - x, y), (x, 1 - y), (1 - x, 1 - y)]

        def copy(a, j, pc):
            px, py = chips[j]
            s = 4 * px + 2 * py + pc
            return pltpu.make_async_remote_copy(
                src_ref=ins[a].at[s], dst_ref=outs[a].at[s], send_sem=send_sems.at[a, j], recv_sem=recv_sems.at[a, j],
                device_id=(x, y, 1 - c), device_id_type=MESH)

        for a in range(n_arr):
            for j in range(3):
                copy(a, j, c).start()
        for a in range(n_arr):
            for j in range(3):
                copy(a, j, 1 - c).wait_recv()
                copy(a, j, c).wait_send()

    return pl.pallas_call(
        body, name=name,
        out_shape=[jax.ShapeDtypeStruct(g.shape, g.dtype) for g in gathered],
        in_specs=[HBM_SPEC] * n_arr, out_specs=[HBM_SPEC] * n_arr,
        input_output_aliases={a: a for a in range(n_arr)},
        scratch_shapes=[pltpu.SemaphoreType.DMA((n_arr, 3)), pltpu.SemaphoreType.DMA((n_arr, 3))],
    )(*gathered)


def _exchange_sibling(name, partials):
    n_arr = len(partials)

    def body(*refs):
        ins, outs = refs[:n_arr], refs[n_arr:2 * n_arr]
        send_sems, recv_sems = refs[2 * n_arr:]
        x, y, c = _mesh_pos()
        copies = [pltpu.make_async_remote_copy(
            src_ref=ins[a].at[2 * q + 1 - c], dst_ref=outs[a].at[q], send_sem=send_sems.at[a, q],
            recv_sem=recv_sems.at[a, q], device_id=(x, y, 1 - c), device_id_type=MESH)
            for a in range(n_arr) for q in range(4)]
        for cp in copies:
            cp.start()
        for cp in copies:
            cp.wait_recv()
        for cp in copies:
            cp.wait_send()

    return pl.pallas_call(
        body, name=name,
        out_shape=[jax.ShapeDtypeStruct((4,) + p.shape[1:], p.dtype) for p in partials],
        in_specs=[HBM_SPEC] * n_arr, out_specs=[HBM_SPEC] * n_arr,
        scratch_shapes=[pltpu.SemaphoreType.DMA((n_arr, 4)), pltpu.SemaphoreType.DMA((n_arr, 4))],
    )(*partials)


def _matmul(name, a, b, dims, grid, a_spec, b_spec, o_spec, out_shape, acc_axis=None, fused=None, fused_arrays=()):
    def body(a_ref, b_ref, o_ref):
        r = lax.dot_general(a_ref[...], b_ref[...], dims, preferred_element_type=F32)
        if acc_axis is None:
            o_ref[...] = r.astype(o_ref.dtype)
        else:
            k = pl.program_id(acc_axis)

            @pl.when(k == 0)
            def _():
                o_ref[...] = r

            @pl.when(k > 0)
            def _():
                o_ref[...] += r

    sem = tuple("arbitrary" if i == acc_axis else "parallel" for i in range(len(grid)))
    if fused is None:
        return pl.pallas_call(body, name=name, grid=grid, in_specs=[a_spec, b_spec], out_specs=o_spec,
                              out_shape=out_shape, compiler_params=_params(sem))(a, b)

    def first_last():
        first = last = None
        for ax, n in enumerate(grid):
            f, l = pl.program_id(ax) == 0, pl.program_id(ax) == n - 1
            first, last = (f, l) if first is None else (first & f, last & l)
        return first, last

    (out,), extra = _host_call(body, 2, 1, fused, first_last, name=name, grid=grid, in_specs=[a_spec, b_spec],
                               out_specs=[o_spec], out_shape=[out_shape], scratch_shapes=[], sem=sem,
                               operands=[a, b] + list(fused_arrays))
    return out, extra


def _mm_blocked_rhs(name, a, w_g, tm=MM_TILE, fused=None, fused_arrays=()):
    m, k = a.shape
    nb = w_g.shape[2]
    return _matmul(name, a, w_g, NN, (N_DEV, m // tm),
                   pl.BlockSpec((tm, k), lambda j, i: (i, 0)),
                   pl.BlockSpec((None, k, nb), lambda j, i: (j, 0, 0)),
                   pl.BlockSpec((tm, nb), lambda j, i: (i, j)),
                   jax.ShapeDtypeStruct((m, N_DEV * nb), F32), fused=fused, fused_arrays=fused_arrays)


def _mm_blocked_rhs_t(name, a, w_g, tm=MM_TILE, fused=None, fused_arrays=()):
    m = a.shape[0]
    n, nb = w_g.shape[1], w_g.shape[2]
    return _matmul(name, a, w_g, NT, (m // tm, N_DEV),
                   pl.BlockSpec((tm, nb), lambda i, j: (i, j)),
                   pl.BlockSpec((None, n, nb), lambda i, j: (j, 0, 0)),
                   pl.BlockSpec((tm, n), lambda i, j: (i, 0)),
                   jax.ShapeDtypeStruct((m, n), F32), acc_axis=1, fused=fused, fused_arrays=fused_arrays)


def _mm_wgrad_blocked(name, act, dcols, tk=MM_TILE, fused=None, fused_arrays=()):
    t, k = act.shape
    nb = dcols.shape[1] // N_DEV
    return _matmul(name, act, dcols, TN, (N_DEV, k // tk),
                   pl.BlockSpec((t, tk), lambda j, i: (0, i)),
                   pl.BlockSpec((t, nb), lambda j, i: (0, j)),
                   pl.BlockSpec((None, tk, nb), lambda j, i: (j, i, 0)),
                   jax.ShapeDtypeStruct((N_DEV, k, nb), BF16), fused=fused, fused_arrays=fused_arrays)


def _halves_specs(block, index):
    half = N_DEV // 2
    return (pl.BlockSpec(block, lambda i, j: index(i, jnp.minimum(j, half - 1))),
            pl.BlockSpec(block, lambda i, j: index(i, jnp.maximum(j - half, 0))))


def _mm_halves_rhs_t(name, a_lo, a_hi, w_g, tm=MM_TILE):
    m = a_lo.shape[0]
    n, nb = w_g.shape[1], w_g.shape[2]

    def body(lo_ref, hi_ref, b_ref, o_ref):
        j = pl.program_id(1)

        def accumulate(a_ref):
            r = lax.dot_general(a_ref[...], b_ref[...], NT, preferred_element_type=F32)

            @pl.when(j == 0)
            def _():
                o_ref[...] = r

            @pl.when(j > 0)
            def _():
                o_ref[...] += r

        pl.when(j < N_DEV // 2)(lambda: accumulate(lo_ref))
        pl.when(j >= N_DEV // 2)(lambda: accumulate(hi_ref))

    lo_spec, hi_spec = _halves_specs((tm, nb), lambda i, j: (i, j))
    return pl.pallas_call(
        body, name=name, grid=(m // tm, N_DEV),
        in_specs=[lo_spec, hi_spec, pl.BlockSpec((None, n, nb), lambda i, j: (j, 0, 0))],
        out_specs=pl.BlockSpec((tm, n), lambda i, j: (i, 0)), out_shape=jax.ShapeDtypeStruct((m, n), F32),
        compiler_params=_params(("parallel", "arbitrary")))(a_lo, a_hi, w_g)


def _mm_halves_wgrad(name, act, d_lo, d_hi, tk=MM_TILE):
    t, k = act.shape
    nb = d_lo.shape[1] // (N_DEV // 2)

    def body(a_ref, lo_ref, hi_ref, o_ref):
        j = pl.program_id(0)

        def product(d_ref):
            o_ref[...] = lax.dot_general(a_ref[...], d_ref[...], TN, preferred_element_type=F32).astype(o_ref.dtype)

        pl.when(j < N_DEV // 2)(lambda: product(lo_ref))
        pl.when(j >= N_DEV // 2)(lambda: product(hi_ref))

    half = N_DEV // 2
    return pl.pallas_call(
        body, name=name, grid=(N_DEV, k // tk),
        in_specs=[pl.BlockSpec((t, tk), lambda j, i: (0, i)),
                  pl.BlockSpec((t, nb), lambda j, i: (0, jnp.minimum(j, half - 1))),
                  pl.BlockSpec((t, nb), lambda j, i: (0, jnp.maximum(j - half, 0)))],
        out_specs=pl.BlockSpec((None, tk, nb), lambda j, i: (j, i, 0)),
        out_shape=jax.ShapeDtypeStruct((N_DEV, k, nb), BF16),
        compiler_params=_params(("parallel", "parallel")))(act, d_lo, d_hi)


def _mm_plain(name, a, b, dims, tm, tn, out_dtype, fused=None, fused_arrays=()):
    if dims == NN:
        (m, k), n = a.shape, b.shape[1]
        a_spec = pl.BlockSpec((tm, k), lambda i, j: (i, 0))
        b_spec = pl.BlockSpec((k, tn), lambda i, j: (0, j))
    elif dims == NT:
        (m, k), n = a.shape, b.shape[0]
        a_spec = pl.BlockSpec((tm, k), lambda i, j: (i, 0))
        b_spec = pl.BlockSpec((tn, k), lambda i, j: (j, 0))
    else:
        (k, m), n = a.shape, b.shape[1]
        a_spec = pl.BlockSpec((k, tm), lambda i, j: (0, i))
        b_spec = pl.BlockSpec((k, tn), lambda i, j: (0, j))
    return _matmul(name, a, b, dims, (m // tm, n // tn), a_spec, b_spec,
                   pl.BlockSpec((tm, tn), lambda i, j: (i, j)), jax.ShapeDtypeStruct((m, n), out_dtype),
                   fused=fused, fused_arrays=fused_arrays)


def _ada_fwd(c_all, w_ada_blk, b_blk):
    def body(c_ref, w_ref, b_ref, o_ref):
        cv = c_ref[...]
        o_ref[...] = jnp.dot(cv * _sigmoid(cv), w_ref[...], preferred_element_type=F32) + b_ref[...]

    tn = 512
    return pl.pallas_call(
        body, name="ada_fwd", grid=(ADA_BLK // tn,),
        in_specs=[pl.BlockSpec((N_DEV, D_MODEL), lambda j: (0, 0)),
                  pl.BlockSpec((D_MODEL, tn), lambda j: (0, j)),
                  pl.BlockSpec((1, tn), lambda j: (0, j))],
        out_specs=pl.BlockSpec((N_DEV, tn), lambda j: (0, j)),
        out_shape=jax.ShapeDtypeStruct((N_DEV, ADA_BLK), F32),
        compiler_params=_params(("parallel",)))(c_all, w_ada_blk, b_blk)


def _ada_wgrad(c_all, gmod_cols):
    def body(c_ref, g_ref, o_ref):
        cv = c_ref[...]
        o_ref[...] = lax.dot_general(cv * _sigmoid(cv), g_ref[...], TN, preferred_element_type=F32)

    tk = 512
    return pl.pallas_call(
        body, name="ada_wgrad", grid=(D_MODEL // tk,),
        in_specs=[pl.BlockSpec((N_DEV, tk), lambda i: (0, i)),
                  pl.BlockSpec((N_DEV, ADA_BLK), lambda i: (0, 0))],
        out_specs=pl.BlockSpec((tk, ADA_BLK), lambda i: (i, 0)),
        out_shape=jax.ShapeDtypeStruct((D_MODEL, ADA_BLK), F32),
        compiler_params=_params(("parallel",)))(c_all, gmod_cols)


def _row_spec(cols=D_MODEL):
    return pl.BlockSpec((ROW_TILE, cols), lambda i: (i, 0))


def _vec_spec(cols=D_MODEL):
    return pl.BlockSpec((1, cols), lambda i: (0, 0))


def _norm_fwd(name, x, w, scale, shift, resid=None, gate=None):
    has_res = resid is not None

    def body(*refs):
        if has_res:
            x_ref, r_ref, g_ref, w_ref, sc_ref, sh_ref, xr_ref, h_ref, rs_ref = refs
            xr = x_ref[...] + g_ref[...] * r_ref[...]
            xr_ref[...] = xr
        else:
            x_ref, w_ref, sc_ref, sh_ref, h_ref, rs_ref = refs
            xr = x_ref[...]
        rs = lax.rsqrt(jnp.mean(xr * xr, axis=-1, keepdims=True) + EPS)
        h = (xr * rs) * w_ref[...] * (1.0 + sc_ref[...]) + sh_ref[...]
        h_ref[...] = h.astype(BF16)
        rs_ref[...] = rs

    s = x.shape[0]
    ins = [x] + ([resid, gate] if has_res else []) + [w, scale, shift]
    in_specs = [_row_spec()] + ([_row_spec(), _vec_spec()] if has_res else []) + [_vec_spec()] * 3
    outs = ([jax.ShapeDtypeStruct((s, D_MODEL), F32)] if has_res else []) + [
        jax.ShapeDtypeStruct((s, D_MODEL), BF16), jax.ShapeDtypeStruct((s, 1), F32)]
    out_specs = ([_row_spec()] if has_res else []) + [_row_spec(), pl.BlockSpec((ROW_TILE, 1), lambda i: (i, 0))]
    return pl.pallas_call(body, name=name, grid=(s // ROW_TILE,), in_specs=in_specs, out_specs=out_specs,
                          out_shape=outs, compiler_params=_params(("parallel",)))(*ins)


def _norm_bwd(name, dh, x, rstd, w, scale, dres, mix=None, gate=None, fused=None, fused_arrays=()):
    has_mix = mix is not None

    def body(*refs):
        if has_mix:
            (dh_ref, x_ref, rs_ref, w_ref, sc_ref, dr_ref, mix_ref, g_ref,
             dx_ref, dmix_ref, dsh_ref, dsc_ref, dw_ref, dg_ref) = refs
        else:
            dh_ref, x_ref, rs_ref, w_ref, sc_ref, dr_ref, dx_ref, dsh_ref, dsc_ref, dw_ref = refs
        i = pl.program_id(0)
        dhv = dh_ref[...]
        rs = rs_ref[...]
        xn = x_ref[...] * rs
        wv = w_ref[...]
        one_sc = 1.0 + sc_ref[...]
        dxn = dhv * wv * one_sc
        dx = dr_ref[...] + rs * (dxn - xn * jnp.mean(dxn * xn, axis=-1, keepdims=True))
        dx_ref[...] = dx
        sums = [(dsh_ref, dhv), (dsc_ref, dhv * xn * wv), (dw_ref, dhv * one_sc * xn)]
        if has_mix:
            dmix_ref[...] = (dx * g_ref[...]).astype(BF16)
            sums.append((dg_ref, dx * mix_ref[...]))

        @pl.when(i == 0)
        def _():
            for ref, _v in sums:
                ref[...] = jnp.zeros_like(ref)

        for ref, v in sums:
            ref[...] += jnp.sum(v, axis=0, keepdims=True)

    s = x.shape[0]
    ins = [dh, x, rstd, w, scale, dres] + ([mix, gate] if has_mix else [])
    in_specs = ([_row_spec(), _row_spec(), pl.BlockSpec((ROW_TILE, 1), lambda i: (i, 0)), _vec_spec(), _vec_spec(),
                 _row_spec()] + ([_row_spec(), _vec_spec()] if has_mix else []))
    vec = jax.ShapeDtypeStruct((1, D_MODEL), F32)
    outs = ([jax.ShapeDtypeStruct((s, D_MODEL), F32)] + ([jax.ShapeDtypeStruct((s, D_MODEL), BF16)] if has_mix else [])
            + [vec] * (4 if has_mix else 3))
    out_specs = [_row_spec()] + ([_row_spec()] if has_mix else []) + [_vec_spec()] * (4 if has_mix else 3)

    def first_last():
        i = pl.program_id(0)
        return i == 0, i == s // ROW_TILE - 1

    res, extra = _host_call(body, len(ins), len(outs), fused, first_last, name=name, grid=(s // ROW_TILE,),
                            in_specs=in_specs, out_specs=out_specs, out_shape=outs, scratch_shapes=[],
                            sem=("arbitrary",), operands=ins + list(fused_arrays))
    return res if fused is None else (res, extra)


def _loss_head(x1, ffn, gate2, target):
    def body(x_ref, f_ref, g_ref, t_ref, loss_ref, dout_ref, dffn_ref, dg_ref):
        i = pl.program_id(0)
        fv = f_ref[...]
        gv = g_ref[...]
        err = x_ref[...] + gv * fv - t_ref[...]
        dout = err * (1.0 / D_MODEL)
        dout_ref[...] = dout
        dffn_ref[...] = (dout * gv).astype(BF16)

        @pl.when(i == 0)
        def _():
            loss_ref[...] = jnp.zeros_like(loss_ref)
            dg_ref[...] = jnp.zeros_like(dg_ref)

        row = jnp.sum(err * err, axis=-1, keepdims=True) * (1.0 / D_MODEL)
        loss_ref[...] += jnp.broadcast_to(0.5 * jnp.sum(row, axis=0, keepdims=True), (1, 128))
        dg_ref[...] += jnp.sum(dout * fv, axis=0, keepdims=True)

    s = x1.shape[0]
    return pl.pallas_call(
        body, name="loss_head", grid=(s // ROW_TILE,),
        in_specs=[_row_spec(), _row_spec(), _vec_spec(), _row_spec()],
        out_specs=[pl.BlockSpec((1, 128), lambda i: (0, 0)), _row_spec(), _row_spec(), _vec_spec()],
        out_shape=[jax.ShapeDtypeStruct((1, 128), F32), jax.ShapeDtypeStruct((s, D_MODEL), F32),
                   jax.ShapeDtypeStruct((s, D_MODEL), BF16), jax.ShapeDtypeStruct((1, D_MODEL), F32)],
        compiler_params=_params(("arbitrary",)))(x1, ffn, gate2, target)


CONV_TILE = 512
N_CONV_TILES = D_FF // CONV_TILE


def _shift_rows(a, k, row):
    n = a.shape[0]
    if k > 0:
        return jnp.where(row >= k, pltpu.roll(a, k, 0), 0.0)
    return jnp.where(row < n + k, pltpu.roll(a, n + k, 0), 0.0)


def _conv_gate_fwd(u, conv_w, conv_b, fused=None, fused_arrays=()):
    s = u.shape[0]

    def body(a_ref, g_ref, w_ref, b_ref, y_ref):
        a = a_ref[...]
        w = w_ref[...]
        row = lax.broadcasted_iota(jnp.int32, a.shape, 0)
        ac = b_ref[...] + _shift_rows(a, 2, row) * w[0:1] + _shift_rows(a, 1, row) * w[1:2] + a * w[2:3]
        y_ref[...] = (ac * _sigmoid(ac) * g_ref[...]).astype(BF16)

    def first_last():
        i = pl.program_id(0)
        return i == 0, i == N_CONV_TILES - 1

    col = lambda off: pl.BlockSpec((s, CONV_TILE), lambda i: (0, i + off))
    (y,), extra = _host_call(
        body, 4, 1, fused, first_last, name="conv_gate_fwd", grid=(N_CONV_TILES,),
        in_specs=[col(0), col(N_CONV_TILES), pl.BlockSpec((3, CONV_TILE), lambda i: (0, i)),
                  pl.BlockSpec((1, CONV_TILE), lambda i: (0, i))],
        out_specs=[col(0)], out_shape=[jax.ShapeDtypeStruct((s, D_FF), BF16)], scratch_shapes=[], sem=("parallel",),
        operands=[u, u, conv_w, conv_b] + list(fused_arrays))
    return y if fused is None else (y, extra)


def _conv_gate_bwd(u, dy, conv_w, conv_b):
    s = u.shape[0]

    def body(a_ref, g_ref, dy_ref, w_ref, b_ref, da_ref, dg_ref, gw_ref, gb_ref):
        a = a_ref[...]
        w = w_ref[...]
        row = lax.broadcasted_iota(jnp.int32, a.shape, 0)
        a1 = _shift_rows(a, 1, row)
        a2 = _shift_rows(a, 2, row)
        ac = b_ref[...] + a2 * w[0:1] + a1 * w[1:2] + a * w[2:3]
        sg = _sigmoid(ac)
        dyv = dy_ref[...].astype(F32)
        dg_ref[...] = (dyv * (ac * sg)).astype(BF16)
        dac = dyv * g_ref[...] * _dsilu(ac, sg)
        gb_ref[...] = jnp.sum(dac, axis=0, keepdims=True)
        gw_ref[0:1, :] = jnp.sum(dac * a2, axis=0, keepdims=True)
        gw_ref[1:2, :] = jnp.sum(dac * a1, axis=0, keepdims=True)
        gw_ref[2:3, :] = jnp.sum(dac * a, axis=0, keepdims=True)
        da = dac * w[2:3] + _shift_rows(dac, -1, row) * w[1:2] + _shift_rows(dac, -2, row) * w[0:1]
        da_ref[...] = da.astype(BF16)

    col = lambda off: pl.BlockSpec((s, CONV_TILE), lambda i: (0, i + off))
    return pl.pallas_call(
        body, name="conv_gate_bwd", grid=(N_CONV_TILES,),
        in_specs=[col(0), col(N_CONV_TILES), col(0), pl.BlockSpec((3, CONV_TILE), lambda i: (0, i)),
                  pl.BlockSpec((1, CONV_TILE), lambda i: (0, i))],
        out_specs=[col(0), col(0), pl.BlockSpec((3, CONV_TILE), lambda i: (0, i)),
                   pl.BlockSpec((1, CONV_TILE), lambda i: (0, i))],
        out_shape=[jax.ShapeDtypeStruct((s, D_FF), BF16), jax.ShapeDtypeStruct((s, D_FF), BF16),
                   jax.ShapeDtypeStruct((3, D_FF), F32), jax.ShapeDtypeStruct((1, D_FF), F32)],
        compiler_params=_params(("parallel",)))(u, u, dy, conv_w, conv_b)


HG_TILE = 256
CHUNK_UNROLL = 8


def _unrolled_loop(n, body, init):
    def group(i, carry):
        for u in range(CHUNK_UNROLL):
            carry = body(i * CHUNK_UNROLL + u, carry)
        return carry

    return lax.fori_loop(0, n // CHUNK_UNROLL, group, init)


def _head_col(off):
    return pl.BlockSpec((SEQ, HEAD_DIM), lambda h: (0, h + off))


def _hgrn_gates(hq, hf, lb, pos):
    q = hq * _sigmoid(hq)
    sig = _sigmoid(hf)
    f = lb + (1.0 - lb) * sig
    gl = jnp.log(f)
    for sh in (1, 2, 4, 8):
        gl = gl + jnp.where(pos >= sh, pltpu.roll(gl, sh, 0), 0.0)
    return q, sig, f, 1.0 - f, gl


def _lower_bound(lbl):
    return 1.0 / (1.0 + jnp.exp(lbl[1:2, :] - lbl[0:1, :]))


def _head_first_last():
    h = pl.program_id(0)
    return h == 0, h == HEADS - 1


CHUNKS_PER_TILE = HG_TILE // CHUNK


def _chunk_end(x, pos):
    y = jnp.where(pos == CHUNK - 1, x, 0.0)
    for sh in (1, 2, 4, 8):
        y = y + jnp.where(pos < CHUNK - sh, pltpu.roll(y, x.shape[0] - sh, 0), 0.0)
    return y


def _suffix_in_chunk(x, pos):
    for sh in (1, 2, 4, 8):
        x = x + jnp.where(pos < CHUNK - sh, pltpu.roll(x, x.shape[0] - sh, 0), 0.0)
    return x


def _prefix_in_chunk(x, pos):
    for sh in (1, 2, 4, 8):
        x = x + jnp.where(pos >= sh, pltpu.roll(x, sh, 0), 0.0)
    return x


def _pair_decays(f, pos):
    shifted = jnp.where(pos >= 1, f, 0.0)
    e = shifted
    yield 1, e
    for d in range(2, CHUNK):
        shifted = pltpu.roll(shifted, 1, 0)
        e = e * shifted
        yield d, e


def _chunk_rows(cc):
    return slice(cc * CHUNK, (cc + 1) * CHUNK)


def _outer_products(lhs_b, rhs_b, dst, i):
    for cc in range(CHUNKS_PER_TILE):
        dst[i * CHUNKS_PER_TILE + cc] = lax.dot_general(lhs_b[_chunk_rows(cc)], rhs_b[_chunk_rows(cc)], TN,
                                                        preferred_element_type=F32)


def _state_scan(n_chunks, gl_s, u_s, keep, reverse):
    def step(k, st):
        c = n_chunks - 1 - k if reverse else k
        keep[c] = st.astype(BF16)
        gl = gl_s[pl.ds(pl.multiple_of(c * CHUNK, CHUNK), CHUNK), :]
        return st * jnp.exp(gl[CHUNK - 1:CHUNK, :]) + u_s[c]

    _unrolled_loop(n_chunks, step, jnp.zeros((HEAD_DIM, HEAD_DIM), F32))


def _hgrn_fwd(proj, lb_logits, norm_w, fused=None, fused_arrays=()):
    n_tiles = SEQ // HG_TILE
    n_chunks = SEQ // CHUNK
    fused_arrays = list(fused_arrays)

    def body(hq_ref, hf_ref, hi_ref, hg_ref, lbl_ref, nw_ref, aout_ref, opre_ref, qt_s, gl_s, u_s, st_s):
        lb = _lower_bound(lbl_ref[...])
        ones = jnp.ones((HEAD_DIM, HEAD_DIM), BF16)
        pos = lax.broadcasted_iota(jnp.int32, (HG_TILE, HEAD_DIM), 0) % CHUNK

        def tile(i, carry):
            rows = pl.ds(pl.multiple_of(i * HG_TILE, HG_TILE), HG_TILE)
            v = hi_ref[rows, :]
            q, _sig, f, kk, gl = _hgrn_gates(hq_ref[rows, :], hf_ref[rows, :], lb, pos)
            o = _lane_sum(q * kk, ones) * v
            for d, e in _pair_decays(f, pos):
                o = o + _lane_sum(q * pltpu.roll(kk, d, 0) * e, ones) * pltpu.roll(v, d, 0)
            opre_ref[rows, :] = o
            qt_s[rows, :] = q * jnp.exp(gl)
            gl_s[rows, :] = gl
            kt = kk * jnp.exp(_chunk_end(gl, pos) - gl)
            _outer_products(v.astype(BF16), kt.astype(BF16), u_s, i)
            return carry

        lax.fori_loop(0, n_tiles, tile, 0)
        _state_scan(n_chunks, gl_s, u_s, st_s, reverse=False)

        def finish(i, carry):
            rows = pl.ds(pl.multiple_of(i * HG_TILE, HG_TILE), HG_TILE)
            qt_b = qt_s[rows, :].astype(BF16)
            past = [lax.dot_general(qt_b[_chunk_rows(cc)], st_s[i * CHUNKS_PER_TILE + cc], NT,
                                    preferred_element_type=F32) for cc in range(CHUNKS_PER_TILE)]
            o = opre_ref[rows, :] + jnp.concatenate(past, axis=0)
            opre_ref[rows, :] = o
            hg = hg_ref[rows, :]
            rs = lax.rsqrt(jnp.mean(o * o, axis=-1, keepdims=True) + EPS)
            aout_ref[rows, :] = ((o * rs) * nw_ref[...] * (hg * _sigmoid(hg))).astype(BF16)
            return carry

        lax.fori_loop(0, n_tiles, finish, 0)

    return _host_call(
        body, 6, 2, fused, _head_first_last, name="hgrn_fwd", grid=(HEADS,),
        in_specs=[_head_col(0), _head_col(HEADS), _head_col(2 * HEADS), _head_col(3 * HEADS),
                  pl.BlockSpec((2, HEAD_DIM), lambda h: (0, h)), pl.BlockSpec((1, HEAD_DIM), lambda h: (0, 0))],
        out_specs=[_head_col(0), _head_col(0)],
        out_shape=[jax.ShapeDtypeStruct((SEQ, HEADS * HEAD_DIM), BF16), jax.ShapeDtypeStruct((SEQ, HEADS * HEAD_DIM), F32)],
        scratch_shapes=[pltpu.VMEM((SEQ, HEAD_DIM), F32)] * 2 + [pltpu.VMEM((n_chunks, HEAD_DIM, HEAD_DIM), F32),
                                                                 pltpu.VMEM((n_chunks, HEAD_DIM, HEAD_DIM), BF16)],
        sem=("parallel",), operands=[proj, proj, proj, proj, lb_logits, norm_w] + fused_arrays)


def _hgrn_bwd(proj, lb_logits, norm_w, o_pre, d_aout, fused=None, fused_arrays=()):
    n_tiles = SEQ // HG_TILE
    n_chunks = SEQ // CHUNK

    def body(hq_ref, hf_ref, hi_ref, hg_ref, lbl_ref, nw_ref, opre_ref, da_ref,
             dhq_ref, dhf_ref, dhi_ref, dhg_ref, dlog_ref, gnw_ref,
             q_s, k_s, gl_s, do_s, dq_s, dk_s, dv_s, u_s, st_s, rt_s):
        h = pl.program_id(0)
        lb = _lower_bound(lbl_ref[...])
        nw = nw_ref[...]
        ones = jnp.ones((HEAD_DIM, HEAD_DIM), BF16)
        pos = lax.broadcasted_iota(jnp.int32, (HG_TILE, HEAD_DIM), 0) % CHUNK

        @pl.when(h == 0)
        def _():
            gnw_ref[...] = jnp.zeros_like(gnw_ref)

        def tile(i, carry):
            rows = pl.ds(pl.multiple_of(i * HG_TILE, HG_TILE), HG_TILE)
            v = hi_ref[rows, :]
            q, _sig, f, kk, gl = _hgrn_gates(hq_ref[rows, :], hf_ref[rows, :], lb, pos)
            o = opre_ref[rows, :]
            hg = hg_ref[rows, :]
            da = da_ref[rows, :]
            rs = lax.rsqrt(jnp.mean(o * o, axis=-1, keepdims=True) + EPS)
            oh = o * rs
            sg = _sigmoid(hg)
            dnorm = da * (hg * sg)
            dhg_ref[rows, :] = (da * (oh * nw) * _dsilu(hg, sg)).astype(BF16)
            gnw_ref[...] += jnp.sum(dnorm * oh, axis=0, keepdims=True)
            doh = dnorm * nw
            do = rs * (doh - oh * jnp.mean(doh * oh, axis=-1, keepdims=True))

            d_a = _lane_sum(do * v, ones)
            dq = d_a * kk
            dk = d_a * q
            dv = _lane_sum(q * kk, ones) * do
            for d, e in _pair_decays(f, pos):
                ks = pltpu.roll(kk, d, 0)
                a_d = _lane_sum(q * ks * e, ones)
                d_a = _lane_sum(do * pltpu.roll(v, d, 0), ones) * e
                dq = dq + d_a * ks
                dk = dk + pltpu.roll(d_a * q, HG_TILE - d, 0)
                dv = dv + pltpu.roll(a_d * do, HG_TILE - d, 0)
            q_s[rows, :] = q
            k_s[rows, :] = kk
            gl_s[rows, :] = gl
            do_s[rows, :] = do
            dq_s[rows, :] = dq
            dk_s[rows, :] = dk
            dv_s[rows, :] = dv
            kt = kk * jnp.exp(_chunk_end(gl, pos) - gl)
            _outer_products(v.astype(BF16), kt.astype(BF16), u_s, i)
            return carry

        lax.fori_loop(0, n_tiles, tile, 0)
        _state_scan(n_chunks, gl_s, u_s, st_s, reverse=False)

        def reverse_increments(i, carry):
            rows = pl.ds(pl.multiple_of(i * HG_TILE, HG_TILE), HG_TILE)
            qt = q_s[rows, :] * jnp.exp(gl_s[rows, :])
            _outer_products(do_s[rows, :].astype(BF16), qt.astype(BF16), u_s, i)
            return carry

        lax.fori_loop(0, n_tiles, reverse_increments, 0)
        _state_scan(n_chunks, gl_s, u_s, rt_s, reverse=True)

        def finish(i, dlb):
            rows = pl.ds(pl.multiple_of(i * HG_TILE, HG_TILE), HG_TILE)
            q = q_s[rows, :]
            kk = k_s[rows, :]
            gl = gl_s[rows, :]
            gll = _chunk_end(gl, pos)
            ekt = jnp.exp(gll - gl)
            do_b = do_s[rows, :].astype(BF16)
            v_b = hi_ref[rows, :].astype(BF16)
            kt_b = (kk * ekt).astype(BF16)
            dq_far, dk_far, dv_far, across = [], [], [], []
            for cc in range(CHUNKS_PER_TILE):
                st = st_s[i * CHUNKS_PER_TILE + cc]
                rt = rt_s[i * CHUNKS_PER_TILE + cc]
                sl = _chunk_rows(cc)
                dq_far.append(jnp.dot(do_b[sl], st, preferred_element_type=F32))
                dk_far.append(jnp.dot(v_b[sl], rt, preferred_element_type=F32))
                dv_far.append(lax.dot_general(kt_b[sl], rt, NT, preferred_element_type=F32))
                both = jnp.sum(st.astype(F32) * rt.astype(F32), axis=0, keepdims=True)
                across.append(jnp.broadcast_to(both, (CHUNK, HEAD_DIM)))
            dq = dq_s[rows, :] + jnp.concatenate(dq_far, axis=0) * jnp.exp(gl)
            dk_in = dk_s[rows, :]
            dk_out = jnp.concatenate(dk_far, axis=0) * ekt
            dk = dk_in + dk_out
            dv = dv_s[rows, :] + jnp.concatenate(dv_far, axis=0)
            pc = kk * dk_out
            dgl = (_suffix_in_chunk(q * dq - kk * dk_in, pos) + (_prefix_in_chunk(pc, pos) - pc)
                   + jnp.concatenate(across, axis=0) * jnp.exp(gll))
            hf = hf_ref[rows, :]
            sig = _sigmoid(hf)
            f = lb + (1.0 - lb) * sig
            df = dgl / f - dk
            dhf_ref[rows, :] = (df * (1.0 - lb) * sig * (1.0 - sig)).astype(BF16)
            hq = hq_ref[rows, :]
            dhq_ref[rows, :] = (dq * _dsilu(hq, _sigmoid(hq))).astype(BF16)
            dhi_ref[rows, :] = dv.astype(BF16)
            return dlb + jnp.sum(df * (1.0 - sig), axis=0, keepdims=True)

        dlb = lax.fori_loop(0, n_tiles, finish, jnp.zeros((1, HEAD_DIM), F32))
        dl0 = lb * (1.0 - lb) * dlb
        dlog_ref[0:1, :] = dl0
        dlog_ref[1:2, :] = -dl0

    wide = HEADS * HEAD_DIM
    return _host_call(
        body, 8, 6, fused, _head_first_last, name="hgrn_bwd", grid=(HEADS,),
        in_specs=[_head_col(0), _head_col(HEADS), _head_col(2 * HEADS), _head_col(3 * HEADS),
                  pl.BlockSpec((2, HEAD_DIM), lambda h: (0, h)), pl.BlockSpec((1, HEAD_DIM), lambda h: (0, 0)),
                  _head_col(0), _head_col(0)],
        out_specs=[_head_col(0)] * 4 + [pl.BlockSpec((2, HEAD_DIM), lambda h: (0, h)),
                                        pl.BlockSpec((1, HEAD_DIM), lambda h: (0, 0))],
        out_shape=[jax.ShapeDtypeStruct((SEQ, wide), BF16)] * 4 + [jax.ShapeDtypeStruct((2, wide), F32),
                                                                    jax.ShapeDtypeStruct((1, HEAD_DIM), F32)],
        scratch_shapes=[pltpu.VMEM((SEQ, HEAD_DIM), F32)] * 7 + [pltpu.VMEM((n_chunks, HEAD_DIM, HEAD_DIM), F32),
                                                                 pltpu.VMEM((n_chunks, HEAD_DIM, HEAD_DIM), BF16),
                                                                 pltpu.VMEM((n_chunks, HEAD_DIM, HEAD_DIM), BF16)],
        sem=("arbitrary",),
        operands=[proj, proj, proj, proj, lb_logits, norm_w, o_pre, d_aout] + list(fused_arrays))


Q_TILE = 512
ATT_SCALE = HEAD_DIM ** -0.5
ATT_OFF = 4 * HEADS


def _qk_prep(proj, q_w, k_w, fused=None, fused_arrays=()):
    def body(aq_ref, ak_ref, av_ref, qw_ref, kw_ref, qn_ref, kn_ref, v_ref):
        aq = aq_ref[...]
        ak = ak_ref[...]
        qn_ref[...] = (aq * lax.rsqrt(jnp.mean(aq * aq, axis=-1, keepdims=True) + EPS) * qw_ref[...]).astype(BF16)
        kn_ref[...] = (ak * lax.rsqrt(jnp.mean(ak * ak, axis=-1, keepdims=True) + EPS) * kw_ref[...]).astype(BF16)
        v_ref[...] = av_ref[...].astype(BF16)

    wide = HEADS * HEAD_DIM
    vec = pl.BlockSpec((1, HEAD_DIM), lambda h: (0, 0))
    return _host_call(
        body, 5, 3, fused, _head_first_last, name="qk_prep", grid=(HEADS,),
        in_specs=[_head_col(ATT_OFF), _head_col(ATT_OFF + HEADS), _head_col(ATT_OFF + 2 * HEADS), vec, vec],
        out_specs=[_head_col(0)] * 3, out_shape=[jax.ShapeDtypeStruct((SEQ, wide), BF16)] * 3,
        scratch_shapes=[], sem=("parallel",), operands=[proj, proj, proj, q_w, k_w] + list(fused_arrays))


def _alibi_slopes():
    slopes = np.exp2(-8.0 * np.arange(1, HEADS + 1, dtype=np.float32) / HEADS).astype(np.float32)
    return np.broadcast_to(slopes[:, None, None], (HEADS, 1, HEAD_DIM))


SLOPE_SPEC = pl.BlockSpec((None, 1, HEAD_DIM), lambda h, i: (h, 0, 0))


N_Q_TILES = SEQ // Q_TILE
K_BLOCK = 512
NOT_ATTENDED = 1e35


def _att_tables():
    o = np.arange(N_Q_TILES, dtype=np.int32)[:, None, None]
    r = np.arange(Q_TILE, dtype=np.int32)[None, :, None]
    c = np.arange(K_BLOCK, dtype=np.int32)[None, None, :]
    dist = o * Q_TILE + r - c
    mult = ((dist <= 128).astype(np.float32) + (((dist % 4) == 0) & (dist <= 512)).astype(np.float32)
            + ((dist % 16) == 0).astype(np.float32))
    valid = (dist >= 0) & (mult > 0)
    return (np.where(valid, dist.astype(np.float32), np.float32(NOT_ATTENDED)).astype(np.float32),
            np.where(valid, np.log(np.maximum(mult, 1.0)), 0.0).astype(np.float32))


TABLE_SPEC = pl.BlockSpec((N_Q_TILES, Q_TILE, K_BLOCK), lambda h, i: (0, 0, 0))


def _att_block(q, k_ref, j, i, slope, dist_ref, lmul_ref):
    rows = pl.ds(pl.multiple_of(j * K_BLOCK, K_BLOCK), K_BLOCK)
    off = i - j * (K_BLOCK // Q_TILE)
    s = lax.dot_general(q, k_ref[rows, :], NT, preferred_element_type=F32) * ATT_SCALE
    return s - slope * dist_ref[off] + lmul_ref[off], rows


def _n_key_blocks(i):
    return (i + K_BLOCK // Q_TILE) // (K_BLOCK // Q_TILE)


def _att_first_last():
    h, i = pl.program_id(0), pl.program_id(1)
    return (h == 0) & (i == 0), (h == HEADS - 1) & (i == N_Q_TILES - 1)


def _attn_fwd(qn, kn, vb, fused=None, fused_arrays=()):
    def body(q_ref, k_ref, v_ref, sl_ref, dist_ref, lmul_ref, o_ref, lse_ref):
        i = pl.program_id(1)
        q = q_ref[...]
        slope = sl_ref[0:1, 0:1]

        def step(j, carry):
            m, l, acc = carry
            sb, rows = _att_block(q, k_ref, j, i, slope, dist_ref, lmul_ref)
            m_new = jnp.maximum(m, jnp.max(sb, axis=-1, keepdims=True))
            alpha = jnp.exp(m - m_new)
            p = jnp.exp(sb - m_new)
            l = alpha * l + jnp.sum(p, axis=-1, keepdims=True)
            acc = alpha * acc + jnp.dot(p.astype(BF16), v_ref[rows, :], preferred_element_type=F32)
            return m_new, l, acc

        m, l, acc = lax.fori_loop(0, _n_key_blocks(i), step,
                                  (jnp.full((Q_TILE, 1), -1e30, F32), jnp.zeros((Q_TILE, 1), F32),
                                   jnp.zeros((Q_TILE, HEAD_DIM), F32)))
        o_ref[...] = acc / l
        lse_ref[...] = m + jnp.log(l)

    wide = HEADS * HEAD_DIM
    qt = pl.BlockSpec((Q_TILE, HEAD_DIM), lambda h, i: (i, h))
    full = pl.BlockSpec((SEQ, HEAD_DIM), lambda h, i: (0, h))
    return _host_call(
        body, 6, 2, fused, _att_first_last, name="attn_fwd", grid=(HEADS, N_Q_TILES),
        in_specs=[qt, full, full, SLOPE_SPEC, TABLE_SPEC, TABLE_SPEC],
        out_specs=[qt, pl.BlockSpec((None, Q_TILE, 1), lambda h, i: (h, i, 0))],
        out_shape=[jax.ShapeDtypeStruct((SEQ, wide), F32), jax.ShapeDtypeStruct((HEADS, SEQ, 1), F32)],
        scratch_shapes=[], sem=("parallel", "parallel"),
        operands=[qn, kn, vb, _alibi_slopes(), *_att_tables()] + list(fused_arrays))


def _attn_bwd(qn, kn, vb, o, lse, d_mix, fused=None, fused_arrays=()):
    def body(q_ref, k_ref, v_ref, o_ref, lse_ref, do_ref, sl_ref, dist_ref, lmul_ref, dq_ref, dk_ref, dv_ref):
        i = pl.program_id(1)
        q = q_ref[...]
        do = do_ref[...]
        do_b = do.astype(BF16)
        slope = sl_ref[0:1, 0:1]
        lse = lse_ref[...]
        delta = jnp.sum(do * o_ref[...], axis=-1, keepdims=True)

        @pl.when(i == 0)
        def _():
            dk_ref[...] = jnp.zeros_like(dk_ref)
            dv_ref[...] = jnp.zeros_like(dv_ref)

        def step(j, dq):
            sb, rows = _att_block(q, k_ref, j, i, slope, dist_ref, lmul_ref)
            p = jnp.exp(sb - lse)
            dp = lax.dot_general(do_b, v_ref[rows, :], NT, preferred_element_type=F32)
            ds = (p * (dp - delta)).astype(BF16)
            dk_ref[rows, :] += lax.dot_general(ds, q, TN, preferred_element_type=F32) * ATT_SCALE
            dv_ref[rows, :] += lax.dot_general(p.astype(BF16), do_b, TN, preferred_element_type=F32)
            return dq + jnp.dot(ds, k_ref[rows, :], preferred_element_type=F32)

        dq = lax.fori_loop(0, _n_key_blocks(i), step, jnp.zeros((Q_TILE, HEAD_DIM), F32))
        dq_ref[...] = dq * ATT_SCALE

    wide = HEADS * HEAD_DIM
    qt = pl.BlockSpec((Q_TILE, HEAD_DIM), lambda h, i: (i, h))
    full = pl.BlockSpec((SEQ, HEAD_DIM), lambda h, i: (0, h))
    return _host_call(
        body, 9, 3, fused, _att_first_last, name="attn_bwd", grid=(HEADS, N_Q_TILES),
        in_specs=[qt, full, full, qt, pl.BlockSpec((None, Q_TILE, 1), lambda h, i: (h, i, 0)),
                  pl.BlockSpec((Q_TILE, HEAD_DIM), lambda h, i: (i, h + HEADS)), SLOPE_SPEC, TABLE_SPEC, TABLE_SPEC],
        out_specs=[qt, full, full], out_shape=[jax.ShapeDtypeStruct((SEQ, wide), F32)] * 3,
        scratch_shapes=[], sem=("parallel", "arbitrary"),
        operands=[qn, kn, vb, o, lse, d_mix, _alibi_slopes(), *_att_tables()] + list(fused_arrays))


def _qk_bwd(proj, q_w, k_w, dqn, dkn, dv, fused=None, fused_arrays=()):
    def body(aq_ref, ak_ref, qw_ref, kw_ref, dqn_ref, dkn_ref, dv_ref, daq_ref, dak_ref, dav_ref, gq_ref, gk_ref):
        h = pl.program_id(0)

        @pl.when(h == 0)
        def _():
            gq_ref[...] = jnp.zeros_like(gq_ref)
            gk_ref[...] = jnp.zeros_like(gk_ref)

        def one(a_ref, w_ref, d_ref, da_ref, g_ref):
            a = a_ref[...]
            d = d_ref[...]
            rs = lax.rsqrt(jnp.mean(a * a, axis=-1, keepdims=True) + EPS)
            ah = a * rs
            g_ref[...] += jnp.sum(d * ah, axis=0, keepdims=True)
            dah = d * w_ref[...]
            da_ref[...] = (rs * (dah - ah * jnp.mean(dah * ah, axis=-1, keepdims=True))).astype(BF16)

        one(aq_ref, qw_ref, dqn_ref, daq_ref, gq_ref)
        one(ak_ref, kw_ref, dkn_ref, dak_ref, gk_ref)
        dav_ref[...] = dv_ref[...].astype(BF16)

    wide = HEADS * HEAD_DIM
    vec = pl.BlockSpec((1, HEAD_DIM), lambda h: (0, 0))
    res, extra = _host_call(
        body, 7, 5, fused, _head_first_last, name="qk_bwd", grid=(HEADS,),
        in_specs=[_head_col(ATT_OFF), _head_col(ATT_OFF + HEADS), vec, vec, _head_col(0), _head_col(0), _head_col(0)],
        out_specs=[_head_col(0)] * 3 + [vec, vec],
        out_shape=[jax.ShapeDtypeStruct((SEQ, wide), BF16)] * 3 + [jax.ShapeDtypeStruct((1, HEAD_DIM), F32)] * 2,
        scratch_shapes=[], sem=("arbitrary",), operands=[proj, proj, q_w, k_w, dqn, dkn, dv] + list(fused_arrays))
    return res if fused is None else (res, extra)


def _pair_sum(name, partial, theirs, core):
    _, r, c = theirs.shape
    tr = r // 2 if r % 16 == 0 else r

    def body(core_ref, a_ref, b_ref, o_ref):
        o_ref[...] = (a_ref[...].astype(F32) + b_ref[...].astype(F32)).astype(BF16)

    spec = pl.BlockSpec((None, tr, c), lambda q, i, core_ref: (q, i, 0))
    grid_spec = pltpu.PrefetchScalarGridSpec(
        num_scalar_prefetch=1, grid=(4, r // tr),
        in_specs=[pl.BlockSpec((None, tr, c), lambda q, i, core_ref: (2 * q + core_ref[0], i, 0)), spec],
        out_specs=spec)
    return pl.pallas_call(body, name=name, grid_spec=grid_spec, out_shape=jax.ShapeDtypeStruct(theirs.shape, BF16),
                          compiler_params=_params(("parallel", "parallel")))(core, partial, theirs)


def _adamw_step(w, m, v, g):
    nm = ADAM_B1 * m + (1.0 - ADAM_B1) * g
    nv = ADAM_B2 * v + (1.0 - ADAM_B2) * (g * g)
    m_hat = nm / (1.0 - ADAM_B1 ** ADAM_STEP)
    v_hat = nv / (1.0 - ADAM_B2 ** ADAM_STEP)
    return -ADAM_LR * (m_hat / (jnp.sqrt(v_hat) + ADAM_EPS) + ADAM_WD * w), nm, nv


def _adamw(name, w, m, v, addends, tr=None):
    r, c = w.shape
    tr = r if tr is None else tr
    n_add = len(addends)

    def body(*refs):
        w_ref, m_ref, v_ref = refs[:3]
        add_refs = refs[3:3 + n_add]
        g_ref, d_ref, nm_ref, nv_ref = refs[3 + n_add:]
        g = add_refs[0][...].astype(F32)
        for a_ref in add_refs[1:]:
            g = g + a_ref[...].astype(F32)
        g_ref[...] = g
        d_ref[...], nm_ref[...], nv_ref[...] = _adamw_step(w_ref[...], m_ref[...], v_ref[...], g)

    spec = pl.BlockSpec((tr, c), lambda i: (i, 0))
    out = jax.ShapeDtypeStruct((r, c), F32)
    return pl.pallas_call(body, name=name, grid=(r // tr,), in_specs=[spec] * (3 + n_add), out_specs=[spec] * 4,
                          out_shape=[out] * 4, compiler_params=_params(("parallel",)))(w, m, v, *addends)


def _adamw_reduced(name, w, m, v, chip_sums, received, chip, tr):
    r, c = w.shape

    def body(chip_ref, w_ref, m_ref, v_ref, own_ref, r0_ref, r1_ref, r2_ref, g_ref, d_ref, nm_ref, nv_ref):
        g = ((own_ref[...].astype(F32) + r0_ref[...].astype(F32)) + r1_ref[...].astype(F32)) + r2_ref[...].astype(F32)
        g_ref[...] = g
        d_ref[...], nm_ref[...], nv_ref[...] = _adamw_step(w_ref[...], m_ref[...], v_ref[...], g)

    spec = pl.BlockSpec((tr, c), lambda i, chip_ref: (i, 0))

    def slot(k):
        return pl.BlockSpec((None, tr, c), lambda i, chip_ref: (k, i, 0))

    grid_spec = pltpu.PrefetchScalarGridSpec(
        num_scalar_prefetch=1, grid=(r // tr,),
        in_specs=[spec, spec, spec, pl.BlockSpec((None, tr, c), lambda i, chip_ref: (chip_ref[0], i, 0)),
                  slot(0), slot(1), slot(2)],
        out_specs=[spec] * 4)
    out = jax.ShapeDtypeStruct((r, c), F32)
    return pl.pallas_call(body, name=name, grid_spec=grid_spec, out_shape=[out] * 4,
                          compiler_params=_params(("parallel",)))(chip, w, m, v, chip_sums, received, received, received)


def _sum_devices(gathered):
    _, r, c = gathered.shape

    def body(g_ref, o_ref):
        acc = g_ref[0]
        for d in range(1, N_DEV):
            acc = acc + g_ref[d]
        o_ref[...] = acc

    return pl.pallas_call(body, name="sum_devices", out_shape=jax.ShapeDtypeStruct((r, c), F32))(gathered)


def _pack_rows(vectors, rows):
    flat = jnp.concatenate([v.reshape(-1) for v in vectors])
    return jnp.pad(flat, (0, rows * 128 - flat.shape[0])).reshape(rows, 128)


def _unpack(flat, shapes):
    out, off = [], 0
    for shp in shapes:
        n = 1
        for d in shp:
            n *= d
        out.append(flat[off:off + n].reshape(shp))
        off += n
    return out


def _device_step(xs, tgt, mod, norm1_w, norm2_w, lb_logits, hg_norm_w, q_norm_w, k_norm_w, conv_w_full, conv_b,
                 win_g, w_out_x, w_up_x, w_down_x, core=None):
    fused = core is not None
    shift1, scale1, gate1, shift2, scale2, gate2 = (mod[k] for k in range(6))

    h, rstd1 = _norm_fwd("norm1_fwd", xs, norm1_w, scale1, shift1)
    if fused:
        near = (0, 1, 2)
        head_rows, tail_rows = (0, UP_HEAD_ROWS), (UP_HEAD_ROWS, D_MODEL - UP_HEAD_ROWS)
        proj, (wout_g, wup_g) = _mm_blocked_rhs(
            "mm_in", h, win_g, fused_arrays=[w_out_x, w_up_x],
            fused=[_FusedCopies("gather", [w_out_x]), _FusedCopies("gather", [w_up_x], peers=near, rows=head_rows)])
        (a_out, o_pre), (wup_g,) = _hgrn_fwd(
            proj, lb_logits, hg_norm_w, fused_arrays=[w_up_x, wup_g],
            fused=_FusedCopies("gather_more", [w_up_x, wup_g], peers=near, rows=tail_rows, relay_rows=head_rows))
        wout_g, = _forward_to_sibling("allgather_stage2_out", [wout_g])
        wout_full = wout_g.reshape(D_MODEL, D_MODEL)
        (qn, kn, vb), _ = _qk_prep(proj, q_norm_w, k_norm_w)
        (att_o, lse), (wup_g,) = _attn_fwd(qn, kn, vb, _FusedCopies("relay", [wup_g], rows=tail_rows), [wup_g])
    else:
        proj = _mm_blocked_rhs("mm_in", h, win_g)
        (a_out, o_pre), _ = _hgrn_fwd(proj, lb_logits, hg_norm_w)
        wup_g, wout_full, wdown_full = w_up_x, w_out_x, w_down_x
        (qn, kn, vb), _ = _qk_prep(proj, q_norm_w, k_norm_w)
        (att_o, lse), _ = _attn_fwd(qn, kn, vb)
    mixin = jnp.concatenate([a_out, att_o.astype(BF16)], axis=1)
    if fused:
        down_head = (0, DOWN_HEAD_ROWS)
        mix, (wup_g, wdown_g) = _mm_plain(
            "mm_out", mixin, wout_full, NN, 512, 1024, F32, fused_arrays=[wup_g, w_down_x],
            fused=[_FusedCopies("forward", [wup_g]), _FusedCopies("gather", [w_down_x], rows=down_head)])
    else:
        mix = _mm_plain("mm_out", mixin, wout_full, NN, 512, 1024, F32)
    x1, h2, rstd2 = _norm_fwd("norm2_fwd", xs, norm2_w, scale2, shift2, resid=mix, gate=gate1)
    if fused:
        down_tail = (DOWN_HEAD_ROWS, FF_BLK - DOWN_HEAD_ROWS)
        u, (wdown_g,) = _mm_blocked_rhs(
            "mm_up", h2, wup_g, fused_arrays=[w_down_x, wdown_g],
            fused=_FusedCopies("gather_more", [w_down_x, wdown_g], rows=down_tail))
        y, (wdown_g,) = _conv_gate_fwd(u, conv_w_full, conv_b, _FusedCopies("forward", [wdown_g]), [wdown_g])
        wdown_full = wdown_g.reshape(D_FF, D_MODEL)
    else:
        u = _mm_blocked_rhs("mm_up", h2, wup_g)
        y = _conv_gate_fwd(u, conv_w_full, conv_b)
    ffn = _mm_plain("mm_down", y, wdown_full, NN, MM_TILE, 512, F32)
    loss_v, dout, dffn, dgate2 = _loss_head(x1, ffn, gate2, tgt)

    dy = _mm_plain("mm_down_dx", dffn, wdown_full, NT, MM_TILE, UP_BLK, BF16)
    gw_down = _mm_plain("mm_down_dw", y, dffn, TN, UP_BLK, 1024, BF16)
    da, dg, gconv_w, gconv_b = _conv_gate_bwd(u, dy, conv_w_full, conv_b)
    dh2 = _mm_halves_rhs_t("mm_up_dx", da, dg, wup_g)
    gw_up = _mm_halves_wgrad("mm_up_dw", h2, da, dg)
    if fused:
        part_up, part_down = gw_up, gw_down.reshape(N_DEV, FF_BLK, D_MODEL)
        (dx1, dmix, dshift2, dscale2, gnorm2, dgate1), (sib_up,) = _norm_bwd(
            "norm2_bwd", dh2, x1, rstd2, norm2_w, scale2, dout, mix=mix, gate=gate1,
            fused=_FusedCopies("sibling", [part_up]), fused_arrays=[part_up])
    else:
        dx1, dmix, dshift2, dscale2, gnorm2, dgate1 = _norm_bwd(
            "norm2_bwd", dh2, x1, rstd2, norm2_w, scale2, dout, mix=mix, gate=gate1)
    gw_out = _mm_plain("mm_out_dw", mixin, dmix, TN, 512, 1024, BF16)
    if fused:
        part_out = gw_out.reshape(N_DEV, OUT_BLK, D_MODEL)
        dmixin, (sib_out, sib_down) = _mm_plain(
            "mm_out_dx", dmix, wout_full, NT, 512, 1024, F32,
            fused=_FusedCopies("sibling", [part_out, part_down]), fused_arrays=[part_out, part_down])
        cs_up = _pair_sum("grad_pair_sum_up", part_up, sib_up, core)
        cs_out = _pair_sum("grad_pair_sum_out", part_out, sib_out, core)
        cs_down = _pair_sum("grad_pair_sum_down", part_down, sib_down, core)
        (dhq, dhf, dhi, dhg, glog, ghg), (fc_up,) = _hgrn_bwd(
            proj, lb_logits, hg_norm_w, o_pre, dmixin, _FusedCopies("chips", [cs_up], rows=(0, UP_EXCHANGE_HEAD)),
            [cs_up])
        (dqn, dkn, dvv), (fc_down,) = _attn_bwd(
            qn, kn, vb, att_o, lse, dmixin, _FusedCopies("chips", [cs_down], rows=(0, DOWN_EXCHANGE_HEAD)), [cs_down])
        (daq, dak, dav, gqw, gkw), (fc_up,) = _qk_bwd(
            proj, q_norm_w, k_norm_w, dqn, dkn, dvv, fused_arrays=[cs_up, fc_up],
            fused=_FusedCopies("chips_more", [cs_up, fc_up], rows=(UP_EXCHANGE_HEAD, D_MODEL - UP_EXCHANGE_HEAD)))
    else:
        dmixin = _mm_plain("mm_out_dx", dmix, wout_full, NT, 512, 1024, F32)
        (dhq, dhf, dhi, dhg, glog, ghg), _ = _hgrn_bwd(proj, lb_logits, hg_norm_w, o_pre, dmixin)
        (dqn, dkn, dvv), _ = _attn_bwd(qn, kn, vb, att_o, lse, dmixin)
        daq, dak, dav, gqw, gkw = _qk_bwd(proj, q_norm_w, k_norm_w, dqn, dkn, dvv)
    dproj = jnp.concatenate([dhq, dhf, dhi, dhg, daq, dak, dav], axis=1)
    if fused:
        down_tail = (DOWN_EXCHANGE_HEAD, FF_BLK - DOWN_EXCHANGE_HEAD)
        gw_in, (fc_out, fc_down) = _mm_wgrad_blocked(
            "mm_in_dw", h, dproj, fused_arrays=[cs_out, cs_down, fc_down],
            fused=[_FusedCopies("chips", [cs_out]), _FusedCopies("chips_more", [cs_down, fc_down], rows=down_tail)])
        from_sibling, = _exchange_sibling("grad_exchange_sibling_b", [gw_in])
        cs_in = _pair_sum("grad_pair_sum_in", gw_in, from_sibling, core)
        dh, (fc_in,) = _mm_blocked_rhs_t("mm_in_dx", dproj, win_g, fused=_FusedCopies("chips", [cs_in]),
                                         fused_arrays=[cs_in])
        large = [(cs_in, fc_in), (cs_out, fc_out), (cs_up, fc_up), (cs_down, fc_down)]
    else:
        gw_in = _mm_wgrad_blocked("mm_in_dw", h, dproj)
        dh = _mm_blocked_rhs_t("mm_in_dx", dproj, win_g)
        large = [gw_in, gw_out, gw_up, gw_down]
    grad_x, dshift1, dscale1, gnorm1 = _norm_bwd("norm1_bwd", dh, xs, rstd1, norm1_w, scale1, dx1)
    gmod = jnp.concatenate([dshift1, dscale1, dgate1, dshift2, dscale2, dgate2], axis=1)
    return (loss_v, grad_x, gmod, gnorm1, gnorm2, glog, ghg, gqw, gkw, gconv_b, gconv_w, *large)


def kernel(x, c, w_ada, b_ada, norm1_w, w_in, lb_logits, hg_norm_w, q_norm_w, k_norm_w, w_out, norm2_w, w_up, conv_w, conv_b, w_down, loss_target, m_w_ada, m_b_ada, m_norm1_w, m_w_in, m_lb_logits, m_hg_norm_w, m_q_norm_w, m_k_norm_w, m_w_out, m_norm2_w, m_w_up, m_conv_w, m_conv_b, m_w_down, v_w_ada, v_b_ada, v_norm1_w, v_w_in, v_lb_logits, v_hg_norm_w, v_q_norm_w, v_k_norm_w, v_w_out, v_norm2_w, v_w_up, v_conv_w, v_conv_b, v_w_down):
    ix, iy, ic = lax.axis_index("x"), lax.axis_index("y"), lax.axis_index("c")
    me = 4 * ix + 2 * iy + ic
    my_chip = 2 * ix + iy

    xs = x[0]
    tgt = loss_target[0]

    win_g, = _allgather_weights([w_in[0].astype(BF16)])

    first = _allgather_vmem(_pack_rows([c, conv_w[0]], 40), "allgather_c_conv_w").reshape(N_DEV, 40 * 128)
    c_all = first[:, :D_MODEL]
    conv_w_full = (first[:, D_MODEL:D_MODEL + 3 * FF_BLK].reshape(N_DEV, 3, FF_BLK).transpose(1, 0, 2)
                   .reshape(3, D_FF))

    b_blk = lax.dynamic_slice_in_dim(b_ada, me * ADA_BLK, ADA_BLK, axis=1)
    mod_cols = _ada_fwd(c_all, w_ada[0], b_blk)
    mod_all = _allgather_vmem(mod_cols, "allgather_mod").reshape(N_DEV, N_DEV, ADA_BLK)
    mod = lax.dynamic_index_in_dim(mod_all, me, axis=1, keepdims=False).reshape(6, 1, D_MODEL)

    (loss_v, grad_x, gmod, gnorm1, gnorm2, glog, ghg, gqw, gkw, gconv_b, gconv_w,
     rs_in, rs_out, rs_up, rs_down) = _device_step(
        xs, tgt, mod, norm1_w, norm2_w, lb_logits, hg_norm_w, q_norm_w, k_norm_w, conv_w_full, conv_b,
        win_g, w_out[0].astype(BF16), w_up[0].astype(BF16), w_down[0].astype(BF16),
        core=jnp.reshape(ic, (1,)).astype(jnp.int32))

    small_shapes = [(1, 6 * D_MODEL), (1, D_MODEL), (1, D_MODEL), (2, HEADS * HEAD_DIM), (1, HEAD_DIM),
                    (1, HEAD_DIM), (1, HEAD_DIM), (1, D_FF), (3, D_FF), (1, 1)]
    small = [gmod, gnorm1, gnorm2, glog, ghg, gqw, gkw, gconv_b, gconv_w, loss_v[:, 0:1]]
    n_small = sum(a.size for a in small)
    rows = -(-n_small // 1024) * 8
    gathered = _allgather_vmem(_pack_rows(small, rows), "allgather_small").reshape(N_DEV, rows, 128)
    summed = _sum_devices(gathered).reshape(-1)
    (g_b_ada, g_norm1, g_norm2, g_lb, g_hg, g_q, g_k, g_conv_b, g_conv_w_full, loss_sum) = _unpack(summed, small_shapes)
    loss = loss_sum[0, 0]
    g_conv_w = lax.dynamic_slice_in_dim(g_conv_w_full, me * FF_BLK, FF_BLK, axis=1)

    gmod_all = gathered[:, :6 * D_MODEL // 128, :].reshape(N_DEV, 6 * D_MODEL)
    gmod_cols = lax.dynamic_slice_in_dim(gmod_all, me * ADA_BLK, ADA_BLK, axis=1)
    g_w_ada_raw = _ada_wgrad(c_all, gmod_cols)

    chip = jnp.reshape(my_chip, (1,)).astype(jnp.int32)

    def big_update(name, w, m, v, rs, tr):
        chip_sums, received = rs
        return _adamw_reduced(name, w[0], m[0], v[0], chip_sums, received, chip, tr)

    r_in = big_update("adamw_w_in", w_in, m_w_in, v_w_in, rs_in, 256)
    r_out = big_update("adamw_w_out", w_out, m_w_out, v_w_out, rs_out, 128)
    r_up = big_update("adamw_w_up", w_up, m_w_up, v_w_up, rs_up, 256)
    r_down = big_update("adamw_w_down", w_down, m_w_down, v_w_down, rs_down, 176)
    r_ada = _adamw("adamw_w_ada", w_ada[0], m_w_ada[0], v_w_ada[0], [g_w_ada_raw], tr=256)
    r_convw = _adamw("adamw_conv_w", conv_w[0], m_conv_w[0], v_conv_w[0], [g_conv_w])

    rep_shapes = [(1, 6 * D_MODEL), (1, D_MODEL), (1, D_MODEL), (2, HEADS * HEAD_DIM), (1, HEAD_DIM),
                  (1, HEAD_DIM), (1, HEAD_DIM), (1, D_FF)]
    rep_rows = -(-sum(a * b for a, b in rep_shapes) // 1024) * 8
    pack = lambda arrs: _pack_rows(arrs, rep_rows)
    rep = _adamw("adamw_small",
                 pack([b_ada, norm1_w, norm2_w, lb_logits, hg_norm_w, q_norm_w, k_norm_w, conv_b]),
                 pack([m_b_ada, m_norm1_w, m_norm2_w, m_lb_logits, m_hg_norm_w, m_q_norm_w, m_k_norm_w, m_conv_b]),
                 pack([v_b_ada, v_norm1_w, v_norm2_w, v_lb_logits, v_hg_norm_w, v_q_norm_w, v_k_norm_w, v_conv_b]),
                 [pack([g_b_ada, g_norm1, g_norm2, g_lb, g_hg, g_q, g_k, g_conv_b])])
    rep = [_unpack(r.reshape(-1), rep_shapes) for r in rep]

    def big(r):
        return [a[None] for a in r]

    order = {"w_ada": big(r_ada), "b_ada": [r[0] for r in rep], "norm1_w": [r[1] for r in rep],
             "w_in": big(r_in), "lb_logits": [r[3] for r in rep], "hg_norm_w": [r[4] for r in rep],
             "q_norm_w": [r[5] for r in rep], "k_norm_w": [r[6] for r in rep], "w_out": big(r_out),
             "norm2_w": [r[2] for r in rep], "w_up": big(r_up), "conv_w": big(r_convw),
             "conv_b": [r[7] for r in rep], "w_down": big(r_down)}
    names = ["w_ada", "b_ada", "norm1_w", "w_in", "lb_logits", "hg_norm_w", "q_norm_w", "k_norm_w", "w_out",
             "norm2_w", "w_up", "conv_w", "conv_b", "w_down"]
    outs = [loss, grad_x[None]]
    for kind in range(4):
        outs += [order[n][kind] for n in names]
    return tuple(outs)
```

```python
import jax
import jax.numpy as jnp
import numpy as np
from jax import lax
from jax.experimental import pallas as pl
from jax.experimental.pallas import tpu as pltpu

F32 = jnp.float32
BF16 = jnp.bfloat16

N_DEV = 8
SEQ = 2048
D_MODEL = 2048
HEADS = 8
HEAD_DIM = 128
IN_COLS = 7168
IN_BLK = IN_COLS // N_DEV
D_FF = 5632
UP_BLK = 2 * D_FF // N_DEV
FF_BLK = D_FF // N_DEV
ADA_BLK = 6 * D_MODEL // N_DEV
OUT_BLK = D_MODEL // N_DEV
EPS = 1e-6
CHUNK = 16
ROW_TILE = 256
MM_TILE = 1024
V7X_VMEM_LIMIT = 56 * 1024 * 1024

ADAM_LR = 0.001
ADAM_B1 = 0.9
ADAM_B2 = 0.999
ADAM_EPS = 1e-08
ADAM_WD = 0.01
ADAM_STEP = 10

NN = (((1,), (0,)), ((), ()))
NT = (((1,), (1,)), ((), ()))
TN = (((0,), (0,)), ((), ()))
MESH = pl.DeviceIdType.MESH


def _params(sem=None, vmem=V7X_VMEM_LIMIT):
    return pltpu.CompilerParams(dimension_semantics=sem, vmem_limit_bytes=vmem)


def _sigmoid(x):
    return 1.0 / (1.0 + jnp.exp(-x))


def _dsilu(x, s):
    return s * (1.0 + x * (1.0 - s))


def _lane_sum(x, ones_bf16):
    return jnp.dot(x.astype(BF16), ones_bf16, preferred_element_type=F32)


def _mesh_pos():
    return lax.axis_index("x"), lax.axis_index("y"), lax.axis_index("c")


def _allgather_vmem(x_blk, name):
    m_per, n = x_blk.shape

    def body(x_ref, out_ref, send_sems, recv_sems, local_sem):
        x, y, c = _mesh_pos()
        me, sibling = (x, y, c), (x, y, 1 - c)
        chips = [(1 - x, y), (x, 1 - y), (1 - x, 1 - y)]

        def rows(px, py, pc):
            return out_ref.at[pl.ds((4 * px + 2 * py + pc) * m_per, m_per), :]

        def copy(k, block, to, src=None):
            return pltpu.make_async_remote_copy(
                src_ref=rows(*block) if src is None else src, dst_ref=rows(*block),
                send_sem=send_sems.at[k], recv_sem=recv_sems.at[k], device_id=to, device_id_type=MESH)

        mine = pltpu.make_async_copy(x_ref, rows(*me), local_sem)
        mine.start()
        first = [copy(0, me, sibling, src=x_ref)]
        first += [copy(1 + j, me, (*chip, c), src=x_ref) for j, chip in enumerate(chips)]
        for cp in first:
            cp.start()
        passed = [copy(4 + j, (*chip, c), sibling) for j, chip in enumerate(chips)]
        for j, chip in enumerate(chips):
            copy(1 + j, (*chip, c), me).wait_recv()
            passed[j].start()
        copy(0, sibling, me).wait_recv()
        for j, chip in enumerate(chips):
            copy(4 + j, (*chip, 1 - c), me).wait_recv()
        for cp in first + passed:
            cp.wait_send()
        mine.wait()

    return pl.pallas_call(
        body, name=name,
        out_shape=jax.ShapeDtypeStruct((N_DEV * m_per, n), x_blk.dtype),
        in_specs=[pl.BlockSpec(memory_space=pltpu.VMEM)],
        out_specs=pl.BlockSpec(memory_space=pltpu.VMEM),
        scratch_shapes=[pltpu.SemaphoreType.DMA((7,)), pltpu.SemaphoreType.DMA((7,)), pltpu.SemaphoreType.DMA],
    )(x_blk)


def _flip(v, bit):
    return v + bit - 2 * v * bit


def _relay_chips(x, y, c):
    return (_flip(x, 1 - c), _flip(y, c)), (_flip(x, c), _flip(y, 1 - c))


UP_EXCHANGE_HEAD = 1856
DOWN_EXCHANGE_HEAD = 560
DOWN_HEAD_ROWS = 192
UP_HEAD_ROWS = 768
GATHER_PARTS = 4


def _allgather_weights(blocks):
    n_arr = len(blocks)
    parts = GATHER_PARTS

    def body(*refs):
        ins, outs = refs[:n_arr], refs[n_arr:2 * n_arr]
        send_sems, recv_sems, local_sems = refs[2 * n_arr:]
        x, y, c = _mesh_pos()
        me, sibling = (x, y, c), (x, y, 1 - c)
        near = [(1 - x, y), (x, 1 - y)]
        chips = near + [(1 - x, 1 - y)]
        relay_from, relay_to = _relay_chips(x, y, c)

        def rows(a, p):
            hr = ins[a].shape[0] // parts
            return pl.ds(p * hr, hr)

        def slot(a, pos, p):
            return outs[a].at[4 * pos[0] + 2 * pos[1] + pos[2], rows(a, p)]

        def copy(a, k, p, src, lands, to):
            return pltpu.make_async_remote_copy(
                src_ref=src, dst_ref=slot(a, lands, p), send_sem=send_sems.at[a, k, p], recv_sem=recv_sems.at[a, k, p],
                device_id=to, device_id_type=MESH)

        sent = []
        local = [pltpu.make_async_copy(ins[a], outs[a].at[4 * x + 2 * y + c], local_sems.at[a]) for a in range(n_arr)]
        for cp in local:
            cp.start()
        for p in range(parts):
            for a in range(n_arr):
                own = ins[a].at[rows(a, p)]
                sent.append(copy(a, 0, p, own, me, sibling))
                sent += [copy(a, 1 + j, p, own, me, (*chip, c)) for j, chip in enumerate(near)]
        for cp in sent:
            cp.start()

        def start(cp):
            cp.start()
            sent.append(cp)

        for p in range(parts):
            for a in range(n_arr):
                for j, chip in enumerate(near):
                    copy(a, 1 + j, p, ins[a].at[rows(a, p)], (*chip, c), me).wait_recv()
                    start(copy(a, 4 + j, p, slot(a, (*chip, c), p), (*chip, c), sibling))
                start(copy(a, 3, p, slot(a, (*relay_from, c), p), (*relay_from, c), (*relay_to, c)))
        for p in range(parts):
            for a in range(n_arr):
                copy(a, 3, p, ins[a].at[rows(a, p)], (*chips[2], c), me).wait_recv()
                start(copy(a, 6, p, slot(a, (*chips[2], c), p), (*chips[2], c), sibling))
        for p in range(parts):
            for a in range(n_arr):
                copy(a, 0, p, ins[a].at[rows(a, p)], sibling, me).wait_recv()
                for j, chip in enumerate(chips):
                    copy(a, 4 + j, p, ins[a].at[rows(a, p)], (*chip, 1 - c), me).wait_recv()
        for cp in sent:
            cp.wait_send()
        for cp in local:
            cp.wait()

    return pl.pallas_call(
        body, name="allgather_weights",
        out_shape=[jax.ShapeDtypeStruct((N_DEV,) + b.shape, b.dtype) for b in blocks],
        in_specs=[pl.BlockSpec(memory_space=pltpu.HBM)] * n_arr, out_specs=[pl.BlockSpec(memory_space=pltpu.HBM)] * n_arr,
        scratch_shapes=[pltpu.SemaphoreType.DMA((n_arr, 7, parts)), pltpu.SemaphoreType.DMA((n_arr, 7, parts)),
                        pltpu.SemaphoreType.DMA((n_arr,))],
    )(*blocks)


HBM_SPEC = pl.BlockSpec(memory_space=pltpu.HBM)


class _FusedCopies:
    def __init__(self, kind, arrays, peers=(0, 1, 2, 3), rows=None, relay_rows=None):
        self.kind = kind
        self.peers = peers
        self.rows = rows
        self.relay_rows = relay_rows
        n = len(arrays) // 2 if kind == "gather_more" else len(arrays)
        self.n = n
        self.n_in = len(arrays)
        self.aliases = {}
        if kind == "gather":
            self.out_shape = [jax.ShapeDtypeStruct((N_DEV,) + a.shape, a.dtype) for a in arrays]
            self.scratch_shapes = [pltpu.SemaphoreType.DMA((n, 4, GATHER_PARTS)),
                                   pltpu.SemaphoreType.DMA((n, 4, GATHER_PARTS)), pltpu.SemaphoreType.DMA((n,))]
        elif kind == "gather_more":
            self.out_shape = [jax.ShapeDtypeStruct(a.shape, a.dtype) for a in arrays[n:]]
            self.scratch_shapes = [pltpu.SemaphoreType.DMA((n, 5, GATHER_PARTS)),
                                   pltpu.SemaphoreType.DMA((n, 5, GATHER_PARTS)), pltpu.SemaphoreType.DMA((n,))]
            self.aliases = {n + a: a for a in range(n)}
        elif kind == "relay":
            self.out_shape = [jax.ShapeDtypeStruct(a.shape, a.dtype) for a in arrays]
            self.scratch_shapes = [pltpu.SemaphoreType.DMA((n,)), pltpu.SemaphoreType.DMA((n,))]
            self.aliases = {a: a for a in range(n)}
        elif kind == "forward":
            self.out_shape = [jax.ShapeDtypeStruct(a.shape, a.dtype) for a in arrays]
            self.scratch_shapes = [pltpu.SemaphoreType.DMA((n, 3)), pltpu.SemaphoreType.DMA((n, 3))]
            self.aliases = {a: a for a in range(n)}
        elif kind == "sibling":
            self.out_shape = [jax.ShapeDtypeStruct((4,) + a.shape[1:], a.dtype) for a in arrays]
            self.scratch_shapes = [pltpu.SemaphoreType.DMA((n, 4)), pltpu.SemaphoreType.DMA((n, 4))]
        elif kind == "chips_more":
            n = self.n = len(arrays) // 2
            self.out_shape = [jax.ShapeDtypeStruct(a.shape, a.dtype) for a in arrays[n:]]
            self.scratch_shapes = [pltpu.SemaphoreType.DMA((n, 3)), pltpu.SemaphoreType.DMA((n, 3))]
            self.aliases = {n + a: a for a in range(n)}
        else:
            self.out_shape = [jax.ShapeDtypeStruct((3,) + a.shape[1:], a.dtype) for a in arrays]
            self.scratch_shapes = [pltpu.SemaphoreType.DMA((n, 3)), pltpu.SemaphoreType.DMA((n, 3))]
        self.in_specs = [HBM_SPEC] * self.n_in
        self.out_specs = [HBM_SPEC] * n
        self.n_scratch = len(self.scratch_shapes)

    def copies(self, ins, outs, sems):
        x, y, c = _mesh_pos()
        chips = [(1 - x, y), (x, 1 - y), (1 - x, 1 - y)]
        sibling = (x, y, 1 - c)
        starts, waits = [], []
        relay_from, relay_to = _relay_chips(x, y, c)

        def relayed(a, buf, lands, send_sem, recv_sem, rows):
            first, count = rows or (0, buf.shape[1])
            span = pl.ds(first, count)
            return pltpu.make_async_remote_copy(
                src_ref=buf.at[4 * relay_from[0] + 2 * relay_from[1] + c, span],
                dst_ref=outs[a].at[4 * lands[0] + 2 * lands[1] + c, span], send_sem=send_sem, recv_sem=recv_sem,
                device_id=(*relay_to, c), device_id_type=MESH)

        if self.kind in ("gather", "gather_more"):
            send_sems, recv_sems, local_sems = sems
            me = (x, y, c)
            peers = [sibling] + [(px, py, c) for px, py in chips]

            def slot(a, pos):
                return outs[a].at[4 * pos[0] + 2 * pos[1] + pos[2]]

            def span(a, p=None):
                first, count = self.rows or (0, ins[a].shape[0])
                if p is None:
                    return pl.ds(first, count)
                return pl.ds(first + p * (count // GATHER_PARTS), count // GATHER_PARTS)

            def remote(a, k, p, lands_from):
                return pltpu.make_async_remote_copy(
                    src_ref=ins[a].at[span(a, p)], dst_ref=slot(a, lands_from).at[span(a, p)],
                    send_sem=send_sems.at[a, k, p], recv_sem=recv_sems.at[a, k, p], device_id=peers[k],
                    device_id_type=MESH)

            for a in range(self.n):
                local = pltpu.make_async_copy(ins[a].at[span(a)], slot(a, me).at[span(a)], local_sems.at[a])
                starts.append(local)
                waits.append(local)
            for p in range(GATHER_PARTS):
                for a in range(self.n):
                    for k in self.peers:
                        starts.append(remote(a, k, p, me))
                        waits.append(remote(a, k, p, peers[k]))
            if self.kind == "gather_more" and self.relay_rows is not None:
                for a in range(self.n):
                    buf = ins[self.n + a]
                    starts.append(relayed(a, buf, relay_from, send_sems.at[a, 4, 0], recv_sems.at[a, 4, 0],
                                          self.relay_rows))
                    waits.append(relayed(a, buf, chips[2], send_sems.at[a, 4, 0], recv_sems.at[a, 4, 0],
                                         self.relay_rows))
        elif self.kind == "relay":
            send_sems, recv_sems = sems
            for a in range(self.n):
                starts.append(relayed(a, ins[a], relay_from, send_sems.at[a], recv_sems.at[a], self.rows))
                waits.append(relayed(a, ins[a], chips[2], send_sems.at[a], recv_sems.at[a], self.rows))
        elif self.kind == "forward":
            send_sems, recv_sems = sems

            def passed_on(a, j, pc_src, pc_dst):
                px, py = chips[j]
                return pltpu.make_async_remote_copy(
                    src_ref=ins[a].at[4 * px + 2 * py + pc_src], dst_ref=outs[a].at[4 * px + 2 * py + pc_dst],
                    send_sem=send_sems.at[a, j], recv_sem=recv_sems.at[a, j], device_id=sibling, device_id_type=MESH)

            for a in range(self.n):
                for j in range(3):
                    starts.append(passed_on(a, j, c, c))
                    waits.append(passed_on(a, j, c, 1 - c))
        elif self.kind == "sibling":
            send_sems, recv_sems = sems
            for a in range(self.n):
                for q in range(4):
                    cp = pltpu.make_async_remote_copy(
                        src_ref=ins[a].at[2 * q + 1 - c], dst_ref=outs[a].at[q], send_sem=send_sems.at[a, q],
                        recv_sem=recv_sems.at[a, q], device_id=sibling, device_id_type=MESH)
                    starts.append(cp)
                    waits.append(cp)
        else:
            send_sems, recv_sems = sems
            for a in range(self.n):
                first, count = self.rows or (0, ins[a].shape[1])
                span = pl.ds(first, count)
                for j, (px, py) in enumerate(chips):
                    cp = pltpu.make_async_remote_copy(
                        src_ref=ins[a].at[2 * px + py, span], dst_ref=outs[a].at[j, span],
                        send_sem=send_sems.at[a, j], recv_sem=recv_sems.at[a, j], device_id=(px, py, c),
                        device_id_type=MESH)
                    starts.append(cp)
                    waits.append(cp)
        return starts, waits


def _fused_groups(fused):
    if fused is None:
        return []
    return list(fused) if isinstance(fused, (list, tuple)) else [fused]


def _host_body(body, n_in, n_out, fused, first_last):
    groups = _fused_groups(fused)
    if not groups:
        return body
    n_fin, n_fout = sum(g.n_in for g in groups), sum(g.n for g in groups)
    n_fsem = sum(g.n_scratch for g in groups)

    def wrapped(*refs):
        core_in, f_in = refs[:n_in], refs[n_in:n_in + n_fin]
        core_out = refs[n_in + n_fin:n_in + n_fin + n_out]
        f_out = refs[n_in + n_fin + n_out:n_in + n_fin + n_out + n_fout]
        rest = refs[n_in + n_fin + n_out + n_fout:]
        core_scratch, f_sems = rest[:len(rest) - n_fsem], rest[len(rest) - n_fsem:]
        starts, waits = [], []
        for g in groups:
            s, w = g.copies(f_in[:g.n_in], f_out[:g.n], f_sems[:g.n_scratch])
            f_in, f_out, f_sems = f_in[g.n_in:], f_out[g.n:], f_sems[g.n_scratch:]
            starts += s
            waits += w
        first, last = first_last()

        @pl.when(first)
        def _():
            for cp in starts:
                cp.start()

        body(*core_in, *core_out, *core_scratch)

        @pl.when(last)
        def _():
            for cp in waits:
                cp.wait()

    return wrapped


def _host_call(body, n_in, n_out, fused, first_last, *, name, grid, in_specs, out_specs, out_shape, scratch_shapes,
               sem, operands):
    aliases = {}
    in_specs, out_specs, out_shape, scratch_shapes = list(in_specs), list(out_specs), list(out_shape), list(scratch_shapes)
    fin, fout = n_in, n_out
    for g in _fused_groups(fused):
        aliases.update({fin + fi: fout + fo for fi, fo in g.aliases.items()})
        fin, fout = fin + g.n_in, fout + g.n
        in_specs += g.in_specs
        out_specs += g.out_specs
        out_shape += g.out_shape
        scratch_shapes += g.scratch_shapes
        sem = tuple("arbitrary" for _ in sem)
    res = pl.pallas_call(_host_body(body, n_in, n_out, fused, first_last), name=name, grid=grid, in_specs=in_specs,
                         out_specs=out_specs, out_shape=out_shape, scratch_shapes=scratch_shapes,
                         input_output_aliases=aliases, compiler_params=_params(sem))(*operands)
    return list(res[:n_out]), list(res[n_out:])


def _exchange_sibling(name, partials):
    n_arr = len(partials)

    def body(*refs):
        ins, outs = refs[:n_arr], refs[n_arr:2 * n_arr]
        send_sems, recv_sems = refs[2 * n_arr:]
        x, y, c = _mesh_pos()
        copies = [pltpu.make_async_remote_copy(
            src_ref=ins[a].at[2 * q + 1 - c], dst_ref=outs[a].at[q], send_sem=send_sems.at[a, q],
            recv_sem=recv_sems.at[a, q], device_id=(x, y, 1 - c), device_id_type=MESH)
            for a in range(n_arr) for q in range(4)]
        for cp in copies:
            cp.start()
        for cp in copies:
            cp.wait_recv()
        for cp in copies:
            cp.wait_send()

    return pl.pallas_call(
        body, name=name,
        out_shape=[jax.ShapeDtypeStruct((4,) + p.shape[1:], p.dtype) for p in partials],
        in_specs=[HBM_SPEC] * n_arr, out_specs=[HBM_SPEC] * n_arr,
        scratch_shapes=[pltpu.SemaphoreType.DMA((n_arr, 4)), pltpu.SemaphoreType.DMA((n_arr, 4))],
    )(*partials)


def _matmul(name, a, b, dims, grid, a_spec, b_spec, o_spec, out_shape, acc_axis=None, fused=None, fused_arrays=()):
    def body(a_ref, b_ref, o_ref):
        r = lax.dot_general(a_ref[...], b_ref[...], dims, preferred_element_type=F32)
        if acc_axis is None:
            o_ref[...] = r.astype(o_ref.dtype)
        else:
            k = pl.program_id(acc_axis)

            @pl.when(k == 0)
            def _():
                o_ref[...] = r

            @pl.when(k > 0)
            def _():
                o_ref[...] += r

    sem = tuple("arbitrary" if i == acc_axis else "parallel" for i in range(len(grid)))
    if fused is None:
        return pl.pallas_call(body, name=name, grid=grid, in_specs=[a_spec, b_spec], out_specs=o_spec,
                              out_shape=out_shape, compiler_params=_params(sem))(a, b)

    def first_last():
        first = last = None
        for ax, n in enumerate(grid):
            f, l = pl.program_id(ax) == 0, pl.program_id(ax) == n - 1
            first, last = (f, l) if first is None else (first & f, last & l)
        return first, last

    (out,), extra = _host_call(body, 2, 1, fused, first_last, name=name, grid=grid, in_specs=[a_spec, b_spec],
                               out_specs=[o_spec], out_shape=[out_shape], scratch_shapes=[], sem=sem,
                               operands=[a, b] + list(fused_arrays))
    return out, extra


def _mm_blocked_rhs(name, a, w_g, tm=MM_TILE, fused=None, fused_arrays=()):
    m, k = a.shape
    nb = w_g.shape[2]
    return _matmul(name, a, w_g, NN, (N_DEV, m // tm),
                   pl.BlockSpec((tm, k), lambda j, i: (i, 0)),
                   pl.BlockSpec((None, k, nb), lambda j, i: (j, 0, 0)),
                   pl.BlockSpec((tm, nb), lambda j, i: (i, j)),
                   jax.ShapeDtypeStruct((m, N_DEV * nb), F32), fused=fused, fused_arrays=fused_arrays)


def _mm_blocked_rhs_t(name, a, w_g, tm=MM_TILE, fused=None, fused_arrays=()):
    m = a.shape[0]
    n, nb = w_g.shape[1], w_g.shape[2]
    return _matmul(name, a, w_g, NT, (m // tm, N_DEV),
                   pl.BlockSpec((tm, nb), lambda i, j: (i, j)),
                   pl.BlockSpec((None, n, nb), lambda i, j: (j, 0, 0)),
                   pl.BlockSpec((tm, n), lambda i, j: (i, 0)),
                   jax.ShapeDtypeStruct((m, n), F32), acc_axis=1, fused=fused, fused_arrays=fused_arrays)


def _mm_wgrad_blocked(name, act, dcols, tk=MM_TILE, fused=None, fused_arrays=()):
    t, k = act.shape
    nb = dcols.shape[1] // N_DEV
    return _matmul(name, act, dcols, TN, (N_DEV, k // tk),
                   pl.BlockSpec((t, tk), lambda j, i: (0, i)),
                   pl.BlockSpec((t, nb), lambda j, i: (0, j)),
                   pl.BlockSpec((None, tk, nb), lambda j, i: (j, i, 0)),
                   jax.ShapeDtypeStruct((N_DEV, k, nb), BF16), fused=fused, fused_arrays=fused_arrays)


def _halves_specs(block, index):
    half = N_DEV // 2
    return (pl.BlockSpec(block, lambda i, j: index(i, jnp.minimum(j, half - 1))),
            pl.BlockSpec(block, lambda i, j: index(i, jnp.maximum(j - half, 0))))


def _mm_halves_rhs_t(name, a_lo, a_hi, w_g, tm=MM_TILE):
    m = a_lo.shape[0]
    n, nb = w_g.shape[1], w_g.shape[2]

    def body(lo_ref, hi_ref, b_ref, o_ref):
        j = pl.program_id(1)

        def accumulate(a_ref):
            r = lax.dot_general(a_ref[...], b_ref[...], NT, preferred_element_type=F32)

            @pl.when(j == 0)
            def _():
                o_ref[...] = r

            @pl.when(j > 0)
            def _():
                o_ref[...] += r

        pl.when(j < N_DEV // 2)(lambda: accumulate(lo_ref))
        pl.when(j >= N_DEV // 2)(lambda: accumulate(hi_ref))

    lo_spec, hi_spec = _halves_specs((tm, nb), lambda i, j: (i, j))
    return pl.pallas_call(
        body, name=name, grid=(m // tm, N_DEV),
        in_specs=[lo_spec, hi_spec, pl.BlockSpec((None, n, nb), lambda i, j: (j, 0, 0))],
        out_specs=pl.BlockSpec((tm, n), lambda i, j: (i, 0)), out_shape=jax.ShapeDtypeStruct((m, n), F32),
        compiler_params=_params(("parallel", "arbitrary")))(a_lo, a_hi, w_g)


def _mm_halves_wgrad(name, act, d_lo, d_hi, tk=MM_TILE):
    t, k = act.shape
    nb = d_lo.shape[1] // (N_DEV // 2)

    def body(a_ref, lo_ref, hi_ref, o_ref):
        j = pl.program_id(0)

        def product(d_ref):
            o_ref[...] = lax.dot_general(a_ref[...], d_ref[...], TN, preferred_element_type=F32).astype(o_ref.dtype)

        pl.when(j < N_DEV // 2)(lambda: product(lo_ref))
        pl.when(j >= N_DEV // 2)(lambda: product(hi_ref))

    half = N_DEV // 2
    return pl.pallas_call(
        body, name=name, grid=(N_DEV, k // tk),
        in_specs=[pl.BlockSpec((t, tk), lambda j, i: (0, i)),
                  pl.BlockSpec((t, nb), lambda j, i: (0, jnp.minimum(j, half - 1))),
                  pl.BlockSpec((t, nb), lambda j, i: (0, jnp.maximum(j - half, 0)))],
        out_specs=pl.BlockSpec((None, tk, nb), lambda j, i: (j, i, 0)),
        out_shape=jax.ShapeDtypeStruct((N_DEV, k, nb), BF16),
        compiler_params=_params(("parallel", "parallel")))(act, d_lo, d_hi)


def _mm_plain(name, a, b, dims, tm, tn, out_dtype, fused=None, fused_arrays=()):
    if dims == NN:
        (m, k), n = a.shape, b.shape[1]
        a_spec = pl.BlockSpec((tm, k), lambda i, j: (i, 0))
        b_spec = pl.BlockSpec((k, tn), lambda i, j: (0, j))
    elif dims == NT:
        (m, k), n = a.shape, b.shape[0]
        a_spec = pl.BlockSpec((tm, k), lambda i, j: (i, 0))
        b_spec = pl.BlockSpec((tn, k), lambda i, j: (j, 0))
    else:
        (k, m), n = a.shape, b.shape[1]
        a_spec = pl.BlockSpec((k, tm), lambda i, j: (0, i))
        b_spec = pl.BlockSpec((k, tn), lambda i, j: (0, j))
    return _matmul(name, a, b, dims, (m // tm, n // tn), a_spec, b_spec,
                   pl.BlockSpec((tm, tn), lambda i, j: (i, j)), jax.ShapeDtypeStruct((m, n), out_dtype),
                   fused=fused, fused_arrays=fused_arrays)


def _ada_fwd(c_all, w_ada_blk, b_blk):
    def body(c_ref, w_ref, b_ref, o_ref):
        cv = c_ref[...]
        o_ref[...] = jnp.dot(cv * _sigmoid(cv), w_ref[...], preferred_element_type=F32) + b_ref[...]

    tn = 512
    return pl.pallas_call(
        body, name="ada_fwd", grid=(ADA_BLK // tn,),
        in_specs=[pl.BlockSpec((N_DEV, D_MODEL), lambda j: (0, 0)),
                  pl.BlockSpec((D_MODEL, tn), lambda j: (0, j)),
                  pl.BlockSpec((1, tn), lambda j: (0, j))],
        out_specs=pl.BlockSpec((N_DEV, tn), lambda j: (0, j)),
        out_shape=jax.ShapeDtypeStruct((N_DEV, ADA_BLK), F32),
        compiler_params=_params(("parallel",)))(c_all, w_ada_blk, b_blk)


def _ada_wgrad(c_all, gmod_cols):
    def body(c_ref, g_ref, o_ref):
        cv = c_ref[...]
        o_ref[...] = lax.dot_general(cv * _sigmoid(cv), g_ref[...], TN, preferred_element_type=F32)

    tk = 512
    return pl.pallas_call(
        body, name="ada_wgrad", grid=(D_MODEL // tk,),
        in_specs=[pl.BlockSpec((N_DEV, tk), lambda i: (0, i)),
                  pl.BlockSpec((N_DEV, ADA_BLK), lambda i: (0, 0))],
        out_specs=pl.BlockSpec((tk, ADA_BLK), lambda i: (i, 0)),
        out_shape=jax.ShapeDtypeStruct((D_MODEL, ADA_BLK), F32),
        compiler_params=_params(("parallel",)))(c_all, gmod_cols)


def _row_spec(cols=D_MODEL):
    return pl.BlockSpec((ROW_TILE, cols), lambda i: (i, 0))


def _vec_spec(cols=D_MODEL):
    return pl.BlockSpec((1, cols), lambda i: (0, 0))


def _norm_fwd(name, x, w, scale, shift, resid=None, gate=None):
    has_res = resid is not None

    def body(*refs):
        if has_res:
            x_ref, r_ref, g_ref, w_ref, sc_ref, sh_ref, xr_ref, h_ref, rs_ref = refs
            xr = x_ref[...] + g_ref[...] * r_ref[...]
            xr_ref[...] = xr
        else:
            x_ref, w_ref, sc_ref, sh_ref, h_ref, rs_ref = refs
            xr = x_ref[...]
        rs = lax.rsqrt(jnp.mean(xr * xr, axis=-1, keepdims=True) + EPS)
        h = (xr * rs) * w_ref[...] * (1.0 + sc_ref[...]) + sh_ref[...]
        h_ref[...] = h.astype(BF16)
        rs_ref[...] = rs

    s = x.shape[0]
    ins = [x] + ([resid, gate] if has_res else []) + [w, scale, shift]
    in_specs = [_row_spec()] + ([_row_spec(), _vec_spec()] if has_res else []) + [_vec_spec()] * 3
    outs = ([jax.ShapeDtypeStruct((s, D_MODEL), F32)] if has_res else []) + [
        jax.ShapeDtypeStruct((s, D_MODEL), BF16), jax.ShapeDtypeStruct((s, 1), F32)]
    out_specs = ([_row_spec()] if has_res else []) + [_row_spec(), pl.BlockSpec((ROW_TILE, 1), lambda i: (i, 0))]
    return pl.pallas_call(body, name=name, grid=(s // ROW_TILE,), in_specs=in_specs, out_specs=out_specs,
                          out_shape=outs, compiler_params=_params(("parallel",)))(*ins)


def _norm_bwd(name, dh, x, rstd, w, scale, dres, mix=None, gate=None, fused=None, fused_arrays=()):
    has_mix = mix is not None

    def body(*refs):
        if has_mix:
            (dh_ref, x_ref, rs_ref, w_ref, sc_ref, dr_ref, mix_ref, g_ref,
             dx_ref, dmix_ref, dsh_ref, dsc_ref, dw_ref, dg_ref) = refs
        else:
            dh_ref, x_ref, rs_ref, w_ref, sc_ref, dr_ref, dx_ref, dsh_ref, dsc_ref, dw_ref = refs
        i = pl.program_id(0)
        dhv = dh_ref[...]
        rs = rs_ref[...]
        xn = x_ref[...] * rs
        wv = w_ref[...]
        one_sc = 1.0 + sc_ref[...]
        dxn = dhv * wv * one_sc
        dx = dr_ref[...] + rs * (dxn - xn * jnp.mean(dxn * xn, axis=-1, keepdims=True))
        dx_ref[...] = dx
        sums = [(dsh_ref, dhv), (dsc_ref, dhv * xn * wv), (dw_ref, dhv * one_sc * xn)]
        if has_mix:
            dmix_ref[...] = (dx * g_ref[...]).astype(BF16)
            sums.append((dg_ref, dx * mix_ref[...]))

        @pl.when(i == 0)
        def _():
            for ref, _v in sums:
                ref[...] = jnp.zeros_like(ref)

        for ref, v in sums:
            ref[...] += jnp.sum(v, axis=0, keepdims=True)

    s = x.shape[0]
    ins = [dh, x, rstd, w, scale, dres] + ([mix, gate] if has_mix else [])
    in_specs = ([_row_spec(), _row_spec(), pl.BlockSpec((ROW_TILE, 1), lambda i: (i, 0)), _vec_spec(), _vec_spec(),
                 _row_spec()] + ([_row_spec(), _vec_spec()] if has_mix else []))
    vec = jax.ShapeDtypeStruct((1, D_MODEL), F32)
    outs = ([jax.ShapeDtypeStruct((s, D_MODEL), F32)] + ([jax.ShapeDtypeStruct((s, D_MODEL), BF16)] if has_mix else [])
            + [vec] * (4 if has_mix else 3))
    out_specs = [_row_spec()] + ([_row_spec()] if has_mix else []) + [_vec_spec()] * (4 if has_mix else 3)

    def first_last():
        i = pl.program_id(0)
        return i == 0, i == s // ROW_TILE - 1

    res, extra = _host_call(body, len(ins), len(outs), fused, first_last, name=name, grid=(s // ROW_TILE,),
                            in_specs=in_specs, out_specs=out_specs, out_shape=outs, scratch_shapes=[],
                            sem=("arbitrary",), operands=ins + list(fused_arrays))
    return res if fused is None else (res, extra)


def _loss_head(x1, ffn, gate2, target):
    def body(x_ref, f_ref, g_ref, t_ref, loss_ref, dout_ref, dffn_ref, dg_ref):
        i = pl.program_id(0)
        fv = f_ref[...]
        gv = g_ref[...]
        err = x_ref[...] + gv * fv - t_ref[...]
        dout = err * (1.0 / D_MODEL)
        dout_ref[...] = dout
        dffn_ref[...] = (dout * gv).astype(BF16)

        @pl.when(i == 0)
        def _():
            loss_ref[...] = jnp.zeros_like(loss_ref)
            dg_ref[...] = jnp.zeros_like(dg_ref)

        row = jnp.sum(err * err, axis=-1, keepdims=True) * (1.0 / D_MODEL)
        loss_ref[...] += jnp.broadcast_to(0.5 * jnp.sum(row, axis=0, keepdims=True), (1, 128))
        dg_ref[...] += jnp.sum(dout * fv, axis=0, keepdims=True)

    s = x1.shape[0]
    return pl.pallas_call(
        body, name="loss_head", grid=(s // ROW_TILE,),
        in_specs=[_row_spec(), _row_spec(), _vec_spec(), _row_spec()],
        out_specs=[pl.BlockSpec((1, 128), lambda i: (0, 0)), _row_spec(), _row_spec(), _vec_spec()],
        out_shape=[jax.ShapeDtypeStruct((1, 128), F32), jax.ShapeDtypeStruct((s, D_MODEL), F32),
                   jax.ShapeDtypeStruct((s, D_MODEL), BF16), jax.ShapeDtypeStruct((1, D_MODEL), F32)],
        compiler_params=_params(("arbitrary",)))(x1, ffn, gate2, target)


CONV_TILE = 512
N_CONV_TILES = D_FF // CONV_TILE


def _shift_rows(a, k, row):
    n = a.shape[0]
    if k > 0:
        return jnp.where(row >= k, pltpu.roll(a, k, 0), 0.0)
    return jnp.where(row < n + k, pltpu.roll(a, n + k, 0), 0.0)


def _conv_gate_fwd(u, conv_w, conv_b, fused=None, fused_arrays=()):
    s = u.shape[0]

    def body(a_ref, g_ref, w_ref, b_ref, y_ref):
        a = a_ref[...]
        w = w_ref[...]
        row = lax.broadcasted_iota(jnp.int32, a.shape, 0)
        ac = b_ref[...] + _shift_rows(a, 2, row) * w[0:1] + _shift_rows(a, 1, row) * w[1:2] + a * w[2:3]
        y_ref[...] = (ac * _sigmoid(ac) * g_ref[...]).astype(BF16)

    def first_last():
        i = pl.program_id(0)
        return i == 0, i == N_CONV_TILES - 1

    col = lambda off: pl.BlockSpec((s, CONV_TILE), lambda i: (0, i + off))
    (y,), extra = _host_call(
        body, 4, 1, fused, first_last, name="conv_gate_fwd", grid=(N_CONV_TILES,),
        in_specs=[col(0), col(N_CONV_TILES), pl.BlockSpec((3, CONV_TILE), lambda i: (0, i)),
                  pl.BlockSpec((1, CONV_TILE), lambda i: (0, i))],
        out_specs=[col(0)], out_shape=[jax.ShapeDtypeStruct((s, D_FF), BF16)], scratch_shapes=[], sem=("parallel",),
        operands=[u, u, conv_w, conv_b] + list(fused_arrays))
    return y if fused is None else (y, extra)


def _conv_gate_bwd(u, dy, conv_w, conv_b):
    s = u.shape[0]

    def body(a_ref, g_ref, dy_ref, w_ref, b_ref, da_ref, dg_ref, gw_ref, gb_ref):
        a = a_ref[...]
        w = w_ref[...]
        row = lax.broadcasted_iota(jnp.int32, a.shape, 0)
        a1 = _shift_rows(a, 1, row)
        a2 = _shift_rows(a, 2, row)
        ac = b_ref[...] + a2 * w[0:1] + a1 * w[1:2] + a * w[2:3]
        sg = _sigmoid(ac)
        dyv = dy_ref[...].astype(F32)
        dg_ref[...] = (dyv * (ac * sg)).astype(BF16)
        dac = dyv * g_ref[...] * _dsilu(ac, sg)
        gb_ref[...] = jnp.sum(dac, axis=0, keepdims=True)
        gw_ref[0:1, :] = jnp.sum(dac * a2, axis=0, keepdims=True)
        gw_ref[1:2, :] = jnp.sum(dac * a1, axis=0, keepdims=True)
        gw_ref[2:3, :] = jnp.sum(dac * a, axis=0, keepdims=True)
        da = dac * w[2:3] + _shift_rows(dac, -1, row) * w[1:2] + _shift_rows(dac, -2, row) * w[0:1]
        da_ref[...] = da.astype(BF16)

    col = lambda off: pl.BlockSpec((s, CONV_TILE), lambda i: (0, i + off))
    return pl.pallas_call(
        body, name="conv_gate_bwd", grid=(N_CONV_TILES,),
        in_specs=[col(0), col(N_CONV_TILES), col(0), pl.BlockSpec((3, CONV_TILE), lambda i: (0, i)),
                  pl.BlockSpec((1, CONV_TILE), lambda i: (0, i))],
        out_specs=[col(0), col(0), pl.BlockSpec((3, CONV_TILE), lambda i: (0, i)),
                   pl.BlockSpec((1, CONV_TILE), lambda i: (0, i))],
        out_shape=[jax.ShapeDtypeStruct((s, D_FF), BF16), jax.ShapeDtypeStruct((s, D_FF), BF16),
                   jax.ShapeDtypeStruct((3, D_FF), F32), jax.ShapeDtypeStruct((1, D_FF), F32)],
        compiler_params=_params(("parallel",)))(u, u, dy, conv_w, conv_b)


HG_TILE = 256
CHUNK_UNROLL = 8


def _unrolled_loop(n, body, init):
    def group(i, carry):
        for u in range(CHUNK_UNROLL):
            carry = body(i * CHUNK_UNROLL + u, carry)
        return carry

    return lax.fori_loop(0, n // CHUNK_UNROLL, group, init)


def _head_col(off):
    return pl.BlockSpec((SEQ, HEAD_DIM), lambda h: (0, h + off))


def _hgrn_gates(hq, hf, lb, pos):
    q = hq * _sigmoid(hq)
    sig = _sigmoid(hf)
    f = lb + (1.0 - lb) * sig
    gl = jnp.log(f)
    for sh in (1, 2, 4, 8):
        gl = gl + jnp.where(pos >= sh, pltpu.roll(gl, sh, 0), 0.0)
    return q, sig, f, 1.0 - f, gl


def _lower_bound(lbl):
    return 1.0 / (1.0 + jnp.exp(lbl[1:2, :] - lbl[0:1, :]))


def _head_first_last():
    h = pl.program_id(0)
    return h == 0, h == HEADS - 1


CHUNKS_PER_TILE = HG_TILE // CHUNK


def _chunk_end(x, pos):
    y = jnp.where(pos == CHUNK - 1, x, 0.0)
    for sh in (1, 2, 4, 8):
        y = y + jnp.where(pos < CHUNK - sh, pltpu.roll(y, x.shape[0] - sh, 0), 0.0)
    return y


def _suffix_in_chunk(x, pos):
    for sh in (1, 2, 4, 8):
        x = x + jnp.where(pos < CHUNK - sh, pltpu.roll(x, x.shape[0] - sh, 0), 0.0)
    return x


def _prefix_in_chunk(x, pos):
    for sh in (1, 2, 4, 8):
        x = x + jnp.where(pos >= sh, pltpu.roll(x, sh, 0), 0.0)
    return x


def _pair_decays(f, pos):
    shifted = jnp.where(pos >= 1, f, 0.0)
    e = shifted
    yield 1, e
    for d in range(2, CHUNK):
        shifted = pltpu.roll(shifted, 1, 0)
        e = e * shifted
        yield d, e


def _chunk_rows(cc):
    return slice(cc * CHUNK, (cc + 1) * CHUNK)


def _outer_products(lhs_b, rhs_b, dst, i):
    for cc in range(CHUNKS_PER_TILE):
        dst[i * CHUNKS_PER_TILE + cc] = lax.dot_general(lhs_b[_chunk_rows(cc)], rhs_b[_chunk_rows(cc)], TN,
                                                        preferred_element_type=F32)


def _state_scan(n_chunks, gl_s, u_s, keep, reverse):
    def step(k, st):
        c = n_chunks - 1 - k if reverse else k
        keep[c] = st.astype(BF16)
        gl = gl_s[pl.ds(pl.multiple_of(c * CHUNK, CHUNK), CHUNK), :]
        return st * jnp.exp(gl[CHUNK - 1:CHUNK, :]) + u_s[c]

    _unrolled_loop(n_chunks, step, jnp.zeros((HEAD_DIM, HEAD_DIM), F32))


def _hgrn_fwd(proj, lb_logits, norm_w, fused=None, fused_arrays=()):
    n_tiles = SEQ // HG_TILE
    n_chunks = SEQ // CHUNK
    fused_arrays = list(fused_arrays)

    def body(hq_ref, hf_ref, hi_ref, hg_ref, lbl_ref, nw_ref, aout_ref, opre_ref, qt_s, gl_s, u_s, st_s):
        lb = _lower_bound(lbl_ref[...])
        ones = jnp.ones((HEAD_DIM, HEAD_DIM), BF16)
        pos = lax.broadcasted_iota(jnp.int32, (HG_TILE, HEAD_DIM), 0) % CHUNK

        def tile(i, carry):
            rows = pl.ds(pl.multiple_of(i * HG_TILE, HG_TILE), HG_TILE)
            v = hi_ref[rows, :]
            q, _sig, f, kk, gl = _hgrn_gates(hq_ref[rows, :], hf_ref[rows, :], lb, pos)
            o = _lane_sum(q * kk, ones) * v
            for d, e in _pair_decays(f, pos):
                o = o + _lane_sum(q * pltpu.roll(kk, d, 0) * e, ones) * pltpu.roll(v, d, 0)
            opre_ref[rows, :] = o
            qt_s[rows, :] = q * jnp.exp(gl)
            gl_s[rows, :] = gl
            kt = kk * jnp.exp(_chunk_end(gl, pos) - gl)
            _outer_products(v.astype(BF16), kt.astype(BF16), u_s, i)
            return carry

        lax.fori_loop(0, n_tiles, tile, 0)
        _state_scan(n_chunks, gl_s, u_s, st_s, reverse=False)

        def finish(i, carry):
            rows = pl.ds(pl.multiple_of(i * HG_TILE, HG_TILE), HG_TILE)
            qt_b = qt_s[rows, :].astype(BF16)
            past = [lax.dot_general(qt_b[_chunk_rows(cc)], st_s[i * CHUNKS_PER_TILE + cc], NT,
                                    preferred_element_type=F32) for cc in range(CHUNKS_PER_TILE)]
            o = opre_ref[rows, :] + jnp.concatenate(past, axis=0)
            opre_ref[rows, :] = o
            hg = hg_ref[rows, :]
            rs = lax.rsqrt(jnp.mean(o * o, axis=-1, keepdims=True) + EPS)
            aout_ref[rows, :] = ((o * rs) * nw_ref[...] * (hg * _sigmoid(hg))).astype(BF16)
            return carry

        lax.fori_loop(0, n_tiles, finish, 0)

    return _host_call(
        body, 6, 2, fused, _head_first_last, name="hgrn_fwd", grid=(HEADS,),
        in_specs=[_head_col(0), _head_col(HEADS), _head_col(2 * HEADS), _head_col(3 * HEADS),
                  pl.BlockSpec((2, HEAD_DIM), lambda h: (0, h)), pl.BlockSpec((1, HEAD_DIM), lambda h: (0, 0))],
        out_specs=[_head_col(0), _head_col(0)],
        out_shape=[jax.ShapeDtypeStruct((SEQ, HEADS * HEAD_DIM), BF16), jax.ShapeDtypeStruct((SEQ, HEADS * HEAD_DIM), F32)],
        scratch_shapes=[pltpu.VMEM((SEQ, HEAD_DIM), F32)] * 2 + [pltpu.VMEM((n_chunks, HEAD_DIM, HEAD_DIM), F32),
                                                                 pltpu.VMEM((n_chunks, HEAD_DIM, HEAD_DIM), BF16)],
        sem=("parallel",), operands=[proj, proj, proj, proj, lb_logits, norm_w] + fused_arrays)


def _hgrn_bwd(proj, lb_logits, norm_w, o_pre, d_aout, fused=None, fused_arrays=()):
    n_tiles = SEQ // HG_TILE
    n_chunks = SEQ // CHUNK

    def body(hq_ref, hf_ref, hi_ref, hg_ref, lbl_ref, nw_ref, opre_ref, da_ref,
             dhq_ref, dhf_ref, dhi_ref, dhg_ref, dlog_ref, gnw_ref,
             q_s, k_s, gl_s, do_s, dq_s, dk_s, dv_s, u_s, st_s, rt_s):
        h = pl.program_id(0)
        lb = _lower_bound(lbl_ref[...])
        nw = nw_ref[...]
        ones = jnp.ones((HEAD_DIM, HEAD_DIM), BF16)
        pos = lax.broadcasted_iota(jnp.int32, (HG_TILE, HEAD_DIM), 0) % CHUNK

        @pl.when(h == 0)
        def _():
            gnw_ref[...] = jnp.zeros_like(gnw_ref)

        def tile(i, carry):
            rows = pl.ds(pl.multiple_of(i * HG_TILE, HG_TILE), HG_TILE)
            v = hi_ref[rows, :]
            q, _sig, f, kk, gl = _hgrn_gates(hq_ref[rows, :], hf_ref[rows, :], lb, pos)
            o = opre_ref[rows, :]
            hg = hg_ref[rows, :]
            da = da_ref[rows, :]
            rs = lax.rsqrt(jnp.mean(o * o, axis=-1, keepdims=True) + EPS)
            oh = o * rs
            sg = _sigmoid(hg)
            dnorm = da * (hg * sg)
            dhg_ref[rows, :] = (da * (oh * nw) * _dsilu(hg, sg)).astype(BF16)
            gnw_ref[...] += jnp.sum(dnorm * oh, axis=0, keepdims=True)
            doh = dnorm * nw
            do = rs * (doh - oh * jnp.mean(doh * oh, axis=-1, keepdims=True))

            d_a = _lane_sum(do * v, ones)
            dq = d_a * kk
            dk = d_a * q
            dv = _lane_sum(q * kk, ones) * do
            for d, e in _pair_decays(f, pos):
                ks = pltpu.roll(kk, d, 0)
                a_d = _lane_sum(q * ks * e, ones)
                d_a = _lane_sum(do * pltpu.roll(v, d, 0), ones) * e
                dq = dq + d_a * ks
                dk = dk + pltpu.roll(d_a * q, HG_TILE - d, 0)
                dv = dv + pltpu.roll(a_d * do, HG_TILE - d, 0)
            q_s[rows, :] = q
            k_s[rows, :] = kk
            gl_s[rows, :] = gl
            do_s[rows, :] = do
            dq_s[rows, :] = dq
            dk_s[rows, :] = dk
            dv_s[rows, :] = dv
            kt = kk * jnp.exp(_chunk_end(gl, pos) - gl)
            _outer_products(v.astype(BF16), kt.astype(BF16), u_s, i)
            return carry

        lax.fori_loop(0, n_tiles, tile, 0)
        _state_scan(n_chunks, gl_s, u_s, st_s, reverse=False)

        def reverse_increments(i, carry):
            rows = pl.ds(pl.multiple_of(i * HG_TILE, HG_TILE), HG_TILE)
            qt = q_s[rows, :] * jnp.exp(gl_s[rows, :])
            _outer_products(do_s[rows, :].astype(BF16), qt.astype(BF16), u_s, i)
            return carry

        lax.fori_loop(0, n_tiles, reverse_increments, 0)
        _state_scan(n_chunks, gl_s, u_s, rt_s, reverse=True)

        def finish(i, dlb):
            rows = pl.ds(pl.multiple_of(i * HG_TILE, HG_TILE), HG_TILE)
            q = q_s[rows, :]
            kk = k_s[rows, :]
            gl = gl_s[rows, :]
            gll = _chunk_end(gl, pos)
            ekt = jnp.exp(gll - gl)
            do_b = do_s[rows, :].astype(BF16)
            v_b = hi_ref[rows, :].astype(BF16)
            kt_b = (kk * ekt).astype(BF16)
            dq_far, dk_far, dv_far, across = [], [], [], []
            for cc in range(CHUNKS_PER_TILE):
                st = st_s[i * CHUNKS_PER_TILE + cc]
                rt = rt_s[i * CHUNKS_PER_TILE + cc]
                sl = _chunk_rows(cc)
                dq_far.append(jnp.dot(do_b[sl], st, preferred_element_type=F32))
                dk_far.append(jnp.dot(v_b[sl], rt, preferred_element_type=F32))
                dv_far.append(lax.dot_general(kt_b[sl], rt, NT, preferred_element_type=F32))
                both = jnp.sum(st.astype(F32) * rt.astype(F32), axis=0, keepdims=True)
                across.append(jnp.broadcast_to(both, (CHUNK, HEAD_DIM)))
            dq = dq_s[rows, :] + jnp.concatenate(dq_far, axis=0) * jnp.exp(gl)
            dk_in = dk_s[rows, :]
            dk_out = jnp.concatenate(dk_far, axis=0) * ekt
            dk = dk_in + dk_out
            dv = dv_s[rows, :] + jnp.concatenate(dv_far, axis=0)
            pc = kk * dk_out
            dgl = (_suffix_in_chunk(q * dq - kk * dk_in, pos) + (_prefix_in_chunk(pc, pos) - pc)
                   + jnp.concatenate(across, axis=0) * jnp.exp(gll))
            hf = hf_ref[rows, :]
            sig = _sigmoid(hf)
            f = lb + (1.0 - lb) * sig
            df = dgl / f - dk
            dhf_ref[rows, :] = (df * (1.0 - lb) * sig * (1.0 - sig)).astype(BF16)
            hq = hq_ref[rows, :]
            dhq_ref[rows, :] = (dq * _dsilu(hq, _sigmoid(hq))).astype(BF16)
            dhi_ref[rows, :] = dv.astype(BF16)
            return dlb + jnp.sum(df * (1.0 - sig), axis=0, keepdims=True)

        dlb = lax.fori_loop(0, n_tiles, finish, jnp.zeros((1, HEAD_DIM), F32))
        dl0 = lb * (1.0 - lb) * dlb
        dlog_ref[0:1, :] = dl0
        dlog_ref[1:2, :] = -dl0

    wide = HEADS * HEAD_DIM
    return _host_call(
        body, 8, 6, fused, _head_first_last, name="hgrn_bwd", grid=(HEADS,),
        in_specs=[_head_col(0), _head_col(HEADS), _head_col(2 * HEADS), _head_col(3 * HEADS),
                  pl.BlockSpec((2, HEAD_DIM), lambda h: (0, h)), pl.BlockSpec((1, HEAD_DIM), lambda h: (0, 0)),
                  _head_col(0), _head_col(0)],
        out_specs=[_head_col(0)] * 4 + [pl.BlockSpec((2, HEAD_DIM), lambda h: (0, h)),
                                        pl.BlockSpec((1, HEAD_DIM), lambda h: (0, 0))],
        out_shape=[jax.ShapeDtypeStruct((SEQ, wide), BF16)] * 4 + [jax.ShapeDtypeStruct((2, wide), F32),
                                                                    jax.ShapeDtypeStruct((1, HEAD_DIM), F32)],
        scratch_shapes=[pltpu.VMEM((SEQ, HEAD_DIM), F32)] * 7 + [pltpu.VMEM((n_chunks, HEAD_DIM, HEAD_DIM), F32),
                                                                 pltpu.VMEM((n_chunks, HEAD_DIM, HEAD_DIM), BF16),
                                                                 pltpu.VMEM((n_chunks, HEAD_DIM, HEAD_DIM), BF16)],
        sem=("arbitrary",),
        operands=[proj, proj, proj, proj, lb_logits, norm_w, o_pre, d_aout] + list(fused_arrays))


Q_TILE = 512
ATT_SCALE = HEAD_DIM ** -0.5
ATT_OFF = 4 * HEADS


def _qk_prep(proj, q_w, k_w, fused=None, fused_arrays=()):
    def body(aq_ref, ak_ref, av_ref, qw_ref, kw_ref, qn_ref, kn_ref, v_ref):
        aq = aq_ref[...]
        ak = ak_ref[...]
        qn_ref[...] = (aq * lax.rsqrt(jnp.mean(aq * aq, axis=-1, keepdims=True) + EPS) * qw_ref[...]).astype(BF16)
        kn_ref[...] = (ak * lax.rsqrt(jnp.mean(ak * ak, axis=-1, keepdims=True) + EPS) * kw_ref[...]).astype(BF16)
        v_ref[...] = av_ref[...].astype(BF16)

    wide = HEADS * HEAD_DIM
    vec = pl.BlockSpec((1, HEAD_DIM), lambda h: (0, 0))
    return _host_call(
        body, 5, 3, fused, _head_first_last, name="qk_prep", grid=(HEADS,),
        in_specs=[_head_col(ATT_OFF), _head_col(ATT_OFF + HEADS), _head_col(ATT_OFF + 2 * HEADS), vec, vec],
        out_specs=[_head_col(0)] * 3, out_shape=[jax.ShapeDtypeStruct((SEQ, wide), BF16)] * 3,
        scratch_shapes=[], sem=("parallel",), operands=[proj, proj, proj, q_w, k_w] + list(fused_arrays))


def _alibi_slopes():
    slopes = np.exp2(-8.0 * np.arange(1, HEADS + 1, dtype=np.float32) / HEADS).astype(np.float32)
    return np.broadcast_to(slopes[:, None, None], (HEADS, 1, HEAD_DIM))


SLOPE_SPEC = pl.BlockSpec((None, 1, HEAD_DIM), lambda h, i: (h, 0, 0))


N_Q_TILES = SEQ // Q_TILE
K_BLOCK = 512
NOT_ATTENDED = 1e35


def _att_tables():
    o = np.arange(N_Q_TILES, dtype=np.int32)[:, None, None]
    r = np.arange(Q_TILE, dtype=np.int32)[None, :, None]
    c = np.arange(K_BLOCK, dtype=np.int32)[None, None, :]
    dist = o * Q_TILE + r - c
    mult = ((dist <= 128).astype(np.float32) + (((dist % 4) == 0) & (dist <= 512)).astype(np.float32)
            + ((dist % 16) == 0).astype(np.float32))
    valid = (dist >= 0) & (mult > 0)
    return (np.where(valid, dist.astype(np.float32), np.float32(NOT_ATTENDED)).astype(np.float32),
            np.where(valid, np.log(np.maximum(mult, 1.0)), 0.0).astype(np.float32))


TABLE_SPEC = pl.BlockSpec((N_Q_TILES, Q_TILE, K_BLOCK), lambda h, i: (0, 0, 0))


def _att_block(q, k_ref, j, i, slope, dist_ref, lmul_ref):
    rows = pl.ds(pl.multiple_of(j * K_BLOCK, K_BLOCK), K_BLOCK)
    off = i - j * (K_BLOCK // Q_TILE)
    s = lax.dot_general(q, k_ref[rows, :], NT, preferred_element_type=F32) * ATT_SCALE
    return s - slope * dist_ref[off] + lmul_ref[off], rows


def _n_key_blocks(i):
    return (i + K_BLOCK // Q_TILE) // (K_BLOCK // Q_TILE)


def _att_first_last():
    h, i = pl.program_id(0), pl.program_id(1)
    return (h == 0) & (i == 0), (h == HEADS - 1) & (i == N_Q_TILES - 1)


def _attn_fwd(qn, kn, vb, fused=None, fused_arrays=()):
    def body(q_ref, k_ref, v_ref, sl_ref, dist_ref, lmul_ref, o_ref, lse_ref):
        i = pl.program_id(1)
        q = q_ref[...]
        slope = sl_ref[0:1, 0:1]

        def step(j, carry):
            m, l, acc = carry
            sb, rows = _att_block(q, k_ref, j, i, slope, dist_ref, lmul_ref)
            m_new = jnp.maximum(m, jnp.max(sb, axis=-1, keepdims=True))
            alpha = jnp.exp(m - m_new)
            p = jnp.exp(sb - m_new)
            l = alpha * l + jnp.sum(p, axis=-1, keepdims=True)
            acc = alpha * acc + jnp.dot(p.astype(BF16), v_ref[rows, :], preferred_element_type=F32)
            return m_new, l, acc

        m, l, acc = lax.fori_loop(0, _n_key_blocks(i), step,
                                  (jnp.full((Q_TILE, 1), -1e30, F32), jnp.zeros((Q_TILE, 1), F32),
                                   jnp.zeros((Q_TILE, HEAD_DIM), F32)))
        o_ref[...] = acc / l
        lse_ref[...] = m + jnp.log(l)

    wide = HEADS * HEAD_DIM
    qt = pl.BlockSpec((Q_TILE, HEAD_DIM), lambda h, i: (i, h))
    full = pl.BlockSpec((SEQ, HEAD_DIM), lambda h, i: (0, h))
    return _host_call(
        body, 6, 2, fused, _att_first_last, name="attn_fwd", grid=(HEADS, N_Q_TILES),
        in_specs=[qt, full, full, SLOPE_SPEC, TABLE_SPEC, TABLE_SPEC],
        out_specs=[qt, pl.BlockSpec((None, Q_TILE, 1), lambda h, i: (h, i, 0))],
        out_shape=[jax.ShapeDtypeStruct((SEQ, wide), F32), jax.ShapeDtypeStruct((HEADS, SEQ, 1), F32)],
        scratch_shapes=[], sem=("parallel", "parallel"),
        operands=[qn, kn, vb, _alibi_slopes(), *_att_tables()] + list(fused_arrays))


def _attn_bwd(qn, kn, vb, o, lse, d_mix, fused=None, fused_arrays=()):
    def body(q_ref, k_ref, v_ref, o_ref, lse_ref, do_ref, sl_ref, dist_ref, lmul_ref, dq_ref, dk_ref, dv_ref):
        i = pl.program_id(1)
        q = q_ref[...]
        do = do_ref[...]
        do_b = do.astype(BF16)
        slope = sl_ref[0:1, 0:1]
        lse = lse_ref[...]
        delta = jnp.sum(do * o_ref[...], axis=-1, keepdims=True)

        @pl.when(i == 0)
        def _():
            dk_ref[...] = jnp.zeros_like(dk_ref)
            dv_ref[...] = jnp.zeros_like(dv_ref)

        def step(j, dq):
            sb, rows = _att_block(q, k_ref, j, i, slope, dist_ref, lmul_ref)
            p = jnp.exp(sb - lse)
            dp = lax.dot_general(do_b, v_ref[rows, :], NT, preferred_element_type=F32)
            ds = (p * (dp - delta)).astype(BF16)
            dk_ref[rows, :] += lax.dot_general(ds, q, TN, preferred_element_type=F32) * ATT_SCALE
            dv_ref[rows, :] += lax.dot_general(p.astype(BF16), do_b, TN, preferred_element_type=F32)
            return dq + jnp.dot(ds, k_ref[rows, :], preferred_element_type=F32)

        dq = lax.fori_loop(0, _n_key_blocks(i), step, jnp.zeros((Q_TILE, HEAD_DIM), F32))
        dq_ref[...] = dq * ATT_SCALE

    wide = HEADS * HEAD_DIM
    qt = pl.BlockSpec((Q_TILE, HEAD_DIM), lambda h, i: (i, h))
    full = pl.BlockSpec((SEQ, HEAD_DIM), lambda h, i: (0, h))
    return _host_call(
        body, 9, 3, fused, _att_first_last, name="attn_bwd", grid=(HEADS, N_Q_TILES),
        in_specs=[qt, full, full, qt, pl.BlockSpec((None, Q_TILE, 1), lambda h, i: (h, i, 0)),
                  pl.BlockSpec((Q_TILE, HEAD_DIM), lambda h, i: (i, h + HEADS)), SLOPE_SPEC, TABLE_SPEC, TABLE_SPEC],
        out_specs=[qt, full, full], out_shape=[jax.ShapeDtypeStruct((SEQ, wide), F32)] * 3,
        scratch_shapes=[], sem=("parallel", "arbitrary"),
        operands=[qn, kn, vb, o, lse, d_mix, _alibi_slopes(), *_att_tables()] + list(fused_arrays))


def _qk_bwd(proj, q_w, k_w, dqn, dkn, dv, fused=None, fused_arrays=()):
    def body(aq_ref, ak_ref, qw_ref, kw_ref, dqn_ref, dkn_ref, dv_ref, daq_ref, dak_ref, dav_ref, gq_ref, gk_ref):
        h = pl.program_id(0)

        @pl.when(h == 0)
        def _():
            gq_ref[...] = jnp.zeros_like(gq_ref)
            gk_ref[...] = jnp.zeros_like(gk_ref)

        def one(a_ref, w_ref, d_ref, da_ref, g_ref):
            a = a_ref[...]
            d = d_ref[...]
            rs = lax.rsqrt(jnp.mean(a * a, axis=-1, keepdims=True) + EPS)
            ah = a * rs
            g_ref[...] += jnp.sum(d * ah, axis=0, keepdims=True)
            dah = d * w_ref[...]
            da_ref[...] = (rs * (dah - ah * jnp.mean(dah * ah, axis=-1, keepdims=True))).astype(BF16)

        one(aq_ref, qw_ref, dqn_ref, daq_ref, gq_ref)
        one(ak_ref, kw_ref, dkn_ref, dak_ref, gk_ref)
        dav_ref[...] = dv_ref[...].astype(BF16)

    wide = HEADS * HEAD_DIM
    vec = pl.BlockSpec((1, HEAD_DIM), lambda h: (0, 0))
    res, extra = _host_call(
        body, 7, 5, fused, _head_first_last, name="qk_bwd", grid=(HEADS,),
        in_specs=[_head_col(ATT_OFF), _head_col(ATT_OFF + HEADS), vec, vec, _head_col(0), _head_col(0), _head_col(0)],
        out_specs=[_head_col(0)] * 3 + [vec, vec],
        out_shape=[jax.ShapeDtypeStruct((SEQ, wide), BF16)] * 3 + [jax.ShapeDtypeStruct((1, HEAD_DIM), F32)] * 2,
        scratch_shapes=[], sem=("arbitrary",), operands=[proj, proj, q_w, k_w, dqn, dkn, dv] + list(fused_arrays))
    return res if fused is None else (res, extra)


def _pair_sum(name, partial, theirs, core):
    _, r, c = theirs.shape
    tr = r // 2 if r % 16 == 0 else r

    def body(core_ref, a_ref, b_ref, o_ref):
        o_ref[...] = (a_ref[...].astype(F32) + b_ref[...].astype(F32)).astype(BF16)

    spec = pl.BlockSpec((None, tr, c), lambda q, i, core_ref: (q, i, 0))
    grid_spec = pltpu.PrefetchScalarGridSpec(
        num_scalar_prefetch=1, grid=(4, r // tr),
        in_specs=[pl.BlockSpec((None, tr, c), lambda q, i, core_ref: (2 * q + core_ref[0], i, 0)), spec],
        out_specs=spec)
    return pl.pallas_call(body, name=name, grid_spec=grid_spec, out_shape=jax.ShapeDtypeStruct(theirs.shape, BF16),
                          compiler_params=_params(("parallel", "parallel")))(core, partial, theirs)


def _adamw_step(w, m, v, g):
    nm = ADAM_B1 * m + (1.0 - ADAM_B1) * g
    nv = ADAM_B2 * v + (1.0 - ADAM_B2) * (g * g)
    m_hat = nm / (1.0 - ADAM_B1 ** ADAM_STEP)
    v_hat = nv / (1.0 - ADAM_B2 ** ADAM_STEP)
    return -ADAM_LR * (m_hat / (jnp.sqrt(v_hat) + ADAM_EPS) + ADAM_WD * w), nm, nv


def _adamw(name, w, m, v, addends, tr=None):
    r, c = w.shape
    tr = r if tr is None else tr
    n_add = len(addends)

    def body(*refs):
        w_ref, m_ref, v_ref = refs[:3]
        add_refs = refs[3:3 + n_add]
        g_ref, d_ref, nm_ref, nv_ref = refs[3 + n_add:]
        g = add_refs[0][...].astype(F32)
        for a_ref in add_refs[1:]:
            g = g + a_ref[...].astype(F32)
        g_ref[...] = g
        d_ref[...], nm_ref[...], nv_ref[...] = _adamw_step(w_ref[...], m_ref[...], v_ref[...], g)

    spec = pl.BlockSpec((tr, c), lambda i: (i, 0))
    out = jax.ShapeDtypeStruct((r, c), F32)
    return pl.pallas_call(body, name=name, grid=(r // tr,), in_specs=[spec] * (3 + n_add), out_specs=[spec] * 4,
                          out_shape=[out] * 4, compiler_params=_params(("parallel",)))(w, m, v, *addends)


def _adamw_reduced(name, w, m, v, chip_sums, received, chip, tr):
    r, c = w.shape

    def body(chip_ref, w_ref, m_ref, v_ref, own_ref, r0_ref, r1_ref, r2_ref, g_ref, d_ref, nm_ref, nv_ref):
        g = ((own_ref[...].astype(F32) + r0_ref[...].astype(F32)) + r1_ref[...].astype(F32)) + r2_ref[...].astype(F32)
        g_ref[...] = g
        d_ref[...], nm_ref[...], nv_ref[...] = _adamw_step(w_ref[...], m_ref[...], v_ref[...], g)

    spec = pl.BlockSpec((tr, c), lambda i, chip_ref: (i, 0))

    def slot(k):
        return pl.BlockSpec((None, tr, c), lambda i, chip_ref: (k, i, 0))

    grid_spec = pltpu.PrefetchScalarGridSpec(
        num_scalar_prefetch=1, grid=(r // tr,),
        in_specs=[spec, spec, spec, pl.BlockSpec((None, tr, c), lambda i, chip_ref: (chip_ref[0], i, 0)),
                  slot(0), slot(1), slot(2)],
        out_specs=[spec] * 4)
    out = jax.ShapeDtypeStruct((r, c), F32)
    return pl.pallas_call(body, name=name, grid_spec=grid_spec, out_shape=[out] * 4,
                          compiler_params=_params(("parallel",)))(chip, w, m, v, chip_sums, received, received, received)


def _sum_devices(gathered):
    _, r, c = gathered.shape

    def body(g_ref, o_ref):
        acc = g_ref[0]
        for d in range(1, N_DEV):
            acc = acc + g_ref[d]
        o_ref[...] = acc

    return pl.pallas_call(body, name="sum_devices", out_shape=jax.ShapeDtypeStruct((r, c), F32))(gathered)


def _pack_rows(vectors, rows):
    flat = jnp.concatenate([v.reshape(-1) for v in vectors])
    return jnp.pad(flat, (0, rows * 128 - flat.shape[0])).reshape(rows, 128)


def _unpack(flat, shapes):
    out, off = [], 0
    for shp in shapes:
        n = 1
        for d in shp:
            n *= d
        out.append(flat[off:off + n].reshape(shp))
        off += n
    return out


def _device_step(xs, tgt, mod, norm1_w, norm2_w, lb_logits, hg_norm_w, q_norm_w, k_norm_w, conv_w_full, conv_b,
                 win_g, w_out_x, w_up_x, w_down_x, core=None):
    fused = core is not None
    shift1, scale1, gate1, shift2, scale2, gate2 = (mod[k] for k in range(6))

    h, rstd1 = _norm_fwd("norm1_fwd", xs, norm1_w, scale1, shift1)
    if fused:
        near = (0, 1, 2)
        head_rows, tail_rows = (0, UP_HEAD_ROWS), (UP_HEAD_ROWS, D_MODEL - UP_HEAD_ROWS)
        proj, (wout_g, wup_g) = _mm_blocked_rhs(
            "mm_in", h, win_g, fused_arrays=[w_out_x, w_up_x],
            fused=[_FusedCopies("gather", [w_out_x]), _FusedCopies("gather", [w_up_x], peers=near, rows=head_rows)])
        (a_out, o_pre), (wup_g, wout_g) = _hgrn_fwd(
            proj, lb_logits, hg_norm_w, fused_arrays=[w_up_x, wup_g, wout_g],
            fused=[_FusedCopies("gather_more", [w_up_x, wup_g], peers=near, rows=tail_rows, relay_rows=head_rows),
                   _FusedCopies("forward", [wout_g])])
        wout_full = wout_g.reshape(D_MODEL, D_MODEL)
        (qn, kn, vb), _ = _qk_prep(proj, q_norm_w, k_norm_w)
        (att_o, lse), (wup_g,) = _attn_fwd(qn, kn, vb, _FusedCopies("relay", [wup_g], rows=tail_rows), [wup_g])
    else:
        proj = _mm_blocked_rhs("mm_in", h, win_g)
        (a_out, o_pre), _ = _hgrn_fwd(proj, lb_logits, hg_norm_w)
        wup_g, wout_full, wdown_full = w_up_x, w_out_x, w_down_x
        (qn, kn, vb), _ = _qk_prep(proj, q_norm_w, k_norm_w)
        (att_o, lse), _ = _attn_fwd(qn, kn, vb)
    mixin = jnp.concatenate([a_out, att_o.astype(BF16)], axis=1)
    if fused:
        down_head = (0, DOWN_HEAD_ROWS)
        mix, (wup_g, wdown_g) = _mm_plain(
            "mm_out", mixin, wout_full, NN, 512, 1024, F32, fused_arrays=[wup_g, w_down_x],
            fused=[_FusedCopies("forward", [wup_g]), _FusedCopies("gather", [w_down_x], rows=down_head)])
    else:
        mix = _mm_plain("mm_out", mixin, wout_full, NN, 512, 1024, F32)
    x1, h2, rstd2 = _norm_fwd("norm2_fwd", xs, norm2_w, scale2, shift2, resid=mix, gate=gate1)
    if fused:
        down_tail = (DOWN_HEAD_ROWS, FF_BLK - DOWN_HEAD_ROWS)
        u, (wdown_g,) = _mm_blocked_rhs(
            "mm_up", h2, wup_g, fused_arrays=[w_down_x, wdown_g],
            fused=_FusedCopies("gather_more", [w_down_x, wdown_g], rows=down_tail))
        y, (wdown_g,) = _conv_gate_fwd(u, conv_w_full, conv_b, _FusedCopies("forward", [wdown_g]), [wdown_g])
        wdown_full = wdown_g.reshape(D_FF, D_MODEL)
    else:
        u = _mm_blocked_rhs("mm_up", h2, wup_g)
        y = _conv_gate_fwd(u, conv_w_full, conv_b)
    ffn = _mm_plain("mm_down", y, wdown_full, NN, MM_TILE, 512, F32)
    loss_v, dout, dffn, dgate2 = _loss_head(x1, ffn, gate2, tgt)

    dy = _mm_plain("mm_down_dx", dffn, wdown_full, NT, MM_TILE, UP_BLK, BF16)
    gw_down = _mm_plain("mm_down_dw", y, dffn, TN, UP_BLK, 1024, BF16)
    da, dg, gconv_w, gconv_b = _conv_gate_bwd(u, dy, conv_w_full, conv_b)
    dh2 = _mm_halves_rhs_t("mm_up_dx", da, dg, wup_g)
    gw_up = _mm_halves_wgrad("mm_up_dw", h2, da, dg)
    if fused:
        part_up, part_down = gw_up, gw_down.reshape(N_DEV, FF_BLK, D_MODEL)
        (dx1, dmix, dshift2, dscale2, gnorm2, dgate1), (sib_up,) = _norm_bwd(
            "norm2_bwd", dh2, x1, rstd2, norm2_w, scale2, dout, mix=mix, gate=gate1,
            fused=_FusedCopies("sibling", [part_up]), fused_arrays=[part_up])
    else:
        dx1, dmix, dshift2, dscale2, gnorm2, dgate1 = _norm_bwd(
            "norm2_bwd", dh2, x1, rstd2, norm2_w, scale2, dout, mix=mix, gate=gate1)
    gw_out = _mm_plain("mm_out_dw", mixin, dmix, TN, 512, 1024, BF16)
    if fused:
        part_out = gw_out.reshape(N_DEV, OUT_BLK, D_MODEL)
        dmixin, (sib_out, sib_down) = _mm_plain(
            "mm_out_dx", dmix, wout_full, NT, 512, 1024, F32,
            fused=_FusedCopies("sibling", [part_out, part_down]), fused_arrays=[part_out, part_down])
        cs_up = _pair_sum("grad_pair_sum_up", part_up, sib_up, core)
        cs_out = _pair_sum("grad_pair_sum_out", part_out, sib_out, core)
        cs_down = _pair_sum("grad_pair_sum_down", part_down, sib_down, core)
        (dhq, dhf, dhi, dhg, glog, ghg), (fc_up,) = _hgrn_bwd(
            proj, lb_logits, hg_norm_w, o_pre, dmixin, _FusedCopies("chips", [cs_up], rows=(0, UP_EXCHANGE_HEAD)),
            [cs_up])
        (dqn, dkn, dvv), (fc_down,) = _attn_bwd(
            qn, kn, vb, att_o, lse, dmixin, _FusedCopies("chips", [cs_down], rows=(0, DOWN_EXCHANGE_HEAD)), [cs_down])
        (daq, dak, dav, gqw, gkw), (fc_up,) = _qk_bwd(
            proj, q_norm_w, k_norm_w, dqn, dkn, dvv, fused_arrays=[cs_up, fc_up],
            fused=_FusedCopies("chips_more", [cs_up, fc_up], rows=(UP_EXCHANGE_HEAD, D_MODEL - UP_EXCHANGE_HEAD)))
    else:
        dmixin = _mm_plain("mm_out_dx", dmix, wout_full, NT, 512, 1024, F32)
        (dhq, dhf, dhi, dhg, glog, ghg), _ = _hgrn_bwd(proj, lb_logits, hg_norm_w, o_pre, dmixin)
        (dqn, dkn, dvv), _ = _attn_bwd(qn, kn, vb, att_o, lse, dmixin)
        daq, dak, dav, gqw, gkw = _qk_bwd(proj, q_norm_w, k_norm_w, dqn, dkn, dvv)
    dproj = jnp.concatenate([dhq, dhf, dhi, dhg, daq, dak, dav], axis=1)
    if fused:
        down_tail = (DOWN_EXCHANGE_HEAD, FF_BLK - DOWN_EXCHANGE_HEAD)
        gw_in, (fc_out, fc_down) = _mm_wgrad_blocked(
            "mm_in_dw", h, dproj, fused_arrays=[cs_out, cs_down, fc_down],
            fused=[_FusedCopies("chips", [cs_out]), _FusedCopies("chips_more", [cs_down, fc_down], rows=down_tail)])
        from_sibling, = _exchange_sibling("grad_exchange_sibling_b", [gw_in])
        cs_in = _pair_sum("grad_pair_sum_in", gw_in, from_sibling, core)
        dh, (fc_in,) = _mm_blocked_rhs_t("mm_in_dx", dproj, win_g, fused=_FusedCopies("chips", [cs_in]),
                                         fused_arrays=[cs_in])
        large = [(cs_in, fc_in), (cs_out, fc_out), (cs_up, fc_up), (cs_down, fc_down)]
    else:
        gw_in = _mm_wgrad_blocked("mm_in_dw", h, dproj)
        dh = _mm_blocked_rhs_t("mm_in_dx", dproj, win_g)
        large = [gw_in, gw_out, gw_up, gw_down]
    grad_x, dshift1, dscale1, gnorm1 = _norm_bwd("norm1_bwd", dh, xs, rstd1, norm1_w, scale1, dx1)
    gmod = jnp.concatenate([dshift1, dscale1, dgate1, dshift2, dscale2, dgate2], axis=1)
    return (loss_v, grad_x, gmod, gnorm1, gnorm2, glog, ghg, gqw, gkw, gconv_b, gconv_w, *large)


def kernel(x, c, w_ada, b_ada, norm1_w, w_in, lb_logits, hg_norm_w, q_norm_w, k_norm_w, w_out, norm2_w, w_up, conv_w, conv_b, w_down, loss_target, m_w_ada, m_b_ada, m_norm1_w, m_w_in, m_lb_logits, m_hg_norm_w, m_q_norm_w, m_k_norm_w, m_w_out, m_norm2_w, m_w_up, m_conv_w, m_conv_b, m_w_down, v_w_ada, v_b_ada, v_norm1_w, v_w_in, v_lb_logits, v_hg_norm_w, v_q_norm_w, v_k_norm_w, v_w_out, v_norm2_w, v_w_up, v_conv_w, v_conv_b, v_w_down):
    ix, iy, ic = lax.axis_index("x"), lax.axis_index("y"), lax.axis_index("c")
    me = 4 * ix + 2 * iy + ic
    my_chip = 2 * ix + iy

    xs = x[0]
    tgt = loss_target[0]

    win_g, = _allgather_weights([w_in[0].astype(BF16)])

    first = _allgather_vmem(_pack_rows([c, conv_w[0]], 40), "allgather_c_conv_w").reshape(N_DEV, 40 * 128)
    c_all = first[:, :D_MODEL]
    conv_w_full = (first[:, D_MODEL:D_MODEL + 3 * FF_BLK].reshape(N_DEV, 3, FF_BLK).transpose(1, 0, 2)
                   .reshape(3, D_FF))

    b_blk = lax.dynamic_slice_in_dim(b_ada, me * ADA_BLK, ADA_BLK, axis=1)
    mod_cols = _ada_fwd(c_all, w_ada[0], b_blk)
    mod_all = _allgather_vmem(mod_cols, "allgather_mod").reshape(N_DEV, N_DEV, ADA_BLK)
    mod = lax.dynamic_index_in_dim(mod_all, me, axis=1, keepdims=False).reshape(6, 1, D_MODEL)

    (loss_v, grad_x, gmod, gnorm1, gnorm2, glog, ghg, gqw, gkw, gconv_b, gconv_w,
     rs_in, rs_out, rs_up, rs_down) = _device_step(
        xs, tgt, mod, norm1_w, norm2_w, lb_logits, hg_norm_w, q_norm_w, k_norm_w, conv_w_full, conv_b,
        win_g, w_out[0].astype(BF16), w_up[0].astype(BF16), w_down[0].astype(BF16),
        core=jnp.reshape(ic, (1,)).astype(jnp.int32))

    small_shapes = [(1, 6 * D_MODEL), (1, D_MODEL), (1, D_MODEL), (2, HEADS * HEAD_DIM), (1, HEAD_DIM),
                    (1, HEAD_DIM), (1, HEAD_DIM), (1, D_FF), (3, D_FF), (1, 1)]
    small = [gmod, gnorm1, gnorm2, glog, ghg, gqw, gkw, gconv_b, gconv_w, loss_v[:, 0:1]]
    n_small = sum(a.size for a in small)
    rows = -(-n_small // 1024) * 8
    gathered = _allgather_vmem(_pack_rows(small, rows), "allgather_small").reshape(N_DEV, rows, 128)
    summed = _sum_devices(gathered).reshape(-1)
    (g_b_ada, g_norm1, g_norm2, g_lb, g_hg, g_q, g_k, g_conv_b, g_conv_w_full, loss_sum) = _unpack(summed, small_shapes)
    loss = loss_sum[0, 0]
    g_conv_w = lax.dynamic_slice_in_dim(g_conv_w_full, me * FF_BLK, FF_BLK, axis=1)

    gmod_all = gathered[:, :6 * D_MODEL // 128, :].reshape(N_DEV, 6 * D_MODEL)
    gmod_cols = lax.dynamic_slice_in_dim(gmod_all, me * ADA_BLK, ADA_BLK, axis=1)
    g_w_ada_raw = _ada_wgrad(c_all, gmod_cols)

    chip = jnp.reshape(my_chip, (1,)).astype(jnp.int32)

    def big_update(name, w, m, v, rs, tr):
        chip_sums, received = rs
        return _adamw_reduced(name, w[0], m[0], v[0], chip_sums, received, chip, tr)

    r_in = big_update("adamw_w_in", w_in, m_w_in, v_w_in, rs_in, 256)
    r_out = big_update("adamw_w_out", w_out, m_w_out, v_w_out, rs_out, 128)
    r_up = big_update("adamw_w_up", w_up, m_w_up, v_w_up, rs_up, 256)
    r_down = big_update("adamw_w_down", w_down, m_w_down, v_w_down, rs_down, 176)
    r_ada = _adamw("adamw_w_ada", w_ada[0], m_w_ada[0], v_w_ada[0], [g_w_ada_raw], tr=256)
    r_convw = _adamw("adamw_conv_w", conv_w[0], m_conv_w[0], v_conv_w[0], [g_conv_w])

    rep_shapes = [(1, 6 * D_MODEL), (1, D_MODEL), (1, D_MODEL), (2, HEADS * HEAD_DIM), (1, HEAD_DIM),
                  (1, HEAD_DIM), (1, HEAD_DIM), (1, D_FF)]
    rep_rows = -(-sum(a * b for a, b in rep_shapes) // 1024) * 8
    pack = lambda arrs: _pack_rows(arrs, rep_rows)
    rep = _adamw("adamw_small",
                 pack([b_ada, norm1_w, norm2_w, lb_logits, hg_norm_w, q_norm_w, k_norm_w, conv_b]),
                 pack([m_b_ada, m_norm1_w, m_norm2_w, m_lb_logits, m_hg_norm_w, m_q_norm_w, m_k_norm_w, m_conv_b]),
                 pack([v_b_ada, v_norm1_w, v_norm2_w, v_lb_logits, v_hg_norm_w, v_q_norm_w, v_k_norm_w, v_conv_b]),
                 [pack([g_b_ada, g_norm1, g_norm2, g_lb, g_hg, g_q, g_k, g_conv_b])])
    rep = [_unpack(r.reshape(-1), rep_shapes) for r in rep]

    def big(r):
        return [a[None] for a in r]

    order = {"w_ada": big(r_ada), "b_ada": [r[0] for r in rep], "norm1_w": [r[1] for r in rep],
             "w_in": big(r_in), "lb_logits": [r[3] for r in rep], "hg_norm_w": [r[4] for r in rep],
             "q_norm_w": [r[5] for r in rep], "k_norm_w": [r[6] for r in rep], "w_out": big(r_out),
             "norm2_w": [r[2] for r in rep], "w_up": big(r_up), "conv_w": big(r_convw),
             "conv_b": [r[7] for r in rep], "w_down": big(r_down)}
    names = ["w_ada", "b_ada", "norm1_w", "w_in", "lb_logits", "hg_norm_w", "q_norm_w", "k_norm_w", "w_out",
             "norm2_w", "w_up", "conv_w", "conv_b", "w_down"]
    outs = [loss, grad_x[None]]
    for kind in range(4):
        outs += [order[n][kind] for n in names]
    return tuple(outs)
```

```python
import jax
import jax.numpy as jnp
import numpy as np
from jax import lax
from jax.experimental import pallas as pl
from jax.experimental.pallas import tpu as pltpu

F32 = jnp.float32
BF16 = jnp.bfloat16

N_DEV = 8
SEQ = 2048
D_MODEL = 2048
HEADS = 8
HEAD_DIM = 128
IN_COLS = 7168
IN_BLK = IN_COLS // N_DEV
D_FF = 5632
UP_BLK = 2 * D_FF // N_DEV
FF_BLK = D_FF // N_DEV
ADA_BLK = 6 * D_MODEL // N_DEV
OUT_BLK = D_MODEL // N_DEV
EPS = 1e-6
CHUNK = 16
ROW_TILE = 256
MM_TILE = 1024
V7X_VMEM_LIMIT = 56 * 1024 * 1024

ADAM_LR = 0.001
ADAM_B1 = 0.9
ADAM_B2 = 0.999
ADAM_EPS = 1e-08
ADAM_WD = 0.01
ADAM_STEP = 10

NN = (((1,), (0,)), ((), ()))
NT = (((1,), (1,)), ((), ()))
TN = (((0,), (0,)), ((), ()))
MESH = pl.DeviceIdType.MESH


def _params(sem=None, vmem=V7X_VMEM_LIMIT):
    return pltpu.CompilerParams(dimension_semantics=sem, vmem_limit_bytes=vmem)


def _sigmoid(x):
    return 1.0 / (1.0 + jnp.exp(-x))


def _dsilu(x, s):
    return s * (1.0 + x * (1.0 - s))


def _lane_sum(x, ones_bf16):
    return jnp.dot(x.astype(BF16), ones_bf16, preferred_element_type=F32)


def _mesh_pos():
    return lax.axis_index("x"), lax.axis_index("y"), lax.axis_index("c")


def _allgather_vmem(x_blk, name):
    m_per, n = x_blk.shape

    def body(x_ref, out_ref, send_sems, recv_sems, local_sem):
        x, y, c = _mesh_pos()
        me, sibling = (x, y, c), (x, y, 1 - c)
        chips = [(1 - x, y), (x, 1 - y), (1 - x, 1 - y)]

        def rows(px, py, pc):
            return out_ref.at[pl.ds((4 * px + 2 * py + pc) * m_per, m_per), :]

        def copy(k, block, to, src=None):
            return pltpu.make_async_remote_copy(
                src_ref=rows(*block) if src is None else src, dst_ref=rows(*block),
                send_sem=send_sems.at[k], recv_sem=recv_sems.at[k], device_id=to, device_id_type=MESH)

        mine = pltpu.make_async_copy(x_ref, rows(*me), local_sem)
        mine.start()
        first = [copy(0, me, sibling, src=x_ref)]
        first += [copy(1 + j, me, (*chip, c), src=x_ref) for j, chip in enumerate(chips)]
        for cp in first:
            cp.start()
        passed = [copy(4 + j, (*chip, c), sibling) for j, chip in enumerate(chips)]
        for j, chip in enumerate(chips):
            copy(1 + j, (*chip, c), me).wait_recv()
            passed[j].start()
        copy(0, sibling, me).wait_recv()
        for j, chip in enumerate(chips):
            copy(4 + j, (*chip, 1 - c), me).wait_recv()
        for cp in first + passed:
            cp.wait_send()
        mine.wait()

    return pl.pallas_call(
        body, name=name,
        out_shape=jax.ShapeDtypeStruct((N_DEV * m_per, n), x_blk.dtype),
        in_specs=[pl.BlockSpec(memory_space=pltpu.VMEM)],
        out_specs=pl.BlockSpec(memory_space=pltpu.VMEM),
        scratch_shapes=[pltpu.SemaphoreType.DMA((7,)), pltpu.SemaphoreType.DMA((7,)), pltpu.SemaphoreType.DMA],
    )(x_blk)


def _flip(v, bit):
    return v + bit - 2 * v * bit


def _relay_chips(x, y, c):
    return (_flip(x, 1 - c), _flip(y, c)), (_flip(x, c), _flip(y, 1 - c))


UP_EXCHANGE_HEAD = 1856
DOWN_EXCHANGE_HEAD = 560
DOWN_HEAD_ROWS = 192
UP_HEAD_ROWS = 768
GATHER_PARTS = 4


def _allgather_weights(blocks):
    n_arr = len(blocks)
    parts = GATHER_PARTS

    def body(*refs):
        ins, outs = refs[:n_arr], refs[n_arr:2 * n_arr]
        send_sems, recv_sems, local_sems = refs[2 * n_arr:]
        x, y, c = _mesh_pos()
        me, sibling = (x, y, c), (x, y, 1 - c)
        near = [(1 - x, y), (x, 1 - y)]
        chips = near + [(1 - x, 1 - y)]
        relay_from, relay_to = _relay_chips(x, y, c)

        def rows(a, p):
            hr = ins[a].shape[0] // parts
            return pl.ds(p * hr, hr)

        def slot(a, pos, p):
            return outs[a].at[4 * pos[0] + 2 * pos[1] + pos[2], rows(a, p)]

        def copy(a, k, p, src, lands, to):
            return pltpu.make_async_remote_copy(
                src_ref=src, dst_ref=slot(a, lands, p), send_sem=send_sems.at[a, k, p], recv_sem=recv_sems.at[a, k, p],
                device_id=to, device_id_type=MESH)

        sent = []
        local = [pltpu.make_async_copy(ins[a], outs[a].at[4 * x + 2 * y + c], local_sems.at[a]) for a in range(n_arr)]
        for cp in local:
            cp.start()
        for p in range(parts):
            for a in range(n_arr):
                own = ins[a].at[rows(a, p)]
                sent.append(copy(a, 0, p, own, me, sibling))
                sent += [copy(a, 1 + j, p, own, me, (*chip, c)) for j, chip in enumerate(near)]
        for cp in sent:
            cp.start()

        def start(cp):
            cp.start()
            sent.append(cp)

        for p in range(parts):
            for a in range(n_arr):
                for j, chip in enumerate(near):
                    copy(a, 1 + j, p, ins[a].at[rows(a, p)], (*chip, c), me).wait_recv()
                    start(copy(a, 4 + j, p, slot(a, (*chip, c), p), (*chip, c), sibling))
                start(copy(a, 3, p, slot(a, (*relay_from, c), p), (*relay_from, c), (*relay_to, c)))
        for p in range(parts):
            for a in range(n_arr):
                copy(a, 3, p, ins[a].at[rows(a, p)], (*chips[2], c), me).wait_recv()
                start(copy(a, 6, p, slot(a, (*chips[2], c), p), (*chips[2], c), sibling))
        for p in range(parts):
            for a in range(n_arr):
                copy(a, 0, p, ins[a].at[rows(a, p)], sibling, me).wait_recv()
                for j, chip in enumerate(chips):
                    copy(a, 4 + j, p, ins[a].at[rows(a, p)], (*chip, 1 - c), me).wait_recv()
        for cp in sent:
            cp.wait_send()
        for cp in local:
            cp.wait()

    return pl.pallas_call(
        body, name="allgather_weights",
        out_shape=[jax.ShapeDtypeStruct((N_DEV,) + b.shape, b.dtype) for b in blocks],
        in_specs=[pl.BlockSpec(memory_space=pltpu.HBM)] * n_arr, out_specs=[pl.BlockSpec(memory_space=pltpu.HBM)] * n_arr,
        scratch_shapes=[pltpu.SemaphoreType.DMA((n_arr, 7, parts)), pltpu.SemaphoreType.DMA((n_arr, 7, parts)),
                        pltpu.SemaphoreType.DMA((n_arr,))],
    )(*blocks)


HBM_SPEC = pl.BlockSpec(memory_space=pltpu.HBM)


class _FusedCopies:
    def __init__(self, kind, arrays, peers=(0, 1, 2, 3), rows=None, relay_rows=None):
        self.kind = kind
        self.peers = peers
        self.rows = rows
        self.relay_rows = relay_rows
        n = len(arrays) // 2 if kind == "gather_more" else len(arrays)
        self.n = n
        self.n_in = len(arrays)
        self.aliases = {}
        if kind == "gather":
            self.out_shape = [jax.ShapeDtypeStruct((N_DEV,) + a.shape, a.dtype) for a in arrays]
            self.scratch_shapes = [pltpu.SemaphoreType.DMA((n, 4, GATHER_PARTS)),
                                   pltpu.SemaphoreType.DMA((n, 4, GATHER_PARTS)), pltpu.SemaphoreType.DMA((n,))]
        elif kind == "gather_more":
            self.out_shape = [jax.ShapeDtypeStruct(a.shape, a.dtype) for a in arrays[n:]]
            self.scratch_shapes = [pltpu.SemaphoreType.DMA((n, 5, GATHER_PARTS)),
                                   pltpu.SemaphoreType.DMA((n, 5, GATHER_PARTS)), pltpu.SemaphoreType.DMA((n,))]
            self.aliases = {n + a: a for a in range(n)}
        elif kind == "relay":
            self.out_shape = [jax.ShapeDtypeStruct(a.shape, a.dtype) for a in arrays]
            self.scratch_shapes = [pltpu.SemaphoreType.DMA((n,)), pltpu.SemaphoreType.DMA((n,))]
            self.aliases = {a: a for a in range(n)}
        elif kind == "forward":
            self.out_shape = [jax.ShapeDtypeStruct(a.shape, a.dtype) for a in arrays]
            self.scratch_shapes = [pltpu.SemaphoreType.DMA((n, 3)), pltpu.SemaphoreType.DMA((n, 3))]
            self.aliases = {a: a for a in range(n)}
        elif kind == "sibling":
            self.out_shape = [jax.ShapeDtypeStruct((4,) + a.shape[1:], a.dtype) for a in arrays]
            self.scratch_shapes = [pltpu.SemaphoreType.DMA((n, 4)), pltpu.SemaphoreType.DMA((n, 4))]
        elif kind == "chips_more":
            n = self.n = len(arrays) // 2
            self.out_shape = [jax.ShapeDtypeStruct(a.shape, a.dtype) for a in arrays[n:]]
            self.scratch_shapes = [pltpu.SemaphoreType.DMA((n, 3)), pltpu.SemaphoreType.DMA((n, 3))]
            self.aliases = {n + a: a for a in range(n)}
        else:
            self.out_shape = [jax.ShapeDtypeStruct((3,) + a.shape[1:], a.dtype) for a in arrays]
            self.scratch_shapes = [pltpu.SemaphoreType.DMA((n, 3)), pltpu.SemaphoreType.DMA((n, 3))]
        self.in_specs = [HBM_SPEC] * self.n_in
        self.out_specs = [HBM_SPEC] * n
        self.n_scratch = len(self.scratch_shapes)

    def copies(self, ins, outs, sems):
        x, y, c = _mesh_pos()
        chips = [(1 - x, y), (x, 1 - y), (1 - x, 1 - y)]
        sibling = (x, y, 1 - c)
        starts, waits = [], []
        relay_from, relay_to = _relay_chips(x, y, c)

        def relayed(a, buf, lands, send_sem, recv_sem, rows):
            first, count = rows or (0, buf.shape[1])
            span = pl.ds(first, count)
            return pltpu.make_async_remote_copy(
                src_ref=buf.at[4 * relay_from[0] + 2 * relay_from[1] + c, span],
                dst_ref=outs[a].at[4 * lands[0] + 2 * lands[1] + c, span], send_sem=send_sem, recv_sem=recv_sem,
                device_id=(*relay_to, c), device_id_type=MESH)

        if self.kind in ("gather", "gather_more"):
            send_sems, recv_sems, local_sems = sems
            me = (x, y, c)
            peers = [sibling] + [(px, py, c) for px, py in chips]

            def slot(a, pos):
                return outs[a].at[4 * pos[0] + 2 * pos[1] + pos[2]]

            def span(a, p=None):
                first, count = self.rows or (0, ins[a].shape[0])
                if p is None:
                    return pl.ds(first, count)
                return pl.ds(first + p * (count // GATHER_PARTS), count // GATHER_PARTS)

            def remote(a, k, p, lands_from):
                return pltpu.make_async_remote_copy(
                    src_ref=ins[a].at[span(a, p)], dst_ref=slot(a, lands_from).at[span(a, p)],
                    send_sem=send_sems.at[a, k, p], recv_sem=recv_sems.at[a, k, p], device_id=peers[k],
                    device_id_type=MESH)

            for a in range(self.n):
                local = pltpu.make_async_copy(ins[a].at[span(a)], slot(a, me).at[span(a)], local_sems.at[a])
                starts.append(local)
                waits.append(local)
            for p in range(GATHER_PARTS):
                for a in range(self.n):
                    for k in self.peers:
                        starts.append(remote(a, k, p, me))
                        waits.append(remote(a, k, p, peers[k]))
            if self.kind == "gather_more" and self.relay_rows is not None:
                for a in range(self.n):
                    buf = ins[self.n + a]
                    starts.append(relayed(a, buf, relay_from, send_sems.at[a, 4, 0], recv_sems.at[a, 4, 0],
                                          self.relay_rows))
                    waits.append(relayed(a, buf, chips[2], send_sems.at[a, 4, 0], recv_sems.at[a, 4, 0],
                                         self.relay_rows))
        elif self.kind == "relay":
            send_sems, recv_sems = sems
            for a in range(self.n):
                starts.append(relayed(a, ins[a], relay_from, send_sems.at[a], recv_sems.at[a], self.rows))
                waits.append(relayed(a, ins[a], chips[2], send_sems.at[a], recv_sems.at[a], self.rows))
        elif self.kind == "forward":
            send_sems, recv_sems = sems

            def passed_on(a, j, pc_src, pc_dst):
                px, py = chips[j]
                return pltpu.make_async_remote_copy(
                    src_ref=ins[a].at[4 * px + 2 * py + pc_src], dst_ref=outs[a].at[4 * px + 2 * py + pc_dst],
                    send_sem=send_sems.at[a, j], recv_sem=recv_sems.at[a, j], device_id=sibling, device_id_type=MESH)

            for a in range(self.n):
                for j in range(3):
                    starts.append(passed_on(a, j, c, c))
                    waits.append(passed_on(a, j, c, 1 - c))
        elif self.kind == "sibling":
            send_sems, recv_sems = sems
            for a in range(self.n):
                for q in range(4):
                    cp = pltpu.make_async_remote_copy(
                        src_ref=ins[a].at[2 * q + 1 - c], dst_ref=outs[a].at[q], send_sem=send_sems.at[a, q],
                        recv_sem=recv_sems.at[a, q], device_id=sibling, device_id_type=MESH)
                    starts.append(cp)
                    waits.append(cp)
        else:
            send_sems, recv_sems = sems
            for a in range(self.n):
                first, count = self.rows or (0, ins[a].shape[1])
                span = pl.ds(first, count)
                for j, (px, py) in enumerate(chips):
                    cp = pltpu.make_async_remote_copy(
                        src_ref=ins[a].at[2 * px + py, span], dst_ref=outs[a].at[j, span],
                        send_sem=send_sems.at[a, j], recv_sem=recv_sems.at[a, j], device_id=(px, py, c),
                        device_id_type=MESH)
                    starts.append(cp)
                    waits.append(cp)
        return starts, waits


def _fused_groups(fused):
    if fused is None:
        return []
    return list(fused) if isinstance(fused, (list, tuple)) else [fused]


def _host_body(body, n_in, n_out, fused, first_last):
    groups = _fused_groups(fused)
    if not groups:
        return body
    n_fin, n_fout = sum(g.n_in for g in groups), sum(g.n for g in groups)
    n_fsem = sum(g.n_scratch for g in groups)

    def wrapped(*refs):
        core_in, f_in = refs[:n_in], refs[n_in:n_in + n_fin]
        core_out = refs[n_in + n_fin:n_in + n_fin + n_out]
        f_out = refs[n_in + n_fin + n_out:n_in + n_fin + n_out + n_fout]
        rest = refs[n_in + n_fin + n_out + n_fout:]
        core_scratch, f_sems = rest[:len(rest) - n_fsem], rest[len(rest) - n_fsem:]
        starts, waits = [], []
        for g in groups:
            s, w = g.copies(f_in[:g.n_in], f_out[:g.n], f_sems[:g.n_scratch])
            f_in, f_out, f_sems = f_in[g.n_in:], f_out[g.n:], f_sems[g.n_scratch:]
            starts += s
            waits += w
        first, last = first_last()

        @pl.when(first)
        def _():
            for cp in starts:
                cp.start()

        body(*core_in, *core_out, *core_scratch)

        @pl.when(last)
        def _():
            for cp in waits:
                cp.wait()

    return wrapped


def _host_call(body, n_in, n_out, fused, first_last, *, name, grid, in_specs, out_specs, out_shape, scratch_shapes,
               sem, operands):
    aliases = {}
    in_specs, out_specs, out_shape, scratch_shapes = list(in_specs), list(out_specs), list(out_shape), list(scratch_shapes)
    fin, fout = n_in, n_out
    for g in _fused_groups(fused):
        aliases.update({fin + fi: fout + fo for fi, fo in g.aliases.items()})
        fin, fout = fin + g.n_in, fout + g.n
        in_specs += g.in_specs
        out_specs += g.out_specs
        out_shape += g.out_shape
        scratch_shapes += g.scratch_shapes
        sem = tuple("arbitrary" for _ in sem)
    res = pl.pallas_call(_host_body(body, n_in, n_out, fused, first_last), name=name, grid=grid, in_specs=in_specs,
                         out_specs=out_specs, out_shape=out_shape, scratch_shapes=scratch_shapes,
                         input_output_aliases=aliases, compiler_params=_params(sem))(*operands)
    return list(res[:n_out]), list(res[n_out:])


def _exchange_sibling(name, partials):
    n_arr = len(partials)

    def body(*refs):
        ins, outs = refs[:n_arr], refs[n_arr:2 * n_arr]
        send_sems, recv_sems = refs[2 * n_arr:]
        x, y, c = _mesh_pos()
        copies = [pltpu.make_async_remote_copy(
            src_ref=ins[a].at[2 * q + 1 - c], dst_ref=outs[a].at[q], send_sem=send_sems.at[a, q],
            recv_sem=recv_sems.at[a, q], device_id=(x, y, 1 - c), device_id_type=MESH)
            for a in range(n_arr) for q in range(4)]
        for cp in copies:
            cp.start()
        for cp in copies:
            cp.wait_recv()
        for cp in copies:
            cp.wait_send()

    return pl.pallas_call(
        body, name=name,
        out_shape=[jax.ShapeDtypeStruct((4,) + p.shape[1:], p.dtype) for p in partials],
        in_specs=[HBM_SPEC] * n_arr, out_specs=[HBM_SPEC] * n_arr,
        scratch_shapes=[pltpu.SemaphoreType.DMA((n_arr, 4)), pltpu.SemaphoreType.DMA((n_arr, 4))],
    )(*partials)


def _matmul(name, a, b, dims, grid, a_spec, b_spec, o_spec, out_shape, acc_axis=None, fused=None, fused_arrays=()):
    def body(a_ref, b_ref, o_ref):
        r = lax.dot_general(a_ref[...], b_ref[...], dims, preferred_element_type=F32)
        if acc_axis is None:
            o_ref[...] = r.astype(o_ref.dtype)
        else:
            k = pl.program_id(acc_axis)

            @pl.when(k == 0)
            def _():
                o_ref[...] = r

            @pl.when(k > 0)
            def _():
                o_ref[...] += r

    sem = tuple("arbitrary" if i == acc_axis else "parallel" for i in range(len(grid)))
    if fused is None:
        return pl.pallas_call(body, name=name, grid=grid, in_specs=[a_spec, b_spec], out_specs=o_spec,
                              out_shape=out_shape, compiler_params=_params(sem))(a, b)

    def first_last():
        first = last = None
        for ax, n in enumerate(grid):
            f, l = pl.program_id(ax) == 0, pl.program_id(ax) == n - 1
            first, last = (f, l) if first is None else (first & f, last & l)
        return first, last

    (out,), extra = _host_call(body, 2, 1, fused, first_last, name=name, grid=grid, in_specs=[a_spec, b_spec],
                               out_specs=[o_spec], out_shape=[out_shape], scratch_shapes=[], sem=sem,
                               operands=[a, b] + list(fused_arrays))
    return out, extra


def _mm_blocked_rhs(name, a, w_g, tm=MM_TILE, fused=None, fused_arrays=()):
    m, k = a.shape
    nb = w_g.shape[2]
    return _matmul(name, a, w_g, NN, (N_DEV, m // tm),
                   pl.BlockSpec((tm, k), lambda j, i: (i, 0)),
                   pl.BlockSpec((None, k, nb), lambda j, i: (j, 0, 0)),
                   pl.BlockSpec((tm, nb), lambda j, i: (i, j)),
                   jax.ShapeDtypeStruct((m, N_DEV * nb), F32), fused=fused, fused_arrays=fused_arrays)


def _mm_blocked_rhs_t(name, a, w_g, tm=MM_TILE, fused=None, fused_arrays=()):
    m = a.shape[0]
    n, nb = w_g.shape[1], w_g.shape[2]
    return _matmul(name, a, w_g, NT, (m // tm, N_DEV),
                   pl.BlockSpec((tm, nb), lambda i, j: (i, j)),
                   pl.BlockSpec((None, n, nb), lambda i, j: (j, 0, 0)),
                   pl.BlockSpec((tm, n), lambda i, j: (i, 0)),
                   jax.ShapeDtypeStruct((m, n), F32), acc_axis=1, fused=fused, fused_arrays=fused_arrays)


def _mm_wgrad_blocked(name, act, dcols, tk=MM_TILE, fused=None, fused_arrays=()):
    t, k = act.shape
    nb = dcols.shape[1] // N_DEV
    return _matmul(name, act, dcols, TN, (N_DEV, k // tk),
                   pl.BlockSpec((t, tk), lambda j, i: (0, i)),
                   pl.BlockSpec((t, nb), lambda j, i: (0, j)),
                   pl.BlockSpec((None, tk, nb), lambda j, i: (j, i, 0)),
                   jax.ShapeDtypeStruct((N_DEV, k, nb), BF16), fused=fused, fused_arrays=fused_arrays)


def _halves_specs(block, index):
    half = N_DEV // 2
    return (pl.BlockSpec(block, lambda i, j: index(i, jnp.minimum(j, half - 1))),
            pl.BlockSpec(block, lambda i, j: index(i, jnp.maximum(j - half, 0))))


def _mm_halves_rhs_t(name, a_lo, a_hi, w_g, tm=MM_TILE, fused=None, fused_arrays=()):
    m = a_lo.shape[0]
    n, nb = w_g.shape[1], w_g.shape[2]

    def body(lo_ref, hi_ref, b_ref, o_ref):
        j = pl.program_id(1)

        def accumulate(a_ref):
            r = lax.dot_general(a_ref[...], b_ref[...], NT, preferred_element_type=F32)

            @pl.when(j == 0)
            def _():
                o_ref[...] = r

            @pl.when(j > 0)
            def _():
                o_ref[...] += r

        pl.when(j < N_DEV // 2)(lambda: accumulate(lo_ref))
        pl.when(j >= N_DEV // 2)(lambda: accumulate(hi_ref))

    def first_last():
        i, j = pl.program_id(0), pl.program_id(1)
        return (i == 0) & (j == 0), (i == m // tm - 1) & (j == N_DEV - 1)

    lo_spec, hi_spec = _halves_specs((tm, nb), lambda i, j: (i, j))
    (out,), extra = _host_call(
        body, 3, 1, fused, first_last, name=name, grid=(m // tm, N_DEV),
        in_specs=[lo_spec, hi_spec, pl.BlockSpec((None, n, nb), lambda i, j: (j, 0, 0))],
        out_specs=[pl.BlockSpec((tm, n), lambda i, j: (i, 0))], out_shape=[jax.ShapeDtypeStruct((m, n), F32)],
        scratch_shapes=[], sem=("parallel", "arbitrary"), operands=[a_lo, a_hi, w_g] + list(fused_arrays))
    return out if fused is None else (out, extra)


def _mm_halves_wgrad(name, act, d_lo, d_hi, tk=MM_TILE):
    t, k = act.shape
    nb = d_lo.shape[1] // (N_DEV // 2)

    def body(a_ref, lo_ref, hi_ref, o_ref):
        j = pl.program_id(0)

        def product(d_ref):
            o_ref[...] = lax.dot_general(a_ref[...], d_ref[...], TN, preferred_element_type=F32).astype(o_ref.dtype)

        pl.when(j < N_DEV // 2)(lambda: product(lo_ref))
        pl.when(j >= N_DEV // 2)(lambda: product(hi_ref))

    half = N_DEV // 2
    return pl.pallas_call(
        body, name=name, grid=(N_DEV, k // tk),
        in_specs=[pl.BlockSpec((t, tk), lambda j, i: (0, i)),
                  pl.BlockSpec((t, nb), lambda j, i: (0, jnp.minimum(j, half - 1))),
                  pl.BlockSpec((t, nb), lambda j, i: (0, jnp.maximum(j - half, 0)))],
        out_specs=pl.BlockSpec((None, tk, nb), lambda j, i: (j, i, 0)),
        out_shape=jax.ShapeDtypeStruct((N_DEV, k, nb), BF16),
        compiler_params=_params(("parallel", "parallel")))(act, d_lo, d_hi)


def _mm_plain(name, a, b, dims, tm, tn, out_dtype, fused=None, fused_arrays=()):
    if dims == NN:
        (m, k), n = a.shape, b.shape[1]
        a_spec = pl.BlockSpec((tm, k), lambda i, j: (i, 0))
        b_spec = pl.BlockSpec((k, tn), lambda i, j: (0, j))
    elif dims == NT:
        (m, k), n = a.shape, b.shape[0]
        a_spec = pl.BlockSpec((tm, k), lambda i, j: (i, 0))
        b_spec = pl.BlockSpec((tn, k), lambda i, j: (j, 0))
    else:
        (k, m), n = a.shape, b.shape[1]
        a_spec = pl.BlockSpec((k, tm), lambda i, j: (0, i))
        b_spec = pl.BlockSpec((k, tn), lambda i, j: (0, j))
    return _matmul(name, a, b, dims, (m // tm, n // tn), a_spec, b_spec,
                   pl.BlockSpec((tm, tn), lambda i, j: (i, j)), jax.ShapeDtypeStruct((m, n), out_dtype),
                   fused=fused, fused_arrays=fused_arrays)


def _ada_fwd(c_all, w_ada_blk, b_blk):
    def body(c_ref, w_ref, b_ref, o_ref):
        cv = c_ref[...]
        o_ref[...] = jnp.dot(cv * _sigmoid(cv), w_ref[...], preferred_element_type=F32) + b_ref[...]

    tn = 512
    return pl.pallas_call(
        body, name="ada_fwd", grid=(ADA_BLK // tn,),
        in_specs=[pl.BlockSpec((N_DEV, D_MODEL), lambda j: (0, 0)),
                  pl.BlockSpec((D_MODEL, tn), lambda j: (0, j)),
                  pl.BlockSpec((1, tn), lambda j: (0, j))],
        out_specs=pl.BlockSpec((N_DEV, tn), lambda j: (0, j)),
        out_shape=jax.ShapeDtypeStruct((N_DEV, ADA_BLK), F32),
        compiler_params=_params(("parallel",)))(c_all, w_ada_blk, b_blk)


def _ada_wgrad(c_all, gmod_cols):
    def body(c_ref, g_ref, o_ref):
        cv = c_ref[...]
        o_ref[...] = lax.dot_general(cv * _sigmoid(cv), g_ref[...], TN, preferred_element_type=F32)

    tk = 512
    return pl.pallas_call(
        body, name="ada_wgrad", grid=(D_MODEL // tk,),
        in_specs=[pl.BlockSpec((N_DEV, tk), lambda i: (0, i)),
                  pl.BlockSpec((N_DEV, ADA_BLK), lambda i: (0, 0))],
        out_specs=pl.BlockSpec((tk, ADA_BLK), lambda i: (i, 0)),
        out_shape=jax.ShapeDtypeStruct((D_MODEL, ADA_BLK), F32),
        compiler_params=_params(("parallel",)))(c_all, gmod_cols)


def _row_spec(cols=D_MODEL):
    return pl.BlockSpec((ROW_TILE, cols), lambda i: (i, 0))


def _vec_spec(cols=D_MODEL):
    return pl.BlockSpec((1, cols), lambda i: (0, 0))


def _norm_fwd(name, x, w, scale, shift, resid=None, gate=None):
    has_res = resid is not None

    def body(*refs):
        if has_res:
            x_ref, r_ref, g_ref, w_ref, sc_ref, sh_ref, xr_ref, h_ref, rs_ref = refs
            xr = x_ref[...] + g_ref[...] * r_ref[...]
            xr_ref[...] = xr
        else:
            x_ref, w_ref, sc_ref, sh_ref, h_ref, rs_ref = refs
            xr = x_ref[...]
        rs = lax.rsqrt(jnp.mean(xr * xr, axis=-1, keepdims=True) + EPS)
        h = (xr * rs) * w_ref[...] * (1.0 + sc_ref[...]) + sh_ref[...]
        h_ref[...] = h.astype(BF16)
        rs_ref[...] = rs

    s = x.shape[0]
    ins = [x] + ([resid, gate] if has_res else []) + [w, scale, shift]
    in_specs = [_row_spec()] + ([_row_spec(), _vec_spec()] if has_res else []) + [_vec_spec()] * 3
    outs = ([jax.ShapeDtypeStruct((s, D_MODEL), F32)] if has_res else []) + [
        jax.ShapeDtypeStruct((s, D_MODEL), BF16), jax.ShapeDtypeStruct((s, 1), F32)]
    out_specs = ([_row_spec()] if has_res else []) + [_row_spec(), pl.BlockSpec((ROW_TILE, 1), lambda i: (i, 0))]
    return pl.pallas_call(body, name=name, grid=(s // ROW_TILE,), in_specs=in_specs, out_specs=out_specs,
                          out_shape=outs, compiler_params=_params(("parallel",)))(*ins)


def _norm_bwd(name, dh, x, rstd, w, scale, dres, mix=None, gate=None, fused=None, fused_arrays=()):
    has_mix = mix is not None

    def body(*refs):
        if has_mix:
            (dh_ref, x_ref, rs_ref, w_ref, sc_ref, dr_ref, mix_ref, g_ref,
             dx_ref, dmix_ref, dsh_ref, dsc_ref, dw_ref, dg_ref) = refs
        else:
            dh_ref, x_ref, rs_ref, w_ref, sc_ref, dr_ref, dx_ref, dsh_ref, dsc_ref, dw_ref = refs
        i = pl.program_id(0)
        dhv = dh_ref[...]
        rs = rs_ref[...]
        xn = x_ref[...] * rs
        wv = w_ref[...]
        one_sc = 1.0 + sc_ref[...]
        dxn = dhv * wv * one_sc
        dx = dr_ref[...] + rs * (dxn - xn * jnp.mean(dxn * xn, axis=-1, keepdims=True))
        dx_ref[...] = dx
        sums = [(dsh_ref, dhv), (dsc_ref, dhv * xn * wv), (dw_ref, dhv * one_sc * xn)]
        if has_mix:
            dmix_ref[...] = (dx * g_ref[...]).astype(BF16)
            sums.append((dg_ref, dx * mix_ref[...]))

        @pl.when(i == 0)
        def _():
            for ref, _v in sums:
                ref[...] = jnp.zeros_like(ref)

        for ref, v in sums:
            ref[...] += jnp.sum(v, axis=0, keepdims=True)

    s = x.shape[0]
    ins = [dh, x, rstd, w, scale, dres] + ([mix, gate] if has_mix else [])
    in_specs = ([_row_spec(), _row_spec(), pl.BlockSpec((ROW_TILE, 1), lambda i: (i, 0)), _vec_spec(), _vec_spec(),
                 _row_spec()] + ([_row_spec(), _vec_spec()] if has_mix else []))
    vec = jax.ShapeDtypeStruct((1, D_MODEL), F32)
    outs = ([jax.ShapeDtypeStruct((s, D_MODEL), F32)] + ([jax.ShapeDtypeStruct((s, D_MODEL), BF16)] if has_mix else [])
            + [vec] * (4 if has_mix else 3))
    out_specs = [_row_spec()] + ([_row_spec()] if has_mix else []) + [_vec_spec()] * (4 if has_mix else 3)

    def first_last():
        i = pl.program_id(0)
        return i == 0, i == s // ROW_TILE - 1

    res, extra = _host_call(body, len(ins), len(outs), fused, first_last, name=name, grid=(s // ROW_TILE,),
                            in_specs=in_specs, out_specs=out_specs, out_shape=outs, scratch_shapes=[],
                            sem=("arbitrary",), operands=ins + list(fused_arrays))
    return res if fused is None else (res, extra)


def _loss_head(x1, ffn, gate2, target):
    def body(x_ref, f_ref, g_ref, t_ref, loss_ref, dout_ref, dffn_ref, dg_ref):
        i = pl.program_id(0)
        fv = f_ref[...]
        gv = g_ref[...]
        err = x_ref[...] + gv * fv - t_ref[...]
        dout = err * (1.0 / D_MODEL)
        dout_ref[...] = dout
        dffn_ref[...] = (dout * gv).astype(BF16)

        @pl.when(i == 0)
        def _():
            loss_ref[...] = jnp.zeros_like(loss_ref)
            dg_ref[...] = jnp.zeros_like(dg_ref)

        row = jnp.sum(err * err, axis=-1, keepdims=True) * (1.0 / D_MODEL)
        loss_ref[...] += jnp.broadcast_to(0.5 * jnp.sum(row, axis=0, keepdims=True), (1, 128))
        dg_ref[...] += jnp.sum(dout * fv, axis=0, keepdims=True)

    s = x1.shape[0]
    return pl.pallas_call(
        body, name="loss_head", grid=(s // ROW_TILE,),
        in_specs=[_row_spec(), _row_spec(), _vec_spec(), _row_spec()],
        out_specs=[pl.BlockSpec((1, 128), lambda i: (0, 0)), _row_spec(), _row_spec(), _vec_spec()],
        out_shape=[jax.ShapeDtypeStruct((1, 128), F32), jax.ShapeDtypeStruct((s, D_MODEL), F32),
                   jax.ShapeDtypeStruct((s, D_MODEL), BF16), jax.ShapeDtypeStruct((1, D_MODEL), F32)],
        compiler_params=_params(("arbitrary",)))(x1, ffn, gate2, target)


CONV_TILE = 512
N_CONV_TILES = D_FF // CONV_TILE


def _shift_rows(a, k, row):
    n = a.shape[0]
    if k > 0:
        return jnp.where(row >= k, pltpu.roll(a, k, 0), 0.0)
    return jnp.where(row < n + k, pltpu.roll(a, n + k, 0), 0.0)


def _conv_gate_fwd(u, conv_w, conv_b, fused=None, fused_arrays=()):
    s = u.shape[0]

    def body(a_ref, g_ref, w_ref, b_ref, y_ref):
        a = a_ref[...]
        w = w_ref[...]
        row = lax.broadcasted_iota(jnp.int32, a.shape, 0)
        ac = b_ref[...] + _shift_rows(a, 2, row) * w[0:1] + _shift_rows(a, 1, row) * w[1:2] + a * w[2:3]
        y_ref[...] = (ac * _sigmoid(ac) * g_ref[...]).astype(BF16)

    def first_last():
        i = pl.program_id(0)
        return i == 0, i == N_CONV_TILES - 1

    col = lambda off: pl.BlockSpec((s, CONV_TILE), lambda i: (0, i + off))
    (y,), extra = _host_call(
        body, 4, 1, fused, first_last, name="conv_gate_fwd", grid=(N_CONV_TILES,),
        in_specs=[col(0), col(N_CONV_TILES), pl.BlockSpec((3, CONV_TILE), lambda i: (0, i)),
                  pl.BlockSpec((1, CONV_TILE), lambda i: (0, i))],
        out_specs=[col(0)], out_shape=[jax.ShapeDtypeStruct((s, D_FF), BF16)], scratch_shapes=[], sem=("parallel",),
        operands=[u, u, conv_w, conv_b] + list(fused_arrays))
    return y if fused is None else (y, extra)


def _conv_gate_bwd(u, dy, conv_w, conv_b):
    s = u.shape[0]

    def body(a_ref, g_ref, dy_ref, w_ref, b_ref, da_ref, dg_ref, gw_ref, gb_ref):
        a = a_ref[...]
        w = w_ref[...]
        row = lax.broadcasted_iota(jnp.int32, a.shape, 0)
        a1 = _shift_rows(a, 1, row)
        a2 = _shift_rows(a, 2, row)
        ac = b_ref[...] + a2 * w[0:1] + a1 * w[1:2] + a * w[2:3]
        sg = _sigmoid(ac)
        dyv = dy_ref[...].astype(F32)
        dg_ref[...] = (dyv * (ac * sg)).astype(BF16)
        dac = dyv * g_ref[...] * _dsilu(ac, sg)
        gb_ref[...] = jnp.sum(dac, axis=0, keepdims=True)
        gw_ref[0:1, :] = jnp.sum(dac * a2, axis=0, keepdims=True)
        gw_ref[1:2, :] = jnp.sum(dac * a1, axis=0, keepdims=True)
        gw_ref[2:3, :] = jnp.sum(dac * a, axis=0, keepdims=True)
        da = dac * w[2:3] + _shift_rows(dac, -1, row) * w[1:2] + _shift_rows(dac, -2, row) * w[0:1]
        da_ref[...] = da.astype(BF16)

    col = lambda off: pl.BlockSpec((s, CONV_TILE), lambda i: (0, i + off))
    return pl.pallas_call(
        body, name="conv_gate_bwd", grid=(N_CONV_TILES,),
        in_specs=[col(0), col(N_CONV_TILES), col(0), pl.BlockSpec((3, CONV_TILE), lambda i: (0, i)),
                  pl.BlockSpec((1, CONV_TILE), lambda i: (0, i))],
        out_specs=[col(0), col(0), pl.BlockSpec((3, CONV_TILE), lambda i: (0, i)),
                   pl.BlockSpec((1, CONV_TILE), lambda i: (0, i))],
        out_shape=[jax.ShapeDtypeStruct((s, D_FF), BF16), jax.ShapeDtypeStruct((s, D_FF), BF16),
                   jax.ShapeDtypeStruct((3, D_FF), F32), jax.ShapeDtypeStruct((1, D_FF), F32)],
        compiler_params=_params(("parallel",)))(u, u, dy, conv_w, conv_b)


HG_TILE = 256
CHUNK_UNROLL = 8


def _unrolled_loop(n, body, init):
    def group(i, carry):
        for u in range(CHUNK_UNROLL):
            carry = body(i * CHUNK_UNROLL + u, carry)
        return carry

    return lax.fori_loop(0, n // CHUNK_UNROLL, group, init)


def _head_col(off):
    return pl.BlockSpec((SEQ, HEAD_DIM), lambda h: (0, h + off))


def _hgrn_gates(hq, hf, lb, pos):
    q = hq * _sigmoid(hq)
    sig = _sigmoid(hf)
    f = lb + (1.0 - lb) * sig
    gl = jnp.log(f)
    for sh in (1, 2, 4, 8):
        gl = gl + jnp.where(pos >= sh, pltpu.roll(gl, sh, 0), 0.0)
    return q, sig, f, 1.0 - f, gl


def _lower_bound(lbl):
    return 1.0 / (1.0 + jnp.exp(lbl[1:2, :] - lbl[0:1, :]))


def _head_first_last():
    h = pl.program_id(0)
    return h == 0, h == HEADS - 1


CHUNKS_PER_TILE = HG_TILE // CHUNK


def _chunk_end(x, pos):
    y = jnp.where(pos == CHUNK - 1, x, 0.0)
    for sh in (1, 2, 4, 8):
        y = y + jnp.where(pos < CHUNK - sh, pltpu.roll(y, x.shape[0] - sh, 0), 0.0)
    return y


def _suffix_in_chunk(x, pos):
    for sh in (1, 2, 4, 8):
        x = x + jnp.where(pos < CHUNK - sh, pltpu.roll(x, x.shape[0] - sh, 0), 0.0)
    return x


def _prefix_in_chunk(x, pos):
    for sh in (1, 2, 4, 8):
        x = x + jnp.where(pos >= sh, pltpu.roll(x, sh, 0), 0.0)
    return x


def _pair_decays(f, pos):
    shifted = jnp.where(pos >= 1, f, 0.0)
    e = shifted
    yield 1, e
    for d in range(2, CHUNK):
        shifted = pltpu.roll(shifted, 1, 0)
        e = e * shifted
        yield d, e


def _chunk_rows(cc):
    return slice(cc * CHUNK, (cc + 1) * CHUNK)


def _outer_products(lhs_b, rhs_b, dst, i):
    for cc in range(CHUNKS_PER_TILE):
        dst[i * CHUNKS_PER_TILE + cc] = lax.dot_general(lhs_b[_chunk_rows(cc)], rhs_b[_chunk_rows(cc)], TN,
                                                        preferred_element_type=F32)


def _state_scan(n_chunks, gl_s, u_s, keep, reverse):
    def step(k, st):
        c = n_chunks - 1 - k if reverse else k
        keep[c] = st.astype(BF16)
        gl = gl_s[pl.ds(pl.multiple_of(c * CHUNK, CHUNK), CHUNK), :]
        return st * jnp.exp(gl[CHUNK - 1:CHUNK, :]) + u_s[c]

    _unrolled_loop(n_chunks, step, jnp.zeros((HEAD_DIM, HEAD_DIM), F32))


def _hgrn_fwd(proj, lb_logits, norm_w, fused=None, fused_arrays=()):
    n_tiles = SEQ // HG_TILE
    n_chunks = SEQ // CHUNK
    fused_arrays = list(fused_arrays)

    def body(hq_ref, hf_ref, hi_ref, hg_ref, lbl_ref, nw_ref, aout_ref, opre_ref, qt_s, gl_s, u_s, st_s):
        lb = _lower_bound(lbl_ref[...])
        ones = jnp.ones((HEAD_DIM, HEAD_DIM), BF16)
        pos = lax.broadcasted_iota(jnp.int32, (HG_TILE, HEAD_DIM), 0) % CHUNK

        def tile(i, carry):
            rows = pl.ds(pl.multiple_of(i * HG_TILE, HG_TILE), HG_TILE)
            v = hi_ref[rows, :]
            q, _sig, f, kk, gl = _hgrn_gates(hq_ref[rows, :], hf_ref[rows, :], lb, pos)
            o = _lane_sum(q * kk, ones) * v
            for d, e in _pair_decays(f, pos):
                o = o + _lane_sum(q * pltpu.roll(kk, d, 0) * e, ones) * pltpu.roll(v, d, 0)
            opre_ref[rows, :] = o
            qt_s[rows, :] = q * jnp.exp(gl)
            gl_s[rows, :] = gl
            kt = kk * jnp.exp(_chunk_end(gl, pos) - gl)
            _outer_products(v.astype(BF16), kt.astype(BF16), u_s, i)
            return carry

        lax.fori_loop(0, n_tiles, tile, 0)
        _state_scan(n_chunks, gl_s, u_s, st_s, reverse=False)

        def finish(i, carry):
            rows = pl.ds(pl.multiple_of(i * HG_TILE, HG_TILE), HG_TILE)
            qt_b = qt_s[rows, :].astype(BF16)
            past = [lax.dot_general(qt_b[_chunk_rows(cc)], st_s[i * CHUNKS_PER_TILE + cc], NT,
                                    preferred_element_type=F32) for cc in range(CHUNKS_PER_TILE)]
            o = opre_ref[rows, :] + jnp.concatenate(past, axis=0)
            opre_ref[rows, :] = o
            hg = hg_ref[rows, :]
            rs = lax.rsqrt(jnp.mean(o * o, axis=-1, keepdims=True) + EPS)
            aout_ref[rows, :] = ((o * rs) * nw_ref[...] * (hg * _sigmoid(hg))).astype(BF16)
            return carry

        lax.fori_loop(0, n_tiles, finish, 0)

    return _host_call(
        body, 6, 2, fused, _head_first_last, name="hgrn_fwd", grid=(HEADS,),
        in_specs=[_head_col(0), _head_col(HEADS), _head_col(2 * HEADS), _head_col(3 * HEADS),
                  pl.BlockSpec((2, HEAD_DIM), lambda h: (0, h)), pl.BlockSpec((1, HEAD_DIM), lambda h: (0, 0))],
        out_specs=[_head_col(0), _head_col(0)],
        out_shape=[jax.ShapeDtypeStruct((SEQ, HEADS * HEAD_DIM), BF16), jax.ShapeDtypeStruct((SEQ, HEADS * HEAD_DIM), F32)],
        scratch_shapes=[pltpu.VMEM((SEQ, HEAD_DIM), F32)] * 2 + [pltpu.VMEM((n_chunks, HEAD_DIM, HEAD_DIM), F32),
                                                                 pltpu.VMEM((n_chunks, HEAD_DIM, HEAD_DIM), BF16)],
        sem=("parallel",), operands=[proj, proj, proj, proj, lb_logits, norm_w] + fused_arrays)


def _hgrn_bwd(proj, lb_logits, norm_w, o_pre, d_aout, fused=None, fused_arrays=()):
    n_tiles = SEQ // HG_TILE
    n_chunks = SEQ // CHUNK

    def body(hq_ref, hf_ref, hi_ref, hg_ref, lbl_ref, nw_ref, opre_ref, da_ref,
             dhq_ref, dhf_ref, dhi_ref, dhg_ref, dlog_ref, gnw_ref,
             q_s, k_s, gl_s, do_s, dq_s, dk_s, dv_s, u_s, st_s, rt_s):
        h = pl.program_id(0)
        lb = _lower_bound(lbl_ref[...])
        nw = nw_ref[...]
        ones = jnp.ones((HEAD_DIM, HEAD_DIM), BF16)
        pos = lax.broadcasted_iota(jnp.int32, (HG_TILE, HEAD_DIM), 0) % CHUNK

        @pl.when(h == 0)
        def _():
            gnw_ref[...] = jnp.zeros_like(gnw_ref)

        def tile(i, carry):
            rows = pl.ds(pl.multiple_of(i * HG_TILE, HG_TILE), HG_TILE)
            v = hi_ref[rows, :]
            q, _sig, f, kk, gl = _hgrn_gates(hq_ref[rows, :], hf_ref[rows, :], lb, pos)
            o = opre_ref[rows, :]
            hg = hg_ref[rows, :]
            da = da_ref[rows, :]
            rs = lax.rsqrt(jnp.mean(o * o, axis=-1, keepdims=True) + EPS)
            oh = o * rs
            sg = _sigmoid(hg)
            dnorm = da * (hg * sg)
            dhg_ref[rows, :] = (da * (oh * nw) * _dsilu(hg, sg)).astype(BF16)
            gnw_ref[...] += jnp.sum(dnorm * oh, axis=0, keepdims=True)
            doh = dnorm * nw
            do = rs * (doh - oh * jnp.mean(doh * oh, axis=-1, keepdims=True))

            d_a = _lane_sum(do * v, ones)
            dq = d_a * kk
            dk = d_a * q
            dv = _lane_sum(q * kk, ones) * do
            for d, e in _pair_decays(f, pos):
                ks = pltpu.roll(kk, d, 0)
                a_d = _lane_sum(q * ks * e, ones)
                d_a = _lane_sum(do * pltpu.roll(v, d, 0), ones) * e
                dq = dq + d_a * ks
                dk = dk + pltpu.roll(d_a * q, HG_TILE - d, 0)
                dv = dv + pltpu.roll(a_d * do, HG_TILE - d, 0)
            q_s[rows, :] = q
            k_s[rows, :] = kk
            gl_s[rows, :] = gl
            do_s[rows, :] = do
            dq_s[rows, :] = dq
            dk_s[rows, :] = dk
            dv_s[rows, :] = dv
            kt = kk * jnp.exp(_chunk_end(gl, pos) - gl)
            _outer_products(v.astype(BF16), kt.astype(BF16), u_s, i)
            return carry

        lax.fori_loop(0, n_tiles, tile, 0)
        _state_scan(n_chunks, gl_s, u_s, st_s, reverse=False)

        def reverse_increments(i, carry):
            rows = pl.ds(pl.multiple_of(i * HG_TILE, HG_TILE), HG_TILE)
            qt = q_s[rows, :] * jnp.exp(gl_s[rows, :])
            _outer_products(do_s[rows, :].astype(BF16), qt.astype(BF16), u_s, i)
            return carry

        lax.fori_loop(0, n_tiles, reverse_increments, 0)
        _state_scan(n_chunks, gl_s, u_s, rt_s, reverse=True)

        def finish(i, dlb):
            rows = pl.ds(pl.multiple_of(i * HG_TILE, HG_TILE), HG_TILE)
            q = q_s[rows, :]
            kk = k_s[rows, :]
            gl = gl_s[rows, :]
            gll = _chunk_end(gl, pos)
            ekt = jnp.exp(gll - gl)
            do_b = do_s[rows, :].astype(BF16)
            v_b = hi_ref[rows, :].astype(BF16)
            kt_b = (kk * ekt).astype(BF16)
            dq_far, dk_far, dv_far, across = [], [], [], []
            for cc in range(CHUNKS_PER_TILE):
                st = st_s[i * CHUNKS_PER_TILE + cc]
                rt = rt_s[i * CHUNKS_PER_TILE + cc]
                sl = _chunk_rows(cc)
                dq_far.append(jnp.dot(do_b[sl], st, preferred_element_type=F32))
                dk_far.append(jnp.dot(v_b[sl], rt, preferred_element_type=F32))
                dv_far.append(lax.dot_general(kt_b[sl], rt, NT, preferred_element_type=F32))
                both = jnp.sum(st.astype(F32) * rt.astype(F32), axis=0, keepdims=True)
                across.append(jnp.broadcast_to(both, (CHUNK, HEAD_DIM)))
            dq = dq_s[rows, :] + jnp.concatenate(dq_far, axis=0) * jnp.exp(gl)
            dk_in = dk_s[rows, :]
            dk_out = jnp.concatenate(dk_far, axis=0) * ekt
            dk = dk_in + dk_out
            dv = dv_s[rows, :] + jnp.concatenate(dv_far, axis=0)
            pc = kk * dk_out
            dgl = (_suffix_in_chunk(q * dq - kk * dk_in, pos) + (_prefix_in_chunk(pc, pos) - pc)
                   + jnp.concatenate(across, axis=0) * jnp.exp(gll))
            hf = hf_ref[rows, :]
            sig = _sigmoid(hf)
            f = lb + (1.0 - lb) * sig
            df = dgl / f - dk
            dhf_ref[rows, :] = (df * (1.0 - lb) * sig * (1.0 - sig)).astype(BF16)
            hq = hq_ref[rows, :]
            dhq_ref[rows, :] = (dq * _dsilu(hq, _sigmoid(hq))).astype(BF16)
            dhi_ref[rows, :] = dv.astype(BF16)
            return dlb + jnp.sum(df * (1.0 - sig), axis=0, keepdims=True)

        dlb = lax.fori_loop(0, n_tiles, finish, jnp.zeros((1, HEAD_DIM), F32))
        dl0 = lb * (1.0 - lb) * dlb
        dlog_ref[0:1, :] = dl0
        dlog_ref[1:2, :] = -dl0

    wide = HEADS * HEAD_DIM
    return _host_call(
        body, 8, 6, fused, _head_first_last, name="hgrn_bwd", grid=(HEADS,),
        in_specs=[_head_col(0), _head_col(HEADS), _head_col(2 * HEADS), _head_col(3 * HEADS),
                  pl.BlockSpec((2, HEAD_DIM), lambda h: (0, h)), pl.BlockSpec((1, HEAD_DIM), lambda h: (0, 0)),
                  _head_col(0), _head_col(0)],
        out_specs=[_head_col(0)] * 4 + [pl.BlockSpec((2, HEAD_DIM), lambda h: (0, h)),
                                        pl.BlockSpec((1, HEAD_DIM), lambda h: (0, 0))],
        out_shape=[jax.ShapeDtypeStruct((SEQ, wide), BF16)] * 4 + [jax.ShapeDtypeStruct((2, wide), F32),
                                                                    jax.ShapeDtypeStruct((1, HEAD_DIM), F32)],
        scratch_shapes=[pltpu.VMEM((SEQ, HEAD_DIM), F32)] * 7 + [pltpu.VMEM((n_chunks, HEAD_DIM, HEAD_DIM), F32),
                                                                 pltpu.VMEM((n_chunks, HEAD_DIM, HEAD_DIM), BF16),
                                                                 pltpu.VMEM((n_chunks, HEAD_DIM, HEAD_DIM), BF16)],
        sem=("arbitrary",),
        operands=[proj, proj, proj, proj, lb_logits, norm_w, o_pre, d_aout] + list(fused_arrays))


Q_TILE = 512
ATT_SCALE = HEAD_DIM ** -0.5
ATT_OFF = 4 * HEADS


def _qk_prep(proj, q_w, k_w, fused=None, fused_arrays=()):
    def body(aq_ref, ak_ref, av_ref, qw_ref, kw_ref, qn_ref, kn_ref, v_ref):
        aq = aq_ref[...]
        ak = ak_ref[...]
        qn_ref[...] = (aq * lax.rsqrt(jnp.mean(aq * aq, axis=-1, keepdims=True) + EPS) * qw_ref[...]).astype(BF16)
        kn_ref[...] = (ak * lax.rsqrt(jnp.mean(ak * ak, axis=-1, keepdims=True) + EPS) * kw_ref[...]).astype(BF16)
        v_ref[...] = av_ref[...].astype(BF16)

    wide = HEADS * HEAD_DIM
    vec = pl.BlockSpec((1, HEAD_DIM), lambda h: (0, 0))
    return _host_call(
        body, 5, 3, fused, _head_first_last, name="qk_prep", grid=(HEADS,),
        in_specs=[_head_col(ATT_OFF), _head_col(ATT_OFF + HEADS), _head_col(ATT_OFF + 2 * HEADS), vec, vec],
        out_specs=[_head_col(0)] * 3, out_shape=[jax.ShapeDtypeStruct((SEQ, wide), BF16)] * 3,
        scratch_shapes=[], sem=("parallel",), operands=[proj, proj, proj, q_w, k_w] + list(fused_arrays))


def _alibi_slopes():
    slopes = np.exp2(-8.0 * np.arange(1, HEADS + 1, dtype=np.float32) / HEADS).astype(np.float32)
    return np.broadcast_to(slopes[:, None, None], (HEADS, 1, HEAD_DIM))


SLOPE_SPEC = pl.BlockSpec((None, 1, HEAD_DIM), lambda h, i: (h, 0, 0))


N_Q_TILES = SEQ // Q_TILE
K_BLOCK = 512
NOT_ATTENDED = 1e35


def _att_tables():
    o = np.arange(N_Q_TILES, dtype=np.int32)[:, None, None]
    r = np.arange(Q_TILE, dtype=np.int32)[None, :, None]
    c = np.arange(K_BLOCK, dtype=np.int32)[None, None, :]
    dist = o * Q_TILE + r - c
    mult = ((dist <= 128).astype(np.float32) + (((dist % 4) == 0) & (dist <= 512)).astype(np.float32)
            + ((dist % 16) == 0).astype(np.float32))
    valid = (dist >= 0) & (mult > 0)
    return (np.where(valid, dist.astype(np.float32), np.float32(NOT_ATTENDED)).astype(np.float32),
            np.where(valid, np.log(np.maximum(mult, 1.0)), 0.0).astype(np.float32))


TABLE_SPEC = pl.BlockSpec((N_Q_TILES, Q_TILE, K_BLOCK), lambda h, i: (0, 0, 0))


def _att_block(q, k_ref, j, i, slope, dist_ref, lmul_ref):
    rows = pl.ds(pl.multiple_of(j * K_BLOCK, K_BLOCK), K_BLOCK)
    off = i - j * (K_BLOCK // Q_TILE)
    s = lax.dot_general(q, k_ref[rows, :], NT, preferred_element_type=F32) * ATT_SCALE
    return s - slope * dist_ref[off] + lmul_ref[off], rows


def _n_key_blocks(i):
    return (i + K_BLOCK // Q_TILE) // (K_BLOCK // Q_TILE)


def _att_first_last():
    h, i = pl.program_id(0), pl.program_id(1)
    return (h == 0) & (i == 0), (h == HEADS - 1) & (i == N_Q_TILES - 1)


def _attn_fwd(qn, kn, vb, fused=None, fused_arrays=()):
    def body(q_ref, k_ref, v_ref, sl_ref, dist_ref, lmul_ref, o_ref, lse_ref):
        i = pl.program_id(1)
        q = q_ref[...]
        slope = sl_ref[0:1, 0:1]

        def step(j, carry):
            m, l, acc = carry
            sb, rows = _att_block(q, k_ref, j, i, slope, dist_ref, lmul_ref)
            m_new = jnp.maximum(m, jnp.max(sb, axis=-1, keepdims=True))
            alpha = jnp.exp(m - m_new)
            p = jnp.exp(sb - m_new)
            l = alpha * l + jnp.sum(p, axis=-1, keepdims=True)
            acc = alpha * acc + jnp.dot(p.astype(BF16), v_ref[rows, :], preferred_element_type=F32)
            return m_new, l, acc

        m, l, acc = lax.fori_loop(0, _n_key_blocks(i), step,
                                  (jnp.full((Q_TILE, 1), -1e30, F32), jnp.zeros((Q_TILE, 1), F32),
                                   jnp.zeros((Q_TILE, HEAD_DIM), F32)))
        o_ref[...] = acc / l
        lse_ref[...] = m + jnp.log(l)

    wide = HEADS * HEAD_DIM
    qt = pl.BlockSpec((Q_TILE, HEAD_DIM), lambda h, i: (i, h))
    full = pl.BlockSpec((SEQ, HEAD_DIM), lambda h, i: (0, h))
    return _host_call(
        body, 6, 2, fused, _att_first_last, name="attn_fwd", grid=(HEADS, N_Q_TILES),
        in_specs=[qt, full, full, SLOPE_SPEC, TABLE_SPEC, TABLE_SPEC],
        out_specs=[qt, pl.BlockSpec((None, Q_TILE, 1), lambda h, i: (h, i, 0))],
        out_shape=[jax.ShapeDtypeStruct((SEQ, wide), F32), jax.ShapeDtypeStruct((HEADS, SEQ, 1), F32)],
        scratch_shapes=[], sem=("parallel", "parallel"),
        operands=[qn, kn, vb, _alibi_slopes(), *_att_tables()] + list(fused_arrays))


def _attn_bwd(qn, kn, vb, o, lse, d_mix, fused=None, fused_arrays=()):
    def body(q_ref, k_ref, v_ref, o_ref, lse_ref, do_ref, sl_ref, dist_ref, lmul_ref, dq_ref, dk_ref, dv_ref):
        i = pl.program_id(1)
        q = q_ref[...]
        do = do_ref[...]
        do_b = do.astype(BF16)
        slope = sl_ref[0:1, 0:1]
        lse = lse_ref[...]
        delta = jnp.sum(do * o_ref[...], axis=-1, keepdims=True)

        @pl.when(i == 0)
        def _():
            dk_ref[...] = jnp.zeros_like(dk_ref)
            dv_ref[...] = jnp.zeros_like(dv_ref)

        def step(j, dq):
            sb, rows = _att_block(q, k_ref, j, i, slope, dist_ref, lmul_ref)
            p = jnp.exp(sb - lse)
            dp = lax.dot_general(do_b, v_ref[rows, :], NT, preferred_element_type=F32)
            ds = (p * (dp - delta)).astype(BF16)
            dk_ref[rows, :] += lax.dot_general(ds, q, TN, preferred_element_type=F32) * ATT_SCALE
            dv_ref[rows, :] += lax.dot_general(p.astype(BF16), do_b, TN, preferred_element_type=F32)
            return dq + jnp.dot(ds, k_ref[rows, :], preferred_element_type=F32)

        dq = lax.fori_loop(0, _n_key_blocks(i), step, jnp.zeros((Q_TILE, HEAD_DIM), F32))
        dq_ref[...] = dq * ATT_SCALE

    wide = HEADS * HEAD_DIM
    qt = pl.BlockSpec((Q_TILE, HEAD_DIM), lambda h, i: (i, h))
    full = pl.BlockSpec((SEQ, HEAD_DIM), lambda h, i: (0, h))
    return _host_call(
        body, 9, 3, fused, _att_first_last, name="attn_bwd", grid=(HEADS, N_Q_TILES),
        in_specs=[qt, full, full, qt, pl.BlockSpec((None, Q_TILE, 1), lambda h, i: (h, i, 0)),
                  pl.BlockSpec((Q_TILE, HEAD_DIM), lambda h, i: (i, h + HEADS)), SLOPE_SPEC, TABLE_SPEC, TABLE_SPEC],
        out_specs=[qt, full, full], out_shape=[jax.ShapeDtypeStruct((SEQ, wide), F32)] * 3,
        scratch_shapes=[], sem=("parallel", "arbitrary"),
        operands=[qn, kn, vb, o, lse, d_mix, _alibi_slopes(), *_att_tables()] + list(fused_arrays))


def _qk_bwd(proj, q_w, k_w, dqn, dkn, dv, fused=None, fused_arrays=()):
    def body(aq_ref, ak_ref, qw_ref, kw_ref, dqn_ref, dkn_ref, dv_ref, daq_ref, dak_ref, dav_ref, gq_ref, gk_ref):
        h = pl.program_id(0)

        @pl.when(h == 0)
        def _():
            gq_ref[...] = jnp.zeros_like(gq_ref)
            gk_ref[...] = jnp.zeros_like(gk_ref)

        def one(a_ref, w_ref, d_ref, da_ref, g_ref):
            a = a_ref[...]
            d = d_ref[...]
            rs = lax.rsqrt(jnp.mean(a * a, axis=-1, keepdims=True) + EPS)
            ah = a * rs
            g_ref[...] += jnp.sum(d * ah, axis=0, keepdims=True)
            dah = d * w_ref[...]
            da_ref[...] = (rs * (dah - ah * jnp.mean(dah * ah, axis=-1, keepdims=True))).astype(BF16)

        one(aq_ref, qw_ref, dqn_ref, daq_ref, gq_ref)
        one(ak_ref, kw_ref, dkn_ref, dak_ref, gk_ref)
        dav_ref[...] = dv_ref[...].astype(BF16)

    wide = HEADS * HEAD_DIM
    vec = pl.BlockSpec((1, HEAD_DIM), lambda h: (0, 0))
    res, extra = _host_call(
        body, 7, 5, fused, _head_first_last, name="qk_bwd", grid=(HEADS,),
        in_specs=[_head_col(ATT_OFF), _head_col(ATT_OFF + HEADS), vec, vec, _head_col(0), _head_col(0), _head_col(0)],
        out_specs=[_head_col(0)] * 3 + [vec, vec],
        out_shape=[jax.ShapeDtypeStruct((SEQ, wide), BF16)] * 3 + [jax.ShapeDtypeStruct((1, HEAD_DIM), F32)] * 2,
        scratch_shapes=[], sem=("arbitrary",), operands=[proj, proj, q_w, k_w, dqn, dkn, dv] + list(fused_arrays))
    return res if fused is None else (res, extra)


def _pair_sum(name, partial, theirs, core):
    _, r, c = theirs.shape
    tr = r // 2 if r % 16 == 0 else r

    def body(core_ref, a_ref, b_ref, o_ref):
        o_ref[...] = (a_ref[...].astype(F32) + b_ref[...].astype(F32)).astype(BF16)

    spec = pl.BlockSpec((None, tr, c), lambda q, i, core_ref: (q, i, 0))
    grid_spec = pltpu.PrefetchScalarGridSpec(
        num_scalar_prefetch=1, grid=(4, r // tr),
        in_specs=[pl.BlockSpec((None, tr, c), lambda q, i, core_ref: (2 * q + core_ref[0], i, 0)), spec],
        out_specs=spec)
    return pl.pallas_call(body, name=name, grid_spec=grid_spec, out_shape=jax.ShapeDtypeStruct(theirs.shape, BF16),
                          compiler_params=_params(("parallel", "parallel")))(core, partial, theirs)


def _adamw_step(w, m, v, g):
    nm = ADAM_B1 * m + (1.0 - ADAM_B1) * g
    nv = ADAM_B2 * v + (1.0 - ADAM_B2) * (g * g)
    m_hat = nm / (1.0 - ADAM_B1 ** ADAM_STEP)
    v_hat = nv / (1.0 - ADAM_B2 ** ADAM_STEP)
    return -ADAM_LR * (m_hat / (jnp.sqrt(v_hat) + ADAM_EPS) + ADAM_WD * w), nm, nv


def _adamw(name, w, m, v, addends, tr=None):
    r, c = w.shape
    tr = r if tr is None else tr
    n_add = len(addends)

    def body(*refs):
        w_ref, m_ref, v_ref = refs[:3]
        add_refs = refs[3:3 + n_add]
        g_ref, d_ref, nm_ref, nv_ref = refs[3 + n_add:]
        g = add_refs[0][...].astype(F32)
        for a_ref in add_refs[1:]:
            g = g + a_ref[...].astype(F32)
        g_ref[...] = g
        d_ref[...], nm_ref[...], nv_ref[...] = _adamw_step(w_ref[...], m_ref[...], v_ref[...], g)

    spec = pl.BlockSpec((tr, c), lambda i: (i, 0))
    out = jax.ShapeDtypeStruct((r, c), F32)
    return pl.pallas_call(body, name=name, grid=(r // tr,), in_specs=[spec] * (3 + n_add), out_specs=[spec] * 4,
                          out_shape=[out] * 4, compiler_params=_params(("parallel",)))(w, m, v, *addends)


def _adamw_reduced(name, w, m, v, chip_sums, received, chip, tr):
    r, c = w.shape

    def body(chip_ref, w_ref, m_ref, v_ref, own_ref, r0_ref, r1_ref, r2_ref, g_ref, d_ref, nm_ref, nv_ref):
        g = ((own_ref[...].astype(F32) + r0_ref[...].astype(F32)) + r1_ref[...].astype(F32)) + r2_ref[...].astype(F32)
        g_ref[...] = g
        d_ref[...], nm_ref[...], nv_ref[...] = _adamw_step(w_ref[...], m_ref[...], v_ref[...], g)

    spec = pl.BlockSpec((tr, c), lambda i, chip_ref: (i, 0))

    def slot(k):
        return pl.BlockSpec((None, tr, c), lambda i, chip_ref: (k, i, 0))

    grid_spec = pltpu.PrefetchScalarGridSpec(
        num_scalar_prefetch=1, grid=(r // tr,),
        in_specs=[spec, spec, spec, pl.BlockSpec((None, tr, c), lambda i, chip_ref: (chip_ref[0], i, 0)),
                  slot(0), slot(1), slot(2)],
        out_specs=[spec] * 4)
    out = jax.ShapeDtypeStruct((r, c), F32)
    return pl.pallas_call(body, name=name, grid_spec=grid_spec, out_shape=[out] * 4,
                          compiler_params=_params(("parallel",)))(chip, w, m, v, chip_sums, received, received, received)


def _sum_devices(gathered):
    _, r, c = gathered.shape

    def body(g_ref, o_ref):
        acc = g_ref[0]
        for d in range(1, N_DEV):
            acc = acc + g_ref[d]
        o_ref[...] = acc

    return pl.pallas_call(body, name="sum_devices", out_shape=jax.ShapeDtypeStruct((r, c), F32))(gathered)


def _pack_rows(vectors, rows):
    flat = jnp.concatenate([v.reshape(-1) for v in vectors])
    return jnp.pad(flat, (0, rows * 128 - flat.shape[0])).reshape(rows, 128)


def _unpack(flat, shapes):
    out, off = [], 0
    for shp in shapes:
        n = 1
        for d in shp:
            n *= d
        out.append(flat[off:off + n].reshape(shp))
        off += n
    return out


def _device_step(xs, tgt, mod, norm1_w, norm2_w, lb_logits, hg_norm_w, q_norm_w, k_norm_w, conv_w_full, conv_b,
                 win_g, w_out_x, w_up_x, w_down_x, core=None):
    fused = core is not None
    shift1, scale1, gate1, shift2, scale2, gate2 = (mod[k] for k in range(6))

    h, rstd1 = _norm_fwd("norm1_fwd", xs, norm1_w, scale1, shift1)
    if fused:
        near = (0, 1, 2)
        head_rows, tail_rows = (0, UP_HEAD_ROWS), (UP_HEAD_ROWS, D_MODEL - UP_HEAD_ROWS)
        proj, (wout_g, wup_g) = _mm_blocked_rhs(
            "mm_in", h, win_g, fused_arrays=[w_out_x, w_up_x],
            fused=[_FusedCopies("gather", [w_out_x]), _FusedCopies("gather", [w_up_x], peers=near, rows=head_rows)])
        (a_out, o_pre), (wup_g, wout_g) = _hgrn_fwd(
            proj, lb_logits, hg_norm_w, fused_arrays=[w_up_x, wup_g, wout_g],
            fused=[_FusedCopies("gather_more", [w_up_x, wup_g], peers=near, rows=tail_rows, relay_rows=head_rows),
                   _FusedCopies("forward", [wout_g])])
        wout_full = wout_g.reshape(D_MODEL, D_MODEL)
        (qn, kn, vb), _ = _qk_prep(proj, q_norm_w, k_norm_w)
        (att_o, lse), (wup_g,) = _attn_fwd(qn, kn, vb, _FusedCopies("relay", [wup_g], rows=tail_rows), [wup_g])
    else:
        proj = _mm_blocked_rhs("mm_in", h, win_g)
        (a_out, o_pre), _ = _hgrn_fwd(proj, lb_logits, hg_norm_w)
        wup_g, wout_full, wdown_full = w_up_x, w_out_x, w_down_x
        (qn, kn, vb), _ = _qk_prep(proj, q_norm_w, k_norm_w)
        (att_o, lse), _ = _attn_fwd(qn, kn, vb)
    mixin = jnp.concatenate([a_out, att_o.astype(BF16)], axis=1)
    if fused:
        down_head = (0, DOWN_HEAD_ROWS)
        mix, (wup_g, wdown_g) = _mm_plain(
            "mm_out", mixin, wout_full, NN, 512, 1024, F32, fused_arrays=[wup_g, w_down_x],
            fused=[_FusedCopies("forward", [wup_g]), _FusedCopies("gather", [w_down_x], rows=down_head)])
    else:
        mix = _mm_plain("mm_out", mixin, wout_full, NN, 512, 1024, F32)
    x1, h2, rstd2 = _norm_fwd("norm2_fwd", xs, norm2_w, scale2, shift2, resid=mix, gate=gate1)
    if fused:
        down_tail = (DOWN_HEAD_ROWS, FF_BLK - DOWN_HEAD_ROWS)
        u, (wdown_g,) = _mm_blocked_rhs(
            "mm_up", h2, wup_g, fused_arrays=[w_down_x, wdown_g],
            fused=_FusedCopies("gather_more", [w_down_x, wdown_g], rows=down_tail))
        y, (wdown_g,) = _conv_gate_fwd(u, conv_w_full, conv_b, _FusedCopies("forward", [wdown_g]), [wdown_g])
        wdown_full = wdown_g.reshape(D_FF, D_MODEL)
    else:
        u = _mm_blocked_rhs("mm_up", h2, wup_g)
        y = _conv_gate_fwd(u, conv_w_full, conv_b)
    ffn = _mm_plain("mm_down", y, wdown_full, NN, MM_TILE, 512, F32)
    loss_v, dout, dffn, dgate2 = _loss_head(x1, ffn, gate2, tgt)

    dy = _mm_plain("mm_down_dx", dffn, wdown_full, NT, MM_TILE, UP_BLK, BF16)
    gw_down = _mm_plain("mm_down_dw", y, dffn, TN, UP_BLK, 1024, BF16)
    da, dg, gconv_w, gconv_b = _conv_gate_bwd(u, dy, conv_w_full, conv_b)
    gw_up = _mm_halves_wgrad("mm_up_dw", h2, da, dg)
    if fused:
        part_up, part_down = gw_up, gw_down.reshape(N_DEV, FF_BLK, D_MODEL)
        dh2, (sib_up,) = _mm_halves_rhs_t("mm_up_dx", da, dg, wup_g, fused=_FusedCopies("sibling", [part_up]),
                                          fused_arrays=[part_up])
    else:
        dh2 = _mm_halves_rhs_t("mm_up_dx", da, dg, wup_g)
    dx1, dmix, dshift2, dscale2, gnorm2, dgate1 = _norm_bwd(
        "norm2_bwd", dh2, x1, rstd2, norm2_w, scale2, dout, mix=mix, gate=gate1)
    gw_out = _mm_plain("mm_out_dw", mixin, dmix, TN, 512, 1024, BF16)
    if fused:
        part_out = gw_out.reshape(N_DEV, OUT_BLK, D_MODEL)
        dmixin, (sib_out, sib_down) = _mm_plain(
            "mm_out_dx", dmix, wout_full, NT, 512, 1024, F32,
            fused=_FusedCopies("sibling", [part_out, part_down]), fused_arrays=[part_out, part_down])
        cs_up = _pair_sum("grad_pair_sum_up", part_up, sib_up, core)
        cs_out = _pair_sum("grad_pair_sum_out", part_out, sib_out, core)
        cs_down = _pair_sum("grad_pair_sum_down", part_down, sib_down, core)
        (dhq, dhf, dhi, dhg, glog, ghg), (fc_up,) = _hgrn_bwd(
            proj, lb_logits, hg_norm_w, o_pre, dmixin, _FusedCopies("chips", [cs_up], rows=(0, UP_EXCHANGE_HEAD)),
            [cs_up])
        (dqn, dkn, dvv), (fc_down,) = _attn_bwd(
            qn, kn, vb, att_o, lse, dmixin, _FusedCopies("chips", [cs_down], rows=(0, DOWN_EXCHANGE_HEAD)), [cs_down])
        (daq, dak, dav, gqw, gkw), (fc_up,) = _qk_bwd(
            proj, q_norm_w, k_norm_w, dqn, dkn, dvv, fused_arrays=[cs_up, fc_up],
            fused=_FusedCopies("chips_more", [cs_up, fc_up], rows=(UP_EXCHANGE_HEAD, D_MODEL - UP_EXCHANGE_HEAD)))
    else:
        dmixin = _mm_plain("mm_out_dx", dmix, wout_full, NT, 512, 1024, F32)
        (dhq, dhf, dhi, dhg, glog, ghg), _ = _hgrn_bwd(proj, lb_logits, hg_norm_w, o_pre, dmixin)
        (dqn, dkn, dvv), _ = _attn_bwd(qn, kn, vb, att_o, lse, dmixin)
        daq, dak, dav, gqw, gkw = _qk_bwd(proj, q_norm_w, k_norm_w, dqn, dkn, dvv)
    dproj = jnp.concatenate([dhq, dhf, dhi, dhg, daq, dak, dav], axis=1)
    if fused:
        down_tail = (DOWN_EXCHANGE_HEAD, FF_BLK - DOWN_EXCHANGE_HEAD)
        gw_in, (fc_out, fc_down) = _mm_wgrad_blocked(
            "mm_in_dw", h, dproj, fused_arrays=[cs_out, cs_down, fc_down],
            fused=[_FusedCopies("chips", [cs_out]), _FusedCopies("chips_more", [cs_down, fc_down], rows=down_tail)])
        from_sibling, = _exchange_sibling("grad_exchange_sibling_b", [gw_in])
        cs_in = _pair_sum("grad_pair_sum_in", gw_in, from_sibling, core)
        dh, (fc_in,) = _mm_blocked_rhs_t("mm_in_dx", dproj, win_g, fused=_FusedCopies("chips", [cs_in]),
                                         fused_arrays=[cs_in])
        large = [(cs_in, fc_in), (cs_out, fc_out), (cs_up, fc_up), (cs_down, fc_down)]
    else:
        gw_in = _mm_wgrad_blocked("mm_in_dw", h, dproj)
        dh = _mm_blocked_rhs_t("mm_in_dx", dproj, win_g)
        large = [gw_in, gw_out, gw_up, gw_down]
    grad_x, dshift1, dscale1, gnorm1 = _norm_bwd("norm1_bwd", dh, xs, rstd1, norm1_w, scale1, dx1)
    gmod = jnp.concatenate([dshift1, dscale1, dgate1, dshift2, dscale2, dgate2], axis=1)
    return (loss_v, grad_x, gmod, gnorm1, gnorm2, glog, ghg, gqw, gkw, gconv_b, gconv_w, *large)


def kernel(x, c, w_ada, b_ada, norm1_w, w_in, lb_logits, hg_norm_w, q_norm_w, k_norm_w, w_out, norm2_w, w_up, conv_w, conv_b, w_down, loss_target, m_w_ada, m_b_ada, m_norm1_w, m_w_in, m_lb_logits, m_hg_norm_w, m_q_norm_w, m_k_norm_w, m_w_out, m_norm2_w, m_w_up, m_conv_w, m_conv_b, m_w_down, v_w_ada, v_b_ada, v_norm1_w, v_w_in, v_lb_logits, v_hg_norm_w, v_q_norm_w, v_k_norm_w, v_w_out, v_norm2_w, v_w_up, v_conv_w, v_conv_b, v_w_down):
    ix, iy, ic = lax.axis_index("x"), lax.axis_index("y"), lax.axis_index("c")
    me = 4 * ix + 2 * iy + ic
    my_chip = 2 * ix + iy

    xs = x[0]
    tgt = loss_target[0]

    win_g, = _allgather_weights([w_in[0].astype(BF16)])

    first = _allgather_vmem(_pack_rows([c, conv_w[0]], 40), "allgather_c_conv_w").reshape(N_DEV, 40 * 128)
    c_all = first[:, :D_MODEL]
    conv_w_full = (first[:, D_MODEL:D_MODEL + 3 * FF_BLK].reshape(N_DEV, 3, FF_BLK).transpose(1, 0, 2)
                   .reshape(3, D_FF))

    b_blk = lax.dynamic_slice_in_dim(b_ada, me * ADA_BLK, ADA_BLK, axis=1)
    mod_cols = _ada_fwd(c_all, w_ada[0], b_blk)
    mod_all = _allgather_vmem(mod_cols, "allgather_mod").reshape(N_DEV, N_DEV, ADA_BLK)
    mod = lax.dynamic_index_in_dim(mod_all, me, axis=1, keepdims=False).reshape(6, 1, D_MODEL)

    (loss_v, grad_x, gmod, gnorm1, gnorm2, glog, ghg, gqw, gkw, gconv_b, gconv_w,
     rs_in, rs_out, rs_up, rs_down) = _device_step(
        xs, tgt, mod, norm1_w, norm2_w, lb_logits, hg_norm_w, q_norm_w, k_norm_w, conv_w_full, conv_b,
        win_g, w_out[0].astype(BF16), w_up[0].astype(BF16), w_down[0].astype(BF16),
        core=jnp.reshape(ic, (1,)).astype(jnp.int32))

    small_shapes = [(1, 6 * D_MODEL), (1, D_MODEL), (1, D_MODEL), (2, HEADS * HEAD_DIM), (1, HEAD_DIM),
                    (1, HEAD_DIM), (1, HEAD_DIM), (1, D_FF), (3, D_FF), (1, 1)]
    small = [gmod, gnorm1, gnorm2, glog, ghg, gqw, gkw, gconv_b, gconv_w, loss_v[:, 0:1]]
    n_small = sum(a.size for a in small)
    rows = -(-n_small // 1024) * 8
    gathered = _allgather_vmem(_pack_rows(small, rows), "allgather_small").reshape(N_DEV, rows, 128)
    summed = _sum_devices(gathered).reshape(-1)
    (g_b_ada, g_norm1, g_norm2, g_lb, g_hg, g_q, g_k, g_conv_b, g_conv_w_full, loss_sum) = _unpack(summed, small_shapes)
    loss = loss_sum[0, 0]
    g_conv_w = lax.dynamic_slice_in_dim(g_conv_w_full, me * FF_BLK, FF_BLK, axis=1)

    gmod_all = gathered[:, :6 * D_MODEL // 128, :].reshape(N_DEV, 6 * D_MODEL)
    gmod_cols = lax.dynamic_slice_in_dim(gmod_all, me * ADA_BLK, ADA_BLK, axis=1)
    g_w_ada_raw = _ada_wgrad(c_all, gmod_cols)

    chip = jnp.reshape(my_chip, (1,)).astype(jnp.int32)

    def big_update(name, w, m, v, rs, tr):
        chip_sums, received = rs
        return _adamw_reduced(name, w[0], m[0], v[0], chip_sums, received, chip, tr)

    r_in = big_update("adamw_w_in", w_in, m_w_in, v_w_in, rs_in, 256)
    r_out = big_update("adamw_w_out", w_out, m_w_out, v_w_out, rs_out, 128)
    r_up = big_update("adamw_w_up", w_up, m_w_up, v_w_up, rs_up, 256)
    r_down = big_update("adamw_w_down", w_down, m_w_down, v_w_down, rs_down, 176)
    r_ada = _adamw("adamw_w_ada", w_ada[0], m_w_ada[0], v_w_ada[0], [g_w_ada_raw], tr=256)
    r_convw = _adamw("adamw_conv_w", conv_w[0], m_conv_w[0], v_conv_w[0], [g_conv_w])

    rep_shapes = [(1, 6 * D_MODEL), (1, D_MODEL), (1, D_MODEL), (2, HEADS * HEAD_DIM), (1, HEAD_DIM),
                  (1, HEAD_DIM), (1, HEAD_DIM), (1, D_FF)]
    rep_rows = -(-sum(a * b for a, b in rep_shapes) // 1024) * 8
    pack = lambda arrs: _pack_rows(arrs, rep_rows)
    rep = _adamw("adamw_small",
                 pack([b_ada, norm1_w, norm2_w, lb_logits, hg_norm_w, q_norm_w, k_norm_w, conv_b]),
                 pack([m_b_ada, m_norm1_w, m_norm2_w, m_lb_logits, m_hg_norm_w, m_q_norm_w, m_k_norm_w, m_conv_b]),
                 pack([v_b_ada, v_norm1_w, v_norm2_w, v_lb_logits, v_hg_norm_w, v_q_norm_w, v_k_norm_w, v_conv_b]),
                 [pack([g_b_ada, g_norm1, g_norm2, g_lb, g_hg, g_q, g_k, g_conv_b])])
    rep = [_unpack(r.reshape(-1), rep_shapes) for r in rep]

    def big(r):
        return [a[None] for a in r]

    order = {"w_ada": big(r_ada), "b_ada": [r[0] for r in rep], "norm1_w": [r[1] for r in rep],
             "w_in": big(r_in), "lb_logits": [r[3] for r in rep], "hg_norm_w": [r[4] for r in rep],
             "q_norm_w": [r[5] for r in rep], "k_norm_w": [r[6] for r in rep], "w_out": big(r_out),
             "norm2_w": [r[2] for r in rep], "w_up": big(r_up), "conv_w": big(r_convw),
             "conv_b": [r[7] for r in rep], "w_down": big(r_down)}
    names = ["w_ada", "b_ada", "norm1_w", "w_in", "lb_logits", "hg_norm_w", "q_norm_w", "k_norm_w", "w_out",
             "norm2_w", "w_up", "conv_w", "conv_b", "w_down"]
    outs = [loss, grad_x[None]]
    for kind in range(4):
        outs += [order[n][kind] for n in names]
    return tuple(outs)
```

```python
import jax
import jax.numpy as jnp
import numpy as np
from jax import lax
from jax.experimental import pallas as pl
from jax.experimental.pallas import tpu as pltpu

F32 = jnp.float32
BF16 = jnp.bfloat16

N_DEV = 8
SEQ = 2048
D_MODEL = 2048
HEADS = 8
HEAD_DIM = 128
IN_COLS = 7168
IN_BLK = IN_COLS // N_DEV
D_FF = 5632
UP_BLK = 2 * D_FF // N_DEV
FF_BLK = D_FF // N_DEV
ADA_BLK = 6 * D_MODEL // N_DEV
OUT_BLK = D_MODEL // N_DEV
EPS = 1e-6
CHUNK = 16
ROW_TILE = 256
MM_TILE = 1024
V7X_VMEM_LIMIT = 56 * 1024 * 1024

ADAM_LR = 0.001
ADAM_B1 = 0.9
ADAM_B2 = 0.999
ADAM_EPS = 1e-08
ADAM_WD = 0.01
ADAM_STEP = 10

NN = (((1,), (0,)), ((), ()))
NT = (((1,), (1,)), ((), ()))
TN = (((0,), (0,)), ((), ()))
MESH = pl.DeviceIdType.MESH


def _params(sem=None, vmem=V7X_VMEM_LIMIT):
    return pltpu.CompilerParams(dimension_semantics=sem, vmem_limit_bytes=vmem)


def _sigmoid(x):
    return 1.0 / (1.0 + jnp.exp(-x))


def _dsilu(x, s):
    return s * (1.0 + x * (1.0 - s))


def _lane_sum(x, ones_bf16):
    return jnp.dot(x.astype(BF16), ones_bf16, preferred_element_type=F32)


def _mesh_pos():
    return lax.axis_index("x"), lax.axis_index("y"), lax.axis_index("c")


def _allgather_vmem(x_blk, name):
    m_per, n = x_blk.shape

    def body(x_ref, out_ref, send_sems, recv_sems, local_sem):
        x, y, c = _mesh_pos()
        me, sibling = (x, y, c), (x, y, 1 - c)
        chips = [(1 - x, y), (x, 1 - y), (1 - x, 1 - y)]

        def rows(px, py, pc):
            return out_ref.at[pl.ds((4 * px + 2 * py + pc) * m_per, m_per), :]

        def copy(k, block, to, src=None):
            return pltpu.make_async_remote_copy(
                src_ref=rows(*block) if src is None else src, dst_ref=rows(*block),
                send_sem=send_sems.at[k], recv_sem=recv_sems.at[k], device_id=to, device_id_type=MESH)

        mine = pltpu.make_async_copy(x_ref, rows(*me), local_sem)
        mine.start()
        first = [copy(0, me, sibling, src=x_ref)]
        first += [copy(1 + j, me, (*chip, c), src=x_ref) for j, chip in enumerate(chips)]
        for cp in first:
            cp.start()
        passed = [copy(4 + j, (*chip, c), sibling) for j, chip in enumerate(chips)]
        for j, chip in enumerate(chips):
            copy(1 + j, (*chip, c), me).wait_recv()
            passed[j].start()
        copy(0, sibling, me).wait_recv()
        for j, chip in enumerate(chips):
            copy(4 + j, (*chip, 1 - c), me).wait_recv()
        for cp in first + passed:
            cp.wait_send()
        mine.wait()

    return pl.pallas_call(
        body, name=name,
        out_shape=jax.ShapeDtypeStruct((N_DEV * m_per, n), x_blk.dtype),
        in_specs=[pl.BlockSpec(memory_space=pltpu.VMEM)],
        out_specs=pl.BlockSpec(memory_space=pltpu.VMEM),
        scratch_shapes=[pltpu.SemaphoreType.DMA((7,)), pltpu.SemaphoreType.DMA((7,)), pltpu.SemaphoreType.DMA],
    )(x_blk)


def _flip(v, bit):
    return v + bit - 2 * v * bit


def _relay_chips(x, y, c):
    return (_flip(x, 1 - c), _flip(y, c)), (_flip(x, c), _flip(y, 1 - c))


UP_EXCHANGE_HEAD = 1856
DOWN_EXCHANGE_HEAD = 560
DOWN_HEAD_ROWS = 192
UP_HEAD_ROWS = 768
GATHER_PARTS = 4


def _allgather_weights(blocks):
    n_arr = len(blocks)
    parts = GATHER_PARTS

    def body(*refs):
        ins, outs = refs[:n_arr], refs[n_arr:2 * n_arr]
        send_sems, recv_sems, local_sems = refs[2 * n_arr:]
        x, y, c = _mesh_pos()
        me, sibling = (x, y, c), (x, y, 1 - c)
        near = [(1 - x, y), (x, 1 - y)]
        chips = near + [(1 - x, 1 - y)]
        relay_from, relay_to = _relay_chips(x, y, c)

        def rows(a, p):
            hr = ins[a].shape[0] // parts
            return pl.ds(p * hr, hr)

        def slot(a, pos, p):
            return outs[a].at[4 * pos[0] + 2 * pos[1] + pos[2], rows(a, p)]

        def copy(a, k, p, src, lands, to):
            return pltpu.make_async_remote_copy(
                src_ref=src, dst_ref=slot(a, lands, p), send_sem=send_sems.at[a, k, p], recv_sem=recv_sems.at[a, k, p],
                device_id=to, device_id_type=MESH)

        sent = []
        local = [pltpu.make_async_copy(ins[a], outs[a].at[4 * x + 2 * y + c], local_sems.at[a]) for a in range(n_arr)]
        for cp in local:
            cp.start()
        for p in range(parts):
            for a in range(n_arr):
                own = ins[a].at[rows(a, p)]
                sent.append(copy(a, 0, p, own, me, sibling))
                sent += [copy(a, 1 + j, p, own, me, (*chip, c)) for j, chip in enumerate(near)]
        for cp in sent:
            cp.start()

        def start(cp):
            cp.start()
            sent.append(cp)

        for p in range(parts):
            for a in range(n_arr):
                for j, chip in enumerate(near):
                    copy(a, 1 + j, p, ins[a].at[rows(a, p)], (*chip, c), me).wait_recv()
                    start(copy(a, 4 + j, p, slot(a, (*chip, c), p), (*chip, c), sibling))
                start(copy(a, 3, p, slot(a, (*relay_from, c), p), (*relay_from, c), (*relay_to, c)))
        for p in range(parts):
            for a in range(n_arr):
                copy(a, 3, p, ins[a].at[rows(a, p)], (*chips[2], c), me).wait_recv()
                start(copy(a, 6, p, slot(a, (*chips[2], c), p), (*chips[2], c), sibling))
        for p in range(parts):
            for a in range(n_arr):
                copy(a, 0, p, ins[a].at[rows(a, p)], sibling, me).wait_recv()
                for j, chip in enumerate(chips):
                    copy(a, 4 + j, p, ins[a].at[rows(a, p)], (*chip, 1 - c), me).wait_recv()
        for cp in sent:
            cp.wait_send()
        for cp in local:
            cp.wait()

    return pl.pallas_call(
        body, name="allgather_weights",
        out_shape=[jax.ShapeDtypeStruct((N_DEV,) + b.shape, b.dtype) for b in blocks],
        in_specs=[pl.BlockSpec(memory_space=pltpu.HBM)] * n_arr, out_specs=[pl.BlockSpec(memory_space=pltpu.HBM)] * n_arr,
        scratch_shapes=[pltpu.SemaphoreType.DMA((n_arr, 7, parts)), pltpu.SemaphoreType.DMA((n_arr, 7, parts)),
                        pltpu.SemaphoreType.DMA((n_arr,))],
    )(*blocks)


HBM_SPEC = pl.BlockSpec(memory_space=pltpu.HBM)


class _FusedCopies:
    def __init__(self, kind, arrays, peers=(0, 1, 2, 3), rows=None, relay_rows=None):
        self.kind = kind
        self.peers = peers
        self.rows = rows
        self.relay_rows = relay_rows
        n = len(arrays) // 2 if kind == "gather_more" else len(arrays)
        self.n = n
        self.n_in = len(arrays)
        self.aliases = {}
        if kind == "gather":
            self.out_shape = [jax.ShapeDtypeStruct((N_DEV,) + a.shape, a.dtype) for a in arrays]
            self.scratch_shapes = [pltpu.SemaphoreType.DMA((n, 4, GATHER_PARTS)),
                                   pltpu.SemaphoreType.DMA((n, 4, GATHER_PARTS)), pltpu.SemaphoreType.DMA((n,))]
        elif kind == "gather_more":
            self.out_shape = [jax.ShapeDtypeStruct(a.shape, a.dtype) for a in arrays[n:]]
            self.scratch_shapes = [pltpu.SemaphoreType.DMA((n, 5, GATHER_PARTS)),
                                   pltpu.SemaphoreType.DMA((n, 5, GATHER_PARTS)), pltpu.SemaphoreType.DMA((n,))]
            self.aliases = {n + a: a for a in range(n)}
        elif kind == "relay":
            self.out_shape = [jax.ShapeDtypeStruct(a.shape, a.dtype) for a in arrays]
            self.scratch_shapes = [pltpu.SemaphoreType.DMA((n,)), pltpu.SemaphoreType.DMA((n,))]
            self.aliases = {a: a for a in range(n)}
        elif kind == "forward":
            self.out_shape = [jax.ShapeDtypeStruct(a.shape, a.dtype) for a in arrays]
            self.scratch_shapes = [pltpu.SemaphoreType.DMA((n, 3)), pltpu.SemaphoreType.DMA((n, 3))]
            self.aliases = {a: a for a in range(n)}
        elif kind == "sibling":
            self.out_shape = [jax.ShapeDtypeStruct((4,) + a.shape[1:], a.dtype) for a in arrays]
            self.scratch_shapes = [pltpu.SemaphoreType.DMA((n, 4)), pltpu.SemaphoreType.DMA((n, 4))]
        elif kind == "chips_more":
            n = self.n = len(arrays) // 2
            self.out_shape = [jax.ShapeDtypeStruct(a.shape, a.dtype) for a in arrays[n:]]
            self.scratch_shapes = [pltpu.SemaphoreType.DMA((n, 3)), pltpu.SemaphoreType.DMA((n, 3))]
            self.aliases = {n + a: a for a in range(n)}
        else:
            self.out_shape = [jax.ShapeDtypeStruct((3,) + a.shape[1:], a.dtype) for a in arrays]
            self.scratch_shapes = [pltpu.SemaphoreType.DMA((n, 3)), pltpu.SemaphoreType.DMA((n, 3))]
        self.in_specs = [HBM_SPEC] * self.n_in
        self.out_specs = [HBM_SPEC] * n
        self.n_scratch = len(self.scratch_shapes)

    def copies(self, ins, outs, sems):
        x, y, c = _mesh_pos()
        chips = [(1 - x, y), (x, 1 - y), (1 - x, 1 - y)]
        sibling = (x, y, 1 - c)
        starts, waits = [], []
        relay_from, relay_to = _relay_chips(x, y, c)

        def relayed(a, buf, lands, send_sem, recv_sem, rows):
            first, count = rows or (0, buf.shape[1])
            span = pl.ds(first, count)
            return pltpu.make_async_remote_copy(
                src_ref=buf.at[4 * relay_from[0] + 2 * relay_from[1] + c, span],
                dst_ref=outs[a].at[4 * lands[0] + 2 * lands[1] + c, span], send_sem=send_sem, recv_sem=recv_sem,
                device_id=(*relay_to, c), device_id_type=MESH)

        if self.kind in ("gather", "gather_more"):
            send_sems, recv_sems, local_sems = sems
            me = (x, y, c)
            peers = [sibling] + [(px, py, c) for px, py in chips]

            def slot(a, pos):
                return outs[a].at[4 * pos[0] + 2 * pos[1] + pos[2]]

            def span(a, p=None):
                first, count = self.rows or (0, ins[a].shape[0])
                if p is None:
                    return pl.ds(first, count)
                return pl.ds(first + p * (count // GATHER_PARTS), count // GATHER_PARTS)

            def remote(a, k, p, lands_from):
                return pltpu.make_async_remote_copy(
                    src_ref=ins[a].at[span(a, p)], dst_ref=slot(a, lands_from).at[span(a, p)],
                    send_sem=send_sems.at[a, k, p], recv_sem=recv_sems.at[a, k, p], device_id=peers[k],
                    device_id_type=MESH)

            for a in range(self.n):
                local = pltpu.make_async_copy(ins[a].at[span(a)], slot(a, me).at[span(a)], local_sems.at[a])
                starts.append(local)
                waits.append(local)
            for p in range(GATHER_PARTS):
                for a in range(self.n):
                    for k in self.peers:
                        starts.append(remote(a, k, p, me))
                        waits.append(remote(a, k, p, peers[k]))
            if self.kind == "gather_more" and self.relay_rows is not None:
                for a in range(self.n):
                    buf = ins[self.n + a]
                    starts.append(relayed(a, buf, relay_from, send_sems.at[a, 4, 0], recv_sems.at[a, 4, 0],
                                          self.relay_rows))
                    waits.append(relayed(a, buf, chips[2], send_sems.at[a, 4, 0], recv_sems.at[a, 4, 0],
                                         self.relay_rows))
        elif self.kind == "relay":
            send_sems, recv_sems = sems
            for a in range(self.n):
                starts.append(relayed(a, ins[a], relay_from, send_sems.at[a], recv_sems.at[a], self.rows))
                waits.append(relayed(a, ins[a], chips[2], send_sems.at[a], recv_sems.at[a], self.rows))
        elif self.kind == "forward":
            send_sems, recv_sems = sems

            def passed_on(a, j, pc_src, pc_dst):
                px, py = chips[j]
                return pltpu.make_async_remote_copy(
                    src_ref=ins[a].at[4 * px + 2 * py + pc_src], dst_ref=outs[a].at[4 * px + 2 * py + pc_dst],
                    send_sem=send_sems.at[a, j], recv_sem=recv_sems.at[a, j], device_id=sibling, device_id_type=MESH)

            for a in range(self.n):
                for j in range(3):
                    starts.append(passed_on(a, j, c, c))
                    waits.append(passed_on(a, j, c, 1 - c))
        elif self.kind == "sibling":
            send_sems, recv_sems = sems
            for a in range(self.n):
                for q in range(4):
                    cp = pltpu.make_async_remote_copy(
                        src_ref=ins[a].at[2 * q + 1 - c], dst_ref=outs[a].at[q], send_sem=send_sems.at[a, q],
                        recv_sem=recv_sems.at[a, q], device_id=sibling, device_id_type=MESH)
                    starts.append(cp)
                    waits.append(cp)
        else:
            send_sems, recv_sems = sems
            for a in range(self.n):
                first, count = self.rows or (0, ins[a].shape[1])
                span = pl.ds(first, count)
                for j, (px, py) in enumerate(chips):
                    cp = pltpu.make_async_remote_copy(
                        src_ref=ins[a].at[2 * px + py, span], dst_ref=outs[a].at[j, span],
                        send_sem=send_sems.at[a, j], recv_sem=recv_sems.at[a, j], device_id=(px, py, c),
                        device_id_type=MESH)
                    starts.append(cp)
                    waits.append(cp)
        return starts, waits


def _fused_groups(fused):
    if fused is None:
        return []
    return list(fused) if isinstance(fused, (list, tuple)) else [fused]


def _host_body(body, n_in, n_out, fused, first_last):
    groups = _fused_groups(fused)
    if not groups:
        return body
    n_fin, n_fout = sum(g.n_in for g in groups), sum(g.n for g in groups)
    n_fsem = sum(g.n_scratch for g in groups)

    def wrapped(*refs):
        core_in, f_in = refs[:n_in], refs[n_in:n_in + n_fin]
        core_out = refs[n_in + n_fin:n_in + n_fin + n_out]
        f_out = refs[n_in + n_fin + n_out:n_in + n_fin + n_out + n_fout]
        rest = refs[n_in + n_fin + n_out + n_fout:]
        core_scratch, f_sems = rest[:len(rest) - n_fsem], rest[len(rest) - n_fsem:]
        starts, waits = [], []
        for g in groups:
            s, w = g.copies(f_in[:g.n_in], f_out[:g.n], f_sems[:g.n_scratch])
            f_in, f_out, f_sems = f_in[g.n_in:], f_out[g.n:], f_sems[g.n_scratch:]
            starts += s
            waits += w
        first, last = first_last()

        @pl.when(first)
        def _():
            for cp in starts:
                cp.start()

        body(*core_in, *core_out, *core_scratch)

        @pl.when(last)
        def _():
            for cp in waits:
                cp.wait()

    return wrapped


def _host_call(body, n_in, n_out, fused, first_last, *, name, grid, in_specs, out_specs, out_shape, scratch_shapes,
               sem, operands):
    aliases = {}
    in_specs, out_specs, out_shape, scratch_shapes = list(in_specs), list(out_specs), list(out_shape), list(scratch_shapes)
    fin, fout = n_in, n_out
    for g in _fused_groups(fused):
        aliases.update({fin + fi: fout + fo for fi, fo in g.aliases.items()})
        fin, fout = fin + g.n_in, fout + g.n
        in_specs += g.in_specs
        out_specs += g.out_specs
        out_shape += g.out_shape
        scratch_shapes += g.scratch_shapes
        sem = tuple("arbitrary" for _ in sem)
    res = pl.pallas_call(_host_body(body, n_in, n_out, fused, first_last), name=name, grid=grid, in_specs=in_specs,
                         out_specs=out_specs, out_shape=out_shape, scratch_shapes=scratch_shapes,
                         input_output_aliases=aliases, compiler_params=_params(sem))(*operands)
    return list(res[:n_out]), list(res[n_out:])


def _exchange_sibling(name, partials):
    n_arr = len(partials)

    def body(*refs):
        ins, outs = refs[:n_arr], refs[n_arr:2 * n_arr]
        send_sems, recv_sems = refs[2 * n_arr:]
        x, y, c = _mesh_pos()
        copies = [pltpu.make_async_remote_copy(
            src_ref=ins[a].at[2 * q + 1 - c], dst_ref=outs[a].at[q], send_sem=send_sems.at[a, q],
            recv_sem=recv_sems.at[a, q], device_id=(x, y, 1 - c), device_id_type=MESH)
            for a in range(n_arr) for q in range(4)]
        for cp in copies:
            cp.start()
        for cp in copies:
            cp.wait_recv()
        for cp in copies:
            cp.wait_send()

    return pl.pallas_call(
        body, name=name,
        out_shape=[jax.ShapeDtypeStruct((4,) + p.shape[1:], p.dtype) for p in partials],
        in_specs=[HBM_SPEC] * n_arr, out_specs=[HBM_SPEC] * n_arr,
        scratch_shapes=[pltpu.SemaphoreType.DMA((n_arr, 4)), pltpu.SemaphoreType.DMA((n_arr, 4))],
    )(*partials)


def _matmul(name, a, b, dims, grid, a_spec, b_spec, o_spec, out_shape, acc_axis=None, fused=None, fused_arrays=()):
    def body(a_ref, b_ref, o_ref):
        r = lax.dot_general(a_ref[...], b_ref[...], dims, preferred_element_type=F32)
        if acc_axis is None:
            o_ref[...] = r.astype(o_ref.dtype)
        else:
            k = pl.program_id(acc_axis)

            @pl.when(k == 0)
            def _():
                o_ref[...] = r

            @pl.when(k > 0)
            def _():
                o_ref[...] += r

    sem = tuple("arbitrary" if i == acc_axis else "parallel" for i in range(len(grid)))
    if fused is None:
        return pl.pallas_call(body, name=name, grid=grid, in_specs=[a_spec, b_spec], out_specs=o_spec,
                              out_shape=out_shape, compiler_params=_params(sem))(a, b)

    def first_last():
        first = last = None
        for ax, n in enumerate(grid):
            f, l = pl.program_id(ax) == 0, pl.program_id(ax) == n - 1
            first, last = (f, l) if first is None else (first & f, last & l)
        return first, last

    (out,), extra = _host_call(body, 2, 1, fused, first_last, name=name, grid=grid, in_specs=[a_spec, b_spec],
                               out_specs=[o_spec], out_shape=[out_shape], scratch_shapes=[], sem=sem,
                               operands=[a, b] + list(fused_arrays))
    return out, extra


def _mm_blocked_rhs(name, a, w_g, tm=MM_TILE, fused=None, fused_arrays=()):
    m, k = a.shape
    nb = w_g.shape[2]
    return _matmul(name, a, w_g, NN, (N_DEV, m // tm),
                   pl.BlockSpec((tm, k), lambda j, i: (i, 0)),
                   pl.BlockSpec((None, k, nb), lambda j, i: (j, 0, 0)),
                   pl.BlockSpec((tm, nb), lambda j, i: (i, j)),
                   jax.ShapeDtypeStruct((m, N_DEV * nb), F32), fused=fused, fused_arrays=fused_arrays)


def _mm_blocked_rhs_t(name, a, w_g, tm=MM_TILE, fused=None, fused_arrays=()):
    m = a.shape[0]
    n, nb = w_g.shape[1], w_g.shape[2]
    return _matmul(name, a, w_g, NT, (m // tm, N_DEV),
                   pl.BlockSpec((tm, nb), lambda i, j: (i, j)),
                   pl.BlockSpec((None, n, nb), lambda i, j: (j, 0, 0)),
                   pl.BlockSpec((tm, n), lambda i, j: (i, 0)),
                   jax.ShapeDtypeStruct((m, n), F32), acc_axis=1, fused=fused, fused_arrays=fused_arrays)


def _mm_wgrad_blocked(name, act, dcols, tk=MM_TILE, fused=None, fused_arrays=()):
    t, k = act.shape
    nb = dcols.shape[1] // N_DEV
    return _matmul(name, act, dcols, TN, (N_DEV, k // tk),
                   pl.BlockSpec((t, tk), lambda j, i: (0, i)),
                   pl.BlockSpec((t, nb), lambda j, i: (0, j)),
                   pl.BlockSpec((None, tk, nb), lambda j, i: (j, i, 0)),
                   jax.ShapeDtypeStruct((N_DEV, k, nb), BF16), fused=fused, fused_arrays=fused_arrays)


def _halves_specs(block, index):
    half = N_DEV // 2
    return (pl.BlockSpec(block, lambda i, j: index(i, jnp.minimum(j, half - 1))),
            pl.BlockSpec(block, lambda i, j: index(i, jnp.maximum(j - half, 0))))


def _mm_halves_rhs_t(name, a_lo, a_hi, w_g, tm=MM_TILE, fused=None, fused_arrays=()):
    m = a_lo.shape[0]
    n, nb = w_g.shape[1], w_g.shape[2]

    def body(lo_ref, hi_ref, b_ref, o_ref):
        j = pl.program_id(1)

        def accumulate(a_ref):
            r = lax.dot_general(a_ref[...], b_ref[...], NT, preferred_element_type=F32)

            @pl.when(j == 0)
            def _():
                o_ref[...] = r

            @pl.when(j > 0)
            def _():
                o_ref[...] += r

        pl.when(j < N_DEV // 2)(lambda: accumulate(lo_ref))
        pl.when(j >= N_DEV // 2)(lambda: accumulate(hi_ref))

    def first_last():
        i, j = pl.program_id(0), pl.program_id(1)
        return (i == 0) & (j == 0), (i == m // tm - 1) & (j == N_DEV - 1)

    lo_spec, hi_spec = _halves_specs((tm, nb), lambda i, j: (i, j))
    (out,), extra = _host_call(
        body, 3, 1, fused, first_last, name=name, grid=(m // tm, N_DEV),
        in_specs=[lo_spec, hi_spec, pl.BlockSpec((None, n, nb), lambda i, j: (j, 0, 0))],
        out_specs=[pl.BlockSpec((tm, n), lambda i, j: (i, 0))], out_shape=[jax.ShapeDtypeStruct((m, n), F32)],
        scratch_shapes=[], sem=("parallel", "arbitrary"), operands=[a_lo, a_hi, w_g] + list(fused_arrays))
    return out if fused is None else (out, extra)


def _mm_halves_wgrad(name, act, d_lo, d_hi, tk=MM_TILE):
    t, k = act.shape
    nb = d_lo.shape[1] // (N_DEV // 2)

    def body(a_ref, lo_ref, hi_ref, o_ref):
        j = pl.program_id(0)

        def product(d_ref):
            o_ref[...] = lax.dot_general(a_ref[...], d_ref[...], TN, preferred_element_type=F32).astype(o_ref.dtype)

        pl.when(j < N_DEV // 2)(lambda: product(lo_ref))
        pl.when(j >= N_DEV // 2)(lambda: product(hi_ref))

    half = N_DEV // 2
    return pl.pallas_call(
        body, name=name, grid=(N_DEV, k // tk),
        in_specs=[pl.BlockSpec((t, tk), lambda j, i: (0, i)),
                  pl.BlockSpec((t, nb), lambda j, i: (0, jnp.minimum(j, half - 1))),
                  pl.BlockSpec((t, nb), lambda j, i: (0, jnp.maximum(j - half, 0)))],
        out_specs=pl.BlockSpec((None, tk, nb), lambda j, i: (j, i, 0)),
        out_shape=jax.ShapeDtypeStruct((N_DEV, k, nb), BF16),
        compiler_params=_params(("parallel", "parallel")))(act, d_lo, d_hi)


def _mm_plain(name, a, b, dims, tm, tn, out_dtype, fused=None, fused_arrays=()):
    if dims == NN:
        (m, k), n = a.shape, b.shape[1]
        a_spec = pl.BlockSpec((tm, k), lambda i, j: (i, 0))
        b_spec = pl.BlockSpec((k, tn), lambda i, j: (0, j))
    elif dims == NT:
        (m, k), n = a.shape, b.shape[0]
        a_spec = pl.BlockSpec((tm, k), lambda i, j: (i, 0))
        b_spec = pl.BlockSpec((tn, k), lambda i, j: (j, 0))
    else:
        (k, m), n = a.shape, b.shape[1]
        a_spec = pl.BlockSpec((k, tm), lambda i, j: (0, i))
        b_spec = pl.BlockSpec((k, tn), lambda i, j: (0, j))
    return _matmul(name, a, b, dims, (m // tm, n // tn), a_spec, b_spec,
                   pl.BlockSpec((tm, tn), lambda i, j: (i, j)), jax.ShapeDtypeStruct((m, n), out_dtype),
                   fused=fused, fused_arrays=fused_arrays)


def _ada_fwd(c_all, w_ada_blk, b_blk):
    def body(c_ref, w_ref, b_ref, o_ref):
        cv = c_ref[...]
        o_ref[...] = jnp.dot(cv * _sigmoid(cv), w_ref[...], preferred_element_type=F32) + b_ref[...]

    tn = 512
    return pl.pallas_call(
        body, name="ada_fwd", grid=(ADA_BLK // tn,),
        in_specs=[pl.BlockSpec((N_DEV, D_MODEL), lambda j: (0, 0)),
                  pl.BlockSpec((D_MODEL, tn), lambda j: (0, j)),
                  pl.BlockSpec((1, tn), lambda j: (0, j))],
        out_specs=pl.BlockSpec((N_DEV, tn), lambda j: (0, j)),
        out_shape=jax.ShapeDtypeStruct((N_DEV, ADA_BLK), F32),
        compiler_params=_params(("parallel",)))(c_all, w_ada_blk, b_blk)


def _ada_wgrad(c_all, gmod_cols):
    def body(c_ref, g_ref, o_ref):
        cv = c_ref[...]
        o_ref[...] = lax.dot_general(cv * _sigmoid(cv), g_ref[...], TN, preferred_element_type=F32)

    tk = 512
    return pl.pallas_call(
        body, name="ada_wgrad", grid=(D_MODEL // tk,),
        in_specs=[pl.BlockSpec((N_DEV, tk), lambda i: (0, i)),
                  pl.BlockSpec((N_DEV, ADA_BLK), lambda i: (0, 0))],
        out_specs=pl.BlockSpec((tk, ADA_BLK), lambda i: (i, 0)),
        out_shape=jax.ShapeDtypeStruct((D_MODEL, ADA_BLK), F32),
        compiler_params=_params(("parallel",)))(c_all, gmod_cols)


def _row_spec(cols=D_MODEL):
    return pl.BlockSpec((ROW_TILE, cols), lambda i: (i, 0))


def _vec_spec(cols=D_MODEL):
    return pl.BlockSpec((1, cols), lambda i: (0, 0))


def _norm_fwd(name, x, w, scale, shift, resid=None, gate=None):
    has_res = resid is not None

    def body(*refs):
        if has_res:
            x_ref, r_ref, g_ref, w_ref, sc_ref, sh_ref, xr_ref, h_ref, rs_ref = refs
            xr = x_ref[...] + g_ref[...] * r_ref[...]
            xr_ref[...] = xr
        else:
            x_ref, w_ref, sc_ref, sh_ref, h_ref, rs_ref = refs
            xr = x_ref[...]
        rs = lax.rsqrt(jnp.mean(xr * xr, axis=-1, keepdims=True) + EPS)
        h = (xr * rs) * w_ref[...] * (1.0 + sc_ref[...]) + sh_ref[...]
        h_ref[...] = h.astype(BF16)
        rs_ref[...] = rs

    s = x.shape[0]
    ins = [x] + ([resid, gate] if has_res else []) + [w, scale, shift]
    in_specs = [_row_spec()] + ([_row_spec(), _vec_spec()] if has_res else []) + [_vec_spec()] * 3
    outs = ([jax.ShapeDtypeStruct((s, D_MODEL), F32)] if has_res else []) + [
        jax.ShapeDtypeStruct((s, D_MODEL), BF16), jax.ShapeDtypeStruct((s, 1), F32)]
    out_specs = ([_row_spec()] if has_res else []) + [_row_spec(), pl.BlockSpec((ROW_TILE, 1), lambda i: (i, 0))]
    return pl.pallas_call(body, name=name, grid=(s // ROW_TILE,), in_specs=in_specs, out_specs=out_specs,
                          out_shape=outs, compiler_params=_params(("parallel",)))(*ins)


def _norm_bwd(name, dh, x, rstd, w, scale, dres, mix=None, gate=None, fused=None, fused_arrays=()):
    has_mix = mix is not None

    def body(*refs):
        if has_mix:
            (dh_ref, x_ref, rs_ref, w_ref, sc_ref, dr_ref, mix_ref, g_ref,
             dx_ref, dmix_ref, dsh_ref, dsc_ref, dw_ref, dg_ref) = refs
        else:
            dh_ref, x_ref, rs_ref, w_ref, sc_ref, dr_ref, dx_ref, dsh_ref, dsc_ref, dw_ref = refs
        i = pl.program_id(0)
        dhv = dh_ref[...]
        rs = rs_ref[...]
        xn = x_ref[...] * rs
        wv = w_ref[...]
        one_sc = 1.0 + sc_ref[...]
        dxn = dhv * wv * one_sc
        dx = dr_ref[...] + rs * (dxn - xn * jnp.mean(dxn * xn, axis=-1, keepdims=True))
        dx_ref[...] = dx
        sums = [(dsh_ref, dhv), (dsc_ref, dhv * xn * wv), (dw_ref, dhv * one_sc * xn)]
        if has_mix:
            dmix_ref[...] = (dx * g_ref[...]).astype(BF16)
            sums.append((dg_ref, dx * mix_ref[...]))

        @pl.when(i == 0)
        def _():
            for ref, _v in sums:
                ref[...] = jnp.zeros_like(ref)

        for ref, v in sums:
            ref[...] += jnp.sum(v, axis=0, keepdims=True)

    s = x.shape[0]
    ins = [dh, x, rstd, w, scale, dres] + ([mix, gate] if has_mix else [])
    in_specs = ([_row_spec(), _row_spec(), pl.BlockSpec((ROW_TILE, 1), lambda i: (i, 0)), _vec_spec(), _vec_spec(),
                 _row_spec()] + ([_row_spec(), _vec_spec()] if has_mix else []))
    vec = jax.ShapeDtypeStruct((1, D_MODEL), F32)
    outs = ([jax.ShapeDtypeStruct((s, D_MODEL), F32)] + ([jax.ShapeDtypeStruct((s, D_MODEL), BF16)] if has_mix else [])
            + [vec] * (4 if has_mix else 3))
    out_specs = [_row_spec()] + ([_row_spec()] if has_mix else []) + [_vec_spec()] * (4 if has_mix else 3)

    def first_last():
        i = pl.program_id(0)
        return i == 0, i == s // ROW_TILE - 1

    res, extra = _host_call(body, len(ins), len(outs), fused, first_last, name=name, grid=(s // ROW_TILE,),
                            in_specs=in_specs, out_specs=out_specs, out_shape=outs, scratch_shapes=[],
                            sem=("arbitrary",), operands=ins + list(fused_arrays))
    return res if fused is None else (res, extra)


def _loss_head(x1, ffn, gate2, target):
    def body(x_ref, f_ref, g_ref, t_ref, loss_ref, dout_ref, dffn_ref, dg_ref):
        i = pl.program_id(0)
        fv = f_ref[...]
        gv = g_ref[...]
        err = x_ref[...] + gv * fv - t_ref[...]
        dout = err * (1.0 / D_MODEL)
        dout_ref[...] = dout
        dffn_ref[...] = (dout * gv).astype(BF16)

        @pl.when(i == 0)
        def _():
            loss_ref[...] = jnp.zeros_like(loss_ref)
            dg_ref[...] = jnp.zeros_like(dg_ref)

        row = jnp.sum(err * err, axis=-1, keepdims=True) * (1.0 / D_MODEL)
        loss_ref[...] += jnp.broadcast_to(0.5 * jnp.sum(row, axis=0, keepdims=True), (1, 128))
        dg_ref[...] += jnp.sum(dout * fv, axis=0, keepdims=True)

    s = x1.shape[0]
    return pl.pallas_call(
        body, name="loss_head", grid=(s // ROW_TILE,),
        in_specs=[_row_spec(), _row_spec(), _vec_spec(), _row_spec()],
        out_specs=[pl.BlockSpec((1, 128), lambda i: (0, 0)), _row_spec(), _row_spec(), _vec_spec()],
        out_shape=[jax.ShapeDtypeStruct((1, 128), F32), jax.ShapeDtypeStruct((s, D_MODEL), F32),
                   jax.ShapeDtypeStruct((s, D_MODEL), BF16), jax.ShapeDtypeStruct((1, D_MODEL), F32)],
        compiler_params=_params(("arbitrary",)))(x1, ffn, gate2, target)


CONV_TILE = 512
N_CONV_TILES = D_FF // CONV_TILE


def _shift_rows(a, k, row):
    n = a.shape[0]
    if k > 0:
        return jnp.where(row >= k, pltpu.roll(a, k, 0), 0.0)
    return jnp.where(row < n + k, pltpu.roll(a, n + k, 0), 0.0)


def _conv_gate_fwd(u, conv_w, conv_b, fused=None, fused_arrays=()):
    s = u.shape[0]

    def body(a_ref, g_ref, w_ref, b_ref, y_ref):
        a = a_ref[...]
        w = w_ref[...]
        row = lax.broadcasted_iota(jnp.int32, a.shape, 0)
        ac = b_ref[...] + _shift_rows(a, 2, row) * w[0:1] + _shift_rows(a, 1, row) * w[1:2] + a * w[2:3]
        y_ref[...] = (ac * _sigmoid(ac) * g_ref[...]).astype(BF16)

    def first_last():
        i = pl.program_id(0)
        return i == 0, i == N_CONV_TILES - 1

    col = lambda off: pl.BlockSpec((s, CONV_TILE), lambda i: (0, i + off))
    (y,), extra = _host_call(
        body, 4, 1, fused, first_last, name="conv_gate_fwd", grid=(N_CONV_TILES,),
        in_specs=[col(0), col(N_CONV_TILES), pl.BlockSpec((3, CONV_TILE), lambda i: (0, i)),
                  pl.BlockSpec((1, CONV_TILE), lambda i: (0, i))],
        out_specs=[col(0)], out_shape=[jax.ShapeDtypeStruct((s, D_FF), BF16)], scratch_shapes=[], sem=("parallel",),
        operands=[u, u, conv_w, conv_b] + list(fused_arrays))
    return y if fused is None else (y, extra)


def _conv_gate_bwd(u, dy, conv_w, conv_b):
    s = u.shape[0]

    def body(a_ref, g_ref, dy_ref, w_ref, b_ref, da_ref, dg_ref, gw_ref, gb_ref):
        a = a_ref[...]
        w = w_ref[...]
        row = lax.broadcasted_iota(jnp.int32, a.shape, 0)
        a1 = _shift_rows(a, 1, row)
        a2 = _shift_rows(a, 2, row)
        ac = b_ref[...] + a2 * w[0:1] + a1 * w[1:2] + a * w[2:3]
        sg = _sigmoid(ac)
        dyv = dy_ref[...].astype(F32)
        dg_ref[...] = (dyv * (ac * sg)).astype(BF16)
        dac = dyv * g_ref[...] * _dsilu(ac, sg)
        gb_ref[...] = jnp.sum(dac, axis=0, keepdims=True)
        gw_ref[0:1, :] = jnp.sum(dac * a2, axis=0, keepdims=True)
        gw_ref[1:2, :] = jnp.sum(dac * a1, axis=0, keepdims=True)
        gw_ref[2:3, :] = jnp.sum(dac * a, axis=0, keepdims=True)
        da = dac * w[2:3] + _shift_rows(dac, -1, row) * w[1:2] + _shift_rows(dac, -2, row) * w[0:1]
        da_ref[...] = da.astype(BF16)

    col = lambda off: pl.BlockSpec((s, CONV_TILE), lambda i: (0, i + off))
    return pl.pallas_call(
        body, name="conv_gate_bwd", grid=(N_CONV_TILES,),
        in_specs=[col(0), col(N_CONV_TILES), col(0), pl.BlockSpec((3, CONV_TILE), lambda i: (0, i)),
                  pl.BlockSpec((1, CONV_TILE), lambda i: (0, i))],
        out_specs=[col(0), col(0), pl.BlockSpec((3, CONV_TILE), lambda i: (0, i)),
                   pl.BlockSpec((1, CONV_TILE), lambda i: (0, i))],
        out_shape=[jax.ShapeDtypeStruct((s, D_FF), BF16), jax.ShapeDtypeStruct((s, D_FF), BF16),
                   jax.ShapeDtypeStruct((3, D_FF), F32), jax.ShapeDtypeStruct((1, D_FF), F32)],
        compiler_params=_params(("parallel",)))(u, u, dy, conv_w, conv_b)


HG_TILE = 256
CHUNK_UNROLL = 8


def _unrolled_loop(n, body, init):
    def group(i, carry):
        for u in range(CHUNK_UNROLL):
            carry = body(i * CHUNK_UNROLL + u, carry)
        return carry

    return lax.fori_loop(0, n // CHUNK_UNROLL, group, init)


def _head_col(off):
    return pl.BlockSpec((SEQ, HEAD_DIM), lambda h: (0, h + off))


def _hgrn_gates(hq, hf, lb, pos):
    q = hq * _sigmoid(hq)
    sig = _sigmoid(hf)
    f = lb + (1.0 - lb) * sig
    gl = jnp.log(f)
    for sh in (1, 2, 4, 8):
        gl = gl + jnp.where(pos >= sh, pltpu.roll(gl, sh, 0), 0.0)
    return q, sig, f, 1.0 - f, gl


def _lower_bound(lbl):
    return 1.0 / (1.0 + jnp.exp(lbl[1:2, :] - lbl[0:1, :]))


def _head_first_last():
    h = pl.program_id(0)
    return h == 0, h == HEADS - 1


CHUNKS_PER_TILE = HG_TILE // CHUNK


def _chunk_end(x, pos):
    y = jnp.where(pos == CHUNK - 1, x, 0.0)
    for sh in (1, 2, 4, 8):
        y = y + jnp.where(pos < CHUNK - sh, pltpu.roll(y, x.shape[0] - sh, 0), 0.0)
    return y


def _suffix_in_chunk(x, pos):
    for sh in (1, 2, 4, 8):
        x = x + jnp.where(pos < CHUNK - sh, pltpu.roll(x, x.shape[0] - sh, 0), 0.0)
    return x


def _prefix_in_chunk(x, pos):
    for sh in (1, 2, 4, 8):
        x = x + jnp.where(pos >= sh, pltpu.roll(x, sh, 0), 0.0)
    return x


def _pair_decays(f, pos):
    shifted = jnp.where(pos >= 1, f, 0.0)
    e = shifted
    yield 1, e
    for d in range(2, CHUNK):
        shifted = pltpu.roll(shifted, 1, 0)
        e = e * shifted
        yield d, e


def _chunk_rows(cc):
    return slice(cc * CHUNK, (cc + 1) * CHUNK)


def _outer_products(lhs_b, rhs_b, dst, i):
    n = lhs_b.shape[0] // CHUNK
    for cc in range(n):
        dst[i * n + cc] = lax.dot_general(lhs_b[_chunk_rows(cc)], rhs_b[_chunk_rows(cc)], TN,
                                          preferred_element_type=F32)


def _state_scan(n_chunks, gl_s, u_s, keep, reverse):
    def step(k, st):
        c = n_chunks - 1 - k if reverse else k
        keep[c] = st.astype(BF16)
        gl = gl_s[pl.ds(pl.multiple_of(c * CHUNK, CHUNK), CHUNK), :]
        return st * jnp.exp(gl[CHUNK - 1:CHUNK, :]) + u_s[c]

    _unrolled_loop(n_chunks, step, jnp.zeros((HEAD_DIM, HEAD_DIM), F32))


def _hgrn_fwd(proj, lb_logits, norm_w, fused=None, fused_arrays=()):
    n_tiles = SEQ // HG_TILE
    n_chunks = SEQ // CHUNK
    fused_arrays = list(fused_arrays)

    def body(hq_ref, hf_ref, hi_ref, hg_ref, lbl_ref, nw_ref, aout_ref, opre_ref, qt_s, gl_s, u_s, st_s):
        lb = _lower_bound(lbl_ref[...])
        ones = jnp.ones((HEAD_DIM, HEAD_DIM), BF16)
        pos = lax.broadcasted_iota(jnp.int32, (HG_TILE, HEAD_DIM), 0) % CHUNK

        def tile(i, carry):
            rows = pl.ds(pl.multiple_of(i * HG_TILE, HG_TILE), HG_TILE)
            v = hi_ref[rows, :]
            q, _sig, f, kk, gl = _hgrn_gates(hq_ref[rows, :], hf_ref[rows, :], lb, pos)
            o = _lane_sum(q * kk, ones) * v
            for d, e in _pair_decays(f, pos):
                o = o + _lane_sum(q * pltpu.roll(kk, d, 0) * e, ones) * pltpu.roll(v, d, 0)
            opre_ref[rows, :] = o
            qt_s[rows, :] = q * jnp.exp(gl)
            gl_s[rows, :] = gl
            kt = kk * jnp.exp(_chunk_end(gl, pos) - gl)
            _outer_products(v.astype(BF16), kt.astype(BF16), u_s, i)
            return carry

        lax.fori_loop(0, n_tiles, tile, 0)
        _state_scan(n_chunks, gl_s, u_s, st_s, reverse=False)

        def finish(i, carry):
            rows = pl.ds(pl.multiple_of(i * HG_TILE, HG_TILE), HG_TILE)
            qt_b = qt_s[rows, :].astype(BF16)
            past = [lax.dot_general(qt_b[_chunk_rows(cc)], st_s[i * CHUNKS_PER_TILE + cc], NT,
                                    preferred_element_type=F32) for cc in range(CHUNKS_PER_TILE)]
            o = opre_ref[rows, :] + jnp.concatenate(past, axis=0)
            opre_ref[rows, :] = o
            hg = hg_ref[rows, :]
            rs = lax.rsqrt(jnp.mean(o * o, axis=-1, keepdims=True) + EPS)
            aout_ref[rows, :] = ((o * rs) * nw_ref[...] * (hg * _sigmoid(hg))).astype(BF16)
            return carry

        lax.fori_loop(0, n_tiles, finish, 0)

    return _host_call(
        body, 6, 2, fused, _head_first_last, name="hgrn_fwd", grid=(HEADS,),
        in_specs=[_head_col(0), _head_col(HEADS), _head_col(2 * HEADS), _head_col(3 * HEADS),
                  pl.BlockSpec((2, HEAD_DIM), lambda h: (0, h)), pl.BlockSpec((1, HEAD_DIM), lambda h: (0, 0))],
        out_specs=[_head_col(0), _head_col(0)],
        out_shape=[jax.ShapeDtypeStruct((SEQ, HEADS * HEAD_DIM), BF16), jax.ShapeDtypeStruct((SEQ, HEADS * HEAD_DIM), F32)],
        scratch_shapes=[pltpu.VMEM((SEQ, HEAD_DIM), F32)] * 2 + [pltpu.VMEM((n_chunks, HEAD_DIM, HEAD_DIM), F32),
                                                                 pltpu.VMEM((n_chunks, HEAD_DIM, HEAD_DIM), BF16)],
        sem=("parallel",), operands=[proj, proj, proj, proj, lb_logits, norm_w] + fused_arrays)


def _hgrn_bwd(proj, lb_logits, norm_w, o_pre, d_aout, fused=None, fused_arrays=()):
    n_tiles = SEQ // HG_TILE
    n_chunks = SEQ // CHUNK

    def body(hq_ref, hf_ref, hi_ref, hg_ref, lbl_ref, nw_ref, opre_ref, da_ref,
             dhq_ref, dhf_ref, dhi_ref, dhg_ref, dlog_ref, gnw_ref,
             q_s, k_s, gl_s, do_s, dq_s, dk_s, dv_s, u_s, st_s, rt_s):
        h = pl.program_id(0)
        lb = _lower_bound(lbl_ref[...])
        nw = nw_ref[...]
        ones = jnp.ones((HEAD_DIM, HEAD_DIM), BF16)
        pos = lax.broadcasted_iota(jnp.int32, (HG_TILE, HEAD_DIM), 0) % CHUNK

        @pl.when(h == 0)
        def _():
            gnw_ref[...] = jnp.zeros_like(gnw_ref)

        pos_all, half = pos, HG_TILE // 2
        pos = pos_all[:half]

        def tile(i, carry):
            rows = pl.ds(pl.multiple_of(i * half, half), half)
            v = hi_ref[rows, :]
            q, _sig, f, kk, gl = _hgrn_gates(hq_ref[rows, :], hf_ref[rows, :], lb, pos)
            o = opre_ref[rows, :]
            hg = hg_ref[rows, :]
            da = da_ref[rows, :]
            rs = lax.rsqrt(jnp.mean(o * o, axis=-1, keepdims=True) + EPS)
            oh = o * rs
            sg = _sigmoid(hg)
            dnorm = da * (hg * sg)
            dhg_ref[rows, :] = (da * (oh * nw) * _dsilu(hg, sg)).astype(BF16)
            gnw_ref[...] += jnp.sum(dnorm * oh, axis=0, keepdims=True)
            doh = dnorm * nw
            do = rs * (doh - oh * jnp.mean(doh * oh, axis=-1, keepdims=True))

            d_a = _lane_sum(do * v, ones)
            dq = d_a * kk
            dk = d_a * q
            dv = _lane_sum(q * kk, ones) * do
            for d, e in _pair_decays(f, pos):
                ks = pltpu.roll(kk, d, 0)
                a_d = _lane_sum(q * ks * e, ones)
                d_a = _lane_sum(do * pltpu.roll(v, d, 0), ones) * e
                dq = dq + d_a * ks
                dk = dk + pltpu.roll(d_a * q, half - d, 0)
                dv = dv + pltpu.roll(a_d * do, half - d, 0)
            q_s[rows, :] = q
            k_s[rows, :] = kk
            gl_s[rows, :] = gl
            do_s[rows, :] = do
            dq_s[rows, :] = dq
            dk_s[rows, :] = dk
            dv_s[rows, :] = dv
            kt = kk * jnp.exp(_chunk_end(gl, pos) - gl)
            _outer_products(v.astype(BF16), kt.astype(BF16), u_s, i)
            return carry

        lax.fori_loop(0, 2 * n_tiles, tile, 0)
        pos = pos_all
        _state_scan(n_chunks, gl_s, u_s, st_s, reverse=False)

        def reverse_increments(i, carry):
            rows = pl.ds(pl.multiple_of(i * HG_TILE, HG_TILE), HG_TILE)
            qt = q_s[rows, :] * jnp.exp(gl_s[rows, :])
            _outer_products(do_s[rows, :].astype(BF16), qt.astype(BF16), u_s, i)
            return carry

        lax.fori_loop(0, n_tiles, reverse_increments, 0)
        _state_scan(n_chunks, gl_s, u_s, rt_s, reverse=True)

        def finish(i, dlb):
            rows = pl.ds(pl.multiple_of(i * HG_TILE, HG_TILE), HG_TILE)
            q = q_s[rows, :]
            kk = k_s[rows, :]
            gl = gl_s[rows, :]
            gll = _chunk_end(gl, pos)
            ekt = jnp.exp(gll - gl)
            do_b = do_s[rows, :].astype(BF16)
            v_b = hi_ref[rows, :].astype(BF16)
            kt_b = (kk * ekt).astype(BF16)
            dq_far, dk_far, dv_far, across = [], [], [], []
            for cc in range(CHUNKS_PER_TILE):
                st = st_s[i * CHUNKS_PER_TILE + cc]
                rt = rt_s[i * CHUNKS_PER_TILE + cc]
                sl = _chunk_rows(cc)
                dq_far.append(jnp.dot(do_b[sl], st, preferred_element_type=F32))
                dk_far.append(jnp.dot(v_b[sl], rt, preferred_element_type=F32))
                dv_far.append(lax.dot_general(kt_b[sl], rt, NT, preferred_element_type=F32))
                both = jnp.sum(st.astype(F32) * rt.astype(F32), axis=0, keepdims=True)
                across.append(jnp.broadcast_to(both, (CHUNK, HEAD_DIM)))
            dq = dq_s[rows, :] + jnp.concatenate(dq_far, axis=0) * jnp.exp(gl)
            dk_in = dk_s[rows, :]
            dk_out = jnp.concatenate(dk_far, axis=0) * ekt
            dk = dk_in + dk_out
            dv = dv_s[rows, :] + jnp.concatenate(dv_far, axis=0)
            pc = kk * dk_out
            dgl = (_suffix_in_chunk(q * dq - kk * dk_in, pos) + (_prefix_in_chunk(pc, pos) - pc)
                   + jnp.concatenate(across, axis=0) * jnp.exp(gll))
            hf = hf_ref[rows, :]
            sig = _sigmoid(hf)
            f = lb + (1.0 - lb) * sig
            df = dgl / f - dk
            dhf_ref[rows, :] = (df * (1.0 - lb) * sig * (1.0 - sig)).astype(BF16)
            hq = hq_ref[rows, :]
            dhq_ref[rows, :] = (dq * _dsilu(hq, _sigmoid(hq))).astype(BF16)
            dhi_ref[rows, :] = dv.astype(BF16)
            return dlb + jnp.sum(df * (1.0 - sig), axis=0, keepdims=True)

        dlb = lax.fori_loop(0, n_tiles, finish, jnp.zeros((1, HEAD_DIM), F32))
        dl0 = lb * (1.0 - lb) * dlb
        dlog_ref[0:1, :] = dl0
        dlog_ref[1:2, :] = -dl0

    wide = HEADS * HEAD_DIM
    return _host_call(
        body, 8, 6, fused, _head_first_last, name="hgrn_bwd", grid=(HEADS,),
        in_specs=[_head_col(0), _head_col(HEADS), _head_col(2 * HEADS), _head_col(3 * HEADS),
                  pl.BlockSpec((2, HEAD_DIM), lambda h: (0, h)), pl.BlockSpec((1, HEAD_DIM), lambda h: (0, 0)),
                  _head_col(0), _head_col(0)],
        out_specs=[_head_col(0)] * 4 + [pl.BlockSpec((2, HEAD_DIM), lambda h: (0, h)),
                                        pl.BlockSpec((1, HEAD_DIM), lambda h: (0, 0))],
        out_shape=[jax.ShapeDtypeStruct((SEQ, wide), BF16)] * 4 + [jax.ShapeDtypeStruct((2, wide), F32),
                                                                    jax.ShapeDtypeStruct((1, HEAD_DIM), F32)],
        scratch_shapes=[pltpu.VMEM((SEQ, HEAD_DIM), F32)] * 7 + [pltpu.VMEM((n_chunks, HEAD_DIM, HEAD_DIM), F32),
                                                                 pltpu.VMEM((n_chunks, HEAD_DIM, HEAD_DIM), BF16),
                                                                 pltpu.VMEM((n_chunks, HEAD_DIM, HEAD_DIM), BF16)],
        sem=("arbitrary",),
        operands=[proj, proj, proj, proj, lb_logits, norm_w, o_pre, d_aout] + list(fused_arrays))


Q_TILE = 512
ATT_SCALE = HEAD_DIM ** -0.5
ATT_OFF = 4 * HEADS


def _qk_prep(proj, q_w, k_w, fused=None, fused_arrays=()):
    def body(aq_ref, ak_ref, av_ref, qw_ref, kw_ref, qn_ref, kn_ref, v_ref):
        aq = aq_ref[...]
        ak = ak_ref[...]
        qn_ref[...] = (aq * lax.rsqrt(jnp.mean(aq * aq, axis=-1, keepdims=True) + EPS) * qw_ref[...]).astype(BF16)
        kn_ref[...] = (ak * lax.rsqrt(jnp.mean(ak * ak, axis=-1, keepdims=True) + EPS) * kw_ref[...]).astype(BF16)
        v_ref[...] = av_ref[...].astype(BF16)

    wide = HEADS * HEAD_DIM
    vec = pl.BlockSpec((1, HEAD_DIM), lambda h: (0, 0))
    return _host_call(
        body, 5, 3, fused, _head_first_last, name="qk_prep", grid=(HEADS,),
        in_specs=[_head_col(ATT_OFF), _head_col(ATT_OFF + HEADS), _head_col(ATT_OFF + 2 * HEADS), vec, vec],
        out_specs=[_head_col(0)] * 3, out_shape=[jax.ShapeDtypeStruct((SEQ, wide), BF16)] * 3,
        scratch_shapes=[], sem=("parallel",), operands=[proj, proj, proj, q_w, k_w] + list(fused_arrays))


def _alibi_slopes():
    slopes = np.exp2(-8.0 * np.arange(1, HEADS + 1, dtype=np.float32) / HEADS).astype(np.float32)
    return np.broadcast_to(slopes[:, None, None], (HEADS, 1, HEAD_DIM))


SLOPE_SPEC = pl.BlockSpec((None, 1, HEAD_DIM), lambda h, i: (h, 0, 0))


N_Q_TILES = SEQ // Q_TILE
K_BLOCK = 512
NOT_ATTENDED = 1e35


def _att_tables():
    o = np.arange(N_Q_TILES, dtype=np.int32)[:, None, None]
    r = np.arange(Q_TILE, dtype=np.int32)[None, :, None]
    c = np.arange(K_BLOCK, dtype=np.int32)[None, None, :]
    dist = o * Q_TILE + r - c
    mult = ((dist <= 128).astype(np.float32) + (((dist % 4) == 0) & (dist <= 512)).astype(np.float32)
            + ((dist % 16) == 0).astype(np.float32))
    valid = (dist >= 0) & (mult > 0)
    return (np.where(valid, dist.astype(np.float32), np.float32(NOT_ATTENDED)).astype(np.float32),
            np.where(valid, np.log(np.maximum(mult, 1.0)), 0.0).astype(np.float32))


TABLE_SPEC = pl.BlockSpec((N_Q_TILES, Q_TILE, K_BLOCK), lambda h, i: (0, 0, 0))


def _att_block(q, k_ref, j, i, slope, dist_ref, lmul_ref):
    rows = pl.ds(pl.multiple_of(j * K_BLOCK, K_BLOCK), K_BLOCK)
    off = i - j * (K_BLOCK // Q_TILE)
    s = lax.dot_general(q, k_ref[rows, :], NT, preferred_element_type=F32) * ATT_SCALE
    return s - slope * dist_ref[off] + lmul_ref[off], rows


def _n_key_blocks(i):
    return (i + K_BLOCK // Q_TILE) // (K_BLOCK // Q_TILE)


def _att_first_last():
    h, i = pl.program_id(0), pl.program_id(1)
    return (h == 0) & (i == 0), (h == HEADS - 1) & (i == N_Q_TILES - 1)


def _attn_fwd(qn, kn, vb, fused=None, fused_arrays=()):
    def body(q_ref, k_ref, v_ref, sl_ref, dist_ref, lmul_ref, o_ref, lse_ref):
        i = pl.program_id(1)
        q = q_ref[...]
        slope = sl_ref[0:1, 0:1]

        def step(j, carry):
            m, l, acc = carry
            sb, rows = _att_block(q, k_ref, j, i, slope, dist_ref, lmul_ref)
            m_new = jnp.maximum(m, jnp.max(sb, axis=-1, keepdims=True))
            alpha = jnp.exp(m - m_new)
            p = jnp.exp(sb - m_new)
            l = alpha * l + jnp.sum(p, axis=-1, keepdims=True)
            acc = alpha * acc + jnp.dot(p.astype(BF16), v_ref[rows, :], preferred_element_type=F32)
            return m_new, l, acc

        m, l, acc = lax.fori_loop(0, _n_key_blocks(i), step,
                                  (jnp.full((Q_TILE, 1), -1e30, F32), jnp.zeros((Q_TILE, 1), F32),
                                   jnp.zeros((Q_TILE, HEAD_DIM), F32)))
        o_ref[...] = acc / l
        lse_ref[...] = m + jnp.log(l)

    wide = HEADS * HEAD_DIM
    qt = pl.BlockSpec((Q_TILE, HEAD_DIM), lambda h, i: (i, h))
    full = pl.BlockSpec((SEQ, HEAD_DIM), lambda h, i: (0, h))
    return _host_call(
        body, 6, 2, fused, _att_first_last, name="attn_fwd", grid=(HEADS, N_Q_TILES),
        in_specs=[qt, full, full, SLOPE_SPEC, TABLE_SPEC, TABLE_SPEC],
        out_specs=[qt, pl.BlockSpec((None, Q_TILE, 1), lambda h, i: (h, i, 0))],
        out_shape=[jax.ShapeDtypeStruct((SEQ, wide), F32), jax.ShapeDtypeStruct((HEADS, SEQ, 1), F32)],
        scratch_shapes=[], sem=("parallel", "parallel"),
        operands=[qn, kn, vb, _alibi_slopes(), *_att_tables()] + list(fused_arrays))


def _attn_bwd(qn, kn, vb, o, lse, d_mix, fused=None, fused_arrays=()):
    def body(q_ref, k_ref, v_ref, o_ref, lse_ref, do_ref, sl_ref, dist_ref, lmul_ref, dq_ref, dk_ref, dv_ref):
        i = pl.program_id(1)
        q = q_ref[...]
        do = do_ref[...]
        do_b = do.astype(BF16)
        slope = sl_ref[0:1, 0:1]
        lse = lse_ref[...]
        delta = jnp.sum(do * o_ref[...], axis=-1, keepdims=True)

        @pl.when(i == 0)
        def _():
            dk_ref[...] = jnp.zeros_like(dk_ref)
            dv_ref[...] = jnp.zeros_like(dv_ref)

        def step(j, dq):
            sb, rows = _att_block(q, k_ref, j, i, slope, dist_ref, lmul_ref)
            p = jnp.exp(sb - lse)
            dp = lax.dot_general(do_b, v_ref[rows, :], NT, preferred_element_type=F32)
            ds = (p * (dp - delta)).astype(BF16)
            dk_ref[rows, :] += lax.dot_general(ds, q, TN, preferred_element_type=F32) * ATT_SCALE
            dv_ref[rows, :] += lax.dot_general(p.astype(BF16), do_b, TN, preferred_element_type=F32)
            return dq + jnp.dot(ds, k_ref[rows, :], preferred_element_type=F32)

        dq = lax.fori_loop(0, _n_key_blocks(i), step, jnp.zeros((Q_TILE, HEAD_DIM), F32))
        dq_ref[...] = dq * ATT_SCALE

    wide = HEADS * HEAD_DIM
    qt = pl.BlockSpec((Q_TILE, HEAD_DIM), lambda h, i: (i, h))
    full = pl.BlockSpec((SEQ, HEAD_DIM), lambda h, i: (0, h))
    return _host_call(
        body, 9, 3, fused, _att_first_last, name="attn_bwd", grid=(HEADS, N_Q_TILES),
        in_specs=[qt, full, full, qt, pl.BlockSpec((None, Q_TILE, 1), lambda h, i: (h, i, 0)),
                  pl.BlockSpec((Q_TILE, HEAD_DIM), lambda h, i: (i, h + HEADS)), SLOPE_SPEC, TABLE_SPEC, TABLE_SPEC],
        out_specs=[qt, full, full], out_shape=[jax.ShapeDtypeStruct((SEQ, wide), F32)] * 3,
        scratch_shapes=[], sem=("parallel", "arbitrary"),
        operands=[qn, kn, vb, o, lse, d_mix, _alibi_slopes(), *_att_tables()] + list(fused_arrays))


def _qk_bwd(proj, q_w, k_w, dqn, dkn, dv, fused=None, fused_arrays=()):
    def body(aq_ref, ak_ref, qw_ref, kw_ref, dqn_ref, dkn_ref, dv_ref, daq_ref, dak_ref, dav_ref, gq_ref, gk_ref):
        h = pl.program_id(0)

        @pl.when(h == 0)
        def _():
            gq_ref[...] = jnp.zeros_like(gq_ref)
            gk_ref[...] = jnp.zeros_like(gk_ref)

        def one(a_ref, w_ref, d_ref, da_ref, g_ref):
            a = a_ref[...]
            d = d_ref[...]
            rs = lax.rsqrt(jnp.mean(a * a, axis=-1, keepdims=True) + EPS)
            ah = a * rs
            g_ref[...] += jnp.sum(d * ah, axis=0, keepdims=True)
            dah = d * w_ref[...]
            da_ref[...] = (rs * (dah - ah * jnp.mean(dah * ah, axis=-1, keepdims=True))).astype(BF16)

        one(aq_ref, qw_ref, dqn_ref, daq_ref, gq_ref)
        one(ak_ref, kw_ref, dkn_ref, dak_ref, gk_ref)
        dav_ref[...] = dv_ref[...].astype(BF16)

    wide = HEADS * HEAD_DIM
    vec = pl.BlockSpec((1, HEAD_DIM), lambda h: (0, 0))
    res, extra = _host_call(
        body, 7, 5, fused, _head_first_last, name="qk_bwd", grid=(HEADS,),
        in_specs=[_head_col(ATT_OFF), _head_col(ATT_OFF + HEADS), vec, vec, _head_col(0), _head_col(0), _head_col(0)],
        out_specs=[_head_col(0)] * 3 + [vec, vec],
        out_shape=[jax.ShapeDtypeStruct((SEQ, wide), BF16)] * 3 + [jax.ShapeDtypeStruct((1, HEAD_DIM), F32)] * 2,
        scratch_shapes=[], sem=("arbitrary",), operands=[proj, proj, q_w, k_w, dqn, dkn, dv] + list(fused_arrays))
    return res if fused is None else (res, extra)


def _pair_sum(name, partial, theirs, core):
    _, r, c = theirs.shape
    tr = r // 2 if r % 16 == 0 else r

    def body(core_ref, a_ref, b_ref, o_ref):
        o_ref[...] = (a_ref[...].astype(F32) + b_ref[...].astype(F32)).astype(BF16)

    spec = pl.BlockSpec((None, tr, c), lambda q, i, core_ref: (q, i, 0))
    grid_spec = pltpu.PrefetchScalarGridSpec(
        num_scalar_prefetch=1, grid=(4, r // tr),
        in_specs=[pl.BlockSpec((None, tr, c), lambda q, i, core_ref: (2 * q + core_ref[0], i, 0)), spec],
        out_specs=spec)
    return pl.pallas_call(body, name=name, grid_spec=grid_spec, out_shape=jax.ShapeDtypeStruct(theirs.shape, BF16),
                          compiler_params=_params(("parallel", "parallel")))(core, partial, theirs)


def _adamw_step(w, m, v, g):
    nm = ADAM_B1 * m + (1.0 - ADAM_B1) * g
    nv = ADAM_B2 * v + (1.0 - ADAM_B2) * (g * g)
    m_hat = nm / (1.0 - ADAM_B1 ** ADAM_STEP)
    v_hat = nv / (1.0 - ADAM_B2 ** ADAM_STEP)
    return -ADAM_LR * (m_hat / (jnp.sqrt(v_hat) + ADAM_EPS) + ADAM_WD * w), nm, nv


def _adamw(name, w, m, v, addends, tr=None):
    r, c = w.shape
    tr = r if tr is None else tr
    n_add = len(addends)

    def body(*refs):
        w_ref, m_ref, v_ref = refs[:3]
        add_refs = refs[3:3 + n_add]
        g_ref, d_ref, nm_ref, nv_ref = refs[3 + n_add:]
        g = add_refs[0][...].astype(F32)
        for a_ref in add_refs[1:]:
            g = g + a_ref[...].astype(F32)
        g_ref[...] = g
        d_ref[...], nm_ref[...], nv_ref[...] = _adamw_step(w_ref[...], m_ref[...], v_ref[...], g)

    spec = pl.BlockSpec((tr, c), lambda i: (i, 0))
    out = jax.ShapeDtypeStruct((r, c), F32)
    return pl.pallas_call(body, name=name, grid=(r // tr,), in_specs=[spec] * (3 + n_add), out_specs=[spec] * 4,
                          out_shape=[out] * 4, compiler_params=_params(("parallel",)))(w, m, v, *addends)


def _adamw_reduced(name, w, m, v, chip_sums, received, chip, tr):
    r, c = w.shape

    def body(chip_ref, w_ref, m_ref, v_ref, own_ref, r0_ref, r1_ref, r2_ref, g_ref, d_ref, nm_ref, nv_ref):
        g = ((own_ref[...].astype(F32) + r0_ref[...].astype(F32)) + r1_ref[...].astype(F32)) + r2_ref[...].astype(F32)
        g_ref[...] = g
        d_ref[...], nm_ref[...], nv_ref[...] = _adamw_step(w_ref[...], m_ref[...], v_ref[...], g)

    spec = pl.BlockSpec((tr, c), lambda i, chip_ref: (i, 0))

    def slot(k):
        return pl.BlockSpec((None, tr, c), lambda i, chip_ref: (k, i, 0))

    grid_spec = pltpu.PrefetchScalarGridSpec(
        num_scalar_prefetch=1, grid=(r // tr,),
        in_specs=[spec, spec, spec, pl.BlockSpec((None, tr, c), lambda i, chip_ref: (chip_ref[0], i, 0)),
                  slot(0), slot(1), slot(2)],
        out_specs=[spec] * 4)
    out = jax.ShapeDtypeStruct((r, c), F32)
    return pl.pallas_call(body, name=name, grid_spec=grid_spec, out_shape=[out] * 4,
                          compiler_params=_params(("parallel",)))(chip, w, m, v, chip_sums, received, received, received)


def _sum_devices(gathered):
    _, r, c = gathered.shape

    def body(g_ref, o_ref):
        acc = g_ref[0]
        for d in range(1, N_DEV):
            acc = acc + g_ref[d]
        o_ref[...] = acc

    return pl.pallas_call(body, name="sum_devices", out_shape=jax.ShapeDtypeStruct((r, c), F32))(gathered)


def _pack_rows(vectors, rows):
    flat = jnp.concatenate([v.reshape(-1) for v in vectors])
    return jnp.pad(flat, (0, rows * 128 - flat.shape[0])).reshape(rows, 128)


def _unpack(flat, shapes):
    out, off = [], 0
    for shp in shapes:
        n = 1
        for d in shp:
            n *= d
        out.append(flat[off:off + n].reshape(shp))
        off += n
    return out


def _device_step(xs, tgt, mod, norm1_w, norm2_w, lb_logits, hg_norm_w, q_norm_w, k_norm_w, conv_w_full, conv_b,
                 win_g, w_out_x, w_up_x, w_down_x, core=None):
    fused = core is not None
    shift1, scale1, gate1, shift2, scale2, gate2 = (mod[k] for k in range(6))

    h, rstd1 = _norm_fwd("norm1_fwd", xs, norm1_w, scale1, shift1)
    if fused:
        near = (0, 1, 2)
        head_rows, tail_rows = (0, UP_HEAD_ROWS), (UP_HEAD_ROWS, D_MODEL - UP_HEAD_ROWS)
        proj, (wout_g, wup_g) = _mm_blocked_rhs(
            "mm_in", h, win_g, fused_arrays=[w_out_x, w_up_x],
            fused=[_FusedCopies("gather", [w_out_x]), _FusedCopies("gather", [w_up_x], peers=near, rows=head_rows)])
        (a_out, o_pre), (wup_g, wout_g) = _hgrn_fwd(
            proj, lb_logits, hg_norm_w, fused_arrays=[w_up_x, wup_g, wout_g],
            fused=[_FusedCopies("gather_more", [w_up_x, wup_g], peers=near, rows=tail_rows, relay_rows=head_rows),
                   _FusedCopies("forward", [wout_g])])
        wout_full = wout_g.reshape(D_MODEL, D_MODEL)
        (qn, kn, vb), _ = _qk_prep(proj, q_norm_w, k_norm_w)
        (att_o, lse), (wup_g,) = _attn_fwd(qn, kn, vb, _FusedCopies("relay", [wup_g], rows=tail_rows), [wup_g])
    else:
        proj = _mm_blocked_rhs("mm_in", h, win_g)
        (a_out, o_pre), _ = _hgrn_fwd(proj, lb_logits, hg_norm_w)
        wup_g, wout_full, wdown_full = w_up_x, w_out_x, w_down_x
        (qn, kn, vb), _ = _qk_prep(proj, q_norm_w, k_norm_w)
        (att_o, lse), _ = _attn_fwd(qn, kn, vb)
    mixin = jnp.concatenate([a_out, att_o.astype(BF16)], axis=1)
    if fused:
        down_head = (0, DOWN_HEAD_ROWS)
        mix, (wup_g, wdown_g) = _mm_plain(
            "mm_out", mixin, wout_full, NN, 512, 1024, F32, fused_arrays=[wup_g, w_down_x],
            fused=[_FusedCopies("forward", [wup_g]), _FusedCopies("gather", [w_down_x], rows=down_head)])
    else:
        mix = _mm_plain("mm_out", mixin, wout_full, NN, 512, 1024, F32)
    x1, h2, rstd2 = _norm_fwd("norm2_fwd", xs, norm2_w, scale2, shift2, resid=mix, gate=gate1)
    if fused:
        down_tail = (DOWN_HEAD_ROWS, FF_BLK - DOWN_HEAD_ROWS)
        u, (wdown_g,) = _mm_blocked_rhs(
            "mm_up", h2, wup_g, fused_arrays=[w_down_x, wdown_g],
            fused=_FusedCopies("gather_more", [w_down_x, wdown_g], rows=down_tail))
        y, (wdown_g,) = _conv_gate_fwd(u, conv_w_full, conv_b, _FusedCopies("forward", [wdown_g]), [wdown_g])
        wdown_full = wdown_g.reshape(D_FF, D_MODEL)
    else:
        u = _mm_blocked_rhs("mm_up", h2, wup_g)
        y = _conv_gate_fwd(u, conv_w_full, conv_b)
    ffn = _mm_plain("mm_down", y, wdown_full, NN, MM_TILE, 512, F32)
    loss_v, dout, dffn, dgate2 = _loss_head(x1, ffn, gate2, tgt)

    dy = _mm_plain("mm_down_dx", dffn, wdown_full, NT, MM_TILE, UP_BLK, BF16)
    gw_down = _mm_plain("mm_down_dw", y, dffn, TN, UP_BLK, 1024, BF16)
    da, dg, gconv_w, gconv_b = _conv_gate_bwd(u, dy, conv_w_full, conv_b)
    gw_up = _mm_halves_wgrad("mm_up_dw", h2, da, dg)
    if fused:
        part_up, part_down = gw_up, gw_down.reshape(N_DEV, FF_BLK, D_MODEL)
        dh2, (sib_up,) = _mm_halves_rhs_t("mm_up_dx", da, dg, wup_g, fused=_FusedCopies("sibling", [part_up]),
                                          fused_arrays=[part_up])
    else:
        dh2 = _mm_halves_rhs_t("mm_up_dx", da, dg, wup_g)
    dx1, dmix, dshift2, dscale2, gnorm2, dgate1 = _norm_bwd(
        "norm2_bwd", dh2, x1, rstd2, norm2_w, scale2, dout, mix=mix, gate=gate1)
    gw_out = _mm_plain("mm_out_dw", mixin, dmix, TN, 512, 1024, BF16)
    if fused:
        part_out = gw_out.reshape(N_DEV, OUT_BLK, D_MODEL)
        dmixin, (sib_out, sib_down) = _mm_plain(
            "mm_out_dx", dmix, wout_full, NT, 512, 1024, F32,
            fused=_FusedCopies("sibling", [part_out, part_down]), fused_arrays=[part_out, part_down])
        cs_up = _pair_sum("grad_pair_sum_up", part_up, sib_up, core)
        cs_out = _pair_sum("grad_pair_sum_out", part_out, sib_out, core)
        cs_down = _pair_sum("grad_pair_sum_down", part_down, sib_down, core)
        (dhq, dhf, dhi, dhg, glog, ghg), (fc_up,) = _hgrn_bwd(
            proj, lb_logits, hg_norm_w, o_pre, dmixin, _FusedCopies("chips", [cs_up], rows=(0, UP_EXCHANGE_HEAD)),
            [cs_up])
        (dqn, dkn, dvv), (fc_down,) = _attn_bwd(
            qn, kn, vb, att_o, lse, dmixin, _FusedCopies("chips", [cs_down], rows=(0, DOWN_EXCHANGE_HEAD)), [cs_down])
        (daq, dak, dav, gqw, gkw), (fc_up,) = _qk_bwd(
            proj, q_norm_w, k_norm_w, dqn, dkn, dvv, fused_arrays=[cs_up, fc_up],
            fused=_FusedCopies("chips_more", [cs_up, fc_up], rows=(UP_EXCHANGE_HEAD, D_MODEL - UP_EXCHANGE_HEAD)))
    else:
        dmixin = _mm_plain("mm_out_dx", dmix, wout_full, NT, 512, 1024, F32)
        (dhq, dhf, dhi, dhg, glog, ghg), _ = _hgrn_bwd(proj, lb_logits, hg_norm_w, o_pre, dmixin)
        (dqn, dkn, dvv), _ = _attn_bwd(qn, kn, vb, att_o, lse, dmixin)
        daq, dak, dav, gqw, gkw = _qk_bwd(proj, q_norm_w, k_norm_w, dqn, dkn, dvv)
    dproj = jnp.concatenate([dhq, dhf, dhi, dhg, daq, dak, dav], axis=1)
    if fused:
        down_tail = (DOWN_EXCHANGE_HEAD, FF_BLK - DOWN_EXCHANGE_HEAD)
        gw_in, (fc_out, fc_down) = _mm_wgrad_blocked(
            "mm_in_dw", h, dproj, fused_arrays=[cs_out, cs_down, fc_down],
            fused=[_FusedCopies("chips", [cs_out]), _FusedCopies("chips_more", [cs_down, fc_down], rows=down_tail)])
        from_sibling, = _exchange_sibling("grad_exchange_sibling_b", [gw_in])
        cs_in = _pair_sum("grad_pair_sum_in", gw_in, from_sibling, core)
        dh, (fc_in,) = _mm_blocked_rhs_t("mm_in_dx", dproj, win_g, fused=_FusedCopies("chips", [cs_in]),
                                         fused_arrays=[cs_in])
        large = [(cs_in, fc_in), (cs_out, fc_out), (cs_up, fc_up), (cs_down, fc_down)]
    else:
        gw_in = _mm_wgrad_blocked("mm_in_dw", h, dproj)
        dh = _mm_blocked_rhs_t("mm_in_dx", dproj, win_g)
        large = [gw_in, gw_out, gw_up, gw_down]
    grad_x, dshift1, dscale1, gnorm1 = _norm_bwd("norm1_bwd", dh, xs, rstd1, norm1_w, scale1, dx1)
    gmod = jnp.concatenate([dshift1, dscale1, dgate1, dshift2, dscale2, dgate2], axis=1)
    return (loss_v, grad_x, gmod, gnorm1, gnorm2, glog, ghg, gqw, gkw, gconv_b, gconv_w, *large)


def kernel(x, c, w_ada, b_ada, norm1_w, w_in, lb_logits, hg_norm_w, q_norm_w, k_norm_w, w_out, norm2_w, w_up, conv_w, conv_b, w_down, loss_target, m_w_ada, m_b_ada, m_norm1_w, m_w_in, m_lb_logits, m_hg_norm_w, m_q_norm_w, m_k_norm_w, m_w_out, m_norm2_w, m_w_up, m_conv_w, m_conv_b, m_w_down, v_w_ada, v_b_ada, v_norm1_w, v_w_in, v_lb_logits, v_hg_norm_w, v_q_norm_w, v_k_norm_w, v_w_out, v_norm2_w, v_w_up, v_conv_w, v_conv_b, v_w_down):
    ix, iy, ic = lax.axis_index("x"), lax.axis_index("y"), lax.axis_index("c")
    me = 4 * ix + 2 * iy + ic
    my_chip = 2 * ix + iy

    xs = x[0]
    tgt = loss_target[0]

    win_g, = _allgather_weights([w_in[0].astype(BF16)])

    first = _allgather_vmem(_pack_rows([c, conv_w[0]], 40), "allgather_c_conv_w").reshape(N_DEV, 40 * 128)
    c_all = first[:, :D_MODEL]
    conv_w_full = (first[:, D_MODEL:D_MODEL + 3 * FF_BLK].reshape(N_DEV, 3, FF_BLK).transpose(1, 0, 2)
                   .reshape(3, D_FF))

    b_blk = lax.dynamic_slice_in_dim(b_ada, me * ADA_BLK, ADA_BLK, axis=1)
    mod_cols = _ada_fwd(c_all, w_ada[0], b_blk)
    mod_all = _allgather_vmem(mod_cols, "allgather_mod").reshape(N_DEV, N_DEV, ADA_BLK)
    mod = lax.dynamic_index_in_dim(mod_all, me, axis=1, keepdims=False).reshape(6, 1, D_MODEL)

    (loss_v, grad_x, gmod, gnorm1, gnorm2, glog, ghg, gqw, gkw, gconv_b, gconv_w,
     rs_in, rs_out, rs_up, rs_down) = _device_step(
        xs, tgt, mod, norm1_w, norm2_w, lb_logits, hg_norm_w, q_norm_w, k_norm_w, conv_w_full, conv_b,
        win_g, w_out[0].astype(BF16), w_up[0].astype(BF16), w_down[0].astype(BF16),
        core=jnp.reshape(ic, (1,)).astype(jnp.int32))

    small_shapes = [(1, 6 * D_MODEL), (1, D_MODEL), (1, D_MODEL), (2, HEADS * HEAD_DIM), (1, HEAD_DIM),
                    (1, HEAD_DIM), (1, HEAD_DIM), (1, D_FF), (3, D_FF), (1, 1)]
    small = [gmod, gnorm1, gnorm2, glog, ghg, gqw, gkw, gconv_b, gconv_w, loss_v[:, 0:1]]
    n_small = sum(a.size for a in small)
    rows = -(-n_small // 1024) * 8
    gathered = _allgather_vmem(_pack_rows(small, rows), "allgather_small").reshape(N_DEV, rows, 128)
    summed = _sum_devices(gathered).reshape(-1)
    (g_b_ada, g_norm1, g_norm2, g_lb, g_hg, g_q, g_k, g_conv_b, g_conv_w_full, loss_sum) = _unpack(summed, small_shapes)
    loss = loss_sum[0, 0]
    g_conv_w = lax.dynamic_slice_in_dim(g_conv_w_full, me * FF_BLK, FF_BLK, axis=1)

    gmod_all = gathered[:, :6 * D_MODEL // 128, :].reshape(N_DEV, 6 * D_MODEL)
    gmod_cols = lax.dynamic_slice_in_dim(gmod_all, me * ADA_BLK, ADA_BLK, axis=1)
    g_w_ada_raw = _ada_wgrad(c_all, gmod_cols)

    chip = jnp.reshape(my_chip, (1,)).astype(jnp.int32)

    def big_update(name, w, m, v, rs, tr):
        chip_sums, received = rs
        return _adamw_reduced(name, w[0], m[0], v[0], chip_sums, received, chip, tr)

    r_in = big_update("adamw_w_in", w_in, m_w_in, v_w_in, rs_in, 256)
    r_out = big_update("adamw_w_out", w_out, m_w_out, v_w_out, rs_out, 128)
    r_up = big_update("adamw_w_up", w_up, m_w_up, v_w_up, rs_up, 256)
    r_down = big_update("adamw_w_down", w_down, m_w_down, v_w_down, rs_down, 176)
    r_ada = _adamw("adamw_w_ada", w_ada[0], m_w_ada[0], v_w_ada[0], [g_w_ada_raw], tr=256)
    r_convw = _adamw("adamw_conv_w", conv_w[0], m_conv_w[0], v_conv_w[0], [g_conv_w])

    rep_shapes = [(1, 6 * D_MODEL), (1, D_MODEL), (1, D_MODEL), (2, HEADS * HEAD_DIM), (1, HEAD_DIM),
                  (1, HEAD_DIM), (1, HEAD_DIM), (1, D_FF)]
    rep_rows = -(-sum(a * b for a, b in rep_shapes) // 1024) * 8
    pack = lambda arrs: _pack_rows(arrs, rep_rows)
    rep = _adamw("adamw_small",
                 pack([b_ada, norm1_w, norm2_w, lb_logits, hg_norm_w, q_norm_w, k_norm_w, conv_b]),
                 pack([m_b_ada, m_norm1_w, m_norm2_w, m_lb_logits, m_hg_norm_w, m_q_norm_w, m_k_norm_w, m_conv_b]),
                 pack([v_b_ada, v_norm1_w, v_norm2_w, v_lb_logits, v_hg_norm_w, v_q_norm_w, v_k_norm_w, v_conv_b]),
                 [pack([g_b_ada, g_norm1, g_norm2, g_lb, g_hg, g_q, g_k, g_conv_b])])
    rep = [_unpack(r.reshape(-1), rep_shapes) for r in rep]

    def big(r):
        return [a[None] for a in r]

    order = {"w_ada": big(r_ada), "b_ada": [r[0] for r in rep], "norm1_w": [r[1] for r in rep],
             "w_in": big(r_in), "lb_logits": [r[3] for r in rep], "hg_norm_w": [r[4] for r in rep],
             "q_norm_w": [r[5] for r in rep], "k_norm_w": [r[6] for r in rep], "w_out": big(r_out),
             "norm2_w": [r[2] for r in rep], "w_up": big(r_up), "conv_w": big(r_convw),
             "conv_b": [r[7] for r in rep], "w_down": big(r_down)}
    names = ["w_ada", "b_ada", "norm1_w", "w_in", "lb_logits", "hg_norm_w", "q_norm_w", "k_norm_w", "w_out",
             "norm2_w", "w_up", "conv_w", "conv_b", "w_down"]
    outs = [loss, grad_x[None]]
    for kind in range(4):
        outs += [order[n][kind] for n in names]
    return tuple(outs)
```
